```python
import math
import jax, jax.numpy as jnp
from jax import lax
import numpy as np

D_MODEL = 1024
BATCH = 8
SEQ = 8192
DEPTH = 1

N_META = 16
D_MIX = D_MODEL
RET_HEADS = 4
RET_HEAD_DIM = 128
RET_WIDTH = RET_HEADS * RET_HEAD_DIM
SSM_WIDTH = D_MIX - RET_WIDTH
SSM_GROUP = 16
SSM_GROUPS = SSM_WIDTH // SSM_GROUP
SSM_STATE = 64
CHUNK = 128
D_FF = 2816
FFN_RES = 0.5
ROPE_BASE = 10000.0
EPS = 1e-6
IN_PROJ = 4 * RET_WIDTH + SSM_WIDTH

kernel_name = "hymba_retnet_s5_macaron_layer"


def rms_norm(x, w):
    xf = x.astype(jnp.float32)
    y = xf * lax.rsqrt(jnp.mean(xf * xf, axis=-1, keepdims=True) + EPS)
    return (y * w.astype(jnp.float32)).astype(x.dtype)


def swiglu(x, w_gate, w_up, w_down):
    return (jax.nn.silu(x @ w_gate) * (x @ w_up)) @ w_down


def rotary(x, pos):
    dh = x.shape[-1]
    freqs = 1.0 / (ROPE_BASE ** (jnp.arange(0, dh, 2, dtype=jnp.float32) / dh))
    ang = pos.astype(jnp.float32)[:, None] * freqs[None, :]
    cos = jnp.cos(ang)[None, :, None, :]
    sin = jnp.sin(ang)[None, :, None, :]
    xf = x.astype(jnp.float32)
    x1, x2 = xf[..., : dh // 2], xf[..., dh // 2:]
    out = jnp.concatenate([x1 * cos - x2 * sin, x1 * sin + x2 * cos], axis=-1)
    return out.astype(x.dtype)


def retention(q, k, v):
    bsz, L, H, dk = q.shape
    dv = v.shape[-1]
    log_g = jnp.log(1.0 - 2.0 ** (-5.0 - jnp.arange(H, dtype=jnp.float32)))

    def decay_mask(n):
        i = jnp.arange(n)
        diff = i[:, None] - i[None, :]
        return jnp.where(diff[None] >= 0,
                         jnp.exp(log_g[:, None, None] * jnp.maximum(diff, 0)[None].astype(jnp.float32)),
                         0.0)

    qm, km, vm = q[:, :N_META], k[:, :N_META], v[:, :N_META]
    s_m = jnp.einsum('bihd,bjhd->bhij', qm, km) * decay_mask(N_META)
    o_meta = jnp.einsum('bhij,bjhe->bihe', s_m, vm)
    w_m = jnp.exp(log_g[:, None] * (N_META - 1 - jnp.arange(N_META, dtype=jnp.float32))[None])
    state0 = jnp.einsum('bjhd,bjhe,hj->bhde', km, vm, w_m)

    n_chunks = (L - N_META) // CHUNK
    qc = q[:, N_META:].reshape(bsz, n_chunks, CHUNK, H, dk)
    kc = k[:, N_META:].reshape(bsz, n_chunks, CHUNK, H, dk)
    vc = v[:, N_META:].reshape(bsz, n_chunks, CHUNK, H, dv)
    s_c = jnp.einsum('bnihd,bnjhd->bnhij', qc, kc) * decay_mask(CHUNK)
    o_inner = jnp.einsum('bnhij,bnjhe->bnihe', s_c, vc)
    pos_c = jnp.arange(CHUNK, dtype=jnp.float32)
    w_k = jnp.exp(log_g[:, None] * (CHUNK - 1 - pos_c)[None])
    kv = jnp.einsum('bnjhd,bnjhe,hj->nbhde', kc, vc, w_k)
    g_chunk = jnp.exp(log_g * CHUNK)[None, :, None, None]
    state0 = state0.astype(kv.dtype)

    def step(S, kv_n):
        return (g_chunk * S + kv_n).astype(kv_n.dtype), S

    _, s_prev = lax.scan(step, state0, kv)
    w_q = jnp.exp(log_g[:, None] * (pos_c + 1.0)[None])
    o_cross = jnp.einsum('bnihd,nbhde,hi->bnihe', qc, s_prev, w_q)
    o_real = (o_inner + o_cross).reshape(bsz, L - N_META, H, dv)
    return jnp.concatenate([o_meta, o_real.astype(o_meta.dtype)], axis=1)


def head_group_norm(o, w):
    of = o.astype(jnp.float32)
    mu = jnp.mean(of, axis=-1, keepdims=True)
    var = jnp.mean(jnp.square(of - mu), axis=-1, keepdims=True)
    y = (of - mu) * lax.rsqrt(var + EPS)
    y = y.reshape(o.shape[0], o.shape[1], -1) * w.astype(jnp.float32)
    return y


def _linear_recurrence(e1, e2):
    a1, b1 = e1
    a2, b2 = e2
    return a1 * a2, a2 * b1 + b2


def s5_mixer(u, lam_re, lam_im, log_dt, b_re, b_im, c_re, c_im, d, glu_w, glu_b, norm_w):
    bsz, L, _ = u.shape
    uf = u.astype(jnp.float32).reshape(bsz, L, SSM_GROUPS, SSM_GROUP)
    lam = lax.complex(lam_re.astype(jnp.float32), lam_im.astype(jnp.float32))
    dt = jnp.exp(log_dt.astype(jnp.float32))[:, None]
    a_bar = jnp.exp(lam * dt)
    b = lax.complex(b_re.astype(jnp.float32), b_im.astype(jnp.float32))
    b_bar = ((a_bar - 1.0) / lam)[..., None] * b
    bu = jnp.einsum('gnp,blgp->blgn', b_bar, uf.astype(jnp.complex64))
    a = jnp.broadcast_to(a_bar[None, None], bu.shape)
    _, states = lax.associative_scan(_linear_recurrence, (a, bu), axis=1)
    c = lax.complex(c_re.astype(jnp.float32), c_im.astype(jnp.float32))
    y = jnp.real(jnp.einsum('gpn,blgn->blgp', c, states))
    y = y + d.astype(jnp.float32).reshape(SSM_GROUPS, SSM_GROUP) * uf
    y = jax.nn.gelu(y.reshape(bsz, L, SSM_WIDTH)).astype(u.dtype)
    y = y * jax.nn.sigmoid(y @ glu_w + glu_b)
    return rms_norm(y, norm_w)


def _fwd_setup_inputs(seed: int = 0) -> dict:
    key = jax.random.key(seed)
    ks = jax.random.split(key, 32)
    f32 = jnp.float32
    nrm = lambda k, shape, scale: (jax.random.normal(k, shape, f32) * scale)
    gain = lambda k, shape: 1.0 + 0.01 * jax.random.normal(k, shape, f32)
    Ld = DEPTH
    n_idx = jnp.arange(SSM_STATE, dtype=f32)
    return {
        "x": nrm(ks[0], (BATCH, SEQ, D_MODEL), 1.0),
        "meta_tokens": nrm(ks[1], (N_META, D_MODEL), 1.0),
        "ffn1_norm_w": gain(ks[2], (Ld, D_MODEL)),
        "ffn1_w_gate": nrm(ks[3], (Ld, D_MODEL, D_FF), D_MODEL ** -0.5),
        "ffn1_w_up": nrm(ks[4], (Ld, D_MODEL, D_FF), D_MODEL ** -0.5),
        "ffn1_w_down": nrm(ks[5], (Ld, D_FF, D_MODEL), D_FF ** -0.5),
        "mix_norm_w": gain(ks[6], (Ld, D_MODEL)),
        "w_in": nrm(ks[7], (Ld, D_MODEL, IN_PROJ), D_MODEL ** -0.5),
        "ret_norm_w": gain(ks[8], (Ld, RET_WIDTH)),
        "ssm_lambda_re": -0.5 + 0.01 * jax.random.normal(ks[9], (Ld, SSM_GROUPS, SSM_STATE), f32),
        "ssm_lambda_im": jnp.pi * n_idx[None, None, :] + 0.01 * jax.random.normal(ks[10], (Ld, SSM_GROUPS, SSM_STATE), f32),
        "ssm_log_dt": jax.random.uniform(ks[11], (Ld, SSM_GROUPS), f32, minval=math.log(0.001), maxval=math.log(0.1)),
        "ssm_b_re": nrm(ks[12], (Ld, SSM_GROUPS, SSM_STATE, SSM_GROUP), (2.0 * SSM_GROUP) ** -0.5),
        "ssm_b_im": nrm(ks[13], (Ld, SSM_GROUPS, SSM_STATE, SSM_GROUP), (2.0 * SSM_GROUP) ** -0.5),
        "ssm_c_re": nrm(ks[14], (Ld, SSM_GROUPS, SSM_GROUP, SSM_STATE), (2.0 * SSM_STATE) ** -0.5),
        "ssm_c_im": nrm(ks[15], (Ld, SSM_GROUPS, SSM_GROUP, SSM_STATE), (2.0 * SSM_STATE) ** -0.5),
        "ssm_d": nrm(ks[16], (Ld, SSM_WIDTH), 1.0),
        "ssm_glu_w": nrm(ks[17], (Ld, SSM_WIDTH, SSM_WIDTH), SSM_WIDTH ** -0.5),
        "ssm_glu_b": nrm(ks[18], (Ld, SSM_WIDTH), 0.01),
        "ssm_norm_w": gain(ks[19], (Ld, SSM_WIDTH)),
        "w_out": nrm(ks[20], (Ld, D_MIX, D_MODEL), D_MIX ** -0.5),
        "ffn2_norm_w": gain(ks[21], (Ld, D_MODEL)),
        "ffn2_w_gate": nrm(ks[22], (Ld, D_MODEL, D_FF), D_MODEL ** -0.5),
        "ffn2_w_up": nrm(ks[23], (Ld, D_MODEL, D_FF), D_MODEL ** -0.5),
        "ffn2_w_down": nrm(ks[24], (Ld, D_FF, D_MODEL), D_FF ** -0.5),
        "final_norm_w": gain(ks[25], (D_MODEL,)),
    }


def _fwd_reference(x, meta_tokens, ffn1_norm_w, ffn1_w_gate, ffn1_w_up, ffn1_w_down, mix_norm_w,
              w_in, ret_norm_w, ssm_lambda_re, ssm_lambda_im, ssm_log_dt, ssm_b_re, ssm_b_im,
              ssm_c_re, ssm_c_im, ssm_d, ssm_glu_w, ssm_glu_b, ssm_norm_w, w_out,
              ffn2_norm_w, ffn2_w_gate, ffn2_w_up, ffn2_w_down, final_norm_w):
    bsz = x.shape[0]
    meta = jnp.broadcast_to(meta_tokens.astype(x.dtype)[None], (bsz, N_META, D_MODEL))
    h = jnp.concatenate([meta, x], axis=1)
    L = h.shape[1]
    pos = jnp.arange(L)
    for l in range(DEPTH):
        h = h + FFN_RES * swiglu(rms_norm(h, ffn1_norm_w[l]), ffn1_w_gate[l], ffn1_w_up[l], ffn1_w_down[l])
        n = rms_norm(h, mix_norm_w[l])
        proj = n @ w_in[l]
        q = proj[..., 0:RET_WIDTH].reshape(bsz, L, RET_HEADS, RET_HEAD_DIM)
        k = proj[..., RET_WIDTH:2 * RET_WIDTH].reshape(bsz, L, RET_HEADS, RET_HEAD_DIM)
        v = proj[..., 2 * RET_WIDTH:3 * RET_WIDTH].reshape(bsz, L, RET_HEADS, RET_HEAD_DIM)
        g = proj[..., 3 * RET_WIDTH:4 * RET_WIDTH]
        u = proj[..., 4 * RET_WIDTH:]
        q = rotary(q, pos)
        k = rotary(k, pos) * (RET_HEAD_DIM ** -0.5)
        ret = head_group_norm(retention(q, k, v), ret_norm_w[l])
        ret = (jax.nn.silu(g.astype(jnp.float32)) * ret).astype(x.dtype)
        ssm = s5_mixer(u, ssm_lambda_re[l], ssm_lambda_im[l], ssm_log_dt[l], ssm_b_re[l], ssm_b_im[l],
                       ssm_c_re[l], ssm_c_im[l], ssm_d[l], ssm_glu_w[l], ssm_glu_b[l], ssm_norm_w[l])
        mixed = jnp.concatenate([ret, ssm.astype(x.dtype)], axis=-1) @ w_out[l]
        h = h + mixed
        h = h + FFN_RES * swiglu(rms_norm(h, ffn2_norm_w[l]), ffn2_w_gate[l], ffn2_w_up[l], ffn2_w_down[l])
    out = rms_norm(h, final_norm_w)
    return out[:, N_META:]


import jax as _jax
import jax.numpy as _jnp

TWIN_FORMAT = 'train_step'
FWD_PARAMS = ['x', 'meta_tokens', 'ffn1_norm_w', 'ffn1_w_gate', 'ffn1_w_up', 'ffn1_w_down', 'mix_norm_w', 'w_in', 'ret_norm_w', 'ssm_lambda_re', 'ssm_lambda_im', 'ssm_log_dt', 'ssm_b_re', 'ssm_b_im', 'ssm_c_re', 'ssm_c_im', 'ssm_d', 'ssm_glu_w', 'ssm_glu_b', 'ssm_norm_w', 'w_out', 'ffn2_norm_w', 'ffn2_w_gate', 'ffn2_w_up', 'ffn2_w_down', 'final_norm_w']
TWIN_WEIGHTS = ['meta_tokens', 'ffn1_norm_w', 'ffn1_w_gate', 'ffn1_w_up', 'ffn1_w_down', 'mix_norm_w', 'w_in', 'ret_norm_w', 'ssm_lambda_re', 'ssm_lambda_im', 'ssm_log_dt', 'ssm_b_re', 'ssm_b_im', 'ssm_c_re', 'ssm_c_im', 'ssm_d', 'ssm_glu_w', 'ssm_glu_b', 'ssm_norm_w', 'w_out', 'ffn2_norm_w', 'ffn2_w_gate', 'ffn2_w_up', 'ffn2_w_down', 'final_norm_w']
TWIN_DIFF_INPUT = 'x'
TWIN_INPUTS = ['x', 'meta_tokens', 'ffn1_norm_w', 'ffn1_w_gate', 'ffn1_w_up', 'ffn1_w_down', 'mix_norm_w', 'w_in', 'ret_norm_w', 'ssm_lambda_re', 'ssm_lambda_im', 'ssm_log_dt', 'ssm_b_re', 'ssm_b_im', 'ssm_c_re', 'ssm_c_im', 'ssm_d', 'ssm_glu_w', 'ssm_glu_b', 'ssm_norm_w', 'w_out', 'ffn2_norm_w', 'ffn2_w_gate', 'ffn2_w_up', 'ffn2_w_down', 'final_norm_w', 'loss_target', 'm_meta_tokens', 'm_ffn1_norm_w', 'm_ffn1_w_gate', 'm_ffn1_w_up', 'm_ffn1_w_down', 'm_mix_norm_w', 'm_w_in', 'm_ret_norm_w', 'm_ssm_lambda_re', 'm_ssm_lambda_im', 'm_ssm_log_dt', 'm_ssm_b_re', 'm_ssm_b_im', 'm_ssm_c_re', 'm_ssm_c_im', 'm_ssm_d', 'm_ssm_glu_w', 'm_ssm_glu_b', 'm_ssm_norm_w', 'm_w_out', 'm_ffn2_norm_w', 'm_ffn2_w_gate', 'm_ffn2_w_up', 'm_ffn2_w_down', 'm_final_norm_w', 'v_meta_tokens', 'v_ffn1_norm_w', 'v_ffn1_w_gate', 'v_ffn1_w_up', 'v_ffn1_w_down', 'v_mix_norm_w', 'v_w_in', 'v_ret_norm_w', 'v_ssm_lambda_re', 'v_ssm_lambda_im', 'v_ssm_log_dt', 'v_ssm_b_re', 'v_ssm_b_im', 'v_ssm_c_re', 'v_ssm_c_im', 'v_ssm_d', 'v_ssm_glu_w', 'v_ssm_glu_b', 'v_ssm_norm_w', 'v_w_out', 'v_ffn2_norm_w', 'v_ffn2_w_gate', 'v_ffn2_w_up', 'v_ffn2_w_down', 'v_final_norm_w']
TWIN_OUTPUTS = ['loss', 'grad_x', 'grad_meta_tokens', 'grad_ffn1_norm_w', 'grad_ffn1_w_gate', 'grad_ffn1_w_up', 'grad_ffn1_w_down', 'grad_mix_norm_w', 'grad_w_in', 'grad_ret_norm_w', 'grad_ssm_lambda_re', 'grad_ssm_lambda_im', 'grad_ssm_log_dt', 'grad_ssm_b_re', 'grad_ssm_b_im', 'grad_ssm_c_re', 'grad_ssm_c_im', 'grad_ssm_d', 'grad_ssm_glu_w', 'grad_ssm_glu_b', 'grad_ssm_norm_w', 'grad_w_out', 'grad_ffn2_norm_w', 'grad_ffn2_w_gate', 'grad_ffn2_w_up', 'grad_ffn2_w_down', 'grad_final_norm_w', 'delta_meta_tokens', 'delta_ffn1_norm_w', 'delta_ffn1_w_gate', 'delta_ffn1_w_up', 'delta_ffn1_w_down', 'delta_mix_norm_w', 'delta_w_in', 'delta_ret_norm_w', 'delta_ssm_lambda_re', 'delta_ssm_lambda_im', 'delta_ssm_log_dt', 'delta_ssm_b_re', 'delta_ssm_b_im', 'delta_ssm_c_re', 'delta_ssm_c_im', 'delta_ssm_d', 'delta_ssm_glu_w', 'delta_ssm_glu_b', 'delta_ssm_norm_w', 'delta_w_out', 'delta_ffn2_norm_w', 'delta_ffn2_w_gate', 'delta_ffn2_w_up', 'delta_ffn2_w_down', 'delta_final_norm_w', 'new_m_meta_tokens', 'new_m_ffn1_norm_w', 'new_m_ffn1_w_gate', 'new_m_ffn1_w_up', 'new_m_ffn1_w_down', 'new_m_mix_norm_w', 'new_m_w_in', 'new_m_ret_norm_w', 'new_m_ssm_lambda_re', 'new_m_ssm_lambda_im', 'new_m_ssm_log_dt', 'new_m_ssm_b_re', 'new_m_ssm_b_im', 'new_m_ssm_c_re', 'new_m_ssm_c_im', 'new_m_ssm_d', 'new_m_ssm_glu_w', 'new_m_ssm_glu_b', 'new_m_ssm_norm_w', 'new_m_w_out', 'new_m_ffn2_norm_w', 'new_m_ffn2_w_gate', 'new_m_ffn2_w_up', 'new_m_ffn2_w_down', 'new_m_final_norm_w', 'new_v_meta_tokens', 'new_v_ffn1_norm_w', 'new_v_ffn1_w_gate', 'new_v_ffn1_w_up', 'new_v_ffn1_w_down', 'new_v_mix_norm_w', 'new_v_w_in', 'new_v_ret_norm_w', 'new_v_ssm_lambda_re', 'new_v_ssm_lambda_im', 'new_v_ssm_log_dt', 'new_v_ssm_b_re', 'new_v_ssm_b_im', 'new_v_ssm_c_re', 'new_v_ssm_c_im', 'new_v_ssm_d', 'new_v_ssm_glu_w', 'new_v_ssm_glu_b', 'new_v_ssm_norm_w', 'new_v_w_out', 'new_v_ffn2_norm_w', 'new_v_ffn2_w_gate', 'new_v_ffn2_w_up', 'new_v_ffn2_w_down', 'new_v_final_norm_w']
TWIN_LEAF_KINDS = {'loss': 'loss', 'grad_x': 'grad_x', 'grad_meta_tokens': 'grad_w', 'grad_ffn1_norm_w': 'grad_w', 'grad_ffn1_w_gate': 'grad_w', 'grad_ffn1_w_up': 'grad_w', 'grad_ffn1_w_down': 'grad_w', 'grad_mix_norm_w': 'grad_w', 'grad_w_in': 'grad_w', 'grad_ret_norm_w': 'grad_w', 'grad_ssm_lambda_re': 'grad_w', 'grad_ssm_lambda_im': 'grad_w', 'grad_ssm_log_dt': 'grad_w', 'grad_ssm_b_re': 'grad_w', 'grad_ssm_b_im': 'grad_w', 'grad_ssm_c_re': 'grad_w', 'grad_ssm_c_im': 'grad_w', 'grad_ssm_d': 'grad_w', 'grad_ssm_glu_w': 'grad_w', 'grad_ssm_glu_b': 'grad_w', 'grad_ssm_norm_w': 'grad_w', 'grad_w_out': 'grad_w', 'grad_ffn2_norm_w': 'grad_w', 'grad_ffn2_w_gate': 'grad_w', 'grad_ffn2_w_up': 'grad_w', 'grad_ffn2_w_down': 'grad_w', 'grad_final_norm_w': 'grad_w', 'delta_meta_tokens': 'delta_w', 'delta_ffn1_norm_w': 'delta_w', 'delta_ffn1_w_gate': 'delta_w', 'delta_ffn1_w_up': 'delta_w', 'delta_ffn1_w_down': 'delta_w', 'delta_mix_norm_w': 'delta_w', 'delta_w_in': 'delta_w', 'delta_ret_norm_w': 'delta_w', 'delta_ssm_lambda_re': 'delta_w', 'delta_ssm_lambda_im': 'delta_w', 'delta_ssm_log_dt': 'delta_w', 'delta_ssm_b_re': 'delta_w', 'delta_ssm_b_im': 'delta_w', 'delta_ssm_c_re': 'delta_w', 'delta_ssm_c_im': 'delta_w', 'delta_ssm_d': 'delta_w', 'delta_ssm_glu_w': 'delta_w', 'delta_ssm_glu_b': 'delta_w', 'delta_ssm_norm_w': 'delta_w', 'delta_w_out': 'delta_w', 'delta_ffn2_norm_w': 'delta_w', 'delta_ffn2_w_gate': 'delta_w', 'delta_ffn2_w_up': 'delta_w', 'delta_ffn2_w_down': 'delta_w', 'delta_final_norm_w': 'delta_w', 'new_m_meta_tokens': 'new_m', 'new_m_ffn1_norm_w': 'new_m', 'new_m_ffn1_w_gate': 'new_m', 'new_m_ffn1_w_up': 'new_m', 'new_m_ffn1_w_down': 'new_m', 'new_m_mix_norm_w': 'new_m', 'new_m_w_in': 'new_m', 'new_m_ret_norm_w': 'new_m', 'new_m_ssm_lambda_re': 'new_m', 'new_m_ssm_lambda_im': 'new_m', 'new_m_ssm_log_dt': 'new_m', 'new_m_ssm_b_re': 'new_m', 'new_m_ssm_b_im': 'new_m', 'new_m_ssm_c_re': 'new_m', 'new_m_ssm_c_im': 'new_m', 'new_m_ssm_d': 'new_m', 'new_m_ssm_glu_w': 'new_m', 'new_m_ssm_glu_b': 'new_m', 'new_m_ssm_norm_w': 'new_m', 'new_m_w_out': 'new_m', 'new_m_ffn2_norm_w': 'new_m', 'new_m_ffn2_w_gate': 'new_m', 'new_m_ffn2_w_up': 'new_m', 'new_m_ffn2_w_down': 'new_m', 'new_m_final_norm_w': 'new_m', 'new_v_meta_tokens': 'new_v', 'new_v_ffn1_norm_w': 'new_v', 'new_v_ffn1_w_gate': 'new_v', 'new_v_ffn1_w_up': 'new_v', 'new_v_ffn1_w_down': 'new_v', 'new_v_mix_norm_w': 'new_v', 'new_v_w_in': 'new_v', 'new_v_ret_norm_w': 'new_v', 'new_v_ssm_lambda_re': 'new_v', 'new_v_ssm_lambda_im': 'new_v', 'new_v_ssm_log_dt': 'new_v', 'new_v_ssm_b_re': 'new_v', 'new_v_ssm_b_im': 'new_v', 'new_v_ssm_c_re': 'new_v', 'new_v_ssm_c_im': 'new_v', 'new_v_ssm_d': 'new_v', 'new_v_ssm_glu_w': 'new_v', 'new_v_ssm_glu_b': 'new_v', 'new_v_ssm_norm_w': 'new_v', 'new_v_w_out': 'new_v', 'new_v_ffn2_norm_w': 'new_v', 'new_v_ffn2_w_gate': 'new_v', 'new_v_ffn2_w_up': 'new_v', 'new_v_ffn2_w_down': 'new_v', 'new_v_final_norm_w': 'new_v'}


def _forward(args):
    return _fwd_reference(*[args[k] for k in FWD_PARAMS])


def _output_shape():
    def fwd():
        inp = _fwd_setup_inputs(0)
        return _fwd_reference(*[inp[k] for k in FWD_PARAMS])
    out = _jax.eval_shape(fwd)
    return out.shape, out.dtype

N_MICROBATCH = 1
ADAM_LR = 0.001
ADAM_B1 = 0.9
ADAM_B2 = 0.999
ADAM_EPS = 1e-08
ADAM_WD = 0.01
ADAM_STEP = 10
PER_EXAMPLE_BATCH_AXIS = {'x': 0, 'loss_target': 0}
SHARED_INPUTS = []
_WEIGHT_DTYPES = {'meta_tokens': _jnp.float32, 'ffn1_norm_w': _jnp.float32, 'ffn1_w_gate': _jnp.float32, 'ffn1_w_up': _jnp.float32, 'ffn1_w_down': _jnp.float32, 'mix_norm_w': _jnp.float32, 'w_in': _jnp.float32, 'ret_norm_w': _jnp.float32, 'ssm_lambda_re': _jnp.float32, 'ssm_lambda_im': _jnp.float32, 'ssm_log_dt': _jnp.float32, 'ssm_b_re': _jnp.float32, 'ssm_b_im': _jnp.float32, 'ssm_c_re': _jnp.float32, 'ssm_c_im': _jnp.float32, 'ssm_d': _jnp.float32, 'ssm_glu_w': _jnp.float32, 'ssm_glu_b': _jnp.float32, 'ssm_norm_w': _jnp.float32, 'w_out': _jnp.float32, 'ffn2_norm_w': _jnp.float32, 'ffn2_w_gate': _jnp.float32, 'ffn2_w_up': _jnp.float32, 'ffn2_w_down': _jnp.float32, 'final_norm_w': _jnp.float32}
MOMENT_SCALE = {'meta_tokens': 9.079806e-03, 'ffn1_norm_w': 1.249187e-01, 'ffn1_w_gate': 5.298345e-02, 'ffn1_w_up': 5.141912e-02, 'ffn1_w_down': 8.520830e-02, 'mix_norm_w': 2.248674e-01, 'w_in': 1.402467e-01, 'ret_norm_w': 1.194016e-01, 'ssm_lambda_re': 1.074804e-02, 'ssm_lambda_im': 1.051451e-02, 'ssm_log_dt': 1.121164e+01, 'ssm_b_re': 6.642837e-03, 'ssm_b_im': 6.874620e-03, 'ssm_c_re': 1.410689e-02, 'ssm_c_im': 1.370353e-02, 'ssm_d': 2.612838e-01, 'ssm_glu_w': 5.604875e-02, 'ssm_glu_b': 9.218400e-02, 'ssm_norm_w': 2.159208e-01, 'w_out': 1.657442e-01, 'ffn2_norm_w': 8.124954e-02, 'ffn2_w_gate': 3.419686e-02, 'ffn2_w_up': 3.311553e-02, 'ffn2_w_down': 5.527122e-02, 'final_norm_w': 6.421837e+01}


def _to_microbatches(a, axis):
    t = _jnp.moveaxis(a, axis, 0)
    t = t.reshape((N_MICROBATCH, t.shape[0] // N_MICROBATCH) + t.shape[1:])
    return _jnp.moveaxis(t, 1, axis + 1)


def setup_inputs(seed: int = 0) -> dict:
    inp = _fwd_setup_inputs(seed)
    key = _jax.random.fold_in(_jax.random.key(seed), 7919)
    shape, _ = _output_shape()
    out = dict(inp)
    out["loss_target"] = _jax.random.normal(_jax.random.fold_in(key, 0), shape, _jnp.float32)
    for i, name in enumerate(TWIN_WEIGHTS):
        w = inp[name].astype(_jnp.float32)
        if MOMENT_SCALE is None:
            s = _jnp.sqrt(_jnp.mean(_jnp.square(w)) + 1e-30)
        else:
            s = MOMENT_SCALE[name]
        km, kv = _jax.random.split(_jax.random.fold_in(key, i + 1))
        out[name] = w
        out["m_" + name] = s * _jax.random.normal(km, w.shape, _jnp.float32)
        out["v_" + name] = (s * s) * _jax.random.uniform(kv, w.shape, _jnp.float32, 0.5, 1.5)
    if N_MICROBATCH > 1:
        for name, axis in PER_EXAMPLE_BATCH_AXIS.items():
            out[name] = _to_microbatches(out[name], axis)
    return {'x': out['x'], 'meta_tokens': out['meta_tokens'], 'ffn1_norm_w': out['ffn1_norm_w'], 'ffn1_w_gate': out['ffn1_w_gate'], 'ffn1_w_up': out['ffn1_w_up'], 'ffn1_w_down': out['ffn1_w_down'], 'mix_norm_w': out['mix_norm_w'], 'w_in': out['w_in'], 'ret_norm_w': out['ret_norm_w'], 'ssm_lambda_re': out['ssm_lambda_re'], 'ssm_lambda_im': out['ssm_lambda_im'], 'ssm_log_dt': out['ssm_log_dt'], 'ssm_b_re': out['ssm_b_re'], 'ssm_b_im': out['ssm_b_im'], 'ssm_c_re': out['ssm_c_re'], 'ssm_c_im': out['ssm_c_im'], 'ssm_d': out['ssm_d'], 'ssm_glu_w': out['ssm_glu_w'], 'ssm_glu_b': out['ssm_glu_b'], 'ssm_norm_w': out['ssm_norm_w'], 'w_out': out['w_out'], 'ffn2_norm_w': out['ffn2_norm_w'], 'ffn2_w_gate': out['ffn2_w_gate'], 'ffn2_w_up': out['ffn2_w_up'], 'ffn2_w_down': out['ffn2_w_down'], 'final_norm_w': out['final_norm_w'], 'loss_target': out['loss_target'], 'm_meta_tokens': out['m_meta_tokens'], 'm_ffn1_norm_w': out['m_ffn1_norm_w'], 'm_ffn1_w_gate': out['m_ffn1_w_gate'], 'm_ffn1_w_up': out['m_ffn1_w_up'], 'm_ffn1_w_down': out['m_ffn1_w_down'], 'm_mix_norm_w': out['m_mix_norm_w'], 'm_w_in': out['m_w_in'], 'm_ret_norm_w': out['m_ret_norm_w'], 'm_ssm_lambda_re': out['m_ssm_lambda_re'], 'm_ssm_lambda_im': out['m_ssm_lambda_im'], 'm_ssm_log_dt': out['m_ssm_log_dt'], 'm_ssm_b_re': out['m_ssm_b_re'], 'm_ssm_b_im': out['m_ssm_b_im'], 'm_ssm_c_re': out['m_ssm_c_re'], 'm_ssm_c_im': out['m_ssm_c_im'], 'm_ssm_d': out['m_ssm_d'], 'm_ssm_glu_w': out['m_ssm_glu_w'], 'm_ssm_glu_b': out['m_ssm_glu_b'], 'm_ssm_norm_w': out['m_ssm_norm_w'], 'm_w_out': out['m_w_out'], 'm_ffn2_norm_w': out['m_ffn2_norm_w'], 'm_ffn2_w_gate': out['m_ffn2_w_gate'], 'm_ffn2_w_up': out['m_ffn2_w_up'], 'm_ffn2_w_down': out['m_ffn2_w_down'], 'm_final_norm_w': out['m_final_norm_w'], 'v_meta_tokens': out['v_meta_tokens'], 'v_ffn1_norm_w': out['v_ffn1_norm_w'], 'v_ffn1_w_gate': out['v_ffn1_w_gate'], 'v_ffn1_w_up': out['v_ffn1_w_up'], 'v_ffn1_w_down': out['v_ffn1_w_down'], 'v_mix_norm_w': out['v_mix_norm_w'], 'v_w_in': out['v_w_in'], 'v_ret_norm_w': out['v_ret_norm_w'], 'v_ssm_lambda_re': out['v_ssm_lambda_re'], 'v_ssm_lambda_im': out['v_ssm_lambda_im'], 'v_ssm_log_dt': out['v_ssm_log_dt'], 'v_ssm_b_re': out['v_ssm_b_re'], 'v_ssm_b_im': out['v_ssm_b_im'], 'v_ssm_c_re': out['v_ssm_c_re'], 'v_ssm_c_im': out['v_ssm_c_im'], 'v_ssm_d': out['v_ssm_d'], 'v_ssm_glu_w': out['v_ssm_glu_w'], 'v_ssm_glu_b': out['v_ssm_glu_b'], 'v_ssm_norm_w': out['v_ssm_norm_w'], 'v_w_out': out['v_w_out'], 'v_ffn2_norm_w': out['v_ffn2_norm_w'], 'v_ffn2_w_gate': out['v_ffn2_w_gate'], 'v_ffn2_w_up': out['v_ffn2_w_up'], 'v_ffn2_w_down': out['v_ffn2_w_down'], 'v_final_norm_w': out['v_final_norm_w']}


def _loss(weights, diff, rest, loss_target):
    with _jax.named_scope("forward"):
        args = {**rest, TWIN_DIFF_INPUT: diff, **{k: w.astype(_WEIGHT_DTYPES[k]) for k, w in weights.items()}}
        y = _forward(args)
    with _jax.named_scope("loss_head"):
        err = _jnp.square(y.astype(_jnp.float32) - loss_target)
        return 0.5 * _jnp.sum(_jnp.mean(err, axis=-1)) if err.ndim else 0.5 * err


def _adamw(w, g, m, v):
    m = ADAM_B1 * m + (1.0 - ADAM_B1) * g
    v = ADAM_B2 * v + (1.0 - ADAM_B2) * _jnp.square(g)
    m_hat = m / (1.0 - ADAM_B1 ** ADAM_STEP)
    v_hat = v / (1.0 - ADAM_B2 ** ADAM_STEP)
    delta = -ADAM_LR * (m_hat / (_jnp.sqrt(v_hat) + ADAM_EPS) + ADAM_WD * w)
    return delta, m, v


def reference(x, meta_tokens, ffn1_norm_w, ffn1_w_gate, ffn1_w_up, ffn1_w_down, mix_norm_w, w_in, ret_norm_w, ssm_lambda_re, ssm_lambda_im, ssm_log_dt, ssm_b_re, ssm_b_im, ssm_c_re, ssm_c_im, ssm_d, ssm_glu_w, ssm_glu_b, ssm_norm_w, w_out, ffn2_norm_w, ffn2_w_gate, ffn2_w_up, ffn2_w_down, final_norm_w, loss_target, m_meta_tokens, m_ffn1_norm_w, m_ffn1_w_gate, m_ffn1_w_up, m_ffn1_w_down, m_mix_norm_w, m_w_in, m_ret_norm_w, m_ssm_lambda_re, m_ssm_lambda_im, m_ssm_log_dt, m_ssm_b_re, m_ssm_b_im, m_ssm_c_re, m_ssm_c_im, m_ssm_d, m_ssm_glu_w, m_ssm_glu_b, m_ssm_norm_w, m_w_out, m_ffn2_norm_w, m_ffn2_w_gate, m_ffn2_w_up, m_ffn2_w_down, m_final_norm_w, v_meta_tokens, v_ffn1_norm_w, v_ffn1_w_gate, v_ffn1_w_up, v_ffn1_w_down, v_mix_norm_w, v_w_in, v_ret_norm_w, v_ssm_lambda_re, v_ssm_lambda_im, v_ssm_log_dt, v_ssm_b_re, v_ssm_b_im, v_ssm_c_re, v_ssm_c_im, v_ssm_d, v_ssm_glu_w, v_ssm_glu_b, v_ssm_norm_w, v_w_out, v_ffn2_norm_w, v_ffn2_w_gate, v_ffn2_w_up, v_ffn2_w_down, v_final_norm_w):
    given = dict(x=x, meta_tokens=meta_tokens, ffn1_norm_w=ffn1_norm_w, ffn1_w_gate=ffn1_w_gate, ffn1_w_up=ffn1_w_up, ffn1_w_down=ffn1_w_down, mix_norm_w=mix_norm_w, w_in=w_in, ret_norm_w=ret_norm_w, ssm_lambda_re=ssm_lambda_re, ssm_lambda_im=ssm_lambda_im, ssm_log_dt=ssm_log_dt, ssm_b_re=ssm_b_re, ssm_b_im=ssm_b_im, ssm_c_re=ssm_c_re, ssm_c_im=ssm_c_im, ssm_d=ssm_d, ssm_glu_w=ssm_glu_w, ssm_glu_b=ssm_glu_b, ssm_norm_w=ssm_norm_w, w_out=w_out, ffn2_norm_w=ffn2_norm_w, ffn2_w_gate=ffn2_w_gate, ffn2_w_up=ffn2_w_up, ffn2_w_down=ffn2_w_down, final_norm_w=final_norm_w, loss_target=loss_target, m_meta_tokens=m_meta_tokens, m_ffn1_norm_w=m_ffn1_norm_w, m_ffn1_w_gate=m_ffn1_w_gate, m_ffn1_w_up=m_ffn1_w_up, m_ffn1_w_down=m_ffn1_w_down, m_mix_norm_w=m_mix_norm_w, m_w_in=m_w_in, m_ret_norm_w=m_ret_norm_w, m_ssm_lambda_re=m_ssm_lambda_re, m_ssm_lambda_im=m_ssm_lambda_im, m_ssm_log_dt=m_ssm_log_dt, m_ssm_b_re=m_ssm_b_re, m_ssm_b_im=m_ssm_b_im, m_ssm_c_re=m_ssm_c_re, m_ssm_c_im=m_ssm_c_im, m_ssm_d=m_ssm_d, m_ssm_glu_w=m_ssm_glu_w, m_ssm_glu_b=m_ssm_glu_b, m_ssm_norm_w=m_ssm_norm_w, m_w_out=m_w_out, m_ffn2_norm_w=m_ffn2_norm_w, m_ffn2_w_gate=m_ffn2_w_gate, m_ffn2_w_up=m_ffn2_w_up, m_ffn2_w_down=m_ffn2_w_down, m_final_norm_w=m_final_norm_w, v_meta_tokens=v_meta_tokens, v_ffn1_norm_w=v_ffn1_norm_w, v_ffn1_w_gate=v_ffn1_w_gate, v_ffn1_w_up=v_ffn1_w_up, v_ffn1_w_down=v_ffn1_w_down, v_mix_norm_w=v_mix_norm_w, v_w_in=v_w_in, v_ret_norm_w=v_ret_norm_w, v_ssm_lambda_re=v_ssm_lambda_re, v_ssm_lambda_im=v_ssm_lambda_im, v_ssm_log_dt=v_ssm_log_dt, v_ssm_b_re=v_ssm_b_re, v_ssm_b_im=v_ssm_b_im, v_ssm_c_re=v_ssm_c_re, v_ssm_c_im=v_ssm_c_im, v_ssm_d=v_ssm_d, v_ssm_glu_w=v_ssm_glu_w, v_ssm_glu_b=v_ssm_glu_b, v_ssm_norm_w=v_ssm_norm_w, v_w_out=v_w_out, v_ffn2_norm_w=v_ffn2_norm_w, v_ffn2_w_gate=v_ffn2_w_gate, v_ffn2_w_up=v_ffn2_w_up, v_ffn2_w_down=v_ffn2_w_down, v_final_norm_w=v_final_norm_w)
    weights = {n: given[n] for n in TWIN_WEIGHTS}
    shared = {n: given[n] for n in SHARED_INPUTS}
    per_example = {n: given[n] for n in ['x']}
    grad_fn = _jax.value_and_grad(_loss, argnums=(0, 1))

    def one_microbatch(ex, loss_target):
        ex = dict(ex)
        diff = ex.pop(TWIN_DIFF_INPUT)
        return grad_fn(weights, diff, {**shared, **ex}, loss_target)

    if N_MICROBATCH == 1:
        loss, (grad_w, grad_x) = one_microbatch(per_example, given["loss_target"])
    else:
        def body(carry, xs):
            loss_sum, grad_sum = carry
            l_k, (gw_k, gx_k) = one_microbatch(xs[0], xs[1])
            with _jax.named_scope("update"):
                return (loss_sum + l_k, _jax.tree.map(_jnp.add, grad_sum, gw_k)), gx_k

        init = (_jnp.zeros((), _jnp.float32), _jax.tree.map(_jnp.zeros_like, weights))
        (loss, grad_w), grad_x = _jax.lax.scan(body, init, (per_example, given["loss_target"]))
    with _jax.named_scope("update"):
        delta_w, new_m, new_v = {}, {}, {}
        for n in TWIN_WEIGHTS:
            delta_w[n], new_m[n], new_v[n] = _adamw(weights[n], grad_w[n], given["m_" + n], given["v_" + n])
    return (loss, grad_x, *[grad_w[n] for n in TWIN_WEIGHTS], *[delta_w[n] for n in TWIN_WEIGHTS],
            *[new_m[n] for n in TWIN_WEIGHTS], *[new_v[n] for n in TWIN_WEIGHTS])
```

```python
import functools
import math

import jax
import jax.numpy as jnp
from jax import lax
from jax.experimental import pallas as pl
from jax.experimental.pallas import tpu as pltpu

N_META = 16
RET_HEADS = 4
HEAD_DIM = 128
SSM_GROUP = 16
SSM_STATE = 64
CHUNK = 128
ROPE_BASE = 10000.0
EPS = 1e-6
FFN_RES = 0.5
N_SEG = 8
N_SEC = 4
N_CHIP = 4
LANE = 128

ADAM_LR = 0.001
ADAM_B1 = 0.9
ADAM_B2 = 0.999
ADAM_EPS = 1e-08
ADAM_WD = 0.01
ADAM_STEP = 10

VMEM_LIMIT = 56 * 1024 * 1024

F32 = jnp.float32
BF16 = jnp.bfloat16
MESH = pl.DeviceIdType.MESH


def _dot(a, b):
    return jnp.dot(a, b, preferred_element_type=F32)


def _dot_nt(a, b):
    return lax.dot_general(a, b, (((1,), (1,)), ((), ())), preferred_element_type=F32)


def _dot_tn(a, b):
    return lax.dot_general(a, b, (((0,), (0,)), ((), ())), preferred_element_type=F32)


def _tile(n, target, mult=64):
    best = None
    t = mult
    while t <= min(n, target):
        if n % t == 0:
            best = t
        t += mult
    assert best is not None, (n, target)
    return best


def _params(sem, vmem=VMEM_LIMIT):
    return pltpu.CompilerParams(dimension_semantics=sem, vmem_limit_bytes=vmem)


def _rms_stats(xf):
    r = lax.rsqrt(jnp.mean(xf * xf, axis=-1, keepdims=True) + EPS)
    return xf * r, r


def _rms_bwd(dy, xh, r, w):
    dxh = dy * w
    return r * (dxh - xh * jnp.mean(dxh * xh, axis=-1, keepdims=True))


def _sigmoid(x):
    return 1.0 / (1.0 + jnp.exp(-x))


GELU_K0 = math.sqrt(2.0 / math.pi)
GELU_K1 = 0.044715


def _exchange(name, arrays, masks, n_slots, src_slotted, dst_slotted, local_copy):
    n = len(arrays)
    n_m = len(masks)

    def slot_of(px, py, pc):
        if n_slots == 8:
            return 4 * px + 2 * py + pc
        if n_slots == 4:
            return 2 * px + py
        return pc

    def body(*refs):
        ins = refs[:n]
        outs = refs[n:2 * n]
        send_sems, recv_sems, loc_sems = refs[2 * n:]
        x, y, c = lax.axis_index("x"), lax.axis_index("y"), lax.axis_index("c")
        me = slot_of(x, y, c)
        copies = []
        for a in range(n):
            if local_copy:
                src = ins[a].at[me] if src_slotted else ins[a]
                loc = pltpu.make_async_copy(src, outs[a].at[me], loc_sems.at[a])
                loc.start()
                copies.append(loc)
            for m, (mx, my, mc) in enumerate(masks):
                px = 1 - x if mx else x
                py = 1 - y if my else y
                pc = 1 - c if mc else c
                peer = slot_of(px, py, pc)
                src = ins[a].at[peer] if src_slotted else ins[a]
                dst = outs[a].at[me] if dst_slotted else outs[a]
                cp = pltpu.make_async_remote_copy(
                    src_ref=src, dst_ref=dst,
                    send_sem=send_sems.at[a * n_m + m], recv_sem=recv_sems.at[a * n_m + m],
                    device_id=(px, py, pc), device_id_type=MESH)
                cp.start()
                copies.append(cp)
        for cp in copies:
            cp.wait()

    out_shape = []
    for a in arrays:
        if dst_slotted and not src_slotted:
            shp = (n_slots,) + a.shape
        elif src_slotted and not dst_slotted:
            shp = a.shape[1:]
        else:
            shp = a.shape
        out_shape.append(jax.ShapeDtypeStruct(shp, a.dtype))
    any_spec = pl.BlockSpec(memory_space=pl.ANY)
    outs = pl.pallas_call(
        body, name=name, out_shape=tuple(out_shape),
        in_specs=[any_spec] * n, out_specs=tuple([any_spec] * n),
        scratch_shapes=[pltpu.SemaphoreType.DMA((n * n_m,)), pltpu.SemaphoreType.DMA((n * n_m,)),
                        pltpu.SemaphoreType.DMA((n,))],
    )(*arrays)
    return list(outs)


CHIP_MASKS = [(1, 0, 0), (0, 1, 0), (1, 1, 0)]
ALL_MASKS = [(0, 0, 1), (0, 1, 0), (0, 1, 1), (1, 0, 0), (1, 0, 1), (1, 1, 0), (1, 1, 1)]
SIB_MASKS = [(0, 0, 1)]


def _allgather_chips(arrays):
    return _exchange("allgather_chips", arrays, CHIP_MASKS, 4, False, True, True)


def _alltoall_chips(arrays):
    return _exchange("alltoall_chips", arrays, CHIP_MASKS, 4, True, True, True)


def _swap_sibling(arrays):
    return _exchange("swap_sibling", arrays, SIB_MASKS, 2, False, False, False)


def _allgather_all(arrays):
    return _exchange("allgather_all", arrays, ALL_MASKS, 8, False, True, True)


def _sum_slots(name, a, out_dtype):
    s, r, c = a.shape
    tr = _tile(r, 512, 8)

    def body(a_ref, o_ref):
        acc = a_ref[0].astype(F32)
        for i in range(1, s):
            acc = acc + a_ref[i].astype(F32)
        o_ref[...] = acc.astype(out_dtype)

    return pl.pallas_call(
        body, name=name, grid=(r // tr,),
        in_specs=[pl.BlockSpec((s, tr, c), lambda i: (0, i, 0))],
        out_specs=pl.BlockSpec((tr, c), lambda i: (i, 0)),
        out_shape=jax.ShapeDtypeStruct((r, c), out_dtype),
        compiler_params=_params(("arbitrary",)),
    )(a)


def _adam_math(w, g, m, v):
    m_new = ADAM_B1 * m + (1.0 - ADAM_B1) * g
    v_new = ADAM_B2 * v + (1.0 - ADAM_B2) * (g * g)
    m_hat = m_new / (1.0 - ADAM_B1 ** ADAM_STEP)
    v_hat = v_new / (1.0 - ADAM_B2 ** ADAM_STEP)
    delta = -ADAM_LR * (m_hat / (jnp.sqrt(v_hat) + ADAM_EPS) + ADAM_WD * w)
    return delta, m_new, v_new


def _adam(name, w, m, v, g_parts):
    r, c = w.shape
    tr = _tile(r, 256, 8)
    n_g = len(g_parts)

    def body(*refs):
        w_ref, m_ref, v_ref = refs[:3]
        g_refs = refs[3:3 + n_g]
        g_out, d_out, m_out, v_out = refs[3 + n_g:]
        g = g_refs[0][...].astype(F32)
        for gr in g_refs[1:]:
            g = g + gr[...].astype(F32)
        delta, m_new, v_new = _adam_math(w_ref[...], g, m_ref[...], v_ref[...])
        g_out[...] = g
        d_out[...] = delta
        m_out[...] = m_new
        v_out[...] = v_new

    spec = pl.BlockSpec((tr, c), lambda i: (i, 0))
    shp = jax.ShapeDtypeStruct((r, c), F32)
    return pl.pallas_call(
        body, name=name, grid=(r // tr,),
        in_specs=[spec] * (3 + n_g), out_specs=(spec,) * 4, out_shape=(shp,) * 4,
        compiler_params=_params(("arbitrary",)),
    )(w, m, v, *g_parts)


def _ffn_fwd(name, h, nw, wg, wu, wd):
    lp, d = h.shape
    nck, _, f = wg.shape
    tm = _tile(lp, 640)
    last = nck - 1

    def body(h_ref, nw_ref, wg_ref, wu_ref, wd_ref, ho_ref, g_ref, u_ref, n_sc, acc_sc):
        k = pl.program_id(1)

        @pl.when(k == 0)
        def _():
            xh, _ = _rms_stats(h_ref[...])
            n_sc[...] = (xh * nw_ref[...]).astype(BF16)
            acc_sc[...] = jnp.zeros_like(acc_sc)

        n = n_sc[...]
        g = _dot(n, wg_ref[0])
        u = _dot(n, wu_ref[0])
        g_ref[0] = g
        u_ref[0] = u
        a = (g * _sigmoid(g) * u).astype(BF16)
        acc_sc[...] += _dot(a, wd_ref[0])

        @pl.when(k == last)
        def _():
            ho_ref[...] = h_ref[...] + FFN_RES * acc_sc[...]

    return pl.pallas_call(
        body, name=name, grid=(lp // tm, nck),
        in_specs=[pl.BlockSpec((tm, d), lambda i, k: (i, 0)),
                  pl.BlockSpec((1, d), lambda i, k: (0, 0)),
                  pl.BlockSpec((1, d, f), lambda i, k: (k, 0, 0)),
                  pl.BlockSpec((1, d, f), lambda i, k: (k, 0, 0)),
                  pl.BlockSpec((1, f, d), lambda i, k: (k, 0, 0))],
        out_specs=(pl.BlockSpec((tm, d), lambda i, k: (i, 0)),
                   pl.BlockSpec((1, tm, f), lambda i, k: (k, i, 0)),
                   pl.BlockSpec((1, tm, f), lambda i, k: (k, i, 0))),
        out_shape=(jax.ShapeDtypeStruct((lp, d), F32),
                   jax.ShapeDtypeStruct((nck, lp, f), F32),
                   jax.ShapeDtypeStruct((nck, lp, f), F32)),
        scratch_shapes=[pltpu.VMEM((tm, d), BF16), pltpu.VMEM((tm, d), F32)],
        compiler_params=_params(("arbitrary", "arbitrary")),
    )(h, nw, wg, wu, wd)


def _ffn_bwd_act(name, dh, h, nw, g, u, wg, wu, wd):
    lp, d = h.shape
    nck, _, f = wg.shape
    tm = _tile(lp, 320)
    last = nck - 1

    def body(dh_ref, h_ref, nw_ref, g_ref, u_ref, wg_ref, wu_ref, wd_ref,
             dhi_ref, dnw_ref, n_ref, dacc_ref, a_ref, dg_ref, du_ref,
             xh_sc, r_sc, dn_sc):
        i = pl.program_id(0)
        k = pl.program_id(1)

        @pl.when(k == 0)
        def _():
            xh, r = _rms_stats(h_ref[...])
            xh_sc[...] = xh
            r_sc[...] = r
            n_ref[...] = (xh * nw_ref[...]).astype(BF16)
            dacc_ref[...] = (FFN_RES * dh_ref[...]).astype(BF16)
            dn_sc[...] = jnp.zeros_like(dn_sc)

        @pl.when(jnp.logical_and(i == 0, k == 0))
        def _():
            dnw_ref[...] = jnp.zeros_like(dnw_ref)

        gv = g_ref[0]
        uv = u_ref[0]
        sg = _sigmoid(gv)
        sil = gv * sg
        da = _dot_nt(dacc_ref[...], wd_ref[0])
        dgk = (da * uv * (sg * (1.0 + gv * (1.0 - sg)))).astype(BF16)
        duk = (da * sil).astype(BF16)
        a_ref[0] = (sil * uv).astype(BF16)
        dg_ref[0] = dgk
        du_ref[0] = duk
        dn_sc[...] += _dot_nt(dgk, wg_ref[0]) + _dot_nt(duk, wu_ref[0])

        @pl.when(k == last)
        def _():
            dn = dn_sc[...]
            xh = xh_sc[...]
            dhi_ref[...] = dh_ref[...] + _rms_bwd(dn, xh, r_sc[...], nw_ref[...])
            dnw_ref[...] += jnp.sum(dn * xh, axis=0, keepdims=True)

    row = pl.BlockSpec((tm, d), lambda i, k: (i, 0))
    vec = pl.BlockSpec((1, d), lambda i, k: (0, 0))
    hid = pl.BlockSpec((1, tm, f), lambda i, k: (k, i, 0))
    w_df = pl.BlockSpec((1, d, f), lambda i, k: (k, 0, 0))
    w_fd = pl.BlockSpec((1, f, d), lambda i, k: (k, 0, 0))
    return pl.pallas_call(
        body, name=name, grid=(lp // tm, nck),
        in_specs=[row, row, vec, hid, hid, w_df, w_df, w_fd],
        out_specs=(row, vec, row, row, hid, hid, hid),
        out_shape=(jax.ShapeDtypeStruct((lp, d), F32),
                   jax.ShapeDtypeStruct((1, d), F32),
                   jax.ShapeDtypeStruct((lp, d), BF16),
                   jax.ShapeDtypeStruct((lp, d), BF16),
                   jax.ShapeDtypeStruct((nck, lp, f), BF16),
                   jax.ShapeDtypeStruct((nck, lp, f), BF16),
                   jax.ShapeDtypeStruct((nck, lp, f), BF16)),
        scratch_shapes=[pltpu.VMEM((tm, d), F32), pltpu.VMEM((tm, 1), F32), pltpu.VMEM((tm, d), F32)],
        compiler_params=_params(("arbitrary", "arbitrary")),
    )(dh, h, nw, g, u, wg, wu, wd)


def _ffn_bwd_w(name, n, dacc, a, dg, du):
    lp, d = n.shape
    nck, _, f = a.shape
    tm = _tile(lp, 640)
    last = lp // tm - 1

    def body(n_ref, dacc_ref, a_ref, dg_ref, du_ref, dwg_ref, dwu_ref, dwd_ref, ag_sc, au_sc, ad_sc):
        i = pl.program_id(1)

        @pl.when(i == 0)
        def _():
            ag_sc[...] = jnp.zeros_like(ag_sc)
            au_sc[...] = jnp.zeros_like(au_sc)
            ad_sc[...] = jnp.zeros_like(ad_sc)

        nv = n_ref[...]
        ag_sc[...] += _dot_tn(nv, dg_ref[0])
        au_sc[...] += _dot_tn(nv, du_ref[0])
        ad_sc[...] += _dot_tn(a_ref[0], dacc_ref[...])

        @pl.when(i == last)
        def _():
            dwg_ref[0] = ag_sc[...].astype(BF16)
            dwu_ref[0] = au_sc[...].astype(BF16)
            dwd_ref[0] = ad_sc[...].astype(BF16)

    row = pl.BlockSpec((tm, d), lambda k, i: (i, 0))
    hid = pl.BlockSpec((1, tm, f), lambda k, i: (k, i, 0))
    w_df = pl.BlockSpec((1, d, f), lambda k, i: (k, 0, 0))
    w_fd = pl.BlockSpec((1, f, d), lambda k, i: (k, 0, 0))
    return pl.pallas_call(
        body, name=name, grid=(nck, lp // tm),
        in_specs=[row, row, hid, hid, hid],
        out_specs=(w_df, w_df, w_fd),
        out_shape=(jax.ShapeDtypeStruct((nck, d, f), BF16),
                   jax.ShapeDtypeStruct((nck, d, f), BF16),
                   jax.ShapeDtypeStruct((nck, f, d), BF16)),
        scratch_shapes=[pltpu.VMEM((d, f), F32), pltpu.VMEM((d, f), F32), pltpu.VMEM((f, d), F32)],
        compiler_params=_params(("arbitrary", "arbitrary")),
    )(n, dacc, a, dg, du)


def _inproj_fwd(h, nw, w_in, cosf, sinf, rw):
    lp, d = h.shape
    nck, _, ps = w_in.shape
    proj = nck * ps
    sw = proj - 4 * rw
    tm = _tile(lp, 640)
    scale = HEAD_DIM ** -0.5
    heads = rw // HEAD_DIM

    def body(h_ref, nw_ref, w_ref, cos_ref, sin_ref, n_ref, q_ref, k_ref, v_ref, g_ref, u_ref, p_sc):
        xh, _ = _rms_stats(h_ref[...])
        n = (xh * nw_ref[...]).astype(BF16)
        n_ref[...] = n
        for c in range(nck):
            p_sc[:, c * ps:(c + 1) * ps] = _dot(n, w_ref[c])
        cs = cos_ref[...]
        sn = sin_ref[...]
        for hh in range(heads):
            lo = hh * HEAD_DIM
            qh = p_sc[:, lo:lo + HEAD_DIM]
            q_ref[:, lo:lo + HEAD_DIM] = (qh * cs + pltpu.roll(qh, HEAD_DIM // 2, 1) * sn).astype(BF16)
            kh = p_sc[:, rw + lo:rw + lo + HEAD_DIM]
            k_ref[:, lo:lo + HEAD_DIM] = ((kh * cs + pltpu.roll(kh, HEAD_DIM // 2, 1) * sn) * scale).astype(BF16)
        v_ref[...] = p_sc[:, 2 * rw:3 * rw].astype(BF16)
        g_ref[...] = p_sc[:, 3 * rw:4 * rw]
        u_ref[...] = p_sc[:, 4 * rw:]

    row = lambda w: pl.BlockSpec((tm, w), lambda i: (i, 0))
    return pl.pallas_call(
        body, name="inproj_fwd", grid=(lp // tm,),
        in_specs=[row(d), pl.BlockSpec((1, d), lambda i: (0, 0)),
                  pl.BlockSpec((nck, d, ps), lambda i: (0, 0, 0)), row(HEAD_DIM), row(HEAD_DIM)],
        out_specs=(row(d), row(rw), row(rw), row(rw), row(rw), row(sw)),
        out_shape=(jax.ShapeDtypeStruct((lp, d), BF16),
                   jax.ShapeDtypeStruct((lp, rw), BF16),
                   jax.ShapeDtypeStruct((lp, rw), BF16),
                   jax.ShapeDtypeStruct((lp, rw), BF16),
                   jax.ShapeDtypeStruct((lp, rw), F32),
                   jax.ShapeDtypeStruct((lp, sw), F32)),
        scratch_shapes=[pltpu.VMEM((tm, proj), F32)],
        compiler_params=_params(("arbitrary",)),
    )(h, nw, w_in, cosf, sinf)


def _inproj_bwd(dh, h, nw, n, w_in, dq, dk, dv, dg, du):
    lp, d = h.shape
    nck, _, ps = w_in.shape
    rw = dq.shape[1]
    sw = du.shape[1]
    proj = nck * ps
    tm = _tile(lp, 320)
    last = lp // tm - 1

    def body(dh_ref, h_ref, nw_ref, n_ref, w_ref, dq_ref, dk_ref, dv_ref, dg_ref, du_ref,
             dhi_ref, dnw_ref, dw_ref, p_sc, acc_sc):
        i = pl.program_id(0)

        @pl.when(i == 0)
        def _():
            dnw_ref[...] = jnp.zeros_like(dnw_ref)
            acc_sc[...] = jnp.zeros_like(acc_sc)

        p_sc[:, 0:rw] = dq_ref[...]
        p_sc[:, rw:2 * rw] = dk_ref[...]
        p_sc[:, 2 * rw:3 * rw] = dv_ref[...]
        p_sc[:, 3 * rw:4 * rw] = dg_ref[...]
        p_sc[:, 4 * rw:] = du_ref[...]
        nv = n_ref[...]
        dn = jnp.zeros((tm, d), F32)
        for c in range(nck):
            dp = p_sc[:, c * ps:(c + 1) * ps]
            dn = dn + _dot_nt(dp, w_ref[c])
            acc_sc[c] += _dot_tn(nv, dp)
        xh, r = _rms_stats(h_ref[...])
        dhi_ref[...] = dh_ref[...] + _rms_bwd(dn, xh, r, nw_ref[...])
        dnw_ref[...] += jnp.sum(dn * xh, axis=0, keepdims=True)

        @pl.when(i == last)
        def _():
            dw_ref[...] = acc_sc[...].astype(BF16)

    row = lambda w: pl.BlockSpec((tm, w), lambda i: (i, 0))
    vec = pl.BlockSpec((1, d), lambda i: (0, 0))
    wsp = pl.BlockSpec((nck, d, ps), lambda i: (0, 0, 0))
    return pl.pallas_call(
        body, name="inproj_bwd", grid=(lp // tm,),
        in_specs=[row(d), row(d), vec, row(d), wsp, row(rw), row(rw), row(rw), row(rw), row(sw)],
        out_specs=(row(d), vec, wsp),
        out_shape=(jax.ShapeDtypeStruct((lp, d), F32),
                   jax.ShapeDtypeStruct((1, d), F32),
                   jax.ShapeDtypeStruct((nck, d, ps), BF16)),
        scratch_shapes=[pltpu.VMEM((tm, proj), BF16), pltpu.VMEM((nck, d, ps), F32)],
        compiler_params=_params(("arbitrary",)),
    )(dh, h, nw, n, w_in, dq, dk, dv, dg, du)


def _retention_tables():
    h = jnp.arange(RET_HEADS, dtype=F32)
    log_g = jnp.log(1.0 - 2.0 ** (-5.0 - h))
    i = jnp.arange(CHUNK)
    diff = i[:, None] - i[None, :]
    dec = jnp.where(diff[None] >= 0,
                    jnp.exp(log_g[:, None, None] * jnp.maximum(diff, 0)[None].astype(F32)), 0.0)
    pos = jnp.arange(CHUNK, dtype=F32)
    wq = jnp.exp(log_g[:, None] * (pos + 1.0)[None])
    wk = jnp.exp(log_g[:, None] * (CHUNK - 1 - pos)[None])
    gch = jnp.exp(log_g * CHUNK)
    ones = jnp.ones((1, 1, HEAD_DIM), F32)
    return (dec, wq[:, :, None] * ones, wk[:, :, None] * ones,
            gch[:, None, None] * jnp.ones((1, 8, HEAD_DIM), F32))


def _head_norm(o):
    mu = jnp.mean(o, axis=-1, keepdims=True)
    oc = o - mu
    r = lax.rsqrt(jnp.mean(oc * oc, axis=-1, keepdims=True) + EPS)
    return oc * r, r


def _ret_fwd(q, k, v, g, rnw, tables):
    lp, rw = q.shape
    heads = rw // HEAD_DIM
    nch = lp // CHUNK
    dec, wq, wk, gch = tables

    def body(q_ref, k_ref, v_ref, g_ref, w_ref, dec_ref, wq_ref, wk_ref, gch_ref,
             o_ref, ret_ref, sp_ref, s_sc):
        n = pl.program_id(1)

        @pl.when(n == 0)
        def _():
            s_sc[...] = jnp.zeros_like(s_sc)

        qv, kv, vv = q_ref[...], k_ref[...], v_ref[...]
        s_in = s_sc[...]
        a = _dot_nt(qv, kv) * dec_ref[0]
        qw = (qv.astype(F32) * wq_ref[0]).astype(BF16)
        kw = (kv.astype(F32) * wk_ref[0]).astype(BF16)
        o = _dot(a.astype(BF16), vv) + _dot(qw, s_in.astype(BF16))
        sp_ref[0, 0] = s_in
        s_sc[...] = gch_ref[0, 0:1, :] * s_in + _dot_tn(kw, vv)
        o_ref[...] = o
        xh, _ = _head_norm(o)
        gv = g_ref[...]
        ret_ref[...] = (gv * _sigmoid(gv) * (xh * w_ref[...])).astype(BF16)

    blk = pl.BlockSpec((CHUNK, HEAD_DIM), lambda h, n: (n, h))
    tab = pl.BlockSpec((1, CHUNK, HEAD_DIM), lambda h, n: (h, 0, 0))
    return pl.pallas_call(
        body, name="retention_fwd", grid=(heads, nch),
        in_specs=[blk, blk, blk, blk, pl.BlockSpec((1, HEAD_DIM), lambda h, n: (0, h)),
                  tab, tab, tab, pl.BlockSpec((1, 8, HEAD_DIM), lambda h, n: (h, 0, 0))],
        out_specs=(blk, blk, pl.BlockSpec((1, 1, HEAD_DIM, HEAD_DIM), lambda h, n: (h, n, 0, 0))),
        out_shape=(jax.ShapeDtypeStruct((lp, rw), F32),
                   jax.ShapeDtypeStruct((lp, rw), BF16),
                   jax.ShapeDtypeStruct((heads, nch, HEAD_DIM, HEAD_DIM), F32)),
        scratch_shapes=[pltpu.VMEM((HEAD_DIM, HEAD_DIM), F32)],
        compiler_params=_params(("arbitrary", "arbitrary")),
    )(q, k, v, g, rnw, dec, wq, wk, gch)


def _ret_bwd(dret, q, k, v, g, o, sprev, rnw, tables, cosf, sinf):
    lp, rw = q.shape
    heads = rw // HEAD_DIM
    nch = lp // CHUNK
    dec, wq, wk, gch = tables
    scale = HEAD_DIM ** -0.5
    half = HEAD_DIM // 2

    def body(dret_ref, q_ref, k_ref, v_ref, g_ref, o_ref, sp_ref, w_ref, dec_ref, wq_ref, wk_ref, gch_ref,
             cos_ref, sin_ref, dq_ref, dk_ref, dv_ref, dg_ref, dw_ref, ds_sc):
        n = pl.program_id(1)

        @pl.when(n == 0)
        def _():
            ds_sc[...] = jnp.zeros_like(ds_sc)
            dw_ref[...] = jnp.zeros_like(dw_ref)

        qv, kv, vv = q_ref[...], k_ref[...], v_ref[...]
        gv = g_ref[...]
        dr = dret_ref[...]
        w = w_ref[...]
        sg = _sigmoid(gv)
        sil = gv * sg
        xh, r = _head_norm(o_ref[...])
        dg_ref[...] = (dr * (xh * w) * (sg * (1.0 + gv * (1.0 - sg)))).astype(BF16)
        dyw = dr * sil
        dw_ref[...] += jnp.sum(dyw * xh, axis=0, keepdims=True)
        dxh = dyw * w
        do = r * (dxh - jnp.mean(dxh, axis=-1, keepdims=True)
                  - xh * jnp.mean(dxh * xh, axis=-1, keepdims=True))
        dob = do.astype(BF16)
        dmask = dec_ref[0]
        wqv = wq_ref[0]
        wkv = wk_ref[0]
        a = (_dot_nt(qv, kv) * dmask).astype(BF16)
        da = (_dot_nt(dob, vv) * dmask).astype(BF16)
        qw = (qv.astype(F32) * wqv).astype(BF16)
        kw = (kv.astype(F32) * wkv).astype(BF16)
        s_in = sp_ref[0, 0].astype(BF16)
        ds = ds_sc[...]
        dsb = ds.astype(BF16)
        dq = _dot(da, kv) + _dot_nt(dob, s_in) * wqv
        dk = _dot_tn(da, qv) + _dot_nt(vv, dsb) * wkv
        dv = _dot_tn(a, dob) + _dot(kw, dsb)
        ds_sc[...] = gch_ref[0, 0:1, :] * ds + _dot_tn(qw, dob)
        cs = cos_ref[...]
        sn = sin_ref[...]
        dq_ref[...] = (dq * cs + pltpu.roll(dq * sn, half, 1)).astype(BF16)
        dk_ref[...] = ((dk * cs + pltpu.roll(dk * sn, half, 1)) * scale).astype(BF16)
        dv_ref[...] = dv.astype(BF16)

    rev = lambda h, n: (nch - 1 - n, h)
    blk = pl.BlockSpec((CHUNK, HEAD_DIM), rev)
    tab = pl.BlockSpec((1, CHUNK, HEAD_DIM), lambda h, n: (h, 0, 0))
    wsp = pl.BlockSpec((1, HEAD_DIM), lambda h, n: (0, h))
    pos = pl.BlockSpec((CHUNK, HEAD_DIM), lambda h, n: (nch - 1 - n, 0))
    bshape = jax.ShapeDtypeStruct((lp, rw), BF16)
    return pl.pallas_call(
        body, name="retention_bwd", grid=(heads, nch),
        in_specs=[blk, blk, blk, blk, blk, blk,
                  pl.BlockSpec((1, 1, HEAD_DIM, HEAD_DIM), lambda h, n: (h, nch - 1 - n, 0, 0)),
                  wsp, tab, tab, tab, pl.BlockSpec((1, 8, HEAD_DIM), lambda h, n: (h, 0, 0)), pos, pos],
        out_specs=(blk, blk, blk, blk, wsp),
        out_shape=(bshape, bshape, bshape, bshape, jax.ShapeDtypeStruct((1, rw), F32)),
        scratch_shapes=[pltpu.VMEM((HEAD_DIM, HEAD_DIM), F32)],
        compiler_params=_params(("arbitrary", "arbitrary")),
    )(dret, q, k, v, g, o, sprev, rnw, dec, wq, wk, gch, cosf, sinf)


SCAN_CW = 512


def _s5_prepare(lam_re, lam_im, log_dt, b_re, b_im):
    dt = jnp.exp(log_dt)[:, None]
    er = jnp.exp(lam_re * dt)
    ar = er * jnp.cos(lam_im * dt)
    ai = er * jnp.sin(lam_im * dt)
    den = lam_re * lam_re + lam_im * lam_im
    fr = ((ar - 1.0) * lam_re + ai * lam_im) / den
    fi = (ai * lam_re - (ar - 1.0) * lam_im) / den
    bbr = fr[..., None] * b_re - fi[..., None] * b_im
    bbi = fr[..., None] * b_im + fi[..., None] * b_re
    return ar, ai, bbr, bbi


def _blockdiag_in(t):
    g, p, n = t.shape
    gs = g // N_SEC
    t = t.reshape(N_SEC, gs, p, n)
    eye = jnp.eye(gs, dtype=t.dtype)
    return jnp.einsum("sgpn,gh->sgphn", t, eye).reshape(N_SEC, gs * p, gs * n)


def _blockdiag_out(m, g, p, n):
    gs = g // N_SEC
    m = m.reshape(N_SEC, gs, p, gs, n)
    eye = jnp.eye(gs, dtype=m.dtype)
    return jnp.einsum("sgphn,gh->sgpn", m, eye).reshape(g, p, n)


def _scan_step(xr_ref, xi_ref, r0, pr_of, ar_ref, ai_ref, conj, ncols):
    for cc in range(ncols // SCAN_CW):
        cs = pl.ds(cc * SCAN_CW, SCAN_CW)
        pr, pi = pr_of(cs)
        ar = ar_ref[:, cs]
        ai = ai_ref[:, cs]
        if conj:
            nr = ar * pr + ai * pi
            ni = ar * pi - ai * pr
        else:
            nr = ar * pr - ai * pi
            ni = ar * pi + ai * pr
        xr_ref[pl.ds(r0, 8), cs] = xr_ref[pl.ds(r0, 8), cs] + nr
        xi_ref[pl.ds(r0, 8), cs] = xi_ref[pl.ds(r0, 8), cs] + ni


def _shift_rows(z, down):
    row = lax.broadcasted_iota(jnp.int32, z.shape, 0)
    if down:
        return jnp.where(row == 0, 0.0, pltpu.roll(z, 1, 0))
    return jnp.where(row == N_SEG - 1, 0.0, pltpu.roll(z, N_SEG - 1, 0))


def _s5_fwd(u, bsr, bsi, csr, csi, a8r, a8i, al8r, al8i, d, gluw, glub, nw, jb):
    lp, sw = u.shape
    ns = a8r.shape[1]
    rows = N_SEG * jb
    nblk = lp // rows
    secw = sw // N_SEC
    secn = ns // N_SEC

    def local_scan(u_ref, bsr_ref, bsi_ref, ar_ref, ai_ref, xr_ref, xi_ref, pr_sc, pi_sc):
        for s in range(N_SEC):
            ub = u_ref[:, s * secw:(s + 1) * secw].astype(BF16)
            xr_ref[:, s * secn:(s + 1) * secn] = _dot(ub, bsr_ref[s])
            xi_ref[:, s * secn:(s + 1) * secn] = _dot(ub, bsi_ref[s])
        _scan_step(xr_ref, xi_ref, 0, lambda cs: (pr_sc[:, cs], pi_sc[:, cs]), ar_ref, ai_ref, False, ns)

        def step(j, carry):
            r0 = pl.multiple_of(j * 8, 8)
            rp = pl.multiple_of((j - 1) * 8, 8)
            _scan_step(xr_ref, xi_ref, r0,
                       lambda cs: (xr_ref[pl.ds(rp, 8), cs], xi_ref[pl.ds(rp, 8), cs]),
                       ar_ref, ai_ref, False, ns)
            return carry

        lax.fori_loop(1, jb, step, 0)
        pr_sc[...] = xr_ref[rows - 8:rows, :]
        pi_sc[...] = xi_ref[rows - 8:rows, :]

    def carry_body(u_ref, bsr_ref, bsi_ref, ar_ref, ai_ref, alr_ref, ali_ref, cr_ref, ci_ref,
                   xr_sc, xi_sc, pr_sc, pi_sc):
        b = pl.program_id(0)

        @pl.when(b == 0)
        def _():
            pr_sc[...] = jnp.zeros_like(pr_sc)
            pi_sc[...] = jnp.zeros_like(pi_sc)

        local_scan(u_ref, bsr_ref, bsi_ref, ar_ref, ai_ref, xr_sc, xi_sc, pr_sc, pi_sc)

        @pl.when(b == nblk - 1)
        def _():
            er = _shift_rows(pr_sc[...], True)
            ei = _shift_rows(pi_sc[...], True)
            alr, ali = alr_ref[...], ali_ref[...]
            cr, ci = er, ei
            for _ in range(N_SEG - 2):
                sr = _shift_rows(cr, True)
                si = _shift_rows(ci, True)
                cr = er + alr * sr - ali * si
                ci = ei + alr * si + ali * sr
            cr_ref[...] = cr
            ci_ref[...] = ci

    ublk = pl.BlockSpec((rows, sw), lambda b: (b, 0))
    bspec = pl.BlockSpec((N_SEC, secw, secn), lambda b: (0, 0, 0))
    cspec = pl.BlockSpec((N_SEC, secn, secw), lambda b: (0, 0, 0))
    s8 = pl.BlockSpec((N_SEG, ns), lambda b: (0, 0))
    vec = pl.BlockSpec((1, sw), lambda b: (0, 0))
    s8shape = jax.ShapeDtypeStruct((N_SEG, ns), F32)
    c0r, c0i = pl.pallas_call(
        carry_body, name="s5_fwd_carry", grid=(nblk,),
        in_specs=[ublk, bspec, bspec, s8, s8, s8, s8],
        out_specs=(s8, s8), out_shape=(s8shape, s8shape),
        scratch_shapes=[pltpu.VMEM((rows, ns), F32), pltpu.VMEM((rows, ns), F32),
                        pltpu.VMEM((N_SEG, ns), F32), pltpu.VMEM((N_SEG, ns), F32)],
        compiler_params=_params(("arbitrary",)),
    )(u, bsr, bsi, a8r, a8i, al8r, al8i)

    def main_body(u_ref, bsr_ref, bsi_ref, csr_ref, csi_ref, ar_ref, ai_ref, c0r_ref, c0i_ref,
                  d_ref, gw_ref, gb_ref, nw_ref, xr_ref, xi_ref, yp_ref, out_ref, pr_sc, pi_sc):
        b = pl.program_id(0)

        @pl.when(b == 0)
        def _():
            pr_sc[...] = c0r_ref[...]
            pi_sc[...] = c0i_ref[...]

        local_scan(u_ref, bsr_ref, bsi_ref, ar_ref, ai_ref, xr_ref, xi_ref, pr_sc, pi_sc)
        for s in range(N_SEC):
            xs = pl.ds(s * secn, secn)
            us = pl.ds(s * secw, secw)
            y = _dot(xr_ref[:, xs].astype(BF16), csr_ref[s]) + _dot(xi_ref[:, xs].astype(BF16), csi_ref[s])
            yp_ref[:, us] = y + d_ref[:, us] * u_ref[:, us]
        yp = yp_ref[...]
        t = jnp.tanh(GELU_K0 * (yp + GELU_K1 * yp * yp * yp))
        y1 = 0.5 * yp * (1.0 + t)
        z = _dot(y1.astype(BF16), gw_ref[...]) + gb_ref[...]
        y2 = y1 * _sigmoid(z)
        xh, _ = _rms_stats(y2)
        out_ref[...] = (xh * nw_ref[...]).astype(BF16)

    xblk = pl.BlockSpec((rows, ns), lambda b: (b, 0))
    xr, xi, yp, out = pl.pallas_call(
        main_body, name="s5_fwd", grid=(nblk,),
        in_specs=[ublk, bspec, bspec, cspec, cspec, s8, s8, s8, s8, vec,
                  pl.BlockSpec((sw, sw), lambda b: (0, 0)), vec, vec],
        out_specs=(xblk, xblk, ublk, ublk),
        out_shape=(jax.ShapeDtypeStruct((lp, ns), F32), jax.ShapeDtypeStruct((lp, ns), F32),
                   jax.ShapeDtypeStruct((lp, sw), F32), jax.ShapeDtypeStruct((lp, sw), BF16)),
        scratch_shapes=[pltpu.VMEM((N_SEG, ns), F32), pltpu.VMEM((N_SEG, ns), F32)],
        compiler_params=_params(("arbitrary",)),
    )(u, bsr, bsi, csr, csi, a8r, a8i, c0r, c0i, d, gluw, glub, nw)
    return xr, xi, c0r, c0i, yp, out


def _s5_bwd(dout, u, yp, xr, xi, c0r, c0i, bsrt, bsit, csrt, csit, a8r, a8i, al8r, al8i, d, gluw, glub, nw, jb):
    lp, sw = u.shape
    ns = a8r.shape[1]
    rows = N_SEG * jb
    nblk = lp // rows
    secw = sw // N_SEC
    secn = ns // N_SEC

    def rowwise_bwd(dout_ref, yp_ref, gw_ref, gb_ref, nw_ref):
        ypv = yp_ref[...]
        t = jnp.tanh(GELU_K0 * (ypv + GELU_K1 * ypv * ypv * ypv))
        y1 = 0.5 * ypv * (1.0 + t)
        dgelu = 0.5 * (1.0 + t) + 0.5 * ypv * (1.0 - t * t) * GELU_K0 * (1.0 + 3.0 * GELU_K1 * ypv * ypv)
        gw = gw_ref[...]
        y1b = y1.astype(BF16)
        sg = _sigmoid(_dot(y1b, gw) + gb_ref[...])
        xh, r = _rms_stats(y1 * sg)
        dov = dout_ref[...]
        dy2 = _rms_bwd(dov, xh, r, nw_ref[...])
        dz = dy2 * y1 * sg * (1.0 - sg)
        dzb = dz.astype(BF16)
        dy1 = dy2 * sg + _dot_nt(dzb, gw)
        return dy1 * dgelu, dov * xh, y1b, dzb, dz

    def lam_scan(dyp_of, csrt_ref, csit_ref, ar_ref, ai_ref, lr_sc, li_sc, nr_sc, ni_sc, extra):
        for s in range(N_SEC):
            db = dyp_of(s)
            lr_sc[:, s * secn:(s + 1) * secn] = _dot(db, csrt_ref[s])
            li_sc[:, s * secn:(s + 1) * secn] = _dot(db, csit_ref[s])
        top = rows - 8
        _scan_step(lr_sc, li_sc, top, lambda cs: (nr_sc[:, cs], ni_sc[:, cs]), ar_ref, ai_ref, True, ns)
        extra(top, pl.ds(top - 8, 8))

        def step(jj, carry):
            r0 = pl.multiple_of((jb - 1 - jj) * 8, 8)
            rn = pl.multiple_of((jb - jj) * 8, 8)
            rp = pl.multiple_of((jb - 2 - jj) * 8, 8)
            _scan_step(lr_sc, li_sc, r0,
                       lambda cs: (lr_sc[pl.ds(rn, 8), cs], li_sc[pl.ds(rn, 8), cs]),
                       ar_ref, ai_ref, True, ns)
            extra(r0, pl.ds(rp, 8))
            return carry

        lax.fori_loop(1, jb - 1, step, 0)
        _scan_step(lr_sc, li_sc, 0, lambda cs: (lr_sc[8:16, cs], li_sc[8:16, cs]), ar_ref, ai_ref, True, ns)
        extra(0, None)
        nr_sc[...] = lr_sc[0:8, :]
        ni_sc[...] = li_sc[0:8, :]

    def carry_body(dout_ref, yp_ref, gw_ref, gb_ref, nw_ref, csrt_ref, csit_ref, ar_ref, ai_ref,
                   alr_ref, ali_ref, cr_ref, ci_ref, lr_sc, li_sc, nr_sc, ni_sc, dyp_sc):
        b = pl.program_id(0)

        @pl.when(b == 0)
        def _():
            nr_sc[...] = jnp.zeros_like(nr_sc)
            ni_sc[...] = jnp.zeros_like(ni_sc)

        dyp, _, _, _, _ = rowwise_bwd(dout_ref, yp_ref, gw_ref, gb_ref, nw_ref)
        dyp_sc[...] = dyp.astype(BF16)
        lam_scan(lambda s: dyp_sc[:, s * secw:(s + 1) * secw], csrt_ref, csit_ref, ar_ref, ai_ref,
                 lr_sc, li_sc, nr_sc, ni_sc, lambda r0, prev_rows: None)

        @pl.when(b == nblk - 1)
        def _():
            fr = _shift_rows(nr_sc[...], False)
            fi = _shift_rows(ni_sc[...], False)
            alr, ali = alr_ref[...], ali_ref[...]
            cr, ci = fr, fi
            for _ in range(N_SEG - 2):
                sr = _shift_rows(cr, False)
                si = _shift_rows(ci, False)
                cr = fr + alr * sr + ali * si
                ci = fi + alr * si - ali * sr
            cr_ref[...] = cr
            ci_ref[...] = ci

    rev = lambda b: (nblk - 1 - b, 0)
    ublk = pl.BlockSpec((rows, sw), rev)
    xblk = pl.BlockSpec((rows, ns), rev)
    s8 = pl.BlockSpec((N_SEG, ns), lambda b: (0, 0))
    vec = pl.BlockSpec((1, sw), lambda b: (0, 0))
    gws = pl.BlockSpec((sw, sw), lambda b: (0, 0))
    btspec = pl.BlockSpec((N_SEC, secn, secw), lambda b: (0, 0, 0))
    ctspec = pl.BlockSpec((N_SEC, secw, secn), lambda b: (0, 0, 0))
    s8shape = jax.ShapeDtypeStruct((N_SEG, ns), F32)
    lcr, lci = pl.pallas_call(
        carry_body, name="s5_bwd_carry", grid=(nblk,),
        in_specs=[ublk, ublk, gws, vec, vec, ctspec, ctspec, s8, s8, s8, s8],
        out_specs=(s8, s8), out_shape=(s8shape, s8shape),
        scratch_shapes=[pltpu.VMEM((rows, ns), F32), pltpu.VMEM((rows, ns), F32),
                        pltpu.VMEM((N_SEG, ns), F32), pltpu.VMEM((N_SEG, ns), F32),
                        pltpu.VMEM((rows, sw), BF16)],
        compiler_params=_params(("arbitrary",)),
    )(dout, yp, gluw, glub, nw, csrt, csit, a8r, a8i, al8r, al8i)

    def main_body(dout_ref, yp_ref, u_ref, xr_ref, xi_ref, xtr_ref, xti_ref, c0r_ref, c0i_ref, lcr_ref, lci_ref,
                  gw_ref, gb_ref, nw_ref, d_ref, bsrt_ref, bsit_ref, csrt_ref, csit_ref, ar_ref, ai_ref,
                  du_ref, dnw_ref, dgw_ref, dgb_ref, dd_ref, dcr_ref, dci_ref, dbr_ref, dbi_ref, dar_ref, dai_ref,
                  lr_sc, li_sc, nr_sc, ni_sc, dyp_sc):
        b = pl.program_id(0)

        @pl.when(b == 0)
        def _():
            nr_sc[...] = lcr_ref[...]
            ni_sc[...] = lci_ref[...]
            for ref in (dnw_ref, dgw_ref, dgb_ref, dd_ref, dcr_ref, dci_ref, dbr_ref, dbi_ref, dar_ref, dai_ref):
                ref[...] = jnp.zeros_like(ref)

        dyp, dnw_rows, y1b, dzb, dz = rowwise_bwd(dout_ref, yp_ref, gw_ref, gb_ref, nw_ref)
        dnw_ref[...] += jnp.sum(dnw_rows, axis=0, keepdims=True)
        dgw_ref[...] += _dot_tn(y1b, dzb)
        dgb_ref[...] += jnp.sum(dz, axis=0, keepdims=True)
        uv = u_ref[...]
        dd_ref[...] += jnp.sum(dyp * uv, axis=0, keepdims=True)
        dyp_sc[...] = dyp.astype(BF16)
        for s in range(N_SEC):
            db = dyp_sc[:, s * secw:(s + 1) * secw]
            xs = pl.ds(s * secn, secn)
            dcr_ref[s] += _dot_tn(xr_ref[:, xs].astype(BF16), db)
            dci_ref[s] += _dot_tn(xi_ref[:, xs].astype(BF16), db)

        first = b == nblk - 1

        def acc_da(r0, prev_rows):
            for cc in range(ns // SCAN_CW):
                cs = pl.ds(cc * SCAN_CW, SCAN_CW)
                lr = lr_sc[pl.ds(r0, 8), cs]
                li = li_sc[pl.ds(r0, 8), cs]
                if prev_rows is None:
                    xpr = jnp.where(first, c0r_ref[:, cs], xtr_ref[:, cs])
                    xpi = jnp.where(first, c0i_ref[:, cs], xti_ref[:, cs])
                else:
                    xpr = xr_ref[prev_rows, cs]
                    xpi = xi_ref[prev_rows, cs]
                dar_ref[:, cs] += lr * xpr + li * xpi
                dai_ref[:, cs] += li * xpr - lr * xpi

        lam_scan(lambda s: dyp_sc[:, s * secw:(s + 1) * secw], csrt_ref, csit_ref, ar_ref, ai_ref,
                 lr_sc, li_sc, nr_sc, ni_sc, acc_da)

        for s in range(N_SEC):
            xs = pl.ds(s * secn, secn)
            us = pl.ds(s * secw, secw)
            lrb = lr_sc[:, xs].astype(BF16)
            lib = li_sc[:, xs].astype(BF16)
            du = _dot(lrb, bsrt_ref[s]) + _dot(lib, bsit_ref[s]) + d_ref[:, us] * dyp_sc[:, us].astype(F32)
            du_ref[:, us] = du.astype(BF16)
            ub = u_ref[:, us].astype(BF16)
            dbr_ref[s] += _dot_tn(ub, lrb)
            dbi_ref[s] += _dot_tn(ub, lib)

    tail = pl.BlockSpec((N_SEG, ns), lambda b: (jnp.maximum((nblk - 1 - b) * jb - 1, 0), 0))
    acc_c = pl.BlockSpec((N_SEC, secn, secw), lambda b: (0, 0, 0))
    acc_b = pl.BlockSpec((N_SEC, secw, secn), lambda b: (0, 0, 0))
    outs = pl.pallas_call(
        main_body, name="s5_bwd", grid=(nblk,),
        in_specs=[ublk, ublk, ublk, xblk, xblk, tail, tail, s8, s8, s8, s8,
                  gws, vec, vec, vec, btspec, btspec, ctspec, ctspec, s8, s8],
        out_specs=(ublk, vec, gws, vec, vec, acc_c, acc_c, acc_b, acc_b, s8, s8),
        out_shape=(jax.ShapeDtypeStruct((lp, sw), BF16),
                   jax.ShapeDtypeStruct((1, sw), F32),
                   jax.ShapeDtypeStruct((sw, sw), F32),
                   jax.ShapeDtypeStruct((1, sw), F32),
                   jax.ShapeDtypeStruct((1, sw), F32),
                   jax.ShapeDtypeStruct((N_SEC, secn, secw), F32),
                   jax.ShapeDtypeStruct((N_SEC, secn, secw), F32),
                   jax.ShapeDtypeStruct((N_SEC, secw, secn), F32),
                   jax.ShapeDtypeStruct((N_SEC, secw, secn), F32),
                   s8shape, s8shape),
        scratch_shapes=[pltpu.VMEM((rows, ns), F32), pltpu.VMEM((rows, ns), F32),
                        pltpu.VMEM((N_SEG, ns), F32), pltpu.VMEM((N_SEG, ns), F32),
                        pltpu.VMEM((rows, sw), BF16)],
        compiler_params=_params(("arbitrary",)),
    )(dout, yp, u, xr, xi, xr, xi, c0r, c0i, lcr, lci, gluw, glub, nw, d, bsrt, bsit, csrt, csit, a8r, a8i)
    return outs


def _outproj_fwd(h, ret, ssm, wo):
    lp, d = h.shape
    nck, rs, _ = wo.shape
    rw = ret.shape[1]
    tm = _tile(lp, 640)
    per = rw // rs

    def body(h_ref, ret_ref, ssm_ref, w_ref, o_ref):
        acc = h_ref[...]
        for c in range(nck):
            src = ret_ref if c < per else ssm_ref
            lo = (c % per) * rs
            acc = acc + _dot(src[:, lo:lo + rs], w_ref[c])
        o_ref[...] = acc

    row = lambda w: pl.BlockSpec((tm, w), lambda i: (i, 0))
    return pl.pallas_call(
        body, name="outproj_fwd", grid=(lp // tm,),
        in_specs=[row(d), row(rw), row(ssm.shape[1]), pl.BlockSpec((nck, rs, d), lambda i: (0, 0, 0))],
        out_specs=row(d), out_shape=jax.ShapeDtypeStruct((lp, d), F32),
        compiler_params=_params(("arbitrary",)),
    )(h, ret, ssm, wo)


def _outproj_bwd(dh, ret, ssm, wo):
    lp, d = dh.shape
    nck, rs, _ = wo.shape
    rw = ret.shape[1]
    sw = ssm.shape[1]
    tm = _tile(lp, 640)
    per = rw // rs
    last = lp // tm - 1

    def body(dh_ref, ret_ref, ssm_ref, w_ref, dret_ref, dssm_ref, dw_ref, acc_sc):
        i = pl.program_id(0)

        @pl.when(i == 0)
        def _():
            acc_sc[...] = jnp.zeros_like(acc_sc)

        dhb = dh_ref[...].astype(BF16)
        for c in range(nck):
            src, dst = (ret_ref, dret_ref) if c < per else (ssm_ref, dssm_ref)
            lo = (c % per) * rs
            dst[:, lo:lo + rs] = _dot_nt(dhb, w_ref[c])
            acc_sc[c] += _dot_tn(src[:, lo:lo + rs], dhb)

        @pl.when(i == last)
        def _():
            dw_ref[...] = acc_sc[...].astype(BF16)

    row = lambda w: pl.BlockSpec((tm, w), lambda i: (i, 0))
    wsp = pl.BlockSpec((nck, rs, d), lambda i: (0, 0, 0))
    return pl.pallas_call(
        body, name="outproj_bwd", grid=(lp // tm,),
        in_specs=[row(d), row(rw), row(sw), wsp],
        out_specs=(row(rw), row(sw), wsp),
        out_shape=(jax.ShapeDtypeStruct((lp, rw), F32), jax.ShapeDtypeStruct((lp, sw), F32),
                   jax.ShapeDtypeStruct((nck, rs, d), BF16)),
        scratch_shapes=[pltpu.VMEM((nck, rs, d), F32)],
        compiler_params=_params(("arbitrary",)),
    )(dh, ret, ssm, wo)


def _loss_head(h, fw, target):
    lp, d = h.shape
    nblk = lp // CHUNK

    def body(h_ref, w_ref, t_ref, loss_ref, dh_ref, dw_ref):
        i = pl.program_id(0)

        @pl.when(i == 0)
        def _():
            loss_ref[...] = jnp.zeros_like(loss_ref)
            dw_ref[...] = jnp.zeros_like(dw_ref)
            dh_ref[...] = jnp.zeros_like(dh_ref)

        @pl.when(i > 0)
        def _():
            xh, r = _rms_stats(h_ref[...])
            w = w_ref[...]
            err = xh * w - t_ref[...]
            loss_ref[...] += 0.5 * jnp.sum(err * err) / d
            dout = err * (1.0 / d)
            dw_ref[...] += jnp.sum(dout * xh, axis=0, keepdims=True)
            dh_ref[...] = _rms_bwd(dout, xh, r, w)

    return pl.pallas_call(
        body, name="loss_head", grid=(nblk,),
        in_specs=[pl.BlockSpec((CHUNK, d), lambda i: (i, 0)), pl.BlockSpec((1, d), lambda i: (0, 0)),
                  pl.BlockSpec((CHUNK, d), lambda i: (jnp.maximum(i - 1, 0), 0))],
        out_specs=(pl.BlockSpec((8, LANE), lambda i: (0, 0)), pl.BlockSpec((CHUNK, d), lambda i: (i, 0)),
                   pl.BlockSpec((1, d), lambda i: (0, 0))),
        out_shape=(jax.ShapeDtypeStruct((8, LANE), F32), jax.ShapeDtypeStruct((lp, d), F32),
                   jax.ShapeDtypeStruct((1, d), F32)),
        compiler_params=_params(("arbitrary",)),
    )(h, fw, target)


def _pack(arrs):
    flat = jnp.concatenate([a.reshape(-1).astype(F32) for a in arrs])
    n = flat.shape[0]
    rows = -(-n // (8 * LANE)) * 8
    return jnp.pad(flat, (0, rows * LANE - n)).reshape(rows, LANE)


def _unpack(packed, shapes):
    flat = packed.reshape(-1)
    out, off = [], 0
    for s in shapes:
        n = math.prod(s)
        out.append(flat[off:off + n].reshape(s))
        off += n
    return out


def _to_segments(a, seg_len):
    return a.reshape(N_SEG, seg_len, a.shape[1]).transpose(1, 0, 2).reshape(a.shape)


def _from_segments(a, seg_len):
    return a.reshape(seg_len, N_SEG, a.shape[1]).transpose(1, 0, 2).reshape(a.shape)


WEIGHT_NAMES = ['meta_tokens', 'ffn1_norm_w', 'ffn1_w_gate', 'ffn1_w_up', 'ffn1_w_down', 'mix_norm_w', 'w_in',
                'ret_norm_w', 'ssm_lambda_re', 'ssm_lambda_im', 'ssm_log_dt', 'ssm_b_re', 'ssm_b_im', 'ssm_c_re',
                'ssm_c_im', 'ssm_d', 'ssm_glu_w', 'ssm_glu_b', 'ssm_norm_w', 'w_out', 'ffn2_norm_w', 'ffn2_w_gate',
                'ffn2_w_up', 'ffn2_w_down', 'final_norm_w']
BIG = ['ffn1_w_gate', 'ffn1_w_up', 'ffn1_w_down', 'w_in', 'ssm_glu_w', 'w_out', 'ffn2_w_gate', 'ffn2_w_up',
       'ffn2_w_down']
SMALL = [n for n in WEIGHT_NAMES if n not in BIG]


def kernel(x, meta_tokens, ffn1_norm_w, ffn1_w_gate, ffn1_w_up, ffn1_w_down, mix_norm_w, w_in, ret_norm_w, ssm_lambda_re, ssm_lambda_im, ssm_log_dt, ssm_b_re, ssm_b_im, ssm_c_re, ssm_c_im, ssm_d, ssm_glu_w, ssm_glu_b, ssm_norm_w, w_out, ffn2_norm_w, ffn2_w_gate, ffn2_w_up, ffn2_w_down, final_norm_w, loss_target, m_meta_tokens, m_ffn1_norm_w, m_ffn1_w_gate, m_ffn1_w_up, m_ffn1_w_down, m_mix_norm_w, m_w_in, m_ret_norm_w, m_ssm_lambda_re, m_ssm_lambda_im, m_ssm_log_dt, m_ssm_b_re, m_ssm_b_im, m_ssm_c_re, m_ssm_c_im, m_ssm_d, m_ssm_glu_w, m_ssm_glu_b, m_ssm_norm_w, m_w_out, m_ffn2_norm_w, m_ffn2_w_gate, m_ffn2_w_up, m_ffn2_w_down, m_final_norm_w, v_meta_tokens, v_ffn1_norm_w, v_ffn1_w_gate, v_ffn1_w_up, v_ffn1_w_down, v_mix_norm_w, v_w_in, v_ret_norm_w, v_ssm_lambda_re, v_ssm_lambda_im, v_ssm_log_dt, v_ssm_b_re, v_ssm_b_im, v_ssm_c_re, v_ssm_c_im, v_ssm_d, v_ssm_glu_w, v_ssm_glu_b, v_ssm_norm_w, v_w_out, v_ffn2_norm_w, v_ffn2_w_gate, v_ffn2_w_up, v_ffn2_w_down, v_final_norm_w):
    args = locals()
    w = {n: args[n] for n in WEIGHT_NAMES}
    m = {n: args["m_" + n] for n in WEIGHT_NAMES}
    v = {n: args["v_" + n] for n in WEIGHT_NAMES}

    seq, d = x.shape[1], x.shape[2]
    lp = seq + CHUNK
    seg_len = lp // N_SEG
    rw = RET_HEADS * HEAD_DIM
    sw = ssm_d.shape[-1]
    groups = sw // SSM_GROUP
    ns = groups * SSM_STATE
    jb = _tile(seg_len, 40, 8)
    chip = 2 * lax.axis_index("x") + lax.axis_index("y")

    shards = [w[n][0].astype(BF16) for n in BIG] + [meta_tokens]
    gathered = _allgather_chips(shards)
    gw = dict(zip(BIG, gathered[:-1]))
    meta_full = jnp.transpose(gathered[-1], (1, 0, 2)).reshape(N_META, d)
    glu_full = gw['ssm_glu_w'].reshape(sw, sw)

    pos = jnp.arange(lp, dtype=F32) - float(CHUNK - N_META)
    freqs = 1.0 / (ROPE_BASE ** (jnp.arange(0, HEAD_DIM, 2, dtype=F32) / HEAD_DIM))
    ang = pos[:, None] * freqs[None, :]
    cosf = jnp.concatenate([jnp.cos(ang), jnp.cos(ang)], axis=1)
    sinf = jnp.concatenate([-jnp.sin(ang), jnp.sin(ang)], axis=1)
    tables = _retention_tables()

    lam_re, lam_im, log_dt = ssm_lambda_re[0], ssm_lambda_im[0], ssm_log_dt[0]
    b_re, b_im, c_re, c_im = ssm_b_re[0], ssm_b_im[0], ssm_c_re[0], ssm_c_im[0]
    (ar, ai, bbr, bbi), prep_vjp = jax.vjp(_s5_prepare, lam_re, lam_im, log_dt, b_re, b_im)
    dt = jnp.exp(log_dt)[:, None]
    el = jnp.exp(seg_len * lam_re * dt)
    alr = el * jnp.cos(seg_len * lam_im * dt)
    ali = el * jnp.sin(seg_len * lam_im * dt)
    bc8 = lambda t: jnp.broadcast_to(t.reshape(1, ns), (N_SEG, ns))
    a8r, a8i, al8r, al8i = bc8(ar), bc8(ai), bc8(alr), bc8(ali)
    bsr = _blockdiag_in(jnp.transpose(bbr, (0, 2, 1)))
    bsi = _blockdiag_in(jnp.transpose(bbi, (0, 2, 1)))
    csrt = _blockdiag_in(c_re)
    csit = _blockdiag_in(-c_im)
    tr = lambda t: jnp.transpose(t, (0, 2, 1))
    bsr_b, bsi_b = bsr.astype(BF16), bsi.astype(BF16)
    csr_b, csi_b = tr(csrt).astype(BF16), tr(csit).astype(BF16)
    bsrt_b, bsit_b = tr(bsr).astype(BF16), tr(bsi).astype(BF16)
    csrt_b, csit_b = csrt.astype(BF16), csit.astype(BF16)

    h0 = jnp.concatenate([jnp.zeros((CHUNK - N_META, d), F32), meta_full, x[0]], axis=0)
    h1, g1, u1 = _ffn_fwd("ffn1_fwd", h0, ffn1_norm_w, gw['ffn1_w_gate'], gw['ffn1_w_up'], gw['ffn1_w_down'])
    n2, q, k, vv, gate, u = _inproj_fwd(h1, mix_norm_w, gw['w_in'], cosf, sinf, rw)
    o, ret, sprev = _ret_fwd(q, k, vv, gate, ret_norm_w, tables)
    u_seg = _to_segments(u, seg_len)
    xr, xi, c0r, c0i, yp, ssm_seg = _s5_fwd(u_seg, bsr_b, bsi_b, csr_b, csi_b, a8r, a8i, al8r, al8i,
                                            ssm_d, glu_full, ssm_glu_b, ssm_norm_w, jb)
    ssm = _from_segments(ssm_seg, seg_len)
    h2 = _outproj_fwd(h1, ret, ssm, gw['w_out'])
    h3, g2, u2 = _ffn_fwd("ffn2_fwd", h2, ffn2_norm_w, gw['ffn2_w_gate'], gw['ffn2_w_up'], gw['ffn2_w_down'])
    loss_part, dh3, d_final = _loss_head(h3, final_norm_w.reshape(1, d), loss_target[0])

    dh2, d_ffn2_norm, nb, daccb, ab, dgb, dub = _ffn_bwd_act(
        "ffn2_bwd_act", dh3, h2, ffn2_norm_w, g2, u2, gw['ffn2_w_gate'], gw['ffn2_w_up'], gw['ffn2_w_down'])
    dwg2, dwu2, dwd2 = _ffn_bwd_w("ffn2_bwd_w", nb, daccb, ab, dgb, dub)
    dret, dssm, dwo = _outproj_bwd(dh2, ret, ssm, gw['w_out'])
    (du_seg, d_ssm_norm, d_glu_w, d_glu_b, d_ssm_d, dcr_s, dci_s, dbr_s, dbi_s, dar8, dai8) = _s5_bwd(
        _to_segments(dssm, seg_len), u_seg, yp, xr, xi, c0r, c0i, bsrt_b, bsit_b, csrt_b, csit_b,
        a8r, a8i, al8r, al8i, ssm_d, glu_full, ssm_glu_b, ssm_norm_w, jb)
    du = _from_segments(du_seg, seg_len)
    dq, dk, dv, dgate, d_ret_norm = _ret_bwd(dret, q, k, vv, gate, o, sprev, ret_norm_w, tables, cosf, sinf)
    dh1, d_mix_norm, dwin = _inproj_bwd(dh2, h1, mix_norm_w, n2, gw['w_in'], dq, dk, dv, dgate, du)
    dh0, d_ffn1_norm, nb, daccb, ab, dgb, dub = _ffn_bwd_act(
        "ffn1_bwd_act", dh1, h0, ffn1_norm_w, g1, u1, gw['ffn1_w_gate'], gw['ffn1_w_up'], gw['ffn1_w_down'])
    dwg1, dwu1, dwd1 = _ffn_bwd_w("ffn1_bwd_w", nb, daccb, ab, dgb, dub)
    grad_x = dh0[CHUNK:][None]
    d_meta = dh0[CHUNK - N_META:CHUNK]

    big_parts = {
        'ffn1_w_gate': dwg1, 'ffn1_w_up': dwu1, 'ffn1_w_down': dwd1, 'w_in': dwin,
        'ssm_glu_w': d_glu_w.reshape(N_CHIP, sw // N_CHIP, sw).astype(BF16), 'w_out': dwo,
        'ffn2_w_gate': dwg2, 'ffn2_w_up': dwu2, 'ffn2_w_down': dwd2,
    }
    received = _alltoall_chips([big_parts[n] for n in BIG])
    chip_sums = [_sum_slots("sum_chips_" + n, r, BF16) for n, r in zip(BIG, received)]
    sib_sums = _swap_sibling(chip_sums)

    d_c_re = jnp.transpose(_blockdiag_out(tr(dcr_s), groups, SSM_GROUP, SSM_STATE), (0, 1, 2))
    d_c_im = -_blockdiag_out(tr(dci_s), groups, SSM_GROUP, SSM_STATE)
    d_bbr = jnp.transpose(_blockdiag_out(dbr_s, groups, SSM_GROUP, SSM_STATE), (0, 2, 1))
    d_bbi = jnp.transpose(_blockdiag_out(dbi_s, groups, SSM_GROUP, SSM_STATE), (0, 2, 1))
    d_ar = jnp.sum(dar8, axis=0).reshape(groups, SSM_STATE)
    d_ai = jnp.sum(dai8, axis=0).reshape(groups, SSM_STATE)
    small_parts = [loss_part[0:1, :], d_meta, d_ffn1_norm, d_mix_norm, d_ret_norm, d_ar, d_ai, d_bbr, d_bbi,
                   d_c_re, d_c_im, d_ssm_d, d_glu_b, d_ssm_norm, d_ffn2_norm, d_final]
    small_shapes = [a.shape for a in small_parts]
    all_parts = _allgather_all([_pack(small_parts)])[0]
    (loss_row, g_meta_full, g_ffn1_norm, g_mix_norm, g_ret_norm, g_ar, g_ai, g_bbr, g_bbi, g_c_re, g_c_im,
     g_ssm_d, g_glu_b, g_ssm_norm, g_ffn2_norm, g_final) = _unpack(_sum_slots("sum_small", all_parts, F32), small_shapes)
    g_lam_re, g_lam_im, g_log_dt, g_b_re, g_b_im = prep_vjp((g_ar, g_ai, g_bbr, g_bbi))
    loss = loss_row[0, 0]
    g_meta = lax.dynamic_slice(g_meta_full, (0, chip * (d // N_CHIP)), (N_META, d // N_CHIP))
    small_grads = {
        'meta_tokens': g_meta, 'ffn1_norm_w': g_ffn1_norm, 'mix_norm_w': g_mix_norm, 'ret_norm_w': g_ret_norm,
        'ssm_lambda_re': g_lam_re[None], 'ssm_lambda_im': g_lam_im[None], 'ssm_log_dt': g_log_dt[None],
        'ssm_b_re': g_b_re[None], 'ssm_b_im': g_b_im[None], 'ssm_c_re': g_c_re[None], 'ssm_c_im': g_c_im[None],
        'ssm_d': g_ssm_d, 'ssm_glu_b': g_glu_b, 'ssm_norm_w': g_ssm_norm, 'ffn2_norm_w': g_ffn2_norm,
        'final_norm_w': g_final.reshape(d),
    }

    grads, deltas, new_m, new_v = {}, {}, {}, {}
    for n, mine, sib in zip(BIG, chip_sums, sib_sums):
        shp = w[n].shape
        w2 = w[n].reshape(shp[-2], shp[-1])
        outs = _adam("adam_" + n, w2, m[n].reshape(w2.shape), v[n].reshape(w2.shape), [mine, sib])
        grads[n], deltas[n], new_m[n], new_v[n] = [t.reshape(shp) for t in outs]
    sm_shapes = [w[n].shape for n in SMALL]
    sm_out = _adam("adam_small", _pack([w[n] for n in SMALL]), _pack([m[n] for n in SMALL]),
                   _pack([v[n] for n in SMALL]), [_pack([small_grads[n].reshape(w[n].shape) for n in SMALL])])
    for dst, packed in zip((grads, deltas, new_m, new_v), sm_out):
        for n, t in zip(SMALL, _unpack(packed, sm_shapes)):
            dst[n] = t

    return (loss, grad_x, *[grads[n] for n in WEIGHT_NAMES], *[deltas[n] for n in WEIGHT_NAMES],
            *[new_m[n] for n in WEIGHT_NAMES], *[new_v[n] for n in WEIGHT_NAMES])
```

```python
import functools
import math

import jax
import jax.numpy as jnp
from jax import lax
from jax.experimental import pallas as pl
from jax.experimental.pallas import tpu as pltpu

N_META = 16
RET_HEADS = 4
HEAD_DIM = 128
SSM_GROUP = 16
SSM_STATE = 64
CHUNK = 128
ROPE_BASE = 10000.0
EPS = 1e-6
FFN_RES = 0.5
N_SEG = 8
N_SEC = 4
N_CHIP = 4
LANE = 128

ADAM_LR = 0.001
ADAM_B1 = 0.9
ADAM_B2 = 0.999
ADAM_EPS = 1e-08
ADAM_WD = 0.01
ADAM_STEP = 10

VMEM_LIMIT = 56 * 1024 * 1024

F32 = jnp.float32
BF16 = jnp.bfloat16
MESH = pl.DeviceIdType.MESH


def _dot(a, b):
    return jnp.dot(a, b, preferred_element_type=F32)


def _dot_nt(a, b):
    return lax.dot_general(a, b, (((1,), (1,)), ((), ())), preferred_element_type=F32)


def _dot_tn(a, b):
    return lax.dot_general(a, b, (((0,), (0,)), ((), ())), preferred_element_type=F32)


def _tile(n, target, mult=64):
    best = None
    t = mult
    while t <= min(n, target):
        if n % t == 0:
            best = t
        t += mult
    assert best is not None, (n, target)
    return best


def _params(sem, vmem=VMEM_LIMIT):
    return pltpu.CompilerParams(dimension_semantics=sem, vmem_limit_bytes=vmem)


def _rms_stats(xf):
    r = lax.rsqrt(jnp.mean(xf * xf, axis=-1, keepdims=True) + EPS)
    return xf * r, r


def _rms_bwd(dy, xh, r, w):
    dxh = dy * w
    return r * (dxh - xh * jnp.mean(dxh * xh, axis=-1, keepdims=True))


def _sigmoid(x):
    return 1.0 / (1.0 + jnp.exp(-x))


GELU_K0 = math.sqrt(2.0 / math.pi)
GELU_K1 = 0.044715


CHIP_MASKS = [(1, 0, 0), (0, 1, 0), (1, 1, 0)]
ALL_MASKS = [(0, 0, 1), (0, 1, 0), (0, 1, 1), (1, 0, 0), (1, 0, 1), (1, 1, 0), (1, 1, 1)]
SIB_MASKS = [(0, 0, 1)]
ANY_SPEC = pl.BlockSpec(memory_space=pl.ANY)


class _Plan:
    def __init__(self, arrays, masks, n_slots, src_slotted, dst_slotted, local_copy, half=False, forward=False):
        self.shapes = [(a.shape, a.dtype) for a in arrays]
        self.n = len(arrays)
        self.masks = masks
        self.n_slots = n_slots
        self.src_slotted, self.dst_slotted, self.local_copy = src_slotted, dst_slotted, local_copy
        self.half, self.forward = half, forward
        self.n_cp = self.n * len(masks) * (len(CHIP_MASKS) if forward else 1)

    def out_shape(self):
        out = []
        for shp, dt in self.shapes:
            if self.dst_slotted and not self.src_slotted:
                shp = (self.n_slots,) + shp
            elif self.src_slotted and not self.dst_slotted:
                shp = shp[1:]
            out.append(jax.ShapeDtypeStruct(shp, dt))
        return tuple(out)

    def scratch(self):
        return [pltpu.SemaphoreType.DMA((self.n_cp,)), pltpu.SemaphoreType.DMA((self.n_cp,)),
                pltpu.SemaphoreType.DMA((self.n,))]

    def _slot(self, px, py, pc):
        if self.n_slots == 8:
            return 4 * px + 2 * py + pc
        if self.n_slots == 4:
            return 2 * px + py
        return pc

    def copies(self, ins, outs, sems):
        send_sems, recv_sems, loc_sems = sems
        x, y, c = lax.axis_index("x"), lax.axis_index("y"), lax.axis_index("c")
        me = self._slot(x, y, c)
        n_m = len(self.masks)
        cps = []
        for a in range(self.n):
            if self.forward:
                rows = self.shapes[a][0][-2] // 2
                mine = pl.ds(pl.multiple_of(c * rows, 8), rows)
                for j, (mx, my, _) in enumerate(CHIP_MASKS):
                    blk = outs[a].at[2 * (1 - x if mx else x) + (1 - y if my else y), mine]
                    k = a * len(CHIP_MASKS) + j
                    cps.append(pltpu.make_async_remote_copy(
                        src_ref=blk, dst_ref=blk, send_sem=send_sems.at[k], recv_sem=recv_sems.at[k],
                        device_id=(x, y, 1 - c), device_id_type=MESH))
                continue
            if self.local_copy:
                src = ins[a].at[me] if self.src_slotted else ins[a]
                cps.append(pltpu.make_async_copy(src, outs[a].at[me], loc_sems.at[a]))
            for mi, (mx, my, mc) in enumerate(self.masks):
                px = 1 - x if mx else x
                py = 1 - y if my else y
                pc = 1 - c if mc else c
                src = ins[a].at[self._slot(px, py, pc)] if self.src_slotted else ins[a]
                dst = outs[a].at[me] if self.dst_slotted else outs[a]
                if self.half:
                    rows = src.shape[-2] // 2
                    mine = pl.ds(pl.multiple_of(c * rows, 8), rows)
                    src, dst = src.at[mine], dst.at[mine]
                k = a * n_m + mi
                cps.append(pltpu.make_async_remote_copy(
                    src_ref=src, dst_ref=dst, send_sem=send_sems.at[k], recv_sem=recv_sems.at[k],
                    device_id=(px, py, pc), device_id_type=MESH))
        return cps


def _exchange(name, plan, arrays):
    n = plan.n

    def body(*refs):
        cps = plan.copies(refs[:n], refs[n:2 * n], refs[2 * n:])
        for cp in cps:
            cp.start()
        for cp in cps:
            cp.wait()

    outs = pl.pallas_call(
        body, name=name, out_shape=plan.out_shape(),
        in_specs=[ANY_SPEC] * n, out_specs=tuple([ANY_SPEC] * n), scratch_shapes=plan.scratch(),
        input_output_aliases={i: i for i in range(n)} if plan.forward else {},
    )(*arrays)
    return list(outs)


def _pcall(body, *, name, grid, in_specs, out_specs, out_shape, scratch_shapes, args, plan=None, plan_args=()):
    sem = ("arbitrary",) * len(grid)
    if plan is None:
        return pl.pallas_call(body, name=name, grid=grid, in_specs=in_specs, out_specs=out_specs,
                              out_shape=out_shape, scratch_shapes=scratch_shapes,
                              compiler_params=_params(sem))(*args), []
    n_in, n_out, n_scr, n_p = len(in_specs), len(out_specs), len(scratch_shapes), plan.n

    def wrapped(*refs):
        ins = refs[:n_in]
        p_ins = refs[n_in:n_in + n_p]
        o0 = n_in + n_p
        outs = refs[o0:o0 + n_out]
        p_outs = refs[o0 + n_out:o0 + n_out + n_p]
        s0 = o0 + n_out + n_p
        scr = refs[s0:s0 + n_scr]
        sems = refs[s0 + n_scr:]
        ids = [pl.program_id(i) for i in range(len(grid))]
        first = functools.reduce(jnp.logical_and, [i == 0 for i in ids])
        last = functools.reduce(jnp.logical_and, [i == g - 1 for i, g in zip(ids, grid)])

        @pl.when(first)
        def _():
            for cp in plan.copies(p_ins, p_outs, sems):
                cp.start()

        body(*ins, *outs, *scr)

        @pl.when(last)
        def _():
            for cp in plan.copies(p_ins, p_outs, sems):
                cp.wait()

    res = pl.pallas_call(
        wrapped, name=name, grid=grid,
        in_specs=list(in_specs) + [ANY_SPEC] * n_p,
        out_specs=tuple(out_specs) + (ANY_SPEC,) * n_p,
        out_shape=tuple(out_shape) + plan.out_shape(),
        scratch_shapes=list(scratch_shapes) + plan.scratch(),
        compiler_params=_params(sem),
    )(*args, *plan_args)
    return res[:n_out], list(res[n_out:])


def _allgather_chips_plan(arrays):
    return _Plan(arrays, CHIP_MASKS, 4, False, True, True, half=True)


def _forward_sibling(name, gathered):
    return _exchange(name, _Plan(gathered, SIB_MASKS, 4, True, True, False, forward=True), gathered)


def _alltoall_chips_plan(arrays):
    return _Plan(arrays, CHIP_MASKS, 4, True, True, True)


def _swap_sibling(name, arrays):
    return _exchange(name, _Plan(arrays, SIB_MASKS, 2, False, False, False), arrays)


def _allgather_all_plan(arrays):
    return _Plan(arrays, ALL_MASKS, 8, False, True, True)


def _sum_slots(name, a, out_dtype):
    s, r, c = a.shape
    tr = _tile(r, 512, 8)

    def body(a_ref, o_ref):
        acc = a_ref[0].astype(F32)
        for i in range(1, s):
            acc = acc + a_ref[i].astype(F32)
        o_ref[...] = acc.astype(out_dtype)

    return pl.pallas_call(
        body, name=name, grid=(r // tr,),
        in_specs=[pl.BlockSpec((s, tr, c), lambda i: (0, i, 0))],
        out_specs=pl.BlockSpec((tr, c), lambda i: (i, 0)),
        out_shape=jax.ShapeDtypeStruct((r, c), out_dtype),
        compiler_params=_params(("arbitrary",)),
    )(a)


def _adam_math(w, g, m, v):
    m_new = ADAM_B1 * m + (1.0 - ADAM_B1) * g
    v_new = ADAM_B2 * v + (1.0 - ADAM_B2) * (g * g)
    m_hat = m_new / (1.0 - ADAM_B1 ** ADAM_STEP)
    v_hat = v_new / (1.0 - ADAM_B2 ** ADAM_STEP)
    delta = -ADAM_LR * (m_hat / (jnp.sqrt(v_hat) + ADAM_EPS) + ADAM_WD * w)
    return delta, m_new, v_new


def _adam(name, w, m, v, g_parts):
    r, c = w.shape[-2:]
    tr = _tile(r, 256, 8)
    n_g = len(g_parts)
    lead = w.ndim == 3
    at = (lambda ref: ref.at[0]) if lead else (lambda ref: ref)

    def body(*refs):
        w_ref, m_ref, v_ref = [at(t) for t in refs[:3]]
        g_refs = refs[3:3 + n_g]
        g_out, d_out, m_out, v_out = [at(t) for t in refs[3 + n_g:]]
        g = g_refs[0][...].astype(F32)
        for gr in g_refs[1:]:
            g = g + gr[...].astype(F32)
        delta, m_new, v_new = _adam_math(w_ref[...], g, m_ref[...], v_ref[...])
        g_out[...] = g
        d_out[...] = delta
        m_out[...] = m_new
        v_out[...] = v_new

    spec = pl.BlockSpec((tr, c), lambda i: (i, 0))
    wspec = pl.BlockSpec((1, tr, c), lambda i: (0, i, 0)) if lead else spec
    shp = jax.ShapeDtypeStruct(w.shape, F32)
    return pl.pallas_call(
        body, name=name, grid=(r // tr,),
        in_specs=[wspec] * 3 + [spec] * n_g, out_specs=(wspec,) * 4, out_shape=(shp,) * 4,
        compiler_params=_params(("arbitrary",)),
    )(w, m, v, *g_parts)


def _ffn_fwd(name, h, nw, wg, wu, wd, plan=None, plan_args=()):
    lp, d = h.shape
    nck, _, f = wg.shape
    tm = _tile(lp, 640)
    last = nck - 1

    def body(h_ref, nw_ref, wg_ref, wu_ref, wd_ref, ho_ref, g_ref, u_ref, n_sc, acc_sc):
        k = pl.program_id(1)

        @pl.when(k == 0)
        def _():
            xh, _ = _rms_stats(h_ref[...])
            n_sc[...] = (xh * nw_ref[...]).astype(BF16)
            acc_sc[...] = jnp.zeros_like(acc_sc)

        n = n_sc[...]
        g = _dot(n, wg_ref[0])
        u = _dot(n, wu_ref[0])
        g_ref[0] = g
        u_ref[0] = u
        a = (g * _sigmoid(g) * u).astype(BF16)
        acc_sc[...] += _dot(a, wd_ref[0])

        @pl.when(k == last)
        def _():
            ho_ref[...] = h_ref[...] + FFN_RES * acc_sc[...]

    return _pcall(
        body, name=name, grid=(lp // tm, nck), plan=plan, plan_args=plan_args, args=(h, nw, wg, wu, wd),
        in_specs=[pl.BlockSpec((tm, d), lambda i, k: (i, 0)),
                  pl.BlockSpec((1, d), lambda i, k: (0, 0)),
                  pl.BlockSpec((1, d, f), lambda i, k: (k, 0, 0)),
                  pl.BlockSpec((1, d, f), lambda i, k: (k, 0, 0)),
                  pl.BlockSpec((1, f, d), lambda i, k: (k, 0, 0))],
        out_specs=(pl.BlockSpec((tm, d), lambda i, k: (i, 0)),
                   pl.BlockSpec((1, tm, f), lambda i, k: (k, i, 0)),
                   pl.BlockSpec((1, tm, f), lambda i, k: (k, i, 0))),
        out_shape=(jax.ShapeDtypeStruct((lp, d), F32),
                   jax.ShapeDtypeStruct((nck, lp, f), F32),
                   jax.ShapeDtypeStruct((nck, lp, f), F32)),
        scratch_shapes=[pltpu.VMEM((tm, d), BF16), pltpu.VMEM((tm, d), F32)])


def _ffn_bwd_act(name, dh, h, nw, g, u, wg, wu, wd, plan=None, plan_args=()):
    lp, d = h.shape
    nck, _, f = wg.shape
    tm = _tile(lp, 320)
    last = nck - 1

    def body(dh_ref, h_ref, nw_ref, g_ref, u_ref, wg_ref, wu_ref, wd_ref,
             dhi_ref, dnw_ref, n_ref, dacc_ref, a_ref, dg_ref, du_ref,
             xh_sc, r_sc, dn_sc):
        i = pl.program_id(0)
        k = pl.program_id(1)

        @pl.when(k == 0)
        def _():
            xh, r = _rms_stats(h_ref[...])
            xh_sc[...] = xh
            r_sc[...] = r
            n_ref[...] = (xh * nw_ref[...]).astype(BF16)
            dacc_ref[...] = (FFN_RES * dh_ref[...]).astype(BF16)
            dn_sc[...] = jnp.zeros_like(dn_sc)

        @pl.when(jnp.logical_and(i == 0, k == 0))
        def _():
            dnw_ref[...] = jnp.zeros_like(dnw_ref)

        gv = g_ref[0]
        uv = u_ref[0]
        sg = _sigmoid(gv)
        sil = gv * sg
        da = _dot_nt(dacc_ref[...], wd_ref[0])
        dgk = (da * uv * (sg * (1.0 + gv * (1.0 - sg)))).astype(BF16)
        duk = (da * sil).astype(BF16)
        a_ref[0] = (sil * uv).astype(BF16)
        dg_ref[0] = dgk
        du_ref[0] = duk
        dn_sc[...] += _dot_nt(dgk, wg_ref[0]) + _dot_nt(duk, wu_ref[0])

        @pl.when(k == last)
        def _():
            dn = dn_sc[...]
            xh = xh_sc[...]
            dhi_ref[...] = dh_ref[...] + _rms_bwd(dn, xh, r_sc[...], nw_ref[...])
            dnw_ref[...] += jnp.sum(dn * xh, axis=0, keepdims=True)

    row = pl.BlockSpec((tm, d), lambda i, k: (i, 0))
    vec = pl.BlockSpec((1, d), lambda i, k: (0, 0))
    hid = pl.BlockSpec((1, tm, f), lambda i, k: (k, i, 0))
    w_df = pl.BlockSpec((1, d, f), lambda i, k: (k, 0, 0))
    w_fd = pl.BlockSpec((1, f, d), lambda i, k: (k, 0, 0))
    return _pcall(
        body, name=name, grid=(lp // tm, nck), plan=plan, plan_args=plan_args, args=(dh, h, nw, g, u, wg, wu, wd),
        in_specs=[row, row, vec, hid, hid, w_df, w_df, w_fd],
        out_specs=(row, vec, row, row, hid, hid, hid),
        out_shape=(jax.ShapeDtypeStruct((lp, d), F32),
                   jax.ShapeDtypeStruct((1, d), F32),
                   jax.ShapeDtypeStruct((lp, d), BF16),
                   jax.ShapeDtypeStruct((lp, d), BF16),
                   jax.ShapeDtypeStruct((nck, lp, f), BF16),
                   jax.ShapeDtypeStruct((nck, lp, f), BF16),
                   jax.ShapeDtypeStruct((nck, lp, f), BF16)),
        scratch_shapes=[pltpu.VMEM((tm, d), F32), pltpu.VMEM((tm, 1), F32), pltpu.VMEM((tm, d), F32)])


def _ffn_bwd_w(name, n, dacc, a, dg, du, plan=None, plan_args=()):
    lp, d = n.shape
    nck, _, f = a.shape
    tm = _tile(lp, 640)
    last = lp // tm - 1

    def body(n_ref, dacc_ref, a_ref, dg_ref, du_ref, dwg_ref, dwu_ref, dwd_ref, ag_sc, au_sc, ad_sc):
        i = pl.program_id(1)

        @pl.when(i == 0)
        def _():
            ag_sc[...] = jnp.zeros_like(ag_sc)
            au_sc[...] = jnp.zeros_like(au_sc)
            ad_sc[...] = jnp.zeros_like(ad_sc)

        nv = n_ref[...]
        ag_sc[...] += _dot_tn(nv, dg_ref[0])
        au_sc[...] += _dot_tn(nv, du_ref[0])
        ad_sc[...] += _dot_tn(a_ref[0], dacc_ref[...])

        @pl.when(i == last)
        def _():
            dwg_ref[0] = ag_sc[...].astype(BF16)
            dwu_ref[0] = au_sc[...].astype(BF16)
            dwd_ref[0] = ad_sc[...].astype(BF16)

    row = pl.BlockSpec((tm, d), lambda k, i: (i, 0))
    hid = pl.BlockSpec((1, tm, f), lambda k, i: (k, i, 0))
    w_df = pl.BlockSpec((1, d, f), lambda k, i: (k, 0, 0))
    w_fd = pl.BlockSpec((1, f, d), lambda k, i: (k, 0, 0))
    return _pcall(
        body, name=name, grid=(nck, lp // tm), plan=plan, plan_args=plan_args, args=(n, dacc, a, dg, du),
        in_specs=[row, row, hid, hid, hid],
        out_specs=(w_df, w_df, w_fd),
        out_shape=(jax.ShapeDtypeStruct((nck, d, f), BF16),
                   jax.ShapeDtypeStruct((nck, d, f), BF16),
                   jax.ShapeDtypeStruct((nck, f, d), BF16)),
        scratch_shapes=[pltpu.VMEM((d, f), F32), pltpu.VMEM((d, f), F32), pltpu.VMEM((f, d), F32)])


def _inproj_fwd(h, nw, w_in, cosf, sinf, rw):
    lp, d = h.shape
    nck, _, ps = w_in.shape
    proj = nck * ps
    sw = proj - 4 * rw
    tm = _tile(lp, 640)
    scale = HEAD_DIM ** -0.5
    heads = rw // HEAD_DIM

    def body(h_ref, nw_ref, w_ref, cos_ref, sin_ref, n_ref, q_ref, k_ref, v_ref, g_ref, u_ref, p_sc):
        xh, _ = _rms_stats(h_ref[...])
        n = (xh * nw_ref[...]).astype(BF16)
        n_ref[...] = n
        for c in range(nck):
            p_sc[:, c * ps:(c + 1) * ps] = _dot(n, w_ref[c])
        cs = cos_ref[...]
        sn = sin_ref[...]
        for hh in range(heads):
            lo = hh * HEAD_DIM
            qh = p_sc[:, lo:lo + HEAD_DIM]
            q_ref[:, lo:lo + HEAD_DIM] = (qh * cs + pltpu.roll(qh, HEAD_DIM // 2, 1) * sn).astype(BF16)
            kh = p_sc[:, rw + lo:rw + lo + HEAD_DIM]
            k_ref[:, lo:lo + HEAD_DIM] = ((kh * cs + pltpu.roll(kh, HEAD_DIM // 2, 1) * sn) * scale).astype(BF16)
        v_ref[...] = p_sc[:, 2 * rw:3 * rw].astype(BF16)
        g_ref[...] = p_sc[:, 3 * rw:4 * rw]
        u_ref[...] = p_sc[:, 4 * rw:]

    row = lambda w: pl.BlockSpec((tm, w), lambda i: (i, 0))
    return pl.pallas_call(
        body, name="inproj_fwd", grid=(lp // tm,),
        in_specs=[row(d), pl.BlockSpec((1, d), lambda i: (0, 0)),
                  pl.BlockSpec((nck, d, ps), lambda i: (0, 0, 0)), row(HEAD_DIM), row(HEAD_DIM)],
        out_specs=(row(d), row(rw), row(rw), row(rw), row(rw), row(sw)),
        out_shape=(jax.ShapeDtypeStruct((lp, d), BF16),
                   jax.ShapeDtypeStruct((lp, rw), BF16),
                   jax.ShapeDtypeStruct((lp, rw), BF16),
                   jax.ShapeDtypeStruct((lp, rw), BF16),
                   jax.ShapeDtypeStruct((lp, rw), F32),
                   jax.ShapeDtypeStruct((lp, sw), F32)),
        scratch_shapes=[pltpu.VMEM((tm, proj), F32)],
        compiler_params=_params(("arbitrary",)),
    )(h, nw, w_in, cosf, sinf)


def _inproj_bwd(dh, h, nw, n, w_in, dq, dk, dv, dg, du):
    lp, d = h.shape
    nck, _, ps = w_in.shape
    rw = dq.shape[1]
    sw = du.shape[1]
    proj = nck * ps
    tm = _tile(lp, 320)
    last = lp // tm - 1

    def body(dh_ref, h_ref, nw_ref, n_ref, w_ref, dq_ref, dk_ref, dv_ref, dg_ref, du_ref,
             dhi_ref, dnw_ref, dw_ref, p_sc, acc_sc):
        i = pl.program_id(0)

        @pl.when(i == 0)
        def _():
            dnw_ref[...] = jnp.zeros_like(dnw_ref)
            acc_sc[...] = jnp.zeros_like(acc_sc)

        p_sc[:, 0:rw] = dq_ref[...]
        p_sc[:, rw:2 * rw] = dk_ref[...]
        p_sc[:, 2 * rw:3 * rw] = dv_ref[...]
        p_sc[:, 3 * rw:4 * rw] = dg_ref[...]
        p_sc[:, 4 * rw:] = du_ref[...]
        nv = n_ref[...]
        dn = jnp.zeros((tm, d), F32)
        for c in range(nck):
            dp = p_sc[:, c * ps:(c + 1) * ps]
            dn = dn + _dot_nt(dp, w_ref[c])
            acc_sc[c] += _dot_tn(nv, dp)
        xh, r = _rms_stats(h_ref[...])
        dhi_ref[...] = dh_ref[...] + _rms_bwd(dn, xh, r, nw_ref[...])
        dnw_ref[...] += jnp.sum(dn * xh, axis=0, keepdims=True)

        @pl.when(i == last)
        def _():
            dw_ref[...] = acc_sc[...].astype(BF16)

    row = lambda w: pl.BlockSpec((tm, w), lambda i: (i, 0))
    vec = pl.BlockSpec((1, d), lambda i: (0, 0))
    wsp = pl.BlockSpec((nck, d, ps), lambda i: (0, 0, 0))
    return pl.pallas_call(
        body, name="inproj_bwd", grid=(lp // tm,),
        in_specs=[row(d), row(d), vec, row(d), wsp, row(rw), row(rw), row(rw), row(rw), row(sw)],
        out_specs=(row(d), vec, wsp),
        out_shape=(jax.ShapeDtypeStruct((lp, d), F32),
                   jax.ShapeDtypeStruct((1, d), F32),
                   jax.ShapeDtypeStruct((nck, d, ps), BF16)),
        scratch_shapes=[pltpu.VMEM((tm, proj), BF16), pltpu.VMEM((nck, d, ps), F32)],
        compiler_params=_params(("arbitrary",)),
    )(dh, h, nw, n, w_in, dq, dk, dv, dg, du)


def _retention_tables():
    h = jnp.arange(RET_HEADS, dtype=F32)
    log_g = jnp.log(1.0 - 2.0 ** (-5.0 - h))
    i = jnp.arange(CHUNK)
    diff = i[:, None] - i[None, :]
    dec = jnp.where(diff[None] >= 0,
                    jnp.exp(log_g[:, None, None] * jnp.maximum(diff, 0)[None].astype(F32)), 0.0)
    pos = jnp.arange(CHUNK, dtype=F32)
    wq = jnp.exp(log_g[:, None] * (pos + 1.0)[None])
    wk = jnp.exp(log_g[:, None] * (CHUNK - 1 - pos)[None])
    gch = jnp.exp(log_g * CHUNK)
    ones = jnp.ones((1, 1, HEAD_DIM), F32)
    return (dec, wq[:, :, None] * ones, wk[:, :, None] * ones,
            gch[:, None, None] * jnp.ones((1, 8, HEAD_DIM), F32))


def _head_norm(o):
    mu = jnp.mean(o, axis=-1, keepdims=True)
    oc = o - mu
    r = lax.rsqrt(jnp.mean(oc * oc, axis=-1, keepdims=True) + EPS)
    return oc * r, r


def _ret_fwd(q, k, v, g, rnw, tables):
    lp, rw = q.shape
    heads = rw // HEAD_DIM
    nch = lp // CHUNK
    dec, wq, wk, gch = tables

    def body(q_ref, k_ref, v_ref, g_ref, w_ref, dec_ref, wq_ref, wk_ref, gch_ref,
             o_ref, ret_ref, sp_ref, s_sc):
        n = pl.program_id(0)

        @pl.when(n == 0)
        def _():
            s_sc[...] = jnp.zeros_like(s_sc)

        for hh in range(heads):
            cs = slice(hh * HEAD_DIM, (hh + 1) * HEAD_DIM)
            qv, kv, vv = q_ref[:, cs], k_ref[:, cs], v_ref[:, cs]
            s_in = s_sc[hh]
            a = _dot_nt(qv, kv) * dec_ref[hh]
            qw = (qv.astype(F32) * wq_ref[hh]).astype(BF16)
            kw = (kv.astype(F32) * wk_ref[hh]).astype(BF16)
            o = _dot(a.astype(BF16), vv) + _dot(qw, s_in.astype(BF16))
            sp_ref[hh, 0] = s_in
            s_sc[hh] = gch_ref[hh, 0:1, :] * s_in + _dot_tn(kw, vv)
            o_ref[:, cs] = o
            xh, _ = _head_norm(o)
            gv = g_ref[:, cs]
            ret_ref[:, cs] = (gv * _sigmoid(gv) * (xh * w_ref[:, cs])).astype(BF16)

    blk = pl.BlockSpec((CHUNK, rw), lambda n: (n, 0))
    tab = pl.BlockSpec((heads, CHUNK, HEAD_DIM), lambda n: (0, 0, 0))
    return pl.pallas_call(
        body, name="retention_fwd", grid=(nch,),
        in_specs=[blk, blk, blk, blk, pl.BlockSpec((1, rw), lambda n: (0, 0)),
                  tab, tab, tab, pl.BlockSpec((heads, 8, HEAD_DIM), lambda n: (0, 0, 0))],
        out_specs=(blk, blk, pl.BlockSpec((heads, 1, HEAD_DIM, HEAD_DIM), lambda n: (0, n, 0, 0))),
        out_shape=(jax.ShapeDtypeStruct((lp, rw), F32),
                   jax.ShapeDtypeStruct((lp, rw), BF16),
                   jax.ShapeDtypeStruct((heads, nch, HEAD_DIM, HEAD_DIM), F32)),
        scratch_shapes=[pltpu.VMEM((heads, HEAD_DIM, HEAD_DIM), F32)],
        compiler_params=_params(("arbitrary",)),
    )(q, k, v, g, rnw, dec, wq, wk, gch)


def _ret_bwd(dret, q, k, v, g, o, sprev, rnw, tables, cosf, sinf):
    lp, rw = q.shape
    heads = rw // HEAD_DIM
    nch = lp // CHUNK
    dec, wq, wk, gch = tables
    scale = HEAD_DIM ** -0.5
    half = HEAD_DIM // 2

    def body(dret_ref, q_ref, k_ref, v_ref, g_ref, o_ref, sp_ref, w_ref, dec_ref, wq_ref, wk_ref, gch_ref,
             cos_ref, sin_ref, dq_ref, dk_ref, dv_ref, dg_ref, dw_ref, ds_sc):
        n = pl.program_id(0)

        @pl.when(n == 0)
        def _():
            ds_sc[...] = jnp.zeros_like(ds_sc)
            dw_ref[...] = jnp.zeros_like(dw_ref)

        cosv = cos_ref[...]
        sinv = sin_ref[...]
        for hh in range(heads):
            cs = slice(hh * HEAD_DIM, (hh + 1) * HEAD_DIM)
            qv, kv, vv = q_ref[:, cs], k_ref[:, cs], v_ref[:, cs]
            gv = g_ref[:, cs]
            dr = dret_ref[:, cs]
            w = w_ref[:, cs]
            sg = _sigmoid(gv)
            sil = gv * sg
            xh, r = _head_norm(o_ref[:, cs])
            dg_ref[:, cs] = (dr * (xh * w) * (sg * (1.0 + gv * (1.0 - sg)))).astype(BF16)
            dyw = dr * sil
            dw_ref[:, cs] += jnp.sum(dyw * xh, axis=0, keepdims=True)
            dxh = dyw * w
            do = r * (dxh - jnp.mean(dxh, axis=-1, keepdims=True)
                      - xh * jnp.mean(dxh * xh, axis=-1, keepdims=True))
            dob = do.astype(BF16)
            dmask = dec_ref[hh]
            wqv = wq_ref[hh]
            wkv = wk_ref[hh]
            a = (_dot_nt(qv, kv) * dmask).astype(BF16)
            da = (_dot_nt(dob, vv) * dmask).astype(BF16)
            qw = (qv.astype(F32) * wqv).astype(BF16)
            kw = (kv.astype(F32) * wkv).astype(BF16)
            s_in = sp_ref[hh, 0].astype(BF16)
            ds = ds_sc[hh]
            dsb = ds.astype(BF16)
            dq = _dot(da, kv) + _dot_nt(dob, s_in) * wqv
            dk = _dot_tn(da, qv) + _dot_nt(vv, dsb) * wkv
            dv = _dot_tn(a, dob) + _dot(kw, dsb)
            ds_sc[hh] = gch_ref[hh, 0:1, :] * ds + _dot_tn(qw, dob)
            dq_ref[:, cs] = (dq * cosv + pltpu.roll(dq * sinv, half, 1)).astype(BF16)
            dk_ref[:, cs] = ((dk * cosv + pltpu.roll(dk * sinv, half, 1)) * scale).astype(BF16)
            dv_ref[:, cs] = dv.astype(BF16)

    blk = pl.BlockSpec((CHUNK, rw), lambda n: (nch - 1 - n, 0))
    tab = pl.BlockSpec((heads, CHUNK, HEAD_DIM), lambda n: (0, 0, 0))
    wsp = pl.BlockSpec((1, rw), lambda n: (0, 0))
    pos = pl.BlockSpec((CHUNK, HEAD_DIM), lambda n: (nch - 1 - n, 0))
    bshape = jax.ShapeDtypeStruct((lp, rw), BF16)
    return pl.pallas_call(
        body, name="retention_bwd", grid=(nch,),
        in_specs=[blk, blk, blk, blk, blk, blk,
                  pl.BlockSpec((heads, 1, HEAD_DIM, HEAD_DIM), lambda n: (0, nch - 1 - n, 0, 0)),
                  wsp, tab, tab, tab, pl.BlockSpec((heads, 8, HEAD_DIM), lambda n: (0, 0, 0)), pos, pos],
        out_specs=(blk, blk, blk, blk, wsp),
        out_shape=(bshape, bshape, bshape, bshape, jax.ShapeDtypeStruct((1, rw), F32)),
        scratch_shapes=[pltpu.VMEM((heads, HEAD_DIM, HEAD_DIM), F32)],
        compiler_params=_params(("arbitrary",)),
    )(dret, q, k, v, g, o, sprev, rnw, dec, wq, wk, gch, cosf, sinf)


SCAN_CW = 512


def _s5_prepare(lam_re, lam_im, log_dt, b_re, b_im):
    dt = jnp.exp(log_dt)[:, None]
    er = jnp.exp(lam_re * dt)
    ar = er * jnp.cos(lam_im * dt)
    ai = er * jnp.sin(lam_im * dt)
    den = lam_re * lam_re + lam_im * lam_im
    fr = ((ar - 1.0) * lam_re + ai * lam_im) / den
    fi = (ai * lam_re - (ar - 1.0) * lam_im) / den
    bbr = fr[..., None] * b_re - fi[..., None] * b_im
    bbi = fr[..., None] * b_im + fi[..., None] * b_re
    return ar, ai, bbr, bbi


def _blockdiag_in(t):
    g, p, n = t.shape
    gs = g // N_SEC
    t = t.reshape(N_SEC, gs, p, n)
    eye = jnp.eye(gs, dtype=t.dtype)
    return jnp.einsum("sgpn,gh->sgphn", t, eye).reshape(N_SEC, gs * p, gs * n)


def _blockdiag_out(m, g, p, n):
    gs = g // N_SEC
    m = m.reshape(N_SEC, gs, p, gs, n)
    eye = jnp.eye(gs, dtype=m.dtype)
    return jnp.einsum("sgphn,gh->sgpn", m, eye).reshape(g, p, n)


def _scan_step(xr_ref, xi_ref, r0, pr_of, ar_ref, ai_ref, conj, ncols):
    for cc in range(ncols // SCAN_CW):
        cs = pl.ds(cc * SCAN_CW, SCAN_CW)
        pr, pi = pr_of(cs)
        ar = ar_ref[:, cs]
        ai = ai_ref[:, cs]
        if conj:
            nr = ar * pr + ai * pi
            ni = ar * pi - ai * pr
        else:
            nr = ar * pr - ai * pi
            ni = ar * pi + ai * pr
        xr_ref[pl.ds(r0, 8), cs] = xr_ref[pl.ds(r0, 8), cs] + nr
        xi_ref[pl.ds(r0, 8), cs] = xi_ref[pl.ds(r0, 8), cs] + ni


def _shift_rows(z, down):
    row = lax.broadcasted_iota(jnp.int32, z.shape, 0)
    if down:
        return jnp.where(row == 0, 0.0, pltpu.roll(z, 1, 0))
    return jnp.where(row == N_SEG - 1, 0.0, pltpu.roll(z, N_SEG - 1, 0))


def _s5_fwd(u, bsr, bsi, csr, csi, a8r, a8i, al8r, al8i, d, gluw, glub, nw, jb):
    lp, sw = u.shape
    ns = a8r.shape[1]
    rows = N_SEG * jb
    nblk = lp // rows
    secw = sw // N_SEC
    secn = ns // N_SEC

    def local_scan(u_ref, bsr_ref, bsi_ref, ar_ref, ai_ref, xr_ref, xi_ref, pr_sc, pi_sc):
        for s in range(N_SEC):
            ub = u_ref[:, s * secw:(s + 1) * secw].astype(BF16)
            xr_ref[:, s * secn:(s + 1) * secn] = _dot(ub, bsr_ref[s])
            xi_ref[:, s * secn:(s + 1) * secn] = _dot(ub, bsi_ref[s])
        _scan_step(xr_ref, xi_ref, 0, lambda cs: (pr_sc[:, cs], pi_sc[:, cs]), ar_ref, ai_ref, False, ns)

        def step(j, carry):
            r0 = pl.multiple_of(j * 8, 8)
            rp = pl.multiple_of((j - 1) * 8, 8)
            _scan_step(xr_ref, xi_ref, r0,
                       lambda cs: (xr_ref[pl.ds(rp, 8), cs], xi_ref[pl.ds(rp, 8), cs]),
                       ar_ref, ai_ref, False, ns)
            return carry

        lax.fori_loop(1, jb, step, 0)
        pr_sc[...] = xr_ref[rows - 8:rows, :]
        pi_sc[...] = xi_ref[rows - 8:rows, :]

    def carry_body(u_ref, bsr_ref, bsi_ref, ar_ref, ai_ref, alr_ref, ali_ref, cr_ref, ci_ref,
                   xr_sc, xi_sc, pr_sc, pi_sc):
        b = pl.program_id(0)

        @pl.when(b == 0)
        def _():
            pr_sc[...] = jnp.zeros_like(pr_sc)
            pi_sc[...] = jnp.zeros_like(pi_sc)

        local_scan(u_ref, bsr_ref, bsi_ref, ar_ref, ai_ref, xr_sc, xi_sc, pr_sc, pi_sc)

        @pl.when(b == nblk - 1)
        def _():
            er = _shift_rows(pr_sc[...], True)
            ei = _shift_rows(pi_sc[...], True)
            alr, ali = alr_ref[...], ali_ref[...]
            cr, ci = er, ei
            for _ in range(N_SEG - 2):
                sr = _shift_rows(cr, True)
                si = _shift_rows(ci, True)
                cr = er + alr * sr - ali * si
                ci = ei + alr * si + ali * sr
            cr_ref[...] = cr
            ci_ref[...] = ci

    ublk = pl.BlockSpec((rows, sw), lambda b: (b, 0))
    bspec = pl.BlockSpec((N_SEC, secw, secn), lambda b: (0, 0, 0))
    cspec = pl.BlockSpec((N_SEC, secn, secw), lambda b: (0, 0, 0))
    s8 = pl.BlockSpec((N_SEG, ns), lambda b: (0, 0))
    vec = pl.BlockSpec((1, sw), lambda b: (0, 0))
    s8shape = jax.ShapeDtypeStruct((N_SEG, ns), F32)
    c0r, c0i = pl.pallas_call(
        carry_body, name="s5_fwd_carry", grid=(nblk,),
        in_specs=[ublk, bspec, bspec, s8, s8, s8, s8],
        out_specs=(s8, s8), out_shape=(s8shape, s8shape),
        scratch_shapes=[pltpu.VMEM((rows, ns), F32), pltpu.VMEM((rows, ns), F32),
                        pltpu.VMEM((N_SEG, ns), F32), pltpu.VMEM((N_SEG, ns), F32)],
        compiler_params=_params(("arbitrary",)),
    )(u, bsr, bsi, a8r, a8i, al8r, al8i)

    def main_body(u_ref, bsr_ref, bsi_ref, csr_ref, csi_ref, ar_ref, ai_ref, c0r_ref, c0i_ref,
                  d_ref, gw_ref, gb_ref, nw_ref, xr_ref, xi_ref, yp_ref, out_ref, pr_sc, pi_sc):
        b = pl.program_id(0)

        @pl.when(b == 0)
        def _():
            pr_sc[...] = c0r_ref[...]
            pi_sc[...] = c0i_ref[...]

        local_scan(u_ref, bsr_ref, bsi_ref, ar_ref, ai_ref, xr_ref, xi_ref, pr_sc, pi_sc)
        for s in range(N_SEC):
            xs = pl.ds(s * secn, secn)
            us = pl.ds(s * secw, secw)
            y = _dot(xr_ref[:, xs].astype(BF16), csr_ref[s]) + _dot(xi_ref[:, xs].astype(BF16), csi_ref[s])
            yp_ref[:, us] = y + d_ref[:, us] * u_ref[:, us]
        yp = yp_ref[...]
        t = jnp.tanh(GELU_K0 * (yp + GELU_K1 * yp * yp * yp))
        y1 = 0.5 * yp * (1.0 + t)
        z = _dot(y1.astype(BF16), gw_ref[...]) + gb_ref[...]
        y2 = y1 * _sigmoid(z)
        xh, _ = _rms_stats(y2)
        out_ref[...] = (xh * nw_ref[...]).astype(BF16)

    xblk = pl.BlockSpec((rows, ns), lambda b: (b, 0))
    xr, xi, yp, out = pl.pallas_call(
        main_body, name="s5_fwd", grid=(nblk,),
        in_specs=[ublk, bspec, bspec, cspec, cspec, s8, s8, s8, s8, vec,
                  pl.BlockSpec((sw, sw), lambda b: (0, 0)), vec, vec],
        out_specs=(xblk, xblk, ublk, ublk),
        out_shape=(jax.ShapeDtypeStruct((lp, ns), F32), jax.ShapeDtypeStruct((lp, ns), F32),
                   jax.ShapeDtypeStruct((lp, sw), F32), jax.ShapeDtypeStruct((lp, sw), BF16)),
        scratch_shapes=[pltpu.VMEM((N_SEG, ns), F32), pltpu.VMEM((N_SEG, ns), F32)],
        compiler_params=_params(("arbitrary",)),
    )(u, bsr, bsi, csr, csi, a8r, a8i, c0r, c0i, d, gluw, glub, nw)
    return xr, xi, c0r, c0i, yp, out


def _s5_bwd(dout, u, yp, xr, xi, c0r, c0i, bsrt, bsit, csrt, csit, a8r, a8i, al8r, al8i, d, gluw, glub, nw, jb):
    lp, sw = u.shape
    ns = a8r.shape[1]
    rows = N_SEG * jb
    nblk = lp // rows
    secw = sw // N_SEC
    secn = ns // N_SEC

    def rowwise_bwd(dout_ref, yp_ref, gw_ref, gb_ref, nw_ref):
        ypv = yp_ref[...]
        t = jnp.tanh(GELU_K0 * (ypv + GELU_K1 * ypv * ypv * ypv))
        y1 = 0.5 * ypv * (1.0 + t)
        dgelu = 0.5 * (1.0 + t) + 0.5 * ypv * (1.0 - t * t) * GELU_K0 * (1.0 + 3.0 * GELU_K1 * ypv * ypv)
        gw = gw_ref[...]
        y1b = y1.astype(BF16)
        sg = _sigmoid(_dot(y1b, gw) + gb_ref[...])
        xh, r = _rms_stats(y1 * sg)
        dov = dout_ref[...]
        dy2 = _rms_bwd(dov, xh, r, nw_ref[...])
        dz = dy2 * y1 * sg * (1.0 - sg)
        dzb = dz.astype(BF16)
        dy1 = dy2 * sg + _dot_nt(dzb, gw)
        return dy1 * dgelu, dov * xh, y1b, dzb, dz

    def lam_scan(dyp_of, csrt_ref, csit_ref, ar_ref, ai_ref, lr_sc, li_sc, nr_sc, ni_sc, extra):
        for s in range(N_SEC):
            db = dyp_of(s)
            lr_sc[:, s * secn:(s + 1) * secn] = _dot(db, csrt_ref[s])
            li_sc[:, s * secn:(s + 1) * secn] = _dot(db, csit_ref[s])
        top = rows - 8
        _scan_step(lr_sc, li_sc, top, lambda cs: (nr_sc[:, cs], ni_sc[:, cs]), ar_ref, ai_ref, True, ns)
        extra(top, pl.ds(top - 8, 8))

        def step(jj, carry):
            r0 = pl.multiple_of((jb - 1 - jj) * 8, 8)
            rn = pl.multiple_of((jb - jj) * 8, 8)
            rp = pl.multiple_of((jb - 2 - jj) * 8, 8)
            _scan_step(lr_sc, li_sc, r0,
                       lambda cs: (lr_sc[pl.ds(rn, 8), cs], li_sc[pl.ds(rn, 8), cs]),
                       ar_ref, ai_ref, True, ns)
            extra(r0, pl.ds(rp, 8))
            return carry

        lax.fori_loop(1, jb - 1, step, 0)
        _scan_step(lr_sc, li_sc, 0, lambda cs: (lr_sc[8:16, cs], li_sc[8:16, cs]), ar_ref, ai_ref, True, ns)
        extra(0, None)
        nr_sc[...] = lr_sc[0:8, :]
        ni_sc[...] = li_sc[0:8, :]

    def carry_body(dout_ref, yp_ref, gw_ref, gb_ref, nw_ref, csrt_ref, csit_ref, ar_ref, ai_ref,
                   alr_ref, ali_ref, cr_ref, ci_ref, lr_sc, li_sc, nr_sc, ni_sc, dyp_sc):
        b = pl.program_id(0)

        @pl.when(b == 0)
        def _():
            nr_sc[...] = jnp.zeros_like(nr_sc)
            ni_sc[...] = jnp.zeros_like(ni_sc)

        dyp, _, _, _, _ = rowwise_bwd(dout_ref, yp_ref, gw_ref, gb_ref, nw_ref)
        dyp_sc[...] = dyp.astype(BF16)
        lam_scan(lambda s: dyp_sc[:, s * secw:(s + 1) * secw], csrt_ref, csit_ref, ar_ref, ai_ref,
                 lr_sc, li_sc, nr_sc, ni_sc, lambda r0, prev_rows: None)

        @pl.when(b == nblk - 1)
        def _():
            fr = _shift_rows(nr_sc[...], False)
            fi = _shift_rows(ni_sc[...], False)
            alr, ali = alr_ref[...], ali_ref[...]
            cr, ci = fr, fi
            for _ in range(N_SEG - 2):
                sr = _shift_rows(cr, False)
                si = _shift_rows(ci, False)
                cr = fr + alr * sr + ali * si
                ci = fi + alr * si - ali * sr
            cr_ref[...] = cr
            ci_ref[...] = ci

    rev = lambda b: (nblk - 1 - b, 0)
    ublk = pl.BlockSpec((rows, sw), rev)
    xblk = pl.BlockSpec((rows, ns), rev)
    s8 = pl.BlockSpec((N_SEG, ns), lambda b: (0, 0))
    vec = pl.BlockSpec((1, sw), lambda b: (0, 0))
    gws = pl.BlockSpec((sw, sw), lambda b: (0, 0))
    btspec = pl.BlockSpec((N_SEC, secn, secw), lambda b: (0, 0, 0))
    ctspec = pl.BlockSpec((N_SEC, secw, secn), lambda b: (0, 0, 0))
    s8shape = jax.ShapeDtypeStruct((N_SEG, ns), F32)
    lcr, lci = pl.pallas_call(
        carry_body, name="s5_bwd_carry", grid=(nblk,),
        in_specs=[ublk, ublk, gws, vec, vec, ctspec, ctspec, s8, s8, s8, s8],
        out_specs=(s8, s8), out_shape=(s8shape, s8shape),
        scratch_shapes=[pltpu.VMEM((rows, ns), F32), pltpu.VMEM((rows, ns), F32),
                        pltpu.VMEM((N_SEG, ns), F32), pltpu.VMEM((N_SEG, ns), F32),
                        pltpu.VMEM((rows, sw), BF16)],
        compiler_params=_params(("arbitrary",)),
    )(dout, yp, gluw, glub, nw, csrt, csit, a8r, a8i, al8r, al8i)

    def main_body(dout_ref, yp_ref, u_ref, xr_ref, xi_ref, xtr_ref, xti_ref, c0r_ref, c0i_ref, lcr_ref, lci_ref,
                  gw_ref, gb_ref, nw_ref, d_ref, bsrt_ref, bsit_ref, csrt_ref, csit_ref, ar_ref, ai_ref,
                  du_ref, dnw_ref, dgw_ref, dgb_ref, dd_ref, dcr_ref, dci_ref, dbr_ref, dbi_ref, dar_ref, dai_ref,
                  lr_sc, li_sc, nr_sc, ni_sc, dyp_sc):
        b = pl.program_id(0)

        @pl.when(b == 0)
        def _():
            nr_sc[...] = lcr_ref[...]
            ni_sc[...] = lci_ref[...]
            for ref in (dnw_ref, dgw_ref, dgb_ref, dd_ref, dcr_ref, dci_ref, dbr_ref, dbi_ref, dar_ref, dai_ref):
                ref[...] = jnp.zeros_like(ref)

        dyp, dnw_rows, y1b, dzb, dz = rowwise_bwd(dout_ref, yp_ref, gw_ref, gb_ref, nw_ref)
        dnw_ref[...] += jnp.sum(dnw_rows, axis=0, keepdims=True)
        dgw_ref[...] += _dot_tn(y1b, dzb)
        dgb_ref[...] += jnp.sum(dz, axis=0, keepdims=True)
        uv = u_ref[...]
        dd_ref[...] += jnp.sum(dyp * uv, axis=0, keepdims=True)
        dyp_sc[...] = dyp.astype(BF16)
        for s in range(N_SEC):
            db = dyp_sc[:, s * secw:(s + 1) * secw]
            xs = pl.ds(s * secn, secn)
            dcr_ref[s] += _dot_tn(xr_ref[:, xs].astype(BF16), db)
            dci_ref[s] += _dot_tn(xi_ref[:, xs].astype(BF16), db)

        first = b == nblk - 1

        def acc_da(r0, prev_rows):
            for cc in range(ns // SCAN_CW):
                cs = pl.ds(cc * SCAN_CW, SCAN_CW)
                lr = lr_sc[pl.ds(r0, 8), cs]
                li = li_sc[pl.ds(r0, 8), cs]
                if prev_rows is None:
                    xpr = jnp.where(first, c0r_ref[:, cs], xtr_ref[:, cs])
                    xpi = jnp.where(first, c0i_ref[:, cs], xti_ref[:, cs])
                else:
                    xpr = xr_ref[prev_rows, cs]
                    xpi = xi_ref[prev_rows, cs]
                dar_ref[:, cs] += lr * xpr + li * xpi
                dai_ref[:, cs] += li * xpr - lr * xpi

        lam_scan(lambda s: dyp_sc[:, s * secw:(s + 1) * secw], csrt_ref, csit_ref, ar_ref, ai_ref,
                 lr_sc, li_sc, nr_sc, ni_sc, acc_da)

        for s in range(N_SEC):
            xs = pl.ds(s * secn, secn)
            us = pl.ds(s * secw, secw)
            lrb = lr_sc[:, xs].astype(BF16)
            lib = li_sc[:, xs].astype(BF16)
            du = _dot(lrb, bsrt_ref[s]) + _dot(lib, bsit_ref[s]) + d_ref[:, us] * dyp_sc[:, us].astype(F32)
            du_ref[:, us] = du.astype(BF16)
            ub = u_ref[:, us].astype(BF16)
            dbr_ref[s] += _dot_tn(ub, lrb)
            dbi_ref[s] += _dot_tn(ub, lib)

    tail = pl.BlockSpec((N_SEG, ns), lambda b: (jnp.maximum((nblk - 1 - b) * jb - 1, 0), 0))
    acc_c = pl.BlockSpec((N_SEC, secn, secw), lambda b: (0, 0, 0))
    acc_b = pl.BlockSpec((N_SEC, secw, secn), lambda b: (0, 0, 0))
    outs = pl.pallas_call(
        main_body, name="s5_bwd", grid=(nblk,),
        in_specs=[ublk, ublk, ublk, xblk, xblk, tail, tail, s8, s8, s8, s8,
                  gws, vec, vec, vec, btspec, btspec, ctspec, ctspec, s8, s8],
        out_specs=(ublk, vec, gws, vec, vec, acc_c, acc_c, acc_b, acc_b, s8, s8),
        out_shape=(jax.ShapeDtypeStruct((lp, sw), BF16),
                   jax.ShapeDtypeStruct((1, sw), F32),
                   jax.ShapeDtypeStruct((sw, sw), F32),
                   jax.ShapeDtypeStruct((1, sw), F32),
                   jax.ShapeDtypeStruct((1, sw), F32),
                   jax.ShapeDtypeStruct((N_SEC, secn, secw), F32),
                   jax.ShapeDtypeStruct((N_SEC, secn, secw), F32),
                   jax.ShapeDtypeStruct((N_SEC, secw, secn), F32),
                   jax.ShapeDtypeStruct((N_SEC, secw, secn), F32),
                   s8shape, s8shape),
        scratch_shapes=[pltpu.VMEM((rows, ns), F32), pltpu.VMEM((rows, ns), F32),
                        pltpu.VMEM((N_SEG, ns), F32), pltpu.VMEM((N_SEG, ns), F32),
                        pltpu.VMEM((rows, sw), BF16)],
        compiler_params=_params(("arbitrary",)),
    )(dout, yp, u, xr, xi, xr, xi, c0r, c0i, lcr, lci, gluw, glub, nw, d, bsrt, bsit, csrt, csit, a8r, a8i)
    return outs


def _outproj_fwd(h, ret, ssm, wo):
    lp, d = h.shape
    nck, rs, _ = wo.shape
    rw = ret.shape[1]
    tm = _tile(lp, 640)
    per = rw // rs

    def body(h_ref, ret_ref, ssm_ref, w_ref, o_ref):
        acc = h_ref[...]
        for c in range(nck):
            src = ret_ref if c < per else ssm_ref
            lo = (c % per) * rs
            acc = acc + _dot(src[:, lo:lo + rs], w_ref[c])
        o_ref[...] = acc

    row = lambda w: pl.BlockSpec((tm, w), lambda i: (i, 0))
    return pl.pallas_call(
        body, name="outproj_fwd", grid=(lp // tm,),
        in_specs=[row(d), row(rw), row(ssm.shape[1]), pl.BlockSpec((nck, rs, d), lambda i: (0, 0, 0))],
        out_specs=row(d), out_shape=jax.ShapeDtypeStruct((lp, d), F32),
        compiler_params=_params(("arbitrary",)),
    )(h, ret, ssm, wo)


def _outproj_bwd(dh, ret, ssm, wo):
    lp, d = dh.shape
    nck, rs, _ = wo.shape
    rw = ret.shape[1]
    sw = ssm.shape[1]
    tm = _tile(lp, 640)
    per = rw // rs
    last = lp // tm - 1

    def body(dh_ref, ret_ref, ssm_ref, w_ref, dret_ref, dssm_ref, dw_ref, acc_sc):
        i = pl.program_id(0)

        @pl.when(i == 0)
        def _():
            acc_sc[...] = jnp.zeros_like(acc_sc)

        dhb = dh_ref[...].astype(BF16)
        for c in range(nck):
            src, dst = (ret_ref, dret_ref) if c < per else (ssm_ref, dssm_ref)
            lo = (c % per) * rs
            dst[:, lo:lo + rs] = _dot_nt(dhb, w_ref[c])
            acc_sc[c] += _dot_tn(src[:, lo:lo + rs], dhb)

        @pl.when(i == last)
        def _():
            dw_ref[...] = acc_sc[...].astype(BF16)

    row = lambda w: pl.BlockSpec((tm, w), lambda i: (i, 0))
    wsp = pl.BlockSpec((nck, rs, d), lambda i: (0, 0, 0))
    return pl.pallas_call(
        body, name="outproj_bwd", grid=(lp // tm,),
        in_specs=[row(d), row(rw), row(sw), wsp],
        out_specs=(row(rw), row(sw), wsp),
        out_shape=(jax.ShapeDtypeStruct((lp, rw), F32), jax.ShapeDtypeStruct((lp, sw), F32),
                   jax.ShapeDtypeStruct((nck, rs, d), BF16)),
        scratch_shapes=[pltpu.VMEM((nck, rs, d), F32)],
        compiler_params=_params(("arbitrary",)),
    )(dh, ret, ssm, wo)


def _loss_head(h, fw, target):
    lp, d = h.shape
    nblk = lp // CHUNK

    def body(h_ref, w_ref, t_ref, loss_ref, dh_ref, dw_ref):
        i = pl.program_id(0)

        @pl.when(i == 0)
        def _():
            loss_ref[...] = jnp.zeros_like(loss_ref)
            dw_ref[...] = jnp.zeros_like(dw_ref)
            dh_ref[...] = jnp.zeros_like(dh_ref)

        @pl.when(i > 0)
        def _():
            xh, r = _rms_stats(h_ref[...])
            w = w_ref[...]
            err = xh * w - t_ref[...]
            loss_ref[...] += 0.5 * jnp.sum(err * err) / d
            dout = err * (1.0 / d)
            dw_ref[...] += jnp.sum(dout * xh, axis=0, keepdims=True)
            dh_ref[...] = _rms_bwd(dout, xh, r, w)

    return pl.pallas_call(
        body, name="loss_head", grid=(nblk,),
        in_specs=[pl.BlockSpec((CHUNK, d), lambda i: (i, 0)), pl.BlockSpec((1, d), lambda i: (0, 0)),
                  pl.BlockSpec((CHUNK, d), lambda i: (jnp.maximum(i - 1, 0), 0))],
        out_specs=(pl.BlockSpec((8, LANE), lambda i: (0, 0)), pl.BlockSpec((CHUNK, d), lambda i: (i, 0)),
                   pl.BlockSpec((1, d), lambda i: (0, 0))),
        out_shape=(jax.ShapeDtypeStruct((8, LANE), F32), jax.ShapeDtypeStruct((lp, d), F32),
                   jax.ShapeDtypeStruct((1, d), F32)),
        compiler_params=_params(("arbitrary",)),
    )(h, fw, target)


def _pack(arrs):
    flat = jnp.concatenate([a.reshape(-1).astype(F32) for a in arrs])
    n = flat.shape[0]
    rows = -(-n // (8 * LANE)) * 8
    return jnp.pad(flat, (0, rows * LANE - n)).reshape(rows, LANE)


def _unpack(packed, shapes):
    flat = packed.reshape(-1)
    out, off = [], 0
    for s in shapes:
        n = math.prod(s)
        out.append(flat[off:off + n].reshape(s))
        off += n
    return out


def _to_segments(a, seg_len):
    return a.reshape(N_SEG, seg_len, a.shape[1]).transpose(1, 0, 2).reshape(a.shape)


def _from_segments(a, seg_len):
    return a.reshape(seg_len, N_SEG, a.shape[1]).transpose(1, 0, 2).reshape(a.shape)


WEIGHT_NAMES = ['meta_tokens', 'ffn1_norm_w', 'ffn1_w_gate', 'ffn1_w_up', 'ffn1_w_down', 'mix_norm_w', 'w_in',
                'ret_norm_w', 'ssm_lambda_re', 'ssm_lambda_im', 'ssm_log_dt', 'ssm_b_re', 'ssm_b_im', 'ssm_c_re',
                'ssm_c_im', 'ssm_d', 'ssm_glu_w', 'ssm_glu_b', 'ssm_norm_w', 'w_out', 'ffn2_norm_w', 'ffn2_w_gate',
                'ffn2_w_up', 'ffn2_w_down', 'final_norm_w']
BIG = ['ffn1_w_gate', 'ffn1_w_up', 'ffn1_w_down', 'w_in', 'ssm_glu_w', 'w_out', 'ffn2_w_gate', 'ffn2_w_up',
       'ffn2_w_down']
BIG_EARLY = ['ffn1_w_gate', 'ffn1_w_up', 'ffn1_w_down']
BIG_LATE = [n for n in BIG if n not in BIG_EARLY]
SMALL = [n for n in WEIGHT_NAMES if n not in BIG]


def kernel(x, meta_tokens, ffn1_norm_w, ffn1_w_gate, ffn1_w_up, ffn1_w_down, mix_norm_w, w_in, ret_norm_w, ssm_lambda_re, ssm_lambda_im, ssm_log_dt, ssm_b_re, ssm_b_im, ssm_c_re, ssm_c_im, ssm_d, ssm_glu_w, ssm_glu_b, ssm_norm_w, w_out, ffn2_norm_w, ffn2_w_gate, ffn2_w_up, ffn2_w_down, final_norm_w, loss_target, m_meta_tokens, m_ffn1_norm_w, m_ffn1_w_gate, m_ffn1_w_up, m_ffn1_w_down, m_mix_norm_w, m_w_in, m_ret_norm_w, m_ssm_lambda_re, m_ssm_lambda_im, m_ssm_log_dt, m_ssm_b_re, m_ssm_b_im, m_ssm_c_re, m_ssm_c_im, m_ssm_d, m_ssm_glu_w, m_ssm_glu_b, m_ssm_norm_w, m_w_out, m_ffn2_norm_w, m_ffn2_w_gate, m_ffn2_w_up, m_ffn2_w_down, m_final_norm_w, v_meta_tokens, v_ffn1_norm_w, v_ffn1_w_gate, v_ffn1_w_up, v_ffn1_w_down, v_mix_norm_w, v_w_in, v_ret_norm_w, v_ssm_lambda_re, v_ssm_lambda_im, v_ssm_log_dt, v_ssm_b_re, v_ssm_b_im, v_ssm_c_re, v_ssm_c_im, v_ssm_d, v_ssm_glu_w, v_ssm_glu_b, v_ssm_norm_w, v_w_out, v_ffn2_norm_w, v_ffn2_w_gate, v_ffn2_w_up, v_ffn2_w_down, v_final_norm_w):
    args = locals()
    w = {n: args[n] for n in WEIGHT_NAMES}
    m = {n: args["m_" + n] for n in WEIGHT_NAMES}
    v = {n: args["v_" + n] for n in WEIGHT_NAMES}

    seq, d = x.shape[1], x.shape[2]
    lp = seq + CHUNK
    seg_len = lp // N_SEG
    rw = RET_HEADS * HEAD_DIM
    sw = ssm_d.shape[-1]
    groups = sw // SSM_GROUP
    ns = groups * SSM_STATE
    jb = _tile(seg_len, 40, 8)
    chip = 2 * lax.axis_index("x") + lax.axis_index("y")

    shards = {n: w[n][0].astype(BF16) for n in BIG}
    early = [shards[n] for n in BIG_EARLY] + [meta_tokens]
    gathered = _forward_sibling("gather_early_forward",
                                _exchange("gather_early", _allgather_chips_plan(early), early))
    gw = dict(zip(BIG_EARLY, gathered[:-1]))
    meta_full = jnp.transpose(gathered[-1], (1, 0, 2)).reshape(N_META, d)
    late = [shards[n] for n in BIG_LATE]

    pos = jnp.arange(lp, dtype=F32) - float(CHUNK - N_META)
    freqs = 1.0 / (ROPE_BASE ** (jnp.arange(0, HEAD_DIM, 2, dtype=F32) / HEAD_DIM))
    ang = pos[:, None] * freqs[None, :]
    cosf = jnp.concatenate([jnp.cos(ang), jnp.cos(ang)], axis=1)
    sinf = jnp.concatenate([-jnp.sin(ang), jnp.sin(ang)], axis=1)
    tables = _retention_tables()

    lam_re, lam_im, log_dt = ssm_lambda_re[0], ssm_lambda_im[0], ssm_log_dt[0]
    b_re, b_im, c_re, c_im = ssm_b_re[0], ssm_b_im[0], ssm_c_re[0], ssm_c_im[0]
    (ar, ai, bbr, bbi), prep_vjp = jax.vjp(_s5_prepare, lam_re, lam_im, log_dt, b_re, b_im)
    dt = jnp.exp(log_dt)[:, None]
    el = jnp.exp(seg_len * lam_re * dt)
    alr = el * jnp.cos(seg_len * lam_im * dt)
    ali = el * jnp.sin(seg_len * lam_im * dt)
    bc8 = lambda t: jnp.broadcast_to(t.reshape(1, ns), (N_SEG, ns))
    a8r, a8i, al8r, al8i = bc8(ar), bc8(ai), bc8(alr), bc8(ali)
    bsr = _blockdiag_in(jnp.transpose(bbr, (0, 2, 1)))
    bsi = _blockdiag_in(jnp.transpose(bbi, (0, 2, 1)))
    csrt = _blockdiag_in(c_re)
    csit = _blockdiag_in(-c_im)
    tr = lambda t: jnp.transpose(t, (0, 2, 1))
    bsr_b, bsi_b = bsr.astype(BF16), bsi.astype(BF16)
    csr_b, csi_b = tr(csrt).astype(BF16), tr(csit).astype(BF16)
    bsrt_b, bsit_b = tr(bsr).astype(BF16), tr(bsi).astype(BF16)
    csrt_b, csit_b = csrt.astype(BF16), csit.astype(BF16)

    h0 = jnp.concatenate([jnp.zeros((CHUNK - N_META, d), F32), meta_full, x[0]], axis=0)
    (h1, g1, u1), late_half = _ffn_fwd("ffn1_fwd", h0, ffn1_norm_w, gw['ffn1_w_gate'], gw['ffn1_w_up'],
                                       gw['ffn1_w_down'], _allgather_chips_plan(late), late)
    gw.update(zip(BIG_LATE, _forward_sibling("gather_late_forward", late_half)))
    glu_full = gw['ssm_glu_w'].reshape(sw, sw)
    n2, q, k, vv, gate, u = _inproj_fwd(h1, mix_norm_w, gw['w_in'], cosf, sinf, rw)
    o, ret, sprev = _ret_fwd(q, k, vv, gate, ret_norm_w, tables)
    u_seg = _to_segments(u, seg_len)
    xr, xi, c0r, c0i, yp, ssm_seg = _s5_fwd(u_seg, bsr_b, bsi_b, csr_b, csi_b, a8r, a8i, al8r, al8i,
                                            ssm_d, glu_full, ssm_glu_b, ssm_norm_w, jb)
    ssm = _from_segments(ssm_seg, seg_len)
    h2 = _outproj_fwd(h1, ret, ssm, gw['w_out'])
    (h3, g2, u2), _ = _ffn_fwd("ffn2_fwd", h2, ffn2_norm_w, gw['ffn2_w_gate'], gw['ffn2_w_up'], gw['ffn2_w_down'])
    loss_part, dh3, d_final = _loss_head(h3, final_norm_w.reshape(1, d), loss_target[0])

    (dh2, d_ffn2_norm, nb, daccb, ab, dgb, dub), _ = _ffn_bwd_act(
        "ffn2_bwd_act", dh3, h2, ffn2_norm_w, g2, u2, gw['ffn2_w_gate'], gw['ffn2_w_up'], gw['ffn2_w_down'])
    (dwg2, dwu2, dwd2), _ = _ffn_bwd_w("ffn2_bwd_w", nb, daccb, ab, dgb, dub)
    dret, dssm, dwo = _outproj_bwd(dh2, ret, ssm, gw['w_out'])
    (du_seg, d_ssm_norm, d_glu_w, d_glu_b, d_ssm_d, dcr_s, dci_s, dbr_s, dbi_s, dar8, dai8) = _s5_bwd(
        _to_segments(dssm, seg_len), u_seg, yp, xr, xi, c0r, c0i, bsrt_b, bsit_b, csrt_b, csit_b,
        a8r, a8i, al8r, al8i, ssm_d, glu_full, ssm_glu_b, ssm_norm_w, jb)
    du = _from_segments(du_seg, seg_len)
    dq, dk, dv, dgate, d_ret_norm = _ret_bwd(dret, q, k, vv, gate, o, sprev, ret_norm_w, tables, cosf, sinf)
    dh1, d_mix_norm, dwin = _inproj_bwd(dh2, h1, mix_norm_w, n2, gw['w_in'], dq, dk, dv, dgate, du)
    late_parts = {
        'w_in': dwin, 'ssm_glu_w': d_glu_w.reshape(N_CHIP, sw // N_CHIP, sw).astype(BF16), 'w_out': dwo,
        'ffn2_w_gate': dwg2, 'ffn2_w_up': dwu2, 'ffn2_w_down': dwd2,
    }
    late_list = [late_parts[n] for n in BIG_LATE]
    (dh0, d_ffn1_norm, nb, daccb, ab, dgb, dub), late_recv = _ffn_bwd_act(
        "ffn1_bwd_act", dh1, h0, ffn1_norm_w, g1, u1, gw['ffn1_w_gate'], gw['ffn1_w_up'], gw['ffn1_w_down'],
        _alltoall_chips_plan(late_list), late_list)
    grad_x = dh0[CHUNK:][None]
    d_meta = dh0[CHUNK - N_META:CHUNK]

    d_c_re = jnp.transpose(_blockdiag_out(tr(dcr_s), groups, SSM_GROUP, SSM_STATE), (0, 1, 2))
    d_c_im = -_blockdiag_out(tr(dci_s), groups, SSM_GROUP, SSM_STATE)
    d_bbr = jnp.transpose(_blockdiag_out(dbr_s, groups, SSM_GROUP, SSM_STATE), (0, 2, 1))
    d_bbi = jnp.transpose(_blockdiag_out(dbi_s, groups, SSM_GROUP, SSM_STATE), (0, 2, 1))
    d_ar = jnp.sum(dar8, axis=0).reshape(groups, SSM_STATE)
    d_ai = jnp.sum(dai8, axis=0).reshape(groups, SSM_STATE)
    small_parts = [loss_part[0:1, :], d_meta, d_ffn1_norm, d_mix_norm, d_ret_norm, d_ar, d_ai, d_bbr, d_bbi,
                   d_c_re, d_c_im, d_ssm_d, d_glu_b, d_ssm_norm, d_ffn2_norm, d_final]
    small_shapes = [a.shape for a in small_parts]
    packed = _pack(small_parts)
    early_list, (all_parts,) = _ffn_bwd_w("ffn1_bwd_w", nb, daccb, ab, dgb, dub, _allgather_all_plan([packed]), [packed])
    early_list = list(early_list)
    early_recv = _exchange("alltoall_early", _alltoall_chips_plan(early_list), early_list)
    received = dict(zip(BIG_LATE + BIG_EARLY, late_recv + early_recv))
    chip_sums = [_sum_slots("sum_chips_" + n, received[n], BF16) for n in BIG]
    sib_sums = _swap_sibling("swap_sibling", chip_sums)
    (loss_row, g_meta_full, g_ffn1_norm, g_mix_norm, g_ret_norm, g_ar, g_ai, g_bbr, g_bbi, g_c_re, g_c_im,
     g_ssm_d, g_glu_b, g_ssm_norm, g_ffn2_norm, g_final) = _unpack(_sum_slots("sum_small", all_parts, F32), small_shapes)
    g_lam_re, g_lam_im, g_log_dt, g_b_re, g_b_im = prep_vjp((g_ar, g_ai, g_bbr, g_bbi))
    loss = loss_row[0, 0]
    g_meta = lax.dynamic_slice(g_meta_full, (0, chip * (d // N_CHIP)), (N_META, d // N_CHIP))
    small_grads = {
        'meta_tokens': g_meta, 'ffn1_norm_w': g_ffn1_norm, 'mix_norm_w': g_mix_norm, 'ret_norm_w': g_ret_norm,
        'ssm_lambda_re': g_lam_re[None], 'ssm_lambda_im': g_lam_im[None], 'ssm_log_dt': g_log_dt[None],
        'ssm_b_re': g_b_re[None], 'ssm_b_im': g_b_im[None], 'ssm_c_re': g_c_re[None], 'ssm_c_im': g_c_im[None],
        'ssm_d': g_ssm_d, 'ssm_glu_b': g_glu_b, 'ssm_norm_w': g_ssm_norm, 'ffn2_norm_w': g_ffn2_norm,
        'final_norm_w': g_final.reshape(d),
    }

    grads, deltas, new_m, new_v = {}, {}, {}, {}
    for n, mine, sib in zip(BIG, chip_sums, sib_sums):
        grads[n], deltas[n], new_m[n], new_v[n] = _adam("adam_" + n, w[n], m[n], v[n], [mine, sib])
    sm_shapes = [w[n].shape for n in SMALL]
    sm_out = _adam("adam_small", _pack([w[n] for n in SMALL]), _pack([m[n] for n in SMALL]),
                   _pack([v[n] for n in SMALL]), [_pack([small_grads[n].reshape(w[n].shape) for n in SMALL])])
    for dst, packed in zip((grads, deltas, new_m, new_v), sm_out):
        for n, t in zip(SMALL, _unpack(packed, sm_shapes)):
            dst[n] = t

    return (loss, grad_x, *[grads[n] for n in WEIGHT_NAMES], *[deltas[n] for n in WEIGHT_NAMES],
            *[new_m[n] for n in WEIGHT_NAMES], *[new_v[n] for n in WEIGHT_NAMES])
```

```python
import functools
import math

import jax
import jax.numpy as jnp
from jax import lax
from jax.experimental import pallas as pl
from jax.experimental.pallas import tpu as pltpu

N_META = 16
RET_HEADS = 4
HEAD_DIM = 128
SSM_GROUP = 16
SSM_STATE = 64
CHUNK = 128
ROPE_BASE = 10000.0
EPS = 1e-6
FFN_RES = 0.5
N_SEG = 8
N_SEC = 4
N_CHIP = 4
LANE = 128

ADAM_LR = 0.001
ADAM_B1 = 0.9
ADAM_B2 = 0.999
ADAM_EPS = 1e-08
ADAM_WD = 0.01
ADAM_STEP = 10

VMEM_LIMIT = 56 * 1024 * 1024

F32 = jnp.float32
BF16 = jnp.bfloat16
MESH = pl.DeviceIdType.MESH


def _dot(a, b):
    return jnp.dot(a, b, preferred_element_type=F32)


def _dot_nt(a, b):
    return lax.dot_general(a, b, (((1,), (1,)), ((), ())), preferred_element_type=F32)


def _dot_tn(a, b):
    return lax.dot_general(a, b, (((0,), (0,)), ((), ())), preferred_element_type=F32)


def _tile(n, target, mult=64):
    best = None
    t = mult
    while t <= min(n, target):
        if n % t == 0:
            best = t
        t += mult
    assert best is not None, (n, target)
    return best


def _params(sem, vmem=VMEM_LIMIT):
    return pltpu.CompilerParams(dimension_semantics=sem, vmem_limit_bytes=vmem)


def _rms_stats(xf):
    r = lax.rsqrt(jnp.mean(xf * xf, axis=-1, keepdims=True) + EPS)
    return xf * r, r


def _rms_bwd(dy, xh, r, w):
    dxh = dy * w
    return r * (dxh - xh * jnp.mean(dxh * xh, axis=-1, keepdims=True))


def _sigmoid(x):
    return 1.0 / (1.0 + jnp.exp(-x))


GELU_K0 = math.sqrt(2.0 / math.pi)
GELU_K1 = 0.044715


CHIP_MASKS = [(1, 0, 0), (0, 1, 0), (1, 1, 0)]
ALL_MASKS = [(0, 0, 1), (0, 1, 0), (0, 1, 1), (1, 0, 0), (1, 0, 1), (1, 1, 0), (1, 1, 1)]
SIB_MASKS = [(0, 0, 1)]
ANY_SPEC = pl.BlockSpec(memory_space=pl.ANY)


class _Plan:
    def __init__(self, arrays, masks, n_slots, src_slotted, dst_slotted, local_copy, half=False, forward=False):
        self.shapes = [(a.shape, a.dtype) for a in arrays]
        self.n = len(arrays)
        self.masks = masks
        self.n_slots = n_slots
        self.src_slotted, self.dst_slotted, self.local_copy = src_slotted, dst_slotted, local_copy
        self.half, self.forward = half, forward
        self.n_cp = self.n * len(masks) * (len(CHIP_MASKS) if forward else 1)

    def out_shape(self):
        out = []
        for shp, dt in self.shapes:
            if self.dst_slotted and not self.src_slotted:
                shp = (self.n_slots,) + shp
            elif self.src_slotted and not self.dst_slotted:
                shp = shp[1:]
            out.append(jax.ShapeDtypeStruct(shp, dt))
        return tuple(out)

    def scratch(self):
        return [pltpu.SemaphoreType.DMA((self.n_cp,)), pltpu.SemaphoreType.DMA((self.n_cp,)),
                pltpu.SemaphoreType.DMA((self.n,))]

    def _slot(self, px, py, pc):
        if self.n_slots == 8:
            return 4 * px + 2 * py + pc
        if self.n_slots == 4:
            return 2 * px + py
        return pc

    def copies(self, ins, outs, sems):
        send_sems, recv_sems, loc_sems = sems
        x, y, c = lax.axis_index("x"), lax.axis_index("y"), lax.axis_index("c")
        me = self._slot(x, y, c)
        n_m = len(self.masks)
        cps = []
        for a in range(self.n):
            if self.forward:
                rows = self.shapes[a][0][-2] // 2
                mine = pl.ds(pl.multiple_of(c * rows, 8), rows)
                for j, (mx, my, _) in enumerate(CHIP_MASKS):
                    blk = outs[a].at[2 * (1 - x if mx else x) + (1 - y if my else y), mine]
                    k = a * len(CHIP_MASKS) + j
                    cps.append(pltpu.make_async_remote_copy(
                        src_ref=blk, dst_ref=blk, send_sem=send_sems.at[k], recv_sem=recv_sems.at[k],
                        device_id=(x, y, 1 - c), device_id_type=MESH))
                continue
            if self.local_copy:
                src = ins[a].at[me] if self.src_slotted else ins[a]
                cps.append(pltpu.make_async_copy(src, outs[a].at[me], loc_sems.at[a]))
            for mi, (mx, my, mc) in enumerate(self.masks):
                px = 1 - x if mx else x
                py = 1 - y if my else y
                pc = 1 - c if mc else c
                src = ins[a].at[self._slot(px, py, pc)] if self.src_slotted else ins[a]
                dst = outs[a].at[me] if self.dst_slotted else outs[a]
                if self.half:
                    rows = src.shape[-2] // 2
                    mine = pl.ds(pl.multiple_of(c * rows, 8), rows)
                    src, dst = src.at[mine], dst.at[mine]
                k = a * n_m + mi
                cps.append(pltpu.make_async_remote_copy(
                    src_ref=src, dst_ref=dst, send_sem=send_sems.at[k], recv_sem=recv_sems.at[k],
                    device_id=(px, py, pc), device_id_type=MESH))
        return cps


def _exchange(name, plan, arrays):
    n = plan.n

    def body(*refs):
        cps = plan.copies(refs[:n], refs[n:2 * n], refs[2 * n:])
        for cp in cps:
            cp.start()
        for cp in cps:
            cp.wait()

    outs = pl.pallas_call(
        body, name=name, out_shape=plan.out_shape(),
        in_specs=[ANY_SPEC] * n, out_specs=tuple([ANY_SPEC] * n), scratch_shapes=plan.scratch(),
        input_output_aliases={i: i for i in range(n)} if plan.forward else {},
    )(*arrays)
    return list(outs)


def _pcall(body, *, name, grid, in_specs, out_specs, out_shape, scratch_shapes, args, plan=None, plan_args=()):
    sem = ("arbitrary",) * len(grid)
    if plan is None:
        return pl.pallas_call(body, name=name, grid=grid, in_specs=in_specs, out_specs=out_specs,
                              out_shape=out_shape, scratch_shapes=scratch_shapes,
                              compiler_params=_params(sem))(*args), []
    n_in, n_out, n_scr, n_p = len(in_specs), len(out_specs), len(scratch_shapes), plan.n

    def wrapped(*refs):
        ins = refs[:n_in]
        p_ins = refs[n_in:n_in + n_p]
        o0 = n_in + n_p
        outs = refs[o0:o0 + n_out]
        p_outs = refs[o0 + n_out:o0 + n_out + n_p]
        s0 = o0 + n_out + n_p
        scr = refs[s0:s0 + n_scr]
        sems = refs[s0 + n_scr:]
        ids = [pl.program_id(i) for i in range(len(grid))]
        first = functools.reduce(jnp.logical_and, [i == 0 for i in ids])
        last = functools.reduce(jnp.logical_and, [i == g - 1 for i, g in zip(ids, grid)])

        @pl.when(first)
        def _():
            for cp in plan.copies(p_ins, p_outs, sems):
                cp.start()

        body(*ins, *outs, *scr)

        @pl.when(last)
        def _():
            for cp in plan.copies(p_ins, p_outs, sems):
                cp.wait()

    res = pl.pallas_call(
        wrapped, name=name, grid=grid,
        in_specs=list(in_specs) + [ANY_SPEC] * n_p,
        out_specs=tuple(out_specs) + (ANY_SPEC,) * n_p,
        out_shape=tuple(out_shape) + plan.out_shape(),
        scratch_shapes=list(scratch_shapes) + plan.scratch(),
        compiler_params=_params(sem),
    )(*args, *plan_args)
    return res[:n_out], list(res[n_out:])


def _allgather_chips_plan(arrays):
    return _Plan(arrays, CHIP_MASKS, 4, False, True, True, half=True)


def _forward_sibling(name, gathered):
    return _exchange(name, _Plan(gathered, SIB_MASKS, 4, True, True, False, forward=True), gathered)


def _alltoall_chips_plan(arrays):
    return _Plan(arrays, CHIP_MASKS, 4, True, True, True)


def _swap_sibling(name, arrays):
    return _exchange(name, _Plan(arrays, SIB_MASKS, 2, False, False, False), arrays)


def _allgather_all_plan(arrays):
    return _Plan(arrays, ALL_MASKS, 8, False, True, True)


def _sum_slots(name, a, out_dtype):
    s, r, c = a.shape
    tr = _tile(r, 512, 8)

    def body(a_ref, o_ref):
        acc = a_ref[0].astype(F32)
        for i in range(1, s):
            acc = acc + a_ref[i].astype(F32)
        o_ref[...] = acc.astype(out_dtype)

    return pl.pallas_call(
        body, name=name, grid=(r // tr,),
        in_specs=[pl.BlockSpec((s, tr, c), lambda i: (0, i, 0))],
        out_specs=pl.BlockSpec((tr, c), lambda i: (i, 0)),
        out_shape=jax.ShapeDtypeStruct((r, c), out_dtype),
        compiler_params=_params(("arbitrary",)),
    )(a)


def _adam_math(w, g, m, v):
    m_new = ADAM_B1 * m + (1.0 - ADAM_B1) * g
    v_new = ADAM_B2 * v + (1.0 - ADAM_B2) * (g * g)
    m_hat = m_new / (1.0 - ADAM_B1 ** ADAM_STEP)
    v_hat = v_new / (1.0 - ADAM_B2 ** ADAM_STEP)
    delta = -ADAM_LR * (m_hat / (jnp.sqrt(v_hat) + ADAM_EPS) + ADAM_WD * w)
    return delta, m_new, v_new


def _adam(name, w, m, v, g_parts):
    r, c = w.shape[-2:]
    tr = _tile(r, 256, 8)
    n_g = len(g_parts)
    lead = w.ndim == 3
    at = (lambda ref: ref.at[0]) if lead else (lambda ref: ref)

    def body(*refs):
        w_ref, m_ref, v_ref = [at(t) for t in refs[:3]]
        g_refs = refs[3:3 + n_g]
        g_out, d_out, m_out, v_out = [at(t) for t in refs[3 + n_g:]]
        g = g_refs[0][...].astype(F32)
        for gr in g_refs[1:]:
            g = g + gr[...].astype(F32)
        delta, m_new, v_new = _adam_math(w_ref[...], g, m_ref[...], v_ref[...])
        g_out[...] = g
        d_out[...] = delta
        m_out[...] = m_new
        v_out[...] = v_new

    spec = pl.BlockSpec((tr, c), lambda i: (i, 0))
    wspec = pl.BlockSpec((1, tr, c), lambda i: (0, i, 0)) if lead else spec
    shp = jax.ShapeDtypeStruct(w.shape, F32)
    return pl.pallas_call(
        body, name=name, grid=(r // tr,),
        in_specs=[wspec] * 3 + [spec] * n_g, out_specs=(wspec,) * 4, out_shape=(shp,) * 4,
        compiler_params=_params(("arbitrary",)),
    )(w, m, v, *g_parts)


def _ffn_fwd(name, h, nw, wg, wu, wd, plan=None, plan_args=()):
    lp, d = h.shape
    nck, f, _ = wg.shape
    tm = _tile(lp, 640)
    last = nck - 1

    def body(h_ref, nw_ref, wg_ref, wu_ref, wd_ref, ho_ref, g_ref, u_ref, n_sc, acc_sc):
        k = pl.program_id(1)

        @pl.when(k == 0)
        def _():
            xh, _ = _rms_stats(h_ref[...])
            n_sc[...] = (xh * nw_ref[...]).astype(BF16)
            acc_sc[...] = jnp.zeros_like(acc_sc)

        n = n_sc[...]
        g = _dot_nt(n, wg_ref[0])
        u = _dot_nt(n, wu_ref[0])
        g_ref[0] = g.astype(BF16)
        u_ref[0] = u.astype(BF16)
        a = (g * _sigmoid(g) * u).astype(BF16)
        acc_sc[...] += _dot(a, wd_ref[0])

        @pl.when(k == last)
        def _():
            ho_ref[...] = h_ref[...] + FFN_RES * acc_sc[...]

    return _pcall(
        body, name=name, grid=(lp // tm, nck), plan=plan, plan_args=plan_args, args=(h, nw, wg, wu, wd),
        in_specs=[pl.BlockSpec((tm, d), lambda i, k: (i, 0)),
                  pl.BlockSpec((1, d), lambda i, k: (0, 0)),
                  pl.BlockSpec((1, f, d), lambda i, k: (k, 0, 0)),
                  pl.BlockSpec((1, f, d), lambda i, k: (k, 0, 0)),
                  pl.BlockSpec((1, f, d), lambda i, k: (k, 0, 0))],
        out_specs=(pl.BlockSpec((tm, d), lambda i, k: (i, 0)),
                   pl.BlockSpec((1, tm, f), lambda i, k: (k, i, 0)),
                   pl.BlockSpec((1, tm, f), lambda i, k: (k, i, 0))),
        out_shape=(jax.ShapeDtypeStruct((lp, d), F32),
                   jax.ShapeDtypeStruct((nck, lp, f), BF16),
                   jax.ShapeDtypeStruct((nck, lp, f), BF16)),
        scratch_shapes=[pltpu.VMEM((tm, d), BF16), pltpu.VMEM((tm, d), F32)])


def _ffn_bwd_act(name, dh, h, nw, g, u, wg, wu, wd, plan=None, plan_args=()):
    lp, d = h.shape
    nck, f, _ = wg.shape
    tm = _tile(lp, 320)
    last = nck - 1

    def body(dh_ref, h_ref, nw_ref, g_ref, u_ref, wg_ref, wu_ref, wd_ref,
             dhi_ref, dnw_ref, n_ref, dacc_ref, a_ref, dg_ref, du_ref,
             xh_sc, r_sc, dn_sc):
        i = pl.program_id(0)
        k = pl.program_id(1)

        @pl.when(k == 0)
        def _():
            xh, r = _rms_stats(h_ref[...])
            xh_sc[...] = xh
            r_sc[...] = r
            n_ref[...] = (xh * nw_ref[...]).astype(BF16)
            dacc_ref[...] = (FFN_RES * dh_ref[...]).astype(BF16)
            dn_sc[...] = jnp.zeros_like(dn_sc)

        @pl.when(jnp.logical_and(i == 0, k == 0))
        def _():
            dnw_ref[...] = jnp.zeros_like(dnw_ref)

        gv = g_ref[0].astype(F32)
        uv = u_ref[0].astype(F32)
        sg = _sigmoid(gv)
        sil = gv * sg
        da = _dot_nt(dacc_ref[...], wd_ref[0])
        dgk = (da * uv * (sg * (1.0 + gv * (1.0 - sg)))).astype(BF16)
        duk = (da * sil).astype(BF16)
        a_ref[0] = (sil * uv).astype(BF16)
        dg_ref[0] = dgk
        du_ref[0] = duk
        dn_sc[...] += _dot(dgk, wg_ref[0]) + _dot(duk, wu_ref[0])

        @pl.when(k == last)
        def _():
            dn = dn_sc[...]
            xh = xh_sc[...]
            dhi_ref[...] = dh_ref[...] + _rms_bwd(dn, xh, r_sc[...], nw_ref[...])
            dnw_ref[...] += jnp.sum(dn * xh, axis=0, keepdims=True)

    row = pl.BlockSpec((tm, d), lambda i, k: (i, 0))
    vec = pl.BlockSpec((1, d), lambda i, k: (0, 0))
    hid = pl.BlockSpec((1, tm, f), lambda i, k: (k, i, 0))
    w_fd = pl.BlockSpec((1, f, d), lambda i, k: (k, 0, 0))
    return _pcall(
        body, name=name, grid=(lp // tm, nck), plan=plan, plan_args=plan_args, args=(dh, h, nw, g, u, wg, wu, wd),
        in_specs=[row, row, vec, hid, hid, w_fd, w_fd, w_fd],
        out_specs=(row, vec, row, row, hid, hid, hid),
        out_shape=(jax.ShapeDtypeStruct((lp, d), F32),
                   jax.ShapeDtypeStruct((1, d), F32),
                   jax.ShapeDtypeStruct((lp, d), BF16),
                   jax.ShapeDtypeStruct((lp, d), BF16),
                   jax.ShapeDtypeStruct((nck, lp, f), BF16),
                   jax.ShapeDtypeStruct((nck, lp, f), BF16),
                   jax.ShapeDtypeStruct((nck, lp, f), BF16)),
        scratch_shapes=[pltpu.VMEM((tm, d), F32), pltpu.VMEM((tm, 1), F32), pltpu.VMEM((tm, d), F32)])


def _ffn_bwd_w(name, n, dacc, a, dg, du):
    lp, d = n.shape
    nck, _, f = a.shape
    tm = _tile(lp, 640)
    last = lp // tm - 1

    def body(n_ref, dacc_ref, a_ref, dg_ref, du_ref, dwg_ref, dwu_ref, dwd_ref, ag_sc, au_sc, ad_sc):
        i = pl.program_id(1)

        @pl.when(i == 0)
        def _():
            ag_sc[...] = jnp.zeros_like(ag_sc)
            au_sc[...] = jnp.zeros_like(au_sc)
            ad_sc[...] = jnp.zeros_like(ad_sc)

        nv = n_ref[...]
        ag_sc[...] += _dot_tn(dg_ref[0], nv)
        au_sc[...] += _dot_tn(du_ref[0], nv)
        ad_sc[...] += _dot_tn(a_ref[0], dacc_ref[...])

        @pl.when(i == last)
        def _():
            dwg_ref[0] = ag_sc[...].astype(BF16)
            dwu_ref[0] = au_sc[...].astype(BF16)
            dwd_ref[0] = ad_sc[...].astype(BF16)

    row = pl.BlockSpec((tm, d), lambda k, i: (i, 0))
    hid = pl.BlockSpec((1, tm, f), lambda k, i: (k, i, 0))
    w_fd = pl.BlockSpec((1, f, d), lambda k, i: (k, 0, 0))
    wshape = jax.ShapeDtypeStruct((nck, f, d), BF16)
    return pl.pallas_call(
        body, name=name, grid=(nck, lp // tm),
        in_specs=[row, row, hid, hid, hid], out_specs=(w_fd, w_fd, w_fd), out_shape=(wshape,) * 3,
        scratch_shapes=[pltpu.VMEM((f, d), F32)] * 3,
        compiler_params=_params(("arbitrary", "arbitrary")),
    )(n, dacc, a, dg, du)


def _ffn_bwd_w_scatter(name, n, dacc, a, dg, du, chip, plan, plan_args):
    lp, d = n.shape
    nck, _, f = a.shape
    tm = _tile(lp, 640)
    last_i = lp // tm - 1
    n_w = 3
    n_p = plan.n

    def body(me_ref, n_ref, dacc_ref, a_ref, dg_ref, du_ref, *rest):
        p_ins = rest[:n_p]
        recv = rest[n_p:n_p + n_w]
        p_outs = rest[n_p + n_w:2 * n_p + n_w]
        acc = rest[2 * n_p + n_w:2 * n_p + 2 * n_w]
        stage, send_sems, recv_sems, loc_sems = rest[2 * n_p + 2 * n_w:2 * n_p + 2 * n_w + 4]
        p_sems = rest[2 * n_p + 2 * n_w + 4:]
        p = pl.program_id(0)
        i = pl.program_id(1)
        me = me_ref[0]
        c = lax.axis_index("c")

        def send(w, pos):
            kk = lax.rem(me + 1 + pos, nck)
            diff = jnp.bitwise_xor(kk, me)
            m = jnp.where(diff == 2, 0, jnp.where(diff == 1, 1, 2))
            return pltpu.make_async_remote_copy(
                src_ref=stage.at[lax.rem(pos, 2), w], dst_ref=recv[w].at[me],
                send_sem=send_sems.at[w * 3 + m], recv_sem=recv_sems.at[w * 3 + m],
                device_id=(lax.div(kk, 2), lax.rem(kk, 2), c), device_id_type=MESH)

        @pl.when(jnp.logical_and(p == 0, i == 0))
        def _():
            for cp in plan.copies(p_ins, p_outs, p_sems):
                cp.start()

        @pl.when(i == 0)
        def _():
            for t in acc:
                t[...] = jnp.zeros_like(t)

        nv = n_ref[...]
        acc[0][...] += _dot_tn(dg_ref[0], nv)
        acc[1][...] += _dot_tn(du_ref[0], nv)
        acc[2][...] += _dot_tn(a_ref[0], dacc_ref[...])

        @pl.when(jnp.logical_and(i == last_i, p >= 2))
        def _():
            for w in range(n_w):
                send(w, p - 2).wait_send()

        @pl.when(i == last_i)
        def _():
            for w in range(n_w):
                stage[lax.rem(p, 2), w] = acc[w][...].astype(BF16)

        @pl.when(jnp.logical_and(i == last_i, p < nck - 1))
        def _():
            for w in range(n_w):
                send(w, p).start()

        @pl.when(jnp.logical_and(i == last_i, p == nck - 1))
        def _():
            own = [pltpu.make_async_copy(stage.at[(nck - 1) % 2, w], recv[w].at[me], loc_sems.at[w])
                   for w in range(n_w)]
            for cp in own:
                cp.start()
            for w in range(n_w):
                send(w, nck - 2).wait_send()
            for cp in own:
                cp.wait()
            for w in range(n_w):
                for m in range(3):
                    pltpu.make_async_remote_copy(
                        src_ref=stage.at[0, w], dst_ref=recv[w].at[me],
                        send_sem=send_sems.at[w * 3 + m], recv_sem=recv_sems.at[w * 3 + m],
                        device_id=(0, 0, c), device_id_type=MESH).wait_recv()
            for cp in plan.copies(p_ins, p_outs, p_sems):
                cp.wait()

    chunk = lambda k, me_ref: lax.rem(me_ref[0] + 1 + k, nck)
    row = pl.BlockSpec((tm, d), lambda k, i, me_ref: (i, 0))
    hid = pl.BlockSpec((1, tm, f), lambda k, i, me_ref: (chunk(k, me_ref), i, 0))
    wshape = jax.ShapeDtypeStruct((nck, f, d), BF16)
    res = pl.pallas_call(
        body, name=name,
        grid_spec=pltpu.PrefetchScalarGridSpec(
            num_scalar_prefetch=1, grid=(nck, lp // tm),
            in_specs=[row, row, hid, hid, hid] + [ANY_SPEC] * n_p,
            out_specs=(ANY_SPEC,) * (n_w + n_p),
            scratch_shapes=[pltpu.VMEM((f, d), F32)] * n_w + [
                pltpu.VMEM((2, n_w, f, d), BF16), pltpu.SemaphoreType.DMA((n_w * 3,)),
                pltpu.SemaphoreType.DMA((n_w * 3,)), pltpu.SemaphoreType.DMA((n_w,))] + plan.scratch()),
        out_shape=(wshape,) * n_w + plan.out_shape(),
        compiler_params=_params(("arbitrary", "arbitrary")),
    )(chip.reshape(1).astype(jnp.int32), n, dacc, a, dg, du, *plan_args)
    return list(res[:n_w]), list(res[n_w:])


def _inproj_fwd(h, nw, w_in, cosf, sinf, rw):
    lp, d = h.shape
    nck, _, ps = w_in.shape
    proj = nck * ps
    sw = proj - 4 * rw
    tm = _tile(lp, 640)
    scale = HEAD_DIM ** -0.5
    heads = rw // HEAD_DIM

    def body(h_ref, nw_ref, w_ref, cos_ref, sin_ref, n_ref, q_ref, k_ref, v_ref, g_ref, u_ref, p_sc):
        xh, _ = _rms_stats(h_ref[...])
        n = (xh * nw_ref[...]).astype(BF16)
        n_ref[...] = n
        for c in range(nck):
            p_sc[:, c * ps:(c + 1) * ps] = _dot(n, w_ref[c])
        cs = cos_ref[...]
        sn = sin_ref[...]
        for hh in range(heads):
            lo = hh * HEAD_DIM
            qh = p_sc[:, lo:lo + HEAD_DIM]
            q_ref[:, lo:lo + HEAD_DIM] = (qh * cs + pltpu.roll(qh, HEAD_DIM // 2, 1) * sn).astype(BF16)
            kh = p_sc[:, rw + lo:rw + lo + HEAD_DIM]
            k_ref[:, lo:lo + HEAD_DIM] = ((kh * cs + pltpu.roll(kh, HEAD_DIM // 2, 1) * sn) * scale).astype(BF16)
        v_ref[...] = p_sc[:, 2 * rw:3 * rw].astype(BF16)
        g_ref[...] = p_sc[:, 3 * rw:4 * rw]
        u_ref[...] = p_sc[:, 4 * rw:]

    row = lambda w: pl.BlockSpec((tm, w), lambda i: (i, 0))
    return pl.pallas_call(
        body, name="inproj_fwd", grid=(lp // tm,),
        in_specs=[row(d), pl.BlockSpec((1, d), lambda i: (0, 0)),
                  pl.BlockSpec((nck, d, ps), lambda i: (0, 0, 0)), row(HEAD_DIM), row(HEAD_DIM)],
        out_specs=(row(d), row(rw), row(rw), row(rw), row(rw), row(sw)),
        out_shape=(jax.ShapeDtypeStruct((lp, d), BF16),
                   jax.ShapeDtypeStruct((lp, rw), BF16),
                   jax.ShapeDtypeStruct((lp, rw), BF16),
                   jax.ShapeDtypeStruct((lp, rw), BF16),
                   jax.ShapeDtypeStruct((lp, rw), F32),
                   jax.ShapeDtypeStruct((lp, sw), F32)),
        scratch_shapes=[pltpu.VMEM((tm, proj), F32)],
        compiler_params=_params(("arbitrary",)),
    )(h, nw, w_in, cosf, sinf)


def _inproj_bwd(dh, h, nw, n, w_in, dq, dk, dv, dg, du):
    lp, d = h.shape
    nck, _, ps = w_in.shape
    rw = dq.shape[1]
    sw = du.shape[1]
    proj = nck * ps
    tm = _tile(lp, 320)
    last = lp // tm - 1

    def body(dh_ref, h_ref, nw_ref, n_ref, w_ref, dq_ref, dk_ref, dv_ref, dg_ref, du_ref,
             dhi_ref, dnw_ref, dw_ref, p_sc, acc_sc):
        i = pl.program_id(0)

        @pl.when(i == 0)
        def _():
            dnw_ref[...] = jnp.zeros_like(dnw_ref)
            acc_sc[...] = jnp.zeros_like(acc_sc)

        p_sc[:, 0:rw] = dq_ref[...]
        p_sc[:, rw:2 * rw] = dk_ref[...]
        p_sc[:, 2 * rw:3 * rw] = dv_ref[...]
        p_sc[:, 3 * rw:4 * rw] = dg_ref[...]
        p_sc[:, 4 * rw:] = du_ref[...]
        nv = n_ref[...]
        dn = jnp.zeros((tm, d), F32)
        for c in range(nck):
            dp = p_sc[:, c * ps:(c + 1) * ps]
            dn = dn + _dot_nt(dp, w_ref[c])
            acc_sc[c] += _dot_tn(nv, dp)
        xh, r = _rms_stats(h_ref[...])
        dhi_ref[...] = dh_ref[...] + _rms_bwd(dn, xh, r, nw_ref[...])
        dnw_ref[...] += jnp.sum(dn * xh, axis=0, keepdims=True)

        @pl.when(i == last)
        def _():
            dw_ref[...] = acc_sc[...].astype(BF16)

    row = lambda w: pl.BlockSpec((tm, w), lambda i: (i, 0))
    vec = pl.BlockSpec((1, d), lambda i: (0, 0))
    wsp = pl.BlockSpec((nck, d, ps), lambda i: (0, 0, 0))
    return pl.pallas_call(
        body, name="inproj_bwd", grid=(lp // tm,),
        in_specs=[row(d), row(d), vec, row(d), wsp, row(rw), row(rw), row(rw), row(rw), row(sw)],
        out_specs=(row(d), vec, wsp),
        out_shape=(jax.ShapeDtypeStruct((lp, d), F32),
                   jax.ShapeDtypeStruct((1, d), F32),
                   jax.ShapeDtypeStruct((nck, d, ps), BF16)),
        scratch_shapes=[pltpu.VMEM((tm, proj), BF16), pltpu.VMEM((nck, d, ps), F32)],
        compiler_params=_params(("arbitrary",)),
    )(dh, h, nw, n, w_in, dq, dk, dv, dg, du)


def _retention_tables():
    h = jnp.arange(RET_HEADS, dtype=F32)
    log_g = jnp.log(1.0 - 2.0 ** (-5.0 - h))
    i = jnp.arange(CHUNK)
    diff = i[:, None] - i[None, :]
    dec = jnp.where(diff[None] >= 0,
                    jnp.exp(log_g[:, None, None] * jnp.maximum(diff, 0)[None].astype(F32)), 0.0)
    pos = jnp.arange(CHUNK, dtype=F32)
    wq = jnp.exp(log_g[:, None] * (pos + 1.0)[None])
    wk = jnp.exp(log_g[:, None] * (CHUNK - 1 - pos)[None])
    gch = jnp.exp(log_g * CHUNK)
    ones = jnp.ones((1, 1, HEAD_DIM), F32)
    return (dec, wq[:, :, None] * ones, wk[:, :, None] * ones,
            gch[:, None, None] * jnp.ones((1, 8, HEAD_DIM), F32))


def _head_norm(o):
    mu = jnp.mean(o, axis=-1, keepdims=True)
    oc = o - mu
    r = lax.rsqrt(jnp.mean(oc * oc, axis=-1, keepdims=True) + EPS)
    return oc * r, r


def _ret_fwd(q, k, v, g, rnw, tables):
    lp, rw = q.shape
    heads = rw // HEAD_DIM
    nch = lp // CHUNK
    dec, wq, wk, gch = tables

    def body(q_ref, k_ref, v_ref, g_ref, w_ref, dec_ref, wq_ref, wk_ref, gch_ref,
             o_ref, ret_ref, sp_ref, s_sc):
        n = pl.program_id(0)

        @pl.when(n == 0)
        def _():
            s_sc[...] = jnp.zeros_like(s_sc)

        for hh in range(heads):
            cs = slice(hh * HEAD_DIM, (hh + 1) * HEAD_DIM)
            qv, kv, vv = q_ref[:, cs], k_ref[:, cs], v_ref[:, cs]
            s_in = s_sc[hh]
            a = _dot_nt(qv, kv) * dec_ref[hh]
            qw = (qv.astype(F32) * wq_ref[hh]).astype(BF16)
            kw = (kv.astype(F32) * wk_ref[hh]).astype(BF16)
            o = _dot(a.astype(BF16), vv) + _dot(qw, s_in.astype(BF16))
            sp_ref[hh, 0] = s_in
            s_sc[hh] = gch_ref[hh, 0:1, :] * s_in + _dot_tn(kw, vv)
            o_ref[:, cs] = o
            xh, _ = _head_norm(o)
            gv = g_ref[:, cs]
            ret_ref[:, cs] = (gv * _sigmoid(gv) * (xh * w_ref[:, cs])).astype(BF16)

    blk = pl.BlockSpec((CHUNK, rw), lambda n: (n, 0))
    tab = pl.BlockSpec((heads, CHUNK, HEAD_DIM), lambda n: (0, 0, 0))
    return pl.pallas_call(
        body, name="retention_fwd", grid=(nch,),
        in_specs=[blk, blk, blk, blk, pl.BlockSpec((1, rw), lambda n: (0, 0)),
                  tab, tab, tab, pl.BlockSpec((heads, 8, HEAD_DIM), lambda n: (0, 0, 0))],
        out_specs=(blk, blk, pl.BlockSpec((heads, 1, HEAD_DIM, HEAD_DIM), lambda n: (0, n, 0, 0))),
        out_shape=(jax.ShapeDtypeStruct((lp, rw), F32),
                   jax.ShapeDtypeStruct((lp, rw), BF16),
                   jax.ShapeDtypeStruct((heads, nch, HEAD_DIM, HEAD_DIM), F32)),
        scratch_shapes=[pltpu.VMEM((heads, HEAD_DIM, HEAD_DIM), F32)],
        compiler_params=_params(("arbitrary",)),
    )(q, k, v, g, rnw, dec, wq, wk, gch)


def _ret_bwd(dret, q, k, v, g, o, sprev, rnw, tables, cosf, sinf):
    lp, rw = q.shape
    heads = rw // HEAD_DIM
    nch = lp // CHUNK
    dec, wq, wk, gch = tables
    scale = HEAD_DIM ** -0.5
    half = HEAD_DIM // 2

    def body(dret_ref, q_ref, k_ref, v_ref, g_ref, o_ref, sp_ref, w_ref, dec_ref, wq_ref, wk_ref, gch_ref,
             cos_ref, sin_ref, dq_ref, dk_ref, dv_ref, dg_ref, dw_ref, ds_sc):
        n = pl.program_id(0)

        @pl.when(n == 0)
        def _():
            ds_sc[...] = jnp.zeros_like(ds_sc)
            dw_ref[...] = jnp.zeros_like(dw_ref)

        cosv = cos_ref[...]
        sinv = sin_ref[...]
        for hh in range(heads):
            cs = slice(hh * HEAD_DIM, (hh + 1) * HEAD_DIM)
            qv, kv, vv = q_ref[:, cs], k_ref[:, cs], v_ref[:, cs]
            gv = g_ref[:, cs]
            dr = dret_ref[:, cs]
            w = w_ref[:, cs]
            sg = _sigmoid(gv)
            sil = gv * sg
            xh, r = _head_norm(o_ref[:, cs])
            dg_ref[:, cs] = (dr * (xh * w) * (sg * (1.0 + gv * (1.0 - sg)))).astype(BF16)
            dyw = dr * sil
            dw_ref[:, cs] += jnp.sum(dyw * xh, axis=0, keepdims=True)
            dxh = dyw * w
            do = r * (dxh - jnp.mean(dxh, axis=-1, keepdims=True)
                      - xh * jnp.mean(dxh * xh, axis=-1, keepdims=True))
            dob = do.astype(BF16)
            dmask = dec_ref[hh]
            wqv = wq_ref[hh]
            wkv = wk_ref[hh]
            a = (_dot_nt(qv, kv) * dmask).astype(BF16)
            da = (_dot_nt(dob, vv) * dmask).astype(BF16)
            qw = (qv.astype(F32) * wqv).astype(BF16)
            kw = (kv.astype(F32) * wkv).astype(BF16)
            s_in = sp_ref[hh, 0].astype(BF16)
            ds = ds_sc[hh]
            dsb = ds.astype(BF16)
            dq = _dot(da, kv) + _dot_nt(dob, s_in) * wqv
            dk = _dot_tn(da, qv) + _dot_nt(vv, dsb) * wkv
            dv = _dot_tn(a, dob) + _dot(kw, dsb)
            ds_sc[hh] = gch_ref[hh, 0:1, :] * ds + _dot_tn(qw, dob)
            dq_ref[:, cs] = (dq * cosv + pltpu.roll(dq * sinv, half, 1)).astype(BF16)
            dk_ref[:, cs] = ((dk * cosv + pltpu.roll(dk * sinv, half, 1)) * scale).astype(BF16)
            dv_ref[:, cs] = dv.astype(BF16)

    blk = pl.BlockSpec((CHUNK, rw), lambda n: (nch - 1 - n, 0))
    tab = pl.BlockSpec((heads, CHUNK, HEAD_DIM), lambda n: (0, 0, 0))
    wsp = pl.BlockSpec((1, rw), lambda n: (0, 0))
    pos = pl.BlockSpec((CHUNK, HEAD_DIM), lambda n: (nch - 1 - n, 0))
    bshape = jax.ShapeDtypeStruct((lp, rw), BF16)
    return pl.pallas_call(
        body, name="retention_bwd", grid=(nch,),
        in_specs=[blk, blk, blk, blk, blk, blk,
                  pl.BlockSpec((heads, 1, HEAD_DIM, HEAD_DIM), lambda n: (0, nch - 1 - n, 0, 0)),
                  wsp, tab, tab, tab, pl.BlockSpec((heads, 8, HEAD_DIM), lambda n: (0, 0, 0)), pos, pos],
        out_specs=(blk, blk, blk, blk, wsp),
        out_shape=(bshape, bshape, bshape, bshape, jax.ShapeDtypeStruct((1, rw), F32)),
        scratch_shapes=[pltpu.VMEM((heads, HEAD_DIM, HEAD_DIM), F32)],
        compiler_params=_params(("arbitrary",)),
    )(dret, q, k, v, g, o, sprev, rnw, dec, wq, wk, gch, cosf, sinf)


SCAN_CW = 512


def _s5_prepare(lam_re, lam_im, log_dt, b_re, b_im):
    dt = jnp.exp(log_dt)[:, None]
    er = jnp.exp(lam_re * dt)
    ar = er * jnp.cos(lam_im * dt)
    ai = er * jnp.sin(lam_im * dt)
    den = lam_re * lam_re + lam_im * lam_im
    fr = ((ar - 1.0) * lam_re + ai * lam_im) / den
    fi = (ai * lam_re - (ar - 1.0) * lam_im) / den
    bbr = fr[..., None] * b_re - fi[..., None] * b_im
    bbi = fr[..., None] * b_im + fi[..., None] * b_re
    return ar, ai, bbr, bbi


def _blockdiag_in(t):
    g, p, n = t.shape
    gs = g // N_SEC
    t = t.reshape(N_SEC, gs, p, n)
    eye = jnp.eye(gs, dtype=t.dtype)
    return jnp.einsum("sgpn,gh->sgphn", t, eye).reshape(N_SEC, gs * p, gs * n)


def _blockdiag_out(m, g, p, n):
    gs = g // N_SEC
    m = m.reshape(N_SEC, gs, p, gs, n)
    eye = jnp.eye(gs, dtype=m.dtype)
    return jnp.einsum("sgphn,gh->sgpn", m, eye).reshape(g, p, n)


def _scan_step(xr_ref, xi_ref, r0, pr_of, ar_ref, ai_ref, conj, ncols):
    for cc in range(ncols // SCAN_CW):
        cs = pl.ds(cc * SCAN_CW, SCAN_CW)
        pr, pi = pr_of(cs)
        ar = ar_ref[:, cs]
        ai = ai_ref[:, cs]
        if conj:
            nr = ar * pr + ai * pi
            ni = ar * pi - ai * pr
        else:
            nr = ar * pr - ai * pi
            ni = ar * pi + ai * pr
        xr_ref[pl.ds(r0, 8), cs] = xr_ref[pl.ds(r0, 8), cs] + nr
        xi_ref[pl.ds(r0, 8), cs] = xi_ref[pl.ds(r0, 8), cs] + ni


def _shift_rows(z, down):
    row = lax.broadcasted_iota(jnp.int32, z.shape, 0)
    if down:
        return jnp.where(row == 0, 0.0, pltpu.roll(z, 1, 0))
    return jnp.where(row == N_SEG - 1, 0.0, pltpu.roll(z, N_SEG - 1, 0))


def _s5_fwd(u, bsr, bsi, csr, csi, a8r, a8i, al8r, al8i, d, gluw, glub, nw, jb):
    lp, sw = u.shape
    ns = a8r.shape[1]
    rows = N_SEG * jb
    nblk = lp // rows
    secw = sw // N_SEC
    secn = ns // N_SEC

    def local_scan(u_ref, bsr_ref, bsi_ref, ar_ref, ai_ref, xr_ref, xi_ref, pr_sc, pi_sc):
        for s in range(N_SEC):
            ub = u_ref[:, s * secw:(s + 1) * secw].astype(BF16)
            xr_ref[:, s * secn:(s + 1) * secn] = _dot(ub, bsr_ref[s])
            xi_ref[:, s * secn:(s + 1) * secn] = _dot(ub, bsi_ref[s])
        _scan_step(xr_ref, xi_ref, 0, lambda cs: (pr_sc[:, cs], pi_sc[:, cs]), ar_ref, ai_ref, False, ns)

        def step(j, carry):
            r0 = pl.multiple_of(j * 8, 8)
            rp = pl.multiple_of((j - 1) * 8, 8)
            _scan_step(xr_ref, xi_ref, r0,
                       lambda cs: (xr_ref[pl.ds(rp, 8), cs], xi_ref[pl.ds(rp, 8), cs]),
                       ar_ref, ai_ref, False, ns)
            return carry

        lax.fori_loop(1, jb, step, 0)
        pr_sc[...] = xr_ref[rows - 8:rows, :]
        pi_sc[...] = xi_ref[rows - 8:rows, :]

    def carry_body(u_ref, bsr_ref, bsi_ref, ar_ref, ai_ref, alr_ref, ali_ref, cr_ref, ci_ref,
                   xr_sc, xi_sc, pr_sc, pi_sc):
        b = pl.program_id(0)

        @pl.when(b == 0)
        def _():
            pr_sc[...] = jnp.zeros_like(pr_sc)
            pi_sc[...] = jnp.zeros_like(pi_sc)

        local_scan(u_ref, bsr_ref, bsi_ref, ar_ref, ai_ref, xr_sc, xi_sc, pr_sc, pi_sc)

        @pl.when(b == nblk - 1)
        def _():
            er = _shift_rows(pr_sc[...], True)
            ei = _shift_rows(pi_sc[...], True)
            alr, ali = alr_ref[...], ali_ref[...]
            cr, ci = er, ei
            for _ in range(N_SEG - 2):
                sr = _shift_rows(cr, True)
                si = _shift_rows(ci, True)
                cr = er + alr * sr - ali * si
                ci = ei + alr * si + ali * sr
            cr_ref[...] = cr
            ci_ref[...] = ci

    ublk = pl.BlockSpec((rows, sw), lambda b: (b, 0))
    bspec = pl.BlockSpec((N_SEC, secw, secn), lambda b: (0, 0, 0))
    cspec = pl.BlockSpec((N_SEC, secn, secw), lambda b: (0, 0, 0))
    s8 = pl.BlockSpec((N_SEG, ns), lambda b: (0, 0))
    vec = pl.BlockSpec((1, sw), lambda b: (0, 0))
    s8shape = jax.ShapeDtypeStruct((N_SEG, ns), F32)
    c0r, c0i = pl.pallas_call(
        carry_body, name="s5_fwd_carry", grid=(nblk,),
        in_specs=[ublk, bspec, bspec, s8, s8, s8, s8],
        out_specs=(s8, s8), out_shape=(s8shape, s8shape),
        scratch_shapes=[pltpu.VMEM((rows, ns), F32), pltpu.VMEM((rows, ns), F32),
                        pltpu.VMEM((N_SEG, ns), F32), pltpu.VMEM((N_SEG, ns), F32)],
        compiler_params=_params(("arbitrary",)),
    )(u, bsr, bsi, a8r, a8i, al8r, al8i)

    def main_body(u_ref, bsr_ref, bsi_ref, csr_ref, csi_ref, ar_ref, ai_ref, c0r_ref, c0i_ref,
                  d_ref, gw_ref, gb_ref, nw_ref, xr_ref, xi_ref, yp_ref, out_ref, pr_sc, pi_sc):
        b = pl.program_id(0)

        @pl.when(b == 0)
        def _():
            pr_sc[...] = c0r_ref[...]
            pi_sc[...] = c0i_ref[...]

        local_scan(u_ref, bsr_ref, bsi_ref, ar_ref, ai_ref, xr_ref, xi_ref, pr_sc, pi_sc)
        for s in range(N_SEC):
            xs = pl.ds(s * secn, secn)
            us = pl.ds(s * secw, secw)
            y = _dot(xr_ref[:, xs].astype(BF16), csr_ref[s]) + _dot(xi_ref[:, xs].astype(BF16), csi_ref[s])
            yp_ref[:, us] = y + d_ref[:, us] * u_ref[:, us]
        yp = yp_ref[...]
        t = jnp.tanh(GELU_K0 * (yp + GELU_K1 * yp * yp * yp))
        y1 = 0.5 * yp * (1.0 + t)
        z = _dot(y1.astype(BF16), gw_ref[...]) + gb_ref[...]
        y2 = y1 * _sigmoid(z)
        xh, _ = _rms_stats(y2)
        out_ref[...] = (xh * nw_ref[...]).astype(BF16)

    xblk = pl.BlockSpec((rows, ns), lambda b: (b, 0))
    xr, xi, yp, out = pl.pallas_call(
        main_body, name="s5_fwd", grid=(nblk,),
        in_specs=[ublk, bspec, bspec, cspec, cspec, s8, s8, s8, s8, vec,
                  pl.BlockSpec((sw, sw), lambda b: (0, 0)), vec, vec],
        out_specs=(xblk, xblk, ublk, ublk),
        out_shape=(jax.ShapeDtypeStruct((lp, ns), F32), jax.ShapeDtypeStruct((lp, ns), F32),
                   jax.ShapeDtypeStruct((lp, sw), F32), jax.ShapeDtypeStruct((lp, sw), BF16)),
        scratch_shapes=[pltpu.VMEM((N_SEG, ns), F32), pltpu.VMEM((N_SEG, ns), F32)],
        compiler_params=_params(("arbitrary",)),
    )(u, bsr, bsi, csr, csi, a8r, a8i, c0r, c0i, d, gluw, glub, nw)
    return xr, xi, c0r, c0i, yp, out


def _s5_bwd(dout, u, yp, xr, xi, c0r, c0i, bsrt, bsit, csrt, csit, a8r, a8i, al8r, al8i, d, gluw, glub, nw, jb):
    lp, sw = u.shape
    ns = a8r.shape[1]
    rows = N_SEG * jb
    nblk = lp // rows
    secw = sw // N_SEC
    secn = ns // N_SEC

    def rowwise_bwd(dout_ref, yp_ref, gw_ref, gb_ref, nw_ref):
        ypv = yp_ref[...]
        t = jnp.tanh(GELU_K0 * (ypv + GELU_K1 * ypv * ypv * ypv))
        y1 = 0.5 * ypv * (1.0 + t)
        dgelu = 0.5 * (1.0 + t) + 0.5 * ypv * (1.0 - t * t) * GELU_K0 * (1.0 + 3.0 * GELU_K1 * ypv * ypv)
        gw = gw_ref[...]
        y1b = y1.astype(BF16)
        sg = _sigmoid(_dot(y1b, gw) + gb_ref[...])
        xh, r = _rms_stats(y1 * sg)
        dov = dout_ref[...]
        dy2 = _rms_bwd(dov, xh, r, nw_ref[...])
        dz = dy2 * y1 * sg * (1.0 - sg)
        dzb = dz.astype(BF16)
        dy1 = dy2 * sg + _dot_nt(dzb, gw)
        return dy1 * dgelu, dov * xh, y1b, dzb, dz

    def lam_scan(dyp_of, csrt_ref, csit_ref, ar_ref, ai_ref, lr_sc, li_sc, nr_sc, ni_sc, extra):
        for s in range(N_SEC):
            db = dyp_of(s)
            lr_sc[:, s * secn:(s + 1) * secn] = _dot(db, csrt_ref[s])
            li_sc[:, s * secn:(s + 1) * secn] = _dot(db, csit_ref[s])
        top = rows - 8
        _scan_step(lr_sc, li_sc, top, lambda cs: (nr_sc[:, cs], ni_sc[:, cs]), ar_ref, ai_ref, True, ns)
        extra(top, pl.ds(top - 8, 8))

        def step(jj, carry):
            r0 = pl.multiple_of((jb - 1 - jj) * 8, 8)
            rn = pl.multiple_of((jb - jj) * 8, 8)
            rp = pl.multiple_of((jb - 2 - jj) * 8, 8)
            _scan_step(lr_sc, li_sc, r0,
                       lambda cs: (lr_sc[pl.ds(rn, 8), cs], li_sc[pl.ds(rn, 8), cs]),
                       ar_ref, ai_ref, True, ns)
            extra(r0, pl.ds(rp, 8))
            return carry

        lax.fori_loop(1, jb - 1, step, 0)
        _scan_step(lr_sc, li_sc, 0, lambda cs: (lr_sc[8:16, cs], li_sc[8:16, cs]), ar_ref, ai_ref, True, ns)
        extra(0, None)
        nr_sc[...] = lr_sc[0:8, :]
        ni_sc[...] = li_sc[0:8, :]

    def carry_body(dout_ref, yp_ref, gw_ref, gb_ref, nw_ref, csrt_ref, csit_ref, ar_ref, ai_ref,
                   alr_ref, ali_ref, cr_ref, ci_ref, lr_sc, li_sc, nr_sc, ni_sc, dyp_sc):
        b = pl.program_id(0)

        @pl.when(b == 0)
        def _():
            nr_sc[...] = jnp.zeros_like(nr_sc)
            ni_sc[...] = jnp.zeros_like(ni_sc)

        dyp, _, _, _, _ = rowwise_bwd(dout_ref, yp_ref, gw_ref, gb_ref, nw_ref)
        dyp_sc[...] = dyp.astype(BF16)
        lam_scan(lambda s: dyp_sc[:, s * secw:(s + 1) * secw], csrt_ref, csit_ref, ar_ref, ai_ref,
                 lr_sc, li_sc, nr_sc, ni_sc, lambda r0, prev_rows: None)

        @pl.when(b == nblk - 1)
        def _():
            fr = _shift_rows(nr_sc[...], False)
            fi = _shift_rows(ni_sc[...], False)
            alr, ali = alr_ref[...], ali_ref[...]
            cr, ci = fr, fi
            for _ in range(N_SEG - 2):
                sr = _shift_rows(cr, False)
                si = _shift_rows(ci, False)
                cr = fr + alr * sr + ali * si
                ci = fi + alr * si - ali * sr
            cr_ref[...] = cr
            ci_ref[...] = ci

    rev = lambda b: (nblk - 1 - b, 0)
    ublk = pl.BlockSpec((rows, sw), rev)
    xblk = pl.BlockSpec((rows, ns), rev)
    s8 = pl.BlockSpec((N_SEG, ns), lambda b: (0, 0))
    vec = pl.BlockSpec((1, sw), lambda b: (0, 0))
    gws = pl.BlockSpec((sw, sw), lambda b: (0, 0))
    btspec = pl.BlockSpec((N_SEC, secn, secw), lambda b: (0, 0, 0))
    ctspec = pl.BlockSpec((N_SEC, secw, secn), lambda b: (0, 0, 0))
    s8shape = jax.ShapeDtypeStruct((N_SEG, ns), F32)
    lcr, lci = pl.pallas_call(
        carry_body, name="s5_bwd_carry", grid=(nblk,),
        in_specs=[ublk, ublk, gws, vec, vec, ctspec, ctspec, s8, s8, s8, s8],
        out_specs=(s8, s8), out_shape=(s8shape, s8shape),
        scratch_shapes=[pltpu.VMEM((rows, ns), F32), pltpu.VMEM((rows, ns), F32),
                        pltpu.VMEM((N_SEG, ns), F32), pltpu.VMEM((N_SEG, ns), F32),
                        pltpu.VMEM((rows, sw), BF16)],
        compiler_params=_params(("arbitrary",)),
    )(dout, yp, gluw, glub, nw, csrt, csit, a8r, a8i, al8r, al8i)

    def main_body(dout_ref, yp_ref, u_ref, xr_ref, xi_ref, xtr_ref, xti_ref, c0r_ref, c0i_ref, lcr_ref, lci_ref,
                  gw_ref, gb_ref, nw_ref, d_ref, bsrt_ref, bsit_ref, csrt_ref, csit_ref, ar_ref, ai_ref,
                  du_ref, dnw_ref, dgw_ref, dgb_ref, dd_ref, dcr_ref, dci_ref, dbr_ref, dbi_ref, dar_ref, dai_ref,
                  lr_sc, li_sc, nr_sc, ni_sc, dyp_sc):
        b = pl.program_id(0)

        @pl.when(b == 0)
        def _():
            nr_sc[...] = lcr_ref[...]
            ni_sc[...] = lci_ref[...]
            for ref in (dnw_ref, dgw_ref, dgb_ref, dd_ref, dcr_ref, dci_ref, dbr_ref, dbi_ref, dar_ref, dai_ref):
                ref[...] = jnp.zeros_like(ref)

        dyp, dnw_rows, y1b, dzb, dz = rowwise_bwd(dout_ref, yp_ref, gw_ref, gb_ref, nw_ref)
        dnw_ref[...] += jnp.sum(dnw_rows, axis=0, keepdims=True)
        dgw_ref[...] += _dot_tn(y1b, dzb)
        dgb_ref[...] += jnp.sum(dz, axis=0, keepdims=True)
        uv = u_ref[...]
        dd_ref[...] += jnp.sum(dyp * uv, axis=0, keepdims=True)
        dyp_sc[...] = dyp.astype(BF16)
        for s in range(N_SEC):
            db = dyp_sc[:, s * secw:(s + 1) * secw]
            xs = pl.ds(s * secn, secn)
            dcr_ref[s] += _dot_tn(xr_ref[:, xs].astype(BF16), db)
            dci_ref[s] += _dot_tn(xi_ref[:, xs].astype(BF16), db)

        first = b == nblk - 1

        def acc_da(r0, prev_rows):
            for cc in range(ns // SCAN_CW):
                cs = pl.ds(cc * SCAN_CW, SCAN_CW)
                lr = lr_sc[pl.ds(r0, 8), cs]
                li = li_sc[pl.ds(r0, 8), cs]
                if prev_rows is None:
                    xpr = jnp.where(first, c0r_ref[:, cs], xtr_ref[:, cs])
                    xpi = jnp.where(first, c0i_ref[:, cs], xti_ref[:, cs])
                else:
                    xpr = xr_ref[prev_rows, cs]
                    xpi = xi_ref[prev_rows, cs]
                dar_ref[:, cs] += lr * xpr + li * xpi
                dai_ref[:, cs] += li * xpr - lr * xpi

        lam_scan(lambda s: dyp_sc[:, s * secw:(s + 1) * secw], csrt_ref, csit_ref, ar_ref, ai_ref,
                 lr_sc, li_sc, nr_sc, ni_sc, acc_da)

        for s in range(N_SEC):
            xs = pl.ds(s * secn, secn)
            us = pl.ds(s * secw, secw)
            lrb = lr_sc[:, xs].astype(BF16)
            lib = li_sc[:, xs].astype(BF16)
            du = _dot(lrb, bsrt_ref[s]) + _dot(lib, bsit_ref[s]) + d_ref[:, us] * dyp_sc[:, us].astype(F32)
            du_ref[:, us] = du.astype(BF16)
            ub = u_ref[:, us].astype(BF16)
            dbr_ref[s] += _dot_tn(ub, lrb)
            dbi_ref[s] += _dot_tn(ub, lib)

    tail = pl.BlockSpec((N_SEG, ns), lambda b: (jnp.maximum((nblk - 1 - b) * jb - 1, 0), 0))
    acc_c = pl.BlockSpec((N_SEC, secn, secw), lambda b: (0, 0, 0))
    acc_b = pl.BlockSpec((N_SEC, secw, secn), lambda b: (0, 0, 0))
    outs = pl.pallas_call(
        main_body, name="s5_bwd", grid=(nblk,),
        in_specs=[ublk, ublk, ublk, xblk, xblk, tail, tail, s8, s8, s8, s8,
                  gws, vec, vec, vec, btspec, btspec, ctspec, ctspec, s8, s8],
        out_specs=(ublk, vec, gws, vec, vec, acc_c, acc_c, acc_b, acc_b, s8, s8),
        out_shape=(jax.ShapeDtypeStruct((lp, sw), BF16),
                   jax.ShapeDtypeStruct((1, sw), F32),
                   jax.ShapeDtypeStruct((sw, sw), F32),
                   jax.ShapeDtypeStruct((1, sw), F32),
                   jax.ShapeDtypeStruct((1, sw), F32),
                   jax.ShapeDtypeStruct((N_SEC, secn, secw), F32),
                   jax.ShapeDtypeStruct((N_SEC, secn, secw), F32),
                   jax.ShapeDtypeStruct((N_SEC, secw, secn), F32),
                   jax.ShapeDtypeStruct((N_SEC, secw, secn), F32),
                   s8shape, s8shape),
        scratch_shapes=[pltpu.VMEM((rows, ns), F32), pltpu.VMEM((rows, ns), F32),
                        pltpu.VMEM((N_SEG, ns), F32), pltpu.VMEM((N_SEG, ns), F32),
                        pltpu.VMEM((rows, sw), BF16)],
        compiler_params=_params(("arbitrary",)),
    )(dout, yp, u, xr, xi, xr, xi, c0r, c0i, lcr, lci, gluw, glub, nw, d, bsrt, bsit, csrt, csit, a8r, a8i)
    return outs


def _outproj_fwd(h, ret, ssm, wo):
    lp, d = h.shape
    nck, rs, _ = wo.shape
    rw = ret.shape[1]
    tm = _tile(lp, 640)
    per = rw // rs

    def body(h_ref, ret_ref, ssm_ref, w_ref, o_ref):
        acc = h_ref[...]
        for c in range(nck):
            src = ret_ref if c < per else ssm_ref
            lo = (c % per) * rs
            acc = acc + _dot(src[:, lo:lo + rs], w_ref[c])
        o_ref[...] = acc

    row = lambda w: pl.BlockSpec((tm, w), lambda i: (i, 0))
    return pl.pallas_call(
        body, name="outproj_fwd", grid=(lp // tm,),
        in_specs=[row(d), row(rw), row(ssm.shape[1]), pl.BlockSpec((nck, rs, d), lambda i: (0, 0, 0))],
        out_specs=row(d), out_shape=jax.ShapeDtypeStruct((lp, d), F32),
        compiler_params=_params(("arbitrary",)),
    )(h, ret, ssm, wo)


def _outproj_bwd(dh, ret, ssm, wo):
    lp, d = dh.shape
    nck, rs, _ = wo.shape
    rw = ret.shape[1]
    sw = ssm.shape[1]
    tm = _tile(lp, 640)
    per = rw // rs
    last = lp // tm - 1

    def body(dh_ref, ret_ref, ssm_ref, w_ref, dret_ref, dssm_ref, dw_ref, acc_sc):
        i = pl.program_id(0)

        @pl.when(i == 0)
        def _():
            acc_sc[...] = jnp.zeros_like(acc_sc)

        dhb = dh_ref[...].astype(BF16)
        for c in range(nck):
            src, dst = (ret_ref, dret_ref) if c < per else (ssm_ref, dssm_ref)
            lo = (c % per) * rs
            dst[:, lo:lo + rs] = _dot_nt(dhb, w_ref[c])
            acc_sc[c] += _dot_tn(src[:, lo:lo + rs], dhb)

        @pl.when(i == last)
        def _():
            dw_ref[...] = acc_sc[...].astype(BF16)

    row = lambda w: pl.BlockSpec((tm, w), lambda i: (i, 0))
    wsp = pl.BlockSpec((nck, rs, d), lambda i: (0, 0, 0))
    return pl.pallas_call(
        body, name="outproj_bwd", grid=(lp // tm,),
        in_specs=[row(d), row(rw), row(sw), wsp],
        out_specs=(row(rw), row(sw), wsp),
        out_shape=(jax.ShapeDtypeStruct((lp, rw), F32), jax.ShapeDtypeStruct((lp, sw), F32),
                   jax.ShapeDtypeStruct((nck, rs, d), BF16)),
        scratch_shapes=[pltpu.VMEM((nck, rs, d), F32)],
        compiler_params=_params(("arbitrary",)),
    )(dh, ret, ssm, wo)


def _loss_head(h, fw, target):
    lp, d = h.shape
    nblk = lp // CHUNK

    def body(h_ref, w_ref, t_ref, loss_ref, dh_ref, dw_ref):
        i = pl.program_id(0)

        @pl.when(i == 0)
        def _():
            loss_ref[...] = jnp.zeros_like(loss_ref)
            dw_ref[...] = jnp.zeros_like(dw_ref)
            dh_ref[...] = jnp.zeros_like(dh_ref)

        @pl.when(i > 0)
        def _():
            xh, r = _rms_stats(h_ref[...])
            w = w_ref[...]
            err = xh * w - t_ref[...]
            loss_ref[...] += 0.5 * jnp.sum(err * err) / d
            dout = err * (1.0 / d)
            dw_ref[...] += jnp.sum(dout * xh, axis=0, keepdims=True)
            dh_ref[...] = _rms_bwd(dout, xh, r, w)

    return pl.pallas_call(
        body, name="loss_head", grid=(nblk,),
        in_specs=[pl.BlockSpec((CHUNK, d), lambda i: (i, 0)), pl.BlockSpec((1, d), lambda i: (0, 0)),
                  pl.BlockSpec((CHUNK, d), lambda i: (jnp.maximum(i - 1, 0), 0))],
        out_specs=(pl.BlockSpec((8, LANE), lambda i: (0, 0)), pl.BlockSpec((CHUNK, d), lambda i: (i, 0)),
                   pl.BlockSpec((1, d), lambda i: (0, 0))),
        out_shape=(jax.ShapeDtypeStruct((8, LANE), F32), jax.ShapeDtypeStruct((lp, d), F32),
                   jax.ShapeDtypeStruct((1, d), F32)),
        compiler_params=_params(("arbitrary",)),
    )(h, fw, target)


def _pack(arrs):
    flat = jnp.concatenate([a.reshape(-1).astype(F32) for a in arrs])
    n = flat.shape[0]
    rows = -(-n // (8 * LANE)) * 8
    return jnp.pad(flat, (0, rows * LANE - n)).reshape(rows, LANE)


def _unpack(packed, shapes):
    flat = packed.reshape(-1)
    out, off = [], 0
    for s in shapes:
        n = math.prod(s)
        out.append(flat[off:off + n].reshape(s))
        off += n
    return out


def _to_segments(a, seg_len):
    return a.reshape(N_SEG, seg_len, a.shape[1]).transpose(1, 0, 2).reshape(a.shape)


def _from_segments(a, seg_len):
    return a.reshape(seg_len, N_SEG, a.shape[1]).transpose(1, 0, 2).reshape(a.shape)


WEIGHT_NAMES = ['meta_tokens', 'ffn1_norm_w', 'ffn1_w_gate', 'ffn1_w_up', 'ffn1_w_down', 'mix_norm_w', 'w_in',
                'ret_norm_w', 'ssm_lambda_re', 'ssm_lambda_im', 'ssm_log_dt', 'ssm_b_re', 'ssm_b_im', 'ssm_c_re',
                'ssm_c_im', 'ssm_d', 'ssm_glu_w', 'ssm_glu_b', 'ssm_norm_w', 'w_out', 'ffn2_norm_w', 'ffn2_w_gate',
                'ffn2_w_up', 'ffn2_w_down', 'final_norm_w']
BIG = ['ffn1_w_gate', 'ffn1_w_up', 'ffn1_w_down', 'w_in', 'ssm_glu_w', 'w_out', 'ffn2_w_gate', 'ffn2_w_up',
       'ffn2_w_down']
TRANSPOSED = ['ffn1_w_gate', 'ffn1_w_up', 'ffn2_w_gate', 'ffn2_w_up']
BIG_EARLY = ['ffn1_w_gate', 'ffn1_w_up', 'ffn1_w_down']
BIG_LATE = [n for n in BIG if n not in BIG_EARLY]
SMALL = [n for n in WEIGHT_NAMES if n not in BIG]


def kernel(x, meta_tokens, ffn1_norm_w, ffn1_w_gate, ffn1_w_up, ffn1_w_down, mix_norm_w, w_in, ret_norm_w, ssm_lambda_re, ssm_lambda_im, ssm_log_dt, ssm_b_re, ssm_b_im, ssm_c_re, ssm_c_im, ssm_d, ssm_glu_w, ssm_glu_b, ssm_norm_w, w_out, ffn2_norm_w, ffn2_w_gate, ffn2_w_up, ffn2_w_down, final_norm_w, loss_target, m_meta_tokens, m_ffn1_norm_w, m_ffn1_w_gate, m_ffn1_w_up, m_ffn1_w_down, m_mix_norm_w, m_w_in, m_ret_norm_w, m_ssm_lambda_re, m_ssm_lambda_im, m_ssm_log_dt, m_ssm_b_re, m_ssm_b_im, m_ssm_c_re, m_ssm_c_im, m_ssm_d, m_ssm_glu_w, m_ssm_glu_b, m_ssm_norm_w, m_w_out, m_ffn2_norm_w, m_ffn2_w_gate, m_ffn2_w_up, m_ffn2_w_down, m_final_norm_w, v_meta_tokens, v_ffn1_norm_w, v_ffn1_w_gate, v_ffn1_w_up, v_ffn1_w_down, v_mix_norm_w, v_w_in, v_ret_norm_w, v_ssm_lambda_re, v_ssm_lambda_im, v_ssm_log_dt, v_ssm_b_re, v_ssm_b_im, v_ssm_c_re, v_ssm_c_im, v_ssm_d, v_ssm_glu_w, v_ssm_glu_b, v_ssm_norm_w, v_w_out, v_ffn2_norm_w, v_ffn2_w_gate, v_ffn2_w_up, v_ffn2_w_down, v_final_norm_w):
    args = locals()
    w = {n: args[n] for n in WEIGHT_NAMES}
    m = {n: args["m_" + n] for n in WEIGHT_NAMES}
    v = {n: args["v_" + n] for n in WEIGHT_NAMES}

    seq, d = x.shape[1], x.shape[2]
    lp = seq + CHUNK
    seg_len = lp // N_SEG
    rw = RET_HEADS * HEAD_DIM
    sw = ssm_d.shape[-1]
    groups = sw // SSM_GROUP
    ns = groups * SSM_STATE
    jb = _tile(seg_len, 40, 8)
    chip = 2 * lax.axis_index("x") + lax.axis_index("y")

    as_fd = lambda t: jnp.swapaxes(t, -1, -2)
    shards = {n: (as_fd(w[n][0]) if n in TRANSPOSED else w[n][0]).astype(BF16) for n in BIG}
    early = [shards[n] for n in BIG_EARLY] + [meta_tokens]
    gathered = _forward_sibling("gather_early_forward",
                                _exchange("gather_early", _allgather_chips_plan(early), early))
    gw = dict(zip(BIG_EARLY, gathered[:-1]))
    meta_full = jnp.transpose(gathered[-1], (1, 0, 2)).reshape(N_META, d)
    late = [shards[n] for n in BIG_LATE]

    pos = jnp.arange(lp, dtype=F32) - float(CHUNK - N_META)
    freqs = 1.0 / (ROPE_BASE ** (jnp.arange(0, HEAD_DIM, 2, dtype=F32) / HEAD_DIM))
    ang = pos[:, None] * freqs[None, :]
    cosf = jnp.concatenate([jnp.cos(ang), jnp.cos(ang)], axis=1)
    sinf = jnp.concatenate([-jnp.sin(ang), jnp.sin(ang)], axis=1)
    tables = _retention_tables()

    lam_re, lam_im, log_dt = ssm_lambda_re[0], ssm_lambda_im[0], ssm_log_dt[0]
    b_re, b_im, c_re, c_im = ssm_b_re[0], ssm_b_im[0], ssm_c_re[0], ssm_c_im[0]
    (ar, ai, bbr, bbi), prep_vjp = jax.vjp(_s5_prepare, lam_re, lam_im, log_dt, b_re, b_im)
    dt = jnp.exp(log_dt)[:, None]
    el = jnp.exp(seg_len * lam_re * dt)
    alr = el * jnp.cos(seg_len * lam_im * dt)
    ali = el * jnp.sin(seg_len * lam_im * dt)
    bc8 = lambda t: jnp.broadcast_to(t.reshape(1, ns), (N_SEG, ns))
    a8r, a8i, al8r, al8i = bc8(ar), bc8(ai), bc8(alr), bc8(ali)
    bsr = _blockdiag_in(jnp.transpose(bbr, (0, 2, 1)))
    bsi = _blockdiag_in(jnp.transpose(bbi, (0, 2, 1)))
    csrt = _blockdiag_in(c_re)
    csit = _blockdiag_in(-c_im)
    tr = lambda t: jnp.transpose(t, (0, 2, 1))
    bsr_b, bsi_b = bsr.astype(BF16), bsi.astype(BF16)
    csr_b, csi_b = tr(csrt).astype(BF16), tr(csit).astype(BF16)
    bsrt_b, bsit_b = tr(bsr).astype(BF16), tr(bsi).astype(BF16)
    csrt_b, csit_b = csrt.astype(BF16), csit.astype(BF16)

    h0 = jnp.concatenate([jnp.zeros((CHUNK - N_META, d), F32), meta_full, x[0]], axis=0)
    (h1, g1, u1), late_half = _ffn_fwd("ffn1_fwd", h0, ffn1_norm_w, gw['ffn1_w_gate'], gw['ffn1_w_up'],
                                       gw['ffn1_w_down'], _allgather_chips_plan(late), late)
    gw.update(zip(BIG_LATE, _forward_sibling("gather_late_forward", late_half)))
    glu_full = gw['ssm_glu_w'].reshape(sw, sw)
    n2, q, k, vv, gate, u = _inproj_fwd(h1, mix_norm_w, gw['w_in'], cosf, sinf, rw)
    o, ret, sprev = _ret_fwd(q, k, vv, gate, ret_norm_w, tables)
    u_seg = _to_segments(u, seg_len)
    xr, xi, c0r, c0i, yp, ssm_seg = _s5_fwd(u_seg, bsr_b, bsi_b, csr_b, csi_b, a8r, a8i, al8r, al8i,
                                            ssm_d, glu_full, ssm_glu_b, ssm_norm_w, jb)
    ssm = _from_segments(ssm_seg, seg_len)
    h2 = _outproj_fwd(h1, ret, ssm, gw['w_out'])
    (h3, g2, u2), _ = _ffn_fwd("ffn2_fwd", h2, ffn2_norm_w, gw['ffn2_w_gate'], gw['ffn2_w_up'], gw['ffn2_w_down'])
    loss_part, dh3, d_final = _loss_head(h3, final_norm_w.reshape(1, d), loss_target[0])

    (dh2, d_ffn2_norm, nb, daccb, ab, dgb, dub), _ = _ffn_bwd_act(
        "ffn2_bwd_act", dh3, h2, ffn2_norm_w, g2, u2, gw['ffn2_w_gate'], gw['ffn2_w_up'], gw['ffn2_w_down'])
    dwg2, dwu2, dwd2 = _ffn_bwd_w("ffn2_bwd_w", nb, daccb, ab, dgb, dub)
    dret, dssm, dwo = _outproj_bwd(dh2, ret, ssm, gw['w_out'])
    (du_seg, d_ssm_norm, d_glu_w, d_glu_b, d_ssm_d, dcr_s, dci_s, dbr_s, dbi_s, dar8, dai8) = _s5_bwd(
        _to_segments(dssm, seg_len), u_seg, yp, xr, xi, c0r, c0i, bsrt_b, bsit_b, csrt_b, csit_b,
        a8r, a8i, al8r, al8i, ssm_d, glu_full, ssm_glu_b, ssm_norm_w, jb)
    du = _from_segments(du_seg, seg_len)
    dq, dk, dv, dgate, d_ret_norm = _ret_bwd(dret, q, k, vv, gate, o, sprev, ret_norm_w, tables, cosf, sinf)
    dh1, d_mix_norm, dwin = _inproj_bwd(dh2, h1, mix_norm_w, n2, gw['w_in'], dq, dk, dv, dgate, du)
    late_parts = {
        'w_in': dwin, 'ssm_glu_w': d_glu_w.reshape(N_CHIP, sw // N_CHIP, sw).astype(BF16), 'w_out': dwo,
        'ffn2_w_gate': dwg2, 'ffn2_w_up': dwu2, 'ffn2_w_down': dwd2,
    }
    late_list = [late_parts[n] for n in BIG_LATE]
    (dh0, d_ffn1_norm, nb, daccb, ab, dgb, dub), late_recv = _ffn_bwd_act(
        "ffn1_bwd_act", dh1, h0, ffn1_norm_w, g1, u1, gw['ffn1_w_gate'], gw['ffn1_w_up'], gw['ffn1_w_down'],
        _alltoall_chips_plan(late_list), late_list)
    grad_x = dh0[CHUNK:][None]
    d_meta = dh0[CHUNK - N_META:CHUNK]

    d_c_re = jnp.transpose(_blockdiag_out(tr(dcr_s), groups, SSM_GROUP, SSM_STATE), (0, 1, 2))
    d_c_im = -_blockdiag_out(tr(dci_s), groups, SSM_GROUP, SSM_STATE)
    d_bbr = jnp.transpose(_blockdiag_out(dbr_s, groups, SSM_GROUP, SSM_STATE), (0, 2, 1))
    d_bbi = jnp.transpose(_blockdiag_out(dbi_s, groups, SSM_GROUP, SSM_STATE), (0, 2, 1))
    d_ar = jnp.sum(dar8, axis=0).reshape(groups, SSM_STATE)
    d_ai = jnp.sum(dai8, axis=0).reshape(groups, SSM_STATE)
    small_parts = [loss_part[0:1, :], d_meta, d_ffn1_norm, d_mix_norm, d_ret_norm, d_ar, d_ai, d_bbr, d_bbi,
                   d_c_re, d_c_im, d_ssm_d, d_glu_b, d_ssm_norm, d_ffn2_norm, d_final]
    small_shapes = [a.shape for a in small_parts]
    packed = _pack(small_parts)
    early_recv, (all_parts,) = _ffn_bwd_w_scatter("ffn1_bwd_w", nb, daccb, ab, dgb, dub, chip,
                                                  _allgather_all_plan([packed]), [packed])
    received = dict(zip(BIG_LATE + BIG_EARLY, late_recv + early_recv))
    chip_sums = [_sum_slots("sum_chips_" + n, received[n], BF16) for n in BIG]
    sib_sums = _swap_sibling("swap_sibling", chip_sums)
    (loss_row, g_meta_full, g_ffn1_norm, g_mix_norm, g_ret_norm, g_ar, g_ai, g_bbr, g_bbi, g_c_re, g_c_im,
     g_ssm_d, g_glu_b, g_ssm_norm, g_ffn2_norm, g_final) = _unpack(_sum_slots("sum_small", all_parts, F32), small_shapes)
    g_lam_re, g_lam_im, g_log_dt, g_b_re, g_b_im = prep_vjp((g_ar, g_ai, g_bbr, g_bbi))
    loss = loss_row[0, 0]
    g_meta = lax.dynamic_slice(g_meta_full, (0, chip * (d // N_CHIP)), (N_META, d // N_CHIP))
    small_grads = {
        'meta_tokens': g_meta, 'ffn1_norm_w': g_ffn1_norm, 'mix_norm_w': g_mix_norm, 'ret_norm_w': g_ret_norm,
        'ssm_lambda_re': g_lam_re[None], 'ssm_lambda_im': g_lam_im[None], 'ssm_log_dt': g_log_dt[None],
        'ssm_b_re': g_b_re[None], 'ssm_b_im': g_b_im[None], 'ssm_c_re': g_c_re[None], 'ssm_c_im': g_c_im[None],
        'ssm_d': g_ssm_d, 'ssm_glu_b': g_glu_b, 'ssm_norm_w': g_ssm_norm, 'ffn2_norm_w': g_ffn2_norm,
        'final_norm_w': g_final.reshape(d),
    }

    grads, deltas, new_m, new_v = {}, {}, {}, {}
    for n, mine, sib in zip(BIG, chip_sums, sib_sums):
        if n in TRANSPOSED:
            outs = _adam("adam_" + n, as_fd(w[n]), as_fd(m[n]), as_fd(v[n]), [mine, sib])
            grads[n], deltas[n], new_m[n], new_v[n] = [as_fd(t) for t in outs]
        else:
            grads[n], deltas[n], new_m[n], new_v[n] = _adam("adam_" + n, w[n], m[n], v[n], [mine, sib])
    sm_shapes = [w[n].shape for n in SMALL]
    sm_out = _adam("adam_small", _pack([w[n] for n in SMALL]), _pack([m[n] for n in SMALL]),
                   _pack([v[n] for n in SMALL]), [_pack([small_grads[n].reshape(w[n].shape) for n in SMALL])])
    for dst, packed in zip((grads, deltas, new_m, new_v), sm_out):
        for n, t in zip(SMALL, _unpack(packed, sm_shapes)):
            dst[n] = t

    return (loss, grad_x, *[grads[n] for n in WEIGHT_NAMES], *[deltas[n] for n in WEIGHT_NAMES],
            *[new_m[n] for n in WEIGHT_NAMES], *[new_v[n] for n in WEIGHT_NAMES])
```

```python
import functools
import math

import jax
import jax.numpy as jnp
from jax import lax
from jax.experimental import pallas as pl
from jax.experimental.pallas import tpu as pltpu

N_META = 16
RET_HEADS = 4
HEAD_DIM = 128
SSM_GROUP = 16
SSM_STATE = 64
CHUNK = 128
ROPE_BASE = 10000.0
EPS = 1e-6
FFN_RES = 0.5
N_SEG = 8
N_SEC = 4
N_CHIP = 4
LANE = 128
FFN_CPS = 2

ADAM_LR = 0.001
ADAM_B1 = 0.9
ADAM_B2 = 0.999
ADAM_EPS = 1e-08
ADAM_WD = 0.01
ADAM_STEP = 10

VMEM_LIMIT = 56 * 1024 * 1024

F32 = jnp.float32
BF16 = jnp.bfloat16
MESH = pl.DeviceIdType.MESH


def _dot(a, b):
    return jnp.dot(a, b, preferred_element_type=F32)


def _dot_nt(a, b):
    return lax.dot_general(a, b, (((1,), (1,)), ((), ())), preferred_element_type=F32)


def _dot_tn(a, b):
    return lax.dot_general(a, b, (((0,), (0,)), ((), ())), preferred_element_type=F32)


def _tile(n, target, mult=64):
    best = None
    t = mult
    while t <= min(n, target):
        if n % t == 0:
            best = t
        t += mult
    assert best is not None, (n, target)
    return best


def _params(sem, vmem=VMEM_LIMIT):
    return pltpu.CompilerParams(dimension_semantics=sem, vmem_limit_bytes=vmem)


def _rms_stats(xf):
    r = lax.rsqrt(jnp.mean(xf * xf, axis=-1, keepdims=True) + EPS)
    return xf * r, r


def _rms_bwd(dy, xh, r, w):
    dxh = dy * w
    return r * (dxh - xh * jnp.mean(dxh * xh, axis=-1, keepdims=True))


def _sigmoid(x):
    return 1.0 / (1.0 + jnp.exp(-x))


GELU_K0 = math.sqrt(2.0 / math.pi)
GELU_K1 = 0.044715


CHIP_MASKS = [(1, 0, 0), (0, 1, 0), (1, 1, 0)]
ALL_MASKS = [(0, 0, 1), (0, 1, 0), (0, 1, 1), (1, 0, 0), (1, 0, 1), (1, 1, 0), (1, 1, 1)]
SIB_MASKS = [(0, 0, 1)]
ANY_SPEC = pl.BlockSpec(memory_space=pl.ANY)


class _Plan:
    def __init__(self, arrays, masks, n_slots, src_slotted, dst_slotted, local_copy, half=False, forward=False):
        self.shapes = [(a.shape, a.dtype) for a in arrays]
        self.n = len(arrays)
        self.masks = masks
        self.n_slots = n_slots
        self.src_slotted, self.dst_slotted, self.local_copy = src_slotted, dst_slotted, local_copy
        self.half, self.forward = half, forward
        self.n_cp = self.n * len(masks) * (len(CHIP_MASKS) if forward else 1)

    def out_shape(self):
        out = []
        for shp, dt in self.shapes:
            if self.dst_slotted and not self.src_slotted:
                shp = (self.n_slots,) + shp
            elif self.src_slotted and not self.dst_slotted:
                shp = shp[1:]
            out.append(jax.ShapeDtypeStruct(shp, dt))
        return tuple(out)

    def scratch(self):
        return [pltpu.SemaphoreType.DMA((self.n_cp,)), pltpu.SemaphoreType.DMA((self.n_cp,)),
                pltpu.SemaphoreType.DMA((self.n,))]

    def _slot(self, px, py, pc):
        if self.n_slots == 8:
            return 4 * px + 2 * py + pc
        if self.n_slots == 4:
            return 2 * px + py
        return pc

    def copies(self, ins, outs, sems):
        send_sems, recv_sems, loc_sems = sems
        x, y, c = lax.axis_index("x"), lax.axis_index("y"), lax.axis_index("c")
        me = self._slot(x, y, c)
        n_m = len(self.masks)
        cps = []
        for a in range(self.n):
            if self.forward:
                rows = self.shapes[a][0][-2] // 2
                mine = pl.ds(pl.multiple_of(c * rows, 8), rows)
                for j, (mx, my, _) in enumerate(CHIP_MASKS):
                    blk = outs[a].at[2 * (1 - x if mx else x) + (1 - y if my else y), mine]
                    k = a * len(CHIP_MASKS) + j
                    cps.append(pltpu.make_async_remote_copy(
                        src_ref=blk, dst_ref=blk, send_sem=send_sems.at[k], recv_sem=recv_sems.at[k],
                        device_id=(x, y, 1 - c), device_id_type=MESH))
                continue
            if self.local_copy:
                src = ins[a].at[me] if self.src_slotted else ins[a]
                cps.append(pltpu.make_async_copy(src, outs[a].at[me], loc_sems.at[a]))
            for mi, (mx, my, mc) in enumerate(self.masks):
                px = 1 - x if mx else x
                py = 1 - y if my else y
                pc = 1 - c if mc else c
                src = ins[a].at[self._slot(px, py, pc)] if self.src_slotted else ins[a]
                dst = outs[a].at[me] if self.dst_slotted else outs[a]
                if self.half:
                    rows = src.shape[-2] // 2
                    mine = pl.ds(pl.multiple_of(c * rows, 8), rows)
                    src, dst = src.at[mine], dst.at[mine]
                k = a * n_m + mi
                cps.append(pltpu.make_async_remote_copy(
                    src_ref=src, dst_ref=dst, send_sem=send_sems.at[k], recv_sem=recv_sems.at[k],
                    device_id=(px, py, pc), device_id_type=MESH))
        return cps


def _exchange(name, plan, arrays):
    n = plan.n

    def body(*refs):
        cps = plan.copies(refs[:n], refs[n:2 * n], refs[2 * n:])
        for cp in cps:
            cp.start()
        for cp in cps:
            cp.wait()

    outs = pl.pallas_call(
        body, name=name, out_shape=plan.out_shape(),
        in_specs=[ANY_SPEC] * n, out_specs=tuple([ANY_SPEC] * n), scratch_shapes=plan.scratch(),
        input_output_aliases={i: i for i in range(n)} if plan.forward else {},
    )(*arrays)
    return list(outs)


def _pcall(body, *, name, grid, in_specs, out_specs, out_shape, scratch_shapes, args, plan=None, plan_args=()):
    sem = ("arbitrary",) * len(grid)
    if plan is None:
        return pl.pallas_call(body, name=name, grid=grid, in_specs=in_specs, out_specs=out_specs,
                              out_shape=out_shape, scratch_shapes=scratch_shapes,
                              compiler_params=_params(sem))(*args), []
    n_in, n_out, n_scr, n_p = len(in_specs), len(out_specs), len(scratch_shapes), plan.n

    def wrapped(*refs):
        ins = refs[:n_in]
        p_ins = refs[n_in:n_in + n_p]
        o0 = n_in + n_p
        outs = refs[o0:o0 + n_out]
        p_outs = refs[o0 + n_out:o0 + n_out + n_p]
        s0 = o0 + n_out + n_p
        scr = refs[s0:s0 + n_scr]
        sems = refs[s0 + n_scr:]
        ids = [pl.program_id(i) for i in range(len(grid))]
        first = functools.reduce(jnp.logical_and, [i == 0 for i in ids])
        last = functools.reduce(jnp.logical_and, [i == g - 1 for i, g in zip(ids, grid)])

        @pl.when(first)
        def _():
            for cp in plan.copies(p_ins, p_outs, sems):
                cp.start()

        body(*ins, *outs, *scr)

        @pl.when(last)
        def _():
            for cp in plan.copies(p_ins, p_outs, sems):
                cp.wait()

    res = pl.pallas_call(
        wrapped, name=name, grid=grid,
        in_specs=list(in_specs) + [ANY_SPEC] * n_p,
        out_specs=tuple(out_specs) + (ANY_SPEC,) * n_p,
        out_shape=tuple(out_shape) + plan.out_shape(),
        scratch_shapes=list(scratch_shapes) + plan.scratch(),
        compiler_params=_params(sem),
    )(*args, *plan_args)
    return res[:n_out], list(res[n_out:])


def _allgather_chips_plan(arrays):
    return _Plan(arrays, CHIP_MASKS, 4, False, True, True, half=True)


def _forward_sibling(name, gathered):
    return _exchange(name, _Plan(gathered, SIB_MASKS, 4, True, True, False, forward=True), gathered)


def _alltoall_chips_plan(arrays):
    return _Plan(arrays, CHIP_MASKS, 4, True, True, True)


def _swap_sibling(name, arrays):
    return _exchange(name, _Plan(arrays, SIB_MASKS, 2, False, False, False), arrays)


def _allgather_all_plan(arrays):
    return _Plan(arrays, ALL_MASKS, 8, False, True, True)


def _sum_slots(name, a, out_dtype):
    s, r, c = a.shape
    tr = _tile(r, 512, 8)

    def body(a_ref, o_ref):
        acc = a_ref[0].astype(F32)
        for i in range(1, s):
            acc = acc + a_ref[i].astype(F32)
        o_ref[...] = acc.astype(out_dtype)

    return pl.pallas_call(
        body, name=name, grid=(r // tr,),
        in_specs=[pl.BlockSpec((s, tr, c), lambda i: (0, i, 0))],
        out_specs=pl.BlockSpec((tr, c), lambda i: (i, 0)),
        out_shape=jax.ShapeDtypeStruct((r, c), out_dtype),
        compiler_params=_params(("arbitrary",)),
    )(a)


def _adam_math(w, g, m, v):
    m_new = ADAM_B1 * m + (1.0 - ADAM_B1) * g
    v_new = ADAM_B2 * v + (1.0 - ADAM_B2) * (g * g)
    m_hat = m_new / (1.0 - ADAM_B1 ** ADAM_STEP)
    v_hat = v_new / (1.0 - ADAM_B2 ** ADAM_STEP)
    delta = -ADAM_LR * (m_hat / (jnp.sqrt(v_hat) + ADAM_EPS) + ADAM_WD * w)
    return delta, m_new, v_new


def _adam(name, w, m, v, g_parts):
    r, c = w.shape[-2:]
    tr = _tile(r, 256, 8)
    n_g = len(g_parts)
    lead = w.ndim == 3
    at = (lambda ref: ref.at[0]) if lead else (lambda ref: ref)

    def body(*refs):
        w_ref, m_ref, v_ref = [at(t) for t in refs[:3]]
        g_refs = refs[3:3 + n_g]
        g_out, d_out, m_out, v_out = [at(t) for t in refs[3 + n_g:]]
        g = g_refs[0][...].astype(F32)
        for gr in g_refs[1:]:
            g = g + gr[...].astype(F32)
        delta, m_new, v_new = _adam_math(w_ref[...], g, m_ref[...], v_ref[...])
        g_out[...] = g
        d_out[...] = delta
        m_out[...] = m_new
        v_out[...] = v_new

    spec = pl.BlockSpec((tr, c), lambda i: (i, 0))
    wspec = pl.BlockSpec((1, tr, c), lambda i: (0, i, 0)) if lead else spec
    shp = jax.ShapeDtypeStruct(w.shape, F32)
    return pl.pallas_call(
        body, name=name, grid=(r // tr,),
        in_specs=[wspec] * 3 + [spec] * n_g, out_specs=(wspec,) * 4, out_shape=(shp,) * 4,
        compiler_params=_params(("arbitrary",)),
    )(w, m, v, *g_parts)


def _ffn_fwd(name, h, nw, wg, wu, wd, plan=None, plan_args=()):
    lp, d = h.shape
    nck, f, _ = wg.shape
    tm = _tile(lp, 640)
    last = nck // FFN_CPS - 1

    def body(h_ref, nw_ref, wg_ref, wu_ref, wd_ref, ho_ref, g_ref, u_ref, n_sc, acc_sc):
        k = pl.program_id(1)

        @pl.when(k == 0)
        def _():
            xh, _ = _rms_stats(h_ref[...])
            n_sc[...] = (xh * nw_ref[...]).astype(BF16)
            acc_sc[...] = jnp.zeros_like(acc_sc)

        n = n_sc[...]
        acc = acc_sc[...]
        for c in range(FFN_CPS):
            g = _dot_nt(n, wg_ref[c])
            u = _dot_nt(n, wu_ref[c])
            g_ref[c] = g.astype(BF16)
            u_ref[c] = u.astype(BF16)
            a = (g * _sigmoid(g) * u).astype(BF16)
            acc = acc + _dot(a, wd_ref[c])
        acc_sc[...] = acc

        @pl.when(k == last)
        def _():
            ho_ref[...] = h_ref[...] + FFN_RES * acc_sc[...]

    return _pcall(
        body, name=name, grid=(lp // tm, nck // FFN_CPS), plan=plan, plan_args=plan_args, args=(h, nw, wg, wu, wd),
        in_specs=[pl.BlockSpec((tm, d), lambda i, k: (i, 0)),
                  pl.BlockSpec((1, d), lambda i, k: (0, 0)),
                  pl.BlockSpec((FFN_CPS, f, d), lambda i, k: (k, 0, 0)),
                  pl.BlockSpec((FFN_CPS, f, d), lambda i, k: (k, 0, 0)),
                  pl.BlockSpec((FFN_CPS, f, d), lambda i, k: (k, 0, 0))],
        out_specs=(pl.BlockSpec((tm, d), lambda i, k: (i, 0)),
                   pl.BlockSpec((FFN_CPS, tm, f), lambda i, k: (k, i, 0)),
                   pl.BlockSpec((FFN_CPS, tm, f), lambda i, k: (k, i, 0))),
        out_shape=(jax.ShapeDtypeStruct((lp, d), F32),
                   jax.ShapeDtypeStruct((nck, lp, f), BF16),
                   jax.ShapeDtypeStruct((nck, lp, f), BF16)),
        scratch_shapes=[pltpu.VMEM((tm, d), BF16), pltpu.VMEM((tm, d), F32)])


def _ffn_bwd_act(name, dh, h, nw, g, u, wg, wu, wd, plan=None, plan_args=()):
    lp, d = h.shape
    nck, f, _ = wg.shape
    tm = _tile(lp, 320)
    last = nck // FFN_CPS - 1

    def body(dh_ref, h_ref, nw_ref, g_ref, u_ref, wg_ref, wu_ref, wd_ref,
             dhi_ref, dnw_ref, n_ref, dacc_ref, a_ref, dg_ref, du_ref,
             xh_sc, r_sc, dn_sc):
        i = pl.program_id(0)
        k = pl.program_id(1)

        @pl.when(k == 0)
        def _():
            xh, r = _rms_stats(h_ref[...])
            xh_sc[...] = xh
            r_sc[...] = r
            n_ref[...] = (xh * nw_ref[...]).astype(BF16)
            dacc_ref[...] = (FFN_RES * dh_ref[...]).astype(BF16)
            dn_sc[...] = jnp.zeros_like(dn_sc)

        @pl.when(jnp.logical_and(i == 0, k == 0))
        def _():
            dnw_ref[...] = jnp.zeros_like(dnw_ref)

        dacc = dacc_ref[...]
        dn = dn_sc[...]
        for c in range(FFN_CPS):
            gv = g_ref[c].astype(F32)
            uv = u_ref[c].astype(F32)
            sg = _sigmoid(gv)
            sil = gv * sg
            da = _dot_nt(dacc, wd_ref[c])
            dgk = (da * uv * (sg * (1.0 + gv * (1.0 - sg)))).astype(BF16)
            duk = (da * sil).astype(BF16)
            a_ref[c] = (sil * uv).astype(BF16)
            dg_ref[c] = dgk
            du_ref[c] = duk
            dn = dn + _dot(dgk, wg_ref[c]) + _dot(duk, wu_ref[c])
        dn_sc[...] = dn

        @pl.when(k == last)
        def _():
            dn = dn_sc[...]
            xh = xh_sc[...]
            dhi_ref[...] = dh_ref[...] + _rms_bwd(dn, xh, r_sc[...], nw_ref[...])
            dnw_ref[...] += jnp.sum(dn * xh, axis=0, keepdims=True)

    row = pl.BlockSpec((tm, d), lambda i, k: (i, 0))
    vec = pl.BlockSpec((1, d), lambda i, k: (0, 0))
    hid = pl.BlockSpec((FFN_CPS, tm, f), lambda i, k: (k, i, 0))
    w_fd = pl.BlockSpec((FFN_CPS, f, d), lambda i, k: (k, 0, 0))
    return _pcall(
        body, name=name, grid=(lp // tm, nck // FFN_CPS), plan=plan, plan_args=plan_args,
        args=(dh, h, nw, g, u, wg, wu, wd),
        in_specs=[row, row, vec, hid, hid, w_fd, w_fd, w_fd],
        out_specs=(row, vec, row, row, hid, hid, hid),
        out_shape=(jax.ShapeDtypeStruct((lp, d), F32),
                   jax.ShapeDtypeStruct((1, d), F32),
                   jax.ShapeDtypeStruct((lp, d), BF16),
                   jax.ShapeDtypeStruct((lp, d), BF16),
                   jax.ShapeDtypeStruct((nck, lp, f), BF16),
                   jax.ShapeDtypeStruct((nck, lp, f), BF16),
                   jax.ShapeDtypeStruct((nck, lp, f), BF16)),
        scratch_shapes=[pltpu.VMEM((tm, d), F32), pltpu.VMEM((tm, 1), F32), pltpu.VMEM((tm, d), F32)])


def _ffn_bwd_w(name, n, dacc, a, dg, du):
    lp, d = n.shape
    nck, _, f = a.shape
    tm = _tile(lp, 640)
    last = lp // tm - 1

    def body(n_ref, dacc_ref, a_ref, dg_ref, du_ref, dwg_ref, dwu_ref, dwd_ref, ag_sc, au_sc, ad_sc):
        i = pl.program_id(1)

        @pl.when(i == 0)
        def _():
            ag_sc[...] = jnp.zeros_like(ag_sc)
            au_sc[...] = jnp.zeros_like(au_sc)
            ad_sc[...] = jnp.zeros_like(ad_sc)

        nv = n_ref[...]
        ag_sc[...] += _dot_tn(dg_ref[0], nv)
        au_sc[...] += _dot_tn(du_ref[0], nv)
        ad_sc[...] += _dot_tn(a_ref[0], dacc_ref[...])

        @pl.when(i == last)
        def _():
            dwg_ref[0] = ag_sc[...].astype(BF16)
            dwu_ref[0] = au_sc[...].astype(BF16)
            dwd_ref[0] = ad_sc[...].astype(BF16)

    row = pl.BlockSpec((tm, d), lambda k, i: (i, 0))
    hid = pl.BlockSpec((1, tm, f), lambda k, i: (k, i, 0))
    w_fd = pl.BlockSpec((1, f, d), lambda k, i: (k, 0, 0))
    wshape = jax.ShapeDtypeStruct((nck, f, d), BF16)
    return pl.pallas_call(
        body, name=name, grid=(nck, lp // tm),
        in_specs=[row, row, hid, hid, hid], out_specs=(w_fd, w_fd, w_fd), out_shape=(wshape,) * 3,
        scratch_shapes=[pltpu.VMEM((f, d), F32)] * 3,
        compiler_params=_params(("arbitrary", "arbitrary")),
    )(n, dacc, a, dg, du)


def _ffn_bwd_w_scatter(name, n, dacc, a, dg, du, chip, plan, plan_args):
    lp, d = n.shape
    nck, _, f = a.shape
    tm = _tile(lp, 640)
    last_i = lp // tm - 1
    n_w = 3
    n_p = plan.n

    def body(me_ref, n_ref, dacc_ref, a_ref, dg_ref, du_ref, *rest):
        p_ins = rest[:n_p]
        recv = rest[n_p:n_p + n_w]
        p_outs = rest[n_p + n_w:2 * n_p + n_w]
        acc = rest[2 * n_p + n_w:2 * n_p + 2 * n_w]
        stage, send_sems, recv_sems, loc_sems = rest[2 * n_p + 2 * n_w:2 * n_p + 2 * n_w + 4]
        p_sems = rest[2 * n_p + 2 * n_w + 4:]
        p = pl.program_id(0)
        i = pl.program_id(1)
        me = me_ref[0]
        c = lax.axis_index("c")

        def send(w, pos):
            kk = lax.rem(me + 1 + pos, nck)
            diff = jnp.bitwise_xor(kk, me)
            m = jnp.where(diff == 2, 0, jnp.where(diff == 1, 1, 2))
            return pltpu.make_async_remote_copy(
                src_ref=stage.at[lax.rem(pos, 2), w], dst_ref=recv[w].at[me],
                send_sem=send_sems.at[w * 3 + m], recv_sem=recv_sems.at[w * 3 + m],
                device_id=(lax.div(kk, 2), lax.rem(kk, 2), c), device_id_type=MESH)

        @pl.when(jnp.logical_and(p == 0, i == 0))
        def _():
            for cp in plan.copies(p_ins, p_outs, p_sems):
                cp.start()

        @pl.when(i == 0)
        def _():
            for t in acc:
                t[...] = jnp.zeros_like(t)

        nv = n_ref[...]
        acc[0][...] += _dot_tn(dg_ref[0], nv)
        acc[1][...] += _dot_tn(du_ref[0], nv)
        acc[2][...] += _dot_tn(a_ref[0], dacc_ref[...])

        @pl.when(jnp.logical_and(i == last_i, p >= 2))
        def _():
            for w in range(n_w):
                send(w, p - 2).wait_send()

        @pl.when(i == last_i)
        def _():
            for w in range(n_w):
                stage[lax.rem(p, 2), w] = acc[w][...].astype(BF16)

        @pl.when(jnp.logical_and(i == last_i, p < nck - 1))
        def _():
            for w in range(n_w):
                send(w, p).start()

        @pl.when(jnp.logical_and(i == last_i, p == nck - 1))
        def _():
            own = [pltpu.make_async_copy(stage.at[(nck - 1) % 2, w], recv[w].at[me], loc_sems.at[w])
                   for w in range(n_w)]
            for cp in own:
                cp.start()
            for w in range(n_w):
                send(w, nck - 2).wait_send()
            for cp in own:
                cp.wait()
            for w in range(n_w):
                for m in range(3):
                    pltpu.make_async_remote_copy(
                        src_ref=stage.at[0, w], dst_ref=recv[w].at[me],
                        send_sem=send_sems.at[w * 3 + m], recv_sem=recv_sems.at[w * 3 + m],
                        device_id=(0, 0, c), device_id_type=MESH).wait_recv()
            for cp in plan.copies(p_ins, p_outs, p_sems):
                cp.wait()

    chunk = lambda k, me_ref: lax.rem(me_ref[0] + 1 + k, nck)
    row = pl.BlockSpec((tm, d), lambda k, i, me_ref: (i, 0))
    hid = pl.BlockSpec((1, tm, f), lambda k, i, me_ref: (chunk(k, me_ref), i, 0))
    wshape = jax.ShapeDtypeStruct((nck, f, d), BF16)
    res = pl.pallas_call(
        body, name=name,
        grid_spec=pltpu.PrefetchScalarGridSpec(
            num_scalar_prefetch=1, grid=(nck, lp // tm),
            in_specs=[row, row, hid, hid, hid] + [ANY_SPEC] * n_p,
            out_specs=(ANY_SPEC,) * (n_w + n_p),
            scratch_shapes=[pltpu.VMEM((f, d), F32)] * n_w + [
                pltpu.VMEM((2, n_w, f, d), BF16), pltpu.SemaphoreType.DMA((n_w * 3,)),
                pltpu.SemaphoreType.DMA((n_w * 3,)), pltpu.SemaphoreType.DMA((n_w,))] + plan.scratch()),
        out_shape=(wshape,) * n_w + plan.out_shape(),
        compiler_params=_params(("arbitrary", "arbitrary")),
    )(chip.reshape(1).astype(jnp.int32), n, dacc, a, dg, du, *plan_args)
    return list(res[:n_w]), list(res[n_w:])


def _inproj_fwd(h, nw, w_in, cosf, sinf, rw):
    lp, d = h.shape
    nck, _, ps = w_in.shape
    proj = nck * ps
    sw = proj - 4 * rw
    tm = _tile(lp, 640)
    scale = HEAD_DIM ** -0.5
    heads = rw // HEAD_DIM

    def body(h_ref, nw_ref, w_ref, cos_ref, sin_ref, n_ref, q_ref, k_ref, v_ref, g_ref, u_ref, p_sc):
        xh, _ = _rms_stats(h_ref[...])
        n = (xh * nw_ref[...]).astype(BF16)
        n_ref[...] = n
        for c in range(nck):
            p_sc[:, c * ps:(c + 1) * ps] = _dot(n, w_ref[c])
        cs = cos_ref[...]
        sn = sin_ref[...]
        for hh in range(heads):
            lo = hh * HEAD_DIM
            qh = p_sc[:, lo:lo + HEAD_DIM]
            q_ref[:, lo:lo + HEAD_DIM] = (qh * cs + pltpu.roll(qh, HEAD_DIM // 2, 1) * sn).astype(BF16)
            kh = p_sc[:, rw + lo:rw + lo + HEAD_DIM]
            k_ref[:, lo:lo + HEAD_DIM] = ((kh * cs + pltpu.roll(kh, HEAD_DIM // 2, 1) * sn) * scale).astype(BF16)
        v_ref[...] = p_sc[:, 2 * rw:3 * rw].astype(BF16)
        g_ref[...] = p_sc[:, 3 * rw:4 * rw]
        u_ref[...] = p_sc[:, 4 * rw:]

    row = lambda w: pl.BlockSpec((tm, w), lambda i: (i, 0))
    return pl.pallas_call(
        body, name="inproj_fwd", grid=(lp // tm,),
        in_specs=[row(d), pl.BlockSpec((1, d), lambda i: (0, 0)),
                  pl.BlockSpec((nck, d, ps), lambda i: (0, 0, 0)), row(HEAD_DIM), row(HEAD_DIM)],
        out_specs=(row(d), row(rw), row(rw), row(rw), row(rw), row(sw)),
        out_shape=(jax.ShapeDtypeStruct((lp, d), BF16),
                   jax.ShapeDtypeStruct((lp, rw), BF16),
                   jax.ShapeDtypeStruct((lp, rw), BF16),
                   jax.ShapeDtypeStruct((lp, rw), BF16),
                   jax.ShapeDtypeStruct((lp, rw), F32),
                   jax.ShapeDtypeStruct((lp, sw), F32)),
        scratch_shapes=[pltpu.VMEM((tm, proj), F32)],
        compiler_params=_params(("arbitrary",)),
    )(h, nw, w_in, cosf, sinf)


def _inproj_bwd(dh, h, nw, n, w_in, dq, dk, dv, dg, du):
    lp, d = h.shape
    nck, _, ps = w_in.shape
    rw = dq.shape[1]
    sw = du.shape[1]
    proj = nck * ps
    tm = _tile(lp, 320)
    last = lp // tm - 1

    def body(dh_ref, h_ref, nw_ref, n_ref, w_ref, dq_ref, dk_ref, dv_ref, dg_ref, du_ref,
             dhi_ref, dnw_ref, dw_ref, p_sc, acc_sc):
        i = pl.program_id(0)

        @pl.when(i == 0)
        def _():
            dnw_ref[...] = jnp.zeros_like(dnw_ref)
            acc_sc[...] = jnp.zeros_like(acc_sc)

        p_sc[:, 0:rw] = dq_ref[...]
        p_sc[:, rw:2 * rw] = dk_ref[...]
        p_sc[:, 2 * rw:3 * rw] = dv_ref[...]
        p_sc[:, 3 * rw:4 * rw] = dg_ref[...]
        p_sc[:, 4 * rw:] = du_ref[...]
        nv = n_ref[...]
        dn = jnp.zeros((tm, d), F32)
        for c in range(nck):
            dp = p_sc[:, c * ps:(c + 1) * ps]
            dn = dn + _dot_nt(dp, w_ref[c])
            acc_sc[c] += _dot_tn(nv, dp)
        xh, r = _rms_stats(h_ref[...])
        dhi_ref[...] = dh_ref[...] + _rms_bwd(dn, xh, r, nw_ref[...])
        dnw_ref[...] += jnp.sum(dn * xh, axis=0, keepdims=True)

        @pl.when(i == last)
        def _():
            dw_ref[...] = acc_sc[...].astype(BF16)

    row = lambda w: pl.BlockSpec((tm, w), lambda i: (i, 0))
    vec = pl.BlockSpec((1, d), lambda i: (0, 0))
    wsp = pl.BlockSpec((nck, d, ps), lambda i: (0, 0, 0))
    return pl.pallas_call(
        body, name="inproj_bwd", grid=(lp // tm,),
        in_specs=[row(d), row(d), vec, row(d), wsp, row(rw), row(rw), row(rw), row(rw), row(sw)],
        out_specs=(row(d), vec, wsp),
        out_shape=(jax.ShapeDtypeStruct((lp, d), F32),
                   jax.ShapeDtypeStruct((1, d), F32),
                   jax.ShapeDtypeStruct((nck, d, ps), BF16)),
        scratch_shapes=[pltpu.VMEM((tm, proj), BF16), pltpu.VMEM((nck, d, ps), F32)],
        compiler_params=_params(("arbitrary",)),
    )(dh, h, nw, n, w_in, dq, dk, dv, dg, du)


def _retention_tables():
    h = jnp.arange(RET_HEADS, dtype=F32)
    log_g = jnp.log(1.0 - 2.0 ** (-5.0 - h))
    i = jnp.arange(CHUNK)
    diff = i[:, None] - i[None, :]
    dec = jnp.where(diff[None] >= 0,
                    jnp.exp(log_g[:, None, None] * jnp.maximum(diff, 0)[None].astype(F32)), 0.0)
    pos = jnp.arange(CHUNK, dtype=F32)
    wq = jnp.exp(log_g[:, None] * (pos + 1.0)[None])
    wk = jnp.exp(log_g[:, None] * (CHUNK - 1 - pos)[None])
    gch = jnp.exp(log_g * CHUNK)
    ones = jnp.ones((1, 1, HEAD_DIM), F32)
    return (dec, wq[:, :, None] * ones, wk[:, :, None] * ones,
            gch[:, None, None] * jnp.ones((1, 8, HEAD_DIM), F32))


def _head_norm(o):
    mu = jnp.mean(o, axis=-1, keepdims=True)
    oc = o - mu
    r = lax.rsqrt(jnp.mean(oc * oc, axis=-1, keepdims=True) + EPS)
    return oc * r, r


def _ret_fwd(q, k, v, g, rnw, tables):
    lp, rw = q.shape
    heads = rw // HEAD_DIM
    nch = lp // CHUNK
    dec, wq, wk, gch = tables

    def body(q_ref, k_ref, v_ref, g_ref, w_ref, dec_ref, wq_ref, wk_ref, gch_ref,
             o_ref, ret_ref, sp_ref, s_sc):
        n = pl.program_id(0)

        @pl.when(n == 0)
        def _():
            s_sc[...] = jnp.zeros_like(s_sc)

        for hh in range(heads):
            cs = slice(hh * HEAD_DIM, (hh + 1) * HEAD_DIM)
            qv, kv, vv = q_ref[:, cs], k_ref[:, cs], v_ref[:, cs]
            s_in = s_sc[hh]
            a = _dot_nt(qv, kv) * dec_ref[hh]
            qw = (qv.astype(F32) * wq_ref[hh]).astype(BF16)
            kw = (kv.astype(F32) * wk_ref[hh]).astype(BF16)
            o = _dot(a.astype(BF16), vv) + _dot(qw, s_in.astype(BF16))
            sp_ref[hh, 0] = s_in
            s_sc[hh] = gch_ref[hh, 0:1, :] * s_in + _dot_tn(kw, vv)
            o_ref[:, cs] = o
            xh, _ = _head_norm(o)
            gv = g_ref[:, cs]
            ret_ref[:, cs] = (gv * _sigmoid(gv) * (xh * w_ref[:, cs])).astype(BF16)

    blk = pl.BlockSpec((CHUNK, rw), lambda n: (n, 0))
    tab = pl.BlockSpec((heads, CHUNK, HEAD_DIM), lambda n: (0, 0, 0))
    return pl.pallas_call(
        body, name="retention_fwd", grid=(nch,),
        in_specs=[blk, blk, blk, blk, pl.BlockSpec((1, rw), lambda n: (0, 0)),
                  tab, tab, tab, pl.BlockSpec((heads, 8, HEAD_DIM), lambda n: (0, 0, 0))],
        out_specs=(blk, blk, pl.BlockSpec((heads, 1, HEAD_DIM, HEAD_DIM), lambda n: (0, n, 0, 0))),
        out_shape=(jax.ShapeDtypeStruct((lp, rw), F32),
                   jax.ShapeDtypeStruct((lp, rw), BF16),
                   jax.ShapeDtypeStruct((heads, nch, HEAD_DIM, HEAD_DIM), F32)),
        scratch_shapes=[pltpu.VMEM((heads, HEAD_DIM, HEAD_DIM), F32)],
        compiler_params=_params(("arbitrary",)),
    )(q, k, v, g, rnw, dec, wq, wk, gch)


def _ret_bwd(dret, q, k, v, g, o, sprev, rnw, tables, cosf, sinf):
    lp, rw = q.shape
    heads = rw // HEAD_DIM
    nch = lp // CHUNK
    dec, wq, wk, gch = tables
    scale = HEAD_DIM ** -0.5
    half = HEAD_DIM // 2

    def body(dret_ref, q_ref, k_ref, v_ref, g_ref, o_ref, sp_ref, w_ref, dec_ref, wq_ref, wk_ref, gch_ref,
             cos_ref, sin_ref, dq_ref, dk_ref, dv_ref, dg_ref, dw_ref, ds_sc):
        n = pl.program_id(0)

        @pl.when(n == 0)
        def _():
            ds_sc[...] = jnp.zeros_like(ds_sc)
            dw_ref[...] = jnp.zeros_like(dw_ref)

        cosv = cos_ref[...]
        sinv = sin_ref[...]
        for hh in range(heads):
            cs = slice(hh * HEAD_DIM, (hh + 1) * HEAD_DIM)
            qv, kv, vv = q_ref[:, cs], k_ref[:, cs], v_ref[:, cs]
            gv = g_ref[:, cs]
            dr = dret_ref[:, cs]
            w = w_ref[:, cs]
            sg = _sigmoid(gv)
            sil = gv * sg
            xh, r = _head_norm(o_ref[:, cs])
            dg_ref[:, cs] = (dr * (xh * w) * (sg * (1.0 + gv * (1.0 - sg)))).astype(BF16)
            dyw = dr * sil
            dw_ref[:, cs] += jnp.sum(dyw * xh, axis=0, keepdims=True)
            dxh = dyw * w
            do = r * (dxh - jnp.mean(dxh, axis=-1, keepdims=True)
                      - xh * jnp.mean(dxh * xh, axis=-1, keepdims=True))
            dob = do.astype(BF16)
            dmask = dec_ref[hh]
            wqv = wq_ref[hh]
            wkv = wk_ref[hh]
            a = (_dot_nt(qv, kv) * dmask).astype(BF16)
            da = (_dot_nt(dob, vv) * dmask).astype(BF16)
            qw = (qv.astype(F32) * wqv).astype(BF16)
            kw = (kv.astype(F32) * wkv).astype(BF16)
            s_in = sp_ref[hh, 0].astype(BF16)
            ds = ds_sc[hh]
            dsb = ds.astype(BF16)
            dq = _dot(da, kv) + _dot_nt(dob, s_in) * wqv
            dk = _dot_tn(da, qv) + _dot_nt(vv, dsb) * wkv
            dv = _dot_tn(a, dob) + _dot(kw, dsb)
            ds_sc[hh] = gch_ref[hh, 0:1, :] * ds + _dot_tn(qw, dob)
            dq_ref[:, cs] = (dq * cosv + pltpu.roll(dq * sinv, half, 1)).astype(BF16)
            dk_ref[:, cs] = ((dk * cosv + pltpu.roll(dk * sinv, half, 1)) * scale).astype(BF16)
            dv_ref[:, cs] = dv.astype(BF16)

    blk = pl.BlockSpec((CHUNK, rw), lambda n: (nch - 1 - n, 0))
    tab = pl.BlockSpec((heads, CHUNK, HEAD_DIM), lambda n: (0, 0, 0))
    wsp = pl.BlockSpec((1, rw), lambda n: (0, 0))
    pos = pl.BlockSpec((CHUNK, HEAD_DIM), lambda n: (nch - 1 - n, 0))
    bshape = jax.ShapeDtypeStruct((lp, rw), BF16)
    return pl.pallas_call(
        body, name="retention_bwd", grid=(nch,),
        in_specs=[blk, blk, blk, blk, blk, blk,
                  pl.BlockSpec((heads, 1, HEAD_DIM, HEAD_DIM), lambda n: (0, nch - 1 - n, 0, 0)),
                  wsp, tab, tab, tab, pl.BlockSpec((heads, 8, HEAD_DIM), lambda n: (0, 0, 0)), pos, pos],
        out_specs=(blk, blk, blk, blk, wsp),
        out_shape=(bshape, bshape, bshape, bshape, jax.ShapeDtypeStruct((1, rw), F32)),
        scratch_shapes=[pltpu.VMEM((heads, HEAD_DIM, HEAD_DIM), F32)],
        compiler_params=_params(("arbitrary",)),
    )(dret, q, k, v, g, o, sprev, rnw, dec, wq, wk, gch, cosf, sinf)


SCAN_CW = 512


def _s5_prepare(lam_re, lam_im, log_dt, b_re, b_im):
    dt = jnp.exp(log_dt)[:, None]
    er = jnp.exp(lam_re * dt)
    ar = er * jnp.cos(lam_im * dt)
    ai = er * jnp.sin(lam_im * dt)
    den = lam_re * lam_re + lam_im * lam_im
    fr = ((ar - 1.0) * lam_re + ai * lam_im) / den
    fi = (ai * lam_re - (ar - 1.0) * lam_im) / den
    bbr = fr[..., None] * b_re - fi[..., None] * b_im
    bbi = fr[..., None] * b_im + fi[..., None] * b_re
    return ar, ai, bbr, bbi


def _blockdiag_in(t):
    g, p, n = t.shape
    gs = g // N_SEC
    t = t.reshape(N_SEC, gs, p, n)
    eye = jnp.eye(gs, dtype=t.dtype)
    return jnp.einsum("sgpn,gh->sgphn", t, eye).reshape(N_SEC, gs * p, gs * n)


def _blockdiag_out(m, g, p, n):
    gs = g // N_SEC
    m = m.reshape(N_SEC, gs, p, gs, n)
    eye = jnp.eye(gs, dtype=m.dtype)
    return jnp.einsum("sgphn,gh->sgpn", m, eye).reshape(g, p, n)


def _scan_step(xr_ref, xi_ref, r0, pr_of, ar_ref, ai_ref, conj, ncols):
    for cc in range(ncols // SCAN_CW):
        cs = pl.ds(cc * SCAN_CW, SCAN_CW)
        pr, pi = pr_of(cs)
        ar = ar_ref[:, cs]
        ai = ai_ref[:, cs]
        if conj:
            nr = ar * pr + ai * pi
            ni = ar * pi - ai * pr
        else:
            nr = ar * pr - ai * pi
            ni = ar * pi + ai * pr
        xr_ref[pl.ds(r0, 8), cs] = xr_ref[pl.ds(r0, 8), cs] + nr
        xi_ref[pl.ds(r0, 8), cs] = xi_ref[pl.ds(r0, 8), cs] + ni


def _shift_rows(z, down):
    row = lax.broadcasted_iota(jnp.int32, z.shape, 0)
    if down:
        return jnp.where(row == 0, 0.0, pltpu.roll(z, 1, 0))
    return jnp.where(row == N_SEG - 1, 0.0, pltpu.roll(z, N_SEG - 1, 0))


def _s5_fwd(u, bsr, bsi, csr, csi, a8r, a8i, al8r, al8i, d, gluw, glub, nw, jb):
    lp, sw = u.shape
    ns = a8r.shape[1]
    rows = N_SEG * jb
    nblk = lp // rows
    secw = sw // N_SEC
    secn = ns // N_SEC

    def local_scan(u_ref, bsr_ref, bsi_ref, ar_ref, ai_ref, xr_ref, xi_ref, pr_sc, pi_sc):
        for s in range(N_SEC):
            ub = u_ref[:, s * secw:(s + 1) * secw].astype(BF16)
            xr_ref[:, s * secn:(s + 1) * secn] = _dot(ub, bsr_ref[s])
            xi_ref[:, s * secn:(s + 1) * secn] = _dot(ub, bsi_ref[s])
        _scan_step(xr_ref, xi_ref, 0, lambda cs: (pr_sc[:, cs], pi_sc[:, cs]), ar_ref, ai_ref, False, ns)

        def step(j, carry):
            r0 = pl.multiple_of(j * 8, 8)
            rp = pl.multiple_of((j - 1) * 8, 8)
            _scan_step(xr_ref, xi_ref, r0,
                       lambda cs: (xr_ref[pl.ds(rp, 8), cs], xi_ref[pl.ds(rp, 8), cs]),
                       ar_ref, ai_ref, False, ns)
            return carry

        lax.fori_loop(1, jb, step, 0)
        pr_sc[...] = xr_ref[rows - 8:rows, :]
        pi_sc[...] = xi_ref[rows - 8:rows, :]

    def carry_body(u_ref, bsr_ref, bsi_ref, ar_ref, ai_ref, alr_ref, ali_ref, cr_ref, ci_ref,
                   xr_sc, xi_sc, pr_sc, pi_sc):
        b = pl.program_id(0)

        @pl.when(b == 0)
        def _():
            pr_sc[...] = jnp.zeros_like(pr_sc)
            pi_sc[...] = jnp.zeros_like(pi_sc)

        local_scan(u_ref, bsr_ref, bsi_ref, ar_ref, ai_ref, xr_sc, xi_sc, pr_sc, pi_sc)

        @pl.when(b == nblk - 1)
        def _():
            er = _shift_rows(pr_sc[...], True)
            ei = _shift_rows(pi_sc[...], True)
            alr, ali = alr_ref[...], ali_ref[...]
            cr, ci = er, ei
            for _ in range(N_SEG - 2):
                sr = _shift_rows(cr, True)
                si = _shift_rows(ci, True)
                cr = er + alr * sr - ali * si
                ci = ei + alr * si + ali * sr
            cr_ref[...] = cr
            ci_ref[...] = ci

    ublk = pl.BlockSpec((rows, sw), lambda b: (b, 0))
    bspec = pl.BlockSpec((N_SEC, secw, secn), lambda b: (0, 0, 0))
    cspec = pl.BlockSpec((N_SEC, secn, secw), lambda b: (0, 0, 0))
    s8 = pl.BlockSpec((N_SEG, ns), lambda b: (0, 0))
    vec = pl.BlockSpec((1, sw), lambda b: (0, 0))
    s8shape = jax.ShapeDtypeStruct((N_SEG, ns), F32)
    c0r, c0i = pl.pallas_call(
        carry_body, name="s5_fwd_carry", grid=(nblk,),
        in_specs=[ublk, bspec, bspec, s8, s8, s8, s8],
        out_specs=(s8, s8), out_shape=(s8shape, s8shape),
        scratch_shapes=[pltpu.VMEM((rows, ns), F32), pltpu.VMEM((rows, ns), F32),
                        pltpu.VMEM((N_SEG, ns), F32), pltpu.VMEM((N_SEG, ns), F32)],
        compiler_params=_params(("arbitrary",)),
    )(u, bsr, bsi, a8r, a8i, al8r, al8i)

    def main_body(u_ref, bsr_ref, bsi_ref, csr_ref, csi_ref, ar_ref, ai_ref, c0r_ref, c0i_ref,
                  d_ref, gw_ref, gb_ref, nw_ref, xr_ref, xi_ref, yp_ref, out_ref, pr_sc, pi_sc):
        b = pl.program_id(0)

        @pl.when(b == 0)
        def _():
            pr_sc[...] = c0r_ref[...]
            pi_sc[...] = c0i_ref[...]

        local_scan(u_ref, bsr_ref, bsi_ref, ar_ref, ai_ref, xr_ref, xi_ref, pr_sc, pi_sc)
        for s in range(N_SEC):
            xs = pl.ds(s * secn, secn)
            us = pl.ds(s * secw, secw)
            y = _dot(xr_ref[:, xs].astype(BF16), csr_ref[s]) + _dot(xi_ref[:, xs].astype(BF16), csi_ref[s])
            yp_ref[:, us] = y + d_ref[:, us] * u_ref[:, us]
        yp = yp_ref[...]
        t = jnp.tanh(GELU_K0 * (yp + GELU_K1 * yp * yp * yp))
        y1 = 0.5 * yp * (1.0 + t)
        z = _dot(y1.astype(BF16), gw_ref[...]) + gb_ref[...]
        y2 = y1 * _sigmoid(z)
        xh, _ = _rms_stats(y2)
        out_ref[...] = (xh * nw_ref[...]).astype(BF16)

    xblk = pl.BlockSpec((rows, ns), lambda b: (b, 0))
    xr, xi, yp, out = pl.pallas_call(
        main_body, name="s5_fwd", grid=(nblk,),
        in_specs=[ublk, bspec, bspec, cspec, cspec, s8, s8, s8, s8, vec,
                  pl.BlockSpec((sw, sw), lambda b: (0, 0)), vec, vec],
        out_specs=(xblk, xblk, ublk, ublk),
        out_shape=(jax.ShapeDtypeStruct((lp, ns), F32), jax.ShapeDtypeStruct((lp, ns), F32),
                   jax.ShapeDtypeStruct((lp, sw), F32), jax.ShapeDtypeStruct((lp, sw), BF16)),
        scratch_shapes=[pltpu.VMEM((N_SEG, ns), F32), pltpu.VMEM((N_SEG, ns), F32)],
        compiler_params=_params(("arbitrary",)),
    )(u, bsr, bsi, csr, csi, a8r, a8i, c0r, c0i, d, gluw, glub, nw)
    return xr, xi, c0r, c0i, yp, out


def _s5_bwd(dout, u, yp, xr, xi, c0r, c0i, bsrt, bsit, csrt, csit, a8r, a8i, al8r, al8i, d, gluw, glub, nw, jb):
    lp, sw = u.shape
    ns = a8r.shape[1]
    rows = N_SEG * jb
    nblk = lp // rows
    secw = sw // N_SEC
    secn = ns // N_SEC

    def rowwise_bwd(dout_ref, yp_ref, gw_ref, gb_ref, nw_ref):
        ypv = yp_ref[...]
        t = jnp.tanh(GELU_K0 * (ypv + GELU_K1 * ypv * ypv * ypv))
        y1 = 0.5 * ypv * (1.0 + t)
        dgelu = 0.5 * (1.0 + t) + 0.5 * ypv * (1.0 - t * t) * GELU_K0 * (1.0 + 3.0 * GELU_K1 * ypv * ypv)
        gw = gw_ref[...]
        y1b = y1.astype(BF16)
        sg = _sigmoid(_dot(y1b, gw) + gb_ref[...])
        xh, r = _rms_stats(y1 * sg)
        dov = dout_ref[...]
        dy2 = _rms_bwd(dov, xh, r, nw_ref[...])
        dz = dy2 * y1 * sg * (1.0 - sg)
        dzb = dz.astype(BF16)
        dy1 = dy2 * sg + _dot_nt(dzb, gw)
        return dy1 * dgelu, dov * xh, y1b, dzb, dz

    def lam_scan(dyp_of, csrt_ref, csit_ref, ar_ref, ai_ref, lr_sc, li_sc, nr_sc, ni_sc, extra):
        for s in range(N_SEC):
            db = dyp_of(s)
            lr_sc[:, s * secn:(s + 1) * secn] = _dot(db, csrt_ref[s])
            li_sc[:, s * secn:(s + 1) * secn] = _dot(db, csit_ref[s])
        top = rows - 8
        _scan_step(lr_sc, li_sc, top, lambda cs: (nr_sc[:, cs], ni_sc[:, cs]), ar_ref, ai_ref, True, ns)
        extra(top, pl.ds(top - 8, 8))

        def step(jj, carry):
            r0 = pl.multiple_of((jb - 1 - jj) * 8, 8)
            rn = pl.multiple_of((jb - jj) * 8, 8)
            rp = pl.multiple_of((jb - 2 - jj) * 8, 8)
            _scan_step(lr_sc, li_sc, r0,
                       lambda cs: (lr_sc[pl.ds(rn, 8), cs], li_sc[pl.ds(rn, 8), cs]),
                       ar_ref, ai_ref, True, ns)
            extra(r0, pl.ds(rp, 8))
            return carry

        lax.fori_loop(1, jb - 1, step, 0)
        _scan_step(lr_sc, li_sc, 0, lambda cs: (lr_sc[8:16, cs], li_sc[8:16, cs]), ar_ref, ai_ref, True, ns)
        extra(0, None)
        nr_sc[...] = lr_sc[0:8, :]
        ni_sc[...] = li_sc[0:8, :]

    def carry_body(dout_ref, yp_ref, u_ref, gw_ref, gb_ref, nw_ref, csrt_ref, csit_ref, ar_ref, ai_ref,
                   alr_ref, ali_ref, cr_ref, ci_ref, dyp_ref, dnw_ref, dgw_ref, dgb_ref, dd_ref,
                   lr_sc, li_sc, nr_sc, ni_sc):
        b = pl.program_id(0)

        @pl.when(b == 0)
        def _():
            nr_sc[...] = jnp.zeros_like(nr_sc)
            ni_sc[...] = jnp.zeros_like(ni_sc)
            for ref in (dnw_ref, dgw_ref, dgb_ref, dd_ref):
                ref[...] = jnp.zeros_like(ref)

        dyp, dnw_rows, y1b, dzb, dz = rowwise_bwd(dout_ref, yp_ref, gw_ref, gb_ref, nw_ref)
        dnw_ref[...] += jnp.sum(dnw_rows, axis=0, keepdims=True)
        dgw_ref[...] += _dot_tn(y1b, dzb)
        dgb_ref[...] += jnp.sum(dz, axis=0, keepdims=True)
        dd_ref[...] += jnp.sum(dyp * u_ref[...], axis=0, keepdims=True)
        dyp_ref[...] = dyp.astype(BF16)
        lam_scan(lambda s: dyp_ref[:, s * secw:(s + 1) * secw], csrt_ref, csit_ref, ar_ref, ai_ref,
                 lr_sc, li_sc, nr_sc, ni_sc, lambda r0, prev_rows: None)

        @pl.when(b == nblk - 1)
        def _():
            fr = _shift_rows(nr_sc[...], False)
            fi = _shift_rows(ni_sc[...], False)
            alr, ali = alr_ref[...], ali_ref[...]
            cr, ci = fr, fi
            for _ in range(N_SEG - 2):
                sr = _shift_rows(cr, False)
                si = _shift_rows(ci, False)
                cr = fr + alr * sr + ali * si
                ci = fi + alr * si - ali * sr
            cr_ref[...] = cr
            ci_ref[...] = ci

    rev = lambda b: (nblk - 1 - b, 0)
    ublk = pl.BlockSpec((rows, sw), rev)
    xblk = pl.BlockSpec((rows, ns), rev)
    s8 = pl.BlockSpec((N_SEG, ns), lambda b: (0, 0))
    vec = pl.BlockSpec((1, sw), lambda b: (0, 0))
    gws = pl.BlockSpec((sw, sw), lambda b: (0, 0))
    btspec = pl.BlockSpec((N_SEC, secn, secw), lambda b: (0, 0, 0))
    ctspec = pl.BlockSpec((N_SEC, secw, secn), lambda b: (0, 0, 0))
    s8shape = jax.ShapeDtypeStruct((N_SEG, ns), F32)
    lcr, lci, dyp_all, d_nw, d_gw, d_gb, d_d = pl.pallas_call(
        carry_body, name="s5_bwd_carry", grid=(nblk,),
        in_specs=[ublk, ublk, ublk, gws, vec, vec, ctspec, ctspec, s8, s8, s8, s8],
        out_specs=(s8, s8, ublk, vec, gws, vec, vec),
        out_shape=(s8shape, s8shape, jax.ShapeDtypeStruct((lp, sw), BF16), jax.ShapeDtypeStruct((1, sw), F32),
                   jax.ShapeDtypeStruct((sw, sw), F32), jax.ShapeDtypeStruct((1, sw), F32),
                   jax.ShapeDtypeStruct((1, sw), F32)),
        scratch_shapes=[pltpu.VMEM((rows, ns), F32), pltpu.VMEM((rows, ns), F32),
                        pltpu.VMEM((N_SEG, ns), F32), pltpu.VMEM((N_SEG, ns), F32)],
        compiler_params=_params(("arbitrary",)),
    )(dout, yp, u, gluw, glub, nw, csrt, csit, a8r, a8i, al8r, al8i)

    def main_body(dyp_sc, u_ref, xr_ref, xi_ref, xtr_ref, xti_ref, c0r_ref, c0i_ref, lcr_ref, lci_ref,
                  d_ref, bsrt_ref, bsit_ref, csrt_ref, csit_ref, ar_ref, ai_ref,
                  du_ref, dcr_ref, dci_ref, dbr_ref, dbi_ref, dar_ref, dai_ref,
                  lr_sc, li_sc, nr_sc, ni_sc):
        b = pl.program_id(0)

        @pl.when(b == 0)
        def _():
            nr_sc[...] = lcr_ref[...]
            ni_sc[...] = lci_ref[...]
            for ref in (dcr_ref, dci_ref, dbr_ref, dbi_ref, dar_ref, dai_ref):
                ref[...] = jnp.zeros_like(ref)

        for s in range(N_SEC):
            db = dyp_sc[:, s * secw:(s + 1) * secw]
            xs = pl.ds(s * secn, secn)
            dcr_ref[s] += _dot_tn(xr_ref[:, xs].astype(BF16), db)
            dci_ref[s] += _dot_tn(xi_ref[:, xs].astype(BF16), db)

        first = b == nblk - 1

        def acc_da(r0, prev_rows):
            for cc in range(ns // SCAN_CW):
                cs = pl.ds(cc * SCAN_CW, SCAN_CW)
                lr = lr_sc[pl.ds(r0, 8), cs]
                li = li_sc[pl.ds(r0, 8), cs]
                if prev_rows is None:
                    xpr = jnp.where(first, c0r_ref[:, cs], xtr_ref[:, cs])
                    xpi = jnp.where(first, c0i_ref[:, cs], xti_ref[:, cs])
                else:
                    xpr = xr_ref[prev_rows, cs]
                    xpi = xi_ref[prev_rows, cs]
                dar_ref[:, cs] += lr * xpr + li * xpi
                dai_ref[:, cs] += li * xpr - lr * xpi

        lam_scan(lambda s: dyp_sc[:, s * secw:(s + 1) * secw], csrt_ref, csit_ref, ar_ref, ai_ref,
                 lr_sc, li_sc, nr_sc, ni_sc, acc_da)

        for s in range(N_SEC):
            xs = pl.ds(s * secn, secn)
            us = pl.ds(s * secw, secw)
            lrb = lr_sc[:, xs].astype(BF16)
            lib = li_sc[:, xs].astype(BF16)
            du = _dot(lrb, bsrt_ref[s]) + _dot(lib, bsit_ref[s]) + d_ref[:, us] * dyp_sc[:, us].astype(F32)
            du_ref[:, us] = du.astype(BF16)
            ub = u_ref[:, us].astype(BF16)
            dbr_ref[s] += _dot_tn(ub, lrb)
            dbi_ref[s] += _dot_tn(ub, lib)

    tail = pl.BlockSpec((N_SEG, ns), lambda b: (jnp.maximum((nblk - 1 - b) * jb - 1, 0), 0))
    acc_c = pl.BlockSpec((N_SEC, secn, secw), lambda b: (0, 0, 0))
    acc_b = pl.BlockSpec((N_SEC, secw, secn), lambda b: (0, 0, 0))
    du, dcr, dci, dbr, dbi, dar, dai = pl.pallas_call(
        main_body, name="s5_bwd", grid=(nblk,),
        in_specs=[ublk, ublk, xblk, xblk, tail, tail, s8, s8, s8, s8,
                  vec, btspec, btspec, ctspec, ctspec, s8, s8],
        out_specs=(ublk, acc_c, acc_c, acc_b, acc_b, s8, s8),
        out_shape=(jax.ShapeDtypeStruct((lp, sw), BF16),
                   jax.ShapeDtypeStruct((N_SEC, secn, secw), F32),
                   jax.ShapeDtypeStruct((N_SEC, secn, secw), F32),
                   jax.ShapeDtypeStruct((N_SEC, secw, secn), F32),
                   jax.ShapeDtypeStruct((N_SEC, secw, secn), F32),
                   s8shape, s8shape),
        scratch_shapes=[pltpu.VMEM((rows, ns), F32), pltpu.VMEM((rows, ns), F32),
                        pltpu.VMEM((N_SEG, ns), F32), pltpu.VMEM((N_SEG, ns), F32)],
        compiler_params=_params(("arbitrary",)),
    )(dyp_all, u, xr, xi, xr, xi, c0r, c0i, lcr, lci, d, bsrt, bsit, csrt, csit, a8r, a8i)
    return du, d_nw, d_gw, d_gb, d_d, dcr, dci, dbr, dbi, dar, dai


def _outproj_fwd(h, ret, ssm, wo):
    lp, d = h.shape
    nck, rs, _ = wo.shape
    rw = ret.shape[1]
    tm = _tile(lp, 640)
    per = rw // rs

    def body(h_ref, ret_ref, ssm_ref, w_ref, o_ref):
        acc = h_ref[...]
        for c in range(nck):
            src = ret_ref if c < per else ssm_ref
            lo = (c % per) * rs
            acc = acc + _dot(src[:, lo:lo + rs], w_ref[c])
        o_ref[...] = acc

    row = lambda w: pl.BlockSpec((tm, w), lambda i: (i, 0))
    return pl.pallas_call(
        body, name="outproj_fwd", grid=(lp // tm,),
        in_specs=[row(d), row(rw), row(ssm.shape[1]), pl.BlockSpec((nck, rs, d), lambda i: (0, 0, 0))],
        out_specs=row(d), out_shape=jax.ShapeDtypeStruct((lp, d), F32),
        compiler_params=_params(("arbitrary",)),
    )(h, ret, ssm, wo)


def _outproj_bwd(dh, ret, ssm, wo):
    lp, d = dh.shape
    nck, rs, _ = wo.shape
    rw = ret.shape[1]
    sw = ssm.shape[1]
    tm = _tile(lp, 640)
    per = rw // rs
    last = lp // tm - 1

    def body(dh_ref, ret_ref, ssm_ref, w_ref, dret_ref, dssm_ref, dw_ref, acc_sc):
        i = pl.program_id(0)

        @pl.when(i == 0)
        def _():
            acc_sc[...] = jnp.zeros_like(acc_sc)

        dhb = dh_ref[...].astype(BF16)
        for c in range(nck):
            src, dst = (ret_ref, dret_ref) if c < per else (ssm_ref, dssm_ref)
            lo = (c % per) * rs
            dst[:, lo:lo + rs] = _dot_nt(dhb, w_ref[c])
            acc_sc[c] += _dot_tn(src[:, lo:lo + rs], dhb)

        @pl.when(i == last)
        def _():
            dw_ref[...] = acc_sc[...].astype(BF16)

    row = lambda w: pl.BlockSpec((tm, w), lambda i: (i, 0))
    wsp = pl.BlockSpec((nck, rs, d), lambda i: (0, 0, 0))
    return pl.pallas_call(
        body, name="outproj_bwd", grid=(lp // tm,),
        in_specs=[row(d), row(rw), row(sw), wsp],
        out_specs=(row(rw), row(sw), wsp),
        out_shape=(jax.ShapeDtypeStruct((lp, rw), F32), jax.ShapeDtypeStruct((lp, sw), F32),
                   jax.ShapeDtypeStruct((nck, rs, d), BF16)),
        scratch_shapes=[pltpu.VMEM((nck, rs, d), F32)],
        compiler_params=_params(("arbitrary",)),
    )(dh, ret, ssm, wo)


def _loss_head(h, fw, target):
    lp, d = h.shape
    nblk = lp // CHUNK

    def body(h_ref, w_ref, t_ref, loss_ref, dh_ref, dw_ref):
        i = pl.program_id(0)

        @pl.when(i == 0)
        def _():
            loss_ref[...] = jnp.zeros_like(loss_ref)
            dw_ref[...] = jnp.zeros_like(dw_ref)
            dh_ref[...] = jnp.zeros_like(dh_ref)

        @pl.when(i > 0)
        def _():
            xh, r = _rms_stats(h_ref[...])
            w = w_ref[...]
            err = xh * w - t_ref[...]
            loss_ref[...] += 0.5 * jnp.sum(err * err) / d
            dout = err * (1.0 / d)
            dw_ref[...] += jnp.sum(dout * xh, axis=0, keepdims=True)
            dh_ref[...] = _rms_bwd(dout, xh, r, w)

    return pl.pallas_call(
        body, name="loss_head", grid=(nblk,),
        in_specs=[pl.BlockSpec((CHUNK, d), lambda i: (i, 0)), pl.BlockSpec((1, d), lambda i: (0, 0)),
                  pl.BlockSpec((CHUNK, d), lambda i: (jnp.maximum(i - 1, 0), 0))],
        out_specs=(pl.BlockSpec((8, LANE), lambda i: (0, 0)), pl.BlockSpec((CHUNK, d), lambda i: (i, 0)),
                   pl.BlockSpec((1, d), lambda i: (0, 0))),
        out_shape=(jax.ShapeDtypeStruct((8, LANE), F32), jax.ShapeDtypeStruct((lp, d), F32),
                   jax.ShapeDtypeStruct((1, d), F32)),
        compiler_params=_params(("arbitrary",)),
    )(h, fw, target)


def _pack(arrs):
    flat = jnp.concatenate([a.reshape(-1).astype(F32) for a in arrs])
    n = flat.shape[0]
    rows = -(-n // (8 * LANE)) * 8
    return jnp.pad(flat, (0, rows * LANE - n)).reshape(rows, LANE)


def _unpack(packed, shapes):
    flat = packed.reshape(-1)
    out, off = [], 0
    for s in shapes:
        n = math.prod(s)
        out.append(flat[off:off + n].reshape(s))
        off += n
    return out


def _to_segments(a, seg_len):
    return a.reshape(N_SEG, seg_len, a.shape[1]).transpose(1, 0, 2).reshape(a.shape)


def _from_segments(a, seg_len):
    return a.reshape(seg_len, N_SEG, a.shape[1]).transpose(1, 0, 2).reshape(a.shape)


WEIGHT_NAMES = ['meta_tokens', 'ffn1_norm_w', 'ffn1_w_gate', 'ffn1_w_up', 'ffn1_w_down', 'mix_norm_w', 'w_in',
                'ret_norm_w', 'ssm_lambda_re', 'ssm_lambda_im', 'ssm_log_dt', 'ssm_b_re', 'ssm_b_im', 'ssm_c_re',
                'ssm_c_im', 'ssm_d', 'ssm_glu_w', 'ssm_glu_b', 'ssm_norm_w', 'w_out', 'ffn2_norm_w', 'ffn2_w_gate',
                'ffn2_w_up', 'ffn2_w_down', 'final_norm_w']
BIG = ['ffn1_w_gate', 'ffn1_w_up', 'ffn1_w_down', 'w_in', 'ssm_glu_w', 'w_out', 'ffn2_w_gate', 'ffn2_w_up',
       'ffn2_w_down']
TRANSPOSED = ['ffn1_w_gate', 'ffn1_w_up', 'ffn2_w_gate', 'ffn2_w_up']
BIG_EARLY = ['ffn1_w_gate', 'ffn1_w_up', 'ffn1_w_down']
BIG_LATE = [n for n in BIG if n not in BIG_EARLY]
SMALL = [n for n in WEIGHT_NAMES if n not in BIG]


def kernel(x, meta_tokens, ffn1_norm_w, ffn1_w_gate, ffn1_w_up, ffn1_w_down, mix_norm_w, w_in, ret_norm_w, ssm_lambda_re, ssm_lambda_im, ssm_log_dt, ssm_b_re, ssm_b_im, ssm_c_re, ssm_c_im, ssm_d, ssm_glu_w, ssm_glu_b, ssm_norm_w, w_out, ffn2_norm_w, ffn2_w_gate, ffn2_w_up, ffn2_w_down, final_norm_w, loss_target, m_meta_tokens, m_ffn1_norm_w, m_ffn1_w_gate, m_ffn1_w_up, m_ffn1_w_down, m_mix_norm_w, m_w_in, m_ret_norm_w, m_ssm_lambda_re, m_ssm_lambda_im, m_ssm_log_dt, m_ssm_b_re, m_ssm_b_im, m_ssm_c_re, m_ssm_c_im, m_ssm_d, m_ssm_glu_w, m_ssm_glu_b, m_ssm_norm_w, m_w_out, m_ffn2_norm_w, m_ffn2_w_gate, m_ffn2_w_up, m_ffn2_w_down, m_final_norm_w, v_meta_tokens, v_ffn1_norm_w, v_ffn1_w_gate, v_ffn1_w_up, v_ffn1_w_down, v_mix_norm_w, v_w_in, v_ret_norm_w, v_ssm_lambda_re, v_ssm_lambda_im, v_ssm_log_dt, v_ssm_b_re, v_ssm_b_im, v_ssm_c_re, v_ssm_c_im, v_ssm_d, v_ssm_glu_w, v_ssm_glu_b, v_ssm_norm_w, v_w_out, v_ffn2_norm_w, v_ffn2_w_gate, v_ffn2_w_up, v_ffn2_w_down, v_final_norm_w):
    args = locals()
    w = {n: args[n] for n in WEIGHT_NAMES}
    m = {n: args["m_" + n] for n in WEIGHT_NAMES}
    v = {n: args["v_" + n] for n in WEIGHT_NAMES}

    seq, d = x.shape[1], x.shape[2]
    lp = seq + CHUNK
    seg_len = lp // N_SEG
    rw = RET_HEADS * HEAD_DIM
    sw = ssm_d.shape[-1]
    groups = sw // SSM_GROUP
    ns = groups * SSM_STATE
    jb = _tile(seg_len, 40, 8)
    chip = 2 * lax.axis_index("x") + lax.axis_index("y")

    as_fd = lambda t: jnp.swapaxes(t, -1, -2)
    shards = {n: (as_fd(w[n][0]) if n in TRANSPOSED else w[n][0]).astype(BF16) for n in BIG}
    early = [shards[n] for n in BIG_EARLY] + [meta_tokens]
    gathered = _forward_sibling("gather_early_forward",
                                _exchange("gather_early", _allgather_chips_plan(early), early))
    gw = dict(zip(BIG_EARLY, gathered[:-1]))
    meta_full = jnp.transpose(gathered[-1], (1, 0, 2)).reshape(N_META, d)
    late = [shards[n] for n in BIG_LATE]

    pos = jnp.arange(lp, dtype=F32) - float(CHUNK - N_META)
    freqs = 1.0 / (ROPE_BASE ** (jnp.arange(0, HEAD_DIM, 2, dtype=F32) / HEAD_DIM))
    ang = pos[:, None] * freqs[None, :]
    cosf = jnp.concatenate([jnp.cos(ang), jnp.cos(ang)], axis=1)
    sinf = jnp.concatenate([-jnp.sin(ang), jnp.sin(ang)], axis=1)
    tables = _retention_tables()

    lam_re, lam_im, log_dt = ssm_lambda_re[0], ssm_lambda_im[0], ssm_log_dt[0]
    b_re, b_im, c_re, c_im = ssm_b_re[0], ssm_b_im[0], ssm_c_re[0], ssm_c_im[0]
    (ar, ai, bbr, bbi), prep_vjp = jax.vjp(_s5_prepare, lam_re, lam_im, log_dt, b_re, b_im)
    dt = jnp.exp(log_dt)[:, None]
    el = jnp.exp(seg_len * lam_re * dt)
    alr = el * jnp.cos(seg_len * lam_im * dt)
    ali = el * jnp.sin(seg_len * lam_im * dt)
    bc8 = lambda t: jnp.broadcast_to(t.reshape(1, ns), (N_SEG, ns))
    a8r, a8i, al8r, al8i = bc8(ar), bc8(ai), bc8(alr), bc8(ali)
    bsr = _blockdiag_in(jnp.transpose(bbr, (0, 2, 1)))
    bsi = _blockdiag_in(jnp.transpose(bbi, (0, 2, 1)))
    csrt = _blockdiag_in(c_re)
    csit = _blockdiag_in(-c_im)
    tr = lambda t: jnp.transpose(t, (0, 2, 1))
    bsr_b, bsi_b = bsr.astype(BF16), bsi.astype(BF16)
    csr_b, csi_b = tr(csrt).astype(BF16), tr(csit).astype(BF16)
    bsrt_b, bsit_b = tr(bsr).astype(BF16), tr(bsi).astype(BF16)
    csrt_b, csit_b = csrt.astype(BF16), csit.astype(BF16)

    h0 = jnp.concatenate([jnp.zeros((CHUNK - N_META, d), F32), meta_full, x[0]], axis=0)
    (h1, g1, u1), late_half = _ffn_fwd("ffn1_fwd", h0, ffn1_norm_w, gw['ffn1_w_gate'], gw['ffn1_w_up'],
                                       gw['ffn1_w_down'], _allgather_chips_plan(late), late)
    gw.update(zip(BIG_LATE, _forward_sibling("gather_late_forward", late_half)))
    glu_full = gw['ssm_glu_w'].reshape(sw, sw)
    n2, q, k, vv, gate, u = _inproj_fwd(h1, mix_norm_w, gw['w_in'], cosf, sinf, rw)
    o, ret, sprev = _ret_fwd(q, k, vv, gate, ret_norm_w, tables)
    u_seg = _to_segments(u, seg_len)
    xr, xi, c0r, c0i, yp, ssm_seg = _s5_fwd(u_seg, bsr_b, bsi_b, csr_b, csi_b, a8r, a8i, al8r, al8i,
                                            ssm_d, glu_full, ssm_glu_b, ssm_norm_w, jb)
    ssm = _from_segments(ssm_seg, seg_len)
    h2 = _outproj_fwd(h1, ret, ssm, gw['w_out'])
    (h3, g2, u2), _ = _ffn_fwd("ffn2_fwd", h2, ffn2_norm_w, gw['ffn2_w_gate'], gw['ffn2_w_up'], gw['ffn2_w_down'])
    loss_part, dh3, d_final = _loss_head(h3, final_norm_w.reshape(1, d), loss_target[0])

    (dh2, d_ffn2_norm, nb, daccb, ab, dgb, dub), _ = _ffn_bwd_act(
        "ffn2_bwd_act", dh3, h2, ffn2_norm_w, g2, u2, gw['ffn2_w_gate'], gw['ffn2_w_up'], gw['ffn2_w_down'])
    dwg2, dwu2, dwd2 = _ffn_bwd_w("ffn2_bwd_w", nb, daccb, ab, dgb, dub)
    dret, dssm, dwo = _outproj_bwd(dh2, ret, ssm, gw['w_out'])
    (du_seg, d_ssm_norm, d_glu_w, d_glu_b, d_ssm_d, dcr_s, dci_s, dbr_s, dbi_s, dar8, dai8) = _s5_bwd(
        _to_segments(dssm, seg_len), u_seg, yp, xr, xi, c0r, c0i, bsrt_b, bsit_b, csrt_b, csit_b,
        a8r, a8i, al8r, al8i, ssm_d, glu_full, ssm_glu_b, ssm_norm_w, jb)
    du = _from_segments(du_seg, seg_len)
    dq, dk, dv, dgate, d_ret_norm = _ret_bwd(dret, q, k, vv, gate, o, sprev, ret_norm_w, tables, cosf, sinf)
    dh1, d_mix_norm, dwin = _inproj_bwd(dh2, h1, mix_norm_w, n2, gw['w_in'], dq, dk, dv, dgate, du)
    late_parts = {
        'w_in': dwin, 'ssm_glu_w': d_glu_w.reshape(N_CHIP, sw // N_CHIP, sw).astype(BF16), 'w_out': dwo,
        'ffn2_w_gate': dwg2, 'ffn2_w_up': dwu2, 'ffn2_w_down': dwd2,
    }
    late_list = [late_parts[n] for n in BIG_LATE]
    (dh0, d_ffn1_norm, nb, daccb, ab, dgb, dub), late_recv = _ffn_bwd_act(
        "ffn1_bwd_act", dh1, h0, ffn1_norm_w, g1, u1, gw['ffn1_w_gate'], gw['ffn1_w_up'], gw['ffn1_w_down'],
        _alltoall_chips_plan(late_list), late_list)
    grad_x = dh0[CHUNK:][None]
    d_meta = dh0[CHUNK - N_META:CHUNK]

    d_c_re = jnp.transpose(_blockdiag_out(tr(dcr_s), groups, SSM_GROUP, SSM_STATE), (0, 1, 2))
    d_c_im = -_blockdiag_out(tr(dci_s), groups, SSM_GROUP, SSM_STATE)
    d_bbr = jnp.transpose(_blockdiag_out(dbr_s, groups, SSM_GROUP, SSM_STATE), (0, 2, 1))
    d_bbi = jnp.transpose(_blockdiag_out(dbi_s, groups, SSM_GROUP, SSM_STATE), (0, 2, 1))
    d_ar = jnp.sum(dar8, axis=0).reshape(groups, SSM_STATE)
    d_ai = jnp.sum(dai8, axis=0).reshape(groups, SSM_STATE)
    small_parts = [loss_part[0:1, :], d_meta, d_ffn1_norm, d_mix_norm, d_ret_norm, d_ar, d_ai, d_bbr, d_bbi,
                   d_c_re, d_c_im, d_ssm_d, d_glu_b, d_ssm_norm, d_ffn2_norm, d_final]
    small_shapes = [a.shape for a in small_parts]
    packed = _pack(small_parts)
    early_recv, (all_parts,) = _ffn_bwd_w_scatter("ffn1_bwd_w", nb, daccb, ab, dgb, dub, chip,
                                                  _allgather_all_plan([packed]), [packed])
    received = dict(zip(BIG_LATE + BIG_EARLY, late_recv + early_recv))
    chip_sums = [_sum_slots("sum_chips_" + n, received[n], BF16) for n in BIG]
    sib_sums = _swap_sibling("swap_sibling", chip_sums)
    (loss_row, g_meta_full, g_ffn1_norm, g_mix_norm, g_ret_norm, g_ar, g_ai, g_bbr, g_bbi, g_c_re, g_c_im,
     g_ssm_d, g_glu_b, g_ssm_norm, g_ffn2_norm, g_final) = _unpack(_sum_slots("sum_small", all_parts, F32), small_shapes)
    g_lam_re, g_lam_im, g_log_dt, g_b_re, g_b_im = prep_vjp((g_ar, g_ai, g_bbr, g_bbi))
    loss = loss_row[0, 0]
    g_meta = lax.dynamic_slice(g_meta_full, (0, chip * (d // N_CHIP)), (N_META, d // N_CHIP))
    small_grads = {
        'meta_tokens': g_meta, 'ffn1_norm_w': g_ffn1_norm, 'mix_norm_w': g_mix_norm, 'ret_norm_w': g_ret_norm,
        'ssm_lambda_re': g_lam_re[None], 'ssm_lambda_im': g_lam_im[None], 'ssm_log_dt': g_log_dt[None],
        'ssm_b_re': g_b_re[None], 'ssm_b_im': g_b_im[None], 'ssm_c_re': g_c_re[None], 'ssm_c_im': g_c_im[None],
        'ssm_d': g_ssm_d, 'ssm_glu_b': g_glu_b, 'ssm_norm_w': g_ssm_norm, 'ffn2_norm_w': g_ffn2_norm,
        'final_norm_w': g_final.reshape(d),
    }

    grads, deltas, new_m, new_v = {}, {}, {}, {}
    for n, mine, sib in zip(BIG, chip_sums, sib_sums):
        if n in TRANSPOSED:
            outs = _adam("adam_" + n, as_fd(w[n]), as_fd(m[n]), as_fd(v[n]), [mine, sib])
            grads[n], deltas[n], new_m[n], new_v[n] = [as_fd(t) for t in outs]
        else:
            grads[n], deltas[n], new_m[n], new_v[n] = _adam("adam_" + n, w[n], m[n], v[n], [mine, sib])
    sm_shapes = [w[n].shape for n in SMALL]
    sm_out = _adam("adam_small", _pack([w[n] for n in SMALL]), _pack([m[n] for n in SMALL]),
                   _pack([v[n] for n in SMALL]), [_pack([small_grads[n].reshape(w[n].shape) for n in SMALL])])
    for dst, packed in zip((grads, deltas, new_m, new_v), sm_out):
        for n, t in zip(SMALL, _unpack(packed, sm_shapes)):
            dst[n] = t

    return (loss, grad_x, *[grads[n] for n in WEIGHT_NAMES], *[deltas[n] for n in WEIGHT_NAMES],
            *[new_m[n] for n in WEIGHT_NAMES], *[new_v[n] for n in WEIGHT_NAMES])
```

```python
import functools
import math

import jax
import jax.numpy as jnp
from jax import lax
from jax.experimental import pallas as pl
from jax.experimental.pallas import tpu as pltpu

N_META = 16
RET_HEADS = 4
HEAD_DIM = 128
SSM_GROUP = 16
SSM_STATE = 64
CHUNK = 128
ROPE_BASE = 10000.0
EPS = 1e-6
FFN_RES = 0.5
N_SEG = 8
N_SEC = 4
N_CHIP = 4
LANE = 128
FFN_CPS = 2
BWD_W_ROWS = 1664

ADAM_LR = 0.001
ADAM_B1 = 0.9
ADAM_B2 = 0.999
ADAM_EPS = 1e-08
ADAM_WD = 0.01
ADAM_STEP = 10

VMEM_LIMIT = 56 * 1024 * 1024

F32 = jnp.float32
BF16 = jnp.bfloat16
MESH = pl.DeviceIdType.MESH


def _dot(a, b):
    return jnp.dot(a, b, preferred_element_type=F32)


def _dot_nt(a, b):
    return lax.dot_general(a, b, (((1,), (1,)), ((), ())), preferred_element_type=F32)


def _dot_tn(a, b):
    return lax.dot_general(a, b, (((0,), (0,)), ((), ())), preferred_element_type=F32)


def _tile(n, target, mult=64):
    best = None
    t = mult
    while t <= min(n, target):
        if n % t == 0:
            best = t
        t += mult
    assert best is not None, (n, target)
    return best


def _params(sem, vmem=VMEM_LIMIT):
    return pltpu.CompilerParams(dimension_semantics=sem, vmem_limit_bytes=vmem)


def _rms_stats(xf):
    r = lax.rsqrt(jnp.mean(xf * xf, axis=-1, keepdims=True) + EPS)
    return xf * r, r


def _rms_bwd(dy, xh, r, w):
    dxh = dy * w
    return r * (dxh - xh * jnp.mean(dxh * xh, axis=-1, keepdims=True))


def _sigmoid(x):
    return 1.0 / (1.0 + jnp.exp(-x))


GELU_K0 = math.sqrt(2.0 / math.pi)
GELU_K1 = 0.044715


CHIP_MASKS = [(1, 0, 0), (0, 1, 0), (1, 1, 0)]
ALL_MASKS = [(0, 0, 1), (0, 1, 0), (0, 1, 1), (1, 0, 0), (1, 0, 1), (1, 1, 0), (1, 1, 1)]
SIB_MASKS = [(0, 0, 1)]
ANY_SPEC = pl.BlockSpec(memory_space=pl.ANY)


class _Plan:
    def __init__(self, arrays, masks, n_slots, src_slotted, dst_slotted, local_copy, half=False, forward=False):
        self.shapes = [(a.shape, a.dtype) for a in arrays]
        self.n = len(arrays)
        self.masks = masks
        self.n_slots = n_slots
        self.src_slotted, self.dst_slotted, self.local_copy = src_slotted, dst_slotted, local_copy
        self.half, self.forward = half, forward
        self.n_cp = self.n * len(masks) * (len(CHIP_MASKS) if forward else 1)

    def out_shape(self):
        out = []
        for shp, dt in self.shapes:
            if self.dst_slotted and not self.src_slotted:
                shp = (self.n_slots,) + shp
            elif self.src_slotted and not self.dst_slotted:
                shp = shp[1:]
            out.append(jax.ShapeDtypeStruct(shp, dt))
        return tuple(out)

    def scratch(self):
        return [pltpu.SemaphoreType.DMA((self.n_cp,)), pltpu.SemaphoreType.DMA((self.n_cp,)),
                pltpu.SemaphoreType.DMA((self.n,))]

    def _slot(self, px, py, pc):
        if self.n_slots == 8:
            return 4 * px + 2 * py + pc
        if self.n_slots == 4:
            return 2 * px + py
        return pc

    def copies(self, ins, outs, sems):
        send_sems, recv_sems, loc_sems = sems
        x, y, c = lax.axis_index("x"), lax.axis_index("y"), lax.axis_index("c")
        me = self._slot(x, y, c)
        n_m = len(self.masks)
        cps = []
        for a in range(self.n):
            if self.forward:
                rows = self.shapes[a][0][-2] // 2
                mine = pl.ds(pl.multiple_of(c * rows, 8), rows)
                for j, (mx, my, _) in enumerate(CHIP_MASKS):
                    blk = outs[a].at[2 * (1 - x if mx else x) + (1 - y if my else y), mine]
                    k = a * len(CHIP_MASKS) + j
                    cps.append(pltpu.make_async_remote_copy(
                        src_ref=blk, dst_ref=blk, send_sem=send_sems.at[k], recv_sem=recv_sems.at[k],
                        device_id=(x, y, 1 - c), device_id_type=MESH))
                continue
            if self.local_copy:
                src = ins[a].at[me] if self.src_slotted else ins[a]
                cps.append(pltpu.make_async_copy(src, outs[a].at[me], loc_sems.at[a]))
            for mi, (mx, my, mc) in enumerate(self.masks):
                px = 1 - x if mx else x
                py = 1 - y if my else y
                pc = 1 - c if mc else c
                src = ins[a].at[self._slot(px, py, pc)] if self.src_slotted else ins[a]
                dst = outs[a].at[me] if self.dst_slotted else outs[a]
                if self.half:
                    rows = src.shape[-2] // 2
                    mine = pl.ds(pl.multiple_of(c * rows, 8), rows)
                    src, dst = src.at[mine], dst.at[mine]
                k = a * n_m + mi
                cps.append(pltpu.make_async_remote_copy(
                    src_ref=src, dst_ref=dst, send_sem=send_sems.at[k], recv_sem=recv_sems.at[k],
                    device_id=(px, py, pc), device_id_type=MESH))
        return cps


def _exchange(name, plan, arrays):
    n = plan.n

    def body(*refs):
        cps = plan.copies(refs[:n], refs[n:2 * n], refs[2 * n:])
        for cp in cps:
            cp.start()
        for cp in cps:
            cp.wait()

    outs = pl.pallas_call(
        body, name=name, out_shape=plan.out_shape(),
        in_specs=[ANY_SPEC] * n, out_specs=tuple([ANY_SPEC] * n), scratch_shapes=plan.scratch(),
        input_output_aliases={i: i for i in range(n)} if plan.forward else {},
    )(*arrays)
    return list(outs)


def _pcall(body, *, name, grid, in_specs, out_specs, out_shape, scratch_shapes, args, plan=None, plan_args=()):
    sem = ("arbitrary",) * len(grid)
    if plan is None:
        return pl.pallas_call(body, name=name, grid=grid, in_specs=in_specs, out_specs=out_specs,
                              out_shape=out_shape, scratch_shapes=scratch_shapes,
                              compiler_params=_params(sem))(*args), []
    n_in, n_out, n_scr, n_p = len(in_specs), len(out_specs), len(scratch_shapes), plan.n

    def wrapped(*refs):
        ins = refs[:n_in]
        p_ins = refs[n_in:n_in + n_p]
        o0 = n_in + n_p
        outs = refs[o0:o0 + n_out]
        p_outs = refs[o0 + n_out:o0 + n_out + n_p]
        s0 = o0 + n_out + n_p
        scr = refs[s0:s0 + n_scr]
        sems = refs[s0 + n_scr:]
        ids = [pl.program_id(i) for i in range(len(grid))]
        first = functools.reduce(jnp.logical_and, [i == 0 for i in ids])
        last = functools.reduce(jnp.logical_and, [i == g - 1 for i, g in zip(ids, grid)])

        @pl.when(first)
        def _():
            for cp in plan.copies(p_ins, p_outs, sems):
                cp.start()

        body(*ins, *outs, *scr)

        @pl.when(last)
        def _():
            for cp in plan.copies(p_ins, p_outs, sems):
                cp.wait()

    res = pl.pallas_call(
        wrapped, name=name, grid=grid,
        in_specs=list(in_specs) + [ANY_SPEC] * n_p,
        out_specs=tuple(out_specs) + (ANY_SPEC,) * n_p,
        out_shape=tuple(out_shape) + plan.out_shape(),
        scratch_shapes=list(scratch_shapes) + plan.scratch(),
        compiler_params=_params(sem),
    )(*args, *plan_args)
    return res[:n_out], list(res[n_out:])


def _allgather_chips_plan(arrays):
    return _Plan(arrays, CHIP_MASKS, 4, False, True, True, half=True)


def _forward_sibling(name, gathered):
    return _exchange(name, _Plan(gathered, SIB_MASKS, 4, True, True, False, forward=True), gathered)


def _alltoall_chips_plan(arrays):
    return _Plan(arrays, CHIP_MASKS, 4, True, True, True)


def _swap_sibling(name, arrays):
    return _exchange(name, _Plan(arrays, SIB_MASKS, 2, False, False, False), arrays)


def _allgather_all_plan(arrays):
    return _Plan(arrays, ALL_MASKS, 8, False, True, True)


def _sum_slots(name, a, out_dtype):
    s, r, c = a.shape
    tr = _tile(r, 512, 8)

    def body(a_ref, o_ref):
        acc = a_ref[0].astype(F32)
        for i in range(1, s):
            acc = acc + a_ref[i].astype(F32)
        o_ref[...] = acc.astype(out_dtype)

    return pl.pallas_call(
        body, name=name, grid=(r // tr,),
        in_specs=[pl.BlockSpec((s, tr, c), lambda i: (0, i, 0))],
        out_specs=pl.BlockSpec((tr, c), lambda i: (i, 0)),
        out_shape=jax.ShapeDtypeStruct((r, c), out_dtype),
        compiler_params=_params(("arbitrary",)),
    )(a)


def _adam_math(w, g, m, v):
    m_new = ADAM_B1 * m + (1.0 - ADAM_B1) * g
    v_new = ADAM_B2 * v + (1.0 - ADAM_B2) * (g * g)
    m_hat = m_new / (1.0 - ADAM_B1 ** ADAM_STEP)
    v_hat = v_new / (1.0 - ADAM_B2 ** ADAM_STEP)
    delta = -ADAM_LR * (m_hat / (jnp.sqrt(v_hat) + ADAM_EPS) + ADAM_WD * w)
    return delta, m_new, v_new


def _adam(name, w, m, v, g_parts):
    r, c = w.shape[-2:]
    tr = _tile(r, 256, 8)
    n_g = len(g_parts)
    lead = w.ndim == 3
    at = (lambda ref: ref.at[0]) if lead else (lambda ref: ref)

    def body(*refs):
        w_ref, m_ref, v_ref = [at(t) for t in refs[:3]]
        g_refs = refs[3:3 + n_g]
        g_out, d_out, m_out, v_out = [at(t) for t in refs[3 + n_g:]]
        g = g_refs[0][...].astype(F32)
        for gr in g_refs[1:]:
            g = g + gr[...].astype(F32)
        delta, m_new, v_new = _adam_math(w_ref[...], g, m_ref[...], v_ref[...])
        g_out[...] = g
        d_out[...] = delta
        m_out[...] = m_new
        v_out[...] = v_new

    spec = pl.BlockSpec((tr, c), lambda i: (i, 0))
    wspec = pl.BlockSpec((1, tr, c), lambda i: (0, i, 0)) if lead else spec
    shp = jax.ShapeDtypeStruct(w.shape, F32)
    return pl.pallas_call(
        body, name=name, grid=(r // tr,),
        in_specs=[wspec] * 3 + [spec] * n_g, out_specs=(wspec,) * 4, out_shape=(shp,) * 4,
        compiler_params=_params(("arbitrary",)),
    )(w, m, v, *g_parts)


def _ffn_fwd(name, h, nw, wg, wu, wd, plan=None, plan_args=()):
    lp, d = h.shape
    nck, f, _ = wg.shape
    tm = _tile(lp, 640)
    last = nck // FFN_CPS - 1

    def body(h_ref, nw_ref, wg_ref, wu_ref, wd_ref, ho_ref, g_ref, u_ref, n_sc, acc_sc):
        k = pl.program_id(1)

        @pl.when(k == 0)
        def _():
            xh, _ = _rms_stats(h_ref[...])
            n_sc[...] = (xh * nw_ref[...]).astype(BF16)
            acc_sc[...] = jnp.zeros_like(acc_sc)

        n = n_sc[...]
        acc = acc_sc[...]
        for c in range(FFN_CPS):
            g = _dot_nt(n, wg_ref[c])
            u = _dot_nt(n, wu_ref[c])
            g_ref[c] = g.astype(BF16)
            u_ref[c] = u.astype(BF16)
            a = (g * _sigmoid(g) * u).astype(BF16)
            acc = acc + _dot(a, wd_ref[c])
        acc_sc[...] = acc

        @pl.when(k == last)
        def _():
            ho_ref[...] = h_ref[...] + FFN_RES * acc_sc[...]

    return _pcall(
        body, name=name, grid=(lp // tm, nck // FFN_CPS), plan=plan, plan_args=plan_args, args=(h, nw, wg, wu, wd),
        in_specs=[pl.BlockSpec((tm, d), lambda i, k: (i, 0)),
                  pl.BlockSpec((1, d), lambda i, k: (0, 0)),
                  pl.BlockSpec((FFN_CPS, f, d), lambda i, k: (k, 0, 0)),
                  pl.BlockSpec((FFN_CPS, f, d), lambda i, k: (k, 0, 0)),
                  pl.BlockSpec((FFN_CPS, f, d), lambda i, k: (k, 0, 0))],
        out_specs=(pl.BlockSpec((tm, d), lambda i, k: (i, 0)),
                   pl.BlockSpec((FFN_CPS, tm, f), lambda i, k: (k, i, 0)),
                   pl.BlockSpec((FFN_CPS, tm, f), lambda i, k: (k, i, 0))),
        out_shape=(jax.ShapeDtypeStruct((lp, d), F32),
                   jax.ShapeDtypeStruct((nck, lp, f), BF16),
                   jax.ShapeDtypeStruct((nck, lp, f), BF16)),
        scratch_shapes=[pltpu.VMEM((tm, d), BF16), pltpu.VMEM((tm, d), F32)])


def _ffn_bwd_act(name, dh, h, nw, g, u, wg, wu, wd, plan=None, plan_args=()):
    lp, d = h.shape
    nck, f, _ = wg.shape
    tm = _tile(lp, 320)
    last = nck // FFN_CPS - 1

    def body(dh_ref, h_ref, nw_ref, g_ref, u_ref, wg_ref, wu_ref, wd_ref,
             dhi_ref, dnw_ref, n_ref, dacc_ref, a_ref, dg_ref, du_ref,
             xh_sc, r_sc, dn_sc):
        i = pl.program_id(0)
        k = pl.program_id(1)

        @pl.when(k == 0)
        def _():
            xh, r = _rms_stats(h_ref[...])
            xh_sc[...] = xh
            r_sc[...] = r
            n_ref[...] = (xh * nw_ref[...]).astype(BF16)
            dacc_ref[...] = (FFN_RES * dh_ref[...]).astype(BF16)
            dn_sc[...] = jnp.zeros_like(dn_sc)

        @pl.when(jnp.logical_and(i == 0, k == 0))
        def _():
            dnw_ref[...] = jnp.zeros_like(dnw_ref)

        dacc = dacc_ref[...]
        dn = dn_sc[...]
        for c in range(FFN_CPS):
            gv = g_ref[c].astype(F32)
            uv = u_ref[c].astype(F32)
            sg = _sigmoid(gv)
            sil = gv * sg
            da = _dot_nt(dacc, wd_ref[c])
            dgk = (da * uv * (sg * (1.0 + gv * (1.0 - sg)))).astype(BF16)
            duk = (da * sil).astype(BF16)
            a_ref[c] = (sil * uv).astype(BF16)
            dg_ref[c] = dgk
            du_ref[c] = duk
            dn = dn + _dot(dgk, wg_ref[c]) + _dot(duk, wu_ref[c])
        dn_sc[...] = dn

        @pl.when(k == last)
        def _():
            dn = dn_sc[...]
            xh = xh_sc[...]
            dhi_ref[...] = dh_ref[...] + _rms_bwd(dn, xh, r_sc[...], nw_ref[...])
            dnw_ref[...] += jnp.sum(dn * xh, axis=0, keepdims=True)

    row = pl.BlockSpec((tm, d), lambda i, k: (i, 0))
    vec = pl.BlockSpec((1, d), lambda i, k: (0, 0))
    hid = pl.BlockSpec((FFN_CPS, tm, f), lambda i, k: (k, i, 0))
    w_fd = pl.BlockSpec((FFN_CPS, f, d), lambda i, k: (k, 0, 0))
    return _pcall(
        body, name=name, grid=(lp // tm, nck // FFN_CPS), plan=plan, plan_args=plan_args,
        args=(dh, h, nw, g, u, wg, wu, wd),
        in_specs=[row, row, vec, hid, hid, w_fd, w_fd, w_fd],
        out_specs=(row, vec, row, row, hid, hid, hid),
        out_shape=(jax.ShapeDtypeStruct((lp, d), F32),
                   jax.ShapeDtypeStruct((1, d), F32),
                   jax.ShapeDtypeStruct((lp, d), BF16),
                   jax.ShapeDtypeStruct((lp, d), BF16),
                   jax.ShapeDtypeStruct((nck, lp, f), BF16),
                   jax.ShapeDtypeStruct((nck, lp, f), BF16),
                   jax.ShapeDtypeStruct((nck, lp, f), BF16)),
        scratch_shapes=[pltpu.VMEM((tm, d), F32), pltpu.VMEM((tm, 1), F32), pltpu.VMEM((tm, d), F32)])


def _ffn_bwd_w(name, n, dacc, a, dg, du):
    lp, d = n.shape
    nck, _, f = a.shape
    tm = _tile(lp, BWD_W_ROWS)
    last = lp // tm - 1

    def body(n_ref, dacc_ref, a_ref, dg_ref, du_ref, dwg_ref, dwu_ref, dwd_ref, ag_sc, au_sc, ad_sc):
        i = pl.program_id(1)

        @pl.when(i == 0)
        def _():
            ag_sc[...] = jnp.zeros_like(ag_sc)
            au_sc[...] = jnp.zeros_like(au_sc)
            ad_sc[...] = jnp.zeros_like(ad_sc)

        nv = n_ref[...]
        ag_sc[...] += _dot_tn(dg_ref[0], nv)
        au_sc[...] += _dot_tn(du_ref[0], nv)
        ad_sc[...] += _dot_tn(a_ref[0], dacc_ref[...])

        @pl.when(i == last)
        def _():
            dwg_ref[0] = ag_sc[...].astype(BF16)
            dwu_ref[0] = au_sc[...].astype(BF16)
            dwd_ref[0] = ad_sc[...].astype(BF16)

    row = pl.BlockSpec((tm, d), lambda k, i: (i, 0))
    hid = pl.BlockSpec((1, tm, f), lambda k, i: (k, i, 0))
    w_fd = pl.BlockSpec((1, f, d), lambda k, i: (k, 0, 0))
    wshape = jax.ShapeDtypeStruct((nck, f, d), BF16)
    return pl.pallas_call(
        body, name=name, grid=(nck, lp // tm),
        in_specs=[row, row, hid, hid, hid], out_specs=(w_fd, w_fd, w_fd), out_shape=(wshape,) * 3,
        scratch_shapes=[pltpu.VMEM((f, d), F32)] * 3,
        compiler_params=_params(("arbitrary", "arbitrary")),
    )(n, dacc, a, dg, du)


def _ffn_bwd_w_scatter(name, n, dacc, a, dg, du, chip, plan, plan_args):
    lp, d = n.shape
    nck, _, f = a.shape
    tm = _tile(lp, BWD_W_ROWS)
    last_i = lp // tm - 1
    n_w = 3
    n_p = plan.n

    def body(me_ref, n_ref, dacc_ref, a_ref, dg_ref, du_ref, *rest):
        p_ins = rest[:n_p]
        recv = rest[n_p:n_p + n_w]
        p_outs = rest[n_p + n_w:2 * n_p + n_w]
        acc = rest[2 * n_p + n_w:2 * n_p + 2 * n_w]
        stage, send_sems, recv_sems, loc_sems = rest[2 * n_p + 2 * n_w:2 * n_p + 2 * n_w + 4]
        p_sems = rest[2 * n_p + 2 * n_w + 4:]
        p = pl.program_id(0)
        i = pl.program_id(1)
        me = me_ref[0]
        c = lax.axis_index("c")

        def send(w, pos):
            kk = jnp.bitwise_xor(me, nck - 1 - pos)
            diff = jnp.bitwise_xor(kk, me)
            m = jnp.where(diff == 2, 0, jnp.where(diff == 1, 1, 2))
            return pltpu.make_async_remote_copy(
                src_ref=stage.at[lax.rem(pos, 2), w], dst_ref=recv[w].at[me],
                send_sem=send_sems.at[w * 3 + m], recv_sem=recv_sems.at[w * 3 + m],
                device_id=(lax.div(kk, 2), lax.rem(kk, 2), c), device_id_type=MESH)

        @pl.when(jnp.logical_and(p == 0, i == 0))
        def _():
            for cp in plan.copies(p_ins, p_outs, p_sems):
                cp.start()

        @pl.when(i == 0)
        def _():
            for t in acc:
                t[...] = jnp.zeros_like(t)

        nv = n_ref[...]
        acc[0][...] += _dot_tn(dg_ref[0], nv)
        acc[1][...] += _dot_tn(du_ref[0], nv)
        acc[2][...] += _dot_tn(a_ref[0], dacc_ref[...])

        @pl.when(jnp.logical_and(i == last_i, p >= 2))
        def _():
            for w in range(n_w):
                send(w, p - 2).wait_send()

        @pl.when(i == last_i)
        def _():
            for w in range(n_w):
                stage[lax.rem(p, 2), w] = acc[w][...].astype(BF16)

        @pl.when(jnp.logical_and(i == last_i, p < nck - 1))
        def _():
            for w in range(n_w):
                send(w, p).start()

        @pl.when(jnp.logical_and(i == last_i, p == nck - 1))
        def _():
            own = [pltpu.make_async_copy(stage.at[(nck - 1) % 2, w], recv[w].at[me], loc_sems.at[w])
                   for w in range(n_w)]
            for cp in own:
                cp.start()
            for w in range(n_w):
                send(w, nck - 2).wait_send()
            for cp in own:
                cp.wait()
            for w in range(n_w):
                for m in range(3):
                    pltpu.make_async_remote_copy(
                        src_ref=stage.at[0, w], dst_ref=recv[w].at[me],
                        send_sem=send_sems.at[w * 3 + m], recv_sem=recv_sems.at[w * 3 + m],
                        device_id=(0, 0, c), device_id_type=MESH).wait_recv()
            for cp in plan.copies(p_ins, p_outs, p_sems):
                cp.wait()

    chunk = lambda k, me_ref: jnp.bitwise_xor(me_ref[0], nck - 1 - k)
    row = pl.BlockSpec((tm, d), lambda k, i, me_ref: (i, 0))
    hid = pl.BlockSpec((1, tm, f), lambda k, i, me_ref: (chunk(k, me_ref), i, 0))
    wshape = jax.ShapeDtypeStruct((nck, f, d), BF16)
    res = pl.pallas_call(
        body, name=name,
        grid_spec=pltpu.PrefetchScalarGridSpec(
            num_scalar_prefetch=1, grid=(nck, lp // tm),
            in_specs=[row, row, hid, hid, hid] + [ANY_SPEC] * n_p,
            out_specs=(ANY_SPEC,) * (n_w + n_p),
            scratch_shapes=[pltpu.VMEM((f, d), F32)] * n_w + [
                pltpu.VMEM((2, n_w, f, d), BF16), pltpu.SemaphoreType.DMA((n_w * 3,)),
                pltpu.SemaphoreType.DMA((n_w * 3,)), pltpu.SemaphoreType.DMA((n_w,))] + plan.scratch()),
        out_shape=(wshape,) * n_w + plan.out_shape(),
        compiler_params=_params(("arbitrary", "arbitrary")),
    )(chip.reshape(1).astype(jnp.int32), n, dacc, a, dg, du, *plan_args)
    return list(res[:n_w]), list(res[n_w:])


def _inproj_fwd(h, nw, w_in, cosf, sinf, rw):
    lp, d = h.shape
    nck, _, ps = w_in.shape
    proj = nck * ps
    sw = proj - 4 * rw
    tm = _tile(lp, 640)
    scale = HEAD_DIM ** -0.5
    heads = rw // HEAD_DIM

    def body(h_ref, nw_ref, w_ref, cos_ref, sin_ref, n_ref, q_ref, k_ref, v_ref, g_ref, u_ref, p_sc):
        xh, _ = _rms_stats(h_ref[...])
        n = (xh * nw_ref[...]).astype(BF16)
        n_ref[...] = n
        for c in range(nck):
            p_sc[:, c * ps:(c + 1) * ps] = _dot(n, w_ref[c])
        cs = cos_ref[...]
        sn = sin_ref[...]
        for hh in range(heads):
            lo = hh * HEAD_DIM
            qh = p_sc[:, lo:lo + HEAD_DIM]
            q_ref[:, lo:lo + HEAD_DIM] = (qh * cs + pltpu.roll(qh, HEAD_DIM // 2, 1) * sn).astype(BF16)
            kh = p_sc[:, rw + lo:rw + lo + HEAD_DIM]
            k_ref[:, lo:lo + HEAD_DIM] = ((kh * cs + pltpu.roll(kh, HEAD_DIM // 2, 1) * sn) * scale).astype(BF16)
        v_ref[...] = p_sc[:, 2 * rw:3 * rw].astype(BF16)
        g_ref[...] = p_sc[:, 3 * rw:4 * rw]
        u_ref[...] = p_sc[:, 4 * rw:]

    row = lambda w: pl.BlockSpec((tm, w), lambda i: (i, 0))
    return pl.pallas_call(
        body, name="inproj_fwd", grid=(lp // tm,),
        in_specs=[row(d), pl.BlockSpec((1, d), lambda i: (0, 0)),
                  pl.BlockSpec((nck, d, ps), lambda i: (0, 0, 0)), row(HEAD_DIM), row(HEAD_DIM)],
        out_specs=(row(d), row(rw), row(rw), row(rw), row(rw), row(sw)),
        out_shape=(jax.ShapeDtypeStruct((lp, d), BF16),
                   jax.ShapeDtypeStruct((lp, rw), BF16),
                   jax.ShapeDtypeStruct((lp, rw), BF16),
                   jax.ShapeDtypeStruct((lp, rw), BF16),
                   jax.ShapeDtypeStruct((lp, rw), F32),
                   jax.ShapeDtypeStruct((lp, sw), F32)),
        scratch_shapes=[pltpu.VMEM((tm, proj), F32)],
        compiler_params=_params(("arbitrary",)),
    )(h, nw, w_in, cosf, sinf)


def _inproj_bwd(dh, h, nw, n, w_in, dq, dk, dv, dg, du):
    lp, d = h.shape
    nck, _, ps = w_in.shape
    rw = dq.shape[1]
    sw = du.shape[1]
    proj = nck * ps
    tm = _tile(lp, 640)
    last = lp // tm - 1

    def gather_dproj(p_sc, dq_ref, dk_ref, dv_ref, dg_ref, du_ref):
        p_sc[:, 0:rw] = dq_ref[...]
        p_sc[:, rw:2 * rw] = dk_ref[...]
        p_sc[:, 2 * rw:3 * rw] = dv_ref[...]
        p_sc[:, 3 * rw:4 * rw] = dg_ref[...]
        p_sc[:, 4 * rw:] = du_ref[...]

    def act_body(dh_ref, h_ref, nw_ref, w_ref, dq_ref, dk_ref, dv_ref, dg_ref, du_ref, dhi_ref, dnw_ref, p_sc):
        i = pl.program_id(0)

        @pl.when(i == 0)
        def _():
            dnw_ref[...] = jnp.zeros_like(dnw_ref)

        gather_dproj(p_sc, dq_ref, dk_ref, dv_ref, dg_ref, du_ref)
        dn = jnp.zeros((tm, d), F32)
        for c in range(nck):
            dn = dn + _dot_nt(p_sc[:, c * ps:(c + 1) * ps], w_ref[c])
        xh, r = _rms_stats(h_ref[...])
        dhi_ref[...] = dh_ref[...] + _rms_bwd(dn, xh, r, nw_ref[...])
        dnw_ref[...] += jnp.sum(dn * xh, axis=0, keepdims=True)

    def w_body(n_ref, dq_ref, dk_ref, dv_ref, dg_ref, du_ref, dw_ref, p_sc, acc_sc):
        i = pl.program_id(0)

        @pl.when(i == 0)
        def _():
            acc_sc[...] = jnp.zeros_like(acc_sc)

        gather_dproj(p_sc, dq_ref, dk_ref, dv_ref, dg_ref, du_ref)
        nv = n_ref[...]
        for c in range(nck):
            acc_sc[c] += _dot_tn(nv, p_sc[:, c * ps:(c + 1) * ps])

        @pl.when(i == last)
        def _():
            dw_ref[...] = acc_sc[...].astype(BF16)

    row = lambda w: pl.BlockSpec((tm, w), lambda i: (i, 0))
    vec = pl.BlockSpec((1, d), lambda i: (0, 0))
    wsp = pl.BlockSpec((nck, d, ps), lambda i: (0, 0, 0))
    dproj_specs = [row(rw), row(rw), row(rw), row(rw), row(sw)]
    dhi, dnw = pl.pallas_call(
        act_body, name="inproj_bwd_act", grid=(lp // tm,),
        in_specs=[row(d), row(d), vec, wsp] + dproj_specs,
        out_specs=(row(d), vec),
        out_shape=(jax.ShapeDtypeStruct((lp, d), F32), jax.ShapeDtypeStruct((1, d), F32)),
        scratch_shapes=[pltpu.VMEM((tm, proj), BF16)],
        compiler_params=_params(("arbitrary",)),
    )(dh, h, nw, w_in, dq, dk, dv, dg, du)
    dw = pl.pallas_call(
        w_body, name="inproj_bwd_w", grid=(lp // tm,),
        in_specs=[row(d)] + dproj_specs,
        out_specs=wsp, out_shape=jax.ShapeDtypeStruct((nck, d, ps), BF16),
        scratch_shapes=[pltpu.VMEM((tm, proj), BF16), pltpu.VMEM((nck, d, ps), F32)],
        compiler_params=_params(("arbitrary",)),
    )(n, dq, dk, dv, dg, du)
    return dhi, dnw, dw


def _retention_tables():
    h = jnp.arange(RET_HEADS, dtype=F32)
    log_g = jnp.log(1.0 - 2.0 ** (-5.0 - h))
    i = jnp.arange(CHUNK)
    diff = i[:, None] - i[None, :]
    dec = jnp.where(diff[None] >= 0,
                    jnp.exp(log_g[:, None, None] * jnp.maximum(diff, 0)[None].astype(F32)), 0.0)
    pos = jnp.arange(CHUNK, dtype=F32)
    wq = jnp.exp(log_g[:, None] * (pos + 1.0)[None])
    wk = jnp.exp(log_g[:, None] * (CHUNK - 1 - pos)[None])
    gch = jnp.exp(log_g * CHUNK)
    ones = jnp.ones((1, 1, HEAD_DIM), F32)
    return (dec, wq[:, :, None] * ones, wk[:, :, None] * ones,
            gch[:, None, None] * jnp.ones((1, 8, HEAD_DIM), F32))


def _head_norm(o):
    mu = jnp.mean(o, axis=-1, keepdims=True)
    oc = o - mu
    r = lax.rsqrt(jnp.mean(oc * oc, axis=-1, keepdims=True) + EPS)
    return oc * r, r


def _ret_fwd(q, k, v, g, rnw, tables):
    lp, rw = q.shape
    heads = rw // HEAD_DIM
    nch = lp // CHUNK
    dec, wq, wk, gch = tables

    def body(q_ref, k_ref, v_ref, g_ref, w_ref, dec_ref, wq_ref, wk_ref, gch_ref,
             o_ref, ret_ref, sp_ref, s_sc):
        n = pl.program_id(0)

        @pl.when(n == 0)
        def _():
            s_sc[...] = jnp.zeros_like(s_sc)

        for hh in range(heads):
            cs = slice(hh * HEAD_DIM, (hh + 1) * HEAD_DIM)
            qv, kv, vv = q_ref[:, cs], k_ref[:, cs], v_ref[:, cs]
            s_in = s_sc[hh]
            a = _dot_nt(qv, kv) * dec_ref[hh]
            qw = (qv.astype(F32) * wq_ref[hh]).astype(BF16)
            kw = (kv.astype(F32) * wk_ref[hh]).astype(BF16)
            o = _dot(a.astype(BF16), vv) + _dot(qw, s_in.astype(BF16))
            sp_ref[hh, 0] = s_in
            s_sc[hh] = gch_ref[hh, 0:1, :] * s_in + _dot_tn(kw, vv)
            o_ref[:, cs] = o
            xh, _ = _head_norm(o)
            gv = g_ref[:, cs]
            ret_ref[:, cs] = (gv * _sigmoid(gv) * (xh * w_ref[:, cs])).astype(BF16)

    blk = pl.BlockSpec((CHUNK, rw), lambda n: (n, 0))
    tab = pl.BlockSpec((heads, CHUNK, HEAD_DIM), lambda n: (0, 0, 0))
    return pl.pallas_call(
        body, name="retention_fwd", grid=(nch,),
        in_specs=[blk, blk, blk, blk, pl.BlockSpec((1, rw), lambda n: (0, 0)),
                  tab, tab, tab, pl.BlockSpec((heads, 8, HEAD_DIM), lambda n: (0, 0, 0))],
        out_specs=(blk, blk, pl.BlockSpec((heads, 1, HEAD_DIM, HEAD_DIM), lambda n: (0, n, 0, 0))),
        out_shape=(jax.ShapeDtypeStruct((lp, rw), F32),
                   jax.ShapeDtypeStruct((lp, rw), BF16),
                   jax.ShapeDtypeStruct((heads, nch, HEAD_DIM, HEAD_DIM), F32)),
        scratch_shapes=[pltpu.VMEM((heads, HEAD_DIM, HEAD_DIM), F32)],
        compiler_params=_params(("arbitrary",)),
    )(q, k, v, g, rnw, dec, wq, wk, gch)


def _ret_bwd(dret, q, k, v, g, o, sprev, rnw, tables, cosf, sinf):
    lp, rw = q.shape
    heads = rw // HEAD_DIM
    nch = lp // CHUNK
    dec, wq, wk, gch = tables
    scale = HEAD_DIM ** -0.5
    half = HEAD_DIM // 2

    def body(dret_ref, q_ref, k_ref, v_ref, g_ref, o_ref, sp_ref, w_ref, dec_ref, wq_ref, wk_ref, gch_ref,
             cos_ref, sin_ref, dq_ref, dk_ref, dv_ref, dg_ref, dw_ref, ds_sc):
        n = pl.program_id(0)

        @pl.when(n == 0)
        def _():
            ds_sc[...] = jnp.zeros_like(ds_sc)
            dw_ref[...] = jnp.zeros_like(dw_ref)

        cosv = cos_ref[...]
        sinv = sin_ref[...]
        for hh in range(heads):
            cs = slice(hh * HEAD_DIM, (hh + 1) * HEAD_DIM)
            qv, kv, vv = q_ref[:, cs], k_ref[:, cs], v_ref[:, cs]
            gv = g_ref[:, cs]
            dr = dret_ref[:, cs]
            w = w_ref[:, cs]
            sg = _sigmoid(gv)
            sil = gv * sg
            xh, r = _head_norm(o_ref[:, cs])
            dg_ref[:, cs] = (dr * (xh * w) * (sg * (1.0 + gv * (1.0 - sg)))).astype(BF16)
            dyw = dr * sil
            dw_ref[:, cs] += jnp.sum(dyw * xh, axis=0, keepdims=True)
            dxh = dyw * w
            do = r * (dxh - jnp.mean(dxh, axis=-1, keepdims=True)
                      - xh * jnp.mean(dxh * xh, axis=-1, keepdims=True))
            dob = do.astype(BF16)
            dmask = dec_ref[hh]
            wqv = wq_ref[hh]
            wkv = wk_ref[hh]
            a = (_dot_nt(qv, kv) * dmask).astype(BF16)
            da = (_dot_nt(dob, vv) * dmask).astype(BF16)
            qw = (qv.astype(F32) * wqv).astype(BF16)
            kw = (kv.astype(F32) * wkv).astype(BF16)
            s_in = sp_ref[hh, 0].astype(BF16)
            ds = ds_sc[hh]
            dsb = ds.astype(BF16)
            dq = _dot(da, kv) + _dot_nt(dob, s_in) * wqv
            dk = _dot_tn(da, qv) + _dot_nt(vv, dsb) * wkv
            dv = _dot_tn(a, dob) + _dot(kw, dsb)
            ds_sc[hh] = gch_ref[hh, 0:1, :] * ds + _dot_tn(qw, dob)
            dq_ref[:, cs] = (dq * cosv + pltpu.roll(dq * sinv, half, 1)).astype(BF16)
            dk_ref[:, cs] = ((dk * cosv + pltpu.roll(dk * sinv, half, 1)) * scale).astype(BF16)
            dv_ref[:, cs] = dv.astype(BF16)

    blk = pl.BlockSpec((CHUNK, rw), lambda n: (nch - 1 - n, 0))
    tab = pl.BlockSpec((heads, CHUNK, HEAD_DIM), lambda n: (0, 0, 0))
    wsp = pl.BlockSpec((1, rw), lambda n: (0, 0))
    pos = pl.BlockSpec((CHUNK, HEAD_DIM), lambda n: (nch - 1 - n, 0))
    bshape = jax.ShapeDtypeStruct((lp, rw), BF16)
    return pl.pallas_call(
        body, name="retention_bwd", grid=(nch,),
        in_specs=[blk, blk, blk, blk, blk, blk,
                  pl.BlockSpec((heads, 1, HEAD_DIM, HEAD_DIM), lambda n: (0, nch - 1 - n, 0, 0)),
                  wsp, tab, tab, tab, pl.BlockSpec((heads, 8, HEAD_DIM), lambda n: (0, 0, 0)), pos, pos],
        out_specs=(blk, blk, blk, blk, wsp),
        out_shape=(bshape, bshape, bshape, bshape, jax.ShapeDtypeStruct((1, rw), F32)),
        scratch_shapes=[pltpu.VMEM((heads, HEAD_DIM, HEAD_DIM), F32)],
        compiler_params=_params(("arbitrary",)),
    )(dret, q, k, v, g, o, sprev, rnw, dec, wq, wk, gch, cosf, sinf)


SCAN_CW = 512


def _s5_prepare(lam_re, lam_im, log_dt, b_re, b_im):
    dt = jnp.exp(log_dt)[:, None]
    er = jnp.exp(lam_re * dt)
    ar = er * jnp.cos(lam_im * dt)
    ai = er * jnp.sin(lam_im * dt)
    den = lam_re * lam_re + lam_im * lam_im
    fr = ((ar - 1.0) * lam_re + ai * lam_im) / den
    fi = (ai * lam_re - (ar - 1.0) * lam_im) / den
    bbr = fr[..., None] * b_re - fi[..., None] * b_im
    bbi = fr[..., None] * b_im + fi[..., None] * b_re
    return ar, ai, bbr, bbi


def _blockdiag_in(t):
    g, p, n = t.shape
    gs = g // N_SEC
    t = t.reshape(N_SEC, gs, p, n)
    eye = jnp.eye(gs, dtype=t.dtype)
    return jnp.einsum("sgpn,gh->sgphn", t, eye).reshape(N_SEC, gs * p, gs * n)


def _blockdiag_out(m, g, p, n):
    gs = g // N_SEC
    m = m.reshape(N_SEC, gs, p, gs, n)
    eye = jnp.eye(gs, dtype=m.dtype)
    return jnp.einsum("sgphn,gh->sgpn", m, eye).reshape(g, p, n)


def _scan_step(xr_ref, xi_ref, r0, pr_of, ar_ref, ai_ref, conj, ncols):
    for cc in range(ncols // SCAN_CW):
        cs = pl.ds(cc * SCAN_CW, SCAN_CW)
        pr, pi = pr_of(cs)
        ar = ar_ref[:, cs]
        ai = ai_ref[:, cs]
        if conj:
            nr = ar * pr + ai * pi
            ni = ar * pi - ai * pr
        else:
            nr = ar * pr - ai * pi
            ni = ar * pi + ai * pr
        xr_ref[pl.ds(r0, 8), cs] = xr_ref[pl.ds(r0, 8), cs] + nr
        xi_ref[pl.ds(r0, 8), cs] = xi_ref[pl.ds(r0, 8), cs] + ni


def _shift_rows(z, down):
    row = lax.broadcasted_iota(jnp.int32, z.shape, 0)
    if down:
        return jnp.where(row == 0, 0.0, pltpu.roll(z, 1, 0))
    return jnp.where(row == N_SEG - 1, 0.0, pltpu.roll(z, N_SEG - 1, 0))


def _s5_fwd(u, bsr, bsi, csr, csi, a8r, a8i, al8r, al8i, d, gluw, glub, nw, jb):
    lp, sw = u.shape
    ns = a8r.shape[1]
    rows = N_SEG * jb
    nblk = lp // rows
    secw = sw // N_SEC
    secn = ns // N_SEC

    def local_scan(u_ref, bsr_ref, bsi_ref, ar_ref, ai_ref, xr_ref, xi_ref, pr_sc, pi_sc):
        for s in range(N_SEC):
            ub = u_ref[:, s * secw:(s + 1) * secw].astype(BF16)
            xr_ref[:, s * secn:(s + 1) * secn] = _dot(ub, bsr_ref[s])
            xi_ref[:, s * secn:(s + 1) * secn] = _dot(ub, bsi_ref[s])
        _scan_step(xr_ref, xi_ref, 0, lambda cs: (pr_sc[:, cs], pi_sc[:, cs]), ar_ref, ai_ref, False, ns)

        def step(j, carry):
            r0 = pl.multiple_of(j * 8, 8)
            rp = pl.multiple_of((j - 1) * 8, 8)
            _scan_step(xr_ref, xi_ref, r0,
                       lambda cs: (xr_ref[pl.ds(rp, 8), cs], xi_ref[pl.ds(rp, 8), cs]),
                       ar_ref, ai_ref, False, ns)
            return carry

        lax.fori_loop(1, jb, step, 0)
        pr_sc[...] = xr_ref[rows - 8:rows, :]
        pi_sc[...] = xi_ref[rows - 8:rows, :]

    def carry_body(u_ref, bsr_ref, bsi_ref, ar_ref, ai_ref, alr_ref, ali_ref, cr_ref, ci_ref,
                   xr_sc, xi_sc, pr_sc, pi_sc):
        b = pl.program_id(0)

        @pl.when(b == 0)
        def _():
            pr_sc[...] = jnp.zeros_like(pr_sc)
            pi_sc[...] = jnp.zeros_like(pi_sc)

        local_scan(u_ref, bsr_ref, bsi_ref, ar_ref, ai_ref, xr_sc, xi_sc, pr_sc, pi_sc)

        @pl.when(b == nblk - 1)
        def _():
            er = _shift_rows(pr_sc[...], True)
            ei = _shift_rows(pi_sc[...], True)
            alr, ali = alr_ref[...], ali_ref[...]
            cr, ci = er, ei
            for _ in range(N_SEG - 2):
                sr = _shift_rows(cr, True)
                si = _shift_rows(ci, True)
                cr = er + alr * sr - ali * si
                ci = ei + alr * si + ali * sr
            cr_ref[...] = cr
            ci_ref[...] = ci

    ublk = pl.BlockSpec((rows, sw), lambda b: (b, 0))
    bspec = pl.BlockSpec((N_SEC, secw, secn), lambda b: (0, 0, 0))
    cspec = pl.BlockSpec((N_SEC, secn, secw), lambda b: (0, 0, 0))
    s8 = pl.BlockSpec((N_SEG, ns), lambda b: (0, 0))
    vec = pl.BlockSpec((1, sw), lambda b: (0, 0))
    s8shape = jax.ShapeDtypeStruct((N_SEG, ns), F32)
    c0r, c0i = pl.pallas_call(
        carry_body, name="s5_fwd_carry", grid=(nblk,),
        in_specs=[ublk, bspec, bspec, s8, s8, s8, s8],
        out_specs=(s8, s8), out_shape=(s8shape, s8shape),
        scratch_shapes=[pltpu.VMEM((rows, ns), F32), pltpu.VMEM((rows, ns), F32),
                        pltpu.VMEM((N_SEG, ns), F32), pltpu.VMEM((N_SEG, ns), F32)],
        compiler_params=_params(("arbitrary",)),
    )(u, bsr, bsi, a8r, a8i, al8r, al8i)

    def main_body(u_ref, bsr_ref, bsi_ref, csr_ref, csi_ref, ar_ref, ai_ref, c0r_ref, c0i_ref,
                  d_ref, gw_ref, gb_ref, nw_ref, xr_ref, xi_ref, yp_ref, out_ref, pr_sc, pi_sc):
        b = pl.program_id(0)

        @pl.when(b == 0)
        def _():
            pr_sc[...] = c0r_ref[...]
            pi_sc[...] = c0i_ref[...]

        local_scan(u_ref, bsr_ref, bsi_ref, ar_ref, ai_ref, xr_ref, xi_ref, pr_sc, pi_sc)
        for s in range(N_SEC):
            xs = pl.ds(s * secn, secn)
            us = pl.ds(s * secw, secw)
            y = _dot(xr_ref[:, xs].astype(BF16), csr_ref[s]) + _dot(xi_ref[:, xs].astype(BF16), csi_ref[s])
            yp_ref[:, us] = y + d_ref[:, us] * u_ref[:, us]
        yp = yp_ref[...]
        t = jnp.tanh(GELU_K0 * (yp + GELU_K1 * yp * yp * yp))
        y1 = 0.5 * yp * (1.0 + t)
        z = _dot(y1.astype(BF16), gw_ref[...]) + gb_ref[...]
        y2 = y1 * _sigmoid(z)
        xh, _ = _rms_stats(y2)
        out_ref[...] = (xh * nw_ref[...]).astype(BF16)

    xblk = pl.BlockSpec((rows, ns), lambda b: (b, 0))
    xr, xi, yp, out = pl.pallas_call(
        main_body, name="s5_fwd", grid=(nblk,),
        in_specs=[ublk, bspec, bspec, cspec, cspec, s8, s8, s8, s8, vec,
                  pl.BlockSpec((sw, sw), lambda b: (0, 0)), vec, vec],
        out_specs=(xblk, xblk, ublk, ublk),
        out_shape=(jax.ShapeDtypeStruct((lp, ns), F32), jax.ShapeDtypeStruct((lp, ns), F32),
                   jax.ShapeDtypeStruct((lp, sw), F32), jax.ShapeDtypeStruct((lp, sw), BF16)),
        scratch_shapes=[pltpu.VMEM((N_SEG, ns), F32), pltpu.VMEM((N_SEG, ns), F32)],
        compiler_params=_params(("arbitrary",)),
    )(u, bsr, bsi, csr, csi, a8r, a8i, c0r, c0i, d, gluw, glub, nw)
    return xr, xi, c0r, c0i, yp, out


def _s5_bwd(dout, u, yp, xr, xi, c0r, c0i, bsrt, bsit, csrt, csit, a8r, a8i, al8r, al8i, d, gluw, glub, nw, jb):
    lp, sw = u.shape
    ns = a8r.shape[1]
    rows = N_SEG * jb
    nblk = lp // rows
    secw = sw // N_SEC
    secn = ns // N_SEC

    def rowwise_bwd(dout_ref, yp_ref, gw_ref, gb_ref, nw_ref):
        ypv = yp_ref[...]
        t = jnp.tanh(GELU_K0 * (ypv + GELU_K1 * ypv * ypv * ypv))
        y1 = 0.5 * ypv * (1.0 + t)
        dgelu = 0.5 * (1.0 + t) + 0.5 * ypv * (1.0 - t * t) * GELU_K0 * (1.0 + 3.0 * GELU_K1 * ypv * ypv)
        gw = gw_ref[...]
        y1b = y1.astype(BF16)
        sg = _sigmoid(_dot(y1b, gw) + gb_ref[...])
        xh, r = _rms_stats(y1 * sg)
        dov = dout_ref[...]
        dy2 = _rms_bwd(dov, xh, r, nw_ref[...])
        dz = dy2 * y1 * sg * (1.0 - sg)
        dzb = dz.astype(BF16)
        dy1 = dy2 * sg + _dot_nt(dzb, gw)
        return dy1 * dgelu, dov * xh, y1b, dzb, dz

    def lam_scan(dyp_of, csrt_ref, csit_ref, ar_ref, ai_ref, lr_sc, li_sc, nr_sc, ni_sc, extra):
        for s in range(N_SEC):
            db = dyp_of(s)
            lr_sc[:, s * secn:(s + 1) * secn] = _dot(db, csrt_ref[s])
            li_sc[:, s * secn:(s + 1) * secn] = _dot(db, csit_ref[s])
        top = rows - 8
        _scan_step(lr_sc, li_sc, top, lambda cs: (nr_sc[:, cs], ni_sc[:, cs]), ar_ref, ai_ref, True, ns)
        extra(top, pl.ds(top - 8, 8))

        def step(jj, carry):
            r0 = pl.multiple_of((jb - 1 - jj) * 8, 8)
            rn = pl.multiple_of((jb - jj) * 8, 8)
            rp = pl.multiple_of((jb - 2 - jj) * 8, 8)
            _scan_step(lr_sc, li_sc, r0,
                       lambda cs: (lr_sc[pl.ds(rn, 8), cs], li_sc[pl.ds(rn, 8), cs]),
                       ar_ref, ai_ref, True, ns)
            extra(r0, pl.ds(rp, 8))
            return carry

        lax.fori_loop(1, jb - 1, step, 0)
        _scan_step(lr_sc, li_sc, 0, lambda cs: (lr_sc[8:16, cs], li_sc[8:16, cs]), ar_ref, ai_ref, True, ns)
        extra(0, None)
        nr_sc[...] = lr_sc[0:8, :]
        ni_sc[...] = li_sc[0:8, :]

    def carry_body(dout_ref, yp_ref, u_ref, gw_ref, gb_ref, nw_ref, csrt_ref, csit_ref, ar_ref, ai_ref,
                   alr_ref, ali_ref, cr_ref, ci_ref, dyp_ref, dnw_ref, dgw_ref, dgb_ref, dd_ref,
                   lr_sc, li_sc, nr_sc, ni_sc):
        b = pl.program_id(0)

        @pl.when(b == 0)
        def _():
            nr_sc[...] = jnp.zeros_like(nr_sc)
            ni_sc[...] = jnp.zeros_like(ni_sc)
            for ref in (dnw_ref, dgw_ref, dgb_ref, dd_ref):
                ref[...] = jnp.zeros_like(ref)

        dyp, dnw_rows, y1b, dzb, dz = rowwise_bwd(dout_ref, yp_ref, gw_ref, gb_ref, nw_ref)
        dnw_ref[...] += jnp.sum(dnw_rows, axis=0, keepdims=True)
        dgw_ref[...] += _dot_tn(y1b, dzb)
        dgb_ref[...] += jnp.sum(dz, axis=0, keepdims=True)
        dd_ref[...] += jnp.sum(dyp * u_ref[...], axis=0, keepdims=True)
        dyp_ref[...] = dyp.astype(BF16)
        lam_scan(lambda s: dyp_ref[:, s * secw:(s + 1) * secw], csrt_ref, csit_ref, ar_ref, ai_ref,
                 lr_sc, li_sc, nr_sc, ni_sc, lambda r0, prev_rows: None)

        @pl.when(b == nblk - 1)
        def _():
            fr = _shift_rows(nr_sc[...], False)
            fi = _shift_rows(ni_sc[...], False)
            alr, ali = alr_ref[...], ali_ref[...]
            cr, ci = fr, fi
            for _ in range(N_SEG - 2):
                sr = _shift_rows(cr, False)
                si = _shift_rows(ci, False)
                cr = fr + alr * sr + ali * si
                ci = fi + alr * si - ali * sr
            cr_ref[...] = cr
            ci_ref[...] = ci

    rev = lambda b: (nblk - 1 - b, 0)
    ublk = pl.BlockSpec((rows, sw), rev)
    xblk = pl.BlockSpec((rows, ns), rev)
    s8 = pl.BlockSpec((N_SEG, ns), lambda b: (0, 0))
    vec = pl.BlockSpec((1, sw), lambda b: (0, 0))
    gws = pl.BlockSpec((sw, sw), lambda b: (0, 0))
    btspec = pl.BlockSpec((N_SEC, secn, secw), lambda b: (0, 0, 0))
    ctspec = pl.BlockSpec((N_SEC, secw, secn), lambda b: (0, 0, 0))
    s8shape = jax.ShapeDtypeStruct((N_SEG, ns), F32)
    lcr, lci, dyp_all, d_nw, d_gw, d_gb, d_d = pl.pallas_call(
        carry_body, name="s5_bwd_carry", grid=(nblk,),
        in_specs=[ublk, ublk, ublk, gws, vec, vec, ctspec, ctspec, s8, s8, s8, s8],
        out_specs=(s8, s8, ublk, vec, gws, vec, vec),
        out_shape=(s8shape, s8shape, jax.ShapeDtypeStruct((lp, sw), BF16), jax.ShapeDtypeStruct((1, sw), F32),
                   jax.ShapeDtypeStruct((sw, sw), F32), jax.ShapeDtypeStruct((1, sw), F32),
                   jax.ShapeDtypeStruct((1, sw), F32)),
        scratch_shapes=[pltpu.VMEM((rows, ns), F32), pltpu.VMEM((rows, ns), F32),
                        pltpu.VMEM((N_SEG, ns), F32), pltpu.VMEM((N_SEG, ns), F32)],
        compiler_params=_params(("arbitrary",)),
    )(dout, yp, u, gluw, glub, nw, csrt, csit, a8r, a8i, al8r, al8i)

    def main_body(dyp_sc, u_ref, xr_ref, xi_ref, xtr_ref, xti_ref, c0r_ref, c0i_ref, lcr_ref, lci_ref,
                  d_ref, bsrt_ref, bsit_ref, csrt_ref, csit_ref, ar_ref, ai_ref,
                  du_ref, dcr_ref, dci_ref, dbr_ref, dbi_ref, dar_ref, dai_ref,
                  lr_sc, li_sc, nr_sc, ni_sc):
        b = pl.program_id(0)

        @pl.when(b == 0)
        def _():
            nr_sc[...] = lcr_ref[...]
            ni_sc[...] = lci_ref[...]
            for ref in (dcr_ref, dci_ref, dbr_ref, dbi_ref, dar_ref, dai_ref):
                ref[...] = jnp.zeros_like(ref)

        for s in range(N_SEC):
            db = dyp_sc[:, s * secw:(s + 1) * secw]
            xs = pl.ds(s * secn, secn)
            dcr_ref[s] += _dot_tn(xr_ref[:, xs].astype(BF16), db)
            dci_ref[s] += _dot_tn(xi_ref[:, xs].astype(BF16), db)

        first = b == nblk - 1

        def acc_da(r0, prev_rows):
            for cc in range(ns // SCAN_CW):
                cs = pl.ds(cc * SCAN_CW, SCAN_CW)
                lr = lr_sc[pl.ds(r0, 8), cs]
                li = li_sc[pl.ds(r0, 8), cs]
                if prev_rows is None:
                    xpr = jnp.where(first, c0r_ref[:, cs], xtr_ref[:, cs])
                    xpi = jnp.where(first, c0i_ref[:, cs], xti_ref[:, cs])
                else:
                    xpr = xr_ref[prev_rows, cs]
                    xpi = xi_ref[prev_rows, cs]
                dar_ref[:, cs] += lr * xpr + li * xpi
                dai_ref[:, cs] += li * xpr - lr * xpi

        lam_scan(lambda s: dyp_sc[:, s * secw:(s + 1) * secw], csrt_ref, csit_ref, ar_ref, ai_ref,
                 lr_sc, li_sc, nr_sc, ni_sc, acc_da)

        for s in range(N_SEC):
            xs = pl.ds(s * secn, secn)
            us = pl.ds(s * secw, secw)
            lrb = lr_sc[:, xs].astype(BF16)
            lib = li_sc[:, xs].astype(BF16)
            du = _dot(lrb, bsrt_ref[s]) + _dot(lib, bsit_ref[s]) + d_ref[:, us] * dyp_sc[:, us].astype(F32)
            du_ref[:, us] = du.astype(BF16)
            ub = u_ref[:, us].astype(BF16)
            dbr_ref[s] += _dot_tn(ub, lrb)
            dbi_ref[s] += _dot_tn(ub, lib)

    tail = pl.BlockSpec((N_SEG, ns), lambda b: (jnp.maximum((nblk - 1 - b) * jb - 1, 0), 0))
    acc_c = pl.BlockSpec((N_SEC, secn, secw), lambda b: (0, 0, 0))
    acc_b = pl.BlockSpec((N_SEC, secw, secn), lambda b: (0, 0, 0))
    du, dcr, dci, dbr, dbi, dar, dai = pl.pallas_call(
        main_body, name="s5_bwd", grid=(nblk,),
        in_specs=[ublk, ublk, xblk, xblk, tail, tail, s8, s8, s8, s8,
                  vec, btspec, btspec, ctspec, ctspec, s8, s8],
        out_specs=(ublk, acc_c, acc_c, acc_b, acc_b, s8, s8),
        out_shape=(jax.ShapeDtypeStruct((lp, sw), BF16),
                   jax.ShapeDtypeStruct((N_SEC, secn, secw), F32),
                   jax.ShapeDtypeStruct((N_SEC, secn, secw), F32),
                   jax.ShapeDtypeStruct((N_SEC, secw, secn), F32),
                   jax.ShapeDtypeStruct((N_SEC, secw, secn), F32),
                   s8shape, s8shape),
        scratch_shapes=[pltpu.VMEM((rows, ns), F32), pltpu.VMEM((rows, ns), F32),
                        pltpu.VMEM((N_SEG, ns), F32), pltpu.VMEM((N_SEG, ns), F32)],
        compiler_params=_params(("arbitrary",)),
    )(dyp_all, u, xr, xi, xr, xi, c0r, c0i, lcr, lci, d, bsrt, bsit, csrt, csit, a8r, a8i)
    return du, d_nw, d_gw, d_gb, d_d, dcr, dci, dbr, dbi, dar, dai


def _outproj_fwd(h, ret, ssm, wo):
    lp, d = h.shape
    nck, rs, _ = wo.shape
    rw = ret.shape[1]
    tm = _tile(lp, 640)
    per = rw // rs

    def body(h_ref, ret_ref, ssm_ref, w_ref, o_ref):
        acc = h_ref[...]
        for c in range(nck):
            src = ret_ref if c < per else ssm_ref
            lo = (c % per) * rs
            acc = acc + _dot(src[:, lo:lo + rs], w_ref[c])
        o_ref[...] = acc

    row = lambda w: pl.BlockSpec((tm, w), lambda i: (i, 0))
    return pl.pallas_call(
        body, name="outproj_fwd", grid=(lp // tm,),
        in_specs=[row(d), row(rw), row(ssm.shape[1]), pl.BlockSpec((nck, rs, d), lambda i: (0, 0, 0))],
        out_specs=row(d), out_shape=jax.ShapeDtypeStruct((lp, d), F32),
        compiler_params=_params(("arbitrary",)),
    )(h, ret, ssm, wo)


def _outproj_bwd(dh, ret, ssm, wo):
    lp, d = dh.shape
    nck, rs, _ = wo.shape
    rw = ret.shape[1]
    sw = ssm.shape[1]
    tm = _tile(lp, 640)
    per = rw // rs
    last = lp // tm - 1

    def body(dh_ref, ret_ref, ssm_ref, w_ref, dret_ref, dssm_ref, dw_ref, acc_sc):
        i = pl.program_id(0)

        @pl.when(i == 0)
        def _():
            acc_sc[...] = jnp.zeros_like(acc_sc)

        dhb = dh_ref[...].astype(BF16)
        for c in range(nck):
            src, dst = (ret_ref, dret_ref) if c < per else (ssm_ref, dssm_ref)
            lo = (c % per) * rs
            dst[:, lo:lo + rs] = _dot_nt(dhb, w_ref[c])
            acc_sc[c] += _dot_tn(src[:, lo:lo + rs], dhb)

        @pl.when(i == last)
        def _():
            dw_ref[...] = acc_sc[...].astype(BF16)

    row = lambda w: pl.BlockSpec((tm, w), lambda i: (i, 0))
    wsp = pl.BlockSpec((nck, rs, d), lambda i: (0, 0, 0))
    return pl.pallas_call(
        body, name="outproj_bwd", grid=(lp // tm,),
        in_specs=[row(d), row(rw), row(sw), wsp],
        out_specs=(row(rw), row(sw), wsp),
        out_shape=(jax.ShapeDtypeStruct((lp, rw), F32), jax.ShapeDtypeStruct((lp, sw), F32),
                   jax.ShapeDtypeStruct((nck, rs, d), BF16)),
        scratch_shapes=[pltpu.VMEM((nck, rs, d), F32)],
        compiler_params=_params(("arbitrary",)),
    )(dh, ret, ssm, wo)


def _loss_head(h, fw, target):
    lp, d = h.shape
    tm = _tile(lp, 640, CHUNK)
    sub = tm // CHUNK

    def body(h_ref, w_ref, *rest):
        t_refs = rest[:sub]
        loss_ref, dh_ref, dw_ref = rest[sub:]
        i = pl.program_id(0)

        @pl.when(i == 0)
        def _():
            loss_ref[...] = jnp.zeros_like(loss_ref)
            dw_ref[...] = jnp.zeros_like(dw_ref)

        w = w_ref[...]
        for j in range(sub):
            rows = pl.ds(j * CHUNK, CHUNK)
            xh, r = _rms_stats(h_ref[rows, :])
            err = xh * w - t_refs[j][...]
            if j == 0:
                err = jnp.where(i == 0, 0.0, err)
            loss_ref[...] += 0.5 * jnp.sum(err * err) / d
            dout = err * (1.0 / d)
            dw_ref[...] += jnp.sum(dout * xh, axis=0, keepdims=True)
            dh_ref[rows, :] = _rms_bwd(dout, xh, r, w)

    t_spec = lambda j: pl.BlockSpec((CHUNK, d), lambda i: (jnp.maximum(i * sub + j - 1, 0), 0))
    return pl.pallas_call(
        body, name="loss_head", grid=(lp // tm,),
        in_specs=[pl.BlockSpec((tm, d), lambda i: (i, 0)), pl.BlockSpec((1, d), lambda i: (0, 0))]
        + [t_spec(j) for j in range(sub)],
        out_specs=(pl.BlockSpec((8, LANE), lambda i: (0, 0)), pl.BlockSpec((tm, d), lambda i: (i, 0)),
                   pl.BlockSpec((1, d), lambda i: (0, 0))),
        out_shape=(jax.ShapeDtypeStruct((8, LANE), F32), jax.ShapeDtypeStruct((lp, d), F32),
                   jax.ShapeDtypeStruct((1, d), F32)),
        compiler_params=_params(("arbitrary",)),
    )(h, fw, *([target] * sub))


def _pack(arrs):
    flat = jnp.concatenate([a.reshape(-1).astype(F32) for a in arrs])
    n = flat.shape[0]
    rows = -(-n // (8 * LANE)) * 8
    return jnp.pad(flat, (0, rows * LANE - n)).reshape(rows, LANE)


def _unpack(packed, shapes):
    flat = packed.reshape(-1)
    out, off = [], 0
    for s in shapes:
        n = math.prod(s)
        out.append(flat[off:off + n].reshape(s))
        off += n
    return out


def _to_segments(a, seg_len):
    return a.reshape(N_SEG, seg_len, a.shape[1]).transpose(1, 0, 2).reshape(a.shape)


def _from_segments(a, seg_len):
    return a.reshape(seg_len, N_SEG, a.shape[1]).transpose(1, 0, 2).reshape(a.shape)


WEIGHT_NAMES = ['meta_tokens', 'ffn1_norm_w', 'ffn1_w_gate', 'ffn1_w_up', 'ffn1_w_down', 'mix_norm_w', 'w_in',
                'ret_norm_w', 'ssm_lambda_re', 'ssm_lambda_im', 'ssm_log_dt', 'ssm_b_re', 'ssm_b_im', 'ssm_c_re',
                'ssm_c_im', 'ssm_d', 'ssm_glu_w', 'ssm_glu_b', 'ssm_norm_w', 'w_out', 'ffn2_norm_w', 'ffn2_w_gate',
                'ffn2_w_up', 'ffn2_w_down', 'final_norm_w']
BIG = ['ffn1_w_gate', 'ffn1_w_up', 'ffn1_w_down', 'w_in', 'ssm_glu_w', 'w_out', 'ffn2_w_gate', 'ffn2_w_up',
       'ffn2_w_down']
TRANSPOSED = ['ffn1_w_gate', 'ffn1_w_up', 'ffn2_w_gate', 'ffn2_w_up']
BIG_EARLY = ['ffn1_w_gate', 'ffn1_w_up', 'ffn1_w_down']
BIG_LATE = [n for n in BIG if n not in BIG_EARLY]
SMALL = [n for n in WEIGHT_NAMES if n not in BIG]


def kernel(x, meta_tokens, ffn1_norm_w, ffn1_w_gate, ffn1_w_up, ffn1_w_down, mix_norm_w, w_in, ret_norm_w, ssm_lambda_re, ssm_lambda_im, ssm_log_dt, ssm_b_re, ssm_b_im, ssm_c_re, ssm_c_im, ssm_d, ssm_glu_w, ssm_glu_b, ssm_norm_w, w_out, ffn2_norm_w, ffn2_w_gate, ffn2_w_up, ffn2_w_down, final_norm_w, loss_target, m_meta_tokens, m_ffn1_norm_w, m_ffn1_w_gate, m_ffn1_w_up, m_ffn1_w_down, m_mix_norm_w, m_w_in, m_ret_norm_w, m_ssm_lambda_re, m_ssm_lambda_im, m_ssm_log_dt, m_ssm_b_re, m_ssm_b_im, m_ssm_c_re, m_ssm_c_im, m_ssm_d, m_ssm_glu_w, m_ssm_glu_b, m_ssm_norm_w, m_w_out, m_ffn2_norm_w, m_ffn2_w_gate, m_ffn2_w_up, m_ffn2_w_down, m_final_norm_w, v_meta_tokens, v_ffn1_norm_w, v_ffn1_w_gate, v_ffn1_w_up, v_ffn1_w_down, v_mix_norm_w, v_w_in, v_ret_norm_w, v_ssm_lambda_re, v_ssm_lambda_im, v_ssm_log_dt, v_ssm_b_re, v_ssm_b_im, v_ssm_c_re, v_ssm_c_im, v_ssm_d, v_ssm_glu_w, v_ssm_glu_b, v_ssm_norm_w, v_w_out, v_ffn2_norm_w, v_ffn2_w_gate, v_ffn2_w_up, v_ffn2_w_down, v_final_norm_w):
    args = locals()
    w = {n: args[n] for n in WEIGHT_NAMES}
    m = {n: args["m_" + n] for n in WEIGHT_NAMES}
    v = {n: args["v_" + n] for n in WEIGHT_NAMES}

    seq, d = x.shape[1], x.shape[2]
    lp = seq + CHUNK
    seg_len = lp // N_SEG
    rw = RET_HEADS * HEAD_DIM
    sw = ssm_d.shape[-1]
    groups = sw // SSM_GROUP
    ns = groups * SSM_STATE
    jb = _tile(seg_len, 40, 8)
    chip = 2 * lax.axis_index("x") + lax.axis_index("y")

    as_fd = lambda t: jnp.swapaxes(t, -1, -2)
    shards = {n: (as_fd(w[n][0]) if n in TRANSPOSED else w[n][0]).astype(BF16) for n in BIG}
    early = [shards[n] for n in BIG_EARLY] + [meta_tokens]
    gathered = _forward_sibling("gather_early_forward",
                                _exchange("gather_early", _allgather_chips_plan(early), early))
    gw = dict(zip(BIG_EARLY, gathered[:-1]))
    meta_full = jnp.transpose(gathered[-1], (1, 0, 2)).reshape(N_META, d)
    late = [shards[n] for n in BIG_LATE]

    pos = jnp.arange(lp, dtype=F32) - float(CHUNK - N_META)
    freqs = 1.0 / (ROPE_BASE ** (jnp.arange(0, HEAD_DIM, 2, dtype=F32) / HEAD_DIM))
    ang = pos[:, None] * freqs[None, :]
    cosf = jnp.concatenate([jnp.cos(ang), jnp.cos(ang)], axis=1)
    sinf = jnp.concatenate([-jnp.sin(ang), jnp.sin(ang)], axis=1)
    tables = _retention_tables()

    lam_re, lam_im, log_dt = ssm_lambda_re[0], ssm_lambda_im[0], ssm_log_dt[0]
    b_re, b_im, c_re, c_im = ssm_b_re[0], ssm_b_im[0], ssm_c_re[0], ssm_c_im[0]
    (ar, ai, bbr, bbi), prep_vjp = jax.vjp(_s5_prepare, lam_re, lam_im, log_dt, b_re, b_im)
    dt = jnp.exp(log_dt)[:, None]
    el = jnp.exp(seg_len * lam_re * dt)
    alr = el * jnp.cos(seg_len * lam_im * dt)
    ali = el * jnp.sin(seg_len * lam_im * dt)
    bc8 = lambda t: jnp.broadcast_to(t.reshape(1, ns), (N_SEG, ns))
    a8r, a8i, al8r, al8i = bc8(ar), bc8(ai), bc8(alr), bc8(ali)
    bsr = _blockdiag_in(jnp.transpose(bbr, (0, 2, 1)))
    bsi = _blockdiag_in(jnp.transpose(bbi, (0, 2, 1)))
    csrt = _blockdiag_in(c_re)
    csit = _blockdiag_in(-c_im)
    tr = lambda t: jnp.transpose(t, (0, 2, 1))
    bsr_b, bsi_b = bsr.astype(BF16), bsi.astype(BF16)
    csr_b, csi_b = tr(csrt).astype(BF16), tr(csit).astype(BF16)
    bsrt_b, bsit_b = tr(bsr).astype(BF16), tr(bsi).astype(BF16)
    csrt_b, csit_b = csrt.astype(BF16), csit.astype(BF16)

    h0 = jnp.concatenate([jnp.zeros((CHUNK - N_META, d), F32), meta_full, x[0]], axis=0)
    (h1, g1, u1), late_half = _ffn_fwd("ffn1_fwd", h0, ffn1_norm_w, gw['ffn1_w_gate'], gw['ffn1_w_up'],
                                       gw['ffn1_w_down'], _allgather_chips_plan(late), late)
    gw.update(zip(BIG_LATE, _forward_sibling("gather_late_forward", late_half)))
    glu_full = gw['ssm_glu_w'].reshape(sw, sw)
    n2, q, k, vv, gate, u = _inproj_fwd(h1, mix_norm_w, gw['w_in'], cosf, sinf, rw)
    o, ret, sprev = _ret_fwd(q, k, vv, gate, ret_norm_w, tables)
    u_seg = _to_segments(u, seg_len)
    xr, xi, c0r, c0i, yp, ssm_seg = _s5_fwd(u_seg, bsr_b, bsi_b, csr_b, csi_b, a8r, a8i, al8r, al8i,
                                            ssm_d, glu_full, ssm_glu_b, ssm_norm_w, jb)
    ssm = _from_segments(ssm_seg, seg_len)
    h2 = _outproj_fwd(h1, ret, ssm, gw['w_out'])
    (h3, g2, u2), _ = _ffn_fwd("ffn2_fwd", h2, ffn2_norm_w, gw['ffn2_w_gate'], gw['ffn2_w_up'], gw['ffn2_w_down'])
    loss_part, dh3, d_final = _loss_head(h3, final_norm_w.reshape(1, d), loss_target[0])

    (dh2, d_ffn2_norm, nb, daccb, ab, dgb, dub), _ = _ffn_bwd_act(
        "ffn2_bwd_act", dh3, h2, ffn2_norm_w, g2, u2, gw['ffn2_w_gate'], gw['ffn2_w_up'], gw['ffn2_w_down'])
    dwg2, dwu2, dwd2 = _ffn_bwd_w("ffn2_bwd_w", nb, daccb, ab, dgb, dub)
    dret, dssm, dwo = _outproj_bwd(dh2, ret, ssm, gw['w_out'])
    (du_seg, d_ssm_norm, d_glu_w, d_glu_b, d_ssm_d, dcr_s, dci_s, dbr_s, dbi_s, dar8, dai8) = _s5_bwd(
        _to_segments(dssm, seg_len), u_seg, yp, xr, xi, c0r, c0i, bsrt_b, bsit_b, csrt_b, csit_b,
        a8r, a8i, al8r, al8i, ssm_d, glu_full, ssm_glu_b, ssm_norm_w, jb)
    du = _from_segments(du_seg, seg_len)
    dq, dk, dv, dgate, d_ret_norm = _ret_bwd(dret, q, k, vv, gate, o, sprev, ret_norm_w, tables, cosf, sinf)
    dh1, d_mix_norm, dwin = _inproj_bwd(dh2, h1, mix_norm_w, n2, gw['w_in'], dq, dk, dv, dgate, du)
    late_parts = {
        'w_in': dwin, 'ssm_glu_w': d_glu_w.reshape(N_CHIP, sw // N_CHIP, sw).astype(BF16), 'w_out': dwo,
        'ffn2_w_gate': dwg2, 'ffn2_w_up': dwu2, 'ffn2_w_down': dwd2,
    }
    late_list = [late_parts[n] for n in BIG_LATE]
    (dh0, d_ffn1_norm, nb, daccb, ab, dgb, dub), late_recv = _ffn_bwd_act(
        "ffn1_bwd_act", dh1, h0, ffn1_norm_w, g1, u1, gw['ffn1_w_gate'], gw['ffn1_w_up'], gw['ffn1_w_down'],
        _alltoall_chips_plan(late_list), late_list)
    grad_x = dh0[CHUNK:][None]
    d_meta = dh0[CHUNK - N_META:CHUNK]

    d_c_re = jnp.transpose(_blockdiag_out(tr(dcr_s), groups, SSM_GROUP, SSM_STATE), (0, 1, 2))
    d_c_im = -_blockdiag_out(tr(dci_s), groups, SSM_GROUP, SSM_STATE)
    d_bbr = jnp.transpose(_blockdiag_out(dbr_s, groups, SSM_GROUP, SSM_STATE), (0, 2, 1))
    d_bbi = jnp.transpose(_blockdiag_out(dbi_s, groups, SSM_GROUP, SSM_STATE), (0, 2, 1))
    d_ar = jnp.sum(dar8, axis=0).reshape(groups, SSM_STATE)
    d_ai = jnp.sum(dai8, axis=0).reshape(groups, SSM_STATE)
    small_parts = [loss_part[0:1, :], d_meta, d_ffn1_norm, d_mix_norm, d_ret_norm, d_ar, d_ai, d_bbr, d_bbi,
                   d_c_re, d_c_im, d_ssm_d, d_glu_b, d_ssm_norm, d_ffn2_norm, d_final]
    small_shapes = [a.shape for a in small_parts]
    packed = _pack(small_parts)
    early_recv, (all_parts,) = _ffn_bwd_w_scatter("ffn1_bwd_w", nb, daccb, ab, dgb, dub, chip,
                                                  _allgather_all_plan([packed]), [packed])
    received = dict(zip(BIG_LATE + BIG_EARLY, late_recv + early_recv))
    chip_sums = [_sum_slots("sum_chips_" + n, received[n], BF16) for n in BIG]
    sib_sums = _swap_sibling("swap_sibling", chip_sums)
    (loss_row, g_meta_full, g_ffn1_norm, g_mix_norm, g_ret_norm, g_ar, g_ai, g_bbr, g_bbi, g_c_re, g_c_im,
     g_ssm_d, g_glu_b, g_ssm_norm, g_ffn2_norm, g_final) = _unpack(_sum_slots("sum_small", all_parts, F32), small_shapes)
    g_lam_re, g_lam_im, g_log_dt, g_b_re, g_b_im = prep_vjp((g_ar, g_ai, g_bbr, g_bbi))
    loss = loss_row[0, 0]
    g_meta = lax.dynamic_slice(g_meta_full, (0, chip * (d // N_CHIP)), (N_META, d // N_CHIP))
    small_grads = {
        'meta_tokens': g_meta, 'ffn1_norm_w': g_ffn1_norm, 'mix_norm_w': g_mix_norm, 'ret_norm_w': g_ret_norm,
        'ssm_lambda_re': g_lam_re[None], 'ssm_lambda_im': g_lam_im[None], 'ssm_log_dt': g_log_dt[None],
        'ssm_b_re': g_b_re[None], 'ssm_b_im': g_b_im[None], 'ssm_c_re': g_c_re[None], 'ssm_c_im': g_c_im[None],
        'ssm_d': g_ssm_d, 'ssm_glu_b': g_glu_b, 'ssm_norm_w': g_ssm_norm, 'ffn2_norm_w': g_ffn2_norm,
        'final_norm_w': g_final.reshape(d),
    }

    grads, deltas, new_m, new_v = {}, {}, {}, {}
    for n, mine, sib in zip(BIG, chip_sums, sib_sums):
        if n in TRANSPOSED:
            outs = _adam("adam_" + n, as_fd(w[n]), as_fd(m[n]), as_fd(v[n]), [mine, sib])
            grads[n], deltas[n], new_m[n], new_v[n] = [as_fd(t) for t in outs]
        else:
            grads[n], deltas[n], new_m[n], new_v[n] = _adam("adam_" + n, w[n], m[n], v[n], [mine, sib])
    sm_shapes = [w[n].shape for n in SMALL]
    sm_out = _adam("adam_small", _pack([w[n] for n in SMALL]), _pack([m[n] for n in SMALL]),
                   _pack([v[n] for n in SMALL]), [_pack([small_grads[n].reshape(w[n].shape) for n in SMALL])])
    for dst, packed in zip((grads, deltas, new_m, new_v), sm_out):
        for n, t in zip(SMALL, _unpack(packed, sm_shapes)):
            dst[n] = t

    return (loss, grad_x, *[grads[n] for n in WEIGHT_NAMES], *[deltas[n] for n in WEIGHT_NAMES],
            *[new_m[n] for n in WEIGHT_NAMES], *[new_v[n] for n in WEIGHT_NAMES])
```

```python
import functools
import math

import jax
import jax.numpy as jnp
from jax import lax
from jax.experimental import pallas as pl
from jax.experimental.pallas import tpu as pltpu

N_META = 16
RET_HEADS = 4
HEAD_DIM = 128
SSM_GROUP = 16
SSM_STATE = 64
CHUNK = 128
ROPE_BASE = 10000.0
EPS = 1e-6
FFN_RES = 0.5
N_SEG = 8
N_SEC = 4
N_CHIP = 4
LANE = 128
FFN_CPS = 2
BWD_W_ROWS = 1664

ADAM_LR = 0.001
ADAM_B1 = 0.9
ADAM_B2 = 0.999
ADAM_EPS = 1e-08
ADAM_WD = 0.01
ADAM_STEP = 10

VMEM_LIMIT = 56 * 1024 * 1024

F32 = jnp.float32
BF16 = jnp.bfloat16
MESH = pl.DeviceIdType.MESH


def _dot(a, b):
    return jnp.dot(a, b, preferred_element_type=F32)


def _dot_nt(a, b):
    return lax.dot_general(a, b, (((1,), (1,)), ((), ())), preferred_element_type=F32)


def _dot_tn(a, b):
    return lax.dot_general(a, b, (((0,), (0,)), ((), ())), preferred_element_type=F32)


def _tile(n, target, mult=64):
    best = None
    t = mult
    while t <= min(n, target):
        if n % t == 0:
            best = t
        t += mult
    assert best is not None, (n, target)
    return best


def _params(sem, vmem=VMEM_LIMIT):
    return pltpu.CompilerParams(dimension_semantics=sem, vmem_limit_bytes=vmem)


def _rms_stats(xf):
    r = lax.rsqrt(jnp.mean(xf * xf, axis=-1, keepdims=True) + EPS)
    return xf * r, r


def _rms_bwd(dy, xh, r, w):
    dxh = dy * w
    return r * (dxh - xh * jnp.mean(dxh * xh, axis=-1, keepdims=True))


def _sigmoid(x):
    return 0.5 * jnp.tanh(0.5 * x) + 0.5


GELU_K0 = math.sqrt(2.0 / math.pi)
GELU_K1 = 0.044715


CHIP_MASKS = [(1, 0, 0), (0, 1, 0), (1, 1, 0)]
ALL_MASKS = [(0, 0, 1), (0, 1, 0), (0, 1, 1), (1, 0, 0), (1, 0, 1), (1, 1, 0), (1, 1, 1)]
SIB_MASKS = [(0, 0, 1)]
ANY_SPEC = pl.BlockSpec(memory_space=pl.ANY)


class _Plan:
    def __init__(self, arrays, masks, n_slots, src_slotted, dst_slotted, local_copy, half=False, forward=False):
        self.shapes = [(a.shape, a.dtype) for a in arrays]
        self.n = len(arrays)
        self.masks = masks
        self.n_slots = n_slots
        self.src_slotted, self.dst_slotted, self.local_copy = src_slotted, dst_slotted, local_copy
        self.half, self.forward = half, forward
        self.n_cp = self.n * len(masks) * (len(CHIP_MASKS) if forward else 1)

    def out_shape(self):
        out = []
        for shp, dt in self.shapes:
            if self.dst_slotted and not self.src_slotted:
                shp = (self.n_slots,) + shp
            elif self.src_slotted and not self.dst_slotted:
                shp = shp[1:]
            out.append(jax.ShapeDtypeStruct(shp, dt))
        return tuple(out)

    def scratch(self):
        return [pltpu.SemaphoreType.DMA((self.n_cp,)), pltpu.SemaphoreType.DMA((self.n_cp,)),
                pltpu.SemaphoreType.DMA((self.n,))]

    def _slot(self, px, py, pc):
        if self.n_slots == 8:
            return 4 * px + 2 * py + pc
        if self.n_slots == 4:
            return 2 * px + py
        return pc

    def copies(self, ins, outs, sems):
        send_sems, recv_sems, loc_sems = sems
        x, y, c = lax.axis_index("x"), lax.axis_index("y"), lax.axis_index("c")
        me = self._slot(x, y, c)
        n_m = len(self.masks)
        cps = []
        for a in range(self.n):
            if self.forward:
                rows = self.shapes[a][0][-2] // 2
                mine = pl.ds(pl.multiple_of(c * rows, 8), rows)
                for j, (mx, my, _) in enumerate(CHIP_MASKS):
                    blk = outs[a].at[2 * (1 - x if mx else x) + (1 - y if my else y), mine]
                    k = a * len(CHIP_MASKS) + j
                    cps.append(pltpu.make_async_remote_copy(
                        src_ref=blk, dst_ref=blk, send_sem=send_sems.at[k], recv_sem=recv_sems.at[k],
                        device_id=(x, y, 1 - c), device_id_type=MESH))
                continue
            if self.local_copy:
                src = ins[a].at[me] if self.src_slotted else ins[a]
                cps.append(pltpu.make_async_copy(src, outs[a].at[me], loc_sems.at[a]))
            for mi, (mx, my, mc) in enumerate(self.masks):
                px = 1 - x if mx else x
                py = 1 - y if my else y
                pc = 1 - c if mc else c
                src = ins[a].at[self._slot(px, py, pc)] if self.src_slotted else ins[a]
                dst = outs[a].at[me] if self.dst_slotted else outs[a]
                if self.half:
                    rows = src.shape[-2] // 2
                    mine = pl.ds(pl.multiple_of(c * rows, 8), rows)
                    src, dst = src.at[mine], dst.at[mine]
                k = a * n_m + mi
                cps.append(pltpu.make_async_remote_copy(
                    src_ref=src, dst_ref=dst, send_sem=send_sems.at[k], recv_sem=recv_sems.at[k],
                    device_id=(px, py, pc), device_id_type=MESH))
        return cps


def _exchange(name, plan, arrays):
    n = plan.n

    def body(*refs):
        cps = plan.copies(refs[:n], refs[n:2 * n], refs[2 * n:])
        for cp in cps:
            cp.start()
        for cp in cps:
            cp.wait()

    outs = pl.pallas_call(
        body, name=name, out_shape=plan.out_shape(),
        in_specs=[ANY_SPEC] * n, out_specs=tuple([ANY_SPEC] * n), scratch_shapes=plan.scratch(),
        input_output_aliases={i: i for i in range(n)} if plan.forward else {},
    )(*arrays)
    return list(outs)


def _pcall(body, *, name, grid, in_specs, out_specs, out_shape, scratch_shapes, args, plan=None, plan_args=()):
    sem = ("arbitrary",) * len(grid)
    if plan is None:
        return pl.pallas_call(body, name=name, grid=grid, in_specs=in_specs, out_specs=out_specs,
                              out_shape=out_shape, scratch_shapes=scratch_shapes,
                              compiler_params=_params(sem))(*args), []
    n_in, n_out, n_scr, n_p = len(in_specs), len(out_specs), len(scratch_shapes), plan.n

    def wrapped(*refs):
        ins = refs[:n_in]
        p_ins = refs[n_in:n_in + n_p]
        o0 = n_in + n_p
        outs = refs[o0:o0 + n_out]
        p_outs = refs[o0 + n_out:o0 + n_out + n_p]
        s0 = o0 + n_out + n_p
        scr = refs[s0:s0 + n_scr]
        sems = refs[s0 + n_scr:]
        ids = [pl.program_id(i) for i in range(len(grid))]
        first = functools.reduce(jnp.logical_and, [i == 0 for i in ids])
        last = functools.reduce(jnp.logical_and, [i == g - 1 for i, g in zip(ids, grid)])

        @pl.when(first)
        def _():
            for cp in plan.copies(p_ins, p_outs, sems):
                cp.start()

        body(*ins, *outs, *scr)

        @pl.when(last)
        def _():
            for cp in plan.copies(p_ins, p_outs, sems):
                cp.wait()

    res = pl.pallas_call(
        wrapped, name=name, grid=grid,
        in_specs=list(in_specs) + [ANY_SPEC] * n_p,
        out_specs=tuple(out_specs) + (ANY_SPEC,) * n_p,
        out_shape=tuple(out_shape) + plan.out_shape(),
        scratch_shapes=list(scratch_shapes) + plan.scratch(),
        compiler_params=_params(sem),
    )(*args, *plan_args)
    return res[:n_out], list(res[n_out:])


def _allgather_chips_plan(arrays):
    return _Plan(arrays, CHIP_MASKS, 4, False, True, True, half=True)


def _forward_sibling(name, gathered):
    return _exchange(name, _Plan(gathered, SIB_MASKS, 4, True, True, False, forward=True), gathered)


def _alltoall_chips_plan(arrays):
    return _Plan(arrays, CHIP_MASKS, 4, True, True, True)


def _swap_sibling(name, arrays):
    return _exchange(name, _Plan(arrays, SIB_MASKS, 2, False, False, False), arrays)


def _allgather_all_plan(arrays):
    return _Plan(arrays, ALL_MASKS, 8, False, True, True)


def _sum_slots(name, a, out_dtype):
    s, r, c = a.shape
    tr = _tile(r, 512, 8)

    def body(a_ref, o_ref):
        acc = a_ref[0].astype(F32)
        for i in range(1, s):
            acc = acc + a_ref[i].astype(F32)
        o_ref[...] = acc.astype(out_dtype)

    return pl.pallas_call(
        body, name=name, grid=(r // tr,),
        in_specs=[pl.BlockSpec((s, tr, c), lambda i: (0, i, 0))],
        out_specs=pl.BlockSpec((tr, c), lambda i: (i, 0)),
        out_shape=jax.ShapeDtypeStruct((r, c), out_dtype),
        compiler_params=_params(("arbitrary",)),
    )(a)


def _adam_math(w, g, m, v):
    m_new = ADAM_B1 * m + (1.0 - ADAM_B1) * g
    v_new = ADAM_B2 * v + (1.0 - ADAM_B2) * (g * g)
    m_hat = m_new / (1.0 - ADAM_B1 ** ADAM_STEP)
    v_hat = v_new / (1.0 - ADAM_B2 ** ADAM_STEP)
    delta = -ADAM_LR * (m_hat / (jnp.sqrt(v_hat) + ADAM_EPS) + ADAM_WD * w)
    return delta, m_new, v_new


def _adam(name, w, m, v, g_parts):
    r, c = w.shape[-2:]
    tr = _tile(r, 256, 8)
    n_g = len(g_parts)
    lead = w.ndim == 3
    at = (lambda ref: ref.at[0]) if lead else (lambda ref: ref)

    def body(*refs):
        w_ref, m_ref, v_ref = [at(t) for t in refs[:3]]
        g_refs = refs[3:3 + n_g]
        g_out, d_out, m_out, v_out = [at(t) for t in refs[3 + n_g:]]
        g = g_refs[0][...].astype(F32)
        for gr in g_refs[1:]:
            g = g + gr[...].astype(F32)
        delta, m_new, v_new = _adam_math(w_ref[...], g, m_ref[...], v_ref[...])
        g_out[...] = g
        d_out[...] = delta
        m_out[...] = m_new
        v_out[...] = v_new

    spec = pl.BlockSpec((tr, c), lambda i: (i, 0))
    wspec = pl.BlockSpec((1, tr, c), lambda i: (0, i, 0)) if lead else spec
    shp = jax.ShapeDtypeStruct(w.shape, F32)
    return pl.pallas_call(
        body, name=name, grid=(r // tr,),
        in_specs=[wspec] * 3 + [spec] * n_g, out_specs=(wspec,) * 4, out_shape=(shp,) * 4,
        compiler_params=_params(("arbitrary",)),
    )(w, m, v, *g_parts)


def _ffn_fwd(name, h, nw, wg, wu, wd, plan=None, plan_args=()):
    lp, d = h.shape
    nck, f, _ = wg.shape
    tm = _tile(lp, 640)
    last = nck // FFN_CPS - 1

    def body(h_ref, nw_ref, wg_ref, wu_ref, wd_ref, ho_ref, g_ref, u_ref, n_sc, acc_sc):
        k = pl.program_id(1)

        @pl.when(k == 0)
        def _():
            xh, _ = _rms_stats(h_ref[...])
            n_sc[...] = (xh * nw_ref[...]).astype(BF16)
            acc_sc[...] = jnp.zeros_like(acc_sc)

        n = n_sc[...]
        acc = acc_sc[...]
        for c in range(FFN_CPS):
            g = _dot_nt(n, wg_ref[c])
            u = _dot_nt(n, wu_ref[c])
            g_ref[c] = g.astype(BF16)
            u_ref[c] = u.astype(BF16)
            a = (g * _sigmoid(g) * u).astype(BF16)
            acc = acc + _dot(a, wd_ref[c])
        acc_sc[...] = acc

        @pl.when(k == last)
        def _():
            ho_ref[...] = h_ref[...] + FFN_RES * acc_sc[...]

    return _pcall(
        body, name=name, grid=(lp // tm, nck // FFN_CPS), plan=plan, plan_args=plan_args, args=(h, nw, wg, wu, wd),
        in_specs=[pl.BlockSpec((tm, d), lambda i, k: (i, 0)),
                  pl.BlockSpec((1, d), lambda i, k: (0, 0)),
                  pl.BlockSpec((FFN_CPS, f, d), lambda i, k: (k, 0, 0)),
                  pl.BlockSpec((FFN_CPS, f, d), lambda i, k: (k, 0, 0)),
                  pl.BlockSpec((FFN_CPS, f, d), lambda i, k: (k, 0, 0))],
        out_specs=(pl.BlockSpec((tm, d), lambda i, k: (i, 0)),
                   pl.BlockSpec((FFN_CPS, tm, f), lambda i, k: (k, i, 0)),
                   pl.BlockSpec((FFN_CPS, tm, f), lambda i, k: (k, i, 0))),
        out_shape=(jax.ShapeDtypeStruct((lp, d), F32),
                   jax.ShapeDtypeStruct((nck, lp, f), BF16),
                   jax.ShapeDtypeStruct((nck, lp, f), BF16)),
        scratch_shapes=[pltpu.VMEM((tm, d), BF16), pltpu.VMEM((tm, d), F32)])


def _ffn_bwd_act(name, dh, h, nw, g, u, wg, wu, wd, plan=None, plan_args=()):
    lp, d = h.shape
    nck, f, _ = wg.shape
    tm = _tile(lp, 320)
    last = nck // FFN_CPS - 1

    def body(dh_ref, h_ref, nw_ref, g_ref, u_ref, wg_ref, wu_ref, wd_ref,
             dhi_ref, dnw_ref, n_ref, dacc_ref, a_ref, dg_ref, du_ref,
             xh_sc, r_sc, dn_sc):
        i = pl.program_id(0)
        k = pl.program_id(1)

        @pl.when(k == 0)
        def _():
            xh, r = _rms_stats(h_ref[...])
            xh_sc[...] = xh
            r_sc[...] = r
            n_ref[...] = (xh * nw_ref[...]).astype(BF16)
            dacc_ref[...] = (FFN_RES * dh_ref[...]).astype(BF16)
            dn_sc[...] = jnp.zeros_like(dn_sc)

        @pl.when(jnp.logical_and(i == 0, k == 0))
        def _():
            dnw_ref[...] = jnp.zeros_like(dnw_ref)

        dacc = dacc_ref[...]
        dn = dn_sc[...]
        for c in range(FFN_CPS):
            gv = g_ref[c].astype(F32)
            uv = u_ref[c].astype(F32)
            sg = _sigmoid(gv)
            sil = gv * sg
            da = _dot_nt(dacc, wd_ref[c])
            dgk = (da * uv * (sg * (1.0 + gv * (1.0 - sg)))).astype(BF16)
            duk = (da * sil).astype(BF16)
            a_ref[c] = (sil * uv).astype(BF16)
            dg_ref[c] = dgk
            du_ref[c] = duk
            dn = dn + _dot(dgk, wg_ref[c]) + _dot(duk, wu_ref[c])
        dn_sc[...] = dn

        @pl.when(k == last)
        def _():
            dn = dn_sc[...]
            xh = xh_sc[...]
            dhi_ref[...] = dh_ref[...] + _rms_bwd(dn, xh, r_sc[...], nw_ref[...])
            dnw_ref[...] += jnp.sum(dn * xh, axis=0, keepdims=True)

    row = pl.BlockSpec((tm, d), lambda i, k: (i, 0))
    vec = pl.BlockSpec((1, d), lambda i, k: (0, 0))
    hid = pl.BlockSpec((FFN_CPS, tm, f), lambda i, k: (k, i, 0))
    w_fd = pl.BlockSpec((FFN_CPS, f, d), lambda i, k: (k, 0, 0))
    return _pcall(
        body, name=name, grid=(lp // tm, nck // FFN_CPS), plan=plan, plan_args=plan_args,
        args=(dh, h, nw, g, u, wg, wu, wd),
        in_specs=[row, row, vec, hid, hid, w_fd, w_fd, w_fd],
        out_specs=(row, vec, row, row, hid, hid, hid),
        out_shape=(jax.ShapeDtypeStruct((lp, d), F32),
                   jax.ShapeDtypeStruct((1, d), F32),
                   jax.ShapeDtypeStruct((lp, d), BF16),
                   jax.ShapeDtypeStruct((lp, d), BF16),
                   jax.ShapeDtypeStruct((nck, lp, f), BF16),
                   jax.ShapeDtypeStruct((nck, lp, f), BF16),
                   jax.ShapeDtypeStruct((nck, lp, f), BF16)),
        scratch_shapes=[pltpu.VMEM((tm, d), F32), pltpu.VMEM((tm, 1), F32), pltpu.VMEM((tm, d), F32)])


def _ffn_bwd_w(name, n, dacc, a, dg, du):
    lp, d = n.shape
    nck, _, f = a.shape
    tm = _tile(lp, BWD_W_ROWS)
    last = lp // tm - 1

    def body(n_ref, dacc_ref, a_ref, dg_ref, du_ref, dwg_ref, dwu_ref, dwd_ref, ag_sc, au_sc, ad_sc):
        i = pl.program_id(1)

        @pl.when(i == 0)
        def _():
            ag_sc[...] = jnp.zeros_like(ag_sc)
            au_sc[...] = jnp.zeros_like(au_sc)
            ad_sc[...] = jnp.zeros_like(ad_sc)

        nv = n_ref[...]
        ag_sc[...] += _dot_tn(dg_ref[0], nv)
        au_sc[...] += _dot_tn(du_ref[0], nv)
        ad_sc[...] += _dot_tn(a_ref[0], dacc_ref[...])

        @pl.when(i == last)
        def _():
            dwg_ref[0] = ag_sc[...].astype(BF16)
            dwu_ref[0] = au_sc[...].astype(BF16)
            dwd_ref[0] = ad_sc[...].astype(BF16)

    row = pl.BlockSpec((tm, d), lambda k, i: (i, 0))
    hid = pl.BlockSpec((1, tm, f), lambda k, i: (k, i, 0))
    w_fd = pl.BlockSpec((1, f, d), lambda k, i: (k, 0, 0))
    wshape = jax.ShapeDtypeStruct((nck, f, d), BF16)
    return pl.pallas_call(
        body, name=name, grid=(nck, lp // tm),
        in_specs=[row, row, hid, hid, hid], out_specs=(w_fd, w_fd, w_fd), out_shape=(wshape,) * 3,
        scratch_shapes=[pltpu.VMEM((f, d), F32)] * 3,
        compiler_params=_params(("arbitrary", "arbitrary")),
    )(n, dacc, a, dg, du)


def _ffn_bwd_w_scatter(name, n, dacc, a, dg, du, chip, plan, plan_args):
    lp, d = n.shape
    nck, _, f = a.shape
    tm = _tile(lp, BWD_W_ROWS)
    last_i = lp // tm - 1
    n_w = 3
    n_p = plan.n

    def body(me_ref, n_ref, dacc_ref, a_ref, dg_ref, du_ref, *rest):
        p_ins = rest[:n_p]
        recv = rest[n_p:n_p + n_w]
        p_outs = rest[n_p + n_w:2 * n_p + n_w]
        acc = rest[2 * n_p + n_w:2 * n_p + 2 * n_w]
        stage, send_sems, recv_sems, loc_sems = rest[2 * n_p + 2 * n_w:2 * n_p + 2 * n_w + 4]
        p_sems = rest[2 * n_p + 2 * n_w + 4:]
        p = pl.program_id(0)
        i = pl.program_id(1)
        me = me_ref[0]
        c = lax.axis_index("c")

        def send(w, pos):
            kk = jnp.bitwise_xor(me, nck - 1 - pos)
            diff = jnp.bitwise_xor(kk, me)
            m = jnp.where(diff == 2, 0, jnp.where(diff == 1, 1, 2))
            return pltpu.make_async_remote_copy(
                src_ref=stage.at[lax.rem(pos, 2), w], dst_ref=recv[w].at[me],
                send_sem=send_sems.at[w * 3 + m], recv_sem=recv_sems.at[w * 3 + m],
                device_id=(lax.div(kk, 2), lax.rem(kk, 2), c), device_id_type=MESH)

        @pl.when(jnp.logical_and(p == 0, i == 0))
        def _():
            for cp in plan.copies(p_ins, p_outs, p_sems):
                cp.start()

        @pl.when(i == 0)
        def _():
            for t in acc:
                t[...] = jnp.zeros_like(t)

        nv = n_ref[...]
        acc[0][...] += _dot_tn(dg_ref[0], nv)
        acc[1][...] += _dot_tn(du_ref[0], nv)
        acc[2][...] += _dot_tn(a_ref[0], dacc_ref[...])

        @pl.when(jnp.logical_and(i == last_i, p >= 2))
        def _():
            for w in range(n_w):
                send(w, p - 2).wait_send()

        @pl.when(i == last_i)
        def _():
            for w in range(n_w):
                stage[lax.rem(p, 2), w] = acc[w][...].astype(BF16)

        @pl.when(jnp.logical_and(i == last_i, p < nck - 1))
        def _():
            for w in range(n_w):
                send(w, p).start()

        @pl.when(jnp.logical_and(i == last_i, p == nck - 1))
        def _():
            own = [pltpu.make_async_copy(stage.at[(nck - 1) % 2, w], recv[w].at[me], loc_sems.at[w])
                   for w in range(n_w)]
            for cp in own:
                cp.start()
            for w in range(n_w):
                send(w, nck - 2).wait_send()
            for cp in own:
                cp.wait()
            for w in range(n_w):
                for m in range(3):
                    pltpu.make_async_remote_copy(
                        src_ref=stage.at[0, w], dst_ref=recv[w].at[me],
                        send_sem=send_sems.at[w * 3 + m], recv_sem=recv_sems.at[w * 3 + m],
                        device_id=(0, 0, c), device_id_type=MESH).wait_recv()
            for cp in plan.copies(p_ins, p_outs, p_sems):
                cp.wait()

    chunk = lambda k, me_ref: jnp.bitwise_xor(me_ref[0], nck - 1 - k)
    row = pl.BlockSpec((tm, d), lambda k, i, me_ref: (i, 0))
    hid = pl.BlockSpec((1, tm, f), lambda k, i, me_ref: (chunk(k, me_ref), i, 0))
    wshape = jax.ShapeDtypeStruct((nck, f, d), BF16)
    res = pl.pallas_call(
        body, name=name,
        grid_spec=pltpu.PrefetchScalarGridSpec(
            num_scalar_prefetch=1, grid=(nck, lp // tm),
            in_specs=[row, row, hid, hid, hid] + [ANY_SPEC] * n_p,
            out_specs=(ANY_SPEC,) * (n_w + n_p),
            scratch_shapes=[pltpu.VMEM((f, d), F32)] * n_w + [
                pltpu.VMEM((2, n_w, f, d), BF16), pltpu.SemaphoreType.DMA((n_w * 3,)),
                pltpu.SemaphoreType.DMA((n_w * 3,)), pltpu.SemaphoreType.DMA((n_w,))] + plan.scratch()),
        out_shape=(wshape,) * n_w + plan.out_shape(),
        compiler_params=_params(("arbitrary", "arbitrary")),
    )(chip.reshape(1).astype(jnp.int32), n, dacc, a, dg, du, *plan_args)
    return list(res[:n_w]), list(res[n_w:])


def _inproj_fwd(h, nw, w_in, cosf, sinf, rw):
    lp, d = h.shape
    nck, _, ps = w_in.shape
    proj = nck * ps
    sw = proj - 4 * rw
    tm = _tile(lp, 640)
    scale = HEAD_DIM ** -0.5
    heads = rw // HEAD_DIM

    def body(h_ref, nw_ref, w_ref, cos_ref, sin_ref, n_ref, q_ref, k_ref, v_ref, g_ref, u_ref, p_sc):
        xh, _ = _rms_stats(h_ref[...])
        n = (xh * nw_ref[...]).astype(BF16)
        n_ref[...] = n
        for c in range(nck):
            p_sc[:, c * ps:(c + 1) * ps] = _dot(n, w_ref[c])
        cs = cos_ref[...]
        sn = sin_ref[...]
        for hh in range(heads):
            lo = hh * HEAD_DIM
            qh = p_sc[:, lo:lo + HEAD_DIM]
            q_ref[:, lo:lo + HEAD_DIM] = (qh * cs + pltpu.roll(qh, HEAD_DIM // 2, 1) * sn).astype(BF16)
            kh = p_sc[:, rw + lo:rw + lo + HEAD_DIM]
            k_ref[:, lo:lo + HEAD_DIM] = ((kh * cs + pltpu.roll(kh, HEAD_DIM // 2, 1) * sn) * scale).astype(BF16)
        v_ref[...] = p_sc[:, 2 * rw:3 * rw].astype(BF16)
        g_ref[...] = p_sc[:, 3 * rw:4 * rw]
        u_ref[...] = p_sc[:, 4 * rw:]

    row = lambda w: pl.BlockSpec((tm, w), lambda i: (i, 0))
    return pl.pallas_call(
        body, name="inproj_fwd", grid=(lp // tm,),
        in_specs=[row(d), pl.BlockSpec((1, d), lambda i: (0, 0)),
                  pl.BlockSpec((nck, d, ps), lambda i: (0, 0, 0)), row(HEAD_DIM), row(HEAD_DIM)],
        out_specs=(row(d), row(rw), row(rw), row(rw), row(rw), row(sw)),
        out_shape=(jax.ShapeDtypeStruct((lp, d), BF16),
                   jax.ShapeDtypeStruct((lp, rw), BF16),
                   jax.ShapeDtypeStruct((lp, rw), BF16),
                   jax.ShapeDtypeStruct((lp, rw), BF16),
                   jax.ShapeDtypeStruct((lp, rw), F32),
                   jax.ShapeDtypeStruct((lp, sw), F32)),
        scratch_shapes=[pltpu.VMEM((tm, proj), F32)],
        compiler_params=_params(("arbitrary",)),
    )(h, nw, w_in, cosf, sinf)


def _inproj_bwd(dh, h, nw, n, w_in, dq, dk, dv, dg, du):
    lp, d = h.shape
    nck, _, ps = w_in.shape
    rw = dq.shape[1]
    sw = du.shape[1]
    proj = nck * ps
    tm = _tile(lp, 640)
    last = lp // tm - 1

    def gather_dproj(p_sc, dq_ref, dk_ref, dv_ref, dg_ref, du_ref):
        p_sc[:, 0:rw] = dq_ref[...]
        p_sc[:, rw:2 * rw] = dk_ref[...]
        p_sc[:, 2 * rw:3 * rw] = dv_ref[...]
        p_sc[:, 3 * rw:4 * rw] = dg_ref[...]
        p_sc[:, 4 * rw:] = du_ref[...]

    def act_body(dh_ref, h_ref, nw_ref, w_ref, dq_ref, dk_ref, dv_ref, dg_ref, du_ref, dhi_ref, dnw_ref, p_sc):
        i = pl.program_id(0)

        @pl.when(i == 0)
        def _():
            dnw_ref[...] = jnp.zeros_like(dnw_ref)

        gather_dproj(p_sc, dq_ref, dk_ref, dv_ref, dg_ref, du_ref)
        dn = jnp.zeros((tm, d), F32)
        for c in range(nck):
            dn = dn + _dot_nt(p_sc[:, c * ps:(c + 1) * ps], w_ref[c])
        xh, r = _rms_stats(h_ref[...])
        dhi_ref[...] = dh_ref[...] + _rms_bwd(dn, xh, r, nw_ref[...])
        dnw_ref[...] += jnp.sum(dn * xh, axis=0, keepdims=True)

    def w_body(n_ref, dq_ref, dk_ref, dv_ref, dg_ref, du_ref, dw_ref, p_sc, acc_sc):
        i = pl.program_id(0)

        @pl.when(i == 0)
        def _():
            acc_sc[...] = jnp.zeros_like(acc_sc)

        gather_dproj(p_sc, dq_ref, dk_ref, dv_ref, dg_ref, du_ref)
        nv = n_ref[...]
        for c in range(nck):
            acc_sc[c] += _dot_tn(nv, p_sc[:, c * ps:(c + 1) * ps])

        @pl.when(i == last)
        def _():
            dw_ref[...] = acc_sc[...].astype(BF16)

    row = lambda w: pl.BlockSpec((tm, w), lambda i: (i, 0))
    vec = pl.BlockSpec((1, d), lambda i: (0, 0))
    wsp = pl.BlockSpec((nck, d, ps), lambda i: (0, 0, 0))
    dproj_specs = [row(rw), row(rw), row(rw), row(rw), row(sw)]
    dhi, dnw = pl.pallas_call(
        act_body, name="inproj_bwd_act", grid=(lp // tm,),
        in_specs=[row(d), row(d), vec, wsp] + dproj_specs,
        out_specs=(row(d), vec),
        out_shape=(jax.ShapeDtypeStruct((lp, d), F32), jax.ShapeDtypeStruct((1, d), F32)),
        scratch_shapes=[pltpu.VMEM((tm, proj), BF16)],
        compiler_params=_params(("arbitrary",)),
    )(dh, h, nw, w_in, dq, dk, dv, dg, du)
    dw = pl.pallas_call(
        w_body, name="inproj_bwd_w", grid=(lp // tm,),
        in_specs=[row(d)] + dproj_specs,
        out_specs=wsp, out_shape=jax.ShapeDtypeStruct((nck, d, ps), BF16),
        scratch_shapes=[pltpu.VMEM((tm, proj), BF16), pltpu.VMEM((nck, d, ps), F32)],
        compiler_params=_params(("arbitrary",)),
    )(n, dq, dk, dv, dg, du)
    return dhi, dnw, dw


def _retention_tables():
    h = jnp.arange(RET_HEADS, dtype=F32)
    log_g = jnp.log(1.0 - 2.0 ** (-5.0 - h))
    i = jnp.arange(CHUNK)
    diff = i[:, None] - i[None, :]
    dec = jnp.where(diff[None] >= 0,
                    jnp.exp(log_g[:, None, None] * jnp.maximum(diff, 0)[None].astype(F32)), 0.0)
    pos = jnp.arange(CHUNK, dtype=F32)
    wq = jnp.exp(log_g[:, None] * (pos + 1.0)[None])
    wk = jnp.exp(log_g[:, None] * (CHUNK - 1 - pos)[None])
    gch = jnp.exp(log_g * CHUNK)
    ones = jnp.ones((1, 1, HEAD_DIM), F32)
    return (dec, wq[:, :, None] * ones, wk[:, :, None] * ones,
            gch[:, None, None] * jnp.ones((1, 8, HEAD_DIM), F32))


def _head_norm(o):
    mu = jnp.mean(o, axis=-1, keepdims=True)
    oc = o - mu
    r = lax.rsqrt(jnp.mean(oc * oc, axis=-1, keepdims=True) + EPS)
    return oc * r, r


def _ret_fwd(q, k, v, g, rnw, tables):
    lp, rw = q.shape
    heads = rw // HEAD_DIM
    nch = lp // CHUNK
    dec, wq, wk, gch = tables

    def body(q_ref, k_ref, v_ref, g_ref, w_ref, dec_ref, wq_ref, wk_ref, gch_ref,
             o_ref, ret_ref, sp_ref, s_sc):
        n = pl.program_id(0)

        @pl.when(n == 0)
        def _():
            s_sc[...] = jnp.zeros_like(s_sc)

        cols = [slice(hh * HEAD_DIM, (hh + 1) * HEAD_DIM) for hh in range(heads)]
        s_ins = [s_sc[hh] for hh in range(heads)]
        outs = []
        for hh, cs in enumerate(cols):
            qv, kv, vv = q_ref[:, cs], k_ref[:, cs], v_ref[:, cs]
            s_in = s_ins[hh]
            a = _dot_nt(qv, kv) * dec_ref[hh]
            qw = (qv.astype(F32) * wq_ref[hh]).astype(BF16)
            kw = (kv.astype(F32) * wk_ref[hh]).astype(BF16)
            o = _dot(a.astype(BF16), vv) + _dot(qw, s_in.astype(BF16))
            s_new = gch_ref[hh, 0:1, :] * s_in + _dot_tn(kw, vv)
            xh, _ = _head_norm(o)
            gv = g_ref[:, cs]
            outs.append((o, s_new, (gv * _sigmoid(gv) * (xh * w_ref[:, cs])).astype(BF16)))
        for hh, cs in enumerate(cols):
            o, s_new, ret = outs[hh]
            sp_ref[hh, 0] = s_ins[hh]
            s_sc[hh] = s_new
            o_ref[:, cs] = o
            ret_ref[:, cs] = ret

    blk = pl.BlockSpec((CHUNK, rw), lambda n: (n, 0))
    tab = pl.BlockSpec((heads, CHUNK, HEAD_DIM), lambda n: (0, 0, 0))
    return pl.pallas_call(
        body, name="retention_fwd", grid=(nch,),
        in_specs=[blk, blk, blk, blk, pl.BlockSpec((1, rw), lambda n: (0, 0)),
                  tab, tab, tab, pl.BlockSpec((heads, 8, HEAD_DIM), lambda n: (0, 0, 0))],
        out_specs=(blk, blk, pl.BlockSpec((heads, 1, HEAD_DIM, HEAD_DIM), lambda n: (0, n, 0, 0))),
        out_shape=(jax.ShapeDtypeStruct((lp, rw), F32),
                   jax.ShapeDtypeStruct((lp, rw), BF16),
                   jax.ShapeDtypeStruct((heads, nch, HEAD_DIM, HEAD_DIM), F32)),
        scratch_shapes=[pltpu.VMEM((heads, HEAD_DIM, HEAD_DIM), F32)],
        compiler_params=_params(("arbitrary",)),
    )(q, k, v, g, rnw, dec, wq, wk, gch)


def _ret_bwd(dret, q, k, v, g, o, sprev, rnw, tables, cosf, sinf):
    lp, rw = q.shape
    heads = rw // HEAD_DIM
    nch = lp // CHUNK
    dec, wq, wk, gch = tables
    scale = HEAD_DIM ** -0.5
    half = HEAD_DIM // 2

    def body(dret_ref, q_ref, k_ref, v_ref, g_ref, o_ref, sp_ref, w_ref, dec_ref, wq_ref, wk_ref, gch_ref,
             cos_ref, sin_ref, dq_ref, dk_ref, dv_ref, dg_ref, dw_ref, ds_sc):
        n = pl.program_id(0)

        @pl.when(n == 0)
        def _():
            ds_sc[...] = jnp.zeros_like(ds_sc)
            dw_ref[...] = jnp.zeros_like(dw_ref)

        cosv = cos_ref[...]
        sinv = sin_ref[...]
        cols = [slice(hh * HEAD_DIM, (hh + 1) * HEAD_DIM) for hh in range(heads)]
        ds_ins = [ds_sc[hh] for hh in range(heads)]
        dw_ins = [dw_ref[:, cs] for cs in cols]
        outs = []
        for hh, cs in enumerate(cols):
            qv, kv, vv = q_ref[:, cs], k_ref[:, cs], v_ref[:, cs]
            gv = g_ref[:, cs]
            dr = dret_ref[:, cs]
            w = w_ref[:, cs]
            sg = _sigmoid(gv)
            sil = gv * sg
            xh, r = _head_norm(o_ref[:, cs])
            dgate = (dr * (xh * w) * (sg * (1.0 + gv * (1.0 - sg)))).astype(BF16)
            dyw = dr * sil
            dw_new = dw_ins[hh] + jnp.sum(dyw * xh, axis=0, keepdims=True)
            dxh = dyw * w
            do = r * (dxh - jnp.mean(dxh, axis=-1, keepdims=True)
                      - xh * jnp.mean(dxh * xh, axis=-1, keepdims=True))
            dob = do.astype(BF16)
            dmask = dec_ref[hh]
            wqv = wq_ref[hh]
            wkv = wk_ref[hh]
            a = (_dot_nt(qv, kv) * dmask).astype(BF16)
            da = (_dot_nt(dob, vv) * dmask).astype(BF16)
            qw = (qv.astype(F32) * wqv).astype(BF16)
            kw = (kv.astype(F32) * wkv).astype(BF16)
            s_in = sp_ref[hh, 0].astype(BF16)
            ds = ds_ins[hh]
            dsb = ds.astype(BF16)
            dq = _dot(da, kv) + _dot_nt(dob, s_in) * wqv
            dk = _dot_tn(da, qv) + _dot_nt(vv, dsb) * wkv
            dv = _dot_tn(a, dob) + _dot(kw, dsb)
            ds_new = gch_ref[hh, 0:1, :] * ds + _dot_tn(qw, dob)
            outs.append((dgate, dw_new, ds_new,
                         (dq * cosv + pltpu.roll(dq * sinv, half, 1)).astype(BF16),
                         ((dk * cosv + pltpu.roll(dk * sinv, half, 1)) * scale).astype(BF16),
                         dv.astype(BF16)))
        for hh, cs in enumerate(cols):
            dgate, dw_new, ds_new, dqv, dkv, dvv = outs[hh]
            dg_ref[:, cs] = dgate
            dw_ref[:, cs] = dw_new
            ds_sc[hh] = ds_new
            dq_ref[:, cs] = dqv
            dk_ref[:, cs] = dkv
            dv_ref[:, cs] = dvv

    blk = pl.BlockSpec((CHUNK, rw), lambda n: (nch - 1 - n, 0))
    tab = pl.BlockSpec((heads, CHUNK, HEAD_DIM), lambda n: (0, 0, 0))
    wsp = pl.BlockSpec((1, rw), lambda n: (0, 0))
    pos = pl.BlockSpec((CHUNK, HEAD_DIM), lambda n: (nch - 1 - n, 0))
    bshape = jax.ShapeDtypeStruct((lp, rw), BF16)
    return pl.pallas_call(
        body, name="retention_bwd", grid=(nch,),
        in_specs=[blk, blk, blk, blk, blk, blk,
                  pl.BlockSpec((heads, 1, HEAD_DIM, HEAD_DIM), lambda n: (0, nch - 1 - n, 0, 0)),
                  wsp, tab, tab, tab, pl.BlockSpec((heads, 8, HEAD_DIM), lambda n: (0, 0, 0)), pos, pos],
        out_specs=(blk, blk, blk, blk, wsp),
        out_shape=(bshape, bshape, bshape, bshape, jax.ShapeDtypeStruct((1, rw), F32)),
        scratch_shapes=[pltpu.VMEM((heads, HEAD_DIM, HEAD_DIM), F32)],
        compiler_params=_params(("arbitrary",)),
    )(dret, q, k, v, g, o, sprev, rnw, dec, wq, wk, gch, cosf, sinf)


SCAN_CW = 512


def _s5_prepare(lam_re, lam_im, log_dt, b_re, b_im):
    dt = jnp.exp(log_dt)[:, None]
    er = jnp.exp(lam_re * dt)
    ar = er * jnp.cos(lam_im * dt)
    ai = er * jnp.sin(lam_im * dt)
    den = lam_re * lam_re + lam_im * lam_im
    fr = ((ar - 1.0) * lam_re + ai * lam_im) / den
    fi = (ai * lam_re - (ar - 1.0) * lam_im) / den
    bbr = fr[..., None] * b_re - fi[..., None] * b_im
    bbi = fr[..., None] * b_im + fi[..., None] * b_re
    return ar, ai, bbr, bbi


def _blockdiag_in(t):
    g, p, n = t.shape
    gs = g // N_SEC
    t = t.reshape(N_SEC, gs, p, n)
    eye = jnp.eye(gs, dtype=t.dtype)
    return jnp.einsum("sgpn,gh->sgphn", t, eye).reshape(N_SEC, gs * p, gs * n)


def _blockdiag_out(m, g, p, n):
    gs = g // N_SEC
    m = m.reshape(N_SEC, gs, p, gs, n)
    eye = jnp.eye(gs, dtype=m.dtype)
    return jnp.einsum("sgphn,gh->sgpn", m, eye).reshape(g, p, n)


def _scan_step(xr_ref, xi_ref, r0, pr_of, ar_ref, ai_ref, conj, ncols):
    for cc in range(ncols // SCAN_CW):
        cs = pl.ds(cc * SCAN_CW, SCAN_CW)
        pr, pi = pr_of(cs)
        ar = ar_ref[:, cs]
        ai = ai_ref[:, cs]
        if conj:
            nr = ar * pr + ai * pi
            ni = ar * pi - ai * pr
        else:
            nr = ar * pr - ai * pi
            ni = ar * pi + ai * pr
        xr_ref[pl.ds(r0, 8), cs] = xr_ref[pl.ds(r0, 8), cs] + nr
        xi_ref[pl.ds(r0, 8), cs] = xi_ref[pl.ds(r0, 8), cs] + ni


def _shift_rows(z, down):
    row = lax.broadcasted_iota(jnp.int32, z.shape, 0)
    if down:
        return jnp.where(row == 0, 0.0, pltpu.roll(z, 1, 0))
    return jnp.where(row == N_SEG - 1, 0.0, pltpu.roll(z, N_SEG - 1, 0))


def _s5_fwd(u, bsr, bsi, csr, csi, a8r, a8i, al8r, al8i, d, gluw, glub, nw, jb):
    lp, sw = u.shape
    ns = a8r.shape[1]
    rows = N_SEG * jb
    nblk = lp // rows
    secw = sw // N_SEC
    secn = ns // N_SEC

    def local_scan(u_ref, bsr_ref, bsi_ref, ar_ref, ai_ref, xr_ref, xi_ref, pr_sc, pi_sc):
        for s in range(N_SEC):
            ub = u_ref[:, s * secw:(s + 1) * secw].astype(BF16)
            xr_ref[:, s * secn:(s + 1) * secn] = _dot(ub, bsr_ref[s])
            xi_ref[:, s * secn:(s + 1) * secn] = _dot(ub, bsi_ref[s])
        _scan_step(xr_ref, xi_ref, 0, lambda cs: (pr_sc[:, cs], pi_sc[:, cs]), ar_ref, ai_ref, False, ns)

        def step(j, carry):
            r0 = pl.multiple_of(j * 8, 8)
            rp = pl.multiple_of((j - 1) * 8, 8)
            _scan_step(xr_ref, xi_ref, r0,
                       lambda cs: (xr_ref[pl.ds(rp, 8), cs], xi_ref[pl.ds(rp, 8), cs]),
                       ar_ref, ai_ref, False, ns)
            return carry

        lax.fori_loop(1, jb, step, 0)
        pr_sc[...] = xr_ref[rows - 8:rows, :]
        pi_sc[...] = xi_ref[rows - 8:rows, :]

    def carry_body(u_ref, bsr_ref, bsi_ref, ar_ref, ai_ref, alr_ref, ali_ref, cr_ref, ci_ref,
                   xr_sc, xi_sc, pr_sc, pi_sc):
        b = pl.program_id(0)

        @pl.when(b == 0)
        def _():
            pr_sc[...] = jnp.zeros_like(pr_sc)
            pi_sc[...] = jnp.zeros_like(pi_sc)

        local_scan(u_ref, bsr_ref, bsi_ref, ar_ref, ai_ref, xr_sc, xi_sc, pr_sc, pi_sc)

        @pl.when(b == nblk - 1)
        def _():
            er = _shift_rows(pr_sc[...], True)
            ei = _shift_rows(pi_sc[...], True)
            alr, ali = alr_ref[...], ali_ref[...]
            cr, ci = er, ei
            for _ in range(N_SEG - 2):
                sr = _shift_rows(cr, True)
                si = _shift_rows(ci, True)
                cr = er + alr * sr - ali * si
                ci = ei + alr * si + ali * sr
            cr_ref[...] = cr
            ci_ref[...] = ci

    ublk = pl.BlockSpec((rows, sw), lambda b: (b, 0))
    bspec = pl.BlockSpec((N_SEC, secw, secn), lambda b: (0, 0, 0))
    cspec = pl.BlockSpec((N_SEC, secn, secw), lambda b: (0, 0, 0))
    s8 = pl.BlockSpec((N_SEG, ns), lambda b: (0, 0))
    vec = pl.BlockSpec((1, sw), lambda b: (0, 0))
    s8shape = jax.ShapeDtypeStruct((N_SEG, ns), F32)
    c0r, c0i = pl.pallas_call(
        carry_body, name="s5_fwd_carry", grid=(nblk,),
        in_specs=[ublk, bspec, bspec, s8, s8, s8, s8],
        out_specs=(s8, s8), out_shape=(s8shape, s8shape),
        scratch_shapes=[pltpu.VMEM((rows, ns), F32), pltpu.VMEM((rows, ns), F32),
                        pltpu.VMEM((N_SEG, ns), F32), pltpu.VMEM((N_SEG, ns), F32)],
        compiler_params=_params(("arbitrary",)),
    )(u, bsr, bsi, a8r, a8i, al8r, al8i)

    def main_body(u_ref, bsr_ref, bsi_ref, csr_ref, csi_ref, ar_ref, ai_ref, c0r_ref, c0i_ref,
                  d_ref, gw_ref, gb_ref, nw_ref, xr_ref, xi_ref, yp_ref, out_ref, pr_sc, pi_sc):
        b = pl.program_id(0)

        @pl.when(b == 0)
        def _():
            pr_sc[...] = c0r_ref[...]
            pi_sc[...] = c0i_ref[...]

        local_scan(u_ref, bsr_ref, bsi_ref, ar_ref, ai_ref, xr_ref, xi_ref, pr_sc, pi_sc)
        for s in range(N_SEC):
            xs = pl.ds(s * secn, secn)
            us = pl.ds(s * secw, secw)
            y = _dot(xr_ref[:, xs].astype(BF16), csr_ref[s]) + _dot(xi_ref[:, xs].astype(BF16), csi_ref[s])
            yp_ref[:, us] = y + d_ref[:, us] * u_ref[:, us]
        yp = yp_ref[...]
        t = jnp.tanh(GELU_K0 * (yp + GELU_K1 * yp * yp * yp))
        y1 = 0.5 * yp * (1.0 + t)
        z = _dot(y1.astype(BF16), gw_ref[...]) + gb_ref[...]
        y2 = y1 * _sigmoid(z)
        xh, _ = _rms_stats(y2)
        out_ref[...] = (xh * nw_ref[...]).astype(BF16)

    xblk = pl.BlockSpec((rows, ns), lambda b: (b, 0))
    xr, xi, yp, out = pl.pallas_call(
        main_body, name="s5_fwd", grid=(nblk,),
        in_specs=[ublk, bspec, bspec, cspec, cspec, s8, s8, s8, s8, vec,
                  pl.BlockSpec((sw, sw), lambda b: (0, 0)), vec, vec],
        out_specs=(xblk, xblk, ublk, ublk),
        out_shape=(jax.ShapeDtypeStruct((lp, ns), F32), jax.ShapeDtypeStruct((lp, ns), F32),
                   jax.ShapeDtypeStruct((lp, sw), F32), jax.ShapeDtypeStruct((lp, sw), BF16)),
        scratch_shapes=[pltpu.VMEM((N_SEG, ns), F32), pltpu.VMEM((N_SEG, ns), F32)],
        compiler_params=_params(("arbitrary",)),
    )(u, bsr, bsi, csr, csi, a8r, a8i, c0r, c0i, d, gluw, glub, nw)
    return xr, xi, c0r, c0i, yp, out


def _s5_bwd(dout, u, yp, xr, xi, c0r, c0i, bsrt, bsit, csrt, csit, a8r, a8i, al8r, al8i, d, gluw, glub, nw, jb):
    lp, sw = u.shape
    ns = a8r.shape[1]
    rows = N_SEG * jb
    nblk = lp // rows
    secw = sw // N_SEC
    secn = ns // N_SEC

    def rowwise_bwd(dout_ref, yp_ref, gw_ref, gb_ref, nw_ref):
        ypv = yp_ref[...]
        t = jnp.tanh(GELU_K0 * (ypv + GELU_K1 * ypv * ypv * ypv))
        y1 = 0.5 * ypv * (1.0 + t)
        dgelu = 0.5 * (1.0 + t) + 0.5 * ypv * (1.0 - t * t) * GELU_K0 * (1.0 + 3.0 * GELU_K1 * ypv * ypv)
        gw = gw_ref[...]
        y1b = y1.astype(BF16)
        sg = _sigmoid(_dot(y1b, gw) + gb_ref[...])
        xh, r = _rms_stats(y1 * sg)
        dov = dout_ref[...]
        dy2 = _rms_bwd(dov, xh, r, nw_ref[...])
        dz = dy2 * y1 * sg * (1.0 - sg)
        dzb = dz.astype(BF16)
        dy1 = dy2 * sg + _dot_nt(dzb, gw)
        return dy1 * dgelu, dov * xh, y1b, dzb, dz

    def lam_scan(dyp_of, csrt_ref, csit_ref, ar_ref, ai_ref, lr_sc, li_sc, nr_sc, ni_sc, extra):
        for s in range(N_SEC):
            db = dyp_of(s)
            lr_sc[:, s * secn:(s + 1) * secn] = _dot(db, csrt_ref[s])
            li_sc[:, s * secn:(s + 1) * secn] = _dot(db, csit_ref[s])
        top = rows - 8
        _scan_step(lr_sc, li_sc, top, lambda cs: (nr_sc[:, cs], ni_sc[:, cs]), ar_ref, ai_ref, True, ns)
        extra(top, pl.ds(top - 8, 8))

        def step(jj, carry):
            r0 = pl.multiple_of((jb - 1 - jj) * 8, 8)
            rn = pl.multiple_of((jb - jj) * 8, 8)
            rp = pl.multiple_of((jb - 2 - jj) * 8, 8)
            _scan_step(lr_sc, li_sc, r0,
                       lambda cs: (lr_sc[pl.ds(rn, 8), cs], li_sc[pl.ds(rn, 8), cs]),
                       ar_ref, ai_ref, True, ns)
            extra(r0, pl.ds(rp, 8))
            return carry

        lax.fori_loop(1, jb - 1, step, 0)
        _scan_step(lr_sc, li_sc, 0, lambda cs: (lr_sc[8:16, cs], li_sc[8:16, cs]), ar_ref, ai_ref, True, ns)
        extra(0, None)
        nr_sc[...] = lr_sc[0:8, :]
        ni_sc[...] = li_sc[0:8, :]

    def carry_body(dout_ref, yp_ref, u_ref, gw_ref, gb_ref, nw_ref, csrt_ref, csit_ref, ar_ref, ai_ref,
                   alr_ref, ali_ref, cr_ref, ci_ref, dyp_ref, dnw_ref, dgw_ref, dgb_ref, dd_ref,
                   lr_sc, li_sc, nr_sc, ni_sc):
        b = pl.program_id(0)

        @pl.when(b == 0)
        def _():
            nr_sc[...] = jnp.zeros_like(nr_sc)
            ni_sc[...] = jnp.zeros_like(ni_sc)
            for ref in (dnw_ref, dgw_ref, dgb_ref, dd_ref):
                ref[...] = jnp.zeros_like(ref)

        dyp, dnw_rows, y1b, dzb, dz = rowwise_bwd(dout_ref, yp_ref, gw_ref, gb_ref, nw_ref)
        dnw_ref[...] += jnp.sum(dnw_rows, axis=0, keepdims=True)
        dgw_ref[...] += _dot_tn(y1b, dzb)
        dgb_ref[...] += jnp.sum(dz, axis=0, keepdims=True)
        dd_ref[...] += jnp.sum(dyp * u_ref[...], axis=0, keepdims=True)
        dyp_ref[...] = dyp.astype(BF16)
        lam_scan(lambda s: dyp_ref[:, s * secw:(s + 1) * secw], csrt_ref, csit_ref, ar_ref, ai_ref,
                 lr_sc, li_sc, nr_sc, ni_sc, lambda r0, prev_rows: None)

        @pl.when(b == nblk - 1)
        def _():
            fr = _shift_rows(nr_sc[...], False)
            fi = _shift_rows(ni_sc[...], False)
            alr, ali = alr_ref[...], ali_ref[...]
            cr, ci = fr, fi
            for _ in range(N_SEG - 2):
                sr = _shift_rows(cr, False)
                si = _shift_rows(ci, False)
                cr = fr + alr * sr + ali * si
                ci = fi + alr * si - ali * sr
            cr_ref[...] = cr
            ci_ref[...] = ci

    rev = lambda b: (nblk - 1 - b, 0)
    ublk = pl.BlockSpec((rows, sw), rev)
    xblk = pl.BlockSpec((rows, ns), rev)
    s8 = pl.BlockSpec((N_SEG, ns), lambda b: (0, 0))
    vec = pl.BlockSpec((1, sw), lambda b: (0, 0))
    gws = pl.BlockSpec((sw, sw), lambda b: (0, 0))
    btspec = pl.BlockSpec((N_SEC, secn, secw), lambda b: (0, 0, 0))
    ctspec = pl.BlockSpec((N_SEC, secw, secn), lambda b: (0, 0, 0))
    s8shape = jax.ShapeDtypeStruct((N_SEG, ns), F32)
    lcr, lci, dyp_all, d_nw, d_gw, d_gb, d_d = pl.pallas_call(
        carry_body, name="s5_bwd_carry", grid=(nblk,),
        in_specs=[ublk, ublk, ublk, gws, vec, vec, ctspec, ctspec, s8, s8, s8, s8],
        out_specs=(s8, s8, ublk, vec, gws, vec, vec),
        out_shape=(s8shape, s8shape, jax.ShapeDtypeStruct((lp, sw), BF16), jax.ShapeDtypeStruct((1, sw), F32),
                   jax.ShapeDtypeStruct((sw, sw), F32), jax.ShapeDtypeStruct((1, sw), F32),
                   jax.ShapeDtypeStruct((1, sw), F32)),
        scratch_shapes=[pltpu.VMEM((rows, ns), F32), pltpu.VMEM((rows, ns), F32),
                        pltpu.VMEM((N_SEG, ns), F32), pltpu.VMEM((N_SEG, ns), F32)],
        compiler_params=_params(("arbitrary",)),
    )(dout, yp, u, gluw, glub, nw, csrt, csit, a8r, a8i, al8r, al8i)

    def main_body(dyp_sc, u_ref, xr_ref, xi_ref, xtr_ref, xti_ref, c0r_ref, c0i_ref, lcr_ref, lci_ref,
                  d_ref, bsrt_ref, bsit_ref, csrt_ref, csit_ref, ar_ref, ai_ref,
                  du_ref, dcr_ref, dci_ref, dbr_ref, dbi_ref, dar_ref, dai_ref,
                  lr_sc, li_sc, nr_sc, ni_sc):
        b = pl.program_id(0)

        @pl.when(b == 0)
        def _():
            nr_sc[...] = lcr_ref[...]
            ni_sc[...] = lci_ref[...]
            for ref in (dcr_ref, dci_ref, dbr_ref, dbi_ref, dar_ref, dai_ref):
                ref[...] = jnp.zeros_like(ref)

        for s in range(N_SEC):
            db = dyp_sc[:, s * secw:(s + 1) * secw]
            xs = pl.ds(s * secn, secn)
            dcr_ref[s] += _dot_tn(xr_ref[:, xs].astype(BF16), db)
            dci_ref[s] += _dot_tn(xi_ref[:, xs].astype(BF16), db)

        first = b == nblk - 1

        def acc_da(r0, prev_rows):
            for cc in range(ns // SCAN_CW):
                cs = pl.ds(cc * SCAN_CW, SCAN_CW)
                lr = lr_sc[pl.ds(r0, 8), cs]
                li = li_sc[pl.ds(r0, 8), cs]
                if prev_rows is None:
                    xpr = jnp.where(first, c0r_ref[:, cs], xtr_ref[:, cs])
                    xpi = jnp.where(first, c0i_ref[:, cs], xti_ref[:, cs])
                else:
                    xpr = xr_ref[prev_rows, cs]
                    xpi = xi_ref[prev_rows, cs]
                dar_ref[:, cs] += lr * xpr + li * xpi
                dai_ref[:, cs] += li * xpr - lr * xpi

        lam_scan(lambda s: dyp_sc[:, s * secw:(s + 1) * secw], csrt_ref, csit_ref, ar_ref, ai_ref,
                 lr_sc, li_sc, nr_sc, ni_sc, acc_da)

        for s in range(N_SEC):
            xs = pl.ds(s * secn, secn)
            us = pl.ds(s * secw, secw)
            lrb = lr_sc[:, xs].astype(BF16)
            lib = li_sc[:, xs].astype(BF16)
            du = _dot(lrb, bsrt_ref[s]) + _dot(lib, bsit_ref[s]) + d_ref[:, us] * dyp_sc[:, us].astype(F32)
            du_ref[:, us] = du.astype(BF16)
            ub = u_ref[:, us].astype(BF16)
            dbr_ref[s] += _dot_tn(ub, lrb)
            dbi_ref[s] += _dot_tn(ub, lib)

    tail = pl.BlockSpec((N_SEG, ns), lambda b: (jnp.maximum((nblk - 1 - b) * jb - 1, 0), 0))
    acc_c = pl.BlockSpec((N_SEC, secn, secw), lambda b: (0, 0, 0))
    acc_b = pl.BlockSpec((N_SEC, secw, secn), lambda b: (0, 0, 0))
    du, dcr, dci, dbr, dbi, dar, dai = pl.pallas_call(
        main_body, name="s5_bwd", grid=(nblk,),
        in_specs=[ublk, ublk, xblk, xblk, tail, tail, s8, s8, s8, s8,
                  vec, btspec, btspec, ctspec, ctspec, s8, s8],
        out_specs=(ublk, acc_c, acc_c, acc_b, acc_b, s8, s8),
        out_shape=(jax.ShapeDtypeStruct((lp, sw), BF16),
                   jax.ShapeDtypeStruct((N_SEC, secn, secw), F32),
                   jax.ShapeDtypeStruct((N_SEC, secn, secw), F32),
                   jax.ShapeDtypeStruct((N_SEC, secw, secn), F32),
                   jax.ShapeDtypeStruct((N_SEC, secw, secn), F32),
                   s8shape, s8shape),
        scratch_shapes=[pltpu.VMEM((rows, ns), F32), pltpu.VMEM((rows, ns), F32),
                        pltpu.VMEM((N_SEG, ns), F32), pltpu.VMEM((N_SEG, ns), F32)],
        compiler_params=_params(("arbitrary",)),
    )(dyp_all, u, xr, xi, xr, xi, c0r, c0i, lcr, lci, d, bsrt, bsit, csrt, csit, a8r, a8i)
    return du, d_nw, d_gw, d_gb, d_d, dcr, dci, dbr, dbi, dar, dai


def _outproj_fwd(h, ret, ssm, wo):
    lp, d = h.shape
    nck, rs, _ = wo.shape
    rw = ret.shape[1]
    tm = _tile(lp, 640)
    per = rw // rs

    def body(h_ref, ret_ref, ssm_ref, w_ref, o_ref):
        acc = h_ref[...]
        for c in range(nck):
            src = ret_ref if c < per else ssm_ref
            lo = (c % per) * rs
            acc = acc + _dot(src[:, lo:lo + rs], w_ref[c])
        o_ref[...] = acc

    row = lambda w: pl.BlockSpec((tm, w), lambda i: (i, 0))
    return pl.pallas_call(
        body, name="outproj_fwd", grid=(lp // tm,),
        in_specs=[row(d), row(rw), row(ssm.shape[1]), pl.BlockSpec((nck, rs, d), lambda i: (0, 0, 0))],
        out_specs=row(d), out_shape=jax.ShapeDtypeStruct((lp, d), F32),
        compiler_params=_params(("arbitrary",)),
    )(h, ret, ssm, wo)


def _outproj_bwd(dh, ret, ssm, wo):
    lp, d = dh.shape
    nck, rs, _ = wo.shape
    rw = ret.shape[1]
    sw = ssm.shape[1]
    tm = _tile(lp, 640)
    per = rw // rs
    last = lp // tm - 1

    def body(dh_ref, ret_ref, ssm_ref, w_ref, dret_ref, dssm_ref, dw_ref, acc_sc):
        i = pl.program_id(0)

        @pl.when(i == 0)
        def _():
            acc_sc[...] = jnp.zeros_like(acc_sc)

        dhb = dh_ref[...].astype(BF16)
        for c in range(nck):
            src, dst = (ret_ref, dret_ref) if c < per else (ssm_ref, dssm_ref)
            lo = (c % per) * rs
            dst[:, lo:lo + rs] = _dot_nt(dhb, w_ref[c])
            acc_sc[c] += _dot_tn(src[:, lo:lo + rs], dhb)

        @pl.when(i == last)
        def _():
            dw_ref[...] = acc_sc[...].astype(BF16)

    row = lambda w: pl.BlockSpec((tm, w), lambda i: (i, 0))
    wsp = pl.BlockSpec((nck, rs, d), lambda i: (0, 0, 0))
    return pl.pallas_call(
        body, name="outproj_bwd", grid=(lp // tm,),
        in_specs=[row(d), row(rw), row(sw), wsp],
        out_specs=(row(rw), row(sw), wsp),
        out_shape=(jax.ShapeDtypeStruct((lp, rw), F32), jax.ShapeDtypeStruct((lp, sw), F32),
                   jax.ShapeDtypeStruct((nck, rs, d), BF16)),
        scratch_shapes=[pltpu.VMEM((nck, rs, d), F32)],
        compiler_params=_params(("arbitrary",)),
    )(dh, ret, ssm, wo)


def _loss_head(h, fw, target):
    lp, d = h.shape
    tm = _tile(lp, 640, CHUNK)
    sub = tm // CHUNK

    def body(h_ref, w_ref, *rest):
        t_refs = rest[:sub]
        loss_ref, dh_ref, dw_ref = rest[sub:]
        i = pl.program_id(0)

        @pl.when(i == 0)
        def _():
            loss_ref[...] = jnp.zeros_like(loss_ref)
            dw_ref[...] = jnp.zeros_like(dw_ref)

        w = w_ref[...]
        for j in range(sub):
            rows = pl.ds(j * CHUNK, CHUNK)
            xh, r = _rms_stats(h_ref[rows, :])
            err = xh * w - t_refs[j][...]
            if j == 0:
                err = jnp.where(i == 0, 0.0, err)
            loss_ref[...] += 0.5 * jnp.sum(err * err) / d
            dout = err * (1.0 / d)
            dw_ref[...] += jnp.sum(dout * xh, axis=0, keepdims=True)
            dh_ref[rows, :] = _rms_bwd(dout, xh, r, w)

    t_spec = lambda j: pl.BlockSpec((CHUNK, d), lambda i: (jnp.maximum(i * sub + j - 1, 0), 0))
    return pl.pallas_call(
        body, name="loss_head", grid=(lp // tm,),
        in_specs=[pl.BlockSpec((tm, d), lambda i: (i, 0)), pl.BlockSpec((1, d), lambda i: (0, 0))]
        + [t_spec(j) for j in range(sub)],
        out_specs=(pl.BlockSpec((8, LANE), lambda i: (0, 0)), pl.BlockSpec((tm, d), lambda i: (i, 0)),
                   pl.BlockSpec((1, d), lambda i: (0, 0))),
        out_shape=(jax.ShapeDtypeStruct((8, LANE), F32), jax.ShapeDtypeStruct((lp, d), F32),
                   jax.ShapeDtypeStruct((1, d), F32)),
        compiler_params=_params(("arbitrary",)),
    )(h, fw, *([target] * sub))


def _pack(arrs):
    flat = jnp.concatenate([a.reshape(-1).astype(F32) for a in arrs])
    n = flat.shape[0]
    rows = -(-n // (8 * LANE)) * 8
    return jnp.pad(flat, (0, rows * LANE - n)).reshape(rows, LANE)


def _unpack(packed, shapes):
    flat = packed.reshape(-1)
    out, off = [], 0
    for s in shapes:
        n = math.prod(s)
        out.append(flat[off:off + n].reshape(s))
        off += n
    return out


def _to_segments(a, seg_len):
    return a.reshape(N_SEG, seg_len, a.shape[1]).transpose(1, 0, 2).reshape(a.shape)


def _from_segments(a, seg_len):
    return a.reshape(seg_len, N_SEG, a.shape[1]).transpose(1, 0, 2).reshape(a.shape)


WEIGHT_NAMES = ['meta_tokens', 'ffn1_norm_w', 'ffn1_w_gate', 'ffn1_w_up', 'ffn1_w_down', 'mix_norm_w', 'w_in',
                'ret_norm_w', 'ssm_lambda_re', 'ssm_lambda_im', 'ssm_log_dt', 'ssm_b_re', 'ssm_b_im', 'ssm_c_re',
                'ssm_c_im', 'ssm_d', 'ssm_glu_w', 'ssm_glu_b', 'ssm_norm_w', 'w_out', 'ffn2_norm_w', 'ffn2_w_gate',
                'ffn2_w_up', 'ffn2_w_down', 'final_norm_w']
BIG = ['ffn1_w_gate', 'ffn1_w_up', 'ffn1_w_down', 'w_in', 'ssm_glu_w', 'w_out', 'ffn2_w_gate', 'ffn2_w_up',
       'ffn2_w_down']
TRANSPOSED = ['ffn1_w_gate', 'ffn1_w_up', 'ffn2_w_gate', 'ffn2_w_up']
BIG_EARLY = ['ffn1_w_gate', 'ffn1_w_up', 'ffn1_w_down']
BIG_LATE = [n for n in BIG if n not in BIG_EARLY]
SMALL = [n for n in WEIGHT_NAMES if n not in BIG]


def kernel(x, meta_tokens, ffn1_norm_w, ffn1_w_gate, ffn1_w_up, ffn1_w_down, mix_norm_w, w_in, ret_norm_w, ssm_lambda_re, ssm_lambda_im, ssm_log_dt, ssm_b_re, ssm_b_im, ssm_c_re, ssm_c_im, ssm_d, ssm_glu_w, ssm_glu_b, ssm_norm_w, w_out, ffn2_norm_w, ffn2_w_gate, ffn2_w_up, ffn2_w_down, final_norm_w, loss_target, m_meta_tokens, m_ffn1_norm_w, m_ffn1_w_gate, m_ffn1_w_up, m_ffn1_w_down, m_mix_norm_w, m_w_in, m_ret_norm_w, m_ssm_lambda_re, m_ssm_lambda_im, m_ssm_log_dt, m_ssm_b_re, m_ssm_b_im, m_ssm_c_re, m_ssm_c_im, m_ssm_d, m_ssm_glu_w, m_ssm_glu_b, m_ssm_norm_w, m_w_out, m_ffn2_norm_w, m_ffn2_w_gate, m_ffn2_w_up, m_ffn2_w_down, m_final_norm_w, v_meta_tokens, v_ffn1_norm_w, v_ffn1_w_gate, v_ffn1_w_up, v_ffn1_w_down, v_mix_norm_w, v_w_in, v_ret_norm_w, v_ssm_lambda_re, v_ssm_lambda_im, v_ssm_log_dt, v_ssm_b_re, v_ssm_b_im, v_ssm_c_re, v_ssm_c_im, v_ssm_d, v_ssm_glu_w, v_ssm_glu_b, v_ssm_norm_w, v_w_out, v_ffn2_norm_w, v_ffn2_w_gate, v_ffn2_w_up, v_ffn2_w_down, v_final_norm_w):
    args = locals()
    w = {n: args[n] for n in WEIGHT_NAMES}
    m = {n: args["m_" + n] for n in WEIGHT_NAMES}
    v = {n: args["v_" + n] for n in WEIGHT_NAMES}

    seq, d = x.shape[1], x.shape[2]
    lp = seq + CHUNK
    seg_len = lp // N_SEG
    rw = RET_HEADS * HEAD_DIM
    sw = ssm_d.shape[-1]
    groups = sw // SSM_GROUP
    ns = groups * SSM_STATE
    jb = _tile(seg_len, 40, 8)
    chip = 2 * lax.axis_index("x") + lax.axis_index("y")

    as_fd = lambda t: jnp.swapaxes(t, -1, -2)
    shards = {n: (as_fd(w[n][0]) if n in TRANSPOSED else w[n][0]).astype(BF16) for n in BIG}
    early = [shards[n] for n in BIG_EARLY] + [meta_tokens]
    gathered = _forward_sibling("gather_early_forward",
                                _exchange("gather_early", _allgather_chips_plan(early), early))
    gw = dict(zip(BIG_EARLY, gathered[:-1]))
    meta_full = jnp.transpose(gathered[-1], (1, 0, 2)).reshape(N_META, d)
    late = [shards[n] for n in BIG_LATE]

    pos = jnp.arange(lp, dtype=F32) - float(CHUNK - N_META)
    freqs = 1.0 / (ROPE_BASE ** (jnp.arange(0, HEAD_DIM, 2, dtype=F32) / HEAD_DIM))
    ang = pos[:, None] * freqs[None, :]
    cosf = jnp.concatenate([jnp.cos(ang), jnp.cos(ang)], axis=1)
    sinf = jnp.concatenate([-jnp.sin(ang), jnp.sin(ang)], axis=1)
    tables = _retention_tables()

    lam_re, lam_im, log_dt = ssm_lambda_re[0], ssm_lambda_im[0], ssm_log_dt[0]
    b_re, b_im, c_re, c_im = ssm_b_re[0], ssm_b_im[0], ssm_c_re[0], ssm_c_im[0]
    (ar, ai, bbr, bbi), prep_vjp = jax.vjp(_s5_prepare, lam_re, lam_im, log_dt, b_re, b_im)
    dt = jnp.exp(log_dt)[:, None]
    el = jnp.exp(seg_len * lam_re * dt)
    alr = el * jnp.cos(seg_len * lam_im * dt)
    ali = el * jnp.sin(seg_len * lam_im * dt)
    bc8 = lambda t: jnp.broadcast_to(t.reshape(1, ns), (N_SEG, ns))
    a8r, a8i, al8r, al8i = bc8(ar), bc8(ai), bc8(alr), bc8(ali)
    bsr = _blockdiag_in(jnp.transpose(bbr, (0, 2, 1)))
    bsi = _blockdiag_in(jnp.transpose(bbi, (0, 2, 1)))
    csrt = _blockdiag_in(c_re)
    csit = _blockdiag_in(-c_im)
    tr = lambda t: jnp.transpose(t, (0, 2, 1))
    bsr_b, bsi_b = bsr.astype(BF16), bsi.astype(BF16)
    csr_b, csi_b = tr(csrt).astype(BF16), tr(csit).astype(BF16)
    bsrt_b, bsit_b = tr(bsr).astype(BF16), tr(bsi).astype(BF16)
    csrt_b, csit_b = csrt.astype(BF16), csit.astype(BF16)

    h0 = jnp.concatenate([jnp.zeros((CHUNK - N_META, d), F32), meta_full, x[0]], axis=0)
    (h1, g1, u1), late_half = _ffn_fwd("ffn1_fwd", h0, ffn1_norm_w, gw['ffn1_w_gate'], gw['ffn1_w_up'],
                                       gw['ffn1_w_down'], _allgather_chips_plan(late), late)
    gw.update(zip(BIG_LATE, _forward_sibling("gather_late_forward", late_half)))
    glu_full = gw['ssm_glu_w'].reshape(sw, sw)
    n2, q, k, vv, gate, u = _inproj_fwd(h1, mix_norm_w, gw['w_in'], cosf, sinf, rw)
    o, ret, sprev = _ret_fwd(q, k, vv, gate, ret_norm_w, tables)
    u_seg = _to_segments(u, seg_len)
    xr, xi, c0r, c0i, yp, ssm_seg = _s5_fwd(u_seg, bsr_b, bsi_b, csr_b, csi_b, a8r, a8i, al8r, al8i,
                                            ssm_d, glu_full, ssm_glu_b, ssm_norm_w, jb)
    ssm = _from_segments(ssm_seg, seg_len)
    h2 = _outproj_fwd(h1, ret, ssm, gw['w_out'])
    (h3, g2, u2), _ = _ffn_fwd("ffn2_fwd", h2, ffn2_norm_w, gw['ffn2_w_gate'], gw['ffn2_w_up'], gw['ffn2_w_down'])
    loss_part, dh3, d_final = _loss_head(h3, final_norm_w.reshape(1, d), loss_target[0])

    (dh2, d_ffn2_norm, nb, daccb, ab, dgb, dub), _ = _ffn_bwd_act(
        "ffn2_bwd_act", dh3, h2, ffn2_norm_w, g2, u2, gw['ffn2_w_gate'], gw['ffn2_w_up'], gw['ffn2_w_down'])
    dwg2, dwu2, dwd2 = _ffn_bwd_w("ffn2_bwd_w", nb, daccb, ab, dgb, dub)
    dret, dssm, dwo = _outproj_bwd(dh2, ret, ssm, gw['w_out'])
    (du_seg, d_ssm_norm, d_glu_w, d_glu_b, d_ssm_d, dcr_s, dci_s, dbr_s, dbi_s, dar8, dai8) = _s5_bwd(
        _to_segments(dssm, seg_len), u_seg, yp, xr, xi, c0r, c0i, bsrt_b, bsit_b, csrt_b, csit_b,
        a8r, a8i, al8r, al8i, ssm_d, glu_full, ssm_glu_b, ssm_norm_w, jb)
    du = _from_segments(du_seg, seg_len)
    dq, dk, dv, dgate, d_ret_norm = _ret_bwd(dret, q, k, vv, gate, o, sprev, ret_norm_w, tables, cosf, sinf)
    dh1, d_mix_norm, dwin = _inproj_bwd(dh2, h1, mix_norm_w, n2, gw['w_in'], dq, dk, dv, dgate, du)
    late_parts = {
        'w_in': dwin, 'ssm_glu_w': d_glu_w.reshape(N_CHIP, sw // N_CHIP, sw).astype(BF16), 'w_out': dwo,
        'ffn2_w_gate': dwg2, 'ffn2_w_up': dwu2, 'ffn2_w_down': dwd2,
    }
    late_list = [late_parts[n] for n in BIG_LATE]
    (dh0, d_ffn1_norm, nb, daccb, ab, dgb, dub), late_recv = _ffn_bwd_act(
        "ffn1_bwd_act", dh1, h0, ffn1_norm_w, g1, u1, gw['ffn1_w_gate'], gw['ffn1_w_up'], gw['ffn1_w_down'],
        _alltoall_chips_plan(late_list), late_list)
    grad_x = dh0[CHUNK:][None]
    d_meta = dh0[CHUNK - N_META:CHUNK]

    d_c_re = jnp.transpose(_blockdiag_out(tr(dcr_s), groups, SSM_GROUP, SSM_STATE), (0, 1, 2))
    d_c_im = -_blockdiag_out(tr(dci_s), groups, SSM_GROUP, SSM_STATE)
    d_bbr = jnp.transpose(_blockdiag_out(dbr_s, groups, SSM_GROUP, SSM_STATE), (0, 2, 1))
    d_bbi = jnp.transpose(_blockdiag_out(dbi_s, groups, SSM_GROUP, SSM_STATE), (0, 2, 1))
    d_ar = jnp.sum(dar8, axis=0).reshape(groups, SSM_STATE)
    d_ai = jnp.sum(dai8, axis=0).reshape(groups, SSM_STATE)
    small_parts = [loss_part[0:1, :], d_meta, d_ffn1_norm, d_mix_norm, d_ret_norm, d_ar, d_ai, d_bbr, d_bbi,
                   d_c_re, d_c_im, d_ssm_d, d_glu_b, d_ssm_norm, d_ffn2_norm, d_final]
    small_shapes = [a.shape for a in small_parts]
    packed = _pack(small_parts)
    early_recv, (all_parts,) = _ffn_bwd_w_scatter("ffn1_bwd_w", nb, daccb, ab, dgb, dub, chip,
                                                  _allgather_all_plan([packed]), [packed])
    received = dict(zip(BIG_LATE + BIG_EARLY, late_recv + early_recv))
    chip_sums = [_sum_slots("sum_chips_" + n, received[n], BF16) for n in BIG]
    sib_sums = _swap_sibling("swap_sibling", chip_sums)
    (loss_row, g_meta_full, g_ffn1_norm, g_mix_norm, g_ret_norm, g_ar, g_ai, g_bbr, g_bbi, g_c_re, g_c_im,
     g_ssm_d, g_glu_b, g_ssm_norm, g_ffn2_norm, g_final) = _unpack(_sum_slots("sum_small", all_parts, F32), small_shapes)
    g_lam_re, g_lam_im, g_log_dt, g_b_re, g_b_im = prep_vjp((g_ar, g_ai, g_bbr, g_bbi))
    loss = loss_row[0, 0]
    g_meta = lax.dynamic_slice(g_meta_full, (0, chip * (d // N_CHIP)), (N_META, d // N_CHIP))
    small_grads = {
        'meta_tokens': g_meta, 'ffn1_norm_w': g_ffn1_norm, 'mix_norm_w': g_mix_norm, 'ret_norm_w': g_ret_norm,
        'ssm_lambda_re': g_lam_re[None], 'ssm_lambda_im': g_lam_im[None], 'ssm_log_dt': g_log_dt[None],
        'ssm_b_re': g_b_re[None], 'ssm_b_im': g_b_im[None], 'ssm_c_re': g_c_re[None], 'ssm_c_im': g_c_im[None],
        'ssm_d': g_ssm_d, 'ssm_glu_b': g_glu_b, 'ssm_norm_w': g_ssm_norm, 'ffn2_norm_w': g_ffn2_norm,
        'final_norm_w': g_final.reshape(d),
    }

    grads, deltas, new_m, new_v = {}, {}, {}, {}
    for n, mine, sib in zip(BIG, chip_sums, sib_sums):
        if n in TRANSPOSED:
            outs = _adam("adam_" + n, as_fd(w[n]), as_fd(m[n]), as_fd(v[n]), [mine, sib])
            grads[n], deltas[n], new_m[n], new_v[n] = [as_fd(t) for t in outs]
        else:
            grads[n], deltas[n], new_m[n], new_v[n] = _adam("adam_" + n, w[n], m[n], v[n], [mine, sib])
    sm_shapes = [w[n].shape for n in SMALL]
    sm_out = _adam("adam_small", _pack([w[n] for n in SMALL]), _pack([m[n] for n in SMALL]),
                   _pack([v[n] for n in SMALL]), [_pack([small_grads[n].reshape(w[n].shape) for n in SMALL])])
    for dst, packed in zip((grads, deltas, new_m, new_v), sm_out):
        for n, t in zip(SMALL, _unpack(packed, sm_shapes)):
            dst[n] = t

    return (loss, grad_x, *[grads[n] for n in WEIGHT_NAMES], *[deltas[n] for n in WEIGHT_NAMES],
            *[new_m[n] for n in WEIGHT_NAMES], *[new_v[n] for n in WEIGHT_NAMES])
```

```python
import functools
import math

import jax
import jax.numpy as jnp
from jax import lax
from jax.experimental import pallas as pl
from jax.experimental.pallas import tpu as pltpu

N_META = 16
RET_HEADS = 4
HEAD_DIM = 128
SSM_GROUP = 16
SSM_STATE = 64
CHUNK = 128
ROPE_BASE = 10000.0
EPS = 1e-6
FFN_RES = 0.5
N_SEG = 8
N_SEC = 4
N_CHIP = 4
LANE = 128
FFN_CPS = 2
BWD_W_ROWS = 1664

ADAM_LR = 0.001
ADAM_B1 = 0.9
ADAM_B2 = 0.999
ADAM_EPS = 1e-08
ADAM_WD = 0.01
ADAM_STEP = 10

VMEM_LIMIT = 56 * 1024 * 1024

F32 = jnp.float32
BF16 = jnp.bfloat16
MESH = pl.DeviceIdType.MESH


def _dot(a, b):
    return jnp.dot(a, b, preferred_element_type=F32)


def _dot_nt(a, b):
    return lax.dot_general(a, b, (((1,), (1,)), ((), ())), preferred_element_type=F32)


def _dot_tn(a, b):
    return lax.dot_general(a, b, (((0,), (0,)), ((), ())), preferred_element_type=F32)


def _tile(n, target, mult=64):
    best = None
    t = mult
    while t <= min(n, target):
        if n % t == 0:
            best = t
        t += mult
    assert best is not None, (n, target)
    return best


def _params(sem, vmem=VMEM_LIMIT):
    return pltpu.CompilerParams(dimension_semantics=sem, vmem_limit_bytes=vmem)


def _rms_stats(xf):
    r = lax.rsqrt(jnp.mean(xf * xf, axis=-1, keepdims=True) + EPS)
    return xf * r, r


def _rms_bwd(dy, xh, r, w):
    dxh = dy * w
    return r * (dxh - xh * jnp.mean(dxh * xh, axis=-1, keepdims=True))


def _sigmoid(x):
    return 0.5 * jnp.tanh(0.5 * x) + 0.5


GELU_K0 = math.sqrt(2.0 / math.pi)
GELU_K1 = 0.044715


CHIP_MASKS = [(1, 0, 0), (0, 1, 0), (1, 1, 0)]
ALL_MASKS = [(0, 0, 1), (0, 1, 0), (0, 1, 1), (1, 0, 0), (1, 0, 1), (1, 1, 0), (1, 1, 1)]
SIB_MASKS = [(0, 0, 1)]
ANY_SPEC = pl.BlockSpec(memory_space=pl.ANY)


class _Plan:
    def __init__(self, arrays, masks, n_slots, src_slotted, dst_slotted, local_copy, half=False, forward=False):
        self.shapes = [(a.shape, a.dtype) for a in arrays]
        self.n = len(arrays)
        self.masks = masks
        self.n_slots = n_slots
        self.src_slotted, self.dst_slotted, self.local_copy = src_slotted, dst_slotted, local_copy
        self.half, self.forward = half, forward
        self.n_cp = self.n * len(masks) * (len(CHIP_MASKS) if forward else 1)

    def out_shape(self):
        out = []
        for shp, dt in self.shapes:
            if self.dst_slotted and not self.src_slotted:
                shp = (self.n_slots,) + shp
            elif self.src_slotted and not self.dst_slotted:
                shp = shp[1:]
            out.append(jax.ShapeDtypeStruct(shp, dt))
        return tuple(out)

    def scratch(self):
        return [pltpu.SemaphoreType.DMA((self.n_cp,)), pltpu.SemaphoreType.DMA((self.n_cp,)),
                pltpu.SemaphoreType.DMA((self.n,))]

    def _slot(self, px, py, pc):
        if self.n_slots == 8:
            return 4 * px + 2 * py + pc
        if self.n_slots == 4:
            return 2 * px + py
        return pc

    def copies(self, ins, outs, sems):
        send_sems, recv_sems, loc_sems = sems
        x, y, c = lax.axis_index("x"), lax.axis_index("y"), lax.axis_index("c")
        me = self._slot(x, y, c)
        n_m = len(self.masks)
        cps = []
        for a in range(self.n):
            if self.forward:
                rows = self.shapes[a][0][-2] // 2
                mine = pl.ds(pl.multiple_of(c * rows, 8), rows)
                for j, (mx, my, _) in enumerate(CHIP_MASKS):
                    blk = outs[a].at[2 * (1 - x if mx else x) + (1 - y if my else y), mine]
                    k = a * len(CHIP_MASKS) + j
                    cps.append(pltpu.make_async_remote_copy(
                        src_ref=blk, dst_ref=blk, send_sem=send_sems.at[k], recv_sem=recv_sems.at[k],
                        device_id=(x, y, 1 - c), device_id_type=MESH))
                continue
            if self.local_copy:
                src = ins[a].at[me] if self.src_slotted else ins[a]
                cps.append(pltpu.make_async_copy(src, outs[a].at[me], loc_sems.at[a]))
            for mi, (mx, my, mc) in enumerate(self.masks):
                px = 1 - x if mx else x
                py = 1 - y if my else y
                pc = 1 - c if mc else c
                src = ins[a].at[self._slot(px, py, pc)] if self.src_slotted else ins[a]
                dst = outs[a].at[me] if self.dst_slotted else outs[a]
                if self.half:
                    rows = src.shape[-2] // 2
                    mine = pl.ds(pl.multiple_of(c * rows, 8), rows)
                    src, dst = src.at[mine], dst.at[mine]
                k = a * n_m + mi
                cps.append(pltpu.make_async_remote_copy(
                    src_ref=src, dst_ref=dst, send_sem=send_sems.at[k], recv_sem=recv_sems.at[k],
                    device_id=(px, py, pc), device_id_type=MESH))
        return cps


class _Plans:
    def __init__(self, plans):
        self.plans = plans
        self.n = sum(p.n for p in plans)
        self.forward = False

    def out_shape(self):
        return tuple(s for p in self.plans for s in p.out_shape())

    def scratch(self):
        return [s for p in self.plans for s in p.scratch()]

    def copies(self, ins, outs, sems):
        cps, a0 = [], 0
        for j, p in enumerate(self.plans):
            cps += p.copies(ins[a0:a0 + p.n], outs[a0:a0 + p.n], sems[3 * j:3 * j + 3])
            a0 += p.n
        return cps


def _exchange(name, plan, arrays):
    n = plan.n

    def body(*refs):
        cps = plan.copies(refs[:n], refs[n:2 * n], refs[2 * n:])
        for cp in cps:
            cp.start()
        for cp in cps:
            cp.wait()

    outs = pl.pallas_call(
        body, name=name, out_shape=plan.out_shape(),
        in_specs=[ANY_SPEC] * n, out_specs=tuple([ANY_SPEC] * n), scratch_shapes=plan.scratch(),
        input_output_aliases={i: i for i in range(n)} if plan.forward else {},
    )(*arrays)
    return list(outs)


def _pcall(body, *, name, grid, in_specs, out_specs, out_shape, scratch_shapes, args, plan=None, plan_args=()):
    sem = ("arbitrary",) * len(grid)
    if plan is None:
        return pl.pallas_call(body, name=name, grid=grid, in_specs=in_specs, out_specs=out_specs,
                              out_shape=out_shape, scratch_shapes=scratch_shapes,
                              compiler_params=_params(sem))(*args), []
    n_in, n_out, n_scr, n_p = len(in_specs), len(out_specs), len(scratch_shapes), plan.n

    def wrapped(*refs):
        ins = refs[:n_in]
        p_ins = refs[n_in:n_in + n_p]
        o0 = n_in + n_p
        outs = refs[o0:o0 + n_out]
        p_outs = refs[o0 + n_out:o0 + n_out + n_p]
        s0 = o0 + n_out + n_p
        scr = refs[s0:s0 + n_scr]
        sems = refs[s0 + n_scr:]
        ids = [pl.program_id(i) for i in range(len(grid))]
        first = functools.reduce(jnp.logical_and, [i == 0 for i in ids])
        last = functools.reduce(jnp.logical_and, [i == g - 1 for i, g in zip(ids, grid)])

        @pl.when(first)
        def _():
            for cp in plan.copies(p_ins, p_outs, sems):
                cp.start()

        body(*ins, *outs, *scr)

        @pl.when(last)
        def _():
            for cp in plan.copies(p_ins, p_outs, sems):
                cp.wait()

    res = pl.pallas_call(
        wrapped, name=name, grid=grid,
        in_specs=list(in_specs) + [ANY_SPEC] * n_p,
        out_specs=tuple(out_specs) + (ANY_SPEC,) * n_p,
        out_shape=tuple(out_shape) + plan.out_shape(),
        scratch_shapes=list(scratch_shapes) + plan.scratch(),
        compiler_params=_params(sem),
    )(*args, *plan_args)
    return res[:n_out], list(res[n_out:])


def _allgather_chips_plan(arrays):
    return _Plan(arrays, CHIP_MASKS, 4, False, True, True, half=True)


def _forward_sibling(name, gathered):
    return _exchange(name, _Plan(gathered, SIB_MASKS, 4, True, True, False, forward=True), gathered)


def _alltoall_chips_plan(arrays):
    return _Plan(arrays, CHIP_MASKS, 4, True, True, True)


def _swap_sibling(name, arrays):
    return _exchange(name, _Plan(arrays, SIB_MASKS, 2, False, False, False), arrays)


def _allgather_all_plan(arrays):
    return _Plan(arrays, ALL_MASKS, 8, False, True, True)


def _sum_slots(name, a, out_dtype):
    s, r, c = a.shape
    tr = _tile(r, 512, 8)

    def body(a_ref, o_ref):
        acc = a_ref[0].astype(F32)
        for i in range(1, s):
            acc = acc + a_ref[i].astype(F32)
        o_ref[...] = acc.astype(out_dtype)

    return pl.pallas_call(
        body, name=name, grid=(r // tr,),
        in_specs=[pl.BlockSpec((s, tr, c), lambda i: (0, i, 0))],
        out_specs=pl.BlockSpec((tr, c), lambda i: (i, 0)),
        out_shape=jax.ShapeDtypeStruct((r, c), out_dtype),
        compiler_params=_params(("arbitrary",)),
    )(a)


def _adam_math(w, g, m, v):
    m_new = ADAM_B1 * m + (1.0 - ADAM_B1) * g
    v_new = ADAM_B2 * v + (1.0 - ADAM_B2) * (g * g)
    m_hat = m_new / (1.0 - ADAM_B1 ** ADAM_STEP)
    v_hat = v_new / (1.0 - ADAM_B2 ** ADAM_STEP)
    delta = -ADAM_LR * (m_hat / (jnp.sqrt(v_hat) + ADAM_EPS) + ADAM_WD * w)
    return delta, m_new, v_new


def _adam(name, w, m, v, g_parts):
    r, c = w.shape[-2:]
    tr = _tile(r, 256, 8)
    n_g = len(g_parts)
    lead = w.ndim == 3
    at = (lambda ref: ref.at[0]) if lead else (lambda ref: ref)

    def body(*refs):
        w_ref, m_ref, v_ref = [at(t) for t in refs[:3]]
        g_refs = refs[3:3 + n_g]
        g_out, d_out, m_out, v_out = [at(t) for t in refs[3 + n_g:]]
        g = g_refs[0][...].astype(F32)
        for gr in g_refs[1:]:
            g = g + gr[...].astype(F32)
        delta, m_new, v_new = _adam_math(w_ref[...], g, m_ref[...], v_ref[...])
        g_out[...] = g
        d_out[...] = delta
        m_out[...] = m_new
        v_out[...] = v_new

    spec = pl.BlockSpec((tr, c), lambda i: (i, 0))
    wspec = pl.BlockSpec((1, tr, c), lambda i: (0, i, 0)) if lead else spec
    shp = jax.ShapeDtypeStruct(w.shape, F32)
    return pl.pallas_call(
        body, name=name, grid=(r // tr,),
        in_specs=[wspec] * 3 + [spec] * n_g, out_specs=(wspec,) * 4, out_shape=(shp,) * 4,
        compiler_params=_params(("arbitrary",)),
    )(w, m, v, *g_parts)


def _ffn_fwd(name, h, nw, wg, wu, wd, plan=None, plan_args=()):
    lp, d = h.shape
    nck, f, _ = wg.shape
    tm = _tile(lp, 640)
    last = nck // FFN_CPS - 1

    def body(h_ref, nw_ref, wg_ref, wu_ref, wd_ref, ho_ref, g_ref, u_ref, n_sc, acc_sc):
        k = pl.program_id(1)

        @pl.when(k == 0)
        def _():
            xh, _ = _rms_stats(h_ref[...])
            n_sc[...] = (xh * nw_ref[...]).astype(BF16)
            acc_sc[...] = jnp.zeros_like(acc_sc)

        n = n_sc[...]
        acc = acc_sc[...]
        for c in range(FFN_CPS):
            g = _dot_nt(n, wg_ref[c])
            u = _dot_nt(n, wu_ref[c])
            g_ref[c] = g.astype(BF16)
            u_ref[c] = u.astype(BF16)
            a = (g * _sigmoid(g) * u).astype(BF16)
            acc = acc + _dot(a, wd_ref[c])
        acc_sc[...] = acc

        @pl.when(k == last)
        def _():
            ho_ref[...] = h_ref[...] + FFN_RES * acc_sc[...]

    return _pcall(
        body, name=name, grid=(lp // tm, nck // FFN_CPS), plan=plan, plan_args=plan_args, args=(h, nw, wg, wu, wd),
        in_specs=[pl.BlockSpec((tm, d), lambda i, k: (i, 0)),
                  pl.BlockSpec((1, d), lambda i, k: (0, 0)),
                  pl.BlockSpec((FFN_CPS, f, d), lambda i, k: (k, 0, 0)),
                  pl.BlockSpec((FFN_CPS, f, d), lambda i, k: (k, 0, 0)),
                  pl.BlockSpec((FFN_CPS, f, d), lambda i, k: (k, 0, 0))],
        out_specs=(pl.BlockSpec((tm, d), lambda i, k: (i, 0)),
                   pl.BlockSpec((FFN_CPS, tm, f), lambda i, k: (k, i, 0)),
                   pl.BlockSpec((FFN_CPS, tm, f), lambda i, k: (k, i, 0))),
        out_shape=(jax.ShapeDtypeStruct((lp, d), F32),
                   jax.ShapeDtypeStruct((nck, lp, f), BF16),
                   jax.ShapeDtypeStruct((nck, lp, f), BF16)),
        scratch_shapes=[pltpu.VMEM((tm, d), BF16), pltpu.VMEM((tm, d), F32)])


def _ffn_bwd_hidden(name, dh, h, nw, g, u, wd, plan=None, plan_args=()):
    lp, d = h.shape
    nck, f, _ = wd.shape
    tm = _tile(lp, 640)

    def body(dh_ref, h_ref, nw_ref, g_ref, u_ref, wd_ref, n_ref, dacc_ref, a_ref, dg_ref, du_ref):
        k = pl.program_id(1)

        @pl.when(k == 0)
        def _():
            xh, _ = _rms_stats(h_ref[...])
            n_ref[...] = (xh * nw_ref[...]).astype(BF16)
            dacc_ref[...] = (FFN_RES * dh_ref[...]).astype(BF16)

        dacc = dacc_ref[...]
        for c in range(FFN_CPS):
            gv = g_ref[c].astype(F32)
            uv = u_ref[c].astype(F32)
            sg = _sigmoid(gv)
            sil = gv * sg
            da = _dot_nt(dacc, wd_ref[c])
            a_ref[c] = (sil * uv).astype(BF16)
            dg_ref[c] = (da * uv * (sg * (1.0 + gv * (1.0 - sg)))).astype(BF16)
            du_ref[c] = (da * sil).astype(BF16)

    row = pl.BlockSpec((tm, d), lambda i, k: (i, 0))
    vec = pl.BlockSpec((1, d), lambda i, k: (0, 0))
    hid = pl.BlockSpec((FFN_CPS, tm, f), lambda i, k: (k, i, 0))
    w_fd = pl.BlockSpec((FFN_CPS, f, d), lambda i, k: (k, 0, 0))
    rshape = jax.ShapeDtypeStruct((lp, d), BF16)
    hshape = jax.ShapeDtypeStruct((nck, lp, f), BF16)
    return _pcall(
        body, name=name, grid=(lp // tm, nck // FFN_CPS), plan=plan, plan_args=plan_args,
        args=(dh, h, nw, g, u, wd),
        in_specs=[row, row, vec, hid, hid, w_fd],
        out_specs=(row, row, hid, hid, hid), out_shape=(rshape, rshape, hshape, hshape, hshape),
        scratch_shapes=[])


def _ffn_bwd_input(name, dh, h, nw, dg, du, wg, wu, plan=None, plan_args=()):
    lp, d = h.shape
    nck, f, _ = wg.shape
    tm = _tile(lp, 640)
    last = nck // FFN_CPS - 1

    def body(dh_ref, h_ref, nw_ref, dg_ref, du_ref, wg_ref, wu_ref, dhi_ref, dnw_ref, dn_sc):
        i = pl.program_id(0)
        k = pl.program_id(1)

        @pl.when(jnp.logical_and(i == 0, k == 0))
        def _():
            dnw_ref[...] = jnp.zeros_like(dnw_ref)

        @pl.when(k == 0)
        def _():
            dn_sc[...] = jnp.zeros_like(dn_sc)

        dn = dn_sc[...]
        for c in range(FFN_CPS):
            dn = dn + _dot(dg_ref[c], wg_ref[c]) + _dot(du_ref[c], wu_ref[c])
        dn_sc[...] = dn

        @pl.when(k == last)
        def _():
            dnl = dn_sc[...]
            xh, r = _rms_stats(h_ref[...])
            dhi_ref[...] = dh_ref[...] + _rms_bwd(dnl, xh, r, nw_ref[...])
            dnw_ref[...] += jnp.sum(dnl * xh, axis=0, keepdims=True)

    row = pl.BlockSpec((tm, d), lambda i, k: (i, 0))
    vec = pl.BlockSpec((1, d), lambda i, k: (0, 0))
    hid = pl.BlockSpec((FFN_CPS, tm, f), lambda i, k: (k, i, 0))
    w_fd = pl.BlockSpec((FFN_CPS, f, d), lambda i, k: (k, 0, 0))
    return _pcall(
        body, name=name, grid=(lp // tm, nck // FFN_CPS), plan=plan, plan_args=plan_args,
        args=(dh, h, nw, dg, du, wg, wu),
        in_specs=[row, row, vec, hid, hid, w_fd, w_fd],
        out_specs=(row, vec),
        out_shape=(jax.ShapeDtypeStruct((lp, d), F32), jax.ShapeDtypeStruct((1, d), F32)),
        scratch_shapes=[pltpu.VMEM((tm, d), F32)])


def _ffn_bwd_w(name, n, dacc, a, dg, du, plan=None, plan_args=()):
    lp, d = n.shape
    nck, _, f = a.shape
    tm = _tile(lp, BWD_W_ROWS)
    last = lp // tm - 1

    def body(n_ref, dacc_ref, a_ref, dg_ref, du_ref, dwg_ref, dwu_ref, dwd_ref, ag_sc, au_sc, ad_sc):
        i = pl.program_id(1)

        @pl.when(i == 0)
        def _():
            ag_sc[...] = jnp.zeros_like(ag_sc)
            au_sc[...] = jnp.zeros_like(au_sc)
            ad_sc[...] = jnp.zeros_like(ad_sc)

        nv = n_ref[...]
        ag_sc[...] += _dot_tn(dg_ref[0], nv)
        au_sc[...] += _dot_tn(du_ref[0], nv)
        ad_sc[...] += _dot_tn(a_ref[0], dacc_ref[...])

        @pl.when(i == last)
        def _():
            dwg_ref[0] = ag_sc[...].astype(BF16)
            dwu_ref[0] = au_sc[...].astype(BF16)
            dwd_ref[0] = ad_sc[...].astype(BF16)

    row = pl.BlockSpec((tm, d), lambda k, i: (i, 0))
    hid = pl.BlockSpec((1, tm, f), lambda k, i: (k, i, 0))
    w_fd = pl.BlockSpec((1, f, d), lambda k, i: (k, 0, 0))
    wshape = jax.ShapeDtypeStruct((nck, f, d), BF16)
    return _pcall(
        body, name=name, grid=(nck, lp // tm), plan=plan, plan_args=plan_args, args=(n, dacc, a, dg, du),
        in_specs=[row, row, hid, hid, hid], out_specs=(w_fd, w_fd, w_fd), out_shape=(wshape,) * 3,
        scratch_shapes=[pltpu.VMEM((f, d), F32)] * 3)


def _ffn_bwd_w_scatter(name, n, dacc, a, dg, du, chip, plan, plan_args):
    lp, d = n.shape
    nck, _, f = a.shape
    tm = _tile(lp, BWD_W_ROWS)
    last_i = lp // tm - 1
    n_w = 3
    n_p = plan.n

    def body(me_ref, n_ref, dacc_ref, a_ref, dg_ref, du_ref, *rest):
        p_ins = rest[:n_p]
        recv = rest[n_p:n_p + n_w]
        p_outs = rest[n_p + n_w:2 * n_p + n_w]
        acc = rest[2 * n_p + n_w:2 * n_p + 2 * n_w]
        stage, send_sems, recv_sems, loc_sems = rest[2 * n_p + 2 * n_w:2 * n_p + 2 * n_w + 4]
        p_sems = rest[2 * n_p + 2 * n_w + 4:]
        p = pl.program_id(0)
        i = pl.program_id(1)
        me = me_ref[0]
        c = lax.axis_index("c")

        def send(w, pos):
            kk = jnp.bitwise_xor(me, nck - 1 - pos)
            diff = jnp.bitwise_xor(kk, me)
            m = jnp.where(diff == 2, 0, jnp.where(diff == 1, 1, 2))
            return pltpu.make_async_remote_copy(
                src_ref=stage.at[lax.rem(pos, 2), w], dst_ref=recv[w].at[me],
                send_sem=send_sems.at[w * 3 + m], recv_sem=recv_sems.at[w * 3 + m],
                device_id=(lax.div(kk, 2), lax.rem(kk, 2), c), device_id_type=MESH)

        @pl.when(jnp.logical_and(p == 0, i == 0))
        def _():
            for cp in plan.copies(p_ins, p_outs, p_sems):
                cp.start()

        @pl.when(i == 0)
        def _():
            for t in acc:
                t[...] = jnp.zeros_like(t)

        nv = n_ref[...]
        acc[0][...] += _dot_tn(dg_ref[0], nv)
        acc[1][...] += _dot_tn(du_ref[0], nv)
        acc[2][...] += _dot_tn(a_ref[0], dacc_ref[...])

        @pl.when(jnp.logical_and(i == last_i, p >= 2))
        def _():
            for w in range(n_w):
                send(w, p - 2).wait_send()

        @pl.when(i == last_i)
        def _():
            for w in range(n_w):
                stage[lax.rem(p, 2), w] = acc[w][...].astype(BF16)

        @pl.when(jnp.logical_and(i == last_i, p < nck - 1))
        def _():
            for w in range(n_w):
                send(w, p).start()

        @pl.when(jnp.logical_and(i == last_i, p == nck - 1))
        def _():
            own = [pltpu.make_async_copy(stage.at[(nck - 1) % 2, w], recv[w].at[me], loc_sems.at[w])
                   for w in range(n_w)]
            for cp in own:
                cp.start()
            for w in range(n_w):
                send(w, nck - 2).wait_send()
            for cp in own:
                cp.wait()
            for w in range(n_w):
                for m in range(3):
                    pltpu.make_async_remote_copy(
                        src_ref=stage.at[0, w], dst_ref=recv[w].at[me],
                        send_sem=send_sems.at[w * 3 + m], recv_sem=recv_sems.at[w * 3 + m],
                        device_id=(0, 0, c), device_id_type=MESH).wait_recv()
            for cp in plan.copies(p_ins, p_outs, p_sems):
                cp.wait()

    chunk = lambda k, me_ref: jnp.bitwise_xor(me_ref[0], nck - 1 - k)
    row = pl.BlockSpec((tm, d), lambda k, i, me_ref: (i, 0))
    hid = pl.BlockSpec((1, tm, f), lambda k, i, me_ref: (chunk(k, me_ref), i, 0))
    wshape = jax.ShapeDtypeStruct((nck, f, d), BF16)
    res = pl.pallas_call(
        body, name=name,
        grid_spec=pltpu.PrefetchScalarGridSpec(
            num_scalar_prefetch=1, grid=(nck, lp // tm),
            in_specs=[row, row, hid, hid, hid] + [ANY_SPEC] * n_p,
            out_specs=(ANY_SPEC,) * (n_w + n_p),
            scratch_shapes=[pltpu.VMEM((f, d), F32)] * n_w + [
                pltpu.VMEM((2, n_w, f, d), BF16), pltpu.SemaphoreType.DMA((n_w * 3,)),
                pltpu.SemaphoreType.DMA((n_w * 3,)), pltpu.SemaphoreType.DMA((n_w,))] + plan.scratch()),
        out_shape=(wshape,) * n_w + plan.out_shape(),
        compiler_params=_params(("arbitrary", "arbitrary")),
    )(chip.reshape(1).astype(jnp.int32), n, dacc, a, dg, du, *plan_args)
    return list(res[:n_w]), list(res[n_w:])


def _inproj_fwd(h, nw, w_in, cosf, sinf, rw):
    lp, d = h.shape
    nck, _, ps = w_in.shape
    proj = nck * ps
    sw = proj - 4 * rw
    tm = _tile(lp, 640)
    scale = HEAD_DIM ** -0.5
    heads = rw // HEAD_DIM

    def body(h_ref, nw_ref, w_ref, cos_ref, sin_ref, n_ref, q_ref, k_ref, v_ref, g_ref, u_ref, p_sc):
        xh, _ = _rms_stats(h_ref[...])
        n = (xh * nw_ref[...]).astype(BF16)
        n_ref[...] = n
        for c in range(nck):
            p_sc[:, c * ps:(c + 1) * ps] = _dot(n, w_ref[c])
        cs = cos_ref[...]
        sn = sin_ref[...]
        for hh in range(heads):
            lo = hh * HEAD_DIM
            qh = p_sc[:, lo:lo + HEAD_DIM]
            q_ref[:, lo:lo + HEAD_DIM] = (qh * cs + pltpu.roll(qh, HEAD_DIM // 2, 1) * sn).astype(BF16)
            kh = p_sc[:, rw + lo:rw + lo + HEAD_DIM]
            k_ref[:, lo:lo + HEAD_DIM] = ((kh * cs + pltpu.roll(kh, HEAD_DIM // 2, 1) * sn) * scale).astype(BF16)
        v_ref[...] = p_sc[:, 2 * rw:3 * rw].astype(BF16)
        g_ref[...] = p_sc[:, 3 * rw:4 * rw]
        u_ref[...] = p_sc[:, 4 * rw:]

    row = lambda w: pl.BlockSpec((tm, w), lambda i: (i, 0))
    return pl.pallas_call(
        body, name="inproj_fwd", grid=(lp // tm,),
        in_specs=[row(d), pl.BlockSpec((1, d), lambda i: (0, 0)),
                  pl.BlockSpec((nck, d, ps), lambda i: (0, 0, 0)), row(HEAD_DIM), row(HEAD_DIM)],
        out_specs=(row(d), row(rw), row(rw), row(rw), row(rw), row(sw)),
        out_shape=(jax.ShapeDtypeStruct((lp, d), BF16),
                   jax.ShapeDtypeStruct((lp, rw), BF16),
                   jax.ShapeDtypeStruct((lp, rw), BF16),
                   jax.ShapeDtypeStruct((lp, rw), BF16),
                   jax.ShapeDtypeStruct((lp, rw), F32),
                   jax.ShapeDtypeStruct((lp, sw), F32)),
        scratch_shapes=[pltpu.VMEM((tm, proj), F32)],
        compiler_params=_params(("arbitrary",)),
    )(h, nw, w_in, cosf, sinf)


def _inproj_bwd(dh, h, nw, n, w_in, dq, dk, dv, dg, du):
    lp, d = h.shape
    nck, _, ps = w_in.shape
    rw = dq.shape[1]
    sw = du.shape[1]
    proj = nck * ps
    tm = _tile(lp, 640)
    last = lp // tm - 1

    def gather_dproj(p_sc, dq_ref, dk_ref, dv_ref, dg_ref, du_ref):
        p_sc[:, 0:rw] = dq_ref[...]
        p_sc[:, rw:2 * rw] = dk_ref[...]
        p_sc[:, 2 * rw:3 * rw] = dv_ref[...]
        p_sc[:, 3 * rw:4 * rw] = dg_ref[...]
        p_sc[:, 4 * rw:] = du_ref[...]

    def act_body(dh_ref, h_ref, nw_ref, w_ref, dq_ref, dk_ref, dv_ref, dg_ref, du_ref, dhi_ref, dnw_ref, p_sc):
        i = pl.program_id(0)

        @pl.when(i == 0)
        def _():
            dnw_ref[...] = jnp.zeros_like(dnw_ref)

        gather_dproj(p_sc, dq_ref, dk_ref, dv_ref, dg_ref, du_ref)
        dn = jnp.zeros((tm, d), F32)
        for c in range(nck):
            dn = dn + _dot_nt(p_sc[:, c * ps:(c + 1) * ps], w_ref[c])
        xh, r = _rms_stats(h_ref[...])
        dhi_ref[...] = dh_ref[...] + _rms_bwd(dn, xh, r, nw_ref[...])
        dnw_ref[...] += jnp.sum(dn * xh, axis=0, keepdims=True)

    def w_body(n_ref, dq_ref, dk_ref, dv_ref, dg_ref, du_ref, dw_ref, p_sc, acc_sc):
        i = pl.program_id(0)

        @pl.when(i == 0)
        def _():
            acc_sc[...] = jnp.zeros_like(acc_sc)

        gather_dproj(p_sc, dq_ref, dk_ref, dv_ref, dg_ref, du_ref)
        nv = n_ref[...]
        for c in range(nck):
            acc_sc[c] += _dot_tn(nv, p_sc[:, c * ps:(c + 1) * ps])

        @pl.when(i == last)
        def _():
            dw_ref[...] = acc_sc[...].astype(BF16)

    row = lambda w: pl.BlockSpec((tm, w), lambda i: (i, 0))
    vec = pl.BlockSpec((1, d), lambda i: (0, 0))
    wsp = pl.BlockSpec((nck, d, ps), lambda i: (0, 0, 0))
    dproj_specs = [row(rw), row(rw), row(rw), row(rw), row(sw)]
    dhi, dnw = pl.pallas_call(
        act_body, name="inproj_bwd_act", grid=(lp // tm,),
        in_specs=[row(d), row(d), vec, wsp] + dproj_specs,
        out_specs=(row(d), vec),
        out_shape=(jax.ShapeDtypeStruct((lp, d), F32), jax.ShapeDtypeStruct((1, d), F32)),
        scratch_shapes=[pltpu.VMEM((tm, proj), BF16)],
        compiler_params=_params(("arbitrary",)),
    )(dh, h, nw, w_in, dq, dk, dv, dg, du)
    dw = pl.pallas_call(
        w_body, name="inproj_bwd_w", grid=(lp // tm,),
        in_specs=[row(d)] + dproj_specs,
        out_specs=wsp, out_shape=jax.ShapeDtypeStruct((nck, d, ps), BF16),
        scratch_shapes=[pltpu.VMEM((tm, proj), BF16), pltpu.VMEM((nck, d, ps), F32)],
        compiler_params=_params(("arbitrary",)),
    )(n, dq, dk, dv, dg, du)
    return dhi, dnw, dw


def _retention_tables():
    h = jnp.arange(RET_HEADS, dtype=F32)
    log_g = jnp.log(1.0 - 2.0 ** (-5.0 - h))
    i = jnp.arange(CHUNK)
    diff = i[:, None] - i[None, :]
    dec = jnp.where(diff[None] >= 0,
                    jnp.exp(log_g[:, None, None] * jnp.maximum(diff, 0)[None].astype(F32)), 0.0)
    pos = jnp.arange(CHUNK, dtype=F32)
    wq = jnp.exp(log_g[:, None] * (pos + 1.0)[None])
    wk = jnp.exp(log_g[:, None] * (CHUNK - 1 - pos)[None])
    gch = jnp.exp(log_g * CHUNK)
    ones = jnp.ones((1, 1, HEAD_DIM), F32)
    return (dec, wq[:, :, None] * ones, wk[:, :, None] * ones,
            gch[:, None, None] * jnp.ones((1, 8, HEAD_DIM), F32))


def _head_norm(o):
    mu = jnp.mean(o, axis=-1, keepdims=True)
    oc = o - mu
    r = lax.rsqrt(jnp.mean(oc * oc, axis=-1, keepdims=True) + EPS)
    return oc * r, r


def _ret_fwd(q, k, v, g, rnw, tables):
    lp, rw = q.shape
    heads = rw // HEAD_DIM
    nch = lp // CHUNK
    dec, wq, wk, gch = tables

    def body(q_ref, k_ref, v_ref, g_ref, w_ref, dec_ref, wq_ref, wk_ref, gch_ref,
             o_ref, ret_ref, sp_ref, s_sc):
        n = pl.program_id(0)

        @pl.when(n == 0)
        def _():
            s_sc[...] = jnp.zeros_like(s_sc)

        cols = [slice(hh * HEAD_DIM, (hh + 1) * HEAD_DIM) for hh in range(heads)]
        s_ins = [s_sc[hh] for hh in range(heads)]
        outs = []
        for hh, cs in enumerate(cols):
            qv, kv, vv = q_ref[:, cs], k_ref[:, cs], v_ref[:, cs]
            s_in = s_ins[hh]
            a = _dot_nt(qv, kv) * dec_ref[hh]
            qw = (qv.astype(F32) * wq_ref[hh]).astype(BF16)
            kw = (kv.astype(F32) * wk_ref[hh]).astype(BF16)
            o = _dot(a.astype(BF16), vv) + _dot(qw, s_in.astype(BF16))
            s_new = gch_ref[hh, 0:1, :] * s_in + _dot_tn(kw, vv)
            xh, _ = _head_norm(o)
            gv = g_ref[:, cs]
            outs.append((o, s_new, (gv * _sigmoid(gv) * (xh * w_ref[:, cs])).astype(BF16)))
        for hh, cs in enumerate(cols):
            o, s_new, ret = outs[hh]
            sp_ref[hh, 0] = s_ins[hh]
            s_sc[hh] = s_new
            o_ref[:, cs] = o
            ret_ref[:, cs] = ret

    blk = pl.BlockSpec((CHUNK, rw), lambda n: (n, 0))
    tab = pl.BlockSpec((heads, CHUNK, HEAD_DIM), lambda n: (0, 0, 0))
    return pl.pallas_call(
        body, name="retention_fwd", grid=(nch,),
        in_specs=[blk, blk, blk, blk, pl.BlockSpec((1, rw), lambda n: (0, 0)),
                  tab, tab, tab, pl.BlockSpec((heads, 8, HEAD_DIM), lambda n: (0, 0, 0))],
        out_specs=(blk, blk, pl.BlockSpec((heads, 1, HEAD_DIM, HEAD_DIM), lambda n: (0, n, 0, 0))),
        out_shape=(jax.ShapeDtypeStruct((lp, rw), F32),
                   jax.ShapeDtypeStruct((lp, rw), BF16),
                   jax.ShapeDtypeStruct((heads, nch, HEAD_DIM, HEAD_DIM), F32)),
        scratch_shapes=[pltpu.VMEM((heads, HEAD_DIM, HEAD_DIM), F32)],
        compiler_params=_params(("arbitrary",)),
    )(q, k, v, g, rnw, dec, wq, wk, gch)


def _ret_bwd(dret, q, k, v, g, o, sprev, rnw, tables, cosf, sinf):
    lp, rw = q.shape
    heads = rw // HEAD_DIM
    nch = lp // CHUNK
    dec, wq, wk, gch = tables
    scale = HEAD_DIM ** -0.5
    half = HEAD_DIM // 2

    def body(dret_ref, q_ref, k_ref, v_ref, g_ref, o_ref, sp_ref, w_ref, dec_ref, wq_ref, wk_ref, gch_ref,
             cos_ref, sin_ref, dq_ref, dk_ref, dv_ref, dg_ref, dw_ref, ds_sc):
        n = pl.program_id(0)

        @pl.when(n == 0)
        def _():
            ds_sc[...] = jnp.zeros_like(ds_sc)
            dw_ref[...] = jnp.zeros_like(dw_ref)

        cosv = cos_ref[...]
        sinv = sin_ref[...]
        cols = [slice(hh * HEAD_DIM, (hh + 1) * HEAD_DIM) for hh in range(heads)]
        ds_ins = [ds_sc[hh] for hh in range(heads)]
        dw_ins = [dw_ref[:, cs] for cs in cols]
        outs = []
        for hh, cs in enumerate(cols):
            qv, kv, vv = q_ref[:, cs], k_ref[:, cs], v_ref[:, cs]
            gv = g_ref[:, cs]
            dr = dret_ref[:, cs]
            w = w_ref[:, cs]
            sg = _sigmoid(gv)
            sil = gv * sg
            xh, r = _head_norm(o_ref[:, cs])
            dgate = (dr * (xh * w) * (sg * (1.0 + gv * (1.0 - sg)))).astype(BF16)
            dyw = dr * sil
            dw_new = dw_ins[hh] + jnp.sum(dyw * xh, axis=0, keepdims=True)
            dxh = dyw * w
            do = r * (dxh - jnp.mean(dxh, axis=-1, keepdims=True)
                      - xh * jnp.mean(dxh * xh, axis=-1, keepdims=True))
            dob = do.astype(BF16)
            dmask = dec_ref[hh]
            wqv = wq_ref[hh]
            wkv = wk_ref[hh]
            a = (_dot_nt(qv, kv) * dmask).astype(BF16)
            da = (_dot_nt(dob, vv) * dmask).astype(BF16)
            qw = (qv.astype(F32) * wqv).astype(BF16)
            kw = (kv.astype(F32) * wkv).astype(BF16)
            s_in = sp_ref[hh, 0].astype(BF16)
            ds = ds_ins[hh]
            dsb = ds.astype(BF16)
            dq = _dot(da, kv) + _dot_nt(dob, s_in) * wqv
            dk = _dot_tn(da, qv) + _dot_nt(vv, dsb) * wkv
            dv = _dot_tn(a, dob) + _dot(kw, dsb)
            ds_new = gch_ref[hh, 0:1, :] * ds + _dot_tn(qw, dob)
            outs.append((dgate, dw_new, ds_new,
                         (dq * cosv + pltpu.roll(dq * sinv, half, 1)).astype(BF16),
                         ((dk * cosv + pltpu.roll(dk * sinv, half, 1)) * scale).astype(BF16),
                         dv.astype(BF16)))
        for hh, cs in enumerate(cols):
            dgate, dw_new, ds_new, dqv, dkv, dvv = outs[hh]
            dg_ref[:, cs] = dgate
            dw_ref[:, cs] = dw_new
            ds_sc[hh] = ds_new
            dq_ref[:, cs] = dqv
            dk_ref[:, cs] = dkv
            dv_ref[:, cs] = dvv

    blk = pl.BlockSpec((CHUNK, rw), lambda n: (nch - 1 - n, 0))
    tab = pl.BlockSpec((heads, CHUNK, HEAD_DIM), lambda n: (0, 0, 0))
    wsp = pl.BlockSpec((1, rw), lambda n: (0, 0))
    pos = pl.BlockSpec((CHUNK, HEAD_DIM), lambda n: (nch - 1 - n, 0))
    bshape = jax.ShapeDtypeStruct((lp, rw), BF16)
    return pl.pallas_call(
        body, name="retention_bwd", grid=(nch,),
        in_specs=[blk, blk, blk, blk, blk, blk,
                  pl.BlockSpec((heads, 1, HEAD_DIM, HEAD_DIM), lambda n: (0, nch - 1 - n, 0, 0)),
                  wsp, tab, tab, tab, pl.BlockSpec((heads, 8, HEAD_DIM), lambda n: (0, 0, 0)), pos, pos],
        out_specs=(blk, blk, blk, blk, wsp),
        out_shape=(bshape, bshape, bshape, bshape, jax.ShapeDtypeStruct((1, rw), F32)),
        scratch_shapes=[pltpu.VMEM((heads, HEAD_DIM, HEAD_DIM), F32)],
        compiler_params=_params(("arbitrary",)),
    )(dret, q, k, v, g, o, sprev, rnw, dec, wq, wk, gch, cosf, sinf)


SCAN_CW = 512


def _s5_prepare(lam_re, lam_im, log_dt, b_re, b_im):
    dt = jnp.exp(log_dt)[:, None]
    er = jnp.exp(lam_re * dt)
    ar = er * jnp.cos(lam_im * dt)
    ai = er * jnp.sin(lam_im * dt)
    den = lam_re * lam_re + lam_im * lam_im
    fr = ((ar - 1.0) * lam_re + ai * lam_im) / den
    fi = (ai * lam_re - (ar - 1.0) * lam_im) / den
    bbr = fr[..., None] * b_re - fi[..., None] * b_im
    bbi = fr[..., None] * b_im + fi[..., None] * b_re
    return ar, ai, bbr, bbi


def _blockdiag_in(t):
    g, p, n = t.shape
    gs = g // N_SEC
    t = t.reshape(N_SEC, gs, p, n)
    eye = jnp.eye(gs, dtype=t.dtype)
    return jnp.einsum("sgpn,gh->sgphn", t, eye).reshape(N_SEC, gs * p, gs * n)


def _blockdiag_out(m, g, p, n):
    gs = g // N_SEC
    m = m.reshape(N_SEC, gs, p, gs, n)
    eye = jnp.eye(gs, dtype=m.dtype)
    return jnp.einsum("sgphn,gh->sgpn", m, eye).reshape(g, p, n)


def _scan_step(xr_ref, xi_ref, r0, pr_of, ar_ref, ai_ref, conj, ncols):
    for cc in range(ncols // SCAN_CW):
        cs = pl.ds(cc * SCAN_CW, SCAN_CW)
        pr, pi = pr_of(cs)
        ar = ar_ref[:, cs]
        ai = ai_ref[:, cs]
        if conj:
            nr = ar * pr + ai * pi
            ni = ar * pi - ai * pr
        else:
            nr = ar * pr - ai * pi
            ni = ar * pi + ai * pr
        xr_ref[pl.ds(r0, 8), cs] = xr_ref[pl.ds(r0, 8), cs] + nr
        xi_ref[pl.ds(r0, 8), cs] = xi_ref[pl.ds(r0, 8), cs] + ni


def _shift_rows(z, down):
    row = lax.broadcasted_iota(jnp.int32, z.shape, 0)
    if down:
        return jnp.where(row == 0, 0.0, pltpu.roll(z, 1, 0))
    return jnp.where(row == N_SEG - 1, 0.0, pltpu.roll(z, N_SEG - 1, 0))


def _s5_fwd(u, bsr, bsi, csr, csi, a8r, a8i, al8r, al8i, d, gluw, glub, nw, jb):
    lp, sw = u.shape
    ns = a8r.shape[1]
    rows = N_SEG * jb
    nblk = lp // rows
    secw = sw // N_SEC
    secn = ns // N_SEC

    def local_scan(u_ref, bsr_ref, bsi_ref, ar_ref, ai_ref, xr_ref, xi_ref, pr_sc, pi_sc):
        for s in range(N_SEC):
            ub = u_ref[:, s * secw:(s + 1) * secw].astype(BF16)
            xr_ref[:, s * secn:(s + 1) * secn] = _dot(ub, bsr_ref[s])
            xi_ref[:, s * secn:(s + 1) * secn] = _dot(ub, bsi_ref[s])
        _scan_step(xr_ref, xi_ref, 0, lambda cs: (pr_sc[:, cs], pi_sc[:, cs]), ar_ref, ai_ref, False, ns)

        def step(j, carry):
            r0 = pl.multiple_of(j * 8, 8)
            rp = pl.multiple_of((j - 1) * 8, 8)
            _scan_step(xr_ref, xi_ref, r0,
                       lambda cs: (xr_ref[pl.ds(rp, 8), cs], xi_ref[pl.ds(rp, 8), cs]),
                       ar_ref, ai_ref, False, ns)
            return carry

        lax.fori_loop(1, jb, step, 0)
        pr_sc[...] = xr_ref[rows - 8:rows, :]
        pi_sc[...] = xi_ref[rows - 8:rows, :]

    def carry_body(u_ref, bsr_ref, bsi_ref, ar_ref, ai_ref, alr_ref, ali_ref, cr_ref, ci_ref,
                   xr_sc, xi_sc, pr_sc, pi_sc):
        b = pl.program_id(0)

        @pl.when(b == 0)
        def _():
            pr_sc[...] = jnp.zeros_like(pr_sc)
            pi_sc[...] = jnp.zeros_like(pi_sc)

        local_scan(u_ref, bsr_ref, bsi_ref, ar_ref, ai_ref, xr_sc, xi_sc, pr_sc, pi_sc)

        @pl.when(b == nblk - 1)
        def _():
            er = _shift_rows(pr_sc[...], True)
            ei = _shift_rows(pi_sc[...], True)
            alr, ali = alr_ref[...], ali_ref[...]
            cr, ci = er, ei
            for _ in range(N_SEG - 2):
                sr = _shift_rows(cr, True)
                si = _shift_rows(ci, True)
                cr = er + alr * sr - ali * si
                ci = ei + alr * si + ali * sr
            cr_ref[...] = cr
            ci_ref[...] = ci

    ublk = pl.BlockSpec((rows, sw), lambda b: (b, 0))
    bspec = pl.BlockSpec((N_SEC, secw, secn), lambda b: (0, 0, 0))
    cspec = pl.BlockSpec((N_SEC, secn, secw), lambda b: (0, 0, 0))
    s8 = pl.BlockSpec((N_SEG, ns), lambda b: (0, 0))
    vec = pl.BlockSpec((1, sw), lambda b: (0, 0))
    s8shape = jax.ShapeDtypeStruct((N_SEG, ns), F32)
    c0r, c0i = pl.pallas_call(
        carry_body, name="s5_fwd_carry", grid=(nblk,),
        in_specs=[ublk, bspec, bspec, s8, s8, s8, s8],
        out_specs=(s8, s8), out_shape=(s8shape, s8shape),
        scratch_shapes=[pltpu.VMEM((rows, ns), F32), pltpu.VMEM((rows, ns), F32),
                        pltpu.VMEM((N_SEG, ns), F32), pltpu.VMEM((N_SEG, ns), F32)],
        compiler_params=_params(("arbitrary",)),
    )(u, bsr, bsi, a8r, a8i, al8r, al8i)

    def main_body(u_ref, bsr_ref, bsi_ref, csr_ref, csi_ref, ar_ref, ai_ref, c0r_ref, c0i_ref,
                  d_ref, gw_ref, gb_ref, nw_ref, xr_ref, xi_ref, yp_ref, out_ref, pr_sc, pi_sc):
        b = pl.program_id(0)

        @pl.when(b == 0)
        def _():
            pr_sc[...] = c0r_ref[...]
            pi_sc[...] = c0i_ref[...]

        local_scan(u_ref, bsr_ref, bsi_ref, ar_ref, ai_ref, xr_ref, xi_ref, pr_sc, pi_sc)
        for s in range(N_SEC):
            xs = pl.ds(s * secn, secn)
            us = pl.ds(s * secw, secw)
            y = _dot(xr_ref[:, xs].astype(BF16), csr_ref[s]) + _dot(xi_ref[:, xs].astype(BF16), csi_ref[s])
            yp_ref[:, us] = y + d_ref[:, us] * u_ref[:, us]
        yp = yp_ref[...]
        t = jnp.tanh(GELU_K0 * (yp + GELU_K1 * yp * yp * yp))
        y1 = 0.5 * yp * (1.0 + t)
        z = _dot(y1.astype(BF16), gw_ref[...]) + gb_ref[...]
        y2 = y1 * _sigmoid(z)
        xh, _ = _rms_stats(y2)
        out_ref[...] = (xh * nw_ref[...]).astype(BF16)

    xblk = pl.BlockSpec((rows, ns), lambda b: (b, 0))
    xr, xi, yp, out = pl.pallas_call(
        main_body, name="s5_fwd", grid=(nblk,),
        in_specs=[ublk, bspec, bspec, cspec, cspec, s8, s8, s8, s8, vec,
                  pl.BlockSpec((sw, sw), lambda b: (0, 0)), vec, vec],
        out_specs=(xblk, xblk, ublk, ublk),
        out_shape=(jax.ShapeDtypeStruct((lp, ns), F32), jax.ShapeDtypeStruct((lp, ns), F32),
                   jax.ShapeDtypeStruct((lp, sw), F32), jax.ShapeDtypeStruct((lp, sw), BF16)),
        scratch_shapes=[pltpu.VMEM((N_SEG, ns), F32), pltpu.VMEM((N_SEG, ns), F32)],
        compiler_params=_params(("arbitrary",)),
    )(u, bsr, bsi, csr, csi, a8r, a8i, c0r, c0i, d, gluw, glub, nw)
    return xr, xi, c0r, c0i, yp, out


def _s5_bwd(dout, u, yp, xr, xi, c0r, c0i, bsrt, bsit, csrt, csit, a8r, a8i, al8r, al8i, d, gluw, glub, nw, jb):
    lp, sw = u.shape
    ns = a8r.shape[1]
    rows = N_SEG * jb
    nblk = lp // rows
    secw = sw // N_SEC
    secn = ns // N_SEC

    def rowwise_bwd(dout_ref, yp_ref, gw_ref, gb_ref, nw_ref):
        ypv = yp_ref[...]
        t = jnp.tanh(GELU_K0 * (ypv + GELU_K1 * ypv * ypv * ypv))
        y1 = 0.5 * ypv * (1.0 + t)
        dgelu = 0.5 * (1.0 + t) + 0.5 * ypv * (1.0 - t * t) * GELU_K0 * (1.0 + 3.0 * GELU_K1 * ypv * ypv)
        gw = gw_ref[...]
        y1b = y1.astype(BF16)
        sg = _sigmoid(_dot(y1b, gw) + gb_ref[...])
        xh, r = _rms_stats(y1 * sg)
        dov = dout_ref[...]
        dy2 = _rms_bwd(dov, xh, r, nw_ref[...])
        dz = dy2 * y1 * sg * (1.0 - sg)
        dzb = dz.astype(BF16)
        dy1 = dy2 * sg + _dot_nt(dzb, gw)
        return dy1 * dgelu, dov * xh, y1b, dzb, dz

    def lam_scan(dyp_of, csrt_ref, csit_ref, ar_ref, ai_ref, lr_sc, li_sc, nr_sc, ni_sc, extra):
        for s in range(N_SEC):
            db = dyp_of(s)
            lr_sc[:, s * secn:(s + 1) * secn] = _dot(db, csrt_ref[s])
            li_sc[:, s * secn:(s + 1) * secn] = _dot(db, csit_ref[s])
        top = rows - 8
        _scan_step(lr_sc, li_sc, top, lambda cs: (nr_sc[:, cs], ni_sc[:, cs]), ar_ref, ai_ref, True, ns)
        extra(top, pl.ds(top - 8, 8))

        def step(jj, carry):
            r0 = pl.multiple_of((jb - 1 - jj) * 8, 8)
            rn = pl.multiple_of((jb - jj) * 8, 8)
            rp = pl.multiple_of((jb - 2 - jj) * 8, 8)
            _scan_step(lr_sc, li_sc, r0,
                       lambda cs: (lr_sc[pl.ds(rn, 8), cs], li_sc[pl.ds(rn, 8), cs]),
                       ar_ref, ai_ref, True, ns)
            extra(r0, pl.ds(rp, 8))
            return carry

        lax.fori_loop(1, jb - 1, step, 0)
        _scan_step(lr_sc, li_sc, 0, lambda cs: (lr_sc[8:16, cs], li_sc[8:16, cs]), ar_ref, ai_ref, True, ns)
        extra(0, None)
        nr_sc[...] = lr_sc[0:8, :]
        ni_sc[...] = li_sc[0:8, :]

    def carry_body(dout_ref, yp_ref, u_ref, gw_ref, gb_ref, nw_ref, csrt_ref, csit_ref, ar_ref, ai_ref,
                   alr_ref, ali_ref, cr_ref, ci_ref, dyp_ref, dnw_ref, dgw_ref, dgb_ref, dd_ref,
                   lr_sc, li_sc, nr_sc, ni_sc):
        b = pl.program_id(0)

        @pl.when(b == 0)
        def _():
            nr_sc[...] = jnp.zeros_like(nr_sc)
            ni_sc[...] = jnp.zeros_like(ni_sc)
            for ref in (dnw_ref, dgw_ref, dgb_ref, dd_ref):
                ref[...] = jnp.zeros_like(ref)

        dyp, dnw_rows, y1b, dzb, dz = rowwise_bwd(dout_ref, yp_ref, gw_ref, gb_ref, nw_ref)
        dnw_ref[...] += jnp.sum(dnw_rows, axis=0, keepdims=True)
        dgw_ref[...] += _dot_tn(y1b, dzb)
        dgb_ref[...] += jnp.sum(dz, axis=0, keepdims=True)
        dd_ref[...] += jnp.sum(dyp * u_ref[...], axis=0, keepdims=True)
        dyp_ref[...] = dyp.astype(BF16)
        lam_scan(lambda s: dyp_ref[:, s * secw:(s + 1) * secw], csrt_ref, csit_ref, ar_ref, ai_ref,
                 lr_sc, li_sc, nr_sc, ni_sc, lambda r0, prev_rows: None)

        @pl.when(b == nblk - 1)
        def _():
            fr = _shift_rows(nr_sc[...], False)
            fi = _shift_rows(ni_sc[...], False)
            alr, ali = alr_ref[...], ali_ref[...]
            cr, ci = fr, fi
            for _ in range(N_SEG - 2):
                sr = _shift_rows(cr, False)
                si = _shift_rows(ci, False)
                cr = fr + alr * sr + ali * si
                ci = fi + alr * si - ali * sr
            cr_ref[...] = cr
            ci_ref[...] = ci

    rev = lambda b: (nblk - 1 - b, 0)
    ublk = pl.BlockSpec((rows, sw), rev)
    xblk = pl.BlockSpec((rows, ns), rev)
    s8 = pl.BlockSpec((N_SEG, ns), lambda b: (0, 0))
    vec = pl.BlockSpec((1, sw), lambda b: (0, 0))
    gws = pl.BlockSpec((sw, sw), lambda b: (0, 0))
    btspec = pl.BlockSpec((N_SEC, secn, secw), lambda b: (0, 0, 0))
    ctspec = pl.BlockSpec((N_SEC, secw, secn), lambda b: (0, 0, 0))
    s8shape = jax.ShapeDtypeStruct((N_SEG, ns), F32)
    lcr, lci, dyp_all, d_nw, d_gw, d_gb, d_d = pl.pallas_call(
        carry_body, name="s5_bwd_carry", grid=(nblk,),
        in_specs=[ublk, ublk, ublk, gws, vec, vec, ctspec, ctspec, s8, s8, s8, s8],
        out_specs=(s8, s8, ublk, vec, gws, vec, vec),
        out_shape=(s8shape, s8shape, jax.ShapeDtypeStruct((lp, sw), BF16), jax.ShapeDtypeStruct((1, sw), F32),
                   jax.ShapeDtypeStruct((sw, sw), F32), jax.ShapeDtypeStruct((1, sw), F32),
                   jax.ShapeDtypeStruct((1, sw), F32)),
        scratch_shapes=[pltpu.VMEM((rows, ns), F32), pltpu.VMEM((rows, ns), F32),
                        pltpu.VMEM((N_SEG, ns), F32), pltpu.VMEM((N_SEG, ns), F32)],
        compiler_params=_params(("arbitrary",)),
    )(dout, yp, u, gluw, glub, nw, csrt, csit, a8r, a8i, al8r, al8i)

    def main_body(dyp_sc, u_ref, xr_ref, xi_ref, xtr_ref, xti_ref, c0r_ref, c0i_ref, lcr_ref, lci_ref,
                  d_ref, bsrt_ref, bsit_ref, csrt_ref, csit_ref, ar_ref, ai_ref,
                  du_ref, dcr_ref, dci_ref, dbr_ref, dbi_ref, dar_ref, dai_ref,
                  lr_sc, li_sc, nr_sc, ni_sc):
        b = pl.program_id(0)

        @pl.when(b == 0)
        def _():
            nr_sc[...] = lcr_ref[...]
            ni_sc[...] = lci_ref[...]
            for ref in (dcr_ref, dci_ref, dbr_ref, dbi_ref, dar_ref, dai_ref):
                ref[...] = jnp.zeros_like(ref)

        for s in range(N_SEC):
            db = dyp_sc[:, s * secw:(s + 1) * secw]
            xs = pl.ds(s * secn, secn)
            dcr_ref[s] += _dot_tn(xr_ref[:, xs].astype(BF16), db)
            dci_ref[s] += _dot_tn(xi_ref[:, xs].astype(BF16), db)

        first = b == nblk - 1

        def acc_da(r0, prev_rows):
            for cc in range(ns // SCAN_CW):
                cs = pl.ds(cc * SCAN_CW, SCAN_CW)
                lr = lr_sc[pl.ds(r0, 8), cs]
                li = li_sc[pl.ds(r0, 8), cs]
                if prev_rows is None:
                    xpr = jnp.where(first, c0r_ref[:, cs], xtr_ref[:, cs])
                    xpi = jnp.where(first, c0i_ref[:, cs], xti_ref[:, cs])
                else:
                    xpr = xr_ref[prev_rows, cs]
                    xpi = xi_ref[prev_rows, cs]
                dar_ref[:, cs] += lr * xpr + li * xpi
                dai_ref[:, cs] += li * xpr - lr * xpi

        lam_scan(lambda s: dyp_sc[:, s * secw:(s + 1) * secw], csrt_ref, csit_ref, ar_ref, ai_ref,
                 lr_sc, li_sc, nr_sc, ni_sc, acc_da)

        for s in range(N_SEC):
            xs = pl.ds(s * secn, secn)
            us = pl.ds(s * secw, secw)
            lrb = lr_sc[:, xs].astype(BF16)
            lib = li_sc[:, xs].astype(BF16)
            du = _dot(lrb, bsrt_ref[s]) + _dot(lib, bsit_ref[s]) + d_ref[:, us] * dyp_sc[:, us].astype(F32)
            du_ref[:, us] = du.astype(BF16)
            ub = u_ref[:, us].astype(BF16)
            dbr_ref[s] += _dot_tn(ub, lrb)
            dbi_ref[s] += _dot_tn(ub, lib)

    tail = pl.BlockSpec((N_SEG, ns), lambda b: (jnp.maximum((nblk - 1 - b) * jb - 1, 0), 0))
    acc_c = pl.BlockSpec((N_SEC, secn, secw), lambda b: (0, 0, 0))
    acc_b = pl.BlockSpec((N_SEC, secw, secn), lambda b: (0, 0, 0))
    du, dcr, dci, dbr, dbi, dar, dai = pl.pallas_call(
        main_body, name="s5_bwd", grid=(nblk,),
        in_specs=[ublk, ublk, xblk, xblk, tail, tail, s8, s8, s8, s8,
                  vec, btspec, btspec, ctspec, ctspec, s8, s8],
        out_specs=(ublk, acc_c, acc_c, acc_b, acc_b, s8, s8),
        out_shape=(jax.ShapeDtypeStruct((lp, sw), BF16),
                   jax.ShapeDtypeStruct((N_SEC, secn, secw), F32),
                   jax.ShapeDtypeStruct((N_SEC, secn, secw), F32),
                   jax.ShapeDtypeStruct((N_SEC, secw, secn), F32),
                   jax.ShapeDtypeStruct((N_SEC, secw, secn), F32),
                   s8shape, s8shape),
        scratch_shapes=[pltpu.VMEM((rows, ns), F32), pltpu.VMEM((rows, ns), F32),
                        pltpu.VMEM((N_SEG, ns), F32), pltpu.VMEM((N_SEG, ns), F32)],
        compiler_params=_params(("arbitrary",)),
    )(dyp_all, u, xr, xi, xr, xi, c0r, c0i, lcr, lci, d, bsrt, bsit, csrt, csit, a8r, a8i)
    return du, d_nw, d_gw, d_gb, d_d, dcr, dci, dbr, dbi, dar, dai


def _outproj_fwd(h, ret, ssm, wo):
    lp, d = h.shape
    nck, rs, _ = wo.shape
    rw = ret.shape[1]
    tm = _tile(lp, 640)
    per = rw // rs

    def body(h_ref, ret_ref, ssm_ref, w_ref, o_ref):
        acc = h_ref[...]
        for c in range(nck):
            src = ret_ref if c < per else ssm_ref
            lo = (c % per) * rs
            acc = acc + _dot(src[:, lo:lo + rs], w_ref[c])
        o_ref[...] = acc

    row = lambda w: pl.BlockSpec((tm, w), lambda i: (i, 0))
    return pl.pallas_call(
        body, name="outproj_fwd", grid=(lp // tm,),
        in_specs=[row(d), row(rw), row(ssm.shape[1]), pl.BlockSpec((nck, rs, d), lambda i: (0, 0, 0))],
        out_specs=row(d), out_shape=jax.ShapeDtypeStruct((lp, d), F32),
        compiler_params=_params(("arbitrary",)),
    )(h, ret, ssm, wo)


def _outproj_bwd(dh, ret, ssm, wo):
    lp, d = dh.shape
    nck, rs, _ = wo.shape
    rw = ret.shape[1]
    sw = ssm.shape[1]
    tm = _tile(lp, 640)
    per = rw // rs
    last = lp // tm - 1

    def body(dh_ref, ret_ref, ssm_ref, w_ref, dret_ref, dssm_ref, dw_ref, acc_sc):
        i = pl.program_id(0)

        @pl.when(i == 0)
        def _():
            acc_sc[...] = jnp.zeros_like(acc_sc)

        dhb = dh_ref[...].astype(BF16)
        for c in range(nck):
            src, dst = (ret_ref, dret_ref) if c < per else (ssm_ref, dssm_ref)
            lo = (c % per) * rs
            dst[:, lo:lo + rs] = _dot_nt(dhb, w_ref[c])
            acc_sc[c] += _dot_tn(src[:, lo:lo + rs], dhb)

        @pl.when(i == last)
        def _():
            dw_ref[...] = acc_sc[...].astype(BF16)

    row = lambda w: pl.BlockSpec((tm, w), lambda i: (i, 0))
    wsp = pl.BlockSpec((nck, rs, d), lambda i: (0, 0, 0))
    return pl.pallas_call(
        body, name="outproj_bwd", grid=(lp // tm,),
        in_specs=[row(d), row(rw), row(sw), wsp],
        out_specs=(row(rw), row(sw), wsp),
        out_shape=(jax.ShapeDtypeStruct((lp, rw), F32), jax.ShapeDtypeStruct((lp, sw), F32),
                   jax.ShapeDtypeStruct((nck, rs, d), BF16)),
        scratch_shapes=[pltpu.VMEM((nck, rs, d), F32)],
        compiler_params=_params(("arbitrary",)),
    )(dh, ret, ssm, wo)


def _loss_head(h, fw, target):
    lp, d = h.shape
    tm = _tile(lp, 640, CHUNK)
    sub = tm // CHUNK

    def body(h_ref, w_ref, *rest):
        t_refs = rest[:sub]
        loss_ref, dh_ref, dw_ref = rest[sub:]
        i = pl.program_id(0)

        @pl.when(i == 0)
        def _():
            loss_ref[...] = jnp.zeros_like(loss_ref)
            dw_ref[...] = jnp.zeros_like(dw_ref)

        w = w_ref[...]
        for j in range(sub):
            rows = pl.ds(j * CHUNK, CHUNK)
            xh, r = _rms_stats(h_ref[rows, :])
            err = xh * w - t_refs[j][...]
            if j == 0:
                err = jnp.where(i == 0, 0.0, err)
            loss_ref[...] += 0.5 * jnp.sum(err * err) / d
            dout = err * (1.0 / d)
            dw_ref[...] += jnp.sum(dout * xh, axis=0, keepdims=True)
            dh_ref[rows, :] = _rms_bwd(dout, xh, r, w)

    t_spec = lambda j: pl.BlockSpec((CHUNK, d), lambda i: (jnp.maximum(i * sub + j - 1, 0), 0))
    return pl.pallas_call(
        body, name="loss_head", grid=(lp // tm,),
        in_specs=[pl.BlockSpec((tm, d), lambda i: (i, 0)), pl.BlockSpec((1, d), lambda i: (0, 0))]
        + [t_spec(j) for j in range(sub)],
        out_specs=(pl.BlockSpec((8, LANE), lambda i: (0, 0)), pl.BlockSpec((tm, d), lambda i: (i, 0)),
                   pl.BlockSpec((1, d), lambda i: (0, 0))),
        out_shape=(jax.ShapeDtypeStruct((8, LANE), F32), jax.ShapeDtypeStruct((lp, d), F32),
                   jax.ShapeDtypeStruct((1, d), F32)),
        compiler_params=_params(("arbitrary",)),
    )(h, fw, *([target] * sub))


def _pack(arrs):
    flat = jnp.concatenate([a.reshape(-1).astype(F32) for a in arrs])
    n = flat.shape[0]
    rows = -(-n // (8 * LANE)) * 8
    return jnp.pad(flat, (0, rows * LANE - n)).reshape(rows, LANE)


def _unpack(packed, shapes):
    flat = packed.reshape(-1)
    out, off = [], 0
    for s in shapes:
        n = math.prod(s)
        out.append(flat[off:off + n].reshape(s))
        off += n
    return out


def _to_segments(a, seg_len):
    return a.reshape(N_SEG, seg_len, a.shape[1]).transpose(1, 0, 2).reshape(a.shape)


def _from_segments(a, seg_len):
    return a.reshape(seg_len, N_SEG, a.shape[1]).transpose(1, 0, 2).reshape(a.shape)


WEIGHT_NAMES = ['meta_tokens', 'ffn1_norm_w', 'ffn1_w_gate', 'ffn1_w_up', 'ffn1_w_down', 'mix_norm_w', 'w_in',
                'ret_norm_w', 'ssm_lambda_re', 'ssm_lambda_im', 'ssm_log_dt', 'ssm_b_re', 'ssm_b_im', 'ssm_c_re',
                'ssm_c_im', 'ssm_d', 'ssm_glu_w', 'ssm_glu_b', 'ssm_norm_w', 'w_out', 'ffn2_norm_w', 'ffn2_w_gate',
                'ffn2_w_up', 'ffn2_w_down', 'final_norm_w']
BIG = ['ffn1_w_gate', 'ffn1_w_up', 'ffn1_w_down', 'w_in', 'ssm_glu_w', 'w_out', 'ffn2_w_gate', 'ffn2_w_up',
       'ffn2_w_down']
TRANSPOSED = ['ffn1_w_gate', 'ffn1_w_up', 'ffn2_w_gate', 'ffn2_w_up']
BIG_EARLY = ['ffn1_w_gate', 'ffn1_w_up', 'ffn1_w_down']
BIG_LATE = [n for n in BIG if n not in BIG_EARLY]
SMALL = [n for n in WEIGHT_NAMES if n not in BIG]


def kernel(x, meta_tokens, ffn1_norm_w, ffn1_w_gate, ffn1_w_up, ffn1_w_down, mix_norm_w, w_in, ret_norm_w, ssm_lambda_re, ssm_lambda_im, ssm_log_dt, ssm_b_re, ssm_b_im, ssm_c_re, ssm_c_im, ssm_d, ssm_glu_w, ssm_glu_b, ssm_norm_w, w_out, ffn2_norm_w, ffn2_w_gate, ffn2_w_up, ffn2_w_down, final_norm_w, loss_target, m_meta_tokens, m_ffn1_norm_w, m_ffn1_w_gate, m_ffn1_w_up, m_ffn1_w_down, m_mix_norm_w, m_w_in, m_ret_norm_w, m_ssm_lambda_re, m_ssm_lambda_im, m_ssm_log_dt, m_ssm_b_re, m_ssm_b_im, m_ssm_c_re, m_ssm_c_im, m_ssm_d, m_ssm_glu_w, m_ssm_glu_b, m_ssm_norm_w, m_w_out, m_ffn2_norm_w, m_ffn2_w_gate, m_ffn2_w_up, m_ffn2_w_down, m_final_norm_w, v_meta_tokens, v_ffn1_norm_w, v_ffn1_w_gate, v_ffn1_w_up, v_ffn1_w_down, v_mix_norm_w, v_w_in, v_ret_norm_w, v_ssm_lambda_re, v_ssm_lambda_im, v_ssm_log_dt, v_ssm_b_re, v_ssm_b_im, v_ssm_c_re, v_ssm_c_im, v_ssm_d, v_ssm_glu_w, v_ssm_glu_b, v_ssm_norm_w, v_w_out, v_ffn2_norm_w, v_ffn2_w_gate, v_ffn2_w_up, v_ffn2_w_down, v_final_norm_w):
    args = locals()
    w = {n: args[n] for n in WEIGHT_NAMES}
    m = {n: args["m_" + n] for n in WEIGHT_NAMES}
    v = {n: args["v_" + n] for n in WEIGHT_NAMES}

    seq, d = x.shape[1], x.shape[2]
    lp = seq + CHUNK
    seg_len = lp // N_SEG
    rw = RET_HEADS * HEAD_DIM
    sw = ssm_d.shape[-1]
    groups = sw // SSM_GROUP
    ns = groups * SSM_STATE
    jb = _tile(seg_len, 40, 8)
    chip = 2 * lax.axis_index("x") + lax.axis_index("y")

    as_fd = lambda t: jnp.swapaxes(t, -1, -2)
    shards = {n: (as_fd(w[n][0]) if n in TRANSPOSED else w[n][0]).astype(BF16) for n in BIG}
    early = [shards[n] for n in BIG_EARLY] + [meta_tokens]
    gathered = _forward_sibling("gather_early_forward",
                                _exchange("gather_early", _allgather_chips_plan(early), early))
    gw = dict(zip(BIG_EARLY, gathered[:-1]))
    meta_full = jnp.transpose(gathered[-1], (1, 0, 2)).reshape(N_META, d)
    late = [shards[n] for n in BIG_LATE]

    pos = jnp.arange(lp, dtype=F32) - float(CHUNK - N_META)
    freqs = 1.0 / (ROPE_BASE ** (jnp.arange(0, HEAD_DIM, 2, dtype=F32) / HEAD_DIM))
    ang = pos[:, None] * freqs[None, :]
    cosf = jnp.concatenate([jnp.cos(ang), jnp.cos(ang)], axis=1)
    sinf = jnp.concatenate([-jnp.sin(ang), jnp.sin(ang)], axis=1)
    tables = _retention_tables()

    lam_re, lam_im, log_dt = ssm_lambda_re[0], ssm_lambda_im[0], ssm_log_dt[0]
    b_re, b_im, c_re, c_im = ssm_b_re[0], ssm_b_im[0], ssm_c_re[0], ssm_c_im[0]
    (ar, ai, bbr, bbi), prep_vjp = jax.vjp(_s5_prepare, lam_re, lam_im, log_dt, b_re, b_im)
    dt = jnp.exp(log_dt)[:, None]
    el = jnp.exp(seg_len * lam_re * dt)
    alr = el * jnp.cos(seg_len * lam_im * dt)
    ali = el * jnp.sin(seg_len * lam_im * dt)
    bc8 = lambda t: jnp.broadcast_to(t.reshape(1, ns), (N_SEG, ns))
    a8r, a8i, al8r, al8i = bc8(ar), bc8(ai), bc8(alr), bc8(ali)
    bsr = _blockdiag_in(jnp.transpose(bbr, (0, 2, 1)))
    bsi = _blockdiag_in(jnp.transpose(bbi, (0, 2, 1)))
    csrt = _blockdiag_in(c_re)
    csit = _blockdiag_in(-c_im)
    tr = lambda t: jnp.transpose(t, (0, 2, 1))
    bsr_b, bsi_b = bsr.astype(BF16), bsi.astype(BF16)
    csr_b, csi_b = tr(csrt).astype(BF16), tr(csit).astype(BF16)
    bsrt_b, bsit_b = tr(bsr).astype(BF16), tr(bsi).astype(BF16)
    csrt_b, csit_b = csrt.astype(BF16), csit.astype(BF16)

    h0 = jnp.concatenate([jnp.zeros((CHUNK - N_META, d), F32), meta_full, x[0]], axis=0)
    (h1, g1, u1), late_half = _ffn_fwd("ffn1_fwd", h0, ffn1_norm_w, gw['ffn1_w_gate'], gw['ffn1_w_up'],
                                       gw['ffn1_w_down'], _allgather_chips_plan(late), late)
    gw.update(zip(BIG_LATE, _forward_sibling("gather_late_forward", late_half)))
    glu_full = gw['ssm_glu_w'].reshape(sw, sw)
    n2, q, k, vv, gate, u = _inproj_fwd(h1, mix_norm_w, gw['w_in'], cosf, sinf, rw)
    o, ret, sprev = _ret_fwd(q, k, vv, gate, ret_norm_w, tables)
    u_seg = _to_segments(u, seg_len)
    xr, xi, c0r, c0i, yp, ssm_seg = _s5_fwd(u_seg, bsr_b, bsi_b, csr_b, csi_b, a8r, a8i, al8r, al8i,
                                            ssm_d, glu_full, ssm_glu_b, ssm_norm_w, jb)
    ssm = _from_segments(ssm_seg, seg_len)
    h2 = _outproj_fwd(h1, ret, ssm, gw['w_out'])
    (h3, g2, u2), _ = _ffn_fwd("ffn2_fwd", h2, ffn2_norm_w, gw['ffn2_w_gate'], gw['ffn2_w_up'], gw['ffn2_w_down'])
    loss_part, dh3, d_final = _loss_head(h3, final_norm_w.reshape(1, d), loss_target[0])

    (nb, daccb, ab, dgb, dub), _ = _ffn_bwd_hidden("ffn2_bwd_hidden", dh3, h2, ffn2_norm_w, g2, u2,
                                                   gw['ffn2_w_down'])
    (dh2, d_ffn2_norm), _ = _ffn_bwd_input("ffn2_bwd_input", dh3, h2, ffn2_norm_w, dgb, dub,
                                           gw['ffn2_w_gate'], gw['ffn2_w_up'])
    ffn2_list, _ = _ffn_bwd_w("ffn2_bwd_w", nb, daccb, ab, dgb, dub)
    ffn2_list = list(ffn2_list)
    dret, dssm, dwo = _outproj_bwd(dh2, ret, ssm, gw['w_out'])
    (du_seg, d_ssm_norm, d_glu_w, d_glu_b, d_ssm_d, dcr_s, dci_s, dbr_s, dbi_s, dar8, dai8) = _s5_bwd(
        _to_segments(dssm, seg_len), u_seg, yp, xr, xi, c0r, c0i, bsrt_b, bsit_b, csrt_b, csit_b,
        a8r, a8i, al8r, al8i, ssm_d, glu_full, ssm_glu_b, ssm_norm_w, jb)
    du = _from_segments(du_seg, seg_len)
    dq, dk, dv, dgate, d_ret_norm = _ret_bwd(dret, q, k, vv, gate, o, sprev, ret_norm_w, tables, cosf, sinf)
    dh1, d_mix_norm, dwin = _inproj_bwd(dh2, h1, mix_norm_w, n2, gw['w_in'], dq, dk, dv, dgate, du)
    d_c_re = jnp.transpose(_blockdiag_out(tr(dcr_s), groups, SSM_GROUP, SSM_STATE), (0, 1, 2))
    d_c_im = -_blockdiag_out(tr(dci_s), groups, SSM_GROUP, SSM_STATE)
    d_bbr = jnp.transpose(_blockdiag_out(dbr_s, groups, SSM_GROUP, SSM_STATE), (0, 2, 1))
    d_bbi = jnp.transpose(_blockdiag_out(dbi_s, groups, SSM_GROUP, SSM_STATE), (0, 2, 1))
    d_ar = jnp.sum(dar8, axis=0).reshape(groups, SSM_STATE)
    d_ai = jnp.sum(dai8, axis=0).reshape(groups, SSM_STATE)
    small_parts = [loss_part[0:1, :], d_mix_norm, d_ret_norm, d_ar, d_ai, d_bbr, d_bbi,
                   d_c_re, d_c_im, d_ssm_d, d_glu_b, d_ssm_norm, d_ffn2_norm, d_final]
    small_shapes = [a.shape for a in small_parts]
    packed = _pack(small_parts)
    mid_list = [dwin, d_glu_w.reshape(N_CHIP, sw // N_CHIP, sw).astype(BF16), dwo]
    (nb, daccb, ab, dgb, dub), mid_recv = _ffn_bwd_hidden(
        "ffn1_bwd_hidden", dh1, h0, ffn1_norm_w, g1, u1, gw['ffn1_w_down'], _alltoall_chips_plan(mid_list), mid_list)
    ffn1_list, rode = _ffn_bwd_w(
        "ffn1_bwd_w", nb, daccb, ab, dgb, dub,
        _Plans([_alltoall_chips_plan(ffn2_list), _allgather_all_plan([packed])]), ffn2_list + [packed])
    ffn2_recv, all_parts = rode[:3], rode[3]
    ffn1_list = list(ffn1_list)
    (dh0, d_ffn1_norm), ffn1_recv = _ffn_bwd_input(
        "ffn1_bwd_input", dh1, h0, ffn1_norm_w, dgb, dub, gw['ffn1_w_gate'], gw['ffn1_w_up'],
        _alltoall_chips_plan(ffn1_list), ffn1_list)
    grad_x = dh0[CHUNK:][None]
    tail_parts = [dh0[CHUNK - N_META:CHUNK], d_ffn1_norm]
    packed_tail = _pack(tail_parts)
    tail_all = _exchange("allgather_tail", _allgather_all_plan([packed_tail]), [packed_tail])[0]
    received = dict(zip(['w_in', 'ssm_glu_w', 'w_out'] + BIG_LATE[3:] + BIG_EARLY, mid_recv + ffn2_recv + ffn1_recv))
    chip_sums = [_sum_slots("sum_chips_" + n, received[n], BF16) for n in BIG]
    sib_sums = _swap_sibling("swap_sibling", chip_sums)
    (loss_row, g_mix_norm, g_ret_norm, g_ar, g_ai, g_bbr, g_bbi, g_c_re, g_c_im,
     g_ssm_d, g_glu_b, g_ssm_norm, g_ffn2_norm, g_final) = _unpack(_sum_slots("sum_small", all_parts, F32), small_shapes)
    g_meta_full, g_ffn1_norm = _unpack(_sum_slots("sum_tail", tail_all, F32), [a.shape for a in tail_parts])
    g_lam_re, g_lam_im, g_log_dt, g_b_re, g_b_im = prep_vjp((g_ar, g_ai, g_bbr, g_bbi))
    loss = loss_row[0, 0]
    g_meta = lax.dynamic_slice(g_meta_full, (0, chip * (d // N_CHIP)), (N_META, d // N_CHIP))
    small_grads = {
        'meta_tokens': g_meta, 'ffn1_norm_w': g_ffn1_norm, 'mix_norm_w': g_mix_norm, 'ret_norm_w': g_ret_norm,
        'ssm_lambda_re': g_lam_re[None], 'ssm_lambda_im': g_lam_im[None], 'ssm_log_dt': g_log_dt[None],
        'ssm_b_re': g_b_re[None], 'ssm_b_im': g_b_im[None], 'ssm_c_re': g_c_re[None], 'ssm_c_im': g_c_im[None],
        'ssm_d': g_ssm_d, 'ssm_glu_b': g_glu_b, 'ssm_norm_w': g_ssm_norm, 'ffn2_norm_w': g_ffn2_norm,
        'final_norm_w': g_final.reshape(d),
    }

    grads, deltas, new_m, new_v = {}, {}, {}, {}
    for n, mine, sib in zip(BIG, chip_sums, sib_sums):
        if n in TRANSPOSED:
            outs = _adam("adam_" + n, as_fd(w[n]), as_fd(m[n]), as_fd(v[n]), [mine, sib])
            grads[n], deltas[n], new_m[n], new_v[n] = [as_fd(t) for t in outs]
        else:
            grads[n], deltas[n], new_m[n], new_v[n] = _adam("adam_" + n, w[n], m[n], v[n], [mine, sib])
    sm_shapes = [w[n].shape for n in SMALL]
    sm_out = _adam("adam_small", _pack([w[n] for n in SMALL]), _pack([m[n] for n in SMALL]),
                   _pack([v[n] for n in SMALL]), [_pack([small_grads[n].reshape(w[n].shape) for n in SMALL])])
    for dst, packed in zip((grads, deltas, new_m, new_v), sm_out):
        for n, t in zip(SMALL, _unpack(packed, sm_shapes)):
            dst[n] = t

    return (loss, grad_x, *[grads[n] for n in WEIGHT_NAMES], *[deltas[n] for n in WEIGHT_NAMES],
            *[new_m[n] for n in WEIGHT_NAMES], *[new_v[n] for n in WEIGHT_NAMES])
```

```python
import functools
import math

import jax
import jax.numpy as jnp
from jax import lax
from jax.experimental import pallas as pl
from jax.experimental.pallas import tpu as pltpu

N_META = 16
RET_HEADS = 4
HEAD_DIM = 128
SSM_GROUP = 16
SSM_STATE = 64
CHUNK = 128
ROPE_BASE = 10000.0
EPS = 1e-6
FFN_RES = 0.5
N_SEG = 8
N_SEC = 4
N_CHIP = 4
LANE = 128
FFN_CPS = 2
BWD_W_ROWS = 1664

ADAM_LR = 0.001
ADAM_B1 = 0.9
ADAM_B2 = 0.999
ADAM_EPS = 1e-08
ADAM_WD = 0.01
ADAM_STEP = 10

VMEM_LIMIT = 56 * 1024 * 1024

F32 = jnp.float32
BF16 = jnp.bfloat16
MESH = pl.DeviceIdType.MESH


def _dot(a, b):
    return jnp.dot(a, b, preferred_element_type=F32)


def _dot_nt(a, b):
    return lax.dot_general(a, b, (((1,), (1,)), ((), ())), preferred_element_type=F32)


def _dot_tn(a, b):
    return lax.dot_general(a, b, (((0,), (0,)), ((), ())), preferred_element_type=F32)


def _tile(n, target, mult=64):
    best = None
    t = mult
    while t <= min(n, target):
        if n % t == 0:
            best = t
        t += mult
    assert best is not None, (n, target)
    return best


def _params(sem, vmem=VMEM_LIMIT):
    return pltpu.CompilerParams(dimension_semantics=sem, vmem_limit_bytes=vmem)


def _rms_stats(xf):
    r = lax.rsqrt(jnp.mean(xf * xf, axis=-1, keepdims=True) + EPS)
    return xf * r, r


def _rms_bwd(dy, xh, r, w):
    dxh = dy * w
    return r * (dxh - xh * jnp.mean(dxh * xh, axis=-1, keepdims=True))


def _sigmoid(x):
    return 0.5 * jnp.tanh(0.5 * x) + 0.5


GELU_K0 = math.sqrt(2.0 / math.pi)
GELU_K1 = 0.044715


CHIP_MASKS = [(1, 0, 0), (0, 1, 0), (1, 1, 0)]
ALL_MASKS = [(0, 0, 1), (0, 1, 0), (0, 1, 1), (1, 0, 0), (1, 0, 1), (1, 1, 0), (1, 1, 1)]
SIB_MASKS = [(0, 0, 1)]
ANY_SPEC = pl.BlockSpec(memory_space=pl.ANY)


class _Plan:
    def __init__(self, arrays, masks, n_slots, src_slotted, dst_slotted, local_copy, half=False, forward=False):
        self.shapes = [(a.shape, a.dtype) for a in arrays]
        self.n = len(arrays)
        self.masks = masks
        self.n_slots = n_slots
        self.src_slotted, self.dst_slotted, self.local_copy = src_slotted, dst_slotted, local_copy
        self.half, self.forward = half, forward
        self.n_cp = self.n * len(masks) * (len(CHIP_MASKS) if forward else 1)

    def out_shape(self):
        out = []
        for shp, dt in self.shapes:
            if self.dst_slotted and not self.src_slotted:
                shp = (self.n_slots,) + shp
            elif self.src_slotted and not self.dst_slotted:
                shp = shp[1:]
            out.append(jax.ShapeDtypeStruct(shp, dt))
        return tuple(out)

    def scratch(self):
        return [pltpu.SemaphoreType.DMA((self.n_cp,)), pltpu.SemaphoreType.DMA((self.n_cp,)),
                pltpu.SemaphoreType.DMA((self.n,))]

    def _slot(self, px, py, pc):
        if self.n_slots == 8:
            return 4 * px + 2 * py + pc
        if self.n_slots == 4:
            return 2 * px + py
        return pc

    def copies(self, ins, outs, sems):
        send_sems, recv_sems, loc_sems = sems
        x, y, c = lax.axis_index("x"), lax.axis_index("y"), lax.axis_index("c")
        me = self._slot(x, y, c)
        n_m = len(self.masks)
        cps = []
        for a in range(self.n):
            if self.forward:
                rows = self.shapes[a][0][-2] // 2
                mine = pl.ds(pl.multiple_of(c * rows, 8), rows)
                for j, (mx, my, _) in enumerate(CHIP_MASKS):
                    blk = outs[a].at[2 * (1 - x if mx else x) + (1 - y if my else y), mine]
                    k = a * len(CHIP_MASKS) + j
                    cps.append(pltpu.make_async_remote_copy(
                        src_ref=blk, dst_ref=blk, send_sem=send_sems.at[k], recv_sem=recv_sems.at[k],
                        device_id=(x, y, 1 - c), device_id_type=MESH))
                continue
            if self.local_copy:
                src = ins[a].at[me] if self.src_slotted else ins[a]
                cps.append(pltpu.make_async_copy(src, outs[a].at[me], loc_sems.at[a]))
            for mi, (mx, my, mc) in enumerate(self.masks):
                px = 1 - x if mx else x
                py = 1 - y if my else y
                pc = 1 - c if mc else c
                src = ins[a].at[self._slot(px, py, pc)] if self.src_slotted else ins[a]
                dst = outs[a].at[me] if self.dst_slotted else outs[a]
                if self.half:
                    rows = src.shape[-2] // 2
                    mine = pl.ds(pl.multiple_of(c * rows, 8), rows)
                    src, dst = src.at[mine], dst.at[mine]
                k = a * n_m + mi
                cps.append(pltpu.make_async_remote_copy(
                    src_ref=src, dst_ref=dst, send_sem=send_sems.at[k], recv_sem=recv_sems.at[k],
                    device_id=(px, py, pc), device_id_type=MESH))
        return cps


def _exchange(name, plan, arrays, shifted=None):
    n = plan.n
    n_x = 0 if shifted is None else 1

    def body(*refs):
        ins, outs, sems = refs[:n + n_x], refs[n + n_x:2 * (n + n_x)], refs[2 * (n + n_x):]
        cps = plan.copies(ins[:n], outs[:n], sems[:3])
        if shifted is not None:
            rows = shifted[0].shape[0]
            cps.append(pltpu.make_async_copy(ins[n], outs[n].at[pl.ds(shifted[1], rows)], sems[3]))
        for cp in cps:
            cp.start()
        for cp in cps:
            cp.wait()

    out_shape, scratch, args = plan.out_shape(), plan.scratch(), list(arrays)
    if shifted is not None:
        src, off = shifted
        out_shape += (jax.ShapeDtypeStruct((off + src.shape[0], src.shape[1]), src.dtype),)
        scratch = scratch + [pltpu.SemaphoreType.DMA]
        args.append(src)
    outs = pl.pallas_call(
        body, name=name, out_shape=out_shape,
        in_specs=[ANY_SPEC] * (n + n_x), out_specs=tuple([ANY_SPEC] * (n + n_x)), scratch_shapes=scratch,
        input_output_aliases={i: i for i in range(n)} if plan.forward else {},
    )(*args)
    return list(outs)


def _pcall(body, *, name, grid, in_specs, out_specs, out_shape, scratch_shapes, args, plan=None, plan_args=()):
    sem = ("arbitrary",) * len(grid)
    if plan is None:
        return pl.pallas_call(body, name=name, grid=grid, in_specs=in_specs, out_specs=out_specs,
                              out_shape=out_shape, scratch_shapes=scratch_shapes,
                              compiler_params=_params(sem))(*args), []
    n_in, n_out, n_scr, n_p = len(in_specs), len(out_specs), len(scratch_shapes), plan.n

    def wrapped(*refs):
        ins = refs[:n_in]
        p_ins = refs[n_in:n_in + n_p]
        o0 = n_in + n_p
        outs = refs[o0:o0 + n_out]
        p_outs = refs[o0 + n_out:o0 + n_out + n_p]
        s0 = o0 + n_out + n_p
        scr = refs[s0:s0 + n_scr]
        sems = refs[s0 + n_scr:]
        ids = [pl.program_id(i) for i in range(len(grid))]
        first = functools.reduce(jnp.logical_and, [i == 0 for i in ids])
        last = functools.reduce(jnp.logical_and, [i == g - 1 for i, g in zip(ids, grid)])

        @pl.when(first)
        def _():
            for cp in plan.copies(p_ins, p_outs, sems):
                cp.start()

        body(*ins, *outs, *scr)

        @pl.when(last)
        def _():
            for cp in plan.copies(p_ins, p_outs, sems):
                cp.wait()

    res = pl.pallas_call(
        wrapped, name=name, grid=grid,
        in_specs=list(in_specs) + [ANY_SPEC] * n_p,
        out_specs=tuple(out_specs) + (ANY_SPEC,) * n_p,
        out_shape=tuple(out_shape) + plan.out_shape(),
        scratch_shapes=list(scratch_shapes) + plan.scratch(),
        compiler_params=_params(sem),
    )(*args, *plan_args)
    return res[:n_out], list(res[n_out:])


def _allgather_chips_plan(arrays):
    return _Plan(arrays, CHIP_MASKS, 4, False, True, True, half=True)


def _forward_sibling(name, gathered):
    return _exchange(name, _Plan(gathered, SIB_MASKS, 4, True, True, False, forward=True), gathered)


def _alltoall_chips_plan(arrays):
    return _Plan(arrays, CHIP_MASKS, 4, True, True, True)


def _swap_sibling(name, arrays):
    return _exchange(name, _Plan(arrays, SIB_MASKS, 2, False, False, False), arrays)


def _allgather_all_plan(arrays):
    return _Plan(arrays, ALL_MASKS, 8, False, True, True)


def _sum_slots(name, arrs, out_dtype):
    s, r = arrs[0].shape[0], arrs[0].shape[-2]
    c = arrs[0].shape[-1] * (2 if arrs[0].ndim == 4 else 1)
    n = len(arrs)
    tr = _tile(r, 512 if n == 1 else 176, 8)

    def body(*refs):
        for a_ref, o_ref in zip(refs[:n], refs[n:]):
            if len(a_ref.shape) == 4:
                for half in range(2):
                    acc = a_ref[0, half].astype(F32)
                    for i in range(1, s):
                        acc = acc + a_ref[i, half].astype(F32)
                    o_ref[:, half * (c // 2):(half + 1) * (c // 2)] = acc.astype(out_dtype)
            else:
                acc = a_ref[0].astype(F32)
                for i in range(1, s):
                    acc = acc + a_ref[i].astype(F32)
                o_ref[...] = acc.astype(out_dtype)

    def in_spec(a):
        if a.ndim == 4:
            return pl.BlockSpec((s, 2, tr, c // 2), lambda i: (0, 0, i, 0))
        return pl.BlockSpec((s, tr, c), lambda i: (0, i, 0))

    return list(pl.pallas_call(
        body, name=name, grid=(r // tr,),
        in_specs=[in_spec(a) for a in arrs],
        out_specs=(pl.BlockSpec((tr, c), lambda i: (i, 0)),) * n,
        out_shape=(jax.ShapeDtypeStruct((r, c), out_dtype),) * n,
        compiler_params=_params(("arbitrary",)),
    )(*arrs))


def _adam_math(w, g, m, v):
    m_new = ADAM_B1 * m + (1.0 - ADAM_B1) * g
    v_new = ADAM_B2 * v + (1.0 - ADAM_B2) * (g * g)
    m_hat = m_new / (1.0 - ADAM_B1 ** ADAM_STEP)
    v_hat = v_new / (1.0 - ADAM_B2 ** ADAM_STEP)
    delta = -ADAM_LR * (m_hat / (jnp.sqrt(v_hat) + ADAM_EPS) + ADAM_WD * w)
    return delta, m_new, v_new


def _adam(name, wmv, g_parts):
    w0 = wmv[0][0]
    r, c = w0.shape[-2:]
    n_w = len(wmv)
    n_g = len(g_parts[0])
    tr = _tile(r, 256 if n_w == 1 else 88, 8)
    lead = w0.ndim == 3
    at = (lambda ref: ref.at[0]) if lead else (lambda ref: ref)
    n_in = 3 + n_g

    def body(*refs):
        for j in range(n_w):
            ins = refs[j * n_in:(j + 1) * n_in]
            outs = refs[n_w * n_in + 4 * j:n_w * n_in + 4 * j + 4]
            w_ref, m_ref, v_ref = [at(t) for t in ins[:3]]
            g_out, d_out, m_out, v_out = [at(t) for t in outs]
            g = ins[3][...].astype(F32)
            for gr in ins[4:]:
                g = g + gr[...].astype(F32)
            delta, m_new, v_new = _adam_math(w_ref[...], g, m_ref[...], v_ref[...])
            g_out[...] = g
            d_out[...] = delta
            m_out[...] = m_new
            v_out[...] = v_new

    spec = pl.BlockSpec((tr, c), lambda i: (i, 0))
    wspec = pl.BlockSpec((1, tr, c), lambda i: (0, i, 0)) if lead else spec
    shp = jax.ShapeDtypeStruct(w0.shape, F32)
    args = [t for (w, m, v), gp in zip(wmv, g_parts) for t in (w, m, v, *gp)]
    res = pl.pallas_call(
        body, name=name, grid=(r // tr,),
        in_specs=([wspec] * 3 + [spec] * n_g) * n_w, out_specs=(wspec,) * (4 * n_w), out_shape=(shp,) * (4 * n_w),
        compiler_params=_params(("arbitrary",)),
    )(*args)
    return [tuple(res[4 * j:4 * j + 4]) for j in range(n_w)]


def _ffn_fwd(name, h, nw, wg, wu, wd, plan=None, plan_args=()):
    lp, d = h.shape
    nck, f, _ = wg.shape
    tm = _tile(lp, 640)
    last = nck // FFN_CPS - 1

    def body(h_ref, nw_ref, wg_ref, wu_ref, wd_ref, ho_ref, g_ref, u_ref, n_sc, acc_sc):
        k = pl.program_id(1)

        @pl.when(k == 0)
        def _():
            xh, _ = _rms_stats(h_ref[...])
            n_sc[...] = (xh * nw_ref[...]).astype(BF16)
            acc_sc[...] = jnp.zeros_like(acc_sc)

        n = n_sc[...]
        acc = acc_sc[...]
        for c in range(FFN_CPS):
            g = _dot_nt(n, wg_ref[c])
            u = _dot_nt(n, wu_ref[c])
            g_ref[c] = g.astype(BF16)
            u_ref[c] = u.astype(BF16)
            a = (g * _sigmoid(g) * u).astype(BF16)
            acc = acc + _dot(a, wd_ref[c])
        acc_sc[...] = acc

        @pl.when(k == last)
        def _():
            ho_ref[...] = h_ref[...] + FFN_RES * acc_sc[...]

    return _pcall(
        body, name=name, grid=(lp // tm, nck // FFN_CPS), plan=plan, plan_args=plan_args, args=(h, nw, wg, wu, wd),
        in_specs=[pl.BlockSpec((tm, d), lambda i, k: (i, 0)),
                  pl.BlockSpec((1, d), lambda i, k: (0, 0)),
                  pl.BlockSpec((FFN_CPS, f, d), lambda i, k: (k, 0, 0)),
                  pl.BlockSpec((FFN_CPS, f, d), lambda i, k: (k, 0, 0)),
                  pl.BlockSpec((FFN_CPS, f, d), lambda i, k: (k, 0, 0))],
        out_specs=(pl.BlockSpec((tm, d), lambda i, k: (i, 0)),
                   pl.BlockSpec((FFN_CPS, tm, f), lambda i, k: (k, i, 0)),
                   pl.BlockSpec((FFN_CPS, tm, f), lambda i, k: (k, i, 0))),
        out_shape=(jax.ShapeDtypeStruct((lp, d), F32),
                   jax.ShapeDtypeStruct((nck, lp, f), BF16),
                   jax.ShapeDtypeStruct((nck, lp, f), BF16)),
        scratch_shapes=[pltpu.VMEM((tm, d), BF16), pltpu.VMEM((tm, d), F32)])


def _ffn_bwd_act(name, dh, h, nw, g, u, wg, wu, wd, plan=None, plan_args=()):
    lp, d = h.shape
    nck, f, _ = wg.shape
    tm = _tile(lp, 320)
    last = nck // FFN_CPS - 1

    def body(dh_ref, h_ref, nw_ref, g_ref, u_ref, wg_ref, wu_ref, wd_ref,
             dhi_ref, dnw_ref, n_ref, dacc_ref, a_ref, dg_ref, du_ref,
             xh_sc, r_sc, dn_sc):
        i = pl.program_id(0)
        k = pl.program_id(1)

        @pl.when(k == 0)
        def _():
            xh, r = _rms_stats(h_ref[...])
            xh_sc[...] = xh
            r_sc[...] = r
            n_ref[...] = (xh * nw_ref[...]).astype(BF16)
            dacc_ref[...] = (FFN_RES * dh_ref[...]).astype(BF16)
            dn_sc[...] = jnp.zeros_like(dn_sc)

        @pl.when(jnp.logical_and(i == 0, k == 0))
        def _():
            dnw_ref[...] = jnp.zeros_like(dnw_ref)

        dacc = dacc_ref[...]
        dn = dn_sc[...]
        for c in range(FFN_CPS):
            gv = g_ref[c].astype(F32)
            uv = u_ref[c].astype(F32)
            sg = _sigmoid(gv)
            sil = gv * sg
            da = _dot_nt(dacc, wd_ref[c])
            dgk = (da * uv * (sg * (1.0 + gv * (1.0 - sg)))).astype(BF16)
            duk = (da * sil).astype(BF16)
            a_ref[c] = (sil * uv).astype(BF16)
            dg_ref[c] = dgk
            du_ref[c] = duk
            dn = dn + _dot(dgk, wg_ref[c]) + _dot(duk, wu_ref[c])
        dn_sc[...] = dn

        @pl.when(k == last)
        def _():
            dnl = dn_sc[...]
            xh = xh_sc[...]
            dhi_ref[...] = dh_ref[...] + _rms_bwd(dnl, xh, r_sc[...], nw_ref[...])
            dnw_ref[...] += jnp.sum(dnl * xh, axis=0, keepdims=True)

    row = pl.BlockSpec((tm, d), lambda i, k: (i, 0))
    vec = pl.BlockSpec((1, d), lambda i, k: (0, 0))
    hid = pl.BlockSpec((FFN_CPS, tm, f), lambda i, k: (k, i, 0))
    w_fd = pl.BlockSpec((FFN_CPS, f, d), lambda i, k: (k, 0, 0))
    rshape = jax.ShapeDtypeStruct((lp, d), BF16)
    hshape = jax.ShapeDtypeStruct((nck, lp, f), BF16)
    return _pcall(
        body, name=name, grid=(lp // tm, nck // FFN_CPS), plan=plan, plan_args=plan_args,
        args=(dh, h, nw, g, u, wg, wu, wd),
        in_specs=[row, row, vec, hid, hid, w_fd, w_fd, w_fd],
        out_specs=(row, vec, row, row, hid, hid, hid),
        out_shape=(jax.ShapeDtypeStruct((lp, d), F32), jax.ShapeDtypeStruct((1, d), F32),
                   rshape, rshape, hshape, hshape, hshape),
        scratch_shapes=[pltpu.VMEM((tm, d), F32), pltpu.VMEM((tm, 1), F32), pltpu.VMEM((tm, d), F32)])


def _ffn_bwd_w(name, n, dacc, a, dg, du, plan=None, plan_args=()):
    lp, d = n.shape
    nck, _, f = a.shape
    tm = _tile(lp, BWD_W_ROWS)
    last = lp // tm - 1

    def body(n_ref, dacc_ref, a_ref, dg_ref, du_ref, dwg_ref, dwu_ref, dwd_ref, ag_sc, au_sc, ad_sc):
        i = pl.program_id(1)

        @pl.when(i == 0)
        def _():
            ag_sc[...] = jnp.zeros_like(ag_sc)
            au_sc[...] = jnp.zeros_like(au_sc)
            ad_sc[...] = jnp.zeros_like(ad_sc)

        nv = n_ref[...]
        ag_sc[...] += _dot_tn(dg_ref[0], nv)
        au_sc[...] += _dot_tn(du_ref[0], nv)
        ad_sc[...] += _dot_tn(a_ref[0], dacc_ref[...])

        @pl.when(i == last)
        def _():
            dwg_ref[0] = ag_sc[...].astype(BF16)
            dwu_ref[0] = au_sc[...].astype(BF16)
            dwd_ref[0] = ad_sc[...].astype(BF16)

    row = pl.BlockSpec((tm, d), lambda k, i: (i, 0))
    hid = pl.BlockSpec((1, tm, f), lambda k, i: (k, i, 0))
    w_fd = pl.BlockSpec((1, f, d), lambda k, i: (k, 0, 0))
    wshape = jax.ShapeDtypeStruct((nck, f, d), BF16)
    return _pcall(
        body, name=name, grid=(nck, lp // tm), plan=plan, plan_args=plan_args, args=(n, dacc, a, dg, du),
        in_specs=[row, row, hid, hid, hid], out_specs=(w_fd, w_fd, w_fd), out_shape=(wshape,) * 3,
        scratch_shapes=[pltpu.VMEM((f, d), F32)] * 3)


def _ffn_bwd_w_scatter(name, n, dacc, a, dg, du, chip, plan, plan_args):
    lp, d = n.shape
    nck, _, f = a.shape
    tm = _tile(lp, BWD_W_ROWS)
    last_i = lp // tm - 1
    n_w = 3
    n_p = plan.n
    n_q = 2 * nck
    dh = d // 2

    def body(me_ref, n_ref, dacc_ref, a_ref, dg_ref, du_ref, *rest):
        p_ins = rest[:n_p]
        recv = rest[n_p:n_p + n_w]
        p_outs = rest[n_p + n_w:2 * n_p + n_w]
        acc = rest[2 * n_p + n_w:2 * n_p + 2 * n_w]
        stage, send_sems, recv_sems, loc_sems = rest[2 * n_p + 2 * n_w:2 * n_p + 2 * n_w + 4]
        p_sems = rest[2 * n_p + 2 * n_w + 4:]
        q = pl.program_id(0)
        i = pl.program_id(1)
        me = me_ref[0]
        c = lax.axis_index("c")

        def send(w, pos):
            half = lax.rem(pos, 2)
            kk = jnp.bitwise_xor(me, nck - 1 - lax.div(pos, 2))
            diff = jnp.bitwise_xor(kk, me)
            m = jnp.where(diff == 2, 0, jnp.where(diff == 1, 1, 2))
            k = (w * 3 + m) * 2 + half
            return pltpu.make_async_remote_copy(
                src_ref=stage.at[half, w], dst_ref=recv[w].at[me, half],
                send_sem=send_sems.at[k], recv_sem=recv_sems.at[k],
                device_id=(lax.div(kk, 2), lax.rem(kk, 2), c), device_id_type=MESH)

        def own(w, half):
            return pltpu.make_async_copy(stage.at[half, w], recv[w].at[me, half], loc_sems.at[w * 2 + half])

        @pl.when(jnp.logical_and(q == 0, i == 0))
        def _():
            for cp in plan.copies(p_ins, p_outs, p_sems):
                cp.start()

        @pl.when(i == 0)
        def _():
            for t in acc:
                t[...] = jnp.zeros_like(t)

        nv = n_ref[...]
        acc[0][...] += _dot_tn(dg_ref[0], nv)
        acc[1][...] += _dot_tn(du_ref[0], nv)
        acc[2][...] += _dot_tn(a_ref[0], dacc_ref[...])

        @pl.when(jnp.logical_and(i == last_i, q >= 2))
        def _():
            for w in range(n_w):
                send(w, q - 2).wait_send()

        @pl.when(i == last_i)
        def _():
            for w in range(n_w):
                stage[lax.rem(q, 2), w] = acc[w][...].astype(BF16)

        @pl.when(jnp.logical_and(i == last_i, q < n_q - 2))
        def _():
            for w in range(n_w):
                send(w, q).start()

        for half in range(2):
            @pl.when(jnp.logical_and(i == last_i, q == n_q - 2 + half))
            def _():
                for w in range(n_w):
                    own(w, half).start()

        @pl.when(jnp.logical_and(i == last_i, q == n_q - 1))
        def _():
            for w in range(n_w):
                for half in range(2):
                    own(w, half).wait()
            for w in range(n_w):
                for k in range(6):
                    pltpu.make_async_remote_copy(
                        src_ref=stage.at[0, w], dst_ref=recv[w].at[me, 0],
                        send_sem=send_sems.at[w * 6 + k], recv_sem=recv_sems.at[w * 6 + k],
                        device_id=(0, 0, c), device_id_type=MESH).wait_recv()
            for cp in plan.copies(p_ins, p_outs, p_sems):
                cp.wait()

    chunk = lambda k, me_ref: jnp.bitwise_xor(me_ref[0], nck - 1 - k // 2)
    row = pl.BlockSpec((tm, dh), lambda k, i, me_ref: (i, k % 2))
    hid = pl.BlockSpec((1, tm, f), lambda k, i, me_ref: (chunk(k, me_ref), i, 0))
    wshape = jax.ShapeDtypeStruct((nck, 2, f, dh), BF16)
    res = pl.pallas_call(
        body, name=name,
        grid_spec=pltpu.PrefetchScalarGridSpec(
            num_scalar_prefetch=1, grid=(n_q, lp // tm),
            in_specs=[row, row, hid, hid, hid] + [ANY_SPEC] * n_p,
            out_specs=(ANY_SPEC,) * (n_w + n_p),
            scratch_shapes=[pltpu.VMEM((f, dh), F32)] * n_w + [
                pltpu.VMEM((2, n_w, f, dh), BF16), pltpu.SemaphoreType.DMA((n_w * 6,)),
                pltpu.SemaphoreType.DMA((n_w * 6,)), pltpu.SemaphoreType.DMA((n_w * 2,))] + plan.scratch()),
        out_shape=(wshape,) * n_w + plan.out_shape(),
        compiler_params=_params(("arbitrary", "arbitrary")),
    )(chip.reshape(1).astype(jnp.int32), n, dacc, a, dg, du, *plan_args)
    return list(res[:n_w]), list(res[n_w:])


def _inproj_fwd(h, nw, w_in, cosf, sinf, rw):
    lp, d = h.shape
    nck, _, ps = w_in.shape
    proj = nck * ps
    sw = proj - 4 * rw
    tm = _tile(lp, 640)
    scale = HEAD_DIM ** -0.5
    heads = rw // HEAD_DIM

    def body(h_ref, nw_ref, w_ref, cos_ref, sin_ref, n_ref, q_ref, k_ref, v_ref, g_ref, u_ref, p_sc):
        xh, _ = _rms_stats(h_ref[...])
        n = (xh * nw_ref[...]).astype(BF16)
        n_ref[...] = n
        for c in range(nck):
            p_sc[:, c * ps:(c + 1) * ps] = _dot(n, w_ref[c])
        cs = cos_ref[...]
        sn = sin_ref[...]
        for hh in range(heads):
            lo = hh * HEAD_DIM
            qh = p_sc[:, lo:lo + HEAD_DIM]
            q_ref[:, lo:lo + HEAD_DIM] = (qh * cs + pltpu.roll(qh, HEAD_DIM // 2, 1) * sn).astype(BF16)
            kh = p_sc[:, rw + lo:rw + lo + HEAD_DIM]
            k_ref[:, lo:lo + HEAD_DIM] = ((kh * cs + pltpu.roll(kh, HEAD_DIM // 2, 1) * sn) * scale).astype(BF16)
        v_ref[...] = p_sc[:, 2 * rw:3 * rw].astype(BF16)
        g_ref[...] = p_sc[:, 3 * rw:4 * rw]
        u_ref[...] = p_sc[:, 4 * rw:]

    row = lambda w: pl.BlockSpec((tm, w), lambda i: (i, 0))
    return pl.pallas_call(
        body, name="inproj_fwd", grid=(lp // tm,),
        in_specs=[row(d), pl.BlockSpec((1, d), lambda i: (0, 0)),
                  pl.BlockSpec((nck, d, ps), lambda i: (0, 0, 0)), row(HEAD_DIM), row(HEAD_DIM)],
        out_specs=(row(d), row(rw), row(rw), row(rw), row(rw), row(sw)),
        out_shape=(jax.ShapeDtypeStruct((lp, d), BF16),
                   jax.ShapeDtypeStruct((lp, rw), BF16),
                   jax.ShapeDtypeStruct((lp, rw), BF16),
                   jax.ShapeDtypeStruct((lp, rw), BF16),
                   jax.ShapeDtypeStruct((lp, rw), F32),
                   jax.ShapeDtypeStruct((lp, sw), F32)),
        scratch_shapes=[pltpu.VMEM((tm, proj), F32)],
        compiler_params=_params(("arbitrary",)),
    )(h, nw, w_in, cosf, sinf)


def _inproj_bwd(dh, h, nw, n, w_in, dq, dk, dv, dg, du):
    lp, d = h.shape
    nck, _, ps = w_in.shape
    rw = dq.shape[1]
    sw = du.shape[1]
    proj = nck * ps
    tm = _tile(lp, 640)
    last = lp // tm - 1

    def gather_dproj(p_sc, dq_ref, dk_ref, dv_ref, dg_ref, du_ref):
        p_sc[:, 0:rw] = dq_ref[...]
        p_sc[:, rw:2 * rw] = dk_ref[...]
        p_sc[:, 2 * rw:3 * rw] = dv_ref[...]
        p_sc[:, 3 * rw:4 * rw] = dg_ref[...]
        p_sc[:, 4 * rw:] = du_ref[...]

    def act_body(dh_ref, h_ref, nw_ref, w_ref, dq_ref, dk_ref, dv_ref, dg_ref, du_ref, dhi_ref, dnw_ref, p_sc):
        i = pl.program_id(0)

        @pl.when(i == 0)
        def _():
            dnw_ref[...] = jnp.zeros_like(dnw_ref)

        gather_dproj(p_sc, dq_ref, dk_ref, dv_ref, dg_ref, du_ref)
        dn = jnp.zeros((tm, d), F32)
        for c in range(nck):
            dn = dn + _dot_nt(p_sc[:, c * ps:(c + 1) * ps], w_ref[c])
        xh, r = _rms_stats(h_ref[...])
        dhi_ref[...] = dh_ref[...] + _rms_bwd(dn, xh, r, nw_ref[...])
        dnw_ref[...] += jnp.sum(dn * xh, axis=0, keepdims=True)

    def w_body(n_ref, dq_ref, dk_ref, dv_ref, dg_ref, du_ref, dw_ref, p_sc, acc_sc):
        i = pl.program_id(0)

        @pl.when(i == 0)
        def _():
            acc_sc[...] = jnp.zeros_like(acc_sc)

        gather_dproj(p_sc, dq_ref, dk_ref, dv_ref, dg_ref, du_ref)
        nv = n_ref[...]
        for c in range(nck):
            acc_sc[c] += _dot_tn(nv, p_sc[:, c * ps:(c + 1) * ps])

        @pl.when(i == last)
        def _():
            dw_ref[...] = acc_sc[...].astype(BF16)

    row = lambda w: pl.BlockSpec((tm, w), lambda i: (i, 0))
    vec = pl.BlockSpec((1, d), lambda i: (0, 0))
    wsp = pl.BlockSpec((nck, d, ps), lambda i: (0, 0, 0))
    dproj_specs = [row(rw), row(rw), row(rw), row(rw), row(sw)]
    dhi, dnw = pl.pallas_call(
        act_body, name="inproj_bwd_act", grid=(lp // tm,),
        in_specs=[row(d), row(d), vec, wsp] + dproj_specs,
        out_specs=(row(d), vec),
        out_shape=(jax.ShapeDtypeStruct((lp, d), F32), jax.ShapeDtypeStruct((1, d), F32)),
        scratch_shapes=[pltpu.VMEM((tm, proj), BF16)],
        compiler_params=_params(("arbitrary",)),
    )(dh, h, nw, w_in, dq, dk, dv, dg, du)
    dw = pl.pallas_call(
        w_body, name="inproj_bwd_w", grid=(lp // tm,),
        in_specs=[row(d)] + dproj_specs,
        out_specs=wsp, out_shape=jax.ShapeDtypeStruct((nck, d, ps), BF16),
        scratch_shapes=[pltpu.VMEM((tm, proj), BF16), pltpu.VMEM((nck, d, ps), F32)],
        compiler_params=_params(("arbitrary",)),
    )(n, dq, dk, dv, dg, du)
    return dhi, dnw, dw


def _retention_tables():
    h = jnp.arange(RET_HEADS, dtype=F32)
    log_g = jnp.log(1.0 - 2.0 ** (-5.0 - h))
    i = jnp.arange(CHUNK)
    diff = i[:, None] - i[None, :]
    dec = jnp.where(diff[None] >= 0,
                    jnp.exp(log_g[:, None, None] * jnp.maximum(diff, 0)[None].astype(F32)), 0.0)
    pos = jnp.arange(CHUNK, dtype=F32)
    wq = jnp.exp(log_g[:, None] * (pos + 1.0)[None])
    wk = jnp.exp(log_g[:, None] * (CHUNK - 1 - pos)[None])
    gch = jnp.exp(log_g * CHUNK)
    ones = jnp.ones((1, 1, HEAD_DIM), F32)
    return (dec, wq[:, :, None] * ones, wk[:, :, None] * ones,
            gch[:, None, None] * jnp.ones((1, 8, HEAD_DIM), F32))


def _head_norm(o):
    mu = jnp.mean(o, axis=-1, keepdims=True)
    oc = o - mu
    r = lax.rsqrt(jnp.mean(oc * oc, axis=-1, keepdims=True) + EPS)
    return oc * r, r


def _ret_fwd(q, k, v, g, rnw, tables):
    lp, rw = q.shape
    heads = rw // HEAD_DIM
    nch = lp // CHUNK
    dec, wq, wk, gch = tables

    def body(q_ref, k_ref, v_ref, g_ref, w_ref, dec_ref, wq_ref, wk_ref, gch_ref,
             o_ref, ret_ref, sp_ref, s_sc):
        n = pl.program_id(0)

        @pl.when(n == 0)
        def _():
            s_sc[...] = jnp.zeros_like(s_sc)

        cols = [slice(hh * HEAD_DIM, (hh + 1) * HEAD_DIM) for hh in range(heads)]
        s_ins = [s_sc[hh] for hh in range(heads)]
        outs = []
        for hh, cs in enumerate(cols):
            qv, kv, vv = q_ref[:, cs], k_ref[:, cs], v_ref[:, cs]
            s_in = s_ins[hh]
            a = _dot_nt(qv, kv) * dec_ref[hh]
            qw = (qv.astype(F32) * wq_ref[hh]).astype(BF16)
            kw = (kv.astype(F32) * wk_ref[hh]).astype(BF16)
            o = _dot(a.astype(BF16), vv) + _dot(qw, s_in.astype(BF16))
            s_new = gch_ref[hh, 0:1, :] * s_in + _dot_tn(kw, vv)
            xh, _ = _head_norm(o)
            gv = g_ref[:, cs]
            outs.append((o, s_new, (gv * _sigmoid(gv) * (xh * w_ref[:, cs])).astype(BF16)))
        for hh, cs in enumerate(cols):
            o, s_new, ret = outs[hh]
            sp_ref[hh, 0] = s_ins[hh]
            s_sc[hh] = s_new
            o_ref[:, cs] = o
            ret_ref[:, cs] = ret

    blk = pl.BlockSpec((CHUNK, rw), lambda n: (n, 0))
    tab = pl.BlockSpec((heads, CHUNK, HEAD_DIM), lambda n: (0, 0, 0))
    return pl.pallas_call(
        body, name="retention_fwd", grid=(nch,),
        in_specs=[blk, blk, blk, blk, pl.BlockSpec((1, rw), lambda n: (0, 0)),
                  tab, tab, tab, pl.BlockSpec((heads, 8, HEAD_DIM), lambda n: (0, 0, 0))],
        out_specs=(blk, blk, pl.BlockSpec((heads, 1, HEAD_DIM, HEAD_DIM), lambda n: (0, n, 0, 0))),
        out_shape=(jax.ShapeDtypeStruct((lp, rw), F32),
                   jax.ShapeDtypeStruct((lp, rw), BF16),
                   jax.ShapeDtypeStruct((heads, nch, HEAD_DIM, HEAD_DIM), F32)),
        scratch_shapes=[pltpu.VMEM((heads, HEAD_DIM, HEAD_DIM), F32)],
        compiler_params=_params(("arbitrary",)),
    )(q, k, v, g, rnw, dec, wq, wk, gch)


def _ret_bwd(dret, q, k, v, g, o, sprev, rnw, tables, cosf, sinf):
    lp, rw = q.shape
    heads = rw // HEAD_DIM
    nch = lp // CHUNK
    dec, wq, wk, gch = tables
    scale = HEAD_DIM ** -0.5
    half = HEAD_DIM // 2

    def body(dret_ref, q_ref, k_ref, v_ref, g_ref, o_ref, sp_ref, w_ref, dec_ref, wq_ref, wk_ref, gch_ref,
             cos_ref, sin_ref, dq_ref, dk_ref, dv_ref, dg_ref, dw_ref, ds_sc):
        n = pl.program_id(0)

        @pl.when(n == 0)
        def _():
            ds_sc[...] = jnp.zeros_like(ds_sc)
            dw_ref[...] = jnp.zeros_like(dw_ref)

        cosv = cos_ref[...]
        sinv = sin_ref[...]
        cols = [slice(hh * HEAD_DIM, (hh + 1) * HEAD_DIM) for hh in range(heads)]
        ds_ins = [ds_sc[hh] for hh in range(heads)]
        dw_ins = [dw_ref[:, cs] for cs in cols]
        outs = []
        for hh, cs in enumerate(cols):
            qv, kv, vv = q_ref[:, cs], k_ref[:, cs], v_ref[:, cs]
            gv = g_ref[:, cs]
            dr = dret_ref[:, cs]
            w = w_ref[:, cs]
            sg = _sigmoid(gv)
            sil = gv * sg
            xh, r = _head_norm(o_ref[:, cs])
            dgate = (dr * (xh * w) * (sg * (1.0 + gv * (1.0 - sg)))).astype(BF16)
            dyw = dr * sil
            dw_new = dw_ins[hh] + jnp.sum(dyw * xh, axis=0, keepdims=True)
            dxh = dyw * w
            do = r * (dxh - jnp.mean(dxh, axis=-1, keepdims=True)
                      - xh * jnp.mean(dxh * xh, axis=-1, keepdims=True))
            dob = do.astype(BF16)
            dmask = dec_ref[hh]
            wqv = wq_ref[hh]
            wkv = wk_ref[hh]
            a = (_dot_nt(qv, kv) * dmask).astype(BF16)
            da = (_dot_nt(dob, vv) * dmask).astype(BF16)
            qw = (qv.astype(F32) * wqv).astype(BF16)
            kw = (kv.astype(F32) * wkv).astype(BF16)
            s_in = sp_ref[hh, 0].astype(BF16)
            ds = ds_ins[hh]
            dsb = ds.astype(BF16)
            dq = _dot(da, kv) + _dot_nt(dob, s_in) * wqv
            dk = _dot_tn(da, qv) + _dot_nt(vv, dsb) * wkv
            dv = _dot_tn(a, dob) + _dot(kw, dsb)
            ds_new = gch_ref[hh, 0:1, :] * ds + _dot_tn(qw, dob)
            outs.append((dgate, dw_new, ds_new,
                         (dq * cosv + pltpu.roll(dq * sinv, half, 1)).astype(BF16),
                         ((dk * cosv + pltpu.roll(dk * sinv, half, 1)) * scale).astype(BF16),
                         dv.astype(BF16)))
        for hh, cs in enumerate(cols):
            dgate, dw_new, ds_new, dqv, dkv, dvv = outs[hh]
            dg_ref[:, cs] = dgate
            dw_ref[:, cs] = dw_new
            ds_sc[hh] = ds_new
            dq_ref[:, cs] = dqv
            dk_ref[:, cs] = dkv
            dv_ref[:, cs] = dvv

    blk = pl.BlockSpec((CHUNK, rw), lambda n: (nch - 1 - n, 0))
    tab = pl.BlockSpec((heads, CHUNK, HEAD_DIM), lambda n: (0, 0, 0))
    wsp = pl.BlockSpec((1, rw), lambda n: (0, 0))
    pos = pl.BlockSpec((CHUNK, HEAD_DIM), lambda n: (nch - 1 - n, 0))
    bshape = jax.ShapeDtypeStruct((lp, rw), BF16)
    return pl.pallas_call(
        body, name="retention_bwd", grid=(nch,),
        in_specs=[blk, blk, blk, blk, blk, blk,
                  pl.BlockSpec((heads, 1, HEAD_DIM, HEAD_DIM), lambda n: (0, nch - 1 - n, 0, 0)),
                  wsp, tab, tab, tab, pl.BlockSpec((heads, 8, HEAD_DIM), lambda n: (0, 0, 0)), pos, pos],
        out_specs=(blk, blk, blk, blk, wsp),
        out_shape=(bshape, bshape, bshape, bshape, jax.ShapeDtypeStruct((1, rw), F32)),
        scratch_shapes=[pltpu.VMEM((heads, HEAD_DIM, HEAD_DIM), F32)],
        compiler_params=_params(("arbitrary",)),
    )(dret, q, k, v, g, o, sprev, rnw, dec, wq, wk, gch, cosf, sinf)


SCAN_CW = 512


def _s5_prepare(lam_re, lam_im, log_dt, b_re, b_im):
    dt = jnp.exp(log_dt)[:, None]
    er = jnp.exp(lam_re * dt)
    ar = er * jnp.cos(lam_im * dt)
    ai = er * jnp.sin(lam_im * dt)
    den = lam_re * lam_re + lam_im * lam_im
    fr = ((ar - 1.0) * lam_re + ai * lam_im) / den
    fi = (ai * lam_re - (ar - 1.0) * lam_im) / den
    bbr = fr[..., None] * b_re - fi[..., None] * b_im
    bbi = fr[..., None] * b_im + fi[..., None] * b_re
    return ar, ai, bbr, bbi


def _blockdiag_in(t):
    g, p, n = t.shape
    gs = g // N_SEC
    t = t.reshape(N_SEC, gs, p, n)
    eye = jnp.eye(gs, dtype=t.dtype)
    return jnp.einsum("sgpn,gh->sgphn", t, eye).reshape(N_SEC, gs * p, gs * n)


def _blockdiag_out(m, g, p, n):
    gs = g // N_SEC
    m = m.reshape(N_SEC, gs, p, gs, n)
    eye = jnp.eye(gs, dtype=m.dtype)
    return jnp.einsum("sgphn,gh->sgpn", m, eye).reshape(g, p, n)


def _scan_step(xr_ref, xi_ref, r0, pr_of, ar_ref, ai_ref, conj, ncols):
    for cc in range(ncols // SCAN_CW):
        cs = pl.ds(cc * SCAN_CW, SCAN_CW)
        pr, pi = pr_of(cs)
        ar = ar_ref[:, cs]
        ai = ai_ref[:, cs]
        if conj:
            nr = ar * pr + ai * pi
            ni = ar * pi - ai * pr
        else:
            nr = ar * pr - ai * pi
            ni = ar * pi + ai * pr
        xr_ref[pl.ds(r0, 8), cs] = xr_ref[pl.ds(r0, 8), cs] + nr
        xi_ref[pl.ds(r0, 8), cs] = xi_ref[pl.ds(r0, 8), cs] + ni


def _shift_rows(z, down):
    row = lax.broadcasted_iota(jnp.int32, z.shape, 0)
    if down:
        return jnp.where(row == 0, 0.0, pltpu.roll(z, 1, 0))
    return jnp.where(row == N_SEG - 1, 0.0, pltpu.roll(z, N_SEG - 1, 0))


def _s5_fwd(u, bsr, bsi, csr, csi, a8r, a8i, al8r, al8i, d, gluw, glub, nw, jb):
    lp, sw = u.shape
    ns = a8r.shape[1]
    rows = N_SEG * jb
    nblk = lp // rows
    secw = sw // N_SEC
    secn = ns // N_SEC

    def local_scan(u_ref, bsr_ref, bsi_ref, ar_ref, ai_ref, xr_ref, xi_ref, pr_sc, pi_sc):
        for s in range(N_SEC):
            ub = u_ref[:, s * secw:(s + 1) * secw].astype(BF16)
            xr_ref[:, s * secn:(s + 1) * secn] = _dot(ub, bsr_ref[s])
            xi_ref[:, s * secn:(s + 1) * secn] = _dot(ub, bsi_ref[s])
        _scan_step(xr_ref, xi_ref, 0, lambda cs: (pr_sc[:, cs], pi_sc[:, cs]), ar_ref, ai_ref, False, ns)

        def step(j, carry):
            r0 = pl.multiple_of(j * 8, 8)
            rp = pl.multiple_of((j - 1) * 8, 8)
            _scan_step(xr_ref, xi_ref, r0,
                       lambda cs: (xr_ref[pl.ds(rp, 8), cs], xi_ref[pl.ds(rp, 8), cs]),
                       ar_ref, ai_ref, False, ns)
            return carry

        lax.fori_loop(1, jb, step, 0)
        pr_sc[...] = xr_ref[rows - 8:rows, :]
        pi_sc[...] = xi_ref[rows - 8:rows, :]

    def carry_body(u_ref, bsr_ref, bsi_ref, ar_ref, ai_ref, alr_ref, ali_ref, cr_ref, ci_ref,
                   xr_sc, xi_sc, pr_sc, pi_sc):
        b = pl.program_id(0)

        @pl.when(b == 0)
        def _():
            pr_sc[...] = jnp.zeros_like(pr_sc)
            pi_sc[...] = jnp.zeros_like(pi_sc)

        local_scan(u_ref, bsr_ref, bsi_ref, ar_ref, ai_ref, xr_sc, xi_sc, pr_sc, pi_sc)

        @pl.when(b == nblk - 1)
        def _():
            er = _shift_rows(pr_sc[...], True)
            ei = _shift_rows(pi_sc[...], True)
            alr, ali = alr_ref[...], ali_ref[...]
            cr, ci = er, ei
            for _ in range(N_SEG - 2):
                sr = _shift_rows(cr, True)
                si = _shift_rows(ci, True)
                cr = er + alr * sr - ali * si
                ci = ei + alr * si + ali * sr
            cr_ref[...] = cr
            ci_ref[...] = ci

    ublk = pl.BlockSpec((rows, sw), lambda b: (b, 0))
    bspec = pl.BlockSpec((N_SEC, secw, secn), lambda b: (0, 0, 0))
    cspec = pl.BlockSpec((N_SEC, secn, secw), lambda b: (0, 0, 0))
    s8 = pl.BlockSpec((N_SEG, ns), lambda b: (0, 0))
    vec = pl.BlockSpec((1, sw), lambda b: (0, 0))
    s8shape = jax.ShapeDtypeStruct((N_SEG, ns), F32)
    c0r, c0i = pl.pallas_call(
        carry_body, name="s5_fwd_carry", grid=(nblk,),
        in_specs=[ublk, bspec, bspec, s8, s8, s8, s8],
        out_specs=(s8, s8), out_shape=(s8shape, s8shape),
        scratch_shapes=[pltpu.VMEM((rows, ns), F32), pltpu.VMEM((rows, ns), F32),
                        pltpu.VMEM((N_SEG, ns), F32), pltpu.VMEM((N_SEG, ns), F32)],
        compiler_params=_params(("arbitrary",)),
    )(u, bsr, bsi, a8r, a8i, al8r, al8i)

    def main_body(u_ref, bsr_ref, bsi_ref, csr_ref, csi_ref, ar_ref, ai_ref, c0r_ref, c0i_ref,
                  d_ref, gw_ref, gb_ref, nw_ref, xr_ref, xi_ref, yp_ref, out_ref, pr_sc, pi_sc):
        b = pl.program_id(0)

        @pl.when(b == 0)
        def _():
            pr_sc[...] = c0r_ref[...]
            pi_sc[...] = c0i_ref[...]

        local_scan(u_ref, bsr_ref, bsi_ref, ar_ref, ai_ref, xr_ref, xi_ref, pr_sc, pi_sc)
        for s in range(N_SEC):
            xs = pl.ds(s * secn, secn)
            us = pl.ds(s * secw, secw)
            y = _dot(xr_ref[:, xs].astype(BF16), csr_ref[s]) + _dot(xi_ref[:, xs].astype(BF16), csi_ref[s])
            yp_ref[:, us] = y + d_ref[:, us] * u_ref[:, us]
        yp = yp_ref[...]
        t = jnp.tanh(GELU_K0 * (yp + GELU_K1 * yp * yp * yp))
        y1 = 0.5 * yp * (1.0 + t)
        z = _dot(y1.astype(BF16), gw_ref[...]) + gb_ref[...]
        y2 = y1 * _sigmoid(z)
        xh, _ = _rms_stats(y2)
        out_ref[...] = (xh * nw_ref[...]).astype(BF16)

    xblk = pl.BlockSpec((rows, ns), lambda b: (b, 0))
    xr, xi, yp, out = pl.pallas_call(
        main_body, name="s5_fwd", grid=(nblk,),
        in_specs=[ublk, bspec, bspec, cspec, cspec, s8, s8, s8, s8, vec,
                  pl.BlockSpec((sw, sw), lambda b: (0, 0)), vec, vec],
        out_specs=(xblk, xblk, ublk, ublk),
        out_shape=(jax.ShapeDtypeStruct((lp, ns), F32), jax.ShapeDtypeStruct((lp, ns), F32),
                   jax.ShapeDtypeStruct((lp, sw), F32), jax.ShapeDtypeStruct((lp, sw), BF16)),
        scratch_shapes=[pltpu.VMEM((N_SEG, ns), F32), pltpu.VMEM((N_SEG, ns), F32)],
        compiler_params=_params(("arbitrary",)),
    )(u, bsr, bsi, csr, csi, a8r, a8i, c0r, c0i, d, gluw, glub, nw)
    return xr, xi, c0r, c0i, yp, out


def _s5_bwd(dout, u, yp, xr, xi, c0r, c0i, bsrt, bsit, csrt, csit, a8r, a8i, al8r, al8i, d, gluw, glub, nw, jb):
    lp, sw = u.shape
    ns = a8r.shape[1]
    rows = N_SEG * jb
    nblk = lp // rows
    secw = sw // N_SEC
    secn = ns // N_SEC

    def rowwise_bwd(dout_ref, yp_ref, gw_ref, gb_ref, nw_ref):
        ypv = yp_ref[...]
        t = jnp.tanh(GELU_K0 * (ypv + GELU_K1 * ypv * ypv * ypv))
        y1 = 0.5 * ypv * (1.0 + t)
        dgelu = 0.5 * (1.0 + t) + 0.5 * ypv * (1.0 - t * t) * GELU_K0 * (1.0 + 3.0 * GELU_K1 * ypv * ypv)
        gw = gw_ref[...]
        y1b = y1.astype(BF16)
        sg = _sigmoid(_dot(y1b, gw) + gb_ref[...])
        xh, r = _rms_stats(y1 * sg)
        dov = dout_ref[...]
        dy2 = _rms_bwd(dov, xh, r, nw_ref[...])
        dz = dy2 * y1 * sg * (1.0 - sg)
        dzb = dz.astype(BF16)
        dy1 = dy2 * sg + _dot_nt(dzb, gw)
        return dy1 * dgelu, dov * xh, y1b, dzb, dz

    def lam_scan(dyp_of, csrt_ref, csit_ref, ar_ref, ai_ref, lr_sc, li_sc, nr_sc, ni_sc, extra):
        for s in range(N_SEC):
            db = dyp_of(s)
            lr_sc[:, s * secn:(s + 1) * secn] = _dot(db, csrt_ref[s])
            li_sc[:, s * secn:(s + 1) * secn] = _dot(db, csit_ref[s])
        top = rows - 8
        _scan_step(lr_sc, li_sc, top, lambda cs: (nr_sc[:, cs], ni_sc[:, cs]), ar_ref, ai_ref, True, ns)
        extra(top, pl.ds(top - 8, 8))

        def step(jj, carry):
            r0 = pl.multiple_of((jb - 1 - jj) * 8, 8)
            rn = pl.multiple_of((jb - jj) * 8, 8)
            rp = pl.multiple_of((jb - 2 - jj) * 8, 8)
            _scan_step(lr_sc, li_sc, r0,
                       lambda cs: (lr_sc[pl.ds(rn, 8), cs], li_sc[pl.ds(rn, 8), cs]),
                       ar_ref, ai_ref, True, ns)
            extra(r0, pl.ds(rp, 8))
            return carry

        lax.fori_loop(1, jb - 1, step, 0)
        _scan_step(lr_sc, li_sc, 0, lambda cs: (lr_sc[8:16, cs], li_sc[8:16, cs]), ar_ref, ai_ref, True, ns)
        extra(0, None)
        nr_sc[...] = lr_sc[0:8, :]
        ni_sc[...] = li_sc[0:8, :]

    def carry_body(dout_ref, yp_ref, u_ref, gw_ref, gb_ref, nw_ref, csrt_ref, csit_ref, ar_ref, ai_ref,
                   alr_ref, ali_ref, cr_ref, ci_ref, dyp_ref, dnw_ref, dgw_ref, dgb_ref, dd_ref,
                   lr_sc, li_sc, nr_sc, ni_sc):
        b = pl.program_id(0)

        @pl.when(b == 0)
        def _():
            nr_sc[...] = jnp.zeros_like(nr_sc)
            ni_sc[...] = jnp.zeros_like(ni_sc)
            for ref in (dnw_ref, dgw_ref, dgb_ref, dd_ref):
                ref[...] = jnp.zeros_like(ref)

        dyp, dnw_rows, y1b, dzb, dz = rowwise_bwd(dout_ref, yp_ref, gw_ref, gb_ref, nw_ref)
        dnw_ref[...] += jnp.sum(dnw_rows, axis=0, keepdims=True)
        dgw_ref[...] += _dot_tn(y1b, dzb)
        dgb_ref[...] += jnp.sum(dz, axis=0, keepdims=True)
        dd_ref[...] += jnp.sum(dyp * u_ref[...], axis=0, keepdims=True)
        dyp_ref[...] = dyp.astype(BF16)
        lam_scan(lambda s: dyp_ref[:, s * secw:(s + 1) * secw], csrt_ref, csit_ref, ar_ref, ai_ref,
                 lr_sc, li_sc, nr_sc, ni_sc, lambda r0, prev_rows: None)

        @pl.when(b == nblk - 1)
        def _():
            fr = _shift_rows(nr_sc[...], False)
            fi = _shift_rows(ni_sc[...], False)
            alr, ali = alr_ref[...], ali_ref[...]
            cr, ci = fr, fi
            for _ in range(N_SEG - 2):
                sr = _shift_rows(cr, False)
                si = _shift_rows(ci, False)
                cr = fr + alr * sr + ali * si
                ci = fi + alr * si - ali * sr
            cr_ref[...] = cr
            ci_ref[...] = ci

    rev = lambda b: (nblk - 1 - b, 0)
    ublk = pl.BlockSpec((rows, sw), rev)
    xblk = pl.BlockSpec((rows, ns), rev)
    s8 = pl.BlockSpec((N_SEG, ns), lambda b: (0, 0))
    vec = pl.BlockSpec((1, sw), lambda b: (0, 0))
    gws = pl.BlockSpec((sw, sw), lambda b: (0, 0))
    btspec = pl.BlockSpec((N_SEC, secn, secw), lambda b: (0, 0, 0))
    ctspec = pl.BlockSpec((N_SEC, secw, secn), lambda b: (0, 0, 0))
    s8shape = jax.ShapeDtypeStruct((N_SEG, ns), F32)
    lcr, lci, dyp_all, d_nw, d_gw, d_gb, d_d = pl.pallas_call(
        carry_body, name="s5_bwd_carry", grid=(nblk,),
        in_specs=[ublk, ublk, ublk, gws, vec, vec, ctspec, ctspec, s8, s8, s8, s8],
        out_specs=(s8, s8, ublk, vec, gws, vec, vec),
        out_shape=(s8shape, s8shape, jax.ShapeDtypeStruct((lp, sw), BF16), jax.ShapeDtypeStruct((1, sw), F32),
                   jax.ShapeDtypeStruct((sw, sw), F32), jax.ShapeDtypeStruct((1, sw), F32),
                   jax.ShapeDtypeStruct((1, sw), F32)),
        scratch_shapes=[pltpu.VMEM((rows, ns), F32), pltpu.VMEM((rows, ns), F32),
                        pltpu.VMEM((N_SEG, ns), F32), pltpu.VMEM((N_SEG, ns), F32)],
        compiler_params=_params(("arbitrary",)),
    )(dout, yp, u, gluw, glub, nw, csrt, csit, a8r, a8i, al8r, al8i)

    def main_body(dyp_sc, u_ref, xr_ref, xi_ref, xtr_ref, xti_ref, c0r_ref, c0i_ref, lcr_ref, lci_ref,
                  d_ref, bsrt_ref, bsit_ref, csrt_ref, csit_ref, ar_ref, ai_ref,
                  du_ref, dcr_ref, dci_ref, dbr_ref, dbi_ref, dar_ref, dai_ref,
                  lr_sc, li_sc, nr_sc, ni_sc):
        b = pl.program_id(0)

        @pl.when(b == 0)
        def _():
            nr_sc[...] = lcr_ref[...]
            ni_sc[...] = lci_ref[...]
            for ref in (dcr_ref, dci_ref, dbr_ref, dbi_ref, dar_ref, dai_ref):
                ref[...] = jnp.zeros_like(ref)

        for s in range(N_SEC):
            db = dyp_sc[:, s * secw:(s + 1) * secw]
            xs = pl.ds(s * secn, secn)
            dcr_ref[s] += _dot_tn(xr_ref[:, xs].astype(BF16), db)
            dci_ref[s] += _dot_tn(xi_ref[:, xs].astype(BF16), db)

        first = b == nblk - 1

        def acc_da(r0, prev_rows):
            for cc in range(ns // SCAN_CW):
                cs = pl.ds(cc * SCAN_CW, SCAN_CW)
                lr = lr_sc[pl.ds(r0, 8), cs]
                li = li_sc[pl.ds(r0, 8), cs]
                if prev_rows is None:
                    xpr = jnp.where(first, c0r_ref[:, cs], xtr_ref[:, cs])
                    xpi = jnp.where(first, c0i_ref[:, cs], xti_ref[:, cs])
                else:
                    xpr = xr_ref[prev_rows, cs]
                    xpi = xi_ref[prev_rows, cs]
                dar_ref[:, cs] += lr * xpr + li * xpi
                dai_ref[:, cs] += li * xpr - lr * xpi

        lam_scan(lambda s: dyp_sc[:, s * secw:(s + 1) * secw], csrt_ref, csit_ref, ar_ref, ai_ref,
                 lr_sc, li_sc, nr_sc, ni_sc, acc_da)

        for s in range(N_SEC):
            xs = pl.ds(s * secn, secn)
            us = pl.ds(s * secw, secw)
            lrb = lr_sc[:, xs].astype(BF16)
            lib = li_sc[:, xs].astype(BF16)
            du = _dot(lrb, bsrt_ref[s]) + _dot(lib, bsit_ref[s]) + d_ref[:, us] * dyp_sc[:, us].astype(F32)
            du_ref[:, us] = du.astype(BF16)
            ub = u_ref[:, us].astype(BF16)
            dbr_ref[s] += _dot_tn(ub, lrb)
            dbi_ref[s] += _dot_tn(ub, lib)

    tail = pl.BlockSpec((N_SEG, ns), lambda b: (jnp.maximum((nblk - 1 - b) * jb - 1, 0), 0))
    acc_c = pl.BlockSpec((N_SEC, secn, secw), lambda b: (0, 0, 0))
    acc_b = pl.BlockSpec((N_SEC, secw, secn), lambda b: (0, 0, 0))
    du, dcr, dci, dbr, dbi, dar, dai = pl.pallas_call(
        main_body, name="s5_bwd", grid=(nblk,),
        in_specs=[ublk, ublk, xblk, xblk, tail, tail, s8, s8, s8, s8,
                  vec, btspec, btspec, ctspec, ctspec, s8, s8],
        out_specs=(ublk, acc_c, acc_c, acc_b, acc_b, s8, s8),
        out_shape=(jax.ShapeDtypeStruct((lp, sw), BF16),
                   jax.ShapeDtypeStruct((N_SEC, secn, secw), F32),
                   jax.ShapeDtypeStruct((N_SEC, secn, secw), F32),
                   jax.ShapeDtypeStruct((N_SEC, secw, secn), F32),
                   jax.ShapeDtypeStruct((N_SEC, secw, secn), F32),
                   s8shape, s8shape),
        scratch_shapes=[pltpu.VMEM((rows, ns), F32), pltpu.VMEM((rows, ns), F32),
                        pltpu.VMEM((N_SEG, ns), F32), pltpu.VMEM((N_SEG, ns), F32)],
        compiler_params=_params(("arbitrary",)),
    )(dyp_all, u, xr, xi, xr, xi, c0r, c0i, lcr, lci, d, bsrt, bsit, csrt, csit, a8r, a8i)
    return du, d_nw, d_gw, d_gb, d_d, dcr, dci, dbr, dbi, dar, dai


def _outproj_fwd(h, ret, ssm, wo):
    lp, d = h.shape
    nck, rs, _ = wo.shape
    rw = ret.shape[1]
    tm = _tile(lp, 640)
    per = rw // rs

    def body(h_ref, ret_ref, ssm_ref, w_ref, o_ref):
        acc = h_ref[...]
        for c in range(nck):
            src = ret_ref if c < per else ssm_ref
            lo = (c % per) * rs
            acc = acc + _dot(src[:, lo:lo + rs], w_ref[c])
        o_ref[...] = acc

    row = lambda w: pl.BlockSpec((tm, w), lambda i: (i, 0))
    return pl.pallas_call(
        body, name="outproj_fwd", grid=(lp // tm,),
        in_specs=[row(d), row(rw), row(ssm.shape[1]), pl.BlockSpec((nck, rs, d), lambda i: (0, 0, 0))],
        out_specs=row(d), out_shape=jax.ShapeDtypeStruct((lp, d), F32),
        compiler_params=_params(("arbitrary",)),
    )(h, ret, ssm, wo)


def _outproj_bwd(dh, ret, ssm, wo):
    lp, d = dh.shape
    nck, rs, _ = wo.shape
    rw = ret.shape[1]
    sw = ssm.shape[1]
    tm = _tile(lp, 640)
    per = rw // rs
    last = lp // tm - 1

    def body(dh_ref, ret_ref, ssm_ref, w_ref, dret_ref, dssm_ref, dw_ref, acc_sc):
        i = pl.program_id(0)

        @pl.when(i == 0)
        def _():
            acc_sc[...] = jnp.zeros_like(acc_sc)

        dhb = dh_ref[...].astype(BF16)
        for c in range(nck):
            src, dst = (ret_ref, dret_ref) if c < per else (ssm_ref, dssm_ref)
            lo = (c % per) * rs
            dst[:, lo:lo + rs] = _dot_nt(dhb, w_ref[c])
            acc_sc[c] += _dot_tn(src[:, lo:lo + rs], dhb)

        @pl.when(i == last)
        def _():
            dw_ref[...] = acc_sc[...].astype(BF16)

    row = lambda w: pl.BlockSpec((tm, w), lambda i: (i, 0))
    wsp = pl.BlockSpec((nck, rs, d), lambda i: (0, 0, 0))
    return pl.pallas_call(
        body, name="outproj_bwd", grid=(lp // tm,),
        in_specs=[row(d), row(rw), row(sw), wsp],
        out_specs=(row(rw), row(sw), wsp),
        out_shape=(jax.ShapeDtypeStruct((lp, rw), F32), jax.ShapeDtypeStruct((lp, sw), F32),
                   jax.ShapeDtypeStruct((nck, rs, d), BF16)),
        scratch_shapes=[pltpu.VMEM((nck, rs, d), F32)],
        compiler_params=_params(("arbitrary",)),
    )(dh, ret, ssm, wo)


def _loss_head(h, fw, target):
    lp, d = h.shape
    tm = _tile(lp, 640, CHUNK)
    sub = tm // CHUNK

    def body(h_ref, w_ref, *rest):
        t_refs = rest[:sub]
        loss_ref, dh_ref, dw_ref = rest[sub:]
        i = pl.program_id(0)

        @pl.when(i == 0)
        def _():
            loss_ref[...] = jnp.zeros_like(loss_ref)
            dw_ref[...] = jnp.zeros_like(dw_ref)

        w = w_ref[...]
        for j in range(sub):
            rows = pl.ds(j * CHUNK, CHUNK)
            xh, r = _rms_stats(h_ref[rows, :])
            err = xh * w - t_refs[j][...]
            if j == 0:
                err = jnp.where(i == 0, 0.0, err)
            loss_ref[...] += 0.5 * jnp.sum(err * err) / d
            dout = err * (1.0 / d)
            dw_ref[...] += jnp.sum(dout * xh, axis=0, keepdims=True)
            dh_ref[rows, :] = _rms_bwd(dout, xh, r, w)

    t_spec = lambda j: pl.BlockSpec((CHUNK, d), lambda i: (jnp.maximum(i * sub + j - 1, 0), 0))
    return pl.pallas_call(
        body, name="loss_head", grid=(lp // tm,),
        in_specs=[pl.BlockSpec((tm, d), lambda i: (i, 0)), pl.BlockSpec((1, d), lambda i: (0, 0))]
        + [t_spec(j) for j in range(sub)],
        out_specs=(pl.BlockSpec((8, LANE), lambda i: (0, 0)), pl.BlockSpec((tm, d), lambda i: (i, 0)),
                   pl.BlockSpec((1, d), lambda i: (0, 0))),
        out_shape=(jax.ShapeDtypeStruct((8, LANE), F32), jax.ShapeDtypeStruct((lp, d), F32),
                   jax.ShapeDtypeStruct((1, d), F32)),
        compiler_params=_params(("arbitrary",)),
    )(h, fw, *([target] * sub))


def _pack(arrs):
    flat = jnp.concatenate([a.reshape(-1).astype(F32) for a in arrs])
    n = flat.shape[0]
    rows = -(-n // (8 * LANE)) * 8
    return jnp.pad(flat, (0, rows * LANE - n)).reshape(rows, LANE)


def _unpack(packed, shapes):
    flat = packed.reshape(-1)
    out, off = [], 0
    for s in shapes:
        n = math.prod(s)
        out.append(flat[off:off + n].reshape(s))
        off += n
    return out


def _to_segments(a, seg_len):
    return a.reshape(N_SEG, seg_len, a.shape[1]).transpose(1, 0, 2).reshape(a.shape)


def _from_segments(a, seg_len):
    return a.reshape(seg_len, N_SEG, a.shape[1]).transpose(1, 0, 2).reshape(a.shape)


WEIGHT_NAMES = ['meta_tokens', 'ffn1_norm_w', 'ffn1_w_gate', 'ffn1_w_up', 'ffn1_w_down', 'mix_norm_w', 'w_in',
                'ret_norm_w', 'ssm_lambda_re', 'ssm_lambda_im', 'ssm_log_dt', 'ssm_b_re', 'ssm_b_im', 'ssm_c_re',
                'ssm_c_im', 'ssm_d', 'ssm_glu_w', 'ssm_glu_b', 'ssm_norm_w', 'w_out', 'ffn2_norm_w', 'ffn2_w_gate',
                'ffn2_w_up', 'ffn2_w_down', 'final_norm_w']
BIG = ['ffn1_w_gate', 'ffn1_w_up', 'ffn1_w_down', 'w_in', 'ssm_glu_w', 'w_out', 'ffn2_w_gate', 'ffn2_w_up',
       'ffn2_w_down']
TRANSPOSED = ['ffn1_w_gate', 'ffn1_w_up', 'ffn2_w_gate', 'ffn2_w_up']
BIG_EARLY = ['ffn1_w_gate', 'ffn1_w_up', 'ffn1_w_down']
BIG_LATE = [n for n in BIG if n not in BIG_EARLY]
SMALL = [n for n in WEIGHT_NAMES if n not in BIG]


def kernel(x, meta_tokens, ffn1_norm_w, ffn1_w_gate, ffn1_w_up, ffn1_w_down, mix_norm_w, w_in, ret_norm_w, ssm_lambda_re, ssm_lambda_im, ssm_log_dt, ssm_b_re, ssm_b_im, ssm_c_re, ssm_c_im, ssm_d, ssm_glu_w, ssm_glu_b, ssm_norm_w, w_out, ffn2_norm_w, ffn2_w_gate, ffn2_w_up, ffn2_w_down, final_norm_w, loss_target, m_meta_tokens, m_ffn1_norm_w, m_ffn1_w_gate, m_ffn1_w_up, m_ffn1_w_down, m_mix_norm_w, m_w_in, m_ret_norm_w, m_ssm_lambda_re, m_ssm_lambda_im, m_ssm_log_dt, m_ssm_b_re, m_ssm_b_im, m_ssm_c_re, m_ssm_c_im, m_ssm_d, m_ssm_glu_w, m_ssm_glu_b, m_ssm_norm_w, m_w_out, m_ffn2_norm_w, m_ffn2_w_gate, m_ffn2_w_up, m_ffn2_w_down, m_final_norm_w, v_meta_tokens, v_ffn1_norm_w, v_ffn1_w_gate, v_ffn1_w_up, v_ffn1_w_down, v_mix_norm_w, v_w_in, v_ret_norm_w, v_ssm_lambda_re, v_ssm_lambda_im, v_ssm_log_dt, v_ssm_b_re, v_ssm_b_im, v_ssm_c_re, v_ssm_c_im, v_ssm_d, v_ssm_glu_w, v_ssm_glu_b, v_ssm_norm_w, v_w_out, v_ffn2_norm_w, v_ffn2_w_gate, v_ffn2_w_up, v_ffn2_w_down, v_final_norm_w):
    args = locals()
    w = {n: args[n] for n in WEIGHT_NAMES}
    m = {n: args["m_" + n] for n in WEIGHT_NAMES}
    v = {n: args["v_" + n] for n in WEIGHT_NAMES}

    seq, d = x.shape[1], x.shape[2]
    lp = seq + CHUNK
    seg_len = lp // N_SEG
    rw = RET_HEADS * HEAD_DIM
    sw = ssm_d.shape[-1]
    groups = sw // SSM_GROUP
    ns = groups * SSM_STATE
    jb = _tile(seg_len, 40, 8)
    chip = 2 * lax.axis_index("x") + lax.axis_index("y")

    as_fd = lambda t: jnp.swapaxes(t, -1, -2)
    shards = {n: (as_fd(w[n][0]) if n in TRANSPOSED else w[n][0]).astype(BF16) for n in BIG}
    early = [shards[n] for n in BIG_EARLY] + [meta_tokens]
    *half_gathered, h0_body = _exchange("gather_early", _allgather_chips_plan(early), early, shifted=(x[0], CHUNK))
    gathered = _forward_sibling("gather_early_forward", half_gathered)
    gw = dict(zip(BIG_EARLY, gathered[:-1]))
    meta_full = jnp.transpose(gathered[-1], (1, 0, 2)).reshape(N_META, d)
    late = [shards[n] for n in BIG_LATE]

    pos = jnp.arange(lp, dtype=F32) - float(CHUNK - N_META)
    freqs = 1.0 / (ROPE_BASE ** (jnp.arange(0, HEAD_DIM, 2, dtype=F32) / HEAD_DIM))
    ang = pos[:, None] * freqs[None, :]
    cosf = jnp.concatenate([jnp.cos(ang), jnp.cos(ang)], axis=1)
    sinf = jnp.concatenate([-jnp.sin(ang), jnp.sin(ang)], axis=1)
    tables = _retention_tables()

    lam_re, lam_im, log_dt = ssm_lambda_re[0], ssm_lambda_im[0], ssm_log_dt[0]
    b_re, b_im, c_re, c_im = ssm_b_re[0], ssm_b_im[0], ssm_c_re[0], ssm_c_im[0]
    (ar, ai, bbr, bbi), prep_vjp = jax.vjp(_s5_prepare, lam_re, lam_im, log_dt, b_re, b_im)
    dt = jnp.exp(log_dt)[:, None]
    el = jnp.exp(seg_len * lam_re * dt)
    alr = el * jnp.cos(seg_len * lam_im * dt)
    ali = el * jnp.sin(seg_len * lam_im * dt)
    bc8 = lambda t: jnp.broadcast_to(t.reshape(1, ns), (N_SEG, ns))
    a8r, a8i, al8r, al8i = bc8(ar), bc8(ai), bc8(alr), bc8(ali)
    bsr = _blockdiag_in(jnp.transpose(bbr, (0, 2, 1)))
    bsi = _blockdiag_in(jnp.transpose(bbi, (0, 2, 1)))
    csrt = _blockdiag_in(c_re)
    csit = _blockdiag_in(-c_im)
    tr = lambda t: jnp.transpose(t, (0, 2, 1))
    bsr_b, bsi_b = bsr.astype(BF16), bsi.astype(BF16)
    csr_b, csi_b = tr(csrt).astype(BF16), tr(csit).astype(BF16)
    bsrt_b, bsit_b = tr(bsr).astype(BF16), tr(bsi).astype(BF16)
    csrt_b, csit_b = csrt.astype(BF16), csit.astype(BF16)

    head = jnp.concatenate([jnp.zeros((CHUNK - N_META, d), F32), meta_full], axis=0)
    h0 = lax.dynamic_update_slice(h0_body, head, (0, 0))
    (h1, g1, u1), late_half = _ffn_fwd("ffn1_fwd", h0, ffn1_norm_w, gw['ffn1_w_gate'], gw['ffn1_w_up'],
                                       gw['ffn1_w_down'], _allgather_chips_plan(late), late)
    gw.update(zip(BIG_LATE, _forward_sibling("gather_late_forward", late_half)))
    glu_full = gw['ssm_glu_w'].reshape(sw, sw)
    n2, q, k, vv, gate, u = _inproj_fwd(h1, mix_norm_w, gw['w_in'], cosf, sinf, rw)
    o, ret, sprev = _ret_fwd(q, k, vv, gate, ret_norm_w, tables)
    u_seg = _to_segments(u, seg_len)
    xr, xi, c0r, c0i, yp, ssm_seg = _s5_fwd(u_seg, bsr_b, bsi_b, csr_b, csi_b, a8r, a8i, al8r, al8i,
                                            ssm_d, glu_full, ssm_glu_b, ssm_norm_w, jb)
    ssm = _from_segments(ssm_seg, seg_len)
    h2 = _outproj_fwd(h1, ret, ssm, gw['w_out'])
    (h3, g2, u2), _ = _ffn_fwd("ffn2_fwd", h2, ffn2_norm_w, gw['ffn2_w_gate'], gw['ffn2_w_up'], gw['ffn2_w_down'])
    loss_part, dh3, d_final = _loss_head(h3, final_norm_w.reshape(1, d), loss_target[0])

    (dh2, d_ffn2_norm, nb, daccb, ab, dgb, dub), _ = _ffn_bwd_act(
        "ffn2_bwd_act", dh3, h2, ffn2_norm_w, g2, u2, gw['ffn2_w_gate'], gw['ffn2_w_up'], gw['ffn2_w_down'])
    (dwg2, dwu2, dwd2), _ = _ffn_bwd_w("ffn2_bwd_w", nb, daccb, ab, dgb, dub)
    dret, dssm, dwo = _outproj_bwd(dh2, ret, ssm, gw['w_out'])
    (du_seg, d_ssm_norm, d_glu_w, d_glu_b, d_ssm_d, dcr_s, dci_s, dbr_s, dbi_s, dar8, dai8) = _s5_bwd(
        _to_segments(dssm, seg_len), u_seg, yp, xr, xi, c0r, c0i, bsrt_b, bsit_b, csrt_b, csit_b,
        a8r, a8i, al8r, al8i, ssm_d, glu_full, ssm_glu_b, ssm_norm_w, jb)
    du = _from_segments(du_seg, seg_len)
    dq, dk, dv, dgate, d_ret_norm = _ret_bwd(dret, q, k, vv, gate, o, sprev, ret_norm_w, tables, cosf, sinf)
    dh1, d_mix_norm, dwin = _inproj_bwd(dh2, h1, mix_norm_w, n2, gw['w_in'], dq, dk, dv, dgate, du)
    late_parts = {
        'w_in': dwin, 'ssm_glu_w': d_glu_w.reshape(N_CHIP, sw // N_CHIP, sw).astype(BF16), 'w_out': dwo,
        'ffn2_w_gate': dwg2, 'ffn2_w_up': dwu2, 'ffn2_w_down': dwd2,
    }
    late_list = [late_parts[n] for n in BIG_LATE]
    (dh0, d_ffn1_norm, nb, daccb, ab, dgb, dub), late_recv = _ffn_bwd_act(
        "ffn1_bwd_act", dh1, h0, ffn1_norm_w, g1, u1, gw['ffn1_w_gate'], gw['ffn1_w_up'], gw['ffn1_w_down'],
        _alltoall_chips_plan(late_list), late_list)
    grad_x = dh0[CHUNK:][None]
    d_meta = dh0[CHUNK - N_META:CHUNK]

    d_c_re = jnp.transpose(_blockdiag_out(tr(dcr_s), groups, SSM_GROUP, SSM_STATE), (0, 1, 2))
    d_c_im = -_blockdiag_out(tr(dci_s), groups, SSM_GROUP, SSM_STATE)
    d_bbr = jnp.transpose(_blockdiag_out(dbr_s, groups, SSM_GROUP, SSM_STATE), (0, 2, 1))
    d_bbi = jnp.transpose(_blockdiag_out(dbi_s, groups, SSM_GROUP, SSM_STATE), (0, 2, 1))
    d_ar = jnp.sum(dar8, axis=0).reshape(groups, SSM_STATE)
    d_ai = jnp.sum(dai8, axis=0).reshape(groups, SSM_STATE)
    small_parts = [loss_part[0:1, :], d_meta, d_ffn1_norm, d_mix_norm, d_ret_norm, d_ar, d_ai, d_bbr, d_bbi,
                   d_c_re, d_c_im, d_ssm_d, d_glu_b, d_ssm_norm, d_ffn2_norm, d_final]
    small_shapes = [a.shape for a in small_parts]
    packed = _pack(small_parts)
    early_recv, (all_parts,) = _ffn_bwd_w_scatter("ffn1_bwd_w", nb, daccb, ab, dgb, dub, chip,
                                                  _allgather_all_plan([packed]), [packed])
    received = dict(zip(BIG_LATE + BIG_EARLY, late_recv + early_recv))
    ffn_names = [n for n in BIG if n.startswith('ffn')]
    chip_sum = dict(zip(ffn_names, _sum_slots("sum_chips_ffn", [received[n] for n in ffn_names], BF16)))
    for n in BIG:
        if n not in chip_sum:
            chip_sum[n] = _sum_slots("sum_chips_" + n, [received[n]], BF16)[0]
    chip_sums = [chip_sum[n] for n in BIG]
    sib_sums = _swap_sibling("swap_sibling", chip_sums)
    (loss_row, g_meta_full, g_ffn1_norm, g_mix_norm, g_ret_norm, g_ar, g_ai, g_bbr, g_bbi, g_c_re, g_c_im,
     g_ssm_d, g_glu_b, g_ssm_norm, g_ffn2_norm, g_final) = _unpack(_sum_slots("sum_small", [all_parts], F32)[0],
                                                                  small_shapes)
    g_lam_re, g_lam_im, g_log_dt, g_b_re, g_b_im = prep_vjp((g_ar, g_ai, g_bbr, g_bbi))
    loss = loss_row[0, 0]
    g_meta = lax.dynamic_slice(g_meta_full, (0, chip * (d // N_CHIP)), (N_META, d // N_CHIP))
    small_grads = {
        'meta_tokens': g_meta, 'ffn1_norm_w': g_ffn1_norm, 'mix_norm_w': g_mix_norm, 'ret_norm_w': g_ret_norm,
        'ssm_lambda_re': g_lam_re[None], 'ssm_lambda_im': g_lam_im[None], 'ssm_log_dt': g_log_dt[None],
        'ssm_b_re': g_b_re[None], 'ssm_b_im': g_b_im[None], 'ssm_c_re': g_c_re[None], 'ssm_c_im': g_c_im[None],
        'ssm_d': g_ssm_d, 'ssm_glu_b': g_glu_b, 'ssm_norm_w': g_ssm_norm, 'ffn2_norm_w': g_ffn2_norm,
        'final_norm_w': g_final.reshape(d),
    }

    grads, deltas, new_m, new_v = {}, {}, {}, {}
    g_pair = {n: [mine, sib] for n, mine, sib in zip(BIG, chip_sums, sib_sums)}
    view = lambda n, t: as_fd(t) if n in TRANSPOSED else t
    ffn_out = _adam("adam_ffn", [(view(n, w[n]), view(n, m[n]), view(n, v[n])) for n in ffn_names],
                    [g_pair[n] for n in ffn_names])
    for n, outs in zip(ffn_names, ffn_out):
        grads[n], deltas[n], new_m[n], new_v[n] = [view(n, t) for t in outs]
    for n in BIG:
        if n not in ffn_names:
            grads[n], deltas[n], new_m[n], new_v[n] = _adam("adam_" + n, [(w[n], m[n], v[n])], [g_pair[n]])[0]
    sm_shapes = [w[n].shape for n in SMALL]
    sm_out = _adam("adam_small", [(_pack([w[n] for n in SMALL]), _pack([m[n] for n in SMALL]),
                                  _pack([v[n] for n in SMALL]))],
                   [[_pack([small_grads[n].reshape(w[n].shape) for n in SMALL])]])[0]
    for dst, packed in zip((grads, deltas, new_m, new_v), sm_out):
        for n, t in zip(SMALL, _unpack(packed, sm_shapes)):
            dst[n] = t

    return (loss, grad_x, *[grads[n] for n in WEIGHT_NAMES], *[deltas[n] for n in WEIGHT_NAMES],
            *[new_m[n] for n in WEIGHT_NAMES], *[new_v[n] for n in WEIGHT_NAMES])
```

```python
import functools
import math

import jax
import jax.numpy as jnp
from jax import lax
from jax.experimental import pallas as pl
from jax.experimental.pallas import tpu as pltpu

N_META = 16
RET_HEADS = 4
HEAD_DIM = 128
SSM_GROUP = 16
SSM_STATE = 64
CHUNK = 128
ROPE_BASE = 10000.0
EPS = 1e-6
FFN_RES = 0.5
N_SEG = 8
N_SEC = 4
N_CHIP = 4
LANE = 128
FFN_CPS = 2
BWD_W_ROWS = 1664

ADAM_LR = 0.001
ADAM_B1 = 0.9
ADAM_B2 = 0.999
ADAM_EPS = 1e-08
ADAM_WD = 0.01
ADAM_STEP = 10

VMEM_LIMIT = 56 * 1024 * 1024

F32 = jnp.float32
BF16 = jnp.bfloat16
MESH = pl.DeviceIdType.MESH


def _dot(a, b):
    return jnp.dot(a, b, preferred_element_type=F32)


def _dot_nt(a, b):
    return lax.dot_general(a, b, (((1,), (1,)), ((), ())), preferred_element_type=F32)


def _dot_tn(a, b):
    return lax.dot_general(a, b, (((0,), (0,)), ((), ())), preferred_element_type=F32)


def _tile(n, target, mult=64):
    best = None
    t = mult
    while t <= min(n, target):
        if n % t == 0:
            best = t
        t += mult
    assert best is not None, (n, target)
    return best


def _params(sem, vmem=VMEM_LIMIT):
    return pltpu.CompilerParams(dimension_semantics=sem, vmem_limit_bytes=vmem)


def _rms_stats(xf):
    r = lax.rsqrt(jnp.mean(xf * xf, axis=-1, keepdims=True) + EPS)
    return xf * r, r


def _rms_bwd(dy, xh, r, w):
    dxh = dy * w
    return r * (dxh - xh * jnp.mean(dxh * xh, axis=-1, keepdims=True))


def _sigmoid(x):
    return 0.5 * jnp.tanh(0.5 * x) + 0.5


GELU_K0 = math.sqrt(2.0 / math.pi)
GELU_K1 = 0.044715


CHIP_MASKS = [(1, 0, 0), (0, 1, 0), (1, 1, 0)]
ALL_MASKS = [(0, 0, 1), (0, 1, 0), (0, 1, 1), (1, 0, 0), (1, 0, 1), (1, 1, 0), (1, 1, 1)]
SIB_MASKS = [(0, 0, 1)]
ANY_SPEC = pl.BlockSpec(memory_space=pl.ANY)


class _Plan:
    def __init__(self, arrays, masks, n_slots, src_slotted, dst_slotted, local_copy, half=False, forward=False):
        self.shapes = [(a.shape, a.dtype) for a in arrays]
        self.n = len(arrays)
        self.masks = masks
        self.n_slots = n_slots
        self.src_slotted, self.dst_slotted, self.local_copy = src_slotted, dst_slotted, local_copy
        self.half, self.forward = half, forward
        self.n_cp = self.n * len(masks) * (len(CHIP_MASKS) if forward else 1)

    def out_shape(self):
        out = []
        for shp, dt in self.shapes:
            if self.dst_slotted and not self.src_slotted:
                shp = (self.n_slots,) + shp
            elif self.src_slotted and not self.dst_slotted:
                shp = shp[1:]
            out.append(jax.ShapeDtypeStruct(shp, dt))
        return tuple(out)

    def scratch(self):
        return [pltpu.SemaphoreType.DMA((self.n_cp,)), pltpu.SemaphoreType.DMA((self.n_cp,)),
                pltpu.SemaphoreType.DMA((self.n,))]

    def _slot(self, px, py, pc):
        if self.n_slots == 8:
            return 4 * px + 2 * py + pc
        if self.n_slots == 4:
            return 2 * px + py
        return pc

    def copies(self, ins, outs, sems):
        send_sems, recv_sems, loc_sems = sems
        x, y, c = lax.axis_index("x"), lax.axis_index("y"), lax.axis_index("c")
        me = self._slot(x, y, c)
        n_m = len(self.masks)
        cps = []
        for a in range(self.n):
            if self.forward:
                rows = self.shapes[a][0][-2] // 2
                mine = pl.ds(pl.multiple_of(c * rows, 8), rows)
                for j, (mx, my, _) in enumerate(CHIP_MASKS):
                    blk = outs[a].at[2 * (1 - x if mx else x) + (1 - y if my else y), mine]
                    k = a * len(CHIP_MASKS) + j
                    cps.append(pltpu.make_async_remote_copy(
                        src_ref=blk, dst_ref=blk, send_sem=send_sems.at[k], recv_sem=recv_sems.at[k],
                        device_id=(x, y, 1 - c), device_id_type=MESH))
                continue
            if self.local_copy:
                src = ins[a].at[me] if self.src_slotted else ins[a]
                cps.append(pltpu.make_async_copy(src, outs[a].at[me], loc_sems.at[a]))
            for mi, (mx, my, mc) in enumerate(self.masks):
                px = 1 - x if mx else x
                py = 1 - y if my else y
                pc = 1 - c if mc else c
                src = ins[a].at[self._slot(px, py, pc)] if self.src_slotted else ins[a]
                dst = outs[a].at[me] if self.dst_slotted else outs[a]
                if self.half:
                    rows = src.shape[-2] // 2
                    mine = pl.ds(pl.multiple_of(c * rows, 8), rows)
                    src, dst = src.at[mine], dst.at[mine]
                k = a * n_m + mi
                cps.append(pltpu.make_async_remote_copy(
                    src_ref=src, dst_ref=dst, send_sem=send_sems.at[k], recv_sem=recv_sems.at[k],
                    device_id=(px, py, pc), device_id_type=MESH))
        return cps


def _exchange(name, plan, arrays):
    n = plan.n

    def body(*refs):
        cps = plan.copies(refs[:n], refs[n:2 * n], refs[2 * n:])
        for cp in cps:
            cp.start()
        for cp in cps:
            cp.wait()

    outs = pl.pallas_call(
        body, name=name, out_shape=plan.out_shape(),
        in_specs=[ANY_SPEC] * n, out_specs=tuple([ANY_SPEC] * n), scratch_shapes=plan.scratch(),
        input_output_aliases={i: i for i in range(n)} if plan.forward else {},
    )(*arrays)
    return list(outs)


def _pcall(body, *, name, grid, in_specs, out_specs, out_shape, scratch_shapes, args, plan=None, plan_args=()):
    sem = ("arbitrary",) * len(grid)
    if plan is None:
        return pl.pallas_call(body, name=name, grid=grid, in_specs=in_specs, out_specs=out_specs,
                              out_shape=out_shape, scratch_shapes=scratch_shapes,
                              compiler_params=_params(sem))(*args), []
    n_in, n_out, n_scr, n_p = len(in_specs), len(out_specs), len(scratch_shapes), plan.n

    def wrapped(*refs):
        ins = refs[:n_in]
        p_ins = refs[n_in:n_in + n_p]
        o0 = n_in + n_p
        outs = refs[o0:o0 + n_out]
        p_outs = refs[o0 + n_out:o0 + n_out + n_p]
        s0 = o0 + n_out + n_p
        scr = refs[s0:s0 + n_scr]
        sems = refs[s0 + n_scr:]
        ids = [pl.program_id(i) for i in range(len(grid))]
        first = functools.reduce(jnp.logical_and, [i == 0 for i in ids])
        last = functools.reduce(jnp.logical_and, [i == g - 1 for i, g in zip(ids, grid)])

        @pl.when(first)
        def _():
            for cp in plan.copies(p_ins, p_outs, sems):
                cp.start()

        body(*ins, *outs, *scr)

        @pl.when(last)
        def _():
            for cp in plan.copies(p_ins, p_outs, sems):
                cp.wait()

    res = pl.pallas_call(
        wrapped, name=name, grid=grid,
        in_specs=list(in_specs) + [ANY_SPEC] * n_p,
        out_specs=tuple(out_specs) + (ANY_SPEC,) * n_p,
        out_shape=tuple(out_shape) + plan.out_shape(),
        scratch_shapes=list(scratch_shapes) + plan.scratch(),
        compiler_params=_params(sem),
    )(*args, *plan_args)
    return res[:n_out], list(res[n_out:])


def _allgather_chips_plan(arrays):
    return _Plan(arrays, CHIP_MASKS, 4, False, True, True, half=True)


def _forward_sibling(name, gathered):
    return _exchange(name, _Plan(gathered, SIB_MASKS, 4, True, True, False, forward=True), gathered)


def _alltoall_chips_plan(arrays):
    return _Plan(arrays, CHIP_MASKS, 4, True, True, True)


def _swap_sibling(name, arrays):
    return _exchange(name, _Plan(arrays, SIB_MASKS, 2, False, False, False), arrays)


def _allgather_all_plan(arrays):
    return _Plan(arrays, ALL_MASKS, 8, False, True, True)


def _sum_slots(name, arrs, out_dtype):
    s, r = arrs[0].shape[0], arrs[0].shape[-2]
    c = arrs[0].shape[-1] * (2 if arrs[0].ndim == 4 else 1)
    n = len(arrs)
    tr = _tile(r, 512 if n == 1 else 176, 8)

    def body(*refs):
        for a_ref, o_ref in zip(refs[:n], refs[n:]):
            if len(a_ref.shape) == 4:
                for half in range(2):
                    acc = a_ref[0, half].astype(F32)
                    for i in range(1, s):
                        acc = acc + a_ref[i, half].astype(F32)
                    o_ref[:, half * (c // 2):(half + 1) * (c // 2)] = acc.astype(out_dtype)
            else:
                acc = a_ref[0].astype(F32)
                for i in range(1, s):
                    acc = acc + a_ref[i].astype(F32)
                o_ref[...] = acc.astype(out_dtype)

    def in_spec(a):
        if a.ndim == 4:
            return pl.BlockSpec((s, 2, tr, c // 2), lambda i: (0, 0, i, 0))
        return pl.BlockSpec((s, tr, c), lambda i: (0, i, 0))

    return list(pl.pallas_call(
        body, name=name, grid=(r // tr,),
        in_specs=[in_spec(a) for a in arrs],
        out_specs=(pl.BlockSpec((tr, c), lambda i: (i, 0)),) * n,
        out_shape=(jax.ShapeDtypeStruct((r, c), out_dtype),) * n,
        compiler_params=_params(("arbitrary",)),
    )(*arrs))


def _adam_math(w, g, m, v):
    m_new = ADAM_B1 * m + (1.0 - ADAM_B1) * g
    v_new = ADAM_B2 * v + (1.0 - ADAM_B2) * (g * g)
    m_hat = m_new / (1.0 - ADAM_B1 ** ADAM_STEP)
    v_hat = v_new / (1.0 - ADAM_B2 ** ADAM_STEP)
    delta = -ADAM_LR * (m_hat / (jnp.sqrt(v_hat) + ADAM_EPS) + ADAM_WD * w)
    return delta, m_new, v_new


def _adam(name, wmv, g_parts):
    w0 = wmv[0][0]
    r, c = w0.shape[-2:]
    n_w = len(wmv)
    n_g = len(g_parts[0])
    tr = _tile(r, 256 if n_w == 1 else 88, 8)
    lead = w0.ndim == 3
    at = (lambda ref: ref.at[0]) if lead else (lambda ref: ref)
    n_in = 3 + n_g

    def body(*refs):
        for j in range(n_w):
            ins = refs[j * n_in:(j + 1) * n_in]
            outs = refs[n_w * n_in + 4 * j:n_w * n_in + 4 * j + 4]
            w_ref, m_ref, v_ref = [at(t) for t in ins[:3]]
            g_out, d_out, m_out, v_out = [at(t) for t in outs]
            g = ins[3][...].astype(F32)
            for gr in ins[4:]:
                g = g + gr[...].astype(F32)
            delta, m_new, v_new = _adam_math(w_ref[...], g, m_ref[...], v_ref[...])
            g_out[...] = g
            d_out[...] = delta
            m_out[...] = m_new
            v_out[...] = v_new

    spec = pl.BlockSpec((tr, c), lambda i: (i, 0))
    wspec = pl.BlockSpec((1, tr, c), lambda i: (0, i, 0)) if lead else spec
    shp = jax.ShapeDtypeStruct(w0.shape, F32)
    args = [t for (w, m, v), gp in zip(wmv, g_parts) for t in (w, m, v, *gp)]
    res = pl.pallas_call(
        body, name=name, grid=(r // tr,),
        in_specs=([wspec] * 3 + [spec] * n_g) * n_w, out_specs=(wspec,) * (4 * n_w), out_shape=(shp,) * (4 * n_w),
        compiler_params=_params(("arbitrary",)),
    )(*args)
    return [tuple(res[4 * j:4 * j + 4]) for j in range(n_w)]


def _ffn_fwd(name, h, nw, wg, wu, wd, plan=None, plan_args=()):
    lp, d = h.shape
    nck, f, _ = wg.shape
    tm = _tile(lp, 640)
    last = nck // FFN_CPS - 1

    def body(h_ref, nw_ref, wg_ref, wu_ref, wd_ref, ho_ref, g_ref, u_ref, n_sc, acc_sc):
        k = pl.program_id(1)

        @pl.when(k == 0)
        def _():
            xh, _ = _rms_stats(h_ref[...])
            n_sc[...] = (xh * nw_ref[...]).astype(BF16)
            acc_sc[...] = jnp.zeros_like(acc_sc)

        n = n_sc[...]
        acc = acc_sc[...]
        for c in range(FFN_CPS):
            g = _dot_nt(n, wg_ref[c])
            u = _dot_nt(n, wu_ref[c])
            g_ref[c] = g.astype(BF16)
            u_ref[c] = u.astype(BF16)
            a = (g * _sigmoid(g) * u).astype(BF16)
            acc = acc + _dot(a, wd_ref[c])
        acc_sc[...] = acc

        @pl.when(k == last)
        def _():
            ho_ref[...] = h_ref[...] + FFN_RES * acc_sc[...]

    return _pcall(
        body, name=name, grid=(lp // tm, nck // FFN_CPS), plan=plan, plan_args=plan_args, args=(h, nw, wg, wu, wd),
        in_specs=[pl.BlockSpec((tm, d), lambda i, k: (i, 0)),
                  pl.BlockSpec((1, d), lambda i, k: (0, 0)),
                  pl.BlockSpec((FFN_CPS, f, d), lambda i, k: (k, 0, 0)),
                  pl.BlockSpec((FFN_CPS, f, d), lambda i, k: (k, 0, 0)),
                  pl.BlockSpec((FFN_CPS, f, d), lambda i, k: (k, 0, 0))],
        out_specs=(pl.BlockSpec((tm, d), lambda i, k: (i, 0)),
                   pl.BlockSpec((FFN_CPS, tm, f), lambda i, k: (k, i, 0)),
                   pl.BlockSpec((FFN_CPS, tm, f), lambda i, k: (k, i, 0))),
        out_shape=(jax.ShapeDtypeStruct((lp, d), F32),
                   jax.ShapeDtypeStruct((nck, lp, f), BF16),
                   jax.ShapeDtypeStruct((nck, lp, f), BF16)),
        scratch_shapes=[pltpu.VMEM((tm, d), BF16), pltpu.VMEM((tm, d), F32)])


def _ffn_bwd_act(name, dh, h, nw, g, u, wg, wu, wd, plan=None, plan_args=()):
    lp, d = h.shape
    nck, f, _ = wg.shape
    tm = _tile(lp, 320)
    last = nck // FFN_CPS - 1

    def body(dh_ref, h_ref, nw_ref, g_ref, u_ref, wg_ref, wu_ref, wd_ref,
             dhi_ref, dnw_ref, n_ref, dacc_ref, a_ref, dg_ref, du_ref,
             xh_sc, r_sc, dn_sc):
        i = pl.program_id(0)
        k = pl.program_id(1)

        @pl.when(k == 0)
        def _():
            xh, r = _rms_stats(h_ref[...])
            xh_sc[...] = xh
            r_sc[...] = r
            n_ref[...] = (xh * nw_ref[...]).astype(BF16)
            dacc_ref[...] = (FFN_RES * dh_ref[...]).astype(BF16)
            dn_sc[...] = jnp.zeros_like(dn_sc)

        @pl.when(jnp.logical_and(i == 0, k == 0))
        def _():
            dnw_ref[...] = jnp.zeros_like(dnw_ref)

        dacc = dacc_ref[...]
        dn = dn_sc[...]
        for c in range(FFN_CPS):
            gv = g_ref[c].astype(F32)
            uv = u_ref[c].astype(F32)
            sg = _sigmoid(gv)
            sil = gv * sg
            da = _dot_nt(dacc, wd_ref[c])
            dgk = (da * uv * (sg * (1.0 + gv * (1.0 - sg)))).astype(BF16)
            duk = (da * sil).astype(BF16)
            a_ref[c] = (sil * uv).astype(BF16)
            dg_ref[c] = dgk
            du_ref[c] = duk
            dn = dn + _dot(dgk, wg_ref[c]) + _dot(duk, wu_ref[c])
        dn_sc[...] = dn

        @pl.when(k == last)
        def _():
            dnl = dn_sc[...]
            xh = xh_sc[...]
            dhi_ref[...] = dh_ref[...] + _rms_bwd(dnl, xh, r_sc[...], nw_ref[...])
            dnw_ref[...] += jnp.sum(dnl * xh, axis=0, keepdims=True)

    row = pl.BlockSpec((tm, d), lambda i, k: (i, 0))
    vec = pl.BlockSpec((1, d), lambda i, k: (0, 0))
    hid = pl.BlockSpec((FFN_CPS, tm, f), lambda i, k: (k, i, 0))
    w_fd = pl.BlockSpec((FFN_CPS, f, d), lambda i, k: (k, 0, 0))
    rshape = jax.ShapeDtypeStruct((lp, d), BF16)
    hshape = jax.ShapeDtypeStruct((nck, lp, f), BF16)
    return _pcall(
        body, name=name, grid=(lp // tm, nck // FFN_CPS), plan=plan, plan_args=plan_args,
        args=(dh, h, nw, g, u, wg, wu, wd),
        in_specs=[row, row, vec, hid, hid, w_fd, w_fd, w_fd],
        out_specs=(row, vec, row, row, hid, hid, hid),
        out_shape=(jax.ShapeDtypeStruct((lp, d), F32), jax.ShapeDtypeStruct((1, d), F32),
                   rshape, rshape, hshape, hshape, hshape),
        scratch_shapes=[pltpu.VMEM((tm, d), F32), pltpu.VMEM((tm, 1), F32), pltpu.VMEM((tm, d), F32)])


def _ffn_bwd_w(name, n, dacc, a, dg, du, plan=None, plan_args=()):
    lp, d = n.shape
    nck, _, f = a.shape
    tm = _tile(lp, BWD_W_ROWS)
    last = lp // tm - 1

    def body(n_ref, dacc_ref, a_ref, dg_ref, du_ref, dwg_ref, dwu_ref, dwd_ref, ag_sc, au_sc, ad_sc):
        i = pl.program_id(1)

        @pl.when(i == 0)
        def _():
            ag_sc[...] = jnp.zeros_like(ag_sc)
            au_sc[...] = jnp.zeros_like(au_sc)
            ad_sc[...] = jnp.zeros_like(ad_sc)

        nv = n_ref[...]
        ag_sc[...] += _dot_tn(dg_ref[0], nv)
        au_sc[...] += _dot_tn(du_ref[0], nv)
        ad_sc[...] += _dot_tn(a_ref[0], dacc_ref[...])

        @pl.when(i == last)
        def _():
            dwg_ref[0] = ag_sc[...].astype(BF16)
            dwu_ref[0] = au_sc[...].astype(BF16)
            dwd_ref[0] = ad_sc[...].astype(BF16)

    row = pl.BlockSpec((tm, d), lambda k, i: (i, 0))
    hid = pl.BlockSpec((1, tm, f), lambda k, i: (k, i, 0))
    w_fd = pl.BlockSpec((1, f, d), lambda k, i: (k, 0, 0))
    wshape = jax.ShapeDtypeStruct((nck, f, d), BF16)
    return _pcall(
        body, name=name, grid=(nck, lp // tm), plan=plan, plan_args=plan_args, args=(n, dacc, a, dg, du),
        in_specs=[row, row, hid, hid, hid], out_specs=(w_fd, w_fd, w_fd), out_shape=(wshape,) * 3,
        scratch_shapes=[pltpu.VMEM((f, d), F32)] * 3)


def _ffn_bwd_w_scatter(name, n, dacc, a, dg, du, chip, plan, plan_args):
    lp, d = n.shape
    nck, _, f = a.shape
    tm = _tile(lp, BWD_W_ROWS)
    last_i = lp // tm - 1
    n_w = 3
    n_p = plan.n

    def body(me_ref, n_ref, dacc_ref, a_ref, dg_ref, du_ref, *rest):
        p_ins = rest[:n_p]
        recv = rest[n_p:n_p + n_w]
        p_outs = rest[n_p + n_w:2 * n_p + n_w]
        acc = rest[2 * n_p + n_w:2 * n_p + 2 * n_w]
        stage, send_sems, recv_sems, loc_sems = rest[2 * n_p + 2 * n_w:2 * n_p + 2 * n_w + 4]
        p_sems = rest[2 * n_p + 2 * n_w + 4:]
        p = pl.program_id(0)
        i = pl.program_id(1)
        me = me_ref[0]
        c = lax.axis_index("c")

        def send(w, pos):
            kk = jnp.bitwise_xor(me, nck - 1 - pos)
            diff = jnp.bitwise_xor(kk, me)
            m = jnp.where(diff == 2, 0, jnp.where(diff == 1, 1, 2))
            return pltpu.make_async_remote_copy(
                src_ref=stage.at[lax.rem(pos, 2), w], dst_ref=recv[w].at[me],
                send_sem=send_sems.at[w * 3 + m], recv_sem=recv_sems.at[w * 3 + m],
                device_id=(lax.div(kk, 2), lax.rem(kk, 2), c), device_id_type=MESH)

        @pl.when(jnp.logical_and(p == 0, i == 0))
        def _():
            for cp in plan.copies(p_ins, p_outs, p_sems):
                cp.start()

        @pl.when(i == 0)
        def _():
            for t in acc:
                t[...] = jnp.zeros_like(t)

        nv = n_ref[...]
        acc[0][...] += _dot_tn(dg_ref[0], nv)
        acc[1][...] += _dot_tn(du_ref[0], nv)
        acc[2][...] += _dot_tn(a_ref[0], dacc_ref[...])

        @pl.when(jnp.logical_and(i == last_i, p >= 2))
        def _():
            for w in range(n_w):
                send(w, p - 2).wait_send()

        @pl.when(i == last_i)
        def _():
            for w in range(n_w):
                stage[lax.rem(p, 2), w] = acc[w][...].astype(BF16)

        @pl.when(jnp.logical_and(i == last_i, p < nck - 1))
        def _():
            for w in range(n_w):
                send(w, p).start()

        @pl.when(jnp.logical_and(i == last_i, p == nck - 1))
        def _():
            own = [pltpu.make_async_copy(stage.at[(nck - 1) % 2, w], recv[w].at[me], loc_sems.at[w])
                   for w in range(n_w)]
            for cp in own:
                cp.start()
            for w in range(n_w):
                send(w, nck - 2).wait_send()
            for cp in own:
                cp.wait()
            for w in range(n_w):
                for m in range(3):
                    pltpu.make_async_remote_copy(
                        src_ref=stage.at[0, w], dst_ref=recv[w].at[me],
                        send_sem=send_sems.at[w * 3 + m], recv_sem=recv_sems.at[w * 3 + m],
                        device_id=(0, 0, c), device_id_type=MESH).wait_recv()
            for cp in plan.copies(p_ins, p_outs, p_sems):
                cp.wait()

    chunk = lambda k, me_ref: jnp.bitwise_xor(me_ref[0], nck - 1 - k)
    row = pl.BlockSpec((tm, d), lambda k, i, me_ref: (i, 0))
    hid = pl.BlockSpec((1, tm, f), lambda k, i, me_ref: (chunk(k, me_ref), i, 0))
    wshape = jax.ShapeDtypeStruct((nck, f, d), BF16)
    res = pl.pallas_call(
        body, name=name,
        grid_spec=pltpu.PrefetchScalarGridSpec(
            num_scalar_prefetch=1, grid=(nck, lp // tm),
            in_specs=[row, row, hid, hid, hid] + [ANY_SPEC] * n_p,
            out_specs=(ANY_SPEC,) * (n_w + n_p),
            scratch_shapes=[pltpu.VMEM((f, d), F32)] * n_w + [
                pltpu.VMEM((2, n_w, f, d), BF16), pltpu.SemaphoreType.DMA((n_w * 3,)),
                pltpu.SemaphoreType.DMA((n_w * 3,)), pltpu.SemaphoreType.DMA((n_w,))] + plan.scratch()),
        out_shape=(wshape,) * n_w + plan.out_shape(),
        compiler_params=_params(("arbitrary", "arbitrary")),
    )(chip.reshape(1).astype(jnp.int32), n, dacc, a, dg, du, *plan_args)
    return list(res[:n_w]), list(res[n_w:])


def _inproj_fwd(h, nw, w_in, cosf, sinf, rw):
    lp, d = h.shape
    nck, _, ps = w_in.shape
    proj = nck * ps
    sw = proj - 4 * rw
    tm = _tile(lp, 640)
    scale = HEAD_DIM ** -0.5
    heads = rw // HEAD_DIM

    def body(h_ref, nw_ref, w_ref, cos_ref, sin_ref, n_ref, q_ref, k_ref, v_ref, g_ref, u_ref, p_sc):
        xh, _ = _rms_stats(h_ref[...])
        n = (xh * nw_ref[...]).astype(BF16)
        n_ref[...] = n
        for c in range(nck):
            p_sc[:, c * ps:(c + 1) * ps] = _dot(n, w_ref[c])
        cs = cos_ref[...]
        sn = sin_ref[...]
        for hh in range(heads):
            lo = hh * HEAD_DIM
            qh = p_sc[:, lo:lo + HEAD_DIM]
            q_ref[:, lo:lo + HEAD_DIM] = (qh * cs + pltpu.roll(qh, HEAD_DIM // 2, 1) * sn).astype(BF16)
            kh = p_sc[:, rw + lo:rw + lo + HEAD_DIM]
            k_ref[:, lo:lo + HEAD_DIM] = ((kh * cs + pltpu.roll(kh, HEAD_DIM // 2, 1) * sn) * scale).astype(BF16)
        v_ref[...] = p_sc[:, 2 * rw:3 * rw].astype(BF16)
        g_ref[...] = p_sc[:, 3 * rw:4 * rw]
        u_ref[...] = p_sc[:, 4 * rw:]

    row = lambda w: pl.BlockSpec((tm, w), lambda i: (i, 0))
    return pl.pallas_call(
        body, name="inproj_fwd", grid=(lp // tm,),
        in_specs=[row(d), pl.BlockSpec((1, d), lambda i: (0, 0)),
                  pl.BlockSpec((nck, d, ps), lambda i: (0, 0, 0)), row(HEAD_DIM), row(HEAD_DIM)],
        out_specs=(row(d), row(rw), row(rw), row(rw), row(rw), row(sw)),
        out_shape=(jax.ShapeDtypeStruct((lp, d), BF16),
                   jax.ShapeDtypeStruct((lp, rw), BF16),
                   jax.ShapeDtypeStruct((lp, rw), BF16),
                   jax.ShapeDtypeStruct((lp, rw), BF16),
                   jax.ShapeDtypeStruct((lp, rw), F32),
                   jax.ShapeDtypeStruct((lp, sw), F32)),
        scratch_shapes=[pltpu.VMEM((tm, proj), F32)],
        compiler_params=_params(("arbitrary",)),
    )(h, nw, w_in, cosf, sinf)


def _inproj_bwd(dh, h, nw, n, w_in, dq, dk, dv, dg, du):
    lp, d = h.shape
    nck, _, ps = w_in.shape
    rw = dq.shape[1]
    sw = du.shape[1]
    proj = nck * ps
    tm = _tile(lp, 640)
    last = lp // tm - 1

    def gather_dproj(p_sc, dq_ref, dk_ref, dv_ref, dg_ref, du_ref):
        p_sc[:, 0:rw] = dq_ref[...]
        p_sc[:, rw:2 * rw] = dk_ref[...]
        p_sc[:, 2 * rw:3 * rw] = dv_ref[...]
        p_sc[:, 3 * rw:4 * rw] = dg_ref[...]
        p_sc[:, 4 * rw:] = du_ref[...]

    def act_body(dh_ref, h_ref, nw_ref, w_ref, dq_ref, dk_ref, dv_ref, dg_ref, du_ref, dhi_ref, dnw_ref, p_sc):
        i = pl.program_id(0)

        @pl.when(i == 0)
        def _():
            dnw_ref[...] = jnp.zeros_like(dnw_ref)

        gather_dproj(p_sc, dq_ref, dk_ref, dv_ref, dg_ref, du_ref)
        dn = jnp.zeros((tm, d), F32)
        for c in range(nck):
            dn = dn + _dot_nt(p_sc[:, c * ps:(c + 1) * ps], w_ref[c])
        xh, r = _rms_stats(h_ref[...])
        dhi_ref[...] = dh_ref[...] + _rms_bwd(dn, xh, r, nw_ref[...])
        dnw_ref[...] += jnp.sum(dn * xh, axis=0, keepdims=True)

    def w_body(n_ref, dq_ref, dk_ref, dv_ref, dg_ref, du_ref, dw_ref, p_sc, acc_sc):
        i = pl.program_id(0)

        @pl.when(i == 0)
        def _():
            acc_sc[...] = jnp.zeros_like(acc_sc)

        gather_dproj(p_sc, dq_ref, dk_ref, dv_ref, dg_ref, du_ref)
        nv = n_ref[...]
        for c in range(nck):
            acc_sc[c] += _dot_tn(nv, p_sc[:, c * ps:(c + 1) * ps])

        @pl.when(i == last)
        def _():
            dw_ref[...] = acc_sc[...].astype(BF16)

    row = lambda w: pl.BlockSpec((tm, w), lambda i: (i, 0))
    vec = pl.BlockSpec((1, d), lambda i: (0, 0))
    wsp = pl.BlockSpec((nck, d, ps), lambda i: (0, 0, 0))
    dproj_specs = [row(rw), row(rw), row(rw), row(rw), row(sw)]
    dhi, dnw = pl.pallas_call(
        act_body, name="inproj_bwd_act", grid=(lp // tm,),
        in_specs=[row(d), row(d), vec, wsp] + dproj_specs,
        out_specs=(row(d), vec),
        out_shape=(jax.ShapeDtypeStruct((lp, d), F32), jax.ShapeDtypeStruct((1, d), F32)),
        scratch_shapes=[pltpu.VMEM((tm, proj), BF16)],
        compiler_params=_params(("arbitrary",)),
    )(dh, h, nw, w_in, dq, dk, dv, dg, du)
    dw = pl.pallas_call(
        w_body, name="inproj_bwd_w", grid=(lp // tm,),
        in_specs=[row(d)] + dproj_specs,
        out_specs=wsp, out_shape=jax.ShapeDtypeStruct((nck, d, ps), BF16),
        scratch_shapes=[pltpu.VMEM((tm, proj), BF16), pltpu.VMEM((nck, d, ps), F32)],
        compiler_params=_params(("arbitrary",)),
    )(n, dq, dk, dv, dg, du)
    return dhi, dnw, dw


def _retention_tables():
    h = jnp.arange(RET_HEADS, dtype=F32)
    log_g = jnp.log(1.0 - 2.0 ** (-5.0 - h))
    i = jnp.arange(CHUNK)
    diff = i[:, None] - i[None, :]
    dec = jnp.where(diff[None] >= 0,
                    jnp.exp(log_g[:, None, None] * jnp.maximum(diff, 0)[None].astype(F32)), 0.0)
    pos = jnp.arange(CHUNK, dtype=F32)
    wq = jnp.exp(log_g[:, None] * (pos + 1.0)[None])
    wk = jnp.exp(log_g[:, None] * (CHUNK - 1 - pos)[None])
    gch = jnp.exp(log_g * CHUNK)
    ones = jnp.ones((1, 1, HEAD_DIM), F32)
    return (dec, wq[:, :, None] * ones, wk[:, :, None] * ones,
            gch[:, None, None] * jnp.ones((1, 8, HEAD_DIM), F32))


def _head_norm(o):
    mu = jnp.mean(o, axis=-1, keepdims=True)
    oc = o - mu
    r = lax.rsqrt(jnp.mean(oc * oc, axis=-1, keepdims=True) + EPS)
    return oc * r, r


def _ret_fwd(q, k, v, g, rnw, tables):
    lp, rw = q.shape
    heads = rw // HEAD_DIM
    nch = lp // CHUNK
    dec, wq, wk, gch = tables

    def body(q_ref, k_ref, v_ref, g_ref, w_ref, dec_ref, wq_ref, wk_ref, gch_ref,
             o_ref, ret_ref, sp_ref, s_sc):
        n = pl.program_id(0)

        @pl.when(n == 0)
        def _():
            s_sc[...] = jnp.zeros_like(s_sc)

        cols = [slice(hh * HEAD_DIM, (hh + 1) * HEAD_DIM) for hh in range(heads)]
        s_ins = [s_sc[hh] for hh in range(heads)]
        outs = []
        for hh, cs in enumerate(cols):
            qv, kv, vv = q_ref[:, cs], k_ref[:, cs], v_ref[:, cs]
            s_in = s_ins[hh]
            a = _dot_nt(qv, kv) * dec_ref[hh]
            qw = (qv.astype(F32) * wq_ref[hh]).astype(BF16)
            kw = (kv.astype(F32) * wk_ref[hh]).astype(BF16)
            o = _dot(a.astype(BF16), vv) + _dot(qw, s_in.astype(BF16))
            s_new = gch_ref[hh, 0:1, :] * s_in + _dot_tn(kw, vv)
            xh, _ = _head_norm(o)
            gv = g_ref[:, cs]
            outs.append((o, s_new, (gv * _sigmoid(gv) * (xh * w_ref[:, cs])).astype(BF16)))
        for hh, cs in enumerate(cols):
            o, s_new, ret = outs[hh]
            sp_ref[hh, 0] = s_ins[hh]
            s_sc[hh] = s_new
            o_ref[:, cs] = o
            ret_ref[:, cs] = ret

    blk = pl.BlockSpec((CHUNK, rw), lambda n: (n, 0))
    tab = pl.BlockSpec((heads, CHUNK, HEAD_DIM), lambda n: (0, 0, 0))
    return pl.pallas_call(
        body, name="retention_fwd", grid=(nch,),
        in_specs=[blk, blk, blk, blk, pl.BlockSpec((1, rw), lambda n: (0, 0)),
                  tab, tab, tab, pl.BlockSpec((heads, 8, HEAD_DIM), lambda n: (0, 0, 0))],
        out_specs=(blk, blk, pl.BlockSpec((heads, 1, HEAD_DIM, HEAD_DIM), lambda n: (0, n, 0, 0))),
        out_shape=(jax.ShapeDtypeStruct((lp, rw), F32),
                   jax.ShapeDtypeStruct((lp, rw), BF16),
                   jax.ShapeDtypeStruct((heads, nch, HEAD_DIM, HEAD_DIM), F32)),
        scratch_shapes=[pltpu.VMEM((heads, HEAD_DIM, HEAD_DIM), F32)],
        compiler_params=_params(("arbitrary",)),
    )(q, k, v, g, rnw, dec, wq, wk, gch)


def _ret_bwd(dret, q, k, v, g, o, sprev, rnw, tables, cosf, sinf):
    lp, rw = q.shape
    heads = rw // HEAD_DIM
    nch = lp // CHUNK
    dec, wq, wk, gch = tables
    scale = HEAD_DIM ** -0.5
    half = HEAD_DIM // 2

    def body(dret_ref, q_ref, k_ref, v_ref, g_ref, o_ref, sp_ref, w_ref, dec_ref, wq_ref, wk_ref, gch_ref,
             cos_ref, sin_ref, dq_ref, dk_ref, dv_ref, dg_ref, dw_ref, ds_sc):
        n = pl.program_id(0)

        @pl.when(n == 0)
        def _():
            ds_sc[...] = jnp.zeros_like(ds_sc)
            dw_ref[...] = jnp.zeros_like(dw_ref)

        cosv = cos_ref[...]
        sinv = sin_ref[...]
        cols = [slice(hh * HEAD_DIM, (hh + 1) * HEAD_DIM) for hh in range(heads)]
        ds_ins = [ds_sc[hh] for hh in range(heads)]
        dw_ins = [dw_ref[:, cs] for cs in cols]
        outs = []
        for hh, cs in enumerate(cols):
            qv, kv, vv = q_ref[:, cs], k_ref[:, cs], v_ref[:, cs]
            gv = g_ref[:, cs]
            dr = dret_ref[:, cs]
            w = w_ref[:, cs]
            sg = _sigmoid(gv)
            sil = gv * sg
            xh, r = _head_norm(o_ref[:, cs])
            dgate = (dr * (xh * w) * (sg * (1.0 + gv * (1.0 - sg)))).astype(BF16)
            dyw = dr * sil
            dw_new = dw_ins[hh] + jnp.sum(dyw * xh, axis=0, keepdims=True)
            dxh = dyw * w
            do = r * (dxh - jnp.mean(dxh, axis=-1, keepdims=True)
                      - xh * jnp.mean(dxh * xh, axis=-1, keepdims=True))
            dob = do.astype(BF16)
            dmask = dec_ref[hh]
            wqv = wq_ref[hh]
            wkv = wk_ref[hh]
            a = (_dot_nt(qv, kv) * dmask).astype(BF16)
            da = (_dot_nt(dob, vv) * dmask).astype(BF16)
            qw = (qv.astype(F32) * wqv).astype(BF16)
            kw = (kv.astype(F32) * wkv).astype(BF16)
            s_in = sp_ref[hh, 0].astype(BF16)
            ds = ds_ins[hh]
            dsb = ds.astype(BF16)
            dq = _dot(da, kv) + _dot_nt(dob, s_in) * wqv
            dk = _dot_tn(da, qv) + _dot_nt(vv, dsb) * wkv
            dv = _dot_tn(a, dob) + _dot(kw, dsb)
            ds_new = gch_ref[hh, 0:1, :] * ds + _dot_tn(qw, dob)
            outs.append((dgate, dw_new, ds_new,
                         (dq * cosv + pltpu.roll(dq * sinv, half, 1)).astype(BF16),
                         ((dk * cosv + pltpu.roll(dk * sinv, half, 1)) * scale).astype(BF16),
                         dv.astype(BF16)))
        for hh, cs in enumerate(cols):
            dgate, dw_new, ds_new, dqv, dkv, dvv = outs[hh]
            dg_ref[:, cs] = dgate
            dw_ref[:, cs] = dw_new
            ds_sc[hh] = ds_new
            dq_ref[:, cs] = dqv
            dk_ref[:, cs] = dkv
            dv_ref[:, cs] = dvv

    blk = pl.BlockSpec((CHUNK, rw), lambda n: (nch - 1 - n, 0))
    tab = pl.BlockSpec((heads, CHUNK, HEAD_DIM), lambda n: (0, 0, 0))
    wsp = pl.BlockSpec((1, rw), lambda n: (0, 0))
    pos = pl.BlockSpec((CHUNK, HEAD_DIM), lambda n: (nch - 1 - n, 0))
    bshape = jax.ShapeDtypeStruct((lp, rw), BF16)
    return pl.pallas_call(
        body, name="retention_bwd", grid=(nch,),
        in_specs=[blk, blk, blk, blk, blk, blk,
                  pl.BlockSpec((heads, 1, HEAD_DIM, HEAD_DIM), lambda n: (0, nch - 1 - n, 0, 0)),
                  wsp, tab, tab, tab, pl.BlockSpec((heads, 8, HEAD_DIM), lambda n: (0, 0, 0)), pos, pos],
        out_specs=(blk, blk, blk, blk, wsp),
        out_shape=(bshape, bshape, bshape, bshape, jax.ShapeDtypeStruct((1, rw), F32)),
        scratch_shapes=[pltpu.VMEM((heads, HEAD_DIM, HEAD_DIM), F32)],
        compiler_params=_params(("arbitrary",)),
    )(dret, q, k, v, g, o, sprev, rnw, dec, wq, wk, gch, cosf, sinf)


SCAN_CW = 512


def _s5_prepare(lam_re, lam_im, log_dt, b_re, b_im):
    dt = jnp.exp(log_dt)[:, None]
    er = jnp.exp(lam_re * dt)
    ar = er * jnp.cos(lam_im * dt)
    ai = er * jnp.sin(lam_im * dt)
    den = lam_re * lam_re + lam_im * lam_im
    fr = ((ar - 1.0) * lam_re + ai * lam_im) / den
    fi = (ai * lam_re - (ar - 1.0) * lam_im) / den
    bbr = fr[..., None] * b_re - fi[..., None] * b_im
    bbi = fr[..., None] * b_im + fi[..., None] * b_re
    return ar, ai, bbr, bbi


def _blockdiag_in(t):
    g, p, n = t.shape
    gs = g // N_SEC
    t = t.reshape(N_SEC, gs, p, n)
    eye = jnp.eye(gs, dtype=t.dtype)
    return jnp.einsum("sgpn,gh->sgphn", t, eye).reshape(N_SEC, gs * p, gs * n)


def _blockdiag_out(m, g, p, n):
    gs = g // N_SEC
    m = m.reshape(N_SEC, gs, p, gs, n)
    eye = jnp.eye(gs, dtype=m.dtype)
    return jnp.einsum("sgphn,gh->sgpn", m, eye).reshape(g, p, n)


def _scan_step(xr_ref, xi_ref, r0, pr_of, ar_ref, ai_ref, conj, ncols):
    for cc in range(ncols // SCAN_CW):
        cs = pl.ds(cc * SCAN_CW, SCAN_CW)
        pr, pi = pr_of(cs)
        ar = ar_ref[:, cs]
        ai = ai_ref[:, cs]
        if conj:
            nr = ar * pr + ai * pi
            ni = ar * pi - ai * pr
        else:
            nr = ar * pr - ai * pi
            ni = ar * pi + ai * pr
        xr_ref[pl.ds(r0, 8), cs] = xr_ref[pl.ds(r0, 8), cs] + nr
        xi_ref[pl.ds(r0, 8), cs] = xi_ref[pl.ds(r0, 8), cs] + ni


def _shift_rows(z, down):
    row = lax.broadcasted_iota(jnp.int32, z.shape, 0)
    if down:
        return jnp.where(row == 0, 0.0, pltpu.roll(z, 1, 0))
    return jnp.where(row == N_SEG - 1, 0.0, pltpu.roll(z, N_SEG - 1, 0))


def _s5_fwd(u, bsr, bsi, csr, csi, a8r, a8i, al8r, al8i, d, gluw, glub, nw, jb):
    lp, sw = u.shape
    ns = a8r.shape[1]
    rows = N_SEG * jb
    nblk = lp // rows
    secw = sw // N_SEC
    secn = ns // N_SEC

    def local_scan(u_ref, bsr_ref, bsi_ref, ar_ref, ai_ref, xr_ref, xi_ref, pr_sc, pi_sc):
        for s in range(N_SEC):
            ub = u_ref[:, s * secw:(s + 1) * secw].astype(BF16)
            xr_ref[:, s * secn:(s + 1) * secn] = _dot(ub, bsr_ref[s])
            xi_ref[:, s * secn:(s + 1) * secn] = _dot(ub, bsi_ref[s])
        _scan_step(xr_ref, xi_ref, 0, lambda cs: (pr_sc[:, cs], pi_sc[:, cs]), ar_ref, ai_ref, False, ns)

        def step(j, carry):
            r0 = pl.multiple_of(j * 8, 8)
            rp = pl.multiple_of((j - 1) * 8, 8)
            _scan_step(xr_ref, xi_ref, r0,
                       lambda cs: (xr_ref[pl.ds(rp, 8), cs], xi_ref[pl.ds(rp, 8), cs]),
                       ar_ref, ai_ref, False, ns)
            return carry

        lax.fori_loop(1, jb, step, 0)
        pr_sc[...] = xr_ref[rows - 8:rows, :]
        pi_sc[...] = xi_ref[rows - 8:rows, :]

    def carry_body(u_ref, bsr_ref, bsi_ref, ar_ref, ai_ref, alr_ref, ali_ref, cr_ref, ci_ref,
                   xr_sc, xi_sc, pr_sc, pi_sc):
        b = pl.program_id(0)

        @pl.when(b == 0)
        def _():
            pr_sc[...] = jnp.zeros_like(pr_sc)
            pi_sc[...] = jnp.zeros_like(pi_sc)

        local_scan(u_ref, bsr_ref, bsi_ref, ar_ref, ai_ref, xr_sc, xi_sc, pr_sc, pi_sc)

        @pl.when(b == nblk - 1)
        def _():
            er = _shift_rows(pr_sc[...], True)
            ei = _shift_rows(pi_sc[...], True)
            alr, ali = alr_ref[...], ali_ref[...]
            cr, ci = er, ei
            for _ in range(N_SEG - 2):
                sr = _shift_rows(cr, True)
                si = _shift_rows(ci, True)
                cr = er + alr * sr - ali * si
                ci = ei + alr * si + ali * sr
            cr_ref[...] = cr
            ci_ref[...] = ci

    ublk = pl.BlockSpec((rows, sw), lambda b: (b, 0))
    bspec = pl.BlockSpec((N_SEC, secw, secn), lambda b: (0, 0, 0))
    cspec = pl.BlockSpec((N_SEC, secn, secw), lambda b: (0, 0, 0))
    s8 = pl.BlockSpec((N_SEG, ns), lambda b: (0, 0))
    vec = pl.BlockSpec((1, sw), lambda b: (0, 0))
    s8shape = jax.ShapeDtypeStruct((N_SEG, ns), F32)
    c0r, c0i = pl.pallas_call(
        carry_body, name="s5_fwd_carry", grid=(nblk,),
        in_specs=[ublk, bspec, bspec, s8, s8, s8, s8],
        out_specs=(s8, s8), out_shape=(s8shape, s8shape),
        scratch_shapes=[pltpu.VMEM((rows, ns), F32), pltpu.VMEM((rows, ns), F32),
                        pltpu.VMEM((N_SEG, ns), F32), pltpu.VMEM((N_SEG, ns), F32)],
        compiler_params=_params(("arbitrary",)),
    )(u, bsr, bsi, a8r, a8i, al8r, al8i)

    def main_body(u_ref, bsr_ref, bsi_ref, csr_ref, csi_ref, ar_ref, ai_ref, c0r_ref, c0i_ref,
                  d_ref, gw_ref, gb_ref, nw_ref, xr_ref, xi_ref, yp_ref, out_ref, pr_sc, pi_sc):
        b = pl.program_id(0)

        @pl.when(b == 0)
        def _():
            pr_sc[...] = c0r_ref[...]
            pi_sc[...] = c0i_ref[...]

        local_scan(u_ref, bsr_ref, bsi_ref, ar_ref, ai_ref, xr_ref, xi_ref, pr_sc, pi_sc)
        for s in range(N_SEC):
            xs = pl.ds(s * secn, secn)
            us = pl.ds(s * secw, secw)
            y = _dot(xr_ref[:, xs].astype(BF16), csr_ref[s]) + _dot(xi_ref[:, xs].astype(BF16), csi_ref[s])
            yp_ref[:, us] = y + d_ref[:, us] * u_ref[:, us]
        yp = yp_ref[...]
        t = jnp.tanh(GELU_K0 * (yp + GELU_K1 * yp * yp * yp))
        y1 = 0.5 * yp * (1.0 + t)
        z = _dot(y1.astype(BF16), gw_ref[...]) + gb_ref[...]
        y2 = y1 * _sigmoid(z)
        xh, _ = _rms_stats(y2)
        out_ref[...] = (xh * nw_ref[...]).astype(BF16)

    xblk = pl.BlockSpec((rows, ns), lambda b: (b, 0))
    xr, xi, yp, out = pl.pallas_call(
        main_body, name="s5_fwd", grid=(nblk,),
        in_specs=[ublk, bspec, bspec, cspec, cspec, s8, s8, s8, s8, vec,
                  pl.BlockSpec((sw, sw), lambda b: (0, 0)), vec, vec],
        out_specs=(xblk, xblk, ublk, ublk),
        out_shape=(jax.ShapeDtypeStruct((lp, ns), F32), jax.ShapeDtypeStruct((lp, ns), F32),
                   jax.ShapeDtypeStruct((lp, sw), F32), jax.ShapeDtypeStruct((lp, sw), BF16)),
        scratch_shapes=[pltpu.VMEM((N_SEG, ns), F32), pltpu.VMEM((N_SEG, ns), F32)],
        compiler_params=_params(("arbitrary",)),
    )(u, bsr, bsi, csr, csi, a8r, a8i, c0r, c0i, d, gluw, glub, nw)
    return xr, xi, c0r, c0i, yp, out


def _s5_bwd(dout, u, yp, xr, xi, c0r, c0i, bsrt, bsit, csrt, csit, a8r, a8i, al8r, al8i, d, gluw, glub, nw, jb):
    lp, sw = u.shape
    ns = a8r.shape[1]
    rows = N_SEG * jb
    nblk = lp // rows
    secw = sw // N_SEC
    secn = ns // N_SEC

    def rowwise_bwd(dout_ref, yp_ref, gw_ref, gb_ref, nw_ref):
        ypv = yp_ref[...]
        t = jnp.tanh(GELU_K0 * (ypv + GELU_K1 * ypv * ypv * ypv))
        y1 = 0.5 * ypv * (1.0 + t)
        dgelu = 0.5 * (1.0 + t) + 0.5 * ypv * (1.0 - t * t) * GELU_K0 * (1.0 + 3.0 * GELU_K1 * ypv * ypv)
        gw = gw_ref[...]
        y1b = y1.astype(BF16)
        sg = _sigmoid(_dot(y1b, gw) + gb_ref[...])
        xh, r = _rms_stats(y1 * sg)
        dov = dout_ref[...]
        dy2 = _rms_bwd(dov, xh, r, nw_ref[...])
        dz = dy2 * y1 * sg * (1.0 - sg)
        dzb = dz.astype(BF16)
        dy1 = dy2 * sg + _dot_nt(dzb, gw)
        return dy1 * dgelu, dov * xh, y1b, dzb, dz

    def lam_scan(dyp_of, csrt_ref, csit_ref, ar_ref, ai_ref, lr_sc, li_sc, nr_sc, ni_sc, extra):
        for s in range(N_SEC):
            db = dyp_of(s)
            lr_sc[:, s * secn:(s + 1) * secn] = _dot(db, csrt_ref[s])
            li_sc[:, s * secn:(s + 1) * secn] = _dot(db, csit_ref[s])
        top = rows - 8
        _scan_step(lr_sc, li_sc, top, lambda cs: (nr_sc[:, cs], ni_sc[:, cs]), ar_ref, ai_ref, True, ns)
        extra(top, pl.ds(top - 8, 8))

        def step(jj, carry):
            r0 = pl.multiple_of((jb - 1 - jj) * 8, 8)
            rn = pl.multiple_of((jb - jj) * 8, 8)
            rp = pl.multiple_of((jb - 2 - jj) * 8, 8)
            _scan_step(lr_sc, li_sc, r0,
                       lambda cs: (lr_sc[pl.ds(rn, 8), cs], li_sc[pl.ds(rn, 8), cs]),
                       ar_ref, ai_ref, True, ns)
            extra(r0, pl.ds(rp, 8))
            return carry

        lax.fori_loop(1, jb - 1, step, 0)
        _scan_step(lr_sc, li_sc, 0, lambda cs: (lr_sc[8:16, cs], li_sc[8:16, cs]), ar_ref, ai_ref, True, ns)
        extra(0, None)
        nr_sc[...] = lr_sc[0:8, :]
        ni_sc[...] = li_sc[0:8, :]

    def carry_body(dout_ref, yp_ref, u_ref, gw_ref, gb_ref, nw_ref, csrt_ref, csit_ref, ar_ref, ai_ref,
                   alr_ref, ali_ref, cr_ref, ci_ref, dyp_ref, dnw_ref, dgw_ref, dgb_ref, dd_ref,
                   lr_sc, li_sc, nr_sc, ni_sc):
        b = pl.program_id(0)

        @pl.when(b == 0)
        def _():
            nr_sc[...] = jnp.zeros_like(nr_sc)
            ni_sc[...] = jnp.zeros_like(ni_sc)
            for ref in (dnw_ref, dgw_ref, dgb_ref, dd_ref):
                ref[...] = jnp.zeros_like(ref)

        dyp, dnw_rows, y1b, dzb, dz = rowwise_bwd(dout_ref, yp_ref, gw_ref, gb_ref, nw_ref)
        dnw_ref[...] += jnp.sum(dnw_rows, axis=0, keepdims=True)
        dgw_ref[...] += _dot_tn(y1b, dzb)
        dgb_ref[...] += jnp.sum(dz, axis=0, keepdims=True)
        dd_ref[...] += jnp.sum(dyp * u_ref[...], axis=0, keepdims=True)
        dyp_ref[...] = dyp.astype(BF16)
        lam_scan(lambda s: dyp_ref[:, s * secw:(s + 1) * secw], csrt_ref, csit_ref, ar_ref, ai_ref,
                 lr_sc, li_sc, nr_sc, ni_sc, lambda r0, prev_rows: None)

        @pl.when(b == nblk - 1)
        def _():
            fr = _shift_rows(nr_sc[...], False)
            fi = _shift_rows(ni_sc[...], False)
            alr, ali = alr_ref[...], ali_ref[...]
            cr, ci = fr, fi
            for _ in range(N_SEG - 2):
                sr = _shift_rows(cr, False)
                si = _shift_rows(ci, False)
                cr = fr + alr * sr + ali * si
                ci = fi + alr * si - ali * sr
            cr_ref[...] = cr
            ci_ref[...] = ci

    rev = lambda b: (nblk - 1 - b, 0)
    ublk = pl.BlockSpec((rows, sw), rev)
    xblk = pl.BlockSpec((rows, ns), rev)
    s8 = pl.BlockSpec((N_SEG, ns), lambda b: (0, 0))
    vec = pl.BlockSpec((1, sw), lambda b: (0, 0))
    gws = pl.BlockSpec((sw, sw), lambda b: (0, 0))
    btspec = pl.BlockSpec((N_SEC, secn, secw), lambda b: (0, 0, 0))
    ctspec = pl.BlockSpec((N_SEC, secw, secn), lambda b: (0, 0, 0))
    s8shape = jax.ShapeDtypeStruct((N_SEG, ns), F32)
    lcr, lci, dyp_all, d_nw, d_gw, d_gb, d_d = pl.pallas_call(
        carry_body, name="s5_bwd_carry", grid=(nblk,),
        in_specs=[ublk, ublk, ublk, gws, vec, vec, ctspec, ctspec, s8, s8, s8, s8],
        out_specs=(s8, s8, ublk, vec, gws, vec, vec),
        out_shape=(s8shape, s8shape, jax.ShapeDtypeStruct((lp, sw), BF16), jax.ShapeDtypeStruct((1, sw), F32),
                   jax.ShapeDtypeStruct((sw, sw), F32), jax.ShapeDtypeStruct((1, sw), F32),
                   jax.ShapeDtypeStruct((1, sw), F32)),
        scratch_shapes=[pltpu.VMEM((rows, ns), F32), pltpu.VMEM((rows, ns), F32),
                        pltpu.VMEM((N_SEG, ns), F32), pltpu.VMEM((N_SEG, ns), F32)],
        compiler_params=_params(("arbitrary",)),
    )(dout, yp, u, gluw, glub, nw, csrt, csit, a8r, a8i, al8r, al8i)

    def main_body(dyp_sc, u_ref, xr_ref, xi_ref, xtr_ref, xti_ref, c0r_ref, c0i_ref, lcr_ref, lci_ref,
                  d_ref, bsrt_ref, bsit_ref, csrt_ref, csit_ref, ar_ref, ai_ref,
                  du_ref, dcr_ref, dci_ref, dbr_ref, dbi_ref, dar_ref, dai_ref,
                  lr_sc, li_sc, nr_sc, ni_sc):
        b = pl.program_id(0)

        @pl.when(b == 0)
        def _():
            nr_sc[...] = lcr_ref[...]
            ni_sc[...] = lci_ref[...]
            for ref in (dcr_ref, dci_ref, dbr_ref, dbi_ref, dar_ref, dai_ref):
                ref[...] = jnp.zeros_like(ref)

        for s in range(N_SEC):
            db = dyp_sc[:, s * secw:(s + 1) * secw]
            xs = pl.ds(s * secn, secn)
            dcr_ref[s] += _dot_tn(xr_ref[:, xs].astype(BF16), db)
            dci_ref[s] += _dot_tn(xi_ref[:, xs].astype(BF16), db)

        first = b == nblk - 1

        def acc_da(r0, prev_rows):
            for cc in range(ns // SCAN_CW):
                cs = pl.ds(cc * SCAN_CW, SCAN_CW)
                lr = lr_sc[pl.ds(r0, 8), cs]
                li = li_sc[pl.ds(r0, 8), cs]
                if prev_rows is None:
                    xpr = jnp.where(first, c0r_ref[:, cs], xtr_ref[:, cs])
                    xpi = jnp.where(first, c0i_ref[:, cs], xti_ref[:, cs])
                else:
                    xpr = xr_ref[prev_rows, cs]
                    xpi = xi_ref[prev_rows, cs]
                dar_ref[:, cs] += lr * xpr + li * xpi
                dai_ref[:, cs] += li * xpr - lr * xpi

        lam_scan(lambda s: dyp_sc[:, s * secw:(s + 1) * secw], csrt_ref, csit_ref, ar_ref, ai_ref,
                 lr_sc, li_sc, nr_sc, ni_sc, acc_da)

        for s in range(N_SEC):
            xs = pl.ds(s * secn, secn)
            us = pl.ds(s * secw, secw)
            lrb = lr_sc[:, xs].astype(BF16)
            lib = li_sc[:, xs].astype(BF16)
            du = _dot(lrb, bsrt_ref[s]) + _dot(lib, bsit_ref[s]) + d_ref[:, us] * dyp_sc[:, us].astype(F32)
            du_ref[:, us] = du.astype(BF16)
            ub = u_ref[:, us].astype(BF16)
            dbr_ref[s] += _dot_tn(ub, lrb)
            dbi_ref[s] += _dot_tn(ub, lib)

    tail = pl.BlockSpec((N_SEG, ns), lambda b: (jnp.maximum((nblk - 1 - b) * jb - 1, 0), 0))
    acc_c = pl.BlockSpec((N_SEC, secn, secw), lambda b: (0, 0, 0))
    acc_b = pl.BlockSpec((N_SEC, secw, secn), lambda b: (0, 0, 0))
    du, dcr, dci, dbr, dbi, dar, dai = pl.pallas_call(
        main_body, name="s5_bwd", grid=(nblk,),
        in_specs=[ublk, ublk, xblk, xblk, tail, tail, s8, s8, s8, s8,
                  vec, btspec, btspec, ctspec, ctspec, s8, s8],
        out_specs=(ublk, acc_c, acc_c, acc_b, acc_b, s8, s8),
        out_shape=(jax.ShapeDtypeStruct((lp, sw), BF16),
                   jax.ShapeDtypeStruct((N_SEC, secn, secw), F32),
                   jax.ShapeDtypeStruct((N_SEC, secn, secw), F32),
                   jax.ShapeDtypeStruct((N_SEC, secw, secn), F32),
                   jax.ShapeDtypeStruct((N_SEC, secw, secn), F32),
                   s8shape, s8shape),
        scratch_shapes=[pltpu.VMEM((rows, ns), F32), pltpu.VMEM((rows, ns), F32),
                        pltpu.VMEM((N_SEG, ns), F32), pltpu.VMEM((N_SEG, ns), F32)],
        compiler_params=_params(("arbitrary",)),
    )(dyp_all, u, xr, xi, xr, xi, c0r, c0i, lcr, lci, d, bsrt, bsit, csrt, csit, a8r, a8i)
    return du, d_nw, d_gw, d_gb, d_d, dcr, dci, dbr, dbi, dar, dai


def _outproj_fwd(h, ret, ssm, wo):
    lp, d = h.shape
    nck, rs, _ = wo.shape
    rw = ret.shape[1]
    tm = _tile(lp, 640)
    per = rw // rs

    def body(h_ref, ret_ref, ssm_ref, w_ref, o_ref):
        acc = h_ref[...]
        for c in range(nck):
            src = ret_ref if c < per else ssm_ref
            lo = (c % per) * rs
            acc = acc + _dot(src[:, lo:lo + rs], w_ref[c])
        o_ref[...] = acc

    row = lambda w: pl.BlockSpec((tm, w), lambda i: (i, 0))
    return pl.pallas_call(
        body, name="outproj_fwd", grid=(lp // tm,),
        in_specs=[row(d), row(rw), row(ssm.shape[1]), pl.BlockSpec((nck, rs, d), lambda i: (0, 0, 0))],
        out_specs=row(d), out_shape=jax.ShapeDtypeStruct((lp, d), F32),
        compiler_params=_params(("arbitrary",)),
    )(h, ret, ssm, wo)


def _outproj_bwd(dh, ret, ssm, wo):
    lp, d = dh.shape
    nck, rs, _ = wo.shape
    rw = ret.shape[1]
    sw = ssm.shape[1]
    tm = _tile(lp, 640)
    per = rw // rs
    last = lp // tm - 1

    def body(dh_ref, ret_ref, ssm_ref, w_ref, dret_ref, dssm_ref, dw_ref, acc_sc):
        i = pl.program_id(0)

        @pl.when(i == 0)
        def _():
            acc_sc[...] = jnp.zeros_like(acc_sc)

        dhb = dh_ref[...].astype(BF16)
        for c in range(nck):
            src, dst = (ret_ref, dret_ref) if c < per else (ssm_ref, dssm_ref)
            lo = (c % per) * rs
            dst[:, lo:lo + rs] = _dot_nt(dhb, w_ref[c])
            acc_sc[c] += _dot_tn(src[:, lo:lo + rs], dhb)

        @pl.when(i == last)
        def _():
            dw_ref[...] = acc_sc[...].astype(BF16)

    row = lambda w: pl.BlockSpec((tm, w), lambda i: (i, 0))
    wsp = pl.BlockSpec((nck, rs, d), lambda i: (0, 0, 0))
    return pl.pallas_call(
        body, name="outproj_bwd", grid=(lp // tm,),
        in_specs=[row(d), row(rw), row(sw), wsp],
        out_specs=(row(rw), row(sw), wsp),
        out_shape=(jax.ShapeDtypeStruct((lp, rw), F32), jax.ShapeDtypeStruct((lp, sw), F32),
                   jax.ShapeDtypeStruct((nck, rs, d), BF16)),
        scratch_shapes=[pltpu.VMEM((nck, rs, d), F32)],
        compiler_params=_params(("arbitrary",)),
    )(dh, ret, ssm, wo)


def _loss_head(h, fw, target):
    lp, d = h.shape
    tm = _tile(lp, 640, CHUNK)
    sub = tm // CHUNK

    def body(h_ref, w_ref, *rest):
        t_refs = rest[:sub]
        loss_ref, dh_ref, dw_ref = rest[sub:]
        i = pl.program_id(0)

        @pl.when(i == 0)
        def _():
            loss_ref[...] = jnp.zeros_like(loss_ref)
            dw_ref[...] = jnp.zeros_like(dw_ref)

        w = w_ref[...]
        for j in range(sub):
            rows = pl.ds(j * CHUNK, CHUNK)
            xh, r = _rms_stats(h_ref[rows, :])
            err = xh * w - t_refs[j][...]
            if j == 0:
                err = jnp.where(i == 0, 0.0, err)
            loss_ref[...] += 0.5 * jnp.sum(err * err) / d
            dout = err * (1.0 / d)
            dw_ref[...] += jnp.sum(dout * xh, axis=0, keepdims=True)
            dh_ref[rows, :] = _rms_bwd(dout, xh, r, w)

    t_spec = lambda j: pl.BlockSpec((CHUNK, d), lambda i: (jnp.maximum(i * sub + j - 1, 0), 0))
    return pl.pallas_call(
        body, name="loss_head", grid=(lp // tm,),
        in_specs=[pl.BlockSpec((tm, d), lambda i: (i, 0)), pl.BlockSpec((1, d), lambda i: (0, 0))]
        + [t_spec(j) for j in range(sub)],
        out_specs=(pl.BlockSpec((8, LANE), lambda i: (0, 0)), pl.BlockSpec((tm, d), lambda i: (i, 0)),
                   pl.BlockSpec((1, d), lambda i: (0, 0))),
        out_shape=(jax.ShapeDtypeStruct((8, LANE), F32), jax.ShapeDtypeStruct((lp, d), F32),
                   jax.ShapeDtypeStruct((1, d), F32)),
        compiler_params=_params(("arbitrary",)),
    )(h, fw, *([target] * sub))


def _pack(arrs):
    flat = jnp.concatenate([a.reshape(-1).astype(F32) for a in arrs])
    n = flat.shape[0]
    rows = -(-n // (8 * LANE)) * 8
    return jnp.pad(flat, (0, rows * LANE - n)).reshape(rows, LANE)


def _unpack(packed, shapes):
    flat = packed.reshape(-1)
    out, off = [], 0
    for s in shapes:
        n = math.prod(s)
        out.append(flat[off:off + n].reshape(s))
        off += n
    return out


def _to_segments(a, seg_len):
    return a.reshape(N_SEG, seg_len, a.shape[1]).transpose(1, 0, 2).reshape(a.shape)


def _from_segments(a, seg_len):
    return a.reshape(seg_len, N_SEG, a.shape[1]).transpose(1, 0, 2).reshape(a.shape)


WEIGHT_NAMES = ['meta_tokens', 'ffn1_norm_w', 'ffn1_w_gate', 'ffn1_w_up', 'ffn1_w_down', 'mix_norm_w', 'w_in',
                'ret_norm_w', 'ssm_lambda_re', 'ssm_lambda_im', 'ssm_log_dt', 'ssm_b_re', 'ssm_b_im', 'ssm_c_re',
                'ssm_c_im', 'ssm_d', 'ssm_glu_w', 'ssm_glu_b', 'ssm_norm_w', 'w_out', 'ffn2_norm_w', 'ffn2_w_gate',
                'ffn2_w_up', 'ffn2_w_down', 'final_norm_w']
BIG = ['ffn1_w_gate', 'ffn1_w_up', 'ffn1_w_down', 'w_in', 'ssm_glu_w', 'w_out', 'ffn2_w_gate', 'ffn2_w_up',
       'ffn2_w_down']
TRANSPOSED = ['ffn1_w_gate', 'ffn1_w_up', 'ffn2_w_gate', 'ffn2_w_up']
BIG_EARLY = ['ffn1_w_gate', 'ffn1_w_up', 'ffn1_w_down']
BIG_LATE = [n for n in BIG if n not in BIG_EARLY]
SMALL = [n for n in WEIGHT_NAMES if n not in BIG]


def kernel(x, meta_tokens, ffn1_norm_w, ffn1_w_gate, ffn1_w_up, ffn1_w_down, mix_norm_w, w_in, ret_norm_w, ssm_lambda_re, ssm_lambda_im, ssm_log_dt, ssm_b_re, ssm_b_im, ssm_c_re, ssm_c_im, ssm_d, ssm_glu_w, ssm_glu_b, ssm_norm_w, w_out, ffn2_norm_w, ffn2_w_gate, ffn2_w_up, ffn2_w_down, final_norm_w, loss_target, m_meta_tokens, m_ffn1_norm_w, m_ffn1_w_gate, m_ffn1_w_up, m_ffn1_w_down, m_mix_norm_w, m_w_in, m_ret_norm_w, m_ssm_lambda_re, m_ssm_lambda_im, m_ssm_log_dt, m_ssm_b_re, m_ssm_b_im, m_ssm_c_re, m_ssm_c_im, m_ssm_d, m_ssm_glu_w, m_ssm_glu_b, m_ssm_norm_w, m_w_out, m_ffn2_norm_w, m_ffn2_w_gate, m_ffn2_w_up, m_ffn2_w_down, m_final_norm_w, v_meta_tokens, v_ffn1_norm_w, v_ffn1_w_gate, v_ffn1_w_up, v_ffn1_w_down, v_mix_norm_w, v_w_in, v_ret_norm_w, v_ssm_lambda_re, v_ssm_lambda_im, v_ssm_log_dt, v_ssm_b_re, v_ssm_b_im, v_ssm_c_re, v_ssm_c_im, v_ssm_d, v_ssm_glu_w, v_ssm_glu_b, v_ssm_norm_w, v_w_out, v_ffn2_norm_w, v_ffn2_w_gate, v_ffn2_w_up, v_ffn2_w_down, v_final_norm_w):
    args = locals()
    w = {n: args[n] for n in WEIGHT_NAMES}
    m = {n: args["m_" + n] for n in WEIGHT_NAMES}
    v = {n: args["v_" + n] for n in WEIGHT_NAMES}

    seq, d = x.shape[1], x.shape[2]
    lp = seq + CHUNK
    seg_len = lp // N_SEG
    rw = RET_HEADS * HEAD_DIM
    sw = ssm_d.shape[-1]
    groups = sw // SSM_GROUP
    ns = groups * SSM_STATE
    jb = _tile(seg_len, 40, 8)
    chip = 2 * lax.axis_index("x") + lax.axis_index("y")

    as_fd = lambda t: jnp.swapaxes(t, -1, -2)
    shards = {n: (as_fd(w[n][0]) if n in TRANSPOSED else w[n][0]).astype(BF16) for n in BIG}
    early = [shards[n] for n in BIG_EARLY] + [meta_tokens]
    gathered = _forward_sibling("gather_early_forward",
                                _exchange("gather_early", _allgather_chips_plan(early), early))
    gw = dict(zip(BIG_EARLY, gathered[:-1]))
    meta_full = jnp.transpose(gathered[-1], (1, 0, 2)).reshape(N_META, d)
    late = [shards[n] for n in BIG_LATE]

    pos = jnp.arange(lp, dtype=F32) - float(CHUNK - N_META)
    freqs = 1.0 / (ROPE_BASE ** (jnp.arange(0, HEAD_DIM, 2, dtype=F32) / HEAD_DIM))
    ang = pos[:, None] * freqs[None, :]
    cosf = jnp.concatenate([jnp.cos(ang), jnp.cos(ang)], axis=1)
    sinf = jnp.concatenate([-jnp.sin(ang), jnp.sin(ang)], axis=1)
    tables = _retention_tables()

    lam_re, lam_im, log_dt = ssm_lambda_re[0], ssm_lambda_im[0], ssm_log_dt[0]
    b_re, b_im, c_re, c_im = ssm_b_re[0], ssm_b_im[0], ssm_c_re[0], ssm_c_im[0]
    (ar, ai, bbr, bbi), prep_vjp = jax.vjp(_s5_prepare, lam_re, lam_im, log_dt, b_re, b_im)
    dt = jnp.exp(log_dt)[:, None]
    el = jnp.exp(seg_len * lam_re * dt)
    alr = el * jnp.cos(seg_len * lam_im * dt)
    ali = el * jnp.sin(seg_len * lam_im * dt)
    bc8 = lambda t: jnp.broadcast_to(t.reshape(1, ns), (N_SEG, ns))
    a8r, a8i, al8r, al8i = bc8(ar), bc8(ai), bc8(alr), bc8(ali)
    bsr = _blockdiag_in(jnp.transpose(bbr, (0, 2, 1)))
    bsi = _blockdiag_in(jnp.transpose(bbi, (0, 2, 1)))
    csrt = _blockdiag_in(c_re)
    csit = _blockdiag_in(-c_im)
    tr = lambda t: jnp.transpose(t, (0, 2, 1))
    bsr_b, bsi_b = bsr.astype(BF16), bsi.astype(BF16)
    csr_b, csi_b = tr(csrt).astype(BF16), tr(csit).astype(BF16)
    bsrt_b, bsit_b = tr(bsr).astype(BF16), tr(bsi).astype(BF16)
    csrt_b, csit_b = csrt.astype(BF16), csit.astype(BF16)

    h0 = jnp.concatenate([jnp.zeros((CHUNK - N_META, d), F32), meta_full, x[0]], axis=0)
    (h1, g1, u1), late_half = _ffn_fwd("ffn1_fwd", h0, ffn1_norm_w, gw['ffn1_w_gate'], gw['ffn1_w_up'],
                                       gw['ffn1_w_down'], _allgather_chips_plan(late), late)
    gw.update(zip(BIG_LATE, _forward_sibling("gather_late_forward", late_half)))
    glu_full = gw['ssm_glu_w'].reshape(sw, sw)
    n2, q, k, vv, gate, u = _inproj_fwd(h1, mix_norm_w, gw['w_in'], cosf, sinf, rw)
    o, ret, sprev = _ret_fwd(q, k, vv, gate, ret_norm_w, tables)
    u_seg = _to_segments(u, seg_len)
    xr, xi, c0r, c0i, yp, ssm_seg = _s5_fwd(u_seg, bsr_b, bsi_b, csr_b, csi_b, a8r, a8i, al8r, al8i,
                                            ssm_d, glu_full, ssm_glu_b, ssm_norm_w, jb)
    ssm = _from_segments(ssm_seg, seg_len)
    h2 = _outproj_fwd(h1, ret, ssm, gw['w_out'])
    (h3, g2, u2), _ = _ffn_fwd("ffn2_fwd", h2, ffn2_norm_w, gw['ffn2_w_gate'], gw['ffn2_w_up'], gw['ffn2_w_down'])
    loss_part, dh3, d_final = _loss_head(h3, final_norm_w.reshape(1, d), loss_target[0])

    (dh2, d_ffn2_norm, nb, daccb, ab, dgb, dub), _ = _ffn_bwd_act(
        "ffn2_bwd_act", dh3, h2, ffn2_norm_w, g2, u2, gw['ffn2_w_gate'], gw['ffn2_w_up'], gw['ffn2_w_down'])
    (dwg2, dwu2, dwd2), _ = _ffn_bwd_w("ffn2_bwd_w", nb, daccb, ab, dgb, dub)
    dret, dssm, dwo = _outproj_bwd(dh2, ret, ssm, gw['w_out'])
    (du_seg, d_ssm_norm, d_glu_w, d_glu_b, d_ssm_d, dcr_s, dci_s, dbr_s, dbi_s, dar8, dai8) = _s5_bwd(
        _to_segments(dssm, seg_len), u_seg, yp, xr, xi, c0r, c0i, bsrt_b, bsit_b, csrt_b, csit_b,
        a8r, a8i, al8r, al8i, ssm_d, glu_full, ssm_glu_b, ssm_norm_w, jb)
    du = _from_segments(du_seg, seg_len)
    dq, dk, dv, dgate, d_ret_norm = _ret_bwd(dret, q, k, vv, gate, o, sprev, ret_norm_w, tables, cosf, sinf)
    dh1, d_mix_norm, dwin = _inproj_bwd(dh2, h1, mix_norm_w, n2, gw['w_in'], dq, dk, dv, dgate, du)
    late_parts = {
        'w_in': dwin, 'ssm_glu_w': d_glu_w.reshape(N_CHIP, sw // N_CHIP, sw).astype(BF16), 'w_out': dwo,
        'ffn2_w_gate': dwg2, 'ffn2_w_up': dwu2, 'ffn2_w_down': dwd2,
    }
    late_list = [late_parts[n] for n in BIG_LATE]
    (dh0, d_ffn1_norm, nb, daccb, ab, dgb, dub), late_recv = _ffn_bwd_act(
        "ffn1_bwd_act", dh1, h0, ffn1_norm_w, g1, u1, gw['ffn1_w_gate'], gw['ffn1_w_up'], gw['ffn1_w_down'],
        _alltoall_chips_plan(late_list), late_list)
    grad_x = dh0[CHUNK:][None]
    d_meta = dh0[CHUNK - N_META:CHUNK]

    d_c_re = jnp.transpose(_blockdiag_out(tr(dcr_s), groups, SSM_GROUP, SSM_STATE), (0, 1, 2))
    d_c_im = -_blockdiag_out(tr(dci_s), groups, SSM_GROUP, SSM_STATE)
    d_bbr = jnp.transpose(_blockdiag_out(dbr_s, groups, SSM_GROUP, SSM_STATE), (0, 2, 1))
    d_bbi = jnp.transpose(_blockdiag_out(dbi_s, groups, SSM_GROUP, SSM_STATE), (0, 2, 1))
    d_ar = jnp.sum(dar8, axis=0).reshape(groups, SSM_STATE)
    d_ai = jnp.sum(dai8, axis=0).reshape(groups, SSM_STATE)
    small_parts = [loss_part[0:1, :], d_meta, d_ffn1_norm, d_mix_norm, d_ret_norm, d_ar, d_ai, d_bbr, d_bbi,
                   d_c_re, d_c_im, d_ssm_d, d_glu_b, d_ssm_norm, d_ffn2_norm, d_final]
    small_shapes = [a.shape for a in small_parts]
    packed = _pack(small_parts)
    early_recv, (all_parts,) = _ffn_bwd_w_scatter("ffn1_bwd_w", nb, daccb, ab, dgb, dub, chip,
                                                  _allgather_all_plan([packed]), [packed])
    received = dict(zip(BIG_LATE + BIG_EARLY, late_recv + early_recv))
    ffn_names = [n for n in BIG if n.startswith('ffn')]
    chip_sum = dict(zip(ffn_names, _sum_slots("sum_chips_ffn", [received[n] for n in ffn_names], BF16)))
    for n in BIG:
        if n not in chip_sum:
            chip_sum[n] = _sum_slots("sum_chips_" + n, [received[n]], BF16)[0]
    chip_sums = [chip_sum[n] for n in BIG]
    sib_sums = _swap_sibling("swap_sibling", chip_sums)
    (loss_row, g_meta_full, g_ffn1_norm, g_mix_norm, g_ret_norm, g_ar, g_ai, g_bbr, g_bbi, g_c_re, g_c_im,
     g_ssm_d, g_glu_b, g_ssm_norm, g_ffn2_norm, g_final) = _unpack(_sum_slots("sum_small", [all_parts], F32)[0],
                                                                  small_shapes)
    g_lam_re, g_lam_im, g_log_dt, g_b_re, g_b_im = prep_vjp((g_ar, g_ai, g_bbr, g_bbi))
    loss = loss_row[0, 0]
    g_meta = lax.dynamic_slice(g_meta_full, (0, chip * (d // N_CHIP)), (N_META, d // N_CHIP))
    small_grads = {
        'meta_tokens': g_meta, 'ffn1_norm_w': g_ffn1_norm, 'mix_norm_w': g_mix_norm, 'ret_norm_w': g_ret_norm,
        'ssm_lambda_re': g_lam_re[None], 'ssm_lambda_im': g_lam_im[None], 'ssm_log_dt': g_log_dt[None],
        'ssm_b_re': g_b_re[None], 'ssm_b_im': g_b_im[None], 'ssm_c_re': g_c_re[None], 'ssm_c_im': g_c_im[None],
        'ssm_d': g_ssm_d, 'ssm_glu_b': g_glu_b, 'ssm_norm_w': g_ssm_norm, 'ffn2_norm_w': g_ffn2_norm,
        'final_norm_w': g_final.reshape(d),
    }

    grads, deltas, new_m, new_v = {}, {}, {}, {}
    g_pair = {n: [mine, sib] for n, mine, sib in zip(BIG, chip_sums, sib_sums)}
    view = lambda n, t: as_fd(t) if n in TRANSPOSED else t
    ffn_out = _adam("adam_ffn", [(view(n, w[n]), view(n, m[n]), view(n, v[n])) for n in ffn_names],
                    [g_pair[n] for n in ffn_names])
    for n, outs in zip(ffn_names, ffn_out):
        grads[n], deltas[n], new_m[n], new_v[n] = [view(n, t) for t in outs]
    for n in BIG:
        if n not in ffn_names:
            grads[n], deltas[n], new_m[n], new_v[n] = _adam("adam_" + n, [(w[n], m[n], v[n])], [g_pair[n]])[0]
    sm_shapes = [w[n].shape for n in SMALL]
    sm_out = _adam("adam_small", [(_pack([w[n] for n in SMALL]), _pack([m[n] for n in SMALL]),
                                  _pack([v[n] for n in SMALL]))],
                   [[_pack([small_grads[n].reshape(w[n].shape) for n in SMALL])]])[0]
    for dst, packed in zip((grads, deltas, new_m, new_v), sm_out):
        for n, t in zip(SMALL, _unpack(packed, sm_shapes)):
            dst[n] = t

    return (loss, grad_x, *[grads[n] for n in WEIGHT_NAMES], *[deltas[n] for n in WEIGHT_NAMES],
            *[new_m[n] for n in WEIGHT_NAMES], *[new_v[n] for n in WEIGHT_NAMES])
```

```python
import functools
import math

import jax
import jax.numpy as jnp
from jax import lax
from jax.experimental import pallas as pl
from jax.experimental.pallas import tpu as pltpu

N_META = 16
RET_HEADS = 4
HEAD_DIM = 128
SSM_GROUP = 16
SSM_STATE = 64
CHUNK = 128
ROPE_BASE = 10000.0
EPS = 1e-6
FFN_RES = 0.5
N_SEG = 8
N_SEC = 4
N_CHIP = 4
LANE = 128
FFN_CPS = 2
BWD_W_ROWS = 1664

ADAM_LR = 0.001
ADAM_B1 = 0.9
ADAM_B2 = 0.999
ADAM_EPS = 1e-08
ADAM_WD = 0.01
ADAM_STEP = 10

VMEM_LIMIT = 56 * 1024 * 1024

F32 = jnp.float32
BF16 = jnp.bfloat16
MESH = pl.DeviceIdType.MESH


def _dot(a, b):
    return jnp.dot(a, b, preferred_element_type=F32)


def _dot_nt(a, b):
    return lax.dot_general(a, b, (((1,), (1,)), ((), ())), preferred_element_type=F32)


def _dot_tn(a, b):
    return lax.dot_general(a, b, (((0,), (0,)), ((), ())), preferred_element_type=F32)


def _tile(n, target, mult=64):
    best = None
    t = mult
    while t <= min(n, target):
        if n % t == 0:
            best = t
        t += mult
    assert best is not None, (n, target)
    return best


def _params(sem, vmem=VMEM_LIMIT):
    return pltpu.CompilerParams(dimension_semantics=sem, vmem_limit_bytes=vmem)


def _rms_stats(xf):
    r = lax.rsqrt(jnp.mean(xf * xf, axis=-1, keepdims=True) + EPS)
    return xf * r, r


def _rms_bwd(dy, xh, r, w):
    dxh = dy * w
    return r * (dxh - xh * jnp.mean(dxh * xh, axis=-1, keepdims=True))


def _sigmoid(x):
    return 0.5 * jnp.tanh(0.5 * x) + 0.5


GELU_K0 = math.sqrt(2.0 / math.pi)
GELU_K1 = 0.044715


CHIP_MASKS = [(1, 0, 0), (0, 1, 0), (1, 1, 0)]
ALL_MASKS = [(0, 0, 1), (0, 1, 0), (0, 1, 1), (1, 0, 0), (1, 0, 1), (1, 1, 0), (1, 1, 1)]
SIB_MASKS = [(0, 0, 1)]
ANY_SPEC = pl.BlockSpec(memory_space=pl.ANY)


class _Plan:
    def __init__(self, arrays, masks, n_slots, src_slotted, dst_slotted, local_copy, half=False, forward=False):
        self.shapes = [(a.shape, a.dtype) for a in arrays]
        self.n = len(arrays)
        self.masks = masks
        self.n_slots = n_slots
        self.src_slotted, self.dst_slotted, self.local_copy = src_slotted, dst_slotted, local_copy
        self.half, self.forward = half, forward
        self.n_cp = self.n * len(masks) * (len(CHIP_MASKS) if forward else 1)

    def out_shape(self):
        out = []
        for shp, dt in self.shapes:
            if self.dst_slotted and not self.src_slotted:
                shp = (self.n_slots,) + shp
            elif self.src_slotted and not self.dst_slotted:
                shp = shp[1:]
            out.append(jax.ShapeDtypeStruct(shp, dt))
        return tuple(out)

    def scratch(self):
        return [pltpu.SemaphoreType.DMA((self.n_cp,)), pltpu.SemaphoreType.DMA((self.n_cp,)),
                pltpu.SemaphoreType.DMA((self.n,))]

    def _slot(self, px, py, pc):
        if self.n_slots == 8:
            return 4 * px + 2 * py + pc
        if self.n_slots == 4:
            return 2 * px + py
        return pc

    def copies(self, ins, outs, sems):
        send_sems, recv_sems, loc_sems = sems
        x, y, c = lax.axis_index("x"), lax.axis_index("y"), lax.axis_index("c")
        me = self._slot(x, y, c)
        n_m = len(self.masks)
        cps = []
        for a in range(self.n):
            if self.forward:
                rows = self.shapes[a][0][-2] // 2
                mine = pl.ds(pl.multiple_of(c * rows, 8), rows)
                for j, (mx, my, _) in enumerate(CHIP_MASKS):
                    blk = outs[a].at[2 * (1 - x if mx else x) + (1 - y if my else y), mine]
                    k = a * len(CHIP_MASKS) + j
                    cps.append(pltpu.make_async_remote_copy(
                        src_ref=blk, dst_ref=blk, send_sem=send_sems.at[k], recv_sem=recv_sems.at[k],
                        device_id=(x, y, 1 - c), device_id_type=MESH))
                continue
            if self.local_copy:
                src = ins[a].at[me] if self.src_slotted else ins[a]
                cps.append(pltpu.make_async_copy(src, outs[a].at[me], loc_sems.at[a]))
            for mi, (mx, my, mc) in enumerate(self.masks):
                px = 1 - x if mx else x
                py = 1 - y if my else y
                pc = 1 - c if mc else c
                src = ins[a].at[self._slot(px, py, pc)] if self.src_slotted else ins[a]
                dst = outs[a].at[me] if self.dst_slotted else outs[a]
                if self.half:
                    rows = src.shape[-2] // 2
                    mine = pl.ds(pl.multiple_of(c * rows, 8), rows)
                    src, dst = src.at[mine], dst.at[mine]
                k = a * n_m + mi
                cps.append(pltpu.make_async_remote_copy(
                    src_ref=src, dst_ref=dst, send_sem=send_sems.at[k], recv_sem=recv_sems.at[k],
                    device_id=(px, py, pc), device_id_type=MESH))
        return cps


def _exchange(name, plan, arrays):
    n = plan.n

    def body(*refs):
        cps = plan.copies(refs[:n], refs[n:2 * n], refs[2 * n:])
        for cp in cps:
            cp.start()
        for cp in cps:
            cp.wait()

    outs = pl.pallas_call(
        body, name=name, out_shape=plan.out_shape(),
        in_specs=[ANY_SPEC] * n, out_specs=tuple([ANY_SPEC] * n), scratch_shapes=plan.scratch(),
        input_output_aliases={i: i for i in range(n)} if plan.forward else {},
    )(*arrays)
    return list(outs)


def _pcall(body, *, name, grid, in_specs, out_specs, out_shape, scratch_shapes, args, plan=None, plan_args=()):
    sem = ("arbitrary",) * len(grid)
    if plan is None:
        return pl.pallas_call(body, name=name, grid=grid, in_specs=in_specs, out_specs=out_specs,
                              out_shape=out_shape, scratch_shapes=scratch_shapes,
                              compiler_params=_params(sem))(*args), []
    n_in, n_out, n_scr, n_p = len(in_specs), len(out_specs), len(scratch_shapes), plan.n

    def wrapped(*refs):
        ins = refs[:n_in]
        p_ins = refs[n_in:n_in + n_p]
        o0 = n_in + n_p
        outs = refs[o0:o0 + n_out]
        p_outs = refs[o0 + n_out:o0 + n_out + n_p]
        s0 = o0 + n_out + n_p
        scr = refs[s0:s0 + n_scr]
        sems = refs[s0 + n_scr:]
        ids = [pl.program_id(i) for i in range(len(grid))]
        first = functools.reduce(jnp.logical_and, [i == 0 for i in ids])
        last = functools.reduce(jnp.logical_and, [i == g - 1 for i, g in zip(ids, grid)])

        @pl.when(first)
        def _():
            for cp in plan.copies(p_ins, p_outs, sems):
                cp.start()

        body(*ins, *outs, *scr)

        @pl.when(last)
        def _():
            for cp in plan.copies(p_ins, p_outs, sems):
                cp.wait()

    res = pl.pallas_call(
        wrapped, name=name, grid=grid,
        in_specs=list(in_specs) + [ANY_SPEC] * n_p,
        out_specs=tuple(out_specs) + (ANY_SPEC,) * n_p,
        out_shape=tuple(out_shape) + plan.out_shape(),
        scratch_shapes=list(scratch_shapes) + plan.scratch(),
        compiler_params=_params(sem),
    )(*args, *plan_args)
    return res[:n_out], list(res[n_out:])


def _allgather_chips_plan(arrays):
    return _Plan(arrays, CHIP_MASKS, 4, False, True, True, half=True)


def _forward_sibling(name, gathered):
    return _exchange(name, _Plan(gathered, SIB_MASKS, 4, True, True, False, forward=True), gathered)


def _alltoall_chips_plan(arrays):
    return _Plan(arrays, CHIP_MASKS, 4, True, True, True)


def _swap_sibling(name, arrays):
    return _exchange(name, _Plan(arrays, SIB_MASKS, 2, False, False, False), arrays)


def _allgather_all_plan(arrays):
    return _Plan(arrays, ALL_MASKS, 8, False, True, True)


def _sum_slots(name, arrs, out_dtype):
    s, r = arrs[0].shape[0], arrs[0].shape[-2]
    c = arrs[0].shape[-1] * (2 if arrs[0].ndim == 4 else 1)
    n = len(arrs)
    tr = _tile(r, 512 if n == 1 else 176, 8)

    def body(*refs):
        for a_ref, o_ref in zip(refs[:n], refs[n:]):
            if len(a_ref.shape) == 4:
                for half in range(2):
                    acc = a_ref[0, half].astype(F32)
                    for i in range(1, s):
                        acc = acc + a_ref[i, half].astype(F32)
                    o_ref[:, half * (c // 2):(half + 1) * (c // 2)] = acc.astype(out_dtype)
            else:
                acc = a_ref[0].astype(F32)
                for i in range(1, s):
                    acc = acc + a_ref[i].astype(F32)
                o_ref[...] = acc.astype(out_dtype)

    def in_spec(a):
        if a.ndim == 4:
            return pl.BlockSpec((s, 2, tr, c // 2), lambda i: (0, 0, i, 0))
        return pl.BlockSpec((s, tr, c), lambda i: (0, i, 0))

    return list(pl.pallas_call(
        body, name=name, grid=(r // tr,),
        in_specs=[in_spec(a) for a in arrs],
        out_specs=(pl.BlockSpec((tr, c), lambda i: (i, 0)),) * n,
        out_shape=(jax.ShapeDtypeStruct((r, c), out_dtype),) * n,
        compiler_params=_params(("arbitrary",)),
    )(*arrs))


def _adam_math(w, g, m, v):
    m_new = ADAM_B1 * m + (1.0 - ADAM_B1) * g
    v_new = ADAM_B2 * v + (1.0 - ADAM_B2) * (g * g)
    m_hat = m_new / (1.0 - ADAM_B1 ** ADAM_STEP)
    v_hat = v_new / (1.0 - ADAM_B2 ** ADAM_STEP)
    delta = -ADAM_LR * (m_hat / (jnp.sqrt(v_hat) + ADAM_EPS) + ADAM_WD * w)
    return delta, m_new, v_new


def _adam(name, wmv, g_parts):
    w0 = wmv[0][0]
    r, c = w0.shape[-2:]
    n_w = len(wmv)
    n_g = len(g_parts[0])
    tr = _tile(r, 256 if n_w == 1 else 88, 8)
    lead = w0.ndim == 3
    at = (lambda ref: ref.at[0]) if lead else (lambda ref: ref)
    n_in = 3 + n_g

    def body(*refs):
        for j in range(n_w):
            ins = refs[j * n_in:(j + 1) * n_in]
            outs = refs[n_w * n_in + 4 * j:n_w * n_in + 4 * j + 4]
            w_ref, m_ref, v_ref = [at(t) for t in ins[:3]]
            g_out, d_out, m_out, v_out = [at(t) for t in outs]
            g = ins[3][...].astype(F32)
            for gr in ins[4:]:
                g = g + gr[...].astype(F32)
            delta, m_new, v_new = _adam_math(w_ref[...], g, m_ref[...], v_ref[...])
            g_out[...] = g
            d_out[...] = delta
            m_out[...] = m_new
            v_out[...] = v_new

    spec = pl.BlockSpec((tr, c), lambda i: (i, 0))
    wspec = pl.BlockSpec((1, tr, c), lambda i: (0, i, 0)) if lead else spec
    shp = jax.ShapeDtypeStruct(w0.shape, F32)
    args = [t for (w, m, v), gp in zip(wmv, g_parts) for t in (w, m, v, *gp)]
    res = pl.pallas_call(
        body, name=name, grid=(r // tr,),
        in_specs=([wspec] * 3 + [spec] * n_g) * n_w, out_specs=(wspec,) * (4 * n_w), out_shape=(shp,) * (4 * n_w),
        compiler_params=_params(("arbitrary",)),
    )(*args)
    return [tuple(res[4 * j:4 * j + 4]) for j in range(n_w)]


SUB_ROWS = 64


def _tile_parts(tm, d, head, x):
    nsub = tm // SUB_ROWS
    off = head.shape[0] // SUB_ROWS
    specs = [pl.BlockSpec(head.shape, lambda i, k: (0, 0))] + [
        pl.BlockSpec((SUB_ROWS, d), lambda i, k, j=j: (jnp.maximum(i * nsub + j - off, 0), 0)) for j in range(nsub)]

    def assemble(i, part_refs, h_sc):
        head_ref, x_refs = part_refs[0], part_refs[1:]
        for j in range(nsub):
            rows = slice(j * SUB_ROWS, (j + 1) * SUB_ROWS)
            val = x_refs[j][...]
            if j < off:
                val = jnp.where(i == 0, head_ref[rows, :], val)
            h_sc[rows, :] = val

    return specs, [head] + [x] * nsub, assemble


def _h_source(body, h, tm, d):
    if not isinstance(h, tuple):
        return body, [pl.BlockSpec((tm, d), lambda i, k: (i, 0))], [h], []
    specs, args, assemble = _tile_parts(tm, d, *h)
    n_h = len(specs)

    def with_parts(*refs):
        h_sc = refs[-1]

        @pl.when(pl.program_id(1) == 0)
        def _():
            assemble(pl.program_id(0), refs[:n_h], h_sc)

        body(h_sc, *refs[n_h:-1])

    return with_parts, specs, args, [pltpu.VMEM((tm, d), F32)]


def _ffn_fwd(name, h, nw, wg, wu, wd, plan=None, plan_args=()):
    lp, d = (h[0].shape[0] + h[1].shape[0], h[1].shape[1]) if isinstance(h, tuple) else h.shape
    nck, f, _ = wg.shape
    tm = _tile(lp, 640)
    last = nck // FFN_CPS - 1

    def body(h_ref, nw_ref, wg_ref, wu_ref, wd_ref, ho_ref, g_ref, u_ref, n_sc, acc_sc):
        k = pl.program_id(1)

        @pl.when(k == 0)
        def _():
            xh, _ = _rms_stats(h_ref[...])
            n_sc[...] = (xh * nw_ref[...]).astype(BF16)
            acc_sc[...] = jnp.zeros_like(acc_sc)

        n = n_sc[...]
        acc = acc_sc[...]
        for c in range(FFN_CPS):
            g = _dot_nt(n, wg_ref[c])
            u = _dot_nt(n, wu_ref[c])
            g_ref[c] = g.astype(BF16)
            u_ref[c] = u.astype(BF16)
            a = (g * _sigmoid(g) * u).astype(BF16)
            acc = acc + _dot(a, wd_ref[c])
        acc_sc[...] = acc

        @pl.when(k == last)
        def _():
            ho_ref[...] = h_ref[...] + FFN_RES * acc_sc[...]

    body, h_specs, h_args, h_scratch = _h_source(body, h, tm, d)
    w_fd = pl.BlockSpec((FFN_CPS, f, d), lambda i, k: (k, 0, 0))
    hid = pl.BlockSpec((FFN_CPS, tm, f), lambda i, k: (k, i, 0))
    return _pcall(
        body, name=name, grid=(lp // tm, nck // FFN_CPS), plan=plan, plan_args=plan_args,
        args=(*h_args, nw, wg, wu, wd),
        in_specs=h_specs + [pl.BlockSpec((1, d), lambda i, k: (0, 0)), w_fd, w_fd, w_fd],
        out_specs=(pl.BlockSpec((tm, d), lambda i, k: (i, 0)), hid, hid),
        out_shape=(jax.ShapeDtypeStruct((lp, d), F32),
                   jax.ShapeDtypeStruct((nck, lp, f), BF16),
                   jax.ShapeDtypeStruct((nck, lp, f), BF16)),
        scratch_shapes=[pltpu.VMEM((tm, d), BF16), pltpu.VMEM((tm, d), F32)] + h_scratch)


def _ffn_bwd_act(name, dh, h, nw, g, u, wg, wu, wd, plan=None, plan_args=()):
    lp, d = dh.shape
    nck, f, _ = wg.shape
    tm = _tile(lp, 320)
    last = nck // FFN_CPS - 1

    def body(h_ref, dh_ref, nw_ref, g_ref, u_ref, wg_ref, wu_ref, wd_ref,
             dhi_ref, dnw_ref, n_ref, dacc_ref, a_ref, dg_ref, du_ref,
             xh_sc, r_sc, dn_sc):
        i = pl.program_id(0)
        k = pl.program_id(1)

        @pl.when(k == 0)
        def _():
            xh, r = _rms_stats(h_ref[...])
            xh_sc[...] = xh
            r_sc[...] = r
            n_ref[...] = (xh * nw_ref[...]).astype(BF16)
            dacc_ref[...] = (FFN_RES * dh_ref[...]).astype(BF16)
            dn_sc[...] = jnp.zeros_like(dn_sc)

        @pl.when(jnp.logical_and(i == 0, k == 0))
        def _():
            dnw_ref[...] = jnp.zeros_like(dnw_ref)

        dacc = dacc_ref[...]
        dn = dn_sc[...]
        for c in range(FFN_CPS):
            gv = g_ref[c].astype(F32)
            uv = u_ref[c].astype(F32)
            sg = _sigmoid(gv)
            sil = gv * sg
            da = _dot_nt(dacc, wd_ref[c])
            dgk = (da * uv * (sg * (1.0 + gv * (1.0 - sg)))).astype(BF16)
            duk = (da * sil).astype(BF16)
            a_ref[c] = (sil * uv).astype(BF16)
            dg_ref[c] = dgk
            du_ref[c] = duk
            dn = dn + _dot(dgk, wg_ref[c]) + _dot(duk, wu_ref[c])
        dn_sc[...] = dn

        @pl.when(k == last)
        def _():
            dnl = dn_sc[...]
            xh = xh_sc[...]
            dhi_ref[...] = dh_ref[...] + _rms_bwd(dnl, xh, r_sc[...], nw_ref[...])
            dnw_ref[...] += jnp.sum(dnl * xh, axis=0, keepdims=True)

    body, h_specs, h_args, h_scratch = _h_source(body, h, tm, d)
    row = pl.BlockSpec((tm, d), lambda i, k: (i, 0))
    vec = pl.BlockSpec((1, d), lambda i, k: (0, 0))
    hid = pl.BlockSpec((FFN_CPS, tm, f), lambda i, k: (k, i, 0))
    w_fd = pl.BlockSpec((FFN_CPS, f, d), lambda i, k: (k, 0, 0))
    rshape = jax.ShapeDtypeStruct((lp, d), BF16)
    hshape = jax.ShapeDtypeStruct((nck, lp, f), BF16)
    return _pcall(
        body, name=name, grid=(lp // tm, nck // FFN_CPS), plan=plan, plan_args=plan_args,
        args=(*h_args, dh, nw, g, u, wg, wu, wd),
        in_specs=h_specs + [row, vec, hid, hid, w_fd, w_fd, w_fd],
        out_specs=(row, vec, row, row, hid, hid, hid),
        out_shape=(jax.ShapeDtypeStruct((lp, d), F32), jax.ShapeDtypeStruct((1, d), F32),
                   rshape, rshape, hshape, hshape, hshape),
        scratch_shapes=[pltpu.VMEM((tm, d), F32), pltpu.VMEM((tm, 1), F32), pltpu.VMEM((tm, d), F32)] + h_scratch)


def _ffn_bwd_w(name, n, dacc, a, dg, du, plan=None, plan_args=()):
    lp, d = n.shape
    nck, _, f = a.shape
    tm = _tile(lp, BWD_W_ROWS)
    last = lp // tm - 1

    def body(n_ref, dacc_ref, a_ref, dg_ref, du_ref, dwg_ref, dwu_ref, dwd_ref, ag_sc, au_sc, ad_sc):
        i = pl.program_id(1)

        @pl.when(i == 0)
        def _():
            ag_sc[...] = jnp.zeros_like(ag_sc)
            au_sc[...] = jnp.zeros_like(au_sc)
            ad_sc[...] = jnp.zeros_like(ad_sc)

        nv = n_ref[...]
        ag_sc[...] += _dot_tn(dg_ref[0], nv)
        au_sc[...] += _dot_tn(du_ref[0], nv)
        ad_sc[...] += _dot_tn(a_ref[0], dacc_ref[...])

        @pl.when(i == last)
        def _():
            dwg_ref[0] = ag_sc[...].astype(BF16)
            dwu_ref[0] = au_sc[...].astype(BF16)
            dwd_ref[0] = ad_sc[...].astype(BF16)

    row = pl.BlockSpec((tm, d), lambda k, i: (i, 0))
    hid = pl.BlockSpec((1, tm, f), lambda k, i: (k, i, 0))
    w_fd = pl.BlockSpec((1, f, d), lambda k, i: (k, 0, 0))
    wshape = jax.ShapeDtypeStruct((nck, f, d), BF16)
    return _pcall(
        body, name=name, grid=(nck, lp // tm), plan=plan, plan_args=plan_args, args=(n, dacc, a, dg, du),
        in_specs=[row, row, hid, hid, hid], out_specs=(w_fd, w_fd, w_fd), out_shape=(wshape,) * 3,
        scratch_shapes=[pltpu.VMEM((f, d), F32)] * 3)


def _ffn_bwd_w_scatter(name, n, dacc, a, dg, du, chip, plan, plan_args):
    lp, d = n.shape
    nck, _, f = a.shape
    tm = _tile(lp, BWD_W_ROWS)
    last_i = lp // tm - 1
    n_w = 3
    n_p = plan.n

    def body(me_ref, n_ref, dacc_ref, a_ref, dg_ref, du_ref, *rest):
        p_ins = rest[:n_p]
        recv = rest[n_p:n_p + n_w]
        p_outs = rest[n_p + n_w:2 * n_p + n_w]
        acc = rest[2 * n_p + n_w:2 * n_p + 2 * n_w]
        stage, send_sems, recv_sems, loc_sems = rest[2 * n_p + 2 * n_w:2 * n_p + 2 * n_w + 4]
        p_sems = rest[2 * n_p + 2 * n_w + 4:]
        p = pl.program_id(0)
        i = pl.program_id(1)
        me = me_ref[0]
        c = lax.axis_index("c")

        def send(w, pos):
            kk = jnp.bitwise_xor(me, nck - 1 - pos)
            diff = jnp.bitwise_xor(kk, me)
            m = jnp.where(diff == 2, 0, jnp.where(diff == 1, 1, 2))
            return pltpu.make_async_remote_copy(
                src_ref=stage.at[lax.rem(pos, 2), w], dst_ref=recv[w].at[me],
                send_sem=send_sems.at[w * 3 + m], recv_sem=recv_sems.at[w * 3 + m],
                device_id=(lax.div(kk, 2), lax.rem(kk, 2), c), device_id_type=MESH)

        @pl.when(jnp.logical_and(p == 0, i == 0))
        def _():
            for cp in plan.copies(p_ins, p_outs, p_sems):
                cp.start()

        @pl.when(i == 0)
        def _():
            for t in acc:
                t[...] = jnp.zeros_like(t)

        nv = n_ref[...]
        acc[0][...] += _dot_tn(dg_ref[0], nv)
        acc[1][...] += _dot_tn(du_ref[0], nv)
        acc[2][...] += _dot_tn(a_ref[0], dacc_ref[...])

        @pl.when(jnp.logical_and(i == last_i, p >= 2))
        def _():
            for w in range(n_w):
                send(w, p - 2).wait_send()

        @pl.when(i == last_i)
        def _():
            for w in range(n_w):
                stage[lax.rem(p, 2), w] = acc[w][...].astype(BF16)

        @pl.when(jnp.logical_and(i == last_i, p < nck - 1))
        def _():
            for w in range(n_w):
                send(w, p).start()

        @pl.when(jnp.logical_and(i == last_i, p == nck - 1))
        def _():
            own = [pltpu.make_async_copy(stage.at[(nck - 1) % 2, w], recv[w].at[me], loc_sems.at[w])
                   for w in range(n_w)]
            for cp in own:
                cp.start()
            for w in range(n_w):
                send(w, nck - 2).wait_send()
            for cp in own:
                cp.wait()
            for w in range(n_w):
                for m in range(3):
                    pltpu.make_async_remote_copy(
                        src_ref=stage.at[0, w], dst_ref=recv[w].at[me],
                        send_sem=send_sems.at[w * 3 + m], recv_sem=recv_sems.at[w * 3 + m],
                        device_id=(0, 0, c), device_id_type=MESH).wait_recv()
            for cp in plan.copies(p_ins, p_outs, p_sems):
                cp.wait()

    chunk = lambda k, me_ref: jnp.bitwise_xor(me_ref[0], nck - 1 - k)
    row = pl.BlockSpec((tm, d), lambda k, i, me_ref: (i, 0))
    hid = pl.BlockSpec((1, tm, f), lambda k, i, me_ref: (chunk(k, me_ref), i, 0))
    wshape = jax.ShapeDtypeStruct((nck, f, d), BF16)
    res = pl.pallas_call(
        body, name=name,
        grid_spec=pltpu.PrefetchScalarGridSpec(
            num_scalar_prefetch=1, grid=(nck, lp // tm),
            in_specs=[row, row, hid, hid, hid] + [ANY_SPEC] * n_p,
            out_specs=(ANY_SPEC,) * (n_w + n_p),
            scratch_shapes=[pltpu.VMEM((f, d), F32)] * n_w + [
                pltpu.VMEM((2, n_w, f, d), BF16), pltpu.SemaphoreType.DMA((n_w * 3,)),
                pltpu.SemaphoreType.DMA((n_w * 3,)), pltpu.SemaphoreType.DMA((n_w,))] + plan.scratch()),
        out_shape=(wshape,) * n_w + plan.out_shape(),
        compiler_params=_params(("arbitrary", "arbitrary")),
    )(chip.reshape(1).astype(jnp.int32), n, dacc, a, dg, du, *plan_args)
    return list(res[:n_w]), list(res[n_w:])


def _inproj_fwd(h, nw, w_in, cosf, sinf, rw):
    lp, d = h.shape
    nck, _, ps = w_in.shape
    proj = nck * ps
    sw = proj - 4 * rw
    tm = _tile(lp, 640)
    scale = HEAD_DIM ** -0.5
    heads = rw // HEAD_DIM

    def body(h_ref, nw_ref, w_ref, cos_ref, sin_ref, n_ref, q_ref, k_ref, v_ref, g_ref, u_ref, p_sc):
        xh, _ = _rms_stats(h_ref[...])
        n = (xh * nw_ref[...]).astype(BF16)
        n_ref[...] = n
        for c in range(nck):
            p_sc[:, c * ps:(c + 1) * ps] = _dot(n, w_ref[c])
        cs = cos_ref[...]
        sn = sin_ref[...]
        for hh in range(heads):
            lo = hh * HEAD_DIM
            qh = p_sc[:, lo:lo + HEAD_DIM]
            q_ref[:, lo:lo + HEAD_DIM] = (qh * cs + pltpu.roll(qh, HEAD_DIM // 2, 1) * sn).astype(BF16)
            kh = p_sc[:, rw + lo:rw + lo + HEAD_DIM]
            k_ref[:, lo:lo + HEAD_DIM] = ((kh * cs + pltpu.roll(kh, HEAD_DIM // 2, 1) * sn) * scale).astype(BF16)
        v_ref[...] = p_sc[:, 2 * rw:3 * rw].astype(BF16)
        g_ref[...] = p_sc[:, 3 * rw:4 * rw]
        u_ref[...] = p_sc[:, 4 * rw:]

    row = lambda w: pl.BlockSpec((tm, w), lambda i: (i, 0))
    return pl.pallas_call(
        body, name="inproj_fwd", grid=(lp // tm,),
        in_specs=[row(d), pl.BlockSpec((1, d), lambda i: (0, 0)),
                  pl.BlockSpec((nck, d, ps), lambda i: (0, 0, 0)), row(HEAD_DIM), row(HEAD_DIM)],
        out_specs=(row(d), row(rw), row(rw), row(rw), row(rw), row(sw)),
        out_shape=(jax.ShapeDtypeStruct((lp, d), BF16),
                   jax.ShapeDtypeStruct((lp, rw), BF16),
                   jax.ShapeDtypeStruct((lp, rw), BF16),
                   jax.ShapeDtypeStruct((lp, rw), BF16),
                   jax.ShapeDtypeStruct((lp, rw), F32),
                   jax.ShapeDtypeStruct((lp, sw), F32)),
        scratch_shapes=[pltpu.VMEM((tm, proj), F32)],
        compiler_params=_params(("arbitrary",)),
    )(h, nw, w_in, cosf, sinf)


def _inproj_bwd(dh, h, nw, n, w_in, dq, dk, dv, dg, du):
    lp, d = h.shape
    nck, _, ps = w_in.shape
    rw = dq.shape[1]
    sw = du.shape[1]
    proj = nck * ps
    tm = _tile(lp, 640)
    last = lp // tm - 1

    def gather_dproj(p_sc, dq_ref, dk_ref, dv_ref, dg_ref, du_ref):
        p_sc[:, 0:rw] = dq_ref[...]
        p_sc[:, rw:2 * rw] = dk_ref[...]
        p_sc[:, 2 * rw:3 * rw] = dv_ref[...]
        p_sc[:, 3 * rw:4 * rw] = dg_ref[...]
        p_sc[:, 4 * rw:] = du_ref[...]

    def act_body(dh_ref, h_ref, nw_ref, w_ref, dq_ref, dk_ref, dv_ref, dg_ref, du_ref, dhi_ref, dnw_ref, p_sc):
        i = pl.program_id(0)

        @pl.when(i == 0)
        def _():
            dnw_ref[...] = jnp.zeros_like(dnw_ref)

        gather_dproj(p_sc, dq_ref, dk_ref, dv_ref, dg_ref, du_ref)
        dn = jnp.zeros((tm, d), F32)
        for c in range(nck):
            dn = dn + _dot_nt(p_sc[:, c * ps:(c + 1) * ps], w_ref[c])
        xh, r = _rms_stats(h_ref[...])
        dhi_ref[...] = dh_ref[...] + _rms_bwd(dn, xh, r, nw_ref[...])
        dnw_ref[...] += jnp.sum(dn * xh, axis=0, keepdims=True)

    def w_body(n_ref, dq_ref, dk_ref, dv_ref, dg_ref, du_ref, dw_ref, p_sc, acc_sc):
        i = pl.program_id(0)

        @pl.when(i == 0)
        def _():
            acc_sc[...] = jnp.zeros_like(acc_sc)

        gather_dproj(p_sc, dq_ref, dk_ref, dv_ref, dg_ref, du_ref)
        nv = n_ref[...]
        for c in range(nck):
            acc_sc[c] += _dot_tn(nv, p_sc[:, c * ps:(c + 1) * ps])

        @pl.when(i == last)
        def _():
            dw_ref[...] = acc_sc[...].astype(BF16)

    row = lambda w: pl.BlockSpec((tm, w), lambda i: (i, 0))
    vec = pl.BlockSpec((1, d), lambda i: (0, 0))
    wsp = pl.BlockSpec((nck, d, ps), lambda i: (0, 0, 0))
    dproj_specs = [row(rw), row(rw), row(rw), row(rw), row(sw)]
    dhi, dnw = pl.pallas_call(
        act_body, name="inproj_bwd_act", grid=(lp // tm,),
        in_specs=[row(d), row(d), vec, wsp] + dproj_specs,
        out_specs=(row(d), vec),
        out_shape=(jax.ShapeDtypeStruct((lp, d), F32), jax.ShapeDtypeStruct((1, d), F32)),
        scratch_shapes=[pltpu.VMEM((tm, proj), BF16)],
        compiler_params=_params(("arbitrary",)),
    )(dh, h, nw, w_in, dq, dk, dv, dg, du)
    dw = pl.pallas_call(
        w_body, name="inproj_bwd_w", grid=(lp // tm,),
        in_specs=[row(d)] + dproj_specs,
        out_specs=wsp, out_shape=jax.ShapeDtypeStruct((nck, d, ps), BF16),
        scratch_shapes=[pltpu.VMEM((tm, proj), BF16), pltpu.VMEM((nck, d, ps), F32)],
        compiler_params=_params(("arbitrary",)),
    )(n, dq, dk, dv, dg, du)
    return dhi, dnw, dw


def _retention_tables():
    h = jnp.arange(RET_HEADS, dtype=F32)
    log_g = jnp.log(1.0 - 2.0 ** (-5.0 - h))
    i = jnp.arange(CHUNK)
    diff = i[:, None] - i[None, :]
    dec = jnp.where(diff[None] >= 0,
                    jnp.exp(log_g[:, None, None] * jnp.maximum(diff, 0)[None].astype(F32)), 0.0)
    pos = jnp.arange(CHUNK, dtype=F32)
    wq = jnp.exp(log_g[:, None] * (pos + 1.0)[None])
    wk = jnp.exp(log_g[:, None] * (CHUNK - 1 - pos)[None])
    gch = jnp.exp(log_g * CHUNK)
    ones = jnp.ones((1, 1, HEAD_DIM), F32)
    return (dec, wq[:, :, None] * ones, wk[:, :, None] * ones,
            gch[:, None, None] * jnp.ones((1, 8, HEAD_DIM), F32))


def _head_norm(o):
    mu = jnp.mean(o, axis=-1, keepdims=True)
    oc = o - mu
    r = lax.rsqrt(jnp.mean(oc * oc, axis=-1, keepdims=True) + EPS)
    return oc * r, r


def _ret_fwd(q, k, v, g, rnw, tables):
    lp, rw = q.shape
    heads = rw // HEAD_DIM
    nch = lp // CHUNK
    dec, wq, wk, gch = tables

    def body(q_ref, k_ref, v_ref, g_ref, w_ref, dec_ref, wq_ref, wk_ref, gch_ref,
             o_ref, ret_ref, sp_ref, s_sc):
        n = pl.program_id(0)

        @pl.when(n == 0)
        def _():
            s_sc[...] = jnp.zeros_like(s_sc)

        cols = [slice(hh * HEAD_DIM, (hh + 1) * HEAD_DIM) for hh in range(heads)]
        s_ins = [s_sc[hh] for hh in range(heads)]
        outs = []
        for hh, cs in enumerate(cols):
            qv, kv, vv = q_ref[:, cs], k_ref[:, cs], v_ref[:, cs]
            s_in = s_ins[hh]
            a = _dot_nt(qv, kv) * dec_ref[hh]
            qw = (qv.astype(F32) * wq_ref[hh]).astype(BF16)
            kw = (kv.astype(F32) * wk_ref[hh]).astype(BF16)
            o = _dot(a.astype(BF16), vv) + _dot(qw, s_in.astype(BF16))
            s_new = gch_ref[hh, 0:1, :] * s_in + _dot_tn(kw, vv)
            xh, _ = _head_norm(o)
            gv = g_ref[:, cs]
            outs.append((o, s_new, (gv * _sigmoid(gv) * (xh * w_ref[:, cs])).astype(BF16)))
        for hh, cs in enumerate(cols):
            o, s_new, ret = outs[hh]
            sp_ref[hh, 0] = s_ins[hh]
            s_sc[hh] = s_new
            o_ref[:, cs] = o
            ret_ref[:, cs] = ret

    blk = pl.BlockSpec((CHUNK, rw), lambda n: (n, 0))
    tab = pl.BlockSpec((heads, CHUNK, HEAD_DIM), lambda n: (0, 0, 0))
    return pl.pallas_call(
        body, name="retention_fwd", grid=(nch,),
        in_specs=[blk, blk, blk, blk, pl.BlockSpec((1, rw), lambda n: (0, 0)),
                  tab, tab, tab, pl.BlockSpec((heads, 8, HEAD_DIM), lambda n: (0, 0, 0))],
        out_specs=(blk, blk, pl.BlockSpec((heads, 1, HEAD_DIM, HEAD_DIM), lambda n: (0, n, 0, 0))),
        out_shape=(jax.ShapeDtypeStruct((lp, rw), F32),
                   jax.ShapeDtypeStruct((lp, rw), BF16),
                   jax.ShapeDtypeStruct((heads, nch, HEAD_DIM, HEAD_DIM), F32)),
        scratch_shapes=[pltpu.VMEM((heads, HEAD_DIM, HEAD_DIM), F32)],
        compiler_params=_params(("arbitrary",)),
    )(q, k, v, g, rnw, dec, wq, wk, gch)


def _ret_bwd(dret, q, k, v, g, o, sprev, rnw, tables, cosf, sinf):
    lp, rw = q.shape
    heads = rw // HEAD_DIM
    nch = lp // CHUNK
    dec, wq, wk, gch = tables
    scale = HEAD_DIM ** -0.5
    half = HEAD_DIM // 2

    def body(dret_ref, q_ref, k_ref, v_ref, g_ref, o_ref, sp_ref, w_ref, dec_ref, wq_ref, wk_ref, gch_ref,
             cos_ref, sin_ref, dq_ref, dk_ref, dv_ref, dg_ref, dw_ref, ds_sc):
        n = pl.program_id(0)

        @pl.when(n == 0)
        def _():
            ds_sc[...] = jnp.zeros_like(ds_sc)
            dw_ref[...] = jnp.zeros_like(dw_ref)

        cosv = cos_ref[...]
        sinv = sin_ref[...]
        cols = [slice(hh * HEAD_DIM, (hh + 1) * HEAD_DIM) for hh in range(heads)]
        ds_ins = [ds_sc[hh] for hh in range(heads)]
        dw_ins = [dw_ref[:, cs] for cs in cols]
        outs = []
        for hh, cs in enumerate(cols):
            qv, kv, vv = q_ref[:, cs], k_ref[:, cs], v_ref[:, cs]
            gv = g_ref[:, cs]
            dr = dret_ref[:, cs]
            w = w_ref[:, cs]
            sg = _sigmoid(gv)
            sil = gv * sg
            xh, r = _head_norm(o_ref[:, cs])
            dgate = (dr * (xh * w) * (sg * (1.0 + gv * (1.0 - sg)))).astype(BF16)
            dyw = dr * sil
            dw_new = dw_ins[hh] + jnp.sum(dyw * xh, axis=0, keepdims=True)
            dxh = dyw * w
            do = r * (dxh - jnp.mean(dxh, axis=-1, keepdims=True)
                      - xh * jnp.mean(dxh * xh, axis=-1, keepdims=True))
            dob = do.astype(BF16)
            dmask = dec_ref[hh]
            wqv = wq_ref[hh]
            wkv = wk_ref[hh]
            a = (_dot_nt(qv, kv) * dmask).astype(BF16)
            da = (_dot_nt(dob, vv) * dmask).astype(BF16)
            qw = (qv.astype(F32) * wqv).astype(BF16)
            kw = (kv.astype(F32) * wkv).astype(BF16)
            s_in = sp_ref[hh, 0].astype(BF16)
            ds = ds_ins[hh]
            dsb = ds.astype(BF16)
            dq = _dot(da, kv) + _dot_nt(dob, s_in) * wqv
            dk = _dot_tn(da, qv) + _dot_nt(vv, dsb) * wkv
            dv = _dot_tn(a, dob) + _dot(kw, dsb)
            ds_new = gch_ref[hh, 0:1, :] * ds + _dot_tn(qw, dob)
            outs.append((dgate, dw_new, ds_new,
                         (dq * cosv + pltpu.roll(dq * sinv, half, 1)).astype(BF16),
                         ((dk * cosv + pltpu.roll(dk * sinv, half, 1)) * scale).astype(BF16),
                         dv.astype(BF16)))
        for hh, cs in enumerate(cols):
            dgate, dw_new, ds_new, dqv, dkv, dvv = outs[hh]
            dg_ref[:, cs] = dgate
            dw_ref[:, cs] = dw_new
            ds_sc[hh] = ds_new
            dq_ref[:, cs] = dqv
            dk_ref[:, cs] = dkv
            dv_ref[:, cs] = dvv

    blk = pl.BlockSpec((CHUNK, rw), lambda n: (nch - 1 - n, 0))
    tab = pl.BlockSpec((heads, CHUNK, HEAD_DIM), lambda n: (0, 0, 0))
    wsp = pl.BlockSpec((1, rw), lambda n: (0, 0))
    pos = pl.BlockSpec((CHUNK, HEAD_DIM), lambda n: (nch - 1 - n, 0))
    bshape = jax.ShapeDtypeStruct((lp, rw), BF16)
    return pl.pallas_call(
        body, name="retention_bwd", grid=(nch,),
        in_specs=[blk, blk, blk, blk, blk, blk,
                  pl.BlockSpec((heads, 1, HEAD_DIM, HEAD_DIM), lambda n: (0, nch - 1 - n, 0, 0)),
                  wsp, tab, tab, tab, pl.BlockSpec((heads, 8, HEAD_DIM), lambda n: (0, 0, 0)), pos, pos],
        out_specs=(blk, blk, blk, blk, wsp),
        out_shape=(bshape, bshape, bshape, bshape, jax.ShapeDtypeStruct((1, rw), F32)),
        scratch_shapes=[pltpu.VMEM((heads, HEAD_DIM, HEAD_DIM), F32)],
        compiler_params=_params(("arbitrary",)),
    )(dret, q, k, v, g, o, sprev, rnw, dec, wq, wk, gch, cosf, sinf)


SCAN_CW = 512


def _s5_prepare(lam_re, lam_im, log_dt, b_re, b_im):
    dt = jnp.exp(log_dt)[:, None]
    er = jnp.exp(lam_re * dt)
    ar = er * jnp.cos(lam_im * dt)
    ai = er * jnp.sin(lam_im * dt)
    den = lam_re * lam_re + lam_im * lam_im
    fr = ((ar - 1.0) * lam_re + ai * lam_im) / den
    fi = (ai * lam_re - (ar - 1.0) * lam_im) / den
    bbr = fr[..., None] * b_re - fi[..., None] * b_im
    bbi = fr[..., None] * b_im + fi[..., None] * b_re
    return ar, ai, bbr, bbi


def _blockdiag_in(t):
    g, p, n = t.shape
    gs = g // N_SEC
    t = t.reshape(N_SEC, gs, p, n)
    eye = jnp.eye(gs, dtype=t.dtype)
    return jnp.einsum("sgpn,gh->sgphn", t, eye).reshape(N_SEC, gs * p, gs * n)


def _blockdiag_out(m, g, p, n):
    gs = g // N_SEC
    m = m.reshape(N_SEC, gs, p, gs, n)
    eye = jnp.eye(gs, dtype=m.dtype)
    return jnp.einsum("sgphn,gh->sgpn", m, eye).reshape(g, p, n)


def _scan_step(xr_ref, xi_ref, r0, prev, ar_ref, ai_ref, conj, ncols):
    new = []
    for cc in range(ncols // SCAN_CW):
        cs = pl.ds(cc * SCAN_CW, SCAN_CW)
        pr, pi = prev[cc]
        ar = ar_ref[:, cs]
        ai = ai_ref[:, cs]
        if conj:
            nr = ar * pr + ai * pi
            ni = ar * pi - ai * pr
        else:
            nr = ar * pr - ai * pi
            ni = ar * pi + ai * pr
        xr = xr_ref[pl.ds(r0, 8), cs] + nr
        xi = xi_ref[pl.ds(r0, 8), cs] + ni
        xr_ref[pl.ds(r0, 8), cs] = xr
        xi_ref[pl.ds(r0, 8), cs] = xi
        new.append((xr, xi))
    return new


def _scan_chunks(ncols):
    return [pl.ds(cc * SCAN_CW, SCAN_CW) for cc in range(ncols // SCAN_CW)]


def _flat(pairs):
    return tuple(t for p in pairs for t in p)


def _pairs(flat):
    return [(flat[2 * k], flat[2 * k + 1]) for k in range(len(flat) // 2)]


def _shift_rows(z, down):
    row = lax.broadcasted_iota(jnp.int32, z.shape, 0)
    if down:
        return jnp.where(row == 0, 0.0, pltpu.roll(z, 1, 0))
    return jnp.where(row == N_SEG - 1, 0.0, pltpu.roll(z, N_SEG - 1, 0))


def _s5_fwd(u, bsr, bsi, csr, csi, a8r, a8i, al8r, al8i, d, gluw, glub, nw, jb):
    lp, sw = u.shape
    ns = a8r.shape[1]
    rows = N_SEG * jb
    nblk = lp // rows
    secw = sw // N_SEC
    secn = ns // N_SEC

    def local_scan(u_ref, bsr_ref, bsi_ref, ar_ref, ai_ref, xr_ref, xi_ref, pr_sc, pi_sc):
        for s in range(N_SEC):
            ub = u_ref[:, s * secw:(s + 1) * secw].astype(BF16)
            xr_ref[:, s * secn:(s + 1) * secn] = _dot(ub, bsr_ref[s])
            xi_ref[:, s * secn:(s + 1) * secn] = _dot(ub, bsi_ref[s])
        prev = [(pr_sc[:, cs], pi_sc[:, cs]) for cs in _scan_chunks(ns)]
        prev = _scan_step(xr_ref, xi_ref, 0, prev, ar_ref, ai_ref, False, ns)

        def step(j, carry):
            r0 = pl.multiple_of(j * 8, 8)
            return _flat(_scan_step(xr_ref, xi_ref, r0, _pairs(carry), ar_ref, ai_ref, False, ns))

        last = _pairs(lax.fori_loop(1, jb, step, _flat(prev)))
        for cs, (vr, vi) in zip(_scan_chunks(ns), last):
            pr_sc[:, cs] = vr
            pi_sc[:, cs] = vi

    def carry_body(u_ref, bsr_ref, bsi_ref, ar_ref, ai_ref, alr_ref, ali_ref, cr_ref, ci_ref,
                   xr_sc, xi_sc, pr_sc, pi_sc):
        b = pl.program_id(0)

        @pl.when(b == 0)
        def _():
            pr_sc[...] = jnp.zeros_like(pr_sc)
            pi_sc[...] = jnp.zeros_like(pi_sc)

        local_scan(u_ref, bsr_ref, bsi_ref, ar_ref, ai_ref, xr_sc, xi_sc, pr_sc, pi_sc)

        @pl.when(b == nblk - 1)
        def _():
            er = _shift_rows(pr_sc[...], True)
            ei = _shift_rows(pi_sc[...], True)
            alr, ali = alr_ref[...], ali_ref[...]
            cr, ci = er, ei
            for _ in range(N_SEG - 2):
                sr = _shift_rows(cr, True)
                si = _shift_rows(ci, True)
                cr = er + alr * sr - ali * si
                ci = ei + alr * si + ali * sr
            cr_ref[...] = cr
            ci_ref[...] = ci

    ublk = pl.BlockSpec((rows, sw), lambda b: (b, 0))
    bspec = pl.BlockSpec((N_SEC, secw, secn), lambda b: (0, 0, 0))
    cspec = pl.BlockSpec((N_SEC, secn, secw), lambda b: (0, 0, 0))
    s8 = pl.BlockSpec((N_SEG, ns), lambda b: (0, 0))
    vec = pl.BlockSpec((1, sw), lambda b: (0, 0))
    s8shape = jax.ShapeDtypeStruct((N_SEG, ns), F32)
    c0r, c0i = pl.pallas_call(
        carry_body, name="s5_fwd_carry", grid=(nblk,),
        in_specs=[ublk, bspec, bspec, s8, s8, s8, s8],
        out_specs=(s8, s8), out_shape=(s8shape, s8shape),
        scratch_shapes=[pltpu.VMEM((rows, ns), F32), pltpu.VMEM((rows, ns), F32),
                        pltpu.VMEM((N_SEG, ns), F32), pltpu.VMEM((N_SEG, ns), F32)],
        compiler_params=_params(("arbitrary",)),
    )(u, bsr, bsi, a8r, a8i, al8r, al8i)

    def main_body(u_ref, bsr_ref, bsi_ref, csr_ref, csi_ref, ar_ref, ai_ref, c0r_ref, c0i_ref,
                  d_ref, gw_ref, gb_ref, nw_ref, xr_ref, xi_ref, yp_ref, out_ref, pr_sc, pi_sc):
        b = pl.program_id(0)

        @pl.when(b == 0)
        def _():
            pr_sc[...] = c0r_ref[...]
            pi_sc[...] = c0i_ref[...]

        local_scan(u_ref, bsr_ref, bsi_ref, ar_ref, ai_ref, xr_ref, xi_ref, pr_sc, pi_sc)
        for s in range(N_SEC):
            xs = pl.ds(s * secn, secn)
            us = pl.ds(s * secw, secw)
            y = _dot(xr_ref[:, xs].astype(BF16), csr_ref[s]) + _dot(xi_ref[:, xs].astype(BF16), csi_ref[s])
            yp_ref[:, us] = y + d_ref[:, us] * u_ref[:, us]
        yp = yp_ref[...]
        t = jnp.tanh(GELU_K0 * (yp + GELU_K1 * yp * yp * yp))
        y1 = 0.5 * yp * (1.0 + t)
        z = _dot(y1.astype(BF16), gw_ref[...]) + gb_ref[...]
        y2 = y1 * _sigmoid(z)
        xh, _ = _rms_stats(y2)
        out_ref[...] = (xh * nw_ref[...]).astype(BF16)

    xblk = pl.BlockSpec((rows, ns), lambda b: (b, 0))
    xr, xi, yp, out = pl.pallas_call(
        main_body, name="s5_fwd", grid=(nblk,),
        in_specs=[ublk, bspec, bspec, cspec, cspec, s8, s8, s8, s8, vec,
                  pl.BlockSpec((sw, sw), lambda b: (0, 0)), vec, vec],
        out_specs=(xblk, xblk, ublk, ublk),
        out_shape=(jax.ShapeDtypeStruct((lp, ns), F32), jax.ShapeDtypeStruct((lp, ns), F32),
                   jax.ShapeDtypeStruct((lp, sw), F32), jax.ShapeDtypeStruct((lp, sw), BF16)),
        scratch_shapes=[pltpu.VMEM((N_SEG, ns), F32), pltpu.VMEM((N_SEG, ns), F32)],
        compiler_params=_params(("arbitrary",)),
    )(u, bsr, bsi, csr, csi, a8r, a8i, c0r, c0i, d, gluw, glub, nw)
    return xr, xi, c0r, c0i, yp, out


def _s5_bwd(dout, u, yp, xr, xi, c0r, c0i, bsrt, bsit, csrt, csit, a8r, a8i, al8r, al8i, d, gluw, glub, nw, jb):
    lp, sw = u.shape
    ns = a8r.shape[1]
    rows = N_SEG * jb
    nblk = lp // rows
    secw = sw // N_SEC
    secn = ns // N_SEC

    def rowwise_bwd(dout_ref, yp_ref, gw_ref, gb_ref, nw_ref):
        ypv = yp_ref[...]
        t = jnp.tanh(GELU_K0 * (ypv + GELU_K1 * ypv * ypv * ypv))
        y1 = 0.5 * ypv * (1.0 + t)
        dgelu = 0.5 * (1.0 + t) + 0.5 * ypv * (1.0 - t * t) * GELU_K0 * (1.0 + 3.0 * GELU_K1 * ypv * ypv)
        gw = gw_ref[...]
        y1b = y1.astype(BF16)
        sg = _sigmoid(_dot(y1b, gw) + gb_ref[...])
        xh, r = _rms_stats(y1 * sg)
        dov = dout_ref[...]
        dy2 = _rms_bwd(dov, xh, r, nw_ref[...])
        dz = dy2 * y1 * sg * (1.0 - sg)
        dzb = dz.astype(BF16)
        dy1 = dy2 * sg + _dot_nt(dzb, gw)
        return dy1 * dgelu, dov * xh, y1b, dzb, dz

    def lam_scan(dyp_of, csrt_ref, csit_ref, ar_ref, ai_ref, lr_sc, li_sc, nr_sc, ni_sc, extra):
        for s in range(N_SEC):
            db = dyp_of(s)
            lr_sc[:, s * secn:(s + 1) * secn] = _dot(db, csrt_ref[s])
            li_sc[:, s * secn:(s + 1) * secn] = _dot(db, csit_ref[s])
        top = rows - 8
        prev = [(nr_sc[:, cs], ni_sc[:, cs]) for cs in _scan_chunks(ns)]
        prev = _scan_step(lr_sc, li_sc, top, prev, ar_ref, ai_ref, True, ns)
        extra(top, pl.ds(top - 8, 8))

        def step(jj, carry):
            r0 = pl.multiple_of((jb - 1 - jj) * 8, 8)
            rp = pl.multiple_of((jb - 2 - jj) * 8, 8)
            new = _scan_step(lr_sc, li_sc, r0, _pairs(carry), ar_ref, ai_ref, True, ns)
            extra(r0, pl.ds(rp, 8))
            return _flat(new)

        prev = _pairs(lax.fori_loop(1, jb - 1, step, _flat(prev)))
        last = _scan_step(lr_sc, li_sc, 0, prev, ar_ref, ai_ref, True, ns)
        extra(0, None)
        for cs, (vr, vi) in zip(_scan_chunks(ns), last):
            nr_sc[:, cs] = vr
            ni_sc[:, cs] = vi

    def carry_body(dout_ref, yp_ref, u_ref, gw_ref, gb_ref, nw_ref, csrt_ref, csit_ref, ar_ref, ai_ref,
                   alr_ref, ali_ref, cr_ref, ci_ref, dyp_ref, dnw_ref, dgw_ref, dgb_ref, dd_ref,
                   lr_sc, li_sc, nr_sc, ni_sc):
        b = pl.program_id(0)

        @pl.when(b == 0)
        def _():
            nr_sc[...] = jnp.zeros_like(nr_sc)
            ni_sc[...] = jnp.zeros_like(ni_sc)
            for ref in (dnw_ref, dgw_ref, dgb_ref, dd_ref):
                ref[...] = jnp.zeros_like(ref)

        dyp, dnw_rows, y1b, dzb, dz = rowwise_bwd(dout_ref, yp_ref, gw_ref, gb_ref, nw_ref)
        dnw_ref[...] += jnp.sum(dnw_rows, axis=0, keepdims=True)
        dgw_ref[...] += _dot_tn(y1b, dzb)
        dgb_ref[...] += jnp.sum(dz, axis=0, keepdims=True)
        dd_ref[...] += jnp.sum(dyp * u_ref[...], axis=0, keepdims=True)
        dyp_ref[...] = dyp.astype(BF16)
        lam_scan(lambda s: dyp_ref[:, s * secw:(s + 1) * secw], csrt_ref, csit_ref, ar_ref, ai_ref,
                 lr_sc, li_sc, nr_sc, ni_sc, lambda r0, prev_rows: None)

        @pl.when(b == nblk - 1)
        def _():
            fr = _shift_rows(nr_sc[...], False)
            fi = _shift_rows(ni_sc[...], False)
            alr, ali = alr_ref[...], ali_ref[...]
            cr, ci = fr, fi
            for _ in range(N_SEG - 2):
                sr = _shift_rows(cr, False)
                si = _shift_rows(ci, False)
                cr = fr + alr * sr + ali * si
                ci = fi + alr * si - ali * sr
            cr_ref[...] = cr
            ci_ref[...] = ci

    rev = lambda b: (nblk - 1 - b, 0)
    ublk = pl.BlockSpec((rows, sw), rev)
    xblk = pl.BlockSpec((rows, ns), rev)
    s8 = pl.BlockSpec((N_SEG, ns), lambda b: (0, 0))
    vec = pl.BlockSpec((1, sw), lambda b: (0, 0))
    gws = pl.BlockSpec((sw, sw), lambda b: (0, 0))
    btspec = pl.BlockSpec((N_SEC, secn, secw), lambda b: (0, 0, 0))
    ctspec = pl.BlockSpec((N_SEC, secw, secn), lambda b: (0, 0, 0))
    s8shape = jax.ShapeDtypeStruct((N_SEG, ns), F32)
    lcr, lci, dyp_all, d_nw, d_gw, d_gb, d_d = pl.pallas_call(
        carry_body, name="s5_bwd_carry", grid=(nblk,),
        in_specs=[ublk, ublk, ublk, gws, vec, vec, ctspec, ctspec, s8, s8, s8, s8],
        out_specs=(s8, s8, ublk, vec, gws, vec, vec),
        out_shape=(s8shape, s8shape, jax.ShapeDtypeStruct((lp, sw), BF16), jax.ShapeDtypeStruct((1, sw), F32),
                   jax.ShapeDtypeStruct((sw, sw), F32), jax.ShapeDtypeStruct((1, sw), F32),
                   jax.ShapeDtypeStruct((1, sw), F32)),
        scratch_shapes=[pltpu.VMEM((rows, ns), F32), pltpu.VMEM((rows, ns), F32),
                        pltpu.VMEM((N_SEG, ns), F32), pltpu.VMEM((N_SEG, ns), F32)],
        compiler_params=_params(("arbitrary",)),
    )(dout, yp, u, gluw, glub, nw, csrt, csit, a8r, a8i, al8r, al8i)

    def main_body(dyp_sc, u_ref, xr_ref, xi_ref, xtr_ref, xti_ref, c0r_ref, c0i_ref, lcr_ref, lci_ref,
                  d_ref, bsrt_ref, bsit_ref, csrt_ref, csit_ref, ar_ref, ai_ref,
                  du_ref, dcr_ref, dci_ref, dbr_ref, dbi_ref, dar_ref, dai_ref,
                  lr_sc, li_sc, nr_sc, ni_sc):
        b = pl.program_id(0)

        @pl.when(b == 0)
        def _():
            nr_sc[...] = lcr_ref[...]
            ni_sc[...] = lci_ref[...]
            for ref in (dcr_ref, dci_ref, dbr_ref, dbi_ref, dar_ref, dai_ref):
                ref[...] = jnp.zeros_like(ref)

        for s in range(N_SEC):
            db = dyp_sc[:, s * secw:(s + 1) * secw]
            xs = pl.ds(s * secn, secn)
            dcr_ref[s] += _dot_tn(xr_ref[:, xs].astype(BF16), db)
            dci_ref[s] += _dot_tn(xi_ref[:, xs].astype(BF16), db)

        first = b == nblk - 1

        def acc_da(r0, prev_rows):
            for cc in range(ns // SCAN_CW):
                cs = pl.ds(cc * SCAN_CW, SCAN_CW)
                lr = lr_sc[pl.ds(r0, 8), cs]
                li = li_sc[pl.ds(r0, 8), cs]
                if prev_rows is None:
                    xpr = jnp.where(first, c0r_ref[:, cs], xtr_ref[:, cs])
                    xpi = jnp.where(first, c0i_ref[:, cs], xti_ref[:, cs])
                else:
                    xpr = xr_ref[prev_rows, cs]
                    xpi = xi_ref[prev_rows, cs]
                dar_ref[:, cs] += lr * xpr + li * xpi
                dai_ref[:, cs] += li * xpr - lr * xpi

        lam_scan(lambda s: dyp_sc[:, s * secw:(s + 1) * secw], csrt_ref, csit_ref, ar_ref, ai_ref,
                 lr_sc, li_sc, nr_sc, ni_sc, acc_da)

        for s in range(N_SEC):
            xs = pl.ds(s * secn, secn)
            us = pl.ds(s * secw, secw)
            lrb = lr_sc[:, xs].astype(BF16)
            lib = li_sc[:, xs].astype(BF16)
            du = _dot(lrb, bsrt_ref[s]) + _dot(lib, bsit_ref[s]) + d_ref[:, us] * dyp_sc[:, us].astype(F32)
            du_ref[:, us] = du.astype(BF16)
            ub = u_ref[:, us].astype(BF16)
            dbr_ref[s] += _dot_tn(ub, lrb)
            dbi_ref[s] += _dot_tn(ub, lib)

    tail = pl.BlockSpec((N_SEG, ns), lambda b: (jnp.maximum((nblk - 1 - b) * jb - 1, 0), 0))
    acc_c = pl.BlockSpec((N_SEC, secn, secw), lambda b: (0, 0, 0))
    acc_b = pl.BlockSpec((N_SEC, secw, secn), lambda b: (0, 0, 0))
    du, dcr, dci, dbr, dbi, dar, dai = pl.pallas_call(
        main_body, name="s5_bwd", grid=(nblk,),
        in_specs=[ublk, ublk, xblk, xblk, tail, tail, s8, s8, s8, s8,
                  vec, btspec, btspec, ctspec, ctspec, s8, s8],
        out_specs=(ublk, acc_c, acc_c, acc_b, acc_b, s8, s8),
        out_shape=(jax.ShapeDtypeStruct((lp, sw), BF16),
                   jax.ShapeDtypeStruct((N_SEC, secn, secw), F32),
                   jax.ShapeDtypeStruct((N_SEC, secn, secw), F32),
                   jax.ShapeDtypeStruct((N_SEC, secw, secn), F32),
                   jax.ShapeDtypeStruct((N_SEC, secw, secn), F32),
                   s8shape, s8shape),
        scratch_shapes=[pltpu.VMEM((rows, ns), F32), pltpu.VMEM((rows, ns), F32),
                        pltpu.VMEM((N_SEG, ns), F32), pltpu.VMEM((N_SEG, ns), F32)],
        compiler_params=_params(("arbitrary",)),
    )(dyp_all, u, xr, xi, xr, xi, c0r, c0i, lcr, lci, d, bsrt, bsit, csrt, csit, a8r, a8i)
    return du, d_nw, d_gw, d_gb, d_d, dcr, dci, dbr, dbi, dar, dai


def _outproj_fwd(h, ret, ssm, wo):
    lp, d = h.shape
    nck, rs, _ = wo.shape
    rw = ret.shape[1]
    tm = _tile(lp, 640)
    per = rw // rs

    def body(h_ref, ret_ref, ssm_ref, w_ref, o_ref):
        acc = h_ref[...]
        for c in range(nck):
            src = ret_ref if c < per else ssm_ref
            lo = (c % per) * rs
            acc = acc + _dot(src[:, lo:lo + rs], w_ref[c])
        o_ref[...] = acc

    row = lambda w: pl.BlockSpec((tm, w), lambda i: (i, 0))
    return pl.pallas_call(
        body, name="outproj_fwd", grid=(lp // tm,),
        in_specs=[row(d), row(rw), row(ssm.shape[1]), pl.BlockSpec((nck, rs, d), lambda i: (0, 0, 0))],
        out_specs=row(d), out_shape=jax.ShapeDtypeStruct((lp, d), F32),
        compiler_params=_params(("arbitrary",)),
    )(h, ret, ssm, wo)


def _outproj_bwd(dh, ret, ssm, wo):
    lp, d = dh.shape
    nck, rs, _ = wo.shape
    rw = ret.shape[1]
    sw = ssm.shape[1]
    tm = _tile(lp, 640)
    per = rw // rs
    last = lp // tm - 1

    def body(dh_ref, ret_ref, ssm_ref, w_ref, dret_ref, dssm_ref, dw_ref, acc_sc):
        i = pl.program_id(0)

        @pl.when(i == 0)
        def _():
            acc_sc[...] = jnp.zeros_like(acc_sc)

        dhb = dh_ref[...].astype(BF16)
        for c in range(nck):
            src, dst = (ret_ref, dret_ref) if c < per else (ssm_ref, dssm_ref)
            lo = (c % per) * rs
            dst[:, lo:lo + rs] = _dot_nt(dhb, w_ref[c])
            acc_sc[c] += _dot_tn(src[:, lo:lo + rs], dhb)

        @pl.when(i == last)
        def _():
            dw_ref[...] = acc_sc[...].astype(BF16)

    row = lambda w: pl.BlockSpec((tm, w), lambda i: (i, 0))
    wsp = pl.BlockSpec((nck, rs, d), lambda i: (0, 0, 0))
    return pl.pallas_call(
        body, name="outproj_bwd", grid=(lp // tm,),
        in_specs=[row(d), row(rw), row(sw), wsp],
        out_specs=(row(rw), row(sw), wsp),
        out_shape=(jax.ShapeDtypeStruct((lp, rw), F32), jax.ShapeDtypeStruct((lp, sw), F32),
                   jax.ShapeDtypeStruct((nck, rs, d), BF16)),
        scratch_shapes=[pltpu.VMEM((nck, rs, d), F32)],
        compiler_params=_params(("arbitrary",)),
    )(dh, ret, ssm, wo)


def _loss_head(h, fw, target):
    lp, d = h.shape
    tm = _tile(lp, 640, CHUNK)
    sub = tm // CHUNK

    def body(h_ref, w_ref, *rest):
        t_refs = rest[:sub]
        loss_ref, dh_ref, dw_ref = rest[sub:]
        i = pl.program_id(0)

        @pl.when(i == 0)
        def _():
            loss_ref[...] = jnp.zeros_like(loss_ref)
            dw_ref[...] = jnp.zeros_like(dw_ref)

        w = w_ref[...]
        for j in range(sub):
            rows = pl.ds(j * CHUNK, CHUNK)
            xh, r = _rms_stats(h_ref[rows, :])
            err = xh * w - t_refs[j][...]
            if j == 0:
                err = jnp.where(i == 0, 0.0, err)
            loss_ref[...] += 0.5 * jnp.sum(err * err) / d
            dout = err * (1.0 / d)
            dw_ref[...] += jnp.sum(dout * xh, axis=0, keepdims=True)
            dh_ref[rows, :] = _rms_bwd(dout, xh, r, w)

    t_spec = lambda j: pl.BlockSpec((CHUNK, d), lambda i: (jnp.maximum(i * sub + j - 1, 0), 0))
    return pl.pallas_call(
        body, name="loss_head", grid=(lp // tm,),
        in_specs=[pl.BlockSpec((tm, d), lambda i: (i, 0)), pl.BlockSpec((1, d), lambda i: (0, 0))]
        + [t_spec(j) for j in range(sub)],
        out_specs=(pl.BlockSpec((8, LANE), lambda i: (0, 0)), pl.BlockSpec((tm, d), lambda i: (i, 0)),
                   pl.BlockSpec((1, d), lambda i: (0, 0))),
        out_shape=(jax.ShapeDtypeStruct((8, LANE), F32), jax.ShapeDtypeStruct((lp, d), F32),
                   jax.ShapeDtypeStruct((1, d), F32)),
        compiler_params=_params(("arbitrary",)),
    )(h, fw, *([target] * sub))


def _pack(arrs):
    flat = jnp.concatenate([a.reshape(-1).astype(F32) for a in arrs])
    n = flat.shape[0]
    rows = -(-n // (8 * LANE)) * 8
    return jnp.pad(flat, (0, rows * LANE - n)).reshape(rows, LANE)


def _unpack(packed, shapes):
    flat = packed.reshape(-1)
    out, off = [], 0
    for s in shapes:
        n = math.prod(s)
        out.append(flat[off:off + n].reshape(s))
        off += n
    return out


def _to_segments(a, seg_len):
    return a.reshape(N_SEG, seg_len, a.shape[1]).transpose(1, 0, 2).reshape(a.shape)


def _from_segments(a, seg_len):
    return a.reshape(seg_len, N_SEG, a.shape[1]).transpose(1, 0, 2).reshape(a.shape)


WEIGHT_NAMES = ['meta_tokens', 'ffn1_norm_w', 'ffn1_w_gate', 'ffn1_w_up', 'ffn1_w_down', 'mix_norm_w', 'w_in',
                'ret_norm_w', 'ssm_lambda_re', 'ssm_lambda_im', 'ssm_log_dt', 'ssm_b_re', 'ssm_b_im', 'ssm_c_re',
                'ssm_c_im', 'ssm_d', 'ssm_glu_w', 'ssm_glu_b', 'ssm_norm_w', 'w_out', 'ffn2_norm_w', 'ffn2_w_gate',
                'ffn2_w_up', 'ffn2_w_down', 'final_norm_w']
BIG = ['ffn1_w_gate', 'ffn1_w_up', 'ffn1_w_down', 'w_in', 'ssm_glu_w', 'w_out', 'ffn2_w_gate', 'ffn2_w_up',
       'ffn2_w_down']
TRANSPOSED = ['ffn1_w_gate', 'ffn1_w_up', 'ffn2_w_gate', 'ffn2_w_up']
BIG_EARLY = ['ffn1_w_gate', 'ffn1_w_up', 'ffn1_w_down']
BIG_LATE = [n for n in BIG if n not in BIG_EARLY]
SMALL = [n for n in WEIGHT_NAMES if n not in BIG]


def kernel(x, meta_tokens, ffn1_norm_w, ffn1_w_gate, ffn1_w_up, ffn1_w_down, mix_norm_w, w_in, ret_norm_w, ssm_lambda_re, ssm_lambda_im, ssm_log_dt, ssm_b_re, ssm_b_im, ssm_c_re, ssm_c_im, ssm_d, ssm_glu_w, ssm_glu_b, ssm_norm_w, w_out, ffn2_norm_w, ffn2_w_gate, ffn2_w_up, ffn2_w_down, final_norm_w, loss_target, m_meta_tokens, m_ffn1_norm_w, m_ffn1_w_gate, m_ffn1_w_up, m_ffn1_w_down, m_mix_norm_w, m_w_in, m_ret_norm_w, m_ssm_lambda_re, m_ssm_lambda_im, m_ssm_log_dt, m_ssm_b_re, m_ssm_b_im, m_ssm_c_re, m_ssm_c_im, m_ssm_d, m_ssm_glu_w, m_ssm_glu_b, m_ssm_norm_w, m_w_out, m_ffn2_norm_w, m_ffn2_w_gate, m_ffn2_w_up, m_ffn2_w_down, m_final_norm_w, v_meta_tokens, v_ffn1_norm_w, v_ffn1_w_gate, v_ffn1_w_up, v_ffn1_w_down, v_mix_norm_w, v_w_in, v_ret_norm_w, v_ssm_lambda_re, v_ssm_lambda_im, v_ssm_log_dt, v_ssm_b_re, v_ssm_b_im, v_ssm_c_re, v_ssm_c_im, v_ssm_d, v_ssm_glu_w, v_ssm_glu_b, v_ssm_norm_w, v_w_out, v_ffn2_norm_w, v_ffn2_w_gate, v_ffn2_w_up, v_ffn2_w_down, v_final_norm_w):
    args = locals()
    w = {n: args[n] for n in WEIGHT_NAMES}
    m = {n: args["m_" + n] for n in WEIGHT_NAMES}
    v = {n: args["v_" + n] for n in WEIGHT_NAMES}

    seq, d = x.shape[1], x.shape[2]
    lp = seq + CHUNK
    seg_len = lp // N_SEG
    rw = RET_HEADS * HEAD_DIM
    sw = ssm_d.shape[-1]
    groups = sw // SSM_GROUP
    ns = groups * SSM_STATE
    jb = _tile(seg_len, 40, 8)
    chip = 2 * lax.axis_index("x") + lax.axis_index("y")

    as_fd = lambda t: jnp.swapaxes(t, -1, -2)
    shards = {n: (as_fd(w[n][0]) if n in TRANSPOSED else w[n][0]).astype(BF16) for n in BIG}
    early = [shards[n] for n in BIG_EARLY] + [meta_tokens]
    gathered = _forward_sibling("gather_early_forward",
                                _exchange("gather_early", _allgather_chips_plan(early), early))
    gw = dict(zip(BIG_EARLY, gathered[:-1]))
    meta_full = jnp.transpose(gathered[-1], (1, 0, 2)).reshape(N_META, d)
    late = [shards[n] for n in BIG_LATE]

    pos = jnp.arange(lp, dtype=F32) - float(CHUNK - N_META)
    freqs = 1.0 / (ROPE_BASE ** (jnp.arange(0, HEAD_DIM, 2, dtype=F32) / HEAD_DIM))
    ang = pos[:, None] * freqs[None, :]
    cosf = jnp.concatenate([jnp.cos(ang), jnp.cos(ang)], axis=1)
    sinf = jnp.concatenate([-jnp.sin(ang), jnp.sin(ang)], axis=1)
    tables = _retention_tables()

    lam_re, lam_im, log_dt = ssm_lambda_re[0], ssm_lambda_im[0], ssm_log_dt[0]
    b_re, b_im, c_re, c_im = ssm_b_re[0], ssm_b_im[0], ssm_c_re[0], ssm_c_im[0]
    (ar, ai, bbr, bbi), prep_vjp = jax.vjp(_s5_prepare, lam_re, lam_im, log_dt, b_re, b_im)
    dt = jnp.exp(log_dt)[:, None]
    el = jnp.exp(seg_len * lam_re * dt)
    alr = el * jnp.cos(seg_len * lam_im * dt)
    ali = el * jnp.sin(seg_len * lam_im * dt)
    bc8 = lambda t: jnp.broadcast_to(t.reshape(1, ns), (N_SEG, ns))
    a8r, a8i, al8r, al8i = bc8(ar), bc8(ai), bc8(alr), bc8(ali)
    bsr = _blockdiag_in(jnp.transpose(bbr, (0, 2, 1)))
    bsi = _blockdiag_in(jnp.transpose(bbi, (0, 2, 1)))
    csrt = _blockdiag_in(c_re)
    csit = _blockdiag_in(-c_im)
    tr = lambda t: jnp.transpose(t, (0, 2, 1))
    bsr_b, bsi_b = bsr.astype(BF16), bsi.astype(BF16)
    csr_b, csi_b = tr(csrt).astype(BF16), tr(csit).astype(BF16)
    bsrt_b, bsit_b = tr(bsr).astype(BF16), tr(bsi).astype(BF16)
    csrt_b, csit_b = csrt.astype(BF16), csit.astype(BF16)

    h0 = (jnp.concatenate([jnp.zeros((CHUNK - N_META, d), F32), meta_full], axis=0), x[0])
    (h1, g1, u1), late_half = _ffn_fwd("ffn1_fwd", h0, ffn1_norm_w, gw['ffn1_w_gate'], gw['ffn1_w_up'],
                                       gw['ffn1_w_down'], _allgather_chips_plan(late), late)
    gw.update(zip(BIG_LATE, _forward_sibling("gather_late_forward", late_half)))
    glu_full = gw['ssm_glu_w'].reshape(sw, sw)
    n2, q, k, vv, gate, u = _inproj_fwd(h1, mix_norm_w, gw['w_in'], cosf, sinf, rw)
    o, ret, sprev = _ret_fwd(q, k, vv, gate, ret_norm_w, tables)
    u_seg = _to_segments(u, seg_len)
    xr, xi, c0r, c0i, yp, ssm_seg = _s5_fwd(u_seg, bsr_b, bsi_b, csr_b, csi_b, a8r, a8i, al8r, al8i,
                                            ssm_d, glu_full, ssm_glu_b, ssm_norm_w, jb)
    ssm = _from_segments(ssm_seg, seg_len)
    h2 = _outproj_fwd(h1, ret, ssm, gw['w_out'])
    (h3, g2, u2), _ = _ffn_fwd("ffn2_fwd", h2, ffn2_norm_w, gw['ffn2_w_gate'], gw['ffn2_w_up'], gw['ffn2_w_down'])
    loss_part, dh3, d_final = _loss_head(h3, final_norm_w.reshape(1, d), loss_target[0])

    (dh2, d_ffn2_norm, nb, daccb, ab, dgb, dub), _ = _ffn_bwd_act(
        "ffn2_bwd_act", dh3, h2, ffn2_norm_w, g2, u2, gw['ffn2_w_gate'], gw['ffn2_w_up'], gw['ffn2_w_down'])
    (dwg2, dwu2, dwd2), _ = _ffn_bwd_w("ffn2_bwd_w", nb, daccb, ab, dgb, dub)
    dret, dssm, dwo = _outproj_bwd(dh2, ret, ssm, gw['w_out'])
    (du_seg, d_ssm_norm, d_glu_w, d_glu_b, d_ssm_d, dcr_s, dci_s, dbr_s, dbi_s, dar8, dai8) = _s5_bwd(
        _to_segments(dssm, seg_len), u_seg, yp, xr, xi, c0r, c0i, bsrt_b, bsit_b, csrt_b, csit_b,
        a8r, a8i, al8r, al8i, ssm_d, glu_full, ssm_glu_b, ssm_norm_w, jb)
    du = _from_segments(du_seg, seg_len)
    dq, dk, dv, dgate, d_ret_norm = _ret_bwd(dret, q, k, vv, gate, o, sprev, ret_norm_w, tables, cosf, sinf)
    dh1, d_mix_norm, dwin = _inproj_bwd(dh2, h1, mix_norm_w, n2, gw['w_in'], dq, dk, dv, dgate, du)
    late_parts = {
        'w_in': dwin, 'ssm_glu_w': d_glu_w.reshape(N_CHIP, sw // N_CHIP, sw).astype(BF16), 'w_out': dwo,
        'ffn2_w_gate': dwg2, 'ffn2_w_up': dwu2, 'ffn2_w_down': dwd2,
    }
    late_list = [late_parts[n] for n in BIG_LATE]
    (dh0, d_ffn1_norm, nb, daccb, ab, dgb, dub), late_recv = _ffn_bwd_act(
        "ffn1_bwd_act", dh1, h0, ffn1_norm_w, g1, u1, gw['ffn1_w_gate'], gw['ffn1_w_up'], gw['ffn1_w_down'],
        _alltoall_chips_plan(late_list), late_list)
    grad_x = dh0[CHUNK:][None]
    d_meta = dh0[CHUNK - N_META:CHUNK]

    d_c_re = jnp.transpose(_blockdiag_out(tr(dcr_s), groups, SSM_GROUP, SSM_STATE), (0, 1, 2))
    d_c_im = -_blockdiag_out(tr(dci_s), groups, SSM_GROUP, SSM_STATE)
    d_bbr = jnp.transpose(_blockdiag_out(dbr_s, groups, SSM_GROUP, SSM_STATE), (0, 2, 1))
    d_bbi = jnp.transpose(_blockdiag_out(dbi_s, groups, SSM_GROUP, SSM_STATE), (0, 2, 1))
    d_ar = jnp.sum(dar8, axis=0).reshape(groups, SSM_STATE)
    d_ai = jnp.sum(dai8, axis=0).reshape(groups, SSM_STATE)
    small_parts = [loss_part[0:1, :], d_meta, d_ffn1_norm, d_mix_norm, d_ret_norm, d_ar, d_ai, d_bbr, d_bbi,
                   d_c_re, d_c_im, d_ssm_d, d_glu_b, d_ssm_norm, d_ffn2_norm, d_final]
    small_shapes = [a.shape for a in small_parts]
    packed = _pack(small_parts)
    early_recv, (all_parts,) = _ffn_bwd_w_scatter("ffn1_bwd_w", nb, daccb, ab, dgb, dub, chip,
                                                  _allgather_all_plan([packed]), [packed])
    received = dict(zip(BIG_LATE + BIG_EARLY, late_recv + early_recv))
    ffn_names = [n for n in BIG if n.startswith('ffn')]
    chip_sum = dict(zip(ffn_names, _sum_slots("sum_chips_ffn", [received[n] for n in ffn_names], BF16)))
    for n in BIG:
        if n not in chip_sum:
            chip_sum[n] = _sum_slots("sum_chips_" + n, [received[n]], BF16)[0]
    chip_sums = [chip_sum[n] for n in BIG]
    sib_sums = _swap_sibling("swap_sibling", chip_sums)
    (loss_row, g_meta_full, g_ffn1_norm, g_mix_norm, g_ret_norm, g_ar, g_ai, g_bbr, g_bbi, g_c_re, g_c_im,
     g_ssm_d, g_glu_b, g_ssm_norm, g_ffn2_norm, g_final) = _unpack(_sum_slots("sum_small", [all_parts], F32)[0],
                                                                  small_shapes)
    g_lam_re, g_lam_im, g_log_dt, g_b_re, g_b_im = prep_vjp((g_ar, g_ai, g_bbr, g_bbi))
    loss = loss_row[0, 0]
    g_meta = lax.dynamic_slice(g_meta_full, (0, chip * (d // N_CHIP)), (N_META, d // N_CHIP))
    small_grads = {
        'meta_tokens': g_meta, 'ffn1_norm_w': g_ffn1_norm, 'mix_norm_w': g_mix_norm, 'ret_norm_w': g_ret_norm,
        'ssm_lambda_re': g_lam_re[None], 'ssm_lambda_im': g_lam_im[None], 'ssm_log_dt': g_log_dt[None],
        'ssm_b_re': g_b_re[None], 'ssm_b_im': g_b_im[None], 'ssm_c_re': g_c_re[None], 'ssm_c_im': g_c_im[None],
        'ssm_d': g_ssm_d, 'ssm_glu_b': g_glu_b, 'ssm_norm_w': g_ssm_norm, 'ffn2_norm_w': g_ffn2_norm,
        'final_norm_w': g_final.reshape(d),
    }

    grads, deltas, new_m, new_v = {}, {}, {}, {}
    g_pair = {n: [mine, sib] for n, mine, sib in zip(BIG, chip_sums, sib_sums)}
    view = lambda n, t: as_fd(t) if n in TRANSPOSED else t
    ffn_out = _adam("adam_ffn", [(view(n, w[n]), view(n, m[n]), view(n, v[n])) for n in ffn_names],
                    [g_pair[n] for n in ffn_names])
    for n, outs in zip(ffn_names, ffn_out):
        grads[n], deltas[n], new_m[n], new_v[n] = [view(n, t) for t in outs]
    for n in BIG:
        if n not in ffn_names:
            grads[n], deltas[n], new_m[n], new_v[n] = _adam("adam_" + n, [(w[n], m[n], v[n])], [g_pair[n]])[0]
    sm_shapes = [w[n].shape for n in SMALL]
    sm_out = _adam("adam_small", [(_pack([w[n] for n in SMALL]), _pack([m[n] for n in SMALL]),
                                  _pack([v[n] for n in SMALL]))],
                   [[_pack([small_grads[n].reshape(w[n].shape) for n in SMALL])]])[0]
    for dst, packed in zip((grads, deltas, new_m, new_v), sm_out):
        for n, t in zip(SMALL, _unpack(packed, sm_shapes)):
            dst[n] = t

    return (loss, grad_x, *[grads[n] for n in WEIGHT_NAMES], *[deltas[n] for n in WEIGHT_NAMES],
            *[new_m[n] for n in WEIGHT_NAMES], *[new_v[n] for n in WEIGHT_NAMES])
```

```python
import functools
import math

import jax
import jax.numpy as jnp
from jax import lax
from jax.experimental import pallas as pl
from jax.experimental.pallas import tpu as pltpu

N_META = 16
RET_HEADS = 4
HEAD_DIM = 128
SSM_GROUP = 16
SSM_STATE = 64
CHUNK = 128
ROPE_BASE = 10000.0
EPS = 1e-6
FFN_RES = 0.5
N_SEG = 8
N_SEC = 4
N_CHIP = 4
LANE = 128
FFN_CPS = 2
BWD_W_ROWS = 1664

ADAM_LR = 0.001
ADAM_B1 = 0.9
ADAM_B2 = 0.999
ADAM_EPS = 1e-08
ADAM_WD = 0.01
ADAM_STEP = 10

VMEM_LIMIT = 56 * 1024 * 1024

F32 = jnp.float32
BF16 = jnp.bfloat16
MESH = pl.DeviceIdType.MESH


def _dot(a, b):
    return jnp.dot(a, b, preferred_element_type=F32)


def _dot_nt(a, b):
    return lax.dot_general(a, b, (((1,), (1,)), ((), ())), preferred_element_type=F32)


def _dot_tn(a, b):
    return lax.dot_general(a, b, (((0,), (0,)), ((), ())), preferred_element_type=F32)


def _tile(n, target, mult=64):
    best = None
    t = mult
    while t <= min(n, target):
        if n % t == 0:
            best = t
        t += mult
    assert best is not None, (n, target)
    return best


def _params(sem, vmem=VMEM_LIMIT):
    return pltpu.CompilerParams(dimension_semantics=sem, vmem_limit_bytes=vmem)


def _rms_stats(xf):
    r = lax.rsqrt(jnp.mean(xf * xf, axis=-1, keepdims=True) + EPS)
    return xf * r, r


def _rms_bwd(dy, xh, r, w):
    dxh = dy * w
    return r * (dxh - xh * jnp.mean(dxh * xh, axis=-1, keepdims=True))


def _sigmoid(x):
    return 0.5 * jnp.tanh(0.5 * x) + 0.5


GELU_K0 = math.sqrt(2.0 / math.pi)
GELU_K1 = 0.044715


CHIP_MASKS = [(1, 0, 0), (0, 1, 0), (1, 1, 0)]
ALL_MASKS = [(0, 0, 1), (0, 1, 0), (0, 1, 1), (1, 0, 0), (1, 0, 1), (1, 1, 0), (1, 1, 1)]
SIB_MASKS = [(0, 0, 1)]
ANY_SPEC = pl.BlockSpec(memory_space=pl.ANY)


class _Plan:
    def __init__(self, arrays, masks, n_slots, src_slotted, dst_slotted, local_copy, half=False, forward=False):
        self.shapes = [(a.shape, a.dtype) for a in arrays]
        self.n = len(arrays)
        self.masks = masks
        self.n_slots = n_slots
        self.src_slotted, self.dst_slotted, self.local_copy = src_slotted, dst_slotted, local_copy
        self.half, self.forward = half, forward
        self.n_cp = self.n * len(masks) * (len(CHIP_MASKS) if forward else 1)

    def out_shape(self):
        out = []
        for shp, dt in self.shapes:
            if self.dst_slotted and not self.src_slotted:
                shp = (self.n_slots,) + shp
            elif self.src_slotted and not self.dst_slotted:
                shp = shp[1:]
            out.append(jax.ShapeDtypeStruct(shp, dt))
        return tuple(out)

    def scratch(self):
        return [pltpu.SemaphoreType.DMA((self.n_cp,)), pltpu.SemaphoreType.DMA((self.n_cp,)),
                pltpu.SemaphoreType.DMA((self.n,))]

    def _slot(self, px, py, pc):
        if self.n_slots == 8:
            return 4 * px + 2 * py + pc
        if self.n_slots == 4:
            return 2 * px + py
        return pc

    def copies(self, ins, outs, sems):
        send_sems, recv_sems, loc_sems = sems
        x, y, c = lax.axis_index("x"), lax.axis_index("y"), lax.axis_index("c")
        me = self._slot(x, y, c)
        n_m = len(self.masks)
        cps = []
        for a in range(self.n):
            if self.forward:
                rows = self.shapes[a][0][-2] // 2
                mine = pl.ds(pl.multiple_of(c * rows, 8), rows)
                for j, (mx, my, _) in enumerate(CHIP_MASKS):
                    blk = outs[a].at[2 * (1 - x if mx else x) + (1 - y if my else y), mine]
                    k = a * len(CHIP_MASKS) + j
                    cps.append(pltpu.make_async_remote_copy(
                        src_ref=blk, dst_ref=blk, send_sem=send_sems.at[k], recv_sem=recv_sems.at[k],
                        device_id=(x, y, 1 - c), device_id_type=MESH))
                continue
            if self.local_copy:
                src = ins[a].at[me] if self.src_slotted else ins[a]
                cps.append(pltpu.make_async_copy(src, outs[a].at[me], loc_sems.at[a]))
            for mi, (mx, my, mc) in enumerate(self.masks):
                px = 1 - x if mx else x
                py = 1 - y if my else y
                pc = 1 - c if mc else c
                src = ins[a].at[self._slot(px, py, pc)] if self.src_slotted else ins[a]
                dst = outs[a].at[me] if self.dst_slotted else outs[a]
                if self.half:
                    rows = src.shape[-2] // 2
                    mine = pl.ds(pl.multiple_of(c * rows, 8), rows)
                    src, dst = src.at[mine], dst.at[mine]
                k = a * n_m + mi
                cps.append(pltpu.make_async_remote_copy(
                    src_ref=src, dst_ref=dst, send_sem=send_sems.at[k], recv_sem=recv_sems.at[k],
                    device_id=(px, py, pc), device_id_type=MESH))
        return cps


def _exchange(name, plan, arrays):
    n = plan.n

    def body(*refs):
        cps = plan.copies(refs[:n], refs[n:2 * n], refs[2 * n:])
        for cp in cps:
            cp.start()
        for cp in cps:
            cp.wait()

    outs = pl.pallas_call(
        body, name=name, out_shape=plan.out_shape(),
        in_specs=[ANY_SPEC] * n, out_specs=tuple([ANY_SPEC] * n), scratch_shapes=plan.scratch(),
        input_output_aliases={i: i for i in range(n)} if plan.forward else {},
    )(*arrays)
    return list(outs)


def _pcall(body, *, name, grid, in_specs, out_specs, out_shape, scratch_shapes, args, plan=None, plan_args=()):
    sem = ("arbitrary",) * len(grid)
    if plan is None:
        return pl.pallas_call(body, name=name, grid=grid, in_specs=in_specs, out_specs=out_specs,
                              out_shape=out_shape, scratch_shapes=scratch_shapes,
                              compiler_params=_params(sem))(*args), []
    n_in, n_out, n_scr, n_p = len(in_specs), len(out_specs), len(scratch_shapes), plan.n

    def wrapped(*refs):
        ins = refs[:n_in]
        p_ins = refs[n_in:n_in + n_p]
        o0 = n_in + n_p
        outs = refs[o0:o0 + n_out]
        p_outs = refs[o0 + n_out:o0 + n_out + n_p]
        s0 = o0 + n_out + n_p
        scr = refs[s0:s0 + n_scr]
        sems = refs[s0 + n_scr:]
        ids = [pl.program_id(i) for i in range(len(grid))]
        first = functools.reduce(jnp.logical_and, [i == 0 for i in ids])
        last = functools.reduce(jnp.logical_and, [i == g - 1 for i, g in zip(ids, grid)])

        @pl.when(first)
        def _():
            for cp in plan.copies(p_ins, p_outs, sems):
                cp.start()

        body(*ins, *outs, *scr)

        @pl.when(last)
        def _():
            for cp in plan.copies(p_ins, p_outs, sems):
                cp.wait()

    res = pl.pallas_call(
        wrapped, name=name, grid=grid,
        in_specs=list(in_specs) + [ANY_SPEC] * n_p,
        out_specs=tuple(out_specs) + (ANY_SPEC,) * n_p,
        out_shape=tuple(out_shape) + plan.out_shape(),
        scratch_shapes=list(scratch_shapes) + plan.scratch(),
        compiler_params=_params(sem),
    )(*args, *plan_args)
    return res[:n_out], list(res[n_out:])


def _allgather_chips_plan(arrays):
    return _Plan(arrays, CHIP_MASKS, 4, False, True, True, half=True)


def _forward_sibling(name, gathered):
    return _exchange(name, _Plan(gathered, SIB_MASKS, 4, True, True, False, forward=True), gathered)


def _alltoall_chips_plan(arrays):
    return _Plan(arrays, CHIP_MASKS, 4, True, True, True)


def _swap_sibling(name, arrays):
    return _exchange(name, _Plan(arrays, SIB_MASKS, 2, False, False, False), arrays)


def _allgather_all_plan(arrays):
    return _Plan(arrays, ALL_MASKS, 8, False, True, True)


def _sum_slots(name, arrs, out_dtype):
    s, r = arrs[0].shape[0], arrs[0].shape[-2]
    c = arrs[0].shape[-1] * (2 if arrs[0].ndim == 4 else 1)
    n = len(arrs)
    tr = _tile(r, 512 if n == 1 else 176, 8)

    def body(*refs):
        for a_ref, o_ref in zip(refs[:n], refs[n:]):
            if len(a_ref.shape) == 4:
                for half in range(2):
                    acc = a_ref[0, half].astype(F32)
                    for i in range(1, s):
                        acc = acc + a_ref[i, half].astype(F32)
                    o_ref[:, half * (c // 2):(half + 1) * (c // 2)] = acc.astype(out_dtype)
            else:
                acc = a_ref[0].astype(F32)
                for i in range(1, s):
                    acc = acc + a_ref[i].astype(F32)
                o_ref[...] = acc.astype(out_dtype)

    def in_spec(a):
        if a.ndim == 4:
            return pl.BlockSpec((s, 2, tr, c // 2), lambda i: (0, 0, i, 0))
        return pl.BlockSpec((s, tr, c), lambda i: (0, i, 0))

    return list(pl.pallas_call(
        body, name=name, grid=(r // tr,),
        in_specs=[in_spec(a) for a in arrs],
        out_specs=(pl.BlockSpec((tr, c), lambda i: (i, 0)),) * n,
        out_shape=(jax.ShapeDtypeStruct((r, c), out_dtype),) * n,
        compiler_params=_params(("arbitrary",)),
    )(*arrs))


def _adam_math(w, g, m, v):
    m_new = ADAM_B1 * m + (1.0 - ADAM_B1) * g
    v_new = ADAM_B2 * v + (1.0 - ADAM_B2) * (g * g)
    m_hat = m_new / (1.0 - ADAM_B1 ** ADAM_STEP)
    v_hat = v_new / (1.0 - ADAM_B2 ** ADAM_STEP)
    delta = -ADAM_LR * (m_hat / (jnp.sqrt(v_hat) + ADAM_EPS) + ADAM_WD * w)
    return delta, m_new, v_new


def _adam(name, wmv, g_parts):
    w0 = wmv[0][0]
    r, c = w0.shape[-2:]
    n_w = len(wmv)
    n_g = len(g_parts[0])
    tr = _tile(r, 256 if n_w == 1 else 88, 8)
    lead = w0.ndim == 3
    at = (lambda ref: ref.at[0]) if lead else (lambda ref: ref)
    n_in = 3 + n_g

    def body(*refs):
        for j in range(n_w):
            ins = refs[j * n_in:(j + 1) * n_in]
            outs = refs[n_w * n_in + 4 * j:n_w * n_in + 4 * j + 4]
            w_ref, m_ref, v_ref = [at(t) for t in ins[:3]]
            g_out, d_out, m_out, v_out = [at(t) for t in outs]
            g = ins[3][...].astype(F32)
            for gr in ins[4:]:
                g = g + gr[...].astype(F32)
            delta, m_new, v_new = _adam_math(w_ref[...], g, m_ref[...], v_ref[...])
            g_out[...] = g
            d_out[...] = delta
            m_out[...] = m_new
            v_out[...] = v_new

    spec = pl.BlockSpec((tr, c), lambda i: (i, 0))
    wspec = pl.BlockSpec((1, tr, c), lambda i: (0, i, 0)) if lead else spec
    shp = jax.ShapeDtypeStruct(w0.shape, F32)
    args = [t for (w, m, v), gp in zip(wmv, g_parts) for t in (w, m, v, *gp)]
    res = pl.pallas_call(
        body, name=name, grid=(r // tr,),
        in_specs=([wspec] * 3 + [spec] * n_g) * n_w, out_specs=(wspec,) * (4 * n_w), out_shape=(shp,) * (4 * n_w),
        compiler_params=_params(("arbitrary",)),
    )(*args)
    return [tuple(res[4 * j:4 * j + 4]) for j in range(n_w)]


SUB_ROWS = 32
FFN_BWD_ROWS = 416
FFN_FWD_ROWS = 832


def _tile_parts(tm, d, head, x):
    nsub = tm // SUB_ROWS
    off = head.shape[0] // SUB_ROWS
    specs = [pl.BlockSpec(head.shape, lambda i, k: (0, 0))] + [
        pl.BlockSpec((SUB_ROWS, d), lambda i, k, j=j: (jnp.maximum(i * nsub + j - off, 0), 0)) for j in range(nsub)]

    def assemble(i, part_refs, h_sc):
        head_ref, x_refs = part_refs[0], part_refs[1:]
        for j in range(nsub):
            rows = slice(j * SUB_ROWS, (j + 1) * SUB_ROWS)
            val = x_refs[j][...]
            if j < off:
                val = jnp.where(i == 0, head_ref[rows, :], val)
            h_sc[rows, :] = val

    return specs, [head] + [x] * nsub, assemble


def _h_source(body, h, tm, d):
    if not isinstance(h, tuple):
        return body, [pl.BlockSpec((tm, d), lambda i, k: (i, 0))], [h], []
    specs, args, assemble = _tile_parts(tm, d, *h)
    n_h = len(specs)

    def with_parts(*refs):
        h_sc = refs[-1]

        @pl.when(pl.program_id(1) == 0)
        def _():
            assemble(pl.program_id(0), refs[:n_h], h_sc)

        body(h_sc, *refs[n_h:-1])

    return with_parts, specs, args, [pltpu.VMEM((tm, d), F32)]


def _ffn_fwd(name, h, nw, wg, wu, wd, plan=None, plan_args=()):
    lp, d = (h[0].shape[0] + h[1].shape[0], h[1].shape[1]) if isinstance(h, tuple) else h.shape
    nck, f, _ = wg.shape
    tm = _tile(lp, FFN_FWD_ROWS)
    last = nck // FFN_CPS - 1

    def body(h_ref, nw_ref, wg_ref, wu_ref, wd_ref, ho_ref, g_ref, u_ref, n_sc, acc_sc):
        k = pl.program_id(1)

        @pl.when(k == 0)
        def _():
            xh, _ = _rms_stats(h_ref[...])
            n_sc[...] = (xh * nw_ref[...]).astype(BF16)
            acc_sc[...] = jnp.zeros_like(acc_sc)

        n = n_sc[...]
        acc = acc_sc[...]
        for c in range(FFN_CPS):
            g = _dot_nt(n, wg_ref[c])
            u = _dot_nt(n, wu_ref[c])
            g_ref[c] = g.astype(BF16)
            u_ref[c] = u.astype(BF16)
            a = (g * _sigmoid(g) * u).astype(BF16)
            acc = acc + _dot(a, wd_ref[c])
        acc_sc[...] = acc

        @pl.when(k == last)
        def _():
            ho_ref[...] = h_ref[...] + FFN_RES * acc_sc[...]

    body, h_specs, h_args, h_scratch = _h_source(body, h, tm, d)
    w_fd = pl.BlockSpec((FFN_CPS, f, d), lambda i, k: (k, 0, 0))
    hid = pl.BlockSpec((FFN_CPS, tm, f), lambda i, k: (k, i, 0))
    return _pcall(
        body, name=name, grid=(lp // tm, nck // FFN_CPS), plan=plan, plan_args=plan_args,
        args=(*h_args, nw, wg, wu, wd),
        in_specs=h_specs + [pl.BlockSpec((1, d), lambda i, k: (0, 0)), w_fd, w_fd, w_fd],
        out_specs=(pl.BlockSpec((tm, d), lambda i, k: (i, 0)), hid, hid),
        out_shape=(jax.ShapeDtypeStruct((lp, d), F32),
                   jax.ShapeDtypeStruct((nck, lp, f), BF16),
                   jax.ShapeDtypeStruct((nck, lp, f), BF16)),
        scratch_shapes=[pltpu.VMEM((tm, d), BF16), pltpu.VMEM((tm, d), F32)] + h_scratch)


def _ffn_bwd_act(name, dh, h, nw, g, u, wg, wu, wd, plan=None, plan_args=()):
    lp, d = dh.shape
    nck, f, _ = wg.shape
    tm = _tile(lp, FFN_BWD_ROWS, SUB_ROWS)
    last = nck // FFN_CPS - 1

    def body(h_ref, dh_ref, nw_ref, g_ref, u_ref, wg_ref, wu_ref, wd_ref,
             dhi_ref, dnw_ref, n_ref, dacc_ref, a_ref, dg_ref, du_ref,
             xh_sc, r_sc, dn_sc):
        i = pl.program_id(0)
        k = pl.program_id(1)

        @pl.when(k == 0)
        def _():
            xh, r = _rms_stats(h_ref[...])
            xh_sc[...] = xh
            r_sc[...] = r
            n_ref[...] = (xh * nw_ref[...]).astype(BF16)
            dacc_ref[...] = (FFN_RES * dh_ref[...]).astype(BF16)
            dn_sc[...] = jnp.zeros_like(dn_sc)

        @pl.when(jnp.logical_and(i == 0, k == 0))
        def _():
            dnw_ref[...] = jnp.zeros_like(dnw_ref)

        dacc = dacc_ref[...]
        dn = dn_sc[...]
        for c in range(FFN_CPS):
            gv = g_ref[c].astype(F32)
            uv = u_ref[c].astype(F32)
            sg = _sigmoid(gv)
            sil = gv * sg
            da = _dot_nt(dacc, wd_ref[c])
            dgk = (da * uv * (sg * (1.0 + gv * (1.0 - sg)))).astype(BF16)
            duk = (da * sil).astype(BF16)
            a_ref[c] = (sil * uv).astype(BF16)
            dg_ref[c] = dgk
            du_ref[c] = duk
            dn = dn + _dot(dgk, wg_ref[c]) + _dot(duk, wu_ref[c])
        dn_sc[...] = dn

        @pl.when(k == last)
        def _():
            dnl = dn_sc[...]
            xh = xh_sc[...]
            dhi_ref[...] = dh_ref[...] + _rms_bwd(dnl, xh, r_sc[...], nw_ref[...])
            dnw_ref[...] += jnp.sum(dnl * xh, axis=0, keepdims=True)

    body, h_specs, h_args, h_scratch = _h_source(body, h, tm, d)
    row = pl.BlockSpec((tm, d), lambda i, k: (i, 0))
    vec = pl.BlockSpec((1, d), lambda i, k: (0, 0))
    hid = pl.BlockSpec((FFN_CPS, tm, f), lambda i, k: (k, i, 0))
    w_fd = pl.BlockSpec((FFN_CPS, f, d), lambda i, k: (k, 0, 0))
    rshape = jax.ShapeDtypeStruct((lp, d), BF16)
    hshape = jax.ShapeDtypeStruct((nck, lp, f), BF16)
    return _pcall(
        body, name=name, grid=(lp // tm, nck // FFN_CPS), plan=plan, plan_args=plan_args,
        args=(*h_args, dh, nw, g, u, wg, wu, wd),
        in_specs=h_specs + [row, vec, hid, hid, w_fd, w_fd, w_fd],
        out_specs=(row, vec, row, row, hid, hid, hid),
        out_shape=(jax.ShapeDtypeStruct((lp, d), F32), jax.ShapeDtypeStruct((1, d), F32),
                   rshape, rshape, hshape, hshape, hshape),
        scratch_shapes=[pltpu.VMEM((tm, d), F32), pltpu.VMEM((tm, 1), F32), pltpu.VMEM((tm, d), F32)] + h_scratch)


def _ffn_bwd_w(name, n, dacc, a, dg, du, plan=None, plan_args=()):
    lp, d = n.shape
    nck, _, f = a.shape
    tm = _tile(lp, BWD_W_ROWS)
    last = lp // tm - 1

    def body(n_ref, dacc_ref, a_ref, dg_ref, du_ref, dwg_ref, dwu_ref, dwd_ref, ag_sc, au_sc, ad_sc):
        i = pl.program_id(1)

        @pl.when(i == 0)
        def _():
            ag_sc[...] = jnp.zeros_like(ag_sc)
            au_sc[...] = jnp.zeros_like(au_sc)
            ad_sc[...] = jnp.zeros_like(ad_sc)

        nv = n_ref[...]
        ag_sc[...] += _dot_tn(dg_ref[0], nv)
        au_sc[...] += _dot_tn(du_ref[0], nv)
        ad_sc[...] += _dot_tn(a_ref[0], dacc_ref[...])

        @pl.when(i == last)
        def _():
            dwg_ref[0] = ag_sc[...].astype(BF16)
            dwu_ref[0] = au_sc[...].astype(BF16)
            dwd_ref[0] = ad_sc[...].astype(BF16)

    row = pl.BlockSpec((tm, d), lambda k, i: (i, 0))
    hid = pl.BlockSpec((1, tm, f), lambda k, i: (k, i, 0))
    w_fd = pl.BlockSpec((1, f, d), lambda k, i: (k, 0, 0))
    wshape = jax.ShapeDtypeStruct((nck, f, d), BF16)
    return _pcall(
        body, name=name, grid=(nck, lp // tm), plan=plan, plan_args=plan_args, args=(n, dacc, a, dg, du),
        in_specs=[row, row, hid, hid, hid], out_specs=(w_fd, w_fd, w_fd), out_shape=(wshape,) * 3,
        scratch_shapes=[pltpu.VMEM((f, d), F32)] * 3)


def _ffn_bwd_w_scatter(name, n, dacc, a, dg, du, chip, plan, plan_args):
    lp, d = n.shape
    nck, _, f = a.shape
    tm = _tile(lp, BWD_W_ROWS)
    last_i = lp // tm - 1
    n_w = 3
    n_p = plan.n

    def body(me_ref, n_ref, dacc_ref, a_ref, dg_ref, du_ref, *rest):
        p_ins = rest[:n_p]
        recv = rest[n_p:n_p + n_w]
        p_outs = rest[n_p + n_w:2 * n_p + n_w]
        acc = rest[2 * n_p + n_w:2 * n_p + 2 * n_w]
        stage, send_sems, recv_sems, loc_sems = rest[2 * n_p + 2 * n_w:2 * n_p + 2 * n_w + 4]
        p_sems = rest[2 * n_p + 2 * n_w + 4:]
        p = pl.program_id(0)
        i = pl.program_id(1)
        me = me_ref[0]
        c = lax.axis_index("c")

        def send(w, pos):
            kk = jnp.bitwise_xor(me, nck - 1 - pos)
            diff = jnp.bitwise_xor(kk, me)
            m = jnp.where(diff == 2, 0, jnp.where(diff == 1, 1, 2))
            return pltpu.make_async_remote_copy(
                src_ref=stage.at[lax.rem(pos, 2), w], dst_ref=recv[w].at[me],
                send_sem=send_sems.at[w * 3 + m], recv_sem=recv_sems.at[w * 3 + m],
                device_id=(lax.div(kk, 2), lax.rem(kk, 2), c), device_id_type=MESH)

        @pl.when(jnp.logical_and(p == 0, i == 0))
        def _():
            for cp in plan.copies(p_ins, p_outs, p_sems):
                cp.start()

        @pl.when(i == 0)
        def _():
            for t in acc:
                t[...] = jnp.zeros_like(t)

        nv = n_ref[...]
        acc[0][...] += _dot_tn(dg_ref[0], nv)
        acc[1][...] += _dot_tn(du_ref[0], nv)
        acc[2][...] += _dot_tn(a_ref[0], dacc_ref[...])

        @pl.when(jnp.logical_and(i == last_i, p >= 2))
        def _():
            for w in range(n_w):
                send(w, p - 2).wait_send()

        @pl.when(i == last_i)
        def _():
            for w in range(n_w):
                stage[lax.rem(p, 2), w] = acc[w][...].astype(BF16)

        @pl.when(jnp.logical_and(i == last_i, p < nck - 1))
        def _():
            for w in range(n_w):
                send(w, p).start()

        @pl.when(jnp.logical_and(i == last_i, p == nck - 1))
        def _():
            own = [pltpu.make_async_copy(stage.at[(nck - 1) % 2, w], recv[w].at[me], loc_sems.at[w])
                   for w in range(n_w)]
            for cp in own:
                cp.start()
            for w in range(n_w):
                send(w, nck - 2).wait_send()
            for cp in own:
                cp.wait()
            for w in range(n_w):
                for m in range(3):
                    pltpu.make_async_remote_copy(
                        src_ref=stage.at[0, w], dst_ref=recv[w].at[me],
                        send_sem=send_sems.at[w * 3 + m], recv_sem=recv_sems.at[w * 3 + m],
                        device_id=(0, 0, c), device_id_type=MESH).wait_recv()
            for cp in plan.copies(p_ins, p_outs, p_sems):
                cp.wait()

    chunk = lambda k, me_ref: jnp.bitwise_xor(me_ref[0], nck - 1 - k)
    row = pl.BlockSpec((tm, d), lambda k, i, me_ref: (i, 0))
    hid = pl.BlockSpec((1, tm, f), lambda k, i, me_ref: (chunk(k, me_ref), i, 0))
    wshape = jax.ShapeDtypeStruct((nck, f, d), BF16)
    res = pl.pallas_call(
        body, name=name,
        grid_spec=pltpu.PrefetchScalarGridSpec(
            num_scalar_prefetch=1, grid=(nck, lp // tm),
            in_specs=[row, row, hid, hid, hid] + [ANY_SPEC] * n_p,
            out_specs=(ANY_SPEC,) * (n_w + n_p),
            scratch_shapes=[pltpu.VMEM((f, d), F32)] * n_w + [
                pltpu.VMEM((2, n_w, f, d), BF16), pltpu.SemaphoreType.DMA((n_w * 3,)),
                pltpu.SemaphoreType.DMA((n_w * 3,)), pltpu.SemaphoreType.DMA((n_w,))] + plan.scratch()),
        out_shape=(wshape,) * n_w + plan.out_shape(),
        compiler_params=_params(("arbitrary", "arbitrary")),
    )(chip.reshape(1).astype(jnp.int32), n, dacc, a, dg, du, *plan_args)
    return list(res[:n_w]), list(res[n_w:])


def _inproj_fwd(h, nw, w_in, cosf, sinf, rw):
    lp, d = h.shape
    nck, _, ps = w_in.shape
    proj = nck * ps
    sw = proj - 4 * rw
    tm = _tile(lp, 640)
    scale = HEAD_DIM ** -0.5
    heads = rw // HEAD_DIM

    def body(h_ref, nw_ref, w_ref, cos_ref, sin_ref, n_ref, q_ref, k_ref, v_ref, g_ref, u_ref, p_sc):
        xh, _ = _rms_stats(h_ref[...])
        n = (xh * nw_ref[...]).astype(BF16)
        n_ref[...] = n
        for c in range(nck):
            p_sc[:, c * ps:(c + 1) * ps] = _dot(n, w_ref[c])
        cs = cos_ref[...]
        sn = sin_ref[...]
        for hh in range(heads):
            lo = hh * HEAD_DIM
            qh = p_sc[:, lo:lo + HEAD_DIM]
            q_ref[:, lo:lo + HEAD_DIM] = (qh * cs + pltpu.roll(qh, HEAD_DIM // 2, 1) * sn).astype(BF16)
            kh = p_sc[:, rw + lo:rw + lo + HEAD_DIM]
            k_ref[:, lo:lo + HEAD_DIM] = ((kh * cs + pltpu.roll(kh, HEAD_DIM // 2, 1) * sn) * scale).astype(BF16)
        v_ref[...] = p_sc[:, 2 * rw:3 * rw].astype(BF16)
        g_ref[...] = p_sc[:, 3 * rw:4 * rw]
        u_ref[...] = p_sc[:, 4 * rw:]

    row = lambda w: pl.BlockSpec((tm, w), lambda i: (i, 0))
    return pl.pallas_call(
        body, name="inproj_fwd", grid=(lp // tm,),
        in_specs=[row(d), pl.BlockSpec((1, d), lambda i: (0, 0)),
                  pl.BlockSpec((nck, d, ps), lambda i: (0, 0, 0)), row(HEAD_DIM), row(HEAD_DIM)],
        out_specs=(row(d), row(rw), row(rw), row(rw), row(rw), row(sw)),
        out_shape=(jax.ShapeDtypeStruct((lp, d), BF16),
                   jax.ShapeDtypeStruct((lp, rw), BF16),
                   jax.ShapeDtypeStruct((lp, rw), BF16),
                   jax.ShapeDtypeStruct((lp, rw), BF16),
                   jax.ShapeDtypeStruct((lp, rw), F32),
                   jax.ShapeDtypeStruct((lp, sw), F32)),
        scratch_shapes=[pltpu.VMEM((tm, proj), F32)],
        compiler_params=_params(("arbitrary",)),
    )(h, nw, w_in, cosf, sinf)


def _inproj_bwd(dh, h, nw, n, w_in, dq, dk, dv, dg, du):
    lp, d = h.shape
    nck, _, ps = w_in.shape
    rw = dq.shape[1]
    sw = du.shape[1]
    proj = nck * ps
    tm = _tile(lp, 640)
    last = lp // tm - 1

    def gather_dproj(p_sc, dq_ref, dk_ref, dv_ref, dg_ref, du_ref):
        p_sc[:, 0:rw] = dq_ref[...]
        p_sc[:, rw:2 * rw] = dk_ref[...]
        p_sc[:, 2 * rw:3 * rw] = dv_ref[...]
        p_sc[:, 3 * rw:4 * rw] = dg_ref[...]
        p_sc[:, 4 * rw:] = du_ref[...]

    def act_body(dh_ref, h_ref, nw_ref, w_ref, dq_ref, dk_ref, dv_ref, dg_ref, du_ref, dhi_ref, dnw_ref, p_sc):
        i = pl.program_id(0)

        @pl.when(i == 0)
        def _():
            dnw_ref[...] = jnp.zeros_like(dnw_ref)

        gather_dproj(p_sc, dq_ref, dk_ref, dv_ref, dg_ref, du_ref)
        dn = jnp.zeros((tm, d), F32)
        for c in range(nck):
            dn = dn + _dot_nt(p_sc[:, c * ps:(c + 1) * ps], w_ref[c])
        xh, r = _rms_stats(h_ref[...])
        dhi_ref[...] = dh_ref[...] + _rms_bwd(dn, xh, r, nw_ref[...])
        dnw_ref[...] += jnp.sum(dn * xh, axis=0, keepdims=True)

    def w_body(n_ref, dq_ref, dk_ref, dv_ref, dg_ref, du_ref, dw_ref, p_sc, acc_sc):
        i = pl.program_id(0)

        @pl.when(i == 0)
        def _():
            acc_sc[...] = jnp.zeros_like(acc_sc)

        gather_dproj(p_sc, dq_ref, dk_ref, dv_ref, dg_ref, du_ref)
        nv = n_ref[...]
        for c in range(nck):
            acc_sc[c] += _dot_tn(nv, p_sc[:, c * ps:(c + 1) * ps])

        @pl.when(i == last)
        def _():
            dw_ref[...] = acc_sc[...].astype(BF16)

    row = lambda w: pl.BlockSpec((tm, w), lambda i: (i, 0))
    vec = pl.BlockSpec((1, d), lambda i: (0, 0))
    wsp = pl.BlockSpec((nck, d, ps), lambda i: (0, 0, 0))
    dproj_specs = [row(rw), row(rw), row(rw), row(rw), row(sw)]
    dhi, dnw = pl.pallas_call(
        act_body, name="inproj_bwd_act", grid=(lp // tm,),
        in_specs=[row(d), row(d), vec, wsp] + dproj_specs,
        out_specs=(row(d), vec),
        out_shape=(jax.ShapeDtypeStruct((lp, d), F32), jax.ShapeDtypeStruct((1, d), F32)),
        scratch_shapes=[pltpu.VMEM((tm, proj), BF16)],
        compiler_params=_params(("arbitrary",)),
    )(dh, h, nw, w_in, dq, dk, dv, dg, du)
    dw = pl.pallas_call(
        w_body, name="inproj_bwd_w", grid=(lp // tm,),
        in_specs=[row(d)] + dproj_specs,
        out_specs=wsp, out_shape=jax.ShapeDtypeStruct((nck, d, ps), BF16),
        scratch_shapes=[pltpu.VMEM((tm, proj), BF16), pltpu.VMEM((nck, d, ps), F32)],
        compiler_params=_params(("arbitrary",)),
    )(n, dq, dk, dv, dg, du)
    return dhi, dnw, dw


def _retention_tables():
    h = jnp.arange(RET_HEADS, dtype=F32)
    log_g = jnp.log(1.0 - 2.0 ** (-5.0 - h))
    i = jnp.arange(CHUNK)
    diff = i[:, None] - i[None, :]
    dec = jnp.where(diff[None] >= 0,
                    jnp.exp(log_g[:, None, None] * jnp.maximum(diff, 0)[None].astype(F32)), 0.0)
    pos = jnp.arange(CHUNK, dtype=F32)
    wq = jnp.exp(log_g[:, None] * (pos + 1.0)[None])
    wk = jnp.exp(log_g[:, None] * (CHUNK - 1 - pos)[None])
    gch = jnp.exp(log_g * CHUNK)
    ones = jnp.ones((1, 1, HEAD_DIM), F32)
    return (dec, wq[:, :, None] * ones, wk[:, :, None] * ones,
            gch[:, None, None] * jnp.ones((1, 8, HEAD_DIM), F32))


def _head_norm(o):
    mu = jnp.mean(o, axis=-1, keepdims=True)
    oc = o - mu
    r = lax.rsqrt(jnp.mean(oc * oc, axis=-1, keepdims=True) + EPS)
    return oc * r, r


def _ret_fwd(q, k, v, g, rnw, tables):
    lp, rw = q.shape
    heads = rw // HEAD_DIM
    nch = lp // CHUNK
    dec, wq, wk, gch = tables

    def body(q_ref, k_ref, v_ref, g_ref, w_ref, dec_ref, wq_ref, wk_ref, gch_ref,
             o_ref, ret_ref, sp_ref, s_sc):
        n = pl.program_id(0)

        @pl.when(n == 0)
        def _():
            s_sc[...] = jnp.zeros_like(s_sc)

        cols = [slice(hh * HEAD_DIM, (hh + 1) * HEAD_DIM) for hh in range(heads)]
        s_ins = [s_sc[hh] for hh in range(heads)]
        outs = []
        for hh, cs in enumerate(cols):
            qv, kv, vv = q_ref[:, cs], k_ref[:, cs], v_ref[:, cs]
            s_in = s_ins[hh]
            a = _dot_nt(qv, kv) * dec_ref[hh]
            qw = (qv.astype(F32) * wq_ref[hh]).astype(BF16)
            kw = (kv.astype(F32) * wk_ref[hh]).astype(BF16)
            o = _dot(a.astype(BF16), vv) + _dot(qw, s_in.astype(BF16))
            s_new = gch_ref[hh, 0:1, :] * s_in + _dot_tn(kw, vv)
            xh, _ = _head_norm(o)
            gv = g_ref[:, cs]
            outs.append((o, s_new, (gv * _sigmoid(gv) * (xh * w_ref[:, cs])).astype(BF16)))
        for hh, cs in enumerate(cols):
            o, s_new, ret = outs[hh]
            sp_ref[hh, 0] = s_ins[hh]
            s_sc[hh] = s_new
            o_ref[:, cs] = o
            ret_ref[:, cs] = ret

    blk = pl.BlockSpec((CHUNK, rw), lambda n: (n, 0))
    tab = pl.BlockSpec((heads, CHUNK, HEAD_DIM), lambda n: (0, 0, 0))
    return pl.pallas_call(
        body, name="retention_fwd", grid=(nch,),
        in_specs=[blk, blk, blk, blk, pl.BlockSpec((1, rw), lambda n: (0, 0)),
                  tab, tab, tab, pl.BlockSpec((heads, 8, HEAD_DIM), lambda n: (0, 0, 0))],
        out_specs=(blk, blk, pl.BlockSpec((heads, 1, HEAD_DIM, HEAD_DIM), lambda n: (0, n, 0, 0))),
        out_shape=(jax.ShapeDtypeStruct((lp, rw), F32),
                   jax.ShapeDtypeStruct((lp, rw), BF16),
                   jax.ShapeDtypeStruct((heads, nch, HEAD_DIM, HEAD_DIM), F32)),
        scratch_shapes=[pltpu.VMEM((heads, HEAD_DIM, HEAD_DIM), F32)],
        compiler_params=_params(("arbitrary",)),
    )(q, k, v, g, rnw, dec, wq, wk, gch)


def _ret_bwd(dret, q, k, v, g, o, sprev, rnw, tables, cosf, sinf):
    lp, rw = q.shape
    heads = rw // HEAD_DIM
    nch = lp // CHUNK
    dec, wq, wk, gch = tables
    scale = HEAD_DIM ** -0.5
    half = HEAD_DIM // 2

    def body(dret_ref, q_ref, k_ref, v_ref, g_ref, o_ref, sp_ref, w_ref, dec_ref, wq_ref, wk_ref, gch_ref,
             cos_ref, sin_ref, dq_ref, dk_ref, dv_ref, dg_ref, dw_ref, ds_sc):
        n = pl.program_id(0)

        @pl.when(n == 0)
        def _():
            ds_sc[...] = jnp.zeros_like(ds_sc)
            dw_ref[...] = jnp.zeros_like(dw_ref)

        cosv = cos_ref[...]
        sinv = sin_ref[...]
        cols = [slice(hh * HEAD_DIM, (hh + 1) * HEAD_DIM) for hh in range(heads)]
        ds_ins = [ds_sc[hh] for hh in range(heads)]
        dw_ins = [dw_ref[:, cs] for cs in cols]
        outs = []
        for hh, cs in enumerate(cols):
            qv, kv, vv = q_ref[:, cs], k_ref[:, cs], v_ref[:, cs]
            gv = g_ref[:, cs]
            dr = dret_ref[:, cs]
            w = w_ref[:, cs]
            sg = _sigmoid(gv)
            sil = gv * sg
            xh, r = _head_norm(o_ref[:, cs])
            dgate = (dr * (xh * w) * (sg * (1.0 + gv * (1.0 - sg)))).astype(BF16)
            dyw = dr * sil
            dw_new = dw_ins[hh] + jnp.sum(dyw * xh, axis=0, keepdims=True)
            dxh = dyw * w
            do = r * (dxh - jnp.mean(dxh, axis=-1, keepdims=True)
                      - xh * jnp.mean(dxh * xh, axis=-1, keepdims=True))
            dob = do.astype(BF16)
            dmask = dec_ref[hh]
            wqv = wq_ref[hh]
            wkv = wk_ref[hh]
            a = (_dot_nt(qv, kv) * dmask).astype(BF16)
            da = (_dot_nt(dob, vv) * dmask).astype(BF16)
            qw = (qv.astype(F32) * wqv).astype(BF16)
            kw = (kv.astype(F32) * wkv).astype(BF16)
            s_in = sp_ref[hh, 0].astype(BF16)
            ds = ds_ins[hh]
            dsb = ds.astype(BF16)
            dq = _dot(da, kv) + _dot_nt(dob, s_in) * wqv
            dk = _dot_tn(da, qv) + _dot_nt(vv, dsb) * wkv
            dv = _dot_tn(a, dob) + _dot(kw, dsb)
            ds_new = gch_ref[hh, 0:1, :] * ds + _dot_tn(qw, dob)
            outs.append((dgate, dw_new, ds_new,
                         (dq * cosv + pltpu.roll(dq * sinv, half, 1)).astype(BF16),
                         ((dk * cosv + pltpu.roll(dk * sinv, half, 1)) * scale).astype(BF16),
                         dv.astype(BF16)))
        for hh, cs in enumerate(cols):
            dgate, dw_new, ds_new, dqv, dkv, dvv = outs[hh]
            dg_ref[:, cs] = dgate
            dw_ref[:, cs] = dw_new
            ds_sc[hh] = ds_new
            dq_ref[:, cs] = dqv
            dk_ref[:, cs] = dkv
            dv_ref[:, cs] = dvv

    blk = pl.BlockSpec((CHUNK, rw), lambda n: (nch - 1 - n, 0))
    tab = pl.BlockSpec((heads, CHUNK, HEAD_DIM), lambda n: (0, 0, 0))
    wsp = pl.BlockSpec((1, rw), lambda n: (0, 0))
    pos = pl.BlockSpec((CHUNK, HEAD_DIM), lambda n: (nch - 1 - n, 0))
    bshape = jax.ShapeDtypeStruct((lp, rw), BF16)
    return pl.pallas_call(
        body, name="retention_bwd", grid=(nch,),
        in_specs=[blk, blk, blk, blk, blk, blk,
                  pl.BlockSpec((heads, 1, HEAD_DIM, HEAD_DIM), lambda n: (0, nch - 1 - n, 0, 0)),
                  wsp, tab, tab, tab, pl.BlockSpec((heads, 8, HEAD_DIM), lambda n: (0, 0, 0)), pos, pos],
        out_specs=(blk, blk, blk, blk, wsp),
        out_shape=(bshape, bshape, bshape, bshape, jax.ShapeDtypeStruct((1, rw), F32)),
        scratch_shapes=[pltpu.VMEM((heads, HEAD_DIM, HEAD_DIM), F32)],
        compiler_params=_params(("arbitrary",)),
    )(dret, q, k, v, g, o, sprev, rnw, dec, wq, wk, gch, cosf, sinf)


SCAN_CW = 512


def _s5_prepare(lam_re, lam_im, log_dt, b_re, b_im):
    dt = jnp.exp(log_dt)[:, None]
    er = jnp.exp(lam_re * dt)
    ar = er * jnp.cos(lam_im * dt)
    ai = er * jnp.sin(lam_im * dt)
    den = lam_re * lam_re + lam_im * lam_im
    fr = ((ar - 1.0) * lam_re + ai * lam_im) / den
    fi = (ai * lam_re - (ar - 1.0) * lam_im) / den
    bbr = fr[..., None] * b_re - fi[..., None] * b_im
    bbi = fr[..., None] * b_im + fi[..., None] * b_re
    return ar, ai, bbr, bbi


def _blockdiag_in(t):
    g, p, n = t.shape
    gs = g // N_SEC
    t = t.reshape(N_SEC, gs, p, n)
    eye = jnp.eye(gs, dtype=t.dtype)
    return jnp.einsum("sgpn,gh->sgphn", t, eye).reshape(N_SEC, gs * p, gs * n)


def _blockdiag_out(m, g, p, n):
    gs = g // N_SEC
    m = m.reshape(N_SEC, gs, p, gs, n)
    eye = jnp.eye(gs, dtype=m.dtype)
    return jnp.einsum("sgphn,gh->sgpn", m, eye).reshape(g, p, n)


def _scan_step(xr_ref, xi_ref, r0, prev, ar_ref, ai_ref, conj, ncols):
    new = []
    for cc in range(ncols // SCAN_CW):
        cs = pl.ds(cc * SCAN_CW, SCAN_CW)
        pr, pi = prev[cc]
        ar = ar_ref[:, cs]
        ai = ai_ref[:, cs]
        if conj:
            nr = ar * pr + ai * pi
            ni = ar * pi - ai * pr
        else:
            nr = ar * pr - ai * pi
            ni = ar * pi + ai * pr
        xr = xr_ref[pl.ds(r0, 8), cs] + nr
        xi = xi_ref[pl.ds(r0, 8), cs] + ni
        xr_ref[pl.ds(r0, 8), cs] = xr
        xi_ref[pl.ds(r0, 8), cs] = xi
        new.append((xr, xi))
    return new


def _scan_chunks(ncols):
    return [pl.ds(cc * SCAN_CW, SCAN_CW) for cc in range(ncols // SCAN_CW)]


def _flat(pairs):
    return tuple(t for p in pairs for t in p)


def _pairs(flat):
    return [(flat[2 * k], flat[2 * k + 1]) for k in range(len(flat) // 2)]


def _shift_rows(z, down):
    row = lax.broadcasted_iota(jnp.int32, z.shape, 0)
    if down:
        return jnp.where(row == 0, 0.0, pltpu.roll(z, 1, 0))
    return jnp.where(row == N_SEG - 1, 0.0, pltpu.roll(z, N_SEG - 1, 0))


def _s5_fwd(u, bsr, bsi, csr, csi, a8r, a8i, al8r, al8i, d, gluw, glub, nw, jb):
    lp, sw = u.shape
    ns = a8r.shape[1]
    rows = N_SEG * jb
    nblk = lp // rows
    secw = sw // N_SEC
    secn = ns // N_SEC

    def local_scan(u_ref, bsr_ref, bsi_ref, ar_ref, ai_ref, xr_ref, xi_ref, pr_sc, pi_sc):
        for s in range(N_SEC):
            ub = u_ref[:, s * secw:(s + 1) * secw].astype(BF16)
            xr_ref[:, s * secn:(s + 1) * secn] = _dot(ub, bsr_ref[s])
            xi_ref[:, s * secn:(s + 1) * secn] = _dot(ub, bsi_ref[s])
        prev = [(pr_sc[:, cs], pi_sc[:, cs]) for cs in _scan_chunks(ns)]
        prev = _scan_step(xr_ref, xi_ref, 0, prev, ar_ref, ai_ref, False, ns)

        def step(j, carry):
            r0 = pl.multiple_of(j * 8, 8)
            return _flat(_scan_step(xr_ref, xi_ref, r0, _pairs(carry), ar_ref, ai_ref, False, ns))

        last = _pairs(lax.fori_loop(1, jb, step, _flat(prev)))
        for cs, (vr, vi) in zip(_scan_chunks(ns), last):
            pr_sc[:, cs] = vr
            pi_sc[:, cs] = vi

    def carry_body(u_ref, bsr_ref, bsi_ref, ar_ref, ai_ref, alr_ref, ali_ref, cr_ref, ci_ref,
                   xr_sc, xi_sc, pr_sc, pi_sc):
        b = pl.program_id(0)

        @pl.when(b == 0)
        def _():
            pr_sc[...] = jnp.zeros_like(pr_sc)
            pi_sc[...] = jnp.zeros_like(pi_sc)

        local_scan(u_ref, bsr_ref, bsi_ref, ar_ref, ai_ref, xr_sc, xi_sc, pr_sc, pi_sc)

        @pl.when(b == nblk - 1)
        def _():
            er = _shift_rows(pr_sc[...], True)
            ei = _shift_rows(pi_sc[...], True)
            alr, ali = alr_ref[...], ali_ref[...]
            cr, ci = er, ei
            for _ in range(N_SEG - 2):
                sr = _shift_rows(cr, True)
                si = _shift_rows(ci, True)
                cr = er + alr * sr - ali * si
                ci = ei + alr * si + ali * sr
            cr_ref[...] = cr
            ci_ref[...] = ci

    ublk = pl.BlockSpec((rows, sw), lambda b: (b, 0))
    bspec = pl.BlockSpec((N_SEC, secw, secn), lambda b: (0, 0, 0))
    cspec = pl.BlockSpec((N_SEC, secn, secw), lambda b: (0, 0, 0))
    s8 = pl.BlockSpec((N_SEG, ns), lambda b: (0, 0))
    vec = pl.BlockSpec((1, sw), lambda b: (0, 0))
    s8shape = jax.ShapeDtypeStruct((N_SEG, ns), F32)
    c0r, c0i = pl.pallas_call(
        carry_body, name="s5_fwd_carry", grid=(nblk,),
        in_specs=[ublk, bspec, bspec, s8, s8, s8, s8],
        out_specs=(s8, s8), out_shape=(s8shape, s8shape),
        scratch_shapes=[pltpu.VMEM((rows, ns), F32), pltpu.VMEM((rows, ns), F32),
                        pltpu.VMEM((N_SEG, ns), F32), pltpu.VMEM((N_SEG, ns), F32)],
        compiler_params=_params(("arbitrary",)),
    )(u, bsr, bsi, a8r, a8i, al8r, al8i)

    def main_body(u_ref, bsr_ref, bsi_ref, csr_ref, csi_ref, ar_ref, ai_ref, c0r_ref, c0i_ref,
                  d_ref, gw_ref, gb_ref, nw_ref, xr_ref, xi_ref, yp_ref, out_ref, pr_sc, pi_sc):
        b = pl.program_id(0)

        @pl.when(b == 0)
        def _():
            pr_sc[...] = c0r_ref[...]
            pi_sc[...] = c0i_ref[...]

        local_scan(u_ref, bsr_ref, bsi_ref, ar_ref, ai_ref, xr_ref, xi_ref, pr_sc, pi_sc)
        for s in range(N_SEC):
            xs = pl.ds(s * secn, secn)
            us = pl.ds(s * secw, secw)
            y = _dot(xr_ref[:, xs].astype(BF16), csr_ref[s]) + _dot(xi_ref[:, xs].astype(BF16), csi_ref[s])
            yp_ref[:, us] = y + d_ref[:, us] * u_ref[:, us]
        yp = yp_ref[...]
        t = jnp.tanh(GELU_K0 * (yp + GELU_K1 * yp * yp * yp))
        y1 = 0.5 * yp * (1.0 + t)
        z = _dot(y1.astype(BF16), gw_ref[...]) + gb_ref[...]
        y2 = y1 * _sigmoid(z)
        xh, _ = _rms_stats(y2)
        out_ref[...] = (xh * nw_ref[...]).astype(BF16)

    xblk = pl.BlockSpec((rows, ns), lambda b: (b, 0))
    xr, xi, yp, out = pl.pallas_call(
        main_body, name="s5_fwd", grid=(nblk,),
        in_specs=[ublk, bspec, bspec, cspec, cspec, s8, s8, s8, s8, vec,
                  pl.BlockSpec((sw, sw), lambda b: (0, 0)), vec, vec],
        out_specs=(xblk, xblk, ublk, ublk),
        out_shape=(jax.ShapeDtypeStruct((lp, ns), F32), jax.ShapeDtypeStruct((lp, ns), F32),
                   jax.ShapeDtypeStruct((lp, sw), F32), jax.ShapeDtypeStruct((lp, sw), BF16)),
        scratch_shapes=[pltpu.VMEM((N_SEG, ns), F32), pltpu.VMEM((N_SEG, ns), F32)],
        compiler_params=_params(("arbitrary",)),
    )(u, bsr, bsi, csr, csi, a8r, a8i, c0r, c0i, d, gluw, glub, nw)
    return xr, xi, c0r, c0i, yp, out


def _s5_bwd(dout, u, yp, xr, xi, c0r, c0i, bsrt, bsit, csrt, csit, a8r, a8i, al8r, al8i, d, gluw, glub, nw, jb):
    lp, sw = u.shape
    ns = a8r.shape[1]
    rows = N_SEG * jb
    nblk = lp // rows
    secw = sw // N_SEC
    secn = ns // N_SEC

    def rowwise_bwd(dout_ref, yp_ref, gw_ref, gb_ref, nw_ref):
        ypv = yp_ref[...]
        t = jnp.tanh(GELU_K0 * (ypv + GELU_K1 * ypv * ypv * ypv))
        y1 = 0.5 * ypv * (1.0 + t)
        dgelu = 0.5 * (1.0 + t) + 0.5 * ypv * (1.0 - t * t) * GELU_K0 * (1.0 + 3.0 * GELU_K1 * ypv * ypv)
        gw = gw_ref[...]
        y1b = y1.astype(BF16)
        sg = _sigmoid(_dot(y1b, gw) + gb_ref[...])
        xh, r = _rms_stats(y1 * sg)
        dov = dout_ref[...]
        dy2 = _rms_bwd(dov, xh, r, nw_ref[...])
        dz = dy2 * y1 * sg * (1.0 - sg)
        dzb = dz.astype(BF16)
        dy1 = dy2 * sg + _dot_nt(dzb, gw)
        return dy1 * dgelu, dov * xh, y1b, dzb, dz

    def lam_scan(dyp_of, csrt_ref, csit_ref, ar_ref, ai_ref, lr_sc, li_sc, nr_sc, ni_sc, extra):
        for s in range(N_SEC):
            db = dyp_of(s)
            lr_sc[:, s * secn:(s + 1) * secn] = _dot(db, csrt_ref[s])
            li_sc[:, s * secn:(s + 1) * secn] = _dot(db, csit_ref[s])
        top = rows - 8
        prev = [(nr_sc[:, cs], ni_sc[:, cs]) for cs in _scan_chunks(ns)]
        prev = _scan_step(lr_sc, li_sc, top, prev, ar_ref, ai_ref, True, ns)
        extra(top, pl.ds(top - 8, 8))

        def step(jj, carry):
            r0 = pl.multiple_of((jb - 1 - jj) * 8, 8)
            rp = pl.multiple_of((jb - 2 - jj) * 8, 8)
            new = _scan_step(lr_sc, li_sc, r0, _pairs(carry), ar_ref, ai_ref, True, ns)
            extra(r0, pl.ds(rp, 8))
            return _flat(new)

        prev = _pairs(lax.fori_loop(1, jb - 1, step, _flat(prev)))
        last = _scan_step(lr_sc, li_sc, 0, prev, ar_ref, ai_ref, True, ns)
        extra(0, None)
        for cs, (vr, vi) in zip(_scan_chunks(ns), last):
            nr_sc[:, cs] = vr
            ni_sc[:, cs] = vi

    def carry_body(dout_ref, yp_ref, u_ref, gw_ref, gb_ref, nw_ref, csrt_ref, csit_ref, ar_ref, ai_ref,
                   alr_ref, ali_ref, cr_ref, ci_ref, dyp_ref, dnw_ref, dgw_ref, dgb_ref, dd_ref,
                   lr_sc, li_sc, nr_sc, ni_sc):
        b = pl.program_id(0)

        @pl.when(b == 0)
        def _():
            nr_sc[...] = jnp.zeros_like(nr_sc)
            ni_sc[...] = jnp.zeros_like(ni_sc)
            for ref in (dnw_ref, dgw_ref, dgb_ref, dd_ref):
                ref[...] = jnp.zeros_like(ref)

        dyp, dnw_rows, y1b, dzb, dz = rowwise_bwd(dout_ref, yp_ref, gw_ref, gb_ref, nw_ref)
        dnw_ref[...] += jnp.sum(dnw_rows, axis=0, keepdims=True)
        dgw_ref[...] += _dot_tn(y1b, dzb)
        dgb_ref[...] += jnp.sum(dz, axis=0, keepdims=True)
        dd_ref[...] += jnp.sum(dyp * u_ref[...], axis=0, keepdims=True)
        dyp_ref[...] = dyp.astype(BF16)
        lam_scan(lambda s: dyp_ref[:, s * secw:(s + 1) * secw], csrt_ref, csit_ref, ar_ref, ai_ref,
                 lr_sc, li_sc, nr_sc, ni_sc, lambda r0, prev_rows: None)

        @pl.when(b == nblk - 1)
        def _():
            fr = _shift_rows(nr_sc[...], False)
            fi = _shift_rows(ni_sc[...], False)
            alr, ali = alr_ref[...], ali_ref[...]
            cr, ci = fr, fi
            for _ in range(N_SEG - 2):
                sr = _shift_rows(cr, False)
                si = _shift_rows(ci, False)
                cr = fr + alr * sr + ali * si
                ci = fi + alr * si - ali * sr
            cr_ref[...] = cr
            ci_ref[...] = ci

    rev = lambda b: (nblk - 1 - b, 0)
    ublk = pl.BlockSpec((rows, sw), rev)
    xblk = pl.BlockSpec((rows, ns), rev)
    s8 = pl.BlockSpec((N_SEG, ns), lambda b: (0, 0))
    vec = pl.BlockSpec((1, sw), lambda b: (0, 0))
    gws = pl.BlockSpec((sw, sw), lambda b: (0, 0))
    btspec = pl.BlockSpec((N_SEC, secn, secw), lambda b: (0, 0, 0))
    ctspec = pl.BlockSpec((N_SEC, secw, secn), lambda b: (0, 0, 0))
    s8shape = jax.ShapeDtypeStruct((N_SEG, ns), F32)
    lcr, lci, dyp_all, d_nw, d_gw, d_gb, d_d = pl.pallas_call(
        carry_body, name="s5_bwd_carry", grid=(nblk,),
        in_specs=[ublk, ublk, ublk, gws, vec, vec, ctspec, ctspec, s8, s8, s8, s8],
        out_specs=(s8, s8, ublk, vec, gws, vec, vec),
        out_shape=(s8shape, s8shape, jax.ShapeDtypeStruct((lp, sw), BF16), jax.ShapeDtypeStruct((1, sw), F32),
                   jax.ShapeDtypeStruct((sw, sw), F32), jax.ShapeDtypeStruct((1, sw), F32),
                   jax.ShapeDtypeStruct((1, sw), F32)),
        scratch_shapes=[pltpu.VMEM((rows, ns), F32), pltpu.VMEM((rows, ns), F32),
                        pltpu.VMEM((N_SEG, ns), F32), pltpu.VMEM((N_SEG, ns), F32)],
        compiler_params=_params(("arbitrary",)),
    )(dout, yp, u, gluw, glub, nw, csrt, csit, a8r, a8i, al8r, al8i)

    def main_body(dyp_sc, u_ref, xr_ref, xi_ref, xtr_ref, xti_ref, c0r_ref, c0i_ref, lcr_ref, lci_ref,
                  d_ref, bsrt_ref, bsit_ref, csrt_ref, csit_ref, ar_ref, ai_ref,
                  du_ref, dcr_ref, dci_ref, dbr_ref, dbi_ref, dar_ref, dai_ref,
                  lr_sc, li_sc, nr_sc, ni_sc):
        b = pl.program_id(0)

        @pl.when(b == 0)
        def _():
            nr_sc[...] = lcr_ref[...]
            ni_sc[...] = lci_ref[...]
            for ref in (dcr_ref, dci_ref, dbr_ref, dbi_ref, dar_ref, dai_ref):
                ref[...] = jnp.zeros_like(ref)

        for s in range(N_SEC):
            db = dyp_sc[:, s * secw:(s + 1) * secw]
            xs = pl.ds(s * secn, secn)
            dcr_ref[s] += _dot_tn(xr_ref[:, xs].astype(BF16), db)
            dci_ref[s] += _dot_tn(xi_ref[:, xs].astype(BF16), db)

        first = b == nblk - 1

        def acc_da(r0, prev_rows):
            for cc in range(ns // SCAN_CW):
                cs = pl.ds(cc * SCAN_CW, SCAN_CW)
                lr = lr_sc[pl.ds(r0, 8), cs]
                li = li_sc[pl.ds(r0, 8), cs]
                if prev_rows is None:
                    xpr = jnp.where(first, c0r_ref[:, cs], xtr_ref[:, cs])
                    xpi = jnp.where(first, c0i_ref[:, cs], xti_ref[:, cs])
                else:
                    xpr = xr_ref[prev_rows, cs]
                    xpi = xi_ref[prev_rows, cs]
                dar_ref[:, cs] += lr * xpr + li * xpi
                dai_ref[:, cs] += li * xpr - lr * xpi

        lam_scan(lambda s: dyp_sc[:, s * secw:(s + 1) * secw], csrt_ref, csit_ref, ar_ref, ai_ref,
                 lr_sc, li_sc, nr_sc, ni_sc, acc_da)

        for s in range(N_SEC):
            xs = pl.ds(s * secn, secn)
            us = pl.ds(s * secw, secw)
            lrb = lr_sc[:, xs].astype(BF16)
            lib = li_sc[:, xs].astype(BF16)
            du = _dot(lrb, bsrt_ref[s]) + _dot(lib, bsit_ref[s]) + d_ref[:, us] * dyp_sc[:, us].astype(F32)
            du_ref[:, us] = du.astype(BF16)
            ub = u_ref[:, us].astype(BF16)
            dbr_ref[s] += _dot_tn(ub, lrb)
            dbi_ref[s] += _dot_tn(ub, lib)

    tail = pl.BlockSpec((N_SEG, ns), lambda b: (jnp.maximum((nblk - 1 - b) * jb - 1, 0), 0))
    acc_c = pl.BlockSpec((N_SEC, secn, secw), lambda b: (0, 0, 0))
    acc_b = pl.BlockSpec((N_SEC, secw, secn), lambda b: (0, 0, 0))
    du, dcr, dci, dbr, dbi, dar, dai = pl.pallas_call(
        main_body, name="s5_bwd", grid=(nblk,),
        in_specs=[ublk, ublk, xblk, xblk, tail, tail, s8, s8, s8, s8,
                  vec, btspec, btspec, ctspec, ctspec, s8, s8],
        out_specs=(ublk, acc_c, acc_c, acc_b, acc_b, s8, s8),
        out_shape=(jax.ShapeDtypeStruct((lp, sw), BF16),
                   jax.ShapeDtypeStruct((N_SEC, secn, secw), F32),
                   jax.ShapeDtypeStruct((N_SEC, secn, secw), F32),
                   jax.ShapeDtypeStruct((N_SEC, secw, secn), F32),
                   jax.ShapeDtypeStruct((N_SEC, secw, secn), F32),
                   s8shape, s8shape),
        scratch_shapes=[pltpu.VMEM((rows, ns), F32), pltpu.VMEM((rows, ns), F32),
                        pltpu.VMEM((N_SEG, ns), F32), pltpu.VMEM((N_SEG, ns), F32)],
        compiler_params=_params(("arbitrary",)),
    )(dyp_all, u, xr, xi, xr, xi, c0r, c0i, lcr, lci, d, bsrt, bsit, csrt, csit, a8r, a8i)
    return du, d_nw, d_gw, d_gb, d_d, dcr, dci, dbr, dbi, dar, dai


def _outproj_fwd(h, ret, ssm, wo):
    lp, d = h.shape
    nck, rs, _ = wo.shape
    rw = ret.shape[1]
    tm = _tile(lp, 640)
    per = rw // rs

    def body(h_ref, ret_ref, ssm_ref, w_ref, o_ref):
        acc = h_ref[...]
        for c in range(nck):
            src = ret_ref if c < per else ssm_ref
            lo = (c % per) * rs
            acc = acc + _dot(src[:, lo:lo + rs], w_ref[c])
        o_ref[...] = acc

    row = lambda w: pl.BlockSpec((tm, w), lambda i: (i, 0))
    return pl.pallas_call(
        body, name="outproj_fwd", grid=(lp // tm,),
        in_specs=[row(d), row(rw), row(ssm.shape[1]), pl.BlockSpec((nck, rs, d), lambda i: (0, 0, 0))],
        out_specs=row(d), out_shape=jax.ShapeDtypeStruct((lp, d), F32),
        compiler_params=_params(("arbitrary",)),
    )(h, ret, ssm, wo)


def _outproj_bwd(dh, ret, ssm, wo):
    lp, d = dh.shape
    nck, rs, _ = wo.shape
    rw = ret.shape[1]
    sw = ssm.shape[1]
    tm = _tile(lp, 640)
    per = rw // rs
    last = lp // tm - 1

    def body(dh_ref, ret_ref, ssm_ref, w_ref, dret_ref, dssm_ref, dw_ref, acc_sc):
        i = pl.program_id(0)

        @pl.when(i == 0)
        def _():
            acc_sc[...] = jnp.zeros_like(acc_sc)

        dhb = dh_ref[...].astype(BF16)
        for c in range(nck):
            src, dst = (ret_ref, dret_ref) if c < per else (ssm_ref, dssm_ref)
            lo = (c % per) * rs
            dst[:, lo:lo + rs] = _dot_nt(dhb, w_ref[c])
            acc_sc[c] += _dot_tn(src[:, lo:lo + rs], dhb)

        @pl.when(i == last)
        def _():
            dw_ref[...] = acc_sc[...].astype(BF16)

    row = lambda w: pl.BlockSpec((tm, w), lambda i: (i, 0))
    wsp = pl.BlockSpec((nck, rs, d), lambda i: (0, 0, 0))
    return pl.pallas_call(
        body, name="outproj_bwd", grid=(lp // tm,),
        in_specs=[row(d), row(rw), row(sw), wsp],
        out_specs=(row(rw), row(sw), wsp),
        out_shape=(jax.ShapeDtypeStruct((lp, rw), F32), jax.ShapeDtypeStruct((lp, sw), F32),
                   jax.ShapeDtypeStruct((nck, rs, d), BF16)),
        scratch_shapes=[pltpu.VMEM((nck, rs, d), F32)],
        compiler_params=_params(("arbitrary",)),
    )(dh, ret, ssm, wo)


def _loss_head(h, fw, target):
    lp, d = h.shape
    tm = _tile(lp, 640, CHUNK)
    sub = tm // CHUNK

    def body(h_ref, w_ref, *rest):
        t_refs = rest[:sub]
        loss_ref, dh_ref, dw_ref = rest[sub:]
        i = pl.program_id(0)

        @pl.when(i == 0)
        def _():
            loss_ref[...] = jnp.zeros_like(loss_ref)
            dw_ref[...] = jnp.zeros_like(dw_ref)

        w = w_ref[...]
        for j in range(sub):
            rows = pl.ds(j * CHUNK, CHUNK)
            xh, r = _rms_stats(h_ref[rows, :])
            err = xh * w - t_refs[j][...]
            if j == 0:
                err = jnp.where(i == 0, 0.0, err)
            loss_ref[...] += 0.5 * jnp.sum(err * err) / d
            dout = err * (1.0 / d)
            dw_ref[...] += jnp.sum(dout * xh, axis=0, keepdims=True)
            dh_ref[rows, :] = _rms_bwd(dout, xh, r, w)

    t_spec = lambda j: pl.BlockSpec((CHUNK, d), lambda i: (jnp.maximum(i * sub + j - 1, 0), 0))
    return pl.pallas_call(
        body, name="loss_head", grid=(lp // tm,),
        in_specs=[pl.BlockSpec((tm, d), lambda i: (i, 0)), pl.BlockSpec((1, d), lambda i: (0, 0))]
        + [t_spec(j) for j in range(sub)],
        out_specs=(pl.BlockSpec((8, LANE), lambda i: (0, 0)), pl.BlockSpec((tm, d), lambda i: (i, 0)),
                   pl.BlockSpec((1, d), lambda i: (0, 0))),
        out_shape=(jax.ShapeDtypeStruct((8, LANE), F32), jax.ShapeDtypeStruct((lp, d), F32),
                   jax.ShapeDtypeStruct((1, d), F32)),
        compiler_params=_params(("arbitrary",)),
    )(h, fw, *([target] * sub))


def _pack(arrs):
    flat = jnp.concatenate([a.reshape(-1).astype(F32) for a in arrs])
    n = flat.shape[0]
    rows = -(-n // (8 * LANE)) * 8
    return jnp.pad(flat, (0, rows * LANE - n)).reshape(rows, LANE)


def _unpack(packed, shapes):
    flat = packed.reshape(-1)
    out, off = [], 0
    for s in shapes:
        n = math.prod(s)
        out.append(flat[off:off + n].reshape(s))
        off += n
    return out


def _to_segments(a, seg_len):
    return a.reshape(N_SEG, seg_len, a.shape[1]).transpose(1, 0, 2).reshape(a.shape)


def _from_segments(a, seg_len):
    return a.reshape(seg_len, N_SEG, a.shape[1]).transpose(1, 0, 2).reshape(a.shape)


WEIGHT_NAMES = ['meta_tokens', 'ffn1_norm_w', 'ffn1_w_gate', 'ffn1_w_up', 'ffn1_w_down', 'mix_norm_w', 'w_in',
                'ret_norm_w', 'ssm_lambda_re', 'ssm_lambda_im', 'ssm_log_dt', 'ssm_b_re', 'ssm_b_im', 'ssm_c_re',
                'ssm_c_im', 'ssm_d', 'ssm_glu_w', 'ssm_glu_b', 'ssm_norm_w', 'w_out', 'ffn2_norm_w', 'ffn2_w_gate',
                'ffn2_w_up', 'ffn2_w_down', 'final_norm_w']
BIG = ['ffn1_w_gate', 'ffn1_w_up', 'ffn1_w_down', 'w_in', 'ssm_glu_w', 'w_out', 'ffn2_w_gate', 'ffn2_w_up',
       'ffn2_w_down']
TRANSPOSED = ['ffn1_w_gate', 'ffn1_w_up', 'ffn2_w_gate', 'ffn2_w_up']
BIG_EARLY = ['ffn1_w_gate', 'ffn1_w_up', 'ffn1_w_down']
BIG_LATE = [n for n in BIG if n not in BIG_EARLY]
SMALL = [n for n in WEIGHT_NAMES if n not in BIG]


def kernel(x, meta_tokens, ffn1_norm_w, ffn1_w_gate, ffn1_w_up, ffn1_w_down, mix_norm_w, w_in, ret_norm_w, ssm_lambda_re, ssm_lambda_im, ssm_log_dt, ssm_b_re, ssm_b_im, ssm_c_re, ssm_c_im, ssm_d, ssm_glu_w, ssm_glu_b, ssm_norm_w, w_out, ffn2_norm_w, ffn2_w_gate, ffn2_w_up, ffn2_w_down, final_norm_w, loss_target, m_meta_tokens, m_ffn1_norm_w, m_ffn1_w_gate, m_ffn1_w_up, m_ffn1_w_down, m_mix_norm_w, m_w_in, m_ret_norm_w, m_ssm_lambda_re, m_ssm_lambda_im, m_ssm_log_dt, m_ssm_b_re, m_ssm_b_im, m_ssm_c_re, m_ssm_c_im, m_ssm_d, m_ssm_glu_w, m_ssm_glu_b, m_ssm_norm_w, m_w_out, m_ffn2_norm_w, m_ffn2_w_gate, m_ffn2_w_up, m_ffn2_w_down, m_final_norm_w, v_meta_tokens, v_ffn1_norm_w, v_ffn1_w_gate, v_ffn1_w_up, v_ffn1_w_down, v_mix_norm_w, v_w_in, v_ret_norm_w, v_ssm_lambda_re, v_ssm_lambda_im, v_ssm_log_dt, v_ssm_b_re, v_ssm_b_im, v_ssm_c_re, v_ssm_c_im, v_ssm_d, v_ssm_glu_w, v_ssm_glu_b, v_ssm_norm_w, v_w_out, v_ffn2_norm_w, v_ffn2_w_gate, v_ffn2_w_up, v_ffn2_w_down, v_final_norm_w):
    args = locals()
    w = {n: args[n] for n in WEIGHT_NAMES}
    m = {n: args["m_" + n] for n in WEIGHT_NAMES}
    v = {n: args["v_" + n] for n in WEIGHT_NAMES}

    seq, d = x.shape[1], x.shape[2]
    lp = seq + CHUNK
    seg_len = lp // N_SEG
    rw = RET_HEADS * HEAD_DIM
    sw = ssm_d.shape[-1]
    groups = sw // SSM_GROUP
    ns = groups * SSM_STATE
    jb = _tile(seg_len, 40, 8)
    chip = 2 * lax.axis_index("x") + lax.axis_index("y")

    as_fd = lambda t: jnp.swapaxes(t, -1, -2)
    shards = {n: (as_fd(w[n][0]) if n in TRANSPOSED else w[n][0]).astype(BF16) for n in BIG}
    early = [shards[n] for n in BIG_EARLY] + [meta_tokens]
    gathered = _forward_sibling("gather_early_forward",
                                _exchange("gather_early", _allgather_chips_plan(early), early))
    gw = dict(zip(BIG_EARLY, gathered[:-1]))
    meta_full = jnp.transpose(gathered[-1], (1, 0, 2)).reshape(N_META, d)
    late = [shards[n] for n in BIG_LATE]

    pos = jnp.arange(lp, dtype=F32) - float(CHUNK - N_META)
    freqs = 1.0 / (ROPE_BASE ** (jnp.arange(0, HEAD_DIM, 2, dtype=F32) / HEAD_DIM))
    ang = pos[:, None] * freqs[None, :]
    cosf = jnp.concatenate([jnp.cos(ang), jnp.cos(ang)], axis=1)
    sinf = jnp.concatenate([-jnp.sin(ang), jnp.sin(ang)], axis=1)
    tables = _retention_tables()

    lam_re, lam_im, log_dt = ssm_lambda_re[0], ssm_lambda_im[0], ssm_log_dt[0]
    b_re, b_im, c_re, c_im = ssm_b_re[0], ssm_b_im[0], ssm_c_re[0], ssm_c_im[0]
    (ar, ai, bbr, bbi), prep_vjp = jax.vjp(_s5_prepare, lam_re, lam_im, log_dt, b_re, b_im)
    dt = jnp.exp(log_dt)[:, None]
    el = jnp.exp(seg_len * lam_re * dt)
    alr = el * jnp.cos(seg_len * lam_im * dt)
    ali = el * jnp.sin(seg_len * lam_im * dt)
    bc8 = lambda t: jnp.broadcast_to(t.reshape(1, ns), (N_SEG, ns))
    a8r, a8i, al8r, al8i = bc8(ar), bc8(ai), bc8(alr), bc8(ali)
    bsr = _blockdiag_in(jnp.transpose(bbr, (0, 2, 1)))
    bsi = _blockdiag_in(jnp.transpose(bbi, (0, 2, 1)))
    csrt = _blockdiag_in(c_re)
    csit = _blockdiag_in(-c_im)
    tr = lambda t: jnp.transpose(t, (0, 2, 1))
    bsr_b, bsi_b = bsr.astype(BF16), bsi.astype(BF16)
    csr_b, csi_b = tr(csrt).astype(BF16), tr(csit).astype(BF16)
    bsrt_b, bsit_b = tr(bsr).astype(BF16), tr(bsi).astype(BF16)
    csrt_b, csit_b = csrt.astype(BF16), csit.astype(BF16)

    h0 = (jnp.concatenate([jnp.zeros((CHUNK - N_META, d), F32), meta_full], axis=0), x[0])
    (h1, g1, u1), late_half = _ffn_fwd("ffn1_fwd", h0, ffn1_norm_w, gw['ffn1_w_gate'], gw['ffn1_w_up'],
                                       gw['ffn1_w_down'], _allgather_chips_plan(late), late)
    gw.update(zip(BIG_LATE, _forward_sibling("gather_late_forward", late_half)))
    glu_full = gw['ssm_glu_w'].reshape(sw, sw)
    n2, q, k, vv, gate, u = _inproj_fwd(h1, mix_norm_w, gw['w_in'], cosf, sinf, rw)
    o, ret, sprev = _ret_fwd(q, k, vv, gate, ret_norm_w, tables)
    u_seg = _to_segments(u, seg_len)
    xr, xi, c0r, c0i, yp, ssm_seg = _s5_fwd(u_seg, bsr_b, bsi_b, csr_b, csi_b, a8r, a8i, al8r, al8i,
                                            ssm_d, glu_full, ssm_glu_b, ssm_norm_w, jb)
    ssm = _from_segments(ssm_seg, seg_len)
    h2 = _outproj_fwd(h1, ret, ssm, gw['w_out'])
    (h3, g2, u2), _ = _ffn_fwd("ffn2_fwd", h2, ffn2_norm_w, gw['ffn2_w_gate'], gw['ffn2_w_up'], gw['ffn2_w_down'])
    loss_part, dh3, d_final = _loss_head(h3, final_norm_w.reshape(1, d), loss_target[0])

    (dh2, d_ffn2_norm, nb, daccb, ab, dgb, dub), _ = _ffn_bwd_act(
        "ffn2_bwd_act", dh3, h2, ffn2_norm_w, g2, u2, gw['ffn2_w_gate'], gw['ffn2_w_up'], gw['ffn2_w_down'])
    (dwg2, dwu2, dwd2), _ = _ffn_bwd_w("ffn2_bwd_w", nb, daccb, ab, dgb, dub)
    dret, dssm, dwo = _outproj_bwd(dh2, ret, ssm, gw['w_out'])
    (du_seg, d_ssm_norm, d_glu_w, d_glu_b, d_ssm_d, dcr_s, dci_s, dbr_s, dbi_s, dar8, dai8) = _s5_bwd(
        _to_segments(dssm, seg_len), u_seg, yp, xr, xi, c0r, c0i, bsrt_b, bsit_b, csrt_b, csit_b,
        a8r, a8i, al8r, al8i, ssm_d, glu_full, ssm_glu_b, ssm_norm_w, jb)
    du = _from_segments(du_seg, seg_len)
    dq, dk, dv, dgate, d_ret_norm = _ret_bwd(dret, q, k, vv, gate, o, sprev, ret_norm_w, tables, cosf, sinf)
    dh1, d_mix_norm, dwin = _inproj_bwd(dh2, h1, mix_norm_w, n2, gw['w_in'], dq, dk, dv, dgate, du)
    late_parts = {
        'w_in': dwin, 'ssm_glu_w': d_glu_w.reshape(N_CHIP, sw // N_CHIP, sw).astype(BF16), 'w_out': dwo,
        'ffn2_w_gate': dwg2, 'ffn2_w_up': dwu2, 'ffn2_w_down': dwd2,
    }
    late_list = [late_parts[n] for n in BIG_LATE]
    (dh0, d_ffn1_norm, nb, daccb, ab, dgb, dub), late_recv = _ffn_bwd_act(
        "ffn1_bwd_act", dh1, h0, ffn1_norm_w, g1, u1, gw['ffn1_w_gate'], gw['ffn1_w_up'], gw['ffn1_w_down'],
        _alltoall_chips_plan(late_list), late_list)
    grad_x = dh0[CHUNK:][None]
    d_meta = dh0[CHUNK - N_META:CHUNK]

    d_c_re = jnp.transpose(_blockdiag_out(tr(dcr_s), groups, SSM_GROUP, SSM_STATE), (0, 1, 2))
    d_c_im = -_blockdiag_out(tr(dci_s), groups, SSM_GROUP, SSM_STATE)
    d_bbr = jnp.transpose(_blockdiag_out(dbr_s, groups, SSM_GROUP, SSM_STATE), (0, 2, 1))
    d_bbi = jnp.transpose(_blockdiag_out(dbi_s, groups, SSM_GROUP, SSM_STATE), (0, 2, 1))
    d_ar = jnp.sum(dar8, axis=0).reshape(groups, SSM_STATE)
    d_ai = jnp.sum(dai8, axis=0).reshape(groups, SSM_STATE)
    small_parts = [loss_part[0:1, :], d_meta, d_ffn1_norm, d_mix_norm, d_ret_norm, d_ar, d_ai, d_bbr, d_bbi,
                   d_c_re, d_c_im, d_ssm_d, d_glu_b, d_ssm_norm, d_ffn2_norm, d_final]
    small_shapes = [a.shape for a in small_parts]
    packed = _pack(small_parts)
    early_recv, (all_parts,) = _ffn_bwd_w_scatter("ffn1_bwd_w", nb, daccb, ab, dgb, dub, chip,
                                                  _allgather_all_plan([packed]), [packed])
    received = dict(zip(BIG_LATE + BIG_EARLY, late_recv + early_recv))
    ffn_names = [n for n in BIG if n.startswith('ffn')]
    chip_sum = dict(zip(ffn_names, _sum_slots("sum_chips_ffn", [received[n] for n in ffn_names], BF16)))
    for n in BIG:
        if n not in chip_sum:
            chip_sum[n] = _sum_slots("sum_chips_" + n, [received[n]], BF16)[0]
    chip_sums = [chip_sum[n] for n in BIG]
    sib_sums = _swap_sibling("swap_sibling", chip_sums)
    (loss_row, g_meta_full, g_ffn1_norm, g_mix_norm, g_ret_norm, g_ar, g_ai, g_bbr, g_bbi, g_c_re, g_c_im,
     g_ssm_d, g_glu_b, g_ssm_norm, g_ffn2_norm, g_final) = _unpack(_sum_slots("sum_small", [all_parts], F32)[0],
                                                                  small_shapes)
    g_lam_re, g_lam_im, g_log_dt, g_b_re, g_b_im = prep_vjp((g_ar, g_ai, g_bbr, g_bbi))
    loss = loss_row[0, 0]
    g_meta = lax.dynamic_slice(g_meta_full, (0, chip * (d // N_CHIP)), (N_META, d // N_CHIP))
    small_grads = {
        'meta_tokens': g_meta, 'ffn1_norm_w': g_ffn1_norm, 'mix_norm_w': g_mix_norm, 'ret_norm_w': g_ret_norm,
        'ssm_lambda_re': g_lam_re[None], 'ssm_lambda_im': g_lam_im[None], 'ssm_log_dt': g_log_dt[None],
        'ssm_b_re': g_b_re[None], 'ssm_b_im': g_b_im[None], 'ssm_c_re': g_c_re[None], 'ssm_c_im': g_c_im[None],
        'ssm_d': g_ssm_d, 'ssm_glu_b': g_glu_b, 'ssm_norm_w': g_ssm_norm, 'ffn2_norm_w': g_ffn2_norm,
        'final_norm_w': g_final.reshape(d),
    }

    grads, deltas, new_m, new_v = {}, {}, {}, {}
    g_pair = {n: [mine, sib] for n, mine, sib in zip(BIG, chip_sums, sib_sums)}
    view = lambda n, t: as_fd(t) if n in TRANSPOSED else t
    ffn_out = _adam("adam_ffn", [(view(n, w[n]), view(n, m[n]), view(n, v[n])) for n in ffn_names],
                    [g_pair[n] for n in ffn_names])
    for n, outs in zip(ffn_names, ffn_out):
        grads[n], deltas[n], new_m[n], new_v[n] = [view(n, t) for t in outs]
    for n in BIG:
        if n not in ffn_names:
            grads[n], deltas[n], new_m[n], new_v[n] = _adam("adam_" + n, [(w[n], m[n], v[n])], [g_pair[n]])[0]
    sm_shapes = [w[n].shape for n in SMALL]
    sm_out = _adam("adam_small", [(_pack([w[n] for n in SMALL]), _pack([m[n] for n in SMALL]),
                                  _pack([v[n] for n in SMALL]))],
                   [[_pack([small_grads[n].reshape(w[n].shape) for n in SMALL])]])[0]
    for dst, packed in zip((grads, deltas, new_m, new_v), sm_out):
        for n, t in zip(SMALL, _unpack(packed, sm_shapes)):
            dst[n] = t

    return (loss, grad_x, *[grads[n] for n in WEIGHT_NAMES], *[deltas[n] for n in WEIGHT_NAMES],
            *[new_m[n] for n in WEIGHT_NAMES], *[new_v[n] for n in WEIGHT_NAMES])
```

```python
import functools
import math

import jax
import jax.numpy as jnp
from jax import lax
from jax.experimental import pallas as pl
from jax.experimental.pallas import tpu as pltpu

N_META = 16
RET_HEADS = 4
HEAD_DIM = 128
SSM_GROUP = 16
SSM_STATE = 64
CHUNK = 128
ROPE_BASE = 10000.0
EPS = 1e-6
FFN_RES = 0.5
N_SEG = 8
N_SEC = 4
N_CHIP = 4
LANE = 128
FFN_CPS = 2
BWD_W_ROWS = 1664

ADAM_LR = 0.001
ADAM_B1 = 0.9
ADAM_B2 = 0.999
ADAM_EPS = 1e-08
ADAM_WD = 0.01
ADAM_STEP = 10

VMEM_LIMIT = 56 * 1024 * 1024

F32 = jnp.float32
BF16 = jnp.bfloat16
MESH = pl.DeviceIdType.MESH


def _dot(a, b):
    return jnp.dot(a, b, preferred_element_type=F32)


def _dot_nt(a, b):
    return lax.dot_general(a, b, (((1,), (1,)), ((), ())), preferred_element_type=F32)


def _dot_tn(a, b):
    return lax.dot_general(a, b, (((0,), (0,)), ((), ())), preferred_element_type=F32)


def _tile(n, target, mult=64):
    best = None
    t = mult
    while t <= min(n, target):
        if n % t == 0:
            best = t
        t += mult
    assert best is not None, (n, target)
    return best


def _params(sem, vmem=VMEM_LIMIT):
    return pltpu.CompilerParams(dimension_semantics=sem, vmem_limit_bytes=vmem)


def _rms_stats(xf):
    r = lax.rsqrt(jnp.mean(xf * xf, axis=-1, keepdims=True) + EPS)
    return xf * r, r


def _rms_bwd(dy, xh, r, w):
    dxh = dy * w
    return r * (dxh - xh * jnp.mean(dxh * xh, axis=-1, keepdims=True))


def _sigmoid(x):
    return 0.5 * jnp.tanh(0.5 * x) + 0.5


GELU_K0 = math.sqrt(2.0 / math.pi)
GELU_K1 = 0.044715


CHIP_MASKS = [(1, 0, 0), (0, 1, 0), (1, 1, 0)]
ALL_MASKS = [(0, 0, 1), (0, 1, 0), (0, 1, 1), (1, 0, 0), (1, 0, 1), (1, 1, 0), (1, 1, 1)]
SIB_MASKS = [(0, 0, 1)]
ANY_SPEC = pl.BlockSpec(memory_space=pl.ANY)


class _Plan:
    def __init__(self, arrays, masks, n_slots, src_slotted, dst_slotted, local_copy, half=False, forward=False):
        self.shapes = [(a.shape, a.dtype) for a in arrays]
        self.n = len(arrays)
        self.masks = masks
        self.n_slots = n_slots
        self.src_slotted, self.dst_slotted, self.local_copy = src_slotted, dst_slotted, local_copy
        self.half, self.forward = half, forward
        self.n_cp = self.n * len(masks) * (len(CHIP_MASKS) if forward else 1)

    def out_shape(self):
        out = []
        for shp, dt in self.shapes:
            if self.dst_slotted and not self.src_slotted:
                shp = (self.n_slots,) + shp
            elif self.src_slotted and not self.dst_slotted:
                shp = shp[1:]
            out.append(jax.ShapeDtypeStruct(shp, dt))
        return tuple(out)

    def scratch(self):
        return [pltpu.SemaphoreType.DMA((self.n_cp,)), pltpu.SemaphoreType.DMA((self.n_cp,)),
                pltpu.SemaphoreType.DMA((self.n,))]

    def _slot(self, px, py, pc):
        if self.n_slots == 8:
            return 4 * px + 2 * py + pc
        if self.n_slots == 4:
            return 2 * px + py
        return pc

    def copies(self, ins, outs, sems):
        send_sems, recv_sems, loc_sems = sems
        x, y, c = lax.axis_index("x"), lax.axis_index("y"), lax.axis_index("c")
        me = self._slot(x, y, c)
        n_m = len(self.masks)
        cps = []
        for a in range(self.n):
            if self.forward:
                rows = self.shapes[a][0][-2] // 2
                mine = pl.ds(pl.multiple_of(c * rows, 8), rows)
                for j, (mx, my, _) in enumerate(CHIP_MASKS):
                    blk = outs[a].at[2 * (1 - x if mx else x) + (1 - y if my else y), mine]
                    k = a * len(CHIP_MASKS) + j
                    cps.append(pltpu.make_async_remote_copy(
                        src_ref=blk, dst_ref=blk, send_sem=send_sems.at[k], recv_sem=recv_sems.at[k],
                        device_id=(x, y, 1 - c), device_id_type=MESH))
                continue
            if self.local_copy:
                src = ins[a].at[me] if self.src_slotted else ins[a]
                cps.append(pltpu.make_async_copy(src, outs[a].at[me], loc_sems.at[a]))
            for mi, (mx, my, mc) in enumerate(self.masks):
                px = 1 - x if mx else x
                py = 1 - y if my else y
                pc = 1 - c if mc else c
                src = ins[a].at[self._slot(px, py, pc)] if self.src_slotted else ins[a]
                dst = outs[a].at[me] if self.dst_slotted else outs[a]
                if self.half:
                    rows = src.shape[-2] // 2
                    mine = pl.ds(pl.multiple_of(c * rows, 8), rows)
                    src, dst = src.at[mine], dst.at[mine]
                k = a * n_m + mi
                cps.append(pltpu.make_async_remote_copy(
                    src_ref=src, dst_ref=dst, send_sem=send_sems.at[k], recv_sem=recv_sems.at[k],
                    device_id=(px, py, pc), device_id_type=MESH))
        return cps


def _exchange(name, plan, arrays):
    n = plan.n

    def body(*refs):
        cps = plan.copies(refs[:n], refs[n:2 * n], refs[2 * n:])
        for cp in cps:
            cp.start()
        for cp in cps:
            cp.wait()

    outs = pl.pallas_call(
        body, name=name, out_shape=plan.out_shape(),
        in_specs=[ANY_SPEC] * n, out_specs=tuple([ANY_SPEC] * n), scratch_shapes=plan.scratch(),
        input_output_aliases={i: i for i in range(n)} if plan.forward else {},
    )(*arrays)
    return list(outs)


def _pcall(body, *, name, grid, in_specs, out_specs, out_shape, scratch_shapes, args, plan=None, plan_args=()):
    sem = ("arbitrary",) * len(grid)
    if plan is None:
        return pl.pallas_call(body, name=name, grid=grid, in_specs=in_specs, out_specs=out_specs,
                              out_shape=out_shape, scratch_shapes=scratch_shapes,
                              compiler_params=_params(sem))(*args), []
    n_in, n_out, n_scr, n_p = len(in_specs), len(out_specs), len(scratch_shapes), plan.n

    def wrapped(*refs):
        ins = refs[:n_in]
        p_ins = refs[n_in:n_in + n_p]
        o0 = n_in + n_p
        outs = refs[o0:o0 + n_out]
        p_outs = refs[o0 + n_out:o0 + n_out + n_p]
        s0 = o0 + n_out + n_p
        scr = refs[s0:s0 + n_scr]
        sems = refs[s0 + n_scr:]
        ids = [pl.program_id(i) for i in range(len(grid))]
        first = functools.reduce(jnp.logical_and, [i == 0 for i in ids])
        last = functools.reduce(jnp.logical_and, [i == g - 1 for i, g in zip(ids, grid)])

        @pl.when(first)
        def _():
            for cp in plan.copies(p_ins, p_outs, sems):
                cp.start()

        body(*ins, *outs, *scr)

        @pl.when(last)
        def _():
            for cp in plan.copies(p_ins, p_outs, sems):
                cp.wait()

    res = pl.pallas_call(
        wrapped, name=name, grid=grid,
        in_specs=list(in_specs) + [ANY_SPEC] * n_p,
        out_specs=tuple(out_specs) + (ANY_SPEC,) * n_p,
        out_shape=tuple(out_shape) + plan.out_shape(),
        scratch_shapes=list(scratch_shapes) + plan.scratch(),
        compiler_params=_params(sem),
    )(*args, *plan_args)
    return res[:n_out], list(res[n_out:])


def _allgather_chips_plan(arrays):
    return _Plan(arrays, CHIP_MASKS, 4, False, True, True, half=True)


def _forward_sibling(name, gathered):
    return _exchange(name, _Plan(gathered, SIB_MASKS, 4, True, True, False, forward=True), gathered)


def _alltoall_chips_plan(arrays):
    return _Plan(arrays, CHIP_MASKS, 4, True, True, True)


def _swap_sibling(name, arrays):
    return _exchange(name, _Plan(arrays, SIB_MASKS, 2, False, False, False), arrays)


def _allgather_all_plan(arrays):
    return _Plan(arrays, ALL_MASKS, 8, False, True, True)


def _sum_slots(name, arrs, out_dtype):
    s, r = arrs[0].shape[0], arrs[0].shape[-2]
    c = arrs[0].shape[-1] * (2 if arrs[0].ndim == 4 else 1)
    n = len(arrs)
    tr = _tile(r, 512 if n == 1 else 176, 8)

    def body(*refs):
        for a_ref, o_ref in zip(refs[:n], refs[n:]):
            if len(a_ref.shape) == 4:
                for half in range(2):
                    acc = a_ref[0, half].astype(F32)
                    for i in range(1, s):
                        acc = acc + a_ref[i, half].astype(F32)
                    o_ref[:, half * (c // 2):(half + 1) * (c // 2)] = acc.astype(out_dtype)
            else:
                acc = a_ref[0].astype(F32)
                for i in range(1, s):
                    acc = acc + a_ref[i].astype(F32)
                o_ref[...] = acc.astype(out_dtype)

    def in_spec(a):
        if a.ndim == 4:
            return pl.BlockSpec((s, 2, tr, c // 2), lambda i: (0, 0, i, 0))
        return pl.BlockSpec((s, tr, c), lambda i: (0, i, 0))

    return list(pl.pallas_call(
        body, name=name, grid=(r // tr,),
        in_specs=[in_spec(a) for a in arrs],
        out_specs=(pl.BlockSpec((tr, c), lambda i: (i, 0)),) * n,
        out_shape=(jax.ShapeDtypeStruct((r, c), out_dtype),) * n,
        compiler_params=_params(("arbitrary",)),
    )(*arrs))


def _adam_math(w, g, m, v):
    m_new = ADAM_B1 * m + (1.0 - ADAM_B1) * g
    v_new = ADAM_B2 * v + (1.0 - ADAM_B2) * (g * g)
    m_hat = m_new / (1.0 - ADAM_B1 ** ADAM_STEP)
    v_hat = v_new / (1.0 - ADAM_B2 ** ADAM_STEP)
    delta = -ADAM_LR * (m_hat / (jnp.sqrt(v_hat) + ADAM_EPS) + ADAM_WD * w)
    return delta, m_new, v_new


def _adam(name, wmv, g_parts):
    w0 = wmv[0][0]
    r, c = w0.shape[-2:]
    n_w = len(wmv)
    n_g = len(g_parts[0])
    tr = _tile(r, 256 if n_w == 1 else 88, 8)
    lead = w0.ndim == 3
    at = (lambda ref: ref.at[0]) if lead else (lambda ref: ref)
    n_in = 3 + n_g

    def body(*refs):
        for j in range(n_w):
            ins = refs[j * n_in:(j + 1) * n_in]
            outs = refs[n_w * n_in + 4 * j:n_w * n_in + 4 * j + 4]
            w_ref, m_ref, v_ref = [at(t) for t in ins[:3]]
            g_out, d_out, m_out, v_out = [at(t) for t in outs]
            g = ins[3][...].astype(F32)
            for gr in ins[4:]:
                g = g + gr[...].astype(F32)
            delta, m_new, v_new = _adam_math(w_ref[...], g, m_ref[...], v_ref[...])
            g_out[...] = g
            d_out[...] = delta
            m_out[...] = m_new
            v_out[...] = v_new

    spec = pl.BlockSpec((tr, c), lambda i: (i, 0))
    wspec = pl.BlockSpec((1, tr, c), lambda i: (0, i, 0)) if lead else spec
    shp = jax.ShapeDtypeStruct(w0.shape, F32)
    args = [t for (w, m, v), gp in zip(wmv, g_parts) for t in (w, m, v, *gp)]
    res = pl.pallas_call(
        body, name=name, grid=(r // tr,),
        in_specs=([wspec] * 3 + [spec] * n_g) * n_w, out_specs=(wspec,) * (4 * n_w), out_shape=(shp,) * (4 * n_w),
        compiler_params=_params(("arbitrary",)),
    )(*args)
    return [tuple(res[4 * j:4 * j + 4]) for j in range(n_w)]


SUB_ROWS = 32
FFN_BWD_ROWS = 416
FFN_FWD_ROWS = 832
RET_ROWS = 640


def _tile_parts(tm, d, head, x):
    nsub = tm // SUB_ROWS
    off = head.shape[0] // SUB_ROWS
    specs = [pl.BlockSpec(head.shape, lambda i, k: (0, 0))] + [
        pl.BlockSpec((SUB_ROWS, d), lambda i, k, j=j: (jnp.maximum(i * nsub + j - off, 0), 0)) for j in range(nsub)]

    def assemble(i, part_refs, h_sc):
        head_ref, x_refs = part_refs[0], part_refs[1:]
        for j in range(nsub):
            rows = slice(j * SUB_ROWS, (j + 1) * SUB_ROWS)
            val = x_refs[j][...]
            if j < off:
                val = jnp.where(i == 0, head_ref[rows, :], val)
            h_sc[rows, :] = val

    return specs, [head] + [x] * nsub, assemble


def _h_source(body, h, tm, d):
    if not isinstance(h, tuple):
        return body, [pl.BlockSpec((tm, d), lambda i, k: (i, 0))], [h], []
    specs, args, assemble = _tile_parts(tm, d, *h)
    n_h = len(specs)

    def with_parts(*refs):
        h_sc = refs[-1]

        @pl.when(pl.program_id(1) == 0)
        def _():
            assemble(pl.program_id(0), refs[:n_h], h_sc)

        body(h_sc, *refs[n_h:-1])

    return with_parts, specs, args, [pltpu.VMEM((tm, d), F32)]


def _ffn_fwd(name, h, nw, wg, wu, wd, plan=None, plan_args=()):
    lp, d = (h[0].shape[0] + h[1].shape[0], h[1].shape[1]) if isinstance(h, tuple) else h.shape
    nck, f, _ = wg.shape
    tm = _tile(lp, FFN_FWD_ROWS)
    last = nck // FFN_CPS - 1

    def body(h_ref, nw_ref, wg_ref, wu_ref, wd_ref, ho_ref, g_ref, u_ref, n_sc, acc_sc):
        k = pl.program_id(1)

        @pl.when(k == 0)
        def _():
            xh, _ = _rms_stats(h_ref[...])
            n_sc[...] = (xh * nw_ref[...]).astype(BF16)
            acc_sc[...] = jnp.zeros_like(acc_sc)

        n = n_sc[...]
        acc = acc_sc[...]
        for c in range(FFN_CPS):
            g = _dot_nt(n, wg_ref[c])
            u = _dot_nt(n, wu_ref[c])
            g_ref[c] = g.astype(BF16)
            u_ref[c] = u.astype(BF16)
            a = (g * _sigmoid(g) * u).astype(BF16)
            acc = acc + _dot(a, wd_ref[c])
        acc_sc[...] = acc

        @pl.when(k == last)
        def _():
            ho_ref[...] = h_ref[...] + FFN_RES * acc_sc[...]

    body, h_specs, h_args, h_scratch = _h_source(body, h, tm, d)
    w_fd = pl.BlockSpec((FFN_CPS, f, d), lambda i, k: (k, 0, 0))
    hid = pl.BlockSpec((FFN_CPS, tm, f), lambda i, k: (k, i, 0))
    return _pcall(
        body, name=name, grid=(lp // tm, nck // FFN_CPS), plan=plan, plan_args=plan_args,
        args=(*h_args, nw, wg, wu, wd),
        in_specs=h_specs + [pl.BlockSpec((1, d), lambda i, k: (0, 0)), w_fd, w_fd, w_fd],
        out_specs=(pl.BlockSpec((tm, d), lambda i, k: (i, 0)), hid, hid),
        out_shape=(jax.ShapeDtypeStruct((lp, d), F32),
                   jax.ShapeDtypeStruct((nck, lp, f), BF16),
                   jax.ShapeDtypeStruct((nck, lp, f), BF16)),
        scratch_shapes=[pltpu.VMEM((tm, d), BF16), pltpu.VMEM((tm, d), F32)] + h_scratch)


def _ffn_bwd_act(name, dh, h, nw, g, u, wg, wu, wd, plan=None, plan_args=()):
    lp, d = dh.shape
    nck, f, _ = wg.shape
    tm = _tile(lp, FFN_BWD_ROWS, SUB_ROWS)
    last = nck // FFN_CPS - 1

    def body(h_ref, dh_ref, nw_ref, g_ref, u_ref, wg_ref, wu_ref, wd_ref,
             dhi_ref, dnw_ref, n_ref, dacc_ref, a_ref, dg_ref, du_ref,
             xh_sc, r_sc, dn_sc):
        i = pl.program_id(0)
        k = pl.program_id(1)

        @pl.when(k == 0)
        def _():
            xh, r = _rms_stats(h_ref[...])
            xh_sc[...] = xh
            r_sc[...] = r
            n_ref[...] = (xh * nw_ref[...]).astype(BF16)
            dacc_ref[...] = (FFN_RES * dh_ref[...]).astype(BF16)
            dn_sc[...] = jnp.zeros_like(dn_sc)

        @pl.when(jnp.logical_and(i == 0, k == 0))
        def _():
            dnw_ref[...] = jnp.zeros_like(dnw_ref)

        dacc = dacc_ref[...]
        dn = dn_sc[...]
        for c in range(FFN_CPS):
            gv = g_ref[c].astype(F32)
            uv = u_ref[c].astype(F32)
            sg = _sigmoid(gv)
            sil = gv * sg
            da = _dot_nt(dacc, wd_ref[c])
            dgk = (da * uv * (sg * (1.0 + gv * (1.0 - sg)))).astype(BF16)
            duk = (da * sil).astype(BF16)
            a_ref[c] = (sil * uv).astype(BF16)
            dg_ref[c] = dgk
            du_ref[c] = duk
            dn = dn + _dot(dgk, wg_ref[c]) + _dot(duk, wu_ref[c])
        dn_sc[...] = dn

        @pl.when(k == last)
        def _():
            dnl = dn_sc[...]
            xh = xh_sc[...]
            dhi_ref[...] = dh_ref[...] + _rms_bwd(dnl, xh, r_sc[...], nw_ref[...])
            dnw_ref[...] += jnp.sum(dnl * xh, axis=0, keepdims=True)

    body, h_specs, h_args, h_scratch = _h_source(body, h, tm, d)
    row = pl.BlockSpec((tm, d), lambda i, k: (i, 0))
    vec = pl.BlockSpec((1, d), lambda i, k: (0, 0))
    hid = pl.BlockSpec((FFN_CPS, tm, f), lambda i, k: (k, i, 0))
    w_fd = pl.BlockSpec((FFN_CPS, f, d), lambda i, k: (k, 0, 0))
    rshape = jax.ShapeDtypeStruct((lp, d), BF16)
    hshape = jax.ShapeDtypeStruct((nck, lp, f), BF16)
    return _pcall(
        body, name=name, grid=(lp // tm, nck // FFN_CPS), plan=plan, plan_args=plan_args,
        args=(*h_args, dh, nw, g, u, wg, wu, wd),
        in_specs=h_specs + [row, vec, hid, hid, w_fd, w_fd, w_fd],
        out_specs=(row, vec, row, row, hid, hid, hid),
        out_shape=(jax.ShapeDtypeStruct((lp, d), F32), jax.ShapeDtypeStruct((1, d), F32),
                   rshape, rshape, hshape, hshape, hshape),
        scratch_shapes=[pltpu.VMEM((tm, d), F32), pltpu.VMEM((tm, 1), F32), pltpu.VMEM((tm, d), F32)] + h_scratch)


def _ffn_bwd_w(name, n, dacc, a, dg, du, plan=None, plan_args=()):
    lp, d = n.shape
    nck, _, f = a.shape
    tm = _tile(lp, BWD_W_ROWS)
    last = lp // tm - 1

    def body(n_ref, dacc_ref, a_ref, dg_ref, du_ref, dwg_ref, dwu_ref, dwd_ref, ag_sc, au_sc, ad_sc):
        i = pl.program_id(1)

        @pl.when(i == 0)
        def _():
            ag_sc[...] = jnp.zeros_like(ag_sc)
            au_sc[...] = jnp.zeros_like(au_sc)
            ad_sc[...] = jnp.zeros_like(ad_sc)

        nv = n_ref[...]
        ag_sc[...] += _dot_tn(dg_ref[0], nv)
        au_sc[...] += _dot_tn(du_ref[0], nv)
        ad_sc[...] += _dot_tn(a_ref[0], dacc_ref[...])

        @pl.when(i == last)
        def _():
            dwg_ref[0] = ag_sc[...].astype(BF16)
            dwu_ref[0] = au_sc[...].astype(BF16)
            dwd_ref[0] = ad_sc[...].astype(BF16)

    row = pl.BlockSpec((tm, d), lambda k, i: (i, 0))
    hid = pl.BlockSpec((1, tm, f), lambda k, i: (k, i, 0))
    w_fd = pl.BlockSpec((1, f, d), lambda k, i: (k, 0, 0))
    wshape = jax.ShapeDtypeStruct((nck, f, d), BF16)
    return _pcall(
        body, name=name, grid=(nck, lp // tm), plan=plan, plan_args=plan_args, args=(n, dacc, a, dg, du),
        in_specs=[row, row, hid, hid, hid], out_specs=(w_fd, w_fd, w_fd), out_shape=(wshape,) * 3,
        scratch_shapes=[pltpu.VMEM((f, d), F32)] * 3)


def _ffn_bwd_w_scatter(name, n, dacc, a, dg, du, chip, plan, plan_args):
    lp, d = n.shape
    nck, _, f = a.shape
    tm = _tile(lp, BWD_W_ROWS)
    last_i = lp // tm - 1
    n_w = 3
    n_p = plan.n

    def body(me_ref, n_ref, dacc_ref, a_ref, dg_ref, du_ref, *rest):
        p_ins = rest[:n_p]
        recv = rest[n_p:n_p + n_w]
        p_outs = rest[n_p + n_w:2 * n_p + n_w]
        acc = rest[2 * n_p + n_w:2 * n_p + 2 * n_w]
        stage, send_sems, recv_sems, loc_sems = rest[2 * n_p + 2 * n_w:2 * n_p + 2 * n_w + 4]
        p_sems = rest[2 * n_p + 2 * n_w + 4:]
        p = pl.program_id(0)
        i = pl.program_id(1)
        me = me_ref[0]
        c = lax.axis_index("c")

        def send(w, pos):
            kk = jnp.bitwise_xor(me, nck - 1 - pos)
            diff = jnp.bitwise_xor(kk, me)
            m = jnp.where(diff == 2, 0, jnp.where(diff == 1, 1, 2))
            return pltpu.make_async_remote_copy(
                src_ref=stage.at[lax.rem(pos, 2), w], dst_ref=recv[w].at[me],
                send_sem=send_sems.at[w * 3 + m], recv_sem=recv_sems.at[w * 3 + m],
                device_id=(lax.div(kk, 2), lax.rem(kk, 2), c), device_id_type=MESH)

        @pl.when(jnp.logical_and(p == 0, i == 0))
        def _():
            for cp in plan.copies(p_ins, p_outs, p_sems):
                cp.start()

        @pl.when(i == 0)
        def _():
            for t in acc:
                t[...] = jnp.zeros_like(t)

        nv = n_ref[...]
        acc[0][...] += _dot_tn(dg_ref[0], nv)
        acc[1][...] += _dot_tn(du_ref[0], nv)
        acc[2][...] += _dot_tn(a_ref[0], dacc_ref[...])

        @pl.when(jnp.logical_and(i == last_i, p >= 2))
        def _():
            for w in range(n_w):
                send(w, p - 2).wait_send()

        @pl.when(i == last_i)
        def _():
            for w in range(n_w):
                stage[lax.rem(p, 2), w] = acc[w][...].astype(BF16)

        @pl.when(jnp.logical_and(i == last_i, p < nck - 1))
        def _():
            for w in range(n_w):
                send(w, p).start()

        @pl.when(jnp.logical_and(i == last_i, p == nck - 1))
        def _():
            own = [pltpu.make_async_copy(stage.at[(nck - 1) % 2, w], recv[w].at[me], loc_sems.at[w])
                   for w in range(n_w)]
            for cp in own:
                cp.start()
            for w in range(n_w):
                send(w, nck - 2).wait_send()
            for cp in own:
                cp.wait()
            for w in range(n_w):
                for m in range(3):
                    pltpu.make_async_remote_copy(
                        src_ref=stage.at[0, w], dst_ref=recv[w].at[me],
                        send_sem=send_sems.at[w * 3 + m], recv_sem=recv_sems.at[w * 3 + m],
                        device_id=(0, 0, c), device_id_type=MESH).wait_recv()
            for cp in plan.copies(p_ins, p_outs, p_sems):
                cp.wait()

    chunk = lambda k, me_ref: jnp.bitwise_xor(me_ref[0], nck - 1 - k)
    row = pl.BlockSpec((tm, d), lambda k, i, me_ref: (i, 0))
    hid = pl.BlockSpec((1, tm, f), lambda k, i, me_ref: (chunk(k, me_ref), i, 0))
    wshape = jax.ShapeDtypeStruct((nck, f, d), BF16)
    res = pl.pallas_call(
        body, name=name,
        grid_spec=pltpu.PrefetchScalarGridSpec(
            num_scalar_prefetch=1, grid=(nck, lp // tm),
            in_specs=[row, row, hid, hid, hid] + [ANY_SPEC] * n_p,
            out_specs=(ANY_SPEC,) * (n_w + n_p),
            scratch_shapes=[pltpu.VMEM((f, d), F32)] * n_w + [
                pltpu.VMEM((2, n_w, f, d), BF16), pltpu.SemaphoreType.DMA((n_w * 3,)),
                pltpu.SemaphoreType.DMA((n_w * 3,)), pltpu.SemaphoreType.DMA((n_w,))] + plan.scratch()),
        out_shape=(wshape,) * n_w + plan.out_shape(),
        compiler_params=_params(("arbitrary", "arbitrary")),
    )(chip.reshape(1).astype(jnp.int32), n, dacc, a, dg, du, *plan_args)
    return list(res[:n_w]), list(res[n_w:])


def _inproj_fwd(h, nw, w_in, cosf, sinf, rw):
    lp, d = h.shape
    nck, _, ps = w_in.shape
    proj = nck * ps
    sw = proj - 4 * rw
    tm = _tile(lp, 640)
    scale = HEAD_DIM ** -0.5
    heads = rw // HEAD_DIM

    def body(h_ref, nw_ref, w_ref, cos_ref, sin_ref, n_ref, q_ref, k_ref, v_ref, g_ref, u_ref, p_sc):
        xh, _ = _rms_stats(h_ref[...])
        n = (xh * nw_ref[...]).astype(BF16)
        n_ref[...] = n
        for c in range(nck):
            p_sc[:, c * ps:(c + 1) * ps] = _dot(n, w_ref[c])
        cs = cos_ref[...]
        sn = sin_ref[...]
        for hh in range(heads):
            lo = hh * HEAD_DIM
            qh = p_sc[:, lo:lo + HEAD_DIM]
            q_ref[:, lo:lo + HEAD_DIM] = (qh * cs + pltpu.roll(qh, HEAD_DIM // 2, 1) * sn).astype(BF16)
            kh = p_sc[:, rw + lo:rw + lo + HEAD_DIM]
            k_ref[:, lo:lo + HEAD_DIM] = ((kh * cs + pltpu.roll(kh, HEAD_DIM // 2, 1) * sn) * scale).astype(BF16)
        v_ref[...] = p_sc[:, 2 * rw:3 * rw].astype(BF16)
        g_ref[...] = p_sc[:, 3 * rw:4 * rw]
        u_ref[...] = p_sc[:, 4 * rw:]

    row = lambda w: pl.BlockSpec((tm, w), lambda i: (i, 0))
    return pl.pallas_call(
        body, name="inproj_fwd", grid=(lp // tm,),
        in_specs=[row(d), pl.BlockSpec((1, d), lambda i: (0, 0)),
                  pl.BlockSpec((nck, d, ps), lambda i: (0, 0, 0)), row(HEAD_DIM), row(HEAD_DIM)],
        out_specs=(row(d), row(rw), row(rw), row(rw), row(rw), row(sw)),
        out_shape=(jax.ShapeDtypeStruct((lp, d), BF16),
                   jax.ShapeDtypeStruct((lp, rw), BF16),
                   jax.ShapeDtypeStruct((lp, rw), BF16),
                   jax.ShapeDtypeStruct((lp, rw), BF16),
                   jax.ShapeDtypeStruct((lp, rw), F32),
                   jax.ShapeDtypeStruct((lp, sw), F32)),
        scratch_shapes=[pltpu.VMEM((tm, proj), F32)],
        compiler_params=_params(("arbitrary",)),
    )(h, nw, w_in, cosf, sinf)


def _inproj_bwd(dh, h, nw, n, w_in, dq, dk, dv, dg, du):
    lp, d = h.shape
    nck, _, ps = w_in.shape
    rw = dq.shape[1]
    sw = du.shape[1]
    proj = nck * ps
    tm = _tile(lp, 640)
    last = lp // tm - 1

    def gather_dproj(p_sc, dq_ref, dk_ref, dv_ref, dg_ref, du_ref):
        p_sc[:, 0:rw] = dq_ref[...]
        p_sc[:, rw:2 * rw] = dk_ref[...]
        p_sc[:, 2 * rw:3 * rw] = dv_ref[...]
        p_sc[:, 3 * rw:4 * rw] = dg_ref[...]
        p_sc[:, 4 * rw:] = du_ref[...]

    def act_body(dh_ref, h_ref, nw_ref, w_ref, dq_ref, dk_ref, dv_ref, dg_ref, du_ref, dhi_ref, dnw_ref, p_sc):
        i = pl.program_id(0)

        @pl.when(i == 0)
        def _():
            dnw_ref[...] = jnp.zeros_like(dnw_ref)

        gather_dproj(p_sc, dq_ref, dk_ref, dv_ref, dg_ref, du_ref)
        dn = jnp.zeros((tm, d), F32)
        for c in range(nck):
            dn = dn + _dot_nt(p_sc[:, c * ps:(c + 1) * ps], w_ref[c])
        xh, r = _rms_stats(h_ref[...])
        dhi_ref[...] = dh_ref[...] + _rms_bwd(dn, xh, r, nw_ref[...])
        dnw_ref[...] += jnp.sum(dn * xh, axis=0, keepdims=True)

    def w_body(n_ref, dq_ref, dk_ref, dv_ref, dg_ref, du_ref, dw_ref, p_sc, acc_sc):
        i = pl.program_id(0)

        @pl.when(i == 0)
        def _():
            acc_sc[...] = jnp.zeros_like(acc_sc)

        gather_dproj(p_sc, dq_ref, dk_ref, dv_ref, dg_ref, du_ref)
        nv = n_ref[...]
        for c in range(nck):
            acc_sc[c] += _dot_tn(nv, p_sc[:, c * ps:(c + 1) * ps])

        @pl.when(i == last)
        def _():
            dw_ref[...] = acc_sc[...].astype(BF16)

    row = lambda w: pl.BlockSpec((tm, w), lambda i: (i, 0))
    vec = pl.BlockSpec((1, d), lambda i: (0, 0))
    wsp = pl.BlockSpec((nck, d, ps), lambda i: (0, 0, 0))
    dproj_specs = [row(rw), row(rw), row(rw), row(rw), row(sw)]
    dhi, dnw = pl.pallas_call(
        act_body, name="inproj_bwd_act", grid=(lp // tm,),
        in_specs=[row(d), row(d), vec, wsp] + dproj_specs,
        out_specs=(row(d), vec),
        out_shape=(jax.ShapeDtypeStruct((lp, d), F32), jax.ShapeDtypeStruct((1, d), F32)),
        scratch_shapes=[pltpu.VMEM((tm, proj), BF16)],
        compiler_params=_params(("arbitrary",)),
    )(dh, h, nw, w_in, dq, dk, dv, dg, du)
    dw = pl.pallas_call(
        w_body, name="inproj_bwd_w", grid=(lp // tm,),
        in_specs=[row(d)] + dproj_specs,
        out_specs=wsp, out_shape=jax.ShapeDtypeStruct((nck, d, ps), BF16),
        scratch_shapes=[pltpu.VMEM((tm, proj), BF16), pltpu.VMEM((nck, d, ps), F32)],
        compiler_params=_params(("arbitrary",)),
    )(n, dq, dk, dv, dg, du)
    return dhi, dnw, dw


def _retention_tables(rc):
    h = jnp.arange(RET_HEADS, dtype=F32)
    log_g = jnp.log(1.0 - 2.0 ** (-5.0 - h))
    i = jnp.arange(rc)
    diff = i[:, None] - i[None, :]
    dec = jnp.where(diff[None] >= 0,
                    jnp.exp(log_g[:, None, None] * jnp.maximum(diff, 0)[None].astype(F32)), 0.0)
    pos = jnp.arange(rc, dtype=F32)
    wq = jnp.exp(log_g[:, None] * (pos + 1.0)[None])
    wk = jnp.exp(log_g[:, None] * (rc - 1 - pos)[None])
    gch = jnp.exp(log_g * rc)
    ones = jnp.ones((1, 1, HEAD_DIM), F32)
    return (dec, wq[:, :, None] * ones, wk[:, :, None] * ones,
            gch[:, None, None] * jnp.ones((1, 8, HEAD_DIM), F32))


def _head_norm(o):
    mu = jnp.mean(o, axis=-1, keepdims=True)
    oc = o - mu
    r = lax.rsqrt(jnp.mean(oc * oc, axis=-1, keepdims=True) + EPS)
    return oc * r, r


def _ret_fwd(q, k, v, g, rnw, tables):
    lp, rw = q.shape
    heads = rw // HEAD_DIM
    rc = tables[0].shape[1]
    nch = lp // rc
    dec, wq, wk, gch = tables

    def body(q_ref, k_ref, v_ref, g_ref, w_ref, dec_ref, wq_ref, wk_ref, gch_ref,
             o_ref, ret_ref, sp_ref, s_sc):
        n = pl.program_id(0)

        @pl.when(n == 0)
        def _():
            s_sc[...] = jnp.zeros_like(s_sc)

        cols = [slice(hh * HEAD_DIM, (hh + 1) * HEAD_DIM) for hh in range(heads)]
        s_ins = [s_sc[hh] for hh in range(heads)]
        outs = []
        for hh, cs in enumerate(cols):
            qv, kv, vv = q_ref[:, cs], k_ref[:, cs], v_ref[:, cs]
            s_in = s_ins[hh]
            a = _dot_nt(qv, kv) * dec_ref[hh]
            qw = (qv.astype(F32) * wq_ref[hh]).astype(BF16)
            kw = (kv.astype(F32) * wk_ref[hh]).astype(BF16)
            o = _dot(a.astype(BF16), vv) + _dot(qw, s_in.astype(BF16))
            s_new = gch_ref[hh, 0:1, :] * s_in + _dot_tn(kw, vv)
            xh, _ = _head_norm(o)
            gv = g_ref[:, cs]
            outs.append((o, s_new, (gv * _sigmoid(gv) * (xh * w_ref[:, cs])).astype(BF16)))
        for hh, cs in enumerate(cols):
            o, s_new, ret = outs[hh]
            sp_ref[hh, 0] = s_ins[hh]
            s_sc[hh] = s_new
            o_ref[:, cs] = o
            ret_ref[:, cs] = ret

    blk = pl.BlockSpec((rc, rw), lambda n: (n, 0))
    tab = pl.BlockSpec((heads, rc, HEAD_DIM), lambda n: (0, 0, 0))
    dtab = pl.BlockSpec((heads, rc, rc), lambda n: (0, 0, 0))
    return pl.pallas_call(
        body, name="retention_fwd", grid=(nch,),
        in_specs=[blk, blk, blk, blk, pl.BlockSpec((1, rw), lambda n: (0, 0)),
                  dtab, tab, tab, pl.BlockSpec((heads, 8, HEAD_DIM), lambda n: (0, 0, 0))],
        out_specs=(blk, blk, pl.BlockSpec((heads, 1, HEAD_DIM, HEAD_DIM), lambda n: (0, n, 0, 0))),
        out_shape=(jax.ShapeDtypeStruct((lp, rw), F32),
                   jax.ShapeDtypeStruct((lp, rw), BF16),
                   jax.ShapeDtypeStruct((heads, nch, HEAD_DIM, HEAD_DIM), F32)),
        scratch_shapes=[pltpu.VMEM((heads, HEAD_DIM, HEAD_DIM), F32)],
        compiler_params=_params(("arbitrary",)),
    )(q, k, v, g, rnw, dec, wq, wk, gch)


def _ret_bwd(dret, q, k, v, g, o, sprev, rnw, tables, cosf, sinf):
    lp, rw = q.shape
    heads = rw // HEAD_DIM
    rc = tables[0].shape[1]
    nch = lp // rc
    dec, wq, wk, gch = tables
    scale = HEAD_DIM ** -0.5
    half = HEAD_DIM // 2

    def body(dret_ref, q_ref, k_ref, v_ref, g_ref, o_ref, sp_ref, w_ref, dec_ref, wq_ref, wk_ref, gch_ref,
             cos_ref, sin_ref, dq_ref, dk_ref, dv_ref, dg_ref, dw_ref, ds_sc):
        n = pl.program_id(0)

        @pl.when(n == 0)
        def _():
            ds_sc[...] = jnp.zeros_like(ds_sc)
            dw_ref[...] = jnp.zeros_like(dw_ref)

        cosv = cos_ref[...]
        sinv = sin_ref[...]
        cols = [slice(hh * HEAD_DIM, (hh + 1) * HEAD_DIM) for hh in range(heads)]
        ds_ins = [ds_sc[hh] for hh in range(heads)]
        dw_ins = [dw_ref[:, cs] for cs in cols]
        outs = []
        for hh, cs in enumerate(cols):
            qv, kv, vv = q_ref[:, cs], k_ref[:, cs], v_ref[:, cs]
            gv = g_ref[:, cs]
            dr = dret_ref[:, cs]
            w = w_ref[:, cs]
            sg = _sigmoid(gv)
            sil = gv * sg
            xh, r = _head_norm(o_ref[:, cs])
            dgate = (dr * (xh * w) * (sg * (1.0 + gv * (1.0 - sg)))).astype(BF16)
            dyw = dr * sil
            dw_new = dw_ins[hh] + jnp.sum(dyw * xh, axis=0, keepdims=True)
            dxh = dyw * w
            do = r * (dxh - jnp.mean(dxh, axis=-1, keepdims=True)
                      - xh * jnp.mean(dxh * xh, axis=-1, keepdims=True))
            dob = do.astype(BF16)
            dmask = dec_ref[hh]
            wqv = wq_ref[hh]
            wkv = wk_ref[hh]
            a = (_dot_nt(qv, kv) * dmask).astype(BF16)
            da = (_dot_nt(dob, vv) * dmask).astype(BF16)
            qw = (qv.astype(F32) * wqv).astype(BF16)
            kw = (kv.astype(F32) * wkv).astype(BF16)
            s_in = sp_ref[hh, 0].astype(BF16)
            ds = ds_ins[hh]
            dsb = ds.astype(BF16)
            dq = _dot(da, kv) + _dot_nt(dob, s_in) * wqv
            dk = _dot_tn(da, qv) + _dot_nt(vv, dsb) * wkv
            dv = _dot_tn(a, dob) + _dot(kw, dsb)
            ds_new = gch_ref[hh, 0:1, :] * ds + _dot_tn(qw, dob)
            outs.append((dgate, dw_new, ds_new,
                         (dq * cosv + pltpu.roll(dq * sinv, half, 1)).astype(BF16),
                         ((dk * cosv + pltpu.roll(dk * sinv, half, 1)) * scale).astype(BF16),
                         dv.astype(BF16)))
        for hh, cs in enumerate(cols):
            dgate, dw_new, ds_new, dqv, dkv, dvv = outs[hh]
            dg_ref[:, cs] = dgate
            dw_ref[:, cs] = dw_new
            ds_sc[hh] = ds_new
            dq_ref[:, cs] = dqv
            dk_ref[:, cs] = dkv
            dv_ref[:, cs] = dvv

    blk = pl.BlockSpec((rc, rw), lambda n: (nch - 1 - n, 0))
    tab = pl.BlockSpec((heads, rc, HEAD_DIM), lambda n: (0, 0, 0))
    dtab = pl.BlockSpec((heads, rc, rc), lambda n: (0, 0, 0))
    wsp = pl.BlockSpec((1, rw), lambda n: (0, 0))
    pos = pl.BlockSpec((rc, HEAD_DIM), lambda n: (nch - 1 - n, 0))
    bshape = jax.ShapeDtypeStruct((lp, rw), BF16)
    return pl.pallas_call(
        body, name="retention_bwd", grid=(nch,),
        in_specs=[blk, blk, blk, blk, blk, blk,
                  pl.BlockSpec((heads, 1, HEAD_DIM, HEAD_DIM), lambda n: (0, nch - 1 - n, 0, 0)),
                  wsp, dtab, tab, tab, pl.BlockSpec((heads, 8, HEAD_DIM), lambda n: (0, 0, 0)), pos, pos],
        out_specs=(blk, blk, blk, blk, wsp),
        out_shape=(bshape, bshape, bshape, bshape, jax.ShapeDtypeStruct((1, rw), F32)),
        scratch_shapes=[pltpu.VMEM((heads, HEAD_DIM, HEAD_DIM), F32)],
        compiler_params=_params(("arbitrary",)),
    )(dret, q, k, v, g, o, sprev, rnw, dec, wq, wk, gch, cosf, sinf)


SCAN_CW = 512


def _s5_prepare(lam_re, lam_im, log_dt, b_re, b_im):
    dt = jnp.exp(log_dt)[:, None]
    er = jnp.exp(lam_re * dt)
    ar = er * jnp.cos(lam_im * dt)
    ai = er * jnp.sin(lam_im * dt)
    den = lam_re * lam_re + lam_im * lam_im
    fr = ((ar - 1.0) * lam_re + ai * lam_im) / den
    fi = (ai * lam_re - (ar - 1.0) * lam_im) / den
    bbr = fr[..., None] * b_re - fi[..., None] * b_im
    bbi = fr[..., None] * b_im + fi[..., None] * b_re
    return ar, ai, bbr, bbi


def _blockdiag_in(t):
    g, p, n = t.shape
    gs = g // N_SEC
    t = t.reshape(N_SEC, gs, p, n)
    eye = jnp.eye(gs, dtype=t.dtype)
    return jnp.einsum("sgpn,gh->sgphn", t, eye).reshape(N_SEC, gs * p, gs * n)


def _blockdiag_out(m, g, p, n):
    gs = g // N_SEC
    m = m.reshape(N_SEC, gs, p, gs, n)
    eye = jnp.eye(gs, dtype=m.dtype)
    return jnp.einsum("sgphn,gh->sgpn", m, eye).reshape(g, p, n)


def _scan_step(xr_ref, xi_ref, r0, prev, ar_ref, ai_ref, conj, ncols):
    new = []
    for cc in range(ncols // SCAN_CW):
        cs = pl.ds(cc * SCAN_CW, SCAN_CW)
        pr, pi = prev[cc]
        ar = ar_ref[:, cs]
        ai = ai_ref[:, cs]
        if conj:
            nr = ar * pr + ai * pi
            ni = ar * pi - ai * pr
        else:
            nr = ar * pr - ai * pi
            ni = ar * pi + ai * pr
        xr = xr_ref[pl.ds(r0, 8), cs] + nr
        xi = xi_ref[pl.ds(r0, 8), cs] + ni
        xr_ref[pl.ds(r0, 8), cs] = xr
        xi_ref[pl.ds(r0, 8), cs] = xi
        new.append((xr, xi))
    return new


def _scan_chunks(ncols):
    return [pl.ds(cc * SCAN_CW, SCAN_CW) for cc in range(ncols // SCAN_CW)]


def _flat(pairs):
    return tuple(t for p in pairs for t in p)


def _pairs(flat):
    return [(flat[2 * k], flat[2 * k + 1]) for k in range(len(flat) // 2)]


def _shift_rows(z, down):
    row = lax.broadcasted_iota(jnp.int32, z.shape, 0)
    if down:
        return jnp.where(row == 0, 0.0, pltpu.roll(z, 1, 0))
    return jnp.where(row == N_SEG - 1, 0.0, pltpu.roll(z, N_SEG - 1, 0))


def _s5_fwd(u, bsr, bsi, csr, csi, a8r, a8i, al8r, al8i, d, gluw, glub, nw, jb):
    lp, sw = u.shape
    ns = a8r.shape[1]
    rows = N_SEG * jb
    nblk = lp // rows
    secw = sw // N_SEC
    secn = ns // N_SEC

    def local_scan(u_ref, bsr_ref, bsi_ref, ar_ref, ai_ref, xr_ref, xi_ref, pr_sc, pi_sc):
        for s in range(N_SEC):
            ub = u_ref[:, s * secw:(s + 1) * secw].astype(BF16)
            xr_ref[:, s * secn:(s + 1) * secn] = _dot(ub, bsr_ref[s])
            xi_ref[:, s * secn:(s + 1) * secn] = _dot(ub, bsi_ref[s])
        prev = [(pr_sc[:, cs], pi_sc[:, cs]) for cs in _scan_chunks(ns)]
        prev = _scan_step(xr_ref, xi_ref, 0, prev, ar_ref, ai_ref, False, ns)

        def step(j, carry):
            r0 = pl.multiple_of(j * 8, 8)
            return _flat(_scan_step(xr_ref, xi_ref, r0, _pairs(carry), ar_ref, ai_ref, False, ns))

        last = _pairs(lax.fori_loop(1, jb, step, _flat(prev)))
        for cs, (vr, vi) in zip(_scan_chunks(ns), last):
            pr_sc[:, cs] = vr
            pi_sc[:, cs] = vi

    def carry_body(u_ref, bsr_ref, bsi_ref, ar_ref, ai_ref, alr_ref, ali_ref, cr_ref, ci_ref,
                   xr_sc, xi_sc, pr_sc, pi_sc):
        b = pl.program_id(0)

        @pl.when(b == 0)
        def _():
            pr_sc[...] = jnp.zeros_like(pr_sc)
            pi_sc[...] = jnp.zeros_like(pi_sc)

        local_scan(u_ref, bsr_ref, bsi_ref, ar_ref, ai_ref, xr_sc, xi_sc, pr_sc, pi_sc)

        @pl.when(b == nblk - 1)
        def _():
            er = _shift_rows(pr_sc[...], True)
            ei = _shift_rows(pi_sc[...], True)
            alr, ali = alr_ref[...], ali_ref[...]
            cr, ci = er, ei
            for _ in range(N_SEG - 2):
                sr = _shift_rows(cr, True)
                si = _shift_rows(ci, True)
                cr = er + alr * sr - ali * si
                ci = ei + alr * si + ali * sr
            cr_ref[...] = cr
            ci_ref[...] = ci

    ublk = pl.BlockSpec((rows, sw), lambda b: (b, 0))
    bspec = pl.BlockSpec((N_SEC, secw, secn), lambda b: (0, 0, 0))
    cspec = pl.BlockSpec((N_SEC, secn, secw), lambda b: (0, 0, 0))
    s8 = pl.BlockSpec((N_SEG, ns), lambda b: (0, 0))
    vec = pl.BlockSpec((1, sw), lambda b: (0, 0))
    s8shape = jax.ShapeDtypeStruct((N_SEG, ns), F32)
    c0r, c0i = pl.pallas_call(
        carry_body, name="s5_fwd_carry", grid=(nblk,),
        in_specs=[ublk, bspec, bspec, s8, s8, s8, s8],
        out_specs=(s8, s8), out_shape=(s8shape, s8shape),
        scratch_shapes=[pltpu.VMEM((rows, ns), F32), pltpu.VMEM((rows, ns), F32),
                        pltpu.VMEM((N_SEG, ns), F32), pltpu.VMEM((N_SEG, ns), F32)],
        compiler_params=_params(("arbitrary",)),
    )(u, bsr, bsi, a8r, a8i, al8r, al8i)

    def main_body(u_ref, bsr_ref, bsi_ref, csr_ref, csi_ref, ar_ref, ai_ref, c0r_ref, c0i_ref,
                  d_ref, gw_ref, gb_ref, nw_ref, xr_ref, xi_ref, yp_ref, out_ref, pr_sc, pi_sc):
        b = pl.program_id(0)

        @pl.when(b == 0)
        def _():
            pr_sc[...] = c0r_ref[...]
            pi_sc[...] = c0i_ref[...]

        local_scan(u_ref, bsr_ref, bsi_ref, ar_ref, ai_ref, xr_ref, xi_ref, pr_sc, pi_sc)
        for s in range(N_SEC):
            xs = pl.ds(s * secn, secn)
            us = pl.ds(s * secw, secw)
            y = _dot(xr_ref[:, xs].astype(BF16), csr_ref[s]) + _dot(xi_ref[:, xs].astype(BF16), csi_ref[s])
            yp_ref[:, us] = y + d_ref[:, us] * u_ref[:, us]
        yp = yp_ref[...]
        t = jnp.tanh(GELU_K0 * (yp + GELU_K1 * yp * yp * yp))
        y1 = 0.5 * yp * (1.0 + t)
        z = _dot(y1.astype(BF16), gw_ref[...]) + gb_ref[...]
        y2 = y1 * _sigmoid(z)
        xh, _ = _rms_stats(y2)
        out_ref[...] = (xh * nw_ref[...]).astype(BF16)

    xblk = pl.BlockSpec((rows, ns), lambda b: (b, 0))
    xr, xi, yp, out = pl.pallas_call(
        main_body, name="s5_fwd", grid=(nblk,),
        in_specs=[ublk, bspec, bspec, cspec, cspec, s8, s8, s8, s8, vec,
                  pl.BlockSpec((sw, sw), lambda b: (0, 0)), vec, vec],
        out_specs=(xblk, xblk, ublk, ublk),
        out_shape=(jax.ShapeDtypeStruct((lp, ns), F32), jax.ShapeDtypeStruct((lp, ns), F32),
                   jax.ShapeDtypeStruct((lp, sw), F32), jax.ShapeDtypeStruct((lp, sw), BF16)),
        scratch_shapes=[pltpu.VMEM((N_SEG, ns), F32), pltpu.VMEM((N_SEG, ns), F32)],
        compiler_params=_params(("arbitrary",)),
    )(u, bsr, bsi, csr, csi, a8r, a8i, c0r, c0i, d, gluw, glub, nw)
    return xr, xi, c0r, c0i, yp, out


def _s5_bwd(dout, u, yp, xr, xi, c0r, c0i, bsrt, bsit, csrt, csit, a8r, a8i, al8r, al8i, d, gluw, glub, nw, jb):
    lp, sw = u.shape
    ns = a8r.shape[1]
    rows = N_SEG * jb
    nblk = lp // rows
    secw = sw // N_SEC
    secn = ns // N_SEC

    def rowwise_bwd(dout_ref, yp_ref, gw_ref, gb_ref, nw_ref):
        ypv = yp_ref[...]
        t = jnp.tanh(GELU_K0 * (ypv + GELU_K1 * ypv * ypv * ypv))
        y1 = 0.5 * ypv * (1.0 + t)
        dgelu = 0.5 * (1.0 + t) + 0.5 * ypv * (1.0 - t * t) * GELU_K0 * (1.0 + 3.0 * GELU_K1 * ypv * ypv)
        gw = gw_ref[...]
        y1b = y1.astype(BF16)
        sg = _sigmoid(_dot(y1b, gw) + gb_ref[...])
        xh, r = _rms_stats(y1 * sg)
        dov = dout_ref[...]
        dy2 = _rms_bwd(dov, xh, r, nw_ref[...])
        dz = dy2 * y1 * sg * (1.0 - sg)
        dzb = dz.astype(BF16)
        dy1 = dy2 * sg + _dot_nt(dzb, gw)
        return dy1 * dgelu, dov * xh, y1b, dzb, dz

    def lam_scan(dyp_of, csrt_ref, csit_ref, ar_ref, ai_ref, lr_sc, li_sc, nr_sc, ni_sc, extra):
        for s in range(N_SEC):
            db = dyp_of(s)
            lr_sc[:, s * secn:(s + 1) * secn] = _dot(db, csrt_ref[s])
            li_sc[:, s * secn:(s + 1) * secn] = _dot(db, csit_ref[s])
        top = rows - 8
        prev = [(nr_sc[:, cs], ni_sc[:, cs]) for cs in _scan_chunks(ns)]
        prev = _scan_step(lr_sc, li_sc, top, prev, ar_ref, ai_ref, True, ns)
        extra(top, pl.ds(top - 8, 8))

        def step(jj, carry):
            r0 = pl.multiple_of((jb - 1 - jj) * 8, 8)
            rp = pl.multiple_of((jb - 2 - jj) * 8, 8)
            new = _scan_step(lr_sc, li_sc, r0, _pairs(carry), ar_ref, ai_ref, True, ns)
            extra(r0, pl.ds(rp, 8))
            return _flat(new)

        prev = _pairs(lax.fori_loop(1, jb - 1, step, _flat(prev)))
        last = _scan_step(lr_sc, li_sc, 0, prev, ar_ref, ai_ref, True, ns)
        extra(0, None)
        for cs, (vr, vi) in zip(_scan_chunks(ns), last):
            nr_sc[:, cs] = vr
            ni_sc[:, cs] = vi

    def carry_body(dout_ref, yp_ref, u_ref, gw_ref, gb_ref, nw_ref, csrt_ref, csit_ref, ar_ref, ai_ref,
                   alr_ref, ali_ref, cr_ref, ci_ref, dyp_ref, dnw_ref, dgw_ref, dgb_ref, dd_ref,
                   lr_sc, li_sc, nr_sc, ni_sc):
        b = pl.program_id(0)

        @pl.when(b == 0)
        def _():
            nr_sc[...] = jnp.zeros_like(nr_sc)
            ni_sc[...] = jnp.zeros_like(ni_sc)
            for ref in (dnw_ref, dgw_ref, dgb_ref, dd_ref):
                ref[...] = jnp.zeros_like(ref)

        dyp, dnw_rows, y1b, dzb, dz = rowwise_bwd(dout_ref, yp_ref, gw_ref, gb_ref, nw_ref)
        dnw_ref[...] += jnp.sum(dnw_rows, axis=0, keepdims=True)
        dgw_ref[...] += _dot_tn(y1b, dzb)
        dgb_ref[...] += jnp.sum(dz, axis=0, keepdims=True)
        dd_ref[...] += jnp.sum(dyp * u_ref[...], axis=0, keepdims=True)
        dyp_ref[...] = dyp.astype(BF16)
        lam_scan(lambda s: dyp_ref[:, s * secw:(s + 1) * secw], csrt_ref, csit_ref, ar_ref, ai_ref,
                 lr_sc, li_sc, nr_sc, ni_sc, lambda r0, prev_rows: None)

        @pl.when(b == nblk - 1)
        def _():
            fr = _shift_rows(nr_sc[...], False)
            fi = _shift_rows(ni_sc[...], False)
            alr, ali = alr_ref[...], ali_ref[...]
            cr, ci = fr, fi
            for _ in range(N_SEG - 2):
                sr = _shift_rows(cr, False)
                si = _shift_rows(ci, False)
                cr = fr + alr * sr + ali * si
                ci = fi + alr * si - ali * sr
            cr_ref[...] = cr
            ci_ref[...] = ci

    rev = lambda b: (nblk - 1 - b, 0)
    ublk = pl.BlockSpec((rows, sw), rev)
    xblk = pl.BlockSpec((rows, ns), rev)
    s8 = pl.BlockSpec((N_SEG, ns), lambda b: (0, 0))
    vec = pl.BlockSpec((1, sw), lambda b: (0, 0))
    gws = pl.BlockSpec((sw, sw), lambda b: (0, 0))
    btspec = pl.BlockSpec((N_SEC, secn, secw), lambda b: (0, 0, 0))
    ctspec = pl.BlockSpec((N_SEC, secw, secn), lambda b: (0, 0, 0))
    s8shape = jax.ShapeDtypeStruct((N_SEG, ns), F32)
    lcr, lci, dyp_all, d_nw, d_gw, d_gb, d_d = pl.pallas_call(
        carry_body, name="s5_bwd_carry", grid=(nblk,),
        in_specs=[ublk, ublk, ublk, gws, vec, vec, ctspec, ctspec, s8, s8, s8, s8],
        out_specs=(s8, s8, ublk, vec, gws, vec, vec),
        out_shape=(s8shape, s8shape, jax.ShapeDtypeStruct((lp, sw), BF16), jax.ShapeDtypeStruct((1, sw), F32),
                   jax.ShapeDtypeStruct((sw, sw), F32), jax.ShapeDtypeStruct((1, sw), F32),
                   jax.ShapeDtypeStruct((1, sw), F32)),
        scratch_shapes=[pltpu.VMEM((rows, ns), F32), pltpu.VMEM((rows, ns), F32),
                        pltpu.VMEM((N_SEG, ns), F32), pltpu.VMEM((N_SEG, ns), F32)],
        compiler_params=_params(("arbitrary",)),
    )(dout, yp, u, gluw, glub, nw, csrt, csit, a8r, a8i, al8r, al8i)

    def main_body(dyp_sc, u_ref, xr_ref, xi_ref, xtr_ref, xti_ref, c0r_ref, c0i_ref, lcr_ref, lci_ref,
                  d_ref, bsrt_ref, bsit_ref, csrt_ref, csit_ref, ar_ref, ai_ref,
                  du_ref, dcr_ref, dci_ref, dbr_ref, dbi_ref, dar_ref, dai_ref,
                  lr_sc, li_sc, nr_sc, ni_sc):
        b = pl.program_id(0)

        @pl.when(b == 0)
        def _():
            nr_sc[...] = lcr_ref[...]
            ni_sc[...] = lci_ref[...]
            for ref in (dcr_ref, dci_ref, dbr_ref, dbi_ref, dar_ref, dai_ref):
                ref[...] = jnp.zeros_like(ref)

        for s in range(N_SEC):
            db = dyp_sc[:, s * secw:(s + 1) * secw]
            xs = pl.ds(s * secn, secn)
            dcr_ref[s] += _dot_tn(xr_ref[:, xs].astype(BF16), db)
            dci_ref[s] += _dot_tn(xi_ref[:, xs].astype(BF16), db)

        first = b == nblk - 1

        def acc_da(r0, prev_rows):
            for cc in range(ns // SCAN_CW):
                cs = pl.ds(cc * SCAN_CW, SCAN_CW)
                lr = lr_sc[pl.ds(r0, 8), cs]
                li = li_sc[pl.ds(r0, 8), cs]
                if prev_rows is None:
                    xpr = jnp.where(first, c0r_ref[:, cs], xtr_ref[:, cs])
                    xpi = jnp.where(first, c0i_ref[:, cs], xti_ref[:, cs])
                else:
                    xpr = xr_ref[prev_rows, cs]
                    xpi = xi_ref[prev_rows, cs]
                dar_ref[:, cs] += lr * xpr + li * xpi
                dai_ref[:, cs] += li * xpr - lr * xpi

        lam_scan(lambda s: dyp_sc[:, s * secw:(s + 1) * secw], csrt_ref, csit_ref, ar_ref, ai_ref,
                 lr_sc, li_sc, nr_sc, ni_sc, acc_da)

        for s in range(N_SEC):
            xs = pl.ds(s * secn, secn)
            us = pl.ds(s * secw, secw)
            lrb = lr_sc[:, xs].astype(BF16)
            lib = li_sc[:, xs].astype(BF16)
            du = _dot(lrb, bsrt_ref[s]) + _dot(lib, bsit_ref[s]) + d_ref[:, us] * dyp_sc[:, us].astype(F32)
            du_ref[:, us] = du.astype(BF16)
            ub = u_ref[:, us].astype(BF16)
            dbr_ref[s] += _dot_tn(ub, lrb)
            dbi_ref[s] += _dot_tn(ub, lib)

    tail = pl.BlockSpec((N_SEG, ns), lambda b: (jnp.maximum((nblk - 1 - b) * jb - 1, 0), 0))
    acc_c = pl.BlockSpec((N_SEC, secn, secw), lambda b: (0, 0, 0))
    acc_b = pl.BlockSpec((N_SEC, secw, secn), lambda b: (0, 0, 0))
    du, dcr, dci, dbr, dbi, dar, dai = pl.pallas_call(
        main_body, name="s5_bwd", grid=(nblk,),
        in_specs=[ublk, ublk, xblk, xblk, tail, tail, s8, s8, s8, s8,
                  vec, btspec, btspec, ctspec, ctspec, s8, s8],
        out_specs=(ublk, acc_c, acc_c, acc_b, acc_b, s8, s8),
        out_shape=(jax.ShapeDtypeStruct((lp, sw), BF16),
                   jax.ShapeDtypeStruct((N_SEC, secn, secw), F32),
                   jax.ShapeDtypeStruct((N_SEC, secn, secw), F32),
                   jax.ShapeDtypeStruct((N_SEC, secw, secn), F32),
                   jax.ShapeDtypeStruct((N_SEC, secw, secn), F32),
                   s8shape, s8shape),
        scratch_shapes=[pltpu.VMEM((rows, ns), F32), pltpu.VMEM((rows, ns), F32),
                        pltpu.VMEM((N_SEG, ns), F32), pltpu.VMEM((N_SEG, ns), F32)],
        compiler_params=_params(("arbitrary",)),
    )(dyp_all, u, xr, xi, xr, xi, c0r, c0i, lcr, lci, d, bsrt, bsit, csrt, csit, a8r, a8i)
    return du, d_nw, d_gw, d_gb, d_d, dcr, dci, dbr, dbi, dar, dai


def _outproj_fwd(h, ret, ssm, wo):
    lp, d = h.shape
    nck, rs, _ = wo.shape
    rw = ret.shape[1]
    tm = _tile(lp, 640)
    per = rw // rs

    def body(h_ref, ret_ref, ssm_ref, w_ref, o_ref):
        acc = h_ref[...]
        for c in range(nck):
            src = ret_ref if c < per else ssm_ref
            lo = (c % per) * rs
            acc = acc + _dot(src[:, lo:lo + rs], w_ref[c])
        o_ref[...] = acc

    row = lambda w: pl.BlockSpec((tm, w), lambda i: (i, 0))
    return pl.pallas_call(
        body, name="outproj_fwd", grid=(lp // tm,),
        in_specs=[row(d), row(rw), row(ssm.shape[1]), pl.BlockSpec((nck, rs, d), lambda i: (0, 0, 0))],
        out_specs=row(d), out_shape=jax.ShapeDtypeStruct((lp, d), F32),
        compiler_params=_params(("arbitrary",)),
    )(h, ret, ssm, wo)


def _outproj_bwd(dh, ret, ssm, wo):
    lp, d = dh.shape
    nck, rs, _ = wo.shape
    rw = ret.shape[1]
    sw = ssm.shape[1]
    tm = _tile(lp, 640)
    per = rw // rs
    last = lp // tm - 1

    def body(dh_ref, ret_ref, ssm_ref, w_ref, dret_ref, dssm_ref, dw_ref, acc_sc):
        i = pl.program_id(0)

        @pl.when(i == 0)
        def _():
            acc_sc[...] = jnp.zeros_like(acc_sc)

        dhb = dh_ref[...].astype(BF16)
        for c in range(nck):
            src, dst = (ret_ref, dret_ref) if c < per else (ssm_ref, dssm_ref)
            lo = (c % per) * rs
            dst[:, lo:lo + rs] = _dot_nt(dhb, w_ref[c])
            acc_sc[c] += _dot_tn(src[:, lo:lo + rs], dhb)

        @pl.when(i == last)
        def _():
            dw_ref[...] = acc_sc[...].astype(BF16)

    row = lambda w: pl.BlockSpec((tm, w), lambda i: (i, 0))
    wsp = pl.BlockSpec((nck, rs, d), lambda i: (0, 0, 0))
    return pl.pallas_call(
        body, name="outproj_bwd", grid=(lp // tm,),
        in_specs=[row(d), row(rw), row(sw), wsp],
        out_specs=(row(rw), row(sw), wsp),
        out_shape=(jax.ShapeDtypeStruct((lp, rw), F32), jax.ShapeDtypeStruct((lp, sw), F32),
                   jax.ShapeDtypeStruct((nck, rs, d), BF16)),
        scratch_shapes=[pltpu.VMEM((nck, rs, d), F32)],
        compiler_params=_params(("arbitrary",)),
    )(dh, ret, ssm, wo)


def _loss_head(h, fw, target):
    lp, d = h.shape
    tm = _tile(lp, 640, CHUNK)
    sub = tm // CHUNK

    def body(h_ref, w_ref, *rest):
        t_refs = rest[:sub]
        loss_ref, dh_ref, dw_ref = rest[sub:]
        i = pl.program_id(0)

        @pl.when(i == 0)
        def _():
            loss_ref[...] = jnp.zeros_like(loss_ref)
            dw_ref[...] = jnp.zeros_like(dw_ref)

        w = w_ref[...]
        for j in range(sub):
            rows = pl.ds(j * CHUNK, CHUNK)
            xh, r = _rms_stats(h_ref[rows, :])
            err = xh * w - t_refs[j][...]
            if j == 0:
                err = jnp.where(i == 0, 0.0, err)
            loss_ref[...] += 0.5 * jnp.sum(err * err) / d
            dout = err * (1.0 / d)
            dw_ref[...] += jnp.sum(dout * xh, axis=0, keepdims=True)
            dh_ref[rows, :] = _rms_bwd(dout, xh, r, w)

    t_spec = lambda j: pl.BlockSpec((CHUNK, d), lambda i: (jnp.maximum(i * sub + j - 1, 0), 0))
    return pl.pallas_call(
        body, name="loss_head", grid=(lp // tm,),
        in_specs=[pl.BlockSpec((tm, d), lambda i: (i, 0)), pl.BlockSpec((1, d), lambda i: (0, 0))]
        + [t_spec(j) for j in range(sub)],
        out_specs=(pl.BlockSpec((8, LANE), lambda i: (0, 0)), pl.BlockSpec((tm, d), lambda i: (i, 0)),
                   pl.BlockSpec((1, d), lambda i: (0, 0))),
        out_shape=(jax.ShapeDtypeStruct((8, LANE), F32), jax.ShapeDtypeStruct((lp, d), F32),
                   jax.ShapeDtypeStruct((1, d), F32)),
        compiler_params=_params(("arbitrary",)),
    )(h, fw, *([target] * sub))


def _pack(arrs):
    flat = jnp.concatenate([a.reshape(-1).astype(F32) for a in arrs])
    n = flat.shape[0]
    rows = -(-n // (8 * LANE)) * 8
    return jnp.pad(flat, (0, rows * LANE - n)).reshape(rows, LANE)


def _unpack(packed, shapes):
    flat = packed.reshape(-1)
    out, off = [], 0
    for s in shapes:
        n = math.prod(s)
        out.append(flat[off:off + n].reshape(s))
        off += n
    return out


def _to_segments(a, seg_len):
    return a.reshape(N_SEG, seg_len, a.shape[1]).transpose(1, 0, 2).reshape(a.shape)


def _from_segments(a, seg_len):
    return a.reshape(seg_len, N_SEG, a.shape[1]).transpose(1, 0, 2).reshape(a.shape)


WEIGHT_NAMES = ['meta_tokens', 'ffn1_norm_w', 'ffn1_w_gate', 'ffn1_w_up', 'ffn1_w_down', 'mix_norm_w', 'w_in',
                'ret_norm_w', 'ssm_lambda_re', 'ssm_lambda_im', 'ssm_log_dt', 'ssm_b_re', 'ssm_b_im', 'ssm_c_re',
                'ssm_c_im', 'ssm_d', 'ssm_glu_w', 'ssm_glu_b', 'ssm_norm_w', 'w_out', 'ffn2_norm_w', 'ffn2_w_gate',
                'ffn2_w_up', 'ffn2_w_down', 'final_norm_w']
BIG = ['ffn1_w_gate', 'ffn1_w_up', 'ffn1_w_down', 'w_in', 'ssm_glu_w', 'w_out', 'ffn2_w_gate', 'ffn2_w_up',
       'ffn2_w_down']
TRANSPOSED = ['ffn1_w_gate', 'ffn1_w_up', 'ffn2_w_gate', 'ffn2_w_up']
BIG_EARLY = ['ffn1_w_gate', 'ffn1_w_up', 'ffn1_w_down']
BIG_LATE = [n for n in BIG if n not in BIG_EARLY]
SMALL = [n for n in WEIGHT_NAMES if n not in BIG]


def kernel(x, meta_tokens, ffn1_norm_w, ffn1_w_gate, ffn1_w_up, ffn1_w_down, mix_norm_w, w_in, ret_norm_w, ssm_lambda_re, ssm_lambda_im, ssm_log_dt, ssm_b_re, ssm_b_im, ssm_c_re, ssm_c_im, ssm_d, ssm_glu_w, ssm_glu_b, ssm_norm_w, w_out, ffn2_norm_w, ffn2_w_gate, ffn2_w_up, ffn2_w_down, final_norm_w, loss_target, m_meta_tokens, m_ffn1_norm_w, m_ffn1_w_gate, m_ffn1_w_up, m_ffn1_w_down, m_mix_norm_w, m_w_in, m_ret_norm_w, m_ssm_lambda_re, m_ssm_lambda_im, m_ssm_log_dt, m_ssm_b_re, m_ssm_b_im, m_ssm_c_re, m_ssm_c_im, m_ssm_d, m_ssm_glu_w, m_ssm_glu_b, m_ssm_norm_w, m_w_out, m_ffn2_norm_w, m_ffn2_w_gate, m_ffn2_w_up, m_ffn2_w_down, m_final_norm_w, v_meta_tokens, v_ffn1_norm_w, v_ffn1_w_gate, v_ffn1_w_up, v_ffn1_w_down, v_mix_norm_w, v_w_in, v_ret_norm_w, v_ssm_lambda_re, v_ssm_lambda_im, v_ssm_log_dt, v_ssm_b_re, v_ssm_b_im, v_ssm_c_re, v_ssm_c_im, v_ssm_d, v_ssm_glu_w, v_ssm_glu_b, v_ssm_norm_w, v_w_out, v_ffn2_norm_w, v_ffn2_w_gate, v_ffn2_w_up, v_ffn2_w_down, v_final_norm_w):
    args = locals()
    w = {n: args[n] for n in WEIGHT_NAMES}
    m = {n: args["m_" + n] for n in WEIGHT_NAMES}
    v = {n: args["v_" + n] for n in WEIGHT_NAMES}

    seq, d = x.shape[1], x.shape[2]
    lp = seq + CHUNK
    seg_len = lp // N_SEG
    rw = RET_HEADS * HEAD_DIM
    sw = ssm_d.shape[-1]
    groups = sw // SSM_GROUP
    ns = groups * SSM_STATE
    jb = _tile(seg_len, 40, 8)
    chip = 2 * lax.axis_index("x") + lax.axis_index("y")

    as_fd = lambda t: jnp.swapaxes(t, -1, -2)
    shards = {n: (as_fd(w[n][0]) if n in TRANSPOSED else w[n][0]).astype(BF16) for n in BIG}
    early = [shards[n] for n in BIG_EARLY] + [meta_tokens]
    gathered = _forward_sibling("gather_early_forward",
                                _exchange("gather_early", _allgather_chips_plan(early), early))
    gw = dict(zip(BIG_EARLY, gathered[:-1]))
    meta_full = jnp.transpose(gathered[-1], (1, 0, 2)).reshape(N_META, d)
    late = [shards[n] for n in BIG_LATE]

    pos = jnp.arange(lp, dtype=F32) - float(CHUNK - N_META)
    freqs = 1.0 / (ROPE_BASE ** (jnp.arange(0, HEAD_DIM, 2, dtype=F32) / HEAD_DIM))
    ang = pos[:, None] * freqs[None, :]
    cosf = jnp.concatenate([jnp.cos(ang), jnp.cos(ang)], axis=1)
    sinf = jnp.concatenate([-jnp.sin(ang), jnp.sin(ang)], axis=1)
    tables = _retention_tables(_tile(lp, RET_ROWS, CHUNK))

    lam_re, lam_im, log_dt = ssm_lambda_re[0], ssm_lambda_im[0], ssm_log_dt[0]
    b_re, b_im, c_re, c_im = ssm_b_re[0], ssm_b_im[0], ssm_c_re[0], ssm_c_im[0]
    (ar, ai, bbr, bbi), prep_vjp = jax.vjp(_s5_prepare, lam_re, lam_im, log_dt, b_re, b_im)
    dt = jnp.exp(log_dt)[:, None]
    el = jnp.exp(seg_len * lam_re * dt)
    alr = el * jnp.cos(seg_len * lam_im * dt)
    ali = el * jnp.sin(seg_len * lam_im * dt)
    bc8 = lambda t: jnp.broadcast_to(t.reshape(1, ns), (N_SEG, ns))
    a8r, a8i, al8r, al8i = bc8(ar), bc8(ai), bc8(alr), bc8(ali)
    bsr = _blockdiag_in(jnp.transpose(bbr, (0, 2, 1)))
    bsi = _blockdiag_in(jnp.transpose(bbi, (0, 2, 1)))
    csrt = _blockdiag_in(c_re)
    csit = _blockdiag_in(-c_im)
    tr = lambda t: jnp.transpose(t, (0, 2, 1))
    bsr_b, bsi_b = bsr.astype(BF16), bsi.astype(BF16)
    csr_b, csi_b = tr(csrt).astype(BF16), tr(csit).astype(BF16)
    bsrt_b, bsit_b = tr(bsr).astype(BF16), tr(bsi).astype(BF16)
    csrt_b, csit_b = csrt.astype(BF16), csit.astype(BF16)

    h0 = (jnp.concatenate([jnp.zeros((CHUNK - N_META, d), F32), meta_full], axis=0), x[0])
    (h1, g1, u1), late_half = _ffn_fwd("ffn1_fwd", h0, ffn1_norm_w, gw['ffn1_w_gate'], gw['ffn1_w_up'],
                                       gw['ffn1_w_down'], _allgather_chips_plan(late), late)
    gw.update(zip(BIG_LATE, _forward_sibling("gather_late_forward", late_half)))
    glu_full = gw['ssm_glu_w'].reshape(sw, sw)
    n2, q, k, vv, gate, u = _inproj_fwd(h1, mix_norm_w, gw['w_in'], cosf, sinf, rw)
    o, ret, sprev = _ret_fwd(q, k, vv, gate, ret_norm_w, tables)
    u_seg = _to_segments(u, seg_len)
    xr, xi, c0r, c0i, yp, ssm_seg = _s5_fwd(u_seg, bsr_b, bsi_b, csr_b, csi_b, a8r, a8i, al8r, al8i,
                                            ssm_d, glu_full, ssm_glu_b, ssm_norm_w, jb)
    ssm = _from_segments(ssm_seg, seg_len)
    h2 = _outproj_fwd(h1, ret, ssm, gw['w_out'])
    (h3, g2, u2), _ = _ffn_fwd("ffn2_fwd", h2, ffn2_norm_w, gw['ffn2_w_gate'], gw['ffn2_w_up'], gw['ffn2_w_down'])
    loss_part, dh3, d_final = _loss_head(h3, final_norm_w.reshape(1, d), loss_target[0])

    (dh2, d_ffn2_norm, nb, daccb, ab, dgb, dub), _ = _ffn_bwd_act(
        "ffn2_bwd_act", dh3, h2, ffn2_norm_w, g2, u2, gw['ffn2_w_gate'], gw['ffn2_w_up'], gw['ffn2_w_down'])
    (dwg2, dwu2, dwd2), _ = _ffn_bwd_w("ffn2_bwd_w", nb, daccb, ab, dgb, dub)
    dret, dssm, dwo = _outproj_bwd(dh2, ret, ssm, gw['w_out'])
    (du_seg, d_ssm_norm, d_glu_w, d_glu_b, d_ssm_d, dcr_s, dci_s, dbr_s, dbi_s, dar8, dai8) = _s5_bwd(
        _to_segments(dssm, seg_len), u_seg, yp, xr, xi, c0r, c0i, bsrt_b, bsit_b, csrt_b, csit_b,
        a8r, a8i, al8r, al8i, ssm_d, glu_full, ssm_glu_b, ssm_norm_w, jb)
    du = _from_segments(du_seg, seg_len)
    dq, dk, dv, dgate, d_ret_norm = _ret_bwd(dret, q, k, vv, gate, o, sprev, ret_norm_w, tables, cosf, sinf)
    dh1, d_mix_norm, dwin = _inproj_bwd(dh2, h1, mix_norm_w, n2, gw['w_in'], dq, dk, dv, dgate, du)
    late_parts = {
        'w_in': dwin, 'ssm_glu_w': d_glu_w.reshape(N_CHIP, sw // N_CHIP, sw).astype(BF16), 'w_out': dwo,
        'ffn2_w_gate': dwg2, 'ffn2_w_up': dwu2, 'ffn2_w_down': dwd2,
    }
    late_list = [late_parts[n] for n in BIG_LATE]
    (dh0, d_ffn1_norm, nb, daccb, ab, dgb, dub), late_recv = _ffn_bwd_act(
        "ffn1_bwd_act", dh1, h0, ffn1_norm_w, g1, u1, gw['ffn1_w_gate'], gw['ffn1_w_up'], gw['ffn1_w_down'],
        _alltoall_chips_plan(late_list), late_list)
    grad_x = dh0[CHUNK:][None]
    d_meta = dh0[CHUNK - N_META:CHUNK]

    d_c_re = jnp.transpose(_blockdiag_out(tr(dcr_s), groups, SSM_GROUP, SSM_STATE), (0, 1, 2))
    d_c_im = -_blockdiag_out(tr(dci_s), groups, SSM_GROUP, SSM_STATE)
    d_bbr = jnp.transpose(_blockdiag_out(dbr_s, groups, SSM_GROUP, SSM_STATE), (0, 2, 1))
    d_bbi = jnp.transpose(_blockdiag_out(dbi_s, groups, SSM_GROUP, SSM_STATE), (0, 2, 1))
    d_ar = jnp.sum(dar8, axis=0).reshape(groups, SSM_STATE)
    d_ai = jnp.sum(dai8, axis=0).reshape(groups, SSM_STATE)
    small_parts = [loss_part[0:1, :], d_meta, d_ffn1_norm, d_mix_norm, d_ret_norm, d_ar, d_ai, d_bbr, d_bbi,
                   d_c_re, d_c_im, d_ssm_d, d_glu_b, d_ssm_norm, d_ffn2_norm, d_final]
    small_shapes = [a.shape for a in small_parts]
    packed = _pack(small_parts)
    early_recv, (all_parts,) = _ffn_bwd_w_scatter("ffn1_bwd_w", nb, daccb, ab, dgb, dub, chip,
                                                  _allgather_all_plan([packed]), [packed])
    received = dict(zip(BIG_LATE + BIG_EARLY, late_recv + early_recv))
    ffn_names = [n for n in BIG if n.startswith('ffn')]
    chip_sum = dict(zip(ffn_names, _sum_slots("sum_chips_ffn", [received[n] for n in ffn_names], BF16)))
    for n in BIG:
        if n not in chip_sum:
            chip_sum[n] = _sum_slots("sum_chips_" + n, [received[n]], BF16)[0]
    chip_sums = [chip_sum[n] for n in BIG]
    sib_sums = _swap_sibling("swap_sibling", chip_sums)
    (loss_row, g_meta_full, g_ffn1_norm, g_mix_norm, g_ret_norm, g_ar, g_ai, g_bbr, g_bbi, g_c_re, g_c_im,
     g_ssm_d, g_glu_b, g_ssm_norm, g_ffn2_norm, g_final) = _unpack(_sum_slots("sum_small", [all_parts], F32)[0],
                                                                  small_shapes)
    g_lam_re, g_lam_im, g_log_dt, g_b_re, g_b_im = prep_vjp((g_ar, g_ai, g_bbr, g_bbi))
    loss = loss_row[0, 0]
    g_meta = lax.dynamic_slice(g_meta_full, (0, chip * (d // N_CHIP)), (N_META, d // N_CHIP))
    small_grads = {
        'meta_tokens': g_meta, 'ffn1_norm_w': g_ffn1_norm, 'mix_norm_w': g_mix_norm, 'ret_norm_w': g_ret_norm,
        'ssm_lambda_re': g_lam_re[None], 'ssm_lambda_im': g_lam_im[None], 'ssm_log_dt': g_log_dt[None],
        'ssm_b_re': g_b_re[None], 'ssm_b_im': g_b_im[None], 'ssm_c_re': g_c_re[None], 'ssm_c_im': g_c_im[None],
        'ssm_d': g_ssm_d, 'ssm_glu_b': g_glu_b, 'ssm_norm_w': g_ssm_norm, 'ffn2_norm_w': g_ffn2_norm,
        'final_norm_w': g_final.reshape(d),
    }

    grads, deltas, new_m, new_v = {}, {}, {}, {}
    g_pair = {n: [mine, sib] for n, mine, sib in zip(BIG, chip_sums, sib_sums)}
    view = lambda n, t: as_fd(t) if n in TRANSPOSED else t
    ffn_out = _adam("adam_ffn", [(view(n, w[n]), view(n, m[n]), view(n, v[n])) for n in ffn_names],
                    [g_pair[n] for n in ffn_names])
    for n, outs in zip(ffn_names, ffn_out):
        grads[n], deltas[n], new_m[n], new_v[n] = [view(n, t) for t in outs]
    for n in BIG:
        if n not in ffn_names:
            grads[n], deltas[n], new_m[n], new_v[n] = _adam("adam_" + n, [(w[n], m[n], v[n])], [g_pair[n]])[0]
    sm_shapes = [w[n].shape for n in SMALL]
    sm_out = _adam("adam_small", [(_pack([w[n] for n in SMALL]), _pack([m[n] for n in SMALL]),
                                  _pack([v[n] for n in SMALL]))],
                   [[_pack([small_grads[n].reshape(w[n].shape) for n in SMALL])]])[0]
    for dst, packed in zip((grads, deltas, new_m, new_v), sm_out):
        for n, t in zip(SMALL, _unpack(packed, sm_shapes)):
            dst[n] = t

    return (loss, grad_x, *[grads[n] for n in WEIGHT_NAMES], *[deltas[n] for n in WEIGHT_NAMES],
            *[new_m[n] for n in WEIGHT_NAMES], *[new_v[n] for n in WEIGHT_NAMES])
```

```python
import functools
import math

import jax
import jax.numpy as jnp
from jax import lax
from jax.experimental import pallas as pl
from jax.experimental.pallas import tpu as pltpu

N_META = 16
RET_HEADS = 4
HEAD_DIM = 128
SSM_GROUP = 16
SSM_STATE = 64
CHUNK = 128
ROPE_BASE = 10000.0
EPS = 1e-6
FFN_RES = 0.5
N_SEG = 8
N_SEC = 4
N_CHIP = 4
LANE = 128
FFN_CPS = 2
BWD_W_ROWS = 1664

ADAM_LR = 0.001
ADAM_B1 = 0.9
ADAM_B2 = 0.999
ADAM_EPS = 1e-08
ADAM_WD = 0.01
ADAM_STEP = 10

VMEM_LIMIT = 56 * 1024 * 1024

F32 = jnp.float32
BF16 = jnp.bfloat16
MESH = pl.DeviceIdType.MESH


def _dot(a, b):
    return jnp.dot(a, b, preferred_element_type=F32)


def _dot_nt(a, b):
    return lax.dot_general(a, b, (((1,), (1,)), ((), ())), preferred_element_type=F32)


def _dot_tn(a, b):
    return lax.dot_general(a, b, (((0,), (0,)), ((), ())), preferred_element_type=F32)


def _tile(n, target, mult=64):
    best = None
    t = mult
    while t <= min(n, target):
        if n % t == 0:
            best = t
        t += mult
    assert best is not None, (n, target)
    return best


def _params(sem, vmem=VMEM_LIMIT):
    return pltpu.CompilerParams(dimension_semantics=sem, vmem_limit_bytes=vmem)


def _rms_stats(xf):
    r = lax.rsqrt(jnp.mean(xf * xf, axis=-1, keepdims=True) + EPS)
    return xf * r, r


def _rms_bwd(dy, xh, r, w):
    dxh = dy * w
    return r * (dxh - xh * jnp.mean(dxh * xh, axis=-1, keepdims=True))


def _sigmoid(x):
    return 0.5 * jnp.tanh(0.5 * x) + 0.5


GELU_K0 = math.sqrt(2.0 / math.pi)
GELU_K1 = 0.044715


CHIP_MASKS = [(1, 0, 0), (0, 1, 0), (1, 1, 0)]
ALL_MASKS = [(0, 0, 1), (0, 1, 0), (0, 1, 1), (1, 0, 0), (1, 0, 1), (1, 1, 0), (1, 1, 1)]
SIB_MASKS = [(0, 0, 1)]
ANY_SPEC = pl.BlockSpec(memory_space=pl.ANY)


class _Plan:
    def __init__(self, arrays, masks, n_slots, src_slotted, dst_slotted, local_copy, half=False, forward=False):
        self.shapes = [(a.shape, a.dtype) for a in arrays]
        self.n = len(arrays)
        self.masks = masks
        self.n_slots = n_slots
        self.src_slotted, self.dst_slotted, self.local_copy = src_slotted, dst_slotted, local_copy
        self.half, self.forward = half, forward
        self.n_cp = self.n * len(masks) * (len(CHIP_MASKS) if forward else 1)

    def out_shape(self):
        out = []
        for shp, dt in self.shapes:
            if self.dst_slotted and not self.src_slotted:
                shp = (self.n_slots,) + shp
            elif self.src_slotted and not self.dst_slotted:
                shp = shp[1:]
            out.append(jax.ShapeDtypeStruct(shp, dt))
        return tuple(out)

    def scratch(self):
        return [pltpu.SemaphoreType.DMA((self.n_cp,)), pltpu.SemaphoreType.DMA((self.n_cp,)),
                pltpu.SemaphoreType.DMA((self.n,))]

    def _slot(self, px, py, pc):
        if self.n_slots == 8:
            return 4 * px + 2 * py + pc
        if self.n_slots == 4:
            return 2 * px + py
        return pc

    def copies(self, ins, outs, sems):
        send_sems, recv_sems, loc_sems = sems
        x, y, c = lax.axis_index("x"), lax.axis_index("y"), lax.axis_index("c")
        me = self._slot(x, y, c)
        n_m = len(self.masks)
        cps = []
        for a in range(self.n):
            if self.forward:
                rows = self.shapes[a][0][-2] // 2
                mine = pl.ds(pl.multiple_of(c * rows, 8), rows)
                for j, (mx, my, _) in enumerate(CHIP_MASKS):
                    blk = outs[a].at[2 * (1 - x if mx else x) + (1 - y if my else y), mine]
                    k = a * len(CHIP_MASKS) + j
                    cps.append(pltpu.make_async_remote_copy(
                        src_ref=blk, dst_ref=blk, send_sem=send_sems.at[k], recv_sem=recv_sems.at[k],
                        device_id=(x, y, 1 - c), device_id_type=MESH))
                continue
            if self.local_copy:
                src = ins[a].at[me] if self.src_slotted else ins[a]
                cps.append(pltpu.make_async_copy(src, outs[a].at[me], loc_sems.at[a]))
            for mi, (mx, my, mc) in enumerate(self.masks):
                px = 1 - x if mx else x
                py = 1 - y if my else y
                pc = 1 - c if mc else c
                src = ins[a].at[self._slot(px, py, pc)] if self.src_slotted else ins[a]
                dst = outs[a].at[me] if self.dst_slotted else outs[a]
                if self.half:
                    rows = src.shape[-2] // 2
                    mine = pl.ds(pl.multiple_of(c * rows, 8), rows)
                    src, dst = src.at[mine], dst.at[mine]
                k = a * n_m + mi
                cps.append(pltpu.make_async_remote_copy(
                    src_ref=src, dst_ref=dst, send_sem=send_sems.at[k], recv_sem=recv_sems.at[k],
                    device_id=(px, py, pc), device_id_type=MESH))
        return cps


def _exchange(name, plan, arrays):
    n = plan.n

    def body(*refs):
        cps = plan.copies(refs[:n], refs[n:2 * n], refs[2 * n:])
        for cp in cps:
            cp.start()
        for cp in cps:
            cp.wait()

    outs = pl.pallas_call(
        body, name=name, out_shape=plan.out_shape(),
        in_specs=[ANY_SPEC] * n, out_specs=tuple([ANY_SPEC] * n), scratch_shapes=plan.scratch(),
        input_output_aliases={i: i for i in range(n)} if plan.forward else {},
    )(*arrays)
    return list(outs)


def _pcall(body, *, name, grid, in_specs, out_specs, out_shape, scratch_shapes, args, plan=None, plan_args=()):
    sem = ("arbitrary",) * len(grid)
    if plan is None:
        return pl.pallas_call(body, name=name, grid=grid, in_specs=in_specs, out_specs=out_specs,
                              out_shape=out_shape, scratch_shapes=scratch_shapes,
                              compiler_params=_params(sem))(*args), []
    n_in, n_out, n_scr, n_p = len(in_specs), len(out_specs), len(scratch_shapes), plan.n

    def wrapped(*refs):
        ins = refs[:n_in]
        p_ins = refs[n_in:n_in + n_p]
        o0 = n_in + n_p
        outs = refs[o0:o0 + n_out]
        p_outs = refs[o0 + n_out:o0 + n_out + n_p]
        s0 = o0 + n_out + n_p
        scr = refs[s0:s0 + n_scr]
        sems = refs[s0 + n_scr:]
        ids = [pl.program_id(i) for i in range(len(grid))]
        first = functools.reduce(jnp.logical_and, [i == 0 for i in ids])
        last = functools.reduce(jnp.logical_and, [i == g - 1 for i, g in zip(ids, grid)])

        @pl.when(first)
        def _():
            for cp in plan.copies(p_ins, p_outs, sems):
                cp.start()

        body(*ins, *outs, *scr)

        @pl.when(last)
        def _():
            for cp in plan.copies(p_ins, p_outs, sems):
                cp.wait()

    res = pl.pallas_call(
        wrapped, name=name, grid=grid,
        in_specs=list(in_specs) + [ANY_SPEC] * n_p,
        out_specs=tuple(out_specs) + (ANY_SPEC,) * n_p,
        out_shape=tuple(out_shape) + plan.out_shape(),
        scratch_shapes=list(scratch_shapes) + plan.scratch(),
        compiler_params=_params(sem),
    )(*args, *plan_args)
    return res[:n_out], list(res[n_out:])


def _allgather_chips_plan(arrays):
    return _Plan(arrays, CHIP_MASKS, 4, False, True, True, half=True)


def _forward_sibling(name, gathered):
    return _exchange(name, _Plan(gathered, SIB_MASKS, 4, True, True, False, forward=True), gathered)


def _alltoall_chips_plan(arrays):
    return _Plan(arrays, CHIP_MASKS, 4, True, True, True)


def _swap_sibling(name, arrays):
    return _exchange(name, _Plan(arrays, SIB_MASKS, 2, False, False, False), arrays)


def _allgather_all_plan(arrays):
    return _Plan(arrays, ALL_MASKS, 8, False, True, True)


def _sum_slots(name, arrs, out_dtype):
    s, r = arrs[0].shape[0], arrs[0].shape[-2]
    c = arrs[0].shape[-1] * (2 if arrs[0].ndim == 4 else 1)
    n = len(arrs)
    tr = _tile(r, 512 if n == 1 else 176, 8)

    def body(*refs):
        for a_ref, o_ref in zip(refs[:n], refs[n:]):
            if len(a_ref.shape) == 4:
                for half in range(2):
                    acc = a_ref[0, half].astype(F32)
                    for i in range(1, s):
                        acc = acc + a_ref[i, half].astype(F32)
                    o_ref[:, half * (c // 2):(half + 1) * (c // 2)] = acc.astype(out_dtype)
            else:
                acc = a_ref[0].astype(F32)
                for i in range(1, s):
                    acc = acc + a_ref[i].astype(F32)
                o_ref[...] = acc.astype(out_dtype)

    def in_spec(a):
        if a.ndim == 4:
            return pl.BlockSpec((s, 2, tr, c // 2), lambda i: (0, 0, i, 0))
        return pl.BlockSpec((s, tr, c), lambda i: (0, i, 0))

    return list(pl.pallas_call(
        body, name=name, grid=(r // tr,),
        in_specs=[in_spec(a) for a in arrs],
        out_specs=(pl.BlockSpec((tr, c), lambda i: (i, 0)),) * n,
        out_shape=(jax.ShapeDtypeStruct((r, c), out_dtype),) * n,
        compiler_params=_params(("arbitrary",)),
    )(*arrs))


def _adam_math(w, g, m, v):
    m_new = ADAM_B1 * m + (1.0 - ADAM_B1) * g
    v_new = ADAM_B2 * v + (1.0 - ADAM_B2) * (g * g)
    m_hat = m_new / (1.0 - ADAM_B1 ** ADAM_STEP)
    v_hat = v_new / (1.0 - ADAM_B2 ** ADAM_STEP)
    delta = -ADAM_LR * (m_hat / (jnp.sqrt(v_hat) + ADAM_EPS) + ADAM_WD * w)
    return delta, m_new, v_new


def _adam(name, wmv, g_parts):
    w0 = wmv[0][0]
    r, c = w0.shape[-2:]
    n_w = len(wmv)
    n_g = len(g_parts[0])
    tr = _tile(r, 256 if n_w == 1 else 88, 8)
    lead = w0.ndim == 3
    at = (lambda ref: ref.at[0]) if lead else (lambda ref: ref)
    n_in = 3 + n_g

    def body(*refs):
        for j in range(n_w):
            ins = refs[j * n_in:(j + 1) * n_in]
            outs = refs[n_w * n_in + 4 * j:n_w * n_in + 4 * j + 4]
            w_ref, m_ref, v_ref = [at(t) for t in ins[:3]]
            g_out, d_out, m_out, v_out = [at(t) for t in outs]
            g = ins[3][...].astype(F32)
            for gr in ins[4:]:
                g = g + gr[...].astype(F32)
            delta, m_new, v_new = _adam_math(w_ref[...], g, m_ref[...], v_ref[...])
            g_out[...] = g
            d_out[...] = delta
            m_out[...] = m_new
            v_out[...] = v_new

    spec = pl.BlockSpec((tr, c), lambda i: (i, 0))
    wspec = pl.BlockSpec((1, tr, c), lambda i: (0, i, 0)) if lead else spec
    shp = jax.ShapeDtypeStruct(w0.shape, F32)
    args = [t for (w, m, v), gp in zip(wmv, g_parts) for t in (w, m, v, *gp)]
    res = pl.pallas_call(
        body, name=name, grid=(r // tr,),
        in_specs=([wspec] * 3 + [spec] * n_g) * n_w, out_specs=(wspec,) * (4 * n_w), out_shape=(shp,) * (4 * n_w),
        compiler_params=_params(("arbitrary",)),
    )(*args)
    return [tuple(res[4 * j:4 * j + 4]) for j in range(n_w)]


SUB_ROWS = 32
FFN_BWD_ROWS = 416
FFN_FWD_ROWS = 832
RET_ROWS = 640
S5_STEPS = 80


def _tile_parts(tm, d, head, x):
    nsub = tm // SUB_ROWS
    off = head.shape[0] // SUB_ROWS
    specs = [pl.BlockSpec(head.shape, lambda i, k: (0, 0))] + [
        pl.BlockSpec((SUB_ROWS, d), lambda i, k, j=j: (jnp.maximum(i * nsub + j - off, 0), 0)) for j in range(nsub)]

    def assemble(i, part_refs, h_sc):
        head_ref, x_refs = part_refs[0], part_refs[1:]
        for j in range(nsub):
            rows = slice(j * SUB_ROWS, (j + 1) * SUB_ROWS)
            val = x_refs[j][...]
            if j < off:
                val = jnp.where(i == 0, head_ref[rows, :], val)
            h_sc[rows, :] = val

    return specs, [head] + [x] * nsub, assemble


def _h_source(body, h, tm, d):
    if not isinstance(h, tuple):
        return body, [pl.BlockSpec((tm, d), lambda i, k: (i, 0))], [h], []
    specs, args, assemble = _tile_parts(tm, d, *h)
    n_h = len(specs)

    def with_parts(*refs):
        h_sc = refs[-1]

        @pl.when(pl.program_id(1) == 0)
        def _():
            assemble(pl.program_id(0), refs[:n_h], h_sc)

        body(h_sc, *refs[n_h:-1])

    return with_parts, specs, args, [pltpu.VMEM((tm, d), F32)]


def _ffn_fwd(name, h, nw, wg, wu, wd, plan=None, plan_args=()):
    lp, d = (h[0].shape[0] + h[1].shape[0], h[1].shape[1]) if isinstance(h, tuple) else h.shape
    nck, f, _ = wg.shape
    tm = _tile(lp, FFN_FWD_ROWS)
    last = nck // FFN_CPS - 1

    def body(h_ref, nw_ref, wg_ref, wu_ref, wd_ref, ho_ref, g_ref, u_ref, n_sc, acc_sc):
        k = pl.program_id(1)

        @pl.when(k == 0)
        def _():
            xh, _ = _rms_stats(h_ref[...])
            n_sc[...] = (xh * nw_ref[...]).astype(BF16)
            acc_sc[...] = jnp.zeros_like(acc_sc)

        n = n_sc[...]
        acc = acc_sc[...]
        for c in range(FFN_CPS):
            g = _dot_nt(n, wg_ref[c])
            u = _dot_nt(n, wu_ref[c])
            g_ref[c] = g.astype(BF16)
            u_ref[c] = u.astype(BF16)
            a = (g * _sigmoid(g) * u).astype(BF16)
            acc = acc + _dot(a, wd_ref[c])
        acc_sc[...] = acc

        @pl.when(k == last)
        def _():
            ho_ref[...] = h_ref[...] + FFN_RES * acc_sc[...]

    body, h_specs, h_args, h_scratch = _h_source(body, h, tm, d)
    w_fd = pl.BlockSpec((FFN_CPS, f, d), lambda i, k: (k, 0, 0))
    hid = pl.BlockSpec((FFN_CPS, tm, f), lambda i, k: (k, i, 0))
    return _pcall(
        body, name=name, grid=(lp // tm, nck // FFN_CPS), plan=plan, plan_args=plan_args,
        args=(*h_args, nw, wg, wu, wd),
        in_specs=h_specs + [pl.BlockSpec((1, d), lambda i, k: (0, 0)), w_fd, w_fd, w_fd],
        out_specs=(pl.BlockSpec((tm, d), lambda i, k: (i, 0)), hid, hid),
        out_shape=(jax.ShapeDtypeStruct((lp, d), F32),
                   jax.ShapeDtypeStruct((nck, lp, f), BF16),
                   jax.ShapeDtypeStruct((nck, lp, f), BF16)),
        scratch_shapes=[pltpu.VMEM((tm, d), BF16), pltpu.VMEM((tm, d), F32)] + h_scratch)


def _ffn_bwd_act(name, dh, h, nw, g, u, wg, wu, wd, plan=None, plan_args=()):
    lp, d = dh.shape
    nck, f, _ = wg.shape
    tm = _tile(lp, FFN_BWD_ROWS, SUB_ROWS)
    last = nck // FFN_CPS - 1

    def body(h_ref, dh_ref, nw_ref, g_ref, u_ref, wg_ref, wu_ref, wd_ref,
             dhi_ref, dnw_ref, n_ref, dacc_ref, a_ref, dg_ref, du_ref,
             xh_sc, r_sc, dn_sc):
        i = pl.program_id(0)
        k = pl.program_id(1)

        @pl.when(k == 0)
        def _():
            xh, r = _rms_stats(h_ref[...])
            xh_sc[...] = xh
            r_sc[...] = r
            n_ref[...] = (xh * nw_ref[...]).astype(BF16)
            dacc_ref[...] = (FFN_RES * dh_ref[...]).astype(BF16)
            dn_sc[...] = jnp.zeros_like(dn_sc)

        @pl.when(jnp.logical_and(i == 0, k == 0))
        def _():
            dnw_ref[...] = jnp.zeros_like(dnw_ref)

        dacc = dacc_ref[...]
        dn = dn_sc[...]
        for c in range(FFN_CPS):
            gv = g_ref[c].astype(F32)
            uv = u_ref[c].astype(F32)
            sg = _sigmoid(gv)
            sil = gv * sg
            da = _dot_nt(dacc, wd_ref[c])
            dgk = (da * uv * (sg * (1.0 + gv * (1.0 - sg)))).astype(BF16)
            duk = (da * sil).astype(BF16)
            a_ref[c] = (sil * uv).astype(BF16)
            dg_ref[c] = dgk
            du_ref[c] = duk
            dn = dn + _dot(dgk, wg_ref[c]) + _dot(duk, wu_ref[c])
        dn_sc[...] = dn

        @pl.when(k == last)
        def _():
            dnl = dn_sc[...]
            xh = xh_sc[...]
            dhi_ref[...] = dh_ref[...] + _rms_bwd(dnl, xh, r_sc[...], nw_ref[...])
            dnw_ref[...] += jnp.sum(dnl * xh, axis=0, keepdims=True)

    body, h_specs, h_args, h_scratch = _h_source(body, h, tm, d)
    row = pl.BlockSpec((tm, d), lambda i, k: (i, 0))
    vec = pl.BlockSpec((1, d), lambda i, k: (0, 0))
    hid = pl.BlockSpec((FFN_CPS, tm, f), lambda i, k: (k, i, 0))
    w_fd = pl.BlockSpec((FFN_CPS, f, d), lambda i, k: (k, 0, 0))
    rshape = jax.ShapeDtypeStruct((lp, d), BF16)
    hshape = jax.ShapeDtypeStruct((nck, lp, f), BF16)
    return _pcall(
        body, name=name, grid=(lp // tm, nck // FFN_CPS), plan=plan, plan_args=plan_args,
        args=(*h_args, dh, nw, g, u, wg, wu, wd),
        in_specs=h_specs + [row, vec, hid, hid, w_fd, w_fd, w_fd],
        out_specs=(row, vec, row, row, hid, hid, hid),
        out_shape=(jax.ShapeDtypeStruct((lp, d), F32), jax.ShapeDtypeStruct((1, d), F32),
                   rshape, rshape, hshape, hshape, hshape),
        scratch_shapes=[pltpu.VMEM((tm, d), F32), pltpu.VMEM((tm, 1), F32), pltpu.VMEM((tm, d), F32)] + h_scratch)


def _ffn_bwd_w(name, n, dacc, a, dg, du, plan=None, plan_args=()):
    lp, d = n.shape
    nck, _, f = a.shape
    tm = _tile(lp, BWD_W_ROWS)
    last = lp // tm - 1

    def body(n_ref, dacc_ref, a_ref, dg_ref, du_ref, dwg_ref, dwu_ref, dwd_ref, ag_sc, au_sc, ad_sc):
        i = pl.program_id(1)

        @pl.when(i == 0)
        def _():
            ag_sc[...] = jnp.zeros_like(ag_sc)
            au_sc[...] = jnp.zeros_like(au_sc)
            ad_sc[...] = jnp.zeros_like(ad_sc)

        nv = n_ref[...]
        ag_sc[...] += _dot_tn(dg_ref[0], nv)
        au_sc[...] += _dot_tn(du_ref[0], nv)
        ad_sc[...] += _dot_tn(a_ref[0], dacc_ref[...])

        @pl.when(i == last)
        def _():
            dwg_ref[0] = ag_sc[...].astype(BF16)
            dwu_ref[0] = au_sc[...].astype(BF16)
            dwd_ref[0] = ad_sc[...].astype(BF16)

    row = pl.BlockSpec((tm, d), lambda k, i: (i, 0))
    hid = pl.BlockSpec((1, tm, f), lambda k, i: (k, i, 0))
    w_fd = pl.BlockSpec((1, f, d), lambda k, i: (k, 0, 0))
    wshape = jax.ShapeDtypeStruct((nck, f, d), BF16)
    return _pcall(
        body, name=name, grid=(nck, lp // tm), plan=plan, plan_args=plan_args, args=(n, dacc, a, dg, du),
        in_specs=[row, row, hid, hid, hid], out_specs=(w_fd, w_fd, w_fd), out_shape=(wshape,) * 3,
        scratch_shapes=[pltpu.VMEM((f, d), F32)] * 3)


def _ffn_bwd_w_scatter(name, n, dacc, a, dg, du, chip, plan, plan_args):
    lp, d = n.shape
    nck, _, f = a.shape
    tm = _tile(lp, BWD_W_ROWS)
    last_i = lp // tm - 1
    n_w = 3
    n_p = plan.n

    def body(me_ref, n_ref, dacc_ref, a_ref, dg_ref, du_ref, *rest):
        p_ins = rest[:n_p]
        recv = rest[n_p:n_p + n_w]
        p_outs = rest[n_p + n_w:2 * n_p + n_w]
        acc = rest[2 * n_p + n_w:2 * n_p + 2 * n_w]
        stage, send_sems, recv_sems, loc_sems = rest[2 * n_p + 2 * n_w:2 * n_p + 2 * n_w + 4]
        p_sems = rest[2 * n_p + 2 * n_w + 4:]
        p = pl.program_id(0)
        i = pl.program_id(1)
        me = me_ref[0]
        c = lax.axis_index("c")

        def send(w, pos):
            kk = jnp.bitwise_xor(me, nck - 1 - pos)
            diff = jnp.bitwise_xor(kk, me)
            m = jnp.where(diff == 2, 0, jnp.where(diff == 1, 1, 2))
            return pltpu.make_async_remote_copy(
                src_ref=stage.at[lax.rem(pos, 2), w], dst_ref=recv[w].at[me],
                send_sem=send_sems.at[w * 3 + m], recv_sem=recv_sems.at[w * 3 + m],
                device_id=(lax.div(kk, 2), lax.rem(kk, 2), c), device_id_type=MESH)

        @pl.when(jnp.logical_and(p == 0, i == 0))
        def _():
            for cp in plan.copies(p_ins, p_outs, p_sems):
                cp.start()

        @pl.when(i == 0)
        def _():
            for t in acc:
                t[...] = jnp.zeros_like(t)

        nv = n_ref[...]
        acc[0][...] += _dot_tn(dg_ref[0], nv)
        acc[1][...] += _dot_tn(du_ref[0], nv)
        acc[2][...] += _dot_tn(a_ref[0], dacc_ref[...])

        @pl.when(jnp.logical_and(i == last_i, p >= 2))
        def _():
            for w in range(n_w):
                send(w, p - 2).wait_send()

        @pl.when(i == last_i)
        def _():
            for w in range(n_w):
                stage[lax.rem(p, 2), w] = acc[w][...].astype(BF16)

        @pl.when(jnp.logical_and(i == last_i, p < nck - 1))
        def _():
            for w in range(n_w):
                send(w, p).start()

        @pl.when(jnp.logical_and(i == last_i, p == nck - 1))
        def _():
            own = [pltpu.make_async_copy(stage.at[(nck - 1) % 2, w], recv[w].at[me], loc_sems.at[w])
                   for w in range(n_w)]
            for cp in own:
                cp.start()
            for w in range(n_w):
                send(w, nck - 2).wait_send()
            for cp in own:
                cp.wait()
            for w in range(n_w):
                for m in range(3):
                    pltpu.make_async_remote_copy(
                        src_ref=stage.at[0, w], dst_ref=recv[w].at[me],
                        send_sem=send_sems.at[w * 3 + m], recv_sem=recv_sems.at[w * 3 + m],
                        device_id=(0, 0, c), device_id_type=MESH).wait_recv()
            for cp in plan.copies(p_ins, p_outs, p_sems):
                cp.wait()

    chunk = lambda k, me_ref: jnp.bitwise_xor(me_ref[0], nck - 1 - k)
    row = pl.BlockSpec((tm, d), lambda k, i, me_ref: (i, 0))
    hid = pl.BlockSpec((1, tm, f), lambda k, i, me_ref: (chunk(k, me_ref), i, 0))
    wshape = jax.ShapeDtypeStruct((nck, f, d), BF16)
    res = pl.pallas_call(
        body, name=name,
        grid_spec=pltpu.PrefetchScalarGridSpec(
            num_scalar_prefetch=1, grid=(nck, lp // tm),
            in_specs=[row, row, hid, hid, hid] + [ANY_SPEC] * n_p,
            out_specs=(ANY_SPEC,) * (n_w + n_p),
            scratch_shapes=[pltpu.VMEM((f, d), F32)] * n_w + [
                pltpu.VMEM((2, n_w, f, d), BF16), pltpu.SemaphoreType.DMA((n_w * 3,)),
                pltpu.SemaphoreType.DMA((n_w * 3,)), pltpu.SemaphoreType.DMA((n_w,))] + plan.scratch()),
        out_shape=(wshape,) * n_w + plan.out_shape(),
        compiler_params=_params(("arbitrary", "arbitrary")),
    )(chip.reshape(1).astype(jnp.int32), n, dacc, a, dg, du, *plan_args)
    return list(res[:n_w]), list(res[n_w:])


def _inproj_fwd(h, nw, w_in, cosf, sinf, rw):
    lp, d = h.shape
    nck, _, ps = w_in.shape
    proj = nck * ps
    sw = proj - 4 * rw
    tm = _tile(lp, 640)
    scale = HEAD_DIM ** -0.5
    heads = rw // HEAD_DIM

    def body(h_ref, nw_ref, w_ref, cos_ref, sin_ref, n_ref, q_ref, k_ref, v_ref, g_ref, u_ref, p_sc):
        xh, _ = _rms_stats(h_ref[...])
        n = (xh * nw_ref[...]).astype(BF16)
        n_ref[...] = n
        for c in range(nck):
            p_sc[:, c * ps:(c + 1) * ps] = _dot(n, w_ref[c])
        cs = cos_ref[...]
        sn = sin_ref[...]
        for hh in range(heads):
            lo = hh * HEAD_DIM
            qh = p_sc[:, lo:lo + HEAD_DIM]
            q_ref[:, lo:lo + HEAD_DIM] = (qh * cs + pltpu.roll(qh, HEAD_DIM // 2, 1) * sn).astype(BF16)
            kh = p_sc[:, rw + lo:rw + lo + HEAD_DIM]
            k_ref[:, lo:lo + HEAD_DIM] = ((kh * cs + pltpu.roll(kh, HEAD_DIM // 2, 1) * sn) * scale).astype(BF16)
        v_ref[...] = p_sc[:, 2 * rw:3 * rw].astype(BF16)
        g_ref[...] = p_sc[:, 3 * rw:4 * rw]
        u_ref[...] = p_sc[:, 4 * rw:]

    row = lambda w: pl.BlockSpec((tm, w), lambda i: (i, 0))
    return pl.pallas_call(
        body, name="inproj_fwd", grid=(lp // tm,),
        in_specs=[row(d), pl.BlockSpec((1, d), lambda i: (0, 0)),
                  pl.BlockSpec((nck, d, ps), lambda i: (0, 0, 0)), row(HEAD_DIM), row(HEAD_DIM)],
        out_specs=(row(d), row(rw), row(rw), row(rw), row(rw), row(sw)),
        out_shape=(jax.ShapeDtypeStruct((lp, d), BF16),
                   jax.ShapeDtypeStruct((lp, rw), BF16),
                   jax.ShapeDtypeStruct((lp, rw), BF16),
                   jax.ShapeDtypeStruct((lp, rw), BF16),
                   jax.ShapeDtypeStruct((lp, rw), F32),
                   jax.ShapeDtypeStruct((lp, sw), F32)),
        scratch_shapes=[pltpu.VMEM((tm, proj), F32)],
        compiler_params=_params(("arbitrary",)),
    )(h, nw, w_in, cosf, sinf)


def _inproj_bwd(dh, h, nw, n, w_in, dq, dk, dv, dg, du):
    lp, d = h.shape
    nck, _, ps = w_in.shape
    rw = dq.shape[1]
    sw = du.shape[1]
    proj = nck * ps
    tm = _tile(lp, 640)
    last = lp // tm - 1

    def gather_dproj(p_sc, dq_ref, dk_ref, dv_ref, dg_ref, du_ref):
        p_sc[:, 0:rw] = dq_ref[...]
        p_sc[:, rw:2 * rw] = dk_ref[...]
        p_sc[:, 2 * rw:3 * rw] = dv_ref[...]
        p_sc[:, 3 * rw:4 * rw] = dg_ref[...]
        p_sc[:, 4 * rw:] = du_ref[...]

    def act_body(dh_ref, h_ref, nw_ref, w_ref, dq_ref, dk_ref, dv_ref, dg_ref, du_ref, dhi_ref, dnw_ref, p_sc):
        i = pl.program_id(0)

        @pl.when(i == 0)
        def _():
            dnw_ref[...] = jnp.zeros_like(dnw_ref)

        gather_dproj(p_sc, dq_ref, dk_ref, dv_ref, dg_ref, du_ref)
        dn = jnp.zeros((tm, d), F32)
        for c in range(nck):
            dn = dn + _dot_nt(p_sc[:, c * ps:(c + 1) * ps], w_ref[c])
        xh, r = _rms_stats(h_ref[...])
        dhi_ref[...] = dh_ref[...] + _rms_bwd(dn, xh, r, nw_ref[...])
        dnw_ref[...] += jnp.sum(dn * xh, axis=0, keepdims=True)

    def w_body(n_ref, dq_ref, dk_ref, dv_ref, dg_ref, du_ref, dw_ref, p_sc, acc_sc):
        i = pl.program_id(0)

        @pl.when(i == 0)
        def _():
            acc_sc[...] = jnp.zeros_like(acc_sc)

        gather_dproj(p_sc, dq_ref, dk_ref, dv_ref, dg_ref, du_ref)
        nv = n_ref[...]
        for c in range(nck):
            acc_sc[c] += _dot_tn(nv, p_sc[:, c * ps:(c + 1) * ps])

        @pl.when(i == last)
        def _():
            dw_ref[...] = acc_sc[...].astype(BF16)

    row = lambda w: pl.BlockSpec((tm, w), lambda i: (i, 0))
    vec = pl.BlockSpec((1, d), lambda i: (0, 0))
    wsp = pl.BlockSpec((nck, d, ps), lambda i: (0, 0, 0))
    dproj_specs = [row(rw), row(rw), row(rw), row(rw), row(sw)]
    dhi, dnw = pl.pallas_call(
        act_body, name="inproj_bwd_act", grid=(lp // tm,),
        in_specs=[row(d), row(d), vec, wsp] + dproj_specs,
        out_specs=(row(d), vec),
        out_shape=(jax.ShapeDtypeStruct((lp, d), F32), jax.ShapeDtypeStruct((1, d), F32)),
        scratch_shapes=[pltpu.VMEM((tm, proj), BF16)],
        compiler_params=_params(("arbitrary",)),
    )(dh, h, nw, w_in, dq, dk, dv, dg, du)
    dw = pl.pallas_call(
        w_body, name="inproj_bwd_w", grid=(lp // tm,),
        in_specs=[row(d)] + dproj_specs,
        out_specs=wsp, out_shape=jax.ShapeDtypeStruct((nck, d, ps), BF16),
        scratch_shapes=[pltpu.VMEM((tm, proj), BF16), pltpu.VMEM((nck, d, ps), F32)],
        compiler_params=_params(("arbitrary",)),
    )(n, dq, dk, dv, dg, du)
    return dhi, dnw, dw


def _retention_tables(rc):
    h = jnp.arange(RET_HEADS, dtype=F32)
    log_g = jnp.log(1.0 - 2.0 ** (-5.0 - h))
    i = jnp.arange(rc)
    diff = i[:, None] - i[None, :]
    dec = jnp.where(diff[None] >= 0,
                    jnp.exp(log_g[:, None, None] * jnp.maximum(diff, 0)[None].astype(F32)), 0.0)
    pos = jnp.arange(rc, dtype=F32)
    wq = jnp.exp(log_g[:, None] * (pos + 1.0)[None])
    wk = jnp.exp(log_g[:, None] * (rc - 1 - pos)[None])
    gch = jnp.exp(log_g * rc)
    ones = jnp.ones((1, 1, HEAD_DIM), F32)
    return (dec, wq[:, :, None] * ones, wk[:, :, None] * ones,
            gch[:, None, None] * jnp.ones((1, 8, HEAD_DIM), F32))


def _head_norm(o):
    mu = jnp.mean(o, axis=-1, keepdims=True)
    oc = o - mu
    r = lax.rsqrt(jnp.mean(oc * oc, axis=-1, keepdims=True) + EPS)
    return oc * r, r


def _ret_fwd(q, k, v, g, rnw, tables):
    lp, rw = q.shape
    heads = rw // HEAD_DIM
    rc = tables[0].shape[1]
    nch = lp // rc
    dec, wq, wk, gch = tables

    def body(q_ref, k_ref, v_ref, g_ref, w_ref, dec_ref, wq_ref, wk_ref, gch_ref,
             o_ref, ret_ref, sp_ref, s_sc):
        n = pl.program_id(0)

        @pl.when(n == 0)
        def _():
            s_sc[...] = jnp.zeros_like(s_sc)

        cols = [slice(hh * HEAD_DIM, (hh + 1) * HEAD_DIM) for hh in range(heads)]
        s_ins = [s_sc[hh] for hh in range(heads)]
        outs = []
        for hh, cs in enumerate(cols):
            qv, kv, vv = q_ref[:, cs], k_ref[:, cs], v_ref[:, cs]
            s_in = s_ins[hh]
            a = _dot_nt(qv, kv) * dec_ref[hh]
            qw = (qv.astype(F32) * wq_ref[hh]).astype(BF16)
            kw = (kv.astype(F32) * wk_ref[hh]).astype(BF16)
            o = _dot(a.astype(BF16), vv) + _dot(qw, s_in.astype(BF16))
            s_new = gch_ref[hh, 0:1, :] * s_in + _dot_tn(kw, vv)
            xh, _ = _head_norm(o)
            gv = g_ref[:, cs]
            outs.append((o, s_new, (gv * _sigmoid(gv) * (xh * w_ref[:, cs])).astype(BF16)))
        for hh, cs in enumerate(cols):
            o, s_new, ret = outs[hh]
            sp_ref[hh, 0] = s_ins[hh]
            s_sc[hh] = s_new
            o_ref[:, cs] = o
            ret_ref[:, cs] = ret

    blk = pl.BlockSpec((rc, rw), lambda n: (n, 0))
    tab = pl.BlockSpec((heads, rc, HEAD_DIM), lambda n: (0, 0, 0))
    dtab = pl.BlockSpec((heads, rc, rc), lambda n: (0, 0, 0))
    return pl.pallas_call(
        body, name="retention_fwd", grid=(nch,),
        in_specs=[blk, blk, blk, blk, pl.BlockSpec((1, rw), lambda n: (0, 0)),
                  dtab, tab, tab, pl.BlockSpec((heads, 8, HEAD_DIM), lambda n: (0, 0, 0))],
        out_specs=(blk, blk, pl.BlockSpec((heads, 1, HEAD_DIM, HEAD_DIM), lambda n: (0, n, 0, 0))),
        out_shape=(jax.ShapeDtypeStruct((lp, rw), F32),
                   jax.ShapeDtypeStruct((lp, rw), BF16),
                   jax.ShapeDtypeStruct((heads, nch, HEAD_DIM, HEAD_DIM), F32)),
        scratch_shapes=[pltpu.VMEM((heads, HEAD_DIM, HEAD_DIM), F32)],
        compiler_params=_params(("arbitrary",)),
    )(q, k, v, g, rnw, dec, wq, wk, gch)


def _ret_bwd(dret, q, k, v, g, o, sprev, rnw, tables, cosf, sinf):
    lp, rw = q.shape
    heads = rw // HEAD_DIM
    rc = tables[0].shape[1]
    nch = lp // rc
    dec, wq, wk, gch = tables
    scale = HEAD_DIM ** -0.5
    half = HEAD_DIM // 2

    def body(dret_ref, q_ref, k_ref, v_ref, g_ref, o_ref, sp_ref, w_ref, dec_ref, wq_ref, wk_ref, gch_ref,
             cos_ref, sin_ref, dq_ref, dk_ref, dv_ref, dg_ref, dw_ref, ds_sc):
        n = pl.program_id(0)

        @pl.when(n == 0)
        def _():
            ds_sc[...] = jnp.zeros_like(ds_sc)
            dw_ref[...] = jnp.zeros_like(dw_ref)

        cosv = cos_ref[...]
        sinv = sin_ref[...]
        cols = [slice(hh * HEAD_DIM, (hh + 1) * HEAD_DIM) for hh in range(heads)]
        ds_ins = [ds_sc[hh] for hh in range(heads)]
        dw_ins = [dw_ref[:, cs] for cs in cols]
        outs = []
        for hh, cs in enumerate(cols):
            qv, kv, vv = q_ref[:, cs], k_ref[:, cs], v_ref[:, cs]
            gv = g_ref[:, cs]
            dr = dret_ref[:, cs]
            w = w_ref[:, cs]
            sg = _sigmoid(gv)
            sil = gv * sg
            xh, r = _head_norm(o_ref[:, cs])
            dgate = (dr * (xh * w) * (sg * (1.0 + gv * (1.0 - sg)))).astype(BF16)
            dyw = dr * sil
            dw_new = dw_ins[hh] + jnp.sum(dyw * xh, axis=0, keepdims=True)
            dxh = dyw * w
            do = r * (dxh - jnp.mean(dxh, axis=-1, keepdims=True)
                      - xh * jnp.mean(dxh * xh, axis=-1, keepdims=True))
            dob = do.astype(BF16)
            dmask = dec_ref[hh]
            wqv = wq_ref[hh]
            wkv = wk_ref[hh]
            a = (_dot_nt(qv, kv) * dmask).astype(BF16)
            da = (_dot_nt(dob, vv) * dmask).astype(BF16)
            qw = (qv.astype(F32) * wqv).astype(BF16)
            kw = (kv.astype(F32) * wkv).astype(BF16)
            s_in = sp_ref[hh, 0].astype(BF16)
            ds = ds_ins[hh]
            dsb = ds.astype(BF16)
            dq = _dot(da, kv) + _dot_nt(dob, s_in) * wqv
            dk = _dot_tn(da, qv) + _dot_nt(vv, dsb) * wkv
            dv = _dot_tn(a, dob) + _dot(kw, dsb)
            ds_new = gch_ref[hh, 0:1, :] * ds + _dot_tn(qw, dob)
            outs.append((dgate, dw_new, ds_new,
                         (dq * cosv + pltpu.roll(dq * sinv, half, 1)).astype(BF16),
                         ((dk * cosv + pltpu.roll(dk * sinv, half, 1)) * scale).astype(BF16),
                         dv.astype(BF16)))
        for hh, cs in enumerate(cols):
            dgate, dw_new, ds_new, dqv, dkv, dvv = outs[hh]
            dg_ref[:, cs] = dgate
            dw_ref[:, cs] = dw_new
            ds_sc[hh] = ds_new
            dq_ref[:, cs] = dqv
            dk_ref[:, cs] = dkv
            dv_ref[:, cs] = dvv

    blk = pl.BlockSpec((rc, rw), lambda n: (nch - 1 - n, 0))
    tab = pl.BlockSpec((heads, rc, HEAD_DIM), lambda n: (0, 0, 0))
    dtab = pl.BlockSpec((heads, rc, rc), lambda n: (0, 0, 0))
    wsp = pl.BlockSpec((1, rw), lambda n: (0, 0))
    pos = pl.BlockSpec((rc, HEAD_DIM), lambda n: (nch - 1 - n, 0))
    bshape = jax.ShapeDtypeStruct((lp, rw), BF16)
    return pl.pallas_call(
        body, name="retention_bwd", grid=(nch,),
        in_specs=[blk, blk, blk, blk, blk, blk,
                  pl.BlockSpec((heads, 1, HEAD_DIM, HEAD_DIM), lambda n: (0, nch - 1 - n, 0, 0)),
                  wsp, dtab, tab, tab, pl.BlockSpec((heads, 8, HEAD_DIM), lambda n: (0, 0, 0)), pos, pos],
        out_specs=(blk, blk, blk, blk, wsp),
        out_shape=(bshape, bshape, bshape, bshape, jax.ShapeDtypeStruct((1, rw), F32)),
        scratch_shapes=[pltpu.VMEM((heads, HEAD_DIM, HEAD_DIM), F32)],
        compiler_params=_params(("arbitrary",)),
    )(dret, q, k, v, g, o, sprev, rnw, dec, wq, wk, gch, cosf, sinf)


SCAN_CW = 512


def _s5_prepare(lam_re, lam_im, log_dt, b_re, b_im):
    dt = jnp.exp(log_dt)[:, None]
    er = jnp.exp(lam_re * dt)
    ar = er * jnp.cos(lam_im * dt)
    ai = er * jnp.sin(lam_im * dt)
    den = lam_re * lam_re + lam_im * lam_im
    fr = ((ar - 1.0) * lam_re + ai * lam_im) / den
    fi = (ai * lam_re - (ar - 1.0) * lam_im) / den
    bbr = fr[..., None] * b_re - fi[..., None] * b_im
    bbi = fr[..., None] * b_im + fi[..., None] * b_re
    return ar, ai, bbr, bbi


def _blockdiag_in(t):
    g, p, n = t.shape
    gs = g // N_SEC
    t = t.reshape(N_SEC, gs, p, n)
    eye = jnp.eye(gs, dtype=t.dtype)
    return jnp.einsum("sgpn,gh->sgphn", t, eye).reshape(N_SEC, gs * p, gs * n)


def _blockdiag_out(m, g, p, n):
    gs = g // N_SEC
    m = m.reshape(N_SEC, gs, p, gs, n)
    eye = jnp.eye(gs, dtype=m.dtype)
    return jnp.einsum("sgphn,gh->sgpn", m, eye).reshape(g, p, n)


def _scan_step(xr_ref, xi_ref, r0, prev, ar_ref, ai_ref, conj, ncols):
    new = []
    for cc in range(ncols // SCAN_CW):
        cs = pl.ds(cc * SCAN_CW, SCAN_CW)
        pr, pi = prev[cc]
        ar = ar_ref[:, cs]
        ai = ai_ref[:, cs]
        if conj:
            nr = ar * pr + ai * pi
            ni = ar * pi - ai * pr
        else:
            nr = ar * pr - ai * pi
            ni = ar * pi + ai * pr
        xr = xr_ref[pl.ds(r0, 8), cs] + nr
        xi = xi_ref[pl.ds(r0, 8), cs] + ni
        xr_ref[pl.ds(r0, 8), cs] = xr
        xi_ref[pl.ds(r0, 8), cs] = xi
        new.append((xr, xi))
    return new


def _scan_chunks(ncols):
    return [pl.ds(cc * SCAN_CW, SCAN_CW) for cc in range(ncols // SCAN_CW)]


def _flat(pairs):
    return tuple(t for p in pairs for t in p)


def _pairs(flat):
    return [(flat[2 * k], flat[2 * k + 1]) for k in range(len(flat) // 2)]


def _shift_rows(z, down):
    row = lax.broadcasted_iota(jnp.int32, z.shape, 0)
    if down:
        return jnp.where(row == 0, 0.0, pltpu.roll(z, 1, 0))
    return jnp.where(row == N_SEG - 1, 0.0, pltpu.roll(z, N_SEG - 1, 0))


def _s5_fwd(u, bsr, bsi, csr, csi, a8r, a8i, al8r, al8i, d, gluw, glub, nw, jb):
    lp, sw = u.shape
    ns = a8r.shape[1]
    rows = N_SEG * jb
    nblk = lp // rows
    secw = sw // N_SEC
    secn = ns // N_SEC

    def local_scan(u_ref, bsr_ref, bsi_ref, ar_ref, ai_ref, xr_ref, xi_ref, pr_sc, pi_sc):
        for s in range(N_SEC):
            ub = u_ref[:, s * secw:(s + 1) * secw].astype(BF16)
            xr_ref[:, s * secn:(s + 1) * secn] = _dot(ub, bsr_ref[s])
            xi_ref[:, s * secn:(s + 1) * secn] = _dot(ub, bsi_ref[s])
        prev = [(pr_sc[:, cs], pi_sc[:, cs]) for cs in _scan_chunks(ns)]
        prev = _scan_step(xr_ref, xi_ref, 0, prev, ar_ref, ai_ref, False, ns)

        def step(j, carry):
            r0 = pl.multiple_of(j * 8, 8)
            return _flat(_scan_step(xr_ref, xi_ref, r0, _pairs(carry), ar_ref, ai_ref, False, ns))

        last = _pairs(lax.fori_loop(1, jb, step, _flat(prev)))
        for cs, (vr, vi) in zip(_scan_chunks(ns), last):
            pr_sc[:, cs] = vr
            pi_sc[:, cs] = vi

    def carry_body(u_ref, bsr_ref, bsi_ref, ar_ref, ai_ref, alr_ref, ali_ref, cr_ref, ci_ref,
                   xr_sc, xi_sc, pr_sc, pi_sc):
        b = pl.program_id(0)

        @pl.when(b == 0)
        def _():
            pr_sc[...] = jnp.zeros_like(pr_sc)
            pi_sc[...] = jnp.zeros_like(pi_sc)

        local_scan(u_ref, bsr_ref, bsi_ref, ar_ref, ai_ref, xr_sc, xi_sc, pr_sc, pi_sc)

        @pl.when(b == nblk - 1)
        def _():
            er = _shift_rows(pr_sc[...], True)
            ei = _shift_rows(pi_sc[...], True)
            alr, ali = alr_ref[...], ali_ref[...]
            cr, ci = er, ei
            for _ in range(N_SEG - 2):
                sr = _shift_rows(cr, True)
                si = _shift_rows(ci, True)
                cr = er + alr * sr - ali * si
                ci = ei + alr * si + ali * sr
            cr_ref[...] = cr
            ci_ref[...] = ci

    ublk = pl.BlockSpec((rows, sw), lambda b: (b, 0))
    bspec = pl.BlockSpec((N_SEC, secw, secn), lambda b: (0, 0, 0))
    cspec = pl.BlockSpec((N_SEC, secn, secw), lambda b: (0, 0, 0))
    s8 = pl.BlockSpec((N_SEG, ns), lambda b: (0, 0))
    vec = pl.BlockSpec((1, sw), lambda b: (0, 0))
    s8shape = jax.ShapeDtypeStruct((N_SEG, ns), F32)
    c0r, c0i = pl.pallas_call(
        carry_body, name="s5_fwd_carry", grid=(nblk,),
        in_specs=[ublk, bspec, bspec, s8, s8, s8, s8],
        out_specs=(s8, s8), out_shape=(s8shape, s8shape),
        scratch_shapes=[pltpu.VMEM((rows, ns), F32), pltpu.VMEM((rows, ns), F32),
                        pltpu.VMEM((N_SEG, ns), F32), pltpu.VMEM((N_SEG, ns), F32)],
        compiler_params=_params(("arbitrary",)),
    )(u, bsr, bsi, a8r, a8i, al8r, al8i)

    def main_body(u_ref, bsr_ref, bsi_ref, csr_ref, csi_ref, ar_ref, ai_ref, c0r_ref, c0i_ref,
                  d_ref, gw_ref, gb_ref, nw_ref, xr_ref, xi_ref, yp_ref, out_ref, pr_sc, pi_sc):
        b = pl.program_id(0)

        @pl.when(b == 0)
        def _():
            pr_sc[...] = c0r_ref[...]
            pi_sc[...] = c0i_ref[...]

        local_scan(u_ref, bsr_ref, bsi_ref, ar_ref, ai_ref, xr_ref, xi_ref, pr_sc, pi_sc)
        for s in range(N_SEC):
            xs = pl.ds(s * secn, secn)
            us = pl.ds(s * secw, secw)
            y = _dot(xr_ref[:, xs].astype(BF16), csr_ref[s]) + _dot(xi_ref[:, xs].astype(BF16), csi_ref[s])
            yp_ref[:, us] = y + d_ref[:, us] * u_ref[:, us]
        yp = yp_ref[...]
        t = jnp.tanh(GELU_K0 * (yp + GELU_K1 * yp * yp * yp))
        y1 = 0.5 * yp * (1.0 + t)
        z = _dot(y1.astype(BF16), gw_ref[...]) + gb_ref[...]
        y2 = y1 * _sigmoid(z)
        xh, _ = _rms_stats(y2)
        out_ref[...] = (xh * nw_ref[...]).astype(BF16)

    xblk = pl.BlockSpec((rows, ns), lambda b: (b, 0))
    xr, xi, yp, out = pl.pallas_call(
        main_body, name="s5_fwd", grid=(nblk,),
        in_specs=[ublk, bspec, bspec, cspec, cspec, s8, s8, s8, s8, vec,
                  pl.BlockSpec((sw, sw), lambda b: (0, 0)), vec, vec],
        out_specs=(xblk, xblk, ublk, ublk),
        out_shape=(jax.ShapeDtypeStruct((lp, ns), F32), jax.ShapeDtypeStruct((lp, ns), F32),
                   jax.ShapeDtypeStruct((lp, sw), F32), jax.ShapeDtypeStruct((lp, sw), BF16)),
        scratch_shapes=[pltpu.VMEM((N_SEG, ns), F32), pltpu.VMEM((N_SEG, ns), F32)],
        compiler_params=_params(("arbitrary",)),
    )(u, bsr, bsi, csr, csi, a8r, a8i, c0r, c0i, d, gluw, glub, nw)
    return xr, xi, c0r, c0i, yp, out


def _s5_bwd(dout, u, yp, xr, xi, c0r, c0i, bsrt, bsit, csrt, csit, a8r, a8i, al8r, al8i, d, gluw, glub, nw, jb):
    lp, sw = u.shape
    ns = a8r.shape[1]
    rows = N_SEG * jb
    nblk = lp // rows
    secw = sw // N_SEC
    secn = ns // N_SEC

    def rowwise_bwd(dout_ref, yp_ref, gw_ref, gb_ref, nw_ref):
        ypv = yp_ref[...]
        t = jnp.tanh(GELU_K0 * (ypv + GELU_K1 * ypv * ypv * ypv))
        y1 = 0.5 * ypv * (1.0 + t)
        dgelu = 0.5 * (1.0 + t) + 0.5 * ypv * (1.0 - t * t) * GELU_K0 * (1.0 + 3.0 * GELU_K1 * ypv * ypv)
        gw = gw_ref[...]
        y1b = y1.astype(BF16)
        sg = _sigmoid(_dot(y1b, gw) + gb_ref[...])
        xh, r = _rms_stats(y1 * sg)
        dov = dout_ref[...]
        dy2 = _rms_bwd(dov, xh, r, nw_ref[...])
        dz = dy2 * y1 * sg * (1.0 - sg)
        dzb = dz.astype(BF16)
        dy1 = dy2 * sg + _dot_nt(dzb, gw)
        return dy1 * dgelu, dov * xh, y1b, dzb, dz

    def lam_scan(dyp_of, csrt_ref, csit_ref, ar_ref, ai_ref, lr_sc, li_sc, nr_sc, ni_sc, extra):
        for s in range(N_SEC):
            db = dyp_of(s)
            lr_sc[:, s * secn:(s + 1) * secn] = _dot(db, csrt_ref[s])
            li_sc[:, s * secn:(s + 1) * secn] = _dot(db, csit_ref[s])
        top = rows - 8
        prev = [(nr_sc[:, cs], ni_sc[:, cs]) for cs in _scan_chunks(ns)]
        prev = _scan_step(lr_sc, li_sc, top, prev, ar_ref, ai_ref, True, ns)
        extra(top, pl.ds(top - 8, 8))

        def step(jj, carry):
            r0 = pl.multiple_of((jb - 1 - jj) * 8, 8)
            rp = pl.multiple_of((jb - 2 - jj) * 8, 8)
            new = _scan_step(lr_sc, li_sc, r0, _pairs(carry), ar_ref, ai_ref, True, ns)
            extra(r0, pl.ds(rp, 8))
            return _flat(new)

        prev = _pairs(lax.fori_loop(1, jb - 1, step, _flat(prev)))
        last = _scan_step(lr_sc, li_sc, 0, prev, ar_ref, ai_ref, True, ns)
        extra(0, None)
        for cs, (vr, vi) in zip(_scan_chunks(ns), last):
            nr_sc[:, cs] = vr
            ni_sc[:, cs] = vi

    def carry_body(dout_ref, yp_ref, u_ref, gw_ref, gb_ref, nw_ref, csrt_ref, csit_ref, ar_ref, ai_ref,
                   alr_ref, ali_ref, cr_ref, ci_ref, dyp_ref, dnw_ref, dgw_ref, dgb_ref, dd_ref,
                   lr_sc, li_sc, nr_sc, ni_sc):
        b = pl.program_id(0)

        @pl.when(b == 0)
        def _():
            nr_sc[...] = jnp.zeros_like(nr_sc)
            ni_sc[...] = jnp.zeros_like(ni_sc)
            for ref in (dnw_ref, dgw_ref, dgb_ref, dd_ref):
                ref[...] = jnp.zeros_like(ref)

        dyp, dnw_rows, y1b, dzb, dz = rowwise_bwd(dout_ref, yp_ref, gw_ref, gb_ref, nw_ref)
        dnw_ref[...] += jnp.sum(dnw_rows, axis=0, keepdims=True)
        dgw_ref[...] += _dot_tn(y1b, dzb)
        dgb_ref[...] += jnp.sum(dz, axis=0, keepdims=True)
        dd_ref[...] += jnp.sum(dyp * u_ref[...], axis=0, keepdims=True)
        dyp_ref[...] = dyp.astype(BF16)
        lam_scan(lambda s: dyp_ref[:, s * secw:(s + 1) * secw], csrt_ref, csit_ref, ar_ref, ai_ref,
                 lr_sc, li_sc, nr_sc, ni_sc, lambda r0, prev_rows: None)

        @pl.when(b == nblk - 1)
        def _():
            fr = _shift_rows(nr_sc[...], False)
            fi = _shift_rows(ni_sc[...], False)
            alr, ali = alr_ref[...], ali_ref[...]
            cr, ci = fr, fi
            for _ in range(N_SEG - 2):
                sr = _shift_rows(cr, False)
                si = _shift_rows(ci, False)
                cr = fr + alr * sr + ali * si
                ci = fi + alr * si - ali * sr
            cr_ref[...] = cr
            ci_ref[...] = ci

    rev = lambda b: (nblk - 1 - b, 0)
    ublk = pl.BlockSpec((rows, sw), rev)
    xblk = pl.BlockSpec((rows, ns), rev)
    s8 = pl.BlockSpec((N_SEG, ns), lambda b: (0, 0))
    vec = pl.BlockSpec((1, sw), lambda b: (0, 0))
    gws = pl.BlockSpec((sw, sw), lambda b: (0, 0))
    btspec = pl.BlockSpec((N_SEC, secn, secw), lambda b: (0, 0, 0))
    ctspec = pl.BlockSpec((N_SEC, secw, secn), lambda b: (0, 0, 0))
    s8shape = jax.ShapeDtypeStruct((N_SEG, ns), F32)
    lcr, lci, dyp_all, d_nw, d_gw, d_gb, d_d = pl.pallas_call(
        carry_body, name="s5_bwd_carry", grid=(nblk,),
        in_specs=[ublk, ublk, ublk, gws, vec, vec, ctspec, ctspec, s8, s8, s8, s8],
        out_specs=(s8, s8, ublk, vec, gws, vec, vec),
        out_shape=(s8shape, s8shape, jax.ShapeDtypeStruct((lp, sw), BF16), jax.ShapeDtypeStruct((1, sw), F32),
                   jax.ShapeDtypeStruct((sw, sw), F32), jax.ShapeDtypeStruct((1, sw), F32),
                   jax.ShapeDtypeStruct((1, sw), F32)),
        scratch_shapes=[pltpu.VMEM((rows, ns), F32), pltpu.VMEM((rows, ns), F32),
                        pltpu.VMEM((N_SEG, ns), F32), pltpu.VMEM((N_SEG, ns), F32)],
        compiler_params=_params(("arbitrary",)),
    )(dout, yp, u, gluw, glub, nw, csrt, csit, a8r, a8i, al8r, al8i)

    def main_body(dyp_sc, u_ref, xr_ref, xi_ref, xtr_ref, xti_ref, c0r_ref, c0i_ref, lcr_ref, lci_ref,
                  d_ref, bsrt_ref, bsit_ref, csrt_ref, csit_ref, ar_ref, ai_ref,
                  du_ref, dcr_ref, dci_ref, dbr_ref, dbi_ref, dar_ref, dai_ref,
                  lr_sc, li_sc, nr_sc, ni_sc):
        b = pl.program_id(0)

        @pl.when(b == 0)
        def _():
            nr_sc[...] = lcr_ref[...]
            ni_sc[...] = lci_ref[...]
            for ref in (dcr_ref, dci_ref, dbr_ref, dbi_ref, dar_ref, dai_ref):
                ref[...] = jnp.zeros_like(ref)

        for s in range(N_SEC):
            db = dyp_sc[:, s * secw:(s + 1) * secw]
            xs = pl.ds(s * secn, secn)
            dcr_ref[s] += _dot_tn(xr_ref[:, xs].astype(BF16), db)
            dci_ref[s] += _dot_tn(xi_ref[:, xs].astype(BF16), db)

        first = b == nblk - 1

        def acc_da(r0, prev_rows):
            for cc in range(ns // SCAN_CW):
                cs = pl.ds(cc * SCAN_CW, SCAN_CW)
                lr = lr_sc[pl.ds(r0, 8), cs]
                li = li_sc[pl.ds(r0, 8), cs]
                if prev_rows is None:
                    xpr = jnp.where(first, c0r_ref[:, cs], xtr_ref[:, cs])
                    xpi = jnp.where(first, c0i_ref[:, cs], xti_ref[:, cs])
                else:
                    xpr = xr_ref[prev_rows, cs]
                    xpi = xi_ref[prev_rows, cs]
                dar_ref[:, cs] += lr * xpr + li * xpi
                dai_ref[:, cs] += li * xpr - lr * xpi

        lam_scan(lambda s: dyp_sc[:, s * secw:(s + 1) * secw], csrt_ref, csit_ref, ar_ref, ai_ref,
                 lr_sc, li_sc, nr_sc, ni_sc, acc_da)

        for s in range(N_SEC):
            xs = pl.ds(s * secn, secn)
            us = pl.ds(s * secw, secw)
            lrb = lr_sc[:, xs].astype(BF16)
            lib = li_sc[:, xs].astype(BF16)
            du = _dot(lrb, bsrt_ref[s]) + _dot(lib, bsit_ref[s]) + d_ref[:, us] * dyp_sc[:, us].astype(F32)
            du_ref[:, us] = du.astype(BF16)
            ub = u_ref[:, us].astype(BF16)
            dbr_ref[s] += _dot_tn(ub, lrb)
            dbi_ref[s] += _dot_tn(ub, lib)

    tail = pl.BlockSpec((N_SEG, ns), lambda b: (jnp.maximum((nblk - 1 - b) * jb - 1, 0), 0))
    acc_c = pl.BlockSpec((N_SEC, secn, secw), lambda b: (0, 0, 0))
    acc_b = pl.BlockSpec((N_SEC, secw, secn), lambda b: (0, 0, 0))
    du, dcr, dci, dbr, dbi, dar, dai = pl.pallas_call(
        main_body, name="s5_bwd", grid=(nblk,),
        in_specs=[ublk, ublk, xblk, xblk, tail, tail, s8, s8, s8, s8,
                  vec, btspec, btspec, ctspec, ctspec, s8, s8],
        out_specs=(ublk, acc_c, acc_c, acc_b, acc_b, s8, s8),
        out_shape=(jax.ShapeDtypeStruct((lp, sw), BF16),
                   jax.ShapeDtypeStruct((N_SEC, secn, secw), F32),
                   jax.ShapeDtypeStruct((N_SEC, secn, secw), F32),
                   jax.ShapeDtypeStruct((N_SEC, secw, secn), F32),
                   jax.ShapeDtypeStruct((N_SEC, secw, secn), F32),
                   s8shape, s8shape),
        scratch_shapes=[pltpu.VMEM((rows, ns), F32), pltpu.VMEM((rows, ns), F32),
                        pltpu.VMEM((N_SEG, ns), F32), pltpu.VMEM((N_SEG, ns), F32)],
        compiler_params=_params(("arbitrary",)),
    )(dyp_all, u, xr, xi, xr, xi, c0r, c0i, lcr, lci, d, bsrt, bsit, csrt, csit, a8r, a8i)
    return du, d_nw, d_gw, d_gb, d_d, dcr, dci, dbr, dbi, dar, dai


def _outproj_fwd(h, ret, ssm, wo):
    lp, d = h.shape
    nck, rs, _ = wo.shape
    rw = ret.shape[1]
    tm = _tile(lp, 640)
    per = rw // rs

    def body(h_ref, ret_ref, ssm_ref, w_ref, o_ref):
        acc = h_ref[...]
        for c in range(nck):
            src = ret_ref if c < per else ssm_ref
            lo = (c % per) * rs
            acc = acc + _dot(src[:, lo:lo + rs], w_ref[c])
        o_ref[...] = acc

    row = lambda w: pl.BlockSpec((tm, w), lambda i: (i, 0))
    return pl.pallas_call(
        body, name="outproj_fwd", grid=(lp // tm,),
        in_specs=[row(d), row(rw), row(ssm.shape[1]), pl.BlockSpec((nck, rs, d), lambda i: (0, 0, 0))],
        out_specs=row(d), out_shape=jax.ShapeDtypeStruct((lp, d), F32),
        compiler_params=_params(("arbitrary",)),
    )(h, ret, ssm, wo)


def _outproj_bwd(dh, ret, ssm, wo):
    lp, d = dh.shape
    nck, rs, _ = wo.shape
    rw = ret.shape[1]
    sw = ssm.shape[1]
    tm = _tile(lp, 640)
    per = rw // rs
    last = lp // tm - 1

    def body(dh_ref, ret_ref, ssm_ref, w_ref, dret_ref, dssm_ref, dw_ref, acc_sc):
        i = pl.program_id(0)

        @pl.when(i == 0)
        def _():
            acc_sc[...] = jnp.zeros_like(acc_sc)

        dhb = dh_ref[...].astype(BF16)
        for c in range(nck):
            src, dst = (ret_ref, dret_ref) if c < per else (ssm_ref, dssm_ref)
            lo = (c % per) * rs
            dst[:, lo:lo + rs] = _dot_nt(dhb, w_ref[c])
            acc_sc[c] += _dot_tn(src[:, lo:lo + rs], dhb)

        @pl.when(i == last)
        def _():
            dw_ref[...] = acc_sc[...].astype(BF16)

    row = lambda w: pl.BlockSpec((tm, w), lambda i: (i, 0))
    wsp = pl.BlockSpec((nck, rs, d), lambda i: (0, 0, 0))
    return pl.pallas_call(
        body, name="outproj_bwd", grid=(lp // tm,),
        in_specs=[row(d), row(rw), row(sw), wsp],
        out_specs=(row(rw), row(sw), wsp),
        out_shape=(jax.ShapeDtypeStruct((lp, rw), F32), jax.ShapeDtypeStruct((lp, sw), F32),
                   jax.ShapeDtypeStruct((nck, rs, d), BF16)),
        scratch_shapes=[pltpu.VMEM((nck, rs, d), F32)],
        compiler_params=_params(("arbitrary",)),
    )(dh, ret, ssm, wo)


def _loss_head(h, fw, target):
    lp, d = h.shape
    tm = _tile(lp, 640, CHUNK)
    sub = tm // CHUNK

    def body(h_ref, w_ref, *rest):
        t_refs = rest[:sub]
        loss_ref, dh_ref, dw_ref = rest[sub:]
        i = pl.program_id(0)

        @pl.when(i == 0)
        def _():
            loss_ref[...] = jnp.zeros_like(loss_ref)
            dw_ref[...] = jnp.zeros_like(dw_ref)

        w = w_ref[...]
        for j in range(sub):
            rows = pl.ds(j * CHUNK, CHUNK)
            xh, r = _rms_stats(h_ref[rows, :])
            err = xh * w - t_refs[j][...]
            if j == 0:
                err = jnp.where(i == 0, 0.0, err)
            loss_ref[...] += 0.5 * jnp.sum(err * err) / d
            dout = err * (1.0 / d)
            dw_ref[...] += jnp.sum(dout * xh, axis=0, keepdims=True)
            dh_ref[rows, :] = _rms_bwd(dout, xh, r, w)

    t_spec = lambda j: pl.BlockSpec((CHUNK, d), lambda i: (jnp.maximum(i * sub + j - 1, 0), 0))
    return pl.pallas_call(
        body, name="loss_head", grid=(lp // tm,),
        in_specs=[pl.BlockSpec((tm, d), lambda i: (i, 0)), pl.BlockSpec((1, d), lambda i: (0, 0))]
        + [t_spec(j) for j in range(sub)],
        out_specs=(pl.BlockSpec((8, LANE), lambda i: (0, 0)), pl.BlockSpec((tm, d), lambda i: (i, 0)),
                   pl.BlockSpec((1, d), lambda i: (0, 0))),
        out_shape=(jax.ShapeDtypeStruct((8, LANE), F32), jax.ShapeDtypeStruct((lp, d), F32),
                   jax.ShapeDtypeStruct((1, d), F32)),
        compiler_params=_params(("arbitrary",)),
    )(h, fw, *([target] * sub))


def _pack(arrs):
    flat = jnp.concatenate([a.reshape(-1).astype(F32) for a in arrs])
    n = flat.shape[0]
    rows = -(-n // (8 * LANE)) * 8
    return jnp.pad(flat, (0, rows * LANE - n)).reshape(rows, LANE)


def _unpack(packed, shapes):
    flat = packed.reshape(-1)
    out, off = [], 0
    for s in shapes:
        n = math.prod(s)
        out.append(flat[off:off + n].reshape(s))
        off += n
    return out


def _to_segments(a, seg_len):
    return a.reshape(N_SEG, seg_len, a.shape[1]).transpose(1, 0, 2).reshape(a.shape)


def _from_segments(a, seg_len):
    return a.reshape(seg_len, N_SEG, a.shape[1]).transpose(1, 0, 2).reshape(a.shape)


WEIGHT_NAMES = ['meta_tokens', 'ffn1_norm_w', 'ffn1_w_gate', 'ffn1_w_up', 'ffn1_w_down', 'mix_norm_w', 'w_in',
                'ret_norm_w', 'ssm_lambda_re', 'ssm_lambda_im', 'ssm_log_dt', 'ssm_b_re', 'ssm_b_im', 'ssm_c_re',
                'ssm_c_im', 'ssm_d', 'ssm_glu_w', 'ssm_glu_b', 'ssm_norm_w', 'w_out', 'ffn2_norm_w', 'ffn2_w_gate',
                'ffn2_w_up', 'ffn2_w_down', 'final_norm_w']
BIG = ['ffn1_w_gate', 'ffn1_w_up', 'ffn1_w_down', 'w_in', 'ssm_glu_w', 'w_out', 'ffn2_w_gate', 'ffn2_w_up',
       'ffn2_w_down']
TRANSPOSED = ['ffn1_w_gate', 'ffn1_w_up', 'ffn2_w_gate', 'ffn2_w_up']
BIG_EARLY = ['ffn1_w_gate', 'ffn1_w_up', 'ffn1_w_down']
BIG_LATE = [n for n in BIG if n not in BIG_EARLY]
SMALL = [n for n in WEIGHT_NAMES if n not in BIG]


def kernel(x, meta_tokens, ffn1_norm_w, ffn1_w_gate, ffn1_w_up, ffn1_w_down, mix_norm_w, w_in, ret_norm_w, ssm_lambda_re, ssm_lambda_im, ssm_log_dt, ssm_b_re, ssm_b_im, ssm_c_re, ssm_c_im, ssm_d, ssm_glu_w, ssm_glu_b, ssm_norm_w, w_out, ffn2_norm_w, ffn2_w_gate, ffn2_w_up, ffn2_w_down, final_norm_w, loss_target, m_meta_tokens, m_ffn1_norm_w, m_ffn1_w_gate, m_ffn1_w_up, m_ffn1_w_down, m_mix_norm_w, m_w_in, m_ret_norm_w, m_ssm_lambda_re, m_ssm_lambda_im, m_ssm_log_dt, m_ssm_b_re, m_ssm_b_im, m_ssm_c_re, m_ssm_c_im, m_ssm_d, m_ssm_glu_w, m_ssm_glu_b, m_ssm_norm_w, m_w_out, m_ffn2_norm_w, m_ffn2_w_gate, m_ffn2_w_up, m_ffn2_w_down, m_final_norm_w, v_meta_tokens, v_ffn1_norm_w, v_ffn1_w_gate, v_ffn1_w_up, v_ffn1_w_down, v_mix_norm_w, v_w_in, v_ret_norm_w, v_ssm_lambda_re, v_ssm_lambda_im, v_ssm_log_dt, v_ssm_b_re, v_ssm_b_im, v_ssm_c_re, v_ssm_c_im, v_ssm_d, v_ssm_glu_w, v_ssm_glu_b, v_ssm_norm_w, v_w_out, v_ffn2_norm_w, v_ffn2_w_gate, v_ffn2_w_up, v_ffn2_w_down, v_final_norm_w):
    args = locals()
    w = {n: args[n] for n in WEIGHT_NAMES}
    m = {n: args["m_" + n] for n in WEIGHT_NAMES}
    v = {n: args["v_" + n] for n in WEIGHT_NAMES}

    seq, d = x.shape[1], x.shape[2]
    lp = seq + CHUNK
    seg_len = lp // N_SEG
    rw = RET_HEADS * HEAD_DIM
    sw = ssm_d.shape[-1]
    groups = sw // SSM_GROUP
    ns = groups * SSM_STATE
    jb = _tile(seg_len, S5_STEPS, 8)
    chip = 2 * lax.axis_index("x") + lax.axis_index("y")

    as_fd = lambda t: jnp.swapaxes(t, -1, -2)
    shards = {n: (as_fd(w[n][0]) if n in TRANSPOSED else w[n][0]).astype(BF16) for n in BIG}
    early = [shards[n] for n in BIG_EARLY] + [meta_tokens]
    gathered = _forward_sibling("gather_early_forward",
                                _exchange("gather_early", _allgather_chips_plan(early), early))
    gw = dict(zip(BIG_EARLY, gathered[:-1]))
    meta_full = jnp.transpose(gathered[-1], (1, 0, 2)).reshape(N_META, d)
    late = [shards[n] for n in BIG_LATE]

    pos = jnp.arange(lp, dtype=F32) - float(CHUNK - N_META)
    freqs = 1.0 / (ROPE_BASE ** (jnp.arange(0, HEAD_DIM, 2, dtype=F32) / HEAD_DIM))
    ang = pos[:, None] * freqs[None, :]
    cosf = jnp.concatenate([jnp.cos(ang), jnp.cos(ang)], axis=1)
    sinf = jnp.concatenate([-jnp.sin(ang), jnp.sin(ang)], axis=1)
    tables = _retention_tables(_tile(lp, RET_ROWS, CHUNK))

    lam_re, lam_im, log_dt = ssm_lambda_re[0], ssm_lambda_im[0], ssm_log_dt[0]
    b_re, b_im, c_re, c_im = ssm_b_re[0], ssm_b_im[0], ssm_c_re[0], ssm_c_im[0]
    (ar, ai, bbr, bbi), prep_vjp = jax.vjp(_s5_prepare, lam_re, lam_im, log_dt, b_re, b_im)
    dt = jnp.exp(log_dt)[:, None]
    el = jnp.exp(seg_len * lam_re * dt)
    alr = el * jnp.cos(seg_len * lam_im * dt)
    ali = el * jnp.sin(seg_len * lam_im * dt)
    bc8 = lambda t: jnp.broadcast_to(t.reshape(1, ns), (N_SEG, ns))
    a8r, a8i, al8r, al8i = bc8(ar), bc8(ai), bc8(alr), bc8(ali)
    bsr = _blockdiag_in(jnp.transpose(bbr, (0, 2, 1)))
    bsi = _blockdiag_in(jnp.transpose(bbi, (0, 2, 1)))
    csrt = _blockdiag_in(c_re)
    csit = _blockdiag_in(-c_im)
    tr = lambda t: jnp.transpose(t, (0, 2, 1))
    bsr_b, bsi_b = bsr.astype(BF16), bsi.astype(BF16)
    csr_b, csi_b = tr(csrt).astype(BF16), tr(csit).astype(BF16)
    bsrt_b, bsit_b = tr(bsr).astype(BF16), tr(bsi).astype(BF16)
    csrt_b, csit_b = csrt.astype(BF16), csit.astype(BF16)

    h0 = (jnp.concatenate([jnp.zeros((CHUNK - N_META, d), F32), meta_full], axis=0), x[0])
    (h1, g1, u1), late_half = _ffn_fwd("ffn1_fwd", h0, ffn1_norm_w, gw['ffn1_w_gate'], gw['ffn1_w_up'],
                                       gw['ffn1_w_down'], _allgather_chips_plan(late), late)
    gw.update(zip(BIG_LATE, _forward_sibling("gather_late_forward", late_half)))
    glu_full = gw['ssm_glu_w'].reshape(sw, sw)
    n2, q, k, vv, gate, u = _inproj_fwd(h1, mix_norm_w, gw['w_in'], cosf, sinf, rw)
    o, ret, sprev = _ret_fwd(q, k, vv, gate, ret_norm_w, tables)
    u_seg = _to_segments(u, seg_len)
    xr, xi, c0r, c0i, yp, ssm_seg = _s5_fwd(u_seg, bsr_b, bsi_b, csr_b, csi_b, a8r, a8i, al8r, al8i,
                                            ssm_d, glu_full, ssm_glu_b, ssm_norm_w, jb)
    ssm = _from_segments(ssm_seg, seg_len)
    h2 = _outproj_fwd(h1, ret, ssm, gw['w_out'])
    (h3, g2, u2), _ = _ffn_fwd("ffn2_fwd", h2, ffn2_norm_w, gw['ffn2_w_gate'], gw['ffn2_w_up'], gw['ffn2_w_down'])
    loss_part, dh3, d_final = _loss_head(h3, final_norm_w.reshape(1, d), loss_target[0])

    (dh2, d_ffn2_norm, nb, daccb, ab, dgb, dub), _ = _ffn_bwd_act(
        "ffn2_bwd_act", dh3, h2, ffn2_norm_w, g2, u2, gw['ffn2_w_gate'], gw['ffn2_w_up'], gw['ffn2_w_down'])
    (dwg2, dwu2, dwd2), _ = _ffn_bwd_w("ffn2_bwd_w", nb, daccb, ab, dgb, dub)
    dret, dssm, dwo = _outproj_bwd(dh2, ret, ssm, gw['w_out'])
    (du_seg, d_ssm_norm, d_glu_w, d_glu_b, d_ssm_d, dcr_s, dci_s, dbr_s, dbi_s, dar8, dai8) = _s5_bwd(
        _to_segments(dssm, seg_len), u_seg, yp, xr, xi, c0r, c0i, bsrt_b, bsit_b, csrt_b, csit_b,
        a8r, a8i, al8r, al8i, ssm_d, glu_full, ssm_glu_b, ssm_norm_w, jb)
    du = _from_segments(du_seg, seg_len)
    dq, dk, dv, dgate, d_ret_norm = _ret_bwd(dret, q, k, vv, gate, o, sprev, ret_norm_w, tables, cosf, sinf)
    dh1, d_mix_norm, dwin = _inproj_bwd(dh2, h1, mix_norm_w, n2, gw['w_in'], dq, dk, dv, dgate, du)
    late_parts = {
        'w_in': dwin, 'ssm_glu_w': d_glu_w.reshape(N_CHIP, sw // N_CHIP, sw).astype(BF16), 'w_out': dwo,
        'ffn2_w_gate': dwg2, 'ffn2_w_up': dwu2, 'ffn2_w_down': dwd2,
    }
    late_list = [late_parts[n] for n in BIG_LATE]
    (dh0, d_ffn1_norm, nb, daccb, ab, dgb, dub), late_recv = _ffn_bwd_act(
        "ffn1_bwd_act", dh1, h0, ffn1_norm_w, g1, u1, gw['ffn1_w_gate'], gw['ffn1_w_up'], gw['ffn1_w_down'],
        _alltoall_chips_plan(late_list), late_list)
    grad_x = dh0[CHUNK:][None]
    d_meta = dh0[CHUNK - N_META:CHUNK]

    d_c_re = jnp.transpose(_blockdiag_out(tr(dcr_s), groups, SSM_GROUP, SSM_STATE), (0, 1, 2))
    d_c_im = -_blockdiag_out(tr(dci_s), groups, SSM_GROUP, SSM_STATE)
    d_bbr = jnp.transpose(_blockdiag_out(dbr_s, groups, SSM_GROUP, SSM_STATE), (0, 2, 1))
    d_bbi = jnp.transpose(_blockdiag_out(dbi_s, groups, SSM_GROUP, SSM_STATE), (0, 2, 1))
    d_ar = jnp.sum(dar8, axis=0).reshape(groups, SSM_STATE)
    d_ai = jnp.sum(dai8, axis=0).reshape(groups, SSM_STATE)
    small_parts = [loss_part[0:1, :], d_meta, d_ffn1_norm, d_mix_norm, d_ret_norm, d_ar, d_ai, d_bbr, d_bbi,
                   d_c_re, d_c_im, d_ssm_d, d_glu_b, d_ssm_norm, d_ffn2_norm, d_final]
    small_shapes = [a.shape for a in small_parts]
    packed = _pack(small_parts)
    early_recv, (all_parts,) = _ffn_bwd_w_scatter("ffn1_bwd_w", nb, daccb, ab, dgb, dub, chip,
                                                  _allgather_all_plan([packed]), [packed])
    received = dict(zip(BIG_LATE + BIG_EARLY, late_recv + early_recv))
    ffn_names = [n for n in BIG if n.startswith('ffn')]
    chip_sum = dict(zip(ffn_names, _sum_slots("sum_chips_ffn", [received[n] for n in ffn_names], BF16)))
    for n in BIG:
        if n not in chip_sum:
            chip_sum[n] = _sum_slots("sum_chips_" + n, [received[n]], BF16)[0]
    chip_sums = [chip_sum[n] for n in BIG]
    sib_sums = _swap_sibling("swap_sibling", chip_sums)
    (loss_row, g_meta_full, g_ffn1_norm, g_mix_norm, g_ret_norm, g_ar, g_ai, g_bbr, g_bbi, g_c_re, g_c_im,
     g_ssm_d, g_glu_b, g_ssm_norm, g_ffn2_norm, g_final) = _unpack(_sum_slots("sum_small", [all_parts], F32)[0],
                                                                  small_shapes)
    g_lam_re, g_lam_im, g_log_dt, g_b_re, g_b_im = prep_vjp((g_ar, g_ai, g_bbr, g_bbi))
    loss = loss_row[0, 0]
    g_meta = lax.dynamic_slice(g_meta_full, (0, chip * (d // N_CHIP)), (N_META, d // N_CHIP))
    small_grads = {
        'meta_tokens': g_meta, 'ffn1_norm_w': g_ffn1_norm, 'mix_norm_w': g_mix_norm, 'ret_norm_w': g_ret_norm,
        'ssm_lambda_re': g_lam_re[None], 'ssm_lambda_im': g_lam_im[None], 'ssm_log_dt': g_log_dt[None],
        'ssm_b_re': g_b_re[None], 'ssm_b_im': g_b_im[None], 'ssm_c_re': g_c_re[None], 'ssm_c_im': g_c_im[None],
        'ssm_d': g_ssm_d, 'ssm_glu_b': g_glu_b, 'ssm_norm_w': g_ssm_norm, 'ffn2_norm_w': g_ffn2_norm,
        'final_norm_w': g_final.reshape(d),
    }

    grads, deltas, new_m, new_v = {}, {}, {}, {}
    g_pair = {n: [mine, sib] for n, mine, sib in zip(BIG, chip_sums, sib_sums)}
    view = lambda n, t: as_fd(t) if n in TRANSPOSED else t
    ffn_out = _adam("adam_ffn", [(view(n, w[n]), view(n, m[n]), view(n, v[n])) for n in ffn_names],
                    [g_pair[n] for n in ffn_names])
    for n, outs in zip(ffn_names, ffn_out):
        grads[n], deltas[n], new_m[n], new_v[n] = [view(n, t) for t in outs]
    for n in BIG:
        if n not in ffn_names:
            grads[n], deltas[n], new_m[n], new_v[n] = _adam("adam_" + n, [(w[n], m[n], v[n])], [g_pair[n]])[0]
    sm_shapes = [w[n].shape for n in SMALL]
    sm_out = _adam("adam_small", [(_pack([w[n] for n in SMALL]), _pack([m[n] for n in SMALL]),
                                  _pack([v[n] for n in SMALL]))],
                   [[_pack([small_grads[n].reshape(w[n].shape) for n in SMALL])]])[0]
    for dst, packed in zip((grads, deltas, new_m, new_v), sm_out):
        for n, t in zip(SMALL, _unpack(packed, sm_shapes)):
            dst[n] = t

    return (loss, grad_x, *[grads[n] for n in WEIGHT_NAMES], *[deltas[n] for n in WEIGHT_NAMES],
            *[new_m[n] for n in WEIGHT_NAMES], *[new_v[n] for n in WEIGHT_NAMES])
```

```python
import functools
import math

import jax
import jax.numpy as jnp
from jax import lax
from jax.experimental import pallas as pl
from jax.experimental.pallas import tpu as pltpu

N_META = 16
RET_HEADS = 4
HEAD_DIM = 128
SSM_GROUP = 16
SSM_STATE = 64
CHUNK = 128
ROPE_BASE = 10000.0
EPS = 1e-6
FFN_RES = 0.5
N_SEG = 8
N_SEC = 4
N_CHIP = 4
LANE = 128
FFN_CPS = 2
BWD_W_ROWS = 1664

ADAM_LR = 0.001
ADAM_B1 = 0.9
ADAM_B2 = 0.999
ADAM_EPS = 1e-08
ADAM_WD = 0.01
ADAM_STEP = 10

VMEM_LIMIT = 56 * 1024 * 1024

F32 = jnp.float32
BF16 = jnp.bfloat16
MESH = pl.DeviceIdType.MESH


def _dot(a, b):
    return jnp.dot(a, b, preferred_element_type=F32)


def _dot_nt(a, b):
    return lax.dot_general(a, b, (((1,), (1,)), ((), ())), preferred_element_type=F32)


def _dot_tn(a, b):
    return lax.dot_general(a, b, (((0,), (0,)), ((), ())), preferred_element_type=F32)


def _tile(n, target, mult=64):
    best = None
    t = mult
    while t <= min(n, target):
        if n % t == 0:
            best = t
        t += mult
    assert best is not None, (n, target)
    return best


def _params(sem, vmem=VMEM_LIMIT):
    return pltpu.CompilerParams(dimension_semantics=sem, vmem_limit_bytes=vmem)


def _rms_stats(xf):
    r = lax.rsqrt(jnp.mean(xf * xf, axis=-1, keepdims=True) + EPS)
    return xf * r, r


def _rms_bwd(dy, xh, r, w):
    dxh = dy * w
    return r * (dxh - xh * jnp.mean(dxh * xh, axis=-1, keepdims=True))


def _sigmoid(x):
    return 0.5 * jnp.tanh(0.5 * x) + 0.5


GELU_K0 = math.sqrt(2.0 / math.pi)
GELU_K1 = 0.044715


CHIP_MASKS = [(1, 0, 0), (0, 1, 0), (1, 1, 0)]
ALL_MASKS = [(0, 0, 1), (0, 1, 0), (0, 1, 1), (1, 0, 0), (1, 0, 1), (1, 1, 0), (1, 1, 1)]
SIB_MASKS = [(0, 0, 1)]
ANY_SPEC = pl.BlockSpec(memory_space=pl.ANY)


class _Plan:
    def __init__(self, arrays, masks, n_slots, src_slotted, dst_slotted, local_copy, half=False, forward=False):
        self.shapes = [(a.shape, a.dtype) for a in arrays]
        self.n = len(arrays)
        self.masks = masks
        self.n_slots = n_slots
        self.src_slotted, self.dst_slotted, self.local_copy = src_slotted, dst_slotted, local_copy
        self.half, self.forward = half, forward
        self.n_cp = self.n * len(masks) * (len(CHIP_MASKS) if forward else 1)

    def out_shape(self):
        out = []
        for shp, dt in self.shapes:
            if self.dst_slotted and not self.src_slotted:
                shp = (self.n_slots,) + shp
            elif self.src_slotted and not self.dst_slotted:
                shp = shp[1:]
            out.append(jax.ShapeDtypeStruct(shp, dt))
        return tuple(out)

    def scratch(self):
        return [pltpu.SemaphoreType.DMA((self.n_cp,)), pltpu.SemaphoreType.DMA((self.n_cp,)),
                pltpu.SemaphoreType.DMA((self.n,))]

    def _slot(self, px, py, pc):
        if self.n_slots == 8:
            return 4 * px + 2 * py + pc
        if self.n_slots == 4:
            return 2 * px + py
        return pc

    def copies(self, ins, outs, sems):
        send_sems, recv_sems, loc_sems = sems
        x, y, c = lax.axis_index("x"), lax.axis_index("y"), lax.axis_index("c")
        me = self._slot(x, y, c)
        n_m = len(self.masks)
        cps = []
        for a in range(self.n):
            if self.forward:
                rows = self.shapes[a][0][-2] // 2
                mine = pl.ds(pl.multiple_of(c * rows, 8), rows)
                for j, (mx, my, _) in enumerate(CHIP_MASKS):
                    blk = outs[a].at[2 * (1 - x if mx else x) + (1 - y if my else y), mine]
                    k = a * len(CHIP_MASKS) + j
                    cps.append(pltpu.make_async_remote_copy(
                        src_ref=blk, dst_ref=blk, send_sem=send_sems.at[k], recv_sem=recv_sems.at[k],
                        device_id=(x, y, 1 - c), device_id_type=MESH))
                continue
            if self.local_copy:
                src = ins[a].at[me] if self.src_slotted else ins[a]
                cps.append(pltpu.make_async_copy(src, outs[a].at[me], loc_sems.at[a]))
            for mi, (mx, my, mc) in enumerate(self.masks):
                px = 1 - x if mx else x
                py = 1 - y if my else y
                pc = 1 - c if mc else c
                src = ins[a].at[self._slot(px, py, pc)] if self.src_slotted else ins[a]
                dst = outs[a].at[me] if self.dst_slotted else outs[a]
                if self.half:
                    rows = src.shape[-2] // 2
                    mine = pl.ds(pl.multiple_of(c * rows, 8), rows)
                    src, dst = src.at[mine], dst.at[mine]
                k = a * n_m + mi
                cps.append(pltpu.make_async_remote_copy(
                    src_ref=src, dst_ref=dst, send_sem=send_sems.at[k], recv_sem=recv_sems.at[k],
                    device_id=(px, py, pc), device_id_type=MESH))
        return cps


def _exchange(name, plan, arrays):
    n = plan.n

    def body(*refs):
        cps = plan.copies(refs[:n], refs[n:2 * n], refs[2 * n:])
        for cp in cps:
            cp.start()
        for cp in cps:
            cp.wait()

    outs = pl.pallas_call(
        body, name=name, out_shape=plan.out_shape(),
        in_specs=[ANY_SPEC] * n, out_specs=tuple([ANY_SPEC] * n), scratch_shapes=plan.scratch(),
        input_output_aliases={i: i for i in range(n)} if plan.forward else {},
    )(*arrays)
    return list(outs)


def _pcall(body, *, name, grid, in_specs, out_specs, out_shape, scratch_shapes, args, plan=None, plan_args=()):
    sem = ("arbitrary",) * len(grid)
    if plan is None:
        return pl.pallas_call(body, name=name, grid=grid, in_specs=in_specs, out_specs=out_specs,
                              out_shape=out_shape, scratch_shapes=scratch_shapes,
                              compiler_params=_params(sem))(*args), []
    n_in, n_out, n_scr, n_p = len(in_specs), len(out_specs), len(scratch_shapes), plan.n

    def wrapped(*refs):
        ins = refs[:n_in]
        p_ins = refs[n_in:n_in + n_p]
        o0 = n_in + n_p
        outs = refs[o0:o0 + n_out]
        p_outs = refs[o0 + n_out:o0 + n_out + n_p]
        s0 = o0 + n_out + n_p
        scr = refs[s0:s0 + n_scr]
        sems = refs[s0 + n_scr:]
        ids = [pl.program_id(i) for i in range(len(grid))]
        first = functools.reduce(jnp.logical_and, [i == 0 for i in ids])
        last = functools.reduce(jnp.logical_and, [i == g - 1 for i, g in zip(ids, grid)])

        @pl.when(first)
        def _():
            for cp in plan.copies(p_ins, p_outs, sems):
                cp.start()

        body(*ins, *outs, *scr)

        @pl.when(last)
        def _():
            for cp in plan.copies(p_ins, p_outs, sems):
                cp.wait()

    res = pl.pallas_call(
        wrapped, name=name, grid=grid,
        in_specs=list(in_specs) + [ANY_SPEC] * n_p,
        out_specs=tuple(out_specs) + (ANY_SPEC,) * n_p,
        out_shape=tuple(out_shape) + plan.out_shape(),
        scratch_shapes=list(scratch_shapes) + plan.scratch(),
        compiler_params=_params(sem),
    )(*args, *plan_args)
    return res[:n_out], list(res[n_out:])


def _allgather_chips_plan(arrays):
    return _Plan(arrays, CHIP_MASKS, 4, False, True, True, half=True)


def _gather_two_level(name, arrays):
    n = len(arrays)
    ici = _allgather_chips_plan(arrays)
    fwd = _Plan(ici.out_shape(), SIB_MASKS, 4, True, True, False, forward=True)
    n_m = len(CHIP_MASKS)

    def body(*refs):
        ins, outs, sems = refs[:n], refs[n:2 * n], refs[2 * n:]
        ici_cps = ici.copies(ins, outs, sems[:3])
        fwd_cps = fwd.copies(None, outs, sems[3:])
        for cp in ici_cps:
            cp.start()
        for a in range(n):
            for m in range(n_m):
                ici_cps[a * (n_m + 1) + 1 + m].wait_recv()
                fwd_cps[a * n_m + m].start()
        for a in range(n):
            ici_cps[a * (n_m + 1)].wait()
            for m in range(n_m):
                ici_cps[a * (n_m + 1) + 1 + m].wait_send()
        for cp in fwd_cps:
            cp.wait()

    return list(pl.pallas_call(
        body, name=name, out_shape=ici.out_shape(),
        in_specs=[ANY_SPEC] * n, out_specs=tuple([ANY_SPEC] * n), scratch_shapes=ici.scratch() + fwd.scratch(),
    )(*arrays))


def _forward_sibling(name, gathered):
    return _exchange(name, _Plan(gathered, SIB_MASKS, 4, True, True, False, forward=True), gathered)


def _alltoall_chips_plan(arrays):
    return _Plan(arrays, CHIP_MASKS, 4, True, True, True)


def _swap_sibling(name, arrays):
    return _exchange(name, _Plan(arrays, SIB_MASKS, 2, False, False, False), arrays)


def _allgather_all_plan(arrays):
    return _Plan(arrays, ALL_MASKS, 8, False, True, True)


def _sum_slots(name, arrs, out_dtype):
    s, r = arrs[0].shape[0], arrs[0].shape[-2]
    c = arrs[0].shape[-1] * (2 if arrs[0].ndim == 4 else 1)
    n = len(arrs)
    tr = _tile(r, 512 if n == 1 else 176, 8)

    def body(*refs):
        for a_ref, o_ref in zip(refs[:n], refs[n:]):
            if len(a_ref.shape) == 4:
                for half in range(2):
                    acc = a_ref[0, half].astype(F32)
                    for i in range(1, s):
                        acc = acc + a_ref[i, half].astype(F32)
                    o_ref[:, half * (c // 2):(half + 1) * (c // 2)] = acc.astype(out_dtype)
            else:
                acc = a_ref[0].astype(F32)
                for i in range(1, s):
                    acc = acc + a_ref[i].astype(F32)
                o_ref[...] = acc.astype(out_dtype)

    def in_spec(a):
        if a.ndim == 4:
            return pl.BlockSpec((s, 2, tr, c // 2), lambda i: (0, 0, i, 0))
        return pl.BlockSpec((s, tr, c), lambda i: (0, i, 0))

    return list(pl.pallas_call(
        body, name=name, grid=(r // tr,),
        in_specs=[in_spec(a) for a in arrs],
        out_specs=(pl.BlockSpec((tr, c), lambda i: (i, 0)),) * n,
        out_shape=(jax.ShapeDtypeStruct((r, c), out_dtype),) * n,
        compiler_params=_params(("arbitrary",)),
    )(*arrs))


def _adam_math(w, g, m, v):
    m_new = ADAM_B1 * m + (1.0 - ADAM_B1) * g
    v_new = ADAM_B2 * v + (1.0 - ADAM_B2) * (g * g)
    m_hat = m_new / (1.0 - ADAM_B1 ** ADAM_STEP)
    v_hat = v_new / (1.0 - ADAM_B2 ** ADAM_STEP)
    delta = -ADAM_LR * (m_hat / (jnp.sqrt(v_hat) + ADAM_EPS) + ADAM_WD * w)
    return delta, m_new, v_new


def _adam(name, wmv, g_parts):
    w0 = wmv[0][0]
    r, c = w0.shape[-2:]
    n_w = len(wmv)
    n_g = len(g_parts[0])
    tr = _tile(r, 256 if n_w == 1 else 88, 8)
    lead = w0.ndim == 3
    at = (lambda ref: ref.at[0]) if lead else (lambda ref: ref)
    n_in = 3 + n_g

    def body(*refs):
        for j in range(n_w):
            ins = refs[j * n_in:(j + 1) * n_in]
            outs = refs[n_w * n_in + 4 * j:n_w * n_in + 4 * j + 4]
            w_ref, m_ref, v_ref = [at(t) for t in ins[:3]]
            g_out, d_out, m_out, v_out = [at(t) for t in outs]
            g = ins[3][...].astype(F32)
            for gr in ins[4:]:
                g = g + gr[...].astype(F32)
            delta, m_new, v_new = _adam_math(w_ref[...], g, m_ref[...], v_ref[...])
            g_out[...] = g
            d_out[...] = delta
            m_out[...] = m_new
            v_out[...] = v_new

    spec = pl.BlockSpec((tr, c), lambda i: (i, 0))
    wspec = pl.BlockSpec((1, tr, c), lambda i: (0, i, 0)) if lead else spec
    shp = jax.ShapeDtypeStruct(w0.shape, F32)
    args = [t for (w, m, v), gp in zip(wmv, g_parts) for t in (w, m, v, *gp)]
    res = pl.pallas_call(
        body, name=name, grid=(r // tr,),
        in_specs=([wspec] * 3 + [spec] * n_g) * n_w, out_specs=(wspec,) * (4 * n_w), out_shape=(shp,) * (4 * n_w),
        compiler_params=_params(("arbitrary",)),
    )(*args)
    return [tuple(res[4 * j:4 * j + 4]) for j in range(n_w)]


SUB_ROWS = 32
FFN_BWD_ROWS = 416
FFN_FWD_ROWS = 832
RET_ROWS = 640
S5_STEPS = 104


def _tile_parts(tm, d, head, x):
    nsub = tm // SUB_ROWS
    off = head.shape[0] // SUB_ROWS
    specs = [pl.BlockSpec(head.shape, lambda i, k: (0, 0))] + [
        pl.BlockSpec((SUB_ROWS, d), lambda i, k, j=j: (jnp.maximum(i * nsub + j - off, 0), 0)) for j in range(nsub)]

    def assemble(i, part_refs, h_sc):
        head_ref, x_refs = part_refs[0], part_refs[1:]
        for j in range(nsub):
            rows = slice(j * SUB_ROWS, (j + 1) * SUB_ROWS)
            val = x_refs[j][...]
            if j < off:
                val = jnp.where(i == 0, head_ref[rows, :], val)
            h_sc[rows, :] = val

    return specs, [head] + [x] * nsub, assemble


def _h_source(body, h, tm, d):
    if not isinstance(h, tuple):
        return body, [pl.BlockSpec((tm, d), lambda i, k: (i, 0))], [h], []
    specs, args, assemble = _tile_parts(tm, d, *h)
    n_h = len(specs)

    def with_parts(*refs):
        h_sc = refs[-1]

        @pl.when(pl.program_id(1) == 0)
        def _():
            assemble(pl.program_id(0), refs[:n_h], h_sc)

        body(h_sc, *refs[n_h:-1])

    return with_parts, specs, args, [pltpu.VMEM((tm, d), F32)]


def _ffn_fwd(name, h, nw, wg, wu, wd, plan=None, plan_args=()):
    lp, d = (h[0].shape[0] + h[1].shape[0], h[1].shape[1]) if isinstance(h, tuple) else h.shape
    nck, f, _ = wg.shape
    tm = _tile(lp, FFN_FWD_ROWS)
    last = nck // FFN_CPS - 1

    def body(h_ref, nw_ref, wg_ref, wu_ref, wd_ref, ho_ref, g_ref, u_ref, n_sc, acc_sc):
        k = pl.program_id(1)

        @pl.when(k == 0)
        def _():
            xh, _ = _rms_stats(h_ref[...])
            n_sc[...] = (xh * nw_ref[...]).astype(BF16)
            acc_sc[...] = jnp.zeros_like(acc_sc)

        n = n_sc[...]
        acc = acc_sc[...]
        for c in range(FFN_CPS):
            g = _dot_nt(n, wg_ref[c])
            u = _dot_nt(n, wu_ref[c])
            g_ref[c] = g.astype(BF16)
            u_ref[c] = u.astype(BF16)
            a = (g * _sigmoid(g) * u).astype(BF16)
            acc = acc + _dot(a, wd_ref[c])
        acc_sc[...] = acc

        @pl.when(k == last)
        def _():
            ho_ref[...] = h_ref[...] + FFN_RES * acc_sc[...]

    body, h_specs, h_args, h_scratch = _h_source(body, h, tm, d)
    w_fd = pl.BlockSpec((FFN_CPS, f, d), lambda i, k: (k, 0, 0))
    hid = pl.BlockSpec((FFN_CPS, tm, f), lambda i, k: (k, i, 0))
    return _pcall(
        body, name=name, grid=(lp // tm, nck // FFN_CPS), plan=plan, plan_args=plan_args,
        args=(*h_args, nw, wg, wu, wd),
        in_specs=h_specs + [pl.BlockSpec((1, d), lambda i, k: (0, 0)), w_fd, w_fd, w_fd],
        out_specs=(pl.BlockSpec((tm, d), lambda i, k: (i, 0)), hid, hid),
        out_shape=(jax.ShapeDtypeStruct((lp, d), F32),
                   jax.ShapeDtypeStruct((nck, lp, f), BF16),
                   jax.ShapeDtypeStruct((nck, lp, f), BF16)),
        scratch_shapes=[pltpu.VMEM((tm, d), BF16), pltpu.VMEM((tm, d), F32)] + h_scratch)


def _ffn_bwd_act(name, dh, h, nw, g, u, wg, wu, wd, plan=None, plan_args=()):
    lp, d = dh.shape
    nck, f, _ = wg.shape
    tm = _tile(lp, FFN_BWD_ROWS, SUB_ROWS)
    last = nck // FFN_CPS - 1

    def body(h_ref, dh_ref, nw_ref, g_ref, u_ref, wg_ref, wu_ref, wd_ref,
             dhi_ref, dnw_ref, n_ref, dacc_ref, a_ref, dg_ref, du_ref,
             xh_sc, r_sc, dn_sc):
        i = pl.program_id(0)
        k = pl.program_id(1)

        @pl.when(k == 0)
        def _():
            xh, r = _rms_stats(h_ref[...])
            xh_sc[...] = xh
            r_sc[...] = r
            n_ref[...] = (xh * nw_ref[...]).astype(BF16)
            dacc_ref[...] = (FFN_RES * dh_ref[...]).astype(BF16)
            dn_sc[...] = jnp.zeros_like(dn_sc)

        @pl.when(jnp.logical_and(i == 0, k == 0))
        def _():
            dnw_ref[...] = jnp.zeros_like(dnw_ref)

        dacc = dacc_ref[...]
        dn = dn_sc[...]
        for c in range(FFN_CPS):
            gv = g_ref[c].astype(F32)
            uv = u_ref[c].astype(F32)
            sg = _sigmoid(gv)
            sil = gv * sg
            da = _dot_nt(dacc, wd_ref[c])
            dgk = (da * uv * (sg * (1.0 + gv * (1.0 - sg)))).astype(BF16)
            duk = (da * sil).astype(BF16)
            a_ref[c] = (sil * uv).astype(BF16)
            dg_ref[c] = dgk
            du_ref[c] = duk
            dn = dn + _dot(dgk, wg_ref[c]) + _dot(duk, wu_ref[c])
        dn_sc[...] = dn

        @pl.when(k == last)
        def _():
            dnl = dn_sc[...]
            xh = xh_sc[...]
            dhi_ref[...] = dh_ref[...] + _rms_bwd(dnl, xh, r_sc[...], nw_ref[...])
            dnw_ref[...] += jnp.sum(dnl * xh, axis=0, keepdims=True)

    body, h_specs, h_args, h_scratch = _h_source(body, h, tm, d)
    row = pl.BlockSpec((tm, d), lambda i, k: (i, 0))
    vec = pl.BlockSpec((1, d), lambda i, k: (0, 0))
    hid = pl.BlockSpec((FFN_CPS, tm, f), lambda i, k: (k, i, 0))
    w_fd = pl.BlockSpec((FFN_CPS, f, d), lambda i, k: (k, 0, 0))
    rshape = jax.ShapeDtypeStruct((lp, d), BF16)
    hshape = jax.ShapeDtypeStruct((nck, lp, f), BF16)
    return _pcall(
        body, name=name, grid=(lp // tm, nck // FFN_CPS), plan=plan, plan_args=plan_args,
        args=(*h_args, dh, nw, g, u, wg, wu, wd),
        in_specs=h_specs + [row, vec, hid, hid, w_fd, w_fd, w_fd],
        out_specs=(row, vec, row, row, hid, hid, hid),
        out_shape=(jax.ShapeDtypeStruct((lp, d), F32), jax.ShapeDtypeStruct((1, d), F32),
                   rshape, rshape, hshape, hshape, hshape),
        scratch_shapes=[pltpu.VMEM((tm, d), F32), pltpu.VMEM((tm, 1), F32), pltpu.VMEM((tm, d), F32)] + h_scratch)


def _ffn_bwd_w(name, n, dacc, a, dg, du, plan=None, plan_args=()):
    lp, d = n.shape
    nck, _, f = a.shape
    tm = _tile(lp, BWD_W_ROWS)
    last = lp // tm - 1

    def body(n_ref, dacc_ref, a_ref, dg_ref, du_ref, dwg_ref, dwu_ref, dwd_ref, ag_sc, au_sc, ad_sc):
        i = pl.program_id(1)

        @pl.when(i == 0)
        def _():
            ag_sc[...] = jnp.zeros_like(ag_sc)
            au_sc[...] = jnp.zeros_like(au_sc)
            ad_sc[...] = jnp.zeros_like(ad_sc)

        nv = n_ref[...]
        ag_sc[...] += _dot_tn(dg_ref[0], nv)
        au_sc[...] += _dot_tn(du_ref[0], nv)
        ad_sc[...] += _dot_tn(a_ref[0], dacc_ref[...])

        @pl.when(i == last)
        def _():
            dwg_ref[0] = ag_sc[...].astype(BF16)
            dwu_ref[0] = au_sc[...].astype(BF16)
            dwd_ref[0] = ad_sc[...].astype(BF16)

    row = pl.BlockSpec((tm, d), lambda k, i: (i, 0))
    hid = pl.BlockSpec((1, tm, f), lambda k, i: (k, i, 0))
    w_fd = pl.BlockSpec((1, f, d), lambda k, i: (k, 0, 0))
    wshape = jax.ShapeDtypeStruct((nck, f, d), BF16)
    return _pcall(
        body, name=name, grid=(nck, lp // tm), plan=plan, plan_args=plan_args, args=(n, dacc, a, dg, du),
        in_specs=[row, row, hid, hid, hid], out_specs=(w_fd, w_fd, w_fd), out_shape=(wshape,) * 3,
        scratch_shapes=[pltpu.VMEM((f, d), F32)] * 3)


def _ffn_bwd_w_scatter(name, n, dacc, a, dg, du, chip, plan, plan_args):
    lp, d = n.shape
    nck, _, f = a.shape
    tm = _tile(lp, BWD_W_ROWS)
    last_i = lp // tm - 1
    n_w = 3
    n_p = plan.n

    def body(me_ref, n_ref, dacc_ref, a_ref, dg_ref, du_ref, *rest):
        p_ins = rest[:n_p]
        recv = rest[n_p:n_p + n_w]
        p_outs = rest[n_p + n_w:2 * n_p + n_w]
        acc = rest[2 * n_p + n_w:2 * n_p + 2 * n_w]
        stage, send_sems, recv_sems, loc_sems = rest[2 * n_p + 2 * n_w:2 * n_p + 2 * n_w + 4]
        p_sems = rest[2 * n_p + 2 * n_w + 4:]
        p = pl.program_id(0)
        i = pl.program_id(1)
        me = me_ref[0]
        c = lax.axis_index("c")

        def send(w, pos):
            kk = jnp.bitwise_xor(me, nck - 1 - pos)
            diff = jnp.bitwise_xor(kk, me)
            m = jnp.where(diff == 2, 0, jnp.where(diff == 1, 1, 2))
            return pltpu.make_async_remote_copy(
                src_ref=stage.at[lax.rem(pos, 2), w], dst_ref=recv[w].at[me],
                send_sem=send_sems.at[w * 3 + m], recv_sem=recv_sems.at[w * 3 + m],
                device_id=(lax.div(kk, 2), lax.rem(kk, 2), c), device_id_type=MESH)

        @pl.when(jnp.logical_and(p == 0, i == 0))
        def _():
            for cp in plan.copies(p_ins, p_outs, p_sems):
                cp.start()

        @pl.when(i == 0)
        def _():
            for t in acc:
                t[...] = jnp.zeros_like(t)

        nv = n_ref[...]
        acc[0][...] += _dot_tn(dg_ref[0], nv)
        acc[1][...] += _dot_tn(du_ref[0], nv)
        acc[2][...] += _dot_tn(a_ref[0], dacc_ref[...])

        @pl.when(jnp.logical_and(i == last_i, p >= 2))
        def _():
            for w in range(n_w):
                send(w, p - 2).wait_send()

        @pl.when(i == last_i)
        def _():
            for w in range(n_w):
                stage[lax.rem(p, 2), w] = acc[w][...].astype(BF16)

        @pl.when(jnp.logical_and(i == last_i, p < nck - 1))
        def _():
            for w in range(n_w):
                send(w, p).start()

        @pl.when(jnp.logical_and(i == last_i, p == nck - 1))
        def _():
            own = [pltpu.make_async_copy(stage.at[(nck - 1) % 2, w], recv[w].at[me], loc_sems.at[w])
                   for w in range(n_w)]
            for cp in own:
                cp.start()
            for w in range(n_w):
                send(w, nck - 2).wait_send()
            for cp in own:
                cp.wait()
            for w in range(n_w):
                for m in range(3):
                    pltpu.make_async_remote_copy(
                        src_ref=stage.at[0, w], dst_ref=recv[w].at[me],
                        send_sem=send_sems.at[w * 3 + m], recv_sem=recv_sems.at[w * 3 + m],
                        device_id=(0, 0, c), device_id_type=MESH).wait_recv()
            for cp in plan.copies(p_ins, p_outs, p_sems):
                cp.wait()

    chunk = lambda k, me_ref: jnp.bitwise_xor(me_ref[0], nck - 1 - k)
    row = pl.BlockSpec((tm, d), lambda k, i, me_ref: (i, 0))
    hid = pl.BlockSpec((1, tm, f), lambda k, i, me_ref: (chunk(k, me_ref), i, 0))
    wshape = jax.ShapeDtypeStruct((nck, f, d), BF16)
    res = pl.pallas_call(
        body, name=name,
        grid_spec=pltpu.PrefetchScalarGridSpec(
            num_scalar_prefetch=1, grid=(nck, lp // tm),
            in_specs=[row, row, hid, hid, hid] + [ANY_SPEC] * n_p,
            out_specs=(ANY_SPEC,) * (n_w + n_p),
            scratch_shapes=[pltpu.VMEM((f, d), F32)] * n_w + [
                pltpu.VMEM((2, n_w, f, d), BF16), pltpu.SemaphoreType.DMA((n_w * 3,)),
                pltpu.SemaphoreType.DMA((n_w * 3,)), pltpu.SemaphoreType.DMA((n_w,))] + plan.scratch()),
        out_shape=(wshape,) * n_w + plan.out_shape(),
        compiler_params=_params(("arbitrary", "arbitrary")),
    )(chip.reshape(1).astype(jnp.int32), n, dacc, a, dg, du, *plan_args)
    return list(res[:n_w]), list(res[n_w:])


def _inproj_fwd(h, nw, w_in, cosf, sinf, rw):
    lp, d = h.shape
    nck, _, ps = w_in.shape
    proj = nck * ps
    sw = proj - 4 * rw
    tm = _tile(lp, 640)
    scale = HEAD_DIM ** -0.5
    heads = rw // HEAD_DIM

    def body(h_ref, nw_ref, w_ref, cos_ref, sin_ref, n_ref, q_ref, k_ref, v_ref, g_ref, u_ref, p_sc):
        xh, _ = _rms_stats(h_ref[...])
        n = (xh * nw_ref[...]).astype(BF16)
        n_ref[...] = n
        for c in range(nck):
            p_sc[:, c * ps:(c + 1) * ps] = _dot(n, w_ref[c])
        cs = cos_ref[...]
        sn = sin_ref[...]
        for hh in range(heads):
            lo = hh * HEAD_DIM
            qh = p_sc[:, lo:lo + HEAD_DIM]
            q_ref[:, lo:lo + HEAD_DIM] = (qh * cs + pltpu.roll(qh, HEAD_DIM // 2, 1) * sn).astype(BF16)
            kh = p_sc[:, rw + lo:rw + lo + HEAD_DIM]
            k_ref[:, lo:lo + HEAD_DIM] = ((kh * cs + pltpu.roll(kh, HEAD_DIM // 2, 1) * sn) * scale).astype(BF16)
        v_ref[...] = p_sc[:, 2 * rw:3 * rw].astype(BF16)
        g_ref[...] = p_sc[:, 3 * rw:4 * rw]
        u_ref[...] = p_sc[:, 4 * rw:]

    row = lambda w: pl.BlockSpec((tm, w), lambda i: (i, 0))
    return pl.pallas_call(
        body, name="inproj_fwd", grid=(lp // tm,),
        in_specs=[row(d), pl.BlockSpec((1, d), lambda i: (0, 0)),
                  pl.BlockSpec((nck, d, ps), lambda i: (0, 0, 0)), row(HEAD_DIM), row(HEAD_DIM)],
        out_specs=(row(d), row(rw), row(rw), row(rw), row(rw), row(sw)),
        out_shape=(jax.ShapeDtypeStruct((lp, d), BF16),
                   jax.ShapeDtypeStruct((lp, rw), BF16),
                   jax.ShapeDtypeStruct((lp, rw), BF16),
                   jax.ShapeDtypeStruct((lp, rw), BF16),
                   jax.ShapeDtypeStruct((lp, rw), F32),
                   jax.ShapeDtypeStruct((lp, sw), F32)),
        scratch_shapes=[pltpu.VMEM((tm, proj), F32)],
        compiler_params=_params(("arbitrary",)),
    )(h, nw, w_in, cosf, sinf)


def _inproj_bwd(dh, h, nw, n, w_in, dq, dk, dv, dg, du):
    lp, d = h.shape
    nck, _, ps = w_in.shape
    rw = dq.shape[1]
    sw = du.shape[1]
    proj = nck * ps
    tm = _tile(lp, 640)
    last = lp // tm - 1

    def gather_dproj(p_sc, dq_ref, dk_ref, dv_ref, dg_ref, du_ref):
        p_sc[:, 0:rw] = dq_ref[...]
        p_sc[:, rw:2 * rw] = dk_ref[...]
        p_sc[:, 2 * rw:3 * rw] = dv_ref[...]
        p_sc[:, 3 * rw:4 * rw] = dg_ref[...]
        p_sc[:, 4 * rw:] = du_ref[...]

    def act_body(dh_ref, h_ref, nw_ref, w_ref, dq_ref, dk_ref, dv_ref, dg_ref, du_ref, dhi_ref, dnw_ref, p_sc):
        i = pl.program_id(0)

        @pl.when(i == 0)
        def _():
            dnw_ref[...] = jnp.zeros_like(dnw_ref)

        gather_dproj(p_sc, dq_ref, dk_ref, dv_ref, dg_ref, du_ref)
        dn = jnp.zeros((tm, d), F32)
        for c in range(nck):
            dn = dn + _dot_nt(p_sc[:, c * ps:(c + 1) * ps], w_ref[c])
        xh, r = _rms_stats(h_ref[...])
        dhi_ref[...] = dh_ref[...] + _rms_bwd(dn, xh, r, nw_ref[...])
        dnw_ref[...] += jnp.sum(dn * xh, axis=0, keepdims=True)

    def w_body(n_ref, dq_ref, dk_ref, dv_ref, dg_ref, du_ref, dw_ref, p_sc, acc_sc):
        i = pl.program_id(0)

        @pl.when(i == 0)
        def _():
            acc_sc[...] = jnp.zeros_like(acc_sc)

        gather_dproj(p_sc, dq_ref, dk_ref, dv_ref, dg_ref, du_ref)
        nv = n_ref[...]
        for c in range(nck):
            acc_sc[c] += _dot_tn(nv, p_sc[:, c * ps:(c + 1) * ps])

        @pl.when(i == last)
        def _():
            dw_ref[...] = acc_sc[...].astype(BF16)

    row = lambda w: pl.BlockSpec((tm, w), lambda i: (i, 0))
    vec = pl.BlockSpec((1, d), lambda i: (0, 0))
    wsp = pl.BlockSpec((nck, d, ps), lambda i: (0, 0, 0))
    dproj_specs = [row(rw), row(rw), row(rw), row(rw), row(sw)]
    dhi, dnw = pl.pallas_call(
        act_body, name="inproj_bwd_act", grid=(lp // tm,),
        in_specs=[row(d), row(d), vec, wsp] + dproj_specs,
        out_specs=(row(d), vec),
        out_shape=(jax.ShapeDtypeStruct((lp, d), F32), jax.ShapeDtypeStruct((1, d), F32)),
        scratch_shapes=[pltpu.VMEM((tm, proj), BF16)],
        compiler_params=_params(("arbitrary",)),
    )(dh, h, nw, w_in, dq, dk, dv, dg, du)
    dw = pl.pallas_call(
        w_body, name="inproj_bwd_w", grid=(lp // tm,),
        in_specs=[row(d)] + dproj_specs,
        out_specs=wsp, out_shape=jax.ShapeDtypeStruct((nck, d, ps), BF16),
        scratch_shapes=[pltpu.VMEM((tm, proj), BF16), pltpu.VMEM((nck, d, ps), F32)],
        compiler_params=_params(("arbitrary",)),
    )(n, dq, dk, dv, dg, du)
    return dhi, dnw, dw


def _retention_tables(rc):
    h = jnp.arange(RET_HEADS, dtype=F32)
    log_g = jnp.log(1.0 - 2.0 ** (-5.0 - h))
    i = jnp.arange(rc)
    diff = i[:, None] - i[None, :]
    dec = jnp.where(diff[None] >= 0,
                    jnp.exp(log_g[:, None, None] * jnp.maximum(diff, 0)[None].astype(F32)), 0.0)
    pos = jnp.arange(rc, dtype=F32)
    wq = jnp.exp(log_g[:, None] * (pos + 1.0)[None])
    wk = jnp.exp(log_g[:, None] * (rc - 1 - pos)[None])
    gch = jnp.exp(log_g * rc)
    ones = jnp.ones((1, 1, HEAD_DIM), F32)
    return (dec, wq[:, :, None] * ones, wk[:, :, None] * ones,
            gch[:, None, None] * jnp.ones((1, 8, HEAD_DIM), F32))


def _head_norm(o):
    mu = jnp.mean(o, axis=-1, keepdims=True)
    oc = o - mu
    r = lax.rsqrt(jnp.mean(oc * oc, axis=-1, keepdims=True) + EPS)
    return oc * r, r


def _ret_fwd(q, k, v, g, rnw, tables):
    lp, rw = q.shape
    heads = rw // HEAD_DIM
    rc = tables[0].shape[1]
    nch = lp // rc
    dec, wq, wk, gch = tables

    def body(q_ref, k_ref, v_ref, g_ref, w_ref, dec_ref, wq_ref, wk_ref, gch_ref,
             o_ref, ret_ref, sp_ref, s_sc):
        n = pl.program_id(0)

        @pl.when(n == 0)
        def _():
            s_sc[...] = jnp.zeros_like(s_sc)

        cols = [slice(hh * HEAD_DIM, (hh + 1) * HEAD_DIM) for hh in range(heads)]
        s_ins = [s_sc[hh] for hh in range(heads)]
        outs = []
        for hh, cs in enumerate(cols):
            qv, kv, vv = q_ref[:, cs], k_ref[:, cs], v_ref[:, cs]
            s_in = s_ins[hh]
            a = _dot_nt(qv, kv) * dec_ref[hh]
            qw = (qv.astype(F32) * wq_ref[hh]).astype(BF16)
            kw = (kv.astype(F32) * wk_ref[hh]).astype(BF16)
            o = _dot(a.astype(BF16), vv) + _dot(qw, s_in.astype(BF16))
            s_new = gch_ref[hh, 0:1, :] * s_in + _dot_tn(kw, vv)
            xh, _ = _head_norm(o)
            gv = g_ref[:, cs]
            outs.append((o, s_new, (gv * _sigmoid(gv) * (xh * w_ref[:, cs])).astype(BF16)))
        for hh, cs in enumerate(cols):
            o, s_new, ret = outs[hh]
            sp_ref[hh, 0] = s_ins[hh]
            s_sc[hh] = s_new
            o_ref[:, cs] = o
            ret_ref[:, cs] = ret

    blk = pl.BlockSpec((rc, rw), lambda n: (n, 0))
    tab = pl.BlockSpec((heads, rc, HEAD_DIM), lambda n: (0, 0, 0))
    dtab = pl.BlockSpec((heads, rc, rc), lambda n: (0, 0, 0))
    return pl.pallas_call(
        body, name="retention_fwd", grid=(nch,),
        in_specs=[blk, blk, blk, blk, pl.BlockSpec((1, rw), lambda n: (0, 0)),
                  dtab, tab, tab, pl.BlockSpec((heads, 8, HEAD_DIM), lambda n: (0, 0, 0))],
        out_specs=(blk, blk, pl.BlockSpec((heads, 1, HEAD_DIM, HEAD_DIM), lambda n: (0, n, 0, 0))),
        out_shape=(jax.ShapeDtypeStruct((lp, rw), F32),
                   jax.ShapeDtypeStruct((lp, rw), BF16),
                   jax.ShapeDtypeStruct((heads, nch, HEAD_DIM, HEAD_DIM), F32)),
        scratch_shapes=[pltpu.VMEM((heads, HEAD_DIM, HEAD_DIM), F32)],
        compiler_params=_params(("arbitrary",)),
    )(q, k, v, g, rnw, dec, wq, wk, gch)


def _ret_bwd(dret, q, k, v, g, o, sprev, rnw, tables, cosf, sinf):
    lp, rw = q.shape
    heads = rw // HEAD_DIM
    rc = tables[0].shape[1]
    nch = lp // rc
    dec, wq, wk, gch = tables
    scale = HEAD_DIM ** -0.5
    half = HEAD_DIM // 2

    def body(dret_ref, q_ref, k_ref, v_ref, g_ref, o_ref, sp_ref, w_ref, dec_ref, wq_ref, wk_ref, gch_ref,
             cos_ref, sin_ref, dq_ref, dk_ref, dv_ref, dg_ref, dw_ref, ds_sc):
        n = pl.program_id(0)

        @pl.when(n == 0)
        def _():
            ds_sc[...] = jnp.zeros_like(ds_sc)
            dw_ref[...] = jnp.zeros_like(dw_ref)

        cosv = cos_ref[...]
        sinv = sin_ref[...]
        cols = [slice(hh * HEAD_DIM, (hh + 1) * HEAD_DIM) for hh in range(heads)]
        ds_ins = [ds_sc[hh] for hh in range(heads)]
        dw_ins = [dw_ref[:, cs] for cs in cols]
        outs = []
        for hh, cs in enumerate(cols):
            qv, kv, vv = q_ref[:, cs], k_ref[:, cs], v_ref[:, cs]
            gv = g_ref[:, cs]
            dr = dret_ref[:, cs]
            w = w_ref[:, cs]
            sg = _sigmoid(gv)
            sil = gv * sg
            xh, r = _head_norm(o_ref[:, cs])
            dgate = (dr * (xh * w) * (sg * (1.0 + gv * (1.0 - sg)))).astype(BF16)
            dyw = dr * sil
            dw_new = dw_ins[hh] + jnp.sum(dyw * xh, axis=0, keepdims=True)
            dxh = dyw * w
            do = r * (dxh - jnp.mean(dxh, axis=-1, keepdims=True)
                      - xh * jnp.mean(dxh * xh, axis=-1, keepdims=True))
            dob = do.astype(BF16)
            dmask = dec_ref[hh]
            wqv = wq_ref[hh]
            wkv = wk_ref[hh]
            a = (_dot_nt(qv, kv) * dmask).astype(BF16)
            da = (_dot_nt(dob, vv) * dmask).astype(BF16)
            qw = (qv.astype(F32) * wqv).astype(BF16)
            kw = (kv.astype(F32) * wkv).astype(BF16)
            s_in = sp_ref[hh, 0].astype(BF16)
            ds = ds_ins[hh]
            dsb = ds.astype(BF16)
            dq = _dot(da, kv) + _dot_nt(dob, s_in) * wqv
            dk = _dot_tn(da, qv) + _dot_nt(vv, dsb) * wkv
            dv = _dot_tn(a, dob) + _dot(kw, dsb)
            ds_new = gch_ref[hh, 0:1, :] * ds + _dot_tn(qw, dob)
            outs.append((dgate, dw_new, ds_new,
                         (dq * cosv + pltpu.roll(dq * sinv, half, 1)).astype(BF16),
                         ((dk * cosv + pltpu.roll(dk * sinv, half, 1)) * scale).astype(BF16),
                         dv.astype(BF16)))
        for hh, cs in enumerate(cols):
            dgate, dw_new, ds_new, dqv, dkv, dvv = outs[hh]
            dg_ref[:, cs] = dgate
            dw_ref[:, cs] = dw_new
            ds_sc[hh] = ds_new
            dq_ref[:, cs] = dqv
            dk_ref[:, cs] = dkv
            dv_ref[:, cs] = dvv

    blk = pl.BlockSpec((rc, rw), lambda n: (nch - 1 - n, 0))
    tab = pl.BlockSpec((heads, rc, HEAD_DIM), lambda n: (0, 0, 0))
    dtab = pl.BlockSpec((heads, rc, rc), lambda n: (0, 0, 0))
    wsp = pl.BlockSpec((1, rw), lambda n: (0, 0))
    pos = pl.BlockSpec((rc, HEAD_DIM), lambda n: (nch - 1 - n, 0))
    bshape = jax.ShapeDtypeStruct((lp, rw), BF16)
    return pl.pallas_call(
        body, name="retention_bwd", grid=(nch,),
        in_specs=[blk, blk, blk, blk, blk, blk,
                  pl.BlockSpec((heads, 1, HEAD_DIM, HEAD_DIM), lambda n: (0, nch - 1 - n, 0, 0)),
                  wsp, dtab, tab, tab, pl.BlockSpec((heads, 8, HEAD_DIM), lambda n: (0, 0, 0)), pos, pos],
        out_specs=(blk, blk, blk, blk, wsp),
        out_shape=(bshape, bshape, bshape, bshape, jax.ShapeDtypeStruct((1, rw), F32)),
        scratch_shapes=[pltpu.VMEM((heads, HEAD_DIM, HEAD_DIM), F32)],
        compiler_params=_params(("arbitrary",)),
    )(dret, q, k, v, g, o, sprev, rnw, dec, wq, wk, gch, cosf, sinf)


SCAN_CW = 512


def _s5_prepare(lam_re, lam_im, log_dt, b_re, b_im):
    dt = jnp.exp(log_dt)[:, None]
    er = jnp.exp(lam_re * dt)
    ar = er * jnp.cos(lam_im * dt)
    ai = er * jnp.sin(lam_im * dt)
    den = lam_re * lam_re + lam_im * lam_im
    fr = ((ar - 1.0) * lam_re + ai * lam_im) / den
    fi = (ai * lam_re - (ar - 1.0) * lam_im) / den
    bbr = fr[..., None] * b_re - fi[..., None] * b_im
    bbi = fr[..., None] * b_im + fi[..., None] * b_re
    return ar, ai, bbr, bbi


def _blockdiag_in(t):
    g, p, n = t.shape
    gs = g // N_SEC
    t = t.reshape(N_SEC, gs, p, n)
    eye = jnp.eye(gs, dtype=t.dtype)
    return jnp.einsum("sgpn,gh->sgphn", t, eye).reshape(N_SEC, gs * p, gs * n)


def _blockdiag_out(m, g, p, n):
    gs = g // N_SEC
    m = m.reshape(N_SEC, gs, p, gs, n)
    eye = jnp.eye(gs, dtype=m.dtype)
    return jnp.einsum("sgphn,gh->sgpn", m, eye).reshape(g, p, n)


def _scan_step(xr_ref, xi_ref, r0, prev, ar_ref, ai_ref, conj, ncols):
    new = []
    for cc in range(ncols // SCAN_CW):
        cs = pl.ds(cc * SCAN_CW, SCAN_CW)
        pr, pi = prev[cc]
        ar = ar_ref[:, cs]
        ai = ai_ref[:, cs]
        if conj:
            nr = ar * pr + ai * pi
            ni = ar * pi - ai * pr
        else:
            nr = ar * pr - ai * pi
            ni = ar * pi + ai * pr
        xr = xr_ref[pl.ds(r0, 8), cs] + nr
        xi = xi_ref[pl.ds(r0, 8), cs] + ni
        xr_ref[pl.ds(r0, 8), cs] = xr
        xi_ref[pl.ds(r0, 8), cs] = xi
        new.append((xr, xi))
    return new


def _scan_chunks(ncols):
    return [pl.ds(cc * SCAN_CW, SCAN_CW) for cc in range(ncols // SCAN_CW)]


def _flat(pairs):
    return tuple(t for p in pairs for t in p)


def _pairs(flat):
    return [(flat[2 * k], flat[2 * k + 1]) for k in range(len(flat) // 2)]


def _shift_rows(z, down):
    row = lax.broadcasted_iota(jnp.int32, z.shape, 0)
    if down:
        return jnp.where(row == 0, 0.0, pltpu.roll(z, 1, 0))
    return jnp.where(row == N_SEG - 1, 0.0, pltpu.roll(z, N_SEG - 1, 0))


def _s5_fwd(u, bsr, bsi, csr, csi, a8r, a8i, al8r, al8i, d, gluw, glub, nw, jb):
    lp, sw = u.shape
    ns = a8r.shape[1]
    rows = N_SEG * jb
    nblk = lp // rows
    secw = sw // N_SEC
    secn = ns // N_SEC

    def local_scan(u_ref, bsr_ref, bsi_ref, ar_ref, ai_ref, xr_ref, xi_ref, pr_sc, pi_sc):
        for s in range(N_SEC):
            ub = u_ref[:, s * secw:(s + 1) * secw].astype(BF16)
            xr_ref[:, s * secn:(s + 1) * secn] = _dot(ub, bsr_ref[s])
            xi_ref[:, s * secn:(s + 1) * secn] = _dot(ub, bsi_ref[s])
        prev = [(pr_sc[:, cs], pi_sc[:, cs]) for cs in _scan_chunks(ns)]
        prev = _scan_step(xr_ref, xi_ref, 0, prev, ar_ref, ai_ref, False, ns)

        def step(j, carry):
            r0 = pl.multiple_of(j * 8, 8)
            return _flat(_scan_step(xr_ref, xi_ref, r0, _pairs(carry), ar_ref, ai_ref, False, ns))

        last = _pairs(lax.fori_loop(1, jb, step, _flat(prev)))
        for cs, (vr, vi) in zip(_scan_chunks(ns), last):
            pr_sc[:, cs] = vr
            pi_sc[:, cs] = vi

    def carry_body(u_ref, bsr_ref, bsi_ref, ar_ref, ai_ref, alr_ref, ali_ref, cr_ref, ci_ref,
                   xr_sc, xi_sc, pr_sc, pi_sc):
        b = pl.program_id(0)

        @pl.when(b == 0)
        def _():
            pr_sc[...] = jnp.zeros_like(pr_sc)
            pi_sc[...] = jnp.zeros_like(pi_sc)

        local_scan(u_ref, bsr_ref, bsi_ref, ar_ref, ai_ref, xr_sc, xi_sc, pr_sc, pi_sc)

        @pl.when(b == nblk - 1)
        def _():
            er = _shift_rows(pr_sc[...], True)
            ei = _shift_rows(pi_sc[...], True)
            alr, ali = alr_ref[...], ali_ref[...]
            cr, ci = er, ei
            for _ in range(N_SEG - 2):
                sr = _shift_rows(cr, True)
                si = _shift_rows(ci, True)
                cr = er + alr * sr - ali * si
                ci = ei + alr * si + ali * sr
            cr_ref[...] = cr
            ci_ref[...] = ci

    ublk = pl.BlockSpec((rows, sw), lambda b: (b, 0))
    bspec = pl.BlockSpec((N_SEC, secw, secn), lambda b: (0, 0, 0))
    cspec = pl.BlockSpec((N_SEC, secn, secw), lambda b: (0, 0, 0))
    s8 = pl.BlockSpec((N_SEG, ns), lambda b: (0, 0))
    vec = pl.BlockSpec((1, sw), lambda b: (0, 0))
    s8shape = jax.ShapeDtypeStruct((N_SEG, ns), F32)
    c0r, c0i = pl.pallas_call(
        carry_body, name="s5_fwd_carry", grid=(nblk,),
        in_specs=[ublk, bspec, bspec, s8, s8, s8, s8],
        out_specs=(s8, s8), out_shape=(s8shape, s8shape),
        scratch_shapes=[pltpu.VMEM((rows, ns), F32), pltpu.VMEM((rows, ns), F32),
                        pltpu.VMEM((N_SEG, ns), F32), pltpu.VMEM((N_SEG, ns), F32)],
        compiler_params=_params(("arbitrary",)),
    )(u, bsr, bsi, a8r, a8i, al8r, al8i)

    def main_body(u_ref, bsr_ref, bsi_ref, csr_ref, csi_ref, ar_ref, ai_ref, c0r_ref, c0i_ref,
                  d_ref, gw_ref, gb_ref, nw_ref, xr_ref, xi_ref, yp_ref, out_ref, pr_sc, pi_sc):
        b = pl.program_id(0)

        @pl.when(b == 0)
        def _():
            pr_sc[...] = c0r_ref[...]
            pi_sc[...] = c0i_ref[...]

        local_scan(u_ref, bsr_ref, bsi_ref, ar_ref, ai_ref, xr_ref, xi_ref, pr_sc, pi_sc)
        for s in range(N_SEC):
            xs = pl.ds(s * secn, secn)
            us = pl.ds(s * secw, secw)
            y = _dot(xr_ref[:, xs].astype(BF16), csr_ref[s]) + _dot(xi_ref[:, xs].astype(BF16), csi_ref[s])
            yp_ref[:, us] = y + d_ref[:, us] * u_ref[:, us]
        yp = yp_ref[...]
        t = jnp.tanh(GELU_K0 * (yp + GELU_K1 * yp * yp * yp))
        y1 = 0.5 * yp * (1.0 + t)
        z = _dot(y1.astype(BF16), gw_ref[...]) + gb_ref[...]
        y2 = y1 * _sigmoid(z)
        xh, _ = _rms_stats(y2)
        out_ref[...] = (xh * nw_ref[...]).astype(BF16)

    xblk = pl.BlockSpec((rows, ns), lambda b: (b, 0))
    xr, xi, yp, out = pl.pallas_call(
        main_body, name="s5_fwd", grid=(nblk,),
        in_specs=[ublk, bspec, bspec, cspec, cspec, s8, s8, s8, s8, vec,
                  pl.BlockSpec((sw, sw), lambda b: (0, 0)), vec, vec],
        out_specs=(xblk, xblk, ublk, ublk),
        out_shape=(jax.ShapeDtypeStruct((lp, ns), F32), jax.ShapeDtypeStruct((lp, ns), F32),
                   jax.ShapeDtypeStruct((lp, sw), F32), jax.ShapeDtypeStruct((lp, sw), BF16)),
        scratch_shapes=[pltpu.VMEM((N_SEG, ns), F32), pltpu.VMEM((N_SEG, ns), F32)],
        compiler_params=_params(("arbitrary",)),
    )(u, bsr, bsi, csr, csi, a8r, a8i, c0r, c0i, d, gluw, glub, nw)
    return xr, xi, c0r, c0i, yp, out


def _s5_bwd(dout, u, yp, xr, xi, c0r, c0i, bsrt, bsit, csrt, csit, a8r, a8i, al8r, al8i, d, gluw, glub, nw, jb):
    lp, sw = u.shape
    ns = a8r.shape[1]
    rows = N_SEG * jb
    nblk = lp // rows
    secw = sw // N_SEC
    secn = ns // N_SEC

    def rowwise_bwd(dout_ref, yp_ref, gw_ref, gb_ref, nw_ref):
        ypv = yp_ref[...]
        t = jnp.tanh(GELU_K0 * (ypv + GELU_K1 * ypv * ypv * ypv))
        y1 = 0.5 * ypv * (1.0 + t)
        dgelu = 0.5 * (1.0 + t) + 0.5 * ypv * (1.0 - t * t) * GELU_K0 * (1.0 + 3.0 * GELU_K1 * ypv * ypv)
        gw = gw_ref[...]
        y1b = y1.astype(BF16)
        sg = _sigmoid(_dot(y1b, gw) + gb_ref[...])
        xh, r = _rms_stats(y1 * sg)
        dov = dout_ref[...]
        dy2 = _rms_bwd(dov, xh, r, nw_ref[...])
        dz = dy2 * y1 * sg * (1.0 - sg)
        dzb = dz.astype(BF16)
        dy1 = dy2 * sg + _dot_nt(dzb, gw)
        return dy1 * dgelu, dov * xh, y1b, dzb, dz

    def lam_scan(dyp_of, csrt_ref, csit_ref, ar_ref, ai_ref, lr_sc, li_sc, nr_sc, ni_sc, extra):
        for s in range(N_SEC):
            db = dyp_of(s)
            lr_sc[:, s * secn:(s + 1) * secn] = _dot(db, csrt_ref[s])
            li_sc[:, s * secn:(s + 1) * secn] = _dot(db, csit_ref[s])
        top = rows - 8
        prev = [(nr_sc[:, cs], ni_sc[:, cs]) for cs in _scan_chunks(ns)]
        prev = _scan_step(lr_sc, li_sc, top, prev, ar_ref, ai_ref, True, ns)
        extra(top, pl.ds(top - 8, 8))

        def step(jj, carry):
            r0 = pl.multiple_of((jb - 1 - jj) * 8, 8)
            rp = pl.multiple_of((jb - 2 - jj) * 8, 8)
            new = _scan_step(lr_sc, li_sc, r0, _pairs(carry), ar_ref, ai_ref, True, ns)
            extra(r0, pl.ds(rp, 8))
            return _flat(new)

        prev = _pairs(lax.fori_loop(1, jb - 1, step, _flat(prev)))
        last = _scan_step(lr_sc, li_sc, 0, prev, ar_ref, ai_ref, True, ns)
        extra(0, None)
        for cs, (vr, vi) in zip(_scan_chunks(ns), last):
            nr_sc[:, cs] = vr
            ni_sc[:, cs] = vi

    def carry_body(dout_ref, yp_ref, u_ref, gw_ref, gb_ref, nw_ref, csrt_ref, csit_ref, ar_ref, ai_ref,
                   alr_ref, ali_ref, cr_ref, ci_ref, dyp_ref, dnw_ref, dgw_ref, dgb_ref, dd_ref,
                   lr_sc, li_sc, nr_sc, ni_sc):
        b = pl.program_id(0)

        @pl.when(b == 0)
        def _():
            nr_sc[...] = jnp.zeros_like(nr_sc)
            ni_sc[...] = jnp.zeros_like(ni_sc)
            for ref in (dnw_ref, dgw_ref, dgb_ref, dd_ref):
                ref[...] = jnp.zeros_like(ref)

        dyp, dnw_rows, y1b, dzb, dz = rowwise_bwd(dout_ref, yp_ref, gw_ref, gb_ref, nw_ref)
        dnw_ref[...] += jnp.sum(dnw_rows, axis=0, keepdims=True)
        dgw_ref[...] += _dot_tn(y1b, dzb)
        dgb_ref[...] += jnp.sum(dz, axis=0, keepdims=True)
        dd_ref[...] += jnp.sum(dyp * u_ref[...], axis=0, keepdims=True)
        dyp_ref[...] = dyp.astype(BF16)
        lam_scan(lambda s: dyp_ref[:, s * secw:(s + 1) * secw], csrt_ref, csit_ref, ar_ref, ai_ref,
                 lr_sc, li_sc, nr_sc, ni_sc, lambda r0, prev_rows: None)

        @pl.when(b == nblk - 1)
        def _():
            fr = _shift_rows(nr_sc[...], False)
            fi = _shift_rows(ni_sc[...], False)
            alr, ali = alr_ref[...], ali_ref[...]
            cr, ci = fr, fi
            for _ in range(N_SEG - 2):
                sr = _shift_rows(cr, False)
                si = _shift_rows(ci, False)
                cr = fr + alr * sr + ali * si
                ci = fi + alr * si - ali * sr
            cr_ref[...] = cr
            ci_ref[...] = ci

    rev = lambda b: (nblk - 1 - b, 0)
    ublk = pl.BlockSpec((rows, sw), rev)
    xblk = pl.BlockSpec((rows, ns), rev)
    s8 = pl.BlockSpec((N_SEG, ns), lambda b: (0, 0))
    vec = pl.BlockSpec((1, sw), lambda b: (0, 0))
    gws = pl.BlockSpec((sw, sw), lambda b: (0, 0))
    btspec = pl.BlockSpec((N_SEC, secn, secw), lambda b: (0, 0, 0))
    ctspec = pl.BlockSpec((N_SEC, secw, secn), lambda b: (0, 0, 0))
    s8shape = jax.ShapeDtypeStruct((N_SEG, ns), F32)
    lcr, lci, dyp_all, d_nw, d_gw, d_gb, d_d = pl.pallas_call(
        carry_body, name="s5_bwd_carry", grid=(nblk,),
        in_specs=[ublk, ublk, ublk, gws, vec, vec, ctspec, ctspec, s8, s8, s8, s8],
        out_specs=(s8, s8, ublk, vec, gws, vec, vec),
        out_shape=(s8shape, s8shape, jax.ShapeDtypeStruct((lp, sw), BF16), jax.ShapeDtypeStruct((1, sw), F32),
                   jax.ShapeDtypeStruct((sw, sw), F32), jax.ShapeDtypeStruct((1, sw), F32),
                   jax.ShapeDtypeStruct((1, sw), F32)),
        scratch_shapes=[pltpu.VMEM((rows, ns), F32), pltpu.VMEM((rows, ns), F32),
                        pltpu.VMEM((N_SEG, ns), F32), pltpu.VMEM((N_SEG, ns), F32)],
        compiler_params=_params(("arbitrary",)),
    )(dout, yp, u, gluw, glub, nw, csrt, csit, a8r, a8i, al8r, al8i)

    def main_body(dyp_sc, u_ref, xr_ref, xi_ref, xtr_ref, xti_ref, c0r_ref, c0i_ref, lcr_ref, lci_ref,
                  d_ref, bsrt_ref, bsit_ref, csrt_ref, csit_ref, ar_ref, ai_ref,
                  du_ref, dcr_ref, dci_ref, dbr_ref, dbi_ref, dar_ref, dai_ref,
                  lr_sc, li_sc, nr_sc, ni_sc):
        b = pl.program_id(0)

        @pl.when(b == 0)
        def _():
            nr_sc[...] = lcr_ref[...]
            ni_sc[...] = lci_ref[...]
            for ref in (dcr_ref, dci_ref, dbr_ref, dbi_ref, dar_ref, dai_ref):
                ref[...] = jnp.zeros_like(ref)

        for s in range(N_SEC):
            db = dyp_sc[:, s * secw:(s + 1) * secw]
            xs = pl.ds(s * secn, secn)
            dcr_ref[s] += _dot_tn(xr_ref[:, xs].astype(BF16), db)
            dci_ref[s] += _dot_tn(xi_ref[:, xs].astype(BF16), db)

        first = b == nblk - 1

        def acc_da(r0, prev_rows):
            for cc in range(ns // SCAN_CW):
                cs = pl.ds(cc * SCAN_CW, SCAN_CW)
                lr = lr_sc[pl.ds(r0, 8), cs]
                li = li_sc[pl.ds(r0, 8), cs]
                if prev_rows is None:
                    xpr = jnp.where(first, c0r_ref[:, cs], xtr_ref[:, cs])
                    xpi = jnp.where(first, c0i_ref[:, cs], xti_ref[:, cs])
                else:
                    xpr = xr_ref[prev_rows, cs]
                    xpi = xi_ref[prev_rows, cs]
                dar_ref[:, cs] += lr * xpr + li * xpi
                dai_ref[:, cs] += li * xpr - lr * xpi

        lam_scan(lambda s: dyp_sc[:, s * secw:(s + 1) * secw], csrt_ref, csit_ref, ar_ref, ai_ref,
                 lr_sc, li_sc, nr_sc, ni_sc, acc_da)

        for s in range(N_SEC):
            xs = pl.ds(s * secn, secn)
            us = pl.ds(s * secw, secw)
            lrb = lr_sc[:, xs].astype(BF16)
            lib = li_sc[:, xs].astype(BF16)
            du = _dot(lrb, bsrt_ref[s]) + _dot(lib, bsit_ref[s]) + d_ref[:, us] * dyp_sc[:, us].astype(F32)
            du_ref[:, us] = du.astype(BF16)
            ub = u_ref[:, us].astype(BF16)
            dbr_ref[s] += _dot_tn(ub, lrb)
            dbi_ref[s] += _dot_tn(ub, lib)

    tail = pl.BlockSpec((N_SEG, ns), lambda b: (jnp.maximum((nblk - 1 - b) * jb - 1, 0), 0))
    acc_c = pl.BlockSpec((N_SEC, secn, secw), lambda b: (0, 0, 0))
    acc_b = pl.BlockSpec((N_SEC, secw, secn), lambda b: (0, 0, 0))
    du, dcr, dci, dbr, dbi, dar, dai = pl.pallas_call(
        main_body, name="s5_bwd", grid=(nblk,),
        in_specs=[ublk, ublk, xblk, xblk, tail, tail, s8, s8, s8, s8,
                  vec, btspec, btspec, ctspec, ctspec, s8, s8],
        out_specs=(ublk, acc_c, acc_c, acc_b, acc_b, s8, s8),
        out_shape=(jax.ShapeDtypeStruct((lp, sw), BF16),
                   jax.ShapeDtypeStruct((N_SEC, secn, secw), F32),
                   jax.ShapeDtypeStruct((N_SEC, secn, secw), F32),
                   jax.ShapeDtypeStruct((N_SEC, secw, secn), F32),
                   jax.ShapeDtypeStruct((N_SEC, secw, secn), F32),
                   s8shape, s8shape),
        scratch_shapes=[pltpu.VMEM((rows, ns), F32), pltpu.VMEM((rows, ns), F32),
                        pltpu.VMEM((N_SEG, ns), F32), pltpu.VMEM((N_SEG, ns), F32)],
        compiler_params=_params(("arbitrary",)),
    )(dyp_all, u, xr, xi, xr, xi, c0r, c0i, lcr, lci, d, bsrt, bsit, csrt, csit, a8r, a8i)
    return du, d_nw, d_gw, d_gb, d_d, dcr, dci, dbr, dbi, dar, dai


def _outproj_fwd(h, ret, ssm, wo):
    lp, d = h.shape
    nck, rs, _ = wo.shape
    rw = ret.shape[1]
    tm = _tile(lp, 640)
    per = rw // rs

    def body(h_ref, ret_ref, ssm_ref, w_ref, o_ref):
        acc = h_ref[...]
        for c in range(nck):
            src = ret_ref if c < per else ssm_ref
            lo = (c % per) * rs
            acc = acc + _dot(src[:, lo:lo + rs], w_ref[c])
        o_ref[...] = acc

    row = lambda w: pl.BlockSpec((tm, w), lambda i: (i, 0))
    return pl.pallas_call(
        body, name="outproj_fwd", grid=(lp // tm,),
        in_specs=[row(d), row(rw), row(ssm.shape[1]), pl.BlockSpec((nck, rs, d), lambda i: (0, 0, 0))],
        out_specs=row(d), out_shape=jax.ShapeDtypeStruct((lp, d), F32),
        compiler_params=_params(("arbitrary",)),
    )(h, ret, ssm, wo)


def _outproj_bwd(dh, ret, ssm, wo):
    lp, d = dh.shape
    nck, rs, _ = wo.shape
    rw = ret.shape[1]
    sw = ssm.shape[1]
    tm = _tile(lp, 640)
    per = rw // rs
    last = lp // tm - 1

    def body(dh_ref, ret_ref, ssm_ref, w_ref, dret_ref, dssm_ref, dw_ref, acc_sc):
        i = pl.program_id(0)

        @pl.when(i == 0)
        def _():
            acc_sc[...] = jnp.zeros_like(acc_sc)

        dhb = dh_ref[...].astype(BF16)
        for c in range(nck):
            src, dst = (ret_ref, dret_ref) if c < per else (ssm_ref, dssm_ref)
            lo = (c % per) * rs
            dst[:, lo:lo + rs] = _dot_nt(dhb, w_ref[c])
            acc_sc[c] += _dot_tn(src[:, lo:lo + rs], dhb)

        @pl.when(i == last)
        def _():
            dw_ref[...] = acc_sc[...].astype(BF16)

    row = lambda w: pl.BlockSpec((tm, w), lambda i: (i, 0))
    wsp = pl.BlockSpec((nck, rs, d), lambda i: (0, 0, 0))
    return pl.pallas_call(
        body, name="outproj_bwd", grid=(lp // tm,),
        in_specs=[row(d), row(rw), row(sw), wsp],
        out_specs=(row(rw), row(sw), wsp),
        out_shape=(jax.ShapeDtypeStruct((lp, rw), F32), jax.ShapeDtypeStruct((lp, sw), F32),
                   jax.ShapeDtypeStruct((nck, rs, d), BF16)),
        scratch_shapes=[pltpu.VMEM((nck, rs, d), F32)],
        compiler_params=_params(("arbitrary",)),
    )(dh, ret, ssm, wo)


def _loss_head(h, fw, target):
    lp, d = h.shape
    tm = _tile(lp, 1664, CHUNK)
    sub = tm // CHUNK

    def body(h_ref, w_ref, *rest):
        t_refs = rest[:sub]
        loss_ref, dh_ref, dw_ref = rest[sub:]
        i = pl.program_id(0)

        @pl.when(i == 0)
        def _():
            loss_ref[...] = jnp.zeros_like(loss_ref)
            dw_ref[...] = jnp.zeros_like(dw_ref)

        w = w_ref[...]
        for j in range(sub):
            rows = pl.ds(j * CHUNK, CHUNK)
            xh, r = _rms_stats(h_ref[rows, :])
            err = xh * w - t_refs[j][...]
            if j == 0:
                err = jnp.where(i == 0, 0.0, err)
            loss_ref[...] += 0.5 * jnp.sum(err * err) / d
            dout = err * (1.0 / d)
            dw_ref[...] += jnp.sum(dout * xh, axis=0, keepdims=True)
            dh_ref[rows, :] = _rms_bwd(dout, xh, r, w)

    t_spec = lambda j: pl.BlockSpec((CHUNK, d), lambda i: (jnp.maximum(i * sub + j - 1, 0), 0))
    return pl.pallas_call(
        body, name="loss_head", grid=(lp // tm,),
        in_specs=[pl.BlockSpec((tm, d), lambda i: (i, 0)), pl.BlockSpec((1, d), lambda i: (0, 0))]
        + [t_spec(j) for j in range(sub)],
        out_specs=(pl.BlockSpec((8, LANE), lambda i: (0, 0)), pl.BlockSpec((tm, d), lambda i: (i, 0)),
                   pl.BlockSpec((1, d), lambda i: (0, 0))),
        out_shape=(jax.ShapeDtypeStruct((8, LANE), F32), jax.ShapeDtypeStruct((lp, d), F32),
                   jax.ShapeDtypeStruct((1, d), F32)),
        compiler_params=_params(("arbitrary",)),
    )(h, fw, *([target] * sub))


def _pack(arrs):
    flat = jnp.concatenate([a.reshape(-1).astype(F32) for a in arrs])
    n = flat.shape[0]
    rows = -(-n // (8 * LANE)) * 8
    return jnp.pad(flat, (0, rows * LANE - n)).reshape(rows, LANE)


def _unpack(packed, shapes):
    flat = packed.reshape(-1)
    out, off = [], 0
    for s in shapes:
        n = math.prod(s)
        out.append(flat[off:off + n].reshape(s))
        off += n
    return out


def _to_segments(a, seg_len):
    return a.reshape(N_SEG, seg_len, a.shape[1]).transpose(1, 0, 2).reshape(a.shape)


def _from_segments(a, seg_len):
    return a.reshape(seg_len, N_SEG, a.shape[1]).transpose(1, 0, 2).reshape(a.shape)


WEIGHT_NAMES = ['meta_tokens', 'ffn1_norm_w', 'ffn1_w_gate', 'ffn1_w_up', 'ffn1_w_down', 'mix_norm_w', 'w_in',
                'ret_norm_w', 'ssm_lambda_re', 'ssm_lambda_im', 'ssm_log_dt', 'ssm_b_re', 'ssm_b_im', 'ssm_c_re',
                'ssm_c_im', 'ssm_d', 'ssm_glu_w', 'ssm_glu_b', 'ssm_norm_w', 'w_out', 'ffn2_norm_w', 'ffn2_w_gate',
                'ffn2_w_up', 'ffn2_w_down', 'final_norm_w']
BIG = ['ffn1_w_gate', 'ffn1_w_up', 'ffn1_w_down', 'w_in', 'ssm_glu_w', 'w_out', 'ffn2_w_gate', 'ffn2_w_up',
       'ffn2_w_down']
TRANSPOSED = ['ffn1_w_gate', 'ffn1_w_up', 'ffn2_w_gate', 'ffn2_w_up']
BIG_EARLY = ['ffn1_w_gate', 'ffn1_w_up', 'ffn1_w_down']
BIG_LATE = [n for n in BIG if n not in BIG_EARLY]
SMALL = [n for n in WEIGHT_NAMES if n not in BIG]


def kernel(x, meta_tokens, ffn1_norm_w, ffn1_w_gate, ffn1_w_up, ffn1_w_down, mix_norm_w, w_in, ret_norm_w, ssm_lambda_re, ssm_lambda_im, ssm_log_dt, ssm_b_re, ssm_b_im, ssm_c_re, ssm_c_im, ssm_d, ssm_glu_w, ssm_glu_b, ssm_norm_w, w_out, ffn2_norm_w, ffn2_w_gate, ffn2_w_up, ffn2_w_down, final_norm_w, loss_target, m_meta_tokens, m_ffn1_norm_w, m_ffn1_w_gate, m_ffn1_w_up, m_ffn1_w_down, m_mix_norm_w, m_w_in, m_ret_norm_w, m_ssm_lambda_re, m_ssm_lambda_im, m_ssm_log_dt, m_ssm_b_re, m_ssm_b_im, m_ssm_c_re, m_ssm_c_im, m_ssm_d, m_ssm_glu_w, m_ssm_glu_b, m_ssm_norm_w, m_w_out, m_ffn2_norm_w, m_ffn2_w_gate, m_ffn2_w_up, m_ffn2_w_down, m_final_norm_w, v_meta_tokens, v_ffn1_norm_w, v_ffn1_w_gate, v_ffn1_w_up, v_ffn1_w_down, v_mix_norm_w, v_w_in, v_ret_norm_w, v_ssm_lambda_re, v_ssm_lambda_im, v_ssm_log_dt, v_ssm_b_re, v_ssm_b_im, v_ssm_c_re, v_ssm_c_im, v_ssm_d, v_ssm_glu_w, v_ssm_glu_b, v_ssm_norm_w, v_w_out, v_ffn2_norm_w, v_ffn2_w_gate, v_ffn2_w_up, v_ffn2_w_down, v_final_norm_w):
    args = locals()
    w = {n: args[n] for n in WEIGHT_NAMES}
    m = {n: args["m_" + n] for n in WEIGHT_NAMES}
    v = {n: args["v_" + n] for n in WEIGHT_NAMES}

    seq, d = x.shape[1], x.shape[2]
    lp = seq + CHUNK
    seg_len = lp // N_SEG
    rw = RET_HEADS * HEAD_DIM
    sw = ssm_d.shape[-1]
    groups = sw // SSM_GROUP
    ns = groups * SSM_STATE
    jb = _tile(seg_len, S5_STEPS, 8)
    chip = 2 * lax.axis_index("x") + lax.axis_index("y")

    as_fd = lambda t: jnp.swapaxes(t, -1, -2)
    shards = {n: (as_fd(w[n][0]) if n in TRANSPOSED else w[n][0]).astype(BF16) for n in BIG}
    early = [shards[n] for n in BIG_EARLY] + [meta_tokens]
    gathered = _gather_two_level("gather_early", early)
    gw = dict(zip(BIG_EARLY, gathered[:-1]))
    meta_full = jnp.transpose(gathered[-1], (1, 0, 2)).reshape(N_META, d)
    late = [shards[n] for n in BIG_LATE]

    pos = jnp.arange(lp, dtype=F32) - float(CHUNK - N_META)
    freqs = 1.0 / (ROPE_BASE ** (jnp.arange(0, HEAD_DIM, 2, dtype=F32) / HEAD_DIM))
    ang = pos[:, None] * freqs[None, :]
    cosf = jnp.concatenate([jnp.cos(ang), jnp.cos(ang)], axis=1)
    sinf = jnp.concatenate([-jnp.sin(ang), jnp.sin(ang)], axis=1)
    tables = _retention_tables(_tile(lp, RET_ROWS, CHUNK))

    lam_re, lam_im, log_dt = ssm_lambda_re[0], ssm_lambda_im[0], ssm_log_dt[0]
    b_re, b_im, c_re, c_im = ssm_b_re[0], ssm_b_im[0], ssm_c_re[0], ssm_c_im[0]
    (ar, ai, bbr, bbi), prep_vjp = jax.vjp(_s5_prepare, lam_re, lam_im, log_dt, b_re, b_im)
    dt = jnp.exp(log_dt)[:, None]
    el = jnp.exp(seg_len * lam_re * dt)
    alr = el * jnp.cos(seg_len * lam_im * dt)
    ali = el * jnp.sin(seg_len * lam_im * dt)
    bc8 = lambda t: jnp.broadcast_to(t.reshape(1, ns), (N_SEG, ns))
    a8r, a8i, al8r, al8i = bc8(ar), bc8(ai), bc8(alr), bc8(ali)
    bsr = _blockdiag_in(jnp.transpose(bbr, (0, 2, 1)))
    bsi = _blockdiag_in(jnp.transpose(bbi, (0, 2, 1)))
    csrt = _blockdiag_in(c_re)
    csit = _blockdiag_in(-c_im)
    tr = lambda t: jnp.transpose(t, (0, 2, 1))
    bsr_b, bsi_b = bsr.astype(BF16), bsi.astype(BF16)
    csr_b, csi_b = tr(csrt).astype(BF16), tr(csit).astype(BF16)
    bsrt_b, bsit_b = tr(bsr).astype(BF16), tr(bsi).astype(BF16)
    csrt_b, csit_b = csrt.astype(BF16), csit.astype(BF16)

    h0 = (jnp.concatenate([jnp.zeros((CHUNK - N_META, d), F32), meta_full], axis=0), x[0])
    (h1, g1, u1), late_half = _ffn_fwd("ffn1_fwd", h0, ffn1_norm_w, gw['ffn1_w_gate'], gw['ffn1_w_up'],
                                       gw['ffn1_w_down'], _allgather_chips_plan(late), late)
    gw.update(zip(BIG_LATE, _forward_sibling("gather_late_forward", late_half)))
    glu_full = gw['ssm_glu_w'].reshape(sw, sw)
    n2, q, k, vv, gate, u = _inproj_fwd(h1, mix_norm_w, gw['w_in'], cosf, sinf, rw)
    o, ret, sprev = _ret_fwd(q, k, vv, gate, ret_norm_w, tables)
    u_seg = _to_segments(u, seg_len)
    xr, xi, c0r, c0i, yp, ssm_seg = _s5_fwd(u_seg, bsr_b, bsi_b, csr_b, csi_b, a8r, a8i, al8r, al8i,
                                            ssm_d, glu_full, ssm_glu_b, ssm_norm_w, jb)
    ssm = _from_segments(ssm_seg, seg_len)
    h2 = _outproj_fwd(h1, ret, ssm, gw['w_out'])
    (h3, g2, u2), _ = _ffn_fwd("ffn2_fwd", h2, ffn2_norm_w, gw['ffn2_w_gate'], gw['ffn2_w_up'], gw['ffn2_w_down'])
    loss_part, dh3, d_final = _loss_head(h3, final_norm_w.reshape(1, d), loss_target[0])

    (dh2, d_ffn2_norm, nb, daccb, ab, dgb, dub), _ = _ffn_bwd_act(
        "ffn2_bwd_act", dh3, h2, ffn2_norm_w, g2, u2, gw['ffn2_w_gate'], gw['ffn2_w_up'], gw['ffn2_w_down'])
    (dwg2, dwu2, dwd2), _ = _ffn_bwd_w("ffn2_bwd_w", nb, daccb, ab, dgb, dub)
    dret, dssm, dwo = _outproj_bwd(dh2, ret, ssm, gw['w_out'])
    (du_seg, d_ssm_norm, d_glu_w, d_glu_b, d_ssm_d, dcr_s, dci_s, dbr_s, dbi_s, dar8, dai8) = _s5_bwd(
        _to_segments(dssm, seg_len), u_seg, yp, xr, xi, c0r, c0i, bsrt_b, bsit_b, csrt_b, csit_b,
        a8r, a8i, al8r, al8i, ssm_d, glu_full, ssm_glu_b, ssm_norm_w, jb)
    du = _from_segments(du_seg, seg_len)
    dq, dk, dv, dgate, d_ret_norm = _ret_bwd(dret, q, k, vv, gate, o, sprev, ret_norm_w, tables, cosf, sinf)
    dh1, d_mix_norm, dwin = _inproj_bwd(dh2, h1, mix_norm_w, n2, gw['w_in'], dq, dk, dv, dgate, du)
    late_parts = {
        'w_in': dwin, 'ssm_glu_w': d_glu_w.reshape(N_CHIP, sw // N_CHIP, sw).astype(BF16), 'w_out': dwo,
        'ffn2_w_gate': dwg2, 'ffn2_w_up': dwu2, 'ffn2_w_down': dwd2,
    }
    late_list = [late_parts[n] for n in BIG_LATE]
    (dh0, d_ffn1_norm, nb, daccb, ab, dgb, dub), late_recv = _ffn_bwd_act(
        "ffn1_bwd_act", dh1, h0, ffn1_norm_w, g1, u1, gw['ffn1_w_gate'], gw['ffn1_w_up'], gw['ffn1_w_down'],
        _alltoall_chips_plan(late_list), late_list)
    grad_x = dh0[CHUNK:][None]
    d_meta = dh0[CHUNK - N_META:CHUNK]

    d_c_re = jnp.transpose(_blockdiag_out(tr(dcr_s), groups, SSM_GROUP, SSM_STATE), (0, 1, 2))
    d_c_im = -_blockdiag_out(tr(dci_s), groups, SSM_GROUP, SSM_STATE)
    d_bbr = jnp.transpose(_blockdiag_out(dbr_s, groups, SSM_GROUP, SSM_STATE), (0, 2, 1))
    d_bbi = jnp.transpose(_blockdiag_out(dbi_s, groups, SSM_GROUP, SSM_STATE), (0, 2, 1))
    d_ar = jnp.sum(dar8, axis=0).reshape(groups, SSM_STATE)
    d_ai = jnp.sum(dai8, axis=0).reshape(groups, SSM_STATE)
    small_parts = [loss_part[0:1, :], d_meta, d_ffn1_norm, d_mix_norm, d_ret_norm, d_ar, d_ai, d_bbr, d_bbi,
                   d_c_re, d_c_im, d_ssm_d, d_glu_b, d_ssm_norm, d_ffn2_norm, d_final]
    small_shapes = [a.shape for a in small_parts]
    packed = _pack(small_parts)
    early_recv, (all_parts,) = _ffn_bwd_w_scatter("ffn1_bwd_w", nb, daccb, ab, dgb, dub, chip,
                                                  _allgather_all_plan([packed]), [packed])
    received = dict(zip(BIG_LATE + BIG_EARLY, late_recv + early_recv))
    ffn_names = [n for n in BIG if n.startswith('ffn')]
    chip_sum = dict(zip(ffn_names, _sum_slots("sum_chips_ffn", [received[n] for n in ffn_names], BF16)))
    for n in BIG:
        if n not in chip_sum:
            chip_sum[n] = _sum_slots("sum_chips_" + n, [received[n]], BF16)[0]
    chip_sums = [chip_sum[n] for n in BIG]
    sib_sums = _swap_sibling("swap_sibling", chip_sums)
    (loss_row, g_meta_full, g_ffn1_norm, g_mix_norm, g_ret_norm, g_ar, g_ai, g_bbr, g_bbi, g_c_re, g_c_im,
     g_ssm_d, g_glu_b, g_ssm_norm, g_ffn2_norm, g_final) = _unpack(_sum_slots("sum_small", [all_parts], F32)[0],
                                                                  small_shapes)
    g_lam_re, g_lam_im, g_log_dt, g_b_re, g_b_im = prep_vjp((g_ar, g_ai, g_bbr, g_bbi))
    loss = loss_row[0, 0]
    g_meta = lax.dynamic_slice(g_meta_full, (0, chip * (d // N_CHIP)), (N_META, d // N_CHIP))
    small_grads = {
        'meta_tokens': g_meta, 'ffn1_norm_w': g_ffn1_norm, 'mix_norm_w': g_mix_norm, 'ret_norm_w': g_ret_norm,
        'ssm_lambda_re': g_lam_re[None], 'ssm_lambda_im': g_lam_im[None], 'ssm_log_dt': g_log_dt[None],
        'ssm_b_re': g_b_re[None], 'ssm_b_im': g_b_im[None], 'ssm_c_re': g_c_re[None], 'ssm_c_im': g_c_im[None],
        'ssm_d': g_ssm_d, 'ssm_glu_b': g_glu_b, 'ssm_norm_w': g_ssm_norm, 'ffn2_norm_w': g_ffn2_norm,
        'final_norm_w': g_final.reshape(d),
    }

    grads, deltas, new_m, new_v = {}, {}, {}, {}
    g_pair = {n: [mine, sib] for n, mine, sib in zip(BIG, chip_sums, sib_sums)}
    view = lambda n, t: as_fd(t) if n in TRANSPOSED else t
    ffn_out = _adam("adam_ffn", [(view(n, w[n]), view(n, m[n]), view(n, v[n])) for n in ffn_names],
                    [g_pair[n] for n in ffn_names])
    for n, outs in zip(ffn_names, ffn_out):
        grads[n], deltas[n], new_m[n], new_v[n] = [view(n, t) for t in outs]
    for n in BIG:
        if n not in ffn_names:
            grads[n], deltas[n], new_m[n], new_v[n] = _adam("adam_" + n, [(w[n], m[n], v[n])], [g_pair[n]])[0]
    sm_shapes = [w[n].shape for n in SMALL]
    sm_out = _adam("adam_small", [(_pack([w[n] for n in SMALL]), _pack([m[n] for n in SMALL]),
                                  _pack([v[n] for n in SMALL]))],
                   [[_pack([small_grads[n].reshape(w[n].shape) for n in SMALL])]])[0]
    for dst, packed in zip((grads, deltas, new_m, new_v), sm_out):
        for n, t in zip(SMALL, _unpack(packed, sm_shapes)):
            dst[n] = t

    return (loss, grad_x, *[grads[n] for n in WEIGHT_NAMES], *[deltas[n] for n in WEIGHT_NAMES],
            *[new_m[n] for n in WEIGHT_NAMES], *[new_v[n] for n in WEIGHT_NAMES])
```

```python
import functools
import math

import jax
import jax.numpy as jnp
from jax import lax
from jax.experimental import pallas as pl
from jax.experimental.pallas import tpu as pltpu

N_META = 16
RET_HEADS = 4
HEAD_DIM = 128
SSM_GROUP = 16
SSM_STATE = 64
CHUNK = 128
ROPE_BASE = 10000.0
EPS = 1e-6
FFN_RES = 0.5
N_SEG = 8
N_SEC = 4
N_CHIP = 4
LANE = 128
FFN_CPS = 2
BWD_W_ROWS = 1664

ADAM_LR = 0.001
ADAM_B1 = 0.9
ADAM_B2 = 0.999
ADAM_EPS = 1e-08
ADAM_WD = 0.01
ADAM_STEP = 10

VMEM_LIMIT = 56 * 1024 * 1024

F32 = jnp.float32
BF16 = jnp.bfloat16
MESH = pl.DeviceIdType.MESH


def _dot(a, b):
    return jnp.dot(a, b, preferred_element_type=F32)


def _dot_nt(a, b):
    return lax.dot_general(a, b, (((1,), (1,)), ((), ())), preferred_element_type=F32)


def _dot_tn(a, b):
    return lax.dot_general(a, b, (((0,), (0,)), ((), ())), preferred_element_type=F32)


def _tile(n, target, mult=64):
    best = None
    t = mult
    while t <= min(n, target):
        if n % t == 0:
            best = t
        t += mult
    assert best is not None, (n, target)
    return best


def _params(sem, vmem=VMEM_LIMIT):
    return pltpu.CompilerParams(dimension_semantics=sem, vmem_limit_bytes=vmem)


def _rms_stats(xf):
    r = lax.rsqrt(jnp.mean(xf * xf, axis=-1, keepdims=True) + EPS)
    return xf * r, r


def _rms_bwd(dy, xh, r, w):
    dxh = dy * w
    return r * (dxh - xh * jnp.mean(dxh * xh, axis=-1, keepdims=True))


def _sigmoid(x):
    return 0.5 * jnp.tanh(0.5 * x) + 0.5


GELU_K0 = math.sqrt(2.0 / math.pi)
GELU_K1 = 0.044715


CHIP_MASKS = [(1, 0, 0), (0, 1, 0), (1, 1, 0)]
ALL_MASKS = [(0, 0, 1), (0, 1, 0), (0, 1, 1), (1, 0, 0), (1, 0, 1), (1, 1, 0), (1, 1, 1)]
SIB_MASKS = [(0, 0, 1)]
ANY_SPEC = pl.BlockSpec(memory_space=pl.ANY)


class _Plan:
    def __init__(self, arrays, masks, n_slots, src_slotted, dst_slotted, local_copy, half=False, forward=False):
        self.shapes = [(a.shape, a.dtype) for a in arrays]
        self.n = len(arrays)
        self.masks = masks
        self.n_slots = n_slots
        self.src_slotted, self.dst_slotted, self.local_copy = src_slotted, dst_slotted, local_copy
        self.half, self.forward = half, forward
        self.n_cp = self.n * len(masks) * (len(CHIP_MASKS) if forward else 1)

    def out_shape(self):
        out = []
        for shp, dt in self.shapes:
            if self.dst_slotted and not self.src_slotted:
                shp = (self.n_slots,) + shp
            elif self.src_slotted and not self.dst_slotted:
                shp = shp[1:]
            out.append(jax.ShapeDtypeStruct(shp, dt))
        return tuple(out)

    def scratch(self):
        return [pltpu.SemaphoreType.DMA((self.n_cp,)), pltpu.SemaphoreType.DMA((self.n_cp,)),
                pltpu.SemaphoreType.DMA((self.n,))]

    def _slot(self, px, py, pc):
        if self.n_slots == 8:
            return 4 * px + 2 * py + pc
        if self.n_slots == 4:
            return 2 * px + py
        return pc

    def copies(self, ins, outs, sems):
        send_sems, recv_sems, loc_sems = sems
        x, y, c = lax.axis_index("x"), lax.axis_index("y"), lax.axis_index("c")
        me = self._slot(x, y, c)
        n_m = len(self.masks)
        cps = []
        for a in range(self.n):
            if self.forward:
                rows = self.shapes[a][0][-2] // 2
                mine = pl.ds(pl.multiple_of(c * rows, 8), rows)
                for j, (mx, my, _) in enumerate(CHIP_MASKS):
                    blk = outs[a].at[2 * (1 - x if mx else x) + (1 - y if my else y), mine]
                    k = a * len(CHIP_MASKS) + j
                    cps.append(pltpu.make_async_remote_copy(
                        src_ref=blk, dst_ref=blk, send_sem=send_sems.at[k], recv_sem=recv_sems.at[k],
                        device_id=(x, y, 1 - c), device_id_type=MESH))
                continue
            if self.local_copy:
                src = ins[a].at[me] if self.src_slotted else ins[a]
                cps.append(pltpu.make_async_copy(src, outs[a].at[me], loc_sems.at[a]))
            for mi, (mx, my, mc) in enumerate(self.masks):
                px = 1 - x if mx else x
                py = 1 - y if my else y
                pc = 1 - c if mc else c
                src = ins[a].at[self._slot(px, py, pc)] if self.src_slotted else ins[a]
                dst = outs[a].at[me] if self.dst_slotted else outs[a]
                if self.half:
                    rows = src.shape[-2] // 2
                    mine = pl.ds(pl.multiple_of(c * rows, 8), rows)
                    src, dst = src.at[mine], dst.at[mine]
                k = a * n_m + mi
                cps.append(pltpu.make_async_remote_copy(
                    src_ref=src, dst_ref=dst, send_sem=send_sems.at[k], recv_sem=recv_sems.at[k],
                    device_id=(px, py, pc), device_id_type=MESH))
        return cps


def _exchange(name, plan, arrays):
    n = plan.n

    def body(*refs):
        cps = plan.copies(refs[:n], refs[n:2 * n], refs[2 * n:])
        for cp in cps:
            cp.start()
        for cp in cps:
            cp.wait()

    outs = pl.pallas_call(
        body, name=name, out_shape=plan.out_shape(),
        in_specs=[ANY_SPEC] * n, out_specs=tuple([ANY_SPEC] * n), scratch_shapes=plan.scratch(),
        input_output_aliases={i: i for i in range(n)} if plan.forward else {},
    )(*arrays)
    return list(outs)


def _pcall(body, *, name, grid, in_specs, out_specs, out_shape, scratch_shapes, args, plan=None, plan_args=()):
    sem = ("arbitrary",) * len(grid)
    if plan is None:
        return pl.pallas_call(body, name=name, grid=grid, in_specs=in_specs, out_specs=out_specs,
                              out_shape=out_shape, scratch_shapes=scratch_shapes,
                              compiler_params=_params(sem))(*args), []
    n_in, n_out, n_scr, n_p = len(in_specs), len(out_specs), len(scratch_shapes), plan.n

    def wrapped(*refs):
        ins = refs[:n_in]
        p_ins = refs[n_in:n_in + n_p]
        o0 = n_in + n_p
        outs = refs[o0:o0 + n_out]
        p_outs = refs[o0 + n_out:o0 + n_out + n_p]
        s0 = o0 + n_out + n_p
        scr = refs[s0:s0 + n_scr]
        sems = refs[s0 + n_scr:]
        ids = [pl.program_id(i) for i in range(len(grid))]
        first = functools.reduce(jnp.logical_and, [i == 0 for i in ids])
        last = functools.reduce(jnp.logical_and, [i == g - 1 for i, g in zip(ids, grid)])

        @pl.when(first)
        def _():
            for cp in plan.copies(p_ins, p_outs, sems):
                cp.start()

        body(*ins, *outs, *scr)

        @pl.when(last)
        def _():
            for cp in plan.copies(p_ins, p_outs, sems):
                cp.wait()

    res = pl.pallas_call(
        wrapped, name=name, grid=grid,
        in_specs=list(in_specs) + [ANY_SPEC] * n_p,
        out_specs=tuple(out_specs) + (ANY_SPEC,) * n_p,
        out_shape=tuple(out_shape) + plan.out_shape(),
        scratch_shapes=list(scratch_shapes) + plan.scratch(),
        compiler_params=_params(sem),
    )(*args, *plan_args)
    return res[:n_out], list(res[n_out:])


def _allgather_chips_plan(arrays):
    return _Plan(arrays, CHIP_MASKS, 4, False, True, True, half=True)


def _gather_two_level(name, arrays):
    n = len(arrays)
    ici = _allgather_chips_plan(arrays)
    fwd = _Plan(ici.out_shape(), SIB_MASKS, 4, True, True, False, forward=True)
    n_m = len(CHIP_MASKS)

    def body(*refs):
        ins, outs, sems = refs[:n], refs[n:2 * n], refs[2 * n:]
        ici_cps = ici.copies(ins, outs, sems[:3])
        fwd_cps = fwd.copies(None, outs, sems[3:])
        for cp in ici_cps:
            cp.start()
        for a in range(n):
            for m in range(n_m):
                ici_cps[a * (n_m + 1) + 1 + m].wait_recv()
                fwd_cps[a * n_m + m].start()
        for a in range(n):
            ici_cps[a * (n_m + 1)].wait()
            for m in range(n_m):
                ici_cps[a * (n_m + 1) + 1 + m].wait_send()
        for cp in fwd_cps:
            cp.wait()

    return list(pl.pallas_call(
        body, name=name, out_shape=ici.out_shape(),
        in_specs=[ANY_SPEC] * n, out_specs=tuple([ANY_SPEC] * n), scratch_shapes=ici.scratch() + fwd.scratch(),
    )(*arrays))


def _forward_sibling(name, gathered):
    return _exchange(name, _Plan(gathered, SIB_MASKS, 4, True, True, False, forward=True), gathered)


def _alltoall_chips_plan(arrays):
    return _Plan(arrays, CHIP_MASKS, 4, True, True, True)


def _swap_sibling(name, arrays):
    return _exchange(name, _Plan(arrays, SIB_MASKS, 2, False, False, False), arrays)


def _allgather_all_plan(arrays):
    return _Plan(arrays, ALL_MASKS, 8, False, True, True)


def _sum_slots(name, arrs, out_dtype):
    s, r = arrs[0].shape[0], arrs[0].shape[-2]
    c = arrs[0].shape[-1] * (2 if arrs[0].ndim == 4 else 1)
    n = len(arrs)
    tr = _tile(r, 512 if n == 1 else 176, 8)

    def body(*refs):
        for a_ref, o_ref in zip(refs[:n], refs[n:]):
            if len(a_ref.shape) == 4:
                for half in range(2):
                    acc = a_ref[0, half].astype(F32)
                    for i in range(1, s):
                        acc = acc + a_ref[i, half].astype(F32)
                    o_ref[:, half * (c // 2):(half + 1) * (c // 2)] = acc.astype(out_dtype)
            else:
                acc = a_ref[0].astype(F32)
                for i in range(1, s):
                    acc = acc + a_ref[i].astype(F32)
                o_ref[...] = acc.astype(out_dtype)

    def in_spec(a):
        if a.ndim == 4:
            return pl.BlockSpec((s, 2, tr, c // 2), lambda i: (0, 0, i, 0))
        return pl.BlockSpec((s, tr, c), lambda i: (0, i, 0))

    return list(pl.pallas_call(
        body, name=name, grid=(r // tr,),
        in_specs=[in_spec(a) for a in arrs],
        out_specs=(pl.BlockSpec((tr, c), lambda i: (i, 0)),) * n,
        out_shape=(jax.ShapeDtypeStruct((r, c), out_dtype),) * n,
        compiler_params=_params(("arbitrary",)),
    )(*arrs))


def _adam_math(w, g, m, v):
    m_new = ADAM_B1 * m + (1.0 - ADAM_B1) * g
    v_new = ADAM_B2 * v + (1.0 - ADAM_B2) * (g * g)
    m_hat = m_new / (1.0 - ADAM_B1 ** ADAM_STEP)
    v_hat = v_new / (1.0 - ADAM_B2 ** ADAM_STEP)
    delta = -ADAM_LR * (m_hat / (jnp.sqrt(v_hat) + ADAM_EPS) + ADAM_WD * w)
    return delta, m_new, v_new


def _adam(name, wmv, g_parts):
    w0 = wmv[0][0]
    r, c = w0.shape[-2:]
    n_w = len(wmv)
    n_g = len(g_parts[0])
    tr = _tile(r, 256 if n_w == 1 else 88, 8)
    lead = w0.ndim == 3
    at = (lambda ref: ref.at[0]) if lead else (lambda ref: ref)
    n_in = 3 + n_g

    def body(*refs):
        for j in range(n_w):
            ins = refs[j * n_in:(j + 1) * n_in]
            outs = refs[n_w * n_in + 4 * j:n_w * n_in + 4 * j + 4]
            w_ref, m_ref, v_ref = [at(t) for t in ins[:3]]
            g_out, d_out, m_out, v_out = [at(t) for t in outs]
            g = ins[3][...].astype(F32)
            for gr in ins[4:]:
                g = g + gr[...].astype(F32)
            delta, m_new, v_new = _adam_math(w_ref[...], g, m_ref[...], v_ref[...])
            g_out[...] = g
            d_out[...] = delta
            m_out[...] = m_new
            v_out[...] = v_new

    spec = pl.BlockSpec((tr, c), lambda i: (i, 0))
    wspec = pl.BlockSpec((1, tr, c), lambda i: (0, i, 0)) if lead else spec
    shp = jax.ShapeDtypeStruct(w0.shape, F32)
    args = [t for (w, m, v), gp in zip(wmv, g_parts) for t in (w, m, v, *gp)]
    res = pl.pallas_call(
        body, name=name, grid=(r // tr,),
        in_specs=([wspec] * 3 + [spec] * n_g) * n_w, out_specs=(wspec,) * (4 * n_w), out_shape=(shp,) * (4 * n_w),
        compiler_params=_params(("arbitrary",)),
    )(*args)
    return [tuple(res[4 * j:4 * j + 4]) for j in range(n_w)]


SUB_ROWS = 32
FFN_BWD_ROWS = 416
FFN_FWD_ROWS = 832
FFN_LOSS_ROWS = 640
RET_ROWS = 640
S5_STEPS = 104


def _tile_parts(tm, d, head, x):
    nsub = tm // SUB_ROWS
    off = head.shape[0] // SUB_ROWS
    specs = [pl.BlockSpec(head.shape, lambda i, k: (0, 0))] + [
        pl.BlockSpec((SUB_ROWS, d), lambda i, k, j=j: (jnp.maximum(i * nsub + j - off, 0), 0)) for j in range(nsub)]

    def assemble(i, part_refs, h_sc):
        head_ref, x_refs = part_refs[0], part_refs[1:]
        for j in range(nsub):
            rows = slice(j * SUB_ROWS, (j + 1) * SUB_ROWS)
            val = x_refs[j][...]
            if j < off:
                val = jnp.where(i == 0, head_ref[rows, :], val)
            h_sc[rows, :] = val

    return specs, [head] + [x] * nsub, assemble


def _h_source(body, h, tm, d):
    if not isinstance(h, tuple):
        return body, [pl.BlockSpec((tm, d), lambda i, k: (i, 0))], [h], []
    specs, args, assemble = _tile_parts(tm, d, *h)
    n_h = len(specs)

    def with_parts(*refs):
        h_sc = refs[-1]

        @pl.when(pl.program_id(1) == 0)
        def _():
            assemble(pl.program_id(0), refs[:n_h], h_sc)

        body(h_sc, *refs[n_h:-1])

    return with_parts, specs, args, [pltpu.VMEM((tm, d), F32)]


def _ffn_fwd(name, h, nw, wg, wu, wd, plan=None, plan_args=(), loss=None):
    lp, d = (h[0].shape[0] + h[1].shape[0], h[1].shape[1]) if isinstance(h, tuple) else h.shape
    nck, f, _ = wg.shape
    tm = _tile(lp, FFN_FWD_ROWS if loss is None else FFN_LOSS_ROWS)
    last = nck // FFN_CPS - 1
    n_t = 0
    if loss is not None:
        t_specs, t_args, t_assemble = _tile_parts(tm, d, jnp.zeros((lp - loss[1].shape[0], d), F32), loss[1])
        n_t = len(t_specs)

    def body(h_ref, nw_ref, wg_ref, wu_ref, wd_ref, *rest):
        if loss is not None:
            fw_ref, t_parts, rest = rest[0], rest[1:1 + n_t], rest[1 + n_t:]
            ho_ref, g_ref, u_ref, loss_ref, dfw_ref, n_sc, acc_sc, t_sc = rest
        else:
            ho_ref, g_ref, u_ref, n_sc, acc_sc = rest
        i = pl.program_id(0)
        k = pl.program_id(1)

        @pl.when(k == 0)
        def _():
            xh, _ = _rms_stats(h_ref[...])
            n_sc[...] = (xh * nw_ref[...]).astype(BF16)
            acc_sc[...] = jnp.zeros_like(acc_sc)

        n = n_sc[...]
        acc = acc_sc[...]
        for c in range(FFN_CPS):
            g = _dot_nt(n, wg_ref[c])
            u = _dot_nt(n, wu_ref[c])
            g_ref[c] = g.astype(BF16)
            u_ref[c] = u.astype(BF16)
            a = (g * _sigmoid(g) * u).astype(BF16)
            acc = acc + _dot(a, wd_ref[c])
        acc_sc[...] = acc

        if loss is None:
            @pl.when(k == last)
            def _():
                ho_ref[...] = h_ref[...] + FFN_RES * acc_sc[...]
            return

        @pl.when(jnp.logical_and(i == 0, k == 0))
        def _():
            loss_ref[...] = jnp.zeros_like(loss_ref)
            dfw_ref[...] = jnp.zeros_like(dfw_ref)

        @pl.when(k == last)
        def _():
            t_assemble(i, t_parts, t_sc)
            xh, r = _rms_stats(h_ref[...] + FFN_RES * acc_sc[...])
            w = fw_ref[...]
            head_rows = lp - loss[1].shape[0]
            row = lax.broadcasted_iota(jnp.int32, (tm, d), 0) + i * tm
            err = jnp.where(row < head_rows, 0.0, xh * w - t_sc[...])
            loss_ref[...] += 0.5 * jnp.sum(err * err) / d
            dout = err * (1.0 / d)
            dfw_ref[...] += jnp.sum(dout * xh, axis=0, keepdims=True)
            ho_ref[...] = _rms_bwd(dout, xh, r, w)

    body, h_specs, h_args, h_scratch = _h_source(body, h, tm, d)
    vec = pl.BlockSpec((1, d), lambda i, k: (0, 0))
    w_fd = pl.BlockSpec((FFN_CPS, f, d), lambda i, k: (k, 0, 0))
    hid = pl.BlockSpec((FFN_CPS, tm, f), lambda i, k: (k, i, 0))
    hshape = jax.ShapeDtypeStruct((nck, lp, f), BF16)
    args, in_specs = (*h_args, nw, wg, wu, wd), h_specs + [vec, w_fd, w_fd, w_fd]
    out_specs = (pl.BlockSpec((tm, d), lambda i, k: (i, 0)), hid, hid)
    out_shape = (jax.ShapeDtypeStruct((lp, d), F32), hshape, hshape)
    scratch = [pltpu.VMEM((tm, d), BF16), pltpu.VMEM((tm, d), F32)]
    if loss is not None:
        args, in_specs = (*args, loss[0], *t_args), in_specs + [vec] + t_specs
        out_specs += (pl.BlockSpec((8, LANE), lambda i, k: (0, 0)), vec)
        out_shape += (jax.ShapeDtypeStruct((8, LANE), F32), jax.ShapeDtypeStruct((1, d), F32))
        scratch = scratch + [pltpu.VMEM((tm, d), F32)]
    return _pcall(
        body, name=name, grid=(lp // tm, nck // FFN_CPS), plan=plan, plan_args=plan_args,
        args=args, in_specs=in_specs, out_specs=out_specs, out_shape=out_shape,
        scratch_shapes=scratch + h_scratch)


def _ffn_bwd_act(name, dh, h, nw, g, u, wg, wu, wd, plan=None, plan_args=()):
    lp, d = dh.shape
    nck, f, _ = wg.shape
    tm = _tile(lp, FFN_BWD_ROWS, SUB_ROWS)
    last = nck // FFN_CPS - 1

    def body(h_ref, dh_ref, nw_ref, g_ref, u_ref, wg_ref, wu_ref, wd_ref,
             dhi_ref, dnw_ref, n_ref, dacc_ref, a_ref, dg_ref, du_ref,
             xh_sc, r_sc, dn_sc):
        i = pl.program_id(0)
        k = pl.program_id(1)

        @pl.when(k == 0)
        def _():
            xh, r = _rms_stats(h_ref[...])
            xh_sc[...] = xh
            r_sc[...] = r
            n_ref[...] = (xh * nw_ref[...]).astype(BF16)
            dacc_ref[...] = (FFN_RES * dh_ref[...]).astype(BF16)
            dn_sc[...] = jnp.zeros_like(dn_sc)

        @pl.when(jnp.logical_and(i == 0, k == 0))
        def _():
            dnw_ref[...] = jnp.zeros_like(dnw_ref)

        dacc = dacc_ref[...]
        dn = dn_sc[...]
        for c in range(FFN_CPS):
            gv = g_ref[c].astype(F32)
            uv = u_ref[c].astype(F32)
            sg = _sigmoid(gv)
            sil = gv * sg
            da = _dot_nt(dacc, wd_ref[c])
            dgk = (da * uv * (sg * (1.0 + gv * (1.0 - sg)))).astype(BF16)
            duk = (da * sil).astype(BF16)
            a_ref[c] = (sil * uv).astype(BF16)
            dg_ref[c] = dgk
            du_ref[c] = duk
            dn = dn + _dot(dgk, wg_ref[c]) + _dot(duk, wu_ref[c])
        dn_sc[...] = dn

        @pl.when(k == last)
        def _():
            dnl = dn_sc[...]
            xh = xh_sc[...]
            dhi_ref[...] = dh_ref[...] + _rms_bwd(dnl, xh, r_sc[...], nw_ref[...])
            dnw_ref[...] += jnp.sum(dnl * xh, axis=0, keepdims=True)

    body, h_specs, h_args, h_scratch = _h_source(body, h, tm, d)
    row = pl.BlockSpec((tm, d), lambda i, k: (i, 0))
    vec = pl.BlockSpec((1, d), lambda i, k: (0, 0))
    hid = pl.BlockSpec((FFN_CPS, tm, f), lambda i, k: (k, i, 0))
    w_fd = pl.BlockSpec((FFN_CPS, f, d), lambda i, k: (k, 0, 0))
    rshape = jax.ShapeDtypeStruct((lp, d), BF16)
    hshape = jax.ShapeDtypeStruct((nck, lp, f), BF16)
    return _pcall(
        body, name=name, grid=(lp // tm, nck // FFN_CPS), plan=plan, plan_args=plan_args,
        args=(*h_args, dh, nw, g, u, wg, wu, wd),
        in_specs=h_specs + [row, vec, hid, hid, w_fd, w_fd, w_fd],
        out_specs=(row, vec, row, row, hid, hid, hid),
        out_shape=(jax.ShapeDtypeStruct((lp, d), F32), jax.ShapeDtypeStruct((1, d), F32),
                   rshape, rshape, hshape, hshape, hshape),
        scratch_shapes=[pltpu.VMEM((tm, d), F32), pltpu.VMEM((tm, 1), F32), pltpu.VMEM((tm, d), F32)] + h_scratch)


def _ffn_bwd_w(name, n, dacc, a, dg, du, plan=None, plan_args=()):
    lp, d = n.shape
    nck, _, f = a.shape
    tm = _tile(lp, BWD_W_ROWS)
    last = lp // tm - 1

    def body(n_ref, dacc_ref, a_ref, dg_ref, du_ref, dwg_ref, dwu_ref, dwd_ref, ag_sc, au_sc, ad_sc):
        i = pl.program_id(1)

        @pl.when(i == 0)
        def _():
            ag_sc[...] = jnp.zeros_like(ag_sc)
            au_sc[...] = jnp.zeros_like(au_sc)
            ad_sc[...] = jnp.zeros_like(ad_sc)

        nv = n_ref[...]
        ag_sc[...] += _dot_tn(dg_ref[0], nv)
        au_sc[...] += _dot_tn(du_ref[0], nv)
        ad_sc[...] += _dot_tn(a_ref[0], dacc_ref[...])

        @pl.when(i == last)
        def _():
            dwg_ref[0] = ag_sc[...].astype(BF16)
            dwu_ref[0] = au_sc[...].astype(BF16)
            dwd_ref[0] = ad_sc[...].astype(BF16)

    row = pl.BlockSpec((tm, d), lambda k, i: (i, 0))
    hid = pl.BlockSpec((1, tm, f), lambda k, i: (k, i, 0))
    w_fd = pl.BlockSpec((1, f, d), lambda k, i: (k, 0, 0))
    wshape = jax.ShapeDtypeStruct((nck, f, d), BF16)
    return _pcall(
        body, name=name, grid=(nck, lp // tm), plan=plan, plan_args=plan_args, args=(n, dacc, a, dg, du),
        in_specs=[row, row, hid, hid, hid], out_specs=(w_fd, w_fd, w_fd), out_shape=(wshape,) * 3,
        scratch_shapes=[pltpu.VMEM((f, d), F32)] * 3)


def _ffn_bwd_w_scatter(name, n, dacc, a, dg, du, chip, plan, plan_args):
    lp, d = n.shape
    nck, _, f = a.shape
    tm = _tile(lp, BWD_W_ROWS)
    last_i = lp // tm - 1
    n_w = 3
    n_p = plan.n

    def body(me_ref, n_ref, dacc_ref, a_ref, dg_ref, du_ref, *rest):
        p_ins = rest[:n_p]
        recv = rest[n_p:n_p + n_w]
        p_outs = rest[n_p + n_w:2 * n_p + n_w]
        acc = rest[2 * n_p + n_w:2 * n_p + 2 * n_w]
        stage, send_sems, recv_sems, loc_sems = rest[2 * n_p + 2 * n_w:2 * n_p + 2 * n_w + 4]
        p_sems = rest[2 * n_p + 2 * n_w + 4:]
        p = pl.program_id(0)
        i = pl.program_id(1)
        me = me_ref[0]
        c = lax.axis_index("c")

        def send(w, pos):
            kk = jnp.bitwise_xor(me, nck - 1 - pos)
            diff = jnp.bitwise_xor(kk, me)
            m = jnp.where(diff == 2, 0, jnp.where(diff == 1, 1, 2))
            return pltpu.make_async_remote_copy(
                src_ref=stage.at[lax.rem(pos, 2), w], dst_ref=recv[w].at[me],
                send_sem=send_sems.at[w * 3 + m], recv_sem=recv_sems.at[w * 3 + m],
                device_id=(lax.div(kk, 2), lax.rem(kk, 2), c), device_id_type=MESH)

        @pl.when(jnp.logical_and(p == 0, i == 0))
        def _():
            for cp in plan.copies(p_ins, p_outs, p_sems):
                cp.start()

        @pl.when(i == 0)
        def _():
            for t in acc:
                t[...] = jnp.zeros_like(t)

        nv = n_ref[...]
        acc[0][...] += _dot_tn(dg_ref[0], nv)
        acc[1][...] += _dot_tn(du_ref[0], nv)
        acc[2][...] += _dot_tn(a_ref[0], dacc_ref[...])

        @pl.when(jnp.logical_and(i == last_i, p >= 2))
        def _():
            for w in range(n_w):
                send(w, p - 2).wait_send()

        @pl.when(i == last_i)
        def _():
            for w in range(n_w):
                stage[lax.rem(p, 2), w] = acc[w][...].astype(BF16)

        @pl.when(jnp.logical_and(i == last_i, p < nck - 1))
        def _():
            for w in range(n_w):
                send(w, p).start()

        @pl.when(jnp.logical_and(i == last_i, p == nck - 1))
        def _():
            own = [pltpu.make_async_copy(stage.at[(nck - 1) % 2, w], recv[w].at[me], loc_sems.at[w])
                   for w in range(n_w)]
            for cp in own:
                cp.start()
            for w in range(n_w):
                send(w, nck - 2).wait_send()
            for cp in own:
                cp.wait()
            for w in range(n_w):
                for m in range(3):
                    pltpu.make_async_remote_copy(
                        src_ref=stage.at[0, w], dst_ref=recv[w].at[me],
                        send_sem=send_sems.at[w * 3 + m], recv_sem=recv_sems.at[w * 3 + m],
                        device_id=(0, 0, c), device_id_type=MESH).wait_recv()
            for cp in plan.copies(p_ins, p_outs, p_sems):
                cp.wait()

    chunk = lambda k, me_ref: jnp.bitwise_xor(me_ref[0], nck - 1 - k)
    row = pl.BlockSpec((tm, d), lambda k, i, me_ref: (i, 0))
    hid = pl.BlockSpec((1, tm, f), lambda k, i, me_ref: (chunk(k, me_ref), i, 0))
    wshape = jax.ShapeDtypeStruct((nck, f, d), BF16)
    res = pl.pallas_call(
        body, name=name,
        grid_spec=pltpu.PrefetchScalarGridSpec(
            num_scalar_prefetch=1, grid=(nck, lp // tm),
            in_specs=[row, row, hid, hid, hid] + [ANY_SPEC] * n_p,
            out_specs=(ANY_SPEC,) * (n_w + n_p),
            scratch_shapes=[pltpu.VMEM((f, d), F32)] * n_w + [
                pltpu.VMEM((2, n_w, f, d), BF16), pltpu.SemaphoreType.DMA((n_w * 3,)),
                pltpu.SemaphoreType.DMA((n_w * 3,)), pltpu.SemaphoreType.DMA((n_w,))] + plan.scratch()),
        out_shape=(wshape,) * n_w + plan.out_shape(),
        compiler_params=_params(("arbitrary", "arbitrary")),
    )(chip.reshape(1).astype(jnp.int32), n, dacc, a, dg, du, *plan_args)
    return list(res[:n_w]), list(res[n_w:])


def _inproj_fwd(h, nw, w_in, cosf, sinf, rw):
    lp, d = h.shape
    nck, _, ps = w_in.shape
    proj = nck * ps
    sw = proj - 4 * rw
    tm = _tile(lp, 640)
    scale = HEAD_DIM ** -0.5
    heads = rw // HEAD_DIM

    def body(h_ref, nw_ref, w_ref, cos_ref, sin_ref, n_ref, q_ref, k_ref, v_ref, g_ref, u_ref, p_sc):
        xh, _ = _rms_stats(h_ref[...])
        n = (xh * nw_ref[...]).astype(BF16)
        n_ref[...] = n
        for c in range(nck):
            p_sc[:, c * ps:(c + 1) * ps] = _dot(n, w_ref[c])
        cs = cos_ref[...]
        sn = sin_ref[...]
        for hh in range(heads):
            lo = hh * HEAD_DIM
            qh = p_sc[:, lo:lo + HEAD_DIM]
            q_ref[:, lo:lo + HEAD_DIM] = (qh * cs + pltpu.roll(qh, HEAD_DIM // 2, 1) * sn).astype(BF16)
            kh = p_sc[:, rw + lo:rw + lo + HEAD_DIM]
            k_ref[:, lo:lo + HEAD_DIM] = ((kh * cs + pltpu.roll(kh, HEAD_DIM // 2, 1) * sn) * scale).astype(BF16)
        v_ref[...] = p_sc[:, 2 * rw:3 * rw].astype(BF16)
        g_ref[...] = p_sc[:, 3 * rw:4 * rw]
        u_ref[...] = p_sc[:, 4 * rw:]

    row = lambda w: pl.BlockSpec((tm, w), lambda i: (i, 0))
    return pl.pallas_call(
        body, name="inproj_fwd", grid=(lp // tm,),
        in_specs=[row(d), pl.BlockSpec((1, d), lambda i: (0, 0)),
                  pl.BlockSpec((nck, d, ps), lambda i: (0, 0, 0)), row(HEAD_DIM), row(HEAD_DIM)],
        out_specs=(row(d), row(rw), row(rw), row(rw), row(rw), row(sw)),
        out_shape=(jax.ShapeDtypeStruct((lp, d), BF16),
                   jax.ShapeDtypeStruct((lp, rw), BF16),
                   jax.ShapeDtypeStruct((lp, rw), BF16),
                   jax.ShapeDtypeStruct((lp, rw), BF16),
                   jax.ShapeDtypeStruct((lp, rw), F32),
                   jax.ShapeDtypeStruct((lp, sw), F32)),
        scratch_shapes=[pltpu.VMEM((tm, proj), F32)],
        compiler_params=_params(("arbitrary",)),
    )(h, nw, w_in, cosf, sinf)


def _inproj_bwd(dh, h, nw, n, w_in, dq, dk, dv, dg, du):
    lp, d = h.shape
    nck, _, ps = w_in.shape
    rw = dq.shape[1]
    sw = du.shape[1]
    proj = nck * ps
    tm = _tile(lp, 640)
    last = lp // tm - 1

    def gather_dproj(p_sc, dq_ref, dk_ref, dv_ref, dg_ref, du_ref):
        p_sc[:, 0:rw] = dq_ref[...]
        p_sc[:, rw:2 * rw] = dk_ref[...]
        p_sc[:, 2 * rw:3 * rw] = dv_ref[...]
        p_sc[:, 3 * rw:4 * rw] = dg_ref[...]
        p_sc[:, 4 * rw:] = du_ref[...]

    def act_body(dh_ref, h_ref, nw_ref, w_ref, dq_ref, dk_ref, dv_ref, dg_ref, du_ref, dhi_ref, dnw_ref, p_sc):
        i = pl.program_id(0)

        @pl.when(i == 0)
        def _():
            dnw_ref[...] = jnp.zeros_like(dnw_ref)

        gather_dproj(p_sc, dq_ref, dk_ref, dv_ref, dg_ref, du_ref)
        dn = jnp.zeros((tm, d), F32)
        for c in range(nck):
            dn = dn + _dot_nt(p_sc[:, c * ps:(c + 1) * ps], w_ref[c])
        xh, r = _rms_stats(h_ref[...])
        dhi_ref[...] = dh_ref[...] + _rms_bwd(dn, xh, r, nw_ref[...])
        dnw_ref[...] += jnp.sum(dn * xh, axis=0, keepdims=True)

    def w_body(n_ref, dq_ref, dk_ref, dv_ref, dg_ref, du_ref, dw_ref, p_sc, acc_sc):
        i = pl.program_id(0)

        @pl.when(i == 0)
        def _():
            acc_sc[...] = jnp.zeros_like(acc_sc)

        gather_dproj(p_sc, dq_ref, dk_ref, dv_ref, dg_ref, du_ref)
        nv = n_ref[...]
        for c in range(nck):
            acc_sc[c] += _dot_tn(nv, p_sc[:, c * ps:(c + 1) * ps])

        @pl.when(i == last)
        def _():
            dw_ref[...] = acc_sc[...].astype(BF16)

    row = lambda w: pl.BlockSpec((tm, w), lambda i: (i, 0))
    vec = pl.BlockSpec((1, d), lambda i: (0, 0))
    wsp = pl.BlockSpec((nck, d, ps), lambda i: (0, 0, 0))
    dproj_specs = [row(rw), row(rw), row(rw), row(rw), row(sw)]
    dhi, dnw = pl.pallas_call(
        act_body, name="inproj_bwd_act", grid=(lp // tm,),
        in_specs=[row(d), row(d), vec, wsp] + dproj_specs,
        out_specs=(row(d), vec),
        out_shape=(jax.ShapeDtypeStruct((lp, d), F32), jax.ShapeDtypeStruct((1, d), F32)),
        scratch_shapes=[pltpu.VMEM((tm, proj), BF16)],
        compiler_params=_params(("arbitrary",)),
    )(dh, h, nw, w_in, dq, dk, dv, dg, du)
    dw = pl.pallas_call(
        w_body, name="inproj_bwd_w", grid=(lp // tm,),
        in_specs=[row(d)] + dproj_specs,
        out_specs=wsp, out_shape=jax.ShapeDtypeStruct((nck, d, ps), BF16),
        scratch_shapes=[pltpu.VMEM((tm, proj), BF16), pltpu.VMEM((nck, d, ps), F32)],
        compiler_params=_params(("arbitrary",)),
    )(n, dq, dk, dv, dg, du)
    return dhi, dnw, dw


def _retention_tables(rc):
    h = jnp.arange(RET_HEADS, dtype=F32)
    log_g = jnp.log(1.0 - 2.0 ** (-5.0 - h))
    i = jnp.arange(rc)
    diff = i[:, None] - i[None, :]
    dec = jnp.where(diff[None] >= 0,
                    jnp.exp(log_g[:, None, None] * jnp.maximum(diff, 0)[None].astype(F32)), 0.0)
    pos = jnp.arange(rc, dtype=F32)
    wq = jnp.exp(log_g[:, None] * (pos + 1.0)[None])
    wk = jnp.exp(log_g[:, None] * (rc - 1 - pos)[None])
    gch = jnp.exp(log_g * rc)
    ones = jnp.ones((1, 1, HEAD_DIM), F32)
    return (dec, wq[:, :, None] * ones, wk[:, :, None] * ones,
            gch[:, None, None] * jnp.ones((1, 8, HEAD_DIM), F32))


def _head_norm(o):
    mu = jnp.mean(o, axis=-1, keepdims=True)
    oc = o - mu
    r = lax.rsqrt(jnp.mean(oc * oc, axis=-1, keepdims=True) + EPS)
    return oc * r, r


def _ret_fwd(q, k, v, g, rnw, tables):
    lp, rw = q.shape
    heads = rw // HEAD_DIM
    rc = tables[0].shape[1]
    nch = lp // rc
    dec, wq, wk, gch = tables

    def body(q_ref, k_ref, v_ref, g_ref, w_ref, dec_ref, wq_ref, wk_ref, gch_ref,
             o_ref, ret_ref, sp_ref, s_sc):
        n = pl.program_id(0)

        @pl.when(n == 0)
        def _():
            s_sc[...] = jnp.zeros_like(s_sc)

        cols = [slice(hh * HEAD_DIM, (hh + 1) * HEAD_DIM) for hh in range(heads)]
        s_ins = [s_sc[hh] for hh in range(heads)]
        outs = []
        for hh, cs in enumerate(cols):
            qv, kv, vv = q_ref[:, cs], k_ref[:, cs], v_ref[:, cs]
            s_in = s_ins[hh]
            a = _dot_nt(qv, kv) * dec_ref[hh]
            qw = (qv.astype(F32) * wq_ref[hh]).astype(BF16)
            kw = (kv.astype(F32) * wk_ref[hh]).astype(BF16)
            o = _dot(a.astype(BF16), vv) + _dot(qw, s_in.astype(BF16))
            s_new = gch_ref[hh, 0:1, :] * s_in + _dot_tn(kw, vv)
            xh, _ = _head_norm(o)
            gv = g_ref[:, cs]
            outs.append((o, s_new, (gv * _sigmoid(gv) * (xh * w_ref[:, cs])).astype(BF16)))
        for hh, cs in enumerate(cols):
            o, s_new, ret = outs[hh]
            sp_ref[hh, 0] = s_ins[hh]
            s_sc[hh] = s_new
            o_ref[:, cs] = o
            ret_ref[:, cs] = ret

    blk = pl.BlockSpec((rc, rw), lambda n: (n, 0))
    tab = pl.BlockSpec((heads, rc, HEAD_DIM), lambda n: (0, 0, 0))
    dtab = pl.BlockSpec((heads, rc, rc), lambda n: (0, 0, 0))
    return pl.pallas_call(
        body, name="retention_fwd", grid=(nch,),
        in_specs=[blk, blk, blk, blk, pl.BlockSpec((1, rw), lambda n: (0, 0)),
                  dtab, tab, tab, pl.BlockSpec((heads, 8, HEAD_DIM), lambda n: (0, 0, 0))],
        out_specs=(blk, blk, pl.BlockSpec((heads, 1, HEAD_DIM, HEAD_DIM), lambda n: (0, n, 0, 0))),
        out_shape=(jax.ShapeDtypeStruct((lp, rw), F32),
                   jax.ShapeDtypeStruct((lp, rw), BF16),
                   jax.ShapeDtypeStruct((heads, nch, HEAD_DIM, HEAD_DIM), F32)),
        scratch_shapes=[pltpu.VMEM((heads, HEAD_DIM, HEAD_DIM), F32)],
        compiler_params=_params(("arbitrary",)),
    )(q, k, v, g, rnw, dec, wq, wk, gch)


def _ret_bwd(dret, q, k, v, g, o, sprev, rnw, tables, cosf, sinf):
    lp, rw = q.shape
    heads = rw // HEAD_DIM
    rc = tables[0].shape[1]
    nch = lp // rc
    dec, wq, wk, gch = tables
    scale = HEAD_DIM ** -0.5
    half = HEAD_DIM // 2

    def body(dret_ref, q_ref, k_ref, v_ref, g_ref, o_ref, sp_ref, w_ref, dec_ref, wq_ref, wk_ref, gch_ref,
             cos_ref, sin_ref, dq_ref, dk_ref, dv_ref, dg_ref, dw_ref, ds_sc):
        n = pl.program_id(0)

        @pl.when(n == 0)
        def _():
            ds_sc[...] = jnp.zeros_like(ds_sc)
            dw_ref[...] = jnp.zeros_like(dw_ref)

        cosv = cos_ref[...]
        sinv = sin_ref[...]
        cols = [slice(hh * HEAD_DIM, (hh + 1) * HEAD_DIM) for hh in range(heads)]
        ds_ins = [ds_sc[hh] for hh in range(heads)]
        dw_ins = [dw_ref[:, cs] for cs in cols]
        outs = []
        for hh, cs in enumerate(cols):
            qv, kv, vv = q_ref[:, cs], k_ref[:, cs], v_ref[:, cs]
            gv = g_ref[:, cs]
            dr = dret_ref[:, cs]
            w = w_ref[:, cs]
            sg = _sigmoid(gv)
            sil = gv * sg
            xh, r = _head_norm(o_ref[:, cs])
            dgate = (dr * (xh * w) * (sg * (1.0 + gv * (1.0 - sg)))).astype(BF16)
            dyw = dr * sil
            dw_new = dw_ins[hh] + jnp.sum(dyw * xh, axis=0, keepdims=True)
            dxh = dyw * w
            do = r * (dxh - jnp.mean(dxh, axis=-1, keepdims=True)
                      - xh * jnp.mean(dxh * xh, axis=-1, keepdims=True))
            dob = do.astype(BF16)
            dmask = dec_ref[hh]
            wqv = wq_ref[hh]
            wkv = wk_ref[hh]
            a = (_dot_nt(qv, kv) * dmask).astype(BF16)
            da = (_dot_nt(dob, vv) * dmask).astype(BF16)
            qw = (qv.astype(F32) * wqv).astype(BF16)
            kw = (kv.astype(F32) * wkv).astype(BF16)
            s_in = sp_ref[hh, 0].astype(BF16)
            ds = ds_ins[hh]
            dsb = ds.astype(BF16)
            dq = _dot(da, kv) + _dot_nt(dob, s_in) * wqv
            dk = _dot_tn(da, qv) + _dot_nt(vv, dsb) * wkv
            dv = _dot_tn(a, dob) + _dot(kw, dsb)
            ds_new = gch_ref[hh, 0:1, :] * ds + _dot_tn(qw, dob)
            outs.append((dgate, dw_new, ds_new,
                         (dq * cosv + pltpu.roll(dq * sinv, half, 1)).astype(BF16),
                         ((dk * cosv + pltpu.roll(dk * sinv, half, 1)) * scale).astype(BF16),
                         dv.astype(BF16)))
        for hh, cs in enumerate(cols):
            dgate, dw_new, ds_new, dqv, dkv, dvv = outs[hh]
            dg_ref[:, cs] = dgate
            dw_ref[:, cs] = dw_new
            ds_sc[hh] = ds_new
            dq_ref[:, cs] = dqv
            dk_ref[:, cs] = dkv
            dv_ref[:, cs] = dvv

    blk = pl.BlockSpec((rc, rw), lambda n: (nch - 1 - n, 0))
    tab = pl.BlockSpec((heads, rc, HEAD_DIM), lambda n: (0, 0, 0))
    dtab = pl.BlockSpec((heads, rc, rc), lambda n: (0, 0, 0))
    wsp = pl.BlockSpec((1, rw), lambda n: (0, 0))
    pos = pl.BlockSpec((rc, HEAD_DIM), lambda n: (nch - 1 - n, 0))
    bshape = jax.ShapeDtypeStruct((lp, rw), BF16)
    return pl.pallas_call(
        body, name="retention_bwd", grid=(nch,),
        in_specs=[blk, blk, blk, blk, blk, blk,
                  pl.BlockSpec((heads, 1, HEAD_DIM, HEAD_DIM), lambda n: (0, nch - 1 - n, 0, 0)),
                  wsp, dtab, tab, tab, pl.BlockSpec((heads, 8, HEAD_DIM), lambda n: (0, 0, 0)), pos, pos],
        out_specs=(blk, blk, blk, blk, wsp),
        out_shape=(bshape, bshape, bshape, bshape, jax.ShapeDtypeStruct((1, rw), F32)),
        scratch_shapes=[pltpu.VMEM((heads, HEAD_DIM, HEAD_DIM), F32)],
        compiler_params=_params(("arbitrary",)),
    )(dret, q, k, v, g, o, sprev, rnw, dec, wq, wk, gch, cosf, sinf)


SCAN_CW = 512


def _s5_prepare(lam_re, lam_im, log_dt, b_re, b_im):
    dt = jnp.exp(log_dt)[:, None]
    er = jnp.exp(lam_re * dt)
    ar = er * jnp.cos(lam_im * dt)
    ai = er * jnp.sin(lam_im * dt)
    den = lam_re * lam_re + lam_im * lam_im
    fr = ((ar - 1.0) * lam_re + ai * lam_im) / den
    fi = (ai * lam_re - (ar - 1.0) * lam_im) / den
    bbr = fr[..., None] * b_re - fi[..., None] * b_im
    bbi = fr[..., None] * b_im + fi[..., None] * b_re
    return ar, ai, bbr, bbi


def _blockdiag_in(t):
    g, p, n = t.shape
    gs = g // N_SEC
    t = t.reshape(N_SEC, gs, p, n)
    eye = jnp.eye(gs, dtype=t.dtype)
    return jnp.einsum("sgpn,gh->sgphn", t, eye).reshape(N_SEC, gs * p, gs * n)


def _blockdiag_out(m, g, p, n):
    gs = g // N_SEC
    m = m.reshape(N_SEC, gs, p, gs, n)
    eye = jnp.eye(gs, dtype=m.dtype)
    return jnp.einsum("sgphn,gh->sgpn", m, eye).reshape(g, p, n)


def _scan_step(xr_ref, xi_ref, r0, prev, ar_ref, ai_ref, conj, ncols):
    new = []
    for cc in range(ncols // SCAN_CW):
        cs = pl.ds(cc * SCAN_CW, SCAN_CW)
        pr, pi = prev[cc]
        ar = ar_ref[:, cs]
        ai = ai_ref[:, cs]
        if conj:
            nr = ar * pr + ai * pi
            ni = ar * pi - ai * pr
        else:
            nr = ar * pr - ai * pi
            ni = ar * pi + ai * pr
        xr = xr_ref[pl.ds(r0, 8), cs] + nr
        xi = xi_ref[pl.ds(r0, 8), cs] + ni
        xr_ref[pl.ds(r0, 8), cs] = xr
        xi_ref[pl.ds(r0, 8), cs] = xi
        new.append((xr, xi))
    return new


def _scan_chunks(ncols):
    return [pl.ds(cc * SCAN_CW, SCAN_CW) for cc in range(ncols // SCAN_CW)]


def _flat(pairs):
    return tuple(t for p in pairs for t in p)


def _pairs(flat):
    return [(flat[2 * k], flat[2 * k + 1]) for k in range(len(flat) // 2)]


def _shift_rows(z, down):
    row = lax.broadcasted_iota(jnp.int32, z.shape, 0)
    if down:
        return jnp.where(row == 0, 0.0, pltpu.roll(z, 1, 0))
    return jnp.where(row == N_SEG - 1, 0.0, pltpu.roll(z, N_SEG - 1, 0))


def _s5_fwd(u, bsr, bsi, csr, csi, a8r, a8i, al8r, al8i, d, gluw, glub, nw, jb):
    lp, sw = u.shape
    ns = a8r.shape[1]
    rows = N_SEG * jb
    nblk = lp // rows
    secw = sw // N_SEC
    secn = ns // N_SEC

    def local_scan(u_ref, bsr_ref, bsi_ref, ar_ref, ai_ref, xr_ref, xi_ref, pr_sc, pi_sc):
        for s in range(N_SEC):
            ub = u_ref[:, s * secw:(s + 1) * secw].astype(BF16)
            xr_ref[:, s * secn:(s + 1) * secn] = _dot(ub, bsr_ref[s])
            xi_ref[:, s * secn:(s + 1) * secn] = _dot(ub, bsi_ref[s])
        prev = [(pr_sc[:, cs], pi_sc[:, cs]) for cs in _scan_chunks(ns)]
        prev = _scan_step(xr_ref, xi_ref, 0, prev, ar_ref, ai_ref, False, ns)

        def step(j, carry):
            r0 = pl.multiple_of(j * 8, 8)
            return _flat(_scan_step(xr_ref, xi_ref, r0, _pairs(carry), ar_ref, ai_ref, False, ns))

        last = _pairs(lax.fori_loop(1, jb, step, _flat(prev)))
        for cs, (vr, vi) in zip(_scan_chunks(ns), last):
            pr_sc[:, cs] = vr
            pi_sc[:, cs] = vi

    def carry_body(u_ref, bsr_ref, bsi_ref, ar_ref, ai_ref, alr_ref, ali_ref, cr_ref, ci_ref,
                   xr_sc, xi_sc, pr_sc, pi_sc):
        b = pl.program_id(0)

        @pl.when(b == 0)
        def _():
            pr_sc[...] = jnp.zeros_like(pr_sc)
            pi_sc[...] = jnp.zeros_like(pi_sc)

        local_scan(u_ref, bsr_ref, bsi_ref, ar_ref, ai_ref, xr_sc, xi_sc, pr_sc, pi_sc)

        @pl.when(b == nblk - 1)
        def _():
            er = _shift_rows(pr_sc[...], True)
            ei = _shift_rows(pi_sc[...], True)
            alr, ali = alr_ref[...], ali_ref[...]
            cr, ci = er, ei
            for _ in range(N_SEG - 2):
                sr = _shift_rows(cr, True)
                si = _shift_rows(ci, True)
                cr = er + alr * sr - ali * si
                ci = ei + alr * si + ali * sr
            cr_ref[...] = cr
            ci_ref[...] = ci

    ublk = pl.BlockSpec((rows, sw), lambda b: (b, 0))
    bspec = pl.BlockSpec((N_SEC, secw, secn), lambda b: (0, 0, 0))
    cspec = pl.BlockSpec((N_SEC, secn, secw), lambda b: (0, 0, 0))
    s8 = pl.BlockSpec((N_SEG, ns), lambda b: (0, 0))
    vec = pl.BlockSpec((1, sw), lambda b: (0, 0))
    s8shape = jax.ShapeDtypeStruct((N_SEG, ns), F32)
    c0r, c0i = pl.pallas_call(
        carry_body, name="s5_fwd_carry", grid=(nblk,),
        in_specs=[ublk, bspec, bspec, s8, s8, s8, s8],
        out_specs=(s8, s8), out_shape=(s8shape, s8shape),
        scratch_shapes=[pltpu.VMEM((rows, ns), F32), pltpu.VMEM((rows, ns), F32),
                        pltpu.VMEM((N_SEG, ns), F32), pltpu.VMEM((N_SEG, ns), F32)],
        compiler_params=_params(("arbitrary",)),
    )(u, bsr, bsi, a8r, a8i, al8r, al8i)

    def main_body(u_ref, bsr_ref, bsi_ref, csr_ref, csi_ref, ar_ref, ai_ref, c0r_ref, c0i_ref,
                  d_ref, gw_ref, gb_ref, nw_ref, xr_ref, xi_ref, yp_ref, out_ref, pr_sc, pi_sc):
        b = pl.program_id(0)

        @pl.when(b == 0)
        def _():
            pr_sc[...] = c0r_ref[...]
            pi_sc[...] = c0i_ref[...]

        local_scan(u_ref, bsr_ref, bsi_ref, ar_ref, ai_ref, xr_ref, xi_ref, pr_sc, pi_sc)
        for s in range(N_SEC):
            xs = pl.ds(s * secn, secn)
            us = pl.ds(s * secw, secw)
            y = _dot(xr_ref[:, xs].astype(BF16), csr_ref[s]) + _dot(xi_ref[:, xs].astype(BF16), csi_ref[s])
            yp_ref[:, us] = y + d_ref[:, us] * u_ref[:, us]
        yp = yp_ref[...]
        t = jnp.tanh(GELU_K0 * (yp + GELU_K1 * yp * yp * yp))
        y1 = 0.5 * yp * (1.0 + t)
        z = _dot(y1.astype(BF16), gw_ref[...]) + gb_ref[...]
        y2 = y1 * _sigmoid(z)
        xh, _ = _rms_stats(y2)
        out_ref[...] = (xh * nw_ref[...]).astype(BF16)

    xblk = pl.BlockSpec((rows, ns), lambda b: (b, 0))
    xr, xi, yp, out = pl.pallas_call(
        main_body, name="s5_fwd", grid=(nblk,),
        in_specs=[ublk, bspec, bspec, cspec, cspec, s8, s8, s8, s8, vec,
                  pl.BlockSpec((sw, sw), lambda b: (0, 0)), vec, vec],
        out_specs=(xblk, xblk, ublk, ublk),
        out_shape=(jax.ShapeDtypeStruct((lp, ns), F32), jax.ShapeDtypeStruct((lp, ns), F32),
                   jax.ShapeDtypeStruct((lp, sw), F32), jax.ShapeDtypeStruct((lp, sw), BF16)),
        scratch_shapes=[pltpu.VMEM((N_SEG, ns), F32), pltpu.VMEM((N_SEG, ns), F32)],
        compiler_params=_params(("arbitrary",)),
    )(u, bsr, bsi, csr, csi, a8r, a8i, c0r, c0i, d, gluw, glub, nw)
    return xr, xi, c0r, c0i, yp, out


def _s5_bwd(dout, u, yp, xr, xi, c0r, c0i, bsrt, bsit, csrt, csit, a8r, a8i, al8r, al8i, d, gluw, glub, nw, jb):
    lp, sw = u.shape
    ns = a8r.shape[1]
    rows = N_SEG * jb
    nblk = lp // rows
    secw = sw // N_SEC
    secn = ns // N_SEC

    def rowwise_bwd(dout_ref, yp_ref, gw_ref, gb_ref, nw_ref):
        ypv = yp_ref[...]
        t = jnp.tanh(GELU_K0 * (ypv + GELU_K1 * ypv * ypv * ypv))
        y1 = 0.5 * ypv * (1.0 + t)
        dgelu = 0.5 * (1.0 + t) + 0.5 * ypv * (1.0 - t * t) * GELU_K0 * (1.0 + 3.0 * GELU_K1 * ypv * ypv)
        gw = gw_ref[...]
        y1b = y1.astype(BF16)
        sg = _sigmoid(_dot(y1b, gw) + gb_ref[...])
        xh, r = _rms_stats(y1 * sg)
        dov = dout_ref[...]
        dy2 = _rms_bwd(dov, xh, r, nw_ref[...])
        dz = dy2 * y1 * sg * (1.0 - sg)
        dzb = dz.astype(BF16)
        dy1 = dy2 * sg + _dot_nt(dzb, gw)
        return dy1 * dgelu, dov * xh, y1b, dzb, dz

    def lam_scan(dyp_of, csrt_ref, csit_ref, ar_ref, ai_ref, lr_sc, li_sc, nr_sc, ni_sc, extra):
        for s in range(N_SEC):
            db = dyp_of(s)
            lr_sc[:, s * secn:(s + 1) * secn] = _dot(db, csrt_ref[s])
            li_sc[:, s * secn:(s + 1) * secn] = _dot(db, csit_ref[s])
        top = rows - 8
        prev = [(nr_sc[:, cs], ni_sc[:, cs]) for cs in _scan_chunks(ns)]
        prev = _scan_step(lr_sc, li_sc, top, prev, ar_ref, ai_ref, True, ns)
        extra(top, pl.ds(top - 8, 8))

        def step(jj, carry):
            r0 = pl.multiple_of((jb - 1 - jj) * 8, 8)
            rp = pl.multiple_of((jb - 2 - jj) * 8, 8)
            new = _scan_step(lr_sc, li_sc, r0, _pairs(carry), ar_ref, ai_ref, True, ns)
            extra(r0, pl.ds(rp, 8))
            return _flat(new)

        prev = _pairs(lax.fori_loop(1, jb - 1, step, _flat(prev)))
        last = _scan_step(lr_sc, li_sc, 0, prev, ar_ref, ai_ref, True, ns)
        extra(0, None)
        for cs, (vr, vi) in zip(_scan_chunks(ns), last):
            nr_sc[:, cs] = vr
            ni_sc[:, cs] = vi

    def carry_body(dout_ref, yp_ref, u_ref, gw_ref, gb_ref, nw_ref, csrt_ref, csit_ref, ar_ref, ai_ref,
                   alr_ref, ali_ref, cr_ref, ci_ref, dyp_ref, dnw_ref, dgw_ref, dgb_ref, dd_ref,
                   lr_sc, li_sc, nr_sc, ni_sc):
        b = pl.program_id(0)

        @pl.when(b == 0)
        def _():
            nr_sc[...] = jnp.zeros_like(nr_sc)
            ni_sc[...] = jnp.zeros_like(ni_sc)
            for ref in (dnw_ref, dgw_ref, dgb_ref, dd_ref):
                ref[...] = jnp.zeros_like(ref)

        dyp, dnw_rows, y1b, dzb, dz = rowwise_bwd(dout_ref, yp_ref, gw_ref, gb_ref, nw_ref)
        dnw_ref[...] += jnp.sum(dnw_rows, axis=0, keepdims=True)
        dgw_ref[...] += _dot_tn(y1b, dzb)
        dgb_ref[...] += jnp.sum(dz, axis=0, keepdims=True)
        dd_ref[...] += jnp.sum(dyp * u_ref[...], axis=0, keepdims=True)
        dyp_ref[...] = dyp.astype(BF16)
        lam_scan(lambda s: dyp_ref[:, s * secw:(s + 1) * secw], csrt_ref, csit_ref, ar_ref, ai_ref,
                 lr_sc, li_sc, nr_sc, ni_sc, lambda r0, prev_rows: None)

        @pl.when(b == nblk - 1)
        def _():
            fr = _shift_rows(nr_sc[...], False)
            fi = _shift_rows(ni_sc[...], False)
            alr, ali = alr_ref[...], ali_ref[...]
            cr, ci = fr, fi
            for _ in range(N_SEG - 2):
                sr = _shift_rows(cr, False)
                si = _shift_rows(ci, False)
                cr = fr + alr * sr + ali * si
                ci = fi + alr * si - ali * sr
            cr_ref[...] = cr
            ci_ref[...] = ci

    rev = lambda b: (nblk - 1 - b, 0)
    ublk = pl.BlockSpec((rows, sw), rev)
    xblk = pl.BlockSpec((rows, ns), rev)
    s8 = pl.BlockSpec((N_SEG, ns), lambda b: (0, 0))
    vec = pl.BlockSpec((1, sw), lambda b: (0, 0))
    gws = pl.BlockSpec((sw, sw), lambda b: (0, 0))
    btspec = pl.BlockSpec((N_SEC, secn, secw), lambda b: (0, 0, 0))
    ctspec = pl.BlockSpec((N_SEC, secw, secn), lambda b: (0, 0, 0))
    s8shape = jax.ShapeDtypeStruct((N_SEG, ns), F32)
    lcr, lci, dyp_all, d_nw, d_gw, d_gb, d_d = pl.pallas_call(
        carry_body, name="s5_bwd_carry", grid=(nblk,),
        in_specs=[ublk, ublk, ublk, gws, vec, vec, ctspec, ctspec, s8, s8, s8, s8],
        out_specs=(s8, s8, ublk, vec, gws, vec, vec),
        out_shape=(s8shape, s8shape, jax.ShapeDtypeStruct((lp, sw), BF16), jax.ShapeDtypeStruct((1, sw), F32),
                   jax.ShapeDtypeStruct((sw, sw), F32), jax.ShapeDtypeStruct((1, sw), F32),
                   jax.ShapeDtypeStruct((1, sw), F32)),
        scratch_shapes=[pltpu.VMEM((rows, ns), F32), pltpu.VMEM((rows, ns), F32),
                        pltpu.VMEM((N_SEG, ns), F32), pltpu.VMEM((N_SEG, ns), F32)],
        compiler_params=_params(("arbitrary",)),
    )(dout, yp, u, gluw, glub, nw, csrt, csit, a8r, a8i, al8r, al8i)

    def main_body(dyp_sc, u_ref, xr_ref, xi_ref, xtr_ref, xti_ref, c0r_ref, c0i_ref, lcr_ref, lci_ref,
                  d_ref, bsrt_ref, bsit_ref, csrt_ref, csit_ref, ar_ref, ai_ref,
                  du_ref, dcr_ref, dci_ref, dbr_ref, dbi_ref, dar_ref, dai_ref,
                  lr_sc, li_sc, nr_sc, ni_sc):
        b = pl.program_id(0)

        @pl.when(b == 0)
        def _():
            nr_sc[...] = lcr_ref[...]
            ni_sc[...] = lci_ref[...]
            for ref in (dcr_ref, dci_ref, dbr_ref, dbi_ref, dar_ref, dai_ref):
                ref[...] = jnp.zeros_like(ref)

        for s in range(N_SEC):
            db = dyp_sc[:, s * secw:(s + 1) * secw]
            xs = pl.ds(s * secn, secn)
            dcr_ref[s] += _dot_tn(xr_ref[:, xs].astype(BF16), db)
            dci_ref[s] += _dot_tn(xi_ref[:, xs].astype(BF16), db)

        first = b == nblk - 1

        def acc_da(r0, prev_rows):
            for cc in range(ns // SCAN_CW):
                cs = pl.ds(cc * SCAN_CW, SCAN_CW)
                lr = lr_sc[pl.ds(r0, 8), cs]
                li = li_sc[pl.ds(r0, 8), cs]
                if prev_rows is None:
                    xpr = jnp.where(first, c0r_ref[:, cs], xtr_ref[:, cs])
                    xpi = jnp.where(first, c0i_ref[:, cs], xti_ref[:, cs])
                else:
                    xpr = xr_ref[prev_rows, cs]
                    xpi = xi_ref[prev_rows, cs]
                dar_ref[:, cs] += lr * xpr + li * xpi
                dai_ref[:, cs] += li * xpr - lr * xpi

        lam_scan(lambda s: dyp_sc[:, s * secw:(s + 1) * secw], csrt_ref, csit_ref, ar_ref, ai_ref,
                 lr_sc, li_sc, nr_sc, ni_sc, acc_da)

        for s in range(N_SEC):
            xs = pl.ds(s * secn, secn)
            us = pl.ds(s * secw, secw)
            lrb = lr_sc[:, xs].astype(BF16)
            lib = li_sc[:, xs].astype(BF16)
            du = _dot(lrb, bsrt_ref[s]) + _dot(lib, bsit_ref[s]) + d_ref[:, us] * dyp_sc[:, us].astype(F32)
            du_ref[:, us] = du.astype(BF16)
            ub = u_ref[:, us].astype(BF16)
            dbr_ref[s] += _dot_tn(ub, lrb)
            dbi_ref[s] += _dot_tn(ub, lib)

    tail = pl.BlockSpec((N_SEG, ns), lambda b: (jnp.maximum((nblk - 1 - b) * jb - 1, 0), 0))
    acc_c = pl.BlockSpec((N_SEC, secn, secw), lambda b: (0, 0, 0))
    acc_b = pl.BlockSpec((N_SEC, secw, secn), lambda b: (0, 0, 0))
    du, dcr, dci, dbr, dbi, dar, dai = pl.pallas_call(
        main_body, name="s5_bwd", grid=(nblk,),
        in_specs=[ublk, ublk, xblk, xblk, tail, tail, s8, s8, s8, s8,
                  vec, btspec, btspec, ctspec, ctspec, s8, s8],
        out_specs=(ublk, acc_c, acc_c, acc_b, acc_b, s8, s8),
        out_shape=(jax.ShapeDtypeStruct((lp, sw), BF16),
                   jax.ShapeDtypeStruct((N_SEC, secn, secw), F32),
                   jax.ShapeDtypeStruct((N_SEC, secn, secw), F32),
                   jax.ShapeDtypeStruct((N_SEC, secw, secn), F32),
                   jax.ShapeDtypeStruct((N_SEC, secw, secn), F32),
                   s8shape, s8shape),
        scratch_shapes=[pltpu.VMEM((rows, ns), F32), pltpu.VMEM((rows, ns), F32),
                        pltpu.VMEM((N_SEG, ns), F32), pltpu.VMEM((N_SEG, ns), F32)],
        compiler_params=_params(("arbitrary",)),
    )(dyp_all, u, xr, xi, xr, xi, c0r, c0i, lcr, lci, d, bsrt, bsit, csrt, csit, a8r, a8i)
    return du, d_nw, d_gw, d_gb, d_d, dcr, dci, dbr, dbi, dar, dai


def _outproj_fwd(h, ret, ssm, wo):
    lp, d = h.shape
    nck, rs, _ = wo.shape
    rw = ret.shape[1]
    tm = _tile(lp, 640)
    per = rw // rs

    def body(h_ref, ret_ref, ssm_ref, w_ref, o_ref):
        acc = h_ref[...]
        for c in range(nck):
            src = ret_ref if c < per else ssm_ref
            lo = (c % per) * rs
            acc = acc + _dot(src[:, lo:lo + rs], w_ref[c])
        o_ref[...] = acc

    row = lambda w: pl.BlockSpec((tm, w), lambda i: (i, 0))
    return pl.pallas_call(
        body, name="outproj_fwd", grid=(lp // tm,),
        in_specs=[row(d), row(rw), row(ssm.shape[1]), pl.BlockSpec((nck, rs, d), lambda i: (0, 0, 0))],
        out_specs=row(d), out_shape=jax.ShapeDtypeStruct((lp, d), F32),
        compiler_params=_params(("arbitrary",)),
    )(h, ret, ssm, wo)


def _outproj_bwd(dh, ret, ssm, wo):
    lp, d = dh.shape
    nck, rs, _ = wo.shape
    rw = ret.shape[1]
    sw = ssm.shape[1]
    tm = _tile(lp, 640)
    per = rw // rs
    last = lp // tm - 1

    def body(dh_ref, ret_ref, ssm_ref, w_ref, dret_ref, dssm_ref, dw_ref, acc_sc):
        i = pl.program_id(0)

        @pl.when(i == 0)
        def _():
            acc_sc[...] = jnp.zeros_like(acc_sc)

        dhb = dh_ref[...].astype(BF16)
        for c in range(nck):
            src, dst = (ret_ref, dret_ref) if c < per else (ssm_ref, dssm_ref)
            lo = (c % per) * rs
            dst[:, lo:lo + rs] = _dot_nt(dhb, w_ref[c])
            acc_sc[c] += _dot_tn(src[:, lo:lo + rs], dhb)

        @pl.when(i == last)
        def _():
            dw_ref[...] = acc_sc[...].astype(BF16)

    row = lambda w: pl.BlockSpec((tm, w), lambda i: (i, 0))
    wsp = pl.BlockSpec((nck, rs, d), lambda i: (0, 0, 0))
    return pl.pallas_call(
        body, name="outproj_bwd", grid=(lp // tm,),
        in_specs=[row(d), row(rw), row(sw), wsp],
        out_specs=(row(rw), row(sw), wsp),
        out_shape=(jax.ShapeDtypeStruct((lp, rw), F32), jax.ShapeDtypeStruct((lp, sw), F32),
                   jax.ShapeDtypeStruct((nck, rs, d), BF16)),
        scratch_shapes=[pltpu.VMEM((nck, rs, d), F32)],
        compiler_params=_params(("arbitrary",)),
    )(dh, ret, ssm, wo)


def _loss_head(h, fw, target):
    lp, d = h.shape
    tm = _tile(lp, 1664, CHUNK)
    sub = tm // CHUNK

    def body(h_ref, w_ref, *rest):
        t_refs = rest[:sub]
        loss_ref, dh_ref, dw_ref = rest[sub:]
        i = pl.program_id(0)

        @pl.when(i == 0)
        def _():
            loss_ref[...] = jnp.zeros_like(loss_ref)
            dw_ref[...] = jnp.zeros_like(dw_ref)

        w = w_ref[...]
        for j in range(sub):
            rows = pl.ds(j * CHUNK, CHUNK)
            xh, r = _rms_stats(h_ref[rows, :])
            err = xh * w - t_refs[j][...]
            if j == 0:
                err = jnp.where(i == 0, 0.0, err)
            loss_ref[...] += 0.5 * jnp.sum(err * err) / d
            dout = err * (1.0 / d)
            dw_ref[...] += jnp.sum(dout * xh, axis=0, keepdims=True)
            dh_ref[rows, :] = _rms_bwd(dout, xh, r, w)

    t_spec = lambda j: pl.BlockSpec((CHUNK, d), lambda i: (jnp.maximum(i * sub + j - 1, 0), 0))
    return pl.pallas_call(
        body, name="loss_head", grid=(lp // tm,),
        in_specs=[pl.BlockSpec((tm, d), lambda i: (i, 0)), pl.BlockSpec((1, d), lambda i: (0, 0))]
        + [t_spec(j) for j in range(sub)],
        out_specs=(pl.BlockSpec((8, LANE), lambda i: (0, 0)), pl.BlockSpec((tm, d), lambda i: (i, 0)),
                   pl.BlockSpec((1, d), lambda i: (0, 0))),
        out_shape=(jax.ShapeDtypeStruct((8, LANE), F32), jax.ShapeDtypeStruct((lp, d), F32),
                   jax.ShapeDtypeStruct((1, d), F32)),
        compiler_params=_params(("arbitrary",)),
    )(h, fw, *([target] * sub))


def _pack(arrs):
    flat = jnp.concatenate([a.reshape(-1).astype(F32) for a in arrs])
    n = flat.shape[0]
    rows = -(-n // (8 * LANE)) * 8
    return jnp.pad(flat, (0, rows * LANE - n)).reshape(rows, LANE)


def _unpack(packed, shapes):
    flat = packed.reshape(-1)
    out, off = [], 0
    for s in shapes:
        n = math.prod(s)
        out.append(flat[off:off + n].reshape(s))
        off += n
    return out


def _to_segments(a, seg_len):
    return a.reshape(N_SEG, seg_len, a.shape[1]).transpose(1, 0, 2).reshape(a.shape)


def _from_segments(a, seg_len):
    return a.reshape(seg_len, N_SEG, a.shape[1]).transpose(1, 0, 2).reshape(a.shape)


WEIGHT_NAMES = ['meta_tokens', 'ffn1_norm_w', 'ffn1_w_gate', 'ffn1_w_up', 'ffn1_w_down', 'mix_norm_w', 'w_in',
                'ret_norm_w', 'ssm_lambda_re', 'ssm_lambda_im', 'ssm_log_dt', 'ssm_b_re', 'ssm_b_im', 'ssm_c_re',
                'ssm_c_im', 'ssm_d', 'ssm_glu_w', 'ssm_glu_b', 'ssm_norm_w', 'w_out', 'ffn2_norm_w', 'ffn2_w_gate',
                'ffn2_w_up', 'ffn2_w_down', 'final_norm_w']
BIG = ['ffn1_w_gate', 'ffn1_w_up', 'ffn1_w_down', 'w_in', 'ssm_glu_w', 'w_out', 'ffn2_w_gate', 'ffn2_w_up',
       'ffn2_w_down']
TRANSPOSED = ['ffn1_w_gate', 'ffn1_w_up', 'ffn2_w_gate', 'ffn2_w_up']
BIG_EARLY = ['ffn1_w_gate', 'ffn1_w_up', 'ffn1_w_down']
BIG_LATE = [n for n in BIG if n not in BIG_EARLY]
SMALL = [n for n in WEIGHT_NAMES if n not in BIG]


def kernel(x, meta_tokens, ffn1_norm_w, ffn1_w_gate, ffn1_w_up, ffn1_w_down, mix_norm_w, w_in, ret_norm_w, ssm_lambda_re, ssm_lambda_im, ssm_log_dt, ssm_b_re, ssm_b_im, ssm_c_re, ssm_c_im, ssm_d, ssm_glu_w, ssm_glu_b, ssm_norm_w, w_out, ffn2_norm_w, ffn2_w_gate, ffn2_w_up, ffn2_w_down, final_norm_w, loss_target, m_meta_tokens, m_ffn1_norm_w, m_ffn1_w_gate, m_ffn1_w_up, m_ffn1_w_down, m_mix_norm_w, m_w_in, m_ret_norm_w, m_ssm_lambda_re, m_ssm_lambda_im, m_ssm_log_dt, m_ssm_b_re, m_ssm_b_im, m_ssm_c_re, m_ssm_c_im, m_ssm_d, m_ssm_glu_w, m_ssm_glu_b, m_ssm_norm_w, m_w_out, m_ffn2_norm_w, m_ffn2_w_gate, m_ffn2_w_up, m_ffn2_w_down, m_final_norm_w, v_meta_tokens, v_ffn1_norm_w, v_ffn1_w_gate, v_ffn1_w_up, v_ffn1_w_down, v_mix_norm_w, v_w_in, v_ret_norm_w, v_ssm_lambda_re, v_ssm_lambda_im, v_ssm_log_dt, v_ssm_b_re, v_ssm_b_im, v_ssm_c_re, v_ssm_c_im, v_ssm_d, v_ssm_glu_w, v_ssm_glu_b, v_ssm_norm_w, v_w_out, v_ffn2_norm_w, v_ffn2_w_gate, v_ffn2_w_up, v_ffn2_w_down, v_final_norm_w):
    args = locals()
    w = {n: args[n] for n in WEIGHT_NAMES}
    m = {n: args["m_" + n] for n in WEIGHT_NAMES}
    v = {n: args["v_" + n] for n in WEIGHT_NAMES}

    seq, d = x.shape[1], x.shape[2]
    lp = seq + CHUNK
    seg_len = lp // N_SEG
    rw = RET_HEADS * HEAD_DIM
    sw = ssm_d.shape[-1]
    groups = sw // SSM_GROUP
    ns = groups * SSM_STATE
    jb = _tile(seg_len, S5_STEPS, 8)
    chip = 2 * lax.axis_index("x") + lax.axis_index("y")

    as_fd = lambda t: jnp.swapaxes(t, -1, -2)
    shards = {n: (as_fd(w[n][0]) if n in TRANSPOSED else w[n][0]).astype(BF16) for n in BIG}
    early = [shards[n] for n in BIG_EARLY] + [meta_tokens]
    gathered = _gather_two_level("gather_early", early)
    gw = dict(zip(BIG_EARLY, gathered[:-1]))
    meta_full = jnp.transpose(gathered[-1], (1, 0, 2)).reshape(N_META, d)
    late = [shards[n] for n in BIG_LATE]

    pos = jnp.arange(lp, dtype=F32) - float(CHUNK - N_META)
    freqs = 1.0 / (ROPE_BASE ** (jnp.arange(0, HEAD_DIM, 2, dtype=F32) / HEAD_DIM))
    ang = pos[:, None] * freqs[None, :]
    cosf = jnp.concatenate([jnp.cos(ang), jnp.cos(ang)], axis=1)
    sinf = jnp.concatenate([-jnp.sin(ang), jnp.sin(ang)], axis=1)
    tables = _retention_tables(_tile(lp, RET_ROWS, CHUNK))

    lam_re, lam_im, log_dt = ssm_lambda_re[0], ssm_lambda_im[0], ssm_log_dt[0]
    b_re, b_im, c_re, c_im = ssm_b_re[0], ssm_b_im[0], ssm_c_re[0], ssm_c_im[0]
    (ar, ai, bbr, bbi), prep_vjp = jax.vjp(_s5_prepare, lam_re, lam_im, log_dt, b_re, b_im)
    dt = jnp.exp(log_dt)[:, None]
    el = jnp.exp(seg_len * lam_re * dt)
    alr = el * jnp.cos(seg_len * lam_im * dt)
    ali = el * jnp.sin(seg_len * lam_im * dt)
    bc8 = lambda t: jnp.broadcast_to(t.reshape(1, ns), (N_SEG, ns))
    a8r, a8i, al8r, al8i = bc8(ar), bc8(ai), bc8(alr), bc8(ali)
    bsr = _blockdiag_in(jnp.transpose(bbr, (0, 2, 1)))
    bsi = _blockdiag_in(jnp.transpose(bbi, (0, 2, 1)))
    csrt = _blockdiag_in(c_re)
    csit = _blockdiag_in(-c_im)
    tr = lambda t: jnp.transpose(t, (0, 2, 1))
    bsr_b, bsi_b = bsr.astype(BF16), bsi.astype(BF16)
    csr_b, csi_b = tr(csrt).astype(BF16), tr(csit).astype(BF16)
    bsrt_b, bsit_b = tr(bsr).astype(BF16), tr(bsi).astype(BF16)
    csrt_b, csit_b = csrt.astype(BF16), csit.astype(BF16)

    h0 = (jnp.concatenate([jnp.zeros((CHUNK - N_META, d), F32), meta_full], axis=0), x[0])
    (h1, g1, u1), late_half = _ffn_fwd("ffn1_fwd", h0, ffn1_norm_w, gw['ffn1_w_gate'], gw['ffn1_w_up'],
                                       gw['ffn1_w_down'], _allgather_chips_plan(late), late)
    gw.update(zip(BIG_LATE, _forward_sibling("gather_late_forward", late_half)))
    glu_full = gw['ssm_glu_w'].reshape(sw, sw)
    n2, q, k, vv, gate, u = _inproj_fwd(h1, mix_norm_w, gw['w_in'], cosf, sinf, rw)
    o, ret, sprev = _ret_fwd(q, k, vv, gate, ret_norm_w, tables)
    u_seg = _to_segments(u, seg_len)
    xr, xi, c0r, c0i, yp, ssm_seg = _s5_fwd(u_seg, bsr_b, bsi_b, csr_b, csi_b, a8r, a8i, al8r, al8i,
                                            ssm_d, glu_full, ssm_glu_b, ssm_norm_w, jb)
    ssm = _from_segments(ssm_seg, seg_len)
    h2 = _outproj_fwd(h1, ret, ssm, gw['w_out'])
    (dh3, g2, u2, loss_part, d_final), _ = _ffn_fwd(
        "ffn2_fwd_loss", h2, ffn2_norm_w, gw['ffn2_w_gate'], gw['ffn2_w_up'], gw['ffn2_w_down'],
        loss=(final_norm_w.reshape(1, d), loss_target[0]))

    (dh2, d_ffn2_norm, nb, daccb, ab, dgb, dub), _ = _ffn_bwd_act(
        "ffn2_bwd_act", dh3, h2, ffn2_norm_w, g2, u2, gw['ffn2_w_gate'], gw['ffn2_w_up'], gw['ffn2_w_down'])
    (dwg2, dwu2, dwd2), _ = _ffn_bwd_w("ffn2_bwd_w", nb, daccb, ab, dgb, dub)
    dret, dssm, dwo = _outproj_bwd(dh2, ret, ssm, gw['w_out'])
    (du_seg, d_ssm_norm, d_glu_w, d_glu_b, d_ssm_d, dcr_s, dci_s, dbr_s, dbi_s, dar8, dai8) = _s5_bwd(
        _to_segments(dssm, seg_len), u_seg, yp, xr, xi, c0r, c0i, bsrt_b, bsit_b, csrt_b, csit_b,
        a8r, a8i, al8r, al8i, ssm_d, glu_full, ssm_glu_b, ssm_norm_w, jb)
    du = _from_segments(du_seg, seg_len)
    dq, dk, dv, dgate, d_ret_norm = _ret_bwd(dret, q, k, vv, gate, o, sprev, ret_norm_w, tables, cosf, sinf)
    dh1, d_mix_norm, dwin = _inproj_bwd(dh2, h1, mix_norm_w, n2, gw['w_in'], dq, dk, dv, dgate, du)
    late_parts = {
        'w_in': dwin, 'ssm_glu_w': d_glu_w.reshape(N_CHIP, sw // N_CHIP, sw).astype(BF16), 'w_out': dwo,
        'ffn2_w_gate': dwg2, 'ffn2_w_up': dwu2, 'ffn2_w_down': dwd2,
    }
    late_list = [late_parts[n] for n in BIG_LATE]
    (dh0, d_ffn1_norm, nb, daccb, ab, dgb, dub), late_recv = _ffn_bwd_act(
        "ffn1_bwd_act", dh1, h0, ffn1_norm_w, g1, u1, gw['ffn1_w_gate'], gw['ffn1_w_up'], gw['ffn1_w_down'],
        _alltoall_chips_plan(late_list), late_list)
    grad_x = dh0[CHUNK:][None]
    d_meta = dh0[CHUNK - N_META:CHUNK]

    d_c_re = jnp.transpose(_blockdiag_out(tr(dcr_s), groups, SSM_GROUP, SSM_STATE), (0, 1, 2))
    d_c_im = -_blockdiag_out(tr(dci_s), groups, SSM_GROUP, SSM_STATE)
    d_bbr = jnp.transpose(_blockdiag_out(dbr_s, groups, SSM_GROUP, SSM_STATE), (0, 2, 1))
    d_bbi = jnp.transpose(_blockdiag_out(dbi_s, groups, SSM_GROUP, SSM_STATE), (0, 2, 1))
    d_ar = jnp.sum(dar8, axis=0).reshape(groups, SSM_STATE)
    d_ai = jnp.sum(dai8, axis=0).reshape(groups, SSM_STATE)
    small_parts = [loss_part[0:1, :], d_meta, d_ffn1_norm, d_mix_norm, d_ret_norm, d_ar, d_ai, d_bbr, d_bbi,
                   d_c_re, d_c_im, d_ssm_d, d_glu_b, d_ssm_norm, d_ffn2_norm, d_final]
    small_shapes = [a.shape for a in small_parts]
    packed = _pack(small_parts)
    early_recv, (all_parts,) = _ffn_bwd_w_scatter("ffn1_bwd_w", nb, daccb, ab, dgb, dub, chip,
                                                  _allgather_all_plan([packed]), [packed])
    received = dict(zip(BIG_LATE + BIG_EARLY, late_recv + early_recv))
    ffn_names = [n for n in BIG if n.startswith('ffn')]
    chip_sum = dict(zip(ffn_names, _sum_slots("sum_chips_ffn", [received[n] for n in ffn_names], BF16)))
    for n in BIG:
        if n not in chip_sum:
            chip_sum[n] = _sum_slots("sum_chips_" + n, [received[n]], BF16)[0]
    chip_sums = [chip_sum[n] for n in BIG]
    sib_sums = _swap_sibling("swap_sibling", chip_sums)
    (loss_row, g_meta_full, g_ffn1_norm, g_mix_norm, g_ret_norm, g_ar, g_ai, g_bbr, g_bbi, g_c_re, g_c_im,
     g_ssm_d, g_glu_b, g_ssm_norm, g_ffn2_norm, g_final) = _unpack(_sum_slots("sum_small", [all_parts], F32)[0],
                                                                  small_shapes)
    g_lam_re, g_lam_im, g_log_dt, g_b_re, g_b_im = prep_vjp((g_ar, g_ai, g_bbr, g_bbi))
    loss = loss_row[0, 0]
    g_meta = lax.dynamic_slice(g_meta_full, (0, chip * (d // N_CHIP)), (N_META, d // N_CHIP))
    small_grads = {
        'meta_tokens': g_meta, 'ffn1_norm_w': g_ffn1_norm, 'mix_norm_w': g_mix_norm, 'ret_norm_w': g_ret_norm,
        'ssm_lambda_re': g_lam_re[None], 'ssm_lambda_im': g_lam_im[None], 'ssm_log_dt': g_log_dt[None],
        'ssm_b_re': g_b_re[None], 'ssm_b_im': g_b_im[None], 'ssm_c_re': g_c_re[None], 'ssm_c_im': g_c_im[None],
        'ssm_d': g_ssm_d, 'ssm_glu_b': g_glu_b, 'ssm_norm_w': g_ssm_norm, 'ffn2_norm_w': g_ffn2_norm,
        'final_norm_w': g_final.reshape(d),
    }

    grads, deltas, new_m, new_v = {}, {}, {}, {}
    g_pair = {n: [mine, sib] for n, mine, sib in zip(BIG, chip_sums, sib_sums)}
    view = lambda n, t: as_fd(t) if n in TRANSPOSED else t
    ffn_out = _adam("adam_ffn", [(view(n, w[n]), view(n, m[n]), view(n, v[n])) for n in ffn_names],
                    [g_pair[n] for n in ffn_names])
    for n, outs in zip(ffn_names, ffn_out):
        grads[n], deltas[n], new_m[n], new_v[n] = [view(n, t) for t in outs]
    for n in BIG:
        if n not in ffn_names:
            grads[n], deltas[n], new_m[n], new_v[n] = _adam("adam_" + n, [(w[n], m[n], v[n])], [g_pair[n]])[0]
    sm_shapes = [w[n].shape for n in SMALL]
    sm_out = _adam("adam_small", [(_pack([w[n] for n in SMALL]), _pack([m[n] for n in SMALL]),
                                  _pack([v[n] for n in SMALL]))],
                   [[_pack([small_grads[n].reshape(w[n].shape) for n in SMALL])]])[0]
    for dst, packed in zip((grads, deltas, new_m, new_v), sm_out):
        for n, t in zip(SMALL, _unpack(packed, sm_shapes)):
            dst[n] = t

    return (loss, grad_x, *[grads[n] for n in WEIGHT_NAMES], *[deltas[n] for n in WEIGHT_NAMES],
            *[new_m[n] for n in WEIGHT_NAMES], *[new_v[n] for n in WEIGHT_NAMES])
```

```python
import functools
import math

import jax
import jax.numpy as jnp
from jax import lax
from jax.experimental import pallas as pl
from jax.experimental.pallas import tpu as pltpu

N_META = 16
RET_HEADS = 4
HEAD_DIM = 128
SSM_GROUP = 16
SSM_STATE = 64
CHUNK = 128
ROPE_BASE = 10000.0
EPS = 1e-6
FFN_RES = 0.5
N_SEG = 8
N_SEC = 4
N_CHIP = 4
LANE = 128
FFN_CPS = 2
BWD_W_ROWS = 1664

ADAM_LR = 0.001
ADAM_B1 = 0.9
ADAM_B2 = 0.999
ADAM_EPS = 1e-08
ADAM_WD = 0.01
ADAM_STEP = 10

VMEM_LIMIT = 56 * 1024 * 1024

F32 = jnp.float32
BF16 = jnp.bfloat16
MESH = pl.DeviceIdType.MESH


def _dot(a, b):
    return jnp.dot(a, b, preferred_element_type=F32)


def _dot_nt(a, b):
    return lax.dot_general(a, b, (((1,), (1,)), ((), ())), preferred_element_type=F32)


def _dot_tn(a, b):
    return lax.dot_general(a, b, (((0,), (0,)), ((), ())), preferred_element_type=F32)


def _tile(n, target, mult=64):
    best = None
    t = mult
    while t <= min(n, target):
        if n % t == 0:
            best = t
        t += mult
    assert best is not None, (n, target)
    return best


def _params(sem, vmem=VMEM_LIMIT):
    return pltpu.CompilerParams(dimension_semantics=sem, vmem_limit_bytes=vmem)


def _rms_stats(xf):
    r = lax.rsqrt(jnp.mean(xf * xf, axis=-1, keepdims=True) + EPS)
    return xf * r, r


def _rms_bwd(dy, xh, r, w):
    dxh = dy * w
    return r * (dxh - xh * jnp.mean(dxh * xh, axis=-1, keepdims=True))


def _sigmoid(x):
    return 0.5 * jnp.tanh(0.5 * x) + 0.5


GELU_K0 = math.sqrt(2.0 / math.pi)
GELU_K1 = 0.044715


CHIP_MASKS = [(1, 0, 0), (0, 1, 0), (1, 1, 0)]
ALL_MASKS = [(0, 0, 1), (0, 1, 0), (0, 1, 1), (1, 0, 0), (1, 0, 1), (1, 1, 0), (1, 1, 1)]
SIB_MASKS = [(0, 0, 1)]
ANY_SPEC = pl.BlockSpec(memory_space=pl.ANY)


class _Plan:
    def __init__(self, arrays, masks, n_slots, src_slotted, dst_slotted, local_copy, half=False, forward=False):
        self.shapes = [(a.shape, a.dtype) for a in arrays]
        self.n = len(arrays)
        self.masks = masks
        self.n_slots = n_slots
        self.src_slotted, self.dst_slotted, self.local_copy = src_slotted, dst_slotted, local_copy
        self.half, self.forward = half, forward
        self.n_cp = self.n * len(masks) * (len(CHIP_MASKS) if forward else 1)

    def out_shape(self):
        out = []
        for shp, dt in self.shapes:
            if self.dst_slotted and not self.src_slotted:
                shp = (self.n_slots,) + shp
            elif self.src_slotted and not self.dst_slotted:
                shp = shp[1:]
            out.append(jax.ShapeDtypeStruct(shp, dt))
        return tuple(out)

    def scratch(self):
        return [pltpu.SemaphoreType.DMA((self.n_cp,)), pltpu.SemaphoreType.DMA((self.n_cp,)),
                pltpu.SemaphoreType.DMA((self.n,))]

    def _slot(self, px, py, pc):
        if self.n_slots == 8:
            return 4 * px + 2 * py + pc
        if self.n_slots == 4:
            return 2 * px + py
        return pc

    def copies(self, ins, outs, sems):
        send_sems, recv_sems, loc_sems = sems
        x, y, c = lax.axis_index("x"), lax.axis_index("y"), lax.axis_index("c")
        me = self._slot(x, y, c)
        n_m = len(self.masks)
        cps = []
        for a in range(self.n):
            if self.forward:
                rows = self.shapes[a][0][-2] // 2
                mine = pl.ds(pl.multiple_of(c * rows, 8), rows)
                for j, (mx, my, _) in enumerate(CHIP_MASKS):
                    blk = outs[a].at[2 * (1 - x if mx else x) + (1 - y if my else y), mine]
                    k = a * len(CHIP_MASKS) + j
                    cps.append(pltpu.make_async_remote_copy(
                        src_ref=blk, dst_ref=blk, send_sem=send_sems.at[k], recv_sem=recv_sems.at[k],
                        device_id=(x, y, 1 - c), device_id_type=MESH))
                continue
            if self.local_copy:
                src = ins[a].at[me] if self.src_slotted else ins[a]
                cps.append(pltpu.make_async_copy(src, outs[a].at[me], loc_sems.at[a]))
            for mi, (mx, my, mc) in enumerate(self.masks):
                px = 1 - x if mx else x
                py = 1 - y if my else y
                pc = 1 - c if mc else c
                src = ins[a].at[self._slot(px, py, pc)] if self.src_slotted else ins[a]
                dst = outs[a].at[me] if self.dst_slotted else outs[a]
                if self.half:
                    rows = src.shape[-2] // 2
                    mine = pl.ds(pl.multiple_of(c * rows, 8), rows)
                    src, dst = src.at[mine], dst.at[mine]
                k = a * n_m + mi
                cps.append(pltpu.make_async_remote_copy(
                    src_ref=src, dst_ref=dst, send_sem=send_sems.at[k], recv_sem=recv_sems.at[k],
                    device_id=(px, py, pc), device_id_type=MESH))
        return cps


def _exchange(name, plan, arrays):
    n = plan.n

    def body(*refs):
        cps = plan.copies(refs[:n], refs[n:2 * n], refs[2 * n:])
        for cp in cps:
            cp.start()
        for cp in cps:
            cp.wait()

    outs = pl.pallas_call(
        body, name=name, out_shape=plan.out_shape(),
        in_specs=[ANY_SPEC] * n, out_specs=tuple([ANY_SPEC] * n), scratch_shapes=plan.scratch(),
        input_output_aliases={i: i for i in range(n)} if plan.forward else {},
    )(*arrays)
    return list(outs)


def _pcall(body, *, name, grid, in_specs, out_specs, out_shape, scratch_shapes, args, plan=None, plan_args=()):
    sem = ("arbitrary",) * len(grid)
    if plan is None:
        return pl.pallas_call(body, name=name, grid=grid, in_specs=in_specs, out_specs=out_specs,
                              out_shape=out_shape, scratch_shapes=scratch_shapes,
                              compiler_params=_params(sem))(*args), []
    n_in, n_out, n_scr, n_p = len(in_specs), len(out_specs), len(scratch_shapes), plan.n

    def wrapped(*refs):
        ins = refs[:n_in]
        p_ins = refs[n_in:n_in + n_p]
        o0 = n_in + n_p
        outs = refs[o0:o0 + n_out]
        p_outs = refs[o0 + n_out:o0 + n_out + n_p]
        s0 = o0 + n_out + n_p
        scr = refs[s0:s0 + n_scr]
        sems = refs[s0 + n_scr:]
        ids = [pl.program_id(i) for i in range(len(grid))]
        first = functools.reduce(jnp.logical_and, [i == 0 for i in ids])
        last = functools.reduce(jnp.logical_and, [i == g - 1 for i, g in zip(ids, grid)])

        @pl.when(first)
        def _():
            for cp in plan.copies(p_ins, p_outs, sems):
                cp.start()

        body(*ins, *outs, *scr)

        @pl.when(last)
        def _():
            for cp in plan.copies(p_ins, p_outs, sems):
                cp.wait()

    res = pl.pallas_call(
        wrapped, name=name, grid=grid,
        in_specs=list(in_specs) + [ANY_SPEC] * n_p,
        out_specs=tuple(out_specs) + (ANY_SPEC,) * n_p,
        out_shape=tuple(out_shape) + plan.out_shape(),
        scratch_shapes=list(scratch_shapes) + plan.scratch(),
        compiler_params=_params(sem),
    )(*args, *plan_args)
    return res[:n_out], list(res[n_out:])


def _allgather_chips_plan(arrays):
    return _Plan(arrays, CHIP_MASKS, 4, False, True, True, half=True)


def _gather_two_level(name, arrays):
    n = len(arrays)
    ici = _allgather_chips_plan(arrays)
    fwd = _Plan(ici.out_shape(), SIB_MASKS, 4, True, True, False, forward=True)
    n_m = len(CHIP_MASKS)

    def body(*refs):
        ins, outs, sems = refs[:n], refs[n:2 * n], refs[2 * n:]
        ici_cps = ici.copies(ins, outs, sems[:3])
        fwd_cps = fwd.copies(None, outs, sems[3:])
        for cp in ici_cps:
            cp.start()
        for a in range(n):
            for m in range(n_m):
                ici_cps[a * (n_m + 1) + 1 + m].wait_recv()
                fwd_cps[a * n_m + m].start()
        for a in range(n):
            ici_cps[a * (n_m + 1)].wait()
            for m in range(n_m):
                ici_cps[a * (n_m + 1) + 1 + m].wait_send()
        for cp in fwd_cps:
            cp.wait()

    return list(pl.pallas_call(
        body, name=name, out_shape=ici.out_shape(),
        in_specs=[ANY_SPEC] * n, out_specs=tuple([ANY_SPEC] * n), scratch_shapes=ici.scratch() + fwd.scratch(),
    )(*arrays))


def _forward_sibling(name, gathered):
    return _exchange(name, _Plan(gathered, SIB_MASKS, 4, True, True, False, forward=True), gathered)


def _alltoall_chips_plan(arrays):
    return _Plan(arrays, CHIP_MASKS, 4, True, True, True)


def _swap_sibling(name, arrays):
    return _exchange(name, _Plan(arrays, SIB_MASKS, 2, False, False, False), arrays)


def _allgather_all_plan(arrays):
    return _Plan(arrays, ALL_MASKS, 8, False, True, True)


def _sum_slots(name, arrs, out_dtype):
    s, r, c = arrs[0].shape
    n = len(arrs)
    tr = _tile(r, 512 if n == 1 else 176, 8)

    def body(*refs):
        for a_ref, o_ref in zip(refs[:n], refs[n:]):
            acc = a_ref[0].astype(F32)
            for i in range(1, s):
                acc = acc + a_ref[i].astype(F32)
            o_ref[...] = acc.astype(out_dtype)

    return list(pl.pallas_call(
        body, name=name, grid=(r // tr,),
        in_specs=[pl.BlockSpec((s, tr, c), lambda i: (0, i, 0))] * n,
        out_specs=(pl.BlockSpec((tr, c), lambda i: (i, 0)),) * n,
        out_shape=(jax.ShapeDtypeStruct((r, c), out_dtype),) * n,
        compiler_params=_params(("arbitrary",)),
    )(*arrs))


def _adam_math(w, g, m, v):
    m_new = ADAM_B1 * m + (1.0 - ADAM_B1) * g
    v_new = ADAM_B2 * v + (1.0 - ADAM_B2) * (g * g)
    m_hat = m_new / (1.0 - ADAM_B1 ** ADAM_STEP)
    v_hat = v_new / (1.0 - ADAM_B2 ** ADAM_STEP)
    delta = -ADAM_LR * (m_hat / (jnp.sqrt(v_hat) + ADAM_EPS) + ADAM_WD * w)
    return delta, m_new, v_new


def _adam(name, wmv, g_parts):
    w0 = wmv[0][0]
    r, c = w0.shape[-2:]
    n_w = len(wmv)
    n_g = len(g_parts[0])
    tr = _tile(r, 256 if n_w == 1 else 88, 8)
    lead = w0.ndim == 3
    at = (lambda ref: ref.at[0]) if lead else (lambda ref: ref)
    n_in = 3 + n_g

    def body(*refs):
        for j in range(n_w):
            ins = refs[j * n_in:(j + 1) * n_in]
            outs = refs[n_w * n_in + 4 * j:n_w * n_in + 4 * j + 4]
            w_ref, m_ref, v_ref = [at(t) for t in ins[:3]]
            g_out, d_out, m_out, v_out = [at(t) for t in outs]
            g = ins[3][...].astype(F32)
            for gr in ins[4:]:
                g = g + gr[...].astype(F32)
            delta, m_new, v_new = _adam_math(w_ref[...], g, m_ref[...], v_ref[...])
            g_out[...] = g
            d_out[...] = delta
            m_out[...] = m_new
            v_out[...] = v_new

    spec = pl.BlockSpec((tr, c), lambda i: (i, 0))
    wspec = pl.BlockSpec((1, tr, c), lambda i: (0, i, 0)) if lead else spec
    shp = jax.ShapeDtypeStruct(w0.shape, F32)
    args = [t for (w, m, v), gp in zip(wmv, g_parts) for t in (w, m, v, *gp)]
    res = pl.pallas_call(
        body, name=name, grid=(r // tr,),
        in_specs=([wspec] * 3 + [spec] * n_g) * n_w, out_specs=(wspec,) * (4 * n_w), out_shape=(shp,) * (4 * n_w),
        compiler_params=_params(("arbitrary",)),
    )(*args)
    return [tuple(res[4 * j:4 * j + 4]) for j in range(n_w)]


SUB_ROWS = 32
FFN_BWD_ROWS = 416
FFN_FWD_ROWS = 832
FFN_LOSS_ROWS = 640
RET_ROWS = 640
S5_STEPS = 104


def _tile_parts(tm, d, head, x):
    nsub = tm // SUB_ROWS
    off = head.shape[0] // SUB_ROWS
    specs = [pl.BlockSpec(head.shape, lambda i, k: (0, 0))] + [
        pl.BlockSpec((SUB_ROWS, d), lambda i, k, j=j: (jnp.maximum(i * nsub + j - off, 0), 0)) for j in range(nsub)]

    def assemble(i, part_refs, h_sc):
        head_ref, x_refs = part_refs[0], part_refs[1:]
        for j in range(nsub):
            rows = slice(j * SUB_ROWS, (j + 1) * SUB_ROWS)
            val = x_refs[j][...]
            if j < off:
                val = jnp.where(i == 0, head_ref[rows, :], val)
            h_sc[rows, :] = val

    return specs, [head] + [x] * nsub, assemble


def _h_source(body, h, tm, d):
    if not isinstance(h, tuple):
        return body, [pl.BlockSpec((tm, d), lambda i, k: (i, 0))], [h], []
    specs, args, assemble = _tile_parts(tm, d, *h)
    n_h = len(specs)

    def with_parts(*refs):
        h_sc = refs[-1]

        @pl.when(pl.program_id(1) == 0)
        def _():
            assemble(pl.program_id(0), refs[:n_h], h_sc)

        body(h_sc, *refs[n_h:-1])

    return with_parts, specs, args, [pltpu.VMEM((tm, d), F32)]


def _ffn_fwd(name, h, nw, wg, wu, wd, plan=None, plan_args=(), loss=None):
    lp, d = (h[0].shape[0] + h[1].shape[0], h[1].shape[1]) if isinstance(h, tuple) else h.shape
    nck, f, _ = wg.shape
    tm = _tile(lp, FFN_FWD_ROWS if loss is None else FFN_LOSS_ROWS)
    last = nck // FFN_CPS - 1
    n_t = 0
    if loss is not None:
        t_specs, t_args, t_assemble = _tile_parts(tm, d, jnp.zeros((lp - loss[1].shape[0], d), F32), loss[1])
        n_t = len(t_specs)

    def body(h_ref, nw_ref, wg_ref, wu_ref, wd_ref, *rest):
        if loss is not None:
            fw_ref, t_parts, rest = rest[0], rest[1:1 + n_t], rest[1 + n_t:]
            ho_ref, g_ref, u_ref, loss_ref, dfw_ref, n_sc, acc_sc, t_sc = rest
        else:
            ho_ref, g_ref, u_ref, n_sc, acc_sc = rest
        i = pl.program_id(0)
        k = pl.program_id(1)

        @pl.when(k == 0)
        def _():
            xh, _ = _rms_stats(h_ref[...])
            n_sc[...] = (xh * nw_ref[...]).astype(BF16)
            acc_sc[...] = jnp.zeros_like(acc_sc)

        n = n_sc[...]
        acc = acc_sc[...]
        for c in range(FFN_CPS):
            g = _dot_nt(n, wg_ref[c])
            u = _dot_nt(n, wu_ref[c])
            g_ref[c] = g.astype(BF16)
            u_ref[c] = u.astype(BF16)
            a = (g * _sigmoid(g) * u).astype(BF16)
            acc = acc + _dot(a, wd_ref[c])
        acc_sc[...] = acc

        if loss is None:
            @pl.when(k == last)
            def _():
                ho_ref[...] = h_ref[...] + FFN_RES * acc_sc[...]
            return

        @pl.when(jnp.logical_and(i == 0, k == 0))
        def _():
            loss_ref[...] = jnp.zeros_like(loss_ref)
            dfw_ref[...] = jnp.zeros_like(dfw_ref)

        @pl.when(k == last)
        def _():
            t_assemble(i, t_parts, t_sc)
            xh, r = _rms_stats(h_ref[...] + FFN_RES * acc_sc[...])
            w = fw_ref[...]
            head_rows = lp - loss[1].shape[0]
            row = lax.broadcasted_iota(jnp.int32, (tm, d), 0) + i * tm
            err = jnp.where(row < head_rows, 0.0, xh * w - t_sc[...])
            loss_ref[...] += 0.5 * jnp.sum(err * err) / d
            dout = err * (1.0 / d)
            dfw_ref[...] += jnp.sum(dout * xh, axis=0, keepdims=True)
            ho_ref[...] = _rms_bwd(dout, xh, r, w)

    body, h_specs, h_args, h_scratch = _h_source(body, h, tm, d)
    vec = pl.BlockSpec((1, d), lambda i, k: (0, 0))
    w_fd = pl.BlockSpec((FFN_CPS, f, d), lambda i, k: (k, 0, 0))
    hid = pl.BlockSpec((FFN_CPS, tm, f), lambda i, k: (k, i, 0))
    hshape = jax.ShapeDtypeStruct((nck, lp, f), BF16)
    args, in_specs = (*h_args, nw, wg, wu, wd), h_specs + [vec, w_fd, w_fd, w_fd]
    out_specs = (pl.BlockSpec((tm, d), lambda i, k: (i, 0)), hid, hid)
    out_shape = (jax.ShapeDtypeStruct((lp, d), F32), hshape, hshape)
    scratch = [pltpu.VMEM((tm, d), BF16), pltpu.VMEM((tm, d), F32)]
    if loss is not None:
        args, in_specs = (*args, loss[0], *t_args), in_specs + [vec] + t_specs
        out_specs += (pl.BlockSpec((8, LANE), lambda i, k: (0, 0)), vec)
        out_shape += (jax.ShapeDtypeStruct((8, LANE), F32), jax.ShapeDtypeStruct((1, d), F32))
        scratch = scratch + [pltpu.VMEM((tm, d), F32)]
    return _pcall(
        body, name=name, grid=(lp // tm, nck // FFN_CPS), plan=plan, plan_args=plan_args,
        args=args, in_specs=in_specs, out_specs=out_specs, out_shape=out_shape,
        scratch_shapes=scratch + h_scratch)


def _ffn_bwd_act(name, dh, h, nw, g, u, wg, wu, wd, plan=None, plan_args=()):
    lp, d = dh.shape
    nck, f, _ = wg.shape
    tm = _tile(lp, FFN_BWD_ROWS, SUB_ROWS)
    last = nck // FFN_CPS - 1

    def body(h_ref, dh_ref, nw_ref, g_ref, u_ref, wg_ref, wu_ref, wd_ref,
             dhi_ref, dnw_ref, n_ref, dacc_ref, a_ref, dg_ref, du_ref,
             xh_sc, r_sc, dn_sc):
        i = pl.program_id(0)
        k = pl.program_id(1)

        @pl.when(k == 0)
        def _():
            xh, r = _rms_stats(h_ref[...])
            xh_sc[...] = xh
            r_sc[...] = r
            n_ref[...] = (xh * nw_ref[...]).astype(BF16)
            dacc_ref[...] = (FFN_RES * dh_ref[...]).astype(BF16)
            dn_sc[...] = jnp.zeros_like(dn_sc)

        @pl.when(jnp.logical_and(i == 0, k == 0))
        def _():
            dnw_ref[...] = jnp.zeros_like(dnw_ref)

        dacc = dacc_ref[...]
        dn = dn_sc[...]
        for c in range(FFN_CPS):
            gv = g_ref[c].astype(F32)
            uv = u_ref[c].astype(F32)
            sg = _sigmoid(gv)
            sil = gv * sg
            da = _dot_nt(dacc, wd_ref[c])
            dgk = (da * uv * (sg * (1.0 + gv * (1.0 - sg)))).astype(BF16)
            duk = (da * sil).astype(BF16)
            a_ref[c] = (sil * uv).astype(BF16)
            dg_ref[c] = dgk
            du_ref[c] = duk
            dn = dn + _dot(dgk, wg_ref[c]) + _dot(duk, wu_ref[c])
        dn_sc[...] = dn

        @pl.when(k == last)
        def _():
            dnl = dn_sc[...]
            xh = xh_sc[...]
            dhi_ref[...] = dh_ref[...] + _rms_bwd(dnl, xh, r_sc[...], nw_ref[...])
            dnw_ref[...] += jnp.sum(dnl * xh, axis=0, keepdims=True)

    body, h_specs, h_args, h_scratch = _h_source(body, h, tm, d)
    row = pl.BlockSpec((tm, d), lambda i, k: (i, 0))
    vec = pl.BlockSpec((1, d), lambda i, k: (0, 0))
    hid = pl.BlockSpec((FFN_CPS, tm, f), lambda i, k: (k, i, 0))
    w_fd = pl.BlockSpec((FFN_CPS, f, d), lambda i, k: (k, 0, 0))
    rshape = jax.ShapeDtypeStruct((lp, d), BF16)
    hshape = jax.ShapeDtypeStruct((nck, lp, f), BF16)
    return _pcall(
        body, name=name, grid=(lp // tm, nck // FFN_CPS), plan=plan, plan_args=plan_args,
        args=(*h_args, dh, nw, g, u, wg, wu, wd),
        in_specs=h_specs + [row, vec, hid, hid, w_fd, w_fd, w_fd],
        out_specs=(row, vec, row, row, hid, hid, hid),
        out_shape=(jax.ShapeDtypeStruct((lp, d), F32), jax.ShapeDtypeStruct((1, d), F32),
                   rshape, rshape, hshape, hshape, hshape),
        scratch_shapes=[pltpu.VMEM((tm, d), F32), pltpu.VMEM((tm, 1), F32), pltpu.VMEM((tm, d), F32)] + h_scratch)


def _ffn_bwd_w(name, n, dacc, a, dg, du, plan=None, plan_args=()):
    lp, d = n.shape
    nck, _, f = a.shape
    tm = _tile(lp, BWD_W_ROWS)
    last = lp // tm - 1

    def body(n_ref, dacc_ref, a_ref, dg_ref, du_ref, dwg_ref, dwu_ref, dwd_ref, ag_sc, au_sc, ad_sc):
        i = pl.program_id(1)

        @pl.when(i == 0)
        def _():
            ag_sc[...] = jnp.zeros_like(ag_sc)
            au_sc[...] = jnp.zeros_like(au_sc)
            ad_sc[...] = jnp.zeros_like(ad_sc)

        nv = n_ref[...]
        ag_sc[...] += _dot_tn(dg_ref[0], nv)
        au_sc[...] += _dot_tn(du_ref[0], nv)
        ad_sc[...] += _dot_tn(a_ref[0], dacc_ref[...])

        @pl.when(i == last)
        def _():
            dwg_ref[0] = ag_sc[...].astype(BF16)
            dwu_ref[0] = au_sc[...].astype(BF16)
            dwd_ref[0] = ad_sc[...].astype(BF16)

    row = pl.BlockSpec((tm, d), lambda k, i: (i, 0))
    hid = pl.BlockSpec((1, tm, f), lambda k, i: (k, i, 0))
    w_fd = pl.BlockSpec((1, f, d), lambda k, i: (k, 0, 0))
    wshape = jax.ShapeDtypeStruct((nck, f, d), BF16)
    return _pcall(
        body, name=name, grid=(nck, lp // tm), plan=plan, plan_args=plan_args, args=(n, dacc, a, dg, du),
        in_specs=[row, row, hid, hid, hid], out_specs=(w_fd, w_fd, w_fd), out_shape=(wshape,) * 3,
        scratch_shapes=[pltpu.VMEM((f, d), F32)] * 3)


def _ffn_bwd_w_scatter(name, n, dacc, a, dg, du, chip, plan, plan_args):
    lp, d = n.shape
    nck, _, f = a.shape
    tm = _tile(lp, BWD_W_ROWS)
    last_i = lp // tm - 1
    n_w = 3
    n_p = plan.n

    def body(me_ref, n_ref, dacc_ref, a_ref, dg_ref, du_ref, *rest):
        p_ins = rest[:n_p]
        recv = rest[n_p:n_p + n_w]
        p_outs = rest[n_p + n_w:2 * n_p + n_w]
        acc = rest[2 * n_p + n_w:2 * n_p + 2 * n_w]
        stage, send_sems, recv_sems, loc_sems = rest[2 * n_p + 2 * n_w:2 * n_p + 2 * n_w + 4]
        p_sems = rest[2 * n_p + 2 * n_w + 4:]
        p = pl.program_id(0)
        i = pl.program_id(1)
        me = me_ref[0]
        c = lax.axis_index("c")

        def send(w, pos):
            kk = jnp.bitwise_xor(me, nck - 1 - pos)
            diff = jnp.bitwise_xor(kk, me)
            m = jnp.where(diff == 2, 0, jnp.where(diff == 1, 1, 2))
            return pltpu.make_async_remote_copy(
                src_ref=stage.at[lax.rem(pos, 2), w], dst_ref=recv[w].at[me],
                send_sem=send_sems.at[w * 3 + m], recv_sem=recv_sems.at[w * 3 + m],
                device_id=(lax.div(kk, 2), lax.rem(kk, 2), c), device_id_type=MESH)

        @pl.when(jnp.logical_and(p == 0, i == 0))
        def _():
            for cp in plan.copies(p_ins, p_outs, p_sems):
                cp.start()

        @pl.when(i == 0)
        def _():
            for t in acc:
                t[...] = jnp.zeros_like(t)

        nv = n_ref[...]
        acc[0][...] += _dot_tn(dg_ref[0], nv)
        acc[1][...] += _dot_tn(du_ref[0], nv)
        acc[2][...] += _dot_tn(a_ref[0], dacc_ref[...])

        @pl.when(jnp.logical_and(i == last_i, p >= 2))
        def _():
            for w in range(n_w):
                send(w, p - 2).wait_send()

        @pl.when(i == last_i)
        def _():
            for w in range(n_w):
                stage[lax.rem(p, 2), w] = acc[w][...].astype(BF16)

        @pl.when(jnp.logical_and(i == last_i, p < nck - 1))
        def _():
            for w in range(n_w):
                send(w, p).start()

        @pl.when(jnp.logical_and(i == last_i, p == nck - 1))
        def _():
            own = [pltpu.make_async_copy(stage.at[(nck - 1) % 2, w], recv[w].at[me], loc_sems.at[w])
                   for w in range(n_w)]
            for cp in own:
                cp.start()
            for w in range(n_w):
                send(w, nck - 2).wait_send()
            for cp in own:
                cp.wait()
            for w in range(n_w):
                for m in range(3):
                    pltpu.make_async_remote_copy(
                        src_ref=stage.at[0, w], dst_ref=recv[w].at[me],
                        send_sem=send_sems.at[w * 3 + m], recv_sem=recv_sems.at[w * 3 + m],
                        device_id=(0, 0, c), device_id_type=MESH).wait_recv()
            for cp in plan.copies(p_ins, p_outs, p_sems):
                cp.wait()

    chunk = lambda k, me_ref: jnp.bitwise_xor(me_ref[0], nck - 1 - k)
    row = pl.BlockSpec((tm, d), lambda k, i, me_ref: (i, 0))
    hid = pl.BlockSpec((1, tm, f), lambda k, i, me_ref: (chunk(k, me_ref), i, 0))
    wshape = jax.ShapeDtypeStruct((nck, f, d), BF16)
    res = pl.pallas_call(
        body, name=name,
        grid_spec=pltpu.PrefetchScalarGridSpec(
            num_scalar_prefetch=1, grid=(nck, lp // tm),
            in_specs=[row, row, hid, hid, hid] + [ANY_SPEC] * n_p,
            out_specs=(ANY_SPEC,) * (n_w + n_p),
            scratch_shapes=[pltpu.VMEM((f, d), F32)] * n_w + [
                pltpu.VMEM((2, n_w, f, d), BF16), pltpu.SemaphoreType.DMA((n_w * 3,)),
                pltpu.SemaphoreType.DMA((n_w * 3,)), pltpu.SemaphoreType.DMA((n_w,))] + plan.scratch()),
        out_shape=(wshape,) * n_w + plan.out_shape(),
        compiler_params=_params(("arbitrary", "arbitrary")),
    )(chip.reshape(1).astype(jnp.int32), n, dacc, a, dg, du, *plan_args)
    return list(res[:n_w]), list(res[n_w:])


def _inproj_fwd(h, nw, w_in, cosf, sinf, rw):
    lp, d = h.shape
    nck, _, ps = w_in.shape
    proj = nck * ps
    sw = proj - 4 * rw
    tm = _tile(lp, 640)
    scale = HEAD_DIM ** -0.5
    heads = rw // HEAD_DIM

    def body(h_ref, nw_ref, w_ref, cos_ref, sin_ref, n_ref, q_ref, k_ref, v_ref, g_ref, u_ref, p_sc):
        xh, _ = _rms_stats(h_ref[...])
        n = (xh * nw_ref[...]).astype(BF16)
        n_ref[...] = n
        for c in range(nck):
            p_sc[:, c * ps:(c + 1) * ps] = _dot(n, w_ref[c])
        cs = cos_ref[...]
        sn = sin_ref[...]
        for hh in range(heads):
            lo = hh * HEAD_DIM
            qh = p_sc[:, lo:lo + HEAD_DIM]
            q_ref[:, lo:lo + HEAD_DIM] = (qh * cs + pltpu.roll(qh, HEAD_DIM // 2, 1) * sn).astype(BF16)
            kh = p_sc[:, rw + lo:rw + lo + HEAD_DIM]
            k_ref[:, lo:lo + HEAD_DIM] = ((kh * cs + pltpu.roll(kh, HEAD_DIM // 2, 1) * sn) * scale).astype(BF16)
        v_ref[...] = p_sc[:, 2 * rw:3 * rw].astype(BF16)
        g_ref[...] = p_sc[:, 3 * rw:4 * rw]
        u_ref[...] = p_sc[:, 4 * rw:]

    row = lambda w: pl.BlockSpec((tm, w), lambda i: (i, 0))
    return pl.pallas_call(
        body, name="inproj_fwd", grid=(lp // tm,),
        in_specs=[row(d), pl.BlockSpec((1, d), lambda i: (0, 0)),
                  pl.BlockSpec((nck, d, ps), lambda i: (0, 0, 0)), row(HEAD_DIM), row(HEAD_DIM)],
        out_specs=(row(d), row(rw), row(rw), row(rw), row(rw), row(sw)),
        out_shape=(jax.ShapeDtypeStruct((lp, d), BF16),
                   jax.ShapeDtypeStruct((lp, rw), BF16),
                   jax.ShapeDtypeStruct((lp, rw), BF16),
                   jax.ShapeDtypeStruct((lp, rw), BF16),
                   jax.ShapeDtypeStruct((lp, rw), F32),
                   jax.ShapeDtypeStruct((lp, sw), F32)),
        scratch_shapes=[pltpu.VMEM((tm, proj), F32)],
        compiler_params=_params(("arbitrary",)),
    )(h, nw, w_in, cosf, sinf)


def _inproj_bwd(dh, h, nw, n, w_in, dq, dk, dv, dg, du):
    lp, d = h.shape
    nck, _, ps = w_in.shape
    rw = dq.shape[1]
    sw = du.shape[1]
    proj = nck * ps
    tm = _tile(lp, 640)
    last = lp // tm - 1

    def gather_dproj(p_sc, dq_ref, dk_ref, dv_ref, dg_ref, du_ref):
        p_sc[:, 0:rw] = dq_ref[...]
        p_sc[:, rw:2 * rw] = dk_ref[...]
        p_sc[:, 2 * rw:3 * rw] = dv_ref[...]
        p_sc[:, 3 * rw:4 * rw] = dg_ref[...]
        p_sc[:, 4 * rw:] = du_ref[...]

    def act_body(dh_ref, h_ref, nw_ref, w_ref, dq_ref, dk_ref, dv_ref, dg_ref, du_ref, dhi_ref, dnw_ref, p_sc):
        i = pl.program_id(0)

        @pl.when(i == 0)
        def _():
            dnw_ref[...] = jnp.zeros_like(dnw_ref)

        gather_dproj(p_sc, dq_ref, dk_ref, dv_ref, dg_ref, du_ref)
        dn = jnp.zeros((tm, d), F32)
        for c in range(nck):
            dn = dn + _dot_nt(p_sc[:, c * ps:(c + 1) * ps], w_ref[c])
        xh, r = _rms_stats(h_ref[...])
        dhi_ref[...] = dh_ref[...] + _rms_bwd(dn, xh, r, nw_ref[...])
        dnw_ref[...] += jnp.sum(dn * xh, axis=0, keepdims=True)

    def w_body(n_ref, dq_ref, dk_ref, dv_ref, dg_ref, du_ref, dw_ref, p_sc, acc_sc):
        i = pl.program_id(0)

        @pl.when(i == 0)
        def _():
            acc_sc[...] = jnp.zeros_like(acc_sc)

        gather_dproj(p_sc, dq_ref, dk_ref, dv_ref, dg_ref, du_ref)
        nv = n_ref[...]
        for c in range(nck):
            acc_sc[c] += _dot_tn(nv, p_sc[:, c * ps:(c + 1) * ps])

        @pl.when(i == last)
        def _():
            dw_ref[...] = acc_sc[...].astype(BF16)

    row = lambda w: pl.BlockSpec((tm, w), lambda i: (i, 0))
    vec = pl.BlockSpec((1, d), lambda i: (0, 0))
    wsp = pl.BlockSpec((nck, d, ps), lambda i: (0, 0, 0))
    dproj_specs = [row(rw), row(rw), row(rw), row(rw), row(sw)]
    dhi, dnw = pl.pallas_call(
        act_body, name="inproj_bwd_act", grid=(lp // tm,),
        in_specs=[row(d), row(d), vec, wsp] + dproj_specs,
        out_specs=(row(d), vec),
        out_shape=(jax.ShapeDtypeStruct((lp, d), F32), jax.ShapeDtypeStruct((1, d), F32)),
        scratch_shapes=[pltpu.VMEM((tm, proj), BF16)],
        compiler_params=_params(("arbitrary",)),
    )(dh, h, nw, w_in, dq, dk, dv, dg, du)
    dw = pl.pallas_call(
        w_body, name="inproj_bwd_w", grid=(lp // tm,),
        in_specs=[row(d)] + dproj_specs,
        out_specs=wsp, out_shape=jax.ShapeDtypeStruct((nck, d, ps), BF16),
        scratch_shapes=[pltpu.VMEM((tm, proj), BF16), pltpu.VMEM((nck, d, ps), F32)],
        compiler_params=_params(("arbitrary",)),
    )(n, dq, dk, dv, dg, du)
    return dhi, dnw, dw


def _retention_tables(rc):
    h = jnp.arange(RET_HEADS, dtype=F32)
    log_g = jnp.log(1.0 - 2.0 ** (-5.0 - h))
    i = jnp.arange(rc)
    diff = i[:, None] - i[None, :]
    dec = jnp.where(diff[None] >= 0,
                    jnp.exp(log_g[:, None, None] * jnp.maximum(diff, 0)[None].astype(F32)), 0.0)
    pos = jnp.arange(rc, dtype=F32)
    wq = jnp.exp(log_g[:, None] * (pos + 1.0)[None])
    wk = jnp.exp(log_g[:, None] * (rc - 1 - pos)[None])
    gch = jnp.exp(log_g * rc)
    ones = jnp.ones((1, 1, HEAD_DIM), F32)
    return (dec, wq[:, :, None] * ones, wk[:, :, None] * ones,
            gch[:, None, None] * jnp.ones((1, 8, HEAD_DIM), F32))


def _head_norm(o):
    mu = jnp.mean(o, axis=-1, keepdims=True)
    oc = o - mu
    r = lax.rsqrt(jnp.mean(oc * oc, axis=-1, keepdims=True) + EPS)
    return oc * r, r


def _ret_fwd(q, k, v, g, rnw, tables):
    lp, rw = q.shape
    heads = rw // HEAD_DIM
    rc = tables[0].shape[1]
    nch = lp // rc
    dec, wq, wk, gch = tables

    def body(q_ref, k_ref, v_ref, g_ref, w_ref, dec_ref, wq_ref, wk_ref, gch_ref,
             o_ref, ret_ref, sp_ref, s_sc):
        n = pl.program_id(0)

        @pl.when(n == 0)
        def _():
            s_sc[...] = jnp.zeros_like(s_sc)

        cols = [slice(hh * HEAD_DIM, (hh + 1) * HEAD_DIM) for hh in range(heads)]
        s_ins = [s_sc[hh] for hh in range(heads)]
        outs = []
        for hh, cs in enumerate(cols):
            qv, kv, vv = q_ref[:, cs], k_ref[:, cs], v_ref[:, cs]
            s_in = s_ins[hh]
            a = _dot_nt(qv, kv) * dec_ref[hh]
            qw = (qv.astype(F32) * wq_ref[hh]).astype(BF16)
            kw = (kv.astype(F32) * wk_ref[hh]).astype(BF16)
            o = _dot(a.astype(BF16), vv) + _dot(qw, s_in.astype(BF16))
            s_new = gch_ref[hh, 0:1, :] * s_in + _dot_tn(kw, vv)
            xh, _ = _head_norm(o)
            gv = g_ref[:, cs]
            outs.append((o, s_new, (gv * _sigmoid(gv) * (xh * w_ref[:, cs])).astype(BF16)))
        for hh, cs in enumerate(cols):
            o, s_new, ret = outs[hh]
            sp_ref[hh, 0] = s_ins[hh]
            s_sc[hh] = s_new
            o_ref[:, cs] = o
            ret_ref[:, cs] = ret

    blk = pl.BlockSpec((rc, rw), lambda n: (n, 0))
    tab = pl.BlockSpec((heads, rc, HEAD_DIM), lambda n: (0, 0, 0))
    dtab = pl.BlockSpec((heads, rc, rc), lambda n: (0, 0, 0))
    return pl.pallas_call(
        body, name="retention_fwd", grid=(nch,),
        in_specs=[blk, blk, blk, blk, pl.BlockSpec((1, rw), lambda n: (0, 0)),
                  dtab, tab, tab, pl.BlockSpec((heads, 8, HEAD_DIM), lambda n: (0, 0, 0))],
        out_specs=(blk, blk, pl.BlockSpec((heads, 1, HEAD_DIM, HEAD_DIM), lambda n: (0, n, 0, 0))),
        out_shape=(jax.ShapeDtypeStruct((lp, rw), F32),
                   jax.ShapeDtypeStruct((lp, rw), BF16),
                   jax.ShapeDtypeStruct((heads, nch, HEAD_DIM, HEAD_DIM), F32)),
        scratch_shapes=[pltpu.VMEM((heads, HEAD_DIM, HEAD_DIM), F32)],
        compiler_params=_params(("arbitrary",)),
    )(q, k, v, g, rnw, dec, wq, wk, gch)


def _ret_bwd(dret, q, k, v, g, o, sprev, rnw, tables, cosf, sinf):
    lp, rw = q.shape
    heads = rw // HEAD_DIM
    rc = tables[0].shape[1]
    nch = lp // rc
    dec, wq, wk, gch = tables
    scale = HEAD_DIM ** -0.5
    half = HEAD_DIM // 2

    def body(dret_ref, q_ref, k_ref, v_ref, g_ref, o_ref, sp_ref, w_ref, dec_ref, wq_ref, wk_ref, gch_ref,
             cos_ref, sin_ref, dq_ref, dk_ref, dv_ref, dg_ref, dw_ref, ds_sc):
        n = pl.program_id(0)

        @pl.when(n == 0)
        def _():
            ds_sc[...] = jnp.zeros_like(ds_sc)
            dw_ref[...] = jnp.zeros_like(dw_ref)

        cosv = cos_ref[...]
        sinv = sin_ref[...]
        cols = [slice(hh * HEAD_DIM, (hh + 1) * HEAD_DIM) for hh in range(heads)]
        ds_ins = [ds_sc[hh] for hh in range(heads)]
        dw_ins = [dw_ref[:, cs] for cs in cols]
        outs = []
        for hh, cs in enumerate(cols):
            qv, kv, vv = q_ref[:, cs], k_ref[:, cs], v_ref[:, cs]
            gv = g_ref[:, cs]
            dr = dret_ref[:, cs]
            w = w_ref[:, cs]
            sg = _sigmoid(gv)
            sil = gv * sg
            xh, r = _head_norm(o_ref[:, cs])
            dgate = (dr * (xh * w) * (sg * (1.0 + gv * (1.0 - sg)))).astype(BF16)
            dyw = dr * sil
            dw_new = dw_ins[hh] + jnp.sum(dyw * xh, axis=0, keepdims=True)
            dxh = dyw * w
            do = r * (dxh - jnp.mean(dxh, axis=-1, keepdims=True)
                      - xh * jnp.mean(dxh * xh, axis=-1, keepdims=True))
            dob = do.astype(BF16)
            dmask = dec_ref[hh]
            wqv = wq_ref[hh]
            wkv = wk_ref[hh]
            a = (_dot_nt(qv, kv) * dmask).astype(BF16)
            da = (_dot_nt(dob, vv) * dmask).astype(BF16)
            qw = (qv.astype(F32) * wqv).astype(BF16)
            kw = (kv.astype(F32) * wkv).astype(BF16)
            s_in = sp_ref[hh, 0].astype(BF16)
            ds = ds_ins[hh]
            dsb = ds.astype(BF16)
            dq = _dot(da, kv) + _dot_nt(dob, s_in) * wqv
            dk = _dot_tn(da, qv) + _dot_nt(vv, dsb) * wkv
            dv = _dot_tn(a, dob) + _dot(kw, dsb)
            ds_new = gch_ref[hh, 0:1, :] * ds + _dot_tn(qw, dob)
            outs.append((dgate, dw_new, ds_new,
                         (dq * cosv + pltpu.roll(dq * sinv, half, 1)).astype(BF16),
                         ((dk * cosv + pltpu.roll(dk * sinv, half, 1)) * scale).astype(BF16),
                         dv.astype(BF16)))
        for hh, cs in enumerate(cols):
            dgate, dw_new, ds_new, dqv, dkv, dvv = outs[hh]
            dg_ref[:, cs] = dgate
            dw_ref[:, cs] = dw_new
            ds_sc[hh] = ds_new
            dq_ref[:, cs] = dqv
            dk_ref[:, cs] = dkv
            dv_ref[:, cs] = dvv

    blk = pl.BlockSpec((rc, rw), lambda n: (nch - 1 - n, 0))
    tab = pl.BlockSpec((heads, rc, HEAD_DIM), lambda n: (0, 0, 0))
    dtab = pl.BlockSpec((heads, rc, rc), lambda n: (0, 0, 0))
    wsp = pl.BlockSpec((1, rw), lambda n: (0, 0))
    pos = pl.BlockSpec((rc, HEAD_DIM), lambda n: (nch - 1 - n, 0))
    bshape = jax.ShapeDtypeStruct((lp, rw), BF16)
    return pl.pallas_call(
        body, name="retention_bwd", grid=(nch,),
        in_specs=[blk, blk, blk, blk, blk, blk,
                  pl.BlockSpec((heads, 1, HEAD_DIM, HEAD_DIM), lambda n: (0, nch - 1 - n, 0, 0)),
                  wsp, dtab, tab, tab, pl.BlockSpec((heads, 8, HEAD_DIM), lambda n: (0, 0, 0)), pos, pos],
        out_specs=(blk, blk, blk, blk, wsp),
        out_shape=(bshape, bshape, bshape, bshape, jax.ShapeDtypeStruct((1, rw), F32)),
        scratch_shapes=[pltpu.VMEM((heads, HEAD_DIM, HEAD_DIM), F32)],
        compiler_params=_params(("arbitrary",)),
    )(dret, q, k, v, g, o, sprev, rnw, dec, wq, wk, gch, cosf, sinf)


SCAN_CW = 512


def _s5_prepare(lam_re, lam_im, log_dt, b_re, b_im):
    dt = jnp.exp(log_dt)[:, None]
    er = jnp.exp(lam_re * dt)
    ar = er * jnp.cos(lam_im * dt)
    ai = er * jnp.sin(lam_im * dt)
    den = lam_re * lam_re + lam_im * lam_im
    fr = ((ar - 1.0) * lam_re + ai * lam_im) / den
    fi = (ai * lam_re - (ar - 1.0) * lam_im) / den
    bbr = fr[..., None] * b_re - fi[..., None] * b_im
    bbi = fr[..., None] * b_im + fi[..., None] * b_re
    return ar, ai, bbr, bbi


def _blockdiag_in(t):
    g, p, n = t.shape
    gs = g // N_SEC
    t = t.reshape(N_SEC, gs, p, n)
    eye = jnp.eye(gs, dtype=t.dtype)
    return jnp.einsum("sgpn,gh->sgphn", t, eye).reshape(N_SEC, gs * p, gs * n)


def _blockdiag_out(m, g, p, n):
    gs = g // N_SEC
    m = m.reshape(N_SEC, gs, p, gs, n)
    eye = jnp.eye(gs, dtype=m.dtype)
    return jnp.einsum("sgphn,gh->sgpn", m, eye).reshape(g, p, n)


def _scan_step(xr_ref, xi_ref, r0, prev, ar_ref, ai_ref, conj, ncols):
    new = []
    for cc in range(ncols // SCAN_CW):
        cs = pl.ds(cc * SCAN_CW, SCAN_CW)
        pr, pi = prev[cc]
        ar = ar_ref[:, cs]
        ai = ai_ref[:, cs]
        if conj:
            nr = ar * pr + ai * pi
            ni = ar * pi - ai * pr
        else:
            nr = ar * pr - ai * pi
            ni = ar * pi + ai * pr
        xr = xr_ref[pl.ds(r0, 8), cs] + nr
        xi = xi_ref[pl.ds(r0, 8), cs] + ni
        xr_ref[pl.ds(r0, 8), cs] = xr
        xi_ref[pl.ds(r0, 8), cs] = xi
        new.append((xr, xi))
    return new


def _scan_chunks(ncols):
    return [pl.ds(cc * SCAN_CW, SCAN_CW) for cc in range(ncols // SCAN_CW)]


def _flat(pairs):
    return tuple(t for p in pairs for t in p)


def _pairs(flat):
    return [(flat[2 * k], flat[2 * k + 1]) for k in range(len(flat) // 2)]


def _shift_rows(z, down):
    row = lax.broadcasted_iota(jnp.int32, z.shape, 0)
    if down:
        return jnp.where(row == 0, 0.0, pltpu.roll(z, 1, 0))
    return jnp.where(row == N_SEG - 1, 0.0, pltpu.roll(z, N_SEG - 1, 0))


def _s5_fwd(u, bsr, bsi, csr, csi, a8r, a8i, al8r, al8i, d, gluw, glub, nw, jb):
    lp, sw = u.shape
    ns = a8r.shape[1]
    rows = N_SEG * jb
    nblk = lp // rows
    secw = sw // N_SEC
    secn = ns // N_SEC

    def local_scan(u_ref, bsr_ref, bsi_ref, ar_ref, ai_ref, xr_ref, xi_ref, pr_sc, pi_sc):
        for s in range(N_SEC):
            ub = u_ref[:, s * secw:(s + 1) * secw].astype(BF16)
            xr_ref[:, s * secn:(s + 1) * secn] = _dot(ub, bsr_ref[s])
            xi_ref[:, s * secn:(s + 1) * secn] = _dot(ub, bsi_ref[s])
        prev = [(pr_sc[:, cs], pi_sc[:, cs]) for cs in _scan_chunks(ns)]
        prev = _scan_step(xr_ref, xi_ref, 0, prev, ar_ref, ai_ref, False, ns)

        def step(j, carry):
            r0 = pl.multiple_of(j * 8, 8)
            return _flat(_scan_step(xr_ref, xi_ref, r0, _pairs(carry), ar_ref, ai_ref, False, ns))

        last = _pairs(lax.fori_loop(1, jb, step, _flat(prev)))
        for cs, (vr, vi) in zip(_scan_chunks(ns), last):
            pr_sc[:, cs] = vr
            pi_sc[:, cs] = vi

    def carry_body(u_ref, bsr_ref, bsi_ref, ar_ref, ai_ref, alr_ref, ali_ref, cr_ref, ci_ref,
                   xr_sc, xi_sc, pr_sc, pi_sc):
        b = pl.program_id(0)

        @pl.when(b == 0)
        def _():
            pr_sc[...] = jnp.zeros_like(pr_sc)
            pi_sc[...] = jnp.zeros_like(pi_sc)

        local_scan(u_ref, bsr_ref, bsi_ref, ar_ref, ai_ref, xr_sc, xi_sc, pr_sc, pi_sc)

        @pl.when(b == nblk - 1)
        def _():
            er = _shift_rows(pr_sc[...], True)
            ei = _shift_rows(pi_sc[...], True)
            alr, ali = alr_ref[...], ali_ref[...]
            cr, ci = er, ei
            for _ in range(N_SEG - 2):
                sr = _shift_rows(cr, True)
                si = _shift_rows(ci, True)
                cr = er + alr * sr - ali * si
                ci = ei + alr * si + ali * sr
            cr_ref[...] = cr
            ci_ref[...] = ci

    ublk = pl.BlockSpec((rows, sw), lambda b: (b, 0))
    bspec = pl.BlockSpec((N_SEC, secw, secn), lambda b: (0, 0, 0))
    cspec = pl.BlockSpec((N_SEC, secn, secw), lambda b: (0, 0, 0))
    s8 = pl.BlockSpec((N_SEG, ns), lambda b: (0, 0))
    vec = pl.BlockSpec((1, sw), lambda b: (0, 0))
    s8shape = jax.ShapeDtypeStruct((N_SEG, ns), F32)
    c0r, c0i = pl.pallas_call(
        carry_body, name="s5_fwd_carry", grid=(nblk,),
        in_specs=[ublk, bspec, bspec, s8, s8, s8, s8],
        out_specs=(s8, s8), out_shape=(s8shape, s8shape),
        scratch_shapes=[pltpu.VMEM((rows, ns), F32), pltpu.VMEM((rows, ns), F32),
                        pltpu.VMEM((N_SEG, ns), F32), pltpu.VMEM((N_SEG, ns), F32)],
        compiler_params=_params(("arbitrary",)),
    )(u, bsr, bsi, a8r, a8i, al8r, al8i)

    def main_body(u_ref, bsr_ref, bsi_ref, csr_ref, csi_ref, ar_ref, ai_ref, c0r_ref, c0i_ref,
                  d_ref, gw_ref, gb_ref, nw_ref, xr_ref, xi_ref, yp_ref, out_ref, pr_sc, pi_sc):
        b = pl.program_id(0)

        @pl.when(b == 0)
        def _():
            pr_sc[...] = c0r_ref[...]
            pi_sc[...] = c0i_ref[...]

        local_scan(u_ref, bsr_ref, bsi_ref, ar_ref, ai_ref, xr_ref, xi_ref, pr_sc, pi_sc)
        for s in range(N_SEC):
            xs = pl.ds(s * secn, secn)
            us = pl.ds(s * secw, secw)
            y = _dot(xr_ref[:, xs].astype(BF16), csr_ref[s]) + _dot(xi_ref[:, xs].astype(BF16), csi_ref[s])
            yp_ref[:, us] = y + d_ref[:, us] * u_ref[:, us]
        yp = yp_ref[...]
        t = jnp.tanh(GELU_K0 * (yp + GELU_K1 * yp * yp * yp))
        y1 = 0.5 * yp * (1.0 + t)
        z = _dot(y1.astype(BF16), gw_ref[...]) + gb_ref[...]
        y2 = y1 * _sigmoid(z)
        xh, _ = _rms_stats(y2)
        out_ref[...] = (xh * nw_ref[...]).astype(BF16)

    xblk = pl.BlockSpec((rows, ns), lambda b: (b, 0))
    xr, xi, yp, out = pl.pallas_call(
        main_body, name="s5_fwd", grid=(nblk,),
        in_specs=[ublk, bspec, bspec, cspec, cspec, s8, s8, s8, s8, vec,
                  pl.BlockSpec((sw, sw), lambda b: (0, 0)), vec, vec],
        out_specs=(xblk, xblk, ublk, ublk),
        out_shape=(jax.ShapeDtypeStruct((lp, ns), F32), jax.ShapeDtypeStruct((lp, ns), F32),
                   jax.ShapeDtypeStruct((lp, sw), F32), jax.ShapeDtypeStruct((lp, sw), BF16)),
        scratch_shapes=[pltpu.VMEM((N_SEG, ns), F32), pltpu.VMEM((N_SEG, ns), F32)],
        compiler_params=_params(("arbitrary",)),
    )(u, bsr, bsi, csr, csi, a8r, a8i, c0r, c0i, d, gluw, glub, nw)
    return xr, xi, c0r, c0i, yp, out


def _s5_bwd(dout, u, yp, xr, xi, c0r, c0i, bsrt, bsit, csrt, csit, a8r, a8i, al8r, al8i, d, gluw, glub, nw, jb):
    lp, sw = u.shape
    ns = a8r.shape[1]
    rows = N_SEG * jb
    nblk = lp // rows
    secw = sw // N_SEC
    secn = ns // N_SEC

    def rowwise_bwd(dout_ref, yp_ref, gw_ref, gb_ref, nw_ref):
        ypv = yp_ref[...]
        t = jnp.tanh(GELU_K0 * (ypv + GELU_K1 * ypv * ypv * ypv))
        y1 = 0.5 * ypv * (1.0 + t)
        dgelu = 0.5 * (1.0 + t) + 0.5 * ypv * (1.0 - t * t) * GELU_K0 * (1.0 + 3.0 * GELU_K1 * ypv * ypv)
        gw = gw_ref[...]
        y1b = y1.astype(BF16)
        sg = _sigmoid(_dot(y1b, gw) + gb_ref[...])
        xh, r = _rms_stats(y1 * sg)
        dov = dout_ref[...]
        dy2 = _rms_bwd(dov, xh, r, nw_ref[...])
        dz = dy2 * y1 * sg * (1.0 - sg)
        dzb = dz.astype(BF16)
        dy1 = dy2 * sg + _dot_nt(dzb, gw)
        return dy1 * dgelu, dov * xh, y1b, dzb, dz

    def lam_scan(dyp_of, csrt_ref, csit_ref, ar_ref, ai_ref, lr_sc, li_sc, nr_sc, ni_sc, extra):
        for s in range(N_SEC):
            db = dyp_of(s)
            lr_sc[:, s * secn:(s + 1) * secn] = _dot(db, csrt_ref[s])
            li_sc[:, s * secn:(s + 1) * secn] = _dot(db, csit_ref[s])
        top = rows - 8
        prev = [(nr_sc[:, cs], ni_sc[:, cs]) for cs in _scan_chunks(ns)]
        prev = _scan_step(lr_sc, li_sc, top, prev, ar_ref, ai_ref, True, ns)
        extra(top, pl.ds(top - 8, 8))

        def step(jj, carry):
            r0 = pl.multiple_of((jb - 1 - jj) * 8, 8)
            rp = pl.multiple_of((jb - 2 - jj) * 8, 8)
            new = _scan_step(lr_sc, li_sc, r0, _pairs(carry), ar_ref, ai_ref, True, ns)
            extra(r0, pl.ds(rp, 8))
            return _flat(new)

        prev = _pairs(lax.fori_loop(1, jb - 1, step, _flat(prev)))
        last = _scan_step(lr_sc, li_sc, 0, prev, ar_ref, ai_ref, True, ns)
        extra(0, None)
        for cs, (vr, vi) in zip(_scan_chunks(ns), last):
            nr_sc[:, cs] = vr
            ni_sc[:, cs] = vi

    def carry_body(dout_ref, yp_ref, u_ref, gw_ref, gb_ref, nw_ref, csrt_ref, csit_ref, ar_ref, ai_ref,
                   alr_ref, ali_ref, cr_ref, ci_ref, dyp_ref, dnw_ref, dgw_ref, dgb_ref, dd_ref,
                   lr_sc, li_sc, nr_sc, ni_sc):
        b = pl.program_id(0)

        @pl.when(b == 0)
        def _():
            nr_sc[...] = jnp.zeros_like(nr_sc)
            ni_sc[...] = jnp.zeros_like(ni_sc)
            for ref in (dnw_ref, dgw_ref, dgb_ref, dd_ref):
                ref[...] = jnp.zeros_like(ref)

        dyp, dnw_rows, y1b, dzb, dz = rowwise_bwd(dout_ref, yp_ref, gw_ref, gb_ref, nw_ref)
        dnw_ref[...] += jnp.sum(dnw_rows, axis=0, keepdims=True)
        dgw_ref[...] += _dot_tn(y1b, dzb)
        dgb_ref[...] += jnp.sum(dz, axis=0, keepdims=True)
        dd_ref[...] += jnp.sum(dyp * u_ref[...], axis=0, keepdims=True)
        dyp_ref[...] = dyp.astype(BF16)
        lam_scan(lambda s: dyp_ref[:, s * secw:(s + 1) * secw], csrt_ref, csit_ref, ar_ref, ai_ref,
                 lr_sc, li_sc, nr_sc, ni_sc, lambda r0, prev_rows: None)

        @pl.when(b == nblk - 1)
        def _():
            fr = _shift_rows(nr_sc[...], False)
            fi = _shift_rows(ni_sc[...], False)
            alr, ali = alr_ref[...], ali_ref[...]
            cr, ci = fr, fi
            for _ in range(N_SEG - 2):
                sr = _shift_rows(cr, False)
                si = _shift_rows(ci, False)
                cr = fr + alr * sr + ali * si
                ci = fi + alr * si - ali * sr
            cr_ref[...] = cr
            ci_ref[...] = ci

    rev = lambda b: (nblk - 1 - b, 0)
    ublk = pl.BlockSpec((rows, sw), rev)
    xblk = pl.BlockSpec((rows, ns), rev)
    s8 = pl.BlockSpec((N_SEG, ns), lambda b: (0, 0))
    vec = pl.BlockSpec((1, sw), lambda b: (0, 0))
    gws = pl.BlockSpec((sw, sw), lambda b: (0, 0))
    btspec = pl.BlockSpec((N_SEC, secn, secw), lambda b: (0, 0, 0))
    ctspec = pl.BlockSpec((N_SEC, secw, secn), lambda b: (0, 0, 0))
    s8shape = jax.ShapeDtypeStruct((N_SEG, ns), F32)
    lcr, lci, dyp_all, d_nw, d_gw, d_gb, d_d = pl.pallas_call(
        carry_body, name="s5_bwd_carry", grid=(nblk,),
        in_specs=[ublk, ublk, ublk, gws, vec, vec, ctspec, ctspec, s8, s8, s8, s8],
        out_specs=(s8, s8, ublk, vec, gws, vec, vec),
        out_shape=(s8shape, s8shape, jax.ShapeDtypeStruct((lp, sw), BF16), jax.ShapeDtypeStruct((1, sw), F32),
                   jax.ShapeDtypeStruct((sw, sw), F32), jax.ShapeDtypeStruct((1, sw), F32),
                   jax.ShapeDtypeStruct((1, sw), F32)),
        scratch_shapes=[pltpu.VMEM((rows, ns), F32), pltpu.VMEM((rows, ns), F32),
                        pltpu.VMEM((N_SEG, ns), F32), pltpu.VMEM((N_SEG, ns), F32)],
        compiler_params=_params(("arbitrary",)),
    )(dout, yp, u, gluw, glub, nw, csrt, csit, a8r, a8i, al8r, al8i)

    def main_body(dyp_sc, u_ref, xr_ref, xi_ref, xtr_ref, xti_ref, c0r_ref, c0i_ref, lcr_ref, lci_ref,
                  d_ref, bsrt_ref, bsit_ref, csrt_ref, csit_ref, ar_ref, ai_ref,
                  du_ref, dcr_ref, dci_ref, dbr_ref, dbi_ref, dar_ref, dai_ref,
                  lr_sc, li_sc, nr_sc, ni_sc):
        b = pl.program_id(0)

        @pl.when(b == 0)
        def _():
            nr_sc[...] = lcr_ref[...]
            ni_sc[...] = lci_ref[...]
            for ref in (dcr_ref, dci_ref, dbr_ref, dbi_ref, dar_ref, dai_ref):
                ref[...] = jnp.zeros_like(ref)

        for s in range(N_SEC):
            db = dyp_sc[:, s * secw:(s + 1) * secw]
            xs = pl.ds(s * secn, secn)
            dcr_ref[s] += _dot_tn(xr_ref[:, xs].astype(BF16), db)
            dci_ref[s] += _dot_tn(xi_ref[:, xs].astype(BF16), db)

        first = b == nblk - 1

        def acc_da(r0, prev_rows):
            for cc in range(ns // SCAN_CW):
                cs = pl.ds(cc * SCAN_CW, SCAN_CW)
                lr = lr_sc[pl.ds(r0, 8), cs]
                li = li_sc[pl.ds(r0, 8), cs]
                if prev_rows is None:
                    xpr = jnp.where(first, c0r_ref[:, cs], xtr_ref[:, cs])
                    xpi = jnp.where(first, c0i_ref[:, cs], xti_ref[:, cs])
                else:
                    xpr = xr_ref[prev_rows, cs]
                    xpi = xi_ref[prev_rows, cs]
                dar_ref[:, cs] += lr * xpr + li * xpi
                dai_ref[:, cs] += li * xpr - lr * xpi

        lam_scan(lambda s: dyp_sc[:, s * secw:(s + 1) * secw], csrt_ref, csit_ref, ar_ref, ai_ref,
                 lr_sc, li_sc, nr_sc, ni_sc, acc_da)

        for s in range(N_SEC):
            xs = pl.ds(s * secn, secn)
            us = pl.ds(s * secw, secw)
            lrb = lr_sc[:, xs].astype(BF16)
            lib = li_sc[:, xs].astype(BF16)
            du = _dot(lrb, bsrt_ref[s]) + _dot(lib, bsit_ref[s]) + d_ref[:, us] * dyp_sc[:, us].astype(F32)
            du_ref[:, us] = du.astype(BF16)
            ub = u_ref[:, us].astype(BF16)
            dbr_ref[s] += _dot_tn(ub, lrb)
            dbi_ref[s] += _dot_tn(ub, lib)

    tail = pl.BlockSpec((N_SEG, ns), lambda b: (jnp.maximum((nblk - 1 - b) * jb - 1, 0), 0))
    acc_c = pl.BlockSpec((N_SEC, secn, secw), lambda b: (0, 0, 0))
    acc_b = pl.BlockSpec((N_SEC, secw, secn), lambda b: (0, 0, 0))
    du, dcr, dci, dbr, dbi, dar, dai = pl.pallas_call(
        main_body, name="s5_bwd", grid=(nblk,),
        in_specs=[ublk, ublk, xblk, xblk, tail, tail, s8, s8, s8, s8,
                  vec, btspec, btspec, ctspec, ctspec, s8, s8],
        out_specs=(ublk, acc_c, acc_c, acc_b, acc_b, s8, s8),
        out_shape=(jax.ShapeDtypeStruct((lp, sw), BF16),
                   jax.ShapeDtypeStruct((N_SEC, secn, secw), F32),
                   jax.ShapeDtypeStruct((N_SEC, secn, secw), F32),
                   jax.ShapeDtypeStruct((N_SEC, secw, secn), F32),
                   jax.ShapeDtypeStruct((N_SEC, secw, secn), F32),
                   s8shape, s8shape),
        scratch_shapes=[pltpu.VMEM((rows, ns), F32), pltpu.VMEM((rows, ns), F32),
                        pltpu.VMEM((N_SEG, ns), F32), pltpu.VMEM((N_SEG, ns), F32)],
        compiler_params=_params(("arbitrary",)),
    )(dyp_all, u, xr, xi, xr, xi, c0r, c0i, lcr, lci, d, bsrt, bsit, csrt, csit, a8r, a8i)
    return du, d_nw, d_gw, d_gb, d_d, dcr, dci, dbr, dbi, dar, dai


def _outproj_fwd(h, ret, ssm, wo):
    lp, d = h.shape
    nck, rs, _ = wo.shape
    rw = ret.shape[1]
    tm = _tile(lp, 640)
    per = rw // rs

    def body(h_ref, ret_ref, ssm_ref, w_ref, o_ref):
        acc = h_ref[...]
        for c in range(nck):
            src = ret_ref if c < per else ssm_ref
            lo = (c % per) * rs
            acc = acc + _dot(src[:, lo:lo + rs], w_ref[c])
        o_ref[...] = acc

    row = lambda w: pl.BlockSpec((tm, w), lambda i: (i, 0))
    return pl.pallas_call(
        body, name="outproj_fwd", grid=(lp // tm,),
        in_specs=[row(d), row(rw), row(ssm.shape[1]), pl.BlockSpec((nck, rs, d), lambda i: (0, 0, 0))],
        out_specs=row(d), out_shape=jax.ShapeDtypeStruct((lp, d), F32),
        compiler_params=_params(("arbitrary",)),
    )(h, ret, ssm, wo)


def _outproj_bwd(dh, ret, ssm, wo):
    lp, d = dh.shape
    nck, rs, _ = wo.shape
    rw = ret.shape[1]
    sw = ssm.shape[1]
    tm = _tile(lp, 640)
    per = rw // rs
    last = lp // tm - 1

    def body(dh_ref, ret_ref, ssm_ref, w_ref, dret_ref, dssm_ref, dw_ref, acc_sc):
        i = pl.program_id(0)

        @pl.when(i == 0)
        def _():
            acc_sc[...] = jnp.zeros_like(acc_sc)

        dhb = dh_ref[...].astype(BF16)
        for c in range(nck):
            src, dst = (ret_ref, dret_ref) if c < per else (ssm_ref, dssm_ref)
            lo = (c % per) * rs
            dst[:, lo:lo + rs] = _dot_nt(dhb, w_ref[c])
            acc_sc[c] += _dot_tn(src[:, lo:lo + rs], dhb)

        @pl.when(i == last)
        def _():
            dw_ref[...] = acc_sc[...].astype(BF16)

    row = lambda w: pl.BlockSpec((tm, w), lambda i: (i, 0))
    wsp = pl.BlockSpec((nck, rs, d), lambda i: (0, 0, 0))
    return pl.pallas_call(
        body, name="outproj_bwd", grid=(lp // tm,),
        in_specs=[row(d), row(rw), row(sw), wsp],
        out_specs=(row(rw), row(sw), wsp),
        out_shape=(jax.ShapeDtypeStruct((lp, rw), F32), jax.ShapeDtypeStruct((lp, sw), F32),
                   jax.ShapeDtypeStruct((nck, rs, d), BF16)),
        scratch_shapes=[pltpu.VMEM((nck, rs, d), F32)],
        compiler_params=_params(("arbitrary",)),
    )(dh, ret, ssm, wo)


def _pack(arrs):
    flat = jnp.concatenate([a.reshape(-1).astype(F32) for a in arrs])
    n = flat.shape[0]
    rows = -(-n // (8 * LANE)) * 8
    return jnp.pad(flat, (0, rows * LANE - n)).reshape(rows, LANE)


def _unpack(packed, shapes):
    flat = packed.reshape(-1)
    out, off = [], 0
    for s in shapes:
        n = math.prod(s)
        out.append(flat[off:off + n].reshape(s))
        off += n
    return out


def _to_segments(a, seg_len):
    return a.reshape(N_SEG, seg_len, a.shape[1]).transpose(1, 0, 2).reshape(a.shape)


def _from_segments(a, seg_len):
    return a.reshape(seg_len, N_SEG, a.shape[1]).transpose(1, 0, 2).reshape(a.shape)


WEIGHT_NAMES = ['meta_tokens', 'ffn1_norm_w', 'ffn1_w_gate', 'ffn1_w_up', 'ffn1_w_down', 'mix_norm_w', 'w_in',
                'ret_norm_w', 'ssm_lambda_re', 'ssm_lambda_im', 'ssm_log_dt', 'ssm_b_re', 'ssm_b_im', 'ssm_c_re',
                'ssm_c_im', 'ssm_d', 'ssm_glu_w', 'ssm_glu_b', 'ssm_norm_w', 'w_out', 'ffn2_norm_w', 'ffn2_w_gate',
                'ffn2_w_up', 'ffn2_w_down', 'final_norm_w']
BIG = ['ffn1_w_gate', 'ffn1_w_up', 'ffn1_w_down', 'w_in', 'ssm_glu_w', 'w_out', 'ffn2_w_gate', 'ffn2_w_up',
       'ffn2_w_down']
TRANSPOSED = ['ffn1_w_gate', 'ffn1_w_up', 'ffn2_w_gate', 'ffn2_w_up']
BIG_EARLY = ['ffn1_w_gate', 'ffn1_w_up', 'ffn1_w_down']
BIG_LATE = [n for n in BIG if n not in BIG_EARLY]
SMALL = [n for n in WEIGHT_NAMES if n not in BIG]


def kernel(x, meta_tokens, ffn1_norm_w, ffn1_w_gate, ffn1_w_up, ffn1_w_down, mix_norm_w, w_in, ret_norm_w, ssm_lambda_re, ssm_lambda_im, ssm_log_dt, ssm_b_re, ssm_b_im, ssm_c_re, ssm_c_im, ssm_d, ssm_glu_w, ssm_glu_b, ssm_norm_w, w_out, ffn2_norm_w, ffn2_w_gate, ffn2_w_up, ffn2_w_down, final_norm_w, loss_target, m_meta_tokens, m_ffn1_norm_w, m_ffn1_w_gate, m_ffn1_w_up, m_ffn1_w_down, m_mix_norm_w, m_w_in, m_ret_norm_w, m_ssm_lambda_re, m_ssm_lambda_im, m_ssm_log_dt, m_ssm_b_re, m_ssm_b_im, m_ssm_c_re, m_ssm_c_im, m_ssm_d, m_ssm_glu_w, m_ssm_glu_b, m_ssm_norm_w, m_w_out, m_ffn2_norm_w, m_ffn2_w_gate, m_ffn2_w_up, m_ffn2_w_down, m_final_norm_w, v_meta_tokens, v_ffn1_norm_w, v_ffn1_w_gate, v_ffn1_w_up, v_ffn1_w_down, v_mix_norm_w, v_w_in, v_ret_norm_w, v_ssm_lambda_re, v_ssm_lambda_im, v_ssm_log_dt, v_ssm_b_re, v_ssm_b_im, v_ssm_c_re, v_ssm_c_im, v_ssm_d, v_ssm_glu_w, v_ssm_glu_b, v_ssm_norm_w, v_w_out, v_ffn2_norm_w, v_ffn2_w_gate, v_ffn2_w_up, v_ffn2_w_down, v_final_norm_w):
    args = locals()
    w = {n: args[n] for n in WEIGHT_NAMES}
    m = {n: args["m_" + n] for n in WEIGHT_NAMES}
    v = {n: args["v_" + n] for n in WEIGHT_NAMES}

    seq, d = x.shape[1], x.shape[2]
    lp = seq + CHUNK
    seg_len = lp // N_SEG
    rw = RET_HEADS * HEAD_DIM
    sw = ssm_d.shape[-1]
    groups = sw // SSM_GROUP
    ns = groups * SSM_STATE
    jb = _tile(seg_len, S5_STEPS, 8)
    chip = 2 * lax.axis_index("x") + lax.axis_index("y")

    as_fd = lambda t: jnp.swapaxes(t, -1, -2)
    shards = {n: (as_fd(w[n][0]) if n in TRANSPOSED else w[n][0]).astype(BF16) for n in BIG}
    early = [shards[n] for n in BIG_EARLY] + [meta_tokens]
    gathered = _gather_two_level("gather_early", early)
    gw = dict(zip(BIG_EARLY, gathered[:-1]))
    meta_full = jnp.transpose(gathered[-1], (1, 0, 2)).reshape(N_META, d)
    late = [shards[n] for n in BIG_LATE]

    freqs = 1.0 / (ROPE_BASE ** (jnp.arange(0, HEAD_DIM, 2, dtype=F32) / HEAD_DIM))
    ang_c = (jnp.arange(lp // CHUNK, dtype=F32) * CHUNK - float(CHUNK - N_META))[:, None] * freqs[None, :]
    ang_r = jnp.arange(CHUNK, dtype=F32)[:, None] * freqs[None, :]
    cos_c, sin_c = jnp.cos(ang_c)[:, None, :], jnp.sin(ang_c)[:, None, :]
    cos_r, sin_r = jnp.cos(ang_r)[None], jnp.sin(ang_r)[None]
    cos_t = (cos_c * cos_r - sin_c * sin_r).reshape(lp, HEAD_DIM // 2)
    sin_t = (sin_c * cos_r + cos_c * sin_r).reshape(lp, HEAD_DIM // 2)
    cosf = jnp.concatenate([cos_t, cos_t], axis=1)
    sinf = jnp.concatenate([-sin_t, sin_t], axis=1)
    tables = _retention_tables(_tile(lp, RET_ROWS, CHUNK))

    lam_re, lam_im, log_dt = ssm_lambda_re[0], ssm_lambda_im[0], ssm_log_dt[0]
    b_re, b_im, c_re, c_im = ssm_b_re[0], ssm_b_im[0], ssm_c_re[0], ssm_c_im[0]
    (ar, ai, bbr, bbi), prep_vjp = jax.vjp(_s5_prepare, lam_re, lam_im, log_dt, b_re, b_im)
    dt = jnp.exp(log_dt)[:, None]
    el = jnp.exp(seg_len * lam_re * dt)
    alr = el * jnp.cos(seg_len * lam_im * dt)
    ali = el * jnp.sin(seg_len * lam_im * dt)
    bc8 = lambda t: jnp.broadcast_to(t.reshape(1, ns), (N_SEG, ns))
    a8r, a8i, al8r, al8i = bc8(ar), bc8(ai), bc8(alr), bc8(ali)
    bsr = _blockdiag_in(jnp.transpose(bbr, (0, 2, 1)))
    bsi = _blockdiag_in(jnp.transpose(bbi, (0, 2, 1)))
    csrt = _blockdiag_in(c_re)
    csit = _blockdiag_in(-c_im)
    tr = lambda t: jnp.transpose(t, (0, 2, 1))
    bsr_b, bsi_b = bsr.astype(BF16), bsi.astype(BF16)
    csr_b, csi_b = tr(csrt).astype(BF16), tr(csit).astype(BF16)
    bsrt_b, bsit_b = tr(bsr).astype(BF16), tr(bsi).astype(BF16)
    csrt_b, csit_b = csrt.astype(BF16), csit.astype(BF16)

    h0 = (jnp.concatenate([jnp.zeros((CHUNK - N_META, d), F32), meta_full], axis=0), x[0])
    (h1, g1, u1), late_half = _ffn_fwd("ffn1_fwd", h0, ffn1_norm_w, gw['ffn1_w_gate'], gw['ffn1_w_up'],
                                       gw['ffn1_w_down'], _allgather_chips_plan(late), late)
    gw.update(zip(BIG_LATE, _forward_sibling("gather_late_forward", late_half)))
    glu_full = gw['ssm_glu_w'].reshape(sw, sw)
    n2, q, k, vv, gate, u = _inproj_fwd(h1, mix_norm_w, gw['w_in'], cosf, sinf, rw)
    o, ret, sprev = _ret_fwd(q, k, vv, gate, ret_norm_w, tables)
    u_seg = _to_segments(u, seg_len)
    xr, xi, c0r, c0i, yp, ssm_seg = _s5_fwd(u_seg, bsr_b, bsi_b, csr_b, csi_b, a8r, a8i, al8r, al8i,
                                            ssm_d, glu_full, ssm_glu_b, ssm_norm_w, jb)
    ssm = _from_segments(ssm_seg, seg_len)
    h2 = _outproj_fwd(h1, ret, ssm, gw['w_out'])
    (dh3, g2, u2, loss_part, d_final), _ = _ffn_fwd(
        "ffn2_fwd_loss", h2, ffn2_norm_w, gw['ffn2_w_gate'], gw['ffn2_w_up'], gw['ffn2_w_down'],
        loss=(final_norm_w.reshape(1, d), loss_target[0]))

    (dh2, d_ffn2_norm, nb, daccb, ab, dgb, dub), _ = _ffn_bwd_act(
        "ffn2_bwd_act", dh3, h2, ffn2_norm_w, g2, u2, gw['ffn2_w_gate'], gw['ffn2_w_up'], gw['ffn2_w_down'])
    (dwg2, dwu2, dwd2), _ = _ffn_bwd_w("ffn2_bwd_w", nb, daccb, ab, dgb, dub)
    dret, dssm, dwo = _outproj_bwd(dh2, ret, ssm, gw['w_out'])
    (du_seg, d_ssm_norm, d_glu_w, d_glu_b, d_ssm_d, dcr_s, dci_s, dbr_s, dbi_s, dar8, dai8) = _s5_bwd(
        _to_segments(dssm, seg_len), u_seg, yp, xr, xi, c0r, c0i, bsrt_b, bsit_b, csrt_b, csit_b,
        a8r, a8i, al8r, al8i, ssm_d, glu_full, ssm_glu_b, ssm_norm_w, jb)
    du = _from_segments(du_seg, seg_len)
    dq, dk, dv, dgate, d_ret_norm = _ret_bwd(dret, q, k, vv, gate, o, sprev, ret_norm_w, tables, cosf, sinf)
    dh1, d_mix_norm, dwin = _inproj_bwd(dh2, h1, mix_norm_w, n2, gw['w_in'], dq, dk, dv, dgate, du)
    late_parts = {
        'w_in': dwin, 'ssm_glu_w': d_glu_w.reshape(N_CHIP, sw // N_CHIP, sw).astype(BF16), 'w_out': dwo,
        'ffn2_w_gate': dwg2, 'ffn2_w_up': dwu2, 'ffn2_w_down': dwd2,
    }
    late_list = [late_parts[n] for n in BIG_LATE]
    (dh0, d_ffn1_norm, nb, daccb, ab, dgb, dub), late_recv = _ffn_bwd_act(
        "ffn1_bwd_act", dh1, h0, ffn1_norm_w, g1, u1, gw['ffn1_w_gate'], gw['ffn1_w_up'], gw['ffn1_w_down'],
        _alltoall_chips_plan(late_list), late_list)
    grad_x = dh0[CHUNK:][None]
    d_meta = dh0[CHUNK - N_META:CHUNK]

    d_c_re = _blockdiag_out(tr(dcr_s), groups, SSM_GROUP, SSM_STATE)
    d_c_im = -_blockdiag_out(tr(dci_s), groups, SSM_GROUP, SSM_STATE)
    d_bbr = jnp.transpose(_blockdiag_out(dbr_s, groups, SSM_GROUP, SSM_STATE), (0, 2, 1))
    d_bbi = jnp.transpose(_blockdiag_out(dbi_s, groups, SSM_GROUP, SSM_STATE), (0, 2, 1))
    d_ar = jnp.sum(dar8, axis=0).reshape(groups, SSM_STATE)
    d_ai = jnp.sum(dai8, axis=0).reshape(groups, SSM_STATE)
    small_parts = [loss_part[0:1, :], d_meta, d_ffn1_norm, d_mix_norm, d_ret_norm, d_ar, d_ai, d_bbr, d_bbi,
                   d_c_re, d_c_im, d_ssm_d, d_glu_b, d_ssm_norm, d_ffn2_norm, d_final]
    small_shapes = [a.shape for a in small_parts]
    packed = _pack(small_parts)
    early_recv, (all_parts,) = _ffn_bwd_w_scatter("ffn1_bwd_w", nb, daccb, ab, dgb, dub, chip,
                                                  _allgather_all_plan([packed]), [packed])
    received = dict(zip(BIG_LATE + BIG_EARLY, late_recv + early_recv))
    ffn_names = [n for n in BIG if n.startswith('ffn')]
    chip_sum = dict(zip(ffn_names, _sum_slots("sum_chips_ffn", [received[n] for n in ffn_names], BF16)))
    for n in BIG:
        if n not in chip_sum:
            chip_sum[n] = _sum_slots("sum_chips_" + n, [received[n]], BF16)[0]
    chip_sums = [chip_sum[n] for n in BIG]
    sib_sums = _swap_sibling("swap_sibling", chip_sums)
    (loss_row, g_meta_full, g_ffn1_norm, g_mix_norm, g_ret_norm, g_ar, g_ai, g_bbr, g_bbi, g_c_re, g_c_im,
     g_ssm_d, g_glu_b, g_ssm_norm, g_ffn2_norm, g_final) = _unpack(_sum_slots("sum_small", [all_parts], F32)[0],
                                                                  small_shapes)
    g_lam_re, g_lam_im, g_log_dt, g_b_re, g_b_im = prep_vjp((g_ar, g_ai, g_bbr, g_bbi))
    loss = loss_row[0, 0]
    g_meta = lax.dynamic_slice(g_meta_full, (0, chip * (d // N_CHIP)), (N_META, d // N_CHIP))
    small_grads = {
        'meta_tokens': g_meta, 'ffn1_norm_w': g_ffn1_norm, 'mix_norm_w': g_mix_norm, 'ret_norm_w': g_ret_norm,
        'ssm_lambda_re': g_lam_re[None], 'ssm_lambda_im': g_lam_im[None], 'ssm_log_dt': g_log_dt[None],
        'ssm_b_re': g_b_re[None], 'ssm_b_im': g_b_im[None], 'ssm_c_re': g_c_re[None], 'ssm_c_im': g_c_im[None],
        'ssm_d': g_ssm_d, 'ssm_glu_b': g_glu_b, 'ssm_norm_w': g_ssm_norm, 'ffn2_norm_w': g_ffn2_norm,
        'final_norm_w': g_final.reshape(d),
    }

    grads, deltas, new_m, new_v = {}, {}, {}, {}
    g_pair = {n: [mine, sib] for n, mine, sib in zip(BIG, chip_sums, sib_sums)}
    view = lambda n, t: as_fd(t) if n in TRANSPOSED else t
    ffn_out = _adam("adam_ffn", [(view(n, w[n]), view(n, m[n]), view(n, v[n])) for n in ffn_names],
                    [g_pair[n] for n in ffn_names])
    for n, outs in zip(ffn_names, ffn_out):
        grads[n], deltas[n], new_m[n], new_v[n] = [view(n, t) for t in outs]
    for n in BIG:
        if n not in ffn_names:
            grads[n], deltas[n], new_m[n], new_v[n] = _adam("adam_" + n, [(w[n], m[n], v[n])], [g_pair[n]])[0]
    sm_shapes = [w[n].shape for n in SMALL]
    sm_out = _adam("adam_small", [(_pack([w[n] for n in SMALL]), _pack([m[n] for n in SMALL]),
                                  _pack([v[n] for n in SMALL]))],
                   [[_pack([small_grads[n].reshape(w[n].shape) for n in SMALL])]])[0]
    for dst, packed in zip((grads, deltas, new_m, new_v), sm_out):
        for n, t in zip(SMALL, _unpack(packed, sm_shapes)):
            dst[n] = t

    return (loss, grad_x, *[grads[n] for n in WEIGHT_NAMES], *[deltas[n] for n in WEIGHT_NAMES],
            *[new_m[n] for n in WEIGHT_NAMES], *[new_v[n] for n in WEIGHT_NAMES])
```

```python
import functools
import math

import jax
import jax.numpy as jnp
from jax import lax
from jax.experimental import pallas as pl
from jax.experimental.pallas import tpu as pltpu

N_META = 16
RET_HEADS = 4
HEAD_DIM = 128
SSM_GROUP = 16
SSM_STATE = 64
CHUNK = 128
ROPE_BASE = 10000.0
EPS = 1e-6
FFN_RES = 0.5
N_SEG = 8
N_SEC = 4
N_CHIP = 4
LANE = 128
FFN_CPS = 2
BWD_W_ROWS = 1664

ADAM_LR = 0.001
ADAM_B1 = 0.9
ADAM_B2 = 0.999
ADAM_EPS = 1e-08
ADAM_WD = 0.01
ADAM_STEP = 10

VMEM_LIMIT = 56 * 1024 * 1024

F32 = jnp.float32
BF16 = jnp.bfloat16
MESH = pl.DeviceIdType.MESH


def _dot(a, b):
    return jnp.dot(a, b, preferred_element_type=F32)


def _dot_nt(a, b):
    return lax.dot_general(a, b, (((1,), (1,)), ((), ())), preferred_element_type=F32)


def _dot_tn(a, b):
    return lax.dot_general(a, b, (((0,), (0,)), ((), ())), preferred_element_type=F32)


def _tile(n, target, mult=64):
    best = None
    t = mult
    while t <= min(n, target):
        if n % t == 0:
            best = t
        t += mult
    assert best is not None, (n, target)
    return best


def _params(sem, vmem=VMEM_LIMIT):
    return pltpu.CompilerParams(dimension_semantics=sem, vmem_limit_bytes=vmem)


def _rms_stats(xf):
    r = lax.rsqrt(jnp.mean(xf * xf, axis=-1, keepdims=True) + EPS)
    return xf * r, r


def _rms_bwd(dy, xh, r, w):
    dxh = dy * w
    return r * (dxh - xh * jnp.mean(dxh * xh, axis=-1, keepdims=True))


def _sigmoid(x):
    return 0.5 * jnp.tanh(0.5 * x) + 0.5


GELU_K0 = math.sqrt(2.0 / math.pi)
GELU_K1 = 0.044715


CHIP_MASKS = [(1, 0, 0), (0, 1, 0), (1, 1, 0)]
ALL_MASKS = [(0, 0, 1), (0, 1, 0), (0, 1, 1), (1, 0, 0), (1, 0, 1), (1, 1, 0), (1, 1, 1)]
SIB_MASKS = [(0, 0, 1)]
ANY_SPEC = pl.BlockSpec(memory_space=pl.ANY)


class _Plan:
    def __init__(self, arrays, masks, n_slots, src_slotted, dst_slotted, local_copy, half=False, forward=False):
        self.shapes = [(a.shape, a.dtype) for a in arrays]
        self.n = len(arrays)
        self.masks = masks
        self.n_slots = n_slots
        self.src_slotted, self.dst_slotted, self.local_copy = src_slotted, dst_slotted, local_copy
        self.half, self.forward = half, forward
        self.n_cp = self.n * len(masks) * (len(CHIP_MASKS) if forward else 1)

    def out_shape(self):
        out = []
        for shp, dt in self.shapes:
            if self.dst_slotted and not self.src_slotted:
                shp = (self.n_slots,) + shp
            elif self.src_slotted and not self.dst_slotted:
                shp = shp[1:]
            out.append(jax.ShapeDtypeStruct(shp, dt))
        return tuple(out)

    def scratch(self):
        return [pltpu.SemaphoreType.DMA((self.n_cp,)), pltpu.SemaphoreType.DMA((self.n_cp,)),
                pltpu.SemaphoreType.DMA((self.n,))]

    def _slot(self, px, py, pc):
        if self.n_slots == 8:
            return 4 * px + 2 * py + pc
        if self.n_slots == 4:
            return 2 * px + py
        return pc

    def copies(self, ins, outs, sems):
        send_sems, recv_sems, loc_sems = sems
        x, y, c = lax.axis_index("x"), lax.axis_index("y"), lax.axis_index("c")
        me = self._slot(x, y, c)
        n_m = len(self.masks)
        cps = []
        for a in range(self.n):
            if self.forward:
                rows = self.shapes[a][0][-2] // 2
                mine = pl.ds(pl.multiple_of(c * rows, 8), rows)
                for j, (mx, my, _) in enumerate(CHIP_MASKS):
                    blk = outs[a].at[2 * (1 - x if mx else x) + (1 - y if my else y), mine]
                    k = a * len(CHIP_MASKS) + j
                    cps.append(pltpu.make_async_remote_copy(
                        src_ref=blk, dst_ref=blk, send_sem=send_sems.at[k], recv_sem=recv_sems.at[k],
                        device_id=(x, y, 1 - c), device_id_type=MESH))
                continue
            if self.local_copy:
                src = ins[a].at[me] if self.src_slotted else ins[a]
                cps.append(pltpu.make_async_copy(src, outs[a].at[me], loc_sems.at[a]))
            for mi, (mx, my, mc) in enumerate(self.masks):
                px = 1 - x if mx else x
                py = 1 - y if my else y
                pc = 1 - c if mc else c
                src = ins[a].at[self._slot(px, py, pc)] if self.src_slotted else ins[a]
                dst = outs[a].at[me] if self.dst_slotted else outs[a]
                if self.half:
                    rows = src.shape[-2] // 2
                    mine = pl.ds(pl.multiple_of(c * rows, 8), rows)
                    src, dst = src.at[mine], dst.at[mine]
                k = a * n_m + mi
                cps.append(pltpu.make_async_remote_copy(
                    src_ref=src, dst_ref=dst, send_sem=send_sems.at[k], recv_sem=recv_sems.at[k],
                    device_id=(px, py, pc), device_id_type=MESH))
        return cps


def _exchange(name, plan, arrays):
    n = plan.n

    def body(*refs):
        cps = plan.copies(refs[:n], refs[n:2 * n], refs[2 * n:])
        for cp in cps:
            cp.start()
        for cp in cps:
            cp.wait()

    outs = pl.pallas_call(
        body, name=name, out_shape=plan.out_shape(),
        in_specs=[ANY_SPEC] * n, out_specs=tuple([ANY_SPEC] * n), scratch_shapes=plan.scratch(),
        input_output_aliases={i: i for i in range(n)} if plan.forward else {},
    )(*arrays)
    return list(outs)


def _pcall(body, *, name, grid, in_specs, out_specs, out_shape, scratch_shapes, args, plan=None, plan_args=()):
    sem = ("arbitrary",) * len(grid)
    if plan is None:
        return pl.pallas_call(body, name=name, grid=grid, in_specs=in_specs, out_specs=out_specs,
                              out_shape=out_shape, scratch_shapes=scratch_shapes,
                              compiler_params=_params(sem))(*args), []
    n_in, n_out, n_scr, n_p = len(in_specs), len(out_specs), len(scratch_shapes), plan.n

    def wrapped(*refs):
        ins = refs[:n_in]
        p_ins = refs[n_in:n_in + n_p]
        o0 = n_in + n_p
        outs = refs[o0:o0 + n_out]
        p_outs = refs[o0 + n_out:o0 + n_out + n_p]
        s0 = o0 + n_out + n_p
        scr = refs[s0:s0 + n_scr]
        sems = refs[s0 + n_scr:]
        ids = [pl.program_id(i) for i in range(len(grid))]
        first = functools.reduce(jnp.logical_and, [i == 0 for i in ids])
        last = functools.reduce(jnp.logical_and, [i == g - 1 for i, g in zip(ids, grid)])

        @pl.when(first)
        def _():
            for cp in plan.copies(p_ins, p_outs, sems):
                cp.start()

        body(*ins, *outs, *scr)

        @pl.when(last)
        def _():
            for cp in plan.copies(p_ins, p_outs, sems):
                cp.wait()

    res = pl.pallas_call(
        wrapped, name=name, grid=grid,
        in_specs=list(in_specs) + [ANY_SPEC] * n_p,
        out_specs=tuple(out_specs) + (ANY_SPEC,) * n_p,
        out_shape=tuple(out_shape) + plan.out_shape(),
        scratch_shapes=list(scratch_shapes) + plan.scratch(),
        compiler_params=_params(sem),
    )(*args, *plan_args)
    return res[:n_out], list(res[n_out:])


def _allgather_chips_plan(arrays):
    return _Plan(arrays, CHIP_MASKS, 4, False, True, True, half=True)


def _gather_two_level(name, arrays):
    n = len(arrays)
    ici = _allgather_chips_plan(arrays)
    fwd = _Plan(ici.out_shape(), SIB_MASKS, 4, True, True, False, forward=True)
    n_m = len(CHIP_MASKS)

    def body(*refs):
        ins, outs, sems = refs[:n], refs[n:2 * n], refs[2 * n:]
        ici_cps = ici.copies(ins, outs, sems[:3])
        fwd_cps = fwd.copies(None, outs, sems[3:])
        for cp in ici_cps:
            cp.start()
        for a in range(n):
            for m in range(n_m):
                ici_cps[a * (n_m + 1) + 1 + m].wait_recv()
                fwd_cps[a * n_m + m].start()
        for a in range(n):
            ici_cps[a * (n_m + 1)].wait()
            for m in range(n_m):
                ici_cps[a * (n_m + 1) + 1 + m].wait_send()
        for cp in fwd_cps:
            cp.wait()

    return list(pl.pallas_call(
        body, name=name, out_shape=ici.out_shape(),
        in_specs=[ANY_SPEC] * n, out_specs=tuple([ANY_SPEC] * n), scratch_shapes=ici.scratch() + fwd.scratch(),
    )(*arrays))


def _forward_sibling(name, gathered):
    return _exchange(name, _Plan(gathered, SIB_MASKS, 4, True, True, False, forward=True), gathered)


def _alltoall_chips_plan(arrays):
    return _Plan(arrays, CHIP_MASKS, 4, True, True, True)


def _swap_sibling(name, arrays):
    return _exchange(name, _Plan(arrays, SIB_MASKS, 2, False, False, False), arrays)


def _allgather_all_plan(arrays):
    return _Plan(arrays, ALL_MASKS, 8, False, True, True)


def _sum_slots(name, arrs, out_dtype):
    s, r, c = arrs[0].shape
    n = len(arrs)
    tr = _tile(r, 512 if n == 1 else 176, 8)

    def body(*refs):
        for a_ref, o_ref in zip(refs[:n], refs[n:]):
            acc = a_ref[0].astype(F32)
            for i in range(1, s):
                acc = acc + a_ref[i].astype(F32)
            o_ref[...] = acc.astype(out_dtype)

    return list(pl.pallas_call(
        body, name=name, grid=(r // tr,),
        in_specs=[pl.BlockSpec((s, tr, c), lambda i: (0, i, 0))] * n,
        out_specs=(pl.BlockSpec((tr, c), lambda i: (i, 0)),) * n,
        out_shape=(jax.ShapeDtypeStruct((r, c), out_dtype),) * n,
        compiler_params=_params(("arbitrary",)),
    )(*arrs))


def _adam_math(w, g, m, v):
    m_new = ADAM_B1 * m + (1.0 - ADAM_B1) * g
    v_new = ADAM_B2 * v + (1.0 - ADAM_B2) * (g * g)
    m_hat = m_new / (1.0 - ADAM_B1 ** ADAM_STEP)
    v_hat = v_new / (1.0 - ADAM_B2 ** ADAM_STEP)
    delta = -ADAM_LR * (m_hat / (jnp.sqrt(v_hat) + ADAM_EPS) + ADAM_WD * w)
    return delta, m_new, v_new


def _adam(name, wmv, g_parts):
    w0 = wmv[0][0]
    r, c = w0.shape[-2:]
    n_w = len(wmv)
    n_g = len(g_parts[0])
    tr = _tile(r, 256 if n_w == 1 else 88, 8)
    lead = w0.ndim == 3
    at = (lambda ref: ref.at[0]) if lead else (lambda ref: ref)
    n_in = 3 + n_g

    def body(*refs):
        for j in range(n_w):
            ins = refs[j * n_in:(j + 1) * n_in]
            outs = refs[n_w * n_in + 4 * j:n_w * n_in + 4 * j + 4]
            w_ref, m_ref, v_ref = [at(t) for t in ins[:3]]
            g_out, d_out, m_out, v_out = [at(t) for t in outs]
            g = ins[3][...].astype(F32)
            for gr in ins[4:]:
                g = g + gr[...].astype(F32)
            delta, m_new, v_new = _adam_math(w_ref[...], g, m_ref[...], v_ref[...])
            g_out[...] = g
            d_out[...] = delta
            m_out[...] = m_new
            v_out[...] = v_new

    spec = pl.BlockSpec((tr, c), lambda i: (i, 0))
    wspec = pl.BlockSpec((1, tr, c), lambda i: (0, i, 0)) if lead else spec
    shp = jax.ShapeDtypeStruct(w0.shape, F32)
    args = [t for (w, m, v), gp in zip(wmv, g_parts) for t in (w, m, v, *gp)]
    res = pl.pallas_call(
        body, name=name, grid=(r // tr,),
        in_specs=([wspec] * 3 + [spec] * n_g) * n_w, out_specs=(wspec,) * (4 * n_w), out_shape=(shp,) * (4 * n_w),
        compiler_params=_params(("arbitrary",)),
    )(*args)
    return [tuple(res[4 * j:4 * j + 4]) for j in range(n_w)]


SUB_ROWS = 32
FFN_BWD_ROWS = 416
FFN_FWD_ROWS = 832
FFN_LOSS_ROWS = 640
RET_ROWS = 640
S5_STEPS = 104


def _tile_parts(tm, d, head, x):
    nsub = tm // SUB_ROWS
    off = head.shape[0] // SUB_ROWS
    specs = [pl.BlockSpec(head.shape, lambda i, k: (0, 0))] + [
        pl.BlockSpec((SUB_ROWS, d), lambda i, k, j=j: (jnp.maximum(i * nsub + j - off, 0), 0)) for j in range(nsub)]

    def assemble(i, part_refs, h_sc):
        head_ref, x_refs = part_refs[0], part_refs[1:]
        for j in range(nsub):
            rows = slice(j * SUB_ROWS, (j + 1) * SUB_ROWS)
            val = x_refs[j][...]
            if j < off:
                val = jnp.where(i == 0, head_ref[rows, :], val)
            h_sc[rows, :] = val

    return specs, [head] + [x] * nsub, assemble


def _h_source(body, h, tm, d):
    if not isinstance(h, tuple):
        return body, [pl.BlockSpec((tm, d), lambda i, k: (i, 0))], [h], []
    specs, args, assemble = _tile_parts(tm, d, *h)
    n_h = len(specs)

    def with_parts(*refs):
        h_sc = refs[-1]

        @pl.when(pl.program_id(1) == 0)
        def _():
            assemble(pl.program_id(0), refs[:n_h], h_sc)

        body(h_sc, *refs[n_h:-1])

    return with_parts, specs, args, [pltpu.VMEM((tm, d), F32)]


def _ffn_fwd(name, h, nw, wg, wu, wd, plan=None, plan_args=(), loss=None):
    lp, d = (h[0].shape[0] + h[1].shape[0], h[1].shape[1]) if isinstance(h, tuple) else h.shape
    nck, f, _ = wg.shape
    tm = _tile(lp, FFN_FWD_ROWS if loss is None else FFN_LOSS_ROWS)
    last = nck // FFN_CPS - 1
    n_t = 0
    if loss is not None:
        t_specs, t_args, t_assemble = _tile_parts(tm, d, jnp.zeros((lp - loss[1].shape[0], d), F32), loss[1])
        n_t = len(t_specs)

    def body(h_ref, nw_ref, wg_ref, wu_ref, wd_ref, *rest):
        if loss is not None:
            fw_ref, t_parts, rest = rest[0], rest[1:1 + n_t], rest[1 + n_t:]
            ho_ref, g_ref, u_ref, loss_ref, dfw_ref, n_sc, acc_sc, t_sc = rest
        else:
            ho_ref, g_ref, u_ref, n_sc, acc_sc = rest
        i = pl.program_id(0)
        k = pl.program_id(1)

        @pl.when(k == 0)
        def _():
            xh, _ = _rms_stats(h_ref[...])
            n_sc[...] = (xh * nw_ref[...]).astype(BF16)
            acc_sc[...] = jnp.zeros_like(acc_sc)

        n = n_sc[...]
        acc = acc_sc[...]
        for c in range(FFN_CPS):
            g = _dot_nt(n, wg_ref[c])
            u = _dot_nt(n, wu_ref[c])
            g_ref[c] = g.astype(BF16)
            u_ref[c] = u.astype(BF16)
            a = (g * _sigmoid(g) * u).astype(BF16)
            acc = acc + _dot(a, wd_ref[c])
        acc_sc[...] = acc

        if loss is None:
            @pl.when(k == last)
            def _():
                ho_ref[...] = h_ref[...] + FFN_RES * acc_sc[...]
            return

        @pl.when(jnp.logical_and(i == 0, k == 0))
        def _():
            loss_ref[...] = jnp.zeros_like(loss_ref)
            dfw_ref[...] = jnp.zeros_like(dfw_ref)

        @pl.when(k == last)
        def _():
            t_assemble(i, t_parts, t_sc)
            xh, r = _rms_stats(h_ref[...] + FFN_RES * acc_sc[...])
            w = fw_ref[...]
            head_rows = lp - loss[1].shape[0]
            row = lax.broadcasted_iota(jnp.int32, (tm, d), 0) + i * tm
            err = jnp.where(row < head_rows, 0.0, xh * w - t_sc[...])
            loss_ref[...] += 0.5 * jnp.sum(err * err) / d
            dout = err * (1.0 / d)
            dfw_ref[...] += jnp.sum(dout * xh, axis=0, keepdims=True)
            ho_ref[...] = _rms_bwd(dout, xh, r, w)

    body, h_specs, h_args, h_scratch = _h_source(body, h, tm, d)
    vec = pl.BlockSpec((1, d), lambda i, k: (0, 0))
    w_fd = pl.BlockSpec((FFN_CPS, f, d), lambda i, k: (k, 0, 0))
    hid = pl.BlockSpec((FFN_CPS, tm, f), lambda i, k: (k, i, 0))
    hshape = jax.ShapeDtypeStruct((nck, lp, f), BF16)
    args, in_specs = (*h_args, nw, wg, wu, wd), h_specs + [vec, w_fd, w_fd, w_fd]
    out_specs = (pl.BlockSpec((tm, d), lambda i, k: (i, 0)), hid, hid)
    out_shape = (jax.ShapeDtypeStruct((lp, d), F32), hshape, hshape)
    scratch = [pltpu.VMEM((tm, d), BF16), pltpu.VMEM((tm, d), F32)]
    if loss is not None:
        args, in_specs = (*args, loss[0], *t_args), in_specs + [vec] + t_specs
        out_specs += (pl.BlockSpec((8, LANE), lambda i, k: (0, 0)), vec)
        out_shape += (jax.ShapeDtypeStruct((8, LANE), F32), jax.ShapeDtypeStruct((1, d), F32))
        scratch = scratch + [pltpu.VMEM((tm, d), F32)]
    return _pcall(
        body, name=name, grid=(lp // tm, nck // FFN_CPS), plan=plan, plan_args=plan_args,
        args=args, in_specs=in_specs, out_specs=out_specs, out_shape=out_shape,
        scratch_shapes=scratch + h_scratch)


def _ffn_bwd_act(name, dh, h, nw, g, u, wg, wu, wd, plan=None, plan_args=()):
    lp, d = dh.shape
    nck, f, _ = wg.shape
    tm = _tile(lp, FFN_BWD_ROWS, SUB_ROWS)
    last = nck // FFN_CPS - 1

    def body(h_ref, dh_ref, nw_ref, g_ref, u_ref, wg_ref, wu_ref, wd_ref,
             dhi_ref, dnw_ref, n_ref, dacc_ref, a_ref, dg_ref, du_ref,
             xh_sc, r_sc, dn_sc):
        i = pl.program_id(0)
        k = pl.program_id(1)

        @pl.when(k == 0)
        def _():
            xh, r = _rms_stats(h_ref[...])
            xh_sc[...] = xh
            r_sc[...] = r
            n_ref[...] = (xh * nw_ref[...]).astype(BF16)
            dacc_ref[...] = (FFN_RES * dh_ref[...]).astype(BF16)
            dn_sc[...] = jnp.zeros_like(dn_sc)

        @pl.when(jnp.logical_and(i == 0, k == 0))
        def _():
            dnw_ref[...] = jnp.zeros_like(dnw_ref)

        dacc = dacc_ref[...]
        dn = dn_sc[...]
        for c in range(FFN_CPS):
            gv = g_ref[c].astype(F32)
            uv = u_ref[c].astype(F32)
            sg = _sigmoid(gv)
            sil = gv * sg
            da = _dot_nt(dacc, wd_ref[c])
            dgk = (da * uv * (sg * (1.0 + gv * (1.0 - sg)))).astype(BF16)
            duk = (da * sil).astype(BF16)
            a_ref[c] = (sil * uv).astype(BF16)
            dg_ref[c] = dgk
            du_ref[c] = duk
            dn = dn + _dot(dgk, wg_ref[c]) + _dot(duk, wu_ref[c])
        dn_sc[...] = dn

        @pl.when(k == last)
        def _():
            dnl = dn_sc[...]
            xh = xh_sc[...]
            dhi_ref[...] = dh_ref[...] + _rms_bwd(dnl, xh, r_sc[...], nw_ref[...])
            dnw_ref[...] += jnp.sum(dnl * xh, axis=0, keepdims=True)

    body, h_specs, h_args, h_scratch = _h_source(body, h, tm, d)
    row = pl.BlockSpec((tm, d), lambda i, k: (i, 0))
    vec = pl.BlockSpec((1, d), lambda i, k: (0, 0))
    hid = pl.BlockSpec((FFN_CPS, tm, f), lambda i, k: (k, i, 0))
    w_fd = pl.BlockSpec((FFN_CPS, f, d), lambda i, k: (k, 0, 0))
    rshape = jax.ShapeDtypeStruct((lp, d), BF16)
    hshape = jax.ShapeDtypeStruct((nck, lp, f), BF16)
    return _pcall(
        body, name=name, grid=(lp // tm, nck // FFN_CPS), plan=plan, plan_args=plan_args,
        args=(*h_args, dh, nw, g, u, wg, wu, wd),
        in_specs=h_specs + [row, vec, hid, hid, w_fd, w_fd, w_fd],
        out_specs=(row, vec, row, row, hid, hid, hid),
        out_shape=(jax.ShapeDtypeStruct((lp, d), F32), jax.ShapeDtypeStruct((1, d), F32),
                   rshape, rshape, hshape, hshape, hshape),
        scratch_shapes=[pltpu.VMEM((tm, d), F32), pltpu.VMEM((tm, 1), F32), pltpu.VMEM((tm, d), F32)] + h_scratch)


def _ffn_bwd_w(name, n, dacc, a, dg, du, plan=None, plan_args=()):
    lp, d = n.shape
    nck, _, f = a.shape
    tm = _tile(lp, BWD_W_ROWS)
    last = lp // tm - 1

    def body(n_ref, dacc_ref, a_ref, dg_ref, du_ref, dwg_ref, dwu_ref, dwd_ref, ag_sc, au_sc, ad_sc):
        i = pl.program_id(1)

        @pl.when(i == 0)
        def _():
            ag_sc[...] = jnp.zeros_like(ag_sc)
            au_sc[...] = jnp.zeros_like(au_sc)
            ad_sc[...] = jnp.zeros_like(ad_sc)

        nv = n_ref[...]
        ag_sc[...] += _dot_tn(dg_ref[0], nv)
        au_sc[...] += _dot_tn(du_ref[0], nv)
        ad_sc[...] += _dot_tn(a_ref[0], dacc_ref[...])

        @pl.when(i == last)
        def _():
            dwg_ref[0] = ag_sc[...].astype(BF16)
            dwu_ref[0] = au_sc[...].astype(BF16)
            dwd_ref[0] = ad_sc[...].astype(BF16)

    row = pl.BlockSpec((tm, d), lambda k, i: (i, 0))
    hid = pl.BlockSpec((1, tm, f), lambda k, i: (k, i, 0))
    w_fd = pl.BlockSpec((1, f, d), lambda k, i: (k, 0, 0))
    wshape = jax.ShapeDtypeStruct((nck, f, d), BF16)
    return _pcall(
        body, name=name, grid=(nck, lp // tm), plan=plan, plan_args=plan_args, args=(n, dacc, a, dg, du),
        in_specs=[row, row, hid, hid, hid], out_specs=(w_fd, w_fd, w_fd), out_shape=(wshape,) * 3,
        scratch_shapes=[pltpu.VMEM((f, d), F32)] * 3)


def _scatter_mask(pos, c, nck):
    base = nck - 1 - pos
    middle = jnp.logical_and(base >= 1, base <= 2)
    return jnp.where(jnp.logical_and(c == 1, middle), 3 - base, base)


def _ffn_bwd_w_scatter(name, n, dacc, a, dg, du, chip, plan, plan_args):
    lp, d = n.shape
    nck, _, f = a.shape
    tm = _tile(lp, BWD_W_ROWS)
    last_i = lp // tm - 1
    n_w = 3
    n_p = plan.n

    def body(me_ref, n_ref, dacc_ref, a_ref, dg_ref, du_ref, *rest):
        p_ins = rest[:n_p]
        recv = rest[n_p:n_p + n_w]
        p_outs = rest[n_p + n_w:2 * n_p + n_w]
        acc = rest[2 * n_p + n_w:2 * n_p + 2 * n_w]
        stage, send_sems, recv_sems, loc_sems = rest[2 * n_p + 2 * n_w:2 * n_p + 2 * n_w + 4]
        p_sems = rest[2 * n_p + 2 * n_w + 4:]
        p = pl.program_id(0)
        i = pl.program_id(1)
        me = me_ref[0]
        c = lax.axis_index("c")

        def send(w, pos):
            kk = jnp.bitwise_xor(me, _scatter_mask(pos, c, nck))
            diff = jnp.bitwise_xor(kk, me)
            m = jnp.where(diff == 2, 0, jnp.where(diff == 1, 1, 2))
            return pltpu.make_async_remote_copy(
                src_ref=stage.at[lax.rem(pos, 2), w], dst_ref=recv[w].at[me],
                send_sem=send_sems.at[w * 3 + m], recv_sem=recv_sems.at[w * 3 + m],
                device_id=(lax.div(kk, 2), lax.rem(kk, 2), c), device_id_type=MESH)

        @pl.when(jnp.logical_and(p == 0, i == 0))
        def _():
            for cp in plan.copies(p_ins, p_outs, p_sems):
                cp.start()

        @pl.when(i == 0)
        def _():
            for t in acc:
                t[...] = jnp.zeros_like(t)

        nv = n_ref[...]
        acc[0][...] += _dot_tn(dg_ref[0], nv)
        acc[1][...] += _dot_tn(du_ref[0], nv)
        acc[2][...] += _dot_tn(a_ref[0], dacc_ref[...])

        @pl.when(jnp.logical_and(i == last_i, p >= 2))
        def _():
            for w in range(n_w):
                send(w, p - 2).wait_send()

        @pl.when(i == last_i)
        def _():
            for w in range(n_w):
                stage[lax.rem(p, 2), w] = acc[w][...].astype(BF16)

        @pl.when(jnp.logical_and(i == last_i, p < nck - 1))
        def _():
            for w in range(n_w):
                send(w, p).start()

        @pl.when(jnp.logical_and(i == last_i, p == nck - 1))
        def _():
            own = [pltpu.make_async_copy(stage.at[(nck - 1) % 2, w], recv[w].at[me], loc_sems.at[w])
                   for w in range(n_w)]
            for cp in own:
                cp.start()
            for w in range(n_w):
                send(w, nck - 2).wait_send()
            for cp in own:
                cp.wait()
            for w in range(n_w):
                for m in range(3):
                    pltpu.make_async_remote_copy(
                        src_ref=stage.at[0, w], dst_ref=recv[w].at[me],
                        send_sem=send_sems.at[w * 3 + m], recv_sem=recv_sems.at[w * 3 + m],
                        device_id=(0, 0, c), device_id_type=MESH).wait_recv()
            for cp in plan.copies(p_ins, p_outs, p_sems):
                cp.wait()

    chunk = lambda k, me_ref: jnp.bitwise_xor(me_ref[0], _scatter_mask(k, me_ref[1], nck))
    row = pl.BlockSpec((tm, d), lambda k, i, me_ref: (i, 0))
    hid = pl.BlockSpec((1, tm, f), lambda k, i, me_ref: (chunk(k, me_ref), i, 0))
    wshape = jax.ShapeDtypeStruct((nck, f, d), BF16)
    res = pl.pallas_call(
        body, name=name,
        grid_spec=pltpu.PrefetchScalarGridSpec(
            num_scalar_prefetch=1, grid=(nck, lp // tm),
            in_specs=[row, row, hid, hid, hid] + [ANY_SPEC] * n_p,
            out_specs=(ANY_SPEC,) * (n_w + n_p),
            scratch_shapes=[pltpu.VMEM((f, d), F32)] * n_w + [
                pltpu.VMEM((2, n_w, f, d), BF16), pltpu.SemaphoreType.DMA((n_w * 3,)),
                pltpu.SemaphoreType.DMA((n_w * 3,)), pltpu.SemaphoreType.DMA((n_w,))] + plan.scratch()),
        out_shape=(wshape,) * n_w + plan.out_shape(),
        compiler_params=_params(("arbitrary", "arbitrary")),
    )(jnp.stack([chip, lax.axis_index("c")]).astype(jnp.int32), n, dacc, a, dg, du, *plan_args)
    return list(res[:n_w]), list(res[n_w:])


def _inproj_fwd(h, nw, w_in, cosf, sinf, rw):
    lp, d = h.shape
    nck, _, ps = w_in.shape
    proj = nck * ps
    sw = proj - 4 * rw
    tm = _tile(lp, 640)
    scale = HEAD_DIM ** -0.5
    heads = rw // HEAD_DIM

    def body(h_ref, nw_ref, w_ref, cos_ref, sin_ref, n_ref, q_ref, k_ref, v_ref, g_ref, u_ref, p_sc):
        xh, _ = _rms_stats(h_ref[...])
        n = (xh * nw_ref[...]).astype(BF16)
        n_ref[...] = n
        for c in range(nck):
            p_sc[:, c * ps:(c + 1) * ps] = _dot(n, w_ref[c])
        cs = cos_ref[...]
        sn = sin_ref[...]
        for hh in range(heads):
            lo = hh * HEAD_DIM
            qh = p_sc[:, lo:lo + HEAD_DIM]
            q_ref[:, lo:lo + HEAD_DIM] = (qh * cs + pltpu.roll(qh, HEAD_DIM // 2, 1) * sn).astype(BF16)
            kh = p_sc[:, rw + lo:rw + lo + HEAD_DIM]
            k_ref[:, lo:lo + HEAD_DIM] = ((kh * cs + pltpu.roll(kh, HEAD_DIM // 2, 1) * sn) * scale).astype(BF16)
        v_ref[...] = p_sc[:, 2 * rw:3 * rw].astype(BF16)
        g_ref[...] = p_sc[:, 3 * rw:4 * rw]
        u_ref[...] = p_sc[:, 4 * rw:]

    row = lambda w: pl.BlockSpec((tm, w), lambda i: (i, 0))
    return pl.pallas_call(
        body, name="inproj_fwd", grid=(lp // tm,),
        in_specs=[row(d), pl.BlockSpec((1, d), lambda i: (0, 0)),
                  pl.BlockSpec((nck, d, ps), lambda i: (0, 0, 0)), row(HEAD_DIM), row(HEAD_DIM)],
        out_specs=(row(d), row(rw), row(rw), row(rw), row(rw), row(sw)),
        out_shape=(jax.ShapeDtypeStruct((lp, d), BF16),
                   jax.ShapeDtypeStruct((lp, rw), BF16),
                   jax.ShapeDtypeStruct((lp, rw), BF16),
                   jax.ShapeDtypeStruct((lp, rw), BF16),
                   jax.ShapeDtypeStruct((lp, rw), F32),
                   jax.ShapeDtypeStruct((lp, sw), F32)),
        scratch_shapes=[pltpu.VMEM((tm, proj), F32)],
        compiler_params=_params(("arbitrary",)),
    )(h, nw, w_in, cosf, sinf)


def _inproj_bwd(dh, h, nw, n, w_in, dq, dk, dv, dg, du):
    lp, d = h.shape
    nck, _, ps = w_in.shape
    rw = dq.shape[1]
    sw = du.shape[1]
    proj = nck * ps
    tm = _tile(lp, 640)
    last = lp // tm - 1

    def gather_dproj(p_sc, dq_ref, dk_ref, dv_ref, dg_ref, du_ref):
        p_sc[:, 0:rw] = dq_ref[...]
        p_sc[:, rw:2 * rw] = dk_ref[...]
        p_sc[:, 2 * rw:3 * rw] = dv_ref[...]
        p_sc[:, 3 * rw:4 * rw] = dg_ref[...]
        p_sc[:, 4 * rw:] = du_ref[...]

    def act_body(dh_ref, h_ref, nw_ref, w_ref, dq_ref, dk_ref, dv_ref, dg_ref, du_ref, dhi_ref, dnw_ref, p_sc):
        i = pl.program_id(0)

        @pl.when(i == 0)
        def _():
            dnw_ref[...] = jnp.zeros_like(dnw_ref)

        gather_dproj(p_sc, dq_ref, dk_ref, dv_ref, dg_ref, du_ref)
        dn = jnp.zeros((tm, d), F32)
        for c in range(nck):
            dn = dn + _dot_nt(p_sc[:, c * ps:(c + 1) * ps], w_ref[c])
        xh, r = _rms_stats(h_ref[...])
        dhi_ref[...] = dh_ref[...] + _rms_bwd(dn, xh, r, nw_ref[...])
        dnw_ref[...] += jnp.sum(dn * xh, axis=0, keepdims=True)

    def w_body(n_ref, dq_ref, dk_ref, dv_ref, dg_ref, du_ref, dw_ref, p_sc, acc_sc):
        i = pl.program_id(0)

        @pl.when(i == 0)
        def _():
            acc_sc[...] = jnp.zeros_like(acc_sc)

        gather_dproj(p_sc, dq_ref, dk_ref, dv_ref, dg_ref, du_ref)
        nv = n_ref[...]
        for c in range(nck):
            acc_sc[c] += _dot_tn(nv, p_sc[:, c * ps:(c + 1) * ps])

        @pl.when(i == last)
        def _():
            dw_ref[...] = acc_sc[...].astype(BF16)

    row = lambda w: pl.BlockSpec((tm, w), lambda i: (i, 0))
    vec = pl.BlockSpec((1, d), lambda i: (0, 0))
    wsp = pl.BlockSpec((nck, d, ps), lambda i: (0, 0, 0))
    dproj_specs = [row(rw), row(rw), row(rw), row(rw), row(sw)]
    dhi, dnw = pl.pallas_call(
        act_body, name="inproj_bwd_act", grid=(lp // tm,),
        in_specs=[row(d), row(d), vec, wsp] + dproj_specs,
        out_specs=(row(d), vec),
        out_shape=(jax.ShapeDtypeStruct((lp, d), F32), jax.ShapeDtypeStruct((1, d), F32)),
        scratch_shapes=[pltpu.VMEM((tm, proj), BF16)],
        compiler_params=_params(("arbitrary",)),
    )(dh, h, nw, w_in, dq, dk, dv, dg, du)
    dw = pl.pallas_call(
        w_body, name="inproj_bwd_w", grid=(lp // tm,),
        in_specs=[row(d)] + dproj_specs,
        out_specs=wsp, out_shape=jax.ShapeDtypeStruct((nck, d, ps), BF16),
        scratch_shapes=[pltpu.VMEM((tm, proj), BF16), pltpu.VMEM((nck, d, ps), F32)],
        compiler_params=_params(("arbitrary",)),
    )(n, dq, dk, dv, dg, du)
    return dhi, dnw, dw


def _retention_tables(rc):
    h = jnp.arange(RET_HEADS, dtype=F32)
    log_g = jnp.log(1.0 - 2.0 ** (-5.0 - h))
    i = jnp.arange(rc)
    diff = i[:, None] - i[None, :]
    dec = jnp.where(diff[None] >= 0,
                    jnp.exp(log_g[:, None, None] * jnp.maximum(diff, 0)[None].astype(F32)), 0.0)
    pos = jnp.arange(rc, dtype=F32)
    wq = jnp.exp(log_g[:, None] * (pos + 1.0)[None])
    wk = jnp.exp(log_g[:, None] * (rc - 1 - pos)[None])
    gch = jnp.exp(log_g * rc)
    ones = jnp.ones((1, 1, HEAD_DIM), F32)
    return (dec, wq[:, :, None] * ones, wk[:, :, None] * ones,
            gch[:, None, None] * jnp.ones((1, 8, HEAD_DIM), F32))


def _head_norm(o):
    mu = jnp.mean(o, axis=-1, keepdims=True)
    oc = o - mu
    r = lax.rsqrt(jnp.mean(oc * oc, axis=-1, keepdims=True) + EPS)
    return oc * r, r


def _ret_fwd(q, k, v, g, rnw, tables):
    lp, rw = q.shape
    heads = rw // HEAD_DIM
    rc = tables[0].shape[1]
    nch = lp // rc
    dec, wq, wk, gch = tables

    def body(q_ref, k_ref, v_ref, g_ref, w_ref, dec_ref, wq_ref, wk_ref, gch_ref,
             o_ref, ret_ref, sp_ref, s_sc):
        n = pl.program_id(0)

        @pl.when(n == 0)
        def _():
            s_sc[...] = jnp.zeros_like(s_sc)

        cols = [slice(hh * HEAD_DIM, (hh + 1) * HEAD_DIM) for hh in range(heads)]
        s_ins = [s_sc[hh] for hh in range(heads)]
        outs = []
        for hh, cs in enumerate(cols):
            qv, kv, vv = q_ref[:, cs], k_ref[:, cs], v_ref[:, cs]
            s_in = s_ins[hh]
            a = _dot_nt(qv, kv) * dec_ref[hh]
            qw = (qv.astype(F32) * wq_ref[hh]).astype(BF16)
            kw = (kv.astype(F32) * wk_ref[hh]).astype(BF16)
            o = _dot(a.astype(BF16), vv) + _dot(qw, s_in.astype(BF16))
            s_new = gch_ref[hh, 0:1, :] * s_in + _dot_tn(kw, vv)
            xh, _ = _head_norm(o)
            gv = g_ref[:, cs]
            outs.append((o, s_new, (gv * _sigmoid(gv) * (xh * w_ref[:, cs])).astype(BF16)))
        for hh, cs in enumerate(cols):
            o, s_new, ret = outs[hh]
            sp_ref[hh, 0] = s_ins[hh]
            s_sc[hh] = s_new
            o_ref[:, cs] = o
            ret_ref[:, cs] = ret

    blk = pl.BlockSpec((rc, rw), lambda n: (n, 0))
    tab = pl.BlockSpec((heads, rc, HEAD_DIM), lambda n: (0, 0, 0))
    dtab = pl.BlockSpec((heads, rc, rc), lambda n: (0, 0, 0))
    return pl.pallas_call(
        body, name="retention_fwd", grid=(nch,),
        in_specs=[blk, blk, blk, blk, pl.BlockSpec((1, rw), lambda n: (0, 0)),
                  dtab, tab, tab, pl.BlockSpec((heads, 8, HEAD_DIM), lambda n: (0, 0, 0))],
        out_specs=(blk, blk, pl.BlockSpec((heads, 1, HEAD_DIM, HEAD_DIM), lambda n: (0, n, 0, 0))),
        out_shape=(jax.ShapeDtypeStruct((lp, rw), F32),
                   jax.ShapeDtypeStruct((lp, rw), BF16),
                   jax.ShapeDtypeStruct((heads, nch, HEAD_DIM, HEAD_DIM), F32)),
        scratch_shapes=[pltpu.VMEM((heads, HEAD_DIM, HEAD_DIM), F32)],
        compiler_params=_params(("arbitrary",)),
    )(q, k, v, g, rnw, dec, wq, wk, gch)


def _ret_bwd(dret, q, k, v, g, o, sprev, rnw, tables, cosf, sinf):
    lp, rw = q.shape
    heads = rw // HEAD_DIM
    rc = tables[0].shape[1]
    nch = lp // rc
    dec, wq, wk, gch = tables
    scale = HEAD_DIM ** -0.5
    half = HEAD_DIM // 2

    def body(dret_ref, q_ref, k_ref, v_ref, g_ref, o_ref, sp_ref, w_ref, dec_ref, wq_ref, wk_ref, gch_ref,
             cos_ref, sin_ref, dq_ref, dk_ref, dv_ref, dg_ref, dw_ref, ds_sc):
        n = pl.program_id(0)

        @pl.when(n == 0)
        def _():
            ds_sc[...] = jnp.zeros_like(ds_sc)
            dw_ref[...] = jnp.zeros_like(dw_ref)

        cosv = cos_ref[...]
        sinv = sin_ref[...]
        cols = [slice(hh * HEAD_DIM, (hh + 1) * HEAD_DIM) for hh in range(heads)]
        ds_ins = [ds_sc[hh] for hh in range(heads)]
        dw_ins = [dw_ref[:, cs] for cs in cols]
        outs = []
        for hh, cs in enumerate(cols):
            qv, kv, vv = q_ref[:, cs], k_ref[:, cs], v_ref[:, cs]
            gv = g_ref[:, cs]
            dr = dret_ref[:, cs]
            w = w_ref[:, cs]
            sg = _sigmoid(gv)
            sil = gv * sg
            xh, r = _head_norm(o_ref[:, cs])
            dgate = (dr * (xh * w) * (sg * (1.0 + gv * (1.0 - sg)))).astype(BF16)
            dyw = dr * sil
            dw_new = dw_ins[hh] + jnp.sum(dyw * xh, axis=0, keepdims=True)
            dxh = dyw * w
            do = r * (dxh - jnp.mean(dxh, axis=-1, keepdims=True)
                      - xh * jnp.mean(dxh * xh, axis=-1, keepdims=True))
            dob = do.astype(BF16)
            dmask = dec_ref[hh]
            wqv = wq_ref[hh]
            wkv = wk_ref[hh]
            a = (_dot_nt(qv, kv) * dmask).astype(BF16)
            da = (_dot_nt(dob, vv) * dmask).astype(BF16)
            qw = (qv.astype(F32) * wqv).astype(BF16)
            kw = (kv.astype(F32) * wkv).astype(BF16)
            s_in = sp_ref[hh, 0].astype(BF16)
            ds = ds_ins[hh]
            dsb = ds.astype(BF16)
            dq = _dot(da, kv) + _dot_nt(dob, s_in) * wqv
            dk = _dot_tn(da, qv) + _dot_nt(vv, dsb) * wkv
            dv = _dot_tn(a, dob) + _dot(kw, dsb)
            ds_new = gch_ref[hh, 0:1, :] * ds + _dot_tn(qw, dob)
            outs.append((dgate, dw_new, ds_new,
                         (dq * cosv + pltpu.roll(dq * sinv, half, 1)).astype(BF16),
                         ((dk * cosv + pltpu.roll(dk * sinv, half, 1)) * scale).astype(BF16),
                         dv.astype(BF16)))
        for hh, cs in enumerate(cols):
            dgate, dw_new, ds_new, dqv, dkv, dvv = outs[hh]
            dg_ref[:, cs] = dgate
            dw_ref[:, cs] = dw_new
            ds_sc[hh] = ds_new
            dq_ref[:, cs] = dqv
            dk_ref[:, cs] = dkv
            dv_ref[:, cs] = dvv

    blk = pl.BlockSpec((rc, rw), lambda n: (nch - 1 - n, 0))
    tab = pl.BlockSpec((heads, rc, HEAD_DIM), lambda n: (0, 0, 0))
    dtab = pl.BlockSpec((heads, rc, rc), lambda n: (0, 0, 0))
    wsp = pl.BlockSpec((1, rw), lambda n: (0, 0))
    pos = pl.BlockSpec((rc, HEAD_DIM), lambda n: (nch - 1 - n, 0))
    bshape = jax.ShapeDtypeStruct((lp, rw), BF16)
    return pl.pallas_call(
        body, name="retention_bwd", grid=(nch,),
        in_specs=[blk, blk, blk, blk, blk, blk,
                  pl.BlockSpec((heads, 1, HEAD_DIM, HEAD_DIM), lambda n: (0, nch - 1 - n, 0, 0)),
                  wsp, dtab, tab, tab, pl.BlockSpec((heads, 8, HEAD_DIM), lambda n: (0, 0, 0)), pos, pos],
        out_specs=(blk, blk, blk, blk, wsp),
        out_shape=(bshape, bshape, bshape, bshape, jax.ShapeDtypeStruct((1, rw), F32)),
        scratch_shapes=[pltpu.VMEM((heads, HEAD_DIM, HEAD_DIM), F32)],
        compiler_params=_params(("arbitrary",)),
    )(dret, q, k, v, g, o, sprev, rnw, dec, wq, wk, gch, cosf, sinf)


SCAN_CW = 512


def _s5_prepare(lam_re, lam_im, log_dt, b_re, b_im):
    dt = jnp.exp(log_dt)[:, None]
    er = jnp.exp(lam_re * dt)
    ar = er * jnp.cos(lam_im * dt)
    ai = er * jnp.sin(lam_im * dt)
    den = lam_re * lam_re + lam_im * lam_im
    fr = ((ar - 1.0) * lam_re + ai * lam_im) / den
    fi = (ai * lam_re - (ar - 1.0) * lam_im) / den
    bbr = fr[..., None] * b_re - fi[..., None] * b_im
    bbi = fr[..., None] * b_im + fi[..., None] * b_re
    return ar, ai, bbr, bbi


def _blockdiag_in(t):
    g, p, n = t.shape
    gs = g // N_SEC
    t = t.reshape(N_SEC, gs, p, n)
    eye = jnp.eye(gs, dtype=t.dtype)
    return jnp.einsum("sgpn,gh->sgphn", t, eye).reshape(N_SEC, gs * p, gs * n)


def _blockdiag_out(m, g, p, n):
    gs = g // N_SEC
    m = m.reshape(N_SEC, gs, p, gs, n)
    eye = jnp.eye(gs, dtype=m.dtype)
    return jnp.einsum("sgphn,gh->sgpn", m, eye).reshape(g, p, n)


def _scan_step(xr_ref, xi_ref, r0, prev, ar_ref, ai_ref, conj, ncols):
    new = []
    for cc in range(ncols // SCAN_CW):
        cs = pl.ds(cc * SCAN_CW, SCAN_CW)
        pr, pi = prev[cc]
        ar = ar_ref[:, cs]
        ai = ai_ref[:, cs]
        if conj:
            nr = ar * pr + ai * pi
            ni = ar * pi - ai * pr
        else:
            nr = ar * pr - ai * pi
            ni = ar * pi + ai * pr
        xr = xr_ref[pl.ds(r0, 8), cs] + nr
        xi = xi_ref[pl.ds(r0, 8), cs] + ni
        xr_ref[pl.ds(r0, 8), cs] = xr
        xi_ref[pl.ds(r0, 8), cs] = xi
        new.append((xr, xi))
    return new


def _scan_chunks(ncols):
    return [pl.ds(cc * SCAN_CW, SCAN_CW) for cc in range(ncols // SCAN_CW)]


def _flat(pairs):
    return tuple(t for p in pairs for t in p)


def _pairs(flat):
    return [(flat[2 * k], flat[2 * k + 1]) for k in range(len(flat) // 2)]


def _shift_rows(z, down):
    row = lax.broadcasted_iota(jnp.int32, z.shape, 0)
    if down:
        return jnp.where(row == 0, 0.0, pltpu.roll(z, 1, 0))
    return jnp.where(row == N_SEG - 1, 0.0, pltpu.roll(z, N_SEG - 1, 0))


def _s5_fwd(u, bsr, bsi, csr, csi, a8r, a8i, al8r, al8i, d, gluw, glub, nw, jb):
    lp, sw = u.shape
    ns = a8r.shape[1]
    rows = N_SEG * jb
    nblk = lp // rows
    secw = sw // N_SEC
    secn = ns // N_SEC

    def local_scan(u_ref, bsr_ref, bsi_ref, ar_ref, ai_ref, xr_ref, xi_ref, pr_sc, pi_sc):
        for s in range(N_SEC):
            ub = u_ref[:, s * secw:(s + 1) * secw].astype(BF16)
            xr_ref[:, s * secn:(s + 1) * secn] = _dot(ub, bsr_ref[s])
            xi_ref[:, s * secn:(s + 1) * secn] = _dot(ub, bsi_ref[s])
        prev = [(pr_sc[:, cs], pi_sc[:, cs]) for cs in _scan_chunks(ns)]
        prev = _scan_step(xr_ref, xi_ref, 0, prev, ar_ref, ai_ref, False, ns)

        def step(j, carry):
            r0 = pl.multiple_of(j * 8, 8)
            return _flat(_scan_step(xr_ref, xi_ref, r0, _pairs(carry), ar_ref, ai_ref, False, ns))

        last = _pairs(lax.fori_loop(1, jb, step, _flat(prev)))
        for cs, (vr, vi) in zip(_scan_chunks(ns), last):
            pr_sc[:, cs] = vr
            pi_sc[:, cs] = vi

    def carry_body(u_ref, bsr_ref, bsi_ref, ar_ref, ai_ref, alr_ref, ali_ref, cr_ref, ci_ref,
                   xr_sc, xi_sc, pr_sc, pi_sc):
        b = pl.program_id(0)

        @pl.when(b == 0)
        def _():
            pr_sc[...] = jnp.zeros_like(pr_sc)
            pi_sc[...] = jnp.zeros_like(pi_sc)

        local_scan(u_ref, bsr_ref, bsi_ref, ar_ref, ai_ref, xr_sc, xi_sc, pr_sc, pi_sc)

        @pl.when(b == nblk - 1)
        def _():
            er = _shift_rows(pr_sc[...], True)
            ei = _shift_rows(pi_sc[...], True)
            alr, ali = alr_ref[...], ali_ref[...]
            cr, ci = er, ei
            for _ in range(N_SEG - 2):
                sr = _shift_rows(cr, True)
                si = _shift_rows(ci, True)
                cr = er + alr * sr - ali * si
                ci = ei + alr * si + ali * sr
            cr_ref[...] = cr
            ci_ref[...] = ci

    ublk = pl.BlockSpec((rows, sw), lambda b: (b, 0))
    bspec = pl.BlockSpec((N_SEC, secw, secn), lambda b: (0, 0, 0))
    cspec = pl.BlockSpec((N_SEC, secn, secw), lambda b: (0, 0, 0))
    s8 = pl.BlockSpec((N_SEG, ns), lambda b: (0, 0))
    vec = pl.BlockSpec((1, sw), lambda b: (0, 0))
    s8shape = jax.ShapeDtypeStruct((N_SEG, ns), F32)
    c0r, c0i = pl.pallas_call(
        carry_body, name="s5_fwd_carry", grid=(nblk,),
        in_specs=[ublk, bspec, bspec, s8, s8, s8, s8],
        out_specs=(s8, s8), out_shape=(s8shape, s8shape),
        scratch_shapes=[pltpu.VMEM((rows, ns), F32), pltpu.VMEM((rows, ns), F32),
                        pltpu.VMEM((N_SEG, ns), F32), pltpu.VMEM((N_SEG, ns), F32)],
        compiler_params=_params(("arbitrary",)),
    )(u, bsr, bsi, a8r, a8i, al8r, al8i)

    def main_body(u_ref, bsr_ref, bsi_ref, csr_ref, csi_ref, ar_ref, ai_ref, c0r_ref, c0i_ref,
                  d_ref, gw_ref, gb_ref, nw_ref, xr_ref, xi_ref, yp_ref, out_ref, pr_sc, pi_sc):
        b = pl.program_id(0)

        @pl.when(b == 0)
        def _():
            pr_sc[...] = c0r_ref[...]
            pi_sc[...] = c0i_ref[...]

        local_scan(u_ref, bsr_ref, bsi_ref, ar_ref, ai_ref, xr_ref, xi_ref, pr_sc, pi_sc)
        for s in range(N_SEC):
            xs = pl.ds(s * secn, secn)
            us = pl.ds(s * secw, secw)
            y = _dot(xr_ref[:, xs].astype(BF16), csr_ref[s]) + _dot(xi_ref[:, xs].astype(BF16), csi_ref[s])
            yp_ref[:, us] = y + d_ref[:, us] * u_ref[:, us]
        yp = yp_ref[...]
        t = jnp.tanh(GELU_K0 * (yp + GELU_K1 * yp * yp * yp))
        y1 = 0.5 * yp * (1.0 + t)
        z = _dot(y1.astype(BF16), gw_ref[...]) + gb_ref[...]
        y2 = y1 * _sigmoid(z)
        xh, _ = _rms_stats(y2)
        out_ref[...] = (xh * nw_ref[...]).astype(BF16)

    xblk = pl.BlockSpec((rows, ns), lambda b: (b, 0))
    xr, xi, yp, out = pl.pallas_call(
        main_body, name="s5_fwd", grid=(nblk,),
        in_specs=[ublk, bspec, bspec, cspec, cspec, s8, s8, s8, s8, vec,
                  pl.BlockSpec((sw, sw), lambda b: (0, 0)), vec, vec],
        out_specs=(xblk, xblk, ublk, ublk),
        out_shape=(jax.ShapeDtypeStruct((lp, ns), F32), jax.ShapeDtypeStruct((lp, ns), F32),
                   jax.ShapeDtypeStruct((lp, sw), F32), jax.ShapeDtypeStruct((lp, sw), BF16)),
        scratch_shapes=[pltpu.VMEM((N_SEG, ns), F32), pltpu.VMEM((N_SEG, ns), F32)],
        compiler_params=_params(("arbitrary",)),
    )(u, bsr, bsi, csr, csi, a8r, a8i, c0r, c0i, d, gluw, glub, nw)
    return xr, xi, c0r, c0i, yp, out


def _s5_bwd(dout, u, yp, xr, xi, c0r, c0i, bsrt, bsit, csrt, csit, a8r, a8i, al8r, al8i, d, gluw, glub, nw, jb):
    lp, sw = u.shape
    ns = a8r.shape[1]
    rows = N_SEG * jb
    nblk = lp // rows
    secw = sw // N_SEC
    secn = ns // N_SEC

    def rowwise_bwd(dout_ref, yp_ref, gw_ref, gb_ref, nw_ref):
        ypv = yp_ref[...]
        t = jnp.tanh(GELU_K0 * (ypv + GELU_K1 * ypv * ypv * ypv))
        y1 = 0.5 * ypv * (1.0 + t)
        dgelu = 0.5 * (1.0 + t) + 0.5 * ypv * (1.0 - t * t) * GELU_K0 * (1.0 + 3.0 * GELU_K1 * ypv * ypv)
        gw = gw_ref[...]
        y1b = y1.astype(BF16)
        sg = _sigmoid(_dot(y1b, gw) + gb_ref[...])
        xh, r = _rms_stats(y1 * sg)
        dov = dout_ref[...]
        dy2 = _rms_bwd(dov, xh, r, nw_ref[...])
        dz = dy2 * y1 * sg * (1.0 - sg)
        dzb = dz.astype(BF16)
        dy1 = dy2 * sg + _dot_nt(dzb, gw)
        return dy1 * dgelu, dov * xh, y1b, dzb, dz

    def lam_scan(dyp_of, csrt_ref, csit_ref, ar_ref, ai_ref, lr_sc, li_sc, nr_sc, ni_sc, extra):
        for s in range(N_SEC):
            db = dyp_of(s)
            lr_sc[:, s * secn:(s + 1) * secn] = _dot(db, csrt_ref[s])
            li_sc[:, s * secn:(s + 1) * secn] = _dot(db, csit_ref[s])
        top = rows - 8
        prev = [(nr_sc[:, cs], ni_sc[:, cs]) for cs in _scan_chunks(ns)]
        prev = _scan_step(lr_sc, li_sc, top, prev, ar_ref, ai_ref, True, ns)
        extra(top, pl.ds(top - 8, 8))

        def step(jj, carry):
            r0 = pl.multiple_of((jb - 1 - jj) * 8, 8)
            rp = pl.multiple_of((jb - 2 - jj) * 8, 8)
            new = _scan_step(lr_sc, li_sc, r0, _pairs(carry), ar_ref, ai_ref, True, ns)
            extra(r0, pl.ds(rp, 8))
            return _flat(new)

        prev = _pairs(lax.fori_loop(1, jb - 1, step, _flat(prev)))
        last = _scan_step(lr_sc, li_sc, 0, prev, ar_ref, ai_ref, True, ns)
        extra(0, None)
        for cs, (vr, vi) in zip(_scan_chunks(ns), last):
            nr_sc[:, cs] = vr
            ni_sc[:, cs] = vi

    def carry_body(dout_ref, yp_ref, u_ref, gw_ref, gb_ref, nw_ref, csrt_ref, csit_ref, ar_ref, ai_ref,
                   alr_ref, ali_ref, cr_ref, ci_ref, dyp_ref, dnw_ref, dgw_ref, dgb_ref, dd_ref,
                   lr_sc, li_sc, nr_sc, ni_sc):
        b = pl.program_id(0)

        @pl.when(b == 0)
        def _():
            nr_sc[...] = jnp.zeros_like(nr_sc)
            ni_sc[...] = jnp.zeros_like(ni_sc)
            for ref in (dnw_ref, dgw_ref, dgb_ref, dd_ref):
                ref[...] = jnp.zeros_like(ref)

        dyp, dnw_rows, y1b, dzb, dz = rowwise_bwd(dout_ref, yp_ref, gw_ref, gb_ref, nw_ref)
        dnw_ref[...] += jnp.sum(dnw_rows, axis=0, keepdims=True)
        dgw_ref[...] += _dot_tn(y1b, dzb)
        dgb_ref[...] += jnp.sum(dz, axis=0, keepdims=True)
        dd_ref[...] += jnp.sum(dyp * u_ref[...], axis=0, keepdims=True)
        dyp_ref[...] = dyp.astype(BF16)
        lam_scan(lambda s: dyp_ref[:, s * secw:(s + 1) * secw], csrt_ref, csit_ref, ar_ref, ai_ref,
                 lr_sc, li_sc, nr_sc, ni_sc, lambda r0, prev_rows: None)

        @pl.when(b == nblk - 1)
        def _():
            fr = _shift_rows(nr_sc[...], False)
            fi = _shift_rows(ni_sc[...], False)
            alr, ali = alr_ref[...], ali_ref[...]
            cr, ci = fr, fi
            for _ in range(N_SEG - 2):
                sr = _shift_rows(cr, False)
                si = _shift_rows(ci, False)
                cr = fr + alr * sr + ali * si
                ci = fi + alr * si - ali * sr
            cr_ref[...] = cr
            ci_ref[...] = ci

    rev = lambda b: (nblk - 1 - b, 0)
    ublk = pl.BlockSpec((rows, sw), rev)
    xblk = pl.BlockSpec((rows, ns), rev)
    s8 = pl.BlockSpec((N_SEG, ns), lambda b: (0, 0))
    vec = pl.BlockSpec((1, sw), lambda b: (0, 0))
    gws = pl.BlockSpec((sw, sw), lambda b: (0, 0))
    btspec = pl.BlockSpec((N_SEC, secn, secw), lambda b: (0, 0, 0))
    ctspec = pl.BlockSpec((N_SEC, secw, secn), lambda b: (0, 0, 0))
    s8shape = jax.ShapeDtypeStruct((N_SEG, ns), F32)
    lcr, lci, dyp_all, d_nw, d_gw, d_gb, d_d = pl.pallas_call(
        carry_body, name="s5_bwd_carry", grid=(nblk,),
        in_specs=[ublk, ublk, ublk, gws, vec, vec, ctspec, ctspec, s8, s8, s8, s8],
        out_specs=(s8, s8, ublk, vec, gws, vec, vec),
        out_shape=(s8shape, s8shape, jax.ShapeDtypeStruct((lp, sw), BF16), jax.ShapeDtypeStruct((1, sw), F32),
                   jax.ShapeDtypeStruct((sw, sw), F32), jax.ShapeDtypeStruct((1, sw), F32),
                   jax.ShapeDtypeStruct((1, sw), F32)),
        scratch_shapes=[pltpu.VMEM((rows, ns), F32), pltpu.VMEM((rows, ns), F32),
                        pltpu.VMEM((N_SEG, ns), F32), pltpu.VMEM((N_SEG, ns), F32)],
        compiler_params=_params(("arbitrary",)),
    )(dout, yp, u, gluw, glub, nw, csrt, csit, a8r, a8i, al8r, al8i)

    def main_body(dyp_sc, u_ref, xr_ref, xi_ref, xtr_ref, xti_ref, c0r_ref, c0i_ref, lcr_ref, lci_ref,
                  d_ref, bsrt_ref, bsit_ref, csrt_ref, csit_ref, ar_ref, ai_ref,
                  du_ref, dcr_ref, dci_ref, dbr_ref, dbi_ref, dar_ref, dai_ref,
                  lr_sc, li_sc, nr_sc, ni_sc):
        b = pl.program_id(0)

        @pl.when(b == 0)
        def _():
            nr_sc[...] = lcr_ref[...]
            ni_sc[...] = lci_ref[...]
            for ref in (dcr_ref, dci_ref, dbr_ref, dbi_ref, dar_ref, dai_ref):
                ref[...] = jnp.zeros_like(ref)

        for s in range(N_SEC):
            db = dyp_sc[:, s * secw:(s + 1) * secw]
            xs = pl.ds(s * secn, secn)
            dcr_ref[s] += _dot_tn(xr_ref[:, xs].astype(BF16), db)
            dci_ref[s] += _dot_tn(xi_ref[:, xs].astype(BF16), db)

        first = b == nblk - 1

        def acc_da(r0, prev_rows):
            for cc in range(ns // SCAN_CW):
                cs = pl.ds(cc * SCAN_CW, SCAN_CW)
                lr = lr_sc[pl.ds(r0, 8), cs]
                li = li_sc[pl.ds(r0, 8), cs]
                if prev_rows is None:
                    xpr = jnp.where(first, c0r_ref[:, cs], xtr_ref[:, cs])
                    xpi = jnp.where(first, c0i_ref[:, cs], xti_ref[:, cs])
                else:
                    xpr = xr_ref[prev_rows, cs]
                    xpi = xi_ref[prev_rows, cs]
                dar_ref[:, cs] += lr * xpr + li * xpi
                dai_ref[:, cs] += li * xpr - lr * xpi

        lam_scan(lambda s: dyp_sc[:, s * secw:(s + 1) * secw], csrt_ref, csit_ref, ar_ref, ai_ref,
                 lr_sc, li_sc, nr_sc, ni_sc, acc_da)

        for s in range(N_SEC):
            xs = pl.ds(s * secn, secn)
            us = pl.ds(s * secw, secw)
            lrb = lr_sc[:, xs].astype(BF16)
            lib = li_sc[:, xs].astype(BF16)
            du = _dot(lrb, bsrt_ref[s]) + _dot(lib, bsit_ref[s]) + d_ref[:, us] * dyp_sc[:, us].astype(F32)
            du_ref[:, us] = du.astype(BF16)
            ub = u_ref[:, us].astype(BF16)
            dbr_ref[s] += _dot_tn(ub, lrb)
            dbi_ref[s] += _dot_tn(ub, lib)

    tail = pl.BlockSpec((N_SEG, ns), lambda b: (jnp.maximum((nblk - 1 - b) * jb - 1, 0), 0))
    acc_c = pl.BlockSpec((N_SEC, secn, secw), lambda b: (0, 0, 0))
    acc_b = pl.BlockSpec((N_SEC, secw, secn), lambda b: (0, 0, 0))
    du, dcr, dci, dbr, dbi, dar, dai = pl.pallas_call(
        main_body, name="s5_bwd", grid=(nblk,),
        in_specs=[ublk, ublk, xblk, xblk, tail, tail, s8, s8, s8, s8,
                  vec, btspec, btspec, ctspec, ctspec, s8, s8],
        out_specs=(ublk, acc_c, acc_c, acc_b, acc_b, s8, s8),
        out_shape=(jax.ShapeDtypeStruct((lp, sw), BF16),
                   jax.ShapeDtypeStruct((N_SEC, secn, secw), F32),
                   jax.ShapeDtypeStruct((N_SEC, secn, secw), F32),
                   jax.ShapeDtypeStruct((N_SEC, secw, secn), F32),
                   jax.ShapeDtypeStruct((N_SEC, secw, secn), F32),
                   s8shape, s8shape),
        scratch_shapes=[pltpu.VMEM((rows, ns), F32), pltpu.VMEM((rows, ns), F32),
                        pltpu.VMEM((N_SEG, ns), F32), pltpu.VMEM((N_SEG, ns), F32)],
        compiler_params=_params(("arbitrary",)),
    )(dyp_all, u, xr, xi, xr, xi, c0r, c0i, lcr, lci, d, bsrt, bsit, csrt, csit, a8r, a8i)
    return du, d_nw, d_gw, d_gb, d_d, dcr, dci, dbr, dbi, dar, dai


def _outproj_fwd(h, ret, ssm, wo):
    lp, d = h.shape
    nck, rs, _ = wo.shape
    rw = ret.shape[1]
    tm = _tile(lp, 640)
    per = rw // rs

    def body(h_ref, ret_ref, ssm_ref, w_ref, o_ref):
        acc = h_ref[...]
        for c in range(nck):
            src = ret_ref if c < per else ssm_ref
            lo = (c % per) * rs
            acc = acc + _dot(src[:, lo:lo + rs], w_ref[c])
        o_ref[...] = acc

    row = lambda w: pl.BlockSpec((tm, w), lambda i: (i, 0))
    return pl.pallas_call(
        body, name="outproj_fwd", grid=(lp // tm,),
        in_specs=[row(d), row(rw), row(ssm.shape[1]), pl.BlockSpec((nck, rs, d), lambda i: (0, 0, 0))],
        out_specs=row(d), out_shape=jax.ShapeDtypeStruct((lp, d), F32),
        compiler_params=_params(("arbitrary",)),
    )(h, ret, ssm, wo)


def _outproj_bwd(dh, ret, ssm, wo):
    lp, d = dh.shape
    nck, rs, _ = wo.shape
    rw = ret.shape[1]
    sw = ssm.shape[1]
    tm = _tile(lp, 640)
    per = rw // rs
    last = lp // tm - 1

    def body(dh_ref, ret_ref, ssm_ref, w_ref, dret_ref, dssm_ref, dw_ref, acc_sc):
        i = pl.program_id(0)

        @pl.when(i == 0)
        def _():
            acc_sc[...] = jnp.zeros_like(acc_sc)

        dhb = dh_ref[...].astype(BF16)
        for c in range(nck):
            src, dst = (ret_ref, dret_ref) if c < per else (ssm_ref, dssm_ref)
            lo = (c % per) * rs
            dst[:, lo:lo + rs] = _dot_nt(dhb, w_ref[c])
            acc_sc[c] += _dot_tn(src[:, lo:lo + rs], dhb)

        @pl.when(i == last)
        def _():
            dw_ref[...] = acc_sc[...].astype(BF16)

    row = lambda w: pl.BlockSpec((tm, w), lambda i: (i, 0))
    wsp = pl.BlockSpec((nck, rs, d), lambda i: (0, 0, 0))
    return pl.pallas_call(
        body, name="outproj_bwd", grid=(lp // tm,),
        in_specs=[row(d), row(rw), row(sw), wsp],
        out_specs=(row(rw), row(sw), wsp),
        out_shape=(jax.ShapeDtypeStruct((lp, rw), F32), jax.ShapeDtypeStruct((lp, sw), F32),
                   jax.ShapeDtypeStruct((nck, rs, d), BF16)),
        scratch_shapes=[pltpu.VMEM((nck, rs, d), F32)],
        compiler_params=_params(("arbitrary",)),
    )(dh, ret, ssm, wo)


def _pack(arrs):
    flat = jnp.concatenate([a.reshape(-1).astype(F32) for a in arrs])
    n = flat.shape[0]
    rows = -(-n // (8 * LANE)) * 8
    return jnp.pad(flat, (0, rows * LANE - n)).reshape(rows, LANE)


def _unpack(packed, shapes):
    flat = packed.reshape(-1)
    out, off = [], 0
    for s in shapes:
        n = math.prod(s)
        out.append(flat[off:off + n].reshape(s))
        off += n
    return out


def _to_segments(a, seg_len):
    return a.reshape(N_SEG, seg_len, a.shape[1]).transpose(1, 0, 2).reshape(a.shape)


def _from_segments(a, seg_len):
    return a.reshape(seg_len, N_SEG, a.shape[1]).transpose(1, 0, 2).reshape(a.shape)


WEIGHT_NAMES = ['meta_tokens', 'ffn1_norm_w', 'ffn1_w_gate', 'ffn1_w_up', 'ffn1_w_down', 'mix_norm_w', 'w_in',
                'ret_norm_w', 'ssm_lambda_re', 'ssm_lambda_im', 'ssm_log_dt', 'ssm_b_re', 'ssm_b_im', 'ssm_c_re',
                'ssm_c_im', 'ssm_d', 'ssm_glu_w', 'ssm_glu_b', 'ssm_norm_w', 'w_out', 'ffn2_norm_w', 'ffn2_w_gate',
                'ffn2_w_up', 'ffn2_w_down', 'final_norm_w']
BIG = ['ffn1_w_gate', 'ffn1_w_up', 'ffn1_w_down', 'w_in', 'ssm_glu_w', 'w_out', 'ffn2_w_gate', 'ffn2_w_up',
       'ffn2_w_down']
TRANSPOSED = ['ffn1_w_gate', 'ffn1_w_up', 'ffn2_w_gate', 'ffn2_w_up']
BIG_EARLY = ['ffn1_w_gate', 'ffn1_w_up', 'ffn1_w_down']
BIG_LATE = [n for n in BIG if n not in BIG_EARLY]
SMALL = [n for n in WEIGHT_NAMES if n not in BIG]


def kernel(x, meta_tokens, ffn1_norm_w, ffn1_w_gate, ffn1_w_up, ffn1_w_down, mix_norm_w, w_in, ret_norm_w, ssm_lambda_re, ssm_lambda_im, ssm_log_dt, ssm_b_re, ssm_b_im, ssm_c_re, ssm_c_im, ssm_d, ssm_glu_w, ssm_glu_b, ssm_norm_w, w_out, ffn2_norm_w, ffn2_w_gate, ffn2_w_up, ffn2_w_down, final_norm_w, loss_target, m_meta_tokens, m_ffn1_norm_w, m_ffn1_w_gate, m_ffn1_w_up, m_ffn1_w_down, m_mix_norm_w, m_w_in, m_ret_norm_w, m_ssm_lambda_re, m_ssm_lambda_im, m_ssm_log_dt, m_ssm_b_re, m_ssm_b_im, m_ssm_c_re, m_ssm_c_im, m_ssm_d, m_ssm_glu_w, m_ssm_glu_b, m_ssm_norm_w, m_w_out, m_ffn2_norm_w, m_ffn2_w_gate, m_ffn2_w_up, m_ffn2_w_down, m_final_norm_w, v_meta_tokens, v_ffn1_norm_w, v_ffn1_w_gate, v_ffn1_w_up, v_ffn1_w_down, v_mix_norm_w, v_w_in, v_ret_norm_w, v_ssm_lambda_re, v_ssm_lambda_im, v_ssm_log_dt, v_ssm_b_re, v_ssm_b_im, v_ssm_c_re, v_ssm_c_im, v_ssm_d, v_ssm_glu_w, v_ssm_glu_b, v_ssm_norm_w, v_w_out, v_ffn2_norm_w, v_ffn2_w_gate, v_ffn2_w_up, v_ffn2_w_down, v_final_norm_w):
    args = locals()
    w = {n: args[n] for n in WEIGHT_NAMES}
    m = {n: args["m_" + n] for n in WEIGHT_NAMES}
    v = {n: args["v_" + n] for n in WEIGHT_NAMES}

    seq, d = x.shape[1], x.shape[2]
    lp = seq + CHUNK
    seg_len = lp // N_SEG
    rw = RET_HEADS * HEAD_DIM
    sw = ssm_d.shape[-1]
    groups = sw // SSM_GROUP
    ns = groups * SSM_STATE
    jb = _tile(seg_len, S5_STEPS, 8)
    chip = 2 * lax.axis_index("x") + lax.axis_index("y")

    as_fd = lambda t: jnp.swapaxes(t, -1, -2)
    shards = {n: (as_fd(w[n][0]) if n in TRANSPOSED else w[n][0]).astype(BF16) for n in BIG}
    early = [shards[n] for n in BIG_EARLY] + [meta_tokens]
    gathered = _gather_two_level("gather_early", early)
    gw = dict(zip(BIG_EARLY, gathered[:-1]))
    meta_full = jnp.transpose(gathered[-1], (1, 0, 2)).reshape(N_META, d)
    late = [shards[n] for n in BIG_LATE]

    freqs = 1.0 / (ROPE_BASE ** (jnp.arange(0, HEAD_DIM, 2, dtype=F32) / HEAD_DIM))
    ang_c = (jnp.arange(lp // CHUNK, dtype=F32) * CHUNK - float(CHUNK - N_META))[:, None] * freqs[None, :]
    ang_r = jnp.arange(CHUNK, dtype=F32)[:, None] * freqs[None, :]
    cos_c, sin_c = jnp.cos(ang_c)[:, None, :], jnp.sin(ang_c)[:, None, :]
    cos_r, sin_r = jnp.cos(ang_r)[None], jnp.sin(ang_r)[None]
    cos_t = (cos_c * cos_r - sin_c * sin_r).reshape(lp, HEAD_DIM // 2)
    sin_t = (sin_c * cos_r + cos_c * sin_r).reshape(lp, HEAD_DIM // 2)
    cosf = jnp.concatenate([cos_t, cos_t], axis=1)
    sinf = jnp.concatenate([-sin_t, sin_t], axis=1)
    tables = _retention_tables(_tile(lp, RET_ROWS, CHUNK))

    lam_re, lam_im, log_dt = ssm_lambda_re[0], ssm_lambda_im[0], ssm_log_dt[0]
    b_re, b_im, c_re, c_im = ssm_b_re[0], ssm_b_im[0], ssm_c_re[0], ssm_c_im[0]
    (ar, ai, bbr, bbi), prep_vjp = jax.vjp(_s5_prepare, lam_re, lam_im, log_dt, b_re, b_im)
    dt = jnp.exp(log_dt)[:, None]
    el = jnp.exp(seg_len * lam_re * dt)
    alr = el * jnp.cos(seg_len * lam_im * dt)
    ali = el * jnp.sin(seg_len * lam_im * dt)
    bc8 = lambda t: jnp.broadcast_to(t.reshape(1, ns), (N_SEG, ns))
    a8r, a8i, al8r, al8i = bc8(ar), bc8(ai), bc8(alr), bc8(ali)
    bsr = _blockdiag_in(jnp.transpose(bbr, (0, 2, 1)))
    bsi = _blockdiag_in(jnp.transpose(bbi, (0, 2, 1)))
    csrt = _blockdiag_in(c_re)
    csit = _blockdiag_in(-c_im)
    tr = lambda t: jnp.transpose(t, (0, 2, 1))
    bsr_b, bsi_b = bsr.astype(BF16), bsi.astype(BF16)
    csr_b, csi_b = tr(csrt).astype(BF16), tr(csit).astype(BF16)
    bsrt_b, bsit_b = tr(bsr).astype(BF16), tr(bsi).astype(BF16)
    csrt_b, csit_b = csrt.astype(BF16), csit.astype(BF16)

    h0 = (jnp.concatenate([jnp.zeros((CHUNK - N_META, d), F32), meta_full], axis=0), x[0])
    (h1, g1, u1), late_half = _ffn_fwd("ffn1_fwd", h0, ffn1_norm_w, gw['ffn1_w_gate'], gw['ffn1_w_up'],
                                       gw['ffn1_w_down'], _allgather_chips_plan(late), late)
    gw.update(zip(BIG_LATE, _forward_sibling("gather_late_forward", late_half)))
    glu_full = gw['ssm_glu_w'].reshape(sw, sw)
    n2, q, k, vv, gate, u = _inproj_fwd(h1, mix_norm_w, gw['w_in'], cosf, sinf, rw)
    o, ret, sprev = _ret_fwd(q, k, vv, gate, ret_norm_w, tables)
    u_seg = _to_segments(u, seg_len)
    xr, xi, c0r, c0i, yp, ssm_seg = _s5_fwd(u_seg, bsr_b, bsi_b, csr_b, csi_b, a8r, a8i, al8r, al8i,
                                            ssm_d, glu_full, ssm_glu_b, ssm_norm_w, jb)
    ssm = _from_segments(ssm_seg, seg_len)
    h2 = _outproj_fwd(h1, ret, ssm, gw['w_out'])
    (dh3, g2, u2, loss_part, d_final), _ = _ffn_fwd(
        "ffn2_fwd_loss", h2, ffn2_norm_w, gw['ffn2_w_gate'], gw['ffn2_w_up'], gw['ffn2_w_down'],
        loss=(final_norm_w.reshape(1, d), loss_target[0]))

    (dh2, d_ffn2_norm, nb, daccb, ab, dgb, dub), _ = _ffn_bwd_act(
        "ffn2_bwd_act", dh3, h2, ffn2_norm_w, g2, u2, gw['ffn2_w_gate'], gw['ffn2_w_up'], gw['ffn2_w_down'])
    (dwg2, dwu2, dwd2), _ = _ffn_bwd_w("ffn2_bwd_w", nb, daccb, ab, dgb, dub)
    dret, dssm, dwo = _outproj_bwd(dh2, ret, ssm, gw['w_out'])
    (du_seg, d_ssm_norm, d_glu_w, d_glu_b, d_ssm_d, dcr_s, dci_s, dbr_s, dbi_s, dar8, dai8) = _s5_bwd(
        _to_segments(dssm, seg_len), u_seg, yp, xr, xi, c0r, c0i, bsrt_b, bsit_b, csrt_b, csit_b,
        a8r, a8i, al8r, al8i, ssm_d, glu_full, ssm_glu_b, ssm_norm_w, jb)
    du = _from_segments(du_seg, seg_len)
    dq, dk, dv, dgate, d_ret_norm = _ret_bwd(dret, q, k, vv, gate, o, sprev, ret_norm_w, tables, cosf, sinf)
    dh1, d_mix_norm, dwin = _inproj_bwd(dh2, h1, mix_norm_w, n2, gw['w_in'], dq, dk, dv, dgate, du)
    late_parts = {
        'w_in': dwin, 'ssm_glu_w': d_glu_w.reshape(N_CHIP, sw // N_CHIP, sw).astype(BF16), 'w_out': dwo,
        'ffn2_w_gate': dwg2, 'ffn2_w_up': dwu2, 'ffn2_w_down': dwd2,
    }
    late_list = [late_parts[n] for n in BIG_LATE]
    (dh0, d_ffn1_norm, nb, daccb, ab, dgb, dub), late_recv = _ffn_bwd_act(
        "ffn1_bwd_act", dh1, h0, ffn1_norm_w, g1, u1, gw['ffn1_w_gate'], gw['ffn1_w_up'], gw['ffn1_w_down'],
        _alltoall_chips_plan(late_list), late_list)
    grad_x = dh0[CHUNK:][None]
    d_meta = dh0[CHUNK - N_META:CHUNK]

    d_c_re = _blockdiag_out(tr(dcr_s), groups, SSM_GROUP, SSM_STATE)
    d_c_im = -_blockdiag_out(tr(dci_s), groups, SSM_GROUP, SSM_STATE)
    d_bbr = jnp.transpose(_blockdiag_out(dbr_s, groups, SSM_GROUP, SSM_STATE), (0, 2, 1))
    d_bbi = jnp.transpose(_blockdiag_out(dbi_s, groups, SSM_GROUP, SSM_STATE), (0, 2, 1))
    d_ar = jnp.sum(dar8, axis=0).reshape(groups, SSM_STATE)
    d_ai = jnp.sum(dai8, axis=0).reshape(groups, SSM_STATE)
    small_parts = [loss_part[0:1, :], d_meta, d_ffn1_norm, d_mix_norm, d_ret_norm, d_ar, d_ai, d_bbr, d_bbi,
                   d_c_re, d_c_im, d_ssm_d, d_glu_b, d_ssm_norm, d_ffn2_norm, d_final]
    small_shapes = [a.shape for a in small_parts]
    packed = _pack(small_parts)
    early_recv, (all_parts,) = _ffn_bwd_w_scatter("ffn1_bwd_w", nb, daccb, ab, dgb, dub, chip,
                                                  _allgather_all_plan([packed]), [packed])
    received = dict(zip(BIG_LATE + BIG_EARLY, late_recv + early_recv))
    ffn_names = [n for n in BIG if n.startswith('ffn')]
    chip_sum = dict(zip(ffn_names, _sum_slots("sum_chips_ffn", [received[n] for n in ffn_names], BF16)))
    for n in BIG:
        if n not in chip_sum:
            chip_sum[n] = _sum_slots("sum_chips_" + n, [received[n]], BF16)[0]
    chip_sums = [chip_sum[n] for n in BIG]
    sib_sums = _swap_sibling("swap_sibling", chip_sums)
    (loss_row, g_meta_full, g_ffn1_norm, g_mix_norm, g_ret_norm, g_ar, g_ai, g_bbr, g_bbi, g_c_re, g_c_im,
     g_ssm_d, g_glu_b, g_ssm_norm, g_ffn2_norm, g_final) = _unpack(_sum_slots("sum_small", [all_parts], F32)[0],
                                                                  small_shapes)
    g_lam_re, g_lam_im, g_log_dt, g_b_re, g_b_im = prep_vjp((g_ar, g_ai, g_bbr, g_bbi))
    loss = loss_row[0, 0]
    g_meta = lax.dynamic_slice(g_meta_full, (0, chip * (d // N_CHIP)), (N_META, d // N_CHIP))
    small_grads = {
        'meta_tokens': g_meta, 'ffn1_norm_w': g_ffn1_norm, 'mix_norm_w': g_mix_norm, 'ret_norm_w': g_ret_norm,
        'ssm_lambda_re': g_lam_re[None], 'ssm_lambda_im': g_lam_im[None], 'ssm_log_dt': g_log_dt[None],
        'ssm_b_re': g_b_re[None], 'ssm_b_im': g_b_im[None], 'ssm_c_re': g_c_re[None], 'ssm_c_im': g_c_im[None],
        'ssm_d': g_ssm_d, 'ssm_glu_b': g_glu_b, 'ssm_norm_w': g_ssm_norm, 'ffn2_norm_w': g_ffn2_norm,
        'final_norm_w': g_final.reshape(d),
    }

    grads, deltas, new_m, new_v = {}, {}, {}, {}
    g_pair = {n: [mine, sib] for n, mine, sib in zip(BIG, chip_sums, sib_sums)}
    view = lambda n, t: as_fd(t) if n in TRANSPOSED else t
    ffn_out = _adam("adam_ffn", [(view(n, w[n]), view(n, m[n]), view(n, v[n])) for n in ffn_names],
                    [g_pair[n] for n in ffn_names])
    for n, outs in zip(ffn_names, ffn_out):
        grads[n], deltas[n], new_m[n], new_v[n] = [view(n, t) for t in outs]
    for n in BIG:
        if n not in ffn_names:
            grads[n], deltas[n], new_m[n], new_v[n] = _adam("adam_" + n, [(w[n], m[n], v[n])], [g_pair[n]])[0]
    sm_shapes = [w[n].shape for n in SMALL]
    sm_out = _adam("adam_small", [(_pack([w[n] for n in SMALL]), _pack([m[n] for n in SMALL]),
                                  _pack([v[n] for n in SMALL]))],
                   [[_pack([small_grads[n].reshape(w[n].shape) for n in SMALL])]])[0]
    for dst, packed in zip((grads, deltas, new_m, new_v), sm_out):
        for n, t in zip(SMALL, _unpack(packed, sm_shapes)):
            dst[n] = t

    return (loss, grad_x, *[grads[n] for n in WEIGHT_NAMES], *[deltas[n] for n in WEIGHT_NAMES],
            *[new_m[n] for n in WEIGHT_NAMES], *[new_v[n] for n in WEIGHT_NAMES])
```

```python
import functools
import math

import jax
import jax.numpy as jnp
from jax import lax
from jax.experimental import pallas as pl
from jax.experimental.pallas import tpu as pltpu

N_META = 16
RET_HEADS = 4
HEAD_DIM = 128
SSM_GROUP = 16
SSM_STATE = 64
CHUNK = 128
ROPE_BASE = 10000.0
EPS = 1e-6
FFN_RES = 0.5
N_SEG = 8
N_SEC = 4
N_CHIP = 4
LANE = 128
FFN_CPS = 2
BWD_W_ROWS = 1664

ADAM_LR = 0.001
ADAM_B1 = 0.9
ADAM_B2 = 0.999
ADAM_EPS = 1e-08
ADAM_WD = 0.01
ADAM_STEP = 10

VMEM_LIMIT = 56 * 1024 * 1024

F32 = jnp.float32
BF16 = jnp.bfloat16
MESH = pl.DeviceIdType.MESH


def _dot(a, b):
    return jnp.dot(a, b, preferred_element_type=F32)


def _dot_nt(a, b):
    return lax.dot_general(a, b, (((1,), (1,)), ((), ())), preferred_element_type=F32)


def _dot_tn(a, b):
    return lax.dot_general(a, b, (((0,), (0,)), ((), ())), preferred_element_type=F32)


def _tile(n, target, mult=64):
    best = None
    t = mult
    while t <= min(n, target):
        if n % t == 0:
            best = t
        t += mult
    assert best is not None, (n, target)
    return best


def _params(sem, vmem=VMEM_LIMIT):
    return pltpu.CompilerParams(dimension_semantics=sem, vmem_limit_bytes=vmem)


def _rms_stats(xf):
    r = lax.rsqrt(jnp.mean(xf * xf, axis=-1, keepdims=True) + EPS)
    return xf * r, r


def _rms_bwd(dy, xh, r, w):
    dxh = dy * w
    return r * (dxh - xh * jnp.mean(dxh * xh, axis=-1, keepdims=True))


def _sigmoid(x):
    return 0.5 * jnp.tanh(0.5 * x) + 0.5


GELU_K0 = math.sqrt(2.0 / math.pi)
GELU_K1 = 0.044715


CHIP_MASKS = [(1, 0, 0), (0, 1, 0), (1, 1, 0)]
ALL_MASKS = [(0, 0, 1), (0, 1, 0), (0, 1, 1), (1, 0, 0), (1, 0, 1), (1, 1, 0), (1, 1, 1)]
SIB_MASKS = [(0, 0, 1)]
ANY_SPEC = pl.BlockSpec(memory_space=pl.ANY)
LOCAL_PIECES = 4


class _Plan:
    def __init__(self, arrays, masks, n_slots, src_slotted, dst_slotted, local_copy, half=False, forward=False):
        self.shapes = [(a.shape, a.dtype) for a in arrays]
        self.n = len(arrays)
        self.masks = masks
        self.n_slots = n_slots
        self.src_slotted, self.dst_slotted, self.local_copy = src_slotted, dst_slotted, local_copy
        self.half, self.forward = half, forward
        self.n_cp = self.n * len(masks) * (len(CHIP_MASKS) if forward else 1)

    def out_shape(self):
        out = []
        for shp, dt in self.shapes:
            if self.dst_slotted and not self.src_slotted:
                shp = (self.n_slots,) + shp
            elif self.src_slotted and not self.dst_slotted:
                shp = shp[1:]
            out.append(jax.ShapeDtypeStruct(shp, dt))
        return tuple(out)

    def scratch(self):
        return [pltpu.SemaphoreType.DMA((self.n_cp,)), pltpu.SemaphoreType.DMA((self.n_cp,)),
                pltpu.SemaphoreType.DMA((self.n * LOCAL_PIECES,))]

    def _slot(self, px, py, pc):
        if self.n_slots == 8:
            return 4 * px + 2 * py + pc
        if self.n_slots == 4:
            return 2 * px + py
        return pc

    def copies(self, ins, outs, sems):
        local, remote = self.split_copies(ins, outs, sems)
        return local + remote

    def split_copies(self, ins, outs, sems):
        send_sems, recv_sems, loc_sems = sems
        x, y, c = lax.axis_index("x"), lax.axis_index("y"), lax.axis_index("c")
        me = self._slot(x, y, c)
        n_m = len(self.masks)
        cps, local = [], []
        for a in range(self.n):
            if self.forward:
                rows = self.shapes[a][0][-2] // 2
                mine = pl.ds(pl.multiple_of(c * rows, 8), rows)
                for j, (mx, my, _) in enumerate(CHIP_MASKS):
                    blk = outs[a].at[2 * (1 - x if mx else x) + (1 - y if my else y), mine]
                    k = a * len(CHIP_MASKS) + j
                    cps.append(pltpu.make_async_remote_copy(
                        src_ref=blk, dst_ref=blk, send_sem=send_sems.at[k], recv_sem=recv_sems.at[k],
                        device_id=(x, y, 1 - c), device_id_type=MESH))
                continue
            if self.local_copy:
                src, dst = (ins[a].at[me] if self.src_slotted else ins[a]), outs[a].at[me]
                rows = src.shape[0]
                pieces = LOCAL_PIECES if rows % (16 * LOCAL_PIECES) == 0 else 1
                for q in range(pieces):
                    part = pl.ds(q * (rows // pieces), rows // pieces)
                    local.append(pltpu.make_async_copy(src.at[part], dst.at[part], loc_sems.at[a * LOCAL_PIECES + q]))
            for mi, (mx, my, mc) in enumerate(self.masks):
                px = 1 - x if mx else x
                py = 1 - y if my else y
                pc = 1 - c if mc else c
                src = ins[a].at[self._slot(px, py, pc)] if self.src_slotted else ins[a]
                dst = outs[a].at[me] if self.dst_slotted else outs[a]
                if self.half:
                    rows = src.shape[-2] // 2
                    mine = pl.ds(pl.multiple_of(c * rows, 8), rows)
                    src, dst = src.at[mine], dst.at[mine]
                k = a * n_m + mi
                cps.append(pltpu.make_async_remote_copy(
                    src_ref=src, dst_ref=dst, send_sem=send_sems.at[k], recv_sem=recv_sems.at[k],
                    device_id=(px, py, pc), device_id_type=MESH))
        return local, cps


def _exchange(name, plan, arrays):
    n = plan.n

    def body(*refs):
        cps = plan.copies(refs[:n], refs[n:2 * n], refs[2 * n:])
        for cp in cps:
            cp.start()
        for cp in cps:
            cp.wait()

    outs = pl.pallas_call(
        body, name=name, out_shape=plan.out_shape(),
        in_specs=[ANY_SPEC] * n, out_specs=tuple([ANY_SPEC] * n), scratch_shapes=plan.scratch(),
        input_output_aliases={i: i for i in range(n)} if plan.forward else {},
    )(*arrays)
    return list(outs)


def _pcall(body, *, name, grid, in_specs, out_specs, out_shape, scratch_shapes, args, plan=None, plan_args=()):
    sem = ("arbitrary",) * len(grid)
    if plan is None:
        return pl.pallas_call(body, name=name, grid=grid, in_specs=in_specs, out_specs=out_specs,
                              out_shape=out_shape, scratch_shapes=scratch_shapes,
                              compiler_params=_params(sem))(*args), []
    n_in, n_out, n_scr, n_p = len(in_specs), len(out_specs), len(scratch_shapes), plan.n

    def wrapped(*refs):
        ins = refs[:n_in]
        p_ins = refs[n_in:n_in + n_p]
        o0 = n_in + n_p
        outs = refs[o0:o0 + n_out]
        p_outs = refs[o0 + n_out:o0 + n_out + n_p]
        s0 = o0 + n_out + n_p
        scr = refs[s0:s0 + n_scr]
        sems = refs[s0 + n_scr:]
        ids = [pl.program_id(i) for i in range(len(grid))]
        first = functools.reduce(jnp.logical_and, [i == 0 for i in ids])
        last = functools.reduce(jnp.logical_and, [i == g - 1 for i, g in zip(ids, grid)])

        @pl.when(first)
        def _():
            for cp in plan.copies(p_ins, p_outs, sems):
                cp.start()

        body(*ins, *outs, *scr)

        @pl.when(last)
        def _():
            for cp in plan.copies(p_ins, p_outs, sems):
                cp.wait()

    res = pl.pallas_call(
        wrapped, name=name, grid=grid,
        in_specs=list(in_specs) + [ANY_SPEC] * n_p,
        out_specs=tuple(out_specs) + (ANY_SPEC,) * n_p,
        out_shape=tuple(out_shape) + plan.out_shape(),
        scratch_shapes=list(scratch_shapes) + plan.scratch(),
        compiler_params=_params(sem),
    )(*args, *plan_args)
    return res[:n_out], list(res[n_out:])


def _allgather_chips_plan(arrays):
    return _Plan(arrays, CHIP_MASKS, 4, False, True, True, half=True)


def _gather_two_level(name, arrays):
    n = len(arrays)
    ici = _allgather_chips_plan(arrays)
    fwd = _Plan(ici.out_shape(), SIB_MASKS, 4, True, True, False, forward=True)
    n_m = len(CHIP_MASKS)

    def body(*refs):
        ins, outs, sems = refs[:n], refs[n:2 * n], refs[2 * n:]
        own_cps, ici_cps = ici.split_copies(ins, outs, sems[:3])
        fwd_cps = fwd.copies(None, outs, sems[3:])
        for cp in own_cps + ici_cps:
            cp.start()
        for k in range(n * n_m):
            ici_cps[k].wait_recv()
            fwd_cps[k].start()
        for cp in own_cps:
            cp.wait()
        for cp in ici_cps:
            cp.wait_send()
        for cp in fwd_cps:
            cp.wait()

    return list(pl.pallas_call(
        body, name=name, out_shape=ici.out_shape(),
        in_specs=[ANY_SPEC] * n, out_specs=tuple([ANY_SPEC] * n), scratch_shapes=ici.scratch() + fwd.scratch(),
    )(*arrays))


def _forward_sibling(name, gathered):
    return _exchange(name, _Plan(gathered, SIB_MASKS, 4, True, True, False, forward=True), gathered)


def _alltoall_chips_plan(arrays):
    return _Plan(arrays, CHIP_MASKS, 4, True, True, True)


def _swap_sibling(name, arrays):
    return _exchange(name, _Plan(arrays, SIB_MASKS, 2, False, False, False), arrays)


def _allgather_all_plan(arrays):
    return _Plan(arrays, ALL_MASKS, 8, False, True, True)


def _sum_slots(name, arrs, out_dtype):
    s, r, c = arrs[0].shape
    n = len(arrs)
    tr = _tile(r, 512 if n == 1 else 176, 8)

    def body(*refs):
        for a_ref, o_ref in zip(refs[:n], refs[n:]):
            acc = a_ref[0].astype(F32)
            for i in range(1, s):
                acc = acc + a_ref[i].astype(F32)
            o_ref[...] = acc.astype(out_dtype)

    return list(pl.pallas_call(
        body, name=name, grid=(r // tr,),
        in_specs=[pl.BlockSpec((s, tr, c), lambda i: (0, i, 0))] * n,
        out_specs=(pl.BlockSpec((tr, c), lambda i: (i, 0)),) * n,
        out_shape=(jax.ShapeDtypeStruct((r, c), out_dtype),) * n,
        compiler_params=_params(("arbitrary",)),
    )(*arrs))


def _adam_math(w, g, m, v):
    m_new = ADAM_B1 * m + (1.0 - ADAM_B1) * g
    v_new = ADAM_B2 * v + (1.0 - ADAM_B2) * (g * g)
    m_hat = m_new / (1.0 - ADAM_B1 ** ADAM_STEP)
    v_hat = v_new / (1.0 - ADAM_B2 ** ADAM_STEP)
    delta = -ADAM_LR * (m_hat / (jnp.sqrt(v_hat) + ADAM_EPS) + ADAM_WD * w)
    return delta, m_new, v_new


def _adam(name, wmv, g_parts):
    w0 = wmv[0][0]
    r, c = w0.shape[-2:]
    n_w = len(wmv)
    n_g = len(g_parts[0])
    tr = _tile(r, 256 if n_w == 1 else 88, 8)
    lead = w0.ndim == 3
    at = (lambda ref: ref.at[0]) if lead else (lambda ref: ref)
    n_in = 3 + n_g

    def body(*refs):
        for j in range(n_w):
            ins = refs[j * n_in:(j + 1) * n_in]
            outs = refs[n_w * n_in + 4 * j:n_w * n_in + 4 * j + 4]
            w_ref, m_ref, v_ref = [at(t) for t in ins[:3]]
            g_out, d_out, m_out, v_out = [at(t) for t in outs]
            g = ins[3][...].astype(F32)
            for gr in ins[4:]:
                g = g + gr[...].astype(F32)
            delta, m_new, v_new = _adam_math(w_ref[...], g, m_ref[...], v_ref[...])
            g_out[...] = g
            d_out[...] = delta
            m_out[...] = m_new
            v_out[...] = v_new

    spec = pl.BlockSpec((tr, c), lambda i: (i, 0))
    wspec = pl.BlockSpec((1, tr, c), lambda i: (0, i, 0)) if lead else spec
    shp = jax.ShapeDtypeStruct(w0.shape, F32)
    args = [t for (w, m, v), gp in zip(wmv, g_parts) for t in (w, m, v, *gp)]
    res = pl.pallas_call(
        body, name=name, grid=(r // tr,),
        in_specs=([wspec] * 3 + [spec] * n_g) * n_w, out_specs=(wspec,) * (4 * n_w), out_shape=(shp,) * (4 * n_w),
        compiler_params=_params(("arbitrary",)),
    )(*args)
    return [tuple(res[4 * j:4 * j + 4]) for j in range(n_w)]


SUB_ROWS = 32
FFN_BWD_ROWS = 416
FFN_FWD_ROWS = 832
FFN_LOSS_ROWS = 640
RET_ROWS = 640
S5_STEPS = 104


def _tile_parts(tm, d, head, x):
    nsub = tm // SUB_ROWS
    off = head.shape[0] // SUB_ROWS
    specs = [pl.BlockSpec(head.shape, lambda i, k: (0, 0))] + [
        pl.BlockSpec((SUB_ROWS, d), lambda i, k, j=j: (jnp.maximum(i * nsub + j - off, 0), 0)) for j in range(nsub)]

    def assemble(i, part_refs, h_sc):
        head_ref, x_refs = part_refs[0], part_refs[1:]
        for j in range(nsub):
            rows = slice(j * SUB_ROWS, (j + 1) * SUB_ROWS)
            val = x_refs[j][...]
            if j < off:
                val = jnp.where(i == 0, head_ref[rows, :], val)
            h_sc[rows, :] = val

    return specs, [head] + [x] * nsub, assemble


def _h_source(body, h, tm, d):
    if not isinstance(h, tuple):
        return body, [pl.BlockSpec((tm, d), lambda i, k: (i, 0))], [h], []
    specs, args, assemble = _tile_parts(tm, d, *h)
    n_h = len(specs)

    def with_parts(*refs):
        h_sc = refs[-1]

        @pl.when(pl.program_id(1) == 0)
        def _():
            assemble(pl.program_id(0), refs[:n_h], h_sc)

        body(h_sc, *refs[n_h:-1])

    return with_parts, specs, args, [pltpu.VMEM((tm, d), F32)]


def _ffn_fwd(name, h, nw, wg, wu, wd, plan=None, plan_args=(), loss=None):
    lp, d = (h[0].shape[0] + h[1].shape[0], h[1].shape[1]) if isinstance(h, tuple) else h.shape
    nck, f, _ = wg.shape
    tm = _tile(lp, FFN_FWD_ROWS if loss is None else FFN_LOSS_ROWS)
    last = nck // FFN_CPS - 1
    n_t = 0
    if loss is not None:
        t_specs, t_args, t_assemble = _tile_parts(tm, d, jnp.zeros((lp - loss[1].shape[0], d), F32), loss[1])
        n_t = len(t_specs)

    def body(h_ref, nw_ref, wg_ref, wu_ref, wd_ref, *rest):
        if loss is not None:
            fw_ref, t_parts, rest = rest[0], rest[1:1 + n_t], rest[1 + n_t:]
            ho_ref, g_ref, u_ref, loss_ref, dfw_ref, n_sc, acc_sc, t_sc = rest
        else:
            ho_ref, g_ref, u_ref, n_sc, acc_sc = rest
        i = pl.program_id(0)
        k = pl.program_id(1)

        @pl.when(k == 0)
        def _():
            xh, _ = _rms_stats(h_ref[...])
            n_sc[...] = (xh * nw_ref[...]).astype(BF16)
            acc_sc[...] = jnp.zeros_like(acc_sc)

        n = n_sc[...]
        acc = acc_sc[...]
        for c in range(FFN_CPS):
            g = _dot_nt(n, wg_ref[c])
            u = _dot_nt(n, wu_ref[c])
            g_ref[c] = g.astype(BF16)
            u_ref[c] = u.astype(BF16)
            a = (g * _sigmoid(g) * u).astype(BF16)
            acc = acc + _dot(a, wd_ref[c])
        acc_sc[...] = acc

        if loss is None:
            @pl.when(k == last)
            def _():
                ho_ref[...] = h_ref[...] + FFN_RES * acc_sc[...]
            return

        @pl.when(jnp.logical_and(i == 0, k == 0))
        def _():
            loss_ref[...] = jnp.zeros_like(loss_ref)
            dfw_ref[...] = jnp.zeros_like(dfw_ref)

        @pl.when(k == last)
        def _():
            t_assemble(i, t_parts, t_sc)
            xh, r = _rms_stats(h_ref[...] + FFN_RES * acc_sc[...])
            w = fw_ref[...]
            head_rows = lp - loss[1].shape[0]
            row = lax.broadcasted_iota(jnp.int32, (tm, d), 0) + i * tm
            err = jnp.where(row < head_rows, 0.0, xh * w - t_sc[...])
            loss_ref[...] += 0.5 * jnp.sum(err * err) / d
            dout = err * (1.0 / d)
            dfw_ref[...] += jnp.sum(dout * xh, axis=0, keepdims=True)
            ho_ref[...] = _rms_bwd(dout, xh, r, w)

    body, h_specs, h_args, h_scratch = _h_source(body, h, tm, d)
    vec = pl.BlockSpec((1, d), lambda i, k: (0, 0))
    w_fd = pl.BlockSpec((FFN_CPS, f, d), lambda i, k: (k, 0, 0))
    hid = pl.BlockSpec((FFN_CPS, tm, f), lambda i, k: (k, i, 0))
    hshape = jax.ShapeDtypeStruct((nck, lp, f), BF16)
    args, in_specs = (*h_args, nw, wg, wu, wd), h_specs + [vec, w_fd, w_fd, w_fd]
    out_specs = (pl.BlockSpec((tm, d), lambda i, k: (i, 0)), hid, hid)
    out_shape = (jax.ShapeDtypeStruct((lp, d), F32), hshape, hshape)
    scratch = [pltpu.VMEM((tm, d), BF16), pltpu.VMEM((tm, d), F32)]
    if loss is not None:
        args, in_specs = (*args, loss[0], *t_args), in_specs + [vec] + t_specs
        out_specs += (pl.BlockSpec((8, LANE), lambda i, k: (0, 0)), vec)
        out_shape += (jax.ShapeDtypeStruct((8, LANE), F32), jax.ShapeDtypeStruct((1, d), F32))
        scratch = scratch + [pltpu.VMEM((tm, d), F32)]
    return _pcall(
        body, name=name, grid=(lp // tm, nck // FFN_CPS), plan=plan, plan_args=plan_args,
        args=args, in_specs=in_specs, out_specs=out_specs, out_shape=out_shape,
        scratch_shapes=scratch + h_scratch)


def _ffn_bwd_act(name, dh, h, nw, g, u, wg, wu, wd, plan=None, plan_args=()):
    lp, d = dh.shape
    nck, f, _ = wg.shape
    tm = _tile(lp, FFN_BWD_ROWS, SUB_ROWS)
    last = nck // FFN_CPS - 1

    def body(h_ref, dh_ref, nw_ref, g_ref, u_ref, wg_ref, wu_ref, wd_ref,
             dhi_ref, dnw_ref, n_ref, dacc_ref, a_ref, dg_ref, du_ref,
             xh_sc, r_sc, dn_sc):
        i = pl.program_id(0)
        k = pl.program_id(1)

        @pl.when(k == 0)
        def _():
            xh, r = _rms_stats(h_ref[...])
            xh_sc[...] = xh
            r_sc[...] = r
            n_ref[...] = (xh * nw_ref[...]).astype(BF16)
            dacc_ref[...] = (FFN_RES * dh_ref[...]).astype(BF16)
            dn_sc[...] = jnp.zeros_like(dn_sc)

        @pl.when(jnp.logical_and(i == 0, k == 0))
        def _():
            dnw_ref[...] = jnp.zeros_like(dnw_ref)

        dacc = dacc_ref[...]
        dn = dn_sc[...]
        for c in range(FFN_CPS):
            gv = g_ref[c].astype(F32)
            uv = u_ref[c].astype(F32)
            sg = _sigmoid(gv)
            sil = gv * sg
            da = _dot_nt(dacc, wd_ref[c])
            dgk = (da * uv * (sg * (1.0 + gv * (1.0 - sg)))).astype(BF16)
            duk = (da * sil).astype(BF16)
            a_ref[c] = (sil * uv).astype(BF16)
            dg_ref[c] = dgk
            du_ref[c] = duk
            dn = dn + _dot(dgk, wg_ref[c]) + _dot(duk, wu_ref[c])
        dn_sc[...] = dn

        @pl.when(k == last)
        def _():
            dnl = dn_sc[...]
            xh = xh_sc[...]
            dhi_ref[...] = dh_ref[...] + _rms_bwd(dnl, xh, r_sc[...], nw_ref[...])
            dnw_ref[...] += jnp.sum(dnl * xh, axis=0, keepdims=True)

    body, h_specs, h_args, h_scratch = _h_source(body, h, tm, d)
    row = pl.BlockSpec((tm, d), lambda i, k: (i, 0))
    vec = pl.BlockSpec((1, d), lambda i, k: (0, 0))
    hid = pl.BlockSpec((FFN_CPS, tm, f), lambda i, k: (k, i, 0))
    w_fd = pl.BlockSpec((FFN_CPS, f, d), lambda i, k: (k, 0, 0))
    rshape = jax.ShapeDtypeStruct((lp, d), BF16)
    hshape = jax.ShapeDtypeStruct((nck, lp, f), BF16)
    return _pcall(
        body, name=name, grid=(lp // tm, nck // FFN_CPS), plan=plan, plan_args=plan_args,
        args=(*h_args, dh, nw, g, u, wg, wu, wd),
        in_specs=h_specs + [row, vec, hid, hid, w_fd, w_fd, w_fd],
        out_specs=(row, vec, row, row, hid, hid, hid),
        out_shape=(jax.ShapeDtypeStruct((lp, d), F32), jax.ShapeDtypeStruct((1, d), F32),
                   rshape, rshape, hshape, hshape, hshape),
        scratch_shapes=[pltpu.VMEM((tm, d), F32), pltpu.VMEM((tm, 1), F32), pltpu.VMEM((tm, d), F32)] + h_scratch)


def _ffn_bwd_w(name, n, dacc, a, dg, du, plan=None, plan_args=()):
    lp, d = n.shape
    nck, _, f = a.shape
    tm = _tile(lp, BWD_W_ROWS)
    last = lp // tm - 1

    def body(n_ref, dacc_ref, a_ref, dg_ref, du_ref, dwg_ref, dwu_ref, dwd_ref, ag_sc, au_sc, ad_sc):
        i = pl.program_id(1)

        @pl.when(i == 0)
        def _():
            ag_sc[...] = jnp.zeros_like(ag_sc)
            au_sc[...] = jnp.zeros_like(au_sc)
            ad_sc[...] = jnp.zeros_like(ad_sc)

        nv = n_ref[...]
        ag_sc[...] += _dot_tn(dg_ref[0], nv)
        au_sc[...] += _dot_tn(du_ref[0], nv)
        ad_sc[...] += _dot_tn(a_ref[0], dacc_ref[...])

        @pl.when(i == last)
        def _():
            dwg_ref[0] = ag_sc[...].astype(BF16)
            dwu_ref[0] = au_sc[...].astype(BF16)
            dwd_ref[0] = ad_sc[...].astype(BF16)

    row = pl.BlockSpec((tm, d), lambda k, i: (i, 0))
    hid = pl.BlockSpec((1, tm, f), lambda k, i: (k, i, 0))
    w_fd = pl.BlockSpec((1, f, d), lambda k, i: (k, 0, 0))
    wshape = jax.ShapeDtypeStruct((nck, f, d), BF16)
    return _pcall(
        body, name=name, grid=(nck, lp // tm), plan=plan, plan_args=plan_args, args=(n, dacc, a, dg, du),
        in_specs=[row, row, hid, hid, hid], out_specs=(w_fd, w_fd, w_fd), out_shape=(wshape,) * 3,
        scratch_shapes=[pltpu.VMEM((f, d), F32)] * 3)


def _ffn_bwd_w_scatter(name, n, dacc, a, dg, du, chip, plan, plan_args):
    lp, d = n.shape
    nck, _, f = a.shape
    tm = _tile(lp, BWD_W_ROWS)
    last_i = lp // tm - 1
    n_w = 3
    n_p = plan.n

    def body(me_ref, n_ref, dacc_ref, a_ref, dg_ref, du_ref, *rest):
        p_ins = rest[:n_p]
        recv = rest[n_p:n_p + n_w]
        p_outs = rest[n_p + n_w:2 * n_p + n_w]
        acc = rest[2 * n_p + n_w:2 * n_p + 2 * n_w]
        stage, send_sems, recv_sems, loc_sems = rest[2 * n_p + 2 * n_w:2 * n_p + 2 * n_w + 4]
        p_sems = rest[2 * n_p + 2 * n_w + 4:]
        p = pl.program_id(0)
        i = pl.program_id(1)
        me = me_ref[0]
        c = lax.axis_index("c")

        def send(w, pos):
            kk = jnp.bitwise_xor(me, nck - 1 - pos)
            diff = jnp.bitwise_xor(kk, me)
            m = jnp.where(diff == 2, 0, jnp.where(diff == 1, 1, 2))
            return pltpu.make_async_remote_copy(
                src_ref=stage.at[lax.rem(pos, 2), w], dst_ref=recv[w].at[me],
                send_sem=send_sems.at[w * 3 + m], recv_sem=recv_sems.at[w * 3 + m],
                device_id=(lax.div(kk, 2), lax.rem(kk, 2), c), device_id_type=MESH)

        @pl.when(jnp.logical_and(p == 0, i == 0))
        def _():
            for cp in plan.copies(p_ins, p_outs, p_sems):
                cp.start()

        @pl.when(i == 0)
        def _():
            for t in acc:
                t[...] = jnp.zeros_like(t)

        nv = n_ref[...]
        acc[0][...] += _dot_tn(dg_ref[0], nv)
        acc[1][...] += _dot_tn(du_ref[0], nv)
        acc[2][...] += _dot_tn(a_ref[0], dacc_ref[...])

        @pl.when(jnp.logical_and(i == last_i, p >= 2))
        def _():
            for w in range(n_w):
                send(w, p - 2).wait_send()

        @pl.when(i == last_i)
        def _():
            for w in range(n_w):
                stage[lax.rem(p, 2), w] = acc[w][...].astype(BF16)

        @pl.when(jnp.logical_and(i == last_i, p < nck - 1))
        def _():
            for w in range(n_w):
                send(w, p).start()

        @pl.when(jnp.logical_and(i == last_i, p == nck - 1))
        def _():
            own = [pltpu.make_async_copy(stage.at[(nck - 1) % 2, w], recv[w].at[me], loc_sems.at[w])
                   for w in range(n_w)]
            for cp in own:
                cp.start()
            for w in range(n_w):
                send(w, nck - 2).wait_send()
            for cp in own:
                cp.wait()
            for w in range(n_w):
                for m in range(3):
                    pltpu.make_async_remote_copy(
                        src_ref=stage.at[0, w], dst_ref=recv[w].at[me],
                        send_sem=send_sems.at[w * 3 + m], recv_sem=recv_sems.at[w * 3 + m],
                        device_id=(0, 0, c), device_id_type=MESH).wait_recv()
            for cp in plan.copies(p_ins, p_outs, p_sems):
                cp.wait()

    chunk = lambda k, me_ref: jnp.bitwise_xor(me_ref[0], nck - 1 - k)
    row = pl.BlockSpec((tm, d), lambda k, i, me_ref: (i, 0))
    hid = pl.BlockSpec((1, tm, f), lambda k, i, me_ref: (chunk(k, me_ref), i, 0))
    wshape = jax.ShapeDtypeStruct((nck, f, d), BF16)
    res = pl.pallas_call(
        body, name=name,
        grid_spec=pltpu.PrefetchScalarGridSpec(
            num_scalar_prefetch=1, grid=(nck, lp // tm),
            in_specs=[row, row, hid, hid, hid] + [ANY_SPEC] * n_p,
            out_specs=(ANY_SPEC,) * (n_w + n_p),
            scratch_shapes=[pltpu.VMEM((f, d), F32)] * n_w + [
                pltpu.VMEM((2, n_w, f, d), BF16), pltpu.SemaphoreType.DMA((n_w * 3,)),
                pltpu.SemaphoreType.DMA((n_w * 3,)), pltpu.SemaphoreType.DMA((n_w,))] + plan.scratch()),
        out_shape=(wshape,) * n_w + plan.out_shape(),
        compiler_params=_params(("arbitrary", "arbitrary")),
    )(chip.reshape(1).astype(jnp.int32), n, dacc, a, dg, du, *plan_args)
    return list(res[:n_w]), list(res[n_w:])


def _inproj_fwd(h, nw, w_in, cosf, sinf, rw):
    lp, d = h.shape
    nck, _, ps = w_in.shape
    proj = nck * ps
    sw = proj - 4 * rw
    tm = _tile(lp, 640)
    scale = HEAD_DIM ** -0.5
    heads = rw // HEAD_DIM

    def body(h_ref, nw_ref, w_ref, cos_ref, sin_ref, n_ref, q_ref, k_ref, v_ref, g_ref, u_ref, p_sc):
        xh, _ = _rms_stats(h_ref[...])
        n = (xh * nw_ref[...]).astype(BF16)
        n_ref[...] = n
        for c in range(nck):
            p_sc[:, c * ps:(c + 1) * ps] = _dot(n, w_ref[c])
        cs = cos_ref[...]
        sn = sin_ref[...]
        for hh in range(heads):
            lo = hh * HEAD_DIM
            qh = p_sc[:, lo:lo + HEAD_DIM]
            q_ref[:, lo:lo + HEAD_DIM] = (qh * cs + pltpu.roll(qh, HEAD_DIM // 2, 1) * sn).astype(BF16)
            kh = p_sc[:, rw + lo:rw + lo + HEAD_DIM]
            k_ref[:, lo:lo + HEAD_DIM] = ((kh * cs + pltpu.roll(kh, HEAD_DIM // 2, 1) * sn) * scale).astype(BF16)
        v_ref[...] = p_sc[:, 2 * rw:3 * rw].astype(BF16)
        g_ref[...] = p_sc[:, 3 * rw:4 * rw]
        u_ref[...] = p_sc[:, 4 * rw:]

    row = lambda w: pl.BlockSpec((tm, w), lambda i: (i, 0))
    return pl.pallas_call(
        body, name="inproj_fwd", grid=(lp // tm,),
        in_specs=[row(d), pl.BlockSpec((1, d), lambda i: (0, 0)),
                  pl.BlockSpec((nck, d, ps), lambda i: (0, 0, 0)), row(HEAD_DIM), row(HEAD_DIM)],
        out_specs=(row(d), row(rw), row(rw), row(rw), row(rw), row(sw)),
        out_shape=(jax.ShapeDtypeStruct((lp, d), BF16),
                   jax.ShapeDtypeStruct((lp, rw), BF16),
                   jax.ShapeDtypeStruct((lp, rw), BF16),
                   jax.ShapeDtypeStruct((lp, rw), BF16),
                   jax.ShapeDtypeStruct((lp, rw), F32),
                   jax.ShapeDtypeStruct((lp, sw), F32)),
        scratch_shapes=[pltpu.VMEM((tm, proj), F32)],
        compiler_params=_params(("arbitrary",)),
    )(h, nw, w_in, cosf, sinf)


def _inproj_bwd(dh, h, nw, n, w_in, dq, dk, dv, dg, du):
    lp, d = h.shape
    nck, _, ps = w_in.shape
    rw = dq.shape[1]
    sw = du.shape[1]
    proj = nck * ps
    tm = _tile(lp, 640)
    last = lp // tm - 1

    def gather_dproj(p_sc, dq_ref, dk_ref, dv_ref, dg_ref, du_ref):
        p_sc[:, 0:rw] = dq_ref[...]
        p_sc[:, rw:2 * rw] = dk_ref[...]
        p_sc[:, 2 * rw:3 * rw] = dv_ref[...]
        p_sc[:, 3 * rw:4 * rw] = dg_ref[...]
        p_sc[:, 4 * rw:] = du_ref[...]

    def act_body(dh_ref, h_ref, nw_ref, w_ref, dq_ref, dk_ref, dv_ref, dg_ref, du_ref, dhi_ref, dnw_ref, p_sc):
        i = pl.program_id(0)

        @pl.when(i == 0)
        def _():
            dnw_ref[...] = jnp.zeros_like(dnw_ref)

        gather_dproj(p_sc, dq_ref, dk_ref, dv_ref, dg_ref, du_ref)
        dn = jnp.zeros((tm, d), F32)
        for c in range(nck):
            dn = dn + _dot_nt(p_sc[:, c * ps:(c + 1) * ps], w_ref[c])
        xh, r = _rms_stats(h_ref[...])
        dhi_ref[...] = dh_ref[...] + _rms_bwd(dn, xh, r, nw_ref[...])
        dnw_ref[...] += jnp.sum(dn * xh, axis=0, keepdims=True)

    def w_body(n_ref, dq_ref, dk_ref, dv_ref, dg_ref, du_ref, dw_ref, p_sc, acc_sc):
        i = pl.program_id(0)

        @pl.when(i == 0)
        def _():
            acc_sc[...] = jnp.zeros_like(acc_sc)

        gather_dproj(p_sc, dq_ref, dk_ref, dv_ref, dg_ref, du_ref)
        nv = n_ref[...]
        for c in range(nck):
            acc_sc[c] += _dot_tn(nv, p_sc[:, c * ps:(c + 1) * ps])

        @pl.when(i == last)
        def _():
            dw_ref[...] = acc_sc[...].astype(BF16)

    row = lambda w: pl.BlockSpec((tm, w), lambda i: (i, 0))
    vec = pl.BlockSpec((1, d), lambda i: (0, 0))
    wsp = pl.BlockSpec((nck, d, ps), lambda i: (0, 0, 0))
    dproj_specs = [row(rw), row(rw), row(rw), row(rw), row(sw)]
    dhi, dnw = pl.pallas_call(
        act_body, name="inproj_bwd_act", grid=(lp // tm,),
        in_specs=[row(d), row(d), vec, wsp] + dproj_specs,
        out_specs=(row(d), vec),
        out_shape=(jax.ShapeDtypeStruct((lp, d), F32), jax.ShapeDtypeStruct((1, d), F32)),
        scratch_shapes=[pltpu.VMEM((tm, proj), BF16)],
        compiler_params=_params(("arbitrary",)),
    )(dh, h, nw, w_in, dq, dk, dv, dg, du)
    dw = pl.pallas_call(
        w_body, name="inproj_bwd_w", grid=(lp // tm,),
        in_specs=[row(d)] + dproj_specs,
        out_specs=wsp, out_shape=jax.ShapeDtypeStruct((nck, d, ps), BF16),
        scratch_shapes=[pltpu.VMEM((tm, proj), BF16), pltpu.VMEM((nck, d, ps), F32)],
        compiler_params=_params(("arbitrary",)),
    )(n, dq, dk, dv, dg, du)
    return dhi, dnw, dw


def _retention_tables(rc):
    h = jnp.arange(RET_HEADS, dtype=F32)
    log_g = jnp.log(1.0 - 2.0 ** (-5.0 - h))
    i = jnp.arange(rc)
    diff = i[:, None] - i[None, :]
    dec = jnp.where(diff[None] >= 0,
                    jnp.exp(log_g[:, None, None] * jnp.maximum(diff, 0)[None].astype(F32)), 0.0)
    pos = jnp.arange(rc, dtype=F32)
    wq = jnp.exp(log_g[:, None] * (pos + 1.0)[None])
    wk = jnp.exp(log_g[:, None] * (rc - 1 - pos)[None])
    gch = jnp.exp(log_g * rc)
    ones = jnp.ones((1, 1, HEAD_DIM), F32)
    return (dec, wq[:, :, None] * ones, wk[:, :, None] * ones,
            gch[:, None, None] * jnp.ones((1, 8, HEAD_DIM), F32))


def _head_norm(o):
    mu = jnp.mean(o, axis=-1, keepdims=True)
    oc = o - mu
    r = lax.rsqrt(jnp.mean(oc * oc, axis=-1, keepdims=True) + EPS)
    return oc * r, r


def _ret_fwd(q, k, v, g, rnw, tables):
    lp, rw = q.shape
    heads = rw // HEAD_DIM
    rc = tables[0].shape[1]
    nch = lp // rc
    dec, wq, wk, gch = tables

    def body(q_ref, k_ref, v_ref, g_ref, w_ref, dec_ref, wq_ref, wk_ref, gch_ref,
             o_ref, ret_ref, sp_ref, s_sc):
        n = pl.program_id(0)

        @pl.when(n == 0)
        def _():
            s_sc[...] = jnp.zeros_like(s_sc)

        cols = [slice(hh * HEAD_DIM, (hh + 1) * HEAD_DIM) for hh in range(heads)]
        s_ins = [s_sc[hh] for hh in range(heads)]
        outs = []
        for hh, cs in enumerate(cols):
            qv, kv, vv = q_ref[:, cs], k_ref[:, cs], v_ref[:, cs]
            s_in = s_ins[hh]
            a = _dot_nt(qv, kv) * dec_ref[hh]
            qw = (qv.astype(F32) * wq_ref[hh]).astype(BF16)
            kw = (kv.astype(F32) * wk_ref[hh]).astype(BF16)
            o = _dot(a.astype(BF16), vv) + _dot(qw, s_in.astype(BF16))
            s_new = gch_ref[hh, 0:1, :] * s_in + _dot_tn(kw, vv)
            xh, _ = _head_norm(o)
            gv = g_ref[:, cs]
            outs.append((o, s_new, (gv * _sigmoid(gv) * (xh * w_ref[:, cs])).astype(BF16)))
        for hh, cs in enumerate(cols):
            o, s_new, ret = outs[hh]
            sp_ref[hh, 0] = s_ins[hh]
            s_sc[hh] = s_new
            o_ref[:, cs] = o
            ret_ref[:, cs] = ret

    blk = pl.BlockSpec((rc, rw), lambda n: (n, 0))
    tab = pl.BlockSpec((heads, rc, HEAD_DIM), lambda n: (0, 0, 0))
    dtab = pl.BlockSpec((heads, rc, rc), lambda n: (0, 0, 0))
    return pl.pallas_call(
        body, name="retention_fwd", grid=(nch,),
        in_specs=[blk, blk, blk, blk, pl.BlockSpec((1, rw), lambda n: (0, 0)),
                  dtab, tab, tab, pl.BlockSpec((heads, 8, HEAD_DIM), lambda n: (0, 0, 0))],
        out_specs=(blk, blk, pl.BlockSpec((heads, 1, HEAD_DIM, HEAD_DIM), lambda n: (0, n, 0, 0))),
        out_shape=(jax.ShapeDtypeStruct((lp, rw), F32),
                   jax.ShapeDtypeStruct((lp, rw), BF16),
                   jax.ShapeDtypeStruct((heads, nch, HEAD_DIM, HEAD_DIM), F32)),
        scratch_shapes=[pltpu.VMEM((heads, HEAD_DIM, HEAD_DIM), F32)],
        compiler_params=_params(("arbitrary",)),
    )(q, k, v, g, rnw, dec, wq, wk, gch)


def _ret_bwd(dret, q, k, v, g, o, sprev, rnw, tables, cosf, sinf):
    lp, rw = q.shape
    heads = rw // HEAD_DIM
    rc = tables[0].shape[1]
    nch = lp // rc
    dec, wq, wk, gch = tables
    scale = HEAD_DIM ** -0.5
    half = HEAD_DIM // 2

    def body(dret_ref, q_ref, k_ref, v_ref, g_ref, o_ref, sp_ref, w_ref, dec_ref, wq_ref, wk_ref, gch_ref,
             cos_ref, sin_ref, dq_ref, dk_ref, dv_ref, dg_ref, dw_ref, ds_sc):
        n = pl.program_id(0)

        @pl.when(n == 0)
        def _():
            ds_sc[...] = jnp.zeros_like(ds_sc)
            dw_ref[...] = jnp.zeros_like(dw_ref)

        cosv = cos_ref[...]
        sinv = sin_ref[...]
        cols = [slice(hh * HEAD_DIM, (hh + 1) * HEAD_DIM) for hh in range(heads)]
        ds_ins = [ds_sc[hh] for hh in range(heads)]
        dw_ins = [dw_ref[:, cs] for cs in cols]
        outs = []
        for hh, cs in enumerate(cols):
            qv, kv, vv = q_ref[:, cs], k_ref[:, cs], v_ref[:, cs]
            gv = g_ref[:, cs]
            dr = dret_ref[:, cs]
            w = w_ref[:, cs]
            sg = _sigmoid(gv)
            sil = gv * sg
            xh, r = _head_norm(o_ref[:, cs])
            dgate = (dr * (xh * w) * (sg * (1.0 + gv * (1.0 - sg)))).astype(BF16)
            dyw = dr * sil
            dw_new = dw_ins[hh] + jnp.sum(dyw * xh, axis=0, keepdims=True)
            dxh = dyw * w
            do = r * (dxh - jnp.mean(dxh, axis=-1, keepdims=True)
                      - xh * jnp.mean(dxh * xh, axis=-1, keepdims=True))
            dob = do.astype(BF16)
            dmask = dec_ref[hh]
            wqv = wq_ref[hh]
            wkv = wk_ref[hh]
            a = (_dot_nt(qv, kv) * dmask).astype(BF16)
            da = (_dot_nt(dob, vv) * dmask).astype(BF16)
            qw = (qv.astype(F32) * wqv).astype(BF16)
            kw = (kv.astype(F32) * wkv).astype(BF16)
            s_in = sp_ref[hh, 0].astype(BF16)
            ds = ds_ins[hh]
            dsb = ds.astype(BF16)
            dq = _dot(da, kv) + _dot_nt(dob, s_in) * wqv
            dk = _dot_tn(da, qv) + _dot_nt(vv, dsb) * wkv
            dv = _dot_tn(a, dob) + _dot(kw, dsb)
            ds_new = gch_ref[hh, 0:1, :] * ds + _dot_tn(qw, dob)
            outs.append((dgate, dw_new, ds_new,
                         (dq * cosv + pltpu.roll(dq * sinv, half, 1)).astype(BF16),
                         ((dk * cosv + pltpu.roll(dk * sinv, half, 1)) * scale).astype(BF16),
                         dv.astype(BF16)))
        for hh, cs in enumerate(cols):
            dgate, dw_new, ds_new, dqv, dkv, dvv = outs[hh]
            dg_ref[:, cs] = dgate
            dw_ref[:, cs] = dw_new
            ds_sc[hh] = ds_new
            dq_ref[:, cs] = dqv
            dk_ref[:, cs] = dkv
            dv_ref[:, cs] = dvv

    blk = pl.BlockSpec((rc, rw), lambda n: (nch - 1 - n, 0))
    tab = pl.BlockSpec((heads, rc, HEAD_DIM), lambda n: (0, 0, 0))
    dtab = pl.BlockSpec((heads, rc, rc), lambda n: (0, 0, 0))
    wsp = pl.BlockSpec((1, rw), lambda n: (0, 0))
    pos = pl.BlockSpec((rc, HEAD_DIM), lambda n: (nch - 1 - n, 0))
    bshape = jax.ShapeDtypeStruct((lp, rw), BF16)
    return pl.pallas_call(
        body, name="retention_bwd", grid=(nch,),
        in_specs=[blk, blk, blk, blk, blk, blk,
                  pl.BlockSpec((heads, 1, HEAD_DIM, HEAD_DIM), lambda n: (0, nch - 1 - n, 0, 0)),
                  wsp, dtab, tab, tab, pl.BlockSpec((heads, 8, HEAD_DIM), lambda n: (0, 0, 0)), pos, pos],
        out_specs=(blk, blk, blk, blk, wsp),
        out_shape=(bshape, bshape, bshape, bshape, jax.ShapeDtypeStruct((1, rw), F32)),
        scratch_shapes=[pltpu.VMEM((heads, HEAD_DIM, HEAD_DIM), F32)],
        compiler_params=_params(("arbitrary",)),
    )(dret, q, k, v, g, o, sprev, rnw, dec, wq, wk, gch, cosf, sinf)


SCAN_CW = 512


def _s5_prepare(lam_re, lam_im, log_dt, b_re, b_im):
    dt = jnp.exp(log_dt)[:, None]
    er = jnp.exp(lam_re * dt)
    ar = er * jnp.cos(lam_im * dt)
    ai = er * jnp.sin(lam_im * dt)
    den = lam_re * lam_re + lam_im * lam_im
    fr = ((ar - 1.0) * lam_re + ai * lam_im) / den
    fi = (ai * lam_re - (ar - 1.0) * lam_im) / den
    bbr = fr[..., None] * b_re - fi[..., None] * b_im
    bbi = fr[..., None] * b_im + fi[..., None] * b_re
    return ar, ai, bbr, bbi


def _blockdiag_in(t):
    g, p, n = t.shape
    gs = g // N_SEC
    t = t.reshape(N_SEC, gs, p, n)
    eye = jnp.eye(gs, dtype=t.dtype)
    return jnp.einsum("sgpn,gh->sgphn", t, eye).reshape(N_SEC, gs * p, gs * n)


def _blockdiag_out(m, g, p, n):
    gs = g // N_SEC
    m = m.reshape(N_SEC, gs, p, gs, n)
    eye = jnp.eye(gs, dtype=m.dtype)
    return jnp.einsum("sgphn,gh->sgpn", m, eye).reshape(g, p, n)


def _scan_step(xr_ref, xi_ref, r0, prev, ar_ref, ai_ref, conj, ncols):
    new = []
    for cc in range(ncols // SCAN_CW):
        cs = pl.ds(cc * SCAN_CW, SCAN_CW)
        pr, pi = prev[cc]
        ar = ar_ref[:, cs]
        ai = ai_ref[:, cs]
        if conj:
            nr = ar * pr + ai * pi
            ni = ar * pi - ai * pr
        else:
            nr = ar * pr - ai * pi
            ni = ar * pi + ai * pr
        xr = xr_ref[pl.ds(r0, 8), cs] + nr
        xi = xi_ref[pl.ds(r0, 8), cs] + ni
        xr_ref[pl.ds(r0, 8), cs] = xr
        xi_ref[pl.ds(r0, 8), cs] = xi
        new.append((xr, xi))
    return new


def _scan_chunks(ncols):
    return [pl.ds(cc * SCAN_CW, SCAN_CW) for cc in range(ncols // SCAN_CW)]


def _flat(pairs):
    return tuple(t for p in pairs for t in p)


def _pairs(flat):
    return [(flat[2 * k], flat[2 * k + 1]) for k in range(len(flat) // 2)]


def _shift_rows(z, down):
    row = lax.broadcasted_iota(jnp.int32, z.shape, 0)
    if down:
        return jnp.where(row == 0, 0.0, pltpu.roll(z, 1, 0))
    return jnp.where(row == N_SEG - 1, 0.0, pltpu.roll(z, N_SEG - 1, 0))


def _s5_fwd(u, bsr, bsi, csr, csi, a8r, a8i, al8r, al8i, d, gluw, glub, nw, jb):
    lp, sw = u.shape
    ns = a8r.shape[1]
    rows = N_SEG * jb
    nblk = lp // rows
    secw = sw // N_SEC
    secn = ns // N_SEC

    def local_scan(u_ref, bsr_ref, bsi_ref, ar_ref, ai_ref, xr_ref, xi_ref, pr_sc, pi_sc):
        for s in range(N_SEC):
            ub = u_ref[:, s * secw:(s + 1) * secw].astype(BF16)
            xr_ref[:, s * secn:(s + 1) * secn] = _dot(ub, bsr_ref[s])
            xi_ref[:, s * secn:(s + 1) * secn] = _dot(ub, bsi_ref[s])
        prev = [(pr_sc[:, cs], pi_sc[:, cs]) for cs in _scan_chunks(ns)]
        prev = _scan_step(xr_ref, xi_ref, 0, prev, ar_ref, ai_ref, False, ns)

        def step(j, carry):
            r0 = pl.multiple_of(j * 8, 8)
            return _flat(_scan_step(xr_ref, xi_ref, r0, _pairs(carry), ar_ref, ai_ref, False, ns))

        last = _pairs(lax.fori_loop(1, jb, step, _flat(prev)))
        for cs, (vr, vi) in zip(_scan_chunks(ns), last):
            pr_sc[:, cs] = vr
            pi_sc[:, cs] = vi

    def carry_body(u_ref, bsr_ref, bsi_ref, ar_ref, ai_ref, alr_ref, ali_ref, cr_ref, ci_ref,
                   xr_sc, xi_sc, pr_sc, pi_sc):
        b = pl.program_id(0)

        @pl.when(b == 0)
        def _():
            pr_sc[...] = jnp.zeros_like(pr_sc)
            pi_sc[...] = jnp.zeros_like(pi_sc)

        local_scan(u_ref, bsr_ref, bsi_ref, ar_ref, ai_ref, xr_sc, xi_sc, pr_sc, pi_sc)

        @pl.when(b == nblk - 1)
        def _():
            er = _shift_rows(pr_sc[...], True)
            ei = _shift_rows(pi_sc[...], True)
            alr, ali = alr_ref[...], ali_ref[...]
            cr, ci = er, ei
            for _ in range(N_SEG - 2):
                sr = _shift_rows(cr, True)
                si = _shift_rows(ci, True)
                cr = er + alr * sr - ali * si
                ci = ei + alr * si + ali * sr
            cr_ref[...] = cr
            ci_ref[...] = ci

    ublk = pl.BlockSpec((rows, sw), lambda b: (b, 0))
    bspec = pl.BlockSpec((N_SEC, secw, secn), lambda b: (0, 0, 0))
    cspec = pl.BlockSpec((N_SEC, secn, secw), lambda b: (0, 0, 0))
    s8 = pl.BlockSpec((N_SEG, ns), lambda b: (0, 0))
    vec = pl.BlockSpec((1, sw), lambda b: (0, 0))
    s8shape = jax.ShapeDtypeStruct((N_SEG, ns), F32)
    c0r, c0i = pl.pallas_call(
        carry_body, name="s5_fwd_carry", grid=(nblk,),
        in_specs=[ublk, bspec, bspec, s8, s8, s8, s8],
        out_specs=(s8, s8), out_shape=(s8shape, s8shape),
        scratch_shapes=[pltpu.VMEM((rows, ns), F32), pltpu.VMEM((rows, ns), F32),
                        pltpu.VMEM((N_SEG, ns), F32), pltpu.VMEM((N_SEG, ns), F32)],
        compiler_params=_params(("arbitrary",)),
    )(u, bsr, bsi, a8r, a8i, al8r, al8i)

    def main_body(u_ref, bsr_ref, bsi_ref, csr_ref, csi_ref, ar_ref, ai_ref, c0r_ref, c0i_ref,
                  d_ref, gw_ref, gb_ref, nw_ref, xr_ref, xi_ref, yp_ref, out_ref, pr_sc, pi_sc):
        b = pl.program_id(0)

        @pl.when(b == 0)
        def _():
            pr_sc[...] = c0r_ref[...]
            pi_sc[...] = c0i_ref[...]

        local_scan(u_ref, bsr_ref, bsi_ref, ar_ref, ai_ref, xr_ref, xi_ref, pr_sc, pi_sc)
        for s in range(N_SEC):
            xs = pl.ds(s * secn, secn)
            us = pl.ds(s * secw, secw)
            y = _dot(xr_ref[:, xs].astype(BF16), csr_ref[s]) + _dot(xi_ref[:, xs].astype(BF16), csi_ref[s])
            yp_ref[:, us] = y + d_ref[:, us] * u_ref[:, us]
        yp = yp_ref[...]
        t = jnp.tanh(GELU_K0 * (yp + GELU_K1 * yp * yp * yp))
        y1 = 0.5 * yp * (1.0 + t)
        z = _dot(y1.astype(BF16), gw_ref[...]) + gb_ref[...]
        y2 = y1 * _sigmoid(z)
        xh, _ = _rms_stats(y2)
        out_ref[...] = (xh * nw_ref[...]).astype(BF16)

    xblk = pl.BlockSpec((rows, ns), lambda b: (b, 0))
    xr, xi, yp, out = pl.pallas_call(
        main_body, name="s5_fwd", grid=(nblk,),
        in_specs=[ublk, bspec, bspec, cspec, cspec, s8, s8, s8, s8, vec,
                  pl.BlockSpec((sw, sw), lambda b: (0, 0)), vec, vec],
        out_specs=(xblk, xblk, ublk, ublk),
        out_shape=(jax.ShapeDtypeStruct((lp, ns), F32), jax.ShapeDtypeStruct((lp, ns), F32),
                   jax.ShapeDtypeStruct((lp, sw), F32), jax.ShapeDtypeStruct((lp, sw), BF16)),
        scratch_shapes=[pltpu.VMEM((N_SEG, ns), F32), pltpu.VMEM((N_SEG, ns), F32)],
        compiler_params=_params(("arbitrary",)),
    )(u, bsr, bsi, csr, csi, a8r, a8i, c0r, c0i, d, gluw, glub, nw)
    return xr, xi, c0r, c0i, yp, out


def _s5_bwd(dout, u, yp, xr, xi, c0r, c0i, bsrt, bsit, csrt, csit, a8r, a8i, al8r, al8i, d, gluw, glub, nw, jb):
    lp, sw = u.shape
    ns = a8r.shape[1]
    rows = N_SEG * jb
    nblk = lp // rows
    secw = sw // N_SEC
    secn = ns // N_SEC

    def rowwise_bwd(dout_ref, yp_ref, gw_ref, gb_ref, nw_ref):
        ypv = yp_ref[...]
        t = jnp.tanh(GELU_K0 * (ypv + GELU_K1 * ypv * ypv * ypv))
        y1 = 0.5 * ypv * (1.0 + t)
        dgelu = 0.5 * (1.0 + t) + 0.5 * ypv * (1.0 - t * t) * GELU_K0 * (1.0 + 3.0 * GELU_K1 * ypv * ypv)
        gw = gw_ref[...]
        y1b = y1.astype(BF16)
        sg = _sigmoid(_dot(y1b, gw) + gb_ref[...])
        xh, r = _rms_stats(y1 * sg)
        dov = dout_ref[...]
        dy2 = _rms_bwd(dov, xh, r, nw_ref[...])
        dz = dy2 * y1 * sg * (1.0 - sg)
        dzb = dz.astype(BF16)
        dy1 = dy2 * sg + _dot_nt(dzb, gw)
        return dy1 * dgelu, dov * xh, y1b, dzb, dz

    def lam_scan(dyp_of, csrt_ref, csit_ref, ar_ref, ai_ref, lr_sc, li_sc, nr_sc, ni_sc, extra):
        for s in range(N_SEC):
            db = dyp_of(s)
            lr_sc[:, s * secn:(s + 1) * secn] = _dot(db, csrt_ref[s])
            li_sc[:, s * secn:(s + 1) * secn] = _dot(db, csit_ref[s])
        top = rows - 8
        prev = [(nr_sc[:, cs], ni_sc[:, cs]) for cs in _scan_chunks(ns)]
        prev = _scan_step(lr_sc, li_sc, top, prev, ar_ref, ai_ref, True, ns)
        extra(top, pl.ds(top - 8, 8))

        def step(jj, carry):
            r0 = pl.multiple_of((jb - 1 - jj) * 8, 8)
            rp = pl.multiple_of((jb - 2 - jj) * 8, 8)
            new = _scan_step(lr_sc, li_sc, r0, _pairs(carry), ar_ref, ai_ref, True, ns)
            extra(r0, pl.ds(rp, 8))
            return _flat(new)

        prev = _pairs(lax.fori_loop(1, jb - 1, step, _flat(prev)))
        last = _scan_step(lr_sc, li_sc, 0, prev, ar_ref, ai_ref, True, ns)
        extra(0, None)
        for cs, (vr, vi) in zip(_scan_chunks(ns), last):
            nr_sc[:, cs] = vr
            ni_sc[:, cs] = vi

    def carry_body(dout_ref, yp_ref, u_ref, gw_ref, gb_ref, nw_ref, csrt_ref, csit_ref, ar_ref, ai_ref,
                   alr_ref, ali_ref, cr_ref, ci_ref, dyp_ref, dnw_ref, dgw_ref, dgb_ref, dd_ref,
                   lr_sc, li_sc, nr_sc, ni_sc):
        b = pl.program_id(0)

        @pl.when(b == 0)
        def _():
            nr_sc[...] = jnp.zeros_like(nr_sc)
            ni_sc[...] = jnp.zeros_like(ni_sc)
            for ref in (dnw_ref, dgw_ref, dgb_ref, dd_ref):
                ref[...] = jnp.zeros_like(ref)

        dyp, dnw_rows, y1b, dzb, dz = rowwise_bwd(dout_ref, yp_ref, gw_ref, gb_ref, nw_ref)
        dnw_ref[...] += jnp.sum(dnw_rows, axis=0, keepdims=True)
        dgw_ref[...] += _dot_tn(y1b, dzb)
        dgb_ref[...] += jnp.sum(dz, axis=0, keepdims=True)
        dd_ref[...] += jnp.sum(dyp * u_ref[...], axis=0, keepdims=True)
        dyp_ref[...] = dyp.astype(BF16)
        lam_scan(lambda s: dyp_ref[:, s * secw:(s + 1) * secw], csrt_ref, csit_ref, ar_ref, ai_ref,
                 lr_sc, li_sc, nr_sc, ni_sc, lambda r0, prev_rows: None)

        @pl.when(b == nblk - 1)
        def _():
            fr = _shift_rows(nr_sc[...], False)
            fi = _shift_rows(ni_sc[...], False)
            alr, ali = alr_ref[...], ali_ref[...]
            cr, ci = fr, fi
            for _ in range(N_SEG - 2):
                sr = _shift_rows(cr, False)
                si = _shift_rows(ci, False)
                cr = fr + alr * sr + ali * si
                ci = fi + alr * si - ali * sr
            cr_ref[...] = cr
            ci_ref[...] = ci

    rev = lambda b: (nblk - 1 - b, 0)
    ublk = pl.BlockSpec((rows, sw), rev)
    xblk = pl.BlockSpec((rows, ns), rev)
    s8 = pl.BlockSpec((N_SEG, ns), lambda b: (0, 0))
    vec = pl.BlockSpec((1, sw), lambda b: (0, 0))
    gws = pl.BlockSpec((sw, sw), lambda b: (0, 0))
    btspec = pl.BlockSpec((N_SEC, secn, secw), lambda b: (0, 0, 0))
    ctspec = pl.BlockSpec((N_SEC, secw, secn), lambda b: (0, 0, 0))
    s8shape = jax.ShapeDtypeStruct((N_SEG, ns), F32)
    lcr, lci, dyp_all, d_nw, d_gw, d_gb, d_d = pl.pallas_call(
        carry_body, name="s5_bwd_carry", grid=(nblk,),
        in_specs=[ublk, ublk, ublk, gws, vec, vec, ctspec, ctspec, s8, s8, s8, s8],
        out_specs=(s8, s8, ublk, vec, gws, vec, vec),
        out_shape=(s8shape, s8shape, jax.ShapeDtypeStruct((lp, sw), BF16), jax.ShapeDtypeStruct((1, sw), F32),
                   jax.ShapeDtypeStruct((sw, sw), F32), jax.ShapeDtypeStruct((1, sw), F32),
                   jax.ShapeDtypeStruct((1, sw), F32)),
        scratch_shapes=[pltpu.VMEM((rows, ns), F32), pltpu.VMEM((rows, ns), F32),
                        pltpu.VMEM((N_SEG, ns), F32), pltpu.VMEM((N_SEG, ns), F32)],
        compiler_params=_params(("arbitrary",)),
    )(dout, yp, u, gluw, glub, nw, csrt, csit, a8r, a8i, al8r, al8i)

    def main_body(dyp_sc, u_ref, xr_ref, xi_ref, xtr_ref, xti_ref, c0r_ref, c0i_ref, lcr_ref, lci_ref,
                  d_ref, bsrt_ref, bsit_ref, csrt_ref, csit_ref, ar_ref, ai_ref,
                  du_ref, dcr_ref, dci_ref, dbr_ref, dbi_ref, dar_ref, dai_ref,
                  lr_sc, li_sc, nr_sc, ni_sc):
        b = pl.program_id(0)

        @pl.when(b == 0)
        def _():
            nr_sc[...] = lcr_ref[...]
            ni_sc[...] = lci_ref[...]
            for ref in (dcr_ref, dci_ref, dbr_ref, dbi_ref, dar_ref, dai_ref):
                ref[...] = jnp.zeros_like(ref)

        for s in range(N_SEC):
            db = dyp_sc[:, s * secw:(s + 1) * secw]
            xs = pl.ds(s * secn, secn)
            dcr_ref[s] += _dot_tn(xr_ref[:, xs].astype(BF16), db)
            dci_ref[s] += _dot_tn(xi_ref[:, xs].astype(BF16), db)

        first = b == nblk - 1

        def acc_da(r0, prev_rows):
            for cc in range(ns // SCAN_CW):
                cs = pl.ds(cc * SCAN_CW, SCAN_CW)
                lr = lr_sc[pl.ds(r0, 8), cs]
                li = li_sc[pl.ds(r0, 8), cs]
                if prev_rows is None:
                    xpr = jnp.where(first, c0r_ref[:, cs], xtr_ref[:, cs])
                    xpi = jnp.where(first, c0i_ref[:, cs], xti_ref[:, cs])
                else:
                    xpr = xr_ref[prev_rows, cs]
                    xpi = xi_ref[prev_rows, cs]
                dar_ref[:, cs] += lr * xpr + li * xpi
                dai_ref[:, cs] += li * xpr - lr * xpi

        lam_scan(lambda s: dyp_sc[:, s * secw:(s + 1) * secw], csrt_ref, csit_ref, ar_ref, ai_ref,
                 lr_sc, li_sc, nr_sc, ni_sc, acc_da)

        for s in range(N_SEC):
            xs = pl.ds(s * secn, secn)
            us = pl.ds(s * secw, secw)
            lrb = lr_sc[:, xs].astype(BF16)
            lib = li_sc[:, xs].astype(BF16)
            du = _dot(lrb, bsrt_ref[s]) + _dot(lib, bsit_ref[s]) + d_ref[:, us] * dyp_sc[:, us].astype(F32)
            du_ref[:, us] = du.astype(BF16)
            ub = u_ref[:, us].astype(BF16)
            dbr_ref[s] += _dot_tn(ub, lrb)
            dbi_ref[s] += _dot_tn(ub, lib)

    tail = pl.BlockSpec((N_SEG, ns), lambda b: (jnp.maximum((nblk - 1 - b) * jb - 1, 0), 0))
    acc_c = pl.BlockSpec((N_SEC, secn, secw), lambda b: (0, 0, 0))
    acc_b = pl.BlockSpec((N_SEC, secw, secn), lambda b: (0, 0, 0))
    du, dcr, dci, dbr, dbi, dar, dai = pl.pallas_call(
        main_body, name="s5_bwd", grid=(nblk,),
        in_specs=[ublk, ublk, xblk, xblk, tail, tail, s8, s8, s8, s8,
                  vec, btspec, btspec, ctspec, ctspec, s8, s8],
        out_specs=(ublk, acc_c, acc_c, acc_b, acc_b, s8, s8),
        out_shape=(jax.ShapeDtypeStruct((lp, sw), BF16),
                   jax.ShapeDtypeStruct((N_SEC, secn, secw), F32),
                   jax.ShapeDtypeStruct((N_SEC, secn, secw), F32),
                   jax.ShapeDtypeStruct((N_SEC, secw, secn), F32),
                   jax.ShapeDtypeStruct((N_SEC, secw, secn), F32),
                   s8shape, s8shape),
        scratch_shapes=[pltpu.VMEM((rows, ns), F32), pltpu.VMEM((rows, ns), F32),
                        pltpu.VMEM((N_SEG, ns), F32), pltpu.VMEM((N_SEG, ns), F32)],
        compiler_params=_params(("arbitrary",)),
    )(dyp_all, u, xr, xi, xr, xi, c0r, c0i, lcr, lci, d, bsrt, bsit, csrt, csit, a8r, a8i)
    return du, d_nw, d_gw, d_gb, d_d, dcr, dci, dbr, dbi, dar, dai


def _outproj_fwd(h, ret, ssm, wo):
    lp, d = h.shape
    nck, rs, _ = wo.shape
    rw = ret.shape[1]
    tm = _tile(lp, 640)
    per = rw // rs

    def body(h_ref, ret_ref, ssm_ref, w_ref, o_ref):
        acc = h_ref[...]
        for c in range(nck):
            src = ret_ref if c < per else ssm_ref
            lo = (c % per) * rs
            acc = acc + _dot(src[:, lo:lo + rs], w_ref[c])
        o_ref[...] = acc

    row = lambda w: pl.BlockSpec((tm, w), lambda i: (i, 0))
    return pl.pallas_call(
        body, name="outproj_fwd", grid=(lp // tm,),
        in_specs=[row(d), row(rw), row(ssm.shape[1]), pl.BlockSpec((nck, rs, d), lambda i: (0, 0, 0))],
        out_specs=row(d), out_shape=jax.ShapeDtypeStruct((lp, d), F32),
        compiler_params=_params(("arbitrary",)),
    )(h, ret, ssm, wo)


def _outproj_bwd(dh, ret, ssm, wo):
    lp, d = dh.shape
    nck, rs, _ = wo.shape
    rw = ret.shape[1]
    sw = ssm.shape[1]
    tm = _tile(lp, 640)
    per = rw // rs
    last = lp // tm - 1

    def body(dh_ref, ret_ref, ssm_ref, w_ref, dret_ref, dssm_ref, dw_ref, acc_sc):
        i = pl.program_id(0)

        @pl.when(i == 0)
        def _():
            acc_sc[...] = jnp.zeros_like(acc_sc)

        dhb = dh_ref[...].astype(BF16)
        for c in range(nck):
            src, dst = (ret_ref, dret_ref) if c < per else (ssm_ref, dssm_ref)
            lo = (c % per) * rs
            dst[:, lo:lo + rs] = _dot_nt(dhb, w_ref[c])
            acc_sc[c] += _dot_tn(src[:, lo:lo + rs], dhb)

        @pl.when(i == last)
        def _():
            dw_ref[...] = acc_sc[...].astype(BF16)

    row = lambda w: pl.BlockSpec((tm, w), lambda i: (i, 0))
    wsp = pl.BlockSpec((nck, rs, d), lambda i: (0, 0, 0))
    return pl.pallas_call(
        body, name="outproj_bwd", grid=(lp // tm,),
        in_specs=[row(d), row(rw), row(sw), wsp],
        out_specs=(row(rw), row(sw), wsp),
        out_shape=(jax.ShapeDtypeStruct((lp, rw), F32), jax.ShapeDtypeStruct((lp, sw), F32),
                   jax.ShapeDtypeStruct((nck, rs, d), BF16)),
        scratch_shapes=[pltpu.VMEM((nck, rs, d), F32)],
        compiler_params=_params(("arbitrary",)),
    )(dh, ret, ssm, wo)


def _pack(arrs):
    flat = jnp.concatenate([a.reshape(-1).astype(F32) for a in arrs])
    n = flat.shape[0]
    rows = -(-n // (8 * LANE)) * 8
    return jnp.pad(flat, (0, rows * LANE - n)).reshape(rows, LANE)


def _unpack(packed, shapes):
    flat = packed.reshape(-1)
    out, off = [], 0
    for s in shapes:
        n = math.prod(s)
        out.append(flat[off:off + n].reshape(s))
        off += n
    return out


def _to_segments(a, seg_len):
    return a.reshape(N_SEG, seg_len, a.shape[1]).transpose(1, 0, 2).reshape(a.shape)


def _from_segments(a, seg_len):
    return a.reshape(seg_len, N_SEG, a.shape[1]).transpose(1, 0, 2).reshape(a.shape)


WEIGHT_NAMES = ['meta_tokens', 'ffn1_norm_w', 'ffn1_w_gate', 'ffn1_w_up', 'ffn1_w_down', 'mix_norm_w', 'w_in',
                'ret_norm_w', 'ssm_lambda_re', 'ssm_lambda_im', 'ssm_log_dt', 'ssm_b_re', 'ssm_b_im', 'ssm_c_re',
                'ssm_c_im', 'ssm_d', 'ssm_glu_w', 'ssm_glu_b', 'ssm_norm_w', 'w_out', 'ffn2_norm_w', 'ffn2_w_gate',
                'ffn2_w_up', 'ffn2_w_down', 'final_norm_w']
BIG = ['ffn1_w_gate', 'ffn1_w_up', 'ffn1_w_down', 'w_in', 'ssm_glu_w', 'w_out', 'ffn2_w_gate', 'ffn2_w_up',
       'ffn2_w_down']
TRANSPOSED = ['ffn1_w_gate', 'ffn1_w_up', 'ffn2_w_gate', 'ffn2_w_up']
BIG_EARLY = ['ffn1_w_gate', 'ffn1_w_up', 'ffn1_w_down']
BIG_LATE = [n for n in BIG if n not in BIG_EARLY]
SMALL = [n for n in WEIGHT_NAMES if n not in BIG]


def kernel(x, meta_tokens, ffn1_norm_w, ffn1_w_gate, ffn1_w_up, ffn1_w_down, mix_norm_w, w_in, ret_norm_w, ssm_lambda_re, ssm_lambda_im, ssm_log_dt, ssm_b_re, ssm_b_im, ssm_c_re, ssm_c_im, ssm_d, ssm_glu_w, ssm_glu_b, ssm_norm_w, w_out, ffn2_norm_w, ffn2_w_gate, ffn2_w_up, ffn2_w_down, final_norm_w, loss_target, m_meta_tokens, m_ffn1_norm_w, m_ffn1_w_gate, m_ffn1_w_up, m_ffn1_w_down, m_mix_norm_w, m_w_in, m_ret_norm_w, m_ssm_lambda_re, m_ssm_lambda_im, m_ssm_log_dt, m_ssm_b_re, m_ssm_b_im, m_ssm_c_re, m_ssm_c_im, m_ssm_d, m_ssm_glu_w, m_ssm_glu_b, m_ssm_norm_w, m_w_out, m_ffn2_norm_w, m_ffn2_w_gate, m_ffn2_w_up, m_ffn2_w_down, m_final_norm_w, v_meta_tokens, v_ffn1_norm_w, v_ffn1_w_gate, v_ffn1_w_up, v_ffn1_w_down, v_mix_norm_w, v_w_in, v_ret_norm_w, v_ssm_lambda_re, v_ssm_lambda_im, v_ssm_log_dt, v_ssm_b_re, v_ssm_b_im, v_ssm_c_re, v_ssm_c_im, v_ssm_d, v_ssm_glu_w, v_ssm_glu_b, v_ssm_norm_w, v_w_out, v_ffn2_norm_w, v_ffn2_w_gate, v_ffn2_w_up, v_ffn2_w_down, v_final_norm_w):
    args = locals()
    w = {n: args[n] for n in WEIGHT_NAMES}
    m = {n: args["m_" + n] for n in WEIGHT_NAMES}
    v = {n: args["v_" + n] for n in WEIGHT_NAMES}

    seq, d = x.shape[1], x.shape[2]
    lp = seq + CHUNK
    seg_len = lp // N_SEG
    rw = RET_HEADS * HEAD_DIM
    sw = ssm_d.shape[-1]
    groups = sw // SSM_GROUP
    ns = groups * SSM_STATE
    jb = _tile(seg_len, S5_STEPS, 8)
    chip = 2 * lax.axis_index("x") + lax.axis_index("y")

    as_fd = lambda t: jnp.swapaxes(t, -1, -2)
    shards = {n: (as_fd(w[n][0]) if n in TRANSPOSED else w[n][0]).astype(BF16) for n in BIG}
    early = [shards[n] for n in BIG_EARLY] + [meta_tokens]
    gathered = _gather_two_level("gather_early", early)
    gw = dict(zip(BIG_EARLY, gathered[:-1]))
    meta_full = jnp.transpose(gathered[-1], (1, 0, 2)).reshape(N_META, d)
    late = [shards[n] for n in BIG_LATE]

    freqs = 1.0 / (ROPE_BASE ** (jnp.arange(0, HEAD_DIM, 2, dtype=F32) / HEAD_DIM))
    ang_c = (jnp.arange(lp // CHUNK, dtype=F32) * CHUNK - float(CHUNK - N_META))[:, None] * freqs[None, :]
    ang_r = jnp.arange(CHUNK, dtype=F32)[:, None] * freqs[None, :]
    cos_c, sin_c = jnp.cos(ang_c)[:, None, :], jnp.sin(ang_c)[:, None, :]
    cos_r, sin_r = jnp.cos(ang_r)[None], jnp.sin(ang_r)[None]
    cos_t = (cos_c * cos_r - sin_c * sin_r).reshape(lp, HEAD_DIM // 2)
    sin_t = (sin_c * cos_r + cos_c * sin_r).reshape(lp, HEAD_DIM // 2)
    cosf = jnp.concatenate([cos_t, cos_t], axis=1)
    sinf = jnp.concatenate([-sin_t, sin_t], axis=1)
    tables = _retention_tables(_tile(lp, RET_ROWS, CHUNK))

    lam_re, lam_im, log_dt = ssm_lambda_re[0], ssm_lambda_im[0], ssm_log_dt[0]
    b_re, b_im, c_re, c_im = ssm_b_re[0], ssm_b_im[0], ssm_c_re[0], ssm_c_im[0]
    (ar, ai, bbr, bbi), prep_vjp = jax.vjp(_s5_prepare, lam_re, lam_im, log_dt, b_re, b_im)
    dt = jnp.exp(log_dt)[:, None]
    el = jnp.exp(seg_len * lam_re * dt)
    alr = el * jnp.cos(seg_len * lam_im * dt)
    ali = el * jnp.sin(seg_len * lam_im * dt)
    bc8 = lambda t: jnp.broadcast_to(t.reshape(1, ns), (N_SEG, ns))
    a8r, a8i, al8r, al8i = bc8(ar), bc8(ai), bc8(alr), bc8(ali)
    bsr = _blockdiag_in(jnp.transpose(bbr, (0, 2, 1)))
    bsi = _blockdiag_in(jnp.transpose(bbi, (0, 2, 1)))
    csrt = _blockdiag_in(c_re)
    csit = _blockdiag_in(-c_im)
    tr = lambda t: jnp.transpose(t, (0, 2, 1))
    bsr_b, bsi_b = bsr.astype(BF16), bsi.astype(BF16)
    csr_b, csi_b = tr(csrt).astype(BF16), tr(csit).astype(BF16)
    bsrt_b, bsit_b = tr(bsr).astype(BF16), tr(bsi).astype(BF16)
    csrt_b, csit_b = csrt.astype(BF16), csit.astype(BF16)

    h0 = (jnp.concatenate([jnp.zeros((CHUNK - N_META, d), F32), meta_full], axis=0), x[0])
    (h1, g1, u1), late_half = _ffn_fwd("ffn1_fwd", h0, ffn1_norm_w, gw['ffn1_w_gate'], gw['ffn1_w_up'],
                                       gw['ffn1_w_down'], _allgather_chips_plan(late), late)
    gw.update(zip(BIG_LATE, _forward_sibling("gather_late_forward", late_half)))
    glu_full = gw['ssm_glu_w'].reshape(sw, sw)
    n2, q, k, vv, gate, u = _inproj_fwd(h1, mix_norm_w, gw['w_in'], cosf, sinf, rw)
    o, ret, sprev = _ret_fwd(q, k, vv, gate, ret_norm_w, tables)
    u_seg = _to_segments(u, seg_len)
    xr, xi, c0r, c0i, yp, ssm_seg = _s5_fwd(u_seg, bsr_b, bsi_b, csr_b, csi_b, a8r, a8i, al8r, al8i,
                                            ssm_d, glu_full, ssm_glu_b, ssm_norm_w, jb)
    ssm = _from_segments(ssm_seg, seg_len)
    h2 = _outproj_fwd(h1, ret, ssm, gw['w_out'])
    (dh3, g2, u2, loss_part, d_final), _ = _ffn_fwd(
        "ffn2_fwd_loss", h2, ffn2_norm_w, gw['ffn2_w_gate'], gw['ffn2_w_up'], gw['ffn2_w_down'],
        loss=(final_norm_w.reshape(1, d), loss_target[0]))

    (dh2, d_ffn2_norm, nb, daccb, ab, dgb, dub), _ = _ffn_bwd_act(
        "ffn2_bwd_act", dh3, h2, ffn2_norm_w, g2, u2, gw['ffn2_w_gate'], gw['ffn2_w_up'], gw['ffn2_w_down'])
    (dwg2, dwu2, dwd2), _ = _ffn_bwd_w("ffn2_bwd_w", nb, daccb, ab, dgb, dub)
    dret, dssm, dwo = _outproj_bwd(dh2, ret, ssm, gw['w_out'])
    (du_seg, d_ssm_norm, d_glu_w, d_glu_b, d_ssm_d, dcr_s, dci_s, dbr_s, dbi_s, dar8, dai8) = _s5_bwd(
        _to_segments(dssm, seg_len), u_seg, yp, xr, xi, c0r, c0i, bsrt_b, bsit_b, csrt_b, csit_b,
        a8r, a8i, al8r, al8i, ssm_d, glu_full, ssm_glu_b, ssm_norm_w, jb)
    du = _from_segments(du_seg, seg_len)
    dq, dk, dv, dgate, d_ret_norm = _ret_bwd(dret, q, k, vv, gate, o, sprev, ret_norm_w, tables, cosf, sinf)
    dh1, d_mix_norm, dwin = _inproj_bwd(dh2, h1, mix_norm_w, n2, gw['w_in'], dq, dk, dv, dgate, du)
    late_parts = {
        'w_in': dwin, 'ssm_glu_w': d_glu_w.reshape(N_CHIP, sw // N_CHIP, sw).astype(BF16), 'w_out': dwo,
        'ffn2_w_gate': dwg2, 'ffn2_w_up': dwu2, 'ffn2_w_down': dwd2,
    }
    late_list = [late_parts[n] for n in BIG_LATE]
    (dh0, d_ffn1_norm, nb, daccb, ab, dgb, dub), late_recv = _ffn_bwd_act(
        "ffn1_bwd_act", dh1, h0, ffn1_norm_w, g1, u1, gw['ffn1_w_gate'], gw['ffn1_w_up'], gw['ffn1_w_down'],
        _alltoall_chips_plan(late_list), late_list)
    grad_x = dh0[CHUNK:][None]
    d_meta = dh0[CHUNK - N_META:CHUNK]

    d_c_re = _blockdiag_out(tr(dcr_s), groups, SSM_GROUP, SSM_STATE)
    d_c_im = -_blockdiag_out(tr(dci_s), groups, SSM_GROUP, SSM_STATE)
    d_bbr = jnp.transpose(_blockdiag_out(dbr_s, groups, SSM_GROUP, SSM_STATE), (0, 2, 1))
    d_bbi = jnp.transpose(_blockdiag_out(dbi_s, groups, SSM_GROUP, SSM_STATE), (0, 2, 1))
    d_ar = jnp.sum(dar8, axis=0).reshape(groups, SSM_STATE)
    d_ai = jnp.sum(dai8, axis=0).reshape(groups, SSM_STATE)
    small_parts = [loss_part[0:1, :], d_meta, d_ffn1_norm, d_mix_norm, d_ret_norm, d_ar, d_ai, d_bbr, d_bbi,
                   d_c_re, d_c_im, d_ssm_d, d_glu_b, d_ssm_norm, d_ffn2_norm, d_final]
    small_shapes = [a.shape for a in small_parts]
    packed = _pack(small_parts)
    early_recv, (all_parts,) = _ffn_bwd_w_scatter("ffn1_bwd_w", nb, daccb, ab, dgb, dub, chip,
                                                  _allgather_all_plan([packed]), [packed])
    received = dict(zip(BIG_LATE + BIG_EARLY, late_recv + early_recv))
    ffn_names = [n for n in BIG if n.startswith('ffn')]
    chip_sum = dict(zip(ffn_names, _sum_slots("sum_chips_ffn", [received[n] for n in ffn_names], BF16)))
    for n in BIG:
        if n not in chip_sum:
            chip_sum[n] = _sum_slots("sum_chips_" + n, [received[n]], BF16)[0]
    chip_sums = [chip_sum[n] for n in BIG]
    sib_sums = _swap_sibling("swap_sibling", chip_sums)
    (loss_row, g_meta_full, g_ffn1_norm, g_mix_norm, g_ret_norm, g_ar, g_ai, g_bbr, g_bbi, g_c_re, g_c_im,
     g_ssm_d, g_glu_b, g_ssm_norm, g_ffn2_norm, g_final) = _unpack(_sum_slots("sum_small", [all_parts], F32)[0],
                                                                  small_shapes)
    g_lam_re, g_lam_im, g_log_dt, g_b_re, g_b_im = prep_vjp((g_ar, g_ai, g_bbr, g_bbi))
    loss = loss_row[0, 0]
    g_meta = lax.dynamic_slice(g_meta_full, (0, chip * (d // N_CHIP)), (N_META, d // N_CHIP))
    small_grads = {
        'meta_tokens': g_meta, 'ffn1_norm_w': g_ffn1_norm, 'mix_norm_w': g_mix_norm, 'ret_norm_w': g_ret_norm,
        'ssm_lambda_re': g_lam_re[None], 'ssm_lambda_im': g_lam_im[None], 'ssm_log_dt': g_log_dt[None],
        'ssm_b_re': g_b_re[None], 'ssm_b_im': g_b_im[None], 'ssm_c_re': g_c_re[None], 'ssm_c_im': g_c_im[None],
        'ssm_d': g_ssm_d, 'ssm_glu_b': g_glu_b, 'ssm_norm_w': g_ssm_norm, 'ffn2_norm_w': g_ffn2_norm,
        'final_norm_w': g_final.reshape(d),
    }

    grads, deltas, new_m, new_v = {}, {}, {}, {}
    g_pair = {n: [mine, sib] for n, mine, sib in zip(BIG, chip_sums, sib_sums)}
    view = lambda n, t: as_fd(t) if n in TRANSPOSED else t
    ffn_out = _adam("adam_ffn", [(view(n, w[n]), view(n, m[n]), view(n, v[n])) for n in ffn_names],
                    [g_pair[n] for n in ffn_names])
    for n, outs in zip(ffn_names, ffn_out):
        grads[n], deltas[n], new_m[n], new_v[n] = [view(n, t) for t in outs]
    for n in BIG:
        if n not in ffn_names:
            grads[n], deltas[n], new_m[n], new_v[n] = _adam("adam_" + n, [(w[n], m[n], v[n])], [g_pair[n]])[0]
    sm_shapes = [w[n].shape for n in SMALL]
    sm_out = _adam("adam_small", [(_pack([w[n] for n in SMALL]), _pack([m[n] for n in SMALL]),
                                  _pack([v[n] for n in SMALL]))],
                   [[_pack([small_grads[n].reshape(w[n].shape) for n in SMALL])]])[0]
    for dst, packed in zip((grads, deltas, new_m, new_v), sm_out):
        for n, t in zip(SMALL, _unpack(packed, sm_shapes)):
            dst[n] = t

    return (loss, grad_x, *[grads[n] for n in WEIGHT_NAMES], *[deltas[n] for n in WEIGHT_NAMES],
            *[new_m[n] for n in WEIGHT_NAMES], *[new_v[n] for n in WEIGHT_NAMES])
```

```python
import functools
import math

import jax
import jax.numpy as jnp
from jax import lax
from jax.experimental import pallas as pl
from jax.experimental.pallas import tpu as pltpu

N_META = 16
RET_HEADS = 4
HEAD_DIM = 128
SSM_GROUP = 16
SSM_STATE = 64
CHUNK = 128
ROPE_BASE = 10000.0
EPS = 1e-6
FFN_RES = 0.5
N_SEG = 8
N_SEC = 4
N_CHIP = 4
LANE = 128
FFN_CPS = 2
BWD_W_ROWS = 1664

ADAM_LR = 0.001
ADAM_B1 = 0.9
ADAM_B2 = 0.999
ADAM_EPS = 1e-08
ADAM_WD = 0.01
ADAM_STEP = 10

VMEM_LIMIT = 56 * 1024 * 1024

F32 = jnp.float32
BF16 = jnp.bfloat16
MESH = pl.DeviceIdType.MESH


def _dot(a, b):
    return jnp.dot(a, b, preferred_element_type=F32)


def _dot_nt(a, b):
    return lax.dot_general(a, b, (((1,), (1,)), ((), ())), preferred_element_type=F32)


def _dot_tn(a, b):
    return lax.dot_general(a, b, (((0,), (0,)), ((), ())), preferred_element_type=F32)


def _tile(n, target, mult=64):
    best = None
    t = mult
    while t <= min(n, target):
        if n % t == 0:
            best = t
        t += mult
    assert best is not None, (n, target)
    return best


def _params(sem, vmem=VMEM_LIMIT):
    return pltpu.CompilerParams(dimension_semantics=sem, vmem_limit_bytes=vmem)


def _rms_stats(xf):
    r = lax.rsqrt(jnp.mean(xf * xf, axis=-1, keepdims=True) + EPS)
    return xf * r, r


def _rms_bwd(dy, xh, r, w):
    dxh = dy * w
    return r * (dxh - xh * jnp.mean(dxh * xh, axis=-1, keepdims=True))


def _sigmoid(x):
    return 0.5 * jnp.tanh(0.5 * x) + 0.5


GELU_K0 = math.sqrt(2.0 / math.pi)
GELU_K1 = 0.044715


CHIP_MASKS = [(1, 0, 0), (0, 1, 0), (1, 1, 0)]
ALL_MASKS = [(0, 0, 1), (0, 1, 0), (0, 1, 1), (1, 0, 0), (1, 0, 1), (1, 1, 0), (1, 1, 1)]
SIB_MASKS = [(0, 0, 1)]
ANY_SPEC = pl.BlockSpec(memory_space=pl.ANY)
MULTI_SUM_STEPS = 4
MULTI_ADAM_STEPS = 8


class _Plan:
    def __init__(self, arrays, masks, n_slots, src_slotted, dst_slotted, local_copy, half=False, forward=False):
        self.shapes = [(a.shape, a.dtype) for a in arrays]
        self.n = len(arrays)
        self.masks = masks
        self.n_slots = n_slots
        self.src_slotted, self.dst_slotted, self.local_copy = src_slotted, dst_slotted, local_copy
        self.half, self.forward = half, forward
        self.n_cp = self.n * len(masks) * (len(CHIP_MASKS) if forward else 1)

    def out_shape(self):
        out = []
        for shp, dt in self.shapes:
            if self.dst_slotted and not self.src_slotted:
                shp = (self.n_slots,) + shp
            elif self.src_slotted and not self.dst_slotted:
                shp = shp[1:]
            out.append(jax.ShapeDtypeStruct(shp, dt))
        return tuple(out)

    def scratch(self):
        return [pltpu.SemaphoreType.DMA((self.n_cp,)), pltpu.SemaphoreType.DMA((self.n_cp,)),
                pltpu.SemaphoreType.DMA((self.n,))]

    def _slot(self, px, py, pc):
        if self.n_slots == 8:
            return 4 * px + 2 * py + pc
        if self.n_slots == 4:
            return 2 * px + py
        return pc

    def copies(self, ins, outs, sems):
        send_sems, recv_sems, loc_sems = sems
        x, y, c = lax.axis_index("x"), lax.axis_index("y"), lax.axis_index("c")
        me = self._slot(x, y, c)
        n_m = len(self.masks)
        cps = []
        for a in range(self.n):
            if self.forward:
                rows = self.shapes[a][0][-2] // 2
                mine = pl.ds(pl.multiple_of(c * rows, 8), rows)
                for j, (mx, my, _) in enumerate(CHIP_MASKS):
                    blk = outs[a].at[2 * (1 - x if mx else x) + (1 - y if my else y), mine]
                    k = a * len(CHIP_MASKS) + j
                    cps.append(pltpu.make_async_remote_copy(
                        src_ref=blk, dst_ref=blk, send_sem=send_sems.at[k], recv_sem=recv_sems.at[k],
                        device_id=(x, y, 1 - c), device_id_type=MESH))
                continue
            if self.local_copy:
                src = ins[a].at[me] if self.src_slotted else ins[a]
                cps.append(pltpu.make_async_copy(src, outs[a].at[me], loc_sems.at[a]))
            for mi, (mx, my, mc) in enumerate(self.masks):
                px = 1 - x if mx else x
                py = 1 - y if my else y
                pc = 1 - c if mc else c
                src = ins[a].at[self._slot(px, py, pc)] if self.src_slotted else ins[a]
                dst = outs[a].at[me] if self.dst_slotted else outs[a]
                if self.half:
                    rows = src.shape[-2] // 2
                    mine = pl.ds(pl.multiple_of(c * rows, 8), rows)
                    src, dst = src.at[mine], dst.at[mine]
                k = a * n_m + mi
                cps.append(pltpu.make_async_remote_copy(
                    src_ref=src, dst_ref=dst, send_sem=send_sems.at[k], recv_sem=recv_sems.at[k],
                    device_id=(px, py, pc), device_id_type=MESH))
        return cps


def _exchange(name, plan, arrays):
    n = plan.n

    def body(*refs):
        cps = plan.copies(refs[:n], refs[n:2 * n], refs[2 * n:])
        for cp in cps:
            cp.start()
        for cp in cps:
            cp.wait()

    outs = pl.pallas_call(
        body, name=name, out_shape=plan.out_shape(),
        in_specs=[ANY_SPEC] * n, out_specs=tuple([ANY_SPEC] * n), scratch_shapes=plan.scratch(),
        input_output_aliases={i: i for i in range(n)} if plan.forward else {},
    )(*arrays)
    return list(outs)


def _pcall(body, *, name, grid, in_specs, out_specs, out_shape, scratch_shapes, args, plan=None, plan_args=()):
    sem = ("arbitrary",) * len(grid)
    if plan is None:
        return pl.pallas_call(body, name=name, grid=grid, in_specs=in_specs, out_specs=out_specs,
                              out_shape=out_shape, scratch_shapes=scratch_shapes,
                              compiler_params=_params(sem))(*args), []
    n_in, n_out, n_scr, n_p = len(in_specs), len(out_specs), len(scratch_shapes), plan.n

    def wrapped(*refs):
        ins = refs[:n_in]
        p_ins = refs[n_in:n_in + n_p]
        o0 = n_in + n_p
        outs = refs[o0:o0 + n_out]
        p_outs = refs[o0 + n_out:o0 + n_out + n_p]
        s0 = o0 + n_out + n_p
        scr = refs[s0:s0 + n_scr]
        sems = refs[s0 + n_scr:]
        ids = [pl.program_id(i) for i in range(len(grid))]
        first = functools.reduce(jnp.logical_and, [i == 0 for i in ids])
        last = functools.reduce(jnp.logical_and, [i == g - 1 for i, g in zip(ids, grid)])

        @pl.when(first)
        def _():
            for cp in plan.copies(p_ins, p_outs, sems):
                cp.start()

        body(*ins, *outs, *scr)

        @pl.when(last)
        def _():
            for cp in plan.copies(p_ins, p_outs, sems):
                cp.wait()

    res = pl.pallas_call(
        wrapped, name=name, grid=grid,
        in_specs=list(in_specs) + [ANY_SPEC] * n_p,
        out_specs=tuple(out_specs) + (ANY_SPEC,) * n_p,
        out_shape=tuple(out_shape) + plan.out_shape(),
        scratch_shapes=list(scratch_shapes) + plan.scratch(),
        compiler_params=_params(sem),
    )(*args, *plan_args)
    return res[:n_out], list(res[n_out:])


def _allgather_chips_plan(arrays):
    return _Plan(arrays, CHIP_MASKS, 4, False, True, True, half=True)


def _gather_two_level(name, arrays):
    n = len(arrays)
    ici = _allgather_chips_plan(arrays)
    fwd = _Plan(ici.out_shape(), SIB_MASKS, 4, True, True, False, forward=True)
    n_m = len(CHIP_MASKS)

    def body(*refs):
        ins, outs, sems = refs[:n], refs[n:2 * n], refs[2 * n:]
        ici_cps = ici.copies(ins, outs, sems[:3])
        fwd_cps = fwd.copies(None, outs, sems[3:])
        for cp in ici_cps:
            cp.start()
        for a in range(n):
            for m in range(n_m):
                ici_cps[a * (n_m + 1) + 1 + m].wait_recv()
                fwd_cps[a * n_m + m].start()
        for a in range(n):
            ici_cps[a * (n_m + 1)].wait()
            for m in range(n_m):
                ici_cps[a * (n_m + 1) + 1 + m].wait_send()
        for cp in fwd_cps:
            cp.wait()

    return list(pl.pallas_call(
        body, name=name, out_shape=ici.out_shape(),
        in_specs=[ANY_SPEC] * n, out_specs=tuple([ANY_SPEC] * n), scratch_shapes=ici.scratch() + fwd.scratch(),
    )(*arrays))


def _forward_sibling(name, gathered):
    return _exchange(name, _Plan(gathered, SIB_MASKS, 4, True, True, False, forward=True), gathered)


def _alltoall_chips_plan(arrays):
    return _Plan(arrays, CHIP_MASKS, 4, True, True, True)


def _swap_sibling(name, arrays):
    return _exchange(name, _Plan(arrays, SIB_MASKS, 2, False, False, False), arrays)


def _allgather_all_plan(arrays):
    return _Plan(arrays, ALL_MASKS, 8, False, True, True)


def _sum_slots(name, arrs, out_dtype):
    s = arrs[0].shape[0]
    n = len(arrs)
    steps = arrs[0].shape[1] // _tile(arrs[0].shape[1], 512, 8) if n == 1 else MULTI_SUM_STEPS
    for a in arrs:
        assert a.shape[1] % (16 * steps) == 0 or n == 1, a.shape

    def body(*refs):
        for a_ref, o_ref in zip(refs[:n], refs[n:]):
            acc = a_ref[0].astype(F32)
            for i in range(1, s):
                acc = acc + a_ref[i].astype(F32)
            o_ref[...] = acc.astype(out_dtype)

    return list(pl.pallas_call(
        body, name=name, grid=(steps,),
        in_specs=[pl.BlockSpec((s, a.shape[1] // steps, a.shape[2]), lambda i: (0, i, 0)) for a in arrs],
        out_specs=tuple(pl.BlockSpec((a.shape[1] // steps, a.shape[2]), lambda i: (i, 0)) for a in arrs),
        out_shape=tuple(jax.ShapeDtypeStruct(a.shape[1:], out_dtype) for a in arrs),
        compiler_params=_params(("arbitrary",)),
    )(*arrs))


def _adam_math(w, g, m, v):
    m_new = ADAM_B1 * m + (1.0 - ADAM_B1) * g
    v_new = ADAM_B2 * v + (1.0 - ADAM_B2) * (g * g)
    m_hat = m_new / (1.0 - ADAM_B1 ** ADAM_STEP)
    v_hat = v_new / (1.0 - ADAM_B2 ** ADAM_STEP)
    delta = -ADAM_LR * (m_hat / (jnp.sqrt(v_hat) + ADAM_EPS) + ADAM_WD * w)
    return delta, m_new, v_new


def _adam(name, wmv, g_parts):
    n_w = len(wmv)
    n_g = len(g_parts[0])
    lead = wmv[0][0].ndim == 3
    at = (lambda ref: ref.at[0]) if lead else (lambda ref: ref)
    n_in = 3 + n_g
    rows0 = wmv[0][0].shape[-2]
    steps = rows0 // _tile(rows0, 256, 8) if n_w == 1 else MULTI_ADAM_STEPS
    for w, _, _ in wmv:
        assert w.shape[-2] % (8 * steps) == 0, w.shape

    def body(*refs):
        for j in range(n_w):
            ins = refs[j * n_in:(j + 1) * n_in]
            outs = refs[n_w * n_in + 4 * j:n_w * n_in + 4 * j + 4]
            w_ref, m_ref, v_ref = [at(t) for t in ins[:3]]
            g_out, d_out, m_out, v_out = [at(t) for t in outs]
            g = ins[3][...].astype(F32)
            for gr in ins[4:]:
                g = g + gr[...].astype(F32)
            delta, m_new, v_new = _adam_math(w_ref[...], g, m_ref[...], v_ref[...])
            g_out[...] = g
            d_out[...] = delta
            m_out[...] = m_new
            v_out[...] = v_new

    in_specs, out_specs, out_shape, args = [], [], [], []
    for (w, m, v), gp in zip(wmv, g_parts):
        r, c = w.shape[-2:]
        spec = pl.BlockSpec((r // steps, c), lambda i: (i, 0))
        wspec = pl.BlockSpec((1, r // steps, c), lambda i: (0, i, 0)) if lead else spec
        in_specs += [wspec] * 3 + [spec] * n_g
        out_specs += [wspec] * 4
        out_shape += [jax.ShapeDtypeStruct(w.shape, F32)] * 4
        args += [w, m, v, *gp]
    res = pl.pallas_call(
        body, name=name, grid=(steps,),
        in_specs=in_specs, out_specs=tuple(out_specs), out_shape=tuple(out_shape),
        compiler_params=_params(("arbitrary",)),
    )(*args)
    return [tuple(res[4 * j:4 * j + 4]) for j in range(n_w)]


SUB_ROWS = 32
FFN_BWD_ROWS = 416
FFN_FWD_ROWS = 832
FFN_LOSS_ROWS = 640
RET_ROWS = 640
S5_STEPS = 104


def _tile_parts(tm, d, head, x):
    nsub = tm // SUB_ROWS
    off = head.shape[0] // SUB_ROWS
    specs = [pl.BlockSpec(head.shape, lambda i, k: (0, 0))] + [
        pl.BlockSpec((SUB_ROWS, d), lambda i, k, j=j: (jnp.maximum(i * nsub + j - off, 0), 0)) for j in range(nsub)]

    def assemble(i, part_refs, h_sc):
        head_ref, x_refs = part_refs[0], part_refs[1:]
        for j in range(nsub):
            rows = slice(j * SUB_ROWS, (j + 1) * SUB_ROWS)
            val = x_refs[j][...]
            if j < off:
                val = jnp.where(i == 0, head_ref[rows, :], val)
            h_sc[rows, :] = val

    return specs, [head] + [x] * nsub, assemble


def _h_source(body, h, tm, d):
    if not isinstance(h, tuple):
        return body, [pl.BlockSpec((tm, d), lambda i, k: (i, 0))], [h], []
    specs, args, assemble = _tile_parts(tm, d, *h)
    n_h = len(specs)

    def with_parts(*refs):
        h_sc = refs[-1]

        @pl.when(pl.program_id(1) == 0)
        def _():
            assemble(pl.program_id(0), refs[:n_h], h_sc)

        body(h_sc, *refs[n_h:-1])

    return with_parts, specs, args, [pltpu.VMEM((tm, d), F32)]


def _ffn_fwd(name, h, nw, wg, wu, wd, plan=None, plan_args=(), loss=None):
    lp, d = (h[0].shape[0] + h[1].shape[0], h[1].shape[1]) if isinstance(h, tuple) else h.shape
    nck, f, _ = wg.shape
    tm = _tile(lp, FFN_FWD_ROWS if loss is None else FFN_LOSS_ROWS)
    last = nck // FFN_CPS - 1
    n_t = 0
    if loss is not None:
        t_specs, t_args, t_assemble = _tile_parts(tm, d, jnp.zeros((lp - loss[1].shape[0], d), F32), loss[1])
        n_t = len(t_specs)

    def body(h_ref, nw_ref, wg_ref, wu_ref, wd_ref, *rest):
        if loss is not None:
            fw_ref, t_parts, rest = rest[0], rest[1:1 + n_t], rest[1 + n_t:]
            ho_ref, g_ref, u_ref, loss_ref, dfw_ref, n_sc, acc_sc, t_sc = rest
        else:
            ho_ref, g_ref, u_ref, n_sc, acc_sc = rest
        i = pl.program_id(0)
        k = pl.program_id(1)

        @pl.when(k == 0)
        def _():
            xh, _ = _rms_stats(h_ref[...])
            n_sc[...] = (xh * nw_ref[...]).astype(BF16)
            acc_sc[...] = jnp.zeros_like(acc_sc)

        n = n_sc[...]
        acc = acc_sc[...]
        for c in range(FFN_CPS):
            g = _dot_nt(n, wg_ref[c])
            u = _dot_nt(n, wu_ref[c])
            g_ref[c] = g.astype(BF16)
            u_ref[c] = u.astype(BF16)
            a = (g * _sigmoid(g) * u).astype(BF16)
            acc = acc + _dot(a, wd_ref[c])
        acc_sc[...] = acc

        if loss is None:
            @pl.when(k == last)
            def _():
                ho_ref[...] = h_ref[...] + FFN_RES * acc_sc[...]
            return

        @pl.when(jnp.logical_and(i == 0, k == 0))
        def _():
            loss_ref[...] = jnp.zeros_like(loss_ref)
            dfw_ref[...] = jnp.zeros_like(dfw_ref)

        @pl.when(k == last)
        def _():
            t_assemble(i, t_parts, t_sc)
            xh, r = _rms_stats(h_ref[...] + FFN_RES * acc_sc[...])
            w = fw_ref[...]
            head_rows = lp - loss[1].shape[0]
            row = lax.broadcasted_iota(jnp.int32, (tm, d), 0) + i * tm
            err = jnp.where(row < head_rows, 0.0, xh * w - t_sc[...])
            loss_ref[...] += 0.5 * jnp.sum(err * err) / d
            dout = err * (1.0 / d)
            dfw_ref[...] += jnp.sum(dout * xh, axis=0, keepdims=True)
            ho_ref[...] = _rms_bwd(dout, xh, r, w)

    body, h_specs, h_args, h_scratch = _h_source(body, h, tm, d)
    vec = pl.BlockSpec((1, d), lambda i, k: (0, 0))
    w_fd = pl.BlockSpec((FFN_CPS, f, d), lambda i, k: (k, 0, 0))
    hid = pl.BlockSpec((FFN_CPS, tm, f), lambda i, k: (k, i, 0))
    hshape = jax.ShapeDtypeStruct((nck, lp, f), BF16)
    args, in_specs = (*h_args, nw, wg, wu, wd), h_specs + [vec, w_fd, w_fd, w_fd]
    out_specs = (pl.BlockSpec((tm, d), lambda i, k: (i, 0)), hid, hid)
    out_shape = (jax.ShapeDtypeStruct((lp, d), F32), hshape, hshape)
    scratch = [pltpu.VMEM((tm, d), BF16), pltpu.VMEM((tm, d), F32)]
    if loss is not None:
        args, in_specs = (*args, loss[0], *t_args), in_specs + [vec] + t_specs
        out_specs += (pl.BlockSpec((8, LANE), lambda i, k: (0, 0)), vec)
        out_shape += (jax.ShapeDtypeStruct((8, LANE), F32), jax.ShapeDtypeStruct((1, d), F32))
        scratch = scratch + [pltpu.VMEM((tm, d), F32)]
    return _pcall(
        body, name=name, grid=(lp // tm, nck // FFN_CPS), plan=plan, plan_args=plan_args,
        args=args, in_specs=in_specs, out_specs=out_specs, out_shape=out_shape,
        scratch_shapes=scratch + h_scratch)


def _ffn_bwd_act(name, dh, h, nw, g, u, wg, wu, wd, plan=None, plan_args=()):
    lp, d = dh.shape
    nck, f, _ = wg.shape
    tm = _tile(lp, FFN_BWD_ROWS, SUB_ROWS)
    last = nck // FFN_CPS - 1

    def body(h_ref, dh_ref, nw_ref, g_ref, u_ref, wg_ref, wu_ref, wd_ref,
             dhi_ref, dnw_ref, n_ref, dacc_ref, a_ref, dg_ref, du_ref,
             xh_sc, r_sc, dn_sc):
        i = pl.program_id(0)
        k = pl.program_id(1)

        @pl.when(k == 0)
        def _():
            xh, r = _rms_stats(h_ref[...])
            xh_sc[...] = xh
            r_sc[...] = r
            n_ref[...] = (xh * nw_ref[...]).astype(BF16)
            dacc_ref[...] = (FFN_RES * dh_ref[...]).astype(BF16)
            dn_sc[...] = jnp.zeros_like(dn_sc)

        @pl.when(jnp.logical_and(i == 0, k == 0))
        def _():
            dnw_ref[...] = jnp.zeros_like(dnw_ref)

        dacc = dacc_ref[...]
        dn = dn_sc[...]
        for c in range(FFN_CPS):
            gv = g_ref[c].astype(F32)
            uv = u_ref[c].astype(F32)
            sg = _sigmoid(gv)
            sil = gv * sg
            da = _dot_nt(dacc, wd_ref[c])
            dgk = (da * uv * (sg * (1.0 + gv * (1.0 - sg)))).astype(BF16)
            duk = (da * sil).astype(BF16)
            a_ref[c] = (sil * uv).astype(BF16)
            dg_ref[c] = dgk
            du_ref[c] = duk
            dn = dn + _dot(dgk, wg_ref[c]) + _dot(duk, wu_ref[c])
        dn_sc[...] = dn

        @pl.when(k == last)
        def _():
            dnl = dn_sc[...]
            xh = xh_sc[...]
            dhi_ref[...] = dh_ref[...] + _rms_bwd(dnl, xh, r_sc[...], nw_ref[...])
            dnw_ref[...] += jnp.sum(dnl * xh, axis=0, keepdims=True)

    body, h_specs, h_args, h_scratch = _h_source(body, h, tm, d)
    row = pl.BlockSpec((tm, d), lambda i, k: (i, 0))
    vec = pl.BlockSpec((1, d), lambda i, k: (0, 0))
    hid = pl.BlockSpec((FFN_CPS, tm, f), lambda i, k: (k, i, 0))
    w_fd = pl.BlockSpec((FFN_CPS, f, d), lambda i, k: (k, 0, 0))
    rshape = jax.ShapeDtypeStruct((lp, d), BF16)
    hshape = jax.ShapeDtypeStruct((nck, lp, f), BF16)
    return _pcall(
        body, name=name, grid=(lp // tm, nck // FFN_CPS), plan=plan, plan_args=plan_args,
        args=(*h_args, dh, nw, g, u, wg, wu, wd),
        in_specs=h_specs + [row, vec, hid, hid, w_fd, w_fd, w_fd],
        out_specs=(row, vec, row, row, hid, hid, hid),
        out_shape=(jax.ShapeDtypeStruct((lp, d), F32), jax.ShapeDtypeStruct((1, d), F32),
                   rshape, rshape, hshape, hshape, hshape),
        scratch_shapes=[pltpu.VMEM((tm, d), F32), pltpu.VMEM((tm, 1), F32), pltpu.VMEM((tm, d), F32)] + h_scratch)


def _ffn_bwd_w(name, n, dacc, a, dg, du, plan=None, plan_args=()):
    lp, d = n.shape
    nck, _, f = a.shape
    tm = _tile(lp, BWD_W_ROWS)
    last = lp // tm - 1

    def body(n_ref, dacc_ref, a_ref, dg_ref, du_ref, dwg_ref, dwu_ref, dwd_ref, ag_sc, au_sc, ad_sc):
        i = pl.program_id(1)

        @pl.when(i == 0)
        def _():
            ag_sc[...] = jnp.zeros_like(ag_sc)
            au_sc[...] = jnp.zeros_like(au_sc)
            ad_sc[...] = jnp.zeros_like(ad_sc)

        nv = n_ref[...]
        ag_sc[...] += _dot_tn(dg_ref[0], nv)
        au_sc[...] += _dot_tn(du_ref[0], nv)
        ad_sc[...] += _dot_tn(a_ref[0], dacc_ref[...])

        @pl.when(i == last)
        def _():
            dwg_ref[0] = ag_sc[...].astype(BF16)
            dwu_ref[0] = au_sc[...].astype(BF16)
            dwd_ref[0] = ad_sc[...].astype(BF16)

    row = pl.BlockSpec((tm, d), lambda k, i: (i, 0))
    hid = pl.BlockSpec((1, tm, f), lambda k, i: (k, i, 0))
    w_fd = pl.BlockSpec((1, f, d), lambda k, i: (k, 0, 0))
    wshape = jax.ShapeDtypeStruct((nck, f, d), BF16)
    return _pcall(
        body, name=name, grid=(nck, lp // tm), plan=plan, plan_args=plan_args, args=(n, dacc, a, dg, du),
        in_specs=[row, row, hid, hid, hid], out_specs=(w_fd, w_fd, w_fd), out_shape=(wshape,) * 3,
        scratch_shapes=[pltpu.VMEM((f, d), F32)] * 3)


def _ffn_bwd_w_scatter(name, n, dacc, a, dg, du, chip, plan, plan_args):
    lp, d = n.shape
    nck, _, f = a.shape
    tm = _tile(lp, BWD_W_ROWS)
    last_i = lp // tm - 1
    n_w = 3
    n_p = plan.n

    def body(me_ref, n_ref, dacc_ref, a_ref, dg_ref, du_ref, *rest):
        p_ins = rest[:n_p]
        recv = rest[n_p:n_p + n_w]
        p_outs = rest[n_p + n_w:2 * n_p + n_w]
        acc = rest[2 * n_p + n_w:2 * n_p + 2 * n_w]
        stage, send_sems, recv_sems, loc_sems = rest[2 * n_p + 2 * n_w:2 * n_p + 2 * n_w + 4]
        p_sems = rest[2 * n_p + 2 * n_w + 4:]
        p = pl.program_id(0)
        i = pl.program_id(1)
        me = me_ref[0]
        c = lax.axis_index("c")

        def send(w, pos):
            kk = jnp.bitwise_xor(me, nck - 1 - pos)
            diff = jnp.bitwise_xor(kk, me)
            m = jnp.where(diff == 2, 0, jnp.where(diff == 1, 1, 2))
            return pltpu.make_async_remote_copy(
                src_ref=stage.at[lax.rem(pos, 2), w], dst_ref=recv[w].at[me],
                send_sem=send_sems.at[w * 3 + m], recv_sem=recv_sems.at[w * 3 + m],
                device_id=(lax.div(kk, 2), lax.rem(kk, 2), c), device_id_type=MESH)

        @pl.when(jnp.logical_and(p == 0, i == 0))
        def _():
            for cp in plan.copies(p_ins, p_outs, p_sems):
                cp.start()

        @pl.when(i == 0)
        def _():
            for t in acc:
                t[...] = jnp.zeros_like(t)

        nv = n_ref[...]
        acc[0][...] += _dot_tn(dg_ref[0], nv)
        acc[1][...] += _dot_tn(du_ref[0], nv)
        acc[2][...] += _dot_tn(a_ref[0], dacc_ref[...])

        @pl.when(jnp.logical_and(i == last_i, p >= 2))
        def _():
            for w in range(n_w):
                send(w, p - 2).wait_send()

        @pl.when(i == last_i)
        def _():
            for w in range(n_w):
                stage[lax.rem(p, 2), w] = acc[w][...].astype(BF16)

        @pl.when(jnp.logical_and(i == last_i, p < nck - 1))
        def _():
            for w in range(n_w):
                send(w, p).start()

        @pl.when(jnp.logical_and(i == last_i, p == nck - 1))
        def _():
            own = [pltpu.make_async_copy(stage.at[(nck - 1) % 2, w], recv[w].at[me], loc_sems.at[w])
                   for w in range(n_w)]
            for cp in own:
                cp.start()
            for w in range(n_w):
                send(w, nck - 2).wait_send()
            for cp in own:
                cp.wait()
            for w in range(n_w):
                for m in range(3):
                    pltpu.make_async_remote_copy(
                        src_ref=stage.at[0, w], dst_ref=recv[w].at[me],
                        send_sem=send_sems.at[w * 3 + m], recv_sem=recv_sems.at[w * 3 + m],
                        device_id=(0, 0, c), device_id_type=MESH).wait_recv()
            for cp in plan.copies(p_ins, p_outs, p_sems):
                cp.wait()

    chunk = lambda k, me_ref: jnp.bitwise_xor(me_ref[0], nck - 1 - k)
    row = pl.BlockSpec((tm, d), lambda k, i, me_ref: (i, 0))
    hid = pl.BlockSpec((1, tm, f), lambda k, i, me_ref: (chunk(k, me_ref), i, 0))
    wshape = jax.ShapeDtypeStruct((nck, f, d), BF16)
    res = pl.pallas_call(
        body, name=name,
        grid_spec=pltpu.PrefetchScalarGridSpec(
            num_scalar_prefetch=1, grid=(nck, lp // tm),
            in_specs=[row, row, hid, hid, hid] + [ANY_SPEC] * n_p,
            out_specs=(ANY_SPEC,) * (n_w + n_p),
            scratch_shapes=[pltpu.VMEM((f, d), F32)] * n_w + [
                pltpu.VMEM((2, n_w, f, d), BF16), pltpu.SemaphoreType.DMA((n_w * 3,)),
                pltpu.SemaphoreType.DMA((n_w * 3,)), pltpu.SemaphoreType.DMA((n_w,))] + plan.scratch()),
        out_shape=(wshape,) * n_w + plan.out_shape(),
        compiler_params=_params(("arbitrary", "arbitrary")),
    )(chip.reshape(1).astype(jnp.int32), n, dacc, a, dg, du, *plan_args)
    return list(res[:n_w]), list(res[n_w:])


def _inproj_fwd(h, nw, w_in, cosf, sinf, rw):
    lp, d = h.shape
    nck, _, ps = w_in.shape
    proj = nck * ps
    sw = proj - 4 * rw
    tm = _tile(lp, 640)
    scale = HEAD_DIM ** -0.5
    heads = rw // HEAD_DIM

    def body(h_ref, nw_ref, w_ref, cos_ref, sin_ref, n_ref, q_ref, k_ref, v_ref, g_ref, u_ref, p_sc):
        xh, _ = _rms_stats(h_ref[...])
        n = (xh * nw_ref[...]).astype(BF16)
        n_ref[...] = n
        for c in range(nck):
            p_sc[:, c * ps:(c + 1) * ps] = _dot(n, w_ref[c])
        cs = cos_ref[...]
        sn = sin_ref[...]
        for hh in range(heads):
            lo = hh * HEAD_DIM
            qh = p_sc[:, lo:lo + HEAD_DIM]
            q_ref[:, lo:lo + HEAD_DIM] = (qh * cs + pltpu.roll(qh, HEAD_DIM // 2, 1) * sn).astype(BF16)
            kh = p_sc[:, rw + lo:rw + lo + HEAD_DIM]
            k_ref[:, lo:lo + HEAD_DIM] = ((kh * cs + pltpu.roll(kh, HEAD_DIM // 2, 1) * sn) * scale).astype(BF16)
        v_ref[...] = p_sc[:, 2 * rw:3 * rw].astype(BF16)
        g_ref[...] = p_sc[:, 3 * rw:4 * rw]
        u_ref[...] = p_sc[:, 4 * rw:]

    row = lambda w: pl.BlockSpec((tm, w), lambda i: (i, 0))
    return pl.pallas_call(
        body, name="inproj_fwd", grid=(lp // tm,),
        in_specs=[row(d), pl.BlockSpec((1, d), lambda i: (0, 0)),
                  pl.BlockSpec((nck, d, ps), lambda i: (0, 0, 0)), row(HEAD_DIM), row(HEAD_DIM)],
        out_specs=(row(d), row(rw), row(rw), row(rw), row(rw), row(sw)),
        out_shape=(jax.ShapeDtypeStruct((lp, d), BF16),
                   jax.ShapeDtypeStruct((lp, rw), BF16),
                   jax.ShapeDtypeStruct((lp, rw), BF16),
                   jax.ShapeDtypeStruct((lp, rw), BF16),
                   jax.ShapeDtypeStruct((lp, rw), F32),
                   jax.ShapeDtypeStruct((lp, sw), F32)),
        scratch_shapes=[pltpu.VMEM((tm, proj), F32)],
        compiler_params=_params(("arbitrary",)),
    )(h, nw, w_in, cosf, sinf)


def _inproj_bwd(dh, h, nw, n, w_in, dq, dk, dv, dg, du):
    lp, d = h.shape
    nck, _, ps = w_in.shape
    rw = dq.shape[1]
    sw = du.shape[1]
    proj = nck * ps
    tm = _tile(lp, 640)
    last = lp // tm - 1

    def gather_dproj(p_sc, dq_ref, dk_ref, dv_ref, dg_ref, du_ref):
        p_sc[:, 0:rw] = dq_ref[...]
        p_sc[:, rw:2 * rw] = dk_ref[...]
        p_sc[:, 2 * rw:3 * rw] = dv_ref[...]
        p_sc[:, 3 * rw:4 * rw] = dg_ref[...]
        p_sc[:, 4 * rw:] = du_ref[...]

    def act_body(dh_ref, h_ref, nw_ref, w_ref, dq_ref, dk_ref, dv_ref, dg_ref, du_ref, dhi_ref, dnw_ref, p_sc):
        i = pl.program_id(0)

        @pl.when(i == 0)
        def _():
            dnw_ref[...] = jnp.zeros_like(dnw_ref)

        gather_dproj(p_sc, dq_ref, dk_ref, dv_ref, dg_ref, du_ref)
        dn = jnp.zeros((tm, d), F32)
        for c in range(nck):
            dn = dn + _dot_nt(p_sc[:, c * ps:(c + 1) * ps], w_ref[c])
        xh, r = _rms_stats(h_ref[...])
        dhi_ref[...] = dh_ref[...] + _rms_bwd(dn, xh, r, nw_ref[...])
        dnw_ref[...] += jnp.sum(dn * xh, axis=0, keepdims=True)

    def w_body(n_ref, dq_ref, dk_ref, dv_ref, dg_ref, du_ref, dw_ref, p_sc, acc_sc):
        i = pl.program_id(0)

        @pl.when(i == 0)
        def _():
            acc_sc[...] = jnp.zeros_like(acc_sc)

        gather_dproj(p_sc, dq_ref, dk_ref, dv_ref, dg_ref, du_ref)
        nv = n_ref[...]
        for c in range(nck):
            acc_sc[c] += _dot_tn(nv, p_sc[:, c * ps:(c + 1) * ps])

        @pl.when(i == last)
        def _():
            dw_ref[...] = acc_sc[...].astype(BF16)

    row = lambda w: pl.BlockSpec((tm, w), lambda i: (i, 0))
    vec = pl.BlockSpec((1, d), lambda i: (0, 0))
    wsp = pl.BlockSpec((nck, d, ps), lambda i: (0, 0, 0))
    dproj_specs = [row(rw), row(rw), row(rw), row(rw), row(sw)]
    dhi, dnw = pl.pallas_call(
        act_body, name="inproj_bwd_act", grid=(lp // tm,),
        in_specs=[row(d), row(d), vec, wsp] + dproj_specs,
        out_specs=(row(d), vec),
        out_shape=(jax.ShapeDtypeStruct((lp, d), F32), jax.ShapeDtypeStruct((1, d), F32)),
        scratch_shapes=[pltpu.VMEM((tm, proj), BF16)],
        compiler_params=_params(("arbitrary",)),
    )(dh, h, nw, w_in, dq, dk, dv, dg, du)
    dw = pl.pallas_call(
        w_body, name="inproj_bwd_w", grid=(lp // tm,),
        in_specs=[row(d)] + dproj_specs,
        out_specs=wsp, out_shape=jax.ShapeDtypeStruct((nck, d, ps), BF16),
        scratch_shapes=[pltpu.VMEM((tm, proj), BF16), pltpu.VMEM((nck, d, ps), F32)],
        compiler_params=_params(("arbitrary",)),
    )(n, dq, dk, dv, dg, du)
    return dhi, dnw, dw


def _retention_tables(rc):
    h = jnp.arange(RET_HEADS, dtype=F32)
    log_g = jnp.log(1.0 - 2.0 ** (-5.0 - h))
    i = jnp.arange(rc)
    diff = i[:, None] - i[None, :]
    dec = jnp.where(diff[None] >= 0,
                    jnp.exp(log_g[:, None, None] * jnp.maximum(diff, 0)[None].astype(F32)), 0.0)
    pos = jnp.arange(rc, dtype=F32)
    wq = jnp.exp(log_g[:, None] * (pos + 1.0)[None])
    wk = jnp.exp(log_g[:, None] * (rc - 1 - pos)[None])
    gch = jnp.exp(log_g * rc)
    ones = jnp.ones((1, 1, HEAD_DIM), F32)
    return (dec, wq[:, :, None] * ones, wk[:, :, None] * ones,
            gch[:, None, None] * jnp.ones((1, 8, HEAD_DIM), F32))


def _head_norm(o):
    mu = jnp.mean(o, axis=-1, keepdims=True)
    oc = o - mu
    r = lax.rsqrt(jnp.mean(oc * oc, axis=-1, keepdims=True) + EPS)
    return oc * r, r


def _ret_fwd(q, k, v, g, rnw, tables):
    lp, rw = q.shape
    heads = rw // HEAD_DIM
    rc = tables[0].shape[1]
    nch = lp // rc
    dec, wq, wk, gch = tables

    def body(q_ref, k_ref, v_ref, g_ref, w_ref, dec_ref, wq_ref, wk_ref, gch_ref,
             o_ref, ret_ref, sp_ref, s_sc):
        n = pl.program_id(0)

        @pl.when(n == 0)
        def _():
            s_sc[...] = jnp.zeros_like(s_sc)

        cols = [slice(hh * HEAD_DIM, (hh + 1) * HEAD_DIM) for hh in range(heads)]
        s_ins = [s_sc[hh] for hh in range(heads)]
        outs = []
        for hh, cs in enumerate(cols):
            qv, kv, vv = q_ref[:, cs], k_ref[:, cs], v_ref[:, cs]
            s_in = s_ins[hh]
            a = _dot_nt(qv, kv) * dec_ref[hh]
            qw = (qv.astype(F32) * wq_ref[hh]).astype(BF16)
            kw = (kv.astype(F32) * wk_ref[hh]).astype(BF16)
            o = _dot(a.astype(BF16), vv) + _dot(qw, s_in.astype(BF16))
            s_new = gch_ref[hh, 0:1, :] * s_in + _dot_tn(kw, vv)
            xh, _ = _head_norm(o)
            gv = g_ref[:, cs]
            outs.append((o, s_new, (gv * _sigmoid(gv) * (xh * w_ref[:, cs])).astype(BF16)))
        for hh, cs in enumerate(cols):
            o, s_new, ret = outs[hh]
            sp_ref[hh, 0] = s_ins[hh]
            s_sc[hh] = s_new
            o_ref[:, cs] = o
            ret_ref[:, cs] = ret

    blk = pl.BlockSpec((rc, rw), lambda n: (n, 0))
    tab = pl.BlockSpec((heads, rc, HEAD_DIM), lambda n: (0, 0, 0))
    dtab = pl.BlockSpec((heads, rc, rc), lambda n: (0, 0, 0))
    return pl.pallas_call(
        body, name="retention_fwd", grid=(nch,),
        in_specs=[blk, blk, blk, blk, pl.BlockSpec((1, rw), lambda n: (0, 0)),
                  dtab, tab, tab, pl.BlockSpec((heads, 8, HEAD_DIM), lambda n: (0, 0, 0))],
        out_specs=(blk, blk, pl.BlockSpec((heads, 1, HEAD_DIM, HEAD_DIM), lambda n: (0, n, 0, 0))),
        out_shape=(jax.ShapeDtypeStruct((lp, rw), F32),
                   jax.ShapeDtypeStruct((lp, rw), BF16),
                   jax.ShapeDtypeStruct((heads, nch, HEAD_DIM, HEAD_DIM), F32)),
        scratch_shapes=[pltpu.VMEM((heads, HEAD_DIM, HEAD_DIM), F32)],
        compiler_params=_params(("arbitrary",)),
    )(q, k, v, g, rnw, dec, wq, wk, gch)


def _ret_bwd(dret, q, k, v, g, o, sprev, rnw, tables, cosf, sinf):
    lp, rw = q.shape
    heads = rw // HEAD_DIM
    rc = tables[0].shape[1]
    nch = lp // rc
    dec, wq, wk, gch = tables
    scale = HEAD_DIM ** -0.5
    half = HEAD_DIM // 2

    def body(dret_ref, q_ref, k_ref, v_ref, g_ref, o_ref, sp_ref, w_ref, dec_ref, wq_ref, wk_ref, gch_ref,
             cos_ref, sin_ref, dq_ref, dk_ref, dv_ref, dg_ref, dw_ref, ds_sc):
        n = pl.program_id(0)

        @pl.when(n == 0)
        def _():
            ds_sc[...] = jnp.zeros_like(ds_sc)
            dw_ref[...] = jnp.zeros_like(dw_ref)

        cosv = cos_ref[...]
        sinv = sin_ref[...]
        cols = [slice(hh * HEAD_DIM, (hh + 1) * HEAD_DIM) for hh in range(heads)]
        ds_ins = [ds_sc[hh] for hh in range(heads)]
        dw_ins = [dw_ref[:, cs] for cs in cols]
        outs = []
        for hh, cs in enumerate(cols):
            qv, kv, vv = q_ref[:, cs], k_ref[:, cs], v_ref[:, cs]
            gv = g_ref[:, cs]
            dr = dret_ref[:, cs]
            w = w_ref[:, cs]
            sg = _sigmoid(gv)
            sil = gv * sg
            xh, r = _head_norm(o_ref[:, cs])
            dgate = (dr * (xh * w) * (sg * (1.0 + gv * (1.0 - sg)))).astype(BF16)
            dyw = dr * sil
            dw_new = dw_ins[hh] + jnp.sum(dyw * xh, axis=0, keepdims=True)
            dxh = dyw * w
            do = r * (dxh - jnp.mean(dxh, axis=-1, keepdims=True)
                      - xh * jnp.mean(dxh * xh, axis=-1, keepdims=True))
            dob = do.astype(BF16)
            dmask = dec_ref[hh]
            wqv = wq_ref[hh]
            wkv = wk_ref[hh]
            a = (_dot_nt(qv, kv) * dmask).astype(BF16)
            da = (_dot_nt(dob, vv) * dmask).astype(BF16)
            qw = (qv.astype(F32) * wqv).astype(BF16)
            kw = (kv.astype(F32) * wkv).astype(BF16)
            s_in = sp_ref[hh, 0].astype(BF16)
            ds = ds_ins[hh]
            dsb = ds.astype(BF16)
            dq = _dot(da, kv) + _dot_nt(dob, s_in) * wqv
            dk = _dot_tn(da, qv) + _dot_nt(vv, dsb) * wkv
            dv = _dot_tn(a, dob) + _dot(kw, dsb)
            ds_new = gch_ref[hh, 0:1, :] * ds + _dot_tn(qw, dob)
            outs.append((dgate, dw_new, ds_new,
                         (dq * cosv + pltpu.roll(dq * sinv, half, 1)).astype(BF16),
                         ((dk * cosv + pltpu.roll(dk * sinv, half, 1)) * scale).astype(BF16),
                         dv.astype(BF16)))
        for hh, cs in enumerate(cols):
            dgate, dw_new, ds_new, dqv, dkv, dvv = outs[hh]
            dg_ref[:, cs] = dgate
            dw_ref[:, cs] = dw_new
            ds_sc[hh] = ds_new
            dq_ref[:, cs] = dqv
            dk_ref[:, cs] = dkv
            dv_ref[:, cs] = dvv

    blk = pl.BlockSpec((rc, rw), lambda n: (nch - 1 - n, 0))
    tab = pl.BlockSpec((heads, rc, HEAD_DIM), lambda n: (0, 0, 0))
    dtab = pl.BlockSpec((heads, rc, rc), lambda n: (0, 0, 0))
    wsp = pl.BlockSpec((1, rw), lambda n: (0, 0))
    pos = pl.BlockSpec((rc, HEAD_DIM), lambda n: (nch - 1 - n, 0))
    bshape = jax.ShapeDtypeStruct((lp, rw), BF16)
    return pl.pallas_call(
        body, name="retention_bwd", grid=(nch,),
        in_specs=[blk, blk, blk, blk, blk, blk,
                  pl.BlockSpec((heads, 1, HEAD_DIM, HEAD_DIM), lambda n: (0, nch - 1 - n, 0, 0)),
                  wsp, dtab, tab, tab, pl.BlockSpec((heads, 8, HEAD_DIM), lambda n: (0, 0, 0)), pos, pos],
        out_specs=(blk, blk, blk, blk, wsp),
        out_shape=(bshape, bshape, bshape, bshape, jax.ShapeDtypeStruct((1, rw), F32)),
        scratch_shapes=[pltpu.VMEM((heads, HEAD_DIM, HEAD_DIM), F32)],
        compiler_params=_params(("arbitrary",)),
    )(dret, q, k, v, g, o, sprev, rnw, dec, wq, wk, gch, cosf, sinf)


SCAN_CW = 512


def _s5_prepare(lam_re, lam_im, log_dt, b_re, b_im):
    dt = jnp.exp(log_dt)[:, None]
    er = jnp.exp(lam_re * dt)
    ar = er * jnp.cos(lam_im * dt)
    ai = er * jnp.sin(lam_im * dt)
    den = lam_re * lam_re + lam_im * lam_im
    fr = ((ar - 1.0) * lam_re + ai * lam_im) / den
    fi = (ai * lam_re - (ar - 1.0) * lam_im) / den
    bbr = fr[..., None] * b_re - fi[..., None] * b_im
    bbi = fr[..., None] * b_im + fi[..., None] * b_re
    return ar, ai, bbr, bbi


def _blockdiag_in(t):
    g, p, n = t.shape
    gs = g // N_SEC
    t = t.reshape(N_SEC, gs, p, n)
    eye = jnp.eye(gs, dtype=t.dtype)
    return jnp.einsum("sgpn,gh->sgphn", t, eye).reshape(N_SEC, gs * p, gs * n)


def _blockdiag_out(m, g, p, n):
    gs = g // N_SEC
    m = m.reshape(N_SEC, gs, p, gs, n)
    eye = jnp.eye(gs, dtype=m.dtype)
    return jnp.einsum("sgphn,gh->sgpn", m, eye).reshape(g, p, n)


def _scan_step(xr_ref, xi_ref, r0, prev, ar_ref, ai_ref, conj, ncols):
    new = []
    for cc in range(ncols // SCAN_CW):
        cs = pl.ds(cc * SCAN_CW, SCAN_CW)
        pr, pi = prev[cc]
        ar = ar_ref[:, cs]
        ai = ai_ref[:, cs]
        if conj:
            nr = ar * pr + ai * pi
            ni = ar * pi - ai * pr
        else:
            nr = ar * pr - ai * pi
            ni = ar * pi + ai * pr
        xr = xr_ref[pl.ds(r0, 8), cs] + nr
        xi = xi_ref[pl.ds(r0, 8), cs] + ni
        xr_ref[pl.ds(r0, 8), cs] = xr
        xi_ref[pl.ds(r0, 8), cs] = xi
        new.append((xr, xi))
    return new


def _scan_chunks(ncols):
    return [pl.ds(cc * SCAN_CW, SCAN_CW) for cc in range(ncols // SCAN_CW)]


def _flat(pairs):
    return tuple(t for p in pairs for t in p)


def _pairs(flat):
    return [(flat[2 * k], flat[2 * k + 1]) for k in range(len(flat) // 2)]


def _shift_rows(z, down):
    row = lax.broadcasted_iota(jnp.int32, z.shape, 0)
    if down:
        return jnp.where(row == 0, 0.0, pltpu.roll(z, 1, 0))
    return jnp.where(row == N_SEG - 1, 0.0, pltpu.roll(z, N_SEG - 1, 0))


def _s5_fwd(u, bsr, bsi, csr, csi, a8r, a8i, al8r, al8i, d, gluw, glub, nw, jb):
    lp, sw = u.shape
    ns = a8r.shape[1]
    rows = N_SEG * jb
    nblk = lp // rows
    secw = sw // N_SEC
    secn = ns // N_SEC

    def local_scan(u_ref, bsr_ref, bsi_ref, ar_ref, ai_ref, xr_ref, xi_ref, pr_sc, pi_sc):
        for s in range(N_SEC):
            ub = u_ref[:, s * secw:(s + 1) * secw].astype(BF16)
            xr_ref[:, s * secn:(s + 1) * secn] = _dot(ub, bsr_ref[s])
            xi_ref[:, s * secn:(s + 1) * secn] = _dot(ub, bsi_ref[s])
        prev = [(pr_sc[:, cs], pi_sc[:, cs]) for cs in _scan_chunks(ns)]
        prev = _scan_step(xr_ref, xi_ref, 0, prev, ar_ref, ai_ref, False, ns)

        def step(j, carry):
            r0 = pl.multiple_of(j * 8, 8)
            return _flat(_scan_step(xr_ref, xi_ref, r0, _pairs(carry), ar_ref, ai_ref, False, ns))

        last = _pairs(lax.fori_loop(1, jb, step, _flat(prev)))
        for cs, (vr, vi) in zip(_scan_chunks(ns), last):
            pr_sc[:, cs] = vr
            pi_sc[:, cs] = vi

    def carry_body(u_ref, bsr_ref, bsi_ref, ar_ref, ai_ref, alr_ref, ali_ref, cr_ref, ci_ref,
                   xr_sc, xi_sc, pr_sc, pi_sc):
        b = pl.program_id(0)

        @pl.when(b == 0)
        def _():
            pr_sc[...] = jnp.zeros_like(pr_sc)
            pi_sc[...] = jnp.zeros_like(pi_sc)

        local_scan(u_ref, bsr_ref, bsi_ref, ar_ref, ai_ref, xr_sc, xi_sc, pr_sc, pi_sc)

        @pl.when(b == nblk - 1)
        def _():
            er = _shift_rows(pr_sc[...], True)
            ei = _shift_rows(pi_sc[...], True)
            alr, ali = alr_ref[...], ali_ref[...]
            cr, ci = er, ei
            for _ in range(N_SEG - 2):
                sr = _shift_rows(cr, True)
                si = _shift_rows(ci, True)
                cr = er + alr * sr - ali * si
                ci = ei + alr * si + ali * sr
            cr_ref[...] = cr
            ci_ref[...] = ci

    ublk = pl.BlockSpec((rows, sw), lambda b: (b, 0))
    bspec = pl.BlockSpec((N_SEC, secw, secn), lambda b: (0, 0, 0))
    cspec = pl.BlockSpec((N_SEC, secn, secw), lambda b: (0, 0, 0))
    s8 = pl.BlockSpec((N_SEG, ns), lambda b: (0, 0))
    vec = pl.BlockSpec((1, sw), lambda b: (0, 0))
    s8shape = jax.ShapeDtypeStruct((N_SEG, ns), F32)
    c0r, c0i = pl.pallas_call(
        carry_body, name="s5_fwd_carry", grid=(nblk,),
        in_specs=[ublk, bspec, bspec, s8, s8, s8, s8],
        out_specs=(s8, s8), out_shape=(s8shape, s8shape),
        scratch_shapes=[pltpu.VMEM((rows, ns), F32), pltpu.VMEM((rows, ns), F32),
                        pltpu.VMEM((N_SEG, ns), F32), pltpu.VMEM((N_SEG, ns), F32)],
        compiler_params=_params(("arbitrary",)),
    )(u, bsr, bsi, a8r, a8i, al8r, al8i)

    def main_body(u_ref, bsr_ref, bsi_ref, csr_ref, csi_ref, ar_ref, ai_ref, c0r_ref, c0i_ref,
                  d_ref, gw_ref, gb_ref, nw_ref, xr_ref, xi_ref, yp_ref, out_ref, pr_sc, pi_sc):
        b = pl.program_id(0)

        @pl.when(b == 0)
        def _():
            pr_sc[...] = c0r_ref[...]
            pi_sc[...] = c0i_ref[...]

        local_scan(u_ref, bsr_ref, bsi_ref, ar_ref, ai_ref, xr_ref, xi_ref, pr_sc, pi_sc)
        for s in range(N_SEC):
            xs = pl.ds(s * secn, secn)
            us = pl.ds(s * secw, secw)
            y = _dot(xr_ref[:, xs].astype(BF16), csr_ref[s]) + _dot(xi_ref[:, xs].astype(BF16), csi_ref[s])
            yp_ref[:, us] = y + d_ref[:, us] * u_ref[:, us]
        yp = yp_ref[...]
        t = jnp.tanh(GELU_K0 * (yp + GELU_K1 * yp * yp * yp))
        y1 = 0.5 * yp * (1.0 + t)
        z = _dot(y1.astype(BF16), gw_ref[...]) + gb_ref[...]
        y2 = y1 * _sigmoid(z)
        xh, _ = _rms_stats(y2)
        out_ref[...] = (xh * nw_ref[...]).astype(BF16)

    xblk = pl.BlockSpec((rows, ns), lambda b: (b, 0))
    xr, xi, yp, out = pl.pallas_call(
        main_body, name="s5_fwd", grid=(nblk,),
        in_specs=[ublk, bspec, bspec, cspec, cspec, s8, s8, s8, s8, vec,
                  pl.BlockSpec((sw, sw), lambda b: (0, 0)), vec, vec],
        out_specs=(xblk, xblk, ublk, ublk),
        out_shape=(jax.ShapeDtypeStruct((lp, ns), F32), jax.ShapeDtypeStruct((lp, ns), F32),
                   jax.ShapeDtypeStruct((lp, sw), F32), jax.ShapeDtypeStruct((lp, sw), BF16)),
        scratch_shapes=[pltpu.VMEM((N_SEG, ns), F32), pltpu.VMEM((N_SEG, ns), F32)],
        compiler_params=_params(("arbitrary",)),
    )(u, bsr, bsi, csr, csi, a8r, a8i, c0r, c0i, d, gluw, glub, nw)
    return xr, xi, c0r, c0i, yp, out


def _s5_bwd(dout, u, yp, xr, xi, c0r, c0i, bsrt, bsit, csrt, csit, a8r, a8i, al8r, al8i, d, gluw, glub, nw, jb):
    lp, sw = u.shape
    ns = a8r.shape[1]
    rows = N_SEG * jb
    nblk = lp // rows
    secw = sw // N_SEC
    secn = ns // N_SEC

    def rowwise_bwd(dout_ref, yp_ref, gw_ref, gb_ref, nw_ref):
        ypv = yp_ref[...]
        t = jnp.tanh(GELU_K0 * (ypv + GELU_K1 * ypv * ypv * ypv))
        y1 = 0.5 * ypv * (1.0 + t)
        dgelu = 0.5 * (1.0 + t) + 0.5 * ypv * (1.0 - t * t) * GELU_K0 * (1.0 + 3.0 * GELU_K1 * ypv * ypv)
        gw = gw_ref[...]
        y1b = y1.astype(BF16)
        sg = _sigmoid(_dot(y1b, gw) + gb_ref[...])
        xh, r = _rms_stats(y1 * sg)
        dov = dout_ref[...]
        dy2 = _rms_bwd(dov, xh, r, nw_ref[...])
        dz = dy2 * y1 * sg * (1.0 - sg)
        dzb = dz.astype(BF16)
        dy1 = dy2 * sg + _dot_nt(dzb, gw)
        return dy1 * dgelu, dov * xh, y1b, dzb, dz

    def lam_scan(dyp_of, csrt_ref, csit_ref, ar_ref, ai_ref, lr_sc, li_sc, nr_sc, ni_sc, extra):
        for s in range(N_SEC):
            db = dyp_of(s)
            lr_sc[:, s * secn:(s + 1) * secn] = _dot(db, csrt_ref[s])
            li_sc[:, s * secn:(s + 1) * secn] = _dot(db, csit_ref[s])
        top = rows - 8
        prev = [(nr_sc[:, cs], ni_sc[:, cs]) for cs in _scan_chunks(ns)]
        prev = _scan_step(lr_sc, li_sc, top, prev, ar_ref, ai_ref, True, ns)
        extra(top, pl.ds(top - 8, 8))

        def step(jj, carry):
            r0 = pl.multiple_of((jb - 1 - jj) * 8, 8)
            rp = pl.multiple_of((jb - 2 - jj) * 8, 8)
            new = _scan_step(lr_sc, li_sc, r0, _pairs(carry), ar_ref, ai_ref, True, ns)
            extra(r0, pl.ds(rp, 8))
            return _flat(new)

        prev = _pairs(lax.fori_loop(1, jb - 1, step, _flat(prev)))
        last = _scan_step(lr_sc, li_sc, 0, prev, ar_ref, ai_ref, True, ns)
        extra(0, None)
        for cs, (vr, vi) in zip(_scan_chunks(ns), last):
            nr_sc[:, cs] = vr
            ni_sc[:, cs] = vi

    def carry_body(dout_ref, yp_ref, u_ref, gw_ref, gb_ref, nw_ref, csrt_ref, csit_ref, ar_ref, ai_ref,
                   alr_ref, ali_ref, cr_ref, ci_ref, dyp_ref, dnw_ref, dgw_ref, dgb_ref, dd_ref,
                   lr_sc, li_sc, nr_sc, ni_sc):
        b = pl.program_id(0)

        @pl.when(b == 0)
        def _():
            nr_sc[...] = jnp.zeros_like(nr_sc)
            ni_sc[...] = jnp.zeros_like(ni_sc)
            for ref in (dnw_ref, dgw_ref, dgb_ref, dd_ref):
                ref[...] = jnp.zeros_like(ref)

        dyp, dnw_rows, y1b, dzb, dz = rowwise_bwd(dout_ref, yp_ref, gw_ref, gb_ref, nw_ref)
        dnw_ref[...] += jnp.sum(dnw_rows, axis=0, keepdims=True)
        dgw_ref[...] += _dot_tn(y1b, dzb)
        dgb_ref[...] += jnp.sum(dz, axis=0, keepdims=True)
        dd_ref[...] += jnp.sum(dyp * u_ref[...], axis=0, keepdims=True)
        dyp_ref[...] = dyp.astype(BF16)
        lam_scan(lambda s: dyp_ref[:, s * secw:(s + 1) * secw], csrt_ref, csit_ref, ar_ref, ai_ref,
                 lr_sc, li_sc, nr_sc, ni_sc, lambda r0, prev_rows: None)

        @pl.when(b == nblk - 1)
        def _():
            fr = _shift_rows(nr_sc[...], False)
            fi = _shift_rows(ni_sc[...], False)
            alr, ali = alr_ref[...], ali_ref[...]
            cr, ci = fr, fi
            for _ in range(N_SEG - 2):
                sr = _shift_rows(cr, False)
                si = _shift_rows(ci, False)
                cr = fr + alr * sr + ali * si
                ci = fi + alr * si - ali * sr
            cr_ref[...] = cr
            ci_ref[...] = ci

    rev = lambda b: (nblk - 1 - b, 0)
    ublk = pl.BlockSpec((rows, sw), rev)
    xblk = pl.BlockSpec((rows, ns), rev)
    s8 = pl.BlockSpec((N_SEG, ns), lambda b: (0, 0))
    vec = pl.BlockSpec((1, sw), lambda b: (0, 0))
    gws = pl.BlockSpec((sw, sw), lambda b: (0, 0))
    btspec = pl.BlockSpec((N_SEC, secn, secw), lambda b: (0, 0, 0))
    ctspec = pl.BlockSpec((N_SEC, secw, secn), lambda b: (0, 0, 0))
    s8shape = jax.ShapeDtypeStruct((N_SEG, ns), F32)
    lcr, lci, dyp_all, d_nw, d_gw, d_gb, d_d = pl.pallas_call(
        carry_body, name="s5_bwd_carry", grid=(nblk,),
        in_specs=[ublk, ublk, ublk, gws, vec, vec, ctspec, ctspec, s8, s8, s8, s8],
        out_specs=(s8, s8, ublk, vec, gws, vec, vec),
        out_shape=(s8shape, s8shape, jax.ShapeDtypeStruct((lp, sw), BF16), jax.ShapeDtypeStruct((1, sw), F32),
                   jax.ShapeDtypeStruct((sw, sw), F32), jax.ShapeDtypeStruct((1, sw), F32),
                   jax.ShapeDtypeStruct((1, sw), F32)),
        scratch_shapes=[pltpu.VMEM((rows, ns), F32), pltpu.VMEM((rows, ns), F32),
                        pltpu.VMEM((N_SEG, ns), F32), pltpu.VMEM((N_SEG, ns), F32)],
        compiler_params=_params(("arbitrary",)),
    )(dout, yp, u, gluw, glub, nw, csrt, csit, a8r, a8i, al8r, al8i)

    def main_body(dyp_sc, u_ref, xr_ref, xi_ref, xtr_ref, xti_ref, c0r_ref, c0i_ref, lcr_ref, lci_ref,
                  d_ref, bsrt_ref, bsit_ref, csrt_ref, csit_ref, ar_ref, ai_ref,
                  du_ref, dcr_ref, dci_ref, dbr_ref, dbi_ref, dar_ref, dai_ref,
                  lr_sc, li_sc, nr_sc, ni_sc):
        b = pl.program_id(0)

        @pl.when(b == 0)
        def _():
            nr_sc[...] = lcr_ref[...]
            ni_sc[...] = lci_ref[...]
            for ref in (dcr_ref, dci_ref, dbr_ref, dbi_ref, dar_ref, dai_ref):
                ref[...] = jnp.zeros_like(ref)

        for s in range(N_SEC):
            db = dyp_sc[:, s * secw:(s + 1) * secw]
            xs = pl.ds(s * secn, secn)
            dcr_ref[s] += _dot_tn(xr_ref[:, xs].astype(BF16), db)
            dci_ref[s] += _dot_tn(xi_ref[:, xs].astype(BF16), db)

        first = b == nblk - 1

        def acc_da(r0, prev_rows):
            for cc in range(ns // SCAN_CW):
                cs = pl.ds(cc * SCAN_CW, SCAN_CW)
                lr = lr_sc[pl.ds(r0, 8), cs]
                li = li_sc[pl.ds(r0, 8), cs]
                if prev_rows is None:
                    xpr = jnp.where(first, c0r_ref[:, cs], xtr_ref[:, cs])
                    xpi = jnp.where(first, c0i_ref[:, cs], xti_ref[:, cs])
                else:
                    xpr = xr_ref[prev_rows, cs]
                    xpi = xi_ref[prev_rows, cs]
                dar_ref[:, cs] += lr * xpr + li * xpi
                dai_ref[:, cs] += li * xpr - lr * xpi

        lam_scan(lambda s: dyp_sc[:, s * secw:(s + 1) * secw], csrt_ref, csit_ref, ar_ref, ai_ref,
                 lr_sc, li_sc, nr_sc, ni_sc, acc_da)

        for s in range(N_SEC):
            xs = pl.ds(s * secn, secn)
            us = pl.ds(s * secw, secw)
            lrb = lr_sc[:, xs].astype(BF16)
            lib = li_sc[:, xs].astype(BF16)
            du = _dot(lrb, bsrt_ref[s]) + _dot(lib, bsit_ref[s]) + d_ref[:, us] * dyp_sc[:, us].astype(F32)
            du_ref[:, us] = du.astype(BF16)
            ub = u_ref[:, us].astype(BF16)
            dbr_ref[s] += _dot_tn(ub, lrb)
            dbi_ref[s] += _dot_tn(ub, lib)

    tail = pl.BlockSpec((N_SEG, ns), lambda b: (jnp.maximum((nblk - 1 - b) * jb - 1, 0), 0))
    acc_c = pl.BlockSpec((N_SEC, secn, secw), lambda b: (0, 0, 0))
    acc_b = pl.BlockSpec((N_SEC, secw, secn), lambda b: (0, 0, 0))
    du, dcr, dci, dbr, dbi, dar, dai = pl.pallas_call(
        main_body, name="s5_bwd", grid=(nblk,),
        in_specs=[ublk, ublk, xblk, xblk, tail, tail, s8, s8, s8, s8,
                  vec, btspec, btspec, ctspec, ctspec, s8, s8],
        out_specs=(ublk, acc_c, acc_c, acc_b, acc_b, s8, s8),
        out_shape=(jax.ShapeDtypeStruct((lp, sw), BF16),
                   jax.ShapeDtypeStruct((N_SEC, secn, secw), F32),
                   jax.ShapeDtypeStruct((N_SEC, secn, secw), F32),
                   jax.ShapeDtypeStruct((N_SEC, secw, secn), F32),
                   jax.ShapeDtypeStruct((N_SEC, secw, secn), F32),
                   s8shape, s8shape),
        scratch_shapes=[pltpu.VMEM((rows, ns), F32), pltpu.VMEM((rows, ns), F32),
                        pltpu.VMEM((N_SEG, ns), F32), pltpu.VMEM((N_SEG, ns), F32)],
        compiler_params=_params(("arbitrary",)),
    )(dyp_all, u, xr, xi, xr, xi, c0r, c0i, lcr, lci, d, bsrt, bsit, csrt, csit, a8r, a8i)
    return du, d_nw, d_gw, d_gb, d_d, dcr, dci, dbr, dbi, dar, dai


def _outproj_fwd(h, ret, ssm, wo):
    lp, d = h.shape
    nck, rs, _ = wo.shape
    rw = ret.shape[1]
    tm = _tile(lp, 640)
    per = rw // rs

    def body(h_ref, ret_ref, ssm_ref, w_ref, o_ref):
        acc = h_ref[...]
        for c in range(nck):
            src = ret_ref if c < per else ssm_ref
            lo = (c % per) * rs
            acc = acc + _dot(src[:, lo:lo + rs], w_ref[c])
        o_ref[...] = acc

    row = lambda w: pl.BlockSpec((tm, w), lambda i: (i, 0))
    return pl.pallas_call(
        body, name="outproj_fwd", grid=(lp // tm,),
        in_specs=[row(d), row(rw), row(ssm.shape[1]), pl.BlockSpec((nck, rs, d), lambda i: (0, 0, 0))],
        out_specs=row(d), out_shape=jax.ShapeDtypeStruct((lp, d), F32),
        compiler_params=_params(("arbitrary",)),
    )(h, ret, ssm, wo)


def _outproj_bwd(dh, ret, ssm, wo):
    lp, d = dh.shape
    nck, rs, _ = wo.shape
    rw = ret.shape[1]
    sw = ssm.shape[1]
    tm = _tile(lp, 640)
    per = rw // rs
    last = lp // tm - 1

    def body(dh_ref, ret_ref, ssm_ref, w_ref, dret_ref, dssm_ref, dw_ref, acc_sc):
        i = pl.program_id(0)

        @pl.when(i == 0)
        def _():
            acc_sc[...] = jnp.zeros_like(acc_sc)

        dhb = dh_ref[...].astype(BF16)
        for c in range(nck):
            src, dst = (ret_ref, dret_ref) if c < per else (ssm_ref, dssm_ref)
            lo = (c % per) * rs
            dst[:, lo:lo + rs] = _dot_nt(dhb, w_ref[c])
            acc_sc[c] += _dot_tn(src[:, lo:lo + rs], dhb)

        @pl.when(i == last)
        def _():
            dw_ref[...] = acc_sc[...].astype(BF16)

    row = lambda w: pl.BlockSpec((tm, w), lambda i: (i, 0))
    wsp = pl.BlockSpec((nck, rs, d), lambda i: (0, 0, 0))
    return pl.pallas_call(
        body, name="outproj_bwd", grid=(lp // tm,),
        in_specs=[row(d), row(rw), row(sw), wsp],
        out_specs=(row(rw), row(sw), wsp),
        out_shape=(jax.ShapeDtypeStruct((lp, rw), F32), jax.ShapeDtypeStruct((lp, sw), F32),
                   jax.ShapeDtypeStruct((nck, rs, d), BF16)),
        scratch_shapes=[pltpu.VMEM((nck, rs, d), F32)],
        compiler_params=_params(("arbitrary",)),
    )(dh, ret, ssm, wo)


def _pack(arrs):
    flat = jnp.concatenate([a.reshape(-1).astype(F32) for a in arrs])
    n = flat.shape[0]
    rows = -(-n // (8 * LANE)) * 8
    return jnp.pad(flat, (0, rows * LANE - n)).reshape(rows, LANE)


def _unpack(packed, shapes):
    flat = packed.reshape(-1)
    out, off = [], 0
    for s in shapes:
        n = math.prod(s)
        out.append(flat[off:off + n].reshape(s))
        off += n
    return out


def _to_segments(a, seg_len):
    return a.reshape(N_SEG, seg_len, a.shape[1]).transpose(1, 0, 2).reshape(a.shape)


def _from_segments(a, seg_len):
    return a.reshape(seg_len, N_SEG, a.shape[1]).transpose(1, 0, 2).reshape(a.shape)


WEIGHT_NAMES = ['meta_tokens', 'ffn1_norm_w', 'ffn1_w_gate', 'ffn1_w_up', 'ffn1_w_down', 'mix_norm_w', 'w_in',
                'ret_norm_w', 'ssm_lambda_re', 'ssm_lambda_im', 'ssm_log_dt', 'ssm_b_re', 'ssm_b_im', 'ssm_c_re',
                'ssm_c_im', 'ssm_d', 'ssm_glu_w', 'ssm_glu_b', 'ssm_norm_w', 'w_out', 'ffn2_norm_w', 'ffn2_w_gate',
                'ffn2_w_up', 'ffn2_w_down', 'final_norm_w']
BIG = ['ffn1_w_gate', 'ffn1_w_up', 'ffn1_w_down', 'w_in', 'ssm_glu_w', 'w_out', 'ffn2_w_gate', 'ffn2_w_up',
       'ffn2_w_down']
TRANSPOSED = ['ffn1_w_gate', 'ffn1_w_up', 'ffn2_w_gate', 'ffn2_w_up']
BIG_EARLY = ['ffn1_w_gate', 'ffn1_w_up', 'ffn1_w_down']
BIG_LATE = [n for n in BIG if n not in BIG_EARLY]
SMALL = [n for n in WEIGHT_NAMES if n not in BIG]


def kernel(x, meta_tokens, ffn1_norm_w, ffn1_w_gate, ffn1_w_up, ffn1_w_down, mix_norm_w, w_in, ret_norm_w, ssm_lambda_re, ssm_lambda_im, ssm_log_dt, ssm_b_re, ssm_b_im, ssm_c_re, ssm_c_im, ssm_d, ssm_glu_w, ssm_glu_b, ssm_norm_w, w_out, ffn2_norm_w, ffn2_w_gate, ffn2_w_up, ffn2_w_down, final_norm_w, loss_target, m_meta_tokens, m_ffn1_norm_w, m_ffn1_w_gate, m_ffn1_w_up, m_ffn1_w_down, m_mix_norm_w, m_w_in, m_ret_norm_w, m_ssm_lambda_re, m_ssm_lambda_im, m_ssm_log_dt, m_ssm_b_re, m_ssm_b_im, m_ssm_c_re, m_ssm_c_im, m_ssm_d, m_ssm_glu_w, m_ssm_glu_b, m_ssm_norm_w, m_w_out, m_ffn2_norm_w, m_ffn2_w_gate, m_ffn2_w_up, m_ffn2_w_down, m_final_norm_w, v_meta_tokens, v_ffn1_norm_w, v_ffn1_w_gate, v_ffn1_w_up, v_ffn1_w_down, v_mix_norm_w, v_w_in, v_ret_norm_w, v_ssm_lambda_re, v_ssm_lambda_im, v_ssm_log_dt, v_ssm_b_re, v_ssm_b_im, v_ssm_c_re, v_ssm_c_im, v_ssm_d, v_ssm_glu_w, v_ssm_glu_b, v_ssm_norm_w, v_w_out, v_ffn2_norm_w, v_ffn2_w_gate, v_ffn2_w_up, v_ffn2_w_down, v_final_norm_w):
    args = locals()
    w = {n: args[n] for n in WEIGHT_NAMES}
    m = {n: args["m_" + n] for n in WEIGHT_NAMES}
    v = {n: args["v_" + n] for n in WEIGHT_NAMES}

    seq, d = x.shape[1], x.shape[2]
    lp = seq + CHUNK
    seg_len = lp // N_SEG
    rw = RET_HEADS * HEAD_DIM
    sw = ssm_d.shape[-1]
    groups = sw // SSM_GROUP
    ns = groups * SSM_STATE
    jb = _tile(seg_len, S5_STEPS, 8)
    chip = 2 * lax.axis_index("x") + lax.axis_index("y")

    as_fd = lambda t: jnp.swapaxes(t, -1, -2)
    shards = {n: (as_fd(w[n][0]) if n in TRANSPOSED else w[n][0]).astype(BF16) for n in BIG}
    early = [shards[n] for n in BIG_EARLY] + [meta_tokens]
    gathered = _gather_two_level("gather_early", early)
    gw = dict(zip(BIG_EARLY, gathered[:-1]))
    meta_full = jnp.transpose(gathered[-1], (1, 0, 2)).reshape(N_META, d)
    late = [shards[n] for n in BIG_LATE]

    freqs = 1.0 / (ROPE_BASE ** (jnp.arange(0, HEAD_DIM, 2, dtype=F32) / HEAD_DIM))
    ang_c = (jnp.arange(lp // CHUNK, dtype=F32) * CHUNK - float(CHUNK - N_META))[:, None] * freqs[None, :]
    ang_r = jnp.arange(CHUNK, dtype=F32)[:, None] * freqs[None, :]
    cos_c, sin_c = jnp.cos(ang_c)[:, None, :], jnp.sin(ang_c)[:, None, :]
    cos_r, sin_r = jnp.cos(ang_r)[None], jnp.sin(ang_r)[None]
    cos_t = (cos_c * cos_r - sin_c * sin_r).reshape(lp, HEAD_DIM // 2)
    sin_t = (sin_c * cos_r + cos_c * sin_r).reshape(lp, HEAD_DIM // 2)
    cosf = jnp.concatenate([cos_t, cos_t], axis=1)
    sinf = jnp.concatenate([-sin_t, sin_t], axis=1)
    tables = _retention_tables(_tile(lp, RET_ROWS, CHUNK))

    lam_re, lam_im, log_dt = ssm_lambda_re[0], ssm_lambda_im[0], ssm_log_dt[0]
    b_re, b_im, c_re, c_im = ssm_b_re[0], ssm_b_im[0], ssm_c_re[0], ssm_c_im[0]
    (ar, ai, bbr, bbi), prep_vjp = jax.vjp(_s5_prepare, lam_re, lam_im, log_dt, b_re, b_im)
    dt = jnp.exp(log_dt)[:, None]
    el = jnp.exp(seg_len * lam_re * dt)
    alr = el * jnp.cos(seg_len * lam_im * dt)
    ali = el * jnp.sin(seg_len * lam_im * dt)
    bc8 = lambda t: jnp.broadcast_to(t.reshape(1, ns), (N_SEG, ns))
    a8r, a8i, al8r, al8i = bc8(ar), bc8(ai), bc8(alr), bc8(ali)
    bsr = _blockdiag_in(jnp.transpose(bbr, (0, 2, 1)))
    bsi = _blockdiag_in(jnp.transpose(bbi, (0, 2, 1)))
    csrt = _blockdiag_in(c_re)
    csit = _blockdiag_in(-c_im)
    tr = lambda t: jnp.transpose(t, (0, 2, 1))
    bsr_b, bsi_b = bsr.astype(BF16), bsi.astype(BF16)
    csr_b, csi_b = tr(csrt).astype(BF16), tr(csit).astype(BF16)
    bsrt_b, bsit_b = tr(bsr).astype(BF16), tr(bsi).astype(BF16)
    csrt_b, csit_b = csrt.astype(BF16), csit.astype(BF16)

    h0 = (jnp.concatenate([jnp.zeros((CHUNK - N_META, d), F32), meta_full], axis=0), x[0])
    (h1, g1, u1), late_half = _ffn_fwd("ffn1_fwd", h0, ffn1_norm_w, gw['ffn1_w_gate'], gw['ffn1_w_up'],
                                       gw['ffn1_w_down'], _allgather_chips_plan(late), late)
    gw.update(zip(BIG_LATE, _forward_sibling("gather_late_forward", late_half)))
    glu_full = gw['ssm_glu_w'].reshape(sw, sw)
    n2, q, k, vv, gate, u = _inproj_fwd(h1, mix_norm_w, gw['w_in'], cosf, sinf, rw)
    o, ret, sprev = _ret_fwd(q, k, vv, gate, ret_norm_w, tables)
    u_seg = _to_segments(u, seg_len)
    xr, xi, c0r, c0i, yp, ssm_seg = _s5_fwd(u_seg, bsr_b, bsi_b, csr_b, csi_b, a8r, a8i, al8r, al8i,
                                            ssm_d, glu_full, ssm_glu_b, ssm_norm_w, jb)
    ssm = _from_segments(ssm_seg, seg_len)
    h2 = _outproj_fwd(h1, ret, ssm, gw['w_out'])
    (dh3, g2, u2, loss_part, d_final), _ = _ffn_fwd(
        "ffn2_fwd_loss", h2, ffn2_norm_w, gw['ffn2_w_gate'], gw['ffn2_w_up'], gw['ffn2_w_down'],
        loss=(final_norm_w.reshape(1, d), loss_target[0]))

    (dh2, d_ffn2_norm, nb, daccb, ab, dgb, dub), _ = _ffn_bwd_act(
        "ffn2_bwd_act", dh3, h2, ffn2_norm_w, g2, u2, gw['ffn2_w_gate'], gw['ffn2_w_up'], gw['ffn2_w_down'])
    (dwg2, dwu2, dwd2), _ = _ffn_bwd_w("ffn2_bwd_w", nb, daccb, ab, dgb, dub)
    dret, dssm, dwo = _outproj_bwd(dh2, ret, ssm, gw['w_out'])
    (du_seg, d_ssm_norm, d_glu_w, d_glu_b, d_ssm_d, dcr_s, dci_s, dbr_s, dbi_s, dar8, dai8) = _s5_bwd(
        _to_segments(dssm, seg_len), u_seg, yp, xr, xi, c0r, c0i, bsrt_b, bsit_b, csrt_b, csit_b,
        a8r, a8i, al8r, al8i, ssm_d, glu_full, ssm_glu_b, ssm_norm_w, jb)
    du = _from_segments(du_seg, seg_len)
    dq, dk, dv, dgate, d_ret_norm = _ret_bwd(dret, q, k, vv, gate, o, sprev, ret_norm_w, tables, cosf, sinf)
    dh1, d_mix_norm, dwin = _inproj_bwd(dh2, h1, mix_norm_w, n2, gw['w_in'], dq, dk, dv, dgate, du)
    late_parts = {
        'w_in': dwin, 'ssm_glu_w': d_glu_w.reshape(N_CHIP, sw // N_CHIP, sw).astype(BF16), 'w_out': dwo,
        'ffn2_w_gate': dwg2, 'ffn2_w_up': dwu2, 'ffn2_w_down': dwd2,
    }
    late_list = [late_parts[n] for n in BIG_LATE]
    (dh0, d_ffn1_norm, nb, daccb, ab, dgb, dub), late_recv = _ffn_bwd_act(
        "ffn1_bwd_act", dh1, h0, ffn1_norm_w, g1, u1, gw['ffn1_w_gate'], gw['ffn1_w_up'], gw['ffn1_w_down'],
        _alltoall_chips_plan(late_list), late_list)
    grad_x = dh0[CHUNK:][None]
    d_meta = dh0[CHUNK - N_META:CHUNK]

    d_c_re = _blockdiag_out(tr(dcr_s), groups, SSM_GROUP, SSM_STATE)
    d_c_im = -_blockdiag_out(tr(dci_s), groups, SSM_GROUP, SSM_STATE)
    d_bbr = jnp.transpose(_blockdiag_out(dbr_s, groups, SSM_GROUP, SSM_STATE), (0, 2, 1))
    d_bbi = jnp.transpose(_blockdiag_out(dbi_s, groups, SSM_GROUP, SSM_STATE), (0, 2, 1))
    d_ar = jnp.sum(dar8, axis=0).reshape(groups, SSM_STATE)
    d_ai = jnp.sum(dai8, axis=0).reshape(groups, SSM_STATE)
    small_parts = [loss_part[0:1, :], d_meta, d_ffn1_norm, d_mix_norm, d_ret_norm, d_ar, d_ai, d_bbr, d_bbi,
                   d_c_re, d_c_im, d_ssm_d, d_glu_b, d_ssm_norm, d_ffn2_norm, d_final]
    small_shapes = [a.shape for a in small_parts]
    packed = _pack(small_parts)
    early_recv, (all_parts,) = _ffn_bwd_w_scatter("ffn1_bwd_w", nb, daccb, ab, dgb, dub, chip,
                                                  _allgather_all_plan([packed]), [packed])
    received = dict(zip(BIG_LATE + BIG_EARLY, late_recv + early_recv))
    chip_sums = _sum_slots("sum_chips", [received[n] for n in BIG], BF16)
    sib_sums = _swap_sibling("swap_sibling", chip_sums)
    (loss_row, g_meta_full, g_ffn1_norm, g_mix_norm, g_ret_norm, g_ar, g_ai, g_bbr, g_bbi, g_c_re, g_c_im,
     g_ssm_d, g_glu_b, g_ssm_norm, g_ffn2_norm, g_final) = _unpack(_sum_slots("sum_small", [all_parts], F32)[0],
                                                                  small_shapes)
    g_lam_re, g_lam_im, g_log_dt, g_b_re, g_b_im = prep_vjp((g_ar, g_ai, g_bbr, g_bbi))
    loss = loss_row[0, 0]
    g_meta = lax.dynamic_slice(g_meta_full, (0, chip * (d // N_CHIP)), (N_META, d // N_CHIP))
    small_grads = {
        'meta_tokens': g_meta, 'ffn1_norm_w': g_ffn1_norm, 'mix_norm_w': g_mix_norm, 'ret_norm_w': g_ret_norm,
        'ssm_lambda_re': g_lam_re[None], 'ssm_lambda_im': g_lam_im[None], 'ssm_log_dt': g_log_dt[None],
        'ssm_b_re': g_b_re[None], 'ssm_b_im': g_b_im[None], 'ssm_c_re': g_c_re[None], 'ssm_c_im': g_c_im[None],
        'ssm_d': g_ssm_d, 'ssm_glu_b': g_glu_b, 'ssm_norm_w': g_ssm_norm, 'ffn2_norm_w': g_ffn2_norm,
        'final_norm_w': g_final.reshape(d),
    }

    grads, deltas, new_m, new_v = {}, {}, {}, {}
    g_pair = {n: [mine, sib] for n, mine, sib in zip(BIG, chip_sums, sib_sums)}
    view = lambda n, t: as_fd(t) if n in TRANSPOSED else t
    big_out = _adam("adam_big", [(view(n, w[n]), view(n, m[n]), view(n, v[n])) for n in BIG], [g_pair[n] for n in BIG])
    for n, outs in zip(BIG, big_out):
        grads[n], deltas[n], new_m[n], new_v[n] = [view(n, t) for t in outs]
    sm_shapes = [w[n].shape for n in SMALL]
    sm_out = _adam("adam_small", [(_pack([w[n] for n in SMALL]), _pack([m[n] for n in SMALL]),
                                  _pack([v[n] for n in SMALL]))],
                   [[_pack([small_grads[n].reshape(w[n].shape) for n in SMALL])]])[0]
    for dst, packed in zip((grads, deltas, new_m, new_v), sm_out):
        for n, t in zip(SMALL, _unpack(packed, sm_shapes)):
            dst[n] = t

    return (loss, grad_x, *[grads[n] for n in WEIGHT_NAMES], *[deltas[n] for n in WEIGHT_NAMES],
            *[new_m[n] for n in WEIGHT_NAMES], *[new_v[n] for n in WEIGHT_NAMES])
```

```python
import functools
import math

import jax
import jax.numpy as jnp
from jax import lax
from jax.experimental import pallas as pl
from jax.experimental.pallas import tpu as pltpu

N_META = 16
RET_HEADS = 4
HEAD_DIM = 128
SSM_GROUP = 16
SSM_STATE = 64
CHUNK = 128
ROPE_BASE = 10000.0
EPS = 1e-6
FFN_RES = 0.5
N_SEG = 8
N_SEC = 4
N_CHIP = 4
LANE = 128
FFN_CPS = 2
BWD_W_ROWS = 1664

ADAM_LR = 0.001
ADAM_B1 = 0.9
ADAM_B2 = 0.999
ADAM_EPS = 1e-08
ADAM_WD = 0.01
ADAM_STEP = 10

VMEM_LIMIT = 56 * 1024 * 1024

F32 = jnp.float32
BF16 = jnp.bfloat16
MESH = pl.DeviceIdType.MESH


def _dot(a, b):
    return jnp.dot(a, b, preferred_element_type=F32)


def _dot_nt(a, b):
    return lax.dot_general(a, b, (((1,), (1,)), ((), ())), preferred_element_type=F32)


def _dot_tn(a, b):
    return lax.dot_general(a, b, (((0,), (0,)), ((), ())), preferred_element_type=F32)


def _tile(n, target, mult=64):
    best = None
    t = mult
    while t <= min(n, target):
        if n % t == 0:
            best = t
        t += mult
    assert best is not None, (n, target)
    return best


def _params(sem, vmem=VMEM_LIMIT):
    return pltpu.CompilerParams(dimension_semantics=sem, vmem_limit_bytes=vmem)


def _rms_stats(xf):
    r = lax.rsqrt(jnp.mean(xf * xf, axis=-1, keepdims=True) + EPS)
    return xf * r, r


def _rms_bwd(dy, xh, r, w):
    dxh = dy * w
    return r * (dxh - xh * jnp.mean(dxh * xh, axis=-1, keepdims=True))


def _sigmoid(x):
    return 0.5 * jnp.tanh(0.5 * x) + 0.5


GELU_K0 = math.sqrt(2.0 / math.pi)
GELU_K1 = 0.044715


CHIP_MASKS = [(1, 0, 0), (0, 1, 0), (1, 1, 0)]
ALL_MASKS = [(0, 0, 1), (0, 1, 0), (0, 1, 1), (1, 0, 0), (1, 0, 1), (1, 1, 0), (1, 1, 1)]
SIB_MASKS = [(0, 0, 1)]
ANY_SPEC = pl.BlockSpec(memory_space=pl.ANY)
MULTI_SUM_STEPS = 4
MULTI_ADAM_STEPS = 8
MXU_COLS = 256


class _Plan:
    def __init__(self, arrays, masks, n_slots, src_slotted, dst_slotted, local_copy, half=False, forward=False):
        self.shapes = [(a.shape, a.dtype) for a in arrays]
        self.n = len(arrays)
        self.masks = masks
        self.n_slots = n_slots
        self.src_slotted, self.dst_slotted, self.local_copy = src_slotted, dst_slotted, local_copy
        self.half, self.forward = half, forward
        self.n_cp = self.n * len(masks) * (len(CHIP_MASKS) if forward else 1)

    def out_shape(self):
        out = []
        for shp, dt in self.shapes:
            if self.dst_slotted and not self.src_slotted:
                shp = (self.n_slots,) + shp
            elif self.src_slotted and not self.dst_slotted:
                shp = shp[1:]
            out.append(jax.ShapeDtypeStruct(shp, dt))
        return tuple(out)

    def scratch(self):
        return [pltpu.SemaphoreType.DMA((self.n_cp,)), pltpu.SemaphoreType.DMA((self.n_cp,)),
                pltpu.SemaphoreType.DMA((self.n,))]

    def _slot(self, px, py, pc):
        if self.n_slots == 8:
            return 4 * px + 2 * py + pc
        if self.n_slots == 4:
            return 2 * px + py
        return pc

    def copies(self, ins, outs, sems):
        send_sems, recv_sems, loc_sems = sems
        x, y, c = lax.axis_index("x"), lax.axis_index("y"), lax.axis_index("c")
        me = self._slot(x, y, c)
        n_m = len(self.masks)
        cps = []
        for a in range(self.n):
            if self.forward:
                rows = self.shapes[a][0][-2] // 2
                mine = pl.ds(pl.multiple_of(c * rows, 8), rows)
                for j, (mx, my, _) in enumerate(CHIP_MASKS):
                    blk = outs[a].at[2 * (1 - x if mx else x) + (1 - y if my else y), mine]
                    k = a * len(CHIP_MASKS) + j
                    cps.append(pltpu.make_async_remote_copy(
                        src_ref=blk, dst_ref=blk, send_sem=send_sems.at[k], recv_sem=recv_sems.at[k],
                        device_id=(x, y, 1 - c), device_id_type=MESH))
                continue
            if self.local_copy:
                src = ins[a].at[me] if self.src_slotted else ins[a]
                cps.append(pltpu.make_async_copy(src, outs[a].at[me], loc_sems.at[a]))
            for mi, (mx, my, mc) in enumerate(self.masks):
                px = 1 - x if mx else x
                py = 1 - y if my else y
                pc = 1 - c if mc else c
                src = ins[a].at[self._slot(px, py, pc)] if self.src_slotted else ins[a]
                dst = outs[a].at[me] if self.dst_slotted else outs[a]
                if self.half:
                    rows = src.shape[-2] // 2
                    mine = pl.ds(pl.multiple_of(c * rows, 8), rows)
                    src, dst = src.at[mine], dst.at[mine]
                k = a * n_m + mi
                cps.append(pltpu.make_async_remote_copy(
                    src_ref=src, dst_ref=dst, send_sem=send_sems.at[k], recv_sem=recv_sems.at[k],
                    device_id=(px, py, pc), device_id_type=MESH))
        return cps


def _exchange(name, plan, arrays):
    n = plan.n

    def body(*refs):
        cps = plan.copies(refs[:n], refs[n:2 * n], refs[2 * n:])
        for cp in cps:
            cp.start()
        for cp in cps:
            cp.wait()

    outs = pl.pallas_call(
        body, name=name, out_shape=plan.out_shape(),
        in_specs=[ANY_SPEC] * n, out_specs=tuple([ANY_SPEC] * n), scratch_shapes=plan.scratch(),
        input_output_aliases={i: i for i in range(n)} if plan.forward else {},
    )(*arrays)
    return list(outs)


def _pcall(body, *, name, grid, in_specs, out_specs, out_shape, scratch_shapes, args, plan=None, plan_args=()):
    sem = ("arbitrary",) * len(grid)
    if plan is None:
        return pl.pallas_call(body, name=name, grid=grid, in_specs=in_specs, out_specs=out_specs,
                              out_shape=out_shape, scratch_shapes=scratch_shapes,
                              compiler_params=_params(sem))(*args), []
    n_in, n_out, n_scr, n_p = len(in_specs), len(out_specs), len(scratch_shapes), plan.n

    def wrapped(*refs):
        ins = refs[:n_in]
        p_ins = refs[n_in:n_in + n_p]
        o0 = n_in + n_p
        outs = refs[o0:o0 + n_out]
        p_outs = refs[o0 + n_out:o0 + n_out + n_p]
        s0 = o0 + n_out + n_p
        scr = refs[s0:s0 + n_scr]
        sems = refs[s0 + n_scr:]
        ids = [pl.program_id(i) for i in range(len(grid))]
        first = functools.reduce(jnp.logical_and, [i == 0 for i in ids])
        last = functools.reduce(jnp.logical_and, [i == g - 1 for i, g in zip(ids, grid)])

        @pl.when(first)
        def _():
            for cp in plan.copies(p_ins, p_outs, sems):
                cp.start()

        body(*ins, *outs, *scr)

        @pl.when(last)
        def _():
            for cp in plan.copies(p_ins, p_outs, sems):
                cp.wait()

    res = pl.pallas_call(
        wrapped, name=name, grid=grid,
        in_specs=list(in_specs) + [ANY_SPEC] * n_p,
        out_specs=tuple(out_specs) + (ANY_SPEC,) * n_p,
        out_shape=tuple(out_shape) + plan.out_shape(),
        scratch_shapes=list(scratch_shapes) + plan.scratch(),
        compiler_params=_params(sem),
    )(*args, *plan_args)
    return res[:n_out], list(res[n_out:])


def _allgather_chips_plan(arrays):
    return _Plan(arrays, CHIP_MASKS, 4, False, True, True, half=True)


def _gather_two_level(name, arrays):
    n = len(arrays)
    ici = _allgather_chips_plan(arrays)
    fwd = _Plan(ici.out_shape(), SIB_MASKS, 4, True, True, False, forward=True)
    n_m = len(CHIP_MASKS)

    def body(*refs):
        ins, outs, sems = refs[:n], refs[n:2 * n], refs[2 * n:]
        ici_cps = ici.copies(ins, outs, sems[:3])
        fwd_cps = fwd.copies(None, outs, sems[3:])
        for cp in ici_cps:
            cp.start()
        for a in range(n):
            for m in range(n_m):
                ici_cps[a * (n_m + 1) + 1 + m].wait_recv()
                fwd_cps[a * n_m + m].start()
        for a in range(n):
            ici_cps[a * (n_m + 1)].wait()
            for m in range(n_m):
                ici_cps[a * (n_m + 1) + 1 + m].wait_send()
        for cp in fwd_cps:
            cp.wait()

    return list(pl.pallas_call(
        body, name=name, out_shape=ici.out_shape(),
        in_specs=[ANY_SPEC] * n, out_specs=tuple([ANY_SPEC] * n), scratch_shapes=ici.scratch() + fwd.scratch(),
    )(*arrays))


def _forward_sibling(name, gathered):
    return _exchange(name, _Plan(gathered, SIB_MASKS, 4, True, True, False, forward=True), gathered)


def _alltoall_chips_plan(arrays):
    return _Plan(arrays, CHIP_MASKS, 4, True, True, True)


def _swap_sibling(name, arrays):
    return _exchange(name, _Plan(arrays, SIB_MASKS, 2, False, False, False), arrays)


def _allgather_all_plan(arrays):
    return _Plan(arrays, ALL_MASKS, 8, False, True, True)


def _sum_slots(name, arrs, out_dtype):
    s = arrs[0].shape[0]
    n = len(arrs)
    steps = arrs[0].shape[1] // _tile(arrs[0].shape[1], 512, 8) if n == 1 else MULTI_SUM_STEPS
    for a in arrs:
        assert a.shape[1] % (16 * steps) == 0 or n == 1, a.shape

    def body(*refs):
        for a_ref, o_ref in zip(refs[:n], refs[n:]):
            acc = a_ref[0].astype(F32)
            for i in range(1, s):
                acc = acc + a_ref[i].astype(F32)
            o_ref[...] = acc.astype(out_dtype)

    return list(pl.pallas_call(
        body, name=name, grid=(steps,),
        in_specs=[pl.BlockSpec((s, a.shape[1] // steps, a.shape[2]), lambda i: (0, i, 0)) for a in arrs],
        out_specs=tuple(pl.BlockSpec((a.shape[1] // steps, a.shape[2]), lambda i: (i, 0)) for a in arrs),
        out_shape=tuple(jax.ShapeDtypeStruct(a.shape[1:], out_dtype) for a in arrs),
        compiler_params=_params(("arbitrary",)),
    )(*arrs))


def _adam_math(w, g, m, v):
    m_new = ADAM_B1 * m + (1.0 - ADAM_B1) * g
    v_new = ADAM_B2 * v + (1.0 - ADAM_B2) * (g * g)
    m_hat = m_new / (1.0 - ADAM_B1 ** ADAM_STEP)
    v_hat = v_new / (1.0 - ADAM_B2 ** ADAM_STEP)
    delta = -ADAM_LR * (m_hat / (jnp.sqrt(v_hat) + ADAM_EPS) + ADAM_WD * w)
    return delta, m_new, v_new


def _adam(name, wmv, g_parts):
    n_w = len(wmv)
    n_g = len(g_parts[0])
    lead = wmv[0][0].ndim == 3
    at = (lambda ref: ref.at[0]) if lead else (lambda ref: ref)
    n_in = 3 + n_g
    rows0 = wmv[0][0].shape[-2]
    steps = rows0 // _tile(rows0, 256, 8) if n_w == 1 else MULTI_ADAM_STEPS
    for w, _, _ in wmv:
        assert w.shape[-2] % (8 * steps) == 0, w.shape

    def body(*refs):
        for j in range(n_w):
            ins = refs[j * n_in:(j + 1) * n_in]
            outs = refs[n_w * n_in + 4 * j:n_w * n_in + 4 * j + 4]
            w_ref, m_ref, v_ref = [at(t) for t in ins[:3]]
            g_out, d_out, m_out, v_out = [at(t) for t in outs]
            g = ins[3][...].astype(F32)
            for gr in ins[4:]:
                g = g + gr[...].astype(F32)
            delta, m_new, v_new = _adam_math(w_ref[...], g, m_ref[...], v_ref[...])
            g_out[...] = g
            d_out[...] = delta
            m_out[...] = m_new
            v_out[...] = v_new

    in_specs, out_specs, out_shape, args = [], [], [], []
    for (w, m, v), gp in zip(wmv, g_parts):
        r, c = w.shape[-2:]
        spec = pl.BlockSpec((r // steps, c), lambda i: (i, 0))
        wspec = pl.BlockSpec((1, r // steps, c), lambda i: (0, i, 0)) if lead else spec
        in_specs += [wspec] * 3 + [spec] * n_g
        out_specs += [wspec] * 4
        out_shape += [jax.ShapeDtypeStruct(w.shape, F32)] * 4
        args += [w, m, v, *gp]
    res = pl.pallas_call(
        body, name=name, grid=(steps,),
        in_specs=in_specs, out_specs=tuple(out_specs), out_shape=tuple(out_shape),
        compiler_params=_params(("arbitrary",)),
    )(*args)
    return [tuple(res[4 * j:4 * j + 4]) for j in range(n_w)]


SUB_ROWS = 32
FFN_BWD_ROWS = 416
FFN_FWD_ROWS = 832
FFN_LOSS_ROWS = 640
RET_ROWS = 640
S5_STEPS = 104


def _tile_parts(tm, d, head, x):
    nsub = tm // SUB_ROWS
    off = head.shape[0] // SUB_ROWS
    specs = [pl.BlockSpec(head.shape, lambda i, k: (0, 0))] + [
        pl.BlockSpec((SUB_ROWS, d), lambda i, k, j=j: (jnp.maximum(i * nsub + j - off, 0), 0)) for j in range(nsub)]

    def assemble(i, part_refs, h_sc):
        head_ref, x_refs = part_refs[0], part_refs[1:]
        for j in range(nsub):
            rows = slice(j * SUB_ROWS, (j + 1) * SUB_ROWS)
            val = x_refs[j][...]
            if j < off:
                val = jnp.where(i == 0, head_ref[rows, :], val)
            h_sc[rows, :] = val

    return specs, [head] + [x] * nsub, assemble


def _h_source(body, h, tm, d):
    if not isinstance(h, tuple):
        return body, [pl.BlockSpec((tm, d), lambda i, k: (i, 0))], [h], []
    specs, args, assemble = _tile_parts(tm, d, *h)
    n_h = len(specs)

    def with_parts(*refs):
        h_sc = refs[-1]

        @pl.when(pl.program_id(1) == 0)
        def _():
            assemble(pl.program_id(0), refs[:n_h], h_sc)

        body(h_sc, *refs[n_h:-1])

    return with_parts, specs, args, [pltpu.VMEM((tm, d), F32)]


def _ffn_fwd(name, h, nw, wg, wu, wd, plan=None, plan_args=(), loss=None):
    lp, d = (h[0].shape[0] + h[1].shape[0], h[1].shape[1]) if isinstance(h, tuple) else h.shape
    nck, f, _ = wg.shape
    tm = _tile(lp, FFN_FWD_ROWS if loss is None else FFN_LOSS_ROWS)
    last = nck // FFN_CPS - 1
    n_t = 0
    if loss is not None:
        t_specs, t_args, t_assemble = _tile_parts(tm, d, jnp.zeros((lp - loss[1].shape[0], d), F32), loss[1])
        n_t = len(t_specs)

    def body(h_ref, nw_ref, wg_ref, wu_ref, wd_ref, *rest):
        if loss is not None:
            fw_ref, t_parts, rest = rest[0], rest[1:1 + n_t], rest[1 + n_t:]
            ho_ref, g_ref, u_ref, loss_ref, dfw_ref, n_sc, acc_sc, t_sc = rest
        else:
            ho_ref, g_ref, u_ref, n_sc, acc_sc = rest
        i = pl.program_id(0)
        k = pl.program_id(1)

        @pl.when(k == 0)
        def _():
            xh, _ = _rms_stats(h_ref[...])
            n_sc[...] = (xh * nw_ref[...]).astype(BF16)
            acc_sc[...] = jnp.zeros_like(acc_sc)

        n = n_sc[...]
        acc = acc_sc[...]
        for c in range(FFN_CPS):
            g = _dot_nt(n, wg_ref[c])
            u = _dot_nt(n, wu_ref[c])
            g_ref[c] = g.astype(BF16)
            u_ref[c] = u.astype(BF16)
            a = (g * _sigmoid(g) * u).astype(BF16)
            acc = acc + _dot(a, wd_ref[c])
        acc_sc[...] = acc

        if loss is None:
            @pl.when(k == last)
            def _():
                ho_ref[...] = h_ref[...] + FFN_RES * acc_sc[...]
            return

        @pl.when(jnp.logical_and(i == 0, k == 0))
        def _():
            loss_ref[...] = jnp.zeros_like(loss_ref)
            dfw_ref[...] = jnp.zeros_like(dfw_ref)

        @pl.when(k == last)
        def _():
            t_assemble(i, t_parts, t_sc)
            xh, r = _rms_stats(h_ref[...] + FFN_RES * acc_sc[...])
            w = fw_ref[...]
            head_rows = lp - loss[1].shape[0]
            row = lax.broadcasted_iota(jnp.int32, (tm, d), 0) + i * tm
            err = jnp.where(row < head_rows, 0.0, xh * w - t_sc[...])
            loss_ref[...] += 0.5 * jnp.sum(err * err) / d
            dout = err * (1.0 / d)
            dfw_ref[...] += jnp.sum(dout * xh, axis=0, keepdims=True)
            ho_ref[...] = _rms_bwd(dout, xh, r, w)

    body, h_specs, h_args, h_scratch = _h_source(body, h, tm, d)
    vec = pl.BlockSpec((1, d), lambda i, k: (0, 0))
    w_fd = pl.BlockSpec((FFN_CPS, f, d), lambda i, k: (k, 0, 0))
    hid = pl.BlockSpec((FFN_CPS, tm, f), lambda i, k: (k, i, 0))
    hshape = jax.ShapeDtypeStruct((nck, lp, f), BF16)
    args, in_specs = (*h_args, nw, wg, wu, wd), h_specs + [vec, w_fd, w_fd, w_fd]
    out_specs = (pl.BlockSpec((tm, d), lambda i, k: (i, 0)), hid, hid)
    out_shape = (jax.ShapeDtypeStruct((lp, d), F32), hshape, hshape)
    scratch = [pltpu.VMEM((tm, d), BF16), pltpu.VMEM((tm, d), F32)]
    if loss is not None:
        args, in_specs = (*args, loss[0], *t_args), in_specs + [vec] + t_specs
        out_specs += (pl.BlockSpec((8, LANE), lambda i, k: (0, 0)), vec)
        out_shape += (jax.ShapeDtypeStruct((8, LANE), F32), jax.ShapeDtypeStruct((1, d), F32))
        scratch = scratch + [pltpu.VMEM((tm, d), F32)]
    return _pcall(
        body, name=name, grid=(lp // tm, nck // FFN_CPS), plan=plan, plan_args=plan_args,
        args=args, in_specs=in_specs, out_specs=out_specs, out_shape=out_shape,
        scratch_shapes=scratch + h_scratch)


def _ffn_bwd_act(name, dh, h, nw, g, u, wg, wu, wd, plan=None, plan_args=()):
    lp, d = dh.shape
    nck, f, _ = wg.shape
    tm = _tile(lp, FFN_BWD_ROWS, SUB_ROWS)
    last = nck // FFN_CPS - 1

    def body(h_ref, dh_ref, nw_ref, g_ref, u_ref, wg_ref, wu_ref, wd_ref,
             dhi_ref, dnw_ref, n_ref, dacc_ref, a_ref, dg_ref, du_ref,
             xh_sc, r_sc, dn_sc):
        i = pl.program_id(0)
        k = pl.program_id(1)

        @pl.when(k == 0)
        def _():
            xh, r = _rms_stats(h_ref[...])
            xh_sc[...] = xh
            r_sc[...] = r
            n_ref[...] = (xh * nw_ref[...]).astype(BF16)
            dacc_ref[...] = (FFN_RES * dh_ref[...]).astype(BF16)
            dn_sc[...] = jnp.zeros_like(dn_sc)

        @pl.when(jnp.logical_and(i == 0, k == 0))
        def _():
            dnw_ref[...] = jnp.zeros_like(dnw_ref)

        dacc = dacc_ref[...]
        dn = dn_sc[...]
        for c in range(FFN_CPS):
            gv = g_ref[c].astype(F32)
            uv = u_ref[c].astype(F32)
            sg = _sigmoid(gv)
            sil = gv * sg
            da = _dot_nt(dacc, wd_ref[c])
            dgk = (da * uv * (sg * (1.0 + gv * (1.0 - sg)))).astype(BF16)
            duk = (da * sil).astype(BF16)
            a_ref[c] = (sil * uv).astype(BF16)
            dg_ref[c] = dgk
            du_ref[c] = duk
            dn = dn + _dot(dgk, wg_ref[c]) + _dot(duk, wu_ref[c])
        dn_sc[...] = dn

        @pl.when(k == last)
        def _():
            dnl = dn_sc[...]
            xh = xh_sc[...]
            dhi_ref[...] = dh_ref[...] + _rms_bwd(dnl, xh, r_sc[...], nw_ref[...])
            dnw_ref[...] += jnp.sum(dnl * xh, axis=0, keepdims=True)

    body, h_specs, h_args, h_scratch = _h_source(body, h, tm, d)
    row = pl.BlockSpec((tm, d), lambda i, k: (i, 0))
    vec = pl.BlockSpec((1, d), lambda i, k: (0, 0))
    hid = pl.BlockSpec((FFN_CPS, tm, f), lambda i, k: (k, i, 0))
    w_fd = pl.BlockSpec((FFN_CPS, f, d), lambda i, k: (k, 0, 0))
    rshape = jax.ShapeDtypeStruct((lp, d), BF16)
    hshape = jax.ShapeDtypeStruct((nck, lp, f), BF16)
    return _pcall(
        body, name=name, grid=(lp // tm, nck // FFN_CPS), plan=plan, plan_args=plan_args,
        args=(*h_args, dh, nw, g, u, wg, wu, wd),
        in_specs=h_specs + [row, vec, hid, hid, w_fd, w_fd, w_fd],
        out_specs=(row, vec, row, row, hid, hid, hid),
        out_shape=(jax.ShapeDtypeStruct((lp, d), F32), jax.ShapeDtypeStruct((1, d), F32),
                   rshape, rshape, hshape, hshape, hshape),
        scratch_shapes=[pltpu.VMEM((tm, d), F32), pltpu.VMEM((tm, 1), F32), pltpu.VMEM((tm, d), F32)] + h_scratch)


def _ffn_bwd_w(name, n, dacc, a, dg, du, plan=None, plan_args=()):
    lp, d = n.shape
    nck, _, f = a.shape
    tm = _tile(lp, BWD_W_ROWS)
    last = lp // tm - 1

    def body(n_ref, dacc_ref, a_ref, dg_ref, du_ref, dwg_ref, dwu_ref, dwd_ref, ag_sc, au_sc, ad_sc):
        i = pl.program_id(1)

        @pl.when(i == 0)
        def _():
            ag_sc[...] = jnp.zeros_like(ag_sc)
            au_sc[...] = jnp.zeros_like(au_sc)
            ad_sc[...] = jnp.zeros_like(ad_sc)

        nv = n_ref[...]
        ag_sc[...] += _dot_tn(dg_ref[0], nv)
        au_sc[...] += _dot_tn(du_ref[0], nv)
        ad_sc[...] += _dot_tn(a_ref[0], dacc_ref[...])

        @pl.when(i == last)
        def _():
            dwg_ref[0] = ag_sc[...].astype(BF16)
            dwu_ref[0] = au_sc[...].astype(BF16)
            dwd_ref[0] = ad_sc[...].astype(BF16)

    row = pl.BlockSpec((tm, d), lambda k, i: (i, 0))
    hid = pl.BlockSpec((1, tm, f), lambda k, i: (k, i, 0))
    w_fd = pl.BlockSpec((1, f, d), lambda k, i: (k, 0, 0))
    wshape = jax.ShapeDtypeStruct((nck, f, d), BF16)
    return _pcall(
        body, name=name, grid=(nck, lp // tm), plan=plan, plan_args=plan_args, args=(n, dacc, a, dg, du),
        in_specs=[row, row, hid, hid, hid], out_specs=(w_fd, w_fd, w_fd), out_shape=(wshape,) * 3,
        scratch_shapes=[pltpu.VMEM((f, d), F32)] * 3)


def _ffn_bwd_w_scatter(name, n, dacc, a, dg, du, chip, plan, plan_args):
    lp, d = n.shape
    nck, _, f = a.shape
    tm = _tile(lp, BWD_W_ROWS)
    last_i = lp // tm - 1
    n_w = 3
    n_p = plan.n

    def body(me_ref, n_ref, dacc_ref, a_ref, dg_ref, du_ref, *rest):
        p_ins = rest[:n_p]
        recv = rest[n_p:n_p + n_w]
        p_outs = rest[n_p + n_w:2 * n_p + n_w]
        acc = rest[2 * n_p + n_w:2 * n_p + 2 * n_w]
        stage, send_sems, recv_sems, loc_sems = rest[2 * n_p + 2 * n_w:2 * n_p + 2 * n_w + 4]
        p_sems = rest[2 * n_p + 2 * n_w + 4:]
        p = pl.program_id(0)
        i = pl.program_id(1)
        me = me_ref[0]
        c = lax.axis_index("c")

        def send(w, pos):
            kk = jnp.bitwise_xor(me, nck - 1 - pos)
            diff = jnp.bitwise_xor(kk, me)
            m = jnp.where(diff == 2, 0, jnp.where(diff == 1, 1, 2))
            return pltpu.make_async_remote_copy(
                src_ref=stage.at[lax.rem(pos, 2), w], dst_ref=recv[w].at[me],
                send_sem=send_sems.at[w * 3 + m], recv_sem=recv_sems.at[w * 3 + m],
                device_id=(lax.div(kk, 2), lax.rem(kk, 2), c), device_id_type=MESH)

        @pl.when(jnp.logical_and(p == 0, i == 0))
        def _():
            for cp in plan.copies(p_ins, p_outs, p_sems):
                cp.start()

        @pl.when(i == 0)
        def _():
            for t in acc:
                t[...] = jnp.zeros_like(t)

        nv = n_ref[...]
        acc[0][...] += _dot_tn(dg_ref[0], nv)
        acc[1][...] += _dot_tn(du_ref[0], nv)
        acc[2][...] += _dot_tn(a_ref[0], dacc_ref[...])

        @pl.when(jnp.logical_and(i == last_i, p >= 2))
        def _():
            for w in range(n_w):
                send(w, p - 2).wait_send()

        @pl.when(i == last_i)
        def _():
            for w in range(n_w):
                stage[lax.rem(p, 2), w] = acc[w][...].astype(BF16)

        @pl.when(jnp.logical_and(i == last_i, p < nck - 1))
        def _():
            for w in range(n_w):
                send(w, p).start()

        @pl.when(jnp.logical_and(i == last_i, p == nck - 1))
        def _():
            own = [pltpu.make_async_copy(stage.at[(nck - 1) % 2, w], recv[w].at[me], loc_sems.at[w])
                   for w in range(n_w)]
            for cp in own:
                cp.start()
            for w in range(n_w):
                send(w, nck - 2).wait_send()
            for cp in own:
                cp.wait()
            for w in range(n_w):
                for m in range(3):
                    pltpu.make_async_remote_copy(
                        src_ref=stage.at[0, w], dst_ref=recv[w].at[me],
                        send_sem=send_sems.at[w * 3 + m], recv_sem=recv_sems.at[w * 3 + m],
                        device_id=(0, 0, c), device_id_type=MESH).wait_recv()
            for cp in plan.copies(p_ins, p_outs, p_sems):
                cp.wait()

    chunk = lambda k, me_ref: jnp.bitwise_xor(me_ref[0], nck - 1 - k)
    row = pl.BlockSpec((tm, d), lambda k, i, me_ref: (i, 0))
    hid = pl.BlockSpec((1, tm, f), lambda k, i, me_ref: (chunk(k, me_ref), i, 0))
    wshape = jax.ShapeDtypeStruct((nck, f, d), BF16)
    res = pl.pallas_call(
        body, name=name,
        grid_spec=pltpu.PrefetchScalarGridSpec(
            num_scalar_prefetch=1, grid=(nck, lp // tm),
            in_specs=[row, row, hid, hid, hid] + [ANY_SPEC] * n_p,
            out_specs=(ANY_SPEC,) * (n_w + n_p),
            scratch_shapes=[pltpu.VMEM((f, d), F32)] * n_w + [
                pltpu.VMEM((2, n_w, f, d), BF16), pltpu.SemaphoreType.DMA((n_w * 3,)),
                pltpu.SemaphoreType.DMA((n_w * 3,)), pltpu.SemaphoreType.DMA((n_w,))] + plan.scratch()),
        out_shape=(wshape,) * n_w + plan.out_shape(),
        compiler_params=_params(("arbitrary", "arbitrary")),
    )(chip.reshape(1).astype(jnp.int32), n, dacc, a, dg, du, *plan_args)
    return list(res[:n_w]), list(res[n_w:])


def _inproj_fwd(h, nw, w_in, cosf, sinf, rw):
    lp, d = h.shape
    nck, _, ps = w_in.shape
    proj = nck * ps
    sw = proj - 4 * rw
    tm = _tile(lp, 640)
    scale = HEAD_DIM ** -0.5
    heads = rw // HEAD_DIM

    def body(h_ref, nw_ref, w_ref, cos_ref, sin_ref, n_ref, q_ref, k_ref, v_ref, g_ref, u_ref, p_sc):
        xh, _ = _rms_stats(h_ref[...])
        n = (xh * nw_ref[...]).astype(BF16)
        n_ref[...] = n
        for c in range(nck):
            p_sc[:, c * ps:(c + 1) * ps] = _dot(n, w_ref[c])
        cs = cos_ref[...]
        sn = sin_ref[...]
        for hh in range(heads):
            lo = hh * HEAD_DIM
            qh = p_sc[:, lo:lo + HEAD_DIM]
            q_ref[:, lo:lo + HEAD_DIM] = (qh * cs + pltpu.roll(qh, HEAD_DIM // 2, 1) * sn).astype(BF16)
            kh = p_sc[:, rw + lo:rw + lo + HEAD_DIM]
            k_ref[:, lo:lo + HEAD_DIM] = ((kh * cs + pltpu.roll(kh, HEAD_DIM // 2, 1) * sn) * scale).astype(BF16)
        v_ref[...] = p_sc[:, 2 * rw:3 * rw].astype(BF16)
        g_ref[...] = p_sc[:, 3 * rw:4 * rw]
        u_ref[...] = p_sc[:, 4 * rw:]

    row = lambda w: pl.BlockSpec((tm, w), lambda i: (i, 0))
    return pl.pallas_call(
        body, name="inproj_fwd", grid=(lp // tm,),
        in_specs=[row(d), pl.BlockSpec((1, d), lambda i: (0, 0)),
                  pl.BlockSpec((nck, d, ps), lambda i: (0, 0, 0)), row(HEAD_DIM), row(HEAD_DIM)],
        out_specs=(row(d), row(rw), row(rw), row(rw), row(rw), row(sw)),
        out_shape=(jax.ShapeDtypeStruct((lp, d), BF16),
                   jax.ShapeDtypeStruct((lp, rw), BF16),
                   jax.ShapeDtypeStruct((lp, rw), BF16),
                   jax.ShapeDtypeStruct((lp, rw), BF16),
                   jax.ShapeDtypeStruct((lp, rw), F32),
                   jax.ShapeDtypeStruct((lp, sw), F32)),
        scratch_shapes=[pltpu.VMEM((tm, proj), F32)],
        compiler_params=_params(("arbitrary",)),
    )(h, nw, w_in, cosf, sinf)


def _inproj_bwd(dh, h, nw, n, w_in, dq, dk, dv, dg, du):
    lp, d = h.shape
    nck, _, ps = w_in.shape
    rw = dq.shape[1]
    sw = du.shape[1]
    proj = nck * ps
    tm = _tile(lp, 640)
    last = lp // tm - 1

    def gather_dproj(p_sc, dq_ref, dk_ref, dv_ref, dg_ref, du_ref):
        p_sc[:, 0:rw] = dq_ref[...]
        p_sc[:, rw:2 * rw] = dk_ref[...]
        p_sc[:, 2 * rw:3 * rw] = dv_ref[...]
        p_sc[:, 3 * rw:4 * rw] = dg_ref[...]
        p_sc[:, 4 * rw:] = du_ref[...]

    def act_body(dh_ref, h_ref, nw_ref, w_ref, dq_ref, dk_ref, dv_ref, dg_ref, du_ref, dhi_ref, dnw_ref, p_sc):
        i = pl.program_id(0)

        @pl.when(i == 0)
        def _():
            dnw_ref[...] = jnp.zeros_like(dnw_ref)

        gather_dproj(p_sc, dq_ref, dk_ref, dv_ref, dg_ref, du_ref)
        dn = jnp.zeros((tm, d), F32)
        for c in range(nck):
            dn = dn + _dot_nt(p_sc[:, c * ps:(c + 1) * ps], w_ref[c])
        xh, r = _rms_stats(h_ref[...])
        dhi_ref[...] = dh_ref[...] + _rms_bwd(dn, xh, r, nw_ref[...])
        dnw_ref[...] += jnp.sum(dn * xh, axis=0, keepdims=True)

    def w_body(n_ref, dq_ref, dk_ref, dv_ref, dg_ref, du_ref, dw_ref, p_sc, acc_sc):
        i = pl.program_id(0)

        @pl.when(i == 0)
        def _():
            acc_sc[...] = jnp.zeros_like(acc_sc)

        gather_dproj(p_sc, dq_ref, dk_ref, dv_ref, dg_ref, du_ref)
        nv = n_ref[...]
        for c in range(nck):
            acc_sc[c] += _dot_tn(nv, p_sc[:, c * ps:(c + 1) * ps])

        @pl.when(i == last)
        def _():
            dw_ref[...] = acc_sc[...].astype(BF16)

    row = lambda w: pl.BlockSpec((tm, w), lambda i: (i, 0))
    vec = pl.BlockSpec((1, d), lambda i: (0, 0))
    wsp = pl.BlockSpec((nck, d, ps), lambda i: (0, 0, 0))
    dproj_specs = [row(rw), row(rw), row(rw), row(rw), row(sw)]
    dhi, dnw = pl.pallas_call(
        act_body, name="inproj_bwd_act", grid=(lp // tm,),
        in_specs=[row(d), row(d), vec, wsp] + dproj_specs,
        out_specs=(row(d), vec),
        out_shape=(jax.ShapeDtypeStruct((lp, d), F32), jax.ShapeDtypeStruct((1, d), F32)),
        scratch_shapes=[pltpu.VMEM((tm, proj), BF16)],
        compiler_params=_params(("arbitrary",)),
    )(dh, h, nw, w_in, dq, dk, dv, dg, du)
    dw = pl.pallas_call(
        w_body, name="inproj_bwd_w", grid=(lp // tm,),
        in_specs=[row(d)] + dproj_specs,
        out_specs=wsp, out_shape=jax.ShapeDtypeStruct((nck, d, ps), BF16),
        scratch_shapes=[pltpu.VMEM((tm, proj), BF16), pltpu.VMEM((nck, d, ps), F32)],
        compiler_params=_params(("arbitrary",)),
    )(n, dq, dk, dv, dg, du)
    return dhi, dnw, dw


def _retention_tables(rc):
    h = jnp.arange(RET_HEADS, dtype=F32)
    log_g = jnp.log(1.0 - 2.0 ** (-5.0 - h))
    i = jnp.arange(rc)
    diff = i[:, None] - i[None, :]
    dec = jnp.where(diff[None] >= 0,
                    jnp.exp(log_g[:, None, None] * jnp.maximum(diff, 0)[None].astype(F32)), 0.0)
    pos = jnp.arange(rc, dtype=F32)
    wq = jnp.exp(log_g[:, None] * (pos + 1.0)[None])
    wk = jnp.exp(log_g[:, None] * (rc - 1 - pos)[None])
    gch = jnp.exp(log_g * rc)
    ones = jnp.ones((1, 1, HEAD_DIM), F32)
    return (dec, wq[:, :, None] * ones, wk[:, :, None] * ones,
            gch[:, None, None] * jnp.ones((1, 8, HEAD_DIM), F32))


def _head_norm(o):
    mu = jnp.mean(o, axis=-1, keepdims=True)
    oc = o - mu
    r = lax.rsqrt(jnp.mean(oc * oc, axis=-1, keepdims=True) + EPS)
    return oc * r, r


def _strips(rc):
    return [(c0, min(MXU_COLS, rc - c0)) for c0 in range(0, rc, MXU_COLS)]


def _pad_rows(x, n):
    return x if n == 0 else jnp.concatenate([jnp.zeros((n, x.shape[1]), x.dtype), x], axis=0)


def _ret_fwd(q, k, v, g, rnw, tables):
    lp, rw = q.shape
    heads = rw // HEAD_DIM
    rc = tables[0].shape[1]
    nch = lp // rc
    strips = _strips(rc)
    dec, wq, wk, gch = tables

    def body(q_ref, k_ref, v_ref, g_ref, w_ref, dec_ref, wq_ref, wk_ref, gch_ref,
             o_ref, ret_ref, sp_ref, s_sc):
        n = pl.program_id(0)

        @pl.when(n == 0)
        def _():
            s_sc[...] = jnp.zeros_like(s_sc)

        cols = [slice(hh * HEAD_DIM, (hh + 1) * HEAD_DIM) for hh in range(heads)]
        s_ins = [s_sc[hh] for hh in range(heads)]
        outs = []
        for hh, cs in enumerate(cols):
            qv, kv, vv = q_ref[:, cs], k_ref[:, cs], v_ref[:, cs]
            s_in = s_ins[hh]
            qw = (qv.astype(F32) * wq_ref[hh]).astype(BF16)
            kw = (kv.astype(F32) * wk_ref[hh]).astype(BF16)
            o = _dot(qw, s_in.astype(BF16))
            for c0, wd in strips:
                a = _dot_nt(qv[c0:], kv[c0:c0 + wd]) * dec_ref[hh, c0:, c0:c0 + wd]
                o = o + _pad_rows(_dot(a.astype(BF16), vv[c0:c0 + wd]), c0)
            s_new = gch_ref[hh, 0:1, :] * s_in + _dot_tn(kw, vv)
            xh, _ = _head_norm(o)
            gv = g_ref[:, cs]
            outs.append((o, s_new, (gv * _sigmoid(gv) * (xh * w_ref[:, cs])).astype(BF16)))
        for hh, cs in enumerate(cols):
            o, s_new, ret = outs[hh]
            sp_ref[hh, 0] = s_ins[hh]
            s_sc[hh] = s_new
            o_ref[:, cs] = o
            ret_ref[:, cs] = ret

    blk = pl.BlockSpec((rc, rw), lambda n: (n, 0))
    tab = pl.BlockSpec((heads, rc, HEAD_DIM), lambda n: (0, 0, 0))
    dtab = pl.BlockSpec((heads, rc, rc), lambda n: (0, 0, 0))
    return pl.pallas_call(
        body, name="retention_fwd", grid=(nch,),
        in_specs=[blk, blk, blk, blk, pl.BlockSpec((1, rw), lambda n: (0, 0)),
                  dtab, tab, tab, pl.BlockSpec((heads, 8, HEAD_DIM), lambda n: (0, 0, 0))],
        out_specs=(blk, blk, pl.BlockSpec((heads, 1, HEAD_DIM, HEAD_DIM), lambda n: (0, n, 0, 0))),
        out_shape=(jax.ShapeDtypeStruct((lp, rw), F32),
                   jax.ShapeDtypeStruct((lp, rw), BF16),
                   jax.ShapeDtypeStruct((heads, nch, HEAD_DIM, HEAD_DIM), F32)),
        scratch_shapes=[pltpu.VMEM((heads, HEAD_DIM, HEAD_DIM), F32)],
        compiler_params=_params(("arbitrary",)),
    )(q, k, v, g, rnw, dec, wq, wk, gch)


def _ret_bwd(dret, q, k, v, g, o, sprev, rnw, tables, cosf, sinf):
    lp, rw = q.shape
    heads = rw // HEAD_DIM
    rc = tables[0].shape[1]
    nch = lp // rc
    strips = _strips(rc)
    dec, wq, wk, gch = tables
    scale = HEAD_DIM ** -0.5
    half = HEAD_DIM // 2

    def body(dret_ref, q_ref, k_ref, v_ref, g_ref, o_ref, sp_ref, w_ref, dec_ref, wq_ref, wk_ref, gch_ref,
             cos_ref, sin_ref, dq_ref, dk_ref, dv_ref, dg_ref, dw_ref, ds_sc):
        n = pl.program_id(0)

        @pl.when(n == 0)
        def _():
            ds_sc[...] = jnp.zeros_like(ds_sc)
            dw_ref[...] = jnp.zeros_like(dw_ref)

        cosv = cos_ref[...]
        sinv = sin_ref[...]
        cols = [slice(hh * HEAD_DIM, (hh + 1) * HEAD_DIM) for hh in range(heads)]
        ds_ins = [ds_sc[hh] for hh in range(heads)]
        dw_ins = [dw_ref[:, cs] for cs in cols]
        outs = []
        for hh, cs in enumerate(cols):
            qv, kv, vv = q_ref[:, cs], k_ref[:, cs], v_ref[:, cs]
            gv = g_ref[:, cs]
            dr = dret_ref[:, cs]
            w = w_ref[:, cs]
            sg = _sigmoid(gv)
            sil = gv * sg
            xh, r = _head_norm(o_ref[:, cs])
            dgate = (dr * (xh * w) * (sg * (1.0 + gv * (1.0 - sg)))).astype(BF16)
            dyw = dr * sil
            dw_new = dw_ins[hh] + jnp.sum(dyw * xh, axis=0, keepdims=True)
            dxh = dyw * w
            do = r * (dxh - jnp.mean(dxh, axis=-1, keepdims=True)
                      - xh * jnp.mean(dxh * xh, axis=-1, keepdims=True))
            dob = do.astype(BF16)
            wqv = wq_ref[hh]
            wkv = wk_ref[hh]
            qw = (qv.astype(F32) * wqv).astype(BF16)
            kw = (kv.astype(F32) * wkv).astype(BF16)
            s_in = sp_ref[hh, 0].astype(BF16)
            ds = ds_ins[hh]
            dsb = ds.astype(BF16)
            dq = _dot_nt(dob, s_in) * wqv
            dk_parts, dv_parts = [], []
            for c0, wd in strips:
                dmask = dec_ref[hh, c0:, c0:c0 + wd]
                a = (_dot_nt(qv[c0:], kv[c0:c0 + wd]) * dmask).astype(BF16)
                da = (_dot_nt(dob[c0:], vv[c0:c0 + wd]) * dmask).astype(BF16)
                dq = dq + _pad_rows(_dot(da, kv[c0:c0 + wd]), c0)
                dk_parts.append(_dot_tn(da, qv[c0:]))
                dv_parts.append(_dot_tn(a, dob[c0:]))
            dk = jnp.concatenate(dk_parts, axis=0) + _dot_nt(vv, dsb) * wkv
            dv = jnp.concatenate(dv_parts, axis=0) + _dot(kw, dsb)
            ds_new = gch_ref[hh, 0:1, :] * ds + _dot_tn(qw, dob)
            outs.append((dgate, dw_new, ds_new,
                         (dq * cosv + pltpu.roll(dq * sinv, half, 1)).astype(BF16),
                         ((dk * cosv + pltpu.roll(dk * sinv, half, 1)) * scale).astype(BF16),
                         dv.astype(BF16)))
        for hh, cs in enumerate(cols):
            dgate, dw_new, ds_new, dqv, dkv, dvv = outs[hh]
            dg_ref[:, cs] = dgate
            dw_ref[:, cs] = dw_new
            ds_sc[hh] = ds_new
            dq_ref[:, cs] = dqv
            dk_ref[:, cs] = dkv
            dv_ref[:, cs] = dvv

    blk = pl.BlockSpec((rc, rw), lambda n: (nch - 1 - n, 0))
    tab = pl.BlockSpec((heads, rc, HEAD_DIM), lambda n: (0, 0, 0))
    dtab = pl.BlockSpec((heads, rc, rc), lambda n: (0, 0, 0))
    wsp = pl.BlockSpec((1, rw), lambda n: (0, 0))
    pos = pl.BlockSpec((rc, HEAD_DIM), lambda n: (nch - 1 - n, 0))
    bshape = jax.ShapeDtypeStruct((lp, rw), BF16)
    return pl.pallas_call(
        body, name="retention_bwd", grid=(nch,),
        in_specs=[blk, blk, blk, blk, blk, blk,
                  pl.BlockSpec((heads, 1, HEAD_DIM, HEAD_DIM), lambda n: (0, nch - 1 - n, 0, 0)),
                  wsp, dtab, tab, tab, pl.BlockSpec((heads, 8, HEAD_DIM), lambda n: (0, 0, 0)), pos, pos],
        out_specs=(blk, blk, blk, blk, wsp),
        out_shape=(bshape, bshape, bshape, bshape, jax.ShapeDtypeStruct((1, rw), F32)),
        scratch_shapes=[pltpu.VMEM((heads, HEAD_DIM, HEAD_DIM), F32)],
        compiler_params=_params(("arbitrary",)),
    )(dret, q, k, v, g, o, sprev, rnw, dec, wq, wk, gch, cosf, sinf)


SCAN_CW = 512


def _s5_prepare(lam_re, lam_im, log_dt, b_re, b_im):
    dt = jnp.exp(log_dt)[:, None]
    er = jnp.exp(lam_re * dt)
    ar = er * jnp.cos(lam_im * dt)
    ai = er * jnp.sin(lam_im * dt)
    den = lam_re * lam_re + lam_im * lam_im
    fr = ((ar - 1.0) * lam_re + ai * lam_im) / den
    fi = (ai * lam_re - (ar - 1.0) * lam_im) / den
    bbr = fr[..., None] * b_re - fi[..., None] * b_im
    bbi = fr[..., None] * b_im + fi[..., None] * b_re
    return ar, ai, bbr, bbi


def _blockdiag_in(t):
    g, p, n = t.shape
    gs = g // N_SEC
    t = t.reshape(N_SEC, gs, p, n)
    eye = jnp.eye(gs, dtype=t.dtype)
    return jnp.einsum("sgpn,gh->sgphn", t, eye).reshape(N_SEC, gs * p, gs * n)


def _blockdiag_out(m, g, p, n):
    gs = g // N_SEC
    m = m.reshape(N_SEC, gs, p, gs, n)
    eye = jnp.eye(gs, dtype=m.dtype)
    return jnp.einsum("sgphn,gh->sgpn", m, eye).reshape(g, p, n)


def _scan_step(xr_ref, xi_ref, r0, prev, ar_ref, ai_ref, conj, ncols):
    new = []
    for cc in range(ncols // SCAN_CW):
        cs = pl.ds(cc * SCAN_CW, SCAN_CW)
        pr, pi = prev[cc]
        ar = ar_ref[:, cs]
        ai = ai_ref[:, cs]
        if conj:
            nr = ar * pr + ai * pi
            ni = ar * pi - ai * pr
        else:
            nr = ar * pr - ai * pi
            ni = ar * pi + ai * pr
        xr = xr_ref[pl.ds(r0, 8), cs] + nr
        xi = xi_ref[pl.ds(r0, 8), cs] + ni
        xr_ref[pl.ds(r0, 8), cs] = xr
        xi_ref[pl.ds(r0, 8), cs] = xi
        new.append((xr, xi))
    return new


def _scan_chunks(ncols):
    return [pl.ds(cc * SCAN_CW, SCAN_CW) for cc in range(ncols // SCAN_CW)]


def _flat(pairs):
    return tuple(t for p in pairs for t in p)


def _pairs(flat):
    return [(flat[2 * k], flat[2 * k + 1]) for k in range(len(flat) // 2)]


def _shift_rows(z, down):
    row = lax.broadcasted_iota(jnp.int32, z.shape, 0)
    if down:
        return jnp.where(row == 0, 0.0, pltpu.roll(z, 1, 0))
    return jnp.where(row == N_SEG - 1, 0.0, pltpu.roll(z, N_SEG - 1, 0))


def _s5_fwd(u, bsr, bsi, csr, csi, a8r, a8i, al8r, al8i, d, gluw, glub, nw, jb):
    lp, sw = u.shape
    ns = a8r.shape[1]
    rows = N_SEG * jb
    nblk = lp // rows
    secw = sw // N_SEC
    secn = ns // N_SEC

    def local_scan(u_ref, bsr_ref, bsi_ref, ar_ref, ai_ref, xr_ref, xi_ref, pr_sc, pi_sc):
        for s in range(N_SEC):
            ub = u_ref[:, s * secw:(s + 1) * secw].astype(BF16)
            xr_ref[:, s * secn:(s + 1) * secn] = _dot(ub, bsr_ref[s])
            xi_ref[:, s * secn:(s + 1) * secn] = _dot(ub, bsi_ref[s])
        prev = [(pr_sc[:, cs], pi_sc[:, cs]) for cs in _scan_chunks(ns)]
        prev = _scan_step(xr_ref, xi_ref, 0, prev, ar_ref, ai_ref, False, ns)

        def step(j, carry):
            r0 = pl.multiple_of(j * 8, 8)
            return _flat(_scan_step(xr_ref, xi_ref, r0, _pairs(carry), ar_ref, ai_ref, False, ns))

        last = _pairs(lax.fori_loop(1, jb, step, _flat(prev)))
        for cs, (vr, vi) in zip(_scan_chunks(ns), last):
            pr_sc[:, cs] = vr
            pi_sc[:, cs] = vi

    def carry_body(u_ref, bsr_ref, bsi_ref, ar_ref, ai_ref, alr_ref, ali_ref, cr_ref, ci_ref,
                   xr_sc, xi_sc, pr_sc, pi_sc):
        b = pl.program_id(0)

        @pl.when(b == 0)
        def _():
            pr_sc[...] = jnp.zeros_like(pr_sc)
            pi_sc[...] = jnp.zeros_like(pi_sc)

        local_scan(u_ref, bsr_ref, bsi_ref, ar_ref, ai_ref, xr_sc, xi_sc, pr_sc, pi_sc)

        @pl.when(b == nblk - 1)
        def _():
            er = _shift_rows(pr_sc[...], True)
            ei = _shift_rows(pi_sc[...], True)
            alr, ali = alr_ref[...], ali_ref[...]
            cr, ci = er, ei
            for _ in range(N_SEG - 2):
                sr = _shift_rows(cr, True)
                si = _shift_rows(ci, True)
                cr = er + alr * sr - ali * si
                ci = ei + alr * si + ali * sr
            cr_ref[...] = cr
            ci_ref[...] = ci

    ublk = pl.BlockSpec((rows, sw), lambda b: (b, 0))
    bspec = pl.BlockSpec((N_SEC, secw, secn), lambda b: (0, 0, 0))
    cspec = pl.BlockSpec((N_SEC, secn, secw), lambda b: (0, 0, 0))
    s8 = pl.BlockSpec((N_SEG, ns), lambda b: (0, 0))
    vec = pl.BlockSpec((1, sw), lambda b: (0, 0))
    s8shape = jax.ShapeDtypeStruct((N_SEG, ns), F32)
    c0r, c0i = pl.pallas_call(
        carry_body, name="s5_fwd_carry", grid=(nblk,),
        in_specs=[ublk, bspec, bspec, s8, s8, s8, s8],
        out_specs=(s8, s8), out_shape=(s8shape, s8shape),
        scratch_shapes=[pltpu.VMEM((rows, ns), F32), pltpu.VMEM((rows, ns), F32),
                        pltpu.VMEM((N_SEG, ns), F32), pltpu.VMEM((N_SEG, ns), F32)],
        compiler_params=_params(("arbitrary",)),
    )(u, bsr, bsi, a8r, a8i, al8r, al8i)

    def main_body(u_ref, bsr_ref, bsi_ref, csr_ref, csi_ref, ar_ref, ai_ref, c0r_ref, c0i_ref,
                  d_ref, gw_ref, gb_ref, nw_ref, xr_ref, xi_ref, yp_ref, out_ref, pr_sc, pi_sc):
        b = pl.program_id(0)

        @pl.when(b == 0)
        def _():
            pr_sc[...] = c0r_ref[...]
            pi_sc[...] = c0i_ref[...]

        local_scan(u_ref, bsr_ref, bsi_ref, ar_ref, ai_ref, xr_ref, xi_ref, pr_sc, pi_sc)
        for s in range(N_SEC):
            xs = pl.ds(s * secn, secn)
            us = pl.ds(s * secw, secw)
            y = _dot(xr_ref[:, xs].astype(BF16), csr_ref[s]) + _dot(xi_ref[:, xs].astype(BF16), csi_ref[s])
            yp_ref[:, us] = y + d_ref[:, us] * u_ref[:, us]
        yp = yp_ref[...]
        t = jnp.tanh(GELU_K0 * (yp + GELU_K1 * yp * yp * yp))
        y1 = 0.5 * yp * (1.0 + t)
        z = _dot(y1.astype(BF16), gw_ref[...]) + gb_ref[...]
        y2 = y1 * _sigmoid(z)
        xh, _ = _rms_stats(y2)
        out_ref[...] = (xh * nw_ref[...]).astype(BF16)

    xblk = pl.BlockSpec((rows, ns), lambda b: (b, 0))
    xr, xi, yp, out = pl.pallas_call(
        main_body, name="s5_fwd", grid=(nblk,),
        in_specs=[ublk, bspec, bspec, cspec, cspec, s8, s8, s8, s8, vec,
                  pl.BlockSpec((sw, sw), lambda b: (0, 0)), vec, vec],
        out_specs=(xblk, xblk, ublk, ublk),
        out_shape=(jax.ShapeDtypeStruct((lp, ns), F32), jax.ShapeDtypeStruct((lp, ns), F32),
                   jax.ShapeDtypeStruct((lp, sw), F32), jax.ShapeDtypeStruct((lp, sw), BF16)),
        scratch_shapes=[pltpu.VMEM((N_SEG, ns), F32), pltpu.VMEM((N_SEG, ns), F32)],
        compiler_params=_params(("arbitrary",)),
    )(u, bsr, bsi, csr, csi, a8r, a8i, c0r, c0i, d, gluw, glub, nw)
    return xr, xi, c0r, c0i, yp, out


def _s5_bwd(dout, u, yp, xr, xi, c0r, c0i, bsrt, bsit, csrt, csit, a8r, a8i, al8r, al8i, d, gluw, glub, nw, jb):
    lp, sw = u.shape
    ns = a8r.shape[1]
    rows = N_SEG * jb
    nblk = lp // rows
    secw = sw // N_SEC
    secn = ns // N_SEC

    def rowwise_bwd(dout_ref, yp_ref, gw_ref, gb_ref, nw_ref):
        ypv = yp_ref[...]
        t = jnp.tanh(GELU_K0 * (ypv + GELU_K1 * ypv * ypv * ypv))
        y1 = 0.5 * ypv * (1.0 + t)
        dgelu = 0.5 * (1.0 + t) + 0.5 * ypv * (1.0 - t * t) * GELU_K0 * (1.0 + 3.0 * GELU_K1 * ypv * ypv)
        gw = gw_ref[...]
        y1b = y1.astype(BF16)
        sg = _sigmoid(_dot(y1b, gw) + gb_ref[...])
        xh, r = _rms_stats(y1 * sg)
        dov = dout_ref[...]
        dy2 = _rms_bwd(dov, xh, r, nw_ref[...])
        dz = dy2 * y1 * sg * (1.0 - sg)
        dzb = dz.astype(BF16)
        dy1 = dy2 * sg + _dot_nt(dzb, gw)
        return dy1 * dgelu, dov * xh, y1b, dzb, dz

    def lam_scan(dyp_of, csrt_ref, csit_ref, ar_ref, ai_ref, lr_sc, li_sc, nr_sc, ni_sc, extra):
        for s in range(N_SEC):
            db = dyp_of(s)
            lr_sc[:, s * secn:(s + 1) * secn] = _dot(db, csrt_ref[s])
            li_sc[:, s * secn:(s + 1) * secn] = _dot(db, csit_ref[s])
        top = rows - 8
        prev = [(nr_sc[:, cs], ni_sc[:, cs]) for cs in _scan_chunks(ns)]
        prev = _scan_step(lr_sc, li_sc, top, prev, ar_ref, ai_ref, True, ns)
        extra(top, pl.ds(top - 8, 8))

        def step(jj, carry):
            r0 = pl.multiple_of((jb - 1 - jj) * 8, 8)
            rp = pl.multiple_of((jb - 2 - jj) * 8, 8)
            new = _scan_step(lr_sc, li_sc, r0, _pairs(carry), ar_ref, ai_ref, True, ns)
            extra(r0, pl.ds(rp, 8))
            return _flat(new)

        prev = _pairs(lax.fori_loop(1, jb - 1, step, _flat(prev)))
        last = _scan_step(lr_sc, li_sc, 0, prev, ar_ref, ai_ref, True, ns)
        extra(0, None)
        for cs, (vr, vi) in zip(_scan_chunks(ns), last):
            nr_sc[:, cs] = vr
            ni_sc[:, cs] = vi

    def carry_body(dout_ref, yp_ref, u_ref, gw_ref, gb_ref, nw_ref, csrt_ref, csit_ref, ar_ref, ai_ref,
                   alr_ref, ali_ref, cr_ref, ci_ref, dyp_ref, dnw_ref, dgw_ref, dgb_ref, dd_ref,
                   lr_sc, li_sc, nr_sc, ni_sc):
        b = pl.program_id(0)

        @pl.when(b == 0)
        def _():
            nr_sc[...] = jnp.zeros_like(nr_sc)
            ni_sc[...] = jnp.zeros_like(ni_sc)
            for ref in (dnw_ref, dgw_ref, dgb_ref, dd_ref):
                ref[...] = jnp.zeros_like(ref)

        dyp, dnw_rows, y1b, dzb, dz = rowwise_bwd(dout_ref, yp_ref, gw_ref, gb_ref, nw_ref)
        dnw_ref[...] += jnp.sum(dnw_rows, axis=0, keepdims=True)
        dgw_ref[...] += _dot_tn(y1b, dzb)
        dgb_ref[...] += jnp.sum(dz, axis=0, keepdims=True)
        dd_ref[...] += jnp.sum(dyp * u_ref[...], axis=0, keepdims=True)
        dyp_ref[...] = dyp.astype(BF16)
        lam_scan(lambda s: dyp_ref[:, s * secw:(s + 1) * secw], csrt_ref, csit_ref, ar_ref, ai_ref,
                 lr_sc, li_sc, nr_sc, ni_sc, lambda r0, prev_rows: None)

        @pl.when(b == nblk - 1)
        def _():
            fr = _shift_rows(nr_sc[...], False)
            fi = _shift_rows(ni_sc[...], False)
            alr, ali = alr_ref[...], ali_ref[...]
            cr, ci = fr, fi
            for _ in range(N_SEG - 2):
                sr = _shift_rows(cr, False)
                si = _shift_rows(ci, False)
                cr = fr + alr * sr + ali * si
                ci = fi + alr * si - ali * sr
            cr_ref[...] = cr
            ci_ref[...] = ci

    rev = lambda b: (nblk - 1 - b, 0)
    ublk = pl.BlockSpec((rows, sw), rev)
    xblk = pl.BlockSpec((rows, ns), rev)
    s8 = pl.BlockSpec((N_SEG, ns), lambda b: (0, 0))
    vec = pl.BlockSpec((1, sw), lambda b: (0, 0))
    gws = pl.BlockSpec((sw, sw), lambda b: (0, 0))
    btspec = pl.BlockSpec((N_SEC, secn, secw), lambda b: (0, 0, 0))
    ctspec = pl.BlockSpec((N_SEC, secw, secn), lambda b: (0, 0, 0))
    s8shape = jax.ShapeDtypeStruct((N_SEG, ns), F32)
    lcr, lci, dyp_all, d_nw, d_gw, d_gb, d_d = pl.pallas_call(
        carry_body, name="s5_bwd_carry", grid=(nblk,),
        in_specs=[ublk, ublk, ublk, gws, vec, vec, ctspec, ctspec, s8, s8, s8, s8],
        out_specs=(s8, s8, ublk, vec, gws, vec, vec),
        out_shape=(s8shape, s8shape, jax.ShapeDtypeStruct((lp, sw), BF16), jax.ShapeDtypeStruct((1, sw), F32),
                   jax.ShapeDtypeStruct((sw, sw), F32), jax.ShapeDtypeStruct((1, sw), F32),
                   jax.ShapeDtypeStruct((1, sw), F32)),
        scratch_shapes=[pltpu.VMEM((rows, ns), F32), pltpu.VMEM((rows, ns), F32),
                        pltpu.VMEM((N_SEG, ns), F32), pltpu.VMEM((N_SEG, ns), F32)],
        compiler_params=_params(("arbitrary",)),
    )(dout, yp, u, gluw, glub, nw, csrt, csit, a8r, a8i, al8r, al8i)

    def main_body(dyp_sc, u_ref, xr_ref, xi_ref, xtr_ref, xti_ref, c0r_ref, c0i_ref, lcr_ref, lci_ref,
                  d_ref, bsrt_ref, bsit_ref, csrt_ref, csit_ref, ar_ref, ai_ref,
                  du_ref, dcr_ref, dci_ref, dbr_ref, dbi_ref, dar_ref, dai_ref,
                  lr_sc, li_sc, nr_sc, ni_sc):
        b = pl.program_id(0)

        @pl.when(b == 0)
        def _():
            nr_sc[...] = lcr_ref[...]
            ni_sc[...] = lci_ref[...]
            for ref in (dcr_ref, dci_ref, dbr_ref, dbi_ref, dar_ref, dai_ref):
                ref[...] = jnp.zeros_like(ref)

        for s in range(N_SEC):
            db = dyp_sc[:, s * secw:(s + 1) * secw]
            xs = pl.ds(s * secn, secn)
            dcr_ref[s] += _dot_tn(xr_ref[:, xs].astype(BF16), db)
            dci_ref[s] += _dot_tn(xi_ref[:, xs].astype(BF16), db)

        first = b == nblk - 1

        def acc_da(r0, prev_rows):
            for cc in range(ns // SCAN_CW):
                cs = pl.ds(cc * SCAN_CW, SCAN_CW)
                lr = lr_sc[pl.ds(r0, 8), cs]
                li = li_sc[pl.ds(r0, 8), cs]
                if prev_rows is None:
                    xpr = jnp.where(first, c0r_ref[:, cs], xtr_ref[:, cs])
                    xpi = jnp.where(first, c0i_ref[:, cs], xti_ref[:, cs])
                else:
                    xpr = xr_ref[prev_rows, cs]
                    xpi = xi_ref[prev_rows, cs]
                dar_ref[:, cs] += lr * xpr + li * xpi
                dai_ref[:, cs] += li * xpr - lr * xpi

        lam_scan(lambda s: dyp_sc[:, s * secw:(s + 1) * secw], csrt_ref, csit_ref, ar_ref, ai_ref,
                 lr_sc, li_sc, nr_sc, ni_sc, acc_da)

        for s in range(N_SEC):
            xs = pl.ds(s * secn, secn)
            us = pl.ds(s * secw, secw)
            lrb = lr_sc[:, xs].astype(BF16)
            lib = li_sc[:, xs].astype(BF16)
            du = _dot(lrb, bsrt_ref[s]) + _dot(lib, bsit_ref[s]) + d_ref[:, us] * dyp_sc[:, us].astype(F32)
            du_ref[:, us] = du.astype(BF16)
            ub = u_ref[:, us].astype(BF16)
            dbr_ref[s] += _dot_tn(ub, lrb)
            dbi_ref[s] += _dot_tn(ub, lib)

    tail = pl.BlockSpec((N_SEG, ns), lambda b: (jnp.maximum((nblk - 1 - b) * jb - 1, 0), 0))
    acc_c = pl.BlockSpec((N_SEC, secn, secw), lambda b: (0, 0, 0))
    acc_b = pl.BlockSpec((N_SEC, secw, secn), lambda b: (0, 0, 0))
    du, dcr, dci, dbr, dbi, dar, dai = pl.pallas_call(
        main_body, name="s5_bwd", grid=(nblk,),
        in_specs=[ublk, ublk, xblk, xblk, tail, tail, s8, s8, s8, s8,
                  vec, btspec, btspec, ctspec, ctspec, s8, s8],
        out_specs=(ublk, acc_c, acc_c, acc_b, acc_b, s8, s8),
        out_shape=(jax.ShapeDtypeStruct((lp, sw), BF16),
                   jax.ShapeDtypeStruct((N_SEC, secn, secw), F32),
                   jax.ShapeDtypeStruct((N_SEC, secn, secw), F32),
                   jax.ShapeDtypeStruct((N_SEC, secw, secn), F32),
                   jax.ShapeDtypeStruct((N_SEC, secw, secn), F32),
                   s8shape, s8shape),
        scratch_shapes=[pltpu.VMEM((rows, ns), F32), pltpu.VMEM((rows, ns), F32),
                        pltpu.VMEM((N_SEG, ns), F32), pltpu.VMEM((N_SEG, ns), F32)],
        compiler_params=_params(("arbitrary",)),
    )(dyp_all, u, xr, xi, xr, xi, c0r, c0i, lcr, lci, d, bsrt, bsit, csrt, csit, a8r, a8i)
    return du, d_nw, d_gw, d_gb, d_d, dcr, dci, dbr, dbi, dar, dai


def _outproj_fwd(h, ret, ssm, wo):
    lp, d = h.shape
    nck, rs, _ = wo.shape
    rw = ret.shape[1]
    tm = _tile(lp, 640)
    per = rw // rs

    def body(h_ref, ret_ref, ssm_ref, w_ref, o_ref):
        acc = h_ref[...]
        for c in range(nck):
            src = ret_ref if c < per else ssm_ref
            lo = (c % per) * rs
            acc = acc + _dot(src[:, lo:lo + rs], w_ref[c])
        o_ref[...] = acc

    row = lambda w: pl.BlockSpec((tm, w), lambda i: (i, 0))
    return pl.pallas_call(
        body, name="outproj_fwd", grid=(lp // tm,),
        in_specs=[row(d), row(rw), row(ssm.shape[1]), pl.BlockSpec((nck, rs, d), lambda i: (0, 0, 0))],
        out_specs=row(d), out_shape=jax.ShapeDtypeStruct((lp, d), F32),
        compiler_params=_params(("arbitrary",)),
    )(h, ret, ssm, wo)


def _outproj_bwd(dh, ret, ssm, wo):
    lp, d = dh.shape
    nck, rs, _ = wo.shape
    rw = ret.shape[1]
    sw = ssm.shape[1]
    tm = _tile(lp, 640)
    per = rw // rs
    last = lp // tm - 1

    def body(dh_ref, ret_ref, ssm_ref, w_ref, dret_ref, dssm_ref, dw_ref, acc_sc):
        i = pl.program_id(0)

        @pl.when(i == 0)
        def _():
            acc_sc[...] = jnp.zeros_like(acc_sc)

        dhb = dh_ref[...].astype(BF16)
        for c in range(nck):
            src, dst = (ret_ref, dret_ref) if c < per else (ssm_ref, dssm_ref)
            lo = (c % per) * rs
            dst[:, lo:lo + rs] = _dot_nt(dhb, w_ref[c])
            acc_sc[c] += _dot_tn(src[:, lo:lo + rs], dhb)

        @pl.when(i == last)
        def _():
            dw_ref[...] = acc_sc[...].astype(BF16)

    row = lambda w: pl.BlockSpec((tm, w), lambda i: (i, 0))
    wsp = pl.BlockSpec((nck, rs, d), lambda i: (0, 0, 0))
    return pl.pallas_call(
        body, name="outproj_bwd", grid=(lp // tm,),
        in_specs=[row(d), row(rw), row(sw), wsp],
        out_specs=(row(rw), row(sw), wsp),
        out_shape=(jax.ShapeDtypeStruct((lp, rw), F32), jax.ShapeDtypeStruct((lp, sw), F32),
                   jax.ShapeDtypeStruct((nck, rs, d), BF16)),
        scratch_shapes=[pltpu.VMEM((nck, rs, d), F32)],
        compiler_params=_params(("arbitrary",)),
    )(dh, ret, ssm, wo)


def _pack(arrs):
    flat = jnp.concatenate([a.reshape(-1).astype(F32) for a in arrs])
    n = flat.shape[0]
    rows = -(-n // (8 * LANE)) * 8
    return jnp.pad(flat, (0, rows * LANE - n)).reshape(rows, LANE)


def _unpack(packed, shapes):
    flat = packed.reshape(-1)
    out, off = [], 0
    for s in shapes:
        n = math.prod(s)
        out.append(flat[off:off + n].reshape(s))
        off += n
    return out


def _to_segments(a, seg_len):
    return a.reshape(N_SEG, seg_len, a.shape[1]).transpose(1, 0, 2).reshape(a.shape)


def _from_segments(a, seg_len):
    return a.reshape(seg_len, N_SEG, a.shape[1]).transpose(1, 0, 2).reshape(a.shape)


WEIGHT_NAMES = ['meta_tokens', 'ffn1_norm_w', 'ffn1_w_gate', 'ffn1_w_up', 'ffn1_w_down', 'mix_norm_w', 'w_in',
                'ret_norm_w', 'ssm_lambda_re', 'ssm_lambda_im', 'ssm_log_dt', 'ssm_b_re', 'ssm_b_im', 'ssm_c_re',
                'ssm_c_im', 'ssm_d', 'ssm_glu_w', 'ssm_glu_b', 'ssm_norm_w', 'w_out', 'ffn2_norm_w', 'ffn2_w_gate',
                'ffn2_w_up', 'ffn2_w_down', 'final_norm_w']
BIG = ['ffn1_w_gate', 'ffn1_w_up', 'ffn1_w_down', 'w_in', 'ssm_glu_w', 'w_out', 'ffn2_w_gate', 'ffn2_w_up',
       'ffn2_w_down']
TRANSPOSED = ['ffn1_w_gate', 'ffn1_w_up', 'ffn2_w_gate', 'ffn2_w_up']
BIG_EARLY = ['ffn1_w_gate', 'ffn1_w_up', 'ffn1_w_down']
BIG_LATE = [n for n in BIG if n not in BIG_EARLY]
SMALL = [n for n in WEIGHT_NAMES if n not in BIG]


def kernel(x, meta_tokens, ffn1_norm_w, ffn1_w_gate, ffn1_w_up, ffn1_w_down, mix_norm_w, w_in, ret_norm_w, ssm_lambda_re, ssm_lambda_im, ssm_log_dt, ssm_b_re, ssm_b_im, ssm_c_re, ssm_c_im, ssm_d, ssm_glu_w, ssm_glu_b, ssm_norm_w, w_out, ffn2_norm_w, ffn2_w_gate, ffn2_w_up, ffn2_w_down, final_norm_w, loss_target, m_meta_tokens, m_ffn1_norm_w, m_ffn1_w_gate, m_ffn1_w_up, m_ffn1_w_down, m_mix_norm_w, m_w_in, m_ret_norm_w, m_ssm_lambda_re, m_ssm_lambda_im, m_ssm_log_dt, m_ssm_b_re, m_ssm_b_im, m_ssm_c_re, m_ssm_c_im, m_ssm_d, m_ssm_glu_w, m_ssm_glu_b, m_ssm_norm_w, m_w_out, m_ffn2_norm_w, m_ffn2_w_gate, m_ffn2_w_up, m_ffn2_w_down, m_final_norm_w, v_meta_tokens, v_ffn1_norm_w, v_ffn1_w_gate, v_ffn1_w_up, v_ffn1_w_down, v_mix_norm_w, v_w_in, v_ret_norm_w, v_ssm_lambda_re, v_ssm_lambda_im, v_ssm_log_dt, v_ssm_b_re, v_ssm_b_im, v_ssm_c_re, v_ssm_c_im, v_ssm_d, v_ssm_glu_w, v_ssm_glu_b, v_ssm_norm_w, v_w_out, v_ffn2_norm_w, v_ffn2_w_gate, v_ffn2_w_up, v_ffn2_w_down, v_final_norm_w):
    args = locals()
    w = {n: args[n] for n in WEIGHT_NAMES}
    m = {n: args["m_" + n] for n in WEIGHT_NAMES}
    v = {n: args["v_" + n] for n in WEIGHT_NAMES}

    seq, d = x.shape[1], x.shape[2]
    lp = seq + CHUNK
    seg_len = lp // N_SEG
    rw = RET_HEADS * HEAD_DIM
    sw = ssm_d.shape[-1]
    groups = sw // SSM_GROUP
    ns = groups * SSM_STATE
    jb = _tile(seg_len, S5_STEPS, 8)
    chip = 2 * lax.axis_index("x") + lax.axis_index("y")

    as_fd = lambda t: jnp.swapaxes(t, -1, -2)
    shards = {n: (as_fd(w[n][0]) if n in TRANSPOSED else w[n][0]).astype(BF16) for n in BIG}
    early = [shards[n] for n in BIG_EARLY] + [meta_tokens]
    gathered = _gather_two_level("gather_early", early)
    gw = dict(zip(BIG_EARLY, gathered[:-1]))
    meta_full = jnp.transpose(gathered[-1], (1, 0, 2)).reshape(N_META, d)
    late = [shards[n] for n in BIG_LATE]

    freqs = 1.0 / (ROPE_BASE ** (jnp.arange(0, HEAD_DIM, 2, dtype=F32) / HEAD_DIM))
    ang_c = (jnp.arange(lp // CHUNK, dtype=F32) * CHUNK - float(CHUNK - N_META))[:, None] * freqs[None, :]
    ang_r = jnp.arange(CHUNK, dtype=F32)[:, None] * freqs[None, :]
    cos_c, sin_c = jnp.cos(ang_c)[:, None, :], jnp.sin(ang_c)[:, None, :]
    cos_r, sin_r = jnp.cos(ang_r)[None], jnp.sin(ang_r)[None]
    cos_t = (cos_c * cos_r - sin_c * sin_r).reshape(lp, HEAD_DIM // 2)
    sin_t = (sin_c * cos_r + cos_c * sin_r).reshape(lp, HEAD_DIM // 2)
    cosf = jnp.concatenate([cos_t, cos_t], axis=1)
    sinf = jnp.concatenate([-sin_t, sin_t], axis=1)
    tables = _retention_tables(_tile(lp, RET_ROWS, CHUNK))

    lam_re, lam_im, log_dt = ssm_lambda_re[0], ssm_lambda_im[0], ssm_log_dt[0]
    b_re, b_im, c_re, c_im = ssm_b_re[0], ssm_b_im[0], ssm_c_re[0], ssm_c_im[0]
    (ar, ai, bbr, bbi), prep_vjp = jax.vjp(_s5_prepare, lam_re, lam_im, log_dt, b_re, b_im)
    dt = jnp.exp(log_dt)[:, None]
    el = jnp.exp(seg_len * lam_re * dt)
    alr = el * jnp.cos(seg_len * lam_im * dt)
    ali = el * jnp.sin(seg_len * lam_im * dt)
    bc8 = lambda t: jnp.broadcast_to(t.reshape(1, ns), (N_SEG, ns))
    a8r, a8i, al8r, al8i = bc8(ar), bc8(ai), bc8(alr), bc8(ali)
    bsr = _blockdiag_in(jnp.transpose(bbr, (0, 2, 1)))
    bsi = _blockdiag_in(jnp.transpose(bbi, (0, 2, 1)))
    csrt = _blockdiag_in(c_re)
    csit = _blockdiag_in(-c_im)
    tr = lambda t: jnp.transpose(t, (0, 2, 1))
    bsr_b, bsi_b = bsr.astype(BF16), bsi.astype(BF16)
    csr_b, csi_b = tr(csrt).astype(BF16), tr(csit).astype(BF16)
    bsrt_b, bsit_b = tr(bsr).astype(BF16), tr(bsi).astype(BF16)
    csrt_b, csit_b = csrt.astype(BF16), csit.astype(BF16)

    h0 = (jnp.concatenate([jnp.zeros((CHUNK - N_META, d), F32), meta_full], axis=0), x[0])
    (h1, g1, u1), late_half = _ffn_fwd("ffn1_fwd", h0, ffn1_norm_w, gw['ffn1_w_gate'], gw['ffn1_w_up'],
                                       gw['ffn1_w_down'], _allgather_chips_plan(late), late)
    gw.update(zip(BIG_LATE, _forward_sibling("gather_late_forward", late_half)))
    glu_full = gw['ssm_glu_w'].reshape(sw, sw)
    n2, q, k, vv, gate, u = _inproj_fwd(h1, mix_norm_w, gw['w_in'], cosf, sinf, rw)
    o, ret, sprev = _ret_fwd(q, k, vv, gate, ret_norm_w, tables)
    u_seg = _to_segments(u, seg_len)
    xr, xi, c0r, c0i, yp, ssm_seg = _s5_fwd(u_seg, bsr_b, bsi_b, csr_b, csi_b, a8r, a8i, al8r, al8i,
                                            ssm_d, glu_full, ssm_glu_b, ssm_norm_w, jb)
    ssm = _from_segments(ssm_seg, seg_len)
    h2 = _outproj_fwd(h1, ret, ssm, gw['w_out'])
    (dh3, g2, u2, loss_part, d_final), _ = _ffn_fwd(
        "ffn2_fwd_loss", h2, ffn2_norm_w, gw['ffn2_w_gate'], gw['ffn2_w_up'], gw['ffn2_w_down'],
        loss=(final_norm_w.reshape(1, d), loss_target[0]))

    (dh2, d_ffn2_norm, nb, daccb, ab, dgb, dub), _ = _ffn_bwd_act(
        "ffn2_bwd_act", dh3, h2, ffn2_norm_w, g2, u2, gw['ffn2_w_gate'], gw['ffn2_w_up'], gw['ffn2_w_down'])
    (dwg2, dwu2, dwd2), _ = _ffn_bwd_w("ffn2_bwd_w", nb, daccb, ab, dgb, dub)
    dret, dssm, dwo = _outproj_bwd(dh2, ret, ssm, gw['w_out'])
    (du_seg, d_ssm_norm, d_glu_w, d_glu_b, d_ssm_d, dcr_s, dci_s, dbr_s, dbi_s, dar8, dai8) = _s5_bwd(
        _to_segments(dssm, seg_len), u_seg, yp, xr, xi, c0r, c0i, bsrt_b, bsit_b, csrt_b, csit_b,
        a8r, a8i, al8r, al8i, ssm_d, glu_full, ssm_glu_b, ssm_norm_w, jb)
    du = _from_segments(du_seg, seg_len)
    dq, dk, dv, dgate, d_ret_norm = _ret_bwd(dret, q, k, vv, gate, o, sprev, ret_norm_w, tables, cosf, sinf)
    dh1, d_mix_norm, dwin = _inproj_bwd(dh2, h1, mix_norm_w, n2, gw['w_in'], dq, dk, dv, dgate, du)
    late_parts = {
        'w_in': dwin, 'ssm_glu_w': d_glu_w.reshape(N_CHIP, sw // N_CHIP, sw).astype(BF16), 'w_out': dwo,
        'ffn2_w_gate': dwg2, 'ffn2_w_up': dwu2, 'ffn2_w_down': dwd2,
    }
    late_list = [late_parts[n] for n in BIG_LATE]
    (dh0, d_ffn1_norm, nb, daccb, ab, dgb, dub), late_recv = _ffn_bwd_act(
        "ffn1_bwd_act", dh1, h0, ffn1_norm_w, g1, u1, gw['ffn1_w_gate'], gw['ffn1_w_up'], gw['ffn1_w_down'],
        _alltoall_chips_plan(late_list), late_list)
    grad_x = dh0[CHUNK:][None]
    d_meta = dh0[CHUNK - N_META:CHUNK]

    d_c_re = _blockdiag_out(tr(dcr_s), groups, SSM_GROUP, SSM_STATE)
    d_c_im = -_blockdiag_out(tr(dci_s), groups, SSM_GROUP, SSM_STATE)
    d_bbr = jnp.transpose(_blockdiag_out(dbr_s, groups, SSM_GROUP, SSM_STATE), (0, 2, 1))
    d_bbi = jnp.transpose(_blockdiag_out(dbi_s, groups, SSM_GROUP, SSM_STATE), (0, 2, 1))
    d_ar = jnp.sum(dar8, axis=0).reshape(groups, SSM_STATE)
    d_ai = jnp.sum(dai8, axis=0).reshape(groups, SSM_STATE)
    small_parts = [loss_part[0:1, :], d_meta, d_ffn1_norm, d_mix_norm, d_ret_norm, d_ar, d_ai, d_bbr, d_bbi,
                   d_c_re, d_c_im, d_ssm_d, d_glu_b, d_ssm_norm, d_ffn2_norm, d_final]
    small_shapes = [a.shape for a in small_parts]
    packed = _pack(small_parts)
    early_recv, (all_parts,) = _ffn_bwd_w_scatter("ffn1_bwd_w", nb, daccb, ab, dgb, dub, chip,
                                                  _allgather_all_plan([packed]), [packed])
    received = dict(zip(BIG_LATE + BIG_EARLY, late_recv + early_recv))
    chip_sums = _sum_slots("sum_chips", [received[n] for n in BIG], BF16)
    sib_sums = _swap_sibling("swap_sibling", chip_sums)
    (loss_row, g_meta_full, g_ffn1_norm, g_mix_norm, g_ret_norm, g_ar, g_ai, g_bbr, g_bbi, g_c_re, g_c_im,
     g_ssm_d, g_glu_b, g_ssm_norm, g_ffn2_norm, g_final) = _unpack(_sum_slots("sum_small", [all_parts], F32)[0],
                                                                  small_shapes)
    g_lam_re, g_lam_im, g_log_dt, g_b_re, g_b_im = prep_vjp((g_ar, g_ai, g_bbr, g_bbi))
    loss = loss_row[0, 0]
    g_meta = lax.dynamic_slice(g_meta_full, (0, chip * (d // N_CHIP)), (N_META, d // N_CHIP))
    small_grads = {
        'meta_tokens': g_meta, 'ffn1_norm_w': g_ffn1_norm, 'mix_norm_w': g_mix_norm, 'ret_norm_w': g_ret_norm,
        'ssm_lambda_re': g_lam_re[None], 'ssm_lambda_im': g_lam_im[None], 'ssm_log_dt': g_log_dt[None],
        'ssm_b_re': g_b_re[None], 'ssm_b_im': g_b_im[None], 'ssm_c_re': g_c_re[None], 'ssm_c_im': g_c_im[None],
        'ssm_d': g_ssm_d, 'ssm_glu_b': g_glu_b, 'ssm_norm_w': g_ssm_norm, 'ffn2_norm_w': g_ffn2_norm,
        'final_norm_w': g_final.reshape(d),
    }

    grads, deltas, new_m, new_v = {}, {}, {}, {}
    g_pair = {n: [mine, sib] for n, mine, sib in zip(BIG, chip_sums, sib_sums)}
    view = lambda n, t: as_fd(t) if n in TRANSPOSED else t
    big_out = _adam("adam_big", [(view(n, w[n]), view(n, m[n]), view(n, v[n])) for n in BIG], [g_pair[n] for n in BIG])
    for n, outs in zip(BIG, big_out):
        grads[n], deltas[n], new_m[n], new_v[n] = [view(n, t) for t in outs]
    sm_shapes = [w[n].shape for n in SMALL]
    sm_out = _adam("adam_small", [(_pack([w[n] for n in SMALL]), _pack([m[n] for n in SMALL]),
                                  _pack([v[n] for n in SMALL]))],
                   [[_pack([small_grads[n].reshape(w[n].shape) for n in SMALL])]])[0]
    for dst, packed in zip((grads, deltas, new_m, new_v), sm_out):
        for n, t in zip(SMALL, _unpack(packed, sm_shapes)):
            dst[n] = t

    return (loss, grad_x, *[grads[n] for n in WEIGHT_NAMES], *[deltas[n] for n in WEIGHT_NAMES],
            *[new_m[n] for n in WEIGHT_NAMES], *[new_v[n] for n in WEIGHT_NAMES])
```

```python
import functools
import math

import jax
import jax.numpy as jnp
from jax import lax
from jax.experimental import pallas as pl
from jax.experimental.pallas import tpu as pltpu

N_META = 16
RET_HEADS = 4
HEAD_DIM = 128
SSM_GROUP = 16
SSM_STATE = 64
CHUNK = 128
ROPE_BASE = 10000.0
EPS = 1e-6
FFN_RES = 0.5
N_SEG = 8
N_SEC = 4
N_CHIP = 4
LANE = 128
FFN_CPS = 2
BWD_W_ROWS = 1664

ADAM_LR = 0.001
ADAM_B1 = 0.9
ADAM_B2 = 0.999
ADAM_EPS = 1e-08
ADAM_WD = 0.01
ADAM_STEP = 10

VMEM_LIMIT = 56 * 1024 * 1024

F32 = jnp.float32
BF16 = jnp.bfloat16
MESH = pl.DeviceIdType.MESH


def _dot(a, b):
    return jnp.dot(a, b, preferred_element_type=F32)


def _dot_nt(a, b):
    return lax.dot_general(a, b, (((1,), (1,)), ((), ())), preferred_element_type=F32)


def _dot_tn(a, b):
    return lax.dot_general(a, b, (((0,), (0,)), ((), ())), preferred_element_type=F32)


def _tile(n, target, mult=64):
    best = None
    t = mult
    while t <= min(n, target):
        if n % t == 0:
            best = t
        t += mult
    assert best is not None, (n, target)
    return best


def _params(sem, vmem=VMEM_LIMIT):
    return pltpu.CompilerParams(dimension_semantics=sem, vmem_limit_bytes=vmem)


def _rms_stats(xf):
    r = lax.rsqrt(jnp.mean(xf * xf, axis=-1, keepdims=True) + EPS)
    return xf * r, r


def _rms_bwd(dy, xh, r, w):
    dxh = dy * w
    return r * (dxh - xh * jnp.mean(dxh * xh, axis=-1, keepdims=True))


def _sigmoid(x):
    return 0.5 * jnp.tanh(0.5 * x) + 0.5


GELU_K0 = math.sqrt(2.0 / math.pi)
GELU_K1 = 0.044715


CHIP_MASKS = [(1, 0, 0), (0, 1, 0), (1, 1, 0)]
ALL_MASKS = [(0, 0, 1), (0, 1, 0), (0, 1, 1), (1, 0, 0), (1, 0, 1), (1, 1, 0), (1, 1, 1)]
SIB_MASKS = [(0, 0, 1)]
ANY_SPEC = pl.BlockSpec(memory_space=pl.ANY)
MULTI_SUM_STEPS = 4
MULTI_ADAM_STEPS = 8


class _Plan:
    def __init__(self, arrays, masks, n_slots, src_slotted, dst_slotted, local_copy, half=False, forward=False):
        self.shapes = [(a.shape, a.dtype) for a in arrays]
        self.n = len(arrays)
        self.masks = masks
        self.n_slots = n_slots
        self.src_slotted, self.dst_slotted, self.local_copy = src_slotted, dst_slotted, local_copy
        self.half, self.forward = half, forward
        self.n_cp = self.n * len(masks) * (len(CHIP_MASKS) if forward else 1)

    def out_shape(self):
        out = []
        for shp, dt in self.shapes:
            if self.dst_slotted and not self.src_slotted:
                shp = (self.n_slots,) + shp
            elif self.src_slotted and not self.dst_slotted:
                shp = shp[1:]
            out.append(jax.ShapeDtypeStruct(shp, dt))
        return tuple(out)

    def scratch(self):
        return [pltpu.SemaphoreType.DMA((self.n_cp,)), pltpu.SemaphoreType.DMA((self.n_cp,)),
                pltpu.SemaphoreType.DMA((self.n,))]

    def _slot(self, px, py, pc):
        if self.n_slots == 8:
            return 4 * px + 2 * py + pc
        if self.n_slots == 4:
            return 2 * px + py
        return pc

    def copies(self, ins, outs, sems):
        send_sems, recv_sems, loc_sems = sems
        x, y, c = lax.axis_index("x"), lax.axis_index("y"), lax.axis_index("c")
        me = self._slot(x, y, c)
        n_m = len(self.masks)
        cps = []
        for a in range(self.n):
            if self.forward:
                rows = self.shapes[a][0][-2] // 2
                mine = pl.ds(pl.multiple_of(c * rows, 8), rows)
                for j, (mx, my, _) in enumerate(CHIP_MASKS):
                    blk = outs[a].at[2 * (1 - x if mx else x) + (1 - y if my else y), mine]
                    k = a * len(CHIP_MASKS) + j
                    cps.append(pltpu.make_async_remote_copy(
                        src_ref=blk, dst_ref=blk, send_sem=send_sems.at[k], recv_sem=recv_sems.at[k],
                        device_id=(x, y, 1 - c), device_id_type=MESH))
                continue
            if self.local_copy:
                src = ins[a].at[me] if self.src_slotted else ins[a]
                cps.append(pltpu.make_async_copy(src, outs[a].at[me], loc_sems.at[a]))
            for mi, (mx, my, mc) in enumerate(self.masks):
                px = 1 - x if mx else x
                py = 1 - y if my else y
                pc = 1 - c if mc else c
                src = ins[a].at[self._slot(px, py, pc)] if self.src_slotted else ins[a]
                dst = outs[a].at[me] if self.dst_slotted else outs[a]
                if self.half:
                    rows = src.shape[-2] // 2
                    mine = pl.ds(pl.multiple_of(c * rows, 8), rows)
                    src, dst = src.at[mine], dst.at[mine]
                k = a * n_m + mi
                cps.append(pltpu.make_async_remote_copy(
                    src_ref=src, dst_ref=dst, send_sem=send_sems.at[k], recv_sem=recv_sems.at[k],
                    device_id=(px, py, pc), device_id_type=MESH))
        return cps


def _exchange(name, plan, arrays):
    n = plan.n

    def body(*refs):
        cps = plan.copies(refs[:n], refs[n:2 * n], refs[2 * n:])
        for cp in cps:
            cp.start()
        for cp in cps:
            cp.wait()

    outs = pl.pallas_call(
        body, name=name, out_shape=plan.out_shape(),
        in_specs=[ANY_SPEC] * n, out_specs=tuple([ANY_SPEC] * n), scratch_shapes=plan.scratch(),
        input_output_aliases={i: i for i in range(n)} if plan.forward else {},
    )(*arrays)
    return list(outs)


def _pcall(body, *, name, grid, in_specs, out_specs, out_shape, scratch_shapes, args, plan=None, plan_args=()):
    sem = ("arbitrary",) * len(grid)
    if plan is None:
        return pl.pallas_call(body, name=name, grid=grid, in_specs=in_specs, out_specs=out_specs,
                              out_shape=out_shape, scratch_shapes=scratch_shapes,
                              compiler_params=_params(sem))(*args), []
    n_in, n_out, n_scr, n_p = len(in_specs), len(out_specs), len(scratch_shapes), plan.n

    def wrapped(*refs):
        ins = refs[:n_in]
        p_ins = refs[n_in:n_in + n_p]
        o0 = n_in + n_p
        outs = refs[o0:o0 + n_out]
        p_outs = refs[o0 + n_out:o0 + n_out + n_p]
        s0 = o0 + n_out + n_p
        scr = refs[s0:s0 + n_scr]
        sems = refs[s0 + n_scr:]
        ids = [pl.program_id(i) for i in range(len(grid))]
        first = functools.reduce(jnp.logical_and, [i == 0 for i in ids])
        last = functools.reduce(jnp.logical_and, [i == g - 1 for i, g in zip(ids, grid)])

        @pl.when(first)
        def _():
            for cp in plan.copies(p_ins, p_outs, sems):
                cp.start()

        body(*ins, *outs, *scr)

        @pl.when(last)
        def _():
            for cp in plan.copies(p_ins, p_outs, sems):
                cp.wait()

    res = pl.pallas_call(
        wrapped, name=name, grid=grid,
        in_specs=list(in_specs) + [ANY_SPEC] * n_p,
        out_specs=tuple(out_specs) + (ANY_SPEC,) * n_p,
        out_shape=tuple(out_shape) + plan.out_shape(),
        scratch_shapes=list(scratch_shapes) + plan.scratch(),
        compiler_params=_params(sem),
    )(*args, *plan_args)
    return res[:n_out], list(res[n_out:])


def _allgather_chips_plan(arrays):
    return _Plan(arrays, CHIP_MASKS, 4, False, True, True, half=True)


def _gather_two_level(name, arrays):
    n = len(arrays)
    ici = _allgather_chips_plan(arrays)
    fwd = _Plan(ici.out_shape(), SIB_MASKS, 4, True, True, False, forward=True)
    n_m = len(CHIP_MASKS)

    def body(*refs):
        ins, outs, sems = refs[:n], refs[n:2 * n], refs[2 * n:]
        ici_cps = ici.copies(ins, outs, sems[:3])
        fwd_cps = fwd.copies(None, outs, sems[3:])
        for cp in ici_cps:
            cp.start()
        for a in range(n):
            for m in range(n_m):
                ici_cps[a * (n_m + 1) + 1 + m].wait_recv()
                fwd_cps[a * n_m + m].start()
        for a in range(n):
            ici_cps[a * (n_m + 1)].wait()
            for m in range(n_m):
                ici_cps[a * (n_m + 1) + 1 + m].wait_send()
        for cp in fwd_cps:
            cp.wait()

    return list(pl.pallas_call(
        body, name=name, out_shape=ici.out_shape(),
        in_specs=[ANY_SPEC] * n, out_specs=tuple([ANY_SPEC] * n), scratch_shapes=ici.scratch() + fwd.scratch(),
    )(*arrays))


def _forward_sibling(name, gathered):
    return _exchange(name, _Plan(gathered, SIB_MASKS, 4, True, True, False, forward=True), gathered)


def _alltoall_chips_plan(arrays):
    return _Plan(arrays, CHIP_MASKS, 4, True, True, True)


def _swap_sibling(name, arrays):
    return _exchange(name, _Plan(arrays, SIB_MASKS, 2, False, False, False), arrays)


def _allgather_all_plan(arrays):
    return _Plan(arrays, ALL_MASKS, 8, False, True, True)


def _sum_slots(name, arrs, out_dtype):
    s = arrs[0].shape[0]
    n = len(arrs)
    steps = arrs[0].shape[1] // _tile(arrs[0].shape[1], 512, 8) if n == 1 else MULTI_SUM_STEPS
    for a in arrs:
        assert a.shape[1] % (16 * steps) == 0 or n == 1, a.shape

    def body(*refs):
        for a_ref, o_ref in zip(refs[:n], refs[n:]):
            acc = a_ref[0].astype(F32)
            for i in range(1, s):
                acc = acc + a_ref[i].astype(F32)
            o_ref[...] = acc.astype(out_dtype)

    return list(pl.pallas_call(
        body, name=name, grid=(steps,),
        in_specs=[pl.BlockSpec((s, a.shape[1] // steps, a.shape[2]), lambda i: (0, i, 0)) for a in arrs],
        out_specs=tuple(pl.BlockSpec((a.shape[1] // steps, a.shape[2]), lambda i: (i, 0)) for a in arrs),
        out_shape=tuple(jax.ShapeDtypeStruct(a.shape[1:], out_dtype) for a in arrs),
        compiler_params=_params(("arbitrary",)),
    )(*arrs))


def _adam_math(w, g, m, v):
    m_new = ADAM_B1 * m + (1.0 - ADAM_B1) * g
    v_new = ADAM_B2 * v + (1.0 - ADAM_B2) * (g * g)
    m_hat = m_new / (1.0 - ADAM_B1 ** ADAM_STEP)
    v_hat = v_new / (1.0 - ADAM_B2 ** ADAM_STEP)
    delta = -ADAM_LR * (m_hat / (jnp.sqrt(v_hat) + ADAM_EPS) + ADAM_WD * w)
    return delta, m_new, v_new


def _adam(name, wmv, g_parts):
    n_w = len(wmv)
    n_g = len(g_parts[0])
    lead = wmv[0][0].ndim == 3
    at = (lambda ref: ref.at[0]) if lead else (lambda ref: ref)
    n_in = 3 + n_g
    rows0 = wmv[0][0].shape[-2]
    steps = rows0 // _tile(rows0, 256, 8) if n_w == 1 else MULTI_ADAM_STEPS
    for w, _, _ in wmv:
        assert w.shape[-2] % (8 * steps) == 0, w.shape

    def body(*refs):
        for j in range(n_w):
            ins = refs[j * n_in:(j + 1) * n_in]
            outs = refs[n_w * n_in + 4 * j:n_w * n_in + 4 * j + 4]
            w_ref, m_ref, v_ref = [at(t) for t in ins[:3]]
            g_out, d_out, m_out, v_out = [at(t) for t in outs]
            g = ins[3][...].astype(F32)
            for gr in ins[4:]:
                g = g + gr[...].astype(F32)
            delta, m_new, v_new = _adam_math(w_ref[...], g, m_ref[...], v_ref[...])
            g_out[...] = g
            d_out[...] = delta
            m_out[...] = m_new
            v_out[...] = v_new

    in_specs, out_specs, out_shape, args = [], [], [], []
    for (w, m, v), gp in zip(wmv, g_parts):
        r, c = w.shape[-2:]
        spec = pl.BlockSpec((r // steps, c), lambda i: (i, 0))
        wspec = pl.BlockSpec((1, r // steps, c), lambda i: (0, i, 0)) if lead else spec
        in_specs += [wspec] * 3 + [spec] * n_g
        out_specs += [wspec] * 4
        out_shape += [jax.ShapeDtypeStruct(w.shape, F32)] * 4
        args += [w, m, v, *gp]
    res = pl.pallas_call(
        body, name=name, grid=(steps,),
        in_specs=in_specs, out_specs=tuple(out_specs), out_shape=tuple(out_shape),
        compiler_params=_params(("arbitrary",)),
    )(*args)
    return [tuple(res[4 * j:4 * j + 4]) for j in range(n_w)]


SUB_ROWS = 32
FFN_BWD_ROWS = 416
FFN_FWD_CPS = 4
FFN_FWD_ROWS = 416
FFN_LOSS_ROWS = 416
RET_ROWS = 640
S5_STEPS = 104


def _tile_parts(tm, d, head, x):
    nsub = tm // SUB_ROWS
    off = head.shape[0] // SUB_ROWS
    specs = [pl.BlockSpec(head.shape, lambda i, k: (0, 0))] + [
        pl.BlockSpec((SUB_ROWS, d), lambda i, k, j=j: (jnp.maximum(i * nsub + j - off, 0), 0)) for j in range(nsub)]

    def assemble(i, part_refs, h_sc):
        head_ref, x_refs = part_refs[0], part_refs[1:]
        for j in range(nsub):
            rows = slice(j * SUB_ROWS, (j + 1) * SUB_ROWS)
            val = x_refs[j][...]
            if j < off:
                val = jnp.where(i == 0, head_ref[rows, :], val)
            h_sc[rows, :] = val

    return specs, [head] + [x] * nsub, assemble


def _h_source(body, h, tm, d):
    if not isinstance(h, tuple):
        return body, [pl.BlockSpec((tm, d), lambda i, k: (i, 0))], [h], []
    specs, args, assemble = _tile_parts(tm, d, *h)
    n_h = len(specs)

    def with_parts(*refs):
        h_sc = refs[-1]

        @pl.when(pl.program_id(1) == 0)
        def _():
            assemble(pl.program_id(0), refs[:n_h], h_sc)

        body(h_sc, *refs[n_h:-1])

    return with_parts, specs, args, [pltpu.VMEM((tm, d), F32)]


def _ffn_fwd(name, h, nw, wg, wu, wd, plan=None, plan_args=(), loss=None):
    lp, d = (h[0].shape[0] + h[1].shape[0], h[1].shape[1]) if isinstance(h, tuple) else h.shape
    nck, f, _ = wg.shape
    tm = _tile(lp, FFN_FWD_ROWS if loss is None else FFN_LOSS_ROWS, SUB_ROWS)
    cps = FFN_FWD_CPS
    last = nck // cps - 1
    n_t = 0
    if loss is not None:
        t_specs, t_args, t_assemble = _tile_parts(tm, d, jnp.zeros((lp - loss[1].shape[0], d), F32), loss[1])
        n_t = len(t_specs)

    def body(h_ref, nw_ref, wg_ref, wu_ref, wd_ref, *rest):
        if loss is not None:
            fw_ref, t_parts, rest = rest[0], rest[1:1 + n_t], rest[1 + n_t:]
            ho_ref, g_ref, u_ref, loss_ref, dfw_ref, n_sc, acc_sc, t_sc = rest
        else:
            ho_ref, g_ref, u_ref, n_sc, acc_sc = rest
        i = pl.program_id(0)
        k = pl.program_id(1)

        @pl.when(k == 0)
        def _():
            xh, _ = _rms_stats(h_ref[...])
            n_sc[...] = (xh * nw_ref[...]).astype(BF16)
            acc_sc[...] = jnp.zeros_like(acc_sc)

        n = n_sc[...]
        acc = acc_sc[...]
        for c in range(cps):
            g = _dot_nt(n, wg_ref[c])
            u = _dot_nt(n, wu_ref[c])
            g_ref[c] = g.astype(BF16)
            u_ref[c] = u.astype(BF16)
            a = (g * _sigmoid(g) * u).astype(BF16)
            acc = acc + _dot(a, wd_ref[c])
        acc_sc[...] = acc

        if loss is None:
            @pl.when(k == last)
            def _():
                ho_ref[...] = h_ref[...] + FFN_RES * acc_sc[...]
            return

        @pl.when(jnp.logical_and(i == 0, k == 0))
        def _():
            loss_ref[...] = jnp.zeros_like(loss_ref)
            dfw_ref[...] = jnp.zeros_like(dfw_ref)

        @pl.when(k == last)
        def _():
            t_assemble(i, t_parts, t_sc)
            xh, r = _rms_stats(h_ref[...] + FFN_RES * acc_sc[...])
            w = fw_ref[...]
            head_rows = lp - loss[1].shape[0]
            row = lax.broadcasted_iota(jnp.int32, (tm, d), 0) + i * tm
            err = jnp.where(row < head_rows, 0.0, xh * w - t_sc[...])
            loss_ref[...] += 0.5 * jnp.sum(err * err) / d
            dout = err * (1.0 / d)
            dfw_ref[...] += jnp.sum(dout * xh, axis=0, keepdims=True)
            ho_ref[...] = _rms_bwd(dout, xh, r, w)

    body, h_specs, h_args, h_scratch = _h_source(body, h, tm, d)
    vec = pl.BlockSpec((1, d), lambda i, k: (0, 0))
    w_fd = pl.BlockSpec((cps, f, d), lambda i, k: (k, 0, 0), **({'pipeline_mode': pl.Buffered(1)} if cps == nck else {}))
    hid = pl.BlockSpec((cps, tm, f), lambda i, k: (k, i, 0))
    hshape = jax.ShapeDtypeStruct((nck, lp, f), BF16)
    args, in_specs = (*h_args, nw, wg, wu, wd), h_specs + [vec, w_fd, w_fd, w_fd]
    out_specs = (pl.BlockSpec((tm, d), lambda i, k: (i, 0)), hid, hid)
    out_shape = (jax.ShapeDtypeStruct((lp, d), F32), hshape, hshape)
    scratch = [pltpu.VMEM((tm, d), BF16), pltpu.VMEM((tm, d), F32)]
    if loss is not None:
        args, in_specs = (*args, loss[0], *t_args), in_specs + [vec] + t_specs
        out_specs += (pl.BlockSpec((8, LANE), lambda i, k: (0, 0)), vec)
        out_shape += (jax.ShapeDtypeStruct((8, LANE), F32), jax.ShapeDtypeStruct((1, d), F32))
        scratch = scratch + [pltpu.VMEM((tm, d), F32)]
    return _pcall(
        body, name=name, grid=(lp // tm, nck // cps), plan=plan, plan_args=plan_args,
        args=args, in_specs=in_specs, out_specs=out_specs, out_shape=out_shape,
        scratch_shapes=scratch + h_scratch)


def _ffn_bwd_act(name, dh, h, nw, g, u, wg, wu, wd, plan=None, plan_args=()):
    lp, d = dh.shape
    nck, f, _ = wg.shape
    tm = _tile(lp, FFN_BWD_ROWS, SUB_ROWS)
    last = nck // FFN_CPS - 1

    def body(h_ref, dh_ref, nw_ref, g_ref, u_ref, wg_ref, wu_ref, wd_ref,
             dhi_ref, dnw_ref, n_ref, dacc_ref, a_ref, dg_ref, du_ref,
             xh_sc, r_sc, dn_sc):
        i = pl.program_id(0)
        k = pl.program_id(1)

        @pl.when(k == 0)
        def _():
            xh, r = _rms_stats(h_ref[...])
            xh_sc[...] = xh
            r_sc[...] = r
            n_ref[...] = (xh * nw_ref[...]).astype(BF16)
            dacc_ref[...] = (FFN_RES * dh_ref[...]).astype(BF16)
            dn_sc[...] = jnp.zeros_like(dn_sc)

        @pl.when(jnp.logical_and(i == 0, k == 0))
        def _():
            dnw_ref[...] = jnp.zeros_like(dnw_ref)

        dacc = dacc_ref[...]
        dn = dn_sc[...]
        for c in range(FFN_CPS):
            gv = g_ref[c].astype(F32)
            uv = u_ref[c].astype(F32)
            sg = _sigmoid(gv)
            sil = gv * sg
            da = _dot_nt(dacc, wd_ref[c])
            dgk = (da * uv * (sg * (1.0 + gv * (1.0 - sg)))).astype(BF16)
            duk = (da * sil).astype(BF16)
            a_ref[c] = (sil * uv).astype(BF16)
            dg_ref[c] = dgk
            du_ref[c] = duk
            dn = dn + _dot(dgk, wg_ref[c]) + _dot(duk, wu_ref[c])
        dn_sc[...] = dn

        @pl.when(k == last)
        def _():
            dnl = dn_sc[...]
            xh = xh_sc[...]
            dhi_ref[...] = dh_ref[...] + _rms_bwd(dnl, xh, r_sc[...], nw_ref[...])
            dnw_ref[...] += jnp.sum(dnl * xh, axis=0, keepdims=True)

    body, h_specs, h_args, h_scratch = _h_source(body, h, tm, d)
    row = pl.BlockSpec((tm, d), lambda i, k: (i, 0))
    vec = pl.BlockSpec((1, d), lambda i, k: (0, 0))
    hid = pl.BlockSpec((FFN_CPS, tm, f), lambda i, k: (k, i, 0))
    w_fd = pl.BlockSpec((FFN_CPS, f, d), lambda i, k: (k, 0, 0))
    rshape = jax.ShapeDtypeStruct((lp, d), BF16)
    hshape = jax.ShapeDtypeStruct((nck, lp, f), BF16)
    return _pcall(
        body, name=name, grid=(lp // tm, nck // FFN_CPS), plan=plan, plan_args=plan_args,
        args=(*h_args, dh, nw, g, u, wg, wu, wd),
        in_specs=h_specs + [row, vec, hid, hid, w_fd, w_fd, w_fd],
        out_specs=(row, vec, row, row, hid, hid, hid),
        out_shape=(jax.ShapeDtypeStruct((lp, d), F32), jax.ShapeDtypeStruct((1, d), F32),
                   rshape, rshape, hshape, hshape, hshape),
        scratch_shapes=[pltpu.VMEM((tm, d), F32), pltpu.VMEM((tm, 1), F32), pltpu.VMEM((tm, d), F32)] + h_scratch)


def _ffn_bwd_w(name, n, dacc, a, dg, du, plan=None, plan_args=()):
    lp, d = n.shape
    nck, _, f = a.shape
    tm = _tile(lp, BWD_W_ROWS)
    last = lp // tm - 1

    def body(n_ref, dacc_ref, a_ref, dg_ref, du_ref, dwg_ref, dwu_ref, dwd_ref, ag_sc, au_sc, ad_sc):
        i = pl.program_id(1)

        @pl.when(i == 0)
        def _():
            ag_sc[...] = jnp.zeros_like(ag_sc)
            au_sc[...] = jnp.zeros_like(au_sc)
            ad_sc[...] = jnp.zeros_like(ad_sc)

        nv = n_ref[...]
        ag_sc[...] += _dot_tn(dg_ref[0], nv)
        au_sc[...] += _dot_tn(du_ref[0], nv)
        ad_sc[...] += _dot_tn(a_ref[0], dacc_ref[...])

        @pl.when(i == last)
        def _():
            dwg_ref[0] = ag_sc[...].astype(BF16)
            dwu_ref[0] = au_sc[...].astype(BF16)
            dwd_ref[0] = ad_sc[...].astype(BF16)

    row = pl.BlockSpec((tm, d), lambda k, i: (i, 0))
    hid = pl.BlockSpec((1, tm, f), lambda k, i: (k, i, 0))
    w_fd = pl.BlockSpec((1, f, d), lambda k, i: (k, 0, 0))
    wshape = jax.ShapeDtypeStruct((nck, f, d), BF16)
    return _pcall(
        body, name=name, grid=(nck, lp // tm), plan=plan, plan_args=plan_args, args=(n, dacc, a, dg, du),
        in_specs=[row, row, hid, hid, hid], out_specs=(w_fd, w_fd, w_fd), out_shape=(wshape,) * 3,
        scratch_shapes=[pltpu.VMEM((f, d), F32)] * 3)


def _ffn_bwd_w_scatter(name, n, dacc, a, dg, du, chip, plan, plan_args):
    lp, d = n.shape
    nck, _, f = a.shape
    tm = _tile(lp, BWD_W_ROWS)
    last_i = lp // tm - 1
    n_w = 3
    n_p = plan.n

    def body(me_ref, n_ref, dacc_ref, a_ref, dg_ref, du_ref, *rest):
        p_ins = rest[:n_p]
        recv = rest[n_p:n_p + n_w]
        p_outs = rest[n_p + n_w:2 * n_p + n_w]
        acc = rest[2 * n_p + n_w:2 * n_p + 2 * n_w]
        stage, send_sems, recv_sems, loc_sems = rest[2 * n_p + 2 * n_w:2 * n_p + 2 * n_w + 4]
        p_sems = rest[2 * n_p + 2 * n_w + 4:]
        p = pl.program_id(0)
        i = pl.program_id(1)
        me = me_ref[0]
        c = lax.axis_index("c")

        def send(w, pos):
            kk = jnp.bitwise_xor(me, nck - 1 - pos)
            diff = jnp.bitwise_xor(kk, me)
            m = jnp.where(diff == 2, 0, jnp.where(diff == 1, 1, 2))
            return pltpu.make_async_remote_copy(
                src_ref=stage.at[lax.rem(pos, 2), w], dst_ref=recv[w].at[me],
                send_sem=send_sems.at[w * 3 + m], recv_sem=recv_sems.at[w * 3 + m],
                device_id=(lax.div(kk, 2), lax.rem(kk, 2), c), device_id_type=MESH)

        @pl.when(jnp.logical_and(p == 0, i == 0))
        def _():
            for cp in plan.copies(p_ins, p_outs, p_sems):
                cp.start()

        @pl.when(i == 0)
        def _():
            for t in acc:
                t[...] = jnp.zeros_like(t)

        nv = n_ref[...]
        acc[0][...] += _dot_tn(dg_ref[0], nv)
        acc[1][...] += _dot_tn(du_ref[0], nv)
        acc[2][...] += _dot_tn(a_ref[0], dacc_ref[...])

        @pl.when(jnp.logical_and(i == last_i, p >= 2))
        def _():
            for w in range(n_w):
                send(w, p - 2).wait_send()

        @pl.when(i == last_i)
        def _():
            for w in range(n_w):
                stage[lax.rem(p, 2), w] = acc[w][...].astype(BF16)

        @pl.when(jnp.logical_and(i == last_i, p < nck - 1))
        def _():
            for w in range(n_w):
                send(w, p).start()

        @pl.when(jnp.logical_and(i == last_i, p == nck - 1))
        def _():
            own = [pltpu.make_async_copy(stage.at[(nck - 1) % 2, w], recv[w].at[me], loc_sems.at[w])
                   for w in range(n_w)]
            for cp in own:
                cp.start()
            for w in range(n_w):
                send(w, nck - 2).wait_send()
            for cp in own:
                cp.wait()
            for w in range(n_w):
                for m in range(3):
                    pltpu.make_async_remote_copy(
                        src_ref=stage.at[0, w], dst_ref=recv[w].at[me],
                        send_sem=send_sems.at[w * 3 + m], recv_sem=recv_sems.at[w * 3 + m],
                        device_id=(0, 0, c), device_id_type=MESH).wait_recv()
            for cp in plan.copies(p_ins, p_outs, p_sems):
                cp.wait()

    chunk = lambda k, me_ref: jnp.bitwise_xor(me_ref[0], nck - 1 - k)
    row = pl.BlockSpec((tm, d), lambda k, i, me_ref: (i, 0))
    hid = pl.BlockSpec((1, tm, f), lambda k, i, me_ref: (chunk(k, me_ref), i, 0))
    wshape = jax.ShapeDtypeStruct((nck, f, d), BF16)
    res = pl.pallas_call(
        body, name=name,
        grid_spec=pltpu.PrefetchScalarGridSpec(
            num_scalar_prefetch=1, grid=(nck, lp // tm),
            in_specs=[row, row, hid, hid, hid] + [ANY_SPEC] * n_p,
            out_specs=(ANY_SPEC,) * (n_w + n_p),
            scratch_shapes=[pltpu.VMEM((f, d), F32)] * n_w + [
                pltpu.VMEM((2, n_w, f, d), BF16), pltpu.SemaphoreType.DMA((n_w * 3,)),
                pltpu.SemaphoreType.DMA((n_w * 3,)), pltpu.SemaphoreType.DMA((n_w,))] + plan.scratch()),
        out_shape=(wshape,) * n_w + plan.out_shape(),
        compiler_params=_params(("arbitrary", "arbitrary")),
    )(chip.reshape(1).astype(jnp.int32), n, dacc, a, dg, du, *plan_args)
    return list(res[:n_w]), list(res[n_w:])


def _inproj_fwd(h, nw, w_in, cosf, sinf, rw):
    lp, d = h.shape
    nck, _, ps = w_in.shape
    proj = nck * ps
    sw = proj - 4 * rw
    tm = _tile(lp, 640)
    scale = HEAD_DIM ** -0.5
    heads = rw // HEAD_DIM

    def body(h_ref, nw_ref, w_ref, cos_ref, sin_ref, n_ref, q_ref, k_ref, v_ref, g_ref, u_ref, p_sc):
        xh, _ = _rms_stats(h_ref[...])
        n = (xh * nw_ref[...]).astype(BF16)
        n_ref[...] = n
        for c in range(nck):
            p_sc[:, c * ps:(c + 1) * ps] = _dot(n, w_ref[c])
        cs = cos_ref[...]
        sn = sin_ref[...]
        for hh in range(heads):
            lo = hh * HEAD_DIM
            qh = p_sc[:, lo:lo + HEAD_DIM]
            q_ref[:, lo:lo + HEAD_DIM] = (qh * cs + pltpu.roll(qh, HEAD_DIM // 2, 1) * sn).astype(BF16)
            kh = p_sc[:, rw + lo:rw + lo + HEAD_DIM]
            k_ref[:, lo:lo + HEAD_DIM] = ((kh * cs + pltpu.roll(kh, HEAD_DIM // 2, 1) * sn) * scale).astype(BF16)
        v_ref[...] = p_sc[:, 2 * rw:3 * rw].astype(BF16)
        g_ref[...] = p_sc[:, 3 * rw:4 * rw]
        u_ref[...] = p_sc[:, 4 * rw:]

    row = lambda w: pl.BlockSpec((tm, w), lambda i: (i, 0))
    return pl.pallas_call(
        body, name="inproj_fwd", grid=(lp // tm,),
        in_specs=[row(d), pl.BlockSpec((1, d), lambda i: (0, 0)),
                  pl.BlockSpec((nck, d, ps), lambda i: (0, 0, 0)), row(HEAD_DIM), row(HEAD_DIM)],
        out_specs=(row(d), row(rw), row(rw), row(rw), row(rw), row(sw)),
        out_shape=(jax.ShapeDtypeStruct((lp, d), BF16),
                   jax.ShapeDtypeStruct((lp, rw), BF16),
                   jax.ShapeDtypeStruct((lp, rw), BF16),
                   jax.ShapeDtypeStruct((lp, rw), BF16),
                   jax.ShapeDtypeStruct((lp, rw), F32),
                   jax.ShapeDtypeStruct((lp, sw), F32)),
        scratch_shapes=[pltpu.VMEM((tm, proj), F32)],
        compiler_params=_params(("arbitrary",)),
    )(h, nw, w_in, cosf, sinf)


def _inproj_bwd(dh, h, nw, n, w_in, dq, dk, dv, dg, du):
    lp, d = h.shape
    nck, _, ps = w_in.shape
    rw = dq.shape[1]
    sw = du.shape[1]
    proj = nck * ps
    tm = _tile(lp, 640)
    last = lp // tm - 1

    def gather_dproj(p_sc, dq_ref, dk_ref, dv_ref, dg_ref, du_ref):
        p_sc[:, 0:rw] = dq_ref[...]
        p_sc[:, rw:2 * rw] = dk_ref[...]
        p_sc[:, 2 * rw:3 * rw] = dv_ref[...]
        p_sc[:, 3 * rw:4 * rw] = dg_ref[...]
        p_sc[:, 4 * rw:] = du_ref[...]

    def act_body(dh_ref, h_ref, nw_ref, w_ref, dq_ref, dk_ref, dv_ref, dg_ref, du_ref, dhi_ref, dnw_ref, p_sc):
        i = pl.program_id(0)

        @pl.when(i == 0)
        def _():
            dnw_ref[...] = jnp.zeros_like(dnw_ref)

        gather_dproj(p_sc, dq_ref, dk_ref, dv_ref, dg_ref, du_ref)
        dn = jnp.zeros((tm, d), F32)
        for c in range(nck):
            dn = dn + _dot_nt(p_sc[:, c * ps:(c + 1) * ps], w_ref[c])
        xh, r = _rms_stats(h_ref[...])
        dhi_ref[...] = dh_ref[...] + _rms_bwd(dn, xh, r, nw_ref[...])
        dnw_ref[...] += jnp.sum(dn * xh, axis=0, keepdims=True)

    def w_body(n_ref, dq_ref, dk_ref, dv_ref, dg_ref, du_ref, dw_ref, p_sc, acc_sc):
        i = pl.program_id(0)

        @pl.when(i == 0)
        def _():
            acc_sc[...] = jnp.zeros_like(acc_sc)

        gather_dproj(p_sc, dq_ref, dk_ref, dv_ref, dg_ref, du_ref)
        nv = n_ref[...]
        for c in range(nck):
            acc_sc[c] += _dot_tn(nv, p_sc[:, c * ps:(c + 1) * ps])

        @pl.when(i == last)
        def _():
            dw_ref[...] = acc_sc[...].astype(BF16)

    row = lambda w: pl.BlockSpec((tm, w), lambda i: (i, 0))
    vec = pl.BlockSpec((1, d), lambda i: (0, 0))
    wsp = pl.BlockSpec((nck, d, ps), lambda i: (0, 0, 0))
    dproj_specs = [row(rw), row(rw), row(rw), row(rw), row(sw)]
    dhi, dnw = pl.pallas_call(
        act_body, name="inproj_bwd_act", grid=(lp // tm,),
        in_specs=[row(d), row(d), vec, wsp] + dproj_specs,
        out_specs=(row(d), vec),
        out_shape=(jax.ShapeDtypeStruct((lp, d), F32), jax.ShapeDtypeStruct((1, d), F32)),
        scratch_shapes=[pltpu.VMEM((tm, proj), BF16)],
        compiler_params=_params(("arbitrary",)),
    )(dh, h, nw, w_in, dq, dk, dv, dg, du)
    dw = pl.pallas_call(
        w_body, name="inproj_bwd_w", grid=(lp // tm,),
        in_specs=[row(d)] + dproj_specs,
        out_specs=wsp, out_shape=jax.ShapeDtypeStruct((nck, d, ps), BF16),
        scratch_shapes=[pltpu.VMEM((tm, proj), BF16), pltpu.VMEM((nck, d, ps), F32)],
        compiler_params=_params(("arbitrary",)),
    )(n, dq, dk, dv, dg, du)
    return dhi, dnw, dw


def _retention_tables(rc):
    h = jnp.arange(RET_HEADS, dtype=F32)
    log_g = jnp.log(1.0 - 2.0 ** (-5.0 - h))
    i = jnp.arange(rc)
    diff = i[:, None] - i[None, :]
    dec = jnp.where(diff[None] >= 0,
                    jnp.exp(log_g[:, None, None] * jnp.maximum(diff, 0)[None].astype(F32)), 0.0)
    pos = jnp.arange(rc, dtype=F32)
    wq = jnp.exp(log_g[:, None] * (pos + 1.0)[None])
    wk = jnp.exp(log_g[:, None] * (rc - 1 - pos)[None])
    gch = jnp.exp(log_g * rc)
    ones = jnp.ones((1, 1, HEAD_DIM), F32)
    return (dec, wq[:, :, None] * ones, wk[:, :, None] * ones,
            gch[:, None, None] * jnp.ones((1, 8, HEAD_DIM), F32))


def _head_norm(o):
    mu = jnp.mean(o, axis=-1, keepdims=True)
    oc = o - mu
    r = lax.rsqrt(jnp.mean(oc * oc, axis=-1, keepdims=True) + EPS)
    return oc * r, r


def _ret_fwd(q, k, v, g, rnw, tables):
    lp, rw = q.shape
    heads = rw // HEAD_DIM
    rc = tables[0].shape[1]
    nch = lp // rc
    dec, wq, wk, gch = tables

    def body(q_ref, k_ref, v_ref, g_ref, w_ref, dec_ref, wq_ref, wk_ref, gch_ref,
             o_ref, ret_ref, sp_ref, s_sc):
        n = pl.program_id(0)

        @pl.when(n == 0)
        def _():
            s_sc[...] = jnp.zeros_like(s_sc)

        cols = [slice(hh * HEAD_DIM, (hh + 1) * HEAD_DIM) for hh in range(heads)]
        s_ins = [s_sc[hh] for hh in range(heads)]
        outs = []
        for hh, cs in enumerate(cols):
            qv, kv, vv = q_ref[:, cs], k_ref[:, cs], v_ref[:, cs]
            s_in = s_ins[hh]
            a = _dot_nt(qv, kv) * dec_ref[hh]
            qw = (qv.astype(F32) * wq_ref[hh]).astype(BF16)
            kw = (kv.astype(F32) * wk_ref[hh]).astype(BF16)
            o = _dot(a.astype(BF16), vv) + _dot(qw, s_in.astype(BF16))
            s_new = gch_ref[hh, 0:1, :] * s_in + _dot_tn(kw, vv)
            xh, _ = _head_norm(o)
            gv = g_ref[:, cs]
            outs.append((o, s_new, (gv * _sigmoid(gv) * (xh * w_ref[:, cs])).astype(BF16)))
        for hh, cs in enumerate(cols):
            o, s_new, ret = outs[hh]
            sp_ref[hh, 0] = s_ins[hh]
            s_sc[hh] = s_new
            o_ref[:, cs] = o
            ret_ref[:, cs] = ret

    blk = pl.BlockSpec((rc, rw), lambda n: (n, 0))
    tab = pl.BlockSpec((heads, rc, HEAD_DIM), lambda n: (0, 0, 0))
    dtab = pl.BlockSpec((heads, rc, rc), lambda n: (0, 0, 0))
    return pl.pallas_call(
        body, name="retention_fwd", grid=(nch,),
        in_specs=[blk, blk, blk, blk, pl.BlockSpec((1, rw), lambda n: (0, 0)),
                  dtab, tab, tab, pl.BlockSpec((heads, 8, HEAD_DIM), lambda n: (0, 0, 0))],
        out_specs=(blk, blk, pl.BlockSpec((heads, 1, HEAD_DIM, HEAD_DIM), lambda n: (0, n, 0, 0))),
        out_shape=(jax.ShapeDtypeStruct((lp, rw), F32),
                   jax.ShapeDtypeStruct((lp, rw), BF16),
                   jax.ShapeDtypeStruct((heads, nch, HEAD_DIM, HEAD_DIM), F32)),
        scratch_shapes=[pltpu.VMEM((heads, HEAD_DIM, HEAD_DIM), F32)],
        compiler_params=_params(("arbitrary",)),
    )(q, k, v, g, rnw, dec, wq, wk, gch)


def _ret_bwd(dret, q, k, v, g, o, sprev, rnw, tables, cosf, sinf):
    lp, rw = q.shape
    heads = rw // HEAD_DIM
    rc = tables[0].shape[1]
    nch = lp // rc
    dec, wq, wk, gch = tables
    scale = HEAD_DIM ** -0.5
    half = HEAD_DIM // 2

    def body(dret_ref, q_ref, k_ref, v_ref, g_ref, o_ref, sp_ref, w_ref, dec_ref, wq_ref, wk_ref, gch_ref,
             cos_ref, sin_ref, dq_ref, dk_ref, dv_ref, dg_ref, dw_ref, ds_sc):
        n = pl.program_id(0)

        @pl.when(n == 0)
        def _():
            ds_sc[...] = jnp.zeros_like(ds_sc)
            dw_ref[...] = jnp.zeros_like(dw_ref)

        cosv = cos_ref[...]
        sinv = sin_ref[...]
        cols = [slice(hh * HEAD_DIM, (hh + 1) * HEAD_DIM) for hh in range(heads)]
        ds_ins = [ds_sc[hh] for hh in range(heads)]
        dw_ins = [dw_ref[:, cs] for cs in cols]
        outs = []
        for hh, cs in enumerate(cols):
            qv, kv, vv = q_ref[:, cs], k_ref[:, cs], v_ref[:, cs]
            gv = g_ref[:, cs]
            dr = dret_ref[:, cs]
            w = w_ref[:, cs]
            sg = _sigmoid(gv)
            sil = gv * sg
            xh, r = _head_norm(o_ref[:, cs])
            dgate = (dr * (xh * w) * (sg * (1.0 + gv * (1.0 - sg)))).astype(BF16)
            dyw = dr * sil
            dw_new = dw_ins[hh] + jnp.sum(dyw * xh, axis=0, keepdims=True)
            dxh = dyw * w
            do = r * (dxh - jnp.mean(dxh, axis=-1, keepdims=True)
                      - xh * jnp.mean(dxh * xh, axis=-1, keepdims=True))
            dob = do.astype(BF16)
            dmask = dec_ref[hh]
            wqv = wq_ref[hh]
            wkv = wk_ref[hh]
            a = (_dot_nt(qv, kv) * dmask).astype(BF16)
            da = (_dot_nt(dob, vv) * dmask).astype(BF16)
            qw = (qv.astype(F32) * wqv).astype(BF16)
            kw = (kv.astype(F32) * wkv).astype(BF16)
            s_in = sp_ref[hh, 0].astype(BF16)
            ds = ds_ins[hh]
            dsb = ds.astype(BF16)
            dq = _dot(da, kv) + _dot_nt(dob, s_in) * wqv
            dk = _dot_tn(da, qv) + _dot_nt(vv, dsb) * wkv
            dv = _dot_tn(a, dob) + _dot(kw, dsb)
            ds_new = gch_ref[hh, 0:1, :] * ds + _dot_tn(qw, dob)
            outs.append((dgate, dw_new, ds_new,
                         (dq * cosv + pltpu.roll(dq * sinv, half, 1)).astype(BF16),
                         ((dk * cosv + pltpu.roll(dk * sinv, half, 1)) * scale).astype(BF16),
                         dv.astype(BF16)))
        for hh, cs in enumerate(cols):
            dgate, dw_new, ds_new, dqv, dkv, dvv = outs[hh]
            dg_ref[:, cs] = dgate
            dw_ref[:, cs] = dw_new
            ds_sc[hh] = ds_new
            dq_ref[:, cs] = dqv
            dk_ref[:, cs] = dkv
            dv_ref[:, cs] = dvv

    blk = pl.BlockSpec((rc, rw), lambda n: (nch - 1 - n, 0))
    tab = pl.BlockSpec((heads, rc, HEAD_DIM), lambda n: (0, 0, 0))
    dtab = pl.BlockSpec((heads, rc, rc), lambda n: (0, 0, 0))
    wsp = pl.BlockSpec((1, rw), lambda n: (0, 0))
    pos = pl.BlockSpec((rc, HEAD_DIM), lambda n: (nch - 1 - n, 0))
    bshape = jax.ShapeDtypeStruct((lp, rw), BF16)
    return pl.pallas_call(
        body, name="retention_bwd", grid=(nch,),
        in_specs=[blk, blk, blk, blk, blk, blk,
                  pl.BlockSpec((heads, 1, HEAD_DIM, HEAD_DIM), lambda n: (0, nch - 1 - n, 0, 0)),
                  wsp, dtab, tab, tab, pl.BlockSpec((heads, 8, HEAD_DIM), lambda n: (0, 0, 0)), pos, pos],
        out_specs=(blk, blk, blk, blk, wsp),
        out_shape=(bshape, bshape, bshape, bshape, jax.ShapeDtypeStruct((1, rw), F32)),
        scratch_shapes=[pltpu.VMEM((heads, HEAD_DIM, HEAD_DIM), F32)],
        compiler_params=_params(("arbitrary",)),
    )(dret, q, k, v, g, o, sprev, rnw, dec, wq, wk, gch, cosf, sinf)


SCAN_CW = 512


def _s5_prepare(lam_re, lam_im, log_dt, b_re, b_im):
    dt = jnp.exp(log_dt)[:, None]
    er = jnp.exp(lam_re * dt)
    ar = er * jnp.cos(lam_im * dt)
    ai = er * jnp.sin(lam_im * dt)
    den = lam_re * lam_re + lam_im * lam_im
    fr = ((ar - 1.0) * lam_re + ai * lam_im) / den
    fi = (ai * lam_re - (ar - 1.0) * lam_im) / den
    bbr = fr[..., None] * b_re - fi[..., None] * b_im
    bbi = fr[..., None] * b_im + fi[..., None] * b_re
    return ar, ai, bbr, bbi


def _blockdiag_in(t):
    g, p, n = t.shape
    gs = g // N_SEC
    t = t.reshape(N_SEC, gs, p, n)
    eye = jnp.eye(gs, dtype=t.dtype)
    return jnp.einsum("sgpn,gh->sgphn", t, eye).reshape(N_SEC, gs * p, gs * n)


def _blockdiag_out(m, g, p, n):
    gs = g // N_SEC
    m = m.reshape(N_SEC, gs, p, gs, n)
    eye = jnp.eye(gs, dtype=m.dtype)
    return jnp.einsum("sgphn,gh->sgpn", m, eye).reshape(g, p, n)


def _scan_step(xr_ref, xi_ref, r0, prev, ar_ref, ai_ref, conj, ncols):
    new = []
    for cc in range(ncols // SCAN_CW):
        cs = pl.ds(cc * SCAN_CW, SCAN_CW)
        pr, pi = prev[cc]
        ar = ar_ref[:, cs]
        ai = ai_ref[:, cs]
        if conj:
            nr = ar * pr + ai * pi
            ni = ar * pi - ai * pr
        else:
            nr = ar * pr - ai * pi
            ni = ar * pi + ai * pr
        xr = xr_ref[pl.ds(r0, 8), cs] + nr
        xi = xi_ref[pl.ds(r0, 8), cs] + ni
        xr_ref[pl.ds(r0, 8), cs] = xr
        xi_ref[pl.ds(r0, 8), cs] = xi
        new.append((xr, xi))
    return new


def _scan_chunks(ncols):
    return [pl.ds(cc * SCAN_CW, SCAN_CW) for cc in range(ncols // SCAN_CW)]


def _flat(pairs):
    return tuple(t for p in pairs for t in p)


def _pairs(flat):
    return [(flat[2 * k], flat[2 * k + 1]) for k in range(len(flat) // 2)]


def _shift_rows(z, down):
    row = lax.broadcasted_iota(jnp.int32, z.shape, 0)
    if down:
        return jnp.where(row == 0, 0.0, pltpu.roll(z, 1, 0))
    return jnp.where(row == N_SEG - 1, 0.0, pltpu.roll(z, N_SEG - 1, 0))


def _s5_fwd(u, bsr, bsi, csr, csi, a8r, a8i, al8r, al8i, d, gluw, glub, nw, jb):
    lp, sw = u.shape
    ns = a8r.shape[1]
    rows = N_SEG * jb
    nblk = lp // rows
    secw = sw // N_SEC
    secn = ns // N_SEC

    def local_scan(u_ref, bsr_ref, bsi_ref, ar_ref, ai_ref, xr_ref, xi_ref, pr_sc, pi_sc):
        for s in range(N_SEC):
            ub = u_ref[:, s * secw:(s + 1) * secw].astype(BF16)
            xr_ref[:, s * secn:(s + 1) * secn] = _dot(ub, bsr_ref[s])
            xi_ref[:, s * secn:(s + 1) * secn] = _dot(ub, bsi_ref[s])
        prev = [(pr_sc[:, cs], pi_sc[:, cs]) for cs in _scan_chunks(ns)]
        prev = _scan_step(xr_ref, xi_ref, 0, prev, ar_ref, ai_ref, False, ns)

        def step(j, carry):
            r0 = pl.multiple_of(j * 8, 8)
            return _flat(_scan_step(xr_ref, xi_ref, r0, _pairs(carry), ar_ref, ai_ref, False, ns))

        last = _pairs(lax.fori_loop(1, jb, step, _flat(prev)))
        for cs, (vr, vi) in zip(_scan_chunks(ns), last):
            pr_sc[:, cs] = vr
            pi_sc[:, cs] = vi

    def carry_body(u_ref, bsr_ref, bsi_ref, ar_ref, ai_ref, alr_ref, ali_ref, cr_ref, ci_ref,
                   xr_sc, xi_sc, pr_sc, pi_sc):
        b = pl.program_id(0)

        @pl.when(b == 0)
        def _():
            pr_sc[...] = jnp.zeros_like(pr_sc)
            pi_sc[...] = jnp.zeros_like(pi_sc)

        local_scan(u_ref, bsr_ref, bsi_ref, ar_ref, ai_ref, xr_sc, xi_sc, pr_sc, pi_sc)

        @pl.when(b == nblk - 1)
        def _():
            er = _shift_rows(pr_sc[...], True)
            ei = _shift_rows(pi_sc[...], True)
            alr, ali = alr_ref[...], ali_ref[...]
            cr, ci = er, ei
            for _ in range(N_SEG - 2):
                sr = _shift_rows(cr, True)
                si = _shift_rows(ci, True)
                cr = er + alr * sr - ali * si
                ci = ei + alr * si + ali * sr
            cr_ref[...] = cr
            ci_ref[...] = ci

    ublk = pl.BlockSpec((rows, sw), lambda b: (b, 0))
    bspec = pl.BlockSpec((N_SEC, secw, secn), lambda b: (0, 0, 0))
    cspec = pl.BlockSpec((N_SEC, secn, secw), lambda b: (0, 0, 0))
    s8 = pl.BlockSpec((N_SEG, ns), lambda b: (0, 0))
    vec = pl.BlockSpec((1, sw), lambda b: (0, 0))
    s8shape = jax.ShapeDtypeStruct((N_SEG, ns), F32)
    c0r, c0i = pl.pallas_call(
        carry_body, name="s5_fwd_carry", grid=(nblk,),
        in_specs=[ublk, bspec, bspec, s8, s8, s8, s8],
        out_specs=(s8, s8), out_shape=(s8shape, s8shape),
        scratch_shapes=[pltpu.VMEM((rows, ns), F32), pltpu.VMEM((rows, ns), F32),
                        pltpu.VMEM((N_SEG, ns), F32), pltpu.VMEM((N_SEG, ns), F32)],
        compiler_params=_params(("arbitrary",)),
    )(u, bsr, bsi, a8r, a8i, al8r, al8i)

    def main_body(u_ref, bsr_ref, bsi_ref, csr_ref, csi_ref, ar_ref, ai_ref, c0r_ref, c0i_ref,
                  d_ref, gw_ref, gb_ref, nw_ref, xr_ref, xi_ref, yp_ref, out_ref, pr_sc, pi_sc):
        b = pl.program_id(0)

        @pl.when(b == 0)
        def _():
            pr_sc[...] = c0r_ref[...]
            pi_sc[...] = c0i_ref[...]

        local_scan(u_ref, bsr_ref, bsi_ref, ar_ref, ai_ref, xr_ref, xi_ref, pr_sc, pi_sc)
        for s in range(N_SEC):
            xs = pl.ds(s * secn, secn)
            us = pl.ds(s * secw, secw)
            y = _dot(xr_ref[:, xs].astype(BF16), csr_ref[s]) + _dot(xi_ref[:, xs].astype(BF16), csi_ref[s])
            yp_ref[:, us] = y + d_ref[:, us] * u_ref[:, us]
        yp = yp_ref[...]
        t = jnp.tanh(GELU_K0 * (yp + GELU_K1 * yp * yp * yp))
        y1 = 0.5 * yp * (1.0 + t)
        z = _dot(y1.astype(BF16), gw_ref[...]) + gb_ref[...]
        y2 = y1 * _sigmoid(z)
        xh, _ = _rms_stats(y2)
        out_ref[...] = (xh * nw_ref[...]).astype(BF16)

    xblk = pl.BlockSpec((rows, ns), lambda b: (b, 0))
    xr, xi, yp, out = pl.pallas_call(
        main_body, name="s5_fwd", grid=(nblk,),
        in_specs=[ublk, bspec, bspec, cspec, cspec, s8, s8, s8, s8, vec,
                  pl.BlockSpec((sw, sw), lambda b: (0, 0)), vec, vec],
        out_specs=(xblk, xblk, ublk, ublk),
        out_shape=(jax.ShapeDtypeStruct((lp, ns), F32), jax.ShapeDtypeStruct((lp, ns), F32),
                   jax.ShapeDtypeStruct((lp, sw), F32), jax.ShapeDtypeStruct((lp, sw), BF16)),
        scratch_shapes=[pltpu.VMEM((N_SEG, ns), F32), pltpu.VMEM((N_SEG, ns), F32)],
        compiler_params=_params(("arbitrary",)),
    )(u, bsr, bsi, csr, csi, a8r, a8i, c0r, c0i, d, gluw, glub, nw)
    return xr, xi, c0r, c0i, yp, out


def _s5_bwd(dout, u, yp, xr, xi, c0r, c0i, bsrt, bsit, csrt, csit, a8r, a8i, al8r, al8i, d, gluw, glub, nw, jb):
    lp, sw = u.shape
    ns = a8r.shape[1]
    rows = N_SEG * jb
    nblk = lp // rows
    secw = sw // N_SEC
    secn = ns // N_SEC

    def rowwise_bwd(dout_ref, yp_ref, gw_ref, gb_ref, nw_ref):
        ypv = yp_ref[...]
        t = jnp.tanh(GELU_K0 * (ypv + GELU_K1 * ypv * ypv * ypv))
        y1 = 0.5 * ypv * (1.0 + t)
        dgelu = 0.5 * (1.0 + t) + 0.5 * ypv * (1.0 - t * t) * GELU_K0 * (1.0 + 3.0 * GELU_K1 * ypv * ypv)
        gw = gw_ref[...]
        y1b = y1.astype(BF16)
        sg = _sigmoid(_dot(y1b, gw) + gb_ref[...])
        xh, r = _rms_stats(y1 * sg)
        dov = dout_ref[...]
        dy2 = _rms_bwd(dov, xh, r, nw_ref[...])
        dz = dy2 * y1 * sg * (1.0 - sg)
        dzb = dz.astype(BF16)
        dy1 = dy2 * sg + _dot_nt(dzb, gw)
        return dy1 * dgelu, dov * xh, y1b, dzb, dz

    def lam_scan(dyp_of, csrt_ref, csit_ref, ar_ref, ai_ref, lr_sc, li_sc, nr_sc, ni_sc, extra):
        for s in range(N_SEC):
            db = dyp_of(s)
            lr_sc[:, s * secn:(s + 1) * secn] = _dot(db, csrt_ref[s])
            li_sc[:, s * secn:(s + 1) * secn] = _dot(db, csit_ref[s])
        top = rows - 8
        prev = [(nr_sc[:, cs], ni_sc[:, cs]) for cs in _scan_chunks(ns)]
        prev = _scan_step(lr_sc, li_sc, top, prev, ar_ref, ai_ref, True, ns)
        extra(top, pl.ds(top - 8, 8))

        def step(jj, carry):
            r0 = pl.multiple_of((jb - 1 - jj) * 8, 8)
            rp = pl.multiple_of((jb - 2 - jj) * 8, 8)
            new = _scan_step(lr_sc, li_sc, r0, _pairs(carry), ar_ref, ai_ref, True, ns)
            extra(r0, pl.ds(rp, 8))
            return _flat(new)

        prev = _pairs(lax.fori_loop(1, jb - 1, step, _flat(prev)))
        last = _scan_step(lr_sc, li_sc, 0, prev, ar_ref, ai_ref, True, ns)
        extra(0, None)
        for cs, (vr, vi) in zip(_scan_chunks(ns), last):
            nr_sc[:, cs] = vr
            ni_sc[:, cs] = vi

    def carry_body(dout_ref, yp_ref, u_ref, gw_ref, gb_ref, nw_ref, csrt_ref, csit_ref, ar_ref, ai_ref,
                   alr_ref, ali_ref, cr_ref, ci_ref, dyp_ref, dnw_ref, dgw_ref, dgb_ref, dd_ref,
                   lr_sc, li_sc, nr_sc, ni_sc):
        b = pl.program_id(0)

        @pl.when(b == 0)
        def _():
            nr_sc[...] = jnp.zeros_like(nr_sc)
            ni_sc[...] = jnp.zeros_like(ni_sc)
            for ref in (dnw_ref, dgw_ref, dgb_ref, dd_ref):
                ref[...] = jnp.zeros_like(ref)

        dyp, dnw_rows, y1b, dzb, dz = rowwise_bwd(dout_ref, yp_ref, gw_ref, gb_ref, nw_ref)
        dnw_ref[...] += jnp.sum(dnw_rows, axis=0, keepdims=True)
        dgw_ref[...] += _dot_tn(y1b, dzb)
        dgb_ref[...] += jnp.sum(dz, axis=0, keepdims=True)
        dd_ref[...] += jnp.sum(dyp * u_ref[...], axis=0, keepdims=True)
        dyp_ref[...] = dyp.astype(BF16)
        lam_scan(lambda s: dyp_ref[:, s * secw:(s + 1) * secw], csrt_ref, csit_ref, ar_ref, ai_ref,
                 lr_sc, li_sc, nr_sc, ni_sc, lambda r0, prev_rows: None)

        @pl.when(b == nblk - 1)
        def _():
            fr = _shift_rows(nr_sc[...], False)
            fi = _shift_rows(ni_sc[...], False)
            alr, ali = alr_ref[...], ali_ref[...]
            cr, ci = fr, fi
            for _ in range(N_SEG - 2):
                sr = _shift_rows(cr, False)
                si = _shift_rows(ci, False)
                cr = fr + alr * sr + ali * si
                ci = fi + alr * si - ali * sr
            cr_ref[...] = cr
            ci_ref[...] = ci

    rev = lambda b: (nblk - 1 - b, 0)
    ublk = pl.BlockSpec((rows, sw), rev)
    xblk = pl.BlockSpec((rows, ns), rev)
    s8 = pl.BlockSpec((N_SEG, ns), lambda b: (0, 0))
    vec = pl.BlockSpec((1, sw), lambda b: (0, 0))
    gws = pl.BlockSpec((sw, sw), lambda b: (0, 0))
    btspec = pl.BlockSpec((N_SEC, secn, secw), lambda b: (0, 0, 0))
    ctspec = pl.BlockSpec((N_SEC, secw, secn), lambda b: (0, 0, 0))
    s8shape = jax.ShapeDtypeStruct((N_SEG, ns), F32)
    lcr, lci, dyp_all, d_nw, d_gw, d_gb, d_d = pl.pallas_call(
        carry_body, name="s5_bwd_carry", grid=(nblk,),
        in_specs=[ublk, ublk, ublk, gws, vec, vec, ctspec, ctspec, s8, s8, s8, s8],
        out_specs=(s8, s8, ublk, vec, gws, vec, vec),
        out_shape=(s8shape, s8shape, jax.ShapeDtypeStruct((lp, sw), BF16), jax.ShapeDtypeStruct((1, sw), F32),
                   jax.ShapeDtypeStruct((sw, sw), F32), jax.ShapeDtypeStruct((1, sw), F32),
                   jax.ShapeDtypeStruct((1, sw), F32)),
        scratch_shapes=[pltpu.VMEM((rows, ns), F32), pltpu.VMEM((rows, ns), F32),
                        pltpu.VMEM((N_SEG, ns), F32), pltpu.VMEM((N_SEG, ns), F32)],
        compiler_params=_params(("arbitrary",)),
    )(dout, yp, u, gluw, glub, nw, csrt, csit, a8r, a8i, al8r, al8i)

    def main_body(dyp_sc, u_ref, xr_ref, xi_ref, xtr_ref, xti_ref, c0r_ref, c0i_ref, lcr_ref, lci_ref,
                  d_ref, bsrt_ref, bsit_ref, csrt_ref, csit_ref, ar_ref, ai_ref,
                  du_ref, dcr_ref, dci_ref, dbr_ref, dbi_ref, dar_ref, dai_ref,
                  lr_sc, li_sc, nr_sc, ni_sc):
        b = pl.program_id(0)

        @pl.when(b == 0)
        def _():
            nr_sc[...] = lcr_ref[...]
            ni_sc[...] = lci_ref[...]
            for ref in (dcr_ref, dci_ref, dbr_ref, dbi_ref, dar_ref, dai_ref):
                ref[...] = jnp.zeros_like(ref)

        for s in range(N_SEC):
            db = dyp_sc[:, s * secw:(s + 1) * secw]
            xs = pl.ds(s * secn, secn)
            dcr_ref[s] += _dot_tn(xr_ref[:, xs].astype(BF16), db)
            dci_ref[s] += _dot_tn(xi_ref[:, xs].astype(BF16), db)

        first = b == nblk - 1

        def acc_da(r0, prev_rows):
            for cc in range(ns // SCAN_CW):
                cs = pl.ds(cc * SCAN_CW, SCAN_CW)
                lr = lr_sc[pl.ds(r0, 8), cs]
                li = li_sc[pl.ds(r0, 8), cs]
                if prev_rows is None:
                    xpr = jnp.where(first, c0r_ref[:, cs], xtr_ref[:, cs])
                    xpi = jnp.where(first, c0i_ref[:, cs], xti_ref[:, cs])
                else:
                    xpr = xr_ref[prev_rows, cs]
                    xpi = xi_ref[prev_rows, cs]
                dar_ref[:, cs] += lr * xpr + li * xpi
                dai_ref[:, cs] += li * xpr - lr * xpi

        lam_scan(lambda s: dyp_sc[:, s * secw:(s + 1) * secw], csrt_ref, csit_ref, ar_ref, ai_ref,
                 lr_sc, li_sc, nr_sc, ni_sc, acc_da)

        for s in range(N_SEC):
            xs = pl.ds(s * secn, secn)
            us = pl.ds(s * secw, secw)
            lrb = lr_sc[:, xs].astype(BF16)
            lib = li_sc[:, xs].astype(BF16)
            du = _dot(lrb, bsrt_ref[s]) + _dot(lib, bsit_ref[s]) + d_ref[:, us] * dyp_sc[:, us].astype(F32)
            du_ref[:, us] = du.astype(BF16)
            ub = u_ref[:, us].astype(BF16)
            dbr_ref[s] += _dot_tn(ub, lrb)
            dbi_ref[s] += _dot_tn(ub, lib)

    tail = pl.BlockSpec((N_SEG, ns), lambda b: (jnp.maximum((nblk - 1 - b) * jb - 1, 0), 0))
    acc_c = pl.BlockSpec((N_SEC, secn, secw), lambda b: (0, 0, 0))
    acc_b = pl.BlockSpec((N_SEC, secw, secn), lambda b: (0, 0, 0))
    du, dcr, dci, dbr, dbi, dar, dai = pl.pallas_call(
        main_body, name="s5_bwd", grid=(nblk,),
        in_specs=[ublk, ublk, xblk, xblk, tail, tail, s8, s8, s8, s8,
                  vec, btspec, btspec, ctspec, ctspec, s8, s8],
        out_specs=(ublk, acc_c, acc_c, acc_b, acc_b, s8, s8),
        out_shape=(jax.ShapeDtypeStruct((lp, sw), BF16),
                   jax.ShapeDtypeStruct((N_SEC, secn, secw), F32),
                   jax.ShapeDtypeStruct((N_SEC, secn, secw), F32),
                   jax.ShapeDtypeStruct((N_SEC, secw, secn), F32),
                   jax.ShapeDtypeStruct((N_SEC, secw, secn), F32),
                   s8shape, s8shape),
        scratch_shapes=[pltpu.VMEM((rows, ns), F32), pltpu.VMEM((rows, ns), F32),
                        pltpu.VMEM((N_SEG, ns), F32), pltpu.VMEM((N_SEG, ns), F32)],
        compiler_params=_params(("arbitrary",)),
    )(dyp_all, u, xr, xi, xr, xi, c0r, c0i, lcr, lci, d, bsrt, bsit, csrt, csit, a8r, a8i)
    return du, d_nw, d_gw, d_gb, d_d, dcr, dci, dbr, dbi, dar, dai


def _outproj_fwd(h, ret, ssm, wo):
    lp, d = h.shape
    nck, rs, _ = wo.shape
    rw = ret.shape[1]
    tm = _tile(lp, 640)
    per = rw // rs

    def body(h_ref, ret_ref, ssm_ref, w_ref, o_ref):
        acc = h_ref[...]
        for c in range(nck):
            src = ret_ref if c < per else ssm_ref
            lo = (c % per) * rs
            acc = acc + _dot(src[:, lo:lo + rs], w_ref[c])
        o_ref[...] = acc

    row = lambda w: pl.BlockSpec((tm, w), lambda i: (i, 0))
    return pl.pallas_call(
        body, name="outproj_fwd", grid=(lp // tm,),
        in_specs=[row(d), row(rw), row(ssm.shape[1]), pl.BlockSpec((nck, rs, d), lambda i: (0, 0, 0))],
        out_specs=row(d), out_shape=jax.ShapeDtypeStruct((lp, d), F32),
        compiler_params=_params(("arbitrary",)),
    )(h, ret, ssm, wo)


def _outproj_bwd(dh, ret, ssm, wo):
    lp, d = dh.shape
    nck, rs, _ = wo.shape
    rw = ret.shape[1]
    sw = ssm.shape[1]
    tm = _tile(lp, 640)
    per = rw // rs
    last = lp // tm - 1

    def body(dh_ref, ret_ref, ssm_ref, w_ref, dret_ref, dssm_ref, dw_ref, acc_sc):
        i = pl.program_id(0)

        @pl.when(i == 0)
        def _():
            acc_sc[...] = jnp.zeros_like(acc_sc)

        dhb = dh_ref[...].astype(BF16)
        for c in range(nck):
            src, dst = (ret_ref, dret_ref) if c < per else (ssm_ref, dssm_ref)
            lo = (c % per) * rs
            dst[:, lo:lo + rs] = _dot_nt(dhb, w_ref[c])
            acc_sc[c] += _dot_tn(src[:, lo:lo + rs], dhb)

        @pl.when(i == last)
        def _():
            dw_ref[...] = acc_sc[...].astype(BF16)

    row = lambda w: pl.BlockSpec((tm, w), lambda i: (i, 0))
    wsp = pl.BlockSpec((nck, rs, d), lambda i: (0, 0, 0))
    return pl.pallas_call(
        body, name="outproj_bwd", grid=(lp // tm,),
        in_specs=[row(d), row(rw), row(sw), wsp],
        out_specs=(row(rw), row(sw), wsp),
        out_shape=(jax.ShapeDtypeStruct((lp, rw), F32), jax.ShapeDtypeStruct((lp, sw), F32),
                   jax.ShapeDtypeStruct((nck, rs, d), BF16)),
        scratch_shapes=[pltpu.VMEM((nck, rs, d), F32)],
        compiler_params=_params(("arbitrary",)),
    )(dh, ret, ssm, wo)


def _pack(arrs):
    flat = jnp.concatenate([a.reshape(-1).astype(F32) for a in arrs])
    n = flat.shape[0]
    rows = -(-n // (8 * LANE)) * 8
    return jnp.pad(flat, (0, rows * LANE - n)).reshape(rows, LANE)


def _unpack(packed, shapes):
    flat = packed.reshape(-1)
    out, off = [], 0
    for s in shapes:
        n = math.prod(s)
        out.append(flat[off:off + n].reshape(s))
        off += n
    return out


def _to_segments(a, seg_len):
    return a.reshape(N_SEG, seg_len, a.shape[1]).transpose(1, 0, 2).reshape(a.shape)


def _from_segments(a, seg_len):
    return a.reshape(seg_len, N_SEG, a.shape[1]).transpose(1, 0, 2).reshape(a.shape)


WEIGHT_NAMES = ['meta_tokens', 'ffn1_norm_w', 'ffn1_w_gate', 'ffn1_w_up', 'ffn1_w_down', 'mix_norm_w', 'w_in',
                'ret_norm_w', 'ssm_lambda_re', 'ssm_lambda_im', 'ssm_log_dt', 'ssm_b_re', 'ssm_b_im', 'ssm_c_re',
                'ssm_c_im', 'ssm_d', 'ssm_glu_w', 'ssm_glu_b', 'ssm_norm_w', 'w_out', 'ffn2_norm_w', 'ffn2_w_gate',
                'ffn2_w_up', 'ffn2_w_down', 'final_norm_w']
BIG = ['ffn1_w_gate', 'ffn1_w_up', 'ffn1_w_down', 'w_in', 'ssm_glu_w', 'w_out', 'ffn2_w_gate', 'ffn2_w_up',
       'ffn2_w_down']
TRANSPOSED = ['ffn1_w_gate', 'ffn1_w_up', 'ffn2_w_gate', 'ffn2_w_up']
BIG_EARLY = ['ffn1_w_gate', 'ffn1_w_up', 'ffn1_w_down']
BIG_LATE = [n for n in BIG if n not in BIG_EARLY]
SMALL = [n for n in WEIGHT_NAMES if n not in BIG]


def kernel(x, meta_tokens, ffn1_norm_w, ffn1_w_gate, ffn1_w_up, ffn1_w_down, mix_norm_w, w_in, ret_norm_w, ssm_lambda_re, ssm_lambda_im, ssm_log_dt, ssm_b_re, ssm_b_im, ssm_c_re, ssm_c_im, ssm_d, ssm_glu_w, ssm_glu_b, ssm_norm_w, w_out, ffn2_norm_w, ffn2_w_gate, ffn2_w_up, ffn2_w_down, final_norm_w, loss_target, m_meta_tokens, m_ffn1_norm_w, m_ffn1_w_gate, m_ffn1_w_up, m_ffn1_w_down, m_mix_norm_w, m_w_in, m_ret_norm_w, m_ssm_lambda_re, m_ssm_lambda_im, m_ssm_log_dt, m_ssm_b_re, m_ssm_b_im, m_ssm_c_re, m_ssm_c_im, m_ssm_d, m_ssm_glu_w, m_ssm_glu_b, m_ssm_norm_w, m_w_out, m_ffn2_norm_w, m_ffn2_w_gate, m_ffn2_w_up, m_ffn2_w_down, m_final_norm_w, v_meta_tokens, v_ffn1_norm_w, v_ffn1_w_gate, v_ffn1_w_up, v_ffn1_w_down, v_mix_norm_w, v_w_in, v_ret_norm_w, v_ssm_lambda_re, v_ssm_lambda_im, v_ssm_log_dt, v_ssm_b_re, v_ssm_b_im, v_ssm_c_re, v_ssm_c_im, v_ssm_d, v_ssm_glu_w, v_ssm_glu_b, v_ssm_norm_w, v_w_out, v_ffn2_norm_w, v_ffn2_w_gate, v_ffn2_w_up, v_ffn2_w_down, v_final_norm_w):
    args = locals()
    w = {n: args[n] for n in WEIGHT_NAMES}
    m = {n: args["m_" + n] for n in WEIGHT_NAMES}
    v = {n: args["v_" + n] for n in WEIGHT_NAMES}

    seq, d = x.shape[1], x.shape[2]
    lp = seq + CHUNK
    seg_len = lp // N_SEG
    rw = RET_HEADS * HEAD_DIM
    sw = ssm_d.shape[-1]
    groups = sw // SSM_GROUP
    ns = groups * SSM_STATE
    jb = _tile(seg_len, S5_STEPS, 8)
    chip = 2 * lax.axis_index("x") + lax.axis_index("y")

    as_fd = lambda t: jnp.swapaxes(t, -1, -2)
    shards = {n: (as_fd(w[n][0]) if n in TRANSPOSED else w[n][0]).astype(BF16) for n in BIG}
    early = [shards[n] for n in BIG_EARLY] + [meta_tokens]
    gathered = _gather_two_level("gather_early", early)
    gw = dict(zip(BIG_EARLY, gathered[:-1]))
    meta_full = jnp.transpose(gathered[-1], (1, 0, 2)).reshape(N_META, d)
    late = [shards[n] for n in BIG_LATE]

    freqs = 1.0 / (ROPE_BASE ** (jnp.arange(0, HEAD_DIM, 2, dtype=F32) / HEAD_DIM))
    ang_c = (jnp.arange(lp // CHUNK, dtype=F32) * CHUNK - float(CHUNK - N_META))[:, None] * freqs[None, :]
    ang_r = jnp.arange(CHUNK, dtype=F32)[:, None] * freqs[None, :]
    cos_c, sin_c = jnp.cos(ang_c)[:, None, :], jnp.sin(ang_c)[:, None, :]
    cos_r, sin_r = jnp.cos(ang_r)[None], jnp.sin(ang_r)[None]
    cos_t = (cos_c * cos_r - sin_c * sin_r).reshape(lp, HEAD_DIM // 2)
    sin_t = (sin_c * cos_r + cos_c * sin_r).reshape(lp, HEAD_DIM // 2)
    cosf = jnp.concatenate([cos_t, cos_t], axis=1)
    sinf = jnp.concatenate([-sin_t, sin_t], axis=1)
    tables = _retention_tables(_tile(lp, RET_ROWS, CHUNK))

    lam_re, lam_im, log_dt = ssm_lambda_re[0], ssm_lambda_im[0], ssm_log_dt[0]
    b_re, b_im, c_re, c_im = ssm_b_re[0], ssm_b_im[0], ssm_c_re[0], ssm_c_im[0]
    (ar, ai, bbr, bbi), prep_vjp = jax.vjp(_s5_prepare, lam_re, lam_im, log_dt, b_re, b_im)
    dt = jnp.exp(log_dt)[:, None]
    el = jnp.exp(seg_len * lam_re * dt)
    alr = el * jnp.cos(seg_len * lam_im * dt)
    ali = el * jnp.sin(seg_len * lam_im * dt)
    bc8 = lambda t: jnp.broadcast_to(t.reshape(1, ns), (N_SEG, ns))
    a8r, a8i, al8r, al8i = bc8(ar), bc8(ai), bc8(alr), bc8(ali)
    bsr = _blockdiag_in(jnp.transpose(bbr, (0, 2, 1)))
    bsi = _blockdiag_in(jnp.transpose(bbi, (0, 2, 1)))
    csrt = _blockdiag_in(c_re)
    csit = _blockdiag_in(-c_im)
    tr = lambda t: jnp.transpose(t, (0, 2, 1))
    bsr_b, bsi_b = bsr.astype(BF16), bsi.astype(BF16)
    csr_b, csi_b = tr(csrt).astype(BF16), tr(csit).astype(BF16)
    bsrt_b, bsit_b = tr(bsr).astype(BF16), tr(bsi).astype(BF16)
    csrt_b, csit_b = csrt.astype(BF16), csit.astype(BF16)

    h0 = (jnp.concatenate([jnp.zeros((CHUNK - N_META, d), F32), meta_full], axis=0), x[0])
    (h1, g1, u1), late_half = _ffn_fwd("ffn1_fwd", h0, ffn1_norm_w, gw['ffn1_w_gate'], gw['ffn1_w_up'],
                                       gw['ffn1_w_down'], _allgather_chips_plan(late), late)
    gw.update(zip(BIG_LATE, _forward_sibling("gather_late_forward", late_half)))
    glu_full = gw['ssm_glu_w'].reshape(sw, sw)
    n2, q, k, vv, gate, u = _inproj_fwd(h1, mix_norm_w, gw['w_in'], cosf, sinf, rw)
    o, ret, sprev = _ret_fwd(q, k, vv, gate, ret_norm_w, tables)
    u_seg = _to_segments(u, seg_len)
    xr, xi, c0r, c0i, yp, ssm_seg = _s5_fwd(u_seg, bsr_b, bsi_b, csr_b, csi_b, a8r, a8i, al8r, al8i,
                                            ssm_d, glu_full, ssm_glu_b, ssm_norm_w, jb)
    ssm = _from_segments(ssm_seg, seg_len)
    h2 = _outproj_fwd(h1, ret, ssm, gw['w_out'])
    (dh3, g2, u2, loss_part, d_final), _ = _ffn_fwd(
        "ffn2_fwd_loss", h2, ffn2_norm_w, gw['ffn2_w_gate'], gw['ffn2_w_up'], gw['ffn2_w_down'],
        loss=(final_norm_w.reshape(1, d), loss_target[0]))

    (dh2, d_ffn2_norm, nb, daccb, ab, dgb, dub), _ = _ffn_bwd_act(
        "ffn2_bwd_act", dh3, h2, ffn2_norm_w, g2, u2, gw['ffn2_w_gate'], gw['ffn2_w_up'], gw['ffn2_w_down'])
    (dwg2, dwu2, dwd2), _ = _ffn_bwd_w("ffn2_bwd_w", nb, daccb, ab, dgb, dub)
    dret, dssm, dwo = _outproj_bwd(dh2, ret, ssm, gw['w_out'])
    (du_seg, d_ssm_norm, d_glu_w, d_glu_b, d_ssm_d, dcr_s, dci_s, dbr_s, dbi_s, dar8, dai8) = _s5_bwd(
        _to_segments(dssm, seg_len), u_seg, yp, xr, xi, c0r, c0i, bsrt_b, bsit_b, csrt_b, csit_b,
        a8r, a8i, al8r, al8i, ssm_d, glu_full, ssm_glu_b, ssm_norm_w, jb)
    du = _from_segments(du_seg, seg_len)
    dq, dk, dv, dgate, d_ret_norm = _ret_bwd(dret, q, k, vv, gate, o, sprev, ret_norm_w, tables, cosf, sinf)
    dh1, d_mix_norm, dwin = _inproj_bwd(dh2, h1, mix_norm_w, n2, gw['w_in'], dq, dk, dv, dgate, du)
    late_parts = {
        'w_in': dwin, 'ssm_glu_w': d_glu_w.reshape(N_CHIP, sw // N_CHIP, sw).astype(BF16), 'w_out': dwo,
        'ffn2_w_gate': dwg2, 'ffn2_w_up': dwu2, 'ffn2_w_down': dwd2,
    }
    late_list = [late_parts[n] for n in BIG_LATE]
    (dh0, d_ffn1_norm, nb, daccb, ab, dgb, dub), late_recv = _ffn_bwd_act(
        "ffn1_bwd_act", dh1, h0, ffn1_norm_w, g1, u1, gw['ffn1_w_gate'], gw['ffn1_w_up'], gw['ffn1_w_down'],
        _alltoall_chips_plan(late_list), late_list)
    grad_x = dh0[CHUNK:][None]
    d_meta = dh0[CHUNK - N_META:CHUNK]

    d_c_re = _blockdiag_out(tr(dcr_s), groups, SSM_GROUP, SSM_STATE)
    d_c_im = -_blockdiag_out(tr(dci_s), groups, SSM_GROUP, SSM_STATE)
    d_bbr = jnp.transpose(_blockdiag_out(dbr_s, groups, SSM_GROUP, SSM_STATE), (0, 2, 1))
    d_bbi = jnp.transpose(_blockdiag_out(dbi_s, groups, SSM_GROUP, SSM_STATE), (0, 2, 1))
    d_ar = jnp.sum(dar8, axis=0).reshape(groups, SSM_STATE)
    d_ai = jnp.sum(dai8, axis=0).reshape(groups, SSM_STATE)
    small_parts = [loss_part[0:1, :], d_meta, d_ffn1_norm, d_mix_norm, d_ret_norm, d_ar, d_ai, d_bbr, d_bbi,
                   d_c_re, d_c_im, d_ssm_d, d_glu_b, d_ssm_norm, d_ffn2_norm, d_final]
    small_shapes = [a.shape for a in small_parts]
    packed = _pack(small_parts)
    early_recv, (all_parts,) = _ffn_bwd_w_scatter("ffn1_bwd_w", nb, daccb, ab, dgb, dub, chip,
                                                  _allgather_all_plan([packed]), [packed])
    received = dict(zip(BIG_LATE + BIG_EARLY, late_recv + early_recv))
    chip_sums = _sum_slots("sum_chips", [received[n] for n in BIG], BF16)
    sib_sums = _swap_sibling("swap_sibling", chip_sums)
    (loss_row, g_meta_full, g_ffn1_norm, g_mix_norm, g_ret_norm, g_ar, g_ai, g_bbr, g_bbi, g_c_re, g_c_im,
     g_ssm_d, g_glu_b, g_ssm_norm, g_ffn2_norm, g_final) = _unpack(_sum_slots("sum_small", [all_parts], F32)[0],
                                                                  small_shapes)
    g_lam_re, g_lam_im, g_log_dt, g_b_re, g_b_im = prep_vjp((g_ar, g_ai, g_bbr, g_bbi))
    loss = loss_row[0, 0]
    g_meta = lax.dynamic_slice(g_meta_full, (0, chip * (d // N_CHIP)), (N_META, d // N_CHIP))
    small_grads = {
        'meta_tokens': g_meta, 'ffn1_norm_w': g_ffn1_norm, 'mix_norm_w': g_mix_norm, 'ret_norm_w': g_ret_norm,
        'ssm_lambda_re': g_lam_re[None], 'ssm_lambda_im': g_lam_im[None], 'ssm_log_dt': g_log_dt[None],
        'ssm_b_re': g_b_re[None], 'ssm_b_im': g_b_im[None], 'ssm_c_re': g_c_re[None], 'ssm_c_im': g_c_im[None],
        'ssm_d': g_ssm_d, 'ssm_glu_b': g_glu_b, 'ssm_norm_w': g_ssm_norm, 'ffn2_norm_w': g_ffn2_norm,
        'final_norm_w': g_final.reshape(d),
    }

    grads, deltas, new_m, new_v = {}, {}, {}, {}
    g_pair = {n: [mine, sib] for n, mine, sib in zip(BIG, chip_sums, sib_sums)}
    view = lambda n, t: as_fd(t) if n in TRANSPOSED else t
    big_out = _adam("adam_big", [(view(n, w[n]), view(n, m[n]), view(n, v[n])) for n in BIG], [g_pair[n] for n in BIG])
    for n, outs in zip(BIG, big_out):
        grads[n], deltas[n], new_m[n], new_v[n] = [view(n, t) for t in outs]
    sm_shapes = [w[n].shape for n in SMALL]
    sm_out = _adam("adam_small", [(_pack([w[n] for n in SMALL]), _pack([m[n] for n in SMALL]),
                                  _pack([v[n] for n in SMALL]))],
                   [[_pack([small_grads[n].reshape(w[n].shape) for n in SMALL])]])[0]
    for dst, packed in zip((grads, deltas, new_m, new_v), sm_out):
        for n, t in zip(SMALL, _unpack(packed, sm_shapes)):
            dst[n] = t

    return (loss, grad_x, *[grads[n] for n in WEIGHT_NAMES], *[deltas[n] for n in WEIGHT_NAMES],
            *[new_m[n] for n in WEIGHT_NAMES], *[new_v[n] for n in WEIGHT_NAMES])
```

```python
import functools
import math

import jax
import jax.numpy as jnp
from jax import lax
from jax.experimental import pallas as pl
from jax.experimental.pallas import tpu as pltpu

N_META = 16
RET_HEADS = 4
HEAD_DIM = 128
SSM_GROUP = 16
SSM_STATE = 64
CHUNK = 128
ROPE_BASE = 10000.0
EPS = 1e-6
FFN_RES = 0.5
N_SEG = 8
N_SEC = 4
N_CHIP = 4
LANE = 128
FFN_CPS = 2
BWD_W_ROWS = 1664

ADAM_LR = 0.001
ADAM_B1 = 0.9
ADAM_B2 = 0.999
ADAM_EPS = 1e-08
ADAM_WD = 0.01
ADAM_STEP = 10

VMEM_LIMIT = 56 * 1024 * 1024

F32 = jnp.float32
BF16 = jnp.bfloat16
MESH = pl.DeviceIdType.MESH


def _dot(a, b):
    return jnp.dot(a, b, preferred_element_type=F32)


def _dot_nt(a, b):
    return lax.dot_general(a, b, (((1,), (1,)), ((), ())), preferred_element_type=F32)


def _dot_tn(a, b):
    return lax.dot_general(a, b, (((0,), (0,)), ((), ())), preferred_element_type=F32)


def _tile(n, target, mult=64):
    best = None
    t = mult
    while t <= min(n, target):
        if n % t == 0:
            best = t
        t += mult
    assert best is not None, (n, target)
    return best


def _params(sem, vmem=VMEM_LIMIT):
    return pltpu.CompilerParams(dimension_semantics=sem, vmem_limit_bytes=vmem)


def _rms_stats(xf):
    r = lax.rsqrt(jnp.mean(xf * xf, axis=-1, keepdims=True) + EPS)
    return xf * r, r


def _rms_bwd(dy, xh, r, w):
    dxh = dy * w
    return r * (dxh - xh * jnp.mean(dxh * xh, axis=-1, keepdims=True))


def _sigmoid(x):
    return 0.5 * jnp.tanh(0.5 * x) + 0.5


GELU_K0 = math.sqrt(2.0 / math.pi)
GELU_K1 = 0.044715


CHIP_MASKS = [(1, 1, 0), (1, 0, 0), (0, 1, 0)]
ALL_MASKS = [(0, 0, 1), (0, 1, 0), (0, 1, 1), (1, 0, 0), (1, 0, 1), (1, 1, 0), (1, 1, 1)]
SIB_MASKS = [(0, 0, 1)]
ANY_SPEC = pl.BlockSpec(memory_space=pl.ANY)
MULTI_SUM_STEPS = 4
MULTI_ADAM_STEPS = 8


class _Plan:
    def __init__(self, arrays, masks, n_slots, src_slotted, dst_slotted, local_copy, half=False, forward=False):
        self.shapes = [(a.shape, a.dtype) for a in arrays]
        self.n = len(arrays)
        self.masks = masks
        self.n_slots = n_slots
        self.src_slotted, self.dst_slotted, self.local_copy = src_slotted, dst_slotted, local_copy
        self.half, self.forward = half, forward
        self.n_cp = self.n * len(masks) * (len(CHIP_MASKS) if forward else 1)

    def out_shape(self):
        out = []
        for shp, dt in self.shapes:
            if self.dst_slotted and not self.src_slotted:
                shp = (self.n_slots,) + shp
            elif self.src_slotted and not self.dst_slotted:
                shp = shp[1:]
            out.append(jax.ShapeDtypeStruct(shp, dt))
        return tuple(out)

    def scratch(self):
        return [pltpu.SemaphoreType.DMA((self.n_cp,)), pltpu.SemaphoreType.DMA((self.n_cp,)),
                pltpu.SemaphoreType.DMA((self.n,))]

    def _slot(self, px, py, pc):
        if self.n_slots == 8:
            return 4 * px + 2 * py + pc
        if self.n_slots == 4:
            return 2 * px + py
        return pc

    def copies(self, ins, outs, sems):
        send_sems, recv_sems, loc_sems = sems
        x, y, c = lax.axis_index("x"), lax.axis_index("y"), lax.axis_index("c")
        me = self._slot(x, y, c)
        n_m = len(self.masks)
        cps = []
        for a in range(self.n):
            if self.forward:
                rows = self.shapes[a][0][-2] // 2
                mine = pl.ds(pl.multiple_of(c * rows, 8), rows)
                for j, (mx, my, _) in enumerate(CHIP_MASKS):
                    blk = outs[a].at[2 * (1 - x if mx else x) + (1 - y if my else y), mine]
                    k = a * len(CHIP_MASKS) + j
                    cps.append(pltpu.make_async_remote_copy(
                        src_ref=blk, dst_ref=blk, send_sem=send_sems.at[k], recv_sem=recv_sems.at[k],
                        device_id=(x, y, 1 - c), device_id_type=MESH))
                continue
            if self.local_copy:
                src = ins[a].at[me] if self.src_slotted else ins[a]
                cps.append(pltpu.make_async_copy(src, outs[a].at[me], loc_sems.at[a]))
            for mi, (mx, my, mc) in enumerate(self.masks):
                px = 1 - x if mx else x
                py = 1 - y if my else y
                pc = 1 - c if mc else c
                src = ins[a].at[self._slot(px, py, pc)] if self.src_slotted else ins[a]
                dst = outs[a].at[me] if self.dst_slotted else outs[a]
                if self.half:
                    rows = src.shape[-2] // 2
                    mine = pl.ds(pl.multiple_of(c * rows, 8), rows)
                    src, dst = src.at[mine], dst.at[mine]
                k = a * n_m + mi
                cps.append(pltpu.make_async_remote_copy(
                    src_ref=src, dst_ref=dst, send_sem=send_sems.at[k], recv_sem=recv_sems.at[k],
                    device_id=(px, py, pc), device_id_type=MESH))
        return cps


def _exchange(name, plan, arrays):
    n = plan.n

    def body(*refs):
        cps = plan.copies(refs[:n], refs[n:2 * n], refs[2 * n:])
        for cp in cps:
            cp.start()
        for cp in cps:
            cp.wait()

    outs = pl.pallas_call(
        body, name=name, out_shape=plan.out_shape(),
        in_specs=[ANY_SPEC] * n, out_specs=tuple([ANY_SPEC] * n), scratch_shapes=plan.scratch(),
        input_output_aliases={i: i for i in range(n)} if plan.forward else {},
    )(*arrays)
    return list(outs)


def _pcall(body, *, name, grid, in_specs, out_specs, out_shape, scratch_shapes, args, plan=None, plan_args=()):
    sem = ("arbitrary",) * len(grid)
    if plan is None:
        return pl.pallas_call(body, name=name, grid=grid, in_specs=in_specs, out_specs=out_specs,
                              out_shape=out_shape, scratch_shapes=scratch_shapes,
                              compiler_params=_params(sem))(*args), []
    n_in, n_out, n_scr, n_p = len(in_specs), len(out_specs), len(scratch_shapes), plan.n

    def wrapped(*refs):
        ins = refs[:n_in]
        p_ins = refs[n_in:n_in + n_p]
        o0 = n_in + n_p
        outs = refs[o0:o0 + n_out]
        p_outs = refs[o0 + n_out:o0 + n_out + n_p]
        s0 = o0 + n_out + n_p
        scr = refs[s0:s0 + n_scr]
        sems = refs[s0 + n_scr:]
        ids = [pl.program_id(i) for i in range(len(grid))]
        first = functools.reduce(jnp.logical_and, [i == 0 for i in ids])
        last = functools.reduce(jnp.logical_and, [i == g - 1 for i, g in zip(ids, grid)])

        @pl.when(first)
        def _():
            for cp in plan.copies(p_ins, p_outs, sems):
                cp.start()

        body(*ins, *outs, *scr)

        @pl.when(last)
        def _():
            for cp in plan.copies(p_ins, p_outs, sems):
                cp.wait()

    res = pl.pallas_call(
        wrapped, name=name, grid=grid,
        in_specs=list(in_specs) + [ANY_SPEC] * n_p,
        out_specs=tuple(out_specs) + (ANY_SPEC,) * n_p,
        out_shape=tuple(out_shape) + plan.out_shape(),
        scratch_shapes=list(scratch_shapes) + plan.scratch(),
        compiler_params=_params(sem),
    )(*args, *plan_args)
    return res[:n_out], list(res[n_out:])


def _allgather_chips_plan(arrays):
    return _Plan(arrays, CHIP_MASKS, 4, False, True, True, half=True)


def _gather_two_level(name, arrays):
    n = len(arrays)
    ici = _allgather_chips_plan(arrays)
    fwd = _Plan(ici.out_shape(), SIB_MASKS, 4, True, True, False, forward=True)
    n_m = len(CHIP_MASKS)

    def body(*refs):
        ins, outs, sems = refs[:n], refs[n:2 * n], refs[2 * n:]
        ici_cps = ici.copies(ins, outs, sems[:3])
        fwd_cps = fwd.copies(None, outs, sems[3:])
        for cp in ici_cps:
            cp.start()
        for a in range(n):
            for m in range(n_m):
                ici_cps[a * (n_m + 1) + 1 + m].wait_recv()
                fwd_cps[a * n_m + m].start()
        for a in range(n):
            ici_cps[a * (n_m + 1)].wait()
            for m in range(n_m):
                ici_cps[a * (n_m + 1) + 1 + m].wait_send()
        for cp in fwd_cps:
            cp.wait()

    return list(pl.pallas_call(
        body, name=name, out_shape=ici.out_shape(),
        in_specs=[ANY_SPEC] * n, out_specs=tuple([ANY_SPEC] * n), scratch_shapes=ici.scratch() + fwd.scratch(),
    )(*arrays))


def _forward_sibling(name, gathered):
    return _exchange(name, _Plan(gathered, SIB_MASKS, 4, True, True, False, forward=True), gathered)


def _alltoall_chips_plan(arrays):
    return _Plan(arrays, CHIP_MASKS, 4, True, True, True)


def _swap_sibling(name, arrays):
    return _exchange(name, _Plan(arrays, SIB_MASKS, 2, False, False, False), arrays)


def _allgather_all_plan(arrays):
    return _Plan(arrays, ALL_MASKS, 8, False, True, True)


def _sum_slots(name, arrs, out_dtype):
    s = arrs[0].shape[0]
    n = len(arrs)
    steps = arrs[0].shape[1] // _tile(arrs[0].shape[1], 512, 8) if n == 1 else MULTI_SUM_STEPS
    for a in arrs:
        assert a.shape[1] % (16 * steps) == 0 or n == 1, a.shape

    def body(*refs):
        for a_ref, o_ref in zip(refs[:n], refs[n:]):
            acc = a_ref[0].astype(F32)
            for i in range(1, s):
                acc = acc + a_ref[i].astype(F32)
            o_ref[...] = acc.astype(out_dtype)

    return list(pl.pallas_call(
        body, name=name, grid=(steps,),
        in_specs=[pl.BlockSpec((s, a.shape[1] // steps, a.shape[2]), lambda i: (0, i, 0)) for a in arrs],
        out_specs=tuple(pl.BlockSpec((a.shape[1] // steps, a.shape[2]), lambda i: (i, 0)) for a in arrs),
        out_shape=tuple(jax.ShapeDtypeStruct(a.shape[1:], out_dtype) for a in arrs),
        compiler_params=_params(("arbitrary",)),
    )(*arrs))


def _adam_math(w, g, m, v):
    m_new = ADAM_B1 * m + (1.0 - ADAM_B1) * g
    v_new = ADAM_B2 * v + (1.0 - ADAM_B2) * (g * g)
    m_hat = m_new / (1.0 - ADAM_B1 ** ADAM_STEP)
    v_hat = v_new / (1.0 - ADAM_B2 ** ADAM_STEP)
    delta = -ADAM_LR * (m_hat / (jnp.sqrt(v_hat) + ADAM_EPS) + ADAM_WD * w)
    return delta, m_new, v_new


def _adam(name, wmv, g_parts):
    n_w = len(wmv)
    n_g = len(g_parts[0])
    lead = wmv[0][0].ndim == 3
    at = (lambda ref: ref.at[0]) if lead else (lambda ref: ref)
    n_in = 3 + n_g
    rows0 = wmv[0][0].shape[-2]
    steps = rows0 // _tile(rows0, 256, 8) if n_w == 1 else MULTI_ADAM_STEPS
    for w, _, _ in wmv:
        assert w.shape[-2] % (8 * steps) == 0, w.shape

    def body(*refs):
        for j in range(n_w):
            ins = refs[j * n_in:(j + 1) * n_in]
            outs = refs[n_w * n_in + 4 * j:n_w * n_in + 4 * j + 4]
            w_ref, m_ref, v_ref = [at(t) for t in ins[:3]]
            g_out, d_out, m_out, v_out = [at(t) for t in outs]
            g = ins[3][...].astype(F32)
            for gr in ins[4:]:
                g = g + gr[...].astype(F32)
            delta, m_new, v_new = _adam_math(w_ref[...], g, m_ref[...], v_ref[...])
            g_out[...] = g
            d_out[...] = delta
            m_out[...] = m_new
            v_out[...] = v_new

    in_specs, out_specs, out_shape, args = [], [], [], []
    for (w, m, v), gp in zip(wmv, g_parts):
        r, c = w.shape[-2:]
        spec = pl.BlockSpec((r // steps, c), lambda i: (i, 0))
        wspec = pl.BlockSpec((1, r // steps, c), lambda i: (0, i, 0)) if lead else spec
        in_specs += [wspec] * 3 + [spec] * n_g
        out_specs += [wspec] * 4
        out_shape += [jax.ShapeDtypeStruct(w.shape, F32)] * 4
        args += [w, m, v, *gp]
    res = pl.pallas_call(
        body, name=name, grid=(steps,),
        in_specs=in_specs, out_specs=tuple(out_specs), out_shape=tuple(out_shape),
        compiler_params=_params(("arbitrary",)),
    )(*args)
    return [tuple(res[4 * j:4 * j + 4]) for j in range(n_w)]


SUB_ROWS = 32
FFN_BWD_ROWS = 416
FFN_FWD_CPS = 4
FFN_FWD_ROWS = 416
FFN_LOSS_ROWS = 416
RET_ROWS = 640
S5_STEPS = 104


def _tile_parts(tm, d, head, x):
    nsub = tm // SUB_ROWS
    off = head.shape[0] // SUB_ROWS
    specs = [pl.BlockSpec(head.shape, lambda i, k: (0, 0))] + [
        pl.BlockSpec((SUB_ROWS, d), lambda i, k, j=j: (jnp.maximum(i * nsub + j - off, 0), 0)) for j in range(nsub)]

    def assemble(i, part_refs, h_sc):
        head_ref, x_refs = part_refs[0], part_refs[1:]
        for j in range(nsub):
            rows = slice(j * SUB_ROWS, (j + 1) * SUB_ROWS)
            val = x_refs[j][...]
            if j < off:
                val = jnp.where(i == 0, head_ref[rows, :], val)
            h_sc[rows, :] = val

    return specs, [head] + [x] * nsub, assemble


def _h_source(body, h, tm, d):
    if not isinstance(h, tuple):
        return body, [pl.BlockSpec((tm, d), lambda i, k: (i, 0))], [h], []
    specs, args, assemble = _tile_parts(tm, d, *h)
    n_h = len(specs)

    def with_parts(*refs):
        h_sc = refs[-1]

        @pl.when(pl.program_id(1) == 0)
        def _():
            assemble(pl.program_id(0), refs[:n_h], h_sc)

        body(h_sc, *refs[n_h:-1])

    return with_parts, specs, args, [pltpu.VMEM((tm, d), F32)]


def _ffn_fwd(name, h, nw, wg, wu, wd, plan=None, plan_args=(), loss=None):
    lp, d = (h[0].shape[0] + h[1].shape[0], h[1].shape[1]) if isinstance(h, tuple) else h.shape
    nck, f, _ = wg.shape
    tm = _tile(lp, FFN_FWD_ROWS if loss is None else FFN_LOSS_ROWS, SUB_ROWS)
    cps = FFN_FWD_CPS
    last = nck // cps - 1
    n_t = 0
    if loss is not None:
        t_specs, t_args, t_assemble = _tile_parts(tm, d, jnp.zeros((lp - loss[1].shape[0], d), F32), loss[1])
        n_t = len(t_specs)

    def body(h_ref, nw_ref, wg_ref, wu_ref, wd_ref, *rest):
        if loss is not None:
            fw_ref, t_parts, rest = rest[0], rest[1:1 + n_t], rest[1 + n_t:]
            ho_ref, g_ref, u_ref, loss_ref, dfw_ref, n_sc, acc_sc, t_sc = rest
        else:
            ho_ref, g_ref, u_ref, n_sc, acc_sc = rest
        i = pl.program_id(0)
        k = pl.program_id(1)

        @pl.when(k == 0)
        def _():
            xh, _ = _rms_stats(h_ref[...])
            n_sc[...] = (xh * nw_ref[...]).astype(BF16)
            acc_sc[...] = jnp.zeros_like(acc_sc)

        n = n_sc[...]
        acc = acc_sc[...]
        for c in range(cps):
            g = _dot_nt(n, wg_ref[c])
            u = _dot_nt(n, wu_ref[c])
            g_ref[c] = g.astype(BF16)
            u_ref[c] = u.astype(BF16)
            a = (g * _sigmoid(g) * u).astype(BF16)
            acc = acc + _dot(a, wd_ref[c])
        acc_sc[...] = acc

        if loss is None:
            @pl.when(k == last)
            def _():
                ho_ref[...] = h_ref[...] + FFN_RES * acc_sc[...]
            return

        @pl.when(jnp.logical_and(i == 0, k == 0))
        def _():
            loss_ref[...] = jnp.zeros_like(loss_ref)
            dfw_ref[...] = jnp.zeros_like(dfw_ref)

        @pl.when(k == last)
        def _():
            t_assemble(i, t_parts, t_sc)
            xh, r = _rms_stats(h_ref[...] + FFN_RES * acc_sc[...])
            w = fw_ref[...]
            head_rows = lp - loss[1].shape[0]
            row = lax.broadcasted_iota(jnp.int32, (tm, d), 0) + i * tm
            err = jnp.where(row < head_rows, 0.0, xh * w - t_sc[...])
            loss_ref[...] += 0.5 * jnp.sum(err * err) / d
            dout = err * (1.0 / d)
            dfw_ref[...] += jnp.sum(dout * xh, axis=0, keepdims=True)
            ho_ref[...] = _rms_bwd(dout, xh, r, w)

    body, h_specs, h_args, h_scratch = _h_source(body, h, tm, d)
    vec = pl.BlockSpec((1, d), lambda i, k: (0, 0))
    w_fd = pl.BlockSpec((cps, f, d), lambda i, k: (k, 0, 0), **({'pipeline_mode': pl.Buffered(1)} if cps == nck else {}))
    hid = pl.BlockSpec((cps, tm, f), lambda i, k: (k, i, 0))
    hshape = jax.ShapeDtypeStruct((nck, lp, f), BF16)
    args, in_specs = (*h_args, nw, wg, wu, wd), h_specs + [vec, w_fd, w_fd, w_fd]
    out_specs = (pl.BlockSpec((tm, d), lambda i, k: (i, 0)), hid, hid)
    out_shape = (jax.ShapeDtypeStruct((lp, d), F32), hshape, hshape)
    scratch = [pltpu.VMEM((tm, d), BF16), pltpu.VMEM((tm, d), F32)]
    if loss is not None:
        args, in_specs = (*args, loss[0], *t_args), in_specs + [vec] + t_specs
        out_specs += (pl.BlockSpec((8, LANE), lambda i, k: (0, 0)), vec)
        out_shape += (jax.ShapeDtypeStruct((8, LANE), F32), jax.ShapeDtypeStruct((1, d), F32))
        scratch = scratch + [pltpu.VMEM((tm, d), F32)]
    return _pcall(
        body, name=name, grid=(lp // tm, nck // cps), plan=plan, plan_args=plan_args,
        args=args, in_specs=in_specs, out_specs=out_specs, out_shape=out_shape,
        scratch_shapes=scratch + h_scratch)


def _ffn_bwd_act(name, dh, h, nw, g, u, wg, wu, wd, plan=None, plan_args=()):
    lp, d = dh.shape
    nck, f, _ = wg.shape
    tm = _tile(lp, FFN_BWD_ROWS, SUB_ROWS)
    last = nck // FFN_CPS - 1

    def body(h_ref, dh_ref, nw_ref, g_ref, u_ref, wg_ref, wu_ref, wd_ref,
             dhi_ref, dnw_ref, n_ref, dacc_ref, a_ref, dg_ref, du_ref,
             xh_sc, r_sc, dn_sc):
        i = pl.program_id(0)
        k = pl.program_id(1)

        @pl.when(k == 0)
        def _():
            xh, r = _rms_stats(h_ref[...])
            xh_sc[...] = xh
            r_sc[...] = r
            n_ref[...] = (xh * nw_ref[...]).astype(BF16)
            dacc_ref[...] = (FFN_RES * dh_ref[...]).astype(BF16)
            dn_sc[...] = jnp.zeros_like(dn_sc)

        @pl.when(jnp.logical_and(i == 0, k == 0))
        def _():
            dnw_ref[...] = jnp.zeros_like(dnw_ref)

        dacc = dacc_ref[...]
        dn = dn_sc[...]
        for c in range(FFN_CPS):
            gv = g_ref[c].astype(F32)
            uv = u_ref[c].astype(F32)
            sg = _sigmoid(gv)
            sil = gv * sg
            da = _dot_nt(dacc, wd_ref[c])
            dgk = (da * uv * (sg * (1.0 + gv * (1.0 - sg)))).astype(BF16)
            duk = (da * sil).astype(BF16)
            a_ref[c] = (sil * uv).astype(BF16)
            dg_ref[c] = dgk
            du_ref[c] = duk
            dn = dn + _dot(dgk, wg_ref[c]) + _dot(duk, wu_ref[c])
        dn_sc[...] = dn

        @pl.when(k == last)
        def _():
            dnl = dn_sc[...]
            xh = xh_sc[...]
            dhi_ref[...] = dh_ref[...] + _rms_bwd(dnl, xh, r_sc[...], nw_ref[...])
            dnw_ref[...] += jnp.sum(dnl * xh, axis=0, keepdims=True)

    body, h_specs, h_args, h_scratch = _h_source(body, h, tm, d)
    row = pl.BlockSpec((tm, d), lambda i, k: (i, 0))
    vec = pl.BlockSpec((1, d), lambda i, k: (0, 0))
    hid = pl.BlockSpec((FFN_CPS, tm, f), lambda i, k: (k, i, 0))
    w_fd = pl.BlockSpec((FFN_CPS, f, d), lambda i, k: (k, 0, 0))
    rshape = jax.ShapeDtypeStruct((lp, d), BF16)
    hshape = jax.ShapeDtypeStruct((nck, lp, f), BF16)
    return _pcall(
        body, name=name, grid=(lp // tm, nck // FFN_CPS), plan=plan, plan_args=plan_args,
        args=(*h_args, dh, nw, g, u, wg, wu, wd),
        in_specs=h_specs + [row, vec, hid, hid, w_fd, w_fd, w_fd],
        out_specs=(row, vec, row, row, hid, hid, hid),
        out_shape=(jax.ShapeDtypeStruct((lp, d), F32), jax.ShapeDtypeStruct((1, d), F32),
                   rshape, rshape, hshape, hshape, hshape),
        scratch_shapes=[pltpu.VMEM((tm, d), F32), pltpu.VMEM((tm, 1), F32), pltpu.VMEM((tm, d), F32)] + h_scratch)


def _ffn_bwd_w(name, n, dacc, a, dg, du, plan=None, plan_args=()):
    lp, d = n.shape
    nck, _, f = a.shape
    tm = _tile(lp, BWD_W_ROWS)
    last = lp // tm - 1

    def body(n_ref, dacc_ref, a_ref, dg_ref, du_ref, dwg_ref, dwu_ref, dwd_ref, ag_sc, au_sc, ad_sc):
        i = pl.program_id(1)

        @pl.when(i == 0)
        def _():
            ag_sc[...] = jnp.zeros_like(ag_sc)
            au_sc[...] = jnp.zeros_like(au_sc)
            ad_sc[...] = jnp.zeros_like(ad_sc)

        nv = n_ref[...]
        ag_sc[...] += _dot_tn(dg_ref[0], nv)
        au_sc[...] += _dot_tn(du_ref[0], nv)
        ad_sc[...] += _dot_tn(a_ref[0], dacc_ref[...])

        @pl.when(i == last)
        def _():
            dwg_ref[0] = ag_sc[...].astype(BF16)
            dwu_ref[0] = au_sc[...].astype(BF16)
            dwd_ref[0] = ad_sc[...].astype(BF16)

    row = pl.BlockSpec((tm, d), lambda k, i: (i, 0))
    hid = pl.BlockSpec((1, tm, f), lambda k, i: (k, i, 0))
    w_fd = pl.BlockSpec((1, f, d), lambda k, i: (k, 0, 0))
    wshape = jax.ShapeDtypeStruct((nck, f, d), BF16)
    return _pcall(
        body, name=name, grid=(nck, lp // tm), plan=plan, plan_args=plan_args, args=(n, dacc, a, dg, du),
        in_specs=[row, row, hid, hid, hid], out_specs=(w_fd, w_fd, w_fd), out_shape=(wshape,) * 3,
        scratch_shapes=[pltpu.VMEM((f, d), F32)] * 3)


def _ffn_bwd_w_scatter(name, n, dacc, a, dg, du, chip, plan, plan_args):
    lp, d = n.shape
    nck, _, f = a.shape
    tm = _tile(lp, BWD_W_ROWS)
    last_i = lp // tm - 1
    n_w = 3
    n_p = plan.n

    def body(me_ref, n_ref, dacc_ref, a_ref, dg_ref, du_ref, *rest):
        p_ins = rest[:n_p]
        recv = rest[n_p:n_p + n_w]
        p_outs = rest[n_p + n_w:2 * n_p + n_w]
        acc = rest[2 * n_p + n_w:2 * n_p + 2 * n_w]
        stage, send_sems, recv_sems, loc_sems = rest[2 * n_p + 2 * n_w:2 * n_p + 2 * n_w + 4]
        p_sems = rest[2 * n_p + 2 * n_w + 4:]
        p = pl.program_id(0)
        i = pl.program_id(1)
        me = me_ref[0]
        c = lax.axis_index("c")

        def send(w, pos):
            kk = jnp.bitwise_xor(me, nck - 1 - pos)
            diff = jnp.bitwise_xor(kk, me)
            m = jnp.where(diff == 2, 0, jnp.where(diff == 1, 1, 2))
            return pltpu.make_async_remote_copy(
                src_ref=stage.at[lax.rem(pos, 2), w], dst_ref=recv[w].at[me],
                send_sem=send_sems.at[w * 3 + m], recv_sem=recv_sems.at[w * 3 + m],
                device_id=(lax.div(kk, 2), lax.rem(kk, 2), c), device_id_type=MESH)

        @pl.when(jnp.logical_and(p == 0, i == 0))
        def _():
            for cp in plan.copies(p_ins, p_outs, p_sems):
                cp.start()

        @pl.when(i == 0)
        def _():
            for t in acc:
                t[...] = jnp.zeros_like(t)

        nv = n_ref[...]
        acc[0][...] += _dot_tn(dg_ref[0], nv)
        acc[1][...] += _dot_tn(du_ref[0], nv)
        acc[2][...] += _dot_tn(a_ref[0], dacc_ref[...])

        @pl.when(jnp.logical_and(i == last_i, p >= 2))
        def _():
            for w in range(n_w):
                send(w, p - 2).wait_send()

        @pl.when(i == last_i)
        def _():
            for w in range(n_w):
                stage[lax.rem(p, 2), w] = acc[w][...].astype(BF16)

        @pl.when(jnp.logical_and(i == last_i, p < nck - 1))
        def _():
            for w in range(n_w):
                send(w, p).start()

        @pl.when(jnp.logical_and(i == last_i, p == nck - 1))
        def _():
            own = [pltpu.make_async_copy(stage.at[(nck - 1) % 2, w], recv[w].at[me], loc_sems.at[w])
                   for w in range(n_w)]
            for cp in own:
                cp.start()
            for w in range(n_w):
                send(w, nck - 2).wait_send()
            for cp in own:
                cp.wait()
            for w in range(n_w):
                for m in range(3):
                    pltpu.make_async_remote_copy(
                        src_ref=stage.at[0, w], dst_ref=recv[w].at[me],
                        send_sem=send_sems.at[w * 3 + m], recv_sem=recv_sems.at[w * 3 + m],
                        device_id=(0, 0, c), device_id_type=MESH).wait_recv()
            for cp in plan.copies(p_ins, p_outs, p_sems):
                cp.wait()

    chunk = lambda k, me_ref: jnp.bitwise_xor(me_ref[0], nck - 1 - k)
    row = pl.BlockSpec((tm, d), lambda k, i, me_ref: (i, 0))
    hid = pl.BlockSpec((1, tm, f), lambda k, i, me_ref: (chunk(k, me_ref), i, 0))
    wshape = jax.ShapeDtypeStruct((nck, f, d), BF16)
    res = pl.pallas_call(
        body, name=name,
        grid_spec=pltpu.PrefetchScalarGridSpec(
            num_scalar_prefetch=1, grid=(nck, lp // tm),
            in_specs=[row, row, hid, hid, hid] + [ANY_SPEC] * n_p,
            out_specs=(ANY_SPEC,) * (n_w + n_p),
            scratch_shapes=[pltpu.VMEM((f, d), F32)] * n_w + [
                pltpu.VMEM((2, n_w, f, d), BF16), pltpu.SemaphoreType.DMA((n_w * 3,)),
                pltpu.SemaphoreType.DMA((n_w * 3,)), pltpu.SemaphoreType.DMA((n_w,))] + plan.scratch()),
        out_shape=(wshape,) * n_w + plan.out_shape(),
        compiler_params=_params(("arbitrary", "arbitrary")),
    )(chip.reshape(1).astype(jnp.int32), n, dacc, a, dg, du, *plan_args)
    return list(res[:n_w]), list(res[n_w:])


def _inproj_fwd(h, nw, w_in, cosf, sinf, rw):
    lp, d = h.shape
    nck, _, ps = w_in.shape
    proj = nck * ps
    sw = proj - 4 * rw
    tm = _tile(lp, 640)
    scale = HEAD_DIM ** -0.5
    heads = rw // HEAD_DIM

    def body(h_ref, nw_ref, w_ref, cos_ref, sin_ref, n_ref, q_ref, k_ref, v_ref, g_ref, u_ref, p_sc):
        xh, _ = _rms_stats(h_ref[...])
        n = (xh * nw_ref[...]).astype(BF16)
        n_ref[...] = n
        for c in range(nck):
            p_sc[:, c * ps:(c + 1) * ps] = _dot(n, w_ref[c])
        cs = cos_ref[...]
        sn = sin_ref[...]
        for hh in range(heads):
            lo = hh * HEAD_DIM
            qh = p_sc[:, lo:lo + HEAD_DIM]
            q_ref[:, lo:lo + HEAD_DIM] = (qh * cs + pltpu.roll(qh, HEAD_DIM // 2, 1) * sn).astype(BF16)
            kh = p_sc[:, rw + lo:rw + lo + HEAD_DIM]
            k_ref[:, lo:lo + HEAD_DIM] = ((kh * cs + pltpu.roll(kh, HEAD_DIM // 2, 1) * sn) * scale).astype(BF16)
        v_ref[...] = p_sc[:, 2 * rw:3 * rw].astype(BF16)
        g_ref[...] = p_sc[:, 3 * rw:4 * rw]
        u_ref[...] = p_sc[:, 4 * rw:]

    row = lambda w: pl.BlockSpec((tm, w), lambda i: (i, 0))
    return pl.pallas_call(
        body, name="inproj_fwd", grid=(lp // tm,),
        in_specs=[row(d), pl.BlockSpec((1, d), lambda i: (0, 0)),
                  pl.BlockSpec((nck, d, ps), lambda i: (0, 0, 0)), row(HEAD_DIM), row(HEAD_DIM)],
        out_specs=(row(d), row(rw), row(rw), row(rw), row(rw), row(sw)),
        out_shape=(jax.ShapeDtypeStruct((lp, d), BF16),
                   jax.ShapeDtypeStruct((lp, rw), BF16),
                   jax.ShapeDtypeStruct((lp, rw), BF16),
                   jax.ShapeDtypeStruct((lp, rw), BF16),
                   jax.ShapeDtypeStruct((lp, rw), F32),
                   jax.ShapeDtypeStruct((lp, sw), F32)),
        scratch_shapes=[pltpu.VMEM((tm, proj), F32)],
        compiler_params=_params(("arbitrary",)),
    )(h, nw, w_in, cosf, sinf)


def _inproj_bwd(dh, h, nw, n, w_in, dq, dk, dv, dg, du):
    lp, d = h.shape
    nck, _, ps = w_in.shape
    rw = dq.shape[1]
    sw = du.shape[1]
    proj = nck * ps
    tm = _tile(lp, 640)
    last = lp // tm - 1

    def gather_dproj(p_sc, dq_ref, dk_ref, dv_ref, dg_ref, du_ref):
        p_sc[:, 0:rw] = dq_ref[...]
        p_sc[:, rw:2 * rw] = dk_ref[...]
        p_sc[:, 2 * rw:3 * rw] = dv_ref[...]
        p_sc[:, 3 * rw:4 * rw] = dg_ref[...]
        p_sc[:, 4 * rw:] = du_ref[...]

    def act_body(dh_ref, h_ref, nw_ref, w_ref, dq_ref, dk_ref, dv_ref, dg_ref, du_ref, dhi_ref, dnw_ref, p_sc):
        i = pl.program_id(0)

        @pl.when(i == 0)
        def _():
            dnw_ref[...] = jnp.zeros_like(dnw_ref)

        gather_dproj(p_sc, dq_ref, dk_ref, dv_ref, dg_ref, du_ref)
        dn = jnp.zeros((tm, d), F32)
        for c in range(nck):
            dn = dn + _dot_nt(p_sc[:, c * ps:(c + 1) * ps], w_ref[c])
        xh, r = _rms_stats(h_ref[...])
        dhi_ref[...] = dh_ref[...] + _rms_bwd(dn, xh, r, nw_ref[...])
        dnw_ref[...] += jnp.sum(dn * xh, axis=0, keepdims=True)

    def w_body(n_ref, dq_ref, dk_ref, dv_ref, dg_ref, du_ref, dw_ref, p_sc, acc_sc):
        i = pl.program_id(0)

        @pl.when(i == 0)
        def _():
            acc_sc[...] = jnp.zeros_like(acc_sc)

        gather_dproj(p_sc, dq_ref, dk_ref, dv_ref, dg_ref, du_ref)
        nv = n_ref[...]
        for c in range(nck):
            acc_sc[c] += _dot_tn(nv, p_sc[:, c * ps:(c + 1) * ps])

        @pl.when(i == last)
        def _():
            dw_ref[...] = acc_sc[...].astype(BF16)

    row = lambda w: pl.BlockSpec((tm, w), lambda i: (i, 0))
    vec = pl.BlockSpec((1, d), lambda i: (0, 0))
    wsp = pl.BlockSpec((nck, d, ps), lambda i: (0, 0, 0))
    dproj_specs = [row(rw), row(rw), row(rw), row(rw), row(sw)]
    dhi, dnw = pl.pallas_call(
        act_body, name="inproj_bwd_act", grid=(lp // tm,),
        in_specs=[row(d), row(d), vec, wsp] + dproj_specs,
        out_specs=(row(d), vec),
        out_shape=(jax.ShapeDtypeStruct((lp, d), F32), jax.ShapeDtypeStruct((1, d), F32)),
        scratch_shapes=[pltpu.VMEM((tm, proj), BF16)],
        compiler_params=_params(("arbitrary",)),
    )(dh, h, nw, w_in, dq, dk, dv, dg, du)
    dw = pl.pallas_call(
        w_body, name="inproj_bwd_w", grid=(lp // tm,),
        in_specs=[row(d)] + dproj_specs,
        out_specs=wsp, out_shape=jax.ShapeDtypeStruct((nck, d, ps), BF16),
        scratch_shapes=[pltpu.VMEM((tm, proj), BF16), pltpu.VMEM((nck, d, ps), F32)],
        compiler_params=_params(("arbitrary",)),
    )(n, dq, dk, dv, dg, du)
    return dhi, dnw, dw


def _retention_tables(rc):
    h = jnp.arange(RET_HEADS, dtype=F32)
    log_g = jnp.log(1.0 - 2.0 ** (-5.0 - h))
    i = jnp.arange(rc)
    diff = i[:, None] - i[None, :]
    dec = jnp.where(diff[None] >= 0,
                    jnp.exp(log_g[:, None, None] * jnp.maximum(diff, 0)[None].astype(F32)), 0.0)
    pos = jnp.arange(rc, dtype=F32)
    wq = jnp.exp(log_g[:, None] * (pos + 1.0)[None])
    wk = jnp.exp(log_g[:, None] * (rc - 1 - pos)[None])
    gch = jnp.exp(log_g * rc)
    ones = jnp.ones((1, 1, HEAD_DIM), F32)
    return (dec, wq[:, :, None] * ones, wk[:, :, None] * ones,
            gch[:, None, None] * jnp.ones((1, 8, HEAD_DIM), F32))


def _head_norm(o):
    mu = jnp.mean(o, axis=-1, keepdims=True)
    oc = o - mu
    r = lax.rsqrt(jnp.mean(oc * oc, axis=-1, keepdims=True) + EPS)
    return oc * r, r


def _ret_fwd(q, k, v, g, rnw, tables):
    lp, rw = q.shape
    heads = rw // HEAD_DIM
    rc = tables[0].shape[1]
    nch = lp // rc
    dec, wq, wk, gch = tables

    def body(q_ref, k_ref, v_ref, g_ref, w_ref, dec_ref, wq_ref, wk_ref, gch_ref,
             o_ref, ret_ref, sp_ref, s_sc):
        n = pl.program_id(0)

        @pl.when(n == 0)
        def _():
            s_sc[...] = jnp.zeros_like(s_sc)

        cols = [slice(hh * HEAD_DIM, (hh + 1) * HEAD_DIM) for hh in range(heads)]
        s_ins = [s_sc[hh] for hh in range(heads)]
        outs = []
        for hh, cs in enumerate(cols):
            qv, kv, vv = q_ref[:, cs], k_ref[:, cs], v_ref[:, cs]
            s_in = s_ins[hh]
            a = _dot_nt(qv, kv) * dec_ref[hh]
            qw = (qv.astype(F32) * wq_ref[hh]).astype(BF16)
            kw = (kv.astype(F32) * wk_ref[hh]).astype(BF16)
            o = _dot(a.astype(BF16), vv) + _dot(qw, s_in.astype(BF16))
            s_new = gch_ref[hh, 0:1, :] * s_in + _dot_tn(kw, vv)
            xh, _ = _head_norm(o)
            gv = g_ref[:, cs]
            outs.append((o, s_new, (gv * _sigmoid(gv) * (xh * w_ref[:, cs])).astype(BF16)))
        for hh, cs in enumerate(cols):
            o, s_new, ret = outs[hh]
            sp_ref[hh, 0] = s_ins[hh]
            s_sc[hh] = s_new
            o_ref[:, cs] = o
            ret_ref[:, cs] = ret

    blk = pl.BlockSpec((rc, rw), lambda n: (n, 0))
    tab = pl.BlockSpec((heads, rc, HEAD_DIM), lambda n: (0, 0, 0))
    dtab = pl.BlockSpec((heads, rc, rc), lambda n: (0, 0, 0))
    return pl.pallas_call(
        body, name="retention_fwd", grid=(nch,),
        in_specs=[blk, blk, blk, blk, pl.BlockSpec((1, rw), lambda n: (0, 0)),
                  dtab, tab, tab, pl.BlockSpec((heads, 8, HEAD_DIM), lambda n: (0, 0, 0))],
        out_specs=(blk, blk, pl.BlockSpec((heads, 1, HEAD_DIM, HEAD_DIM), lambda n: (0, n, 0, 0))),
        out_shape=(jax.ShapeDtypeStruct((lp, rw), F32),
                   jax.ShapeDtypeStruct((lp, rw), BF16),
                   jax.ShapeDtypeStruct((heads, nch, HEAD_DIM, HEAD_DIM), F32)),
        scratch_shapes=[pltpu.VMEM((heads, HEAD_DIM, HEAD_DIM), F32)],
        compiler_params=_params(("arbitrary",)),
    )(q, k, v, g, rnw, dec, wq, wk, gch)


def _ret_bwd(dret, q, k, v, g, o, sprev, rnw, tables, cosf, sinf):
    lp, rw = q.shape
    heads = rw // HEAD_DIM
    rc = tables[0].shape[1]
    nch = lp // rc
    dec, wq, wk, gch = tables
    scale = HEAD_DIM ** -0.5
    half = HEAD_DIM // 2

    def body(dret_ref, q_ref, k_ref, v_ref, g_ref, o_ref, sp_ref, w_ref, dec_ref, wq_ref, wk_ref, gch_ref,
             cos_ref, sin_ref, dq_ref, dk_ref, dv_ref, dg_ref, dw_ref, ds_sc):
        n = pl.program_id(0)

        @pl.when(n == 0)
        def _():
            ds_sc[...] = jnp.zeros_like(ds_sc)
            dw_ref[...] = jnp.zeros_like(dw_ref)

        cosv = cos_ref[...]
        sinv = sin_ref[...]
        cols = [slice(hh * HEAD_DIM, (hh + 1) * HEAD_DIM) for hh in range(heads)]
        ds_ins = [ds_sc[hh] for hh in range(heads)]
        dw_ins = [dw_ref[:, cs] for cs in cols]
        outs = []
        for hh, cs in enumerate(cols):
            qv, kv, vv = q_ref[:, cs], k_ref[:, cs], v_ref[:, cs]
            gv = g_ref[:, cs]
            dr = dret_ref[:, cs]
            w = w_ref[:, cs]
            sg = _sigmoid(gv)
            sil = gv * sg
            xh, r = _head_norm(o_ref[:, cs])
            dgate = (dr * (xh * w) * (sg * (1.0 + gv * (1.0 - sg)))).astype(BF16)
            dyw = dr * sil
            dw_new = dw_ins[hh] + jnp.sum(dyw * xh, axis=0, keepdims=True)
            dxh = dyw * w
            do = r * (dxh - jnp.mean(dxh, axis=-1, keepdims=True)
                      - xh * jnp.mean(dxh * xh, axis=-1, keepdims=True))
            dob = do.astype(BF16)
            dmask = dec_ref[hh]
            wqv = wq_ref[hh]
            wkv = wk_ref[hh]
            a = (_dot_nt(qv, kv) * dmask).astype(BF16)
            da = (_dot_nt(dob, vv) * dmask).astype(BF16)
            qw = (qv.astype(F32) * wqv).astype(BF16)
            kw = (kv.astype(F32) * wkv).astype(BF16)
            s_in = sp_ref[hh, 0].astype(BF16)
            ds = ds_ins[hh]
            dsb = ds.astype(BF16)
            dq = _dot(da, kv) + _dot_nt(dob, s_in) * wqv
            dk = _dot_tn(da, qv) + _dot_nt(vv, dsb) * wkv
            dv = _dot_tn(a, dob) + _dot(kw, dsb)
            ds_new = gch_ref[hh, 0:1, :] * ds + _dot_tn(qw, dob)
            outs.append((dgate, dw_new, ds_new,
                         (dq * cosv + pltpu.roll(dq * sinv, half, 1)).astype(BF16),
                         ((dk * cosv + pltpu.roll(dk * sinv, half, 1)) * scale).astype(BF16),
                         dv.astype(BF16)))
        for hh, cs in enumerate(cols):
            dgate, dw_new, ds_new, dqv, dkv, dvv = outs[hh]
            dg_ref[:, cs] = dgate
            dw_ref[:, cs] = dw_new
            ds_sc[hh] = ds_new
            dq_ref[:, cs] = dqv
            dk_ref[:, cs] = dkv
            dv_ref[:, cs] = dvv

    blk = pl.BlockSpec((rc, rw), lambda n: (nch - 1 - n, 0))
    tab = pl.BlockSpec((heads, rc, HEAD_DIM), lambda n: (0, 0, 0))
    dtab = pl.BlockSpec((heads, rc, rc), lambda n: (0, 0, 0))
    wsp = pl.BlockSpec((1, rw), lambda n: (0, 0))
    pos = pl.BlockSpec((rc, HEAD_DIM), lambda n: (nch - 1 - n, 0))
    bshape = jax.ShapeDtypeStruct((lp, rw), BF16)
    return pl.pallas_call(
        body, name="retention_bwd", grid=(nch,),
        in_specs=[blk, blk, blk, blk, blk, blk,
                  pl.BlockSpec((heads, 1, HEAD_DIM, HEAD_DIM), lambda n: (0, nch - 1 - n, 0, 0)),
                  wsp, dtab, tab, tab, pl.BlockSpec((heads, 8, HEAD_DIM), lambda n: (0, 0, 0)), pos, pos],
        out_specs=(blk, blk, blk, blk, wsp),
        out_shape=(bshape, bshape, bshape, bshape, jax.ShapeDtypeStruct((1, rw), F32)),
        scratch_shapes=[pltpu.VMEM((heads, HEAD_DIM, HEAD_DIM), F32)],
        compiler_params=_params(("arbitrary",)),
    )(dret, q, k, v, g, o, sprev, rnw, dec, wq, wk, gch, cosf, sinf)


SCAN_CW = 512


def _s5_prepare(lam_re, lam_im, log_dt, b_re, b_im):
    dt = jnp.exp(log_dt)[:, None]
    er = jnp.exp(lam_re * dt)
    ar = er * jnp.cos(lam_im * dt)
    ai = er * jnp.sin(lam_im * dt)
    den = lam_re * lam_re + lam_im * lam_im
    fr = ((ar - 1.0) * lam_re + ai * lam_im) / den
    fi = (ai * lam_re - (ar - 1.0) * lam_im) / den
    bbr = fr[..., None] * b_re - fi[..., None] * b_im
    bbi = fr[..., None] * b_im + fi[..., None] * b_re
    return ar, ai, bbr, bbi


def _blockdiag_in(t):
    g, p, n = t.shape
    gs = g // N_SEC
    t = t.reshape(N_SEC, gs, p, n)
    eye = jnp.eye(gs, dtype=t.dtype)
    return jnp.einsum("sgpn,gh->sgphn", t, eye).reshape(N_SEC, gs * p, gs * n)


def _blockdiag_out(m, g, p, n):
    gs = g // N_SEC
    m = m.reshape(N_SEC, gs, p, gs, n)
    eye = jnp.eye(gs, dtype=m.dtype)
    return jnp.einsum("sgphn,gh->sgpn", m, eye).reshape(g, p, n)


def _scan_step(xr_ref, xi_ref, r0, prev, ar_ref, ai_ref, conj, ncols):
    new = []
    for cc in range(ncols // SCAN_CW):
        cs = pl.ds(cc * SCAN_CW, SCAN_CW)
        pr, pi = prev[cc]
        ar = ar_ref[:, cs]
        ai = ai_ref[:, cs]
        if conj:
            nr = ar * pr + ai * pi
            ni = ar * pi - ai * pr
        else:
            nr = ar * pr - ai * pi
            ni = ar * pi + ai * pr
        xr = xr_ref[pl.ds(r0, 8), cs] + nr
        xi = xi_ref[pl.ds(r0, 8), cs] + ni
        xr_ref[pl.ds(r0, 8), cs] = xr
        xi_ref[pl.ds(r0, 8), cs] = xi
        new.append((xr, xi))
    return new


def _scan_chunks(ncols):
    return [pl.ds(cc * SCAN_CW, SCAN_CW) for cc in range(ncols // SCAN_CW)]


def _flat(pairs):
    return tuple(t for p in pairs for t in p)


def _pairs(flat):
    return [(flat[2 * k], flat[2 * k + 1]) for k in range(len(flat) // 2)]


def _shift_rows(z, down):
    row = lax.broadcasted_iota(jnp.int32, z.shape, 0)
    if down:
        return jnp.where(row == 0, 0.0, pltpu.roll(z, 1, 0))
    return jnp.where(row == N_SEG - 1, 0.0, pltpu.roll(z, N_SEG - 1, 0))


def _s5_fwd(u, bsr, bsi, csr, csi, a8r, a8i, al8r, al8i, d, gluw, glub, nw, jb):
    lp, sw = u.shape
    ns = a8r.shape[1]
    rows = N_SEG * jb
    nblk = lp // rows
    secw = sw // N_SEC
    secn = ns // N_SEC

    def local_scan(u_ref, bsr_ref, bsi_ref, ar_ref, ai_ref, xr_ref, xi_ref, pr_sc, pi_sc):
        for s in range(N_SEC):
            ub = u_ref[:, s * secw:(s + 1) * secw].astype(BF16)
            xr_ref[:, s * secn:(s + 1) * secn] = _dot(ub, bsr_ref[s])
            xi_ref[:, s * secn:(s + 1) * secn] = _dot(ub, bsi_ref[s])
        prev = [(pr_sc[:, cs], pi_sc[:, cs]) for cs in _scan_chunks(ns)]
        prev = _scan_step(xr_ref, xi_ref, 0, prev, ar_ref, ai_ref, False, ns)

        def step(j, carry):
            r0 = pl.multiple_of(j * 8, 8)
            return _flat(_scan_step(xr_ref, xi_ref, r0, _pairs(carry), ar_ref, ai_ref, False, ns))

        last = _pairs(lax.fori_loop(1, jb, step, _flat(prev)))
        for cs, (vr, vi) in zip(_scan_chunks(ns), last):
            pr_sc[:, cs] = vr
            pi_sc[:, cs] = vi

    def carry_body(u_ref, bsr_ref, bsi_ref, ar_ref, ai_ref, alr_ref, ali_ref, cr_ref, ci_ref,
                   xr_sc, xi_sc, pr_sc, pi_sc):
        b = pl.program_id(0)

        @pl.when(b == 0)
        def _():
            pr_sc[...] = jnp.zeros_like(pr_sc)
            pi_sc[...] = jnp.zeros_like(pi_sc)

        local_scan(u_ref, bsr_ref, bsi_ref, ar_ref, ai_ref, xr_sc, xi_sc, pr_sc, pi_sc)

        @pl.when(b == nblk - 1)
        def _():
            er = _shift_rows(pr_sc[...], True)
            ei = _shift_rows(pi_sc[...], True)
            alr, ali = alr_ref[...], ali_ref[...]
            cr, ci = er, ei
            for _ in range(N_SEG - 2):
                sr = _shift_rows(cr, True)
                si = _shift_rows(ci, True)
                cr = er + alr * sr - ali * si
                ci = ei + alr * si + ali * sr
            cr_ref[...] = cr
            ci_ref[...] = ci

    ublk = pl.BlockSpec((rows, sw), lambda b: (b, 0))
    bspec = pl.BlockSpec((N_SEC, secw, secn), lambda b: (0, 0, 0))
    cspec = pl.BlockSpec((N_SEC, secn, secw), lambda b: (0, 0, 0))
    s8 = pl.BlockSpec((N_SEG, ns), lambda b: (0, 0))
    vec = pl.BlockSpec((1, sw), lambda b: (0, 0))
    s8shape = jax.ShapeDtypeStruct((N_SEG, ns), F32)
    c0r, c0i = pl.pallas_call(
        carry_body, name="s5_fwd_carry", grid=(nblk,),
        in_specs=[ublk, bspec, bspec, s8, s8, s8, s8],
        out_specs=(s8, s8), out_shape=(s8shape, s8shape),
        scratch_shapes=[pltpu.VMEM((rows, ns), F32), pltpu.VMEM((rows, ns), F32),
                        pltpu.VMEM((N_SEG, ns), F32), pltpu.VMEM((N_SEG, ns), F32)],
        compiler_params=_params(("arbitrary",)),
    )(u, bsr, bsi, a8r, a8i, al8r, al8i)

    def main_body(u_ref, bsr_ref, bsi_ref, csr_ref, csi_ref, ar_ref, ai_ref, c0r_ref, c0i_ref,
                  d_ref, gw_ref, gb_ref, nw_ref, xr_ref, xi_ref, yp_ref, out_ref, pr_sc, pi_sc):
        b = pl.program_id(0)

        @pl.when(b == 0)
        def _():
            pr_sc[...] = c0r_ref[...]
            pi_sc[...] = c0i_ref[...]

        local_scan(u_ref, bsr_ref, bsi_ref, ar_ref, ai_ref, xr_ref, xi_ref, pr_sc, pi_sc)
        for s in range(N_SEC):
            xs = pl.ds(s * secn, secn)
            us = pl.ds(s * secw, secw)
            y = _dot(xr_ref[:, xs].astype(BF16), csr_ref[s]) + _dot(xi_ref[:, xs].astype(BF16), csi_ref[s])
            yp_ref[:, us] = y + d_ref[:, us] * u_ref[:, us]
        yp = yp_ref[...]
        t = jnp.tanh(GELU_K0 * (yp + GELU_K1 * yp * yp * yp))
        y1 = 0.5 * yp * (1.0 + t)
        z = _dot(y1.astype(BF16), gw_ref[...]) + gb_ref[...]
        y2 = y1 * _sigmoid(z)
        xh, _ = _rms_stats(y2)
        out_ref[...] = (xh * nw_ref[...]).astype(BF16)

    xblk = pl.BlockSpec((rows, ns), lambda b: (b, 0))
    xr, xi, yp, out = pl.pallas_call(
        main_body, name="s5_fwd", grid=(nblk,),
        in_specs=[ublk, bspec, bspec, cspec, cspec, s8, s8, s8, s8, vec,
                  pl.BlockSpec((sw, sw), lambda b: (0, 0)), vec, vec],
        out_specs=(xblk, xblk, ublk, ublk),
        out_shape=(jax.ShapeDtypeStruct((lp, ns), F32), jax.ShapeDtypeStruct((lp, ns), F32),
                   jax.ShapeDtypeStruct((lp, sw), F32), jax.ShapeDtypeStruct((lp, sw), BF16)),
        scratch_shapes=[pltpu.VMEM((N_SEG, ns), F32), pltpu.VMEM((N_SEG, ns), F32)],
        compiler_params=_params(("arbitrary",)),
    )(u, bsr, bsi, csr, csi, a8r, a8i, c0r, c0i, d, gluw, glub, nw)
    return xr, xi, c0r, c0i, yp, out


def _s5_bwd(dout, u, yp, xr, xi, c0r, c0i, bsrt, bsit, csrt, csit, a8r, a8i, al8r, al8i, d, gluw, glub, nw, jb):
    lp, sw = u.shape
    ns = a8r.shape[1]
    rows = N_SEG * jb
    nblk = lp // rows
    secw = sw // N_SEC
    secn = ns // N_SEC

    def rowwise_bwd(dout_ref, yp_ref, gw_ref, gb_ref, nw_ref):
        ypv = yp_ref[...]
        t = jnp.tanh(GELU_K0 * (ypv + GELU_K1 * ypv * ypv * ypv))
        y1 = 0.5 * ypv * (1.0 + t)
        dgelu = 0.5 * (1.0 + t) + 0.5 * ypv * (1.0 - t * t) * GELU_K0 * (1.0 + 3.0 * GELU_K1 * ypv * ypv)
        gw = gw_ref[...]
        y1b = y1.astype(BF16)
        sg = _sigmoid(_dot(y1b, gw) + gb_ref[...])
        xh, r = _rms_stats(y1 * sg)
        dov = dout_ref[...]
        dy2 = _rms_bwd(dov, xh, r, nw_ref[...])
        dz = dy2 * y1 * sg * (1.0 - sg)
        dzb = dz.astype(BF16)
        dy1 = dy2 * sg + _dot_nt(dzb, gw)
        return dy1 * dgelu, dov * xh, y1b, dzb, dz

    def lam_scan(dyp_of, csrt_ref, csit_ref, ar_ref, ai_ref, lr_sc, li_sc, nr_sc, ni_sc, extra):
        for s in range(N_SEC):
            db = dyp_of(s)
            lr_sc[:, s * secn:(s + 1) * secn] = _dot(db, csrt_ref[s])
            li_sc[:, s * secn:(s + 1) * secn] = _dot(db, csit_ref[s])
        top = rows - 8
        prev = [(nr_sc[:, cs], ni_sc[:, cs]) for cs in _scan_chunks(ns)]
        prev = _scan_step(lr_sc, li_sc, top, prev, ar_ref, ai_ref, True, ns)
        extra(top, pl.ds(top - 8, 8))

        def step(jj, carry):
            r0 = pl.multiple_of((jb - 1 - jj) * 8, 8)
            rp = pl.multiple_of((jb - 2 - jj) * 8, 8)
            new = _scan_step(lr_sc, li_sc, r0, _pairs(carry), ar_ref, ai_ref, True, ns)
            extra(r0, pl.ds(rp, 8))
            return _flat(new)

        prev = _pairs(lax.fori_loop(1, jb - 1, step, _flat(prev)))
        last = _scan_step(lr_sc, li_sc, 0, prev, ar_ref, ai_ref, True, ns)
        extra(0, None)
        for cs, (vr, vi) in zip(_scan_chunks(ns), last):
            nr_sc[:, cs] = vr
            ni_sc[:, cs] = vi

    def carry_body(dout_ref, yp_ref, u_ref, gw_ref, gb_ref, nw_ref, csrt_ref, csit_ref, ar_ref, ai_ref,
                   alr_ref, ali_ref, cr_ref, ci_ref, dyp_ref, dnw_ref, dgw_ref, dgb_ref, dd_ref,
                   lr_sc, li_sc, nr_sc, ni_sc):
        b = pl.program_id(0)

        @pl.when(b == 0)
        def _():
            nr_sc[...] = jnp.zeros_like(nr_sc)
            ni_sc[...] = jnp.zeros_like(ni_sc)
            for ref in (dnw_ref, dgw_ref, dgb_ref, dd_ref):
                ref[...] = jnp.zeros_like(ref)

        dyp, dnw_rows, y1b, dzb, dz = rowwise_bwd(dout_ref, yp_ref, gw_ref, gb_ref, nw_ref)
        dnw_ref[...] += jnp.sum(dnw_rows, axis=0, keepdims=True)
        dgw_ref[...] += _dot_tn(y1b, dzb)
        dgb_ref[...] += jnp.sum(dz, axis=0, keepdims=True)
        dd_ref[...] += jnp.sum(dyp * u_ref[...], axis=0, keepdims=True)
        dyp_ref[...] = dyp.astype(BF16)
        lam_scan(lambda s: dyp_ref[:, s * secw:(s + 1) * secw], csrt_ref, csit_ref, ar_ref, ai_ref,
                 lr_sc, li_sc, nr_sc, ni_sc, lambda r0, prev_rows: None)

        @pl.when(b == nblk - 1)
        def _():
            fr = _shift_rows(nr_sc[...], False)
            fi = _shift_rows(ni_sc[...], False)
            alr, ali = alr_ref[...], ali_ref[...]
            cr, ci = fr, fi
            for _ in range(N_SEG - 2):
                sr = _shift_rows(cr, False)
                si = _shift_rows(ci, False)
                cr = fr + alr * sr + ali * si
                ci = fi + alr * si - ali * sr
            cr_ref[...] = cr
            ci_ref[...] = ci

    rev = lambda b: (nblk - 1 - b, 0)
    ublk = pl.BlockSpec((rows, sw), rev)
    xblk = pl.BlockSpec((rows, ns), rev)
    s8 = pl.BlockSpec((N_SEG, ns), lambda b: (0, 0))
    vec = pl.BlockSpec((1, sw), lambda b: (0, 0))
    gws = pl.BlockSpec((sw, sw), lambda b: (0, 0))
    btspec = pl.BlockSpec((N_SEC, secn, secw), lambda b: (0, 0, 0))
    ctspec = pl.BlockSpec((N_SEC, secw, secn), lambda b: (0, 0, 0))
    s8shape = jax.ShapeDtypeStruct((N_SEG, ns), F32)
    lcr, lci, dyp_all, d_nw, d_gw, d_gb, d_d = pl.pallas_call(
        carry_body, name="s5_bwd_carry", grid=(nblk,),
        in_specs=[ublk, ublk, ublk, gws, vec, vec, ctspec, ctspec, s8, s8, s8, s8],
        out_specs=(s8, s8, ublk, vec, gws, vec, vec),
        out_shape=(s8shape, s8shape, jax.ShapeDtypeStruct((lp, sw), BF16), jax.ShapeDtypeStruct((1, sw), F32),
                   jax.ShapeDtypeStruct((sw, sw), F32), jax.ShapeDtypeStruct((1, sw), F32),
                   jax.ShapeDtypeStruct((1, sw), F32)),
        scratch_shapes=[pltpu.VMEM((rows, ns), F32), pltpu.VMEM((rows, ns), F32),
                        pltpu.VMEM((N_SEG, ns), F32), pltpu.VMEM((N_SEG, ns), F32)],
        compiler_params=_params(("arbitrary",)),
    )(dout, yp, u, gluw, glub, nw, csrt, csit, a8r, a8i, al8r, al8i)

    def main_body(dyp_sc, u_ref, xr_ref, xi_ref, xtr_ref, xti_ref, c0r_ref, c0i_ref, lcr_ref, lci_ref,
                  d_ref, bsrt_ref, bsit_ref, csrt_ref, csit_ref, ar_ref, ai_ref,
                  du_ref, dcr_ref, dci_ref, dbr_ref, dbi_ref, dar_ref, dai_ref,
                  lr_sc, li_sc, nr_sc, ni_sc):
        b = pl.program_id(0)

        @pl.when(b == 0)
        def _():
            nr_sc[...] = lcr_ref[...]
            ni_sc[...] = lci_ref[...]
            for ref in (dcr_ref, dci_ref, dbr_ref, dbi_ref, dar_ref, dai_ref):
                ref[...] = jnp.zeros_like(ref)

        for s in range(N_SEC):
            db = dyp_sc[:, s * secw:(s + 1) * secw]
            xs = pl.ds(s * secn, secn)
            dcr_ref[s] += _dot_tn(xr_ref[:, xs].astype(BF16), db)
            dci_ref[s] += _dot_tn(xi_ref[:, xs].astype(BF16), db)

        first = b == nblk - 1

        def acc_da(r0, prev_rows):
            for cc in range(ns // SCAN_CW):
                cs = pl.ds(cc * SCAN_CW, SCAN_CW)
                lr = lr_sc[pl.ds(r0, 8), cs]
                li = li_sc[pl.ds(r0, 8), cs]
                if prev_rows is None:
                    xpr = jnp.where(first, c0r_ref[:, cs], xtr_ref[:, cs])
                    xpi = jnp.where(first, c0i_ref[:, cs], xti_ref[:, cs])
                else:
                    xpr = xr_ref[prev_rows, cs]
                    xpi = xi_ref[prev_rows, cs]
                dar_ref[:, cs] += lr * xpr + li * xpi
                dai_ref[:, cs] += li * xpr - lr * xpi

        lam_scan(lambda s: dyp_sc[:, s * secw:(s + 1) * secw], csrt_ref, csit_ref, ar_ref, ai_ref,
                 lr_sc, li_sc, nr_sc, ni_sc, acc_da)

        for s in range(N_SEC):
            xs = pl.ds(s * secn, secn)
            us = pl.ds(s * secw, secw)
            lrb = lr_sc[:, xs].astype(BF16)
            lib = li_sc[:, xs].astype(BF16)
            du = _dot(lrb, bsrt_ref[s]) + _dot(lib, bsit_ref[s]) + d_ref[:, us] * dyp_sc[:, us].astype(F32)
            du_ref[:, us] = du.astype(BF16)
            ub = u_ref[:, us].astype(BF16)
            dbr_ref[s] += _dot_tn(ub, lrb)
            dbi_ref[s] += _dot_tn(ub, lib)

    tail = pl.BlockSpec((N_SEG, ns), lambda b: (jnp.maximum((nblk - 1 - b) * jb - 1, 0), 0))
    acc_c = pl.BlockSpec((N_SEC, secn, secw), lambda b: (0, 0, 0))
    acc_b = pl.BlockSpec((N_SEC, secw, secn), lambda b: (0, 0, 0))
    du, dcr, dci, dbr, dbi, dar, dai = pl.pallas_call(
        main_body, name="s5_bwd", grid=(nblk,),
        in_specs=[ublk, ublk, xblk, xblk, tail, tail, s8, s8, s8, s8,
                  vec, btspec, btspec, ctspec, ctspec, s8, s8],
        out_specs=(ublk, acc_c, acc_c, acc_b, acc_b, s8, s8),
        out_shape=(jax.ShapeDtypeStruct((lp, sw), BF16),
                   jax.ShapeDtypeStruct((N_SEC, secn, secw), F32),
                   jax.ShapeDtypeStruct((N_SEC, secn, secw), F32),
                   jax.ShapeDtypeStruct((N_SEC, secw, secn), F32),
                   jax.ShapeDtypeStruct((N_SEC, secw, secn), F32),
                   s8shape, s8shape),
        scratch_shapes=[pltpu.VMEM((rows, ns), F32), pltpu.VMEM((rows, ns), F32),
                        pltpu.VMEM((N_SEG, ns), F32), pltpu.VMEM((N_SEG, ns), F32)],
        compiler_params=_params(("arbitrary",)),
    )(dyp_all, u, xr, xi, xr, xi, c0r, c0i, lcr, lci, d, bsrt, bsit, csrt, csit, a8r, a8i)
    return du, d_nw, d_gw, d_gb, d_d, dcr, dci, dbr, dbi, dar, dai


def _outproj_fwd(h, ret, ssm, wo):
    lp, d = h.shape
    nck, rs, _ = wo.shape
    rw = ret.shape[1]
    tm = _tile(lp, 640)
    per = rw // rs

    def body(h_ref, ret_ref, ssm_ref, w_ref, o_ref):
        acc = h_ref[...]
        for c in range(nck):
            src = ret_ref if c < per else ssm_ref
            lo = (c % per) * rs
            acc = acc + _dot(src[:, lo:lo + rs], w_ref[c])
        o_ref[...] = acc

    row = lambda w: pl.BlockSpec((tm, w), lambda i: (i, 0))
    return pl.pallas_call(
        body, name="outproj_fwd", grid=(lp // tm,),
        in_specs=[row(d), row(rw), row(ssm.shape[1]), pl.BlockSpec((nck, rs, d), lambda i: (0, 0, 0))],
        out_specs=row(d), out_shape=jax.ShapeDtypeStruct((lp, d), F32),
        compiler_params=_params(("arbitrary",)),
    )(h, ret, ssm, wo)


def _outproj_bwd(dh, ret, ssm, wo):
    lp, d = dh.shape
    nck, rs, _ = wo.shape
    rw = ret.shape[1]
    sw = ssm.shape[1]
    tm = _tile(lp, 640)
    per = rw // rs
    last = lp // tm - 1

    def body(dh_ref, ret_ref, ssm_ref, w_ref, dret_ref, dssm_ref, dw_ref, acc_sc):
        i = pl.program_id(0)

        @pl.when(i == 0)
        def _():
            acc_sc[...] = jnp.zeros_like(acc_sc)

        dhb = dh_ref[...].astype(BF16)
        for c in range(nck):
            src, dst = (ret_ref, dret_ref) if c < per else (ssm_ref, dssm_ref)
            lo = (c % per) * rs
            dst[:, lo:lo + rs] = _dot_nt(dhb, w_ref[c])
            acc_sc[c] += _dot_tn(src[:, lo:lo + rs], dhb)

        @pl.when(i == last)
        def _():
            dw_ref[...] = acc_sc[...].astype(BF16)

    row = lambda w: pl.BlockSpec((tm, w), lambda i: (i, 0))
    wsp = pl.BlockSpec((nck, rs, d), lambda i: (0, 0, 0))
    return pl.pallas_call(
        body, name="outproj_bwd", grid=(lp // tm,),
        in_specs=[row(d), row(rw), row(sw), wsp],
        out_specs=(row(rw), row(sw), wsp),
        out_shape=(jax.ShapeDtypeStruct((lp, rw), F32), jax.ShapeDtypeStruct((lp, sw), F32),
                   jax.ShapeDtypeStruct((nck, rs, d), BF16)),
        scratch_shapes=[pltpu.VMEM((nck, rs, d), F32)],
        compiler_params=_params(("arbitrary",)),
    )(dh, ret, ssm, wo)


def _pack(arrs):
    flat = jnp.concatenate([a.reshape(-1).astype(F32) for a in arrs])
    n = flat.shape[0]
    rows = -(-n // (8 * LANE)) * 8
    return jnp.pad(flat, (0, rows * LANE - n)).reshape(rows, LANE)


def _unpack(packed, shapes):
    flat = packed.reshape(-1)
    out, off = [], 0
    for s in shapes:
        n = math.prod(s)
        out.append(flat[off:off + n].reshape(s))
        off += n
    return out


def _to_segments(a, seg_len):
    return a.reshape(N_SEG, seg_len, a.shape[1]).transpose(1, 0, 2).reshape(a.shape)


def _from_segments(a, seg_len):
    return a.reshape(seg_len, N_SEG, a.shape[1]).transpose(1, 0, 2).reshape(a.shape)


WEIGHT_NAMES = ['meta_tokens', 'ffn1_norm_w', 'ffn1_w_gate', 'ffn1_w_up', 'ffn1_w_down', 'mix_norm_w', 'w_in',
                'ret_norm_w', 'ssm_lambda_re', 'ssm_lambda_im', 'ssm_log_dt', 'ssm_b_re', 'ssm_b_im', 'ssm_c_re',
                'ssm_c_im', 'ssm_d', 'ssm_glu_w', 'ssm_glu_b', 'ssm_norm_w', 'w_out', 'ffn2_norm_w', 'ffn2_w_gate',
                'ffn2_w_up', 'ffn2_w_down', 'final_norm_w']
BIG = ['ffn1_w_gate', 'ffn1_w_up', 'ffn1_w_down', 'w_in', 'ssm_glu_w', 'w_out', 'ffn2_w_gate', 'ffn2_w_up',
       'ffn2_w_down']
TRANSPOSED = ['ffn1_w_gate', 'ffn1_w_up', 'ffn2_w_gate', 'ffn2_w_up']
BIG_EARLY = ['ffn1_w_gate', 'ffn1_w_up', 'ffn1_w_down']
BIG_LATE = [n for n in BIG if n not in BIG_EARLY]
SMALL = [n for n in WEIGHT_NAMES if n not in BIG]


def kernel(x, meta_tokens, ffn1_norm_w, ffn1_w_gate, ffn1_w_up, ffn1_w_down, mix_norm_w, w_in, ret_norm_w, ssm_lambda_re, ssm_lambda_im, ssm_log_dt, ssm_b_re, ssm_b_im, ssm_c_re, ssm_c_im, ssm_d, ssm_glu_w, ssm_glu_b, ssm_norm_w, w_out, ffn2_norm_w, ffn2_w_gate, ffn2_w_up, ffn2_w_down, final_norm_w, loss_target, m_meta_tokens, m_ffn1_norm_w, m_ffn1_w_gate, m_ffn1_w_up, m_ffn1_w_down, m_mix_norm_w, m_w_in, m_ret_norm_w, m_ssm_lambda_re, m_ssm_lambda_im, m_ssm_log_dt, m_ssm_b_re, m_ssm_b_im, m_ssm_c_re, m_ssm_c_im, m_ssm_d, m_ssm_glu_w, m_ssm_glu_b, m_ssm_norm_w, m_w_out, m_ffn2_norm_w, m_ffn2_w_gate, m_ffn2_w_up, m_ffn2_w_down, m_final_norm_w, v_meta_tokens, v_ffn1_norm_w, v_ffn1_w_gate, v_ffn1_w_up, v_ffn1_w_down, v_mix_norm_w, v_w_in, v_ret_norm_w, v_ssm_lambda_re, v_ssm_lambda_im, v_ssm_log_dt, v_ssm_b_re, v_ssm_b_im, v_ssm_c_re, v_ssm_c_im, v_ssm_d, v_ssm_glu_w, v_ssm_glu_b, v_ssm_norm_w, v_w_out, v_ffn2_norm_w, v_ffn2_w_gate, v_ffn2_w_up, v_ffn2_w_down, v_final_norm_w):
    args = locals()
    w = {n: args[n] for n in WEIGHT_NAMES}
    m = {n: args["m_" + n] for n in WEIGHT_NAMES}
    v = {n: args["v_" + n] for n in WEIGHT_NAMES}

    seq, d = x.shape[1], x.shape[2]
    lp = seq + CHUNK
    seg_len = lp // N_SEG
    rw = RET_HEADS * HEAD_DIM
    sw = ssm_d.shape[-1]
    groups = sw // SSM_GROUP
    ns = groups * SSM_STATE
    jb = _tile(seg_len, S5_STEPS, 8)
    chip = 2 * lax.axis_index("x") + lax.axis_index("y")

    as_fd = lambda t: jnp.swapaxes(t, -1, -2)
    shards = {n: (as_fd(w[n][0]) if n in TRANSPOSED else w[n][0]).astype(BF16) for n in BIG}
    early = [shards[n] for n in BIG_EARLY] + [meta_tokens]
    gathered = _gather_two_level("gather_early", early)
    gw = dict(zip(BIG_EARLY, gathered[:-1]))
    meta_full = jnp.transpose(gathered[-1], (1, 0, 2)).reshape(N_META, d)
    late = [shards[n] for n in BIG_LATE]

    freqs = 1.0 / (ROPE_BASE ** (jnp.arange(0, HEAD_DIM, 2, dtype=F32) / HEAD_DIM))
    ang_c = (jnp.arange(lp // CHUNK, dtype=F32) * CHUNK - float(CHUNK - N_META))[:, None] * freqs[None, :]
    ang_r = jnp.arange(CHUNK, dtype=F32)[:, None] * freqs[None, :]
    cos_c, sin_c = jnp.cos(ang_c)[:, None, :], jnp.sin(ang_c)[:, None, :]
    cos_r, sin_r = jnp.cos(ang_r)[None], jnp.sin(ang_r)[None]
    cos_t = (cos_c * cos_r - sin_c * sin_r).reshape(lp, HEAD_DIM // 2)
    sin_t = (sin_c * cos_r + cos_c * sin_r).reshape(lp, HEAD_DIM // 2)
    cosf = jnp.concatenate([cos_t, cos_t], axis=1)
    sinf = jnp.concatenate([-sin_t, sin_t], axis=1)
    tables = _retention_tables(_tile(lp, RET_ROWS, CHUNK))

    lam_re, lam_im, log_dt = ssm_lambda_re[0], ssm_lambda_im[0], ssm_log_dt[0]
    b_re, b_im, c_re, c_im = ssm_b_re[0], ssm_b_im[0], ssm_c_re[0], ssm_c_im[0]
    (ar, ai, bbr, bbi), prep_vjp = jax.vjp(_s5_prepare, lam_re, lam_im, log_dt, b_re, b_im)
    dt = jnp.exp(log_dt)[:, None]
    el = jnp.exp(seg_len * lam_re * dt)
    alr = el * jnp.cos(seg_len * lam_im * dt)
    ali = el * jnp.sin(seg_len * lam_im * dt)
    bc8 = lambda t: jnp.broadcast_to(t.reshape(1, ns), (N_SEG, ns))
    a8r, a8i, al8r, al8i = bc8(ar), bc8(ai), bc8(alr), bc8(ali)
    bsr = _blockdiag_in(jnp.transpose(bbr, (0, 2, 1)))
    bsi = _blockdiag_in(jnp.transpose(bbi, (0, 2, 1)))
    csrt = _blockdiag_in(c_re)
    csit = _blockdiag_in(-c_im)
    tr = lambda t: jnp.transpose(t, (0, 2, 1))
    bsr_b, bsi_b = bsr.astype(BF16), bsi.astype(BF16)
    csr_b, csi_b = tr(csrt).astype(BF16), tr(csit).astype(BF16)
    bsrt_b, bsit_b = tr(bsr).astype(BF16), tr(bsi).astype(BF16)
    csrt_b, csit_b = csrt.astype(BF16), csit.astype(BF16)

    h0 = (jnp.concatenate([jnp.zeros((CHUNK - N_META, d), F32), meta_full], axis=0), x[0])
    (h1, g1, u1), late_half = _ffn_fwd("ffn1_fwd", h0, ffn1_norm_w, gw['ffn1_w_gate'], gw['ffn1_w_up'],
                                       gw['ffn1_w_down'], _allgather_chips_plan(late), late)
    gw.update(zip(BIG_LATE, _forward_sibling("gather_late_forward", late_half)))
    glu_full = gw['ssm_glu_w'].reshape(sw, sw)
    n2, q, k, vv, gate, u = _inproj_fwd(h1, mix_norm_w, gw['w_in'], cosf, sinf, rw)
    o, ret, sprev = _ret_fwd(q, k, vv, gate, ret_norm_w, tables)
    u_seg = _to_segments(u, seg_len)
    xr, xi, c0r, c0i, yp, ssm_seg = _s5_fwd(u_seg, bsr_b, bsi_b, csr_b, csi_b, a8r, a8i, al8r, al8i,
                                            ssm_d, glu_full, ssm_glu_b, ssm_norm_w, jb)
    ssm = _from_segments(ssm_seg, seg_len)
    h2 = _outproj_fwd(h1, ret, ssm, gw['w_out'])
    (dh3, g2, u2, loss_part, d_final), _ = _ffn_fwd(
        "ffn2_fwd_loss", h2, ffn2_norm_w, gw['ffn2_w_gate'], gw['ffn2_w_up'], gw['ffn2_w_down'],
        loss=(final_norm_w.reshape(1, d), loss_target[0]))

    (dh2, d_ffn2_norm, nb, daccb, ab, dgb, dub), _ = _ffn_bwd_act(
        "ffn2_bwd_act", dh3, h2, ffn2_norm_w, g2, u2, gw['ffn2_w_gate'], gw['ffn2_w_up'], gw['ffn2_w_down'])
    (dwg2, dwu2, dwd2), _ = _ffn_bwd_w("ffn2_bwd_w", nb, daccb, ab, dgb, dub)
    dret, dssm, dwo = _outproj_bwd(dh2, ret, ssm, gw['w_out'])
    (du_seg, d_ssm_norm, d_glu_w, d_glu_b, d_ssm_d, dcr_s, dci_s, dbr_s, dbi_s, dar8, dai8) = _s5_bwd(
        _to_segments(dssm, seg_len), u_seg, yp, xr, xi, c0r, c0i, bsrt_b, bsit_b, csrt_b, csit_b,
        a8r, a8i, al8r, al8i, ssm_d, glu_full, ssm_glu_b, ssm_norm_w, jb)
    du = _from_segments(du_seg, seg_len)
    dq, dk, dv, dgate, d_ret_norm = _ret_bwd(dret, q, k, vv, gate, o, sprev, ret_norm_w, tables, cosf, sinf)
    dh1, d_mix_norm, dwin = _inproj_bwd(dh2, h1, mix_norm_w, n2, gw['w_in'], dq, dk, dv, dgate, du)
    late_parts = {
        'w_in': dwin, 'ssm_glu_w': d_glu_w.reshape(N_CHIP, sw // N_CHIP, sw).astype(BF16), 'w_out': dwo,
        'ffn2_w_gate': dwg2, 'ffn2_w_up': dwu2, 'ffn2_w_down': dwd2,
    }
    late_list = [late_parts[n] for n in BIG_LATE]
    (dh0, d_ffn1_norm, nb, daccb, ab, dgb, dub), late_recv = _ffn_bwd_act(
        "ffn1_bwd_act", dh1, h0, ffn1_norm_w, g1, u1, gw['ffn1_w_gate'], gw['ffn1_w_up'], gw['ffn1_w_down'],
        _alltoall_chips_plan(late_list), late_list)
    grad_x = dh0[CHUNK:][None]
    d_meta = dh0[CHUNK - N_META:CHUNK]

    d_c_re = _blockdiag_out(tr(dcr_s), groups, SSM_GROUP, SSM_STATE)
    d_c_im = -_blockdiag_out(tr(dci_s), groups, SSM_GROUP, SSM_STATE)
    d_bbr = jnp.transpose(_blockdiag_out(dbr_s, groups, SSM_GROUP, SSM_STATE), (0, 2, 1))
    d_bbi = jnp.transpose(_blockdiag_out(dbi_s, groups, SSM_GROUP, SSM_STATE), (0, 2, 1))
    d_ar = jnp.sum(dar8, axis=0).reshape(groups, SSM_STATE)
    d_ai = jnp.sum(dai8, axis=0).reshape(groups, SSM_STATE)
    small_parts = [loss_part[0:1, :], d_meta, d_ffn1_norm, d_mix_norm, d_ret_norm, d_ar, d_ai, d_bbr, d_bbi,
                   d_c_re, d_c_im, d_ssm_d, d_glu_b, d_ssm_norm, d_ffn2_norm, d_final]
    small_shapes = [a.shape for a in small_parts]
    packed = _pack(small_parts)
    early_recv, (all_parts,) = _ffn_bwd_w_scatter("ffn1_bwd_w", nb, daccb, ab, dgb, dub, chip,
                                                  _allgather_all_plan([packed]), [packed])
    received = dict(zip(BIG_LATE + BIG_EARLY, late_recv + early_recv))
    chip_sums = _sum_slots("sum_chips", [received[n] for n in BIG], BF16)
    sib_sums = _swap_sibling("swap_sibling", chip_sums)
    (loss_row, g_meta_full, g_ffn1_norm, g_mix_norm, g_ret_norm, g_ar, g_ai, g_bbr, g_bbi, g_c_re, g_c_im,
     g_ssm_d, g_glu_b, g_ssm_norm, g_ffn2_norm, g_final) = _unpack(_sum_slots("sum_small", [all_parts], F32)[0],
                                                                  small_shapes)
    g_lam_re, g_lam_im, g_log_dt, g_b_re, g_b_im = prep_vjp((g_ar, g_ai, g_bbr, g_bbi))
    loss = loss_row[0, 0]
    g_meta = lax.dynamic_slice(g_meta_full, (0, chip * (d // N_CHIP)), (N_META, d // N_CHIP))
    small_grads = {
        'meta_tokens': g_meta, 'ffn1_norm_w': g_ffn1_norm, 'mix_norm_w': g_mix_norm, 'ret_norm_w': g_ret_norm,
        'ssm_lambda_re': g_lam_re[None], 'ssm_lambda_im': g_lam_im[None], 'ssm_log_dt': g_log_dt[None],
        'ssm_b_re': g_b_re[None], 'ssm_b_im': g_b_im[None], 'ssm_c_re': g_c_re[None], 'ssm_c_im': g_c_im[None],
        'ssm_d': g_ssm_d, 'ssm_glu_b': g_glu_b, 'ssm_norm_w': g_ssm_norm, 'ffn2_norm_w': g_ffn2_norm,
        'final_norm_w': g_final.reshape(d),
    }

    grads, deltas, new_m, new_v = {}, {}, {}, {}
    g_pair = {n: [mine, sib] for n, mine, sib in zip(BIG, chip_sums, sib_sums)}
    view = lambda n, t: as_fd(t) if n in TRANSPOSED else t
    big_out = _adam("adam_big", [(view(n, w[n]), view(n, m[n]), view(n, v[n])) for n in BIG], [g_pair[n] for n in BIG])
    for n, outs in zip(BIG, big_out):
        grads[n], deltas[n], new_m[n], new_v[n] = [view(n, t) for t in outs]
    sm_shapes = [w[n].shape for n in SMALL]
    sm_out = _adam("adam_small", [(_pack([w[n] for n in SMALL]), _pack([m[n] for n in SMALL]),
                                  _pack([v[n] for n in SMALL]))],
                   [[_pack([small_grads[n].reshape(w[n].shape) for n in SMALL])]])[0]
    for dst, packed in zip((grads, deltas, new_m, new_v), sm_out):
        for n, t in zip(SMALL, _unpack(packed, sm_shapes)):
            dst[n] = t

    return (loss, grad_x, *[grads[n] for n in WEIGHT_NAMES], *[deltas[n] for n in WEIGHT_NAMES],
            *[new_m[n] for n in WEIGHT_NAMES], *[new_v[n] for n in WEIGHT_NAMES])
```

```python
import functools
import math

import jax
import jax.numpy as jnp
from jax import lax
from jax.experimental import pallas as pl
from jax.experimental.pallas import tpu as pltpu

N_META = 16
RET_HEADS = 4
HEAD_DIM = 128
SSM_GROUP = 16
SSM_STATE = 64
CHUNK = 128
ROPE_BASE = 10000.0
EPS = 1e-6
FFN_RES = 0.5
N_SEG = 8
N_SEC = 4
N_CHIP = 4
LANE = 128
FFN_CPS = 2
BWD_W_ROWS = 1664

ADAM_LR = 0.001
ADAM_B1 = 0.9
ADAM_B2 = 0.999
ADAM_EPS = 1e-08
ADAM_WD = 0.01
ADAM_STEP = 10

VMEM_LIMIT = 56 * 1024 * 1024

F32 = jnp.float32
BF16 = jnp.bfloat16
MESH = pl.DeviceIdType.MESH


def _dot(a, b):
    return jnp.dot(a, b, preferred_element_type=F32)


def _dot_nt(a, b):
    return lax.dot_general(a, b, (((1,), (1,)), ((), ())), preferred_element_type=F32)


def _dot_tn(a, b):
    return lax.dot_general(a, b, (((0,), (0,)), ((), ())), preferred_element_type=F32)


def _tile(n, target, mult=64):
    best = None
    t = mult
    while t <= min(n, target):
        if n % t == 0:
            best = t
        t += mult
    assert best is not None, (n, target)
    return best


def _params(sem, vmem=VMEM_LIMIT):
    return pltpu.CompilerParams(dimension_semantics=sem, vmem_limit_bytes=vmem)


def _rms_stats(xf):
    r = lax.rsqrt(jnp.mean(xf * xf, axis=-1, keepdims=True) + EPS)
    return xf * r, r


def _rms_bwd(dy, xh, r, w):
    dxh = dy * w
    return r * (dxh - xh * jnp.mean(dxh * xh, axis=-1, keepdims=True))


def _sigmoid(x):
    return 0.5 * jnp.tanh(0.5 * x) + 0.5


GELU_K0 = math.sqrt(2.0 / math.pi)
GELU_K1 = 0.044715


CHIP_MASKS = [(1, 0, 0), (0, 1, 0), (1, 1, 0)]
ALL_MASKS = [(0, 0, 1), (0, 1, 0), (0, 1, 1), (1, 0, 0), (1, 0, 1), (1, 1, 0), (1, 1, 1)]
SIB_MASKS = [(0, 0, 1)]
ANY_SPEC = pl.BlockSpec(memory_space=pl.ANY)
MULTI_SUM_STEPS = 4
MULTI_ADAM_STEPS = 8


class _Plan:
    def __init__(self, arrays, masks, n_slots, src_slotted, dst_slotted, local_copy, half=False, forward=False):
        self.shapes = [(a.shape, a.dtype) for a in arrays]
        self.n = len(arrays)
        self.masks = masks
        self.n_slots = n_slots
        self.src_slotted, self.dst_slotted, self.local_copy = src_slotted, dst_slotted, local_copy
        self.half, self.forward = half, forward
        self.n_cp = self.n * len(masks) * (len(CHIP_MASKS) if forward else 1)

    def out_shape(self):
        out = []
        for shp, dt in self.shapes:
            if self.dst_slotted and not self.src_slotted:
                shp = (self.n_slots,) + shp
            elif self.src_slotted and not self.dst_slotted:
                shp = shp[1:]
            out.append(jax.ShapeDtypeStruct(shp, dt))
        return tuple(out)

    def scratch(self):
        return [pltpu.SemaphoreType.DMA((self.n_cp,)), pltpu.SemaphoreType.DMA((self.n_cp,)),
                pltpu.SemaphoreType.DMA((self.n,))]

    def _slot(self, px, py, pc):
        if self.n_slots == 8:
            return 4 * px + 2 * py + pc
        if self.n_slots == 4:
            return 2 * px + py
        return pc

    def copies(self, ins, outs, sems):
        send_sems, recv_sems, loc_sems = sems
        x, y, c = lax.axis_index("x"), lax.axis_index("y"), lax.axis_index("c")
        me = self._slot(x, y, c)
        n_m = len(self.masks)
        cps = []
        for a in range(self.n):
            if self.forward:
                rows = self.shapes[a][0][-2] // 2
                mine = pl.ds(pl.multiple_of(c * rows, 8), rows)
                for j, (mx, my, _) in enumerate(CHIP_MASKS):
                    blk = outs[a].at[2 * (1 - x if mx else x) + (1 - y if my else y), mine]
                    k = a * len(CHIP_MASKS) + j
                    cps.append(pltpu.make_async_remote_copy(
                        src_ref=blk, dst_ref=blk, send_sem=send_sems.at[k], recv_sem=recv_sems.at[k],
                        device_id=(x, y, 1 - c), device_id_type=MESH))
                continue
            if self.local_copy:
                src = ins[a].at[me] if self.src_slotted else ins[a]
                cps.append(pltpu.make_async_copy(src, outs[a].at[me], loc_sems.at[a]))
            for mi, (mx, my, mc) in enumerate(self.masks):
                px = 1 - x if mx else x
                py = 1 - y if my else y
                pc = 1 - c if mc else c
                src = ins[a].at[self._slot(px, py, pc)] if self.src_slotted else ins[a]
                dst = outs[a].at[me] if self.dst_slotted else outs[a]
                if self.half:
                    rows = src.shape[-2] // 2
                    mine = pl.ds(pl.multiple_of(c * rows, 8), rows)
                    src, dst = src.at[mine], dst.at[mine]
                k = a * n_m + mi
                cps.append(pltpu.make_async_remote_copy(
                    src_ref=src, dst_ref=dst, send_sem=send_sems.at[k], recv_sem=recv_sems.at[k],
                    device_id=(px, py, pc), device_id_type=MESH))
        return cps


def _exchange(name, plan, arrays):
    n = plan.n

    def body(*refs):
        cps = plan.copies(refs[:n], refs[n:2 * n], refs[2 * n:])
        for cp in cps:
            cp.start()
        for cp in cps:
            cp.wait()

    outs = pl.pallas_call(
        body, name=name, out_shape=plan.out_shape(),
        in_specs=[ANY_SPEC] * n, out_specs=tuple([ANY_SPEC] * n), scratch_shapes=plan.scratch(),
        input_output_aliases={i: i for i in range(n)} if plan.forward else {},
    )(*arrays)
    return list(outs)


def _pcall(body, *, name, grid, in_specs, out_specs, out_shape, scratch_shapes, args, plan=None, plan_args=()):
    sem = ("arbitrary",) * len(grid)
    if plan is None:
        return pl.pallas_call(body, name=name, grid=grid, in_specs=in_specs, out_specs=out_specs,
                              out_shape=out_shape, scratch_shapes=scratch_shapes,
                              compiler_params=_params(sem))(*args), []
    n_in, n_out, n_scr, n_p = len(in_specs), len(out_specs), len(scratch_shapes), plan.n

    def wrapped(*refs):
        ins = refs[:n_in]
        p_ins = refs[n_in:n_in + n_p]
        o0 = n_in + n_p
        outs = refs[o0:o0 + n_out]
        p_outs = refs[o0 + n_out:o0 + n_out + n_p]
        s0 = o0 + n_out + n_p
        scr = refs[s0:s0 + n_scr]
        sems = refs[s0 + n_scr:]
        ids = [pl.program_id(i) for i in range(len(grid))]
        first = functools.reduce(jnp.logical_and, [i == 0 for i in ids])
        last = functools.reduce(jnp.logical_and, [i == g - 1 for i, g in zip(ids, grid)])

        @pl.when(first)
        def _():
            for cp in plan.copies(p_ins, p_outs, sems):
                cp.start()

        body(*ins, *outs, *scr)

        @pl.when(last)
        def _():
            for cp in plan.copies(p_ins, p_outs, sems):
                cp.wait()

    res = pl.pallas_call(
        wrapped, name=name, grid=grid,
        in_specs=list(in_specs) + [ANY_SPEC] * n_p,
        out_specs=tuple(out_specs) + (ANY_SPEC,) * n_p,
        out_shape=tuple(out_shape) + plan.out_shape(),
        scratch_shapes=list(scratch_shapes) + plan.scratch(),
        compiler_params=_params(sem),
    )(*args, *plan_args)
    return res[:n_out], list(res[n_out:])


def _allgather_chips_plan(arrays):
    return _Plan(arrays, CHIP_MASKS, 4, False, True, True, half=True)


def _gather_two_level(name, arrays):
    n = len(arrays)
    ici = _allgather_chips_plan(arrays)
    fwd = _Plan(ici.out_shape(), SIB_MASKS, 4, True, True, False, forward=True)
    n_m = len(CHIP_MASKS)

    def body(*refs):
        ins, outs, sems = refs[:n], refs[n:2 * n], refs[2 * n:]
        ici_cps = ici.copies(ins, outs, sems[:3])
        fwd_cps = fwd.copies(None, outs, sems[3:])
        for cp in ici_cps:
            cp.start()
        for a in range(n):
            for m in range(n_m):
                ici_cps[a * (n_m + 1) + 1 + m].wait_recv()
                fwd_cps[a * n_m + m].start()
        for a in range(n):
            ici_cps[a * (n_m + 1)].wait()
            for m in range(n_m):
                ici_cps[a * (n_m + 1) + 1 + m].wait_send()
        for cp in fwd_cps:
            cp.wait()

    return list(pl.pallas_call(
        body, name=name, out_shape=ici.out_shape(),
        in_specs=[ANY_SPEC] * n, out_specs=tuple([ANY_SPEC] * n), scratch_shapes=ici.scratch() + fwd.scratch(),
    )(*arrays))


def _forward_sibling(name, gathered):
    return _exchange(name, _Plan(gathered, SIB_MASKS, 4, True, True, False, forward=True), gathered)


def _alltoall_chips_plan(arrays):
    return _Plan(arrays, CHIP_MASKS, 4, True, True, True)


def _swap_sibling(name, arrays):
    return _exchange(name, _Plan(arrays, SIB_MASKS, 2, False, False, False), arrays)


def _allgather_all_plan(arrays):
    return _Plan(arrays, ALL_MASKS, 8, False, True, True)


def _sum_slots(name, arrs, out_dtype):
    s = arrs[0].shape[0]
    n = len(arrs)
    steps = arrs[0].shape[1] // _tile(arrs[0].shape[1], 512, 8) if n == 1 else MULTI_SUM_STEPS
    for a in arrs:
        assert a.shape[1] % (16 * steps) == 0 or n == 1, a.shape

    def body(*refs):
        for a_ref, o_ref in zip(refs[:n], refs[n:]):
            acc = a_ref[0].astype(F32)
            for i in range(1, s):
                acc = acc + a_ref[i].astype(F32)
            o_ref[...] = acc.astype(out_dtype)

    return list(pl.pallas_call(
        body, name=name, grid=(steps,),
        in_specs=[pl.BlockSpec((s, a.shape[1] // steps, a.shape[2]), lambda i: (0, i, 0)) for a in arrs],
        out_specs=tuple(pl.BlockSpec((a.shape[1] // steps, a.shape[2]), lambda i: (i, 0)) for a in arrs),
        out_shape=tuple(jax.ShapeDtypeStruct(a.shape[1:], out_dtype) for a in arrs),
        compiler_params=_params(("arbitrary",)),
    )(*arrs))


def _adam_math(w, g, m, v):
    m_new = ADAM_B1 * m + (1.0 - ADAM_B1) * g
    v_new = ADAM_B2 * v + (1.0 - ADAM_B2) * (g * g)
    m_hat = m_new / (1.0 - ADAM_B1 ** ADAM_STEP)
    v_hat = v_new / (1.0 - ADAM_B2 ** ADAM_STEP)
    delta = -ADAM_LR * (m_hat / (jnp.sqrt(v_hat) + ADAM_EPS) + ADAM_WD * w)
    return delta, m_new, v_new


def _adam(name, wmv, g_parts):
    n_w = len(wmv)
    n_g = len(g_parts[0])
    lead = wmv[0][0].ndim == 3
    at = (lambda ref: ref.at[0]) if lead else (lambda ref: ref)
    n_in = 3 + n_g
    rows0 = wmv[0][0].shape[-2]
    steps = rows0 // _tile(rows0, 256, 8) if n_w == 1 else MULTI_ADAM_STEPS
    for w, _, _ in wmv:
        assert w.shape[-2] % (8 * steps) == 0, w.shape

    def body(*refs):
        for j in range(n_w):
            ins = refs[j * n_in:(j + 1) * n_in]
            outs = refs[n_w * n_in + 4 * j:n_w * n_in + 4 * j + 4]
            w_ref, m_ref, v_ref = [at(t) for t in ins[:3]]
            g_out, d_out, m_out, v_out = [at(t) for t in outs]
            g = ins[3][...].astype(F32)
            for gr in ins[4:]:
                g = g + gr[...].astype(F32)
            delta, m_new, v_new = _adam_math(w_ref[...], g, m_ref[...], v_ref[...])
            g_out[...] = g
            d_out[...] = delta
            m_out[...] = m_new
            v_out[...] = v_new

    in_specs, out_specs, out_shape, args = [], [], [], []
    for (w, m, v), gp in zip(wmv, g_parts):
        r, c = w.shape[-2:]
        spec = pl.BlockSpec((r // steps, c), lambda i: (i, 0))
        wspec = pl.BlockSpec((1, r // steps, c), lambda i: (0, i, 0)) if lead else spec
        in_specs += [wspec] * 3 + [spec] * n_g
        out_specs += [wspec] * 4
        out_shape += [jax.ShapeDtypeStruct(w.shape, F32)] * 4
        args += [w, m, v, *gp]
    res = pl.pallas_call(
        body, name=name, grid=(steps,),
        in_specs=in_specs, out_specs=tuple(out_specs), out_shape=tuple(out_shape),
        compiler_params=_params(("arbitrary",)),
    )(*args)
    return [tuple(res[4 * j:4 * j + 4]) for j in range(n_w)]


SUB_ROWS = 32
FFN_BWD_ROWS = 416
FFN_FWD_CPS = 4
FFN_FWD_ROWS = 416
FFN_LOSS_ROWS = 416
RET_ROWS = 640
S5_STEPS = 104


def _tile_parts(tm, d, head, x):
    nsub = tm // SUB_ROWS
    off = head.shape[0] // SUB_ROWS
    specs = [pl.BlockSpec(head.shape, lambda i, k: (0, 0))] + [
        pl.BlockSpec((SUB_ROWS, d), lambda i, k, j=j: (jnp.maximum(i * nsub + j - off, 0), 0)) for j in range(nsub)]

    def assemble(i, part_refs, h_sc):
        head_ref, x_refs = part_refs[0], part_refs[1:]
        for j in range(nsub):
            rows = slice(j * SUB_ROWS, (j + 1) * SUB_ROWS)
            val = x_refs[j][...]
            if j < off:
                val = jnp.where(i == 0, head_ref[rows, :], val)
            h_sc[rows, :] = val

    return specs, [head] + [x] * nsub, assemble


def _h_source(body, h, tm, d):
    if not isinstance(h, tuple):
        return body, [pl.BlockSpec((tm, d), lambda i, k: (i, 0))], [h], []
    specs, args, assemble = _tile_parts(tm, d, *h)
    n_h = len(specs)

    def with_parts(*refs):
        h_sc = refs[-1]

        @pl.when(pl.program_id(1) == 0)
        def _():
            assemble(pl.program_id(0), refs[:n_h], h_sc)

        body(h_sc, *refs[n_h:-1])

    return with_parts, specs, args, [pltpu.VMEM((tm, d), F32)]


def _ffn_fwd(name, h, nw, wg, wu, wd, plan=None, plan_args=(), loss=None):
    lp, d = (h[0].shape[0] + h[1].shape[0], h[1].shape[1]) if isinstance(h, tuple) else h.shape
    nck, f, _ = wg.shape
    tm = _tile(lp, FFN_FWD_ROWS if loss is None else FFN_LOSS_ROWS, SUB_ROWS)
    cps = FFN_FWD_CPS
    last = nck // cps - 1
    n_t = 0
    if loss is not None:
        t_specs, t_args, t_assemble = _tile_parts(tm, d, jnp.zeros((lp - loss[1].shape[0], d), F32), loss[1])
        n_t = len(t_specs)

    def body(h_ref, nw_ref, wg_ref, wu_ref, wd_ref, *rest):
        if loss is not None:
            fw_ref, t_parts, rest = rest[0], rest[1:1 + n_t], rest[1 + n_t:]
            ho_ref, g_ref, u_ref, loss_ref, dfw_ref, n_sc, acc_sc, t_sc = rest
        else:
            ho_ref, g_ref, u_ref, n_sc, acc_sc = rest
        i = pl.program_id(0)
        k = pl.program_id(1)

        @pl.when(k == 0)
        def _():
            xh, _ = _rms_stats(h_ref[...])
            n_sc[...] = (xh * nw_ref[...]).astype(BF16)
            acc_sc[...] = jnp.zeros_like(acc_sc)

        n = n_sc[...]
        acc = acc_sc[...]
        for c in range(cps):
            g = _dot_nt(n, wg_ref[c])
            u = _dot_nt(n, wu_ref[c])
            g_ref[c] = g.astype(BF16)
            u_ref[c] = u.astype(BF16)
            a = (g * _sigmoid(g) * u).astype(BF16)
            acc = acc + _dot(a, wd_ref[c])
        acc_sc[...] = acc

        if loss is None:
            @pl.when(k == last)
            def _():
                ho_ref[...] = h_ref[...] + FFN_RES * acc_sc[...]
            return

        @pl.when(jnp.logical_and(i == 0, k == 0))
        def _():
            loss_ref[...] = jnp.zeros_like(loss_ref)
            dfw_ref[...] = jnp.zeros_like(dfw_ref)

        @pl.when(k == last)
        def _():
            t_assemble(i, t_parts, t_sc)
            xh, r = _rms_stats(h_ref[...] + FFN_RES * acc_sc[...])
            w = fw_ref[...]
            head_rows = lp - loss[1].shape[0]
            row = lax.broadcasted_iota(jnp.int32, (tm, d), 0) + i * tm
            err = jnp.where(row < head_rows, 0.0, xh * w - t_sc[...])
            loss_ref[...] += 0.5 * jnp.sum(err * err) / d
            dout = err * (1.0 / d)
            dfw_ref[...] += jnp.sum(dout * xh, axis=0, keepdims=True)
            ho_ref[...] = _rms_bwd(dout, xh, r, w)

    body, h_specs, h_args, h_scratch = _h_source(body, h, tm, d)
    vec = pl.BlockSpec((1, d), lambda i, k: (0, 0))
    w_fd = pl.BlockSpec((cps, f, d), lambda i, k: (k, 0, 0), **({'pipeline_mode': pl.Buffered(1)} if cps == nck else {}))
    hid = pl.BlockSpec((cps, tm, f), lambda i, k: (k, i, 0))
    hshape = jax.ShapeDtypeStruct((nck, lp, f), BF16)
    args, in_specs = (*h_args, nw, wg, wu, wd), h_specs + [vec, w_fd, w_fd, w_fd]
    out_specs = (pl.BlockSpec((tm, d), lambda i, k: (i, 0)), hid, hid)
    out_shape = (jax.ShapeDtypeStruct((lp, d), F32), hshape, hshape)
    scratch = [pltpu.VMEM((tm, d), BF16), pltpu.VMEM((tm, d), F32)]
    if loss is not None:
        args, in_specs = (*args, loss[0], *t_args), in_specs + [vec] + t_specs
        out_specs += (pl.BlockSpec((8, LANE), lambda i, k: (0, 0)), vec)
        out_shape += (jax.ShapeDtypeStruct((8, LANE), F32), jax.ShapeDtypeStruct((1, d), F32))
        scratch = scratch + [pltpu.VMEM((tm, d), F32)]
    return _pcall(
        body, name=name, grid=(lp // tm, nck // cps), plan=plan, plan_args=plan_args,
        args=args, in_specs=in_specs, out_specs=out_specs, out_shape=out_shape,
        scratch_shapes=scratch + h_scratch)


def _ffn_bwd_act(name, dh, h, nw, g, u, wg, wu, wd, plan=None, plan_args=()):
    lp, d = dh.shape
    nck, f, _ = wg.shape
    tm = _tile(lp, FFN_BWD_ROWS, SUB_ROWS)
    last = nck // FFN_CPS - 1

    def body(h_ref, dh_ref, nw_ref, g_ref, u_ref, wg_ref, wu_ref, wd_ref,
             dhi_ref, dnw_ref, n_ref, dacc_ref, a_ref, dg_ref, du_ref,
             xh_sc, r_sc, dn_sc):
        i = pl.program_id(0)
        k = pl.program_id(1)

        @pl.when(k == 0)
        def _():
            xh, r = _rms_stats(h_ref[...])
            xh_sc[...] = xh
            r_sc[...] = r
            n_ref[...] = (xh * nw_ref[...]).astype(BF16)
            dacc_ref[...] = (FFN_RES * dh_ref[...]).astype(BF16)
            dn_sc[...] = jnp.zeros_like(dn_sc)

        @pl.when(jnp.logical_and(i == 0, k == 0))
        def _():
            dnw_ref[...] = jnp.zeros_like(dnw_ref)

        dacc = dacc_ref[...]
        dn = dn_sc[...]
        for c in range(FFN_CPS):
            gv = g_ref[c].astype(F32)
            uv = u_ref[c].astype(F32)
            sg = _sigmoid(gv)
            sil = gv * sg
            da = _dot_nt(dacc, wd_ref[c])
            dgk = (da * uv * (sg * (1.0 + gv * (1.0 - sg)))).astype(BF16)
            duk = (da * sil).astype(BF16)
            a_ref[c] = (sil * uv).astype(BF16)
            dg_ref[c] = dgk
            du_ref[c] = duk
            dn = dn + _dot(dgk, wg_ref[c]) + _dot(duk, wu_ref[c])
        dn_sc[...] = dn

        @pl.when(k == last)
        def _():
            dnl = dn_sc[...]
            xh = xh_sc[...]
            dhi_ref[...] = dh_ref[...] + _rms_bwd(dnl, xh, r_sc[...], nw_ref[...])
            dnw_ref[...] += jnp.sum(dnl * xh, axis=0, keepdims=True)

    body, h_specs, h_args, h_scratch = _h_source(body, h, tm, d)
    row = pl.BlockSpec((tm, d), lambda i, k: (i, 0))
    vec = pl.BlockSpec((1, d), lambda i, k: (0, 0))
    hid = pl.BlockSpec((FFN_CPS, tm, f), lambda i, k: (k, i, 0))
    w_fd = pl.BlockSpec((FFN_CPS, f, d), lambda i, k: (k, 0, 0))
    rshape = jax.ShapeDtypeStruct((lp, d), BF16)
    hshape = jax.ShapeDtypeStruct((nck, lp, f), BF16)
    return _pcall(
        body, name=name, grid=(lp // tm, nck // FFN_CPS), plan=plan, plan_args=plan_args,
        args=(*h_args, dh, nw, g, u, wg, wu, wd),
        in_specs=h_specs + [row, vec, hid, hid, w_fd, w_fd, w_fd],
        out_specs=(row, vec, row, row, hid, hid, hid),
        out_shape=(jax.ShapeDtypeStruct((lp, d), F32), jax.ShapeDtypeStruct((1, d), F32),
                   rshape, rshape, hshape, hshape, hshape),
        scratch_shapes=[pltpu.VMEM((tm, d), F32), pltpu.VMEM((tm, 1), F32), pltpu.VMEM((tm, d), F32)] + h_scratch)


def _ffn_bwd_w(name, n, dacc, a, dg, du, plan=None, plan_args=()):
    lp, d = n.shape
    nck, _, f = a.shape
    tm = _tile(lp, BWD_W_ROWS)
    last = lp // tm - 1

    def body(n_ref, dacc_ref, a_ref, dg_ref, du_ref, dwg_ref, dwu_ref, dwd_ref, ag_sc, au_sc, ad_sc):
        i = pl.program_id(1)

        @pl.when(i == 0)
        def _():
            ag_sc[...] = jnp.zeros_like(ag_sc)
            au_sc[...] = jnp.zeros_like(au_sc)
            ad_sc[...] = jnp.zeros_like(ad_sc)

        nv = n_ref[...]
        ag_sc[...] += _dot_tn(dg_ref[0], nv)
        au_sc[...] += _dot_tn(du_ref[0], nv)
        ad_sc[...] += _dot_tn(a_ref[0], dacc_ref[...])

        @pl.when(i == last)
        def _():
            dwg_ref[0] = ag_sc[...].astype(BF16)
            dwu_ref[0] = au_sc[...].astype(BF16)
            dwd_ref[0] = ad_sc[...].astype(BF16)

    row = pl.BlockSpec((tm, d), lambda k, i: (i, 0))
    hid = pl.BlockSpec((1, tm, f), lambda k, i: (k, i, 0))
    w_fd = pl.BlockSpec((1, f, d), lambda k, i: (k, 0, 0))
    wshape = jax.ShapeDtypeStruct((nck, f, d), BF16)
    return _pcall(
        body, name=name, grid=(nck, lp // tm), plan=plan, plan_args=plan_args, args=(n, dacc, a, dg, du),
        in_specs=[row, row, hid, hid, hid], out_specs=(w_fd, w_fd, w_fd), out_shape=(wshape,) * 3,
        scratch_shapes=[pltpu.VMEM((f, d), F32)] * 3)


def _ffn_bwd_w_scatter(name, n, dacc, a, dg, du, chip, plan, plan_args):
    lp, d = n.shape
    nck, _, f = a.shape
    tm = _tile(lp, BWD_W_ROWS)
    last_i = lp // tm - 1
    n_w = 3
    n_p = plan.n

    def body(me_ref, n_ref, dacc_ref, a_ref, dg_ref, du_ref, *rest):
        p_ins = rest[:n_p]
        recv = rest[n_p:n_p + n_w]
        p_outs = rest[n_p + n_w:2 * n_p + n_w]
        acc = rest[2 * n_p + n_w:2 * n_p + 2 * n_w]
        stage, send_sems, recv_sems, loc_sems = rest[2 * n_p + 2 * n_w:2 * n_p + 2 * n_w + 4]
        p_sems = rest[2 * n_p + 2 * n_w + 4:]
        p = pl.program_id(0)
        i = pl.program_id(1)
        me = me_ref[0]
        c = lax.axis_index("c")

        def send(w, pos):
            kk = jnp.bitwise_xor(me, nck - 1 - pos)
            diff = jnp.bitwise_xor(kk, me)
            m = jnp.where(diff == 2, 0, jnp.where(diff == 1, 1, 2))
            return pltpu.make_async_remote_copy(
                src_ref=stage.at[lax.rem(pos, 2), w], dst_ref=recv[w].at[me],
                send_sem=send_sems.at[w * 3 + m], recv_sem=recv_sems.at[w * 3 + m],
                device_id=(lax.div(kk, 2), lax.rem(kk, 2), c), device_id_type=MESH)

        @pl.when(jnp.logical_and(p == 0, i == 0))
        def _():
            for cp in plan.copies(p_ins, p_outs, p_sems):
                cp.start()

        @pl.when(i == 0)
        def _():
            for t in acc:
                t[...] = jnp.zeros_like(t)

        nv = n_ref[...]
        acc[0][...] += _dot_tn(dg_ref[0], nv)
        acc[1][...] += _dot_tn(du_ref[0], nv)
        acc[2][...] += _dot_tn(a_ref[0], dacc_ref[...])

        @pl.when(jnp.logical_and(i == last_i, p >= 2))
        def _():
            for w in range(n_w):
                send(w, p - 2).wait_send()

        @pl.when(i == last_i)
        def _():
            for w in range(n_w):
                stage[lax.rem(p, 2), w] = acc[w][...].astype(BF16)

        @pl.when(jnp.logical_and(i == last_i, p < nck - 1))
        def _():
            for w in range(n_w):
                send(w, p).start()

        @pl.when(jnp.logical_and(i == last_i, p == nck - 1))
        def _():
            own = [pltpu.make_async_copy(stage.at[(nck - 1) % 2, w], recv[w].at[me], loc_sems.at[w])
                   for w in range(n_w)]
            for cp in own:
                cp.start()
            for w in range(n_w):
                send(w, nck - 2).wait_send()
            for cp in own:
                cp.wait()
            for w in range(n_w):
                for m in range(3):
                    pltpu.make_async_remote_copy(
                        src_ref=stage.at[0, w], dst_ref=recv[w].at[me],
                        send_sem=send_sems.at[w * 3 + m], recv_sem=recv_sems.at[w * 3 + m],
                        device_id=(0, 0, c), device_id_type=MESH).wait_recv()
            for cp in plan.copies(p_ins, p_outs, p_sems):
                cp.wait()

    chunk = lambda k, me_ref: jnp.bitwise_xor(me_ref[0], nck - 1 - k)
    row = pl.BlockSpec((tm, d), lambda k, i, me_ref: (i, 0))
    hid = pl.BlockSpec((1, tm, f), lambda k, i, me_ref: (chunk(k, me_ref), i, 0))
    wshape = jax.ShapeDtypeStruct((nck, f, d), BF16)
    res = pl.pallas_call(
        body, name=name,
        grid_spec=pltpu.PrefetchScalarGridSpec(
            num_scalar_prefetch=1, grid=(nck, lp // tm),
            in_specs=[row, row, hid, hid, hid] + [ANY_SPEC] * n_p,
            out_specs=(ANY_SPEC,) * (n_w + n_p),
            scratch_shapes=[pltpu.VMEM((f, d), F32)] * n_w + [
                pltpu.VMEM((2, n_w, f, d), BF16), pltpu.SemaphoreType.DMA((n_w * 3,)),
                pltpu.SemaphoreType.DMA((n_w * 3,)), pltpu.SemaphoreType.DMA((n_w,))] + plan.scratch()),
        out_shape=(wshape,) * n_w + plan.out_shape(),
        compiler_params=_params(("arbitrary", "arbitrary")),
    )(chip.reshape(1).astype(jnp.int32), n, dacc, a, dg, du, *plan_args)
    return list(res[:n_w]), list(res[n_w:])


def _inproj_fwd(h, nw, w_in, cosf, sinf, rw):
    lp, d = h.shape
    nck, _, ps = w_in.shape
    proj = nck * ps
    sw = proj - 4 * rw
    tm = _tile(lp, 640)
    scale = HEAD_DIM ** -0.5
    heads = rw // HEAD_DIM

    def body(h_ref, nw_ref, w_ref, cos_ref, sin_ref, n_ref, q_ref, k_ref, v_ref, g_ref, u_ref, p_sc):
        xh, _ = _rms_stats(h_ref[...])
        n = (xh * nw_ref[...]).astype(BF16)
        n_ref[...] = n
        for c in range(nck):
            p_sc[:, c * ps:(c + 1) * ps] = _dot(n, w_ref[c])
        cs = cos_ref[...]
        sn = sin_ref[...]
        for hh in range(heads):
            lo = hh * HEAD_DIM
            qh = p_sc[:, lo:lo + HEAD_DIM]
            q_ref[:, lo:lo + HEAD_DIM] = (qh * cs + pltpu.roll(qh, HEAD_DIM // 2, 1) * sn).astype(BF16)
            kh = p_sc[:, rw + lo:rw + lo + HEAD_DIM]
            k_ref[:, lo:lo + HEAD_DIM] = ((kh * cs + pltpu.roll(kh, HEAD_DIM // 2, 1) * sn) * scale).astype(BF16)
        v_ref[...] = p_sc[:, 2 * rw:3 * rw].astype(BF16)
        g_ref[...] = p_sc[:, 3 * rw:4 * rw]
        u_ref[...] = p_sc[:, 4 * rw:]

    row = lambda w: pl.BlockSpec((tm, w), lambda i: (i, 0))
    return pl.pallas_call(
        body, name="inproj_fwd", grid=(lp // tm,),
        in_specs=[row(d), pl.BlockSpec((1, d), lambda i: (0, 0)),
                  pl.BlockSpec((nck, d, ps), lambda i: (0, 0, 0)), row(HEAD_DIM), row(HEAD_DIM)],
        out_specs=(row(d), row(rw), row(rw), row(rw), row(rw), row(sw)),
        out_shape=(jax.ShapeDtypeStruct((lp, d), BF16),
                   jax.ShapeDtypeStruct((lp, rw), BF16),
                   jax.ShapeDtypeStruct((lp, rw), BF16),
                   jax.ShapeDtypeStruct((lp, rw), BF16),
                   jax.ShapeDtypeStruct((lp, rw), F32),
                   jax.ShapeDtypeStruct((lp, sw), F32)),
        scratch_shapes=[pltpu.VMEM((tm, proj), F32)],
        compiler_params=_params(("arbitrary",)),
    )(h, nw, w_in, cosf, sinf)


def _inproj_bwd(dh, h, nw, n, w_in, dq, dk, dv, dg, du):
    lp, d = h.shape
    nck, _, ps = w_in.shape
    rw = dq.shape[1]
    sw = du.shape[1]
    proj = nck * ps
    tm = _tile(lp, 640)
    last = lp // tm - 1

    def gather_dproj(p_sc, dq_ref, dk_ref, dv_ref, dg_ref, du_ref):
        p_sc[:, 0:rw] = dq_ref[...]
        p_sc[:, rw:2 * rw] = dk_ref[...]
        p_sc[:, 2 * rw:3 * rw] = dv_ref[...]
        p_sc[:, 3 * rw:4 * rw] = dg_ref[...]
        p_sc[:, 4 * rw:] = du_ref[...]

    def act_body(dh_ref, h_ref, nw_ref, w_ref, dq_ref, dk_ref, dv_ref, dg_ref, du_ref, dhi_ref, dnw_ref, p_sc):
        i = pl.program_id(0)

        @pl.when(i == 0)
        def _():
            dnw_ref[...] = jnp.zeros_like(dnw_ref)

        gather_dproj(p_sc, dq_ref, dk_ref, dv_ref, dg_ref, du_ref)
        dn = jnp.zeros((tm, d), F32)
        for c in range(nck):
            dn = dn + _dot_nt(p_sc[:, c * ps:(c + 1) * ps], w_ref[c])
        xh, r = _rms_stats(h_ref[...])
        dhi_ref[...] = dh_ref[...] + _rms_bwd(dn, xh, r, nw_ref[...])
        dnw_ref[...] += jnp.sum(dn * xh, axis=0, keepdims=True)

    def w_body(n_ref, dq_ref, dk_ref, dv_ref, dg_ref, du_ref, dw_ref, p_sc, acc_sc):
        i = pl.program_id(0)

        @pl.when(i == 0)
        def _():
            acc_sc[...] = jnp.zeros_like(acc_sc)

        gather_dproj(p_sc, dq_ref, dk_ref, dv_ref, dg_ref, du_ref)
        nv = n_ref[...]
        for c in range(nck):
            acc_sc[c] += _dot_tn(nv, p_sc[:, c * ps:(c + 1) * ps])

        @pl.when(i == last)
        def _():
            dw_ref[...] = acc_sc[...].astype(BF16)

    row = lambda w: pl.BlockSpec((tm, w), lambda i: (i, 0))
    vec = pl.BlockSpec((1, d), lambda i: (0, 0))
    wsp = pl.BlockSpec((nck, d, ps), lambda i: (0, 0, 0))
    dproj_specs = [row(rw), row(rw), row(rw), row(rw), row(sw)]
    dhi, dnw = pl.pallas_call(
        act_body, name="inproj_bwd_act", grid=(lp // tm,),
        in_specs=[row(d), row(d), vec, wsp] + dproj_specs,
        out_specs=(row(d), vec),
        out_shape=(jax.ShapeDtypeStruct((lp, d), F32), jax.ShapeDtypeStruct((1, d), F32)),
        scratch_shapes=[pltpu.VMEM((tm, proj), BF16)],
        compiler_params=_params(("arbitrary",)),
    )(dh, h, nw, w_in, dq, dk, dv, dg, du)
    dw = pl.pallas_call(
        w_body, name="inproj_bwd_w", grid=(lp // tm,),
        in_specs=[row(d)] + dproj_specs,
        out_specs=wsp, out_shape=jax.ShapeDtypeStruct((nck, d, ps), BF16),
        scratch_shapes=[pltpu.VMEM((tm, proj), BF16), pltpu.VMEM((nck, d, ps), F32)],
        compiler_params=_params(("arbitrary",)),
    )(n, dq, dk, dv, dg, du)
    return dhi, dnw, dw


def _retention_tables(rc):
    h = jnp.arange(RET_HEADS, dtype=F32)
    log_g = jnp.log(1.0 - 2.0 ** (-5.0 - h))
    i = jnp.arange(rc)
    diff = i[:, None] - i[None, :]
    dec = jnp.where(diff[None] >= 0,
                    jnp.exp(log_g[:, None, None] * jnp.maximum(diff, 0)[None].astype(F32)), 0.0)
    pos = jnp.arange(rc, dtype=F32)
    wq = jnp.exp(log_g[:, None] * (pos + 1.0)[None])
    wk = jnp.exp(log_g[:, None] * (rc - 1 - pos)[None])
    gch = jnp.exp(log_g * rc)
    ones = jnp.ones((1, 1, HEAD_DIM), F32)
    return (dec, wq[:, :, None] * ones, wk[:, :, None] * ones,
            gch[:, None, None] * jnp.ones((1, 8, HEAD_DIM), F32))


def _head_norm(o):
    mu = jnp.mean(o, axis=-1, keepdims=True)
    oc = o - mu
    r = lax.rsqrt(jnp.mean(oc * oc, axis=-1, keepdims=True) + EPS)
    return oc * r, r


def _ret_fwd(q, k, v, g, rnw, tables):
    lp, rw = q.shape
    heads = rw // HEAD_DIM
    rc = tables[0].shape[1]
    nch = lp // rc
    dec, wq, wk, gch = tables

    def body(q_ref, k_ref, v_ref, g_ref, w_ref, dec_ref, wq_ref, wk_ref, gch_ref,
             o_ref, ret_ref, sp_ref, s_sc):
        n = pl.program_id(0)

        @pl.when(n == 0)
        def _():
            s_sc[...] = jnp.zeros_like(s_sc)

        cols = [slice(hh * HEAD_DIM, (hh + 1) * HEAD_DIM) for hh in range(heads)]
        s_ins = [s_sc[hh] for hh in range(heads)]
        outs = []
        for hh, cs in enumerate(cols):
            qv, kv, vv = q_ref[:, cs], k_ref[:, cs], v_ref[:, cs]
            s_in = s_ins[hh]
            a = _dot_nt(qv, kv) * dec_ref[hh]
            qw = (qv.astype(F32) * wq_ref[hh]).astype(BF16)
            kw = (kv.astype(F32) * wk_ref[hh]).astype(BF16)
            o = _dot(a.astype(BF16), vv) + _dot(qw, s_in.astype(BF16))
            s_new = gch_ref[hh, 0:1, :] * s_in + _dot_tn(kw, vv)
            xh, _ = _head_norm(o)
            gv = g_ref[:, cs]
            outs.append((o, s_new, (gv * _sigmoid(gv) * (xh * w_ref[:, cs])).astype(BF16)))
        for hh, cs in enumerate(cols):
            o, s_new, ret = outs[hh]
            sp_ref[hh, 0] = s_ins[hh]
            s_sc[hh] = s_new
            o_ref[:, cs] = o
            ret_ref[:, cs] = ret

    blk = pl.BlockSpec((rc, rw), lambda n: (n, 0))
    tab = pl.BlockSpec((heads, rc, HEAD_DIM), lambda n: (0, 0, 0))
    dtab = pl.BlockSpec((heads, rc, rc), lambda n: (0, 0, 0))
    return pl.pallas_call(
        body, name="retention_fwd", grid=(nch,),
        in_specs=[blk, blk, blk, blk, pl.BlockSpec((1, rw), lambda n: (0, 0)),
                  dtab, tab, tab, pl.BlockSpec((heads, 8, HEAD_DIM), lambda n: (0, 0, 0))],
        out_specs=(blk, blk, pl.BlockSpec((heads, 1, HEAD_DIM, HEAD_DIM), lambda n: (0, n, 0, 0))),
        out_shape=(jax.ShapeDtypeStruct((lp, rw), F32),
                   jax.ShapeDtypeStruct((lp, rw), BF16),
                   jax.ShapeDtypeStruct((heads, nch, HEAD_DIM, HEAD_DIM), F32)),
        scratch_shapes=[pltpu.VMEM((heads, HEAD_DIM, HEAD_DIM), F32)],
        compiler_params=_params(("arbitrary",)),
    )(q, k, v, g, rnw, dec, wq, wk, gch)


def _ret_bwd(dret, q, k, v, g, o, sprev, rnw, tables, cosf, sinf):
    lp, rw = q.shape
    heads = rw // HEAD_DIM
    rc = tables[0].shape[1]
    nch = lp // rc
    dec, wq, wk, gch = tables
    scale = HEAD_DIM ** -0.5
    half = HEAD_DIM // 2

    def body(dret_ref, q_ref, k_ref, v_ref, g_ref, o_ref, sp_ref, w_ref, dec_ref, wq_ref, wk_ref, gch_ref,
             cos_ref, sin_ref, dq_ref, dk_ref, dv_ref, dg_ref, dw_ref, ds_sc):
        n = pl.program_id(0)

        @pl.when(n == 0)
        def _():
            ds_sc[...] = jnp.zeros_like(ds_sc)
            dw_ref[...] = jnp.zeros_like(dw_ref)

        cosv = cos_ref[...]
        sinv = sin_ref[...]
        cols = [slice(hh * HEAD_DIM, (hh + 1) * HEAD_DIM) for hh in range(heads)]
        ds_ins = [ds_sc[hh] for hh in range(heads)]
        dw_ins = [dw_ref[:, cs] for cs in cols]
        outs = []
        for hh, cs in enumerate(cols):
            qv, kv, vv = q_ref[:, cs], k_ref[:, cs], v_ref[:, cs]
            gv = g_ref[:, cs]
            dr = dret_ref[:, cs]
            w = w_ref[:, cs]
            sg = _sigmoid(gv)
            sil = gv * sg
            xh, r = _head_norm(o_ref[:, cs])
            dgate = (dr * (xh * w) * (sg * (1.0 + gv * (1.0 - sg)))).astype(BF16)
            dyw = dr * sil
            dw_new = dw_ins[hh] + jnp.sum(dyw * xh, axis=0, keepdims=True)
            dxh = dyw * w
            do = r * (dxh - jnp.mean(dxh, axis=-1, keepdims=True)
                      - xh * jnp.mean(dxh * xh, axis=-1, keepdims=True))
            dob = do.astype(BF16)
            dmask = dec_ref[hh]
            wqv = wq_ref[hh]
            wkv = wk_ref[hh]
            a = (_dot_nt(qv, kv) * dmask).astype(BF16)
            da = (_dot_nt(dob, vv) * dmask).astype(BF16)
            qw = (qv.astype(F32) * wqv).astype(BF16)
            kw = (kv.astype(F32) * wkv).astype(BF16)
            s_in = sp_ref[hh, 0].astype(BF16)
            ds = ds_ins[hh]
            dsb = ds.astype(BF16)
            dq = _dot(da, kv) + _dot_nt(dob, s_in) * wqv
            dk = _dot_tn(da, qv) + _dot_nt(vv, dsb) * wkv
            dv = _dot_tn(a, dob) + _dot(kw, dsb)
            ds_new = gch_ref[hh, 0:1, :] * ds + _dot_tn(qw, dob)
            outs.append((dgate, dw_new, ds_new,
                         (dq * cosv + pltpu.roll(dq * sinv, half, 1)).astype(BF16),
                         ((dk * cosv + pltpu.roll(dk * sinv, half, 1)) * scale).astype(BF16),
                         dv.astype(BF16)))
        for hh, cs in enumerate(cols):
            dgate, dw_new, ds_new, dqv, dkv, dvv = outs[hh]
            dg_ref[:, cs] = dgate
            dw_ref[:, cs] = dw_new
            ds_sc[hh] = ds_new
            dq_ref[:, cs] = dqv
            dk_ref[:, cs] = dkv
            dv_ref[:, cs] = dvv

    blk = pl.BlockSpec((rc, rw), lambda n: (nch - 1 - n, 0))
    tab = pl.BlockSpec((heads, rc, HEAD_DIM), lambda n: (0, 0, 0))
    dtab = pl.BlockSpec((heads, rc, rc), lambda n: (0, 0, 0))
    wsp = pl.BlockSpec((1, rw), lambda n: (0, 0))
    pos = pl.BlockSpec((rc, HEAD_DIM), lambda n: (nch - 1 - n, 0))
    bshape = jax.ShapeDtypeStruct((lp, rw), BF16)
    return pl.pallas_call(
        body, name="retention_bwd", grid=(nch,),
        in_specs=[blk, blk, blk, blk, blk, blk,
                  pl.BlockSpec((heads, 1, HEAD_DIM, HEAD_DIM), lambda n: (0, nch - 1 - n, 0, 0)),
                  wsp, dtab, tab, tab, pl.BlockSpec((heads, 8, HEAD_DIM), lambda n: (0, 0, 0)), pos, pos],
        out_specs=(blk, blk, blk, blk, wsp),
        out_shape=(bshape, bshape, bshape, bshape, jax.ShapeDtypeStruct((1, rw), F32)),
        scratch_shapes=[pltpu.VMEM((heads, HEAD_DIM, HEAD_DIM), F32)],
        compiler_params=_params(("arbitrary",)),
    )(dret, q, k, v, g, o, sprev, rnw, dec, wq, wk, gch, cosf, sinf)


SCAN_CW = 512


def _s5_prepare(lam_re, lam_im, log_dt, b_re, b_im):
    dt = jnp.exp(log_dt)[:, None]
    er = jnp.exp(lam_re * dt)
    ar = er * jnp.cos(lam_im * dt)
    ai = er * jnp.sin(lam_im * dt)
    den = lam_re * lam_re + lam_im * lam_im
    fr = ((ar - 1.0) * lam_re + ai * lam_im) / den
    fi = (ai * lam_re - (ar - 1.0) * lam_im) / den
    bbr = fr[..., None] * b_re - fi[..., None] * b_im
    bbi = fr[..., None] * b_im + fi[..., None] * b_re
    return ar, ai, bbr, bbi


def _blockdiag_in(t):
    g, p, n = t.shape
    gs = g // N_SEC
    t = t.reshape(N_SEC, gs, p, n)
    eye = jnp.eye(gs, dtype=t.dtype)
    return jnp.einsum("sgpn,gh->sgphn", t, eye).reshape(N_SEC, gs * p, gs * n)


def _blockdiag_out(m, g, p, n):
    gs = g // N_SEC
    m = m.reshape(N_SEC, gs, p, gs, n)
    eye = jnp.eye(gs, dtype=m.dtype)
    return jnp.einsum("sgphn,gh->sgpn", m, eye).reshape(g, p, n)


def _scan_step(xr_ref, xi_ref, r0, prev, ar_ref, ai_ref, conj, ncols):
    new = []
    for cc in range(ncols // SCAN_CW):
        cs = pl.ds(cc * SCAN_CW, SCAN_CW)
        pr, pi = prev[cc]
        ar = ar_ref[:, cs]
        ai = ai_ref[:, cs]
        if conj:
            nr = ar * pr + ai * pi
            ni = ar * pi - ai * pr
        else:
            nr = ar * pr - ai * pi
            ni = ar * pi + ai * pr
        xr = xr_ref[pl.ds(r0, 8), cs] + nr
        xi = xi_ref[pl.ds(r0, 8), cs] + ni
        xr_ref[pl.ds(r0, 8), cs] = xr
        xi_ref[pl.ds(r0, 8), cs] = xi
        new.append((xr, xi))
    return new


def _scan_chunks(ncols):
    return [pl.ds(cc * SCAN_CW, SCAN_CW) for cc in range(ncols // SCAN_CW)]


def _flat(pairs):
    return tuple(t for p in pairs for t in p)


def _pairs(flat):
    return [(flat[2 * k], flat[2 * k + 1]) for k in range(len(flat) // 2)]


def _shift_rows(z, down):
    row = lax.broadcasted_iota(jnp.int32, z.shape, 0)
    if down:
        return jnp.where(row == 0, 0.0, pltpu.roll(z, 1, 0))
    return jnp.where(row == N_SEG - 1, 0.0, pltpu.roll(z, N_SEG - 1, 0))


def _s5_fwd(u, bsr, bsi, csr, csi, a8r, a8i, al8r, al8i, d, gluw, glub, nw, jb, plan=None, plan_args=()):
    lp, sw = u.shape
    ns = a8r.shape[1]
    rows = N_SEG * jb
    nblk = lp // rows
    secw = sw // N_SEC
    secn = ns // N_SEC

    def local_scan(u_ref, bsr_ref, bsi_ref, ar_ref, ai_ref, xr_ref, xi_ref, pr_sc, pi_sc):
        for s in range(N_SEC):
            ub = u_ref[:, s * secw:(s + 1) * secw].astype(BF16)
            xr_ref[:, s * secn:(s + 1) * secn] = _dot(ub, bsr_ref[s])
            xi_ref[:, s * secn:(s + 1) * secn] = _dot(ub, bsi_ref[s])
        prev = [(pr_sc[:, cs], pi_sc[:, cs]) for cs in _scan_chunks(ns)]
        prev = _scan_step(xr_ref, xi_ref, 0, prev, ar_ref, ai_ref, False, ns)

        def step(j, carry):
            r0 = pl.multiple_of(j * 8, 8)
            return _flat(_scan_step(xr_ref, xi_ref, r0, _pairs(carry), ar_ref, ai_ref, False, ns))

        last = _pairs(lax.fori_loop(1, jb, step, _flat(prev)))
        for cs, (vr, vi) in zip(_scan_chunks(ns), last):
            pr_sc[:, cs] = vr
            pi_sc[:, cs] = vi

    def carry_body(u_ref, bsr_ref, bsi_ref, ar_ref, ai_ref, alr_ref, ali_ref, cr_ref, ci_ref,
                   xr_sc, xi_sc, pr_sc, pi_sc):
        b = pl.program_id(0)

        @pl.when(b == 0)
        def _():
            pr_sc[...] = jnp.zeros_like(pr_sc)
            pi_sc[...] = jnp.zeros_like(pi_sc)

        local_scan(u_ref, bsr_ref, bsi_ref, ar_ref, ai_ref, xr_sc, xi_sc, pr_sc, pi_sc)

        @pl.when(b == nblk - 1)
        def _():
            er = _shift_rows(pr_sc[...], True)
            ei = _shift_rows(pi_sc[...], True)
            alr, ali = alr_ref[...], ali_ref[...]
            cr, ci = er, ei
            for _ in range(N_SEG - 2):
                sr = _shift_rows(cr, True)
                si = _shift_rows(ci, True)
                cr = er + alr * sr - ali * si
                ci = ei + alr * si + ali * sr
            cr_ref[...] = cr
            ci_ref[...] = ci

    ublk = pl.BlockSpec((rows, sw), lambda b: (b, 0))
    bspec = pl.BlockSpec((N_SEC, secw, secn), lambda b: (0, 0, 0))
    cspec = pl.BlockSpec((N_SEC, secn, secw), lambda b: (0, 0, 0))
    s8 = pl.BlockSpec((N_SEG, ns), lambda b: (0, 0))
    vec = pl.BlockSpec((1, sw), lambda b: (0, 0))
    s8shape = jax.ShapeDtypeStruct((N_SEG, ns), F32)
    c0r, c0i = pl.pallas_call(
        carry_body, name="s5_fwd_carry", grid=(nblk,),
        in_specs=[ublk, bspec, bspec, s8, s8, s8, s8],
        out_specs=(s8, s8), out_shape=(s8shape, s8shape),
        scratch_shapes=[pltpu.VMEM((rows, ns), F32), pltpu.VMEM((rows, ns), F32),
                        pltpu.VMEM((N_SEG, ns), F32), pltpu.VMEM((N_SEG, ns), F32)],
        compiler_params=_params(("arbitrary",)),
    )(u, bsr, bsi, a8r, a8i, al8r, al8i)

    def main_body(u_ref, bsr_ref, bsi_ref, csr_ref, csi_ref, ar_ref, ai_ref, c0r_ref, c0i_ref,
                  d_ref, gw_ref, gb_ref, nw_ref, xr_ref, xi_ref, yp_ref, out_ref, pr_sc, pi_sc):
        b = pl.program_id(0)

        @pl.when(b == 0)
        def _():
            pr_sc[...] = c0r_ref[...]
            pi_sc[...] = c0i_ref[...]

        local_scan(u_ref, bsr_ref, bsi_ref, ar_ref, ai_ref, xr_ref, xi_ref, pr_sc, pi_sc)
        for s in range(N_SEC):
            xs = pl.ds(s * secn, secn)
            us = pl.ds(s * secw, secw)
            y = _dot(xr_ref[:, xs].astype(BF16), csr_ref[s]) + _dot(xi_ref[:, xs].astype(BF16), csi_ref[s])
            yp_ref[:, us] = y + d_ref[:, us] * u_ref[:, us]
        yp = yp_ref[...]
        t = jnp.tanh(GELU_K0 * (yp + GELU_K1 * yp * yp * yp))
        y1 = 0.5 * yp * (1.0 + t)
        z = _dot(y1.astype(BF16), gw_ref[...]) + gb_ref[...]
        y2 = y1 * _sigmoid(z)
        xh, _ = _rms_stats(y2)
        out_ref[...] = (xh * nw_ref[...]).astype(BF16)

    xblk = pl.BlockSpec((rows, ns), lambda b: (b, 0))
    (xr, xi, yp, out), rode = _pcall(
        main_body, name="s5_fwd", grid=(nblk,), plan=plan, plan_args=plan_args,
        args=(u, bsr, bsi, csr, csi, a8r, a8i, c0r, c0i, d, gluw, glub, nw),
        in_specs=[ublk, bspec, bspec, cspec, cspec, s8, s8, s8, s8, vec,
                  pl.BlockSpec((sw, sw), lambda b: (0, 0)), vec, vec],
        out_specs=(xblk, xblk, ublk, ublk),
        out_shape=(jax.ShapeDtypeStruct((lp, ns), F32), jax.ShapeDtypeStruct((lp, ns), F32),
                   jax.ShapeDtypeStruct((lp, sw), F32), jax.ShapeDtypeStruct((lp, sw), BF16)),
        scratch_shapes=[pltpu.VMEM((N_SEG, ns), F32), pltpu.VMEM((N_SEG, ns), F32)])
    return xr, xi, c0r, c0i, yp, out, rode


def _s5_bwd(dout, u, yp, xr, xi, c0r, c0i, bsrt, bsit, csrt, csit, a8r, a8i, al8r, al8i, d, gluw, glub, nw, jb):
    lp, sw = u.shape
    ns = a8r.shape[1]
    rows = N_SEG * jb
    nblk = lp // rows
    secw = sw // N_SEC
    secn = ns // N_SEC

    def rowwise_bwd(dout_ref, yp_ref, gw_ref, gb_ref, nw_ref):
        ypv = yp_ref[...]
        t = jnp.tanh(GELU_K0 * (ypv + GELU_K1 * ypv * ypv * ypv))
        y1 = 0.5 * ypv * (1.0 + t)
        dgelu = 0.5 * (1.0 + t) + 0.5 * ypv * (1.0 - t * t) * GELU_K0 * (1.0 + 3.0 * GELU_K1 * ypv * ypv)
        gw = gw_ref[...]
        y1b = y1.astype(BF16)
        sg = _sigmoid(_dot(y1b, gw) + gb_ref[...])
        xh, r = _rms_stats(y1 * sg)
        dov = dout_ref[...]
        dy2 = _rms_bwd(dov, xh, r, nw_ref[...])
        dz = dy2 * y1 * sg * (1.0 - sg)
        dzb = dz.astype(BF16)
        dy1 = dy2 * sg + _dot_nt(dzb, gw)
        return dy1 * dgelu, dov * xh, y1b, dzb, dz

    def lam_scan(dyp_of, csrt_ref, csit_ref, ar_ref, ai_ref, lr_sc, li_sc, nr_sc, ni_sc, extra):
        for s in range(N_SEC):
            db = dyp_of(s)
            lr_sc[:, s * secn:(s + 1) * secn] = _dot(db, csrt_ref[s])
            li_sc[:, s * secn:(s + 1) * secn] = _dot(db, csit_ref[s])
        top = rows - 8
        prev = [(nr_sc[:, cs], ni_sc[:, cs]) for cs in _scan_chunks(ns)]
        prev = _scan_step(lr_sc, li_sc, top, prev, ar_ref, ai_ref, True, ns)
        extra(top, pl.ds(top - 8, 8))

        def step(jj, carry):
            r0 = pl.multiple_of((jb - 1 - jj) * 8, 8)
            rp = pl.multiple_of((jb - 2 - jj) * 8, 8)
            new = _scan_step(lr_sc, li_sc, r0, _pairs(carry), ar_ref, ai_ref, True, ns)
            extra(r0, pl.ds(rp, 8))
            return _flat(new)

        prev = _pairs(lax.fori_loop(1, jb - 1, step, _flat(prev)))
        last = _scan_step(lr_sc, li_sc, 0, prev, ar_ref, ai_ref, True, ns)
        extra(0, None)
        for cs, (vr, vi) in zip(_scan_chunks(ns), last):
            nr_sc[:, cs] = vr
            ni_sc[:, cs] = vi

    def carry_body(dout_ref, yp_ref, u_ref, gw_ref, gb_ref, nw_ref, csrt_ref, csit_ref, ar_ref, ai_ref,
                   alr_ref, ali_ref, cr_ref, ci_ref, dyp_ref, dnw_ref, dgw_ref, dgb_ref, dd_ref,
                   lr_sc, li_sc, nr_sc, ni_sc):
        b = pl.program_id(0)

        @pl.when(b == 0)
        def _():
            nr_sc[...] = jnp.zeros_like(nr_sc)
            ni_sc[...] = jnp.zeros_like(ni_sc)
            for ref in (dnw_ref, dgw_ref, dgb_ref, dd_ref):
                ref[...] = jnp.zeros_like(ref)

        dyp, dnw_rows, y1b, dzb, dz = rowwise_bwd(dout_ref, yp_ref, gw_ref, gb_ref, nw_ref)
        dnw_ref[...] += jnp.sum(dnw_rows, axis=0, keepdims=True)
        dgw_ref[...] += _dot_tn(y1b, dzb)
        dgb_ref[...] += jnp.sum(dz, axis=0, keepdims=True)
        dd_ref[...] += jnp.sum(dyp * u_ref[...], axis=0, keepdims=True)
        dyp_ref[...] = dyp.astype(BF16)
        lam_scan(lambda s: dyp_ref[:, s * secw:(s + 1) * secw], csrt_ref, csit_ref, ar_ref, ai_ref,
                 lr_sc, li_sc, nr_sc, ni_sc, lambda r0, prev_rows: None)

        @pl.when(b == nblk - 1)
        def _():
            fr = _shift_rows(nr_sc[...], False)
            fi = _shift_rows(ni_sc[...], False)
            alr, ali = alr_ref[...], ali_ref[...]
            cr, ci = fr, fi
            for _ in range(N_SEG - 2):
                sr = _shift_rows(cr, False)
                si = _shift_rows(ci, False)
                cr = fr + alr * sr + ali * si
                ci = fi + alr * si - ali * sr
            cr_ref[...] = cr
            ci_ref[...] = ci

    rev = lambda b: (nblk - 1 - b, 0)
    ublk = pl.BlockSpec((rows, sw), rev)
    xblk = pl.BlockSpec((rows, ns), rev)
    s8 = pl.BlockSpec((N_SEG, ns), lambda b: (0, 0))
    vec = pl.BlockSpec((1, sw), lambda b: (0, 0))
    gws = pl.BlockSpec((sw, sw), lambda b: (0, 0))
    btspec = pl.BlockSpec((N_SEC, secn, secw), lambda b: (0, 0, 0))
    ctspec = pl.BlockSpec((N_SEC, secw, secn), lambda b: (0, 0, 0))
    s8shape = jax.ShapeDtypeStruct((N_SEG, ns), F32)
    lcr, lci, dyp_all, d_nw, d_gw, d_gb, d_d = pl.pallas_call(
        carry_body, name="s5_bwd_carry", grid=(nblk,),
        in_specs=[ublk, ublk, ublk, gws, vec, vec, ctspec, ctspec, s8, s8, s8, s8],
        out_specs=(s8, s8, ublk, vec, gws, vec, vec),
        out_shape=(s8shape, s8shape, jax.ShapeDtypeStruct((lp, sw), BF16), jax.ShapeDtypeStruct((1, sw), F32),
                   jax.ShapeDtypeStruct((sw, sw), F32), jax.ShapeDtypeStruct((1, sw), F32),
                   jax.ShapeDtypeStruct((1, sw), F32)),
        scratch_shapes=[pltpu.VMEM((rows, ns), F32), pltpu.VMEM((rows, ns), F32),
                        pltpu.VMEM((N_SEG, ns), F32), pltpu.VMEM((N_SEG, ns), F32)],
        compiler_params=_params(("arbitrary",)),
    )(dout, yp, u, gluw, glub, nw, csrt, csit, a8r, a8i, al8r, al8i)

    def main_body(dyp_sc, u_ref, xr_ref, xi_ref, xtr_ref, xti_ref, c0r_ref, c0i_ref, lcr_ref, lci_ref,
                  d_ref, bsrt_ref, bsit_ref, csrt_ref, csit_ref, ar_ref, ai_ref,
                  du_ref, dcr_ref, dci_ref, dbr_ref, dbi_ref, dar_ref, dai_ref,
                  lr_sc, li_sc, nr_sc, ni_sc):
        b = pl.program_id(0)

        @pl.when(b == 0)
        def _():
            nr_sc[...] = lcr_ref[...]
            ni_sc[...] = lci_ref[...]
            for ref in (dcr_ref, dci_ref, dbr_ref, dbi_ref, dar_ref, dai_ref):
                ref[...] = jnp.zeros_like(ref)

        for s in range(N_SEC):
            db = dyp_sc[:, s * secw:(s + 1) * secw]
            xs = pl.ds(s * secn, secn)
            dcr_ref[s] += _dot_tn(xr_ref[:, xs].astype(BF16), db)
            dci_ref[s] += _dot_tn(xi_ref[:, xs].astype(BF16), db)

        first = b == nblk - 1

        def acc_da(r0, prev_rows):
            for cc in range(ns // SCAN_CW):
                cs = pl.ds(cc * SCAN_CW, SCAN_CW)
                lr = lr_sc[pl.ds(r0, 8), cs]
                li = li_sc[pl.ds(r0, 8), cs]
                if prev_rows is None:
                    xpr = jnp.where(first, c0r_ref[:, cs], xtr_ref[:, cs])
                    xpi = jnp.where(first, c0i_ref[:, cs], xti_ref[:, cs])
                else:
                    xpr = xr_ref[prev_rows, cs]
                    xpi = xi_ref[prev_rows, cs]
                dar_ref[:, cs] += lr * xpr + li * xpi
                dai_ref[:, cs] += li * xpr - lr * xpi

        lam_scan(lambda s: dyp_sc[:, s * secw:(s + 1) * secw], csrt_ref, csit_ref, ar_ref, ai_ref,
                 lr_sc, li_sc, nr_sc, ni_sc, acc_da)

        for s in range(N_SEC):
            xs = pl.ds(s * secn, secn)
            us = pl.ds(s * secw, secw)
            lrb = lr_sc[:, xs].astype(BF16)
            lib = li_sc[:, xs].astype(BF16)
            du = _dot(lrb, bsrt_ref[s]) + _dot(lib, bsit_ref[s]) + d_ref[:, us] * dyp_sc[:, us].astype(F32)
            du_ref[:, us] = du.astype(BF16)
            ub = u_ref[:, us].astype(BF16)
            dbr_ref[s] += _dot_tn(ub, lrb)
            dbi_ref[s] += _dot_tn(ub, lib)

    tail = pl.BlockSpec((N_SEG, ns), lambda b: (jnp.maximum((nblk - 1 - b) * jb - 1, 0), 0))
    acc_c = pl.BlockSpec((N_SEC, secn, secw), lambda b: (0, 0, 0))
    acc_b = pl.BlockSpec((N_SEC, secw, secn), lambda b: (0, 0, 0))
    du, dcr, dci, dbr, dbi, dar, dai = pl.pallas_call(
        main_body, name="s5_bwd", grid=(nblk,),
        in_specs=[ublk, ublk, xblk, xblk, tail, tail, s8, s8, s8, s8,
                  vec, btspec, btspec, ctspec, ctspec, s8, s8],
        out_specs=(ublk, acc_c, acc_c, acc_b, acc_b, s8, s8),
        out_shape=(jax.ShapeDtypeStruct((lp, sw), BF16),
                   jax.ShapeDtypeStruct((N_SEC, secn, secw), F32),
                   jax.ShapeDtypeStruct((N_SEC, secn, secw), F32),
                   jax.ShapeDtypeStruct((N_SEC, secw, secn), F32),
                   jax.ShapeDtypeStruct((N_SEC, secw, secn), F32),
                   s8shape, s8shape),
        scratch_shapes=[pltpu.VMEM((rows, ns), F32), pltpu.VMEM((rows, ns), F32),
                        pltpu.VMEM((N_SEG, ns), F32), pltpu.VMEM((N_SEG, ns), F32)],
        compiler_params=_params(("arbitrary",)),
    )(dyp_all, u, xr, xi, xr, xi, c0r, c0i, lcr, lci, d, bsrt, bsit, csrt, csit, a8r, a8i)
    return du, d_nw, d_gw, d_gb, d_d, dcr, dci, dbr, dbi, dar, dai


def _outproj_fwd(h, ret, ssm, wo):
    lp, d = h.shape
    nck, rs, _ = wo.shape
    rw = ret.shape[1]
    tm = _tile(lp, 640)
    per = rw // rs

    def body(h_ref, ret_ref, ssm_ref, w_ref, o_ref):
        acc = h_ref[...]
        for c in range(nck):
            src = ret_ref if c < per else ssm_ref
            lo = (c % per) * rs
            acc = acc + _dot(src[:, lo:lo + rs], w_ref[c])
        o_ref[...] = acc

    row = lambda w: pl.BlockSpec((tm, w), lambda i: (i, 0))
    return pl.pallas_call(
        body, name="outproj_fwd", grid=(lp // tm,),
        in_specs=[row(d), row(rw), row(ssm.shape[1]), pl.BlockSpec((nck, rs, d), lambda i: (0, 0, 0))],
        out_specs=row(d), out_shape=jax.ShapeDtypeStruct((lp, d), F32),
        compiler_params=_params(("arbitrary",)),
    )(h, ret, ssm, wo)


def _outproj_bwd(dh, ret, ssm, wo):
    lp, d = dh.shape
    nck, rs, _ = wo.shape
    rw = ret.shape[1]
    sw = ssm.shape[1]
    tm = _tile(lp, 640)
    per = rw // rs
    last = lp // tm - 1

    def body(dh_ref, ret_ref, ssm_ref, w_ref, dret_ref, dssm_ref, dw_ref, acc_sc):
        i = pl.program_id(0)

        @pl.when(i == 0)
        def _():
            acc_sc[...] = jnp.zeros_like(acc_sc)

        dhb = dh_ref[...].astype(BF16)
        for c in range(nck):
            src, dst = (ret_ref, dret_ref) if c < per else (ssm_ref, dssm_ref)
            lo = (c % per) * rs
            dst[:, lo:lo + rs] = _dot_nt(dhb, w_ref[c])
            acc_sc[c] += _dot_tn(src[:, lo:lo + rs], dhb)

        @pl.when(i == last)
        def _():
            dw_ref[...] = acc_sc[...].astype(BF16)

    row = lambda w: pl.BlockSpec((tm, w), lambda i: (i, 0))
    wsp = pl.BlockSpec((nck, rs, d), lambda i: (0, 0, 0))
    return pl.pallas_call(
        body, name="outproj_bwd", grid=(lp // tm,),
        in_specs=[row(d), row(rw), row(sw), wsp],
        out_specs=(row(rw), row(sw), wsp),
        out_shape=(jax.ShapeDtypeStruct((lp, rw), F32), jax.ShapeDtypeStruct((lp, sw), F32),
                   jax.ShapeDtypeStruct((nck, rs, d), BF16)),
        scratch_shapes=[pltpu.VMEM((nck, rs, d), F32)],
        compiler_params=_params(("arbitrary",)),
    )(dh, ret, ssm, wo)


def _pack(arrs):
    flat = jnp.concatenate([a.reshape(-1).astype(F32) for a in arrs])
    n = flat.shape[0]
    rows = -(-n // (8 * LANE)) * 8
    return jnp.pad(flat, (0, rows * LANE - n)).reshape(rows, LANE)


def _unpack(packed, shapes):
    flat = packed.reshape(-1)
    out, off = [], 0
    for s in shapes:
        n = math.prod(s)
        out.append(flat[off:off + n].reshape(s))
        off += n
    return out


def _to_segments(a, seg_len):
    return a.reshape(N_SEG, seg_len, a.shape[1]).transpose(1, 0, 2).reshape(a.shape)


def _from_segments(a, seg_len):
    return a.reshape(seg_len, N_SEG, a.shape[1]).transpose(1, 0, 2).reshape(a.shape)


WEIGHT_NAMES = ['meta_tokens', 'ffn1_norm_w', 'ffn1_w_gate', 'ffn1_w_up', 'ffn1_w_down', 'mix_norm_w', 'w_in',
                'ret_norm_w', 'ssm_lambda_re', 'ssm_lambda_im', 'ssm_log_dt', 'ssm_b_re', 'ssm_b_im', 'ssm_c_re',
                'ssm_c_im', 'ssm_d', 'ssm_glu_w', 'ssm_glu_b', 'ssm_norm_w', 'w_out', 'ffn2_norm_w', 'ffn2_w_gate',
                'ffn2_w_up', 'ffn2_w_down', 'final_norm_w']
BIG = ['ffn1_w_gate', 'ffn1_w_up', 'ffn1_w_down', 'w_in', 'ssm_glu_w', 'w_out', 'ffn2_w_gate', 'ffn2_w_up',
       'ffn2_w_down']
TRANSPOSED = ['ffn1_w_gate', 'ffn1_w_up', 'ffn2_w_gate', 'ffn2_w_up']
BIG_EARLY = ['ffn1_w_gate', 'ffn1_w_up', 'ffn1_w_down']
BIG_LATE = [n for n in BIG if n not in BIG_EARLY]
SMALL = [n for n in WEIGHT_NAMES if n not in BIG]


def kernel(x, meta_tokens, ffn1_norm_w, ffn1_w_gate, ffn1_w_up, ffn1_w_down, mix_norm_w, w_in, ret_norm_w, ssm_lambda_re, ssm_lambda_im, ssm_log_dt, ssm_b_re, ssm_b_im, ssm_c_re, ssm_c_im, ssm_d, ssm_glu_w, ssm_glu_b, ssm_norm_w, w_out, ffn2_norm_w, ffn2_w_gate, ffn2_w_up, ffn2_w_down, final_norm_w, loss_target, m_meta_tokens, m_ffn1_norm_w, m_ffn1_w_gate, m_ffn1_w_up, m_ffn1_w_down, m_mix_norm_w, m_w_in, m_ret_norm_w, m_ssm_lambda_re, m_ssm_lambda_im, m_ssm_log_dt, m_ssm_b_re, m_ssm_b_im, m_ssm_c_re, m_ssm_c_im, m_ssm_d, m_ssm_glu_w, m_ssm_glu_b, m_ssm_norm_w, m_w_out, m_ffn2_norm_w, m_ffn2_w_gate, m_ffn2_w_up, m_ffn2_w_down, m_final_norm_w, v_meta_tokens, v_ffn1_norm_w, v_ffn1_w_gate, v_ffn1_w_up, v_ffn1_w_down, v_mix_norm_w, v_w_in, v_ret_norm_w, v_ssm_lambda_re, v_ssm_lambda_im, v_ssm_log_dt, v_ssm_b_re, v_ssm_b_im, v_ssm_c_re, v_ssm_c_im, v_ssm_d, v_ssm_glu_w, v_ssm_glu_b, v_ssm_norm_w, v_w_out, v_ffn2_norm_w, v_ffn2_w_gate, v_ffn2_w_up, v_ffn2_w_down, v_final_norm_w):
    args = locals()
    w = {n: args[n] for n in WEIGHT_NAMES}
    m = {n: args["m_" + n] for n in WEIGHT_NAMES}
    v = {n: args["v_" + n] for n in WEIGHT_NAMES}

    seq, d = x.shape[1], x.shape[2]
    lp = seq + CHUNK
    seg_len = lp // N_SEG
    rw = RET_HEADS * HEAD_DIM
    sw = ssm_d.shape[-1]
    groups = sw // SSM_GROUP
    ns = groups * SSM_STATE
    jb = _tile(seg_len, S5_STEPS, 8)
    chip = 2 * lax.axis_index("x") + lax.axis_index("y")

    as_fd = lambda t: jnp.swapaxes(t, -1, -2)
    shards = {n: (as_fd(w[n][0]) if n in TRANSPOSED else w[n][0]).astype(BF16) for n in BIG}
    early = [shards[n] for n in BIG_EARLY] + [meta_tokens]
    gathered = _gather_two_level("gather_early", early)
    gw = dict(zip(BIG_EARLY, gathered[:-1]))
    meta_full = jnp.transpose(gathered[-1], (1, 0, 2)).reshape(N_META, d)
    mixer_names, ffn2_names = BIG_LATE[:3], BIG_LATE[3:]
    late = [shards[n] for n in mixer_names]
    late2 = [shards[n] for n in ffn2_names]

    freqs = 1.0 / (ROPE_BASE ** (jnp.arange(0, HEAD_DIM, 2, dtype=F32) / HEAD_DIM))
    ang_c = (jnp.arange(lp // CHUNK, dtype=F32) * CHUNK - float(CHUNK - N_META))[:, None] * freqs[None, :]
    ang_r = jnp.arange(CHUNK, dtype=F32)[:, None] * freqs[None, :]
    cos_c, sin_c = jnp.cos(ang_c)[:, None, :], jnp.sin(ang_c)[:, None, :]
    cos_r, sin_r = jnp.cos(ang_r)[None], jnp.sin(ang_r)[None]
    cos_t = (cos_c * cos_r - sin_c * sin_r).reshape(lp, HEAD_DIM // 2)
    sin_t = (sin_c * cos_r + cos_c * sin_r).reshape(lp, HEAD_DIM // 2)
    cosf = jnp.concatenate([cos_t, cos_t], axis=1)
    sinf = jnp.concatenate([-sin_t, sin_t], axis=1)
    tables = _retention_tables(_tile(lp, RET_ROWS, CHUNK))

    lam_re, lam_im, log_dt = ssm_lambda_re[0], ssm_lambda_im[0], ssm_log_dt[0]
    b_re, b_im, c_re, c_im = ssm_b_re[0], ssm_b_im[0], ssm_c_re[0], ssm_c_im[0]
    (ar, ai, bbr, bbi), prep_vjp = jax.vjp(_s5_prepare, lam_re, lam_im, log_dt, b_re, b_im)
    dt = jnp.exp(log_dt)[:, None]
    el = jnp.exp(seg_len * lam_re * dt)
    alr = el * jnp.cos(seg_len * lam_im * dt)
    ali = el * jnp.sin(seg_len * lam_im * dt)
    bc8 = lambda t: jnp.broadcast_to(t.reshape(1, ns), (N_SEG, ns))
    a8r, a8i, al8r, al8i = bc8(ar), bc8(ai), bc8(alr), bc8(ali)
    bsr = _blockdiag_in(jnp.transpose(bbr, (0, 2, 1)))
    bsi = _blockdiag_in(jnp.transpose(bbi, (0, 2, 1)))
    csrt = _blockdiag_in(c_re)
    csit = _blockdiag_in(-c_im)
    tr = lambda t: jnp.transpose(t, (0, 2, 1))
    bsr_b, bsi_b = bsr.astype(BF16), bsi.astype(BF16)
    csr_b, csi_b = tr(csrt).astype(BF16), tr(csit).astype(BF16)
    bsrt_b, bsit_b = tr(bsr).astype(BF16), tr(bsi).astype(BF16)
    csrt_b, csit_b = csrt.astype(BF16), csit.astype(BF16)

    h0 = (jnp.concatenate([jnp.zeros((CHUNK - N_META, d), F32), meta_full], axis=0), x[0])
    (h1, g1, u1), late_half = _ffn_fwd("ffn1_fwd", h0, ffn1_norm_w, gw['ffn1_w_gate'], gw['ffn1_w_up'],
                                       gw['ffn1_w_down'], _allgather_chips_plan(late), late)
    gw.update(zip(mixer_names, _forward_sibling("gather_late_forward", late_half)))
    glu_full = gw['ssm_glu_w'].reshape(sw, sw)
    n2, q, k, vv, gate, u = _inproj_fwd(h1, mix_norm_w, gw['w_in'], cosf, sinf, rw)
    o, ret, sprev = _ret_fwd(q, k, vv, gate, ret_norm_w, tables)
    u_seg = _to_segments(u, seg_len)
    xr, xi, c0r, c0i, yp, ssm_seg, ffn2_half = _s5_fwd(u_seg, bsr_b, bsi_b, csr_b, csi_b, a8r, a8i, al8r, al8i,
                                                       ssm_d, glu_full, ssm_glu_b, ssm_norm_w, jb,
                                                       _allgather_chips_plan(late2), late2)
    gw.update(zip(ffn2_names, _forward_sibling("gather_ffn2_forward", ffn2_half)))
    ssm = _from_segments(ssm_seg, seg_len)
    h2 = _outproj_fwd(h1, ret, ssm, gw['w_out'])
    (dh3, g2, u2, loss_part, d_final), _ = _ffn_fwd(
        "ffn2_fwd_loss", h2, ffn2_norm_w, gw['ffn2_w_gate'], gw['ffn2_w_up'], gw['ffn2_w_down'],
        loss=(final_norm_w.reshape(1, d), loss_target[0]))

    (dh2, d_ffn2_norm, nb, daccb, ab, dgb, dub), _ = _ffn_bwd_act(
        "ffn2_bwd_act", dh3, h2, ffn2_norm_w, g2, u2, gw['ffn2_w_gate'], gw['ffn2_w_up'], gw['ffn2_w_down'])
    (dwg2, dwu2, dwd2), _ = _ffn_bwd_w("ffn2_bwd_w", nb, daccb, ab, dgb, dub)
    dret, dssm, dwo = _outproj_bwd(dh2, ret, ssm, gw['w_out'])
    (du_seg, d_ssm_norm, d_glu_w, d_glu_b, d_ssm_d, dcr_s, dci_s, dbr_s, dbi_s, dar8, dai8) = _s5_bwd(
        _to_segments(dssm, seg_len), u_seg, yp, xr, xi, c0r, c0i, bsrt_b, bsit_b, csrt_b, csit_b,
        a8r, a8i, al8r, al8i, ssm_d, glu_full, ssm_glu_b, ssm_norm_w, jb)
    du = _from_segments(du_seg, seg_len)
    dq, dk, dv, dgate, d_ret_norm = _ret_bwd(dret, q, k, vv, gate, o, sprev, ret_norm_w, tables, cosf, sinf)
    dh1, d_mix_norm, dwin = _inproj_bwd(dh2, h1, mix_norm_w, n2, gw['w_in'], dq, dk, dv, dgate, du)
    late_parts = {
        'w_in': dwin, 'ssm_glu_w': d_glu_w.reshape(N_CHIP, sw // N_CHIP, sw).astype(BF16), 'w_out': dwo,
        'ffn2_w_gate': dwg2, 'ffn2_w_up': dwu2, 'ffn2_w_down': dwd2,
    }
    late_list = [late_parts[n] for n in BIG_LATE]
    (dh0, d_ffn1_norm, nb, daccb, ab, dgb, dub), late_recv = _ffn_bwd_act(
        "ffn1_bwd_act", dh1, h0, ffn1_norm_w, g1, u1, gw['ffn1_w_gate'], gw['ffn1_w_up'], gw['ffn1_w_down'],
        _alltoall_chips_plan(late_list), late_list)
    grad_x = dh0[CHUNK:][None]
    d_meta = dh0[CHUNK - N_META:CHUNK]

    d_c_re = _blockdiag_out(tr(dcr_s), groups, SSM_GROUP, SSM_STATE)
    d_c_im = -_blockdiag_out(tr(dci_s), groups, SSM_GROUP, SSM_STATE)
    d_bbr = jnp.transpose(_blockdiag_out(dbr_s, groups, SSM_GROUP, SSM_STATE), (0, 2, 1))
    d_bbi = jnp.transpose(_blockdiag_out(dbi_s, groups, SSM_GROUP, SSM_STATE), (0, 2, 1))
    d_ar = jnp.sum(dar8, axis=0).reshape(groups, SSM_STATE)
    d_ai = jnp.sum(dai8, axis=0).reshape(groups, SSM_STATE)
    small_parts = [loss_part[0:1, :], d_meta, d_ffn1_norm, d_mix_norm, d_ret_norm, d_ar, d_ai, d_bbr, d_bbi,
                   d_c_re, d_c_im, d_ssm_d, d_glu_b, d_ssm_norm, d_ffn2_norm, d_final]
    small_shapes = [a.shape for a in small_parts]
    packed = _pack(small_parts)
    early_recv, (all_parts,) = _ffn_bwd_w_scatter("ffn1_bwd_w", nb, daccb, ab, dgb, dub, chip,
                                                  _allgather_all_plan([packed]), [packed])
    received = dict(zip(BIG_LATE + BIG_EARLY, late_recv + early_recv))
    chip_sums = _sum_slots("sum_chips", [received[n] for n in BIG], BF16)
    sib_sums = _swap_sibling("swap_sibling", chip_sums)
    (loss_row, g_meta_full, g_ffn1_norm, g_mix_norm, g_ret_norm, g_ar, g_ai, g_bbr, g_bbi, g_c_re, g_c_im,
     g_ssm_d, g_glu_b, g_ssm_norm, g_ffn2_norm, g_final) = _unpack(_sum_slots("sum_small", [all_parts], F32)[0],
                                                                  small_shapes)
    g_lam_re, g_lam_im, g_log_dt, g_b_re, g_b_im = prep_vjp((g_ar, g_ai, g_bbr, g_bbi))
    loss = loss_row[0, 0]
    g_meta = lax.dynamic_slice(g_meta_full, (0, chip * (d // N_CHIP)), (N_META, d // N_CHIP))
    small_grads = {
        'meta_tokens': g_meta, 'ffn1_norm_w': g_ffn1_norm, 'mix_norm_w': g_mix_norm, 'ret_norm_w': g_ret_norm,
        'ssm_lambda_re': g_lam_re[None], 'ssm_lambda_im': g_lam_im[None], 'ssm_log_dt': g_log_dt[None],
        'ssm_b_re': g_b_re[None], 'ssm_b_im': g_b_im[None], 'ssm_c_re': g_c_re[None], 'ssm_c_im': g_c_im[None],
        'ssm_d': g_ssm_d, 'ssm_glu_b': g_glu_b, 'ssm_norm_w': g_ssm_norm, 'ffn2_norm_w': g_ffn2_norm,
        'final_norm_w': g_final.reshape(d),
    }

    grads, deltas, new_m, new_v = {}, {}, {}, {}
    g_pair = {n: [mine, sib] for n, mine, sib in zip(BIG, chip_sums, sib_sums)}
    view = lambda n, t: as_fd(t) if n in TRANSPOSED else t
    big_out = _adam("adam_big", [(view(n, w[n]), view(n, m[n]), view(n, v[n])) for n in BIG], [g_pair[n] for n in BIG])
    for n, outs in zip(BIG, big_out):
        grads[n], deltas[n], new_m[n], new_v[n] = [view(n, t) for t in outs]
    sm_shapes = [w[n].shape for n in SMALL]
    sm_out = _adam("adam_small", [(_pack([w[n] for n in SMALL]), _pack([m[n] for n in SMALL]),
                                  _pack([v[n] for n in SMALL]))],
                   [[_pack([small_grads[n].reshape(w[n].shape) for n in SMALL])]])[0]
    for dst, packed in zip((grads, deltas, new_m, new_v), sm_out):
        for n, t in zip(SMALL, _unpack(packed, sm_shapes)):
            dst[n] = t

    return (loss, grad_x, *[grads[n] for n in WEIGHT_NAMES], *[deltas[n] for n in WEIGHT_NAMES],
            *[new_m[n] for n in WEIGHT_NAMES], *[new_v[n] for n in WEIGHT_NAMES])
```

```python
import functools
import math

import jax
import jax.numpy as jnp
from jax import lax
from jax.experimental import pallas as pl
from jax.experimental.pallas import tpu as pltpu

N_META = 16
RET_HEADS = 4
HEAD_DIM = 128
SSM_GROUP = 16
SSM_STATE = 64
CHUNK = 128
ROPE_BASE = 10000.0
EPS = 1e-6
FFN_RES = 0.5
N_SEG = 8
N_SEC = 4
N_CHIP = 4
LANE = 128
FFN_CPS = 2
BWD_W_ROWS = 1664

ADAM_LR = 0.001
ADAM_B1 = 0.9
ADAM_B2 = 0.999
ADAM_EPS = 1e-08
ADAM_WD = 0.01
ADAM_STEP = 10

VMEM_LIMIT = 56 * 1024 * 1024

F32 = jnp.float32
BF16 = jnp.bfloat16
MESH = pl.DeviceIdType.MESH


def _dot(a, b):
    return jnp.dot(a, b, preferred_element_type=F32)


def _dot_nt(a, b):
    return lax.dot_general(a, b, (((1,), (1,)), ((), ())), preferred_element_type=F32)


def _dot_tn(a, b):
    return lax.dot_general(a, b, (((0,), (0,)), ((), ())), preferred_element_type=F32)


def _tile(n, target, mult=64):
    best = None
    t = mult
    while t <= min(n, target):
        if n % t == 0:
            best = t
        t += mult
    assert best is not None, (n, target)
    return best


def _params(sem, vmem=VMEM_LIMIT):
    return pltpu.CompilerParams(dimension_semantics=sem, vmem_limit_bytes=vmem)


def _rms_stats(xf):
    r = lax.rsqrt(jnp.mean(xf * xf, axis=-1, keepdims=True) + EPS)
    return xf * r, r


def _rms_bwd(dy, xh, r, w):
    dxh = dy * w
    return r * (dxh - xh * jnp.mean(dxh * xh, axis=-1, keepdims=True))


def _sigmoid(x):
    return 0.5 * jnp.tanh(0.5 * x) + 0.5


GELU_K0 = math.sqrt(2.0 / math.pi)
GELU_K1 = 0.044715


CHIP_MASKS = [(1, 0, 0), (0, 1, 0), (1, 1, 0)]
ALL_MASKS = [(0, 0, 1), (0, 1, 0), (0, 1, 1), (1, 0, 0), (1, 0, 1), (1, 1, 0), (1, 1, 1)]
SIB_MASKS = [(0, 0, 1)]
ANY_SPEC = pl.BlockSpec(memory_space=pl.ANY)
MULTI_SUM_STEPS = 4
MULTI_ADAM_STEPS = 8


class _Plan:
    def __init__(self, arrays, masks, n_slots, src_slotted, dst_slotted, local_copy, half=False, forward=False):
        self.shapes = [(a.shape, a.dtype) for a in arrays]
        self.n = len(arrays)
        self.masks = masks
        self.n_slots = n_slots
        self.src_slotted, self.dst_slotted, self.local_copy = src_slotted, dst_slotted, local_copy
        self.half, self.forward = half, forward
        self.n_cp = self.n * len(masks) * (len(CHIP_MASKS) if forward else 1)

    def out_shape(self):
        out = []
        for shp, dt in self.shapes:
            if self.dst_slotted and not self.src_slotted:
                shp = (self.n_slots,) + shp
            elif self.src_slotted and not self.dst_slotted:
                shp = shp[1:]
            out.append(jax.ShapeDtypeStruct(shp, dt))
        return tuple(out)

    def scratch(self):
        return [pltpu.SemaphoreType.DMA((self.n_cp,)), pltpu.SemaphoreType.DMA((self.n_cp,)),
                pltpu.SemaphoreType.DMA((self.n,))]

    def _slot(self, px, py, pc):
        if self.n_slots == 8:
            return 4 * px + 2 * py + pc
        if self.n_slots == 4:
            return 2 * px + py
        return pc

    def copies(self, ins, outs, sems):
        send_sems, recv_sems, loc_sems = sems
        x, y, c = lax.axis_index("x"), lax.axis_index("y"), lax.axis_index("c")
        me = self._slot(x, y, c)
        n_m = len(self.masks)
        cps = []
        for a in range(self.n):
            if self.forward:
                rows = self.shapes[a][0][-2] // 2
                mine = pl.ds(pl.multiple_of(c * rows, 8), rows)
                for j, (mx, my, _) in enumerate(CHIP_MASKS):
                    blk = outs[a].at[2 * (1 - x if mx else x) + (1 - y if my else y), mine]
                    k = a * len(CHIP_MASKS) + j
                    cps.append(pltpu.make_async_remote_copy(
                        src_ref=blk, dst_ref=blk, send_sem=send_sems.at[k], recv_sem=recv_sems.at[k],
                        device_id=(x, y, 1 - c), device_id_type=MESH))
                continue
            if self.local_copy:
                src = ins[a].at[me] if self.src_slotted else ins[a]
                cps.append(pltpu.make_async_copy(src, outs[a].at[me], loc_sems.at[a]))
            for mi, (mx, my, mc) in enumerate(self.masks):
                px = 1 - x if mx else x
                py = 1 - y if my else y
                pc = 1 - c if mc else c
                src = ins[a].at[self._slot(px, py, pc)] if self.src_slotted else ins[a]
                dst = outs[a].at[me] if self.dst_slotted else outs[a]
                if self.half:
                    rows = src.shape[-2] // 2
                    mine = pl.ds(pl.multiple_of(c * rows, 8), rows)
                    src, dst = src.at[mine], dst.at[mine]
                k = a * n_m + mi
                cps.append(pltpu.make_async_remote_copy(
                    src_ref=src, dst_ref=dst, send_sem=send_sems.at[k], recv_sem=recv_sems.at[k],
                    device_id=(px, py, pc), device_id_type=MESH))
        return cps


def _exchange(name, plan, arrays):
    n = plan.n

    def body(*refs):
        cps = plan.copies(refs[:n], refs[n:2 * n], refs[2 * n:])
        for cp in cps:
            cp.start()
        for cp in cps:
            cp.wait()

    outs = pl.pallas_call(
        body, name=name, out_shape=plan.out_shape(),
        in_specs=[ANY_SPEC] * n, out_specs=tuple([ANY_SPEC] * n), scratch_shapes=plan.scratch(),
        input_output_aliases={i: i for i in range(n)} if plan.forward else {},
    )(*arrays)
    return list(outs)


def _pcall(body, *, name, grid, in_specs, out_specs, out_shape, scratch_shapes, args, plan=None, plan_args=()):
    sem = ("arbitrary",) * len(grid)
    if plan is None:
        return pl.pallas_call(body, name=name, grid=grid, in_specs=in_specs, out_specs=out_specs,
                              out_shape=out_shape, scratch_shapes=scratch_shapes,
                              compiler_params=_params(sem))(*args), []
    n_in, n_out, n_scr, n_p = len(in_specs), len(out_specs), len(scratch_shapes), plan.n

    def wrapped(*refs):
        ins = refs[:n_in]
        p_ins = refs[n_in:n_in + n_p]
        o0 = n_in + n_p
        outs = refs[o0:o0 + n_out]
        p_outs = refs[o0 + n_out:o0 + n_out + n_p]
        s0 = o0 + n_out + n_p
        scr = refs[s0:s0 + n_scr]
        sems = refs[s0 + n_scr:]
        ids = [pl.program_id(i) for i in range(len(grid))]
        first = functools.reduce(jnp.logical_and, [i == 0 for i in ids])
        last = functools.reduce(jnp.logical_and, [i == g - 1 for i, g in zip(ids, grid)])

        @pl.when(first)
        def _():
            for cp in plan.copies(p_ins, p_outs, sems):
                cp.start()

        body(*ins, *outs, *scr)

        @pl.when(last)
        def _():
            for cp in plan.copies(p_ins, p_outs, sems):
                cp.wait()

    res = pl.pallas_call(
        wrapped, name=name, grid=grid,
        in_specs=list(in_specs) + [ANY_SPEC] * n_p,
        out_specs=tuple(out_specs) + (ANY_SPEC,) * n_p,
        out_shape=tuple(out_shape) + plan.out_shape(),
        scratch_shapes=list(scratch_shapes) + plan.scratch(),
        compiler_params=_params(sem),
    )(*args, *plan_args)
    return res[:n_out], list(res[n_out:])


def _allgather_chips_plan(arrays):
    return _Plan(arrays, CHIP_MASKS, 4, False, True, True, half=True)


def _gather_two_level(name, arrays):
    n = len(arrays)
    ici = _allgather_chips_plan(arrays)
    fwd = _Plan(ici.out_shape(), SIB_MASKS, 4, True, True, False, forward=True)
    n_m = len(CHIP_MASKS)

    def body(*refs):
        ins, outs, sems = refs[:n], refs[n:2 * n], refs[2 * n:]
        ici_cps = ici.copies(ins, outs, sems[:3])
        fwd_cps = fwd.copies(None, outs, sems[3:])
        for cp in ici_cps:
            cp.start()
        for a in range(n):
            for m in range(n_m):
                ici_cps[a * (n_m + 1) + 1 + m].wait_recv()
                fwd_cps[a * n_m + m].start()
        for a in range(n):
            ici_cps[a * (n_m + 1)].wait()
            for m in range(n_m):
                ici_cps[a * (n_m + 1) + 1 + m].wait_send()
        for cp in fwd_cps:
            cp.wait()

    return list(pl.pallas_call(
        body, name=name, out_shape=ici.out_shape(),
        in_specs=[ANY_SPEC] * n, out_specs=tuple([ANY_SPEC] * n), scratch_shapes=ici.scratch() + fwd.scratch(),
    )(*arrays))


def _forward_sibling(name, gathered):
    return _exchange(name, _Plan(gathered, SIB_MASKS, 4, True, True, False, forward=True), gathered)


def _alltoall_chips_plan(arrays):
    return _Plan(arrays, CHIP_MASKS, 4, True, True, True)


def _swap_sibling(name, arrays):
    return _exchange(name, _Plan(arrays, SIB_MASKS, 2, False, False, False), arrays)


def _allgather_all_plan(arrays):
    return _Plan(arrays, ALL_MASKS, 8, False, True, True)


def _sum_slots(name, arrs, out_dtype):
    s = arrs[0].shape[0]
    n = len(arrs)
    steps = arrs[0].shape[1] // _tile(arrs[0].shape[1], 512, 8) if n == 1 else MULTI_SUM_STEPS
    for a in arrs:
        assert a.shape[1] % (16 * steps) == 0 or n == 1, a.shape

    def body(*refs):
        for a_ref, o_ref in zip(refs[:n], refs[n:]):
            acc = a_ref[0].astype(F32)
            for i in range(1, s):
                acc = acc + a_ref[i].astype(F32)
            o_ref[...] = acc.astype(out_dtype)

    return list(pl.pallas_call(
        body, name=name, grid=(steps,),
        in_specs=[pl.BlockSpec((s, a.shape[1] // steps, a.shape[2]), lambda i: (0, i, 0)) for a in arrs],
        out_specs=tuple(pl.BlockSpec((a.shape[1] // steps, a.shape[2]), lambda i: (i, 0)) for a in arrs),
        out_shape=tuple(jax.ShapeDtypeStruct(a.shape[1:], out_dtype) for a in arrs),
        compiler_params=_params(("arbitrary",)),
    )(*arrs))


def _adam_math(w, g, m, v):
    m_new = ADAM_B1 * m + (1.0 - ADAM_B1) * g
    v_new = ADAM_B2 * v + (1.0 - ADAM_B2) * (g * g)
    m_hat = m_new / (1.0 - ADAM_B1 ** ADAM_STEP)
    v_hat = v_new / (1.0 - ADAM_B2 ** ADAM_STEP)
    delta = -ADAM_LR * (m_hat / (jnp.sqrt(v_hat) + ADAM_EPS) + ADAM_WD * w)
    return delta, m_new, v_new


def _adam(name, wmv, g_parts):
    n_w = len(wmv)
    n_g = len(g_parts[0])
    lead = wmv[0][0].ndim == 3
    at = (lambda ref: ref.at[0]) if lead else (lambda ref: ref)
    n_in = 3 + n_g
    rows0 = wmv[0][0].shape[-2]
    steps = rows0 // _tile(rows0, 256, 8) if n_w == 1 else MULTI_ADAM_STEPS
    for w, _, _ in wmv:
        assert w.shape[-2] % (8 * steps) == 0, w.shape

    def body(*refs):
        for j in range(n_w):
            ins = refs[j * n_in:(j + 1) * n_in]
            outs = refs[n_w * n_in + 4 * j:n_w * n_in + 4 * j + 4]
            w_ref, m_ref, v_ref = [at(t) for t in ins[:3]]
            g_out, d_out, m_out, v_out = [at(t) for t in outs]
            g = ins[3][...].astype(F32)
            for gr in ins[4:]:
                g = g + gr[...].astype(F32)
            delta, m_new, v_new = _adam_math(w_ref[...], g, m_ref[...], v_ref[...])
            g_out[...] = g
            d_out[...] = delta
            m_out[...] = m_new
            v_out[...] = v_new

    in_specs, out_specs, out_shape, args = [], [], [], []
    for (w, m, v), gp in zip(wmv, g_parts):
        r, c = w.shape[-2:]
        spec = pl.BlockSpec((r // steps, c), lambda i: (i, 0))
        wspec = pl.BlockSpec((1, r // steps, c), lambda i: (0, i, 0)) if lead else spec
        in_specs += [wspec] * 3 + [spec] * n_g
        out_specs += [wspec] * 4
        out_shape += [jax.ShapeDtypeStruct(w.shape, F32)] * 4
        args += [w, m, v, *gp]
    res = pl.pallas_call(
        body, name=name, grid=(steps,),
        in_specs=in_specs, out_specs=tuple(out_specs), out_shape=tuple(out_shape),
        compiler_params=_params(("arbitrary",)),
    )(*args)
    return [tuple(res[4 * j:4 * j + 4]) for j in range(n_w)]


SUB_ROWS = 32
FFN_BWD_ROWS = 416
FFN_FWD_CPS = 4
FFN_FWD_ROWS = 416
FFN_LOSS_ROWS = 416
RET_ROWS = 640
S5_STEPS = 104


def _tile_parts(tm, d, head, x):
    nsub = tm // SUB_ROWS
    off = head.shape[0] // SUB_ROWS
    specs = [pl.BlockSpec(head.shape, lambda i, k: (0, 0))] + [
        pl.BlockSpec((SUB_ROWS, d), lambda i, k, j=j: (jnp.maximum(i * nsub + j - off, 0), 0)) for j in range(nsub)]

    def assemble(i, part_refs, h_sc):
        head_ref, x_refs = part_refs[0], part_refs[1:]
        for j in range(nsub):
            rows = slice(j * SUB_ROWS, (j + 1) * SUB_ROWS)
            val = x_refs[j][...]
            if j < off:
                val = jnp.where(i == 0, head_ref[rows, :], val)
            h_sc[rows, :] = val

    return specs, [head] + [x] * nsub, assemble


def _h_source(body, h, tm, d):
    if not isinstance(h, tuple):
        return body, [pl.BlockSpec((tm, d), lambda i, k: (i, 0))], [h], []
    specs, args, assemble = _tile_parts(tm, d, *h)
    n_h = len(specs)

    def with_parts(*refs):
        h_sc = refs[-1]

        @pl.when(pl.program_id(1) == 0)
        def _():
            assemble(pl.program_id(0), refs[:n_h], h_sc)

        body(h_sc, *refs[n_h:-1])

    return with_parts, specs, args, [pltpu.VMEM((tm, d), F32)]


def _ffn_fwd(name, h, nw, wg, wu, wd, plan=None, plan_args=(), loss=None):
    lp, d = (h[0].shape[0] + h[1].shape[0], h[1].shape[1]) if isinstance(h, tuple) else h.shape
    nck, f, _ = wg.shape
    tm = _tile(lp, FFN_FWD_ROWS if loss is None else FFN_LOSS_ROWS, SUB_ROWS)
    cps = FFN_FWD_CPS
    last = nck // cps - 1
    n_t = 0
    if loss is not None:
        t_specs, t_args, t_assemble = _tile_parts(tm, d, jnp.zeros((lp - loss[1].shape[0], d), F32), loss[1])
        n_t = len(t_specs)

    def body(h_ref, nw_ref, wg_ref, wu_ref, wd_ref, *rest):
        if loss is not None:
            fw_ref, t_parts, rest = rest[0], rest[1:1 + n_t], rest[1 + n_t:]
            ho_ref, g_ref, u_ref, loss_ref, dfw_ref, n_sc, acc_sc, t_sc = rest
        else:
            ho_ref, g_ref, u_ref, n_sc, acc_sc = rest
        i = pl.program_id(0)
        k = pl.program_id(1)

        @pl.when(k == 0)
        def _():
            xh, _ = _rms_stats(h_ref[...])
            n_sc[...] = (xh * nw_ref[...]).astype(BF16)
            acc_sc[...] = jnp.zeros_like(acc_sc)

        n = n_sc[...]
        acc = acc_sc[...]
        for c in range(cps):
            g = _dot_nt(n, wg_ref[c])
            u = _dot_nt(n, wu_ref[c])
            g_ref[c] = g.astype(BF16)
            u_ref[c] = u.astype(BF16)
            a = (g * _sigmoid(g) * u).astype(BF16)
            acc = acc + _dot(a, wd_ref[c])
        acc_sc[...] = acc

        if loss is None:
            @pl.when(k == last)
            def _():
                ho_ref[...] = h_ref[...] + FFN_RES * acc_sc[...]
            return

        @pl.when(jnp.logical_and(i == 0, k == 0))
        def _():
            loss_ref[...] = jnp.zeros_like(loss_ref)
            dfw_ref[...] = jnp.zeros_like(dfw_ref)

        @pl.when(k == last)
        def _():
            t_assemble(i, t_parts, t_sc)
            xh, r = _rms_stats(h_ref[...] + FFN_RES * acc_sc[...])
            w = fw_ref[...]
            head_rows = lp - loss[1].shape[0]
            row = lax.broadcasted_iota(jnp.int32, (tm, d), 0) + i * tm
            err = jnp.where(row < head_rows, 0.0, xh * w - t_sc[...])
            loss_ref[...] += 0.5 * jnp.sum(err * err) / d
            dout = err * (1.0 / d)
            dfw_ref[...] += jnp.sum(dout * xh, axis=0, keepdims=True)
            ho_ref[...] = _rms_bwd(dout, xh, r, w)

    body, h_specs, h_args, h_scratch = _h_source(body, h, tm, d)
    vec = pl.BlockSpec((1, d), lambda i, k: (0, 0))
    w_fd = pl.BlockSpec((cps, f, d), lambda i, k: (k, 0, 0), **({'pipeline_mode': pl.Buffered(1)} if cps == nck else {}))
    hid = pl.BlockSpec((cps, tm, f), lambda i, k: (k, i, 0))
    hshape = jax.ShapeDtypeStruct((nck, lp, f), BF16)
    args, in_specs = (*h_args, nw, wg, wu, wd), h_specs + [vec, w_fd, w_fd, w_fd]
    out_specs = (pl.BlockSpec((tm, d), lambda i, k: (i, 0)), hid, hid)
    out_shape = (jax.ShapeDtypeStruct((lp, d), F32), hshape, hshape)
    scratch = [pltpu.VMEM((tm, d), BF16), pltpu.VMEM((tm, d), F32)]
    if loss is not None:
        args, in_specs = (*args, loss[0], *t_args), in_specs + [vec] + t_specs
        out_specs += (pl.BlockSpec((8, LANE), lambda i, k: (0, 0)), vec)
        out_shape += (jax.ShapeDtypeStruct((8, LANE), F32), jax.ShapeDtypeStruct((1, d), F32))
        scratch = scratch + [pltpu.VMEM((tm, d), F32)]
    return _pcall(
        body, name=name, grid=(lp // tm, nck // cps), plan=plan, plan_args=plan_args,
        args=args, in_specs=in_specs, out_specs=out_specs, out_shape=out_shape,
        scratch_shapes=scratch + h_scratch)


def _ffn_bwd_act(name, dh, h, nw, g, u, wg, wu, wd, plan=None, plan_args=()):
    lp, d = dh.shape
    nck, f, _ = wg.shape
    tm = _tile(lp, FFN_BWD_ROWS, SUB_ROWS)
    last = nck // FFN_CPS - 1

    def body(h_ref, dh_ref, nw_ref, g_ref, u_ref, wg_ref, wu_ref, wd_ref,
             dhi_ref, dnw_ref, n_ref, dacc_ref, a_ref, dg_ref, du_ref,
             xh_sc, r_sc, dn_sc):
        i = pl.program_id(0)
        k = pl.program_id(1)

        @pl.when(k == 0)
        def _():
            xh, r = _rms_stats(h_ref[...])
            xh_sc[...] = xh
            r_sc[...] = r
            n_ref[...] = (xh * nw_ref[...]).astype(BF16)
            dacc_ref[...] = (FFN_RES * dh_ref[...]).astype(BF16)
            dn_sc[...] = jnp.zeros_like(dn_sc)

        @pl.when(jnp.logical_and(i == 0, k == 0))
        def _():
            dnw_ref[...] = jnp.zeros_like(dnw_ref)

        dacc = dacc_ref[...]
        dn = dn_sc[...]
        for c in range(FFN_CPS):
            gv = g_ref[c].astype(F32)
            uv = u_ref[c].astype(F32)
            sg = _sigmoid(gv)
            sil = gv * sg
            da = _dot_nt(dacc, wd_ref[c])
            dgk = (da * uv * (sg * (1.0 + gv * (1.0 - sg)))).astype(BF16)
            duk = (da * sil).astype(BF16)
            a_ref[c] = (sil * uv).astype(BF16)
            dg_ref[c] = dgk
            du_ref[c] = duk
            dn = dn + _dot(dgk, wg_ref[c]) + _dot(duk, wu_ref[c])
        dn_sc[...] = dn

        @pl.when(k == last)
        def _():
            dnl = dn_sc[...]
            xh = xh_sc[...]
            dhi_ref[...] = dh_ref[...] + _rms_bwd(dnl, xh, r_sc[...], nw_ref[...])
            dnw_ref[...] += jnp.sum(dnl * xh, axis=0, keepdims=True)

    body, h_specs, h_args, h_scratch = _h_source(body, h, tm, d)
    row = pl.BlockSpec((tm, d), lambda i, k: (i, 0))
    vec = pl.BlockSpec((1, d), lambda i, k: (0, 0))
    hid = pl.BlockSpec((FFN_CPS, tm, f), lambda i, k: (k, i, 0))
    w_fd = pl.BlockSpec((FFN_CPS, f, d), lambda i, k: (k, 0, 0))
    rshape = jax.ShapeDtypeStruct((lp, d), BF16)
    hshape = jax.ShapeDtypeStruct((nck, lp, f), BF16)
    return _pcall(
        body, name=name, grid=(lp // tm, nck // FFN_CPS), plan=plan, plan_args=plan_args,
        args=(*h_args, dh, nw, g, u, wg, wu, wd),
        in_specs=h_specs + [row, vec, hid, hid, w_fd, w_fd, w_fd],
        out_specs=(row, vec, row, row, hid, hid, hid),
        out_shape=(jax.ShapeDtypeStruct((lp, d), F32), jax.ShapeDtypeStruct((1, d), F32),
                   rshape, rshape, hshape, hshape, hshape),
        scratch_shapes=[pltpu.VMEM((tm, d), F32), pltpu.VMEM((tm, 1), F32), pltpu.VMEM((tm, d), F32)] + h_scratch)


def _ffn_bwd_w(name, n, dacc, a, dg, du, plan=None, plan_args=()):
    lp, d = n.shape
    nck, _, f = a.shape
    tm = _tile(lp, BWD_W_ROWS)
    last = lp // tm - 1

    def body(n_ref, dacc_ref, a_ref, dg_ref, du_ref, dwg_ref, dwu_ref, dwd_ref, ag_sc, au_sc, ad_sc):
        i = pl.program_id(1)

        @pl.when(i == 0)
        def _():
            ag_sc[...] = jnp.zeros_like(ag_sc)
            au_sc[...] = jnp.zeros_like(au_sc)
            ad_sc[...] = jnp.zeros_like(ad_sc)

        nv = n_ref[...]
        ag_sc[...] += _dot_tn(dg_ref[0], nv)
        au_sc[...] += _dot_tn(du_ref[0], nv)
        ad_sc[...] += _dot_tn(a_ref[0], dacc_ref[...])

        @pl.when(i == last)
        def _():
            dwg_ref[0] = ag_sc[...].astype(BF16)
            dwu_ref[0] = au_sc[...].astype(BF16)
            dwd_ref[0] = ad_sc[...].astype(BF16)

    row = pl.BlockSpec((tm, d), lambda k, i: (i, 0))
    hid = pl.BlockSpec((1, tm, f), lambda k, i: (k, i, 0))
    w_fd = pl.BlockSpec((1, f, d), lambda k, i: (k, 0, 0))
    wshape = jax.ShapeDtypeStruct((nck, f, d), BF16)
    return _pcall(
        body, name=name, grid=(nck, lp // tm), plan=plan, plan_args=plan_args, args=(n, dacc, a, dg, du),
        in_specs=[row, row, hid, hid, hid], out_specs=(w_fd, w_fd, w_fd), out_shape=(wshape,) * 3,
        scratch_shapes=[pltpu.VMEM((f, d), F32)] * 3)


def _ffn_bwd_w_scatter(name, n, dacc, a, dg, du, chip, plan, plan_args):
    lp, d = n.shape
    nck, _, f = a.shape
    tm = _tile(lp, BWD_W_ROWS)
    last_i = lp // tm - 1
    n_w = 3
    n_p = plan.n

    def body(me_ref, n_ref, dacc_ref, a_ref, dg_ref, du_ref, *rest):
        p_ins = rest[:n_p]
        recv = rest[n_p:n_p + n_w]
        p_outs = rest[n_p + n_w:2 * n_p + n_w]
        acc = rest[2 * n_p + n_w:2 * n_p + 2 * n_w]
        stage, send_sems, recv_sems, loc_sems = rest[2 * n_p + 2 * n_w:2 * n_p + 2 * n_w + 4]
        p_sems = rest[2 * n_p + 2 * n_w + 4:]
        p = pl.program_id(0)
        i = pl.program_id(1)
        me = me_ref[0]
        c = lax.axis_index("c")

        def send(w, pos):
            kk = jnp.bitwise_xor(me, nck - 1 - pos)
            diff = jnp.bitwise_xor(kk, me)
            m = jnp.where(diff == 2, 0, jnp.where(diff == 1, 1, 2))
            return pltpu.make_async_remote_copy(
                src_ref=stage.at[lax.rem(pos, 2), w], dst_ref=recv[w].at[me],
                send_sem=send_sems.at[w * 3 + m], recv_sem=recv_sems.at[w * 3 + m],
                device_id=(lax.div(kk, 2), lax.rem(kk, 2), c), device_id_type=MESH)

        @pl.when(jnp.logical_and(p == 0, i == 0))
        def _():
            for cp in plan.copies(p_ins, p_outs, p_sems):
                cp.start()

        @pl.when(i == 0)
        def _():
            for t in acc:
                t[...] = jnp.zeros_like(t)

        nv = n_ref[...]
        acc[0][...] += _dot_tn(dg_ref[0], nv)
        acc[1][...] += _dot_tn(du_ref[0], nv)
        acc[2][...] += _dot_tn(a_ref[0], dacc_ref[...])

        @pl.when(jnp.logical_and(i == last_i, p >= 2))
        def _():
            for w in range(n_w):
                send(w, p - 2).wait_send()

        @pl.when(i == last_i)
        def _():
            for w in range(n_w):
                stage[lax.rem(p, 2), w] = acc[w][...].astype(BF16)

        @pl.when(jnp.logical_and(i == last_i, p < nck - 1))
        def _():
            for w in range(n_w):
                send(w, p).start()

        @pl.when(jnp.logical_and(i == last_i, p == nck - 1))
        def _():
            own = [pltpu.make_async_copy(stage.at[(nck - 1) % 2, w], recv[w].at[me], loc_sems.at[w])
                   for w in range(n_w)]
            for cp in own:
                cp.start()
            for w in range(n_w):
                send(w, nck - 2).wait_send()
            for cp in own:
                cp.wait()
            for w in range(n_w):
                for m in range(3):
                    pltpu.make_async_remote_copy(
                        src_ref=stage.at[0, w], dst_ref=recv[w].at[me],
                        send_sem=send_sems.at[w * 3 + m], recv_sem=recv_sems.at[w * 3 + m],
                        device_id=(0, 0, c), device_id_type=MESH).wait_recv()
            for cp in plan.copies(p_ins, p_outs, p_sems):
                cp.wait()

    chunk = lambda k, me_ref: jnp.bitwise_xor(me_ref[0], nck - 1 - k)
    row = pl.BlockSpec((tm, d), lambda k, i, me_ref: (i, 0))
    hid = pl.BlockSpec((1, tm, f), lambda k, i, me_ref: (chunk(k, me_ref), i, 0))
    wshape = jax.ShapeDtypeStruct((nck, f, d), BF16)
    res = pl.pallas_call(
        body, name=name,
        grid_spec=pltpu.PrefetchScalarGridSpec(
            num_scalar_prefetch=1, grid=(nck, lp // tm),
            in_specs=[row, row, hid, hid, hid] + [ANY_SPEC] * n_p,
            out_specs=(ANY_SPEC,) * (n_w + n_p),
            scratch_shapes=[pltpu.VMEM((f, d), F32)] * n_w + [
                pltpu.VMEM((2, n_w, f, d), BF16), pltpu.SemaphoreType.DMA((n_w * 3,)),
                pltpu.SemaphoreType.DMA((n_w * 3,)), pltpu.SemaphoreType.DMA((n_w,))] + plan.scratch()),
        out_shape=(wshape,) * n_w + plan.out_shape(),
        compiler_params=_params(("arbitrary", "arbitrary")),
    )(chip.reshape(1).astype(jnp.int32), n, dacc, a, dg, du, *plan_args)
    return list(res[:n_w]), list(res[n_w:])


def _inproj_fwd(h, nw, w_in, cosf, sinf, rw):
    lp, d = h.shape
    nck, _, ps = w_in.shape
    proj = nck * ps
    sw = proj - 4 * rw
    tm = _tile(lp, 640)
    scale = HEAD_DIM ** -0.5
    heads = rw // HEAD_DIM

    def body(h_ref, nw_ref, w_ref, cos_ref, sin_ref, n_ref, q_ref, k_ref, v_ref, g_ref, u_ref, p_sc):
        xh, _ = _rms_stats(h_ref[...])
        n = (xh * nw_ref[...]).astype(BF16)
        n_ref[...] = n
        for c in range(nck):
            p_sc[:, c * ps:(c + 1) * ps] = _dot(n, w_ref[c])
        cs = cos_ref[...]
        sn = sin_ref[...]
        for hh in range(heads):
            lo = hh * HEAD_DIM
            qh = p_sc[:, lo:lo + HEAD_DIM]
            q_ref[:, lo:lo + HEAD_DIM] = (qh * cs + pltpu.roll(qh, HEAD_DIM // 2, 1) * sn).astype(BF16)
            kh = p_sc[:, rw + lo:rw + lo + HEAD_DIM]
            k_ref[:, lo:lo + HEAD_DIM] = ((kh * cs + pltpu.roll(kh, HEAD_DIM // 2, 1) * sn) * scale).astype(BF16)
        v_ref[...] = p_sc[:, 2 * rw:3 * rw].astype(BF16)
        g_ref[...] = p_sc[:, 3 * rw:4 * rw]
        u_ref[...] = p_sc[:, 4 * rw:]

    row = lambda w: pl.BlockSpec((tm, w), lambda i: (i, 0))
    return pl.pallas_call(
        body, name="inproj_fwd", grid=(lp // tm,),
        in_specs=[row(d), pl.BlockSpec((1, d), lambda i: (0, 0)),
                  pl.BlockSpec((nck, d, ps), lambda i: (0, 0, 0)), row(HEAD_DIM), row(HEAD_DIM)],
        out_specs=(row(d), row(rw), row(rw), row(rw), row(rw), row(sw)),
        out_shape=(jax.ShapeDtypeStruct((lp, d), BF16),
                   jax.ShapeDtypeStruct((lp, rw), BF16),
                   jax.ShapeDtypeStruct((lp, rw), BF16),
                   jax.ShapeDtypeStruct((lp, rw), BF16),
                   jax.ShapeDtypeStruct((lp, rw), F32),
                   jax.ShapeDtypeStruct((lp, sw), F32)),
        scratch_shapes=[pltpu.VMEM((tm, proj), F32)],
        compiler_params=_params(("arbitrary",)),
    )(h, nw, w_in, cosf, sinf)


def _inproj_bwd(dh, h, nw, n, w_in, dq, dk, dv, dg, du, plan=None, plan_args=()):
    lp, d = h.shape
    nck, _, ps = w_in.shape
    rw = dq.shape[1]
    sw = du.shape[1]
    proj = nck * ps
    tm = _tile(lp, 640)
    last = lp // tm - 1

    def gather_dproj(p_sc, dq_ref, dk_ref, dv_ref, dg_ref, du_ref):
        p_sc[:, 0:rw] = dq_ref[...]
        p_sc[:, rw:2 * rw] = dk_ref[...]
        p_sc[:, 2 * rw:3 * rw] = dv_ref[...]
        p_sc[:, 3 * rw:4 * rw] = dg_ref[...]
        p_sc[:, 4 * rw:] = du_ref[...]

    def act_body(dh_ref, h_ref, nw_ref, w_ref, dq_ref, dk_ref, dv_ref, dg_ref, du_ref, dhi_ref, dnw_ref, p_sc):
        i = pl.program_id(0)

        @pl.when(i == 0)
        def _():
            dnw_ref[...] = jnp.zeros_like(dnw_ref)

        gather_dproj(p_sc, dq_ref, dk_ref, dv_ref, dg_ref, du_ref)
        dn = jnp.zeros((tm, d), F32)
        for c in range(nck):
            dn = dn + _dot_nt(p_sc[:, c * ps:(c + 1) * ps], w_ref[c])
        xh, r = _rms_stats(h_ref[...])
        dhi_ref[...] = dh_ref[...] + _rms_bwd(dn, xh, r, nw_ref[...])
        dnw_ref[...] += jnp.sum(dn * xh, axis=0, keepdims=True)

    def w_body(n_ref, dq_ref, dk_ref, dv_ref, dg_ref, du_ref, dw_ref, p_sc, acc_sc):
        i = pl.program_id(0)

        @pl.when(i == 0)
        def _():
            acc_sc[...] = jnp.zeros_like(acc_sc)

        gather_dproj(p_sc, dq_ref, dk_ref, dv_ref, dg_ref, du_ref)
        nv = n_ref[...]
        for c in range(nck):
            acc_sc[c] += _dot_tn(nv, p_sc[:, c * ps:(c + 1) * ps])

        @pl.when(i == last)
        def _():
            dw_ref[...] = acc_sc[...].astype(BF16)

    row = lambda w: pl.BlockSpec((tm, w), lambda i: (i, 0))
    vec = pl.BlockSpec((1, d), lambda i: (0, 0))
    wsp = pl.BlockSpec((nck, d, ps), lambda i: (0, 0, 0))
    dproj_specs = [row(rw), row(rw), row(rw), row(rw), row(sw)]
    dhi, dnw = pl.pallas_call(
        act_body, name="inproj_bwd_act", grid=(lp // tm,),
        in_specs=[row(d), row(d), vec, wsp] + dproj_specs,
        out_specs=(row(d), vec),
        out_shape=(jax.ShapeDtypeStruct((lp, d), F32), jax.ShapeDtypeStruct((1, d), F32)),
        scratch_shapes=[pltpu.VMEM((tm, proj), BF16)],
        compiler_params=_params(("arbitrary",)),
    )(dh, h, nw, w_in, dq, dk, dv, dg, du)
    (dw,), rode = _pcall(
        w_body, name="inproj_bwd_w", grid=(lp // tm,), plan=plan, plan_args=plan_args, args=(n, dq, dk, dv, dg, du),
        in_specs=[row(d)] + dproj_specs,
        out_specs=(wsp,), out_shape=(jax.ShapeDtypeStruct((nck, d, ps), BF16),),
        scratch_shapes=[pltpu.VMEM((tm, proj), BF16), pltpu.VMEM((nck, d, ps), F32)])
    return dhi, dnw, dw, rode


def _retention_tables(rc):
    h = jnp.arange(RET_HEADS, dtype=F32)
    log_g = jnp.log(1.0 - 2.0 ** (-5.0 - h))
    i = jnp.arange(rc)
    diff = i[:, None] - i[None, :]
    dec = jnp.where(diff[None] >= 0,
                    jnp.exp(log_g[:, None, None] * jnp.maximum(diff, 0)[None].astype(F32)), 0.0)
    pos = jnp.arange(rc, dtype=F32)
    wq = jnp.exp(log_g[:, None] * (pos + 1.0)[None])
    wk = jnp.exp(log_g[:, None] * (rc - 1 - pos)[None])
    gch = jnp.exp(log_g * rc)
    ones = jnp.ones((1, 1, HEAD_DIM), F32)
    return (dec, wq[:, :, None] * ones, wk[:, :, None] * ones,
            gch[:, None, None] * jnp.ones((1, 8, HEAD_DIM), F32))


def _head_norm(o):
    mu = jnp.mean(o, axis=-1, keepdims=True)
    oc = o - mu
    r = lax.rsqrt(jnp.mean(oc * oc, axis=-1, keepdims=True) + EPS)
    return oc * r, r


def _ret_fwd(q, k, v, g, rnw, tables):
    lp, rw = q.shape
    heads = rw // HEAD_DIM
    rc = tables[0].shape[1]
    nch = lp // rc
    dec, wq, wk, gch = tables

    def body(q_ref, k_ref, v_ref, g_ref, w_ref, dec_ref, wq_ref, wk_ref, gch_ref,
             o_ref, ret_ref, sp_ref, s_sc):
        n = pl.program_id(0)

        @pl.when(n == 0)
        def _():
            s_sc[...] = jnp.zeros_like(s_sc)

        cols = [slice(hh * HEAD_DIM, (hh + 1) * HEAD_DIM) for hh in range(heads)]
        s_ins = [s_sc[hh] for hh in range(heads)]
        outs = []
        for hh, cs in enumerate(cols):
            qv, kv, vv = q_ref[:, cs], k_ref[:, cs], v_ref[:, cs]
            s_in = s_ins[hh]
            a = _dot_nt(qv, kv) * dec_ref[hh]
            qw = (qv.astype(F32) * wq_ref[hh]).astype(BF16)
            kw = (kv.astype(F32) * wk_ref[hh]).astype(BF16)
            o = _dot(a.astype(BF16), vv) + _dot(qw, s_in.astype(BF16))
            s_new = gch_ref[hh, 0:1, :] * s_in + _dot_tn(kw, vv)
            xh, _ = _head_norm(o)
            gv = g_ref[:, cs]
            outs.append((o, s_new, (gv * _sigmoid(gv) * (xh * w_ref[:, cs])).astype(BF16)))
        for hh, cs in enumerate(cols):
            o, s_new, ret = outs[hh]
            sp_ref[hh, 0] = s_ins[hh]
            s_sc[hh] = s_new
            o_ref[:, cs] = o
            ret_ref[:, cs] = ret

    blk = pl.BlockSpec((rc, rw), lambda n: (n, 0))
    tab = pl.BlockSpec((heads, rc, HEAD_DIM), lambda n: (0, 0, 0))
    dtab = pl.BlockSpec((heads, rc, rc), lambda n: (0, 0, 0))
    return pl.pallas_call(
        body, name="retention_fwd", grid=(nch,),
        in_specs=[blk, blk, blk, blk, pl.BlockSpec((1, rw), lambda n: (0, 0)),
                  dtab, tab, tab, pl.BlockSpec((heads, 8, HEAD_DIM), lambda n: (0, 0, 0))],
        out_specs=(blk, blk, pl.BlockSpec((heads, 1, HEAD_DIM, HEAD_DIM), lambda n: (0, n, 0, 0))),
        out_shape=(jax.ShapeDtypeStruct((lp, rw), F32),
                   jax.ShapeDtypeStruct((lp, rw), BF16),
                   jax.ShapeDtypeStruct((heads, nch, HEAD_DIM, HEAD_DIM), F32)),
        scratch_shapes=[pltpu.VMEM((heads, HEAD_DIM, HEAD_DIM), F32)],
        compiler_params=_params(("arbitrary",)),
    )(q, k, v, g, rnw, dec, wq, wk, gch)


def _ret_bwd(dret, q, k, v, g, o, sprev, rnw, tables, cosf, sinf):
    lp, rw = q.shape
    heads = rw // HEAD_DIM
    rc = tables[0].shape[1]
    nch = lp // rc
    dec, wq, wk, gch = tables
    scale = HEAD_DIM ** -0.5
    half = HEAD_DIM // 2

    def body(dret_ref, q_ref, k_ref, v_ref, g_ref, o_ref, sp_ref, w_ref, dec_ref, wq_ref, wk_ref, gch_ref,
             cos_ref, sin_ref, dq_ref, dk_ref, dv_ref, dg_ref, dw_ref, ds_sc):
        n = pl.program_id(0)

        @pl.when(n == 0)
        def _():
            ds_sc[...] = jnp.zeros_like(ds_sc)
            dw_ref[...] = jnp.zeros_like(dw_ref)

        cosv = cos_ref[...]
        sinv = sin_ref[...]
        cols = [slice(hh * HEAD_DIM, (hh + 1) * HEAD_DIM) for hh in range(heads)]
        ds_ins = [ds_sc[hh] for hh in range(heads)]
        dw_ins = [dw_ref[:, cs] for cs in cols]
        outs = []
        for hh, cs in enumerate(cols):
            qv, kv, vv = q_ref[:, cs], k_ref[:, cs], v_ref[:, cs]
            gv = g_ref[:, cs]
            dr = dret_ref[:, cs]
            w = w_ref[:, cs]
            sg = _sigmoid(gv)
            sil = gv * sg
            xh, r = _head_norm(o_ref[:, cs])
            dgate = (dr * (xh * w) * (sg * (1.0 + gv * (1.0 - sg)))).astype(BF16)
            dyw = dr * sil
            dw_new = dw_ins[hh] + jnp.sum(dyw * xh, axis=0, keepdims=True)
            dxh = dyw * w
            do = r * (dxh - jnp.mean(dxh, axis=-1, keepdims=True)
                      - xh * jnp.mean(dxh * xh, axis=-1, keepdims=True))
            dob = do.astype(BF16)
            dmask = dec_ref[hh]
            wqv = wq_ref[hh]
            wkv = wk_ref[hh]
            a = (_dot_nt(qv, kv) * dmask).astype(BF16)
            da = (_dot_nt(dob, vv) * dmask).astype(BF16)
            qw = (qv.astype(F32) * wqv).astype(BF16)
            kw = (kv.astype(F32) * wkv).astype(BF16)
            s_in = sp_ref[hh, 0].astype(BF16)
            ds = ds_ins[hh]
            dsb = ds.astype(BF16)
            dq = _dot(da, kv) + _dot_nt(dob, s_in) * wqv
            dk = _dot_tn(da, qv) + _dot_nt(vv, dsb) * wkv
            dv = _dot_tn(a, dob) + _dot(kw, dsb)
            ds_new = gch_ref[hh, 0:1, :] * ds + _dot_tn(qw, dob)
            outs.append((dgate, dw_new, ds_new,
                         (dq * cosv + pltpu.roll(dq * sinv, half, 1)).astype(BF16),
                         ((dk * cosv + pltpu.roll(dk * sinv, half, 1)) * scale).astype(BF16),
                         dv.astype(BF16)))
        for hh, cs in enumerate(cols):
            dgate, dw_new, ds_new, dqv, dkv, dvv = outs[hh]
            dg_ref[:, cs] = dgate
            dw_ref[:, cs] = dw_new
            ds_sc[hh] = ds_new
            dq_ref[:, cs] = dqv
            dk_ref[:, cs] = dkv
            dv_ref[:, cs] = dvv

    blk = pl.BlockSpec((rc, rw), lambda n: (nch - 1 - n, 0))
    tab = pl.BlockSpec((heads, rc, HEAD_DIM), lambda n: (0, 0, 0))
    dtab = pl.BlockSpec((heads, rc, rc), lambda n: (0, 0, 0))
    wsp = pl.BlockSpec((1, rw), lambda n: (0, 0))
    pos = pl.BlockSpec((rc, HEAD_DIM), lambda n: (nch - 1 - n, 0))
    bshape = jax.ShapeDtypeStruct((lp, rw), BF16)
    return pl.pallas_call(
        body, name="retention_bwd", grid=(nch,),
        in_specs=[blk, blk, blk, blk, blk, blk,
                  pl.BlockSpec((heads, 1, HEAD_DIM, HEAD_DIM), lambda n: (0, nch - 1 - n, 0, 0)),
                  wsp, dtab, tab, tab, pl.BlockSpec((heads, 8, HEAD_DIM), lambda n: (0, 0, 0)), pos, pos],
        out_specs=(blk, blk, blk, blk, wsp),
        out_shape=(bshape, bshape, bshape, bshape, jax.ShapeDtypeStruct((1, rw), F32)),
        scratch_shapes=[pltpu.VMEM((heads, HEAD_DIM, HEAD_DIM), F32)],
        compiler_params=_params(("arbitrary",)),
    )(dret, q, k, v, g, o, sprev, rnw, dec, wq, wk, gch, cosf, sinf)


SCAN_CW = 512


def _s5_prepare(lam_re, lam_im, log_dt, b_re, b_im):
    dt = jnp.exp(log_dt)[:, None]
    er = jnp.exp(lam_re * dt)
    ar = er * jnp.cos(lam_im * dt)
    ai = er * jnp.sin(lam_im * dt)
    den = lam_re * lam_re + lam_im * lam_im
    fr = ((ar - 1.0) * lam_re + ai * lam_im) / den
    fi = (ai * lam_re - (ar - 1.0) * lam_im) / den
    bbr = fr[..., None] * b_re - fi[..., None] * b_im
    bbi = fr[..., None] * b_im + fi[..., None] * b_re
    return ar, ai, bbr, bbi


def _blockdiag_in(t):
    g, p, n = t.shape
    gs = g // N_SEC
    t = t.reshape(N_SEC, gs, p, n)
    eye = jnp.eye(gs, dtype=t.dtype)
    return jnp.einsum("sgpn,gh->sgphn", t, eye).reshape(N_SEC, gs * p, gs * n)


def _blockdiag_out(m, g, p, n):
    gs = g // N_SEC
    m = m.reshape(N_SEC, gs, p, gs, n)
    eye = jnp.eye(gs, dtype=m.dtype)
    return jnp.einsum("sgphn,gh->sgpn", m, eye).reshape(g, p, n)


def _scan_step(xr_ref, xi_ref, r0, prev, ar_ref, ai_ref, conj, ncols):
    new = []
    for cc in range(ncols // SCAN_CW):
        cs = pl.ds(cc * SCAN_CW, SCAN_CW)
        pr, pi = prev[cc]
        ar = ar_ref[:, cs]
        ai = ai_ref[:, cs]
        if conj:
            nr = ar * pr + ai * pi
            ni = ar * pi - ai * pr
        else:
            nr = ar * pr - ai * pi
            ni = ar * pi + ai * pr
        xr = xr_ref[pl.ds(r0, 8), cs] + nr
        xi = xi_ref[pl.ds(r0, 8), cs] + ni
        xr_ref[pl.ds(r0, 8), cs] = xr
        xi_ref[pl.ds(r0, 8), cs] = xi
        new.append((xr, xi))
    return new


def _scan_chunks(ncols):
    return [pl.ds(cc * SCAN_CW, SCAN_CW) for cc in range(ncols // SCAN_CW)]


def _flat(pairs):
    return tuple(t for p in pairs for t in p)


def _pairs(flat):
    return [(flat[2 * k], flat[2 * k + 1]) for k in range(len(flat) // 2)]


def _shift_rows(z, down):
    row = lax.broadcasted_iota(jnp.int32, z.shape, 0)
    if down:
        return jnp.where(row == 0, 0.0, pltpu.roll(z, 1, 0))
    return jnp.where(row == N_SEG - 1, 0.0, pltpu.roll(z, N_SEG - 1, 0))


def _s5_fwd(u, bsr, bsi, csr, csi, a8r, a8i, al8r, al8i, d, gluw, glub, nw, jb):
    lp, sw = u.shape
    ns = a8r.shape[1]
    rows = N_SEG * jb
    nblk = lp // rows
    secw = sw // N_SEC
    secn = ns // N_SEC

    def local_scan(u_ref, bsr_ref, bsi_ref, ar_ref, ai_ref, xr_ref, xi_ref, pr_sc, pi_sc):
        for s in range(N_SEC):
            ub = u_ref[:, s * secw:(s + 1) * secw].astype(BF16)
            xr_ref[:, s * secn:(s + 1) * secn] = _dot(ub, bsr_ref[s])
            xi_ref[:, s * secn:(s + 1) * secn] = _dot(ub, bsi_ref[s])
        prev = [(pr_sc[:, cs], pi_sc[:, cs]) for cs in _scan_chunks(ns)]
        prev = _scan_step(xr_ref, xi_ref, 0, prev, ar_ref, ai_ref, False, ns)

        def step(j, carry):
            r0 = pl.multiple_of(j * 8, 8)
            return _flat(_scan_step(xr_ref, xi_ref, r0, _pairs(carry), ar_ref, ai_ref, False, ns))

        last = _pairs(lax.fori_loop(1, jb, step, _flat(prev)))
        for cs, (vr, vi) in zip(_scan_chunks(ns), last):
            pr_sc[:, cs] = vr
            pi_sc[:, cs] = vi

    def carry_body(u_ref, bsr_ref, bsi_ref, ar_ref, ai_ref, alr_ref, ali_ref, cr_ref, ci_ref,
                   xr_sc, xi_sc, pr_sc, pi_sc):
        b = pl.program_id(0)

        @pl.when(b == 0)
        def _():
            pr_sc[...] = jnp.zeros_like(pr_sc)
            pi_sc[...] = jnp.zeros_like(pi_sc)

        local_scan(u_ref, bsr_ref, bsi_ref, ar_ref, ai_ref, xr_sc, xi_sc, pr_sc, pi_sc)

        @pl.when(b == nblk - 1)
        def _():
            er = _shift_rows(pr_sc[...], True)
            ei = _shift_rows(pi_sc[...], True)
            alr, ali = alr_ref[...], ali_ref[...]
            cr, ci = er, ei
            for _ in range(N_SEG - 2):
                sr = _shift_rows(cr, True)
                si = _shift_rows(ci, True)
                cr = er + alr * sr - ali * si
                ci = ei + alr * si + ali * sr
            cr_ref[...] = cr
            ci_ref[...] = ci

    ublk = pl.BlockSpec((rows, sw), lambda b: (b, 0))
    bspec = pl.BlockSpec((N_SEC, secw, secn), lambda b: (0, 0, 0))
    cspec = pl.BlockSpec((N_SEC, secn, secw), lambda b: (0, 0, 0))
    s8 = pl.BlockSpec((N_SEG, ns), lambda b: (0, 0))
    vec = pl.BlockSpec((1, sw), lambda b: (0, 0))
    s8shape = jax.ShapeDtypeStruct((N_SEG, ns), F32)
    c0r, c0i = pl.pallas_call(
        carry_body, name="s5_fwd_carry", grid=(nblk,),
        in_specs=[ublk, bspec, bspec, s8, s8, s8, s8],
        out_specs=(s8, s8), out_shape=(s8shape, s8shape),
        scratch_shapes=[pltpu.VMEM((rows, ns), F32), pltpu.VMEM((rows, ns), F32),
                        pltpu.VMEM((N_SEG, ns), F32), pltpu.VMEM((N_SEG, ns), F32)],
        compiler_params=_params(("arbitrary",)),
    )(u, bsr, bsi, a8r, a8i, al8r, al8i)

    def main_body(u_ref, bsr_ref, bsi_ref, csr_ref, csi_ref, ar_ref, ai_ref, c0r_ref, c0i_ref,
                  d_ref, gw_ref, gb_ref, nw_ref, xr_ref, xi_ref, yp_ref, out_ref, pr_sc, pi_sc):
        b = pl.program_id(0)

        @pl.when(b == 0)
        def _():
            pr_sc[...] = c0r_ref[...]
            pi_sc[...] = c0i_ref[...]

        local_scan(u_ref, bsr_ref, bsi_ref, ar_ref, ai_ref, xr_ref, xi_ref, pr_sc, pi_sc)
        for s in range(N_SEC):
            xs = pl.ds(s * secn, secn)
            us = pl.ds(s * secw, secw)
            y = _dot(xr_ref[:, xs].astype(BF16), csr_ref[s]) + _dot(xi_ref[:, xs].astype(BF16), csi_ref[s])
            yp_ref[:, us] = y + d_ref[:, us] * u_ref[:, us]
        yp = yp_ref[...]
        t = jnp.tanh(GELU_K0 * (yp + GELU_K1 * yp * yp * yp))
        y1 = 0.5 * yp * (1.0 + t)
        z = _dot(y1.astype(BF16), gw_ref[...]) + gb_ref[...]
        y2 = y1 * _sigmoid(z)
        xh, _ = _rms_stats(y2)
        out_ref[...] = (xh * nw_ref[...]).astype(BF16)

    xblk = pl.BlockSpec((rows, ns), lambda b: (b, 0))
    xr, xi, yp, out = pl.pallas_call(
        main_body, name="s5_fwd", grid=(nblk,),
        in_specs=[ublk, bspec, bspec, cspec, cspec, s8, s8, s8, s8, vec,
                  pl.BlockSpec((sw, sw), lambda b: (0, 0)), vec, vec],
        out_specs=(xblk, xblk, ublk, ublk),
        out_shape=(jax.ShapeDtypeStruct((lp, ns), F32), jax.ShapeDtypeStruct((lp, ns), F32),
                   jax.ShapeDtypeStruct((lp, sw), F32), jax.ShapeDtypeStruct((lp, sw), BF16)),
        scratch_shapes=[pltpu.VMEM((N_SEG, ns), F32), pltpu.VMEM((N_SEG, ns), F32)],
        compiler_params=_params(("arbitrary",)),
    )(u, bsr, bsi, csr, csi, a8r, a8i, c0r, c0i, d, gluw, glub, nw)
    return xr, xi, c0r, c0i, yp, out


def _s5_bwd(dout, u, yp, xr, xi, c0r, c0i, bsrt, bsit, csrt, csit, a8r, a8i, al8r, al8i, d, gluw, glub, nw, jb):
    lp, sw = u.shape
    ns = a8r.shape[1]
    rows = N_SEG * jb
    nblk = lp // rows
    secw = sw // N_SEC
    secn = ns // N_SEC

    def rowwise_bwd(dout_ref, yp_ref, gw_ref, gb_ref, nw_ref):
        ypv = yp_ref[...]
        t = jnp.tanh(GELU_K0 * (ypv + GELU_K1 * ypv * ypv * ypv))
        y1 = 0.5 * ypv * (1.0 + t)
        dgelu = 0.5 * (1.0 + t) + 0.5 * ypv * (1.0 - t * t) * GELU_K0 * (1.0 + 3.0 * GELU_K1 * ypv * ypv)
        gw = gw_ref[...]
        y1b = y1.astype(BF16)
        sg = _sigmoid(_dot(y1b, gw) + gb_ref[...])
        xh, r = _rms_stats(y1 * sg)
        dov = dout_ref[...]
        dy2 = _rms_bwd(dov, xh, r, nw_ref[...])
        dz = dy2 * y1 * sg * (1.0 - sg)
        dzb = dz.astype(BF16)
        dy1 = dy2 * sg + _dot_nt(dzb, gw)
        return dy1 * dgelu, dov * xh, y1b, dzb, dz

    def lam_scan(dyp_of, csrt_ref, csit_ref, ar_ref, ai_ref, lr_sc, li_sc, nr_sc, ni_sc, extra):
        for s in range(N_SEC):
            db = dyp_of(s)
            lr_sc[:, s * secn:(s + 1) * secn] = _dot(db, csrt_ref[s])
            li_sc[:, s * secn:(s + 1) * secn] = _dot(db, csit_ref[s])
        top = rows - 8
        prev = [(nr_sc[:, cs], ni_sc[:, cs]) for cs in _scan_chunks(ns)]
        prev = _scan_step(lr_sc, li_sc, top, prev, ar_ref, ai_ref, True, ns)
        extra(top, pl.ds(top - 8, 8))

        def step(jj, carry):
            r0 = pl.multiple_of((jb - 1 - jj) * 8, 8)
            rp = pl.multiple_of((jb - 2 - jj) * 8, 8)
            new = _scan_step(lr_sc, li_sc, r0, _pairs(carry), ar_ref, ai_ref, True, ns)
            extra(r0, pl.ds(rp, 8))
            return _flat(new)

        prev = _pairs(lax.fori_loop(1, jb - 1, step, _flat(prev)))
        last = _scan_step(lr_sc, li_sc, 0, prev, ar_ref, ai_ref, True, ns)
        extra(0, None)
        for cs, (vr, vi) in zip(_scan_chunks(ns), last):
            nr_sc[:, cs] = vr
            ni_sc[:, cs] = vi

    def carry_body(dout_ref, yp_ref, u_ref, gw_ref, gb_ref, nw_ref, csrt_ref, csit_ref, ar_ref, ai_ref,
                   alr_ref, ali_ref, cr_ref, ci_ref, dyp_ref, dnw_ref, dgw_ref, dgb_ref, dd_ref,
                   lr_sc, li_sc, nr_sc, ni_sc):
        b = pl.program_id(0)

        @pl.when(b == 0)
        def _():
            nr_sc[...] = jnp.zeros_like(nr_sc)
            ni_sc[...] = jnp.zeros_like(ni_sc)
            for ref in (dnw_ref, dgw_ref, dgb_ref, dd_ref):
                ref[...] = jnp.zeros_like(ref)

        dyp, dnw_rows, y1b, dzb, dz = rowwise_bwd(dout_ref, yp_ref, gw_ref, gb_ref, nw_ref)
        dnw_ref[...] += jnp.sum(dnw_rows, axis=0, keepdims=True)
        dgw_ref[...] += _dot_tn(y1b, dzb)
        dgb_ref[...] += jnp.sum(dz, axis=0, keepdims=True)
        dd_ref[...] += jnp.sum(dyp * u_ref[...], axis=0, keepdims=True)
        dyp_ref[...] = dyp.astype(BF16)
        lam_scan(lambda s: dyp_ref[:, s * secw:(s + 1) * secw], csrt_ref, csit_ref, ar_ref, ai_ref,
                 lr_sc, li_sc, nr_sc, ni_sc, lambda r0, prev_rows: None)

        @pl.when(b == nblk - 1)
        def _():
            fr = _shift_rows(nr_sc[...], False)
            fi = _shift_rows(ni_sc[...], False)
            alr, ali = alr_ref[...], ali_ref[...]
            cr, ci = fr, fi
            for _ in range(N_SEG - 2):
                sr = _shift_rows(cr, False)
                si = _shift_rows(ci, False)
                cr = fr + alr * sr + ali * si
                ci = fi + alr * si - ali * sr
            cr_ref[...] = cr
            ci_ref[...] = ci

    rev = lambda b: (nblk - 1 - b, 0)
    ublk = pl.BlockSpec((rows, sw), rev)
    xblk = pl.BlockSpec((rows, ns), rev)
    s8 = pl.BlockSpec((N_SEG, ns), lambda b: (0, 0))
    vec = pl.BlockSpec((1, sw), lambda b: (0, 0))
    gws = pl.BlockSpec((sw, sw), lambda b: (0, 0))
    btspec = pl.BlockSpec((N_SEC, secn, secw), lambda b: (0, 0, 0))
    ctspec = pl.BlockSpec((N_SEC, secw, secn), lambda b: (0, 0, 0))
    s8shape = jax.ShapeDtypeStruct((N_SEG, ns), F32)
    lcr, lci, dyp_all, d_nw, d_gw, d_gb, d_d = pl.pallas_call(
        carry_body, name="s5_bwd_carry", grid=(nblk,),
        in_specs=[ublk, ublk, ublk, gws, vec, vec, ctspec, ctspec, s8, s8, s8, s8],
        out_specs=(s8, s8, ublk, vec, gws, vec, vec),
        out_shape=(s8shape, s8shape, jax.ShapeDtypeStruct((lp, sw), BF16), jax.ShapeDtypeStruct((1, sw), F32),
                   jax.ShapeDtypeStruct((sw, sw), F32), jax.ShapeDtypeStruct((1, sw), F32),
                   jax.ShapeDtypeStruct((1, sw), F32)),
        scratch_shapes=[pltpu.VMEM((rows, ns), F32), pltpu.VMEM((rows, ns), F32),
                        pltpu.VMEM((N_SEG, ns), F32), pltpu.VMEM((N_SEG, ns), F32)],
        compiler_params=_params(("arbitrary",)),
    )(dout, yp, u, gluw, glub, nw, csrt, csit, a8r, a8i, al8r, al8i)

    def main_body(dyp_sc, u_ref, xr_ref, xi_ref, xtr_ref, xti_ref, c0r_ref, c0i_ref, lcr_ref, lci_ref,
                  d_ref, bsrt_ref, bsit_ref, csrt_ref, csit_ref, ar_ref, ai_ref,
                  du_ref, dcr_ref, dci_ref, dbr_ref, dbi_ref, dar_ref, dai_ref,
                  lr_sc, li_sc, nr_sc, ni_sc):
        b = pl.program_id(0)

        @pl.when(b == 0)
        def _():
            nr_sc[...] = lcr_ref[...]
            ni_sc[...] = lci_ref[...]
            for ref in (dcr_ref, dci_ref, dbr_ref, dbi_ref, dar_ref, dai_ref):
                ref[...] = jnp.zeros_like(ref)

        for s in range(N_SEC):
            db = dyp_sc[:, s * secw:(s + 1) * secw]
            xs = pl.ds(s * secn, secn)
            dcr_ref[s] += _dot_tn(xr_ref[:, xs].astype(BF16), db)
            dci_ref[s] += _dot_tn(xi_ref[:, xs].astype(BF16), db)

        first = b == nblk - 1

        def acc_da(r0, prev_rows):
            for cc in range(ns // SCAN_CW):
                cs = pl.ds(cc * SCAN_CW, SCAN_CW)
                lr = lr_sc[pl.ds(r0, 8), cs]
                li = li_sc[pl.ds(r0, 8), cs]
                if prev_rows is None:
                    xpr = jnp.where(first, c0r_ref[:, cs], xtr_ref[:, cs])
                    xpi = jnp.where(first, c0i_ref[:, cs], xti_ref[:, cs])
                else:
                    xpr = xr_ref[prev_rows, cs]
                    xpi = xi_ref[prev_rows, cs]
                dar_ref[:, cs] += lr * xpr + li * xpi
                dai_ref[:, cs] += li * xpr - lr * xpi

        lam_scan(lambda s: dyp_sc[:, s * secw:(s + 1) * secw], csrt_ref, csit_ref, ar_ref, ai_ref,
                 lr_sc, li_sc, nr_sc, ni_sc, acc_da)

        for s in range(N_SEC):
            xs = pl.ds(s * secn, secn)
            us = pl.ds(s * secw, secw)
            lrb = lr_sc[:, xs].astype(BF16)
            lib = li_sc[:, xs].astype(BF16)
            du = _dot(lrb, bsrt_ref[s]) + _dot(lib, bsit_ref[s]) + d_ref[:, us] * dyp_sc[:, us].astype(F32)
            du_ref[:, us] = du.astype(BF16)
            ub = u_ref[:, us].astype(BF16)
            dbr_ref[s] += _dot_tn(ub, lrb)
            dbi_ref[s] += _dot_tn(ub, lib)

    tail = pl.BlockSpec((N_SEG, ns), lambda b: (jnp.maximum((nblk - 1 - b) * jb - 1, 0), 0))
    acc_c = pl.BlockSpec((N_SEC, secn, secw), lambda b: (0, 0, 0))
    acc_b = pl.BlockSpec((N_SEC, secw, secn), lambda b: (0, 0, 0))
    du, dcr, dci, dbr, dbi, dar, dai = pl.pallas_call(
        main_body, name="s5_bwd", grid=(nblk,),
        in_specs=[ublk, ublk, xblk, xblk, tail, tail, s8, s8, s8, s8,
                  vec, btspec, btspec, ctspec, ctspec, s8, s8],
        out_specs=(ublk, acc_c, acc_c, acc_b, acc_b, s8, s8),
        out_shape=(jax.ShapeDtypeStruct((lp, sw), BF16),
                   jax.ShapeDtypeStruct((N_SEC, secn, secw), F32),
                   jax.ShapeDtypeStruct((N_SEC, secn, secw), F32),
                   jax.ShapeDtypeStruct((N_SEC, secw, secn), F32),
                   jax.ShapeDtypeStruct((N_SEC, secw, secn), F32),
                   s8shape, s8shape),
        scratch_shapes=[pltpu.VMEM((rows, ns), F32), pltpu.VMEM((rows, ns), F32),
                        pltpu.VMEM((N_SEG, ns), F32), pltpu.VMEM((N_SEG, ns), F32)],
        compiler_params=_params(("arbitrary",)),
    )(dyp_all, u, xr, xi, xr, xi, c0r, c0i, lcr, lci, d, bsrt, bsit, csrt, csit, a8r, a8i)
    return du, d_nw, d_gw, d_gb, d_d, dcr, dci, dbr, dbi, dar, dai


def _outproj_fwd(h, ret, ssm, wo):
    lp, d = h.shape
    nck, rs, _ = wo.shape
    rw = ret.shape[1]
    tm = _tile(lp, 640)
    per = rw // rs

    def body(h_ref, ret_ref, ssm_ref, w_ref, o_ref):
        acc = h_ref[...]
        for c in range(nck):
            src = ret_ref if c < per else ssm_ref
            lo = (c % per) * rs
            acc = acc + _dot(src[:, lo:lo + rs], w_ref[c])
        o_ref[...] = acc

    row = lambda w: pl.BlockSpec((tm, w), lambda i: (i, 0))
    return pl.pallas_call(
        body, name="outproj_fwd", grid=(lp // tm,),
        in_specs=[row(d), row(rw), row(ssm.shape[1]), pl.BlockSpec((nck, rs, d), lambda i: (0, 0, 0))],
        out_specs=row(d), out_shape=jax.ShapeDtypeStruct((lp, d), F32),
        compiler_params=_params(("arbitrary",)),
    )(h, ret, ssm, wo)


def _outproj_bwd(dh, ret, ssm, wo):
    lp, d = dh.shape
    nck, rs, _ = wo.shape
    rw = ret.shape[1]
    sw = ssm.shape[1]
    tm = _tile(lp, 640)
    per = rw // rs
    last = lp // tm - 1

    def body(dh_ref, ret_ref, ssm_ref, w_ref, dret_ref, dssm_ref, dw_ref, acc_sc):
        i = pl.program_id(0)

        @pl.when(i == 0)
        def _():
            acc_sc[...] = jnp.zeros_like(acc_sc)

        dhb = dh_ref[...].astype(BF16)
        for c in range(nck):
            src, dst = (ret_ref, dret_ref) if c < per else (ssm_ref, dssm_ref)
            lo = (c % per) * rs
            dst[:, lo:lo + rs] = _dot_nt(dhb, w_ref[c])
            acc_sc[c] += _dot_tn(src[:, lo:lo + rs], dhb)

        @pl.when(i == last)
        def _():
            dw_ref[...] = acc_sc[...].astype(BF16)

    row = lambda w: pl.BlockSpec((tm, w), lambda i: (i, 0))
    wsp = pl.BlockSpec((nck, rs, d), lambda i: (0, 0, 0))
    return pl.pallas_call(
        body, name="outproj_bwd", grid=(lp // tm,),
        in_specs=[row(d), row(rw), row(sw), wsp],
        out_specs=(row(rw), row(sw), wsp),
        out_shape=(jax.ShapeDtypeStruct((lp, rw), F32), jax.ShapeDtypeStruct((lp, sw), F32),
                   jax.ShapeDtypeStruct((nck, rs, d), BF16)),
        scratch_shapes=[pltpu.VMEM((nck, rs, d), F32)],
        compiler_params=_params(("arbitrary",)),
    )(dh, ret, ssm, wo)


def _pack(arrs):
    flat = jnp.concatenate([a.reshape(-1).astype(F32) for a in arrs])
    n = flat.shape[0]
    rows = -(-n // (8 * LANE)) * 8
    return jnp.pad(flat, (0, rows * LANE - n)).reshape(rows, LANE)


def _unpack(packed, shapes):
    flat = packed.reshape(-1)
    out, off = [], 0
    for s in shapes:
        n = math.prod(s)
        out.append(flat[off:off + n].reshape(s))
        off += n
    return out


def _to_segments(a, seg_len):
    return a.reshape(N_SEG, seg_len, a.shape[1]).transpose(1, 0, 2).reshape(a.shape)


def _from_segments(a, seg_len):
    return a.reshape(seg_len, N_SEG, a.shape[1]).transpose(1, 0, 2).reshape(a.shape)


WEIGHT_NAMES = ['meta_tokens', 'ffn1_norm_w', 'ffn1_w_gate', 'ffn1_w_up', 'ffn1_w_down', 'mix_norm_w', 'w_in',
                'ret_norm_w', 'ssm_lambda_re', 'ssm_lambda_im', 'ssm_log_dt', 'ssm_b_re', 'ssm_b_im', 'ssm_c_re',
                'ssm_c_im', 'ssm_d', 'ssm_glu_w', 'ssm_glu_b', 'ssm_norm_w', 'w_out', 'ffn2_norm_w', 'ffn2_w_gate',
                'ffn2_w_up', 'ffn2_w_down', 'final_norm_w']
BIG = ['ffn1_w_gate', 'ffn1_w_up', 'ffn1_w_down', 'w_in', 'ssm_glu_w', 'w_out', 'ffn2_w_gate', 'ffn2_w_up',
       'ffn2_w_down']
TRANSPOSED = ['ffn1_w_gate', 'ffn1_w_up', 'ffn2_w_gate', 'ffn2_w_up']
BIG_EARLY = ['ffn1_w_gate', 'ffn1_w_up', 'ffn1_w_down']
BIG_LATE = [n for n in BIG if n not in BIG_EARLY]
SMALL = [n for n in WEIGHT_NAMES if n not in BIG]


def kernel(x, meta_tokens, ffn1_norm_w, ffn1_w_gate, ffn1_w_up, ffn1_w_down, mix_norm_w, w_in, ret_norm_w, ssm_lambda_re, ssm_lambda_im, ssm_log_dt, ssm_b_re, ssm_b_im, ssm_c_re, ssm_c_im, ssm_d, ssm_glu_w, ssm_glu_b, ssm_norm_w, w_out, ffn2_norm_w, ffn2_w_gate, ffn2_w_up, ffn2_w_down, final_norm_w, loss_target, m_meta_tokens, m_ffn1_norm_w, m_ffn1_w_gate, m_ffn1_w_up, m_ffn1_w_down, m_mix_norm_w, m_w_in, m_ret_norm_w, m_ssm_lambda_re, m_ssm_lambda_im, m_ssm_log_dt, m_ssm_b_re, m_ssm_b_im, m_ssm_c_re, m_ssm_c_im, m_ssm_d, m_ssm_glu_w, m_ssm_glu_b, m_ssm_norm_w, m_w_out, m_ffn2_norm_w, m_ffn2_w_gate, m_ffn2_w_up, m_ffn2_w_down, m_final_norm_w, v_meta_tokens, v_ffn1_norm_w, v_ffn1_w_gate, v_ffn1_w_up, v_ffn1_w_down, v_mix_norm_w, v_w_in, v_ret_norm_w, v_ssm_lambda_re, v_ssm_lambda_im, v_ssm_log_dt, v_ssm_b_re, v_ssm_b_im, v_ssm_c_re, v_ssm_c_im, v_ssm_d, v_ssm_glu_w, v_ssm_glu_b, v_ssm_norm_w, v_w_out, v_ffn2_norm_w, v_ffn2_w_gate, v_ffn2_w_up, v_ffn2_w_down, v_final_norm_w):
    args = locals()
    w = {n: args[n] for n in WEIGHT_NAMES}
    m = {n: args["m_" + n] for n in WEIGHT_NAMES}
    v = {n: args["v_" + n] for n in WEIGHT_NAMES}

    seq, d = x.shape[1], x.shape[2]
    lp = seq + CHUNK
    seg_len = lp // N_SEG
    rw = RET_HEADS * HEAD_DIM
    sw = ssm_d.shape[-1]
    groups = sw // SSM_GROUP
    ns = groups * SSM_STATE
    jb = _tile(seg_len, S5_STEPS, 8)
    chip = 2 * lax.axis_index("x") + lax.axis_index("y")

    as_fd = lambda t: jnp.swapaxes(t, -1, -2)
    shards = {n: (as_fd(w[n][0]) if n in TRANSPOSED else w[n][0]).astype(BF16) for n in BIG}
    early = [shards[n] for n in BIG_EARLY] + [meta_tokens]
    gathered = _gather_two_level("gather_early", early)
    gw = dict(zip(BIG_EARLY, gathered[:-1]))
    meta_full = jnp.transpose(gathered[-1], (1, 0, 2)).reshape(N_META, d)
    late = [shards[n] for n in BIG_LATE]

    freqs = 1.0 / (ROPE_BASE ** (jnp.arange(0, HEAD_DIM, 2, dtype=F32) / HEAD_DIM))
    ang_c = (jnp.arange(lp // CHUNK, dtype=F32) * CHUNK - float(CHUNK - N_META))[:, None] * freqs[None, :]
    ang_r = jnp.arange(CHUNK, dtype=F32)[:, None] * freqs[None, :]
    cos_c, sin_c = jnp.cos(ang_c)[:, None, :], jnp.sin(ang_c)[:, None, :]
    cos_r, sin_r = jnp.cos(ang_r)[None], jnp.sin(ang_r)[None]
    cos_t = (cos_c * cos_r - sin_c * sin_r).reshape(lp, HEAD_DIM // 2)
    sin_t = (sin_c * cos_r + cos_c * sin_r).reshape(lp, HEAD_DIM // 2)
    cosf = jnp.concatenate([cos_t, cos_t], axis=1)
    sinf = jnp.concatenate([-sin_t, sin_t], axis=1)
    tables = _retention_tables(_tile(lp, RET_ROWS, CHUNK))

    lam_re, lam_im, log_dt = ssm_lambda_re[0], ssm_lambda_im[0], ssm_log_dt[0]
    b_re, b_im, c_re, c_im = ssm_b_re[0], ssm_b_im[0], ssm_c_re[0], ssm_c_im[0]
    (ar, ai, bbr, bbi), prep_vjp = jax.vjp(_s5_prepare, lam_re, lam_im, log_dt, b_re, b_im)
    dt = jnp.exp(log_dt)[:, None]
    el = jnp.exp(seg_len * lam_re * dt)
    alr = el * jnp.cos(seg_len * lam_im * dt)
    ali = el * jnp.sin(seg_len * lam_im * dt)
    bc8 = lambda t: jnp.broadcast_to(t.reshape(1, ns), (N_SEG, ns))
    a8r, a8i, al8r, al8i = bc8(ar), bc8(ai), bc8(alr), bc8(ali)
    bsr = _blockdiag_in(jnp.transpose(bbr, (0, 2, 1)))
    bsi = _blockdiag_in(jnp.transpose(bbi, (0, 2, 1)))
    csrt = _blockdiag_in(c_re)
    csit = _blockdiag_in(-c_im)
    tr = lambda t: jnp.transpose(t, (0, 2, 1))
    bsr_b, bsi_b = bsr.astype(BF16), bsi.astype(BF16)
    csr_b, csi_b = tr(csrt).astype(BF16), tr(csit).astype(BF16)
    bsrt_b, bsit_b = tr(bsr).astype(BF16), tr(bsi).astype(BF16)
    csrt_b, csit_b = csrt.astype(BF16), csit.astype(BF16)

    h0 = (jnp.concatenate([jnp.zeros((CHUNK - N_META, d), F32), meta_full], axis=0), x[0])
    (h1, g1, u1), late_half = _ffn_fwd("ffn1_fwd", h0, ffn1_norm_w, gw['ffn1_w_gate'], gw['ffn1_w_up'],
                                       gw['ffn1_w_down'], _allgather_chips_plan(late), late)
    gw.update(zip(BIG_LATE, _forward_sibling("gather_late_forward", late_half)))
    glu_full = gw['ssm_glu_w'].reshape(sw, sw)
    n2, q, k, vv, gate, u = _inproj_fwd(h1, mix_norm_w, gw['w_in'], cosf, sinf, rw)
    o, ret, sprev = _ret_fwd(q, k, vv, gate, ret_norm_w, tables)
    u_seg = _to_segments(u, seg_len)
    xr, xi, c0r, c0i, yp, ssm_seg = _s5_fwd(u_seg, bsr_b, bsi_b, csr_b, csi_b, a8r, a8i, al8r, al8i,
                                            ssm_d, glu_full, ssm_glu_b, ssm_norm_w, jb)
    ssm = _from_segments(ssm_seg, seg_len)
    h2 = _outproj_fwd(h1, ret, ssm, gw['w_out'])
    (dh3, g2, u2, loss_part, d_final), _ = _ffn_fwd(
        "ffn2_fwd_loss", h2, ffn2_norm_w, gw['ffn2_w_gate'], gw['ffn2_w_up'], gw['ffn2_w_down'],
        loss=(final_norm_w.reshape(1, d), loss_target[0]))

    (dh2, d_ffn2_norm, nb, daccb, ab, dgb, dub), _ = _ffn_bwd_act(
        "ffn2_bwd_act", dh3, h2, ffn2_norm_w, g2, u2, gw['ffn2_w_gate'], gw['ffn2_w_up'], gw['ffn2_w_down'])
    (dwg2, dwu2, dwd2), _ = _ffn_bwd_w("ffn2_bwd_w", nb, daccb, ab, dgb, dub)
    dret, dssm, dwo = _outproj_bwd(dh2, ret, ssm, gw['w_out'])
    (du_seg, d_ssm_norm, d_glu_w, d_glu_b, d_ssm_d, dcr_s, dci_s, dbr_s, dbi_s, dar8, dai8) = _s5_bwd(
        _to_segments(dssm, seg_len), u_seg, yp, xr, xi, c0r, c0i, bsrt_b, bsit_b, csrt_b, csit_b,
        a8r, a8i, al8r, al8i, ssm_d, glu_full, ssm_glu_b, ssm_norm_w, jb)
    du = _from_segments(du_seg, seg_len)
    dq, dk, dv, dgate, d_ret_norm = _ret_bwd(dret, q, k, vv, gate, o, sprev, ret_norm_w, tables, cosf, sinf)
    d_c_re = _blockdiag_out(tr(dcr_s), groups, SSM_GROUP, SSM_STATE)
    d_c_im = -_blockdiag_out(tr(dci_s), groups, SSM_GROUP, SSM_STATE)
    d_bbr = jnp.transpose(_blockdiag_out(dbr_s, groups, SSM_GROUP, SSM_STATE), (0, 2, 1))
    d_bbi = jnp.transpose(_blockdiag_out(dbi_s, groups, SSM_GROUP, SSM_STATE), (0, 2, 1))
    d_ar = jnp.sum(dar8, axis=0).reshape(groups, SSM_STATE)
    d_ai = jnp.sum(dai8, axis=0).reshape(groups, SSM_STATE)
    small_parts = [loss_part[0:1, :], d_ret_norm, d_ar, d_ai, d_bbr, d_bbi,
                   d_c_re, d_c_im, d_ssm_d, d_glu_b, d_ssm_norm, d_ffn2_norm, d_final]
    small_shapes = [a.shape for a in small_parts]
    packed = _pack(small_parts)
    dh1, d_mix_norm, dwin, (all_parts,) = _inproj_bwd(dh2, h1, mix_norm_w, n2, gw['w_in'], dq, dk, dv, dgate, du,
                                                      _allgather_all_plan([packed]), [packed])
    late_parts = {
        'w_in': dwin, 'ssm_glu_w': d_glu_w.reshape(N_CHIP, sw // N_CHIP, sw).astype(BF16), 'w_out': dwo,
        'ffn2_w_gate': dwg2, 'ffn2_w_up': dwu2, 'ffn2_w_down': dwd2,
    }
    late_list = [late_parts[n] for n in BIG_LATE]
    (dh0, d_ffn1_norm, nb, daccb, ab, dgb, dub), late_recv = _ffn_bwd_act(
        "ffn1_bwd_act", dh1, h0, ffn1_norm_w, g1, u1, gw['ffn1_w_gate'], gw['ffn1_w_up'], gw['ffn1_w_down'],
        _alltoall_chips_plan(late_list), late_list)
    grad_x = dh0[CHUNK:][None]
    d_meta = dh0[CHUNK - N_META:CHUNK]

    tail_parts = [d_meta, d_ffn1_norm, d_mix_norm]
    packed_tail = _pack(tail_parts)
    early_recv, (tail_all,) = _ffn_bwd_w_scatter("ffn1_bwd_w", nb, daccb, ab, dgb, dub, chip,
                                                 _allgather_all_plan([packed_tail]), [packed_tail])
    received = dict(zip(BIG_LATE + BIG_EARLY, late_recv + early_recv))
    chip_sums = _sum_slots("sum_chips", [received[n] for n in BIG], BF16)
    sib_sums = _swap_sibling("swap_sibling", chip_sums)
    (loss_row, g_ret_norm, g_ar, g_ai, g_bbr, g_bbi, g_c_re, g_c_im,
     g_ssm_d, g_glu_b, g_ssm_norm, g_ffn2_norm, g_final) = _unpack(_sum_slots("sum_small", [all_parts], F32)[0],
                                                                  small_shapes)
    g_meta_full, g_ffn1_norm, g_mix_norm = _unpack(_sum_slots("sum_tail", [tail_all], F32)[0],
                                                  [a.shape for a in tail_parts])
    g_lam_re, g_lam_im, g_log_dt, g_b_re, g_b_im = prep_vjp((g_ar, g_ai, g_bbr, g_bbi))
    loss = loss_row[0, 0]
    g_meta = lax.dynamic_slice(g_meta_full, (0, chip * (d // N_CHIP)), (N_META, d // N_CHIP))
    small_grads = {
        'meta_tokens': g_meta, 'ffn1_norm_w': g_ffn1_norm, 'mix_norm_w': g_mix_norm, 'ret_norm_w': g_ret_norm,
        'ssm_lambda_re': g_lam_re[None], 'ssm_lambda_im': g_lam_im[None], 'ssm_log_dt': g_log_dt[None],
        'ssm_b_re': g_b_re[None], 'ssm_b_im': g_b_im[None], 'ssm_c_re': g_c_re[None], 'ssm_c_im': g_c_im[None],
        'ssm_d': g_ssm_d, 'ssm_glu_b': g_glu_b, 'ssm_norm_w': g_ssm_norm, 'ffn2_norm_w': g_ffn2_norm,
        'final_norm_w': g_final.reshape(d),
    }

    grads, deltas, new_m, new_v = {}, {}, {}, {}
    g_pair = {n: [mine, sib] for n, mine, sib in zip(BIG, chip_sums, sib_sums)}
    view = lambda n, t: as_fd(t) if n in TRANSPOSED else t
    big_out = _adam("adam_big", [(view(n, w[n]), view(n, m[n]), view(n, v[n])) for n in BIG], [g_pair[n] for n in BIG])
    for n, outs in zip(BIG, big_out):
        grads[n], deltas[n], new_m[n], new_v[n] = [view(n, t) for t in outs]
    sm_shapes = [w[n].shape for n in SMALL]
    sm_out = _adam("adam_small", [(_pack([w[n] for n in SMALL]), _pack([m[n] for n in SMALL]),
                                  _pack([v[n] for n in SMALL]))],
                   [[_pack([small_grads[n].reshape(w[n].shape) for n in SMALL])]])[0]
    for dst, packed in zip((grads, deltas, new_m, new_v), sm_out):
        for n, t in zip(SMALL, _unpack(packed, sm_shapes)):
            dst[n] = t

    return (loss, grad_x, *[grads[n] for n in WEIGHT_NAMES], *[deltas[n] for n in WEIGHT_NAMES],
            *[new_m[n] for n in WEIGHT_NAMES], *[new_v[n] for n in WEIGHT_NAMES])
```
